```python
import math
import jax, jax.numpy as jnp
from jax import lax
import numpy as np

D_MODEL = 1024
BATCH = 16
SEQ = 2048
DEPTH = 4

N_A_LAYERS = DEPTH // 2
N_B_LAYERS = DEPTH - N_A_LAYERS
HEAD_DIM = 64
MEM_LEN = 256
MEM_HEADS = 4
MEM_WIDTH = MEM_HEADS * HEAD_DIM
MIX_WIDTH = D_MODEL - MEM_WIDTH
LRU_WIDTH = MIX_WIDTH
LRU_BLOCKS = LRU_WIDTH // HEAD_DIM
LRU_BLOCK = LRU_WIDTH // LRU_BLOCKS
LRU_CONV = 4
LRU_C = 8.0
SWA_HEADS = MIX_WIDTH // HEAD_DIM
SWA_KV_HEADS = 4
SWA_GROUP = SWA_HEADS // SWA_KV_HEADS
WINDOW = 128
BLOCK = 128
D_FF = 2816
FFN_CONV = 3
EPS = 1e-6

kernel_name = "hawk_yoco_swa_sink_alibi_hybrid"


def rmsnorm(x, g):
    xf = x.astype(jnp.float32)
    y = xf * lax.rsqrt(jnp.mean(xf * xf, axis=-1, keepdims=True) + EPS)
    return (y * g.astype(jnp.float32)).astype(x.dtype)


def causal_dwconv(x, w, b):
    width, ch = w.shape
    y = lax.conv_general_dilated(
        x, w[:, None, :].astype(x.dtype), window_strides=(1,), padding=[(width - 1, 0)],
        dimension_numbers=("NWC", "WIO", "NWC"), feature_group_count=ch)
    return y + b.astype(x.dtype)


def alibi_slopes(n):
    def pow2_slopes(m):
        start = 2.0 ** (-8.0 / m)
        return [start ** (i + 1) for i in range(m)]
    c = 2 ** int(math.floor(math.log2(n)))
    s = pow2_slopes(c)
    if c != n:
        s = s + pow2_slopes(2 * c)[0::2][: n - c]
    return np.asarray(s, dtype=np.float32)


def rglru(u_x, u_gate, w_conv, b_conv, w_r, b_r, w_i, b_i, lam):
    bsz, t, _ = u_x.shape
    xc = causal_dwconv(u_x, w_conv, b_conv)
    xb = xc.reshape(bsz, t, LRU_BLOCKS, LRU_BLOCK)
    r = jax.nn.sigmoid(jnp.einsum("btni,nij->btnj", xb, w_r) + b_r).reshape(bsz, t, LRU_WIDTH)
    i = jax.nn.sigmoid(jnp.einsum("btni,nij->btnj", xb, w_i) + b_i).reshape(bsz, t, LRU_WIDTH)
    log_a = -LRU_C * r.astype(jnp.float32) * jax.nn.softplus(-lam.astype(jnp.float32))
    a = jnp.exp(log_a)
    b = jnp.sqrt(-jnp.expm1(2.0 * log_a)) * (i * xc).astype(jnp.float32)

    def combine(lhs, rhs):
        a1, b1 = lhs
        a2, b2 = rhs
        return a1 * a2, a2 * b1 + b2

    _, h = lax.associative_scan(combine, (a, b), axis=1)
    return h.astype(u_x.dtype) * jax.nn.gelu(u_gate, approximate=True)


def band_blocks(t):
    bsz, seq, nh, hd = t.shape
    tb = t.reshape(bsz, seq // BLOCK, BLOCK, nh, hd)
    prev = jnp.pad(tb[:, :-1], ((0, 0), (1, 0), (0, 0), (0, 0), (0, 0)))
    return jnp.concatenate([prev, tb], axis=2)


def swa_sink_attention(q, k_blk, v_blk, sinks, slopes):
    bsz, seq, _, hd = q.shape
    nb = seq // BLOCK
    qb = q.reshape(bsz, nb, BLOCK, SWA_KV_HEADS, SWA_GROUP, hd)
    s = jnp.einsum("bnqkgd,bnskd->bnkgqs", qb, k_blk,
                   preferred_element_type=jnp.float32) * (hd ** -0.5)
    q_pos = jnp.arange(BLOCK)[:, None] + BLOCK
    k_pos = jnp.arange(2 * BLOCK)[None, :]
    dist = q_pos - k_pos
    in_window = (dist >= 0) & (dist < WINDOW)
    has_prev = (jnp.arange(nb)[:, None, None] > 0) | (k_pos[None] >= BLOCK)
    mask = in_window[None] & has_prev
    alibi = -slopes.reshape(SWA_KV_HEADS, SWA_GROUP, 1, 1) * dist.astype(jnp.float32)
    s = jnp.where(mask[None, :, None, None], s + alibi, -jnp.inf)
    sink = sinks.astype(jnp.float32).reshape(1, 1, SWA_KV_HEADS, SWA_GROUP, 1, 1)
    m = jnp.maximum(jnp.max(s, axis=-1, keepdims=True), sink)
    p = jnp.exp(s - m)
    p = p / (jnp.sum(p, axis=-1, keepdims=True) + jnp.exp(sink - m))
    o = jnp.einsum("bnkgqs,bnskd->bnqkgd", p.astype(v_blk.dtype), v_blk)
    return o.reshape(bsz, seq, SWA_HEADS * hd)


def memory_attention(q, k, v):
    s = jnp.einsum("bthd,bmhd->bhtm", q, k, preferred_element_type=jnp.float32) * (q.shape[-1] ** -0.5)
    p = jax.nn.softmax(s, axis=-1)
    o = jnp.einsum("bhtm,bmhd->bthd", p.astype(v.dtype), v)
    return o.reshape(q.shape[0], q.shape[1], MEM_WIDTH)


def conv_gated_ffn(h, w_up, w_conv, b_conv, w_down):
    u = causal_dwconv(h @ w_up, w_conv, b_conv)
    g, v = jnp.split(u, 2, axis=-1)
    return (jax.nn.gelu(g, approximate=True) * v) @ w_down


def _fwd_setup_inputs(seed: int = 0) -> dict:
    key = jax.random.key(seed)
    ks = list(jax.random.split(key, 25))

    def nrm(k, shape, scale):
        return jax.random.normal(k, shape, jnp.float32) * scale

    din = D_MODEL ** -0.5
    x = nrm(ks[0], (BATCH, SEQ, D_MODEL), 1.0)
    mem = nrm(ks[1], (BATCH, MEM_LEN, D_MODEL), 1.0)
    g_mix_pre = 1.0 + nrm(ks[2], (DEPTH, D_MODEL), 0.02)
    g_mix_post = 1.0 + nrm(ks[3], (DEPTH, D_MODEL), 0.02)
    g_ffn_pre = 1.0 + nrm(ks[4], (DEPTH, D_MODEL), 0.02)
    g_ffn_post = 1.0 + nrm(ks[5], (DEPTH, D_MODEL), 0.02)
    g_mem = 1.0 + nrm(ks[6], (DEPTH, D_MODEL), 0.02)
    w_mem_kv = nrm(ks[7], (DEPTH, D_MODEL, 2 * MEM_WIDTH), din)
    w_mix_out = nrm(ks[8], (DEPTH, MIX_WIDTH + MEM_WIDTH, D_MODEL), (MIX_WIDTH + MEM_WIDTH) ** -0.5)
    w_ffn_up = nrm(ks[9], (DEPTH, D_MODEL, 2 * D_FF), din)
    w_ffn_conv = nrm(ks[10], (DEPTH, FFN_CONV, 2 * D_FF), FFN_CONV ** -0.5)
    b_ffn_conv = nrm(ks[11], (DEPTH, 2 * D_FF), 0.01)
    w_ffn_down = nrm(ks[12], (DEPTH, D_FF, D_MODEL), D_FF ** -0.5)
    w_in_a = nrm(ks[13], (N_A_LAYERS, D_MODEL, 2 * LRU_WIDTH + MEM_WIDTH), din)
    w_conv_a = nrm(ks[14], (N_A_LAYERS, LRU_CONV, LRU_WIDTH), LRU_CONV ** -0.5)
    b_conv_a = nrm(ks[15], (N_A_LAYERS, LRU_WIDTH), 0.01)
    w_rg_r = nrm(ks[16], (N_A_LAYERS, LRU_BLOCKS, LRU_BLOCK, LRU_BLOCK), LRU_BLOCK ** -0.5)
    b_rg_r = nrm(ks[17], (N_A_LAYERS, LRU_BLOCKS, LRU_BLOCK), 0.01)
    w_rg_i = nrm(ks[18], (N_A_LAYERS, LRU_BLOCKS, LRU_BLOCK, LRU_BLOCK), LRU_BLOCK ** -0.5)
    b_rg_i = nrm(ks[19], (N_A_LAYERS, LRU_BLOCKS, LRU_BLOCK), 0.01)
    a_base = jax.random.uniform(ks[20], (N_A_LAYERS, LRU_WIDTH), jnp.float32, 0.9, 0.999) ** (1.0 / LRU_C)
    lru_lambda = jnp.log(a_base) - jnp.log1p(-a_base)
    w_in_b = nrm(ks[21], (N_B_LAYERS, D_MODEL, MIX_WIDTH + MEM_WIDTH), din)
    sinks_b = nrm(ks[22], (N_B_LAYERS, SWA_HEADS), 0.5)
    g_kv = 1.0 + nrm(ks[23], (D_MODEL,), 0.02)
    w_kv = nrm(ks[24], (D_MODEL, 2 * SWA_KV_HEADS * HEAD_DIM), din)
    return {"x": x, "mem": mem, "g_mix_pre": g_mix_pre, "g_mix_post": g_mix_post,
            "g_ffn_pre": g_ffn_pre, "g_ffn_post": g_ffn_post, "g_mem": g_mem,
            "w_mem_kv": w_mem_kv, "w_mix_out": w_mix_out, "w_ffn_up": w_ffn_up,
            "w_ffn_conv": w_ffn_conv, "b_ffn_conv": b_ffn_conv, "w_ffn_down": w_ffn_down,
            "w_in_a": w_in_a, "w_conv_a": w_conv_a, "b_conv_a": b_conv_a,
            "w_rg_r": w_rg_r, "b_rg_r": b_rg_r, "w_rg_i": w_rg_i, "b_rg_i": b_rg_i,
            "lru_lambda": lru_lambda, "w_in_b": w_in_b, "sinks_b": sinks_b,
            "g_kv": g_kv, "w_kv": w_kv}


def _fwd_reference(x, mem, g_mix_pre, g_mix_post, g_ffn_pre, g_ffn_post, g_mem, w_mem_kv, w_mix_out,
              w_ffn_up, w_ffn_conv, b_ffn_conv, w_ffn_down, w_in_a, w_conv_a, b_conv_a,
              w_rg_r, b_rg_r, w_rg_i, b_rg_i, lru_lambda, w_in_b, sinks_b, g_kv, w_kv):
    bsz, seq, _ = x.shape
    mem_len = mem.shape[1]
    slopes = jnp.asarray(alibi_slopes(SWA_HEADS))
    k_blk = None
    v_blk = None
    for layer in range(DEPTH):
        if layer == N_A_LAYERS:
            kv = (rmsnorm(x, g_kv) @ w_kv).reshape(bsz, seq, 2, SWA_KV_HEADS, HEAD_DIM)
            k_blk = band_blocks(kv[:, :, 0])
            v_blk = band_blocks(kv[:, :, 1])

        h = rmsnorm(x, g_mix_pre[layer])
        mkv = (rmsnorm(mem, g_mem[layer]) @ w_mem_kv[layer]).reshape(bsz, mem_len, 2, MEM_HEADS, HEAD_DIM)
        if layer < N_A_LAYERS:
            j = layer
            proj = h @ w_in_a[j]
            u_gate, u_x, q_mem = jnp.split(proj, [LRU_WIDTH, 2 * LRU_WIDTH], axis=-1)
            y_main = rglru(u_x, u_gate, w_conv_a[j], b_conv_a[j], w_rg_r[j], b_rg_r[j],
                           w_rg_i[j], b_rg_i[j], lru_lambda[j])
        else:
            j = layer - N_A_LAYERS
            proj = h @ w_in_b[j]
            q_swa, q_mem = jnp.split(proj, [MIX_WIDTH], axis=-1)
            y_main = swa_sink_attention(q_swa.reshape(bsz, seq, SWA_HEADS, HEAD_DIM),
                                        k_blk, v_blk, sinks_b[j], slopes)
        y_mem = memory_attention(q_mem.reshape(bsz, seq, MEM_HEADS, HEAD_DIM), mkv[:, :, 0], mkv[:, :, 1])
        y = jnp.concatenate([y_main, y_mem], axis=-1) @ w_mix_out[layer]
        x = x + rmsnorm(y, g_mix_post[layer])

        h = rmsnorm(x, g_ffn_pre[layer])
        f = conv_gated_ffn(h, w_ffn_up[layer], w_ffn_conv[layer], b_ffn_conv[layer], w_ffn_down[layer])
        x = x + rmsnorm(f, g_ffn_post[layer])
    return x


import jax as _jax
import jax.numpy as _jnp

TWIN_FORMAT = 'train_step'
FWD_PARAMS = ['x', 'mem', 'g_mix_pre', 'g_mix_post', 'g_ffn_pre', 'g_ffn_post', 'g_mem', 'w_mem_kv', 'w_mix_out', 'w_ffn_up', 'w_ffn_conv', 'b_ffn_conv', 'w_ffn_down', 'w_in_a', 'w_conv_a', 'b_conv_a', 'w_rg_r', 'b_rg_r', 'w_rg_i', 'b_rg_i', 'lru_lambda', 'w_in_b', 'sinks_b', 'g_kv', 'w_kv']
TWIN_WEIGHTS = ['g_mix_pre', 'g_mix_post', 'g_ffn_pre', 'g_ffn_post', 'g_mem', 'w_mem_kv', 'w_mix_out', 'w_ffn_up', 'w_ffn_conv', 'b_ffn_conv', 'w_ffn_down', 'w_in_a', 'w_conv_a', 'b_conv_a', 'w_rg_r', 'b_rg_r', 'w_rg_i', 'b_rg_i', 'lru_lambda', 'w_in_b', 'sinks_b', 'g_kv', 'w_kv']
TWIN_DIFF_INPUT = 'x'
TWIN_INPUTS = ['x', 'mem', 'g_mix_pre', 'g_mix_post', 'g_ffn_pre', 'g_ffn_post', 'g_mem', 'w_mem_kv', 'w_mix_out', 'w_ffn_up', 'w_ffn_conv', 'b_ffn_conv', 'w_ffn_down', 'w_in_a', 'w_conv_a', 'b_conv_a', 'w_rg_r', 'b_rg_r', 'w_rg_i', 'b_rg_i', 'lru_lambda', 'w_in_b', 'sinks_b', 'g_kv', 'w_kv', 'loss_target', 'm_g_mix_pre', 'm_g_mix_post', 'm_g_ffn_pre', 'm_g_ffn_post', 'm_g_mem', 'm_w_mem_kv', 'm_w_mix_out', 'm_w_ffn_up', 'm_w_ffn_conv', 'm_b_ffn_conv', 'm_w_ffn_down', 'm_w_in_a', 'm_w_conv_a', 'm_b_conv_a', 'm_w_rg_r', 'm_b_rg_r', 'm_w_rg_i', 'm_b_rg_i', 'm_lru_lambda', 'm_w_in_b', 'm_sinks_b', 'm_g_kv', 'm_w_kv', 'v_g_mix_pre', 'v_g_mix_post', 'v_g_ffn_pre', 'v_g_ffn_post', 'v_g_mem', 'v_w_mem_kv', 'v_w_mix_out', 'v_w_ffn_up', 'v_w_ffn_conv', 'v_b_ffn_conv', 'v_w_ffn_down', 'v_w_in_a', 'v_w_conv_a', 'v_b_conv_a', 'v_w_rg_r', 'v_b_rg_r', 'v_w_rg_i', 'v_b_rg_i', 'v_lru_lambda', 'v_w_in_b', 'v_sinks_b', 'v_g_kv', 'v_w_kv']
TWIN_OUTPUTS = ['loss', 'grad_x', 'grad_g_mix_pre', 'grad_g_mix_post', 'grad_g_ffn_pre', 'grad_g_ffn_post', 'grad_g_mem', 'grad_w_mem_kv', 'grad_w_mix_out', 'grad_w_ffn_up', 'grad_w_ffn_conv', 'grad_b_ffn_conv', 'grad_w_ffn_down', 'grad_w_in_a', 'grad_w_conv_a', 'grad_b_conv_a', 'grad_w_rg_r', 'grad_b_rg_r', 'grad_w_rg_i', 'grad_b_rg_i', 'grad_lru_lambda', 'grad_w_in_b', 'grad_sinks_b', 'grad_g_kv', 'grad_w_kv', 'delta_g_mix_pre', 'delta_g_mix_post', 'delta_g_ffn_pre', 'delta_g_ffn_post', 'delta_g_mem', 'delta_w_mem_kv', 'delta_w_mix_out', 'delta_w_ffn_up', 'delta_w_ffn_conv', 'delta_b_ffn_conv', 'delta_w_ffn_down', 'delta_w_in_a', 'delta_w_conv_a', 'delta_b_conv_a', 'delta_w_rg_r', 'delta_b_rg_r', 'delta_w_rg_i', 'delta_b_rg_i', 'delta_lru_lambda', 'delta_w_in_b', 'delta_sinks_b', 'delta_g_kv', 'delta_w_kv', 'new_m_g_mix_pre', 'new_m_g_mix_post', 'new_m_g_ffn_pre', 'new_m_g_ffn_post', 'new_m_g_mem', 'new_m_w_mem_kv', 'new_m_w_mix_out', 'new_m_w_ffn_up', 'new_m_w_ffn_conv', 'new_m_b_ffn_conv', 'new_m_w_ffn_down', 'new_m_w_in_a', 'new_m_w_conv_a', 'new_m_b_conv_a', 'new_m_w_rg_r', 'new_m_b_rg_r', 'new_m_w_rg_i', 'new_m_b_rg_i', 'new_m_lru_lambda', 'new_m_w_in_b', 'new_m_sinks_b', 'new_m_g_kv', 'new_m_w_kv', 'new_v_g_mix_pre', 'new_v_g_mix_post', 'new_v_g_ffn_pre', 'new_v_g_ffn_post', 'new_v_g_mem', 'new_v_w_mem_kv', 'new_v_w_mix_out', 'new_v_w_ffn_up', 'new_v_w_ffn_conv', 'new_v_b_ffn_conv', 'new_v_w_ffn_down', 'new_v_w_in_a', 'new_v_w_conv_a', 'new_v_b_conv_a', 'new_v_w_rg_r', 'new_v_b_rg_r', 'new_v_w_rg_i', 'new_v_b_rg_i', 'new_v_lru_lambda', 'new_v_w_in_b', 'new_v_sinks_b', 'new_v_g_kv', 'new_v_w_kv']
TWIN_LEAF_KINDS = {'loss': 'loss', 'grad_x': 'grad_x', 'grad_g_mix_pre': 'grad_w', 'grad_g_mix_post': 'grad_w', 'grad_g_ffn_pre': 'grad_w', 'grad_g_ffn_post': 'grad_w', 'grad_g_mem': 'grad_w', 'grad_w_mem_kv': 'grad_w', 'grad_w_mix_out': 'grad_w', 'grad_w_ffn_up': 'grad_w', 'grad_w_ffn_conv': 'grad_w', 'grad_b_ffn_conv': 'grad_w', 'grad_w_ffn_down': 'grad_w', 'grad_w_in_a': 'grad_w', 'grad_w_conv_a': 'grad_w', 'grad_b_conv_a': 'grad_w', 'grad_w_rg_r': 'grad_w', 'grad_b_rg_r': 'grad_w', 'grad_w_rg_i': 'grad_w', 'grad_b_rg_i': 'grad_w', 'grad_lru_lambda': 'grad_w', 'grad_w_in_b': 'grad_w', 'grad_sinks_b': 'grad_w', 'grad_g_kv': 'grad_w', 'grad_w_kv': 'grad_w', 'delta_g_mix_pre': 'delta_w', 'delta_g_mix_post': 'delta_w', 'delta_g_ffn_pre': 'delta_w', 'delta_g_ffn_post': 'delta_w', 'delta_g_mem': 'delta_w', 'delta_w_mem_kv': 'delta_w', 'delta_w_mix_out': 'delta_w', 'delta_w_ffn_up': 'delta_w', 'delta_w_ffn_conv': 'delta_w', 'delta_b_ffn_conv': 'delta_w', 'delta_w_ffn_down': 'delta_w', 'delta_w_in_a': 'delta_w', 'delta_w_conv_a': 'delta_w', 'delta_b_conv_a': 'delta_w', 'delta_w_rg_r': 'delta_w', 'delta_b_rg_r': 'delta_w', 'delta_w_rg_i': 'delta_w', 'delta_b_rg_i': 'delta_w', 'delta_lru_lambda': 'delta_w', 'delta_w_in_b': 'delta_w', 'delta_sinks_b': 'delta_w', 'delta_g_kv': 'delta_w', 'delta_w_kv': 'delta_w', 'new_m_g_mix_pre': 'new_m', 'new_m_g_mix_post': 'new_m', 'new_m_g_ffn_pre': 'new_m', 'new_m_g_ffn_post': 'new_m', 'new_m_g_mem': 'new_m', 'new_m_w_mem_kv': 'new_m', 'new_m_w_mix_out': 'new_m', 'new_m_w_ffn_up': 'new_m', 'new_m_w_ffn_conv': 'new_m', 'new_m_b_ffn_conv': 'new_m', 'new_m_w_ffn_down': 'new_m', 'new_m_w_in_a': 'new_m', 'new_m_w_conv_a': 'new_m', 'new_m_b_conv_a': 'new_m', 'new_m_w_rg_r': 'new_m', 'new_m_b_rg_r': 'new_m', 'new_m_w_rg_i': 'new_m', 'new_m_b_rg_i': 'new_m', 'new_m_lru_lambda': 'new_m', 'new_m_w_in_b': 'new_m', 'new_m_sinks_b': 'new_m', 'new_m_g_kv': 'new_m', 'new_m_w_kv': 'new_m', 'new_v_g_mix_pre': 'new_v', 'new_v_g_mix_post': 'new_v', 'new_v_g_ffn_pre': 'new_v', 'new_v_g_ffn_post': 'new_v', 'new_v_g_mem': 'new_v', 'new_v_w_mem_kv': 'new_v', 'new_v_w_mix_out': 'new_v', 'new_v_w_ffn_up': 'new_v', 'new_v_w_ffn_conv': 'new_v', 'new_v_b_ffn_conv': 'new_v', 'new_v_w_ffn_down': 'new_v', 'new_v_w_in_a': 'new_v', 'new_v_w_conv_a': 'new_v', 'new_v_b_conv_a': 'new_v', 'new_v_w_rg_r': 'new_v', 'new_v_b_rg_r': 'new_v', 'new_v_w_rg_i': 'new_v', 'new_v_b_rg_i': 'new_v', 'new_v_lru_lambda': 'new_v', 'new_v_w_in_b': 'new_v', 'new_v_sinks_b': 'new_v', 'new_v_g_kv': 'new_v', 'new_v_w_kv': 'new_v'}


def _forward(args):
    return _fwd_reference(*[args[k] for k in FWD_PARAMS])


def _output_shape():
    out = _jax.eval_shape(lambda: _forward(_fwd_setup_inputs(0)))
    return out.shape, out.dtype

N_MICROBATCH = 1
ADAM_LR = 0.001
ADAM_B1 = 0.9
ADAM_B2 = 0.999
ADAM_EPS = 1e-08
ADAM_WD = 0.01
ADAM_STEP = 10
PER_EXAMPLE_BATCH_AXIS = {'x': 0, 'mem': 0, 'loss_target': 0}
SHARED_INPUTS = []
_WEIGHT_DTYPES = {'g_mix_pre': _jnp.float32, 'g_mix_post': _jnp.float32, 'g_ffn_pre': _jnp.float32, 'g_ffn_post': _jnp.float32, 'g_mem': _jnp.float32, 'w_mem_kv': _jnp.float32, 'w_mix_out': _jnp.float32, 'w_ffn_up': _jnp.float32, 'w_ffn_conv': _jnp.float32, 'b_ffn_conv': _jnp.float32, 'w_ffn_down': _jnp.float32, 'w_in_a': _jnp.float32, 'w_conv_a': _jnp.float32, 'b_conv_a': _jnp.float32, 'w_rg_r': _jnp.float32, 'b_rg_r': _jnp.float32, 'w_rg_i': _jnp.float32, 'b_rg_i': _jnp.float32, 'lru_lambda': _jnp.float32, 'w_in_b': _jnp.float32, 'sinks_b': _jnp.float32, 'g_kv': _jnp.float32, 'w_kv': _jnp.float32}
MOMENT_SCALE = {'g_mix_pre': 5.255539e+00, 'g_mix_post': 3.529947e+01, 'g_ffn_pre': 3.812337e+00, 'g_ffn_post': 3.171324e+01, 'g_mem': 1.151294e+00, 'w_mem_kv': 1.621453e+00, 'w_mix_out': 1.250940e+01, 'w_ffn_up': 1.490918e+00, 'w_ffn_conv': 1.722759e+00, 'b_ffn_conv': 8.053332e+00, 'w_ffn_down': 3.395621e+00, 'w_in_a': 5.733291e+00, 'w_conv_a': 1.343745e+01, 'b_conv_a': 1.383638e+02, 'w_rg_r': 5.364356e+00, 'b_rg_r': 3.342084e+00, 'w_rg_i': 9.863122e+00, 'b_rg_i': 3.965537e+00, 'lru_lambda': 5.631842e+00, 'w_in_b': 5.038746e-01, 'sinks_b': 9.360041e+00, 'g_kv': 1.291763e+01, 'w_kv': 1.842101e+01}


def _to_microbatches(a, axis):
    t = _jnp.moveaxis(a, axis, 0)
    t = t.reshape((N_MICROBATCH, t.shape[0] // N_MICROBATCH) + t.shape[1:])
    return _jnp.moveaxis(t, 1, axis + 1)


def setup_inputs(seed: int = 0) -> dict:
    inp = _fwd_setup_inputs(seed)
    key = _jax.random.fold_in(_jax.random.key(seed), 7919)
    shape, _ = _output_shape()
    out = dict(inp)
    out["loss_target"] = _jax.random.normal(_jax.random.fold_in(key, 0), shape, _jnp.float32)
    for i, name in enumerate(TWIN_WEIGHTS):
        w = inp[name].astype(_jnp.float32)
        if MOMENT_SCALE is None:
            s = _jnp.sqrt(_jnp.mean(_jnp.square(w)) + 1e-30)
        else:
            s = MOMENT_SCALE[name]
        km, kv = _jax.random.split(_jax.random.fold_in(key, i + 1))
        out[name] = w
        out["m_" + name] = s * _jax.random.normal(km, w.shape, _jnp.float32)
        out["v_" + name] = (s * s) * _jax.random.uniform(kv, w.shape, _jnp.float32, 0.5, 1.5)
    if N_MICROBATCH > 1:
        for name, axis in PER_EXAMPLE_BATCH_AXIS.items():
            out[name] = _to_microbatches(out[name], axis)
    return {'x': out['x'], 'mem': out['mem'], 'g_mix_pre': out['g_mix_pre'], 'g_mix_post': out['g_mix_post'], 'g_ffn_pre': out['g_ffn_pre'], 'g_ffn_post': out['g_ffn_post'], 'g_mem': out['g_mem'], 'w_mem_kv': out['w_mem_kv'], 'w_mix_out': out['w_mix_out'], 'w_ffn_up': out['w_ffn_up'], 'w_ffn_conv': out['w_ffn_conv'], 'b_ffn_conv': out['b_ffn_conv'], 'w_ffn_down': out['w_ffn_down'], 'w_in_a': out['w_in_a'], 'w_conv_a': out['w_conv_a'], 'b_conv_a': out['b_conv_a'], 'w_rg_r': out['w_rg_r'], 'b_rg_r': out['b_rg_r'], 'w_rg_i': out['w_rg_i'], 'b_rg_i': out['b_rg_i'], 'lru_lambda': out['lru_lambda'], 'w_in_b': out['w_in_b'], 'sinks_b': out['sinks_b'], 'g_kv': out['g_kv'], 'w_kv': out['w_kv'], 'loss_target': out['loss_target'], 'm_g_mix_pre': out['m_g_mix_pre'], 'm_g_mix_post': out['m_g_mix_post'], 'm_g_ffn_pre': out['m_g_ffn_pre'], 'm_g_ffn_post': out['m_g_ffn_post'], 'm_g_mem': out['m_g_mem'], 'm_w_mem_kv': out['m_w_mem_kv'], 'm_w_mix_out': out['m_w_mix_out'], 'm_w_ffn_up': out['m_w_ffn_up'], 'm_w_ffn_conv': out['m_w_ffn_conv'], 'm_b_ffn_conv': out['m_b_ffn_conv'], 'm_w_ffn_down': out['m_w_ffn_down'], 'm_w_in_a': out['m_w_in_a'], 'm_w_conv_a': out['m_w_conv_a'], 'm_b_conv_a': out['m_b_conv_a'], 'm_w_rg_r': out['m_w_rg_r'], 'm_b_rg_r': out['m_b_rg_r'], 'm_w_rg_i': out['m_w_rg_i'], 'm_b_rg_i': out['m_b_rg_i'], 'm_lru_lambda': out['m_lru_lambda'], 'm_w_in_b': out['m_w_in_b'], 'm_sinks_b': out['m_sinks_b'], 'm_g_kv': out['m_g_kv'], 'm_w_kv': out['m_w_kv'], 'v_g_mix_pre': out['v_g_mix_pre'], 'v_g_mix_post': out['v_g_mix_post'], 'v_g_ffn_pre': out['v_g_ffn_pre'], 'v_g_ffn_post': out['v_g_ffn_post'], 'v_g_mem': out['v_g_mem'], 'v_w_mem_kv': out['v_w_mem_kv'], 'v_w_mix_out': out['v_w_mix_out'], 'v_w_ffn_up': out['v_w_ffn_up'], 'v_w_ffn_conv': out['v_w_ffn_conv'], 'v_b_ffn_conv': out['v_b_ffn_conv'], 'v_w_ffn_down': out['v_w_ffn_down'], 'v_w_in_a': out['v_w_in_a'], 'v_w_conv_a': out['v_w_conv_a'], 'v_b_conv_a': out['v_b_conv_a'], 'v_w_rg_r': out['v_w_rg_r'], 'v_b_rg_r': out['v_b_rg_r'], 'v_w_rg_i': out['v_w_rg_i'], 'v_b_rg_i': out['v_b_rg_i'], 'v_lru_lambda': out['v_lru_lambda'], 'v_w_in_b': out['v_w_in_b'], 'v_sinks_b': out['v_sinks_b'], 'v_g_kv': out['v_g_kv'], 'v_w_kv': out['v_w_kv']}


def _loss(weights, diff, rest, loss_target):
    with _jax.named_scope("forward"):
        args = {**rest, TWIN_DIFF_INPUT: diff, **{k: w.astype(_WEIGHT_DTYPES[k]) for k, w in weights.items()}}
        y = _forward(args)
    with _jax.named_scope("loss_head"):
        err = _jnp.square(y.astype(_jnp.float32) - loss_target)
        return 0.5 * _jnp.sum(_jnp.mean(err, axis=-1)) if err.ndim else 0.5 * err


def _adamw(w, g, m, v):
    m = ADAM_B1 * m + (1.0 - ADAM_B1) * g
    v = ADAM_B2 * v + (1.0 - ADAM_B2) * _jnp.square(g)
    m_hat = m / (1.0 - ADAM_B1 ** ADAM_STEP)
    v_hat = v / (1.0 - ADAM_B2 ** ADAM_STEP)
    delta = -ADAM_LR * (m_hat / (_jnp.sqrt(v_hat) + ADAM_EPS) + ADAM_WD * w)
    return delta, m, v


def reference(x, mem, g_mix_pre, g_mix_post, g_ffn_pre, g_ffn_post, g_mem, w_mem_kv, w_mix_out, w_ffn_up, w_ffn_conv, b_ffn_conv, w_ffn_down, w_in_a, w_conv_a, b_conv_a, w_rg_r, b_rg_r, w_rg_i, b_rg_i, lru_lambda, w_in_b, sinks_b, g_kv, w_kv, loss_target, m_g_mix_pre, m_g_mix_post, m_g_ffn_pre, m_g_ffn_post, m_g_mem, m_w_mem_kv, m_w_mix_out, m_w_ffn_up, m_w_ffn_conv, m_b_ffn_conv, m_w_ffn_down, m_w_in_a, m_w_conv_a, m_b_conv_a, m_w_rg_r, m_b_rg_r, m_w_rg_i, m_b_rg_i, m_lru_lambda, m_w_in_b, m_sinks_b, m_g_kv, m_w_kv, v_g_mix_pre, v_g_mix_post, v_g_ffn_pre, v_g_ffn_post, v_g_mem, v_w_mem_kv, v_w_mix_out, v_w_ffn_up, v_w_ffn_conv, v_b_ffn_conv, v_w_ffn_down, v_w_in_a, v_w_conv_a, v_b_conv_a, v_w_rg_r, v_b_rg_r, v_w_rg_i, v_b_rg_i, v_lru_lambda, v_w_in_b, v_sinks_b, v_g_kv, v_w_kv):
    given = dict(x=x, mem=mem, g_mix_pre=g_mix_pre, g_mix_post=g_mix_post, g_ffn_pre=g_ffn_pre, g_ffn_post=g_ffn_post, g_mem=g_mem, w_mem_kv=w_mem_kv, w_mix_out=w_mix_out, w_ffn_up=w_ffn_up, w_ffn_conv=w_ffn_conv, b_ffn_conv=b_ffn_conv, w_ffn_down=w_ffn_down, w_in_a=w_in_a, w_conv_a=w_conv_a, b_conv_a=b_conv_a, w_rg_r=w_rg_r, b_rg_r=b_rg_r, w_rg_i=w_rg_i, b_rg_i=b_rg_i, lru_lambda=lru_lambda, w_in_b=w_in_b, sinks_b=sinks_b, g_kv=g_kv, w_kv=w_kv, loss_target=loss_target, m_g_mix_pre=m_g_mix_pre, m_g_mix_post=m_g_mix_post, m_g_ffn_pre=m_g_ffn_pre, m_g_ffn_post=m_g_ffn_post, m_g_mem=m_g_mem, m_w_mem_kv=m_w_mem_kv, m_w_mix_out=m_w_mix_out, m_w_ffn_up=m_w_ffn_up, m_w_ffn_conv=m_w_ffn_conv, m_b_ffn_conv=m_b_ffn_conv, m_w_ffn_down=m_w_ffn_down, m_w_in_a=m_w_in_a, m_w_conv_a=m_w_conv_a, m_b_conv_a=m_b_conv_a, m_w_rg_r=m_w_rg_r, m_b_rg_r=m_b_rg_r, m_w_rg_i=m_w_rg_i, m_b_rg_i=m_b_rg_i, m_lru_lambda=m_lru_lambda, m_w_in_b=m_w_in_b, m_sinks_b=m_sinks_b, m_g_kv=m_g_kv, m_w_kv=m_w_kv, v_g_mix_pre=v_g_mix_pre, v_g_mix_post=v_g_mix_post, v_g_ffn_pre=v_g_ffn_pre, v_g_ffn_post=v_g_ffn_post, v_g_mem=v_g_mem, v_w_mem_kv=v_w_mem_kv, v_w_mix_out=v_w_mix_out, v_w_ffn_up=v_w_ffn_up, v_w_ffn_conv=v_w_ffn_conv, v_b_ffn_conv=v_b_ffn_conv, v_w_ffn_down=v_w_ffn_down, v_w_in_a=v_w_in_a, v_w_conv_a=v_w_conv_a, v_b_conv_a=v_b_conv_a, v_w_rg_r=v_w_rg_r, v_b_rg_r=v_b_rg_r, v_w_rg_i=v_w_rg_i, v_b_rg_i=v_b_rg_i, v_lru_lambda=v_lru_lambda, v_w_in_b=v_w_in_b, v_sinks_b=v_sinks_b, v_g_kv=v_g_kv, v_w_kv=v_w_kv)
    weights = {n: given[n] for n in TWIN_WEIGHTS}
    shared = {n: given[n] for n in SHARED_INPUTS}
    per_example = {n: given[n] for n in ['x', 'mem']}
    grad_fn = _jax.value_and_grad(_loss, argnums=(0, 1))

    def one_microbatch(ex, loss_target):
        ex = dict(ex)
        diff = ex.pop(TWIN_DIFF_INPUT)
        return grad_fn(weights, diff, {**shared, **ex}, loss_target)

    if N_MICROBATCH == 1:
        loss, (grad_w, grad_x) = one_microbatch(per_example, given["loss_target"])
    else:
        def body(carry, xs):
            loss_sum, grad_sum = carry
            l_k, (gw_k, gx_k) = one_microbatch(xs[0], xs[1])
            with _jax.named_scope("update"):
                return (loss_sum + l_k, _jax.tree.map(_jnp.add, grad_sum, gw_k)), gx_k

        init = (_jnp.zeros((), _jnp.float32), _jax.tree.map(_jnp.zeros_like, weights))
        (loss, grad_w), grad_x = _jax.lax.scan(body, init, (per_example, given["loss_target"]))
    with _jax.named_scope("update"):
        delta_w, new_m, new_v = {}, {}, {}
        for n in TWIN_WEIGHTS:
            delta_w[n], new_m[n], new_v[n] = _adamw(weights[n], grad_w[n], given["m_" + n], given["v_" + n])
    return (loss, grad_x, *[grad_w[n] for n in TWIN_WEIGHTS], *[delta_w[n] for n in TWIN_WEIGHTS],
            *[new_m[n] for n in TWIN_WEIGHTS], *[new_v[n] for n in TWIN_WEIGHTS])
```

```python
import functools
import math

import numpy as np
import jax
import jax.numpy as jnp
from jax import lax
from jax.experimental import pallas as pl
from jax.experimental.pallas import tpu as pltpu

F32 = jnp.float32
MXU = jnp.bfloat16

D_MODEL = 1024
HEAD_DIM = 64
MEM_LEN = 256
MEM_HEADS = 4
MEM_WIDTH = MEM_HEADS * HEAD_DIM
MIX_WIDTH = D_MODEL - MEM_WIDTH
LRU_BLOCKS = MIX_WIDTH // HEAD_DIM
LRU_CONV = 4
LRU_C = 8.0
SWA_HEADS = MIX_WIDTH // HEAD_DIM
SWA_KV_HEADS = 4
SWA_GROUP = SWA_HEADS // SWA_KV_HEADS
WINDOW = 128
D_FF = 2816
FFN_CONV = 3
EPS = 1e-6
DEPTH = 4
N_A = 2

ADAM_LR = 0.001
ADAM_B1 = 0.9
ADAM_B2 = 0.999
ADAM_EPS = 1e-08
ADAM_WD = 0.01
ADAM_STEP = 10

VMEM_LIMIT_V7X = 56 * 1024 * 1024
LANE = 128
SUBLANE = 8
GATE_TILE = 256
MESH_T = pl.DeviceIdType.MESH


def _alibi_slopes(n):
    def pow2_slopes(m):
        start = 2.0 ** (-8.0 / m)
        return [start ** (i + 1) for i in range(m)]
    c = 2 ** int(math.floor(math.log2(n)))
    s = pow2_slopes(c)
    if c != n:
        s = s + pow2_slopes(2 * c)[0::2][: n - c]
    return [float(np.float32(v)) for v in s]


SLOPES = _alibi_slopes(SWA_HEADS)


def _tile(n, cap, mult=LANE):
    best = None
    for t in range(mult, min(n, cap) + 1, mult):
        if n % t == 0:
            best = t
    return best if best is not None else n


def _cp(sem):
    return pltpu.CompilerParams(dimension_semantics=sem, vmem_limit_bytes=VMEM_LIMIT_V7X)


ARB = "arbitrary"
PAR = "parallel"


def _rms_fwd(x, g):
    r = lax.rsqrt(jnp.mean(x * x, axis=-1, keepdims=True) + EPS)
    return x * r * g


def _rms_bwd(dy, x, g):
    r = lax.rsqrt(jnp.mean(x * x, axis=-1, keepdims=True) + EPS)
    xh = x * r
    gdy = dy * g
    dx = r * (gdy - xh * jnp.mean(gdy * xh, axis=-1, keepdims=True))
    dg = jnp.sum(dy * xh, axis=0, keepdims=True)
    return dx, dg


_GELU_K = math.sqrt(2.0 / math.pi)
_GELU_C = 0.044715


def _gelu(x):
    t = jnp.tanh(_GELU_K * (x + _GELU_C * x * x * x))
    return 0.5 * x * (1.0 + t)


def _gelu_and_grad(x):
    x2 = x * x
    t = jnp.tanh(_GELU_K * (x + _GELU_C * x2 * x))
    g = 0.5 * x * (1.0 + t)
    dg = 0.5 * (1.0 + t) + 0.5 * x * (1.0 - t * t) * (_GELU_K * (1.0 + 3.0 * _GELU_C * x2))
    return g, dg


def _shift_down(x, k, row):
    return jnp.where(row >= k, pltpu.roll(x, k, axis=0), 0.0)


def _shift_up(x, k, row):
    n = x.shape[0]
    return jnp.where(row < n - k, pltpu.roll(x, n - k, axis=0), 0.0)


def _dot(a, b):
    return jnp.dot(a, b, preferred_element_type=F32)


def _dot_nt(a, b):
    return lax.dot_general(a, b, (((1,), (1,)), ((), ())), preferred_element_type=F32)


def _dot_tn(a, b):
    return lax.dot_general(a, b, (((0,), (0,)), ((), ())), preferred_element_type=F32)


def _mm_nn(a, b, l, *, name, out_dtype=F32):
    m, k = a.shape
    n = b.shape[-1]
    tm, tn = _tile(m, 512, SUBLANE), _tile(n, 512)

    def body(a_ref, b_ref, o_ref):
        o_ref[...] = _dot(a_ref[...], b_ref[...]).astype(o_ref.dtype)

    return pl.pallas_call(
        body, grid=(m // tm, n // tn),
        in_specs=[pl.BlockSpec((tm, k), lambda i, j: (i, 0)),
                  pl.BlockSpec((None, k, tn), lambda i, j: (l, 0, j))],
        out_specs=pl.BlockSpec((tm, tn), lambda i, j: (i, j)),
        out_shape=jax.ShapeDtypeStruct((m, n), out_dtype),
        name=name, compiler_params=_cp((PAR, PAR)))(a, b)


def _mm_nt(a, b, l, *, name, out_dtype=F32):
    m, k = a.shape
    n = b.shape[-2]
    tm, tn = _tile(m, 512, SUBLANE), _tile(n, 512)

    def body(a_ref, b_ref, o_ref):
        o_ref[...] = _dot_nt(a_ref[...], b_ref[...]).astype(o_ref.dtype)

    return pl.pallas_call(
        body, grid=(m // tm, n // tn),
        in_specs=[pl.BlockSpec((tm, k), lambda i, j: (i, 0)),
                  pl.BlockSpec((None, tn, k), lambda i, j: (l, j, 0))],
        out_specs=pl.BlockSpec((tm, tn), lambda i, j: (i, j)),
        out_shape=jax.ShapeDtypeStruct((m, n), out_dtype),
        name=name, compiler_params=_cp((PAR, PAR)))(a, b)


def _mm_tn_into(a, b, out, l, *, name, col_block_offset=0):
    k, m = a.shape
    n = b.shape[-1]
    tm, tn = _tile(m, 512), _tile(n, 512)
    off = col_block_offset * (n // tn)

    def body(a_ref, b_ref, old_ref, o_ref):
        del old_ref
        o_ref[...] = _dot_tn(a_ref[...], b_ref[...]).astype(o_ref.dtype)

    return pl.pallas_call(
        body, grid=(m // tm, n // tn),
        in_specs=[pl.BlockSpec((k, tm), lambda i, j: (0, i)),
                  pl.BlockSpec((k, tn), lambda i, j: (0, j)),
                  pl.BlockSpec(memory_space=pl.ANY)],
        out_specs=pl.BlockSpec((None, tm, tn), lambda i, j: (l, i, j + off)),
        out_shape=jax.ShapeDtypeStruct(out.shape, out.dtype),
        input_output_aliases={2: 0},
        name=name, compiler_params=_cp((PAR, PAR)))(a, b, out)


def _mm_ffn_dh(dg, dv, w_up, l, *, name):
    m, f = dg.shape
    d = w_up.shape[-2]
    tm, tn = _tile(m, 512, SUBLANE), _tile(d, 512)

    def body(dg_ref, dv_ref, wg_ref, wv_ref, o_ref):
        o_ref[...] = _dot_nt(dg_ref[...], wg_ref[...]) + _dot_nt(dv_ref[...], wv_ref[...])

    return pl.pallas_call(
        body, grid=(m // tm, d // tn),
        in_specs=[pl.BlockSpec((tm, f), lambda i, j: (i, 0)),
                  pl.BlockSpec((tm, f), lambda i, j: (i, 0)),
                  pl.BlockSpec((None, tn, f), lambda i, j: (l, j, 0)),
                  pl.BlockSpec((None, tn, f), lambda i, j: (l, j, 1))],
        out_specs=pl.BlockSpec((tm, tn), lambda i, j: (i, j)),
        out_shape=jax.ShapeDtypeStruct((m, d), F32),
        name=name, compiler_params=_cp((PAR, PAR)))(dg, dv, w_up, w_up)


def _norm_fwd(x, g, *, name):
    n, d = x.shape
    tm = _tile(n, 256, SUBLANE)

    def body(x_ref, g_ref, o_ref):
        o_ref[...] = _rms_fwd(x_ref[...], g_ref[...]).astype(o_ref.dtype)

    return pl.pallas_call(
        body, grid=(n // tm,),
        in_specs=[pl.BlockSpec((tm, d), lambda i: (i, 0)), pl.BlockSpec((1, d), lambda i: (0, 0))],
        out_specs=pl.BlockSpec((tm, d), lambda i: (i, 0)),
        out_shape=jax.ShapeDtypeStruct((n, d), MXU),
        name=name, compiler_params=_cp((PAR,)))(x, g)


def _norm_bwd_dg(dy, x, g, *, name):
    n, d = x.shape
    tm = _tile(n, 256, SUBLANE)

    def body(dy_ref, x_ref, g_ref, dg_ref):
        @pl.when(pl.program_id(0) == 0)
        def _():
            dg_ref[...] = jnp.zeros_like(dg_ref)
        _, dg = _rms_bwd(dy_ref[...], x_ref[...], g_ref[...])
        dg_ref[...] += dg

    return pl.pallas_call(
        body, grid=(n // tm,),
        in_specs=[pl.BlockSpec((tm, d), lambda i: (i, 0)), pl.BlockSpec((tm, d), lambda i: (i, 0)),
                  pl.BlockSpec((1, d), lambda i: (0, 0))],
        out_specs=pl.BlockSpec((1, d), lambda i: (0, 0)),
        out_shape=jax.ShapeDtypeStruct((1, d), F32),
        name=name, compiler_params=_cp((ARB,)))(dy, x, g)


def _resid_norm_fwd(x, y, g_post, g_pres, *, name):
    n, d = x.shape
    tm = _tile(n, 256, SUBLANE)
    nh = len(g_pres)

    def body(x_ref, y_ref, gp_ref, *rest):
        gpre = rest[:nh]
        xo_ref = rest[nh]
        h_refs = rest[nh + 1:]
        xo = x_ref[...] + _rms_fwd(y_ref[...], gp_ref[...])
        xo_ref[...] = xo
        for g_ref, h_ref in zip(gpre, h_refs):
            h_ref[...] = _rms_fwd(xo, g_ref[...]).astype(h_ref.dtype)

    row = pl.BlockSpec((tm, d), lambda i: (i, 0))
    vec = pl.BlockSpec((1, d), lambda i: (0, 0))
    outs = pl.pallas_call(
        body, grid=(n // tm,),
        in_specs=[row, row, vec] + [vec] * nh,
        out_specs=[row] + [row] * nh,
        out_shape=[jax.ShapeDtypeStruct((n, d), F32)] + [jax.ShapeDtypeStruct((n, d), MXU)] * nh,
        name=name, compiler_params=_cp((PAR,)))(x, y, g_post, *g_pres)
    return outs[0], list(outs[1:])


def _loss_fwd(x, y, g_post, target, *, name):
    n, d = x.shape
    tm = _tile(n, 256, SUBLANE)

    def body(x_ref, y_ref, gp_ref, t_ref, dx_ref, sq_ref):
        @pl.when(pl.program_id(0) == 0)
        def _():
            sq_ref[...] = jnp.zeros_like(sq_ref)
        err = x_ref[...] + _rms_fwd(y_ref[...], gp_ref[...]) - t_ref[...]
        dx_ref[...] = err * (1.0 / d)
        sq_ref[...] += jnp.sum(err * err, axis=0, keepdims=True)

    row = pl.BlockSpec((tm, d), lambda i: (i, 0))
    vec = pl.BlockSpec((1, d), lambda i: (0, 0))
    return pl.pallas_call(
        body, grid=(n // tm,),
        in_specs=[row, row, vec, row],
        out_specs=[row, vec],
        out_shape=[jax.ShapeDtypeStruct((n, d), F32), jax.ShapeDtypeStruct((1, d), F32)],
        name=name, compiler_params=_cp((ARB,)))(x, y, g_post, target)


def _resid_norm_bwd(dx_out, dhs, x_out, g_pres, y, g_post, *, name):
    n, d = dx_out.shape
    tm = _tile(n, 256, SUBLANE)
    nh = len(dhs)
    has_y = y is not None

    def body(*refs):
        it = iter(refs)
        dxo_ref = next(it)
        dh_refs = [next(it) for _ in range(nh)]
        xo_ref = next(it) if nh else None
        gpre_refs = [next(it) for _ in range(nh)]
        y_ref = next(it) if has_y else None
        gpost_ref = next(it) if has_y else None
        g_out = next(it)
        dy_out = next(it) if has_y else None
        dgpre_out = [next(it) for _ in range(nh)]
        dgpost_out = next(it) if has_y else None

        @pl.when(pl.program_id(0) == 0)
        def _():
            for r in dgpre_out:
                r[...] = jnp.zeros_like(r)
            if has_y:
                dgpost_out[...] = jnp.zeros_like(dgpost_out)

        g = dxo_ref[...]
        if nh:
            xo = xo_ref[...]
            for dh_ref, gp_ref, dg_ref in zip(dh_refs, gpre_refs, dgpre_out):
                dx, dg = _rms_bwd(dh_ref[...], xo, gp_ref[...])
                g = g + dx
                dg_ref[...] += dg
        g_out[...] = g
        if has_y:
            dy, dg = _rms_bwd(g, y_ref[...], gpost_ref[...])
            dy_out[...] = dy.astype(dy_out.dtype)
            dgpost_out[...] += dg

    row = pl.BlockSpec((tm, d), lambda i: (i, 0))
    vec = pl.BlockSpec((1, d), lambda i: (0, 0))
    ins, in_specs = [dx_out], [row]
    ins += list(dhs)
    in_specs += [row] * nh
    if nh:
        ins.append(x_out)
        in_specs.append(row)
    ins += list(g_pres)
    in_specs += [vec] * nh
    if has_y:
        ins += [y, g_post]
        in_specs += [row, vec]
    out_specs, out_shape = [row], [jax.ShapeDtypeStruct((n, d), F32)]
    if has_y:
        out_specs.append(row)
        out_shape.append(jax.ShapeDtypeStruct((n, d), MXU))
    out_specs += [vec] * nh
    out_shape += [jax.ShapeDtypeStruct((1, d), F32)] * nh
    if has_y:
        out_specs.append(vec)
        out_shape.append(jax.ShapeDtypeStruct((1, d), F32))
    outs = list(pl.pallas_call(
        body, grid=(n // tm,), in_specs=in_specs, out_specs=out_specs, out_shape=out_shape,
        name=name, compiler_params=_cp((ARB,)))(*ins))
    g = outs.pop(0)
    dy = outs.pop(0) if has_y else None
    dgpre = [outs.pop(0) for _ in range(nh)]
    dgpost = outs.pop(0) if has_y else None
    return g, dy, dgpre, dgpost


def _ffn_conv(up, w_ref, b_ref, row):
    u1 = _shift_down(up, 1, row)
    u2 = _shift_down(up, 2, row)
    u = w_ref[0:1, :] * u2 + w_ref[1:2, :] * u1 + w_ref[2:3, :] * up + b_ref[...]
    return u, u1, u2


def _ffn_act_fwd(up, wconv, bconv, bsz, *, name):
    n, f2 = up.shape
    f = f2 // 2
    t = n // bsz
    tc = _tile(f, 256)
    nf = f // tc

    def body(ug_ref, uv_ref, wg_ref, wv_ref, bg_ref, bv_ref, o_ref):
        row = lax.broadcasted_iota(jnp.int32, (t, tc), 0)
        g, _, _ = _ffn_conv(ug_ref[...], wg_ref, bg_ref, row)
        v, _, _ = _ffn_conv(uv_ref[...], wv_ref, bv_ref, row)
        o_ref[...] = (_gelu(g) * v).astype(o_ref.dtype)

    return pl.pallas_call(
        body, grid=(bsz, nf),
        in_specs=[pl.BlockSpec((t, tc), lambda b, j: (b, j)),
                  pl.BlockSpec((t, tc), lambda b, j: (b, j + nf)),
                  pl.BlockSpec((FFN_CONV, tc), lambda b, j: (0, j)),
                  pl.BlockSpec((FFN_CONV, tc), lambda b, j: (0, j + nf)),
                  pl.BlockSpec((1, tc), lambda b, j: (0, j)),
                  pl.BlockSpec((1, tc), lambda b, j: (0, j + nf))],
        out_specs=pl.BlockSpec((t, tc), lambda b, j: (b, j)),
        out_shape=jax.ShapeDtypeStruct((n, f), MXU),
        name=name, compiler_params=_cp((PAR, PAR)))(up, up, wconv, wconv, bconv, bconv)


def _ffn_act_bwd(up, dact, wconv, bconv, bsz, *, name):
    n, f2 = up.shape
    f = f2 // 2
    t = n // bsz
    tc = _tile(f, 256)
    nf = f // tc

    def body(ug_ref, uv_ref, da_ref, wg_ref, wv_ref, bg_ref, bv_ref,
             dug_ref, duv_ref, dwg_ref, dwv_ref, dbg_ref, dbv_ref):
        @pl.when(pl.program_id(1) == 0)
        def _():
            for r in (dwg_ref, dwv_ref, dbg_ref, dbv_ref):
                r[...] = jnp.zeros_like(r)

        row = lax.broadcasted_iota(jnp.int32, (t, tc), 0)
        ug, uv = ug_ref[...], uv_ref[...]
        g, ug1, ug2 = _ffn_conv(ug, wg_ref, bg_ref, row)
        v, uv1, uv2 = _ffn_conv(uv, wv_ref, bv_ref, row)
        gl, dgl = _gelu_and_grad(g)
        da = da_ref[...]
        dg = da * v * dgl
        dv = da * gl

        def conv_bwd(du, w_ref, x0, x1, x2, dx_ref, dw_ref, db_ref):
            dx = w_ref[2:3, :] * du + w_ref[1:2, :] * _shift_up(du, 1, row) + w_ref[0:1, :] * _shift_up(du, 2, row)
            dx_ref[...] = dx.astype(dx_ref.dtype)
            dw_ref[0:1, :] += jnp.sum(du * x2, axis=0, keepdims=True)
            dw_ref[1:2, :] += jnp.sum(du * x1, axis=0, keepdims=True)
            dw_ref[2:3, :] += jnp.sum(du * x0, axis=0, keepdims=True)
            db_ref[...] += jnp.sum(du, axis=0, keepdims=True)

        conv_bwd(dg, wg_ref, ug, ug1, ug2, dug_ref, dwg_ref, dbg_ref)
        conv_bwd(dv, wv_ref, uv, uv1, uv2, duv_ref, dwv_ref, dbv_ref)

    blk = pl.BlockSpec((t, tc), lambda j, b: (b, j))
    wspec = pl.BlockSpec((FFN_CONV, tc), lambda j, b: (0, j))
    bspec = pl.BlockSpec((1, tc), lambda j, b: (0, j))
    outs = pl.pallas_call(
        body, grid=(nf, bsz),
        in_specs=[blk, pl.BlockSpec((t, tc), lambda j, b: (b, j + nf)), blk,
                  wspec, pl.BlockSpec((FFN_CONV, tc), lambda j, b: (0, j + nf)),
                  bspec, pl.BlockSpec((1, tc), lambda j, b: (0, j + nf))],
        out_specs=[blk, blk, wspec, wspec, bspec, bspec],
        out_shape=[jax.ShapeDtypeStruct((n, f), MXU), jax.ShapeDtypeStruct((n, f), MXU),
                   jax.ShapeDtypeStruct((FFN_CONV, f), F32), jax.ShapeDtypeStruct((FFN_CONV, f), F32),
                   jax.ShapeDtypeStruct((1, f), F32), jax.ShapeDtypeStruct((1, f), F32)],
        name=name, compiler_params=_cp((PAR, ARB)))(up, up, dact, wconv, wconv, bconv, bconv)
    dug, duv, dwg, dwv, dbg, dbv = outs
    return dug, duv, jnp.concatenate([dwg, dwv], axis=1), jnp.concatenate([dbg, dbv], axis=1)


def _mem_attn_fwd(proj, q_col_block, mkv, ycat, bsz, *, name):
    n = proj.shape[0]
    t = n // bsz
    tq = _tile(t, 512, SUBLANE)
    nt = t // tq
    scale = HEAD_DIM ** -0.5

    def body(q_ref, kv_ref, old_ref, o_ref):
        del old_ref
        outs = []
        for h in range(MEM_HEADS):
            sl = slice(h * HEAD_DIM, (h + 1) * HEAD_DIM)
            q = q_ref[:, sl].astype(MXU)
            k = kv_ref[:, sl].astype(MXU)
            v = kv_ref[:, MEM_WIDTH + h * HEAD_DIM: MEM_WIDTH + (h + 1) * HEAD_DIM].astype(MXU)
            s = _dot_nt(q, k) * scale
            m = jnp.max(s, axis=-1, keepdims=True)
            p = jnp.exp(s - m)
            p = p / jnp.sum(p, axis=-1, keepdims=True)
            outs.append(_dot(p.astype(MXU), v))
        o_ref[...] = jnp.concatenate(outs, axis=-1).astype(o_ref.dtype)

    return pl.pallas_call(
        body, grid=(bsz, nt),
        in_specs=[pl.BlockSpec((tq, MEM_WIDTH), lambda b, i: (b * nt + i, q_col_block)),
                  pl.BlockSpec((MEM_LEN, 2 * MEM_WIDTH), lambda b, i: (b, 0)),
                  pl.BlockSpec(memory_space=pl.ANY)],
        out_specs=pl.BlockSpec((tq, MEM_WIDTH), lambda b, i: (b * nt + i, MIX_WIDTH // MEM_WIDTH)),
        out_shape=jax.ShapeDtypeStruct(ycat.shape, ycat.dtype),
        input_output_aliases={2: 0},
        name=name, compiler_params=_cp((PAR, PAR)))(proj, mkv, ycat)


def _mem_attn_bwd(proj, q_col_block, mkv, dycat, dproj, bsz, *, name):
    n = proj.shape[0]
    t = n // bsz
    tq = _tile(t, 512, SUBLANE)
    nt = t // tq
    scale = HEAD_DIM ** -0.5

    def body(q_ref, kv_ref, do_ref, old_ref, dq_ref, dkv_ref):
        del old_ref

        @pl.when(pl.program_id(1) == 0)
        def _():
            dkv_ref[...] = jnp.zeros_like(dkv_ref)

        dqs, dks, dvs = [], [], []
        for h in range(MEM_HEADS):
            sl = slice(h * HEAD_DIM, (h + 1) * HEAD_DIM)
            q = q_ref[:, sl].astype(MXU)
            k = kv_ref[:, sl].astype(MXU)
            v = kv_ref[:, MEM_WIDTH + h * HEAD_DIM: MEM_WIDTH + (h + 1) * HEAD_DIM].astype(MXU)
            do = do_ref[:, sl].astype(MXU)
            s = _dot_nt(q, k) * scale
            m = jnp.max(s, axis=-1, keepdims=True)
            p = jnp.exp(s - m)
            p = p / jnp.sum(p, axis=-1, keepdims=True)
            dvs.append(_dot_tn(p.astype(MXU), do))
            dp = _dot_nt(do, v)
            ds = (p * (dp - jnp.sum(dp * p, axis=-1, keepdims=True)) * scale).astype(MXU)
            dqs.append(_dot(ds, k))
            dks.append(_dot_tn(ds, q))
        dq_ref[...] = jnp.concatenate(dqs, axis=-1).astype(dq_ref.dtype)
        dkv_ref[...] += jnp.concatenate(dks + dvs, axis=-1)

    return pl.pallas_call(
        body, grid=(bsz, nt),
        in_specs=[pl.BlockSpec((tq, MEM_WIDTH), lambda b, i: (b * nt + i, q_col_block)),
                  pl.BlockSpec((MEM_LEN, 2 * MEM_WIDTH), lambda b, i: (b, 0)),
                  pl.BlockSpec((tq, MEM_WIDTH), lambda b, i: (b * nt + i, MIX_WIDTH // MEM_WIDTH)),
                  pl.BlockSpec(memory_space=pl.ANY)],
        out_specs=[pl.BlockSpec((tq, MEM_WIDTH), lambda b, i: (b * nt + i, q_col_block)),
                   pl.BlockSpec((MEM_LEN, 2 * MEM_WIDTH), lambda b, i: (b, 0))],
        out_shape=[jax.ShapeDtypeStruct(dproj.shape, dproj.dtype),
                   jax.ShapeDtypeStruct((bsz * MEM_LEN, 2 * MEM_WIDTH), F32)],
        input_output_aliases={3: 0},
        name=name, compiler_params=_cp((PAR, ARB)))(proj, mkv, dycat, dproj)


def _swa_scores(q, k, h, dist, mask, sink):
    s = _dot_nt(q, k) * (HEAD_DIM ** -0.5)
    s = jnp.where(mask, s - SLOPES[h] * dist, -jnp.inf)
    m = jnp.maximum(jnp.max(s, axis=-1, keepdims=True), sink)
    p = jnp.exp(s - m)
    psink = jnp.exp(sink - m)
    inv = 1.0 / (jnp.sum(p, axis=-1, keepdims=True) + psink)
    return p * inv, psink * inv


def _swa_mask(n):
    qi = lax.broadcasted_iota(jnp.int32, (WINDOW, 2 * WINDOW), 0) + WINDOW
    ki = lax.broadcasted_iota(jnp.int32, (WINDOW, 2 * WINDOW), 1)
    dist = qi - ki
    mask = (dist >= 0) & (dist < WINDOW) & ((n > 0) | (ki >= WINDOW))
    return dist.astype(F32), mask


def _swa_fwd(proj, kv, sinks, bsz, *, name):
    n_tok = proj.shape[0]
    nb = n_tok // bsz // WINDOW
    kvw = SWA_KV_HEADS * HEAD_DIM

    def body(sink_ref, q_ref, kvp_ref, kvc_ref, o_ref):
        n = pl.program_id(1)
        dist, mask = _swa_mask(n)
        kk = jnp.concatenate([kvp_ref[:, :kvw], kvc_ref[:, :kvw]], axis=0).astype(MXU)
        vv = jnp.concatenate([kvp_ref[:, kvw:], kvc_ref[:, kvw:]], axis=0).astype(MXU)
        outs = []
        for h in range(SWA_HEADS):
            c = h // SWA_GROUP
            q = q_ref[:, h * HEAD_DIM:(h + 1) * HEAD_DIM].astype(MXU)
            p, _ = _swa_scores(q, kk[:, c * HEAD_DIM:(c + 1) * HEAD_DIM], h, dist, mask, sink_ref[h])
            outs.append(_dot(p.astype(MXU), vv[:, c * HEAD_DIM:(c + 1) * HEAD_DIM]))
        o_ref[...] = jnp.concatenate(outs, axis=-1).astype(o_ref.dtype)

    return pl.pallas_call(
        body, grid=(bsz, nb),
        in_specs=[pl.BlockSpec(memory_space=pltpu.SMEM),
                  pl.BlockSpec((WINDOW, MIX_WIDTH), lambda b, n: (b * nb + n, 0)),
                  pl.BlockSpec((WINDOW, 2 * kvw), lambda b, n: (b * nb + jnp.maximum(n - 1, 0), 0)),
                  pl.BlockSpec((WINDOW, 2 * kvw), lambda b, n: (b * nb + n, 0))],
        out_specs=pl.BlockSpec((WINDOW, MIX_WIDTH), lambda b, n: (b * nb + n, 0)),
        out_shape=jax.ShapeDtypeStruct((n_tok, D_MODEL), MXU),
        name=name, compiler_params=_cp((PAR, PAR)))(sinks, proj, kv, kv)


def _swa_bwd(proj, kv, sinks, dycat, bsz, *, name):
    n_tok = proj.shape[0]
    nb = n_tok // bsz // WINDOW
    kvw = SWA_KV_HEADS * HEAD_DIM

    def body(sink_ref, q_ref, kvp_ref, kvc_ref, do_ref, dq_ref, dkvc_ref, dkvp_ref, dsink_ref):
        n = pl.program_id(1)

        @pl.when((pl.program_id(0) == 0) & (n == 0))
        def _():
            dsink_ref[...] = jnp.zeros_like(dsink_ref)

        dist, mask = _swa_mask(n)
        kk = jnp.concatenate([kvp_ref[:, :kvw], kvc_ref[:, :kvw]], axis=0).astype(MXU)
        vv = jnp.concatenate([kvp_ref[:, kvw:], kvc_ref[:, kvw:]], axis=0).astype(MXU)
        lane = lax.broadcasted_iota(jnp.int32, (SUBLANE, LANE), 1)
        dqs = []
        dks = [None] * SWA_KV_HEADS
        dvs = [None] * SWA_KV_HEADS
        dsink = jnp.zeros((SUBLANE, LANE), F32)
        for h in range(SWA_HEADS):
            c = h // SWA_GROUP
            k = kk[:, c * HEAD_DIM:(c + 1) * HEAD_DIM]
            v = vv[:, c * HEAD_DIM:(c + 1) * HEAD_DIM]
            q = q_ref[:, h * HEAD_DIM:(h + 1) * HEAD_DIM].astype(MXU)
            do = do_ref[:, h * HEAD_DIM:(h + 1) * HEAD_DIM].astype(MXU)
            p, psink = _swa_scores(q, k, h, dist, mask, sink_ref[h])
            dv = _dot_tn(p.astype(MXU), do)
            dp = _dot_nt(do, v)
            rs = jnp.sum(dp * p, axis=-1, keepdims=True)
            ds = (p * (dp - rs) * (HEAD_DIM ** -0.5)).astype(MXU)
            dsink = dsink + jnp.where(lane == h, jnp.sum(-psink * rs, axis=0, keepdims=True), 0.0)
            dqs.append(_dot(ds, k))
            dk = _dot_tn(ds, q)
            dks[c] = dk if dks[c] is None else dks[c] + dk
            dvs[c] = dv if dvs[c] is None else dvs[c] + dv
        dq_ref[...] = jnp.concatenate(dqs, axis=-1).astype(dq_ref.dtype)
        dkv = jnp.concatenate(dks + dvs, axis=-1)
        dkvp_ref[...] = dkv[:WINDOW]
        dkvc_ref[...] = dkv[WINDOW:]
        dsink_ref[...] += dsink

    qspec = pl.BlockSpec((WINDOW, MIX_WIDTH), lambda b, n: (b * nb + n, 0))
    kvspec = pl.BlockSpec((WINDOW, 2 * kvw), lambda b, n: (b * nb + n, 0))
    return pl.pallas_call(
        body, grid=(bsz, nb),
        in_specs=[pl.BlockSpec(memory_space=pltpu.SMEM), qspec,
                  pl.BlockSpec((WINDOW, 2 * kvw), lambda b, n: (b * nb + jnp.maximum(n - 1, 0), 0)),
                  kvspec, qspec],
        out_specs=[qspec, kvspec, kvspec, pl.BlockSpec((SUBLANE, LANE), lambda b, n: (0, 0))],
        out_shape=[jax.ShapeDtypeStruct((n_tok, D_MODEL), MXU),
                   jax.ShapeDtypeStruct((n_tok, 2 * kvw), F32),
                   jax.ShapeDtypeStruct((n_tok, 2 * kvw), F32),
                   jax.ShapeDtypeStruct((SUBLANE, LANE), F32)],
        name=name, compiler_params=_cp((ARB, ARB)))(sinks, proj, kv, kv, dycat)


def _swa_dkv_combine(curs, prevs, bsz, *, name):
    n_tok, w = curs[0].shape
    nb = n_tok // bsz // WINDOW
    k = len(curs)

    def body(*refs):
        o_ref = refs[-1]
        n = pl.program_id(1)
        acc = refs[0][...]
        for r in refs[1:k]:
            acc = acc + r[...]
        nxt = refs[k][...]
        for r in refs[k + 1:2 * k]:
            nxt = nxt + r[...]
        o_ref[...] = (acc + jnp.where(n < nb - 1, nxt, 0.0)).astype(o_ref.dtype)

    cur = pl.BlockSpec((WINDOW, w), lambda b, n: (b * nb + n, 0))
    prv = pl.BlockSpec((WINDOW, w), lambda b, n: (b * nb + jnp.minimum(n + 1, nb - 1), 0))
    return pl.pallas_call(
        body, grid=(bsz, nb), in_specs=[cur] * k + [prv] * k, out_specs=cur,
        out_shape=jax.ShapeDtypeStruct((n_tok, w), MXU),
        name=name, compiler_params=_cp((PAR, PAR)))(*curs, *prevs)


def _lru_gates(ux, halo, ext_ref, wc_ref, bc_ref, wr_ref, br_ref, wi_ref, bi_ref, lam_ref):
    tt = ux.shape[0]
    ext_ref[0:SUBLANE, :] = halo
    ext_ref[SUBLANE:, :] = ux
    xs = [ux] + [ext_ref[pl.ds(SUBLANE - k, tt), :] for k in range(1, LRU_CONV)]
    xc = bc_ref[...] + wc_ref[3:4, :] * xs[0] + wc_ref[2:3, :] * xs[1] + wc_ref[1:2, :] * xs[2] + wc_ref[0:1, :] * xs[3]
    pre_r, pre_i = [], []
    for blk in range(MIX_WIDTH // GATE_TILE):
        xb = xc[:, blk * GATE_TILE:(blk + 1) * GATE_TILE].astype(MXU)
        pre_r.append(_dot(xb, wr_ref[blk]))
        pre_i.append(_dot(xb, wi_ref[blk]))
    r = jax.nn.sigmoid(jnp.concatenate(pre_r, axis=-1) + br_ref[...])
    i = jax.nn.sigmoid(jnp.concatenate(pre_i, axis=-1) + bi_ref[...])
    nlam = -lam_ref[...]
    sp = jnp.maximum(nlam, 0.0) + jnp.log(1.0 + jnp.exp(-jnp.abs(nlam)))
    log_a = -LRU_C * r * sp
    a = jnp.exp(log_a)
    om = -jnp.tanh(log_a) * (a * a + 1.0)
    s = jnp.sqrt(om)
    return xs, xc, r, i, sp, a, s


def _lru_fwd(proj, wconv, bconv, wr, br, wi, bi, lam, bsz, *, name):
    n_tok = proj.shape[0]
    t = n_tok // bsz
    tt = _tile(t, 256, SUBLANE)
    nt = t // tt
    w = MIX_WIDTH
    ng = tt // SUBLANE

    def body(pg_ref, halo_ref, wc_ref, bc_ref, wr_ref, br_ref, wi_ref, bi_ref, lam_ref,
             y_ref, h_ref, ext_ref, a_ref, b_ref, carry_ref):
        ti = pl.program_id(1)

        @pl.when(ti == 0)
        def _():
            carry_ref[...] = jnp.zeros_like(carry_ref)

        gate = pg_ref[:, :w]
        ux = pg_ref[:, w:]
        halo = jnp.where(ti > 0, halo_ref[...], 0.0)
        _, xc, _, i, _, a, s = _lru_gates(ux, halo, ext_ref, wc_ref, bc_ref, wr_ref, br_ref, wi_ref, bi_ref, lam_ref)
        a_ref[...] = a
        b_ref[...] = s * (i * xc)
        row = lax.broadcasted_iota(jnp.int32, (SUBLANE, w), 0)

        def group(g, hprev):
            off = pl.multiple_of(g * SUBLANE, SUBLANE)
            ca = a_ref[pl.ds(off, SUBLANE), :]
            cb = b_ref[pl.ds(off, SUBLANE), :]
            for d in (1, 2, 4):
                a_sh = jnp.where(row >= d, pltpu.roll(ca, d, axis=0), 1.0)
                b_sh = jnp.where(row >= d, pltpu.roll(cb, d, axis=0), 0.0)
                cb = ca * b_sh + cb
                ca = ca * a_sh
            h = ca * hprev + cb
            b_ref[pl.ds(off, SUBLANE), :] = h
            return jnp.broadcast_to(h[SUBLANE - 1:SUBLANE, :], (SUBLANE, w))

        carry_ref[...] = lax.fori_loop(0, ng, group, carry_ref[...])
        h = b_ref[...]
        h_ref[...] = h
        y_ref[...] = (h * _gelu(gate)).astype(y_ref.dtype)

    vec = lambda r: pl.BlockSpec((r, w), lambda b, i: (0, 0))
    wspec = pl.BlockSpec((w // GATE_TILE, GATE_TILE, GATE_TILE), lambda b, i: (0, 0, 0))
    hb = tt // SUBLANE
    return pl.pallas_call(
        body, grid=(bsz, nt),
        in_specs=[pl.BlockSpec((tt, 2 * w), lambda b, i: (b * nt + i, 0)),
                  pl.BlockSpec((SUBLANE, w), lambda b, i: (jnp.maximum((b * nt + i) * hb - 1, 0), 1)),
                  vec(LRU_CONV), vec(1), wspec, vec(1), wspec, vec(1), vec(1)],
        out_specs=[pl.BlockSpec((tt, w), lambda b, i: (b * nt + i, 0)),
                   pl.BlockSpec((tt, w), lambda b, i: (b * nt + i, 0))],
        out_shape=[jax.ShapeDtypeStruct((n_tok, D_MODEL), MXU), jax.ShapeDtypeStruct((n_tok, w), F32)],
        scratch_shapes=[pltpu.VMEM((tt + SUBLANE, w), F32), pltpu.VMEM((tt, w), F32),
                        pltpu.VMEM((tt, w), F32), pltpu.VMEM((SUBLANE, w), F32)],
        name=name, compiler_params=_cp((PAR, ARB)))(proj, proj, wconv, bconv, wr, br, wi, bi, lam)


def _lru_bwd(proj, hs, dycat, wconv, bconv, wr, br, wi, bi, lam, bsz, *, name):
    n_tok = proj.shape[0]
    t = n_tok // bsz
    tt = _tile(t, 256, SUBLANE)
    nt = t // tt
    w = MIX_WIDTH
    ng = tt // SUBLANE
    nblk = w // GATE_TILE

    def body(pg_ref, halo_ref, h_ref, hhalo_ref, dy_ref, wc_ref, bc_ref, wr_ref, br_ref, wi_ref, bi_ref, lam_ref,
             dp_ref, dwc_ref, dbc_ref, dwr_ref, dbr_ref, dwi_ref, dbi_ref, dlam_ref,
             ext_ref, a_ref, c_ref, g_ref, gcarry_ref, xcarry_ref):
        bi_ = pl.program_id(0)
        ti = nt - 1 - pl.program_id(1)

        @pl.when((bi_ == 0) & (pl.program_id(1) == 0))
        def _():
            for r in (dwc_ref, dbc_ref, dwr_ref, dbr_ref, dwi_ref, dbi_ref, dlam_ref):
                r[...] = jnp.zeros_like(r)

        @pl.when(pl.program_id(1) == 0)
        def _():
            gcarry_ref[...] = jnp.zeros_like(gcarry_ref)
            xcarry_ref[...] = jnp.zeros_like(xcarry_ref)

        gate = pg_ref[:, :w]
        ux = pg_ref[:, w:]
        halo = jnp.where(ti > 0, halo_ref[...], 0.0)
        xs, xc, r, i, sp, a, s = _lru_gates(ux, halo, ext_ref, wc_ref, bc_ref, wr_ref, br_ref, wi_ref, bi_ref, lam_ref)
        h = h_ref[...]
        gl, dgl = _gelu_and_grad(gate)
        dy = dy_ref[...]
        dgate = dy * h * dgl
        row_t = lax.broadcasted_iota(jnp.int32, (tt, w), 0)
        g_ref[...] = dy * gl + jnp.where(row_t == tt - 1, gcarry_ref[0:1, :], 0.0)
        c_ref[...] = _shift_up(a, 1, row_t)
        row = lax.broadcasted_iota(jnp.int32, (SUBLANE, w), 0)

        a_ref[...] = a

        def group(k, gnext):
            off = pl.multiple_of((ng - 1 - k) * SUBLANE, SUBLANE)
            cc = c_ref[pl.ds(off, SUBLANE), :]
            cb = g_ref[pl.ds(off, SUBLANE), :]
            cb = cb + jnp.where(row == SUBLANE - 1, gnext, 0.0)
            cc = jnp.where(row == SUBLANE - 1, 0.0, cc)
            for d in (1, 2, 4):
                c_sh = jnp.where(row < SUBLANE - d, pltpu.roll(cc, SUBLANE - d, axis=0), 1.0)
                b_sh = jnp.where(row < SUBLANE - d, pltpu.roll(cb, SUBLANE - d, axis=0), 0.0)
                cb = cc * b_sh + cb
                cc = cc * c_sh
            g_ref[pl.ds(off, SUBLANE), :] = cb
            a0 = a_ref[pl.ds(off, SUBLANE), :]
            return jnp.broadcast_to(a0[0:1, :] * cb[0:1, :], (SUBLANE, w))

        gc = lax.fori_loop(0, ng, group, jnp.zeros((SUBLANE, w), F32))
        gcarry_ref[...] = gc
        gsc = g_ref[...]

        hhalo = jnp.where(ti > 0, hhalo_ref[SUBLANE - 1:SUBLANE, :], 0.0)
        hprev = jnp.where(row_t == 0, hhalo, pltpu.roll(h, 1, axis=0))
        gated = i * xc
        d_gated = gsc * s
        d_atot = gsc * hprev - (gsc * gated) * a / s
        d_loga = d_atot * a
        d_r = d_loga * (-LRU_C) * sp
        dlam_ref[...] += jnp.sum(d_loga * r, axis=0, keepdims=True) * (LRU_C * jax.nn.sigmoid(-lam_ref[...]))
        d_i = d_gated * xc
        d_xc = d_gated * i
        d_pr = d_r * r * (1.0 - r)
        d_pi = d_i * i * (1.0 - i)
        dbr_ref[...] += jnp.sum(d_pr, axis=0, keepdims=True)
        dbi_ref[...] += jnp.sum(d_pi, axis=0, keepdims=True)
        extra = []
        for blk in range(nblk):
            sl = slice(blk * GATE_TILE, (blk + 1) * GATE_TILE)
            xb = xc[:, sl].astype(MXU)
            dr_b = d_pr[:, sl].astype(MXU)
            di_b = d_pi[:, sl].astype(MXU)
            dwr_ref[blk] += _dot_tn(xb, dr_b)
            dwi_ref[blk] += _dot_tn(xb, di_b)
            extra.append(_dot_nt(dr_b, wr_ref[blk]) + _dot_nt(di_b, wi_ref[blk]))
        d_xc = d_xc + jnp.concatenate(extra, axis=-1)
        dbc_ref[...] += jnp.sum(d_xc, axis=0, keepdims=True)
        for k in range(LRU_CONV):
            dwc_ref[k:k + 1, :] += jnp.sum(d_xc * xs[LRU_CONV - 1 - k], axis=0, keepdims=True)
        ext_ref[0:tt, :] = d_xc
        ext_ref[tt:, :] = xcarry_ref[...]
        dux = wc_ref[3:4, :] * d_xc
        for k in range(LRU_CONV - 1):
            dux = dux + wc_ref[k:k + 1, :] * ext_ref[pl.ds(LRU_CONV - 1 - k, tt), :]
        xcarry_ref[...] = d_xc[0:SUBLANE, :]
        dp_ref[:, :w] = dgate.astype(dp_ref.dtype)
        dp_ref[:, w:] = dux.astype(dp_ref.dtype)

    vec = lambda r: pl.BlockSpec((r, w), lambda b, i: (0, 0))
    wspec = pl.BlockSpec((nblk, GATE_TILE, GATE_TILE), lambda b, i: (0, 0, 0))
    hb = tt // SUBLANE
    rblk = lambda b, i: b * nt + (nt - 1 - i)
    halo_idx = lambda b, i: jnp.maximum(rblk(b, i) * hb - 1, 0)
    wide = pl.BlockSpec((tt, 2 * w), lambda b, i: (rblk(b, i), 0))
    narrow = pl.BlockSpec((tt, w), lambda b, i: (rblk(b, i), 0))
    return pl.pallas_call(
        body, grid=(bsz, nt),
        in_specs=[wide, pl.BlockSpec((SUBLANE, w), lambda b, i: (halo_idx(b, i), 1)),
                  narrow, pl.BlockSpec((SUBLANE, w), lambda b, i: (halo_idx(b, i), 0)), narrow,
                  vec(LRU_CONV), vec(1), wspec, vec(1), wspec, vec(1), vec(1)],
        out_specs=[wide, vec(LRU_CONV), vec(1), wspec, vec(1), wspec, vec(1), vec(1)],
        out_shape=[jax.ShapeDtypeStruct((n_tok, 2 * w + MEM_WIDTH), MXU),
                   jax.ShapeDtypeStruct((LRU_CONV, w), F32), jax.ShapeDtypeStruct((1, w), F32),
                   jax.ShapeDtypeStruct((nblk, GATE_TILE, GATE_TILE), F32), jax.ShapeDtypeStruct((1, w), F32),
                   jax.ShapeDtypeStruct((nblk, GATE_TILE, GATE_TILE), F32), jax.ShapeDtypeStruct((1, w), F32),
                   jax.ShapeDtypeStruct((1, w), F32)],
        scratch_shapes=[pltpu.VMEM((tt + SUBLANE, w), F32), pltpu.VMEM((tt, w), F32), pltpu.VMEM((tt, w), F32),
                        pltpu.VMEM((tt, w), F32), pltpu.VMEM((SUBLANE, w), F32), pltpu.VMEM((SUBLANE, w), F32)],
        name=name, compiler_params=_cp((ARB, ARB)))(proj, proj, hs, hs, dycat, wconv, bconv, wr, br, wi, bi, lam)


def _gate_tiles(w):
    per = GATE_TILE // HEAD_DIM
    w4 = w.reshape(LRU_BLOCKS // per, per, HEAD_DIM, HEAD_DIM)
    eye = jnp.eye(per, dtype=w.dtype)
    return jnp.einsum("bnij,nm->bnimj", w4, eye).reshape(LRU_BLOCKS // per, GATE_TILE, GATE_TILE)


def _gate_blocks(t):
    per = GATE_TILE // HEAD_DIM
    t5 = t.reshape(LRU_BLOCKS // per, per, HEAD_DIM, per, HEAD_DIM)
    eye = jnp.eye(per, dtype=t.dtype)
    return jnp.einsum("bnimj,nm->bnij", t5, eye).reshape(LRU_BLOCKS, HEAD_DIM, HEAD_DIM)


def _row(v):
    return v.reshape(1, -1)


def _local_step(x, mem, target, p):
    bsz, t, d = x.shape
    n = bsz * t
    x2d = x.reshape(n, d)
    tgt = target.reshape(n, d)
    mem2d = mem.reshape(bsz * MEM_LEN, d)
    f32z = lambda a: jnp.zeros(a.shape, F32)
    dw = {k: f32z(p[k]) for k in ("w_mem_kv", "w_mix_out", "w_ffn_up", "w_ffn_down", "w_in_a", "w_in_b", "w_kv")}
    wr_t = [_gate_tiles(p["w_rg_r"][j]).astype(MXU) for j in range(N_A)]
    wi_t = [_gate_tiles(p["w_rg_i"][j]).astype(MXU) for j in range(N_A)]

    mn = [_norm_fwd(mem2d, _row(p["g_mem"][l]), name=f"mem_norm{l}") for l in range(DEPTH)]
    mkv = [_mm_nn(mn[l], p["w_mem_kv"], l, name=f"mem_kv{l}") for l in range(DEPTH)]
    h = _norm_fwd(x2d, _row(p["g_mix_pre"][0]), name="in_norm")
    xin = x2d
    sv = []
    kv = hkv = None
    for l in range(DEPTH):
        s = {"xin": xin, "h": h}
        if l < N_A:
            proj = _mm_nn(h, p["w_in_a"], l, name=f"in_proj{l}")
            ycat, hs = _lru_fwd(proj, p["w_conv_a"][l], _row(p["b_conv_a"][l]), wr_t[l], _row(p["b_rg_r"][l]),
                                wi_t[l], _row(p["b_rg_i"][l]), _row(p["lru_lambda"][l]), bsz, name=f"lru_fwd{l}")
            s["hs"] = hs
            qblk = 2 * MIX_WIDTH // MEM_WIDTH
        else:
            if l == N_A:
                kv = _mm_nn(hkv, p["w_kv"], 0, name="kv_proj")
            proj = _mm_nn(h, p["w_in_b"], l - N_A, name=f"in_proj{l}")
            ycat = _swa_fwd(proj, kv, p["sinks_b"][l - N_A], bsz, name=f"swa_fwd{l}")
            qblk = MIX_WIDTH // MEM_WIDTH
        ycat = _mem_attn_fwd(proj, qblk, mkv[l], ycat, bsz, name=f"mem_attn_fwd{l}")
        y = _mm_nn(ycat, p["w_mix_out"], l, name=f"mix_out{l}")
        x1, (h2,) = _resid_norm_fwd(xin, y, _row(p["g_mix_post"][l]), [_row(p["g_ffn_pre"][l])], name=f"mix_resid{l}")
        up = _mm_nn(h2, p["w_ffn_up"], l, name=f"ffn_up{l}")
        act = _ffn_act_fwd(up, p["w_ffn_conv"][l], _row(p["b_ffn_conv"][l]), bsz, name=f"ffn_act{l}")
        f = _mm_nn(act, p["w_ffn_down"], l, name=f"ffn_down{l}")
        s.update(proj=proj, qblk=qblk, ycat=ycat, y=y, x1=x1, h2=h2, up=up, act=act, f=f)
        sv.append(s)
        if l < DEPTH - 1:
            g_pres = [_row(p["g_mix_pre"][l + 1])] + ([_row(p["g_kv"])] if l + 1 == N_A else [])
            xin, hn = _resid_norm_fwd(x1, f, _row(p["g_ffn_post"][l]), g_pres, name=f"ffn_resid{l}")
            h = hn[0]
            if l + 1 == N_A:
                hkv = hn[1]
        else:
            g_tot, sq = _loss_fwd(x1, f, _row(p["g_ffn_post"][l]), tgt, name="loss")

    gs = {k: [None] * DEPTH for k in ("g_mix_pre", "g_mix_post", "g_ffn_pre", "g_ffn_post", "g_mem",
                                       "w_ffn_conv", "b_ffn_conv")}
    ga = {k: [None] * N_A for k in ("w_conv_a", "b_conv_a", "w_rg_r", "b_rg_r", "w_rg_i", "b_rg_i", "lru_lambda")}
    gsink = [None] * (DEPTH - N_A)
    dkv_cur, dkv_prev = [], []
    g_tot, df, _, gs["g_ffn_post"][DEPTH - 1] = _resid_norm_bwd(
        g_tot, [], None, [], sv[-1]["f"], _row(p["g_ffn_post"][DEPTH - 1]), name="loss_bwd")
    grad_x = None
    for l in reversed(range(DEPTH)):
        s = sv[l]
        dact = _mm_nt(df, p["w_ffn_down"], l, name=f"d_act{l}")
        dw["w_ffn_down"] = _mm_tn_into(s["act"], df, dw["w_ffn_down"], l, name=f"dw_down{l}")
        dug, duv, gs["w_ffn_conv"][l], gs["b_ffn_conv"][l] = _ffn_act_bwd(
            s["up"], dact, p["w_ffn_conv"][l], _row(p["b_ffn_conv"][l]), bsz, name=f"ffn_act_bwd{l}")
        dh2 = _mm_ffn_dh(dug, duv, p["w_ffn_up"], l, name=f"d_h2_{l}")
        dw["w_ffn_up"] = _mm_tn_into(s["h2"], dug, dw["w_ffn_up"], l, name=f"dw_up_g{l}")
        dw["w_ffn_up"] = _mm_tn_into(s["h2"], duv, dw["w_ffn_up"], l, name=f"dw_up_v{l}", col_block_offset=1)
        g1, dy, (gs["g_ffn_pre"][l],), gs["g_mix_post"][l] = _resid_norm_bwd(
            g_tot, [dh2], s["x1"], [_row(p["g_ffn_pre"][l])], s["y"], _row(p["g_mix_post"][l]), name=f"mix_resid_bwd{l}")
        dycat = _mm_nt(dy, p["w_mix_out"], l, name=f"d_ycat{l}")
        dw["w_mix_out"] = _mm_tn_into(s["ycat"], dy, dw["w_mix_out"], l, name=f"dw_mix_out{l}")
        if l < N_A:
            dproj, dwc, dbc, dwr, dbr, dwi, dbi, dlam = _lru_bwd(
                s["proj"], s["hs"], dycat, p["w_conv_a"][l], _row(p["b_conv_a"][l]), wr_t[l], _row(p["b_rg_r"][l]),
                wi_t[l], _row(p["b_rg_i"][l]), _row(p["lru_lambda"][l]), bsz, name=f"lru_bwd{l}")
            ga["w_conv_a"][l], ga["b_conv_a"][l], ga["lru_lambda"][l] = dwc, dbc[0], dlam[0]
            ga["w_rg_r"][l], ga["w_rg_i"][l] = _gate_blocks(dwr), _gate_blocks(dwi)
            ga["b_rg_r"][l] = dbr.reshape(LRU_BLOCKS, HEAD_DIM)
            ga["b_rg_i"][l] = dbi.reshape(LRU_BLOCKS, HEAD_DIM)
            w_in, j = "w_in_a", l
        else:
            dproj, dc, dp_, dsk = _swa_bwd(s["proj"], kv, p["sinks_b"][l - N_A], dycat, bsz, name=f"swa_bwd{l}")
            dkv_cur.append(dc)
            dkv_prev.append(dp_)
            gsink[l - N_A] = dsk[0, :SWA_HEADS]
            w_in, j = "w_in_b", l - N_A
        dproj, dmkv = _mem_attn_bwd(s["proj"], s["qblk"], mkv[l], dycat, dproj, bsz, name=f"mem_attn_bwd{l}")
        dh = _mm_nt(dproj, p[w_in], j, name=f"d_h{l}")
        dw[w_in] = _mm_tn_into(s["h"], dproj, dw[w_in], j, name=f"dw_in{l}")
        dmkv = dmkv.astype(MXU)
        dmn = _mm_nt(dmkv, p["w_mem_kv"], l, name=f"d_mem_norm{l}")
        dw["w_mem_kv"] = _mm_tn_into(mn[l], dmkv, dw["w_mem_kv"], l, name=f"dw_mem_kv{l}")
        gs["g_mem"][l] = _norm_bwd_dg(dmn, mem2d, _row(p["g_mem"][l]), name=f"mem_norm_bwd{l}")
        dhs, g_pres = [dh], [_row(p["g_mix_pre"][l])]
        if l == N_A:
            dkv = _swa_dkv_combine(dkv_cur, dkv_prev, bsz, name="dkv_combine")
            dhs.append(_mm_nt(dkv, p["w_kv"], 0, name="d_hkv"))
            g_pres.append(_row(p["g_kv"]))
            dw["w_kv"] = _mm_tn_into(hkv, dkv, dw["w_kv"], 0, name="dw_kv")
        if l > 0:
            g_tot, df, dgpre, gs["g_ffn_post"][l - 1] = _resid_norm_bwd(
                g1, dhs, s["xin"], g_pres, sv[l - 1]["f"], _row(p["g_ffn_post"][l - 1]), name=f"ffn_resid_bwd{l - 1}")
        else:
            grad_x, _, dgpre, _ = _resid_norm_bwd(g1, dhs, s["xin"], g_pres, None, None, name="in_norm_bwd")
        gs["g_mix_pre"][l] = dgpre[0]
        if l == N_A:
            g_kv = dgpre[1][0]

    grads = dict(dw)
    for k in ("g_mix_pre", "g_mix_post", "g_ffn_pre", "g_ffn_post", "g_mem", "b_ffn_conv"):
        grads[k] = jnp.concatenate(gs[k], axis=0)
    grads["w_ffn_conv"] = jnp.stack(gs["w_ffn_conv"])
    for k, v in ga.items():
        grads[k] = jnp.stack(v)
    grads["sinks_b"] = jnp.stack(gsink)
    grads["g_kv"] = g_kv
    return jnp.sum(sq), grad_x.reshape(bsz, t, d), grads


N_CHIP = 4
HALF_ALIGN = 16


def _full_shape(kind, shard_shape):
    l, r, c = shard_shape
    return {"row": (l, N_CHIP * r, c), "col": (l, r, N_CHIP * c), "slot": (N_CHIP, l, r, c)}[kind]


def _slot_view(ref, kind, shard_shape, s, hf):
    _, r, c = shard_shape
    rh = r // 2
    if hf is None:
        start, size = 0, r
    else:
        start, size = hf * rh, rh
    if kind == "row":
        start = s * r + start
    if not isinstance(start, int):
        start = pl.multiple_of(start, HALF_ALIGN)
    rows = pl.ds(start, size)
    if kind == "row":
        return ref.at[:, rows, :]
    if kind == "col":
        return ref.at[:, rows, pl.ds(s * c, c)]
    return ref.at[s, :, rows, :]


def _half_view(ref, shard_shape, hf):
    rh = shard_shape[1] // 2
    return ref.at[:, pl.ds(pl.multiple_of(hf * rh, HALF_ALIGN), rh), :]


def _with_slot(kind, s, fn):
    if kind != "col" or isinstance(s, int):
        fn(s)
        return
    for k in range(N_CHIP):
        @pl.when(s == k)
        def _(k=k):
            fn(k)


def _mesh_pos():
    return lax.axis_index("x"), lax.axis_index("y"), lax.axis_index("c")


def _other_chips(x, y):
    return [(1 - x, y), (x, 1 - y), (1 - x, 1 - y)]


def _gather_weights(shards, kinds):
    n = len(shards)
    shapes = [a.shape for a in shards]

    def body(*refs):
        ins, outs = refs[:n], refs[n:2 * n]
        send_sems, recv_sems, loc_sems = refs[2 * n:]
        x, y, c = _mesh_pos()
        me = 2 * x + y
        others = _other_chips(x, y)
        sib = (x, y, 1 - c)

        def remote(a, k, src, dst, dev):
            return pltpu.make_async_remote_copy(src_ref=src, dst_ref=dst, send_sem=send_sems.at[a, k],
                                                recv_sem=recv_sems.at[a, k], device_id=dev, device_id_type=MESH_T)

        def any_half(a):
            return _slot_view(outs[a], kinds[a], shapes[a], 0, 0)

        for a in range(n):
            def own(s, a=a):
                pltpu.make_async_copy(ins[a], _slot_view(outs[a], kinds[a], shapes[a], s, None), loc_sems.at[a]).start()
            _with_slot(kinds[a], me, own)
            for j, (ox, oy) in enumerate(others):
                def send(s, a=a, j=j, ox=ox, oy=oy):
                    remote(a, j, _half_view(ins[a], shapes[a], c),
                           _slot_view(outs[a], kinds[a], shapes[a], s, c), (ox, oy, c)).start()
                _with_slot(kinds[a], me, send)
        for a in range(n):
            for j, (ox, oy) in enumerate(others):
                remote(a, j, any_half(a), any_half(a), sib).wait_recv()

                def forward(s, a=a, j=j):
                    v = _slot_view(outs[a], kinds[a], shapes[a], s, c)
                    remote(a, N_CHIP - 1 + j, v, v, sib).start()
                _with_slot(kinds[a], 2 * ox + oy, forward)
        for a in range(n):
            for j in range(N_CHIP - 1):
                remote(a, N_CHIP - 1 + j, any_half(a), any_half(a), sib).wait_recv()
            for k in range(2 * (N_CHIP - 1)):
                remote(a, k, any_half(a), any_half(a), sib).wait_send()
            pltpu.make_async_copy(ins[a], _slot_view(outs[a], kinds[a], shapes[a], 0, None), loc_sems.at[a]).wait()

    hbm = pl.BlockSpec(memory_space=pl.ANY)
    return pl.pallas_call(
        body, in_specs=[hbm] * n, out_specs=[hbm] * n,
        out_shape=[jax.ShapeDtypeStruct(_full_shape(k, a.shape), a.dtype) for k, a in zip(kinds, shards)],
        scratch_shapes=[pltpu.SemaphoreType.DMA((n, 2 * (N_CHIP - 1))), pltpu.SemaphoreType.DMA((n, 2 * (N_CHIP - 1))),
                        pltpu.SemaphoreType.DMA((n,))],
        name="gather_weights", compiler_params=pltpu.CompilerParams(has_side_effects=True))(*shards)


def _exchange8(vec, reduce, *, name):
    r = vec.shape[0]
    n_dev = 8

    def body(v_ref, o_ref, *rest):
        if reduce:
            buf, send_sems, recv_sems = rest
        else:
            buf = o_ref
            send_sems, recv_sems = rest
        x, y, c = _mesh_pos()
        me = 4 * x + 2 * y + c
        copies = []
        for k in range(1, n_dev):
            kx, ky, kc = (k >> 2) & 1, (k >> 1) & 1, k & 1
            peer = ((1 - x) if kx else x, (1 - y) if ky else y, (1 - c) if kc else c)
            cp = pltpu.make_async_remote_copy(src_ref=v_ref, dst_ref=buf.at[me], send_sem=send_sems.at[k - 1],
                                              recv_sem=recv_sems.at[k - 1], device_id=peer, device_id_type=MESH_T)
            cp.start()
            copies.append(cp)
        buf[me] = v_ref[...]
        for cp in copies:
            cp.wait()
        if reduce:
            acc = buf[0]
            for d in range(1, n_dev):
                acc = acc + buf[d]
            o_ref[...] = acc

    vm = pl.BlockSpec(memory_space=pltpu.VMEM)
    scratch = [pltpu.SemaphoreType.DMA((n_dev - 1,)), pltpu.SemaphoreType.DMA((n_dev - 1,))]
    if reduce:
        scratch = [pltpu.VMEM((n_dev, r, LANE), F32)] + scratch
        out_shape = jax.ShapeDtypeStruct((r, LANE), F32)
    else:
        out_shape = jax.ShapeDtypeStruct((n_dev, r, LANE), F32)
    return pl.pallas_call(
        body, in_specs=[vm], out_specs=vm, out_shape=out_shape, scratch_shapes=scratch,
        name=name, compiler_params=pltpu.CompilerParams(has_side_effects=True, vmem_limit_bytes=VMEM_LIMIT_V7X))(vec)


def _rs_pair_send(grads, kinds, shapes):
    n = len(grads)

    def body(*refs):
        ins, outs = refs[:n], refs[n:2 * n]
        send_sems, recv_sems = refs[2 * n:]
        x, y, c = _mesh_pos()
        sib = (x, y, 1 - c)
        copies = []
        for a in range(n):
            for s in range(N_CHIP):
                cp = pltpu.make_async_remote_copy(
                    src_ref=_slot_view(ins[a], kinds[a], shapes[a], s, 1 - c), dst_ref=outs[a].at[s],
                    send_sem=send_sems.at[a, s], recv_sem=recv_sems.at[a, s], device_id=sib, device_id_type=MESH_T)
                cp.start()
                copies.append(cp)
        for cp in copies:
            cp.wait()

    hbm = pl.BlockSpec(memory_space=pl.ANY)
    return pl.pallas_call(
        body, in_specs=[hbm] * n, out_specs=[hbm] * n,
        out_shape=[jax.ShapeDtypeStruct((N_CHIP, s[0], s[1] // 2, s[2]), F32) for s in shapes],
        scratch_shapes=[pltpu.SemaphoreType.DMA((n, N_CHIP)), pltpu.SemaphoreType.DMA((n, N_CHIP))],
        name="rs_pair_send", compiler_params=pltpu.CompilerParams(has_side_effects=True))(*grads)


def _rs_pair_add(g, recv, kind, shape, pos, *, name):
    l, r, c = shape
    rh = r // 2
    if kind == "row":
        gspec = pl.BlockSpec((None, rh, c), lambda s, i, pos: (i, 2 * s + pos[0], 0))
    elif kind == "col":
        gspec = pl.BlockSpec((None, rh, c), lambda s, i, pos: (i, pos[0], s))
    else:
        gspec = pl.BlockSpec((None, None, rh, c), lambda s, i, pos: (s, i, pos[0], 0))
    pspec = pl.BlockSpec((None, None, rh, c), lambda s, i, pos: (s, i, 0, 0))

    def body(pos_ref, g_ref, r_ref, p_ref, pw_ref):
        del pos_ref
        v = g_ref[...] + r_ref[...]
        p_ref[...] = v
        pw_ref[...] = v.astype(pw_ref.dtype)

    return pl.pallas_call(
        body,
        grid_spec=pltpu.PrefetchScalarGridSpec(
            num_scalar_prefetch=1, grid=(N_CHIP, l), in_specs=[gspec, pspec], out_specs=[pspec, pspec]),
        out_shape=[jax.ShapeDtypeStruct((N_CHIP, l, rh, c), F32), jax.ShapeDtypeStruct((N_CHIP, l, rh, c), MXU)],
        name=name, compiler_params=_cp((PAR, PAR)))(pos, g, recv)


def _rs_chip_send(pws, shapes):
    n = len(pws)

    def body(*refs):
        ins, outs = refs[:n], refs[n:2 * n]
        send_sems, recv_sems = refs[2 * n:]
        x, y, c = _mesh_pos()
        copies = []
        for a in range(n):
            for j, (ox, oy) in enumerate(_other_chips(x, y)):
                cp = pltpu.make_async_remote_copy(
                    src_ref=ins[a].at[2 * ox + oy], dst_ref=outs[a].at[j],
                    send_sem=send_sems.at[a, j], recv_sem=recv_sems.at[a, j], device_id=(ox, oy, c), device_id_type=MESH_T)
                cp.start()
                copies.append(cp)
        for cp in copies:
            cp.wait()

    hbm = pl.BlockSpec(memory_space=pl.ANY)
    return pl.pallas_call(
        body, in_specs=[hbm] * n, out_specs=[hbm] * n,
        out_shape=[jax.ShapeDtypeStruct((N_CHIP - 1, s[0], s[1] // 2, s[2]), MXU) for s in shapes],
        scratch_shapes=[pltpu.SemaphoreType.DMA((n, N_CHIP - 1)), pltpu.SemaphoreType.DMA((n, N_CHIP - 1))],
        name="rs_chip_send", compiler_params=pltpu.CompilerParams(has_side_effects=True))(*pws)


def _rs_chip_add(p, recv, shape, pos, *, name):
    l, r, c = shape
    rh = r // 2

    def body(pos_ref, p_ref, r_ref, o_ref):
        del pos_ref
        acc = p_ref[...]
        for j in range(N_CHIP - 1):
            acc = acc + r_ref[j].astype(F32)
        o_ref[...] = acc

    return pl.pallas_call(
        body,
        grid_spec=pltpu.PrefetchScalarGridSpec(
            num_scalar_prefetch=1, grid=(l,),
            in_specs=[pl.BlockSpec((None, None, rh, c), lambda i, pos: (pos[1], i, 0, 0)),
                      pl.BlockSpec((N_CHIP - 1, None, rh, c), lambda i, pos: (0, i, 0, 0))],
            out_specs=pl.BlockSpec((None, rh, c), lambda i, pos: (i, pos[0], 0))),
        out_shape=jax.ShapeDtypeStruct((l, r, c), F32),
        name=name, compiler_params=_cp((PAR,)))(pos, p, recv)


def _rs_pair_share(gshards, shapes):
    n = len(gshards)

    def body(*refs):
        ins, outs = refs[:n], refs[n:2 * n]
        send_sems, recv_sems = refs[2 * n:]
        del ins
        x, y, c = _mesh_pos()
        copies = []
        for a in range(n):
            v = _half_view(outs[a], shapes[a], c)
            cp = pltpu.make_async_remote_copy(src_ref=v, dst_ref=v, send_sem=send_sems.at[a], recv_sem=recv_sems.at[a],
                                              device_id=(x, y, 1 - c), device_id_type=MESH_T)
            cp.start()
            copies.append(cp)
        for cp in copies:
            cp.wait()

    hbm = pl.BlockSpec(memory_space=pl.ANY)
    return pl.pallas_call(
        body, in_specs=[hbm] * n, out_specs=[hbm] * n,
        out_shape=[jax.ShapeDtypeStruct(g.shape, g.dtype) for g in gshards],
        input_output_aliases={a: a for a in range(n)},
        scratch_shapes=[pltpu.SemaphoreType.DMA((n,)), pltpu.SemaphoreType.DMA((n,))],
        name="rs_pair_share", compiler_params=pltpu.CompilerParams(has_side_effects=True))(*gshards)


def _adamw(w, g, m, v, *, name):
    shape = w.shape
    if w.ndim == 2:
        w, g, m, v = (a[None] for a in (w, g, m, v))
    l, r, c = w.shape
    tr = _tile(r, 256, SUBLANE)
    c1 = 1.0 / (1.0 - ADAM_B1 ** ADAM_STEP)
    c2 = 1.0 / (1.0 - ADAM_B2 ** ADAM_STEP)

    def body(w_ref, g_ref, m_ref, v_ref, d_ref, nm_ref, nv_ref):
        gg = g_ref[...]
        nm = ADAM_B1 * m_ref[...] + (1.0 - ADAM_B1) * gg
        nv = ADAM_B2 * v_ref[...] + (1.0 - ADAM_B2) * (gg * gg)
        nm_ref[...] = nm
        nv_ref[...] = nv
        d_ref[...] = -ADAM_LR * ((nm * c1) / (jnp.sqrt(nv * c2) + ADAM_EPS) + ADAM_WD * w_ref[...])

    spec = pl.BlockSpec((None, tr, c), lambda i, j: (i, j, 0))
    outs = pl.pallas_call(
        body, grid=(l, r // tr), in_specs=[spec] * 4, out_specs=[spec] * 3,
        out_shape=[jax.ShapeDtypeStruct((l, r, c), F32)] * 3,
        name=name, compiler_params=_cp((PAR, PAR)))(w, g, m, v)
    return tuple(o.reshape(shape) for o in outs)


PACK_ROWS = SUBLANE * LANE


def _pack(arrays):
    flat = jnp.concatenate([a.reshape(-1).astype(F32) for a in arrays])
    pad = (-flat.shape[0]) % PACK_ROWS
    return jnp.pad(flat, (0, pad)).reshape(-1, LANE)


def _unpack(packed, shapes):
    flat = packed.reshape(-1)
    out, off = [], 0
    for s in shapes:
        size = int(np.prod(s))
        out.append(flat[off:off + size].reshape(s))
        off += size
    return out


BIG = (("w_mem_kv", "row"), ("w_mix_out", "row"), ("w_ffn_up", "col"), ("w_ffn_down", "row"),
       ("w_in_a", "slot"), ("w_in_b", "row"), ("w_kv", "row"))
SMALL_SHARDED = (("w_ffn_conv", 2), ("w_conv_a", 2), ("b_conv_a", 1), ("lru_lambda", 1))
SMALL_REPLICATED = ("g_mix_pre", "g_mix_post", "g_ffn_pre", "g_ffn_post", "g_mem", "b_ffn_conv",
                    "w_rg_r", "b_rg_r", "w_rg_i", "b_rg_i", "sinks_b", "g_kv")
WEIGHTS = ("g_mix_pre", "g_mix_post", "g_ffn_pre", "g_ffn_post", "g_mem", "w_mem_kv", "w_mix_out", "w_ffn_up",
           "w_ffn_conv", "b_ffn_conv", "w_ffn_down", "w_in_a", "w_conv_a", "b_conv_a", "w_rg_r", "b_rg_r", "w_rg_i",
           "b_rg_i", "lru_lambda", "w_in_b", "sinks_b", "g_kv", "w_kv")


def _slot_to_cols(a):
    s, l, r, c = a.shape
    return a.transpose(1, 2, 0, 3).reshape(l, r, s * c)


def _cols_to_slot(a):
    l, r, c4 = a.shape
    return a.reshape(l, r, N_CHIP, c4 // N_CHIP).transpose(2, 0, 1, 3)


def _train_step(x, mem, target, w, m, v):
    xi, yi, ci = _mesh_pos()
    chip = 2 * xi + yi
    pos = jnp.stack([ci, chip]).astype(jnp.int32)

    as3 = lambda a: a if a.ndim == 3 else a[None]
    shards = [as3(w[k]).astype(MXU) for k, _ in BIG]
    kinds = [kind for _, kind in BIG]
    shapes = [a.shape for a in shards]
    full = dict(zip([k for k, _ in BIG], _gather_weights(shards, kinds)))
    full["w_in_a"] = _slot_to_cols(full["w_in_a"])
    small_shapes = [w[k].shape for k, _ in SMALL_SHARDED]
    stacked = _exchange8(_pack([w[k] for k, _ in SMALL_SHARDED]), False, name="gather_small")
    per_chip = [_unpack(stacked[2 * s], small_shapes) for s in range(N_CHIP)]
    p = dict(full)
    for i, (k, axis) in enumerate(SMALL_SHARDED):
        p[k] = jnp.concatenate([per_chip[s][i] for s in range(N_CHIP)], axis=axis)
    for k in SMALL_REPLICATED:
        p[k] = w[k]

    sq, grad_x, g = _local_step(x, mem, target, p)
    loss = lax.psum(0.5 * sq / D_MODEL, ("x", "y", "c"))

    small_names = [k for k, _ in SMALL_SHARDED] + list(SMALL_REPLICATED)
    summed = _exchange8(_pack([g[k] for k in small_names]), True, name="allreduce_small")
    gsum = dict(zip(small_names, _unpack(summed, [p[k].shape for k in small_names])))
    for k, axis in SMALL_SHARDED:
        gsum[k] = lax.dynamic_slice_in_dim(gsum[k], chip * w[k].shape[axis], w[k].shape[axis], axis)

    gbig = [g[k] for k, _ in BIG]
    gbig[4] = _cols_to_slot(gbig[4])
    recv = _rs_pair_send(gbig, kinds, shapes)
    pair = [_rs_pair_add(gbig[a], recv[a], kinds[a], shapes[a], pos, name=f"rs_pair_add_{BIG[a][0]}")
            for a in range(len(BIG))]
    recv2 = _rs_chip_send([pw for _, pw in pair], shapes)
    halves = [_rs_chip_add(pair[a][0], recv2[a], shapes[a], pos, name=f"rs_chip_add_{BIG[a][0]}")
              for a in range(len(BIG))]
    gshard = _rs_pair_share(halves, shapes)
    for (k, _), gs in zip(BIG, gshard):
        gsum[k] = gs.reshape(w[k].shape)

    delta, new_m, new_v = {}, {}, {}
    for k, _ in BIG:
        delta[k], new_m[k], new_v[k] = _adamw(w[k], gsum[k], m[k], v[k], name=f"adamw_{k}")
    packed = [_pack([d[k] for k in small_names]) for d in (w, gsum, m, v)]
    outs = _adamw(*packed, name="adamw_small")
    for d, o in zip((delta, new_m, new_v), outs):
        d.update(zip(small_names, _unpack(o, [w[k].shape for k in small_names])))
    return (loss, grad_x, *[gsum[k] for k in WEIGHTS], *[delta[k] for k in WEIGHTS],
            *[new_m[k] for k in WEIGHTS], *[new_v[k] for k in WEIGHTS])


def kernel(x, mem, g_mix_pre, g_mix_post, g_ffn_pre, g_ffn_post, g_mem, w_mem_kv, w_mix_out, w_ffn_up, w_ffn_conv, b_ffn_conv, w_ffn_down, w_in_a, w_conv_a, b_conv_a, w_rg_r, b_rg_r, w_rg_i, b_rg_i, lru_lambda, w_in_b, sinks_b, g_kv, w_kv, loss_target, m_g_mix_pre, m_g_mix_post, m_g_ffn_pre, m_g_ffn_post, m_g_mem, m_w_mem_kv, m_w_mix_out, m_w_ffn_up, m_w_ffn_conv, m_b_ffn_conv, m_w_ffn_down, m_w_in_a, m_w_conv_a, m_b_conv_a, m_w_rg_r, m_b_rg_r, m_w_rg_i, m_b_rg_i, m_lru_lambda, m_w_in_b, m_sinks_b, m_g_kv, m_w_kv, v_g_mix_pre, v_g_mix_post, v_g_ffn_pre, v_g_ffn_post, v_g_mem, v_w_mem_kv, v_w_mix_out, v_w_ffn_up, v_w_ffn_conv, v_b_ffn_conv, v_w_ffn_down, v_w_in_a, v_w_conv_a, v_b_conv_a, v_w_rg_r, v_b_rg_r, v_w_rg_i, v_b_rg_i, v_lru_lambda, v_w_in_b, v_sinks_b, v_g_kv, v_w_kv):
    args = (g_mix_pre, g_mix_post, g_ffn_pre, g_ffn_post, g_mem, w_mem_kv, w_mix_out, w_ffn_up, w_ffn_conv, b_ffn_conv, w_ffn_down, w_in_a, w_conv_a, b_conv_a, w_rg_r, b_rg_r, w_rg_i, b_rg_i, lru_lambda, w_in_b, sinks_b, g_kv, w_kv)
    ms = (m_g_mix_pre, m_g_mix_post, m_g_ffn_pre, m_g_ffn_post, m_g_mem, m_w_mem_kv, m_w_mix_out, m_w_ffn_up, m_w_ffn_conv, m_b_ffn_conv, m_w_ffn_down, m_w_in_a, m_w_conv_a, m_b_conv_a, m_w_rg_r, m_b_rg_r, m_w_rg_i, m_b_rg_i, m_lru_lambda, m_w_in_b, m_sinks_b, m_g_kv, m_w_kv)
    vs = (v_g_mix_pre, v_g_mix_post, v_g_ffn_pre, v_g_ffn_post, v_g_mem, v_w_mem_kv, v_w_mix_out, v_w_ffn_up, v_w_ffn_conv, v_b_ffn_conv, v_w_ffn_down, v_w_in_a, v_w_conv_a, v_b_conv_a, v_w_rg_r, v_b_rg_r, v_w_rg_i, v_b_rg_i, v_lru_lambda, v_w_in_b, v_sinks_b, v_g_kv, v_w_kv)
    return _train_step(x, mem, loss_target, dict(zip(WEIGHTS, args)), dict(zip(WEIGHTS, ms)), dict(zip(WEIGHTS, vs)))
```

```python
import functools
import math

import numpy as np
import jax
import jax.numpy as jnp
from jax import lax
from jax.experimental import pallas as pl
from jax.experimental.pallas import tpu as pltpu

F32 = jnp.float32
MXU = jnp.bfloat16

D_MODEL = 1024
HEAD_DIM = 64
MEM_LEN = 256
MEM_HEADS = 4
MEM_WIDTH = MEM_HEADS * HEAD_DIM
MIX_WIDTH = D_MODEL - MEM_WIDTH
LRU_BLOCKS = MIX_WIDTH // HEAD_DIM
LRU_CONV = 4
LRU_C = 8.0
SWA_HEADS = MIX_WIDTH // HEAD_DIM
SWA_KV_HEADS = 4
SWA_GROUP = SWA_HEADS // SWA_KV_HEADS
WINDOW = 128
D_FF = 2816
FFN_CONV = 3
EPS = 1e-6
DEPTH = 4
N_A = 2

ADAM_LR = 0.001
ADAM_B1 = 0.9
ADAM_B2 = 0.999
ADAM_EPS = 1e-08
ADAM_WD = 0.01
ADAM_STEP = 10

VMEM_LIMIT_V7X = 56 * 1024 * 1024
LANE = 128
SUBLANE = 8
GATE_TILE = 256
MESH_T = pl.DeviceIdType.MESH


def _alibi_slopes(n):
    def pow2_slopes(m):
        start = 2.0 ** (-8.0 / m)
        return [start ** (i + 1) for i in range(m)]
    c = 2 ** int(math.floor(math.log2(n)))
    s = pow2_slopes(c)
    if c != n:
        s = s + pow2_slopes(2 * c)[0::2][: n - c]
    return [float(np.float32(v)) for v in s]


SLOPES = _alibi_slopes(SWA_HEADS)


def _tile(n, cap, mult=LANE):
    best = None
    for t in range(mult, min(n, cap) + 1, mult):
        if n % t == 0:
            best = t
    return best if best is not None else n


def _cp(sem):
    return pltpu.CompilerParams(dimension_semantics=sem, vmem_limit_bytes=VMEM_LIMIT_V7X)


MM_VMEM_BUDGET = 40 * 1024 * 1024
HBM_BYTES_PER_US_V7X = 3.0e6
GRID_STEP_US = 0.35


def _divisors(n, mult):
    return [t for t in range(mult, n + 1, mult) if n % t == 0] or [n]


def _mm_tiles(m, k, n, out_bytes):
    best = None
    for tm in _divisors(m, 256):
        for tn in _divisors(n, LANE):
            vmem = 2 * (tm * k * 2 + k * tn * 2 + tm * tn * out_bytes)
            if vmem > MM_VMEM_BUDGET:
                continue
            steps = (m // tm) * (n // tn)
            b_reads = 1 if tn == n else m // tm
            traffic = m * k * 2 + k * n * 2 * b_reads + m * n * out_bytes
            first = tm * k * 2 + k * tn * 2
            cost = (traffic + first) / HBM_BYTES_PER_US_V7X + steps * GRID_STEP_US
            if best is None or cost < best[0]:
                best = (cost, tm, tn)
    return best[1], best[2]


def _mm_tn_tiles(k, m, n):
    best = None
    for tm in _divisors(m, LANE):
        for tn in _divisors(n, LANE):
            for tk in _divisors(k, 512):
                vmem = 2 * (tk * tm * 2 + tk * tn * 2 + tm * tn * 4)
                if vmem > MM_VMEM_BUDGET:
                    continue
                steps = (m // tm) * (n // tn) * (k // tk)
                traffic = k * m * 2 * (n // tn) + k * n * 2 * (m // tm) + m * n * 4
                cost = traffic / HBM_BYTES_PER_US_V7X + steps * GRID_STEP_US
                if best is None or cost < best[0]:
                    best = (cost, tk, tm, tn)
    return best[1], best[2], best[3]


ARB = "arbitrary"
PAR = "parallel"


def _rms_fwd(x, g):
    r = lax.rsqrt(jnp.mean(x * x, axis=-1, keepdims=True) + EPS)
    return x * r * g


def _rms_bwd(dy, x, g):
    r = lax.rsqrt(jnp.mean(x * x, axis=-1, keepdims=True) + EPS)
    xh = x * r
    gdy = dy * g
    dx = r * (gdy - xh * jnp.mean(gdy * xh, axis=-1, keepdims=True))
    dg = jnp.sum(dy * xh, axis=0, keepdims=True)
    return dx, dg


_GELU_K = math.sqrt(2.0 / math.pi)
_GELU_C = 0.044715


def _gelu(x):
    t = jnp.tanh(_GELU_K * (x + _GELU_C * x * x * x))
    return 0.5 * x * (1.0 + t)


def _gelu_and_grad(x):
    x2 = x * x
    t = jnp.tanh(_GELU_K * (x + _GELU_C * x2 * x))
    g = 0.5 * x * (1.0 + t)
    dg = 0.5 * (1.0 + t) + 0.5 * x * (1.0 - t * t) * (_GELU_K * (1.0 + 3.0 * _GELU_C * x2))
    return g, dg


def _shift_down(x, k, row):
    return jnp.where(row >= k, pltpu.roll(x, k, axis=0), 0.0)


def _shift_up(x, k, row):
    n = x.shape[0]
    return jnp.where(row < n - k, pltpu.roll(x, n - k, axis=0), 0.0)


def _dot(a, b):
    return jnp.dot(a, b, preferred_element_type=F32)


def _dot_nt(a, b):
    return lax.dot_general(a, b, (((1,), (1,)), ((), ())), preferred_element_type=F32)


def _dot_tn(a, b):
    return lax.dot_general(a, b, (((0,), (0,)), ((), ())), preferred_element_type=F32)


def _mm_nn(a, b, l, *, name, out_dtype=F32):
    m, k = a.shape
    n = b.shape[-1]
    tm, tn = _mm_tiles(m, k, n, jnp.dtype(out_dtype).itemsize)

    def body(a_ref, b_ref, o_ref):
        o_ref[...] = _dot(a_ref[...], b_ref[...]).astype(o_ref.dtype)

    return pl.pallas_call(
        body, grid=(m // tm, n // tn),
        in_specs=[pl.BlockSpec((tm, k), lambda i, j: (i, 0)),
                  pl.BlockSpec((None, k, tn), lambda i, j: (l, 0, j))],
        out_specs=pl.BlockSpec((tm, tn), lambda i, j: (i, j)),
        out_shape=jax.ShapeDtypeStruct((m, n), out_dtype),
        name=name, compiler_params=_cp((PAR, PAR)))(a, b)


def _mm_nt(a, b, l, *, name, out_dtype=F32):
    m, k = a.shape
    n = b.shape[-2]
    tm, tn = _mm_tiles(m, k, n, jnp.dtype(out_dtype).itemsize)

    def body(a_ref, b_ref, o_ref):
        o_ref[...] = _dot_nt(a_ref[...], b_ref[...]).astype(o_ref.dtype)

    return pl.pallas_call(
        body, grid=(m // tm, n // tn),
        in_specs=[pl.BlockSpec((tm, k), lambda i, j: (i, 0)),
                  pl.BlockSpec((None, tn, k), lambda i, j: (l, j, 0))],
        out_specs=pl.BlockSpec((tm, tn), lambda i, j: (i, j)),
        out_shape=jax.ShapeDtypeStruct((m, n), out_dtype),
        name=name, compiler_params=_cp((PAR, PAR)))(a, b)


def _mm_tn_into(a, b, out, l, *, name, col_block_offset=0):
    k, m = a.shape
    n = b.shape[-1]
    tk, tm, tn = _mm_tn_tiles(k, m, n)
    off = col_block_offset * (n // tn)

    def body(a_ref, b_ref, old_ref, o_ref):
        del old_ref
        part = _dot_tn(a_ref[...], b_ref[...])

        @pl.when(pl.program_id(2) == 0)
        def _():
            o_ref[...] = part

        @pl.when(pl.program_id(2) > 0)
        def _():
            o_ref[...] += part

    return pl.pallas_call(
        body, grid=(m // tm, n // tn, k // tk),
        in_specs=[pl.BlockSpec((tk, tm), lambda i, j, q: (q, i)),
                  pl.BlockSpec((tk, tn), lambda i, j, q: (q, j)),
                  pl.BlockSpec(memory_space=pl.ANY)],
        out_specs=pl.BlockSpec((None, tm, tn), lambda i, j, q: (l, i, j + off)),
        out_shape=jax.ShapeDtypeStruct(out.shape, out.dtype),
        input_output_aliases={2: 0},
        name=name, compiler_params=_cp((PAR, PAR, ARB)))(a, b, out)


def _mm_ffn_dh(dg, dv, w_up, l, *, name):
    m, f = dg.shape
    d = w_up.shape[-2]
    tm, tn = _mm_tiles(m, 2 * f, d, 4)

    def body(dg_ref, dv_ref, wg_ref, wv_ref, o_ref):
        o_ref[...] = _dot_nt(dg_ref[...], wg_ref[...]) + _dot_nt(dv_ref[...], wv_ref[...])

    return pl.pallas_call(
        body, grid=(m // tm, d // tn),
        in_specs=[pl.BlockSpec((tm, f), lambda i, j: (i, 0)),
                  pl.BlockSpec((tm, f), lambda i, j: (i, 0)),
                  pl.BlockSpec((None, tn, f), lambda i, j: (l, j, 0)),
                  pl.BlockSpec((None, tn, f), lambda i, j: (l, j, 1))],
        out_specs=pl.BlockSpec((tm, tn), lambda i, j: (i, j)),
        out_shape=jax.ShapeDtypeStruct((m, d), F32),
        name=name, compiler_params=_cp((PAR, PAR)))(dg, dv, w_up, w_up)


def _norm_fwd(x, g, *, name):
    n, d = x.shape
    tm = _tile(n, 256, SUBLANE)

    def body(x_ref, g_ref, o_ref):
        o_ref[...] = _rms_fwd(x_ref[...], g_ref[...]).astype(o_ref.dtype)

    return pl.pallas_call(
        body, grid=(n // tm,),
        in_specs=[pl.BlockSpec((tm, d), lambda i: (i, 0)), pl.BlockSpec((1, d), lambda i: (0, 0))],
        out_specs=pl.BlockSpec((tm, d), lambda i: (i, 0)),
        out_shape=jax.ShapeDtypeStruct((n, d), MXU),
        name=name, compiler_params=_cp((PAR,)))(x, g)


def _norm_bwd_dg(dy, x, g, *, name):
    n, d = x.shape
    tm = _tile(n, 256, SUBLANE)

    def body(dy_ref, x_ref, g_ref, dg_ref):
        @pl.when(pl.program_id(0) == 0)
        def _():
            dg_ref[...] = jnp.zeros_like(dg_ref)
        _, dg = _rms_bwd(dy_ref[...], x_ref[...], g_ref[...])
        dg_ref[...] += dg

    return pl.pallas_call(
        body, grid=(n // tm,),
        in_specs=[pl.BlockSpec((tm, d), lambda i: (i, 0)), pl.BlockSpec((tm, d), lambda i: (i, 0)),
                  pl.BlockSpec((1, d), lambda i: (0, 0))],
        out_specs=pl.BlockSpec((1, d), lambda i: (0, 0)),
        out_shape=jax.ShapeDtypeStruct((1, d), F32),
        name=name, compiler_params=_cp((ARB,)))(dy, x, g)


def _resid_norm_fwd(x, y, g_post, g_pres, *, name):
    n, d = x.shape
    tm = _tile(n, 256, SUBLANE)
    nh = len(g_pres)

    def body(x_ref, y_ref, gp_ref, *rest):
        gpre = rest[:nh]
        xo_ref = rest[nh]
        h_refs = rest[nh + 1:]
        xo = x_ref[...] + _rms_fwd(y_ref[...], gp_ref[...])
        xo_ref[...] = xo
        for g_ref, h_ref in zip(gpre, h_refs):
            h_ref[...] = _rms_fwd(xo, g_ref[...]).astype(h_ref.dtype)

    row = pl.BlockSpec((tm, d), lambda i: (i, 0))
    vec = pl.BlockSpec((1, d), lambda i: (0, 0))
    outs = pl.pallas_call(
        body, grid=(n // tm,),
        in_specs=[row, row, vec] + [vec] * nh,
        out_specs=[row] + [row] * nh,
        out_shape=[jax.ShapeDtypeStruct((n, d), F32)] + [jax.ShapeDtypeStruct((n, d), MXU)] * nh,
        name=name, compiler_params=_cp((PAR,)))(x, y, g_post, *g_pres)
    return outs[0], list(outs[1:])


def _loss_fwd(x, y, g_post, target, *, name):
    n, d = x.shape
    tm = _tile(n, 256, SUBLANE)

    def body(x_ref, y_ref, gp_ref, t_ref, dx_ref, sq_ref):
        @pl.when(pl.program_id(0) == 0)
        def _():
            sq_ref[...] = jnp.zeros_like(sq_ref)
        err = x_ref[...] + _rms_fwd(y_ref[...], gp_ref[...]) - t_ref[...]
        dx_ref[...] = err * (1.0 / d)
        sq_ref[...] += jnp.sum(err * err, axis=0, keepdims=True)

    row = pl.BlockSpec((tm, d), lambda i: (i, 0))
    vec = pl.BlockSpec((1, d), lambda i: (0, 0))
    return pl.pallas_call(
        body, grid=(n // tm,),
        in_specs=[row, row, vec, row],
        out_specs=[row, vec],
        out_shape=[jax.ShapeDtypeStruct((n, d), F32), jax.ShapeDtypeStruct((1, d), F32)],
        name=name, compiler_params=_cp((ARB,)))(x, y, g_post, target)


def _resid_norm_bwd(dx_out, dhs, x_out, g_pres, y, g_post, *, name):
    n, d = dx_out.shape
    tm = _tile(n, 256, SUBLANE)
    nh = len(dhs)
    has_y = y is not None

    def body(*refs):
        it = iter(refs)
        dxo_ref = next(it)
        dh_refs = [next(it) for _ in range(nh)]
        xo_ref = next(it) if nh else None
        gpre_refs = [next(it) for _ in range(nh)]
        y_ref = next(it) if has_y else None
        gpost_ref = next(it) if has_y else None
        g_out = next(it)
        dy_out = next(it) if has_y else None
        dgpre_out = [next(it) for _ in range(nh)]
        dgpost_out = next(it) if has_y else None

        @pl.when(pl.program_id(0) == 0)
        def _():
            for r in dgpre_out:
                r[...] = jnp.zeros_like(r)
            if has_y:
                dgpost_out[...] = jnp.zeros_like(dgpost_out)

        g = dxo_ref[...]
        if nh:
            xo = xo_ref[...]
            for dh_ref, gp_ref, dg_ref in zip(dh_refs, gpre_refs, dgpre_out):
                dx, dg = _rms_bwd(dh_ref[...], xo, gp_ref[...])
                g = g + dx
                dg_ref[...] += dg
        g_out[...] = g
        if has_y:
            dy, dg = _rms_bwd(g, y_ref[...], gpost_ref[...])
            dy_out[...] = dy.astype(dy_out.dtype)
            dgpost_out[...] += dg

    row = pl.BlockSpec((tm, d), lambda i: (i, 0))
    vec = pl.BlockSpec((1, d), lambda i: (0, 0))
    ins, in_specs = [dx_out], [row]
    ins += list(dhs)
    in_specs += [row] * nh
    if nh:
        ins.append(x_out)
        in_specs.append(row)
    ins += list(g_pres)
    in_specs += [vec] * nh
    if has_y:
        ins += [y, g_post]
        in_specs += [row, vec]
    out_specs, out_shape = [row], [jax.ShapeDtypeStruct((n, d), F32)]
    if has_y:
        out_specs.append(row)
        out_shape.append(jax.ShapeDtypeStruct((n, d), MXU))
    out_specs += [vec] * nh
    out_shape += [jax.ShapeDtypeStruct((1, d), F32)] * nh
    if has_y:
        out_specs.append(vec)
        out_shape.append(jax.ShapeDtypeStruct((1, d), F32))
    outs = list(pl.pallas_call(
        body, grid=(n // tm,), in_specs=in_specs, out_specs=out_specs, out_shape=out_shape,
        name=name, compiler_params=_cp((ARB,)))(*ins))
    g = outs.pop(0)
    dy = outs.pop(0) if has_y else None
    dgpre = [outs.pop(0) for _ in range(nh)]
    dgpost = outs.pop(0) if has_y else None
    return g, dy, dgpre, dgpost


def _ffn_conv(up, w_ref, b_ref, row):
    u1 = _shift_down(up, 1, row)
    u2 = _shift_down(up, 2, row)
    u = w_ref[0:1, :] * u2 + w_ref[1:2, :] * u1 + w_ref[2:3, :] * up + b_ref[...]
    return u, u1, u2


def _ffn_act_fwd(up, wconv, bconv, bsz, *, name):
    n, f2 = up.shape
    f = f2 // 2
    t = n // bsz
    tc = _tile(f, 256)
    nf = f // tc

    def body(ug_ref, uv_ref, wg_ref, wv_ref, bg_ref, bv_ref, o_ref):
        row = lax.broadcasted_iota(jnp.int32, (t, tc), 0)
        g, _, _ = _ffn_conv(ug_ref[...], wg_ref, bg_ref, row)
        v, _, _ = _ffn_conv(uv_ref[...], wv_ref, bv_ref, row)
        o_ref[...] = (_gelu(g) * v).astype(o_ref.dtype)

    return pl.pallas_call(
        body, grid=(bsz, nf),
        in_specs=[pl.BlockSpec((t, tc), lambda b, j: (b, j)),
                  pl.BlockSpec((t, tc), lambda b, j: (b, j + nf)),
                  pl.BlockSpec((FFN_CONV, tc), lambda b, j: (0, j)),
                  pl.BlockSpec((FFN_CONV, tc), lambda b, j: (0, j + nf)),
                  pl.BlockSpec((1, tc), lambda b, j: (0, j)),
                  pl.BlockSpec((1, tc), lambda b, j: (0, j + nf))],
        out_specs=pl.BlockSpec((t, tc), lambda b, j: (b, j)),
        out_shape=jax.ShapeDtypeStruct((n, f), MXU),
        name=name, compiler_params=_cp((PAR, PAR)))(up, up, wconv, wconv, bconv, bconv)


def _ffn_act_bwd(up, dact, wconv, bconv, bsz, *, name):
    n, f2 = up.shape
    f = f2 // 2
    t = n // bsz
    tc = _tile(f, 256)
    nf = f // tc

    def body(ug_ref, uv_ref, da_ref, wg_ref, wv_ref, bg_ref, bv_ref,
             dug_ref, duv_ref, dwg_ref, dwv_ref, dbg_ref, dbv_ref):
        @pl.when(pl.program_id(1) == 0)
        def _():
            for r in (dwg_ref, dwv_ref, dbg_ref, dbv_ref):
                r[...] = jnp.zeros_like(r)

        row = lax.broadcasted_iota(jnp.int32, (t, tc), 0)
        ug, uv = ug_ref[...], uv_ref[...]
        g, ug1, ug2 = _ffn_conv(ug, wg_ref, bg_ref, row)
        v, uv1, uv2 = _ffn_conv(uv, wv_ref, bv_ref, row)
        gl, dgl = _gelu_and_grad(g)
        da = da_ref[...]
        dg = da * v * dgl
        dv = da * gl

        def conv_bwd(du, w_ref, x0, x1, x2, dx_ref, dw_ref, db_ref):
            dx = w_ref[2:3, :] * du + w_ref[1:2, :] * _shift_up(du, 1, row) + w_ref[0:1, :] * _shift_up(du, 2, row)
            dx_ref[...] = dx.astype(dx_ref.dtype)
            dw_ref[0:1, :] += jnp.sum(du * x2, axis=0, keepdims=True)
            dw_ref[1:2, :] += jnp.sum(du * x1, axis=0, keepdims=True)
            dw_ref[2:3, :] += jnp.sum(du * x0, axis=0, keepdims=True)
            db_ref[...] += jnp.sum(du, axis=0, keepdims=True)

        conv_bwd(dg, wg_ref, ug, ug1, ug2, dug_ref, dwg_ref, dbg_ref)
        conv_bwd(dv, wv_ref, uv, uv1, uv2, duv_ref, dwv_ref, dbv_ref)

    blk = pl.BlockSpec((t, tc), lambda j, b: (b, j))
    wspec = pl.BlockSpec((FFN_CONV, tc), lambda j, b: (0, j))
    bspec = pl.BlockSpec((1, tc), lambda j, b: (0, j))
    outs = pl.pallas_call(
        body, grid=(nf, bsz),
        in_specs=[blk, pl.BlockSpec((t, tc), lambda j, b: (b, j + nf)), blk,
                  wspec, pl.BlockSpec((FFN_CONV, tc), lambda j, b: (0, j + nf)),
                  bspec, pl.BlockSpec((1, tc), lambda j, b: (0, j + nf))],
        out_specs=[blk, blk, wspec, wspec, bspec, bspec],
        out_shape=[jax.ShapeDtypeStruct((n, f), MXU), jax.ShapeDtypeStruct((n, f), MXU),
                   jax.ShapeDtypeStruct((FFN_CONV, f), F32), jax.ShapeDtypeStruct((FFN_CONV, f), F32),
                   jax.ShapeDtypeStruct((1, f), F32), jax.ShapeDtypeStruct((1, f), F32)],
        name=name, compiler_params=_cp((PAR, ARB)))(up, up, dact, wconv, wconv, bconv, bconv)
    dug, duv, dwg, dwv, dbg, dbv = outs
    return dug, duv, jnp.concatenate([dwg, dwv], axis=1), jnp.concatenate([dbg, dbv], axis=1)


def _mem_attn_fwd(proj, q_col_block, mkv, ycat, bsz, *, name):
    n = proj.shape[0]
    t = n // bsz
    tq = _tile(t, 512, SUBLANE)
    nt = t // tq
    scale = HEAD_DIM ** -0.5

    def body(q_ref, kv_ref, old_ref, o_ref):
        del old_ref
        outs = []
        for h in range(MEM_HEADS):
            sl = slice(h * HEAD_DIM, (h + 1) * HEAD_DIM)
            q = q_ref[:, sl].astype(MXU)
            k = kv_ref[:, sl].astype(MXU)
            v = kv_ref[:, MEM_WIDTH + h * HEAD_DIM: MEM_WIDTH + (h + 1) * HEAD_DIM].astype(MXU)
            s = _dot_nt(q, k) * scale
            m = jnp.max(s, axis=-1, keepdims=True)
            p = jnp.exp(s - m)
            p = p / jnp.sum(p, axis=-1, keepdims=True)
            outs.append(_dot(p.astype(MXU), v))
        o_ref[...] = jnp.concatenate(outs, axis=-1).astype(o_ref.dtype)

    return pl.pallas_call(
        body, grid=(bsz, nt),
        in_specs=[pl.BlockSpec((tq, MEM_WIDTH), lambda b, i: (b * nt + i, q_col_block)),
                  pl.BlockSpec((MEM_LEN, 2 * MEM_WIDTH), lambda b, i: (b, 0)),
                  pl.BlockSpec(memory_space=pl.ANY)],
        out_specs=pl.BlockSpec((tq, MEM_WIDTH), lambda b, i: (b * nt + i, MIX_WIDTH // MEM_WIDTH)),
        out_shape=jax.ShapeDtypeStruct(ycat.shape, ycat.dtype),
        input_output_aliases={2: 0},
        name=name, compiler_params=_cp((PAR, PAR)))(proj, mkv, ycat)


def _mem_attn_bwd(proj, q_col_block, mkv, dycat, dproj, bsz, *, name):
    n = proj.shape[0]
    t = n // bsz
    tq = _tile(t, 512, SUBLANE)
    nt = t // tq
    scale = HEAD_DIM ** -0.5

    def body(q_ref, kv_ref, do_ref, old_ref, dq_ref, dkv_ref):
        del old_ref

        @pl.when(pl.program_id(1) == 0)
        def _():
            dkv_ref[...] = jnp.zeros_like(dkv_ref)

        dqs, dks, dvs = [], [], []
        for h in range(MEM_HEADS):
            sl = slice(h * HEAD_DIM, (h + 1) * HEAD_DIM)
            q = q_ref[:, sl].astype(MXU)
            k = kv_ref[:, sl].astype(MXU)
            v = kv_ref[:, MEM_WIDTH + h * HEAD_DIM: MEM_WIDTH + (h + 1) * HEAD_DIM].astype(MXU)
            do = do_ref[:, sl].astype(MXU)
            s = _dot_nt(q, k) * scale
            m = jnp.max(s, axis=-1, keepdims=True)
            p = jnp.exp(s - m)
            p = p / jnp.sum(p, axis=-1, keepdims=True)
            dvs.append(_dot_tn(p.astype(MXU), do))
            dp = _dot_nt(do, v)
            ds = (p * (dp - jnp.sum(dp * p, axis=-1, keepdims=True)) * scale).astype(MXU)
            dqs.append(_dot(ds, k))
            dks.append(_dot_tn(ds, q))
        dq_ref[...] = jnp.concatenate(dqs, axis=-1).astype(dq_ref.dtype)
        dkv_ref[...] += jnp.concatenate(dks + dvs, axis=-1)

    return pl.pallas_call(
        body, grid=(bsz, nt),
        in_specs=[pl.BlockSpec((tq, MEM_WIDTH), lambda b, i: (b * nt + i, q_col_block)),
                  pl.BlockSpec((MEM_LEN, 2 * MEM_WIDTH), lambda b, i: (b, 0)),
                  pl.BlockSpec((tq, MEM_WIDTH), lambda b, i: (b * nt + i, MIX_WIDTH // MEM_WIDTH)),
                  pl.BlockSpec(memory_space=pl.ANY)],
        out_specs=[pl.BlockSpec((tq, MEM_WIDTH), lambda b, i: (b * nt + i, q_col_block)),
                   pl.BlockSpec((MEM_LEN, 2 * MEM_WIDTH), lambda b, i: (b, 0))],
        out_shape=[jax.ShapeDtypeStruct(dproj.shape, dproj.dtype),
                   jax.ShapeDtypeStruct((bsz * MEM_LEN, 2 * MEM_WIDTH), F32)],
        input_output_aliases={3: 0},
        name=name, compiler_params=_cp((PAR, ARB)))(proj, mkv, dycat, dproj)


def _swa_scores(q, k, h, dist, mask, sink):
    s = _dot_nt(q, k) * (HEAD_DIM ** -0.5)
    s = jnp.where(mask, s - SLOPES[h] * dist, -jnp.inf)
    m = jnp.maximum(jnp.max(s, axis=-1, keepdims=True), sink)
    p = jnp.exp(s - m)
    psink = jnp.exp(sink - m)
    inv = 1.0 / (jnp.sum(p, axis=-1, keepdims=True) + psink)
    return p * inv, psink * inv


def _swa_mask(n):
    qi = lax.broadcasted_iota(jnp.int32, (WINDOW, 2 * WINDOW), 0) + WINDOW
    ki = lax.broadcasted_iota(jnp.int32, (WINDOW, 2 * WINDOW), 1)
    dist = qi - ki
    mask = (dist >= 0) & (dist < WINDOW) & ((n > 0) | (ki >= WINDOW))
    return dist.astype(F32), mask


def _swa_fwd(proj, kv, sinks, bsz, *, name):
    n_tok = proj.shape[0]
    nb = n_tok // bsz // WINDOW
    kvw = SWA_KV_HEADS * HEAD_DIM

    def body(sink_ref, q_ref, kvp_ref, kvc_ref, o_ref):
        n = pl.program_id(1)
        dist, mask = _swa_mask(n)
        kk = jnp.concatenate([kvp_ref[:, :kvw], kvc_ref[:, :kvw]], axis=0).astype(MXU)
        vv = jnp.concatenate([kvp_ref[:, kvw:], kvc_ref[:, kvw:]], axis=0).astype(MXU)
        outs = []
        for h in range(SWA_HEADS):
            c = h // SWA_GROUP
            q = q_ref[:, h * HEAD_DIM:(h + 1) * HEAD_DIM].astype(MXU)
            p, _ = _swa_scores(q, kk[:, c * HEAD_DIM:(c + 1) * HEAD_DIM], h, dist, mask, sink_ref[h])
            outs.append(_dot(p.astype(MXU), vv[:, c * HEAD_DIM:(c + 1) * HEAD_DIM]))
        o_ref[...] = jnp.concatenate(outs, axis=-1).astype(o_ref.dtype)

    return pl.pallas_call(
        body, grid=(bsz, nb),
        in_specs=[pl.BlockSpec(memory_space=pltpu.SMEM),
                  pl.BlockSpec((WINDOW, MIX_WIDTH), lambda b, n: (b * nb + n, 0)),
                  pl.BlockSpec((WINDOW, 2 * kvw), lambda b, n: (b * nb + jnp.maximum(n - 1, 0), 0)),
                  pl.BlockSpec((WINDOW, 2 * kvw), lambda b, n: (b * nb + n, 0))],
        out_specs=pl.BlockSpec((WINDOW, MIX_WIDTH), lambda b, n: (b * nb + n, 0)),
        out_shape=jax.ShapeDtypeStruct((n_tok, D_MODEL), MXU),
        name=name, compiler_params=_cp((PAR, PAR)))(sinks, proj, kv, kv)


def _swa_bwd(proj, kv, sinks, dycat, bsz, *, name):
    n_tok = proj.shape[0]
    nb = n_tok // bsz // WINDOW
    kvw = SWA_KV_HEADS * HEAD_DIM

    def body(sink_ref, q_ref, kvp_ref, kvc_ref, do_ref, dq_ref, dkvc_ref, dkvp_ref, dsink_ref):
        n = pl.program_id(1)

        @pl.when((pl.program_id(0) == 0) & (n == 0))
        def _():
            dsink_ref[...] = jnp.zeros_like(dsink_ref)

        dist, mask = _swa_mask(n)
        kk = jnp.concatenate([kvp_ref[:, :kvw], kvc_ref[:, :kvw]], axis=0).astype(MXU)
        vv = jnp.concatenate([kvp_ref[:, kvw:], kvc_ref[:, kvw:]], axis=0).astype(MXU)
        lane = lax.broadcasted_iota(jnp.int32, (SUBLANE, LANE), 1)
        dqs = []
        dks = [None] * SWA_KV_HEADS
        dvs = [None] * SWA_KV_HEADS
        dsink = jnp.zeros((SUBLANE, LANE), F32)
        for h in range(SWA_HEADS):
            c = h // SWA_GROUP
            k = kk[:, c * HEAD_DIM:(c + 1) * HEAD_DIM]
            v = vv[:, c * HEAD_DIM:(c + 1) * HEAD_DIM]
            q = q_ref[:, h * HEAD_DIM:(h + 1) * HEAD_DIM].astype(MXU)
            do = do_ref[:, h * HEAD_DIM:(h + 1) * HEAD_DIM].astype(MXU)
            p, psink = _swa_scores(q, k, h, dist, mask, sink_ref[h])
            dv = _dot_tn(p.astype(MXU), do)
            dp = _dot_nt(do, v)
            rs = jnp.sum(dp * p, axis=-1, keepdims=True)
            ds = (p * (dp - rs) * (HEAD_DIM ** -0.5)).astype(MXU)
            dsink = dsink + jnp.where(lane == h, jnp.sum(-psink * rs, axis=0, keepdims=True), 0.0)
            dqs.append(_dot(ds, k))
            dk = _dot_tn(ds, q)
            dks[c] = dk if dks[c] is None else dks[c] + dk
            dvs[c] = dv if dvs[c] is None else dvs[c] + dv
        dq_ref[...] = jnp.concatenate(dqs, axis=-1).astype(dq_ref.dtype)
        dkv = jnp.concatenate(dks + dvs, axis=-1)
        dkvp_ref[...] = dkv[:WINDOW]
        dkvc_ref[...] = dkv[WINDOW:]
        dsink_ref[...] += dsink

    qspec = pl.BlockSpec((WINDOW, MIX_WIDTH), lambda b, n: (b * nb + n, 0))
    kvspec = pl.BlockSpec((WINDOW, 2 * kvw), lambda b, n: (b * nb + n, 0))
    return pl.pallas_call(
        body, grid=(bsz, nb),
        in_specs=[pl.BlockSpec(memory_space=pltpu.SMEM), qspec,
                  pl.BlockSpec((WINDOW, 2 * kvw), lambda b, n: (b * nb + jnp.maximum(n - 1, 0), 0)),
                  kvspec, qspec],
        out_specs=[qspec, kvspec, kvspec, pl.BlockSpec((SUBLANE, LANE), lambda b, n: (0, 0))],
        out_shape=[jax.ShapeDtypeStruct((n_tok, D_MODEL), MXU),
                   jax.ShapeDtypeStruct((n_tok, 2 * kvw), F32),
                   jax.ShapeDtypeStruct((n_tok, 2 * kvw), F32),
                   jax.ShapeDtypeStruct((SUBLANE, LANE), F32)],
        name=name, compiler_params=_cp((ARB, ARB)))(sinks, proj, kv, kv, dycat)


def _swa_dkv_combine(curs, prevs, bsz, *, name):
    n_tok, w = curs[0].shape
    nb = n_tok // bsz // WINDOW
    k = len(curs)

    def body(*refs):
        o_ref = refs[-1]
        n = pl.program_id(1)
        acc = refs[0][...]
        for r in refs[1:k]:
            acc = acc + r[...]
        nxt = refs[k][...]
        for r in refs[k + 1:2 * k]:
            nxt = nxt + r[...]
        o_ref[...] = (acc + jnp.where(n < nb - 1, nxt, 0.0)).astype(o_ref.dtype)

    cur = pl.BlockSpec((WINDOW, w), lambda b, n: (b * nb + n, 0))
    prv = pl.BlockSpec((WINDOW, w), lambda b, n: (b * nb + jnp.minimum(n + 1, nb - 1), 0))
    return pl.pallas_call(
        body, grid=(bsz, nb), in_specs=[cur] * k + [prv] * k, out_specs=cur,
        out_shape=jax.ShapeDtypeStruct((n_tok, w), MXU),
        name=name, compiler_params=_cp((PAR, PAR)))(*curs, *prevs)


def _lru_gates(ux, halo, ext_ref, wc_ref, bc_ref, wr_ref, br_ref, wi_ref, bi_ref, lam_ref):
    tt = ux.shape[0]
    ext_ref[0:SUBLANE, :] = halo
    ext_ref[SUBLANE:, :] = ux
    xs = [ux] + [ext_ref[pl.ds(SUBLANE - k, tt), :] for k in range(1, LRU_CONV)]
    xc = bc_ref[...] + wc_ref[3:4, :] * xs[0] + wc_ref[2:3, :] * xs[1] + wc_ref[1:2, :] * xs[2] + wc_ref[0:1, :] * xs[3]
    pre_r, pre_i = [], []
    for blk in range(MIX_WIDTH // GATE_TILE):
        xb = xc[:, blk * GATE_TILE:(blk + 1) * GATE_TILE].astype(MXU)
        pre_r.append(_dot(xb, wr_ref[blk]))
        pre_i.append(_dot(xb, wi_ref[blk]))
    r = jax.nn.sigmoid(jnp.concatenate(pre_r, axis=-1) + br_ref[...])
    i = jax.nn.sigmoid(jnp.concatenate(pre_i, axis=-1) + bi_ref[...])
    nlam = -lam_ref[...]
    sp = jnp.maximum(nlam, 0.0) + jnp.log(1.0 + jnp.exp(-jnp.abs(nlam)))
    log_a = -LRU_C * r * sp
    a = jnp.exp(log_a)
    om = -jnp.tanh(log_a) * (a * a + 1.0)
    s = jnp.sqrt(om)
    return xs, xc, r, i, sp, a, s


def _lru_fwd(proj, wconv, bconv, wr, br, wi, bi, lam, bsz, *, name):
    n_tok = proj.shape[0]
    t = n_tok // bsz
    tt = _tile(t, 256, SUBLANE)
    nt = t // tt
    w = MIX_WIDTH
    ng = tt // SUBLANE

    def body(pg_ref, halo_ref, wc_ref, bc_ref, wr_ref, br_ref, wi_ref, bi_ref, lam_ref,
             y_ref, h_ref, ext_ref, a_ref, b_ref, carry_ref):
        ti = pl.program_id(1)

        @pl.when(ti == 0)
        def _():
            carry_ref[...] = jnp.zeros_like(carry_ref)

        gate = pg_ref[:, :w]
        ux = pg_ref[:, w:]
        halo = jnp.where(ti > 0, halo_ref[...], 0.0)
        _, xc, _, i, _, a, s = _lru_gates(ux, halo, ext_ref, wc_ref, bc_ref, wr_ref, br_ref, wi_ref, bi_ref, lam_ref)
        a_ref[...] = a
        b_ref[...] = s * (i * xc)
        row = lax.broadcasted_iota(jnp.int32, (SUBLANE, w), 0)

        def group(g, hprev):
            off = pl.multiple_of(g * SUBLANE, SUBLANE)
            ca = a_ref[pl.ds(off, SUBLANE), :]
            cb = b_ref[pl.ds(off, SUBLANE), :]
            for d in (1, 2, 4):
                a_sh = jnp.where(row >= d, pltpu.roll(ca, d, axis=0), 1.0)
                b_sh = jnp.where(row >= d, pltpu.roll(cb, d, axis=0), 0.0)
                cb = ca * b_sh + cb
                ca = ca * a_sh
            h = ca * hprev + cb
            b_ref[pl.ds(off, SUBLANE), :] = h
            return jnp.broadcast_to(h[SUBLANE - 1:SUBLANE, :], (SUBLANE, w))

        carry_ref[...] = lax.fori_loop(0, ng, group, carry_ref[...])
        h = b_ref[...]
        h_ref[...] = h
        y_ref[...] = (h * _gelu(gate)).astype(y_ref.dtype)

    vec = lambda r: pl.BlockSpec((r, w), lambda b, i: (0, 0))
    wspec = pl.BlockSpec((w // GATE_TILE, GATE_TILE, GATE_TILE), lambda b, i: (0, 0, 0))
    hb = tt // SUBLANE
    return pl.pallas_call(
        body, grid=(bsz, nt),
        in_specs=[pl.BlockSpec((tt, 2 * w), lambda b, i: (b * nt + i, 0)),
                  pl.BlockSpec((SUBLANE, w), lambda b, i: (jnp.maximum((b * nt + i) * hb - 1, 0), 1)),
                  vec(LRU_CONV), vec(1), wspec, vec(1), wspec, vec(1), vec(1)],
        out_specs=[pl.BlockSpec((tt, w), lambda b, i: (b * nt + i, 0)),
                   pl.BlockSpec((tt, w), lambda b, i: (b * nt + i, 0))],
        out_shape=[jax.ShapeDtypeStruct((n_tok, D_MODEL), MXU), jax.ShapeDtypeStruct((n_tok, w), F32)],
        scratch_shapes=[pltpu.VMEM((tt + SUBLANE, w), F32), pltpu.VMEM((tt, w), F32),
                        pltpu.VMEM((tt, w), F32), pltpu.VMEM((SUBLANE, w), F32)],
        name=name, compiler_params=_cp((PAR, ARB)))(proj, proj, wconv, bconv, wr, br, wi, bi, lam)


def _lru_bwd(proj, hs, dycat, wconv, bconv, wr, br, wi, bi, lam, bsz, *, name):
    n_tok = proj.shape[0]
    t = n_tok // bsz
    tt = _tile(t, 256, SUBLANE)
    nt = t // tt
    w = MIX_WIDTH
    ng = tt // SUBLANE
    nblk = w // GATE_TILE

    def body(pg_ref, halo_ref, h_ref, hhalo_ref, dy_ref, wc_ref, bc_ref, wr_ref, br_ref, wi_ref, bi_ref, lam_ref,
             dp_ref, dwc_ref, dbc_ref, dwr_ref, dbr_ref, dwi_ref, dbi_ref, dlam_ref,
             ext_ref, a_ref, c_ref, g_ref, gcarry_ref, xcarry_ref):
        bi_ = pl.program_id(0)
        ti = nt - 1 - pl.program_id(1)

        @pl.when((bi_ == 0) & (pl.program_id(1) == 0))
        def _():
            for r in (dwc_ref, dbc_ref, dwr_ref, dbr_ref, dwi_ref, dbi_ref, dlam_ref):
                r[...] = jnp.zeros_like(r)

        @pl.when(pl.program_id(1) == 0)
        def _():
            gcarry_ref[...] = jnp.zeros_like(gcarry_ref)
            xcarry_ref[...] = jnp.zeros_like(xcarry_ref)

        gate = pg_ref[:, :w]
        ux = pg_ref[:, w:]
        halo = jnp.where(ti > 0, halo_ref[...], 0.0)
        xs, xc, r, i, sp, a, s = _lru_gates(ux, halo, ext_ref, wc_ref, bc_ref, wr_ref, br_ref, wi_ref, bi_ref, lam_ref)
        h = h_ref[...]
        gl, dgl = _gelu_and_grad(gate)
        dy = dy_ref[...]
        dgate = dy * h * dgl
        row_t = lax.broadcasted_iota(jnp.int32, (tt, w), 0)
        g_ref[...] = dy * gl + jnp.where(row_t == tt - 1, gcarry_ref[0:1, :], 0.0)
        c_ref[...] = _shift_up(a, 1, row_t)
        row = lax.broadcasted_iota(jnp.int32, (SUBLANE, w), 0)

        a_ref[...] = a

        def group(k, gnext):
            off = pl.multiple_of((ng - 1 - k) * SUBLANE, SUBLANE)
            cc = c_ref[pl.ds(off, SUBLANE), :]
            cb = g_ref[pl.ds(off, SUBLANE), :]
            cb = cb + jnp.where(row == SUBLANE - 1, gnext, 0.0)
            cc = jnp.where(row == SUBLANE - 1, 0.0, cc)
            for d in (1, 2, 4):
                c_sh = jnp.where(row < SUBLANE - d, pltpu.roll(cc, SUBLANE - d, axis=0), 1.0)
                b_sh = jnp.where(row < SUBLANE - d, pltpu.roll(cb, SUBLANE - d, axis=0), 0.0)
                cb = cc * b_sh + cb
                cc = cc * c_sh
            g_ref[pl.ds(off, SUBLANE), :] = cb
            a0 = a_ref[pl.ds(off, SUBLANE), :]
            return jnp.broadcast_to(a0[0:1, :] * cb[0:1, :], (SUBLANE, w))

        gc = lax.fori_loop(0, ng, group, jnp.zeros((SUBLANE, w), F32))
        gcarry_ref[...] = gc
        gsc = g_ref[...]

        hhalo = jnp.where(ti > 0, hhalo_ref[SUBLANE - 1:SUBLANE, :], 0.0)
        hprev = jnp.where(row_t == 0, hhalo, pltpu.roll(h, 1, axis=0))
        gated = i * xc
        d_gated = gsc * s
        d_atot = gsc * hprev - (gsc * gated) * a / s
        d_loga = d_atot * a
        d_r = d_loga * (-LRU_C) * sp
        dlam_ref[...] += jnp.sum(d_loga * r, axis=0, keepdims=True) * (LRU_C * jax.nn.sigmoid(-lam_ref[...]))
        d_i = d_gated * xc
        d_xc = d_gated * i
        d_pr = d_r * r * (1.0 - r)
        d_pi = d_i * i * (1.0 - i)
        dbr_ref[...] += jnp.sum(d_pr, axis=0, keepdims=True)
        dbi_ref[...] += jnp.sum(d_pi, axis=0, keepdims=True)
        extra = []
        for blk in range(nblk):
            sl = slice(blk * GATE_TILE, (blk + 1) * GATE_TILE)
            xb = xc[:, sl].astype(MXU)
            dr_b = d_pr[:, sl].astype(MXU)
            di_b = d_pi[:, sl].astype(MXU)
            dwr_ref[blk] += _dot_tn(xb, dr_b)
            dwi_ref[blk] += _dot_tn(xb, di_b)
            extra.append(_dot_nt(dr_b, wr_ref[blk]) + _dot_nt(di_b, wi_ref[blk]))
        d_xc = d_xc + jnp.concatenate(extra, axis=-1)
        dbc_ref[...] += jnp.sum(d_xc, axis=0, keepdims=True)
        for k in range(LRU_CONV):
            dwc_ref[k:k + 1, :] += jnp.sum(d_xc * xs[LRU_CONV - 1 - k], axis=0, keepdims=True)
        ext_ref[0:tt, :] = d_xc
        ext_ref[tt:, :] = xcarry_ref[...]
        dux = wc_ref[3:4, :] * d_xc
        for k in range(LRU_CONV - 1):
            dux = dux + wc_ref[k:k + 1, :] * ext_ref[pl.ds(LRU_CONV - 1 - k, tt), :]
        xcarry_ref[...] = d_xc[0:SUBLANE, :]
        dp_ref[:, :w] = dgate.astype(dp_ref.dtype)
        dp_ref[:, w:] = dux.astype(dp_ref.dtype)

    vec = lambda r: pl.BlockSpec((r, w), lambda b, i: (0, 0))
    wspec = pl.BlockSpec((nblk, GATE_TILE, GATE_TILE), lambda b, i: (0, 0, 0))
    hb = tt // SUBLANE
    rblk = lambda b, i: b * nt + (nt - 1 - i)
    halo_idx = lambda b, i: jnp.maximum(rblk(b, i) * hb - 1, 0)
    wide = pl.BlockSpec((tt, 2 * w), lambda b, i: (rblk(b, i), 0))
    narrow = pl.BlockSpec((tt, w), lambda b, i: (rblk(b, i), 0))
    return pl.pallas_call(
        body, grid=(bsz, nt),
        in_specs=[wide, pl.BlockSpec((SUBLANE, w), lambda b, i: (halo_idx(b, i), 1)),
                  narrow, pl.BlockSpec((SUBLANE, w), lambda b, i: (halo_idx(b, i), 0)), narrow,
                  vec(LRU_CONV), vec(1), wspec, vec(1), wspec, vec(1), vec(1)],
        out_specs=[wide, vec(LRU_CONV), vec(1), wspec, vec(1), wspec, vec(1), vec(1)],
        out_shape=[jax.ShapeDtypeStruct((n_tok, 2 * w + MEM_WIDTH), MXU),
                   jax.ShapeDtypeStruct((LRU_CONV, w), F32), jax.ShapeDtypeStruct((1, w), F32),
                   jax.ShapeDtypeStruct((nblk, GATE_TILE, GATE_TILE), F32), jax.ShapeDtypeStruct((1, w), F32),
                   jax.ShapeDtypeStruct((nblk, GATE_TILE, GATE_TILE), F32), jax.ShapeDtypeStruct((1, w), F32),
                   jax.ShapeDtypeStruct((1, w), F32)],
        scratch_shapes=[pltpu.VMEM((tt + SUBLANE, w), F32), pltpu.VMEM((tt, w), F32), pltpu.VMEM((tt, w), F32),
                        pltpu.VMEM((tt, w), F32), pltpu.VMEM((SUBLANE, w), F32), pltpu.VMEM((SUBLANE, w), F32)],
        name=name, compiler_params=_cp((ARB, ARB)))(proj, proj, hs, hs, dycat, wconv, bconv, wr, br, wi, bi, lam)


def _gate_tiles(w):
    per = GATE_TILE // HEAD_DIM
    w4 = w.reshape(LRU_BLOCKS // per, per, HEAD_DIM, HEAD_DIM)
    eye = jnp.eye(per, dtype=w.dtype)
    return jnp.einsum("bnij,nm->bnimj", w4, eye).reshape(LRU_BLOCKS // per, GATE_TILE, GATE_TILE)


def _gate_blocks(t):
    per = GATE_TILE // HEAD_DIM
    t5 = t.reshape(LRU_BLOCKS // per, per, HEAD_DIM, per, HEAD_DIM)
    eye = jnp.eye(per, dtype=t.dtype)
    return jnp.einsum("bnimj,nm->bnij", t5, eye).reshape(LRU_BLOCKS, HEAD_DIM, HEAD_DIM)


def _row(v):
    return v.reshape(1, -1)


def _local_step(x, mem, target, p):
    bsz, t, d = x.shape
    n = bsz * t
    x2d = x.reshape(n, d)
    tgt = target.reshape(n, d)
    mem2d = mem.reshape(bsz * MEM_LEN, d)
    dw = {k: lax.empty(p[k].shape, F32) for k in ("w_mem_kv", "w_mix_out", "w_ffn_up", "w_ffn_down", "w_in_a", "w_in_b", "w_kv")}
    wr_t = [_gate_tiles(p["w_rg_r"][j]).astype(MXU) for j in range(N_A)]
    wi_t = [_gate_tiles(p["w_rg_i"][j]).astype(MXU) for j in range(N_A)]

    mn = [_norm_fwd(mem2d, _row(p["g_mem"][l]), name=f"mem_norm{l}") for l in range(DEPTH)]
    mkv = [_mm_nn(mn[l], p["w_mem_kv"], l, name=f"mem_kv{l}") for l in range(DEPTH)]
    h = _norm_fwd(x2d, _row(p["g_mix_pre"][0]), name="in_norm")
    xin = x2d
    sv = []
    kv = hkv = None
    for l in range(DEPTH):
        s = {"xin": xin, "h": h}
        if l < N_A:
            proj = _mm_nn(h, p["w_in_a"], l, name=f"in_proj{l}")
            ycat, hs = _lru_fwd(proj, p["w_conv_a"][l], _row(p["b_conv_a"][l]), wr_t[l], _row(p["b_rg_r"][l]),
                                wi_t[l], _row(p["b_rg_i"][l]), _row(p["lru_lambda"][l]), bsz, name=f"lru_fwd{l}")
            s["hs"] = hs
            qblk = 2 * MIX_WIDTH // MEM_WIDTH
        else:
            if l == N_A:
                kv = _mm_nn(hkv, p["w_kv"], 0, name="kv_proj")
            proj = _mm_nn(h, p["w_in_b"], l - N_A, name=f"in_proj{l}")
            ycat = _swa_fwd(proj, kv, p["sinks_b"][l - N_A], bsz, name=f"swa_fwd{l}")
            qblk = MIX_WIDTH // MEM_WIDTH
        ycat = _mem_attn_fwd(proj, qblk, mkv[l], ycat, bsz, name=f"mem_attn_fwd{l}")
        y = _mm_nn(ycat, p["w_mix_out"], l, name=f"mix_out{l}")
        x1, (h2,) = _resid_norm_fwd(xin, y, _row(p["g_mix_post"][l]), [_row(p["g_ffn_pre"][l])], name=f"mix_resid{l}")
        up = _mm_nn(h2, p["w_ffn_up"], l, name=f"ffn_up{l}")
        act = _ffn_act_fwd(up, p["w_ffn_conv"][l], _row(p["b_ffn_conv"][l]), bsz, name=f"ffn_act{l}")
        f = _mm_nn(act, p["w_ffn_down"], l, name=f"ffn_down{l}")
        s.update(proj=proj, qblk=qblk, ycat=ycat, y=y, x1=x1, h2=h2, up=up, act=act, f=f)
        sv.append(s)
        if l < DEPTH - 1:
            g_pres = [_row(p["g_mix_pre"][l + 1])] + ([_row(p["g_kv"])] if l + 1 == N_A else [])
            xin, hn = _resid_norm_fwd(x1, f, _row(p["g_ffn_post"][l]), g_pres, name=f"ffn_resid{l}")
            h = hn[0]
            if l + 1 == N_A:
                hkv = hn[1]
        else:
            g_tot, sq = _loss_fwd(x1, f, _row(p["g_ffn_post"][l]), tgt, name="loss")

    gs = {k: [None] * DEPTH for k in ("g_mix_pre", "g_mix_post", "g_ffn_pre", "g_ffn_post", "g_mem",
                                       "w_ffn_conv", "b_ffn_conv")}
    ga = {k: [None] * N_A for k in ("w_conv_a", "b_conv_a", "w_rg_r", "b_rg_r", "w_rg_i", "b_rg_i", "lru_lambda")}
    gsink = [None] * (DEPTH - N_A)
    dkv_cur, dkv_prev = [], []
    g_tot, df, _, gs["g_ffn_post"][DEPTH - 1] = _resid_norm_bwd(
        g_tot, [], None, [], sv[-1]["f"], _row(p["g_ffn_post"][DEPTH - 1]), name="loss_bwd")
    grad_x = None
    for l in reversed(range(DEPTH)):
        s = sv[l]
        dact = _mm_nt(df, p["w_ffn_down"], l, name=f"d_act{l}")
        dw["w_ffn_down"] = _mm_tn_into(s["act"], df, dw["w_ffn_down"], l, name=f"dw_down{l}")
        dug, duv, gs["w_ffn_conv"][l], gs["b_ffn_conv"][l] = _ffn_act_bwd(
            s["up"], dact, p["w_ffn_conv"][l], _row(p["b_ffn_conv"][l]), bsz, name=f"ffn_act_bwd{l}")
        dh2 = _mm_ffn_dh(dug, duv, p["w_ffn_up"], l, name=f"d_h2_{l}")
        dw["w_ffn_up"] = _mm_tn_into(s["h2"], dug, dw["w_ffn_up"], l, name=f"dw_up_g{l}")
        dw["w_ffn_up"] = _mm_tn_into(s["h2"], duv, dw["w_ffn_up"], l, name=f"dw_up_v{l}", col_block_offset=1)
        g1, dy, (gs["g_ffn_pre"][l],), gs["g_mix_post"][l] = _resid_norm_bwd(
            g_tot, [dh2], s["x1"], [_row(p["g_ffn_pre"][l])], s["y"], _row(p["g_mix_post"][l]), name=f"mix_resid_bwd{l}")
        dycat = _mm_nt(dy, p["w_mix_out"], l, name=f"d_ycat{l}")
        dw["w_mix_out"] = _mm_tn_into(s["ycat"], dy, dw["w_mix_out"], l, name=f"dw_mix_out{l}")
        if l < N_A:
            dproj, dwc, dbc, dwr, dbr, dwi, dbi, dlam = _lru_bwd(
                s["proj"], s["hs"], dycat, p["w_conv_a"][l], _row(p["b_conv_a"][l]), wr_t[l], _row(p["b_rg_r"][l]),
                wi_t[l], _row(p["b_rg_i"][l]), _row(p["lru_lambda"][l]), bsz, name=f"lru_bwd{l}")
            ga["w_conv_a"][l], ga["b_conv_a"][l], ga["lru_lambda"][l] = dwc, dbc[0], dlam[0]
            ga["w_rg_r"][l], ga["w_rg_i"][l] = _gate_blocks(dwr), _gate_blocks(dwi)
            ga["b_rg_r"][l] = dbr.reshape(LRU_BLOCKS, HEAD_DIM)
            ga["b_rg_i"][l] = dbi.reshape(LRU_BLOCKS, HEAD_DIM)
            w_in, j = "w_in_a", l
        else:
            dproj, dc, dp_, dsk = _swa_bwd(s["proj"], kv, p["sinks_b"][l - N_A], dycat, bsz, name=f"swa_bwd{l}")
            dkv_cur.append(dc)
            dkv_prev.append(dp_)
            gsink[l - N_A] = dsk[0, :SWA_HEADS]
            w_in, j = "w_in_b", l - N_A
        dproj, dmkv = _mem_attn_bwd(s["proj"], s["qblk"], mkv[l], dycat, dproj, bsz, name=f"mem_attn_bwd{l}")
        dh = _mm_nt(dproj, p[w_in], j, name=f"d_h{l}")
        dw[w_in] = _mm_tn_into(s["h"], dproj, dw[w_in], j, name=f"dw_in{l}")
        dmkv = dmkv.astype(MXU)
        dmn = _mm_nt(dmkv, p["w_mem_kv"], l, name=f"d_mem_norm{l}")
        dw["w_mem_kv"] = _mm_tn_into(mn[l], dmkv, dw["w_mem_kv"], l, name=f"dw_mem_kv{l}")
        gs["g_mem"][l] = _norm_bwd_dg(dmn, mem2d, _row(p["g_mem"][l]), name=f"mem_norm_bwd{l}")
        dhs, g_pres = [dh], [_row(p["g_mix_pre"][l])]
        if l == N_A:
            dkv = _swa_dkv_combine(dkv_cur, dkv_prev, bsz, name="dkv_combine")
            dhs.append(_mm_nt(dkv, p["w_kv"], 0, name="d_hkv"))
            g_pres.append(_row(p["g_kv"]))
            dw["w_kv"] = _mm_tn_into(hkv, dkv, dw["w_kv"], 0, name="dw_kv")
        if l > 0:
            g_tot, df, dgpre, gs["g_ffn_post"][l - 1] = _resid_norm_bwd(
                g1, dhs, s["xin"], g_pres, sv[l - 1]["f"], _row(p["g_ffn_post"][l - 1]), name=f"ffn_resid_bwd{l - 1}")
        else:
            grad_x, _, dgpre, _ = _resid_norm_bwd(g1, dhs, s["xin"], g_pres, None, None, name="in_norm_bwd")
        gs["g_mix_pre"][l] = dgpre[0]
        if l == N_A:
            g_kv = dgpre[1][0]

    grads = dict(dw)
    for k in ("g_mix_pre", "g_mix_post", "g_ffn_pre", "g_ffn_post", "g_mem", "b_ffn_conv"):
        grads[k] = jnp.concatenate(gs[k], axis=0)
    grads["w_ffn_conv"] = jnp.stack(gs["w_ffn_conv"])
    for k, v in ga.items():
        grads[k] = jnp.stack(v)
    grads["sinks_b"] = jnp.stack(gsink)
    grads["g_kv"] = g_kv
    return jnp.sum(sq), grad_x.reshape(bsz, t, d), grads


N_CHIP = 4
HALF_ALIGN = 16


def _full_shape(kind, shard_shape):
    l, r, c = shard_shape
    return {"row": (l, N_CHIP * r, c), "col": (l, r, N_CHIP * c), "slot": (N_CHIP, l, r, c)}[kind]


def _slot_view(ref, kind, shard_shape, s, hf):
    _, r, c = shard_shape
    rh = r // 2
    if hf is None:
        start, size = 0, r
    else:
        start, size = hf * rh, rh
    if kind == "row":
        start = s * r + start
    if not isinstance(start, int):
        start = pl.multiple_of(start, HALF_ALIGN)
    rows = pl.ds(start, size)
    if kind == "row":
        return ref.at[:, rows, :]
    if kind == "col":
        return ref.at[:, rows, pl.ds(s * c, c)]
    return ref.at[s, :, rows, :]


def _half_view(ref, shard_shape, hf):
    rh = shard_shape[1] // 2
    return ref.at[:, pl.ds(pl.multiple_of(hf * rh, HALF_ALIGN), rh), :]


def _with_slot(kind, s, fn):
    if kind != "col" or isinstance(s, int):
        fn(s)
        return
    for k in range(N_CHIP):
        @pl.when(s == k)
        def _(k=k):
            fn(k)


def _mesh_pos():
    return lax.axis_index("x"), lax.axis_index("y"), lax.axis_index("c")


def _other_chips(x, y):
    return [(1 - x, y), (x, 1 - y), (1 - x, 1 - y)]


def _gather_weights(shards, kinds):
    n = len(shards)
    shapes = [a.shape for a in shards]

    def body(*refs):
        ins, outs = refs[:n], refs[n:2 * n]
        send_sems, recv_sems, loc_sems = refs[2 * n:]
        x, y, c = _mesh_pos()
        me = 2 * x + y
        others = _other_chips(x, y)
        sib = (x, y, 1 - c)

        def remote(a, k, src, dst, dev):
            return pltpu.make_async_remote_copy(src_ref=src, dst_ref=dst, send_sem=send_sems.at[a, k],
                                                recv_sem=recv_sems.at[a, k], device_id=dev, device_id_type=MESH_T)

        def any_half(a):
            return _slot_view(outs[a], kinds[a], shapes[a], 0, 0)

        for a in range(n):
            def own(s, a=a):
                pltpu.make_async_copy(ins[a], _slot_view(outs[a], kinds[a], shapes[a], s, None), loc_sems.at[a]).start()
            _with_slot(kinds[a], me, own)
            for j, (ox, oy) in enumerate(others):
                def send(s, a=a, j=j, ox=ox, oy=oy):
                    remote(a, j, _half_view(ins[a], shapes[a], c),
                           _slot_view(outs[a], kinds[a], shapes[a], s, c), (ox, oy, c)).start()
                _with_slot(kinds[a], me, send)
        for a in range(n):
            for j, (ox, oy) in enumerate(others):
                remote(a, j, any_half(a), any_half(a), sib).wait_recv()

                def forward(s, a=a, j=j):
                    v = _slot_view(outs[a], kinds[a], shapes[a], s, c)
                    remote(a, N_CHIP - 1 + j, v, v, sib).start()
                _with_slot(kinds[a], 2 * ox + oy, forward)
        for a in range(n):
            for j in range(N_CHIP - 1):
                remote(a, N_CHIP - 1 + j, any_half(a), any_half(a), sib).wait_recv()
            for k in range(2 * (N_CHIP - 1)):
                remote(a, k, any_half(a), any_half(a), sib).wait_send()
            pltpu.make_async_copy(ins[a], _slot_view(outs[a], kinds[a], shapes[a], 0, None), loc_sems.at[a]).wait()

    hbm = pl.BlockSpec(memory_space=pl.ANY)
    return pl.pallas_call(
        body, in_specs=[hbm] * n, out_specs=[hbm] * n,
        out_shape=[jax.ShapeDtypeStruct(_full_shape(k, a.shape), a.dtype) for k, a in zip(kinds, shards)],
        scratch_shapes=[pltpu.SemaphoreType.DMA((n, 2 * (N_CHIP - 1))), pltpu.SemaphoreType.DMA((n, 2 * (N_CHIP - 1))),
                        pltpu.SemaphoreType.DMA((n,))],
        name="gather_weights", compiler_params=pltpu.CompilerParams(has_side_effects=True))(*shards)


def _exchange8(vec, reduce, *, name):
    r = vec.shape[0]
    n_dev = 8

    def body(v_ref, o_ref, *rest):
        if reduce:
            buf, send_sems, recv_sems = rest
        else:
            buf = o_ref
            send_sems, recv_sems = rest
        x, y, c = _mesh_pos()
        me = 4 * x + 2 * y + c
        copies = []
        for k in range(1, n_dev):
            kx, ky, kc = (k >> 2) & 1, (k >> 1) & 1, k & 1
            peer = ((1 - x) if kx else x, (1 - y) if ky else y, (1 - c) if kc else c)
            cp = pltpu.make_async_remote_copy(src_ref=v_ref, dst_ref=buf.at[me], send_sem=send_sems.at[k - 1],
                                              recv_sem=recv_sems.at[k - 1], device_id=peer, device_id_type=MESH_T)
            cp.start()
            copies.append(cp)
        buf[me] = v_ref[...]
        for cp in copies:
            cp.wait()
        if reduce:
            acc = buf[0]
            for d in range(1, n_dev):
                acc = acc + buf[d]
            o_ref[...] = acc

    vm = pl.BlockSpec(memory_space=pltpu.VMEM)
    scratch = [pltpu.SemaphoreType.DMA((n_dev - 1,)), pltpu.SemaphoreType.DMA((n_dev - 1,))]
    if reduce:
        scratch = [pltpu.VMEM((n_dev, r, LANE), F32)] + scratch
        out_shape = jax.ShapeDtypeStruct((r, LANE), F32)
    else:
        out_shape = jax.ShapeDtypeStruct((n_dev, r, LANE), F32)
    return pl.pallas_call(
        body, in_specs=[vm], out_specs=vm, out_shape=out_shape, scratch_shapes=scratch,
        name=name, compiler_params=pltpu.CompilerParams(has_side_effects=True, vmem_limit_bytes=VMEM_LIMIT_V7X))(vec)


def _rs_pair_send(grads, kinds, shapes):
    n = len(grads)

    def body(*refs):
        ins, outs = refs[:n], refs[n:2 * n]
        send_sems, recv_sems = refs[2 * n:]
        x, y, c = _mesh_pos()
        sib = (x, y, 1 - c)
        copies = []
        for a in range(n):
            for s in range(N_CHIP):
                cp = pltpu.make_async_remote_copy(
                    src_ref=_slot_view(ins[a], kinds[a], shapes[a], s, 1 - c), dst_ref=outs[a].at[s],
                    send_sem=send_sems.at[a, s], recv_sem=recv_sems.at[a, s], device_id=sib, device_id_type=MESH_T)
                cp.start()
                copies.append(cp)
        for cp in copies:
            cp.wait()

    hbm = pl.BlockSpec(memory_space=pl.ANY)
    return pl.pallas_call(
        body, in_specs=[hbm] * n, out_specs=[hbm] * n,
        out_shape=[jax.ShapeDtypeStruct((N_CHIP, s[0], s[1] // 2, s[2]), F32) for s in shapes],
        scratch_shapes=[pltpu.SemaphoreType.DMA((n, N_CHIP)), pltpu.SemaphoreType.DMA((n, N_CHIP))],
        name="rs_pair_send", compiler_params=pltpu.CompilerParams(has_side_effects=True))(*grads)


def _rs_pair_add(g, recv, kind, shape, pos, *, name):
    l, r, c = shape
    rh = r // 2
    if kind == "row":
        gspec = pl.BlockSpec((None, rh, c), lambda s, i, pos: (i, 2 * s + pos[0], 0))
    elif kind == "col":
        gspec = pl.BlockSpec((None, rh, c), lambda s, i, pos: (i, pos[0], s))
    else:
        gspec = pl.BlockSpec((None, None, rh, c), lambda s, i, pos: (s, i, pos[0], 0))
    pspec = pl.BlockSpec((None, None, rh, c), lambda s, i, pos: (s, i, 0, 0))

    def body(pos_ref, g_ref, r_ref, p_ref, pw_ref):
        del pos_ref
        v = g_ref[...] + r_ref[...]
        p_ref[...] = v
        pw_ref[...] = v.astype(pw_ref.dtype)

    return pl.pallas_call(
        body,
        grid_spec=pltpu.PrefetchScalarGridSpec(
            num_scalar_prefetch=1, grid=(N_CHIP, l), in_specs=[gspec, pspec], out_specs=[pspec, pspec]),
        out_shape=[jax.ShapeDtypeStruct((N_CHIP, l, rh, c), F32), jax.ShapeDtypeStruct((N_CHIP, l, rh, c), MXU)],
        name=name, compiler_params=_cp((PAR, PAR)))(pos, g, recv)


def _rs_chip_send(pws, shapes):
    n = len(pws)

    def body(*refs):
        ins, outs = refs[:n], refs[n:2 * n]
        send_sems, recv_sems = refs[2 * n:]
        x, y, c = _mesh_pos()
        copies = []
        for a in range(n):
            for j, (ox, oy) in enumerate(_other_chips(x, y)):
                cp = pltpu.make_async_remote_copy(
                    src_ref=ins[a].at[2 * ox + oy], dst_ref=outs[a].at[j],
                    send_sem=send_sems.at[a, j], recv_sem=recv_sems.at[a, j], device_id=(ox, oy, c), device_id_type=MESH_T)
                cp.start()
                copies.append(cp)
        for cp in copies:
            cp.wait()

    hbm = pl.BlockSpec(memory_space=pl.ANY)
    return pl.pallas_call(
        body, in_specs=[hbm] * n, out_specs=[hbm] * n,
        out_shape=[jax.ShapeDtypeStruct((N_CHIP - 1, s[0], s[1] // 2, s[2]), MXU) for s in shapes],
        scratch_shapes=[pltpu.SemaphoreType.DMA((n, N_CHIP - 1)), pltpu.SemaphoreType.DMA((n, N_CHIP - 1))],
        name="rs_chip_send", compiler_params=pltpu.CompilerParams(has_side_effects=True))(*pws)


def _rs_chip_add(p, recv, shape, pos, *, name):
    l, r, c = shape
    rh = r // 2

    def body(pos_ref, p_ref, r_ref, o_ref):
        del pos_ref
        acc = p_ref[...]
        for j in range(N_CHIP - 1):
            acc = acc + r_ref[j].astype(F32)
        o_ref[...] = acc

    return pl.pallas_call(
        body,
        grid_spec=pltpu.PrefetchScalarGridSpec(
            num_scalar_prefetch=1, grid=(l,),
            in_specs=[pl.BlockSpec((None, None, rh, c), lambda i, pos: (pos[1], i, 0, 0)),
                      pl.BlockSpec((N_CHIP - 1, None, rh, c), lambda i, pos: (0, i, 0, 0))],
            out_specs=pl.BlockSpec((None, rh, c), lambda i, pos: (i, pos[0], 0))),
        out_shape=jax.ShapeDtypeStruct((l, r, c), F32),
        name=name, compiler_params=_cp((PAR,)))(pos, p, recv)


def _rs_pair_share(gshards, shapes):
    n = len(gshards)

    def body(*refs):
        ins, outs = refs[:n], refs[n:2 * n]
        send_sems, recv_sems = refs[2 * n:]
        del ins
        x, y, c = _mesh_pos()
        copies = []
        for a in range(n):
            v = _half_view(outs[a], shapes[a], c)
            cp = pltpu.make_async_remote_copy(src_ref=v, dst_ref=v, send_sem=send_sems.at[a], recv_sem=recv_sems.at[a],
                                              device_id=(x, y, 1 - c), device_id_type=MESH_T)
            cp.start()
            copies.append(cp)
        for cp in copies:
            cp.wait()

    hbm = pl.BlockSpec(memory_space=pl.ANY)
    return pl.pallas_call(
        body, in_specs=[hbm] * n, out_specs=[hbm] * n,
        out_shape=[jax.ShapeDtypeStruct(g.shape, g.dtype) for g in gshards],
        input_output_aliases={a: a for a in range(n)},
        scratch_shapes=[pltpu.SemaphoreType.DMA((n,)), pltpu.SemaphoreType.DMA((n,))],
        name="rs_pair_share", compiler_params=pltpu.CompilerParams(has_side_effects=True))(*gshards)


ADAM_BLOCK_ELEMS = 512 * 1024


def _adamw(w, g, m, v, *, name):
    shape = w.shape
    if w.ndim == 2:
        w, g, m, v = (a[None] for a in (w, g, m, v))
    l, r, c = w.shape
    tr = _tile(r, max(SUBLANE, ADAM_BLOCK_ELEMS // c // SUBLANE * SUBLANE), SUBLANE)
    c1 = 1.0 / (1.0 - ADAM_B1 ** ADAM_STEP)
    c2 = 1.0 / (1.0 - ADAM_B2 ** ADAM_STEP)

    def body(w_ref, g_ref, m_ref, v_ref, d_ref, nm_ref, nv_ref):
        gg = g_ref[...]
        nm = ADAM_B1 * m_ref[...] + (1.0 - ADAM_B1) * gg
        nv = ADAM_B2 * v_ref[...] + (1.0 - ADAM_B2) * (gg * gg)
        nm_ref[...] = nm
        nv_ref[...] = nv
        d_ref[...] = -ADAM_LR * ((nm * c1) / (jnp.sqrt(nv * c2) + ADAM_EPS) + ADAM_WD * w_ref[...])

    spec = pl.BlockSpec((None, tr, c), lambda i, j: (i, j, 0))
    outs = pl.pallas_call(
        body, grid=(l, r // tr), in_specs=[spec] * 4, out_specs=[spec] * 3,
        out_shape=[jax.ShapeDtypeStruct((l, r, c), F32)] * 3,
        name=name, compiler_params=_cp((PAR, PAR)))(w, g, m, v)
    return tuple(o.reshape(shape) for o in outs)


PACK_ROWS = 512 * LANE


def _pack(arrays):
    flat = jnp.concatenate([a.reshape(-1).astype(F32) for a in arrays])
    pad = (-flat.shape[0]) % PACK_ROWS
    return jnp.pad(flat, (0, pad)).reshape(-1, LANE)


def _unpack(packed, shapes):
    flat = packed.reshape(-1)
    out, off = [], 0
    for s in shapes:
        size = int(np.prod(s))
        out.append(flat[off:off + size].reshape(s))
        off += size
    return out


BIG = (("w_mem_kv", "row"), ("w_mix_out", "row"), ("w_ffn_up", "col"), ("w_ffn_down", "row"),
       ("w_in_a", "slot"), ("w_in_b", "row"), ("w_kv", "row"))
SMALL_SHARDED = (("w_ffn_conv", 2), ("w_conv_a", 2), ("b_conv_a", 1), ("lru_lambda", 1))
SMALL_REPLICATED = ("g_mix_pre", "g_mix_post", "g_ffn_pre", "g_ffn_post", "g_mem", "b_ffn_conv",
                    "w_rg_r", "b_rg_r", "w_rg_i", "b_rg_i", "sinks_b", "g_kv")
WEIGHTS = ("g_mix_pre", "g_mix_post", "g_ffn_pre", "g_ffn_post", "g_mem", "w_mem_kv", "w_mix_out", "w_ffn_up",
           "w_ffn_conv", "b_ffn_conv", "w_ffn_down", "w_in_a", "w_conv_a", "b_conv_a", "w_rg_r", "b_rg_r", "w_rg_i",
           "b_rg_i", "lru_lambda", "w_in_b", "sinks_b", "g_kv", "w_kv")


def _slot_to_cols(a):
    s, l, r, c = a.shape
    return a.transpose(1, 2, 0, 3).reshape(l, r, s * c)


def _cols_to_slot(a):
    l, r, c4 = a.shape
    return a.reshape(l, r, N_CHIP, c4 // N_CHIP).transpose(2, 0, 1, 3)


def _train_step(x, mem, target, w, m, v):
    xi, yi, ci = _mesh_pos()
    chip = 2 * xi + yi
    pos = jnp.stack([ci, chip]).astype(jnp.int32)

    as3 = lambda a: a if a.ndim == 3 else a[None]
    shards = [as3(w[k]).astype(MXU) for k, _ in BIG]
    kinds = [kind for _, kind in BIG]
    shapes = [a.shape for a in shards]
    full = dict(zip([k for k, _ in BIG], _gather_weights(shards, kinds)))
    full["w_in_a"] = _slot_to_cols(full["w_in_a"])
    small_shapes = [w[k].shape for k, _ in SMALL_SHARDED]
    stacked = _exchange8(_pack([w[k] for k, _ in SMALL_SHARDED]), False, name="gather_small")
    per_chip = [_unpack(stacked[2 * s], small_shapes) for s in range(N_CHIP)]
    p = dict(full)
    for i, (k, axis) in enumerate(SMALL_SHARDED):
        p[k] = jnp.concatenate([per_chip[s][i] for s in range(N_CHIP)], axis=axis)
    for k in SMALL_REPLICATED:
        p[k] = w[k]

    sq, grad_x, g = _local_step(x, mem, target, p)
    loss = lax.psum(0.5 * sq / D_MODEL, ("x", "y", "c"))

    small_names = [k for k, _ in SMALL_SHARDED] + list(SMALL_REPLICATED)
    summed = _exchange8(_pack([g[k] for k in small_names]), True, name="allreduce_small")
    gsum = dict(zip(small_names, _unpack(summed, [p[k].shape for k in small_names])))
    for k, axis in SMALL_SHARDED:
        gsum[k] = lax.dynamic_slice_in_dim(gsum[k], chip * w[k].shape[axis], w[k].shape[axis], axis)

    gbig = [g[k] for k, _ in BIG]
    gbig[4] = _cols_to_slot(gbig[4])
    recv = _rs_pair_send(gbig, kinds, shapes)
    pair = [_rs_pair_add(gbig[a], recv[a], kinds[a], shapes[a], pos, name=f"rs_pair_add_{BIG[a][0]}")
            for a in range(len(BIG))]
    recv2 = _rs_chip_send([pw for _, pw in pair], shapes)
    halves = [_rs_chip_add(pair[a][0], recv2[a], shapes[a], pos, name=f"rs_chip_add_{BIG[a][0]}")
              for a in range(len(BIG))]
    gshard = _rs_pair_share(halves, shapes)
    for (k, _), gs in zip(BIG, gshard):
        gsum[k] = gs.reshape(w[k].shape)

    delta, new_m, new_v = {}, {}, {}
    for k, _ in BIG:
        delta[k], new_m[k], new_v[k] = _adamw(w[k], gsum[k], m[k], v[k], name=f"adamw_{k}")
    packed = [_pack([d[k] for k in small_names]) for d in (w, gsum, m, v)]
    outs = _adamw(*packed, name="adamw_small")
    for d, o in zip((delta, new_m, new_v), outs):
        d.update(zip(small_names, _unpack(o, [w[k].shape for k in small_names])))
    return (loss, grad_x, *[gsum[k] for k in WEIGHTS], *[delta[k] for k in WEIGHTS],
            *[new_m[k] for k in WEIGHTS], *[new_v[k] for k in WEIGHTS])


def kernel(x, mem, g_mix_pre, g_mix_post, g_ffn_pre, g_ffn_post, g_mem, w_mem_kv, w_mix_out, w_ffn_up, w_ffn_conv, b_ffn_conv, w_ffn_down, w_in_a, w_conv_a, b_conv_a, w_rg_r, b_rg_r, w_rg_i, b_rg_i, lru_lambda, w_in_b, sinks_b, g_kv, w_kv, loss_target, m_g_mix_pre, m_g_mix_post, m_g_ffn_pre, m_g_ffn_post, m_g_mem, m_w_mem_kv, m_w_mix_out, m_w_ffn_up, m_w_ffn_conv, m_b_ffn_conv, m_w_ffn_down, m_w_in_a, m_w_conv_a, m_b_conv_a, m_w_rg_r, m_b_rg_r, m_w_rg_i, m_b_rg_i, m_lru_lambda, m_w_in_b, m_sinks_b, m_g_kv, m_w_kv, v_g_mix_pre, v_g_mix_post, v_g_ffn_pre, v_g_ffn_post, v_g_mem, v_w_mem_kv, v_w_mix_out, v_w_ffn_up, v_w_ffn_conv, v_b_ffn_conv, v_w_ffn_down, v_w_in_a, v_w_conv_a, v_b_conv_a, v_w_rg_r, v_b_rg_r, v_w_rg_i, v_b_rg_i, v_lru_lambda, v_w_in_b, v_sinks_b, v_g_kv, v_w_kv):
    args = (g_mix_pre, g_mix_post, g_ffn_pre, g_ffn_post, g_mem, w_mem_kv, w_mix_out, w_ffn_up, w_ffn_conv, b_ffn_conv, w_ffn_down, w_in_a, w_conv_a, b_conv_a, w_rg_r, b_rg_r, w_rg_i, b_rg_i, lru_lambda, w_in_b, sinks_b, g_kv, w_kv)
    ms = (m_g_mix_pre, m_g_mix_post, m_g_ffn_pre, m_g_ffn_post, m_g_mem, m_w_mem_kv, m_w_mix_out, m_w_ffn_up, m_w_ffn_conv, m_b_ffn_conv, m_w_ffn_down, m_w_in_a, m_w_conv_a, m_b_conv_a, m_w_rg_r, m_b_rg_r, m_w_rg_i, m_b_rg_i, m_lru_lambda, m_w_in_b, m_sinks_b, m_g_kv, m_w_kv)
    vs = (v_g_mix_pre, v_g_mix_post, v_g_ffn_pre, v_g_ffn_post, v_g_mem, v_w_mem_kv, v_w_mix_out, v_w_ffn_up, v_w_ffn_conv, v_b_ffn_conv, v_w_ffn_down, v_w_in_a, v_w_conv_a, v_b_conv_a, v_w_rg_r, v_b_rg_r, v_w_rg_i, v_b_rg_i, v_lru_lambda, v_w_in_b, v_sinks_b, v_g_kv, v_w_kv)
    return _train_step(x, mem, loss_target, dict(zip(WEIGHTS, args)), dict(zip(WEIGHTS, ms)), dict(zip(WEIGHTS, vs)))
```

```python
import functools
import math

import numpy as np
import jax
import jax.numpy as jnp
from jax import lax
from jax.experimental import pallas as pl
from jax.experimental.pallas import tpu as pltpu

F32 = jnp.float32
MXU = jnp.bfloat16

D_MODEL = 1024
HEAD_DIM = 64
MEM_LEN = 256
MEM_HEADS = 4
MEM_WIDTH = MEM_HEADS * HEAD_DIM
MIX_WIDTH = D_MODEL - MEM_WIDTH
LRU_BLOCKS = MIX_WIDTH // HEAD_DIM
LRU_CONV = 4
LRU_C = 8.0
SWA_HEADS = MIX_WIDTH // HEAD_DIM
SWA_KV_HEADS = 4
SWA_GROUP = SWA_HEADS // SWA_KV_HEADS
WINDOW = 128
D_FF = 2816
FFN_CONV = 3
EPS = 1e-6
DEPTH = 4
N_A = 2

ADAM_LR = 0.001
ADAM_B1 = 0.9
ADAM_B2 = 0.999
ADAM_EPS = 1e-08
ADAM_WD = 0.01
ADAM_STEP = 10

VMEM_LIMIT_V7X = 56 * 1024 * 1024
LANE = 128
SUBLANE = 8
GATE_TILE = 256
MESH_T = pl.DeviceIdType.MESH


def _alibi_slopes(n):
    def pow2_slopes(m):
        start = 2.0 ** (-8.0 / m)
        return [start ** (i + 1) for i in range(m)]
    c = 2 ** int(math.floor(math.log2(n)))
    s = pow2_slopes(c)
    if c != n:
        s = s + pow2_slopes(2 * c)[0::2][: n - c]
    return [float(np.float32(v)) for v in s]


SLOPES = _alibi_slopes(SWA_HEADS)


def _tile(n, cap, mult=LANE):
    best = None
    for t in range(mult, min(n, cap) + 1, mult):
        if n % t == 0:
            best = t
    return best if best is not None else n


def _cp(sem):
    return pltpu.CompilerParams(dimension_semantics=sem, vmem_limit_bytes=VMEM_LIMIT_V7X)


MM_VMEM_BUDGET = 40 * 1024 * 1024
HBM_BYTES_PER_US_V7X = 3.0e6
GRID_STEP_US = 0.35


def _divisors(n, mult):
    return [t for t in range(mult, n + 1, mult) if n % t == 0] or [n]


def _mm_tiles(m, k, n, out_bytes):
    best = None
    for tm in _divisors(m, 256):
        for tn in _divisors(n, LANE):
            vmem = 2 * (tm * k * 2 + k * tn * 2 + tm * tn * out_bytes)
            if vmem > MM_VMEM_BUDGET:
                continue
            steps = (m // tm) * (n // tn)
            b_reads = 1 if tn == n else m // tm
            traffic = m * k * 2 + k * n * 2 * b_reads + m * n * out_bytes
            first = tm * k * 2 + k * tn * 2
            cost = (traffic + first) / HBM_BYTES_PER_US_V7X + steps * GRID_STEP_US
            if best is None or cost < best[0]:
                best = (cost, tm, tn)
    return best[1], best[2]


def _mm_tn_tiles(k, m, n):
    best = None
    for tm in _divisors(m, LANE):
        for tn in _divisors(n, LANE):
            for tk in _divisors(k, 512):
                vmem = 2 * (tk * tm * 2 + tk * tn * 2 + tm * tn * 4)
                if vmem > MM_VMEM_BUDGET:
                    continue
                steps = (m // tm) * (n // tn) * (k // tk)
                traffic = k * m * 2 * (n // tn) + k * n * 2 * (m // tm) + m * n * 4
                cost = traffic / HBM_BYTES_PER_US_V7X + steps * GRID_STEP_US
                if best is None or cost < best[0]:
                    best = (cost, tk, tm, tn)
    return best[1], best[2], best[3]


ARB = "arbitrary"
PAR = "parallel"


def _rms_fwd(x, g):
    r = lax.rsqrt(jnp.mean(x * x, axis=-1, keepdims=True) + EPS)
    return x * r * g


def _rms_bwd(dy, x, g):
    r = lax.rsqrt(jnp.mean(x * x, axis=-1, keepdims=True) + EPS)
    xh = x * r
    gdy = dy * g
    dx = r * (gdy - xh * jnp.mean(gdy * xh, axis=-1, keepdims=True))
    dg = jnp.sum(dy * xh, axis=0, keepdims=True)
    return dx, dg


_GELU_K = math.sqrt(2.0 / math.pi)
_GELU_C = 0.044715


def _gelu(x):
    t = jnp.tanh(_GELU_K * (x + _GELU_C * x * x * x))
    return 0.5 * x * (1.0 + t)


def _gelu_and_grad(x):
    x2 = x * x
    t = jnp.tanh(_GELU_K * (x + _GELU_C * x2 * x))
    g = 0.5 * x * (1.0 + t)
    dg = 0.5 * (1.0 + t) + 0.5 * x * (1.0 - t * t) * (_GELU_K * (1.0 + 3.0 * _GELU_C * x2))
    return g, dg


def _shift_down(x, k, row):
    return jnp.where(row >= k, pltpu.roll(x, k, axis=0), 0.0)


def _shift_up(x, k, row):
    n = x.shape[0]
    return jnp.where(row < n - k, pltpu.roll(x, n - k, axis=0), 0.0)


def _dot(a, b):
    return jnp.dot(a, b, preferred_element_type=F32)


def _dot_nt(a, b):
    return lax.dot_general(a, b, (((1,), (1,)), ((), ())), preferred_element_type=F32)


def _dot_tn(a, b):
    return lax.dot_general(a, b, (((0,), (0,)), ((), ())), preferred_element_type=F32)


MXU_FLOPS_PER_US = 7.0e8


def _hosted_call(body, *, grid, in_specs, out_specs, out_shape, args, name, aliases=None, q=None, flops=0.0):
    chunks = q.take(flops / MXU_FLOPS_PER_US) if q is not None else []
    if not chunks:
        return pl.pallas_call(
            body, grid=grid, in_specs=in_specs, out_specs=out_specs, out_shape=out_shape,
            input_output_aliases=aliases or {}, name=name,
            compiler_params=_cp((ARB,) * len(grid)))(*args)
    n_in = len(args)
    c_ins = [a for ch in chunks for a in ch.ins]
    c_outs = [s for ch in chunks for s in ch.out_shapes]
    alias = dict(aliases or {})
    in_off, out_off, sem_off = [], [], []
    i0 = o0 = s0 = 0
    for ch in chunks:
        in_off.append(i0)
        out_off.append(o0)
        sem_off.append(s0)
        for ci, co in ch.alias.items():
            alias[n_in + i0 + ci] = 1 + o0 + co
        i0 += len(ch.ins)
        o0 += len(ch.out_shapes)
        s0 += ch.n_sem

    def wrapped(*refs):
        ins = refs[:n_in]
        cin = refs[n_in:n_in + i0]
        o_ref = refs[n_in + i0]
        cout = refs[n_in + i0 + 1:n_in + i0 + 1 + o0]
        send_sems, recv_sems = refs[n_in + i0 + 1 + o0:]
        first = functools.reduce(lambda u, v: u & v, [pl.program_id(d) == 0 for d in range(len(grid))])
        last = functools.reduce(lambda u, v: u & v, [pl.program_id(d) == grid[d] - 1 for d in range(len(grid))])

        def each(phase):
            for ch, a, b, s in zip(chunks, in_off, out_off, sem_off):
                getattr(ch, phase)(cin[a:a + len(ch.ins)], cout[b:b + len(ch.out_shapes)], send_sems, recv_sems, s)

        pl.when(first)(lambda: each("start"))
        body(*ins, o_ref)
        pl.when(last)(lambda: each("finish"))

    hbm = pl.BlockSpec(memory_space=pl.ANY)
    res = pl.pallas_call(
        wrapped, grid=grid, in_specs=list(in_specs) + [hbm] * i0, out_specs=[out_specs] + [hbm] * o0,
        out_shape=[out_shape] + c_outs,
        scratch_shapes=[pltpu.SemaphoreType.DMA((s0,)), pltpu.SemaphoreType.DMA((s0,))],
        input_output_aliases=alias, name=name,
        compiler_params=pltpu.CompilerParams(dimension_semantics=(ARB,) * len(grid), vmem_limit_bytes=VMEM_LIMIT_V7X,
                                             has_side_effects=True))(*args, *c_ins)
    for ch, b in zip(chunks, out_off):
        ch.done(list(res[1 + b:1 + b + len(ch.out_shapes)]))
    return res[0]


def _mm_nn(a, b, *, name, q=None, out_dtype=F32):
    m, k = a.shape
    n = b.shape[-1]
    tm, tn = _mm_tiles(m, k, n, jnp.dtype(out_dtype).itemsize)

    def body(a_ref, b_ref, o_ref):
        o_ref[...] = _dot(a_ref[...], b_ref[...]).astype(o_ref.dtype)

    return _hosted_call(
        body, grid=(m // tm, n // tn),
        in_specs=[pl.BlockSpec((tm, k), lambda i, j: (i, 0)),
                  pl.BlockSpec((None, k, tn), lambda i, j: (0, 0, j))],
        out_specs=pl.BlockSpec((tm, tn), lambda i, j: (i, j)),
        out_shape=jax.ShapeDtypeStruct((m, n), out_dtype),
        args=(a, b), name=name, q=q, flops=2.0 * m * k * n)


def _mm_nt(a, b, *, name, q=None, out_dtype=F32):
    m, k = a.shape
    n = b.shape[-2]
    tm, tn = _mm_tiles(m, k, n, jnp.dtype(out_dtype).itemsize)

    def body(a_ref, b_ref, o_ref):
        o_ref[...] = _dot_nt(a_ref[...], b_ref[...]).astype(o_ref.dtype)

    return _hosted_call(
        body, grid=(m // tm, n // tn),
        in_specs=[pl.BlockSpec((tm, k), lambda i, j: (i, 0)),
                  pl.BlockSpec((None, tn, k), lambda i, j: (0, j, 0))],
        out_specs=pl.BlockSpec((tm, tn), lambda i, j: (i, j)),
        out_shape=jax.ShapeDtypeStruct((m, n), out_dtype),
        args=(a, b), name=name, q=q, flops=2.0 * m * k * n)


def _mm_tn(a, b, *, name, q=None, out=None, n_total=None, col_block_offset=0):
    k, m = a.shape
    n = b.shape[-1]
    tk, tm, tn = _mm_tn_tiles(k, m, n)
    off = col_block_offset * (n // tn)

    def body(a_ref, b_ref, *rest):
        o_ref = rest[-1]
        part = _dot_tn(a_ref[...], b_ref[...])

        @pl.when(pl.program_id(2) == 0)
        def _():
            o_ref[...] = part

        @pl.when(pl.program_id(2) > 0)
        def _():
            o_ref[...] += part

    in_specs = [pl.BlockSpec((tk, tm), lambda i, j, s: (s, i)), pl.BlockSpec((tk, tn), lambda i, j, s: (s, j))]
    args = (a, b)
    if out is not None:
        in_specs.append(pl.BlockSpec(memory_space=pl.ANY))
        args = (a, b, out)
    return _hosted_call(
        body, grid=(m // tm, n // tn, k // tk), in_specs=in_specs,
        out_specs=pl.BlockSpec((None, tm, tn), lambda i, j, s: (0, i, j + off)),
        out_shape=jax.ShapeDtypeStruct((1, m, n_total or n), F32),
        aliases={2: 0} if out is not None else None,
        args=args, name=name, q=q, flops=2.0 * m * k * n)


def _mm_ffn_dh(dg, dv, w_up, *, name, q=None):
    m, f = dg.shape
    d = w_up.shape[-2]
    tm, tn = _mm_tiles(m, 2 * f, d, 4)

    def body(dg_ref, dv_ref, wg_ref, wv_ref, o_ref):
        o_ref[...] = _dot_nt(dg_ref[...], wg_ref[...]) + _dot_nt(dv_ref[...], wv_ref[...])

    return _hosted_call(
        body, grid=(m // tm, d // tn),
        in_specs=[pl.BlockSpec((tm, f), lambda i, j: (i, 0)),
                  pl.BlockSpec((tm, f), lambda i, j: (i, 0)),
                  pl.BlockSpec((None, tn, f), lambda i, j: (0, j, 0)),
                  pl.BlockSpec((None, tn, f), lambda i, j: (0, j, 1))],
        out_specs=pl.BlockSpec((tm, tn), lambda i, j: (i, j)),
        out_shape=jax.ShapeDtypeStruct((m, d), F32),
        args=(dg, dv, w_up, w_up), name=name, q=q, flops=4.0 * m * f * d)


def _norm_fwd(x, g, *, name):
    n, d = x.shape
    tm = _tile(n, 256, SUBLANE)

    def body(x_ref, g_ref, o_ref):
        o_ref[...] = _rms_fwd(x_ref[...], g_ref[...]).astype(o_ref.dtype)

    return pl.pallas_call(
        body, grid=(n // tm,),
        in_specs=[pl.BlockSpec((tm, d), lambda i: (i, 0)), pl.BlockSpec((1, d), lambda i: (0, 0))],
        out_specs=pl.BlockSpec((tm, d), lambda i: (i, 0)),
        out_shape=jax.ShapeDtypeStruct((n, d), MXU),
        name=name, compiler_params=_cp((PAR,)))(x, g)


def _norm_bwd_dg(dy, x, g, *, name):
    n, d = x.shape
    tm = _tile(n, 256, SUBLANE)

    def body(dy_ref, x_ref, g_ref, dg_ref):
        @pl.when(pl.program_id(0) == 0)
        def _():
            dg_ref[...] = jnp.zeros_like(dg_ref)
        _, dg = _rms_bwd(dy_ref[...], x_ref[...], g_ref[...])
        dg_ref[...] += dg

    return pl.pallas_call(
        body, grid=(n // tm,),
        in_specs=[pl.BlockSpec((tm, d), lambda i: (i, 0)), pl.BlockSpec((tm, d), lambda i: (i, 0)),
                  pl.BlockSpec((1, d), lambda i: (0, 0))],
        out_specs=pl.BlockSpec((1, d), lambda i: (0, 0)),
        out_shape=jax.ShapeDtypeStruct((1, d), F32),
        name=name, compiler_params=_cp((ARB,)))(dy, x, g)


def _resid_norm_fwd(x, y, g_post, g_pres, *, name):
    n, d = x.shape
    tm = _tile(n, 256, SUBLANE)
    nh = len(g_pres)

    def body(x_ref, y_ref, gp_ref, *rest):
        gpre = rest[:nh]
        xo_ref = rest[nh]
        h_refs = rest[nh + 1:]
        xo = x_ref[...] + _rms_fwd(y_ref[...], gp_ref[...])
        xo_ref[...] = xo
        for g_ref, h_ref in zip(gpre, h_refs):
            h_ref[...] = _rms_fwd(xo, g_ref[...]).astype(h_ref.dtype)

    row = pl.BlockSpec((tm, d), lambda i: (i, 0))
    vec = pl.BlockSpec((1, d), lambda i: (0, 0))
    outs = pl.pallas_call(
        body, grid=(n // tm,),
        in_specs=[row, row, vec] + [vec] * nh,
        out_specs=[row] + [row] * nh,
        out_shape=[jax.ShapeDtypeStruct((n, d), F32)] + [jax.ShapeDtypeStruct((n, d), MXU)] * nh,
        name=name, compiler_params=_cp((PAR,)))(x, y, g_post, *g_pres)
    return outs[0], list(outs[1:])


def _loss_fwd(x, y, g_post, target, *, name):
    n, d = x.shape
    tm = _tile(n, 256, SUBLANE)

    def body(x_ref, y_ref, gp_ref, t_ref, dx_ref, sq_ref):
        @pl.when(pl.program_id(0) == 0)
        def _():
            sq_ref[...] = jnp.zeros_like(sq_ref)
        err = x_ref[...] + _rms_fwd(y_ref[...], gp_ref[...]) - t_ref[...]
        dx_ref[...] = err * (1.0 / d)
        sq_ref[...] += jnp.sum(err * err, axis=0, keepdims=True)

    row = pl.BlockSpec((tm, d), lambda i: (i, 0))
    vec = pl.BlockSpec((1, d), lambda i: (0, 0))
    return pl.pallas_call(
        body, grid=(n // tm,),
        in_specs=[row, row, vec, row],
        out_specs=[row, vec],
        out_shape=[jax.ShapeDtypeStruct((n, d), F32), jax.ShapeDtypeStruct((1, d), F32)],
        name=name, compiler_params=_cp((ARB,)))(x, y, g_post, target)


def _resid_norm_bwd(dx_out, dhs, x_out, g_pres, y, g_post, *, name):
    n, d = dx_out.shape
    tm = _tile(n, 256, SUBLANE)
    nh = len(dhs)
    has_y = y is not None

    def body(*refs):
        it = iter(refs)
        dxo_ref = next(it)
        dh_refs = [next(it) for _ in range(nh)]
        xo_ref = next(it) if nh else None
        gpre_refs = [next(it) for _ in range(nh)]
        y_ref = next(it) if has_y else None
        gpost_ref = next(it) if has_y else None
        g_out = next(it)
        dy_out = next(it) if has_y else None
        dgpre_out = [next(it) for _ in range(nh)]
        dgpost_out = next(it) if has_y else None

        @pl.when(pl.program_id(0) == 0)
        def _():
            for r in dgpre_out:
                r[...] = jnp.zeros_like(r)
            if has_y:
                dgpost_out[...] = jnp.zeros_like(dgpost_out)

        g = dxo_ref[...]
        if nh:
            xo = xo_ref[...]
            for dh_ref, gp_ref, dg_ref in zip(dh_refs, gpre_refs, dgpre_out):
                dx, dg = _rms_bwd(dh_ref[...], xo, gp_ref[...])
                g = g + dx
                dg_ref[...] += dg
        g_out[...] = g
        if has_y:
            dy, dg = _rms_bwd(g, y_ref[...], gpost_ref[...])
            dy_out[...] = dy.astype(dy_out.dtype)
            dgpost_out[...] += dg

    row = pl.BlockSpec((tm, d), lambda i: (i, 0))
    vec = pl.BlockSpec((1, d), lambda i: (0, 0))
    ins, in_specs = [dx_out], [row]
    ins += list(dhs)
    in_specs += [row] * nh
    if nh:
        ins.append(x_out)
        in_specs.append(row)
    ins += list(g_pres)
    in_specs += [vec] * nh
    if has_y:
        ins += [y, g_post]
        in_specs += [row, vec]
    out_specs, out_shape = [row], [jax.ShapeDtypeStruct((n, d), F32)]
    if has_y:
        out_specs.append(row)
        out_shape.append(jax.ShapeDtypeStruct((n, d), MXU))
    out_specs += [vec] * nh
    out_shape += [jax.ShapeDtypeStruct((1, d), F32)] * nh
    if has_y:
        out_specs.append(vec)
        out_shape.append(jax.ShapeDtypeStruct((1, d), F32))
    outs = list(pl.pallas_call(
        body, grid=(n // tm,), in_specs=in_specs, out_specs=out_specs, out_shape=out_shape,
        name=name, compiler_params=_cp((ARB,)))(*ins))
    g = outs.pop(0)
    dy = outs.pop(0) if has_y else None
    dgpre = [outs.pop(0) for _ in range(nh)]
    dgpost = outs.pop(0) if has_y else None
    return g, dy, dgpre, dgpost


def _ffn_conv(up, w_ref, b_ref, row):
    u1 = _shift_down(up, 1, row)
    u2 = _shift_down(up, 2, row)
    u = w_ref[0:1, :] * u2 + w_ref[1:2, :] * u1 + w_ref[2:3, :] * up + b_ref[...]
    return u, u1, u2


def _ffn_act_fwd(up, wconv, bconv, bsz, *, name):
    n, f2 = up.shape
    f = f2 // 2
    t = n // bsz
    tc = _tile(f, 256)
    nf = f // tc

    def body(ug_ref, uv_ref, wg_ref, wv_ref, bg_ref, bv_ref, o_ref):
        row = lax.broadcasted_iota(jnp.int32, (t, tc), 0)
        g, _, _ = _ffn_conv(ug_ref[...], wg_ref, bg_ref, row)
        v, _, _ = _ffn_conv(uv_ref[...], wv_ref, bv_ref, row)
        o_ref[...] = (_gelu(g) * v).astype(o_ref.dtype)

    return pl.pallas_call(
        body, grid=(bsz, nf),
        in_specs=[pl.BlockSpec((t, tc), lambda b, j: (b, j)),
                  pl.BlockSpec((t, tc), lambda b, j: (b, j + nf)),
                  pl.BlockSpec((FFN_CONV, tc), lambda b, j: (0, j)),
                  pl.BlockSpec((FFN_CONV, tc), lambda b, j: (0, j + nf)),
                  pl.BlockSpec((1, tc), lambda b, j: (0, j)),
                  pl.BlockSpec((1, tc), lambda b, j: (0, j + nf))],
        out_specs=pl.BlockSpec((t, tc), lambda b, j: (b, j)),
        out_shape=jax.ShapeDtypeStruct((n, f), MXU),
        name=name, compiler_params=_cp((PAR, PAR)))(up, up, wconv, wconv, bconv, bconv)


def _ffn_act_bwd(up, dact, wconv, bconv, bsz, *, name):
    n, f2 = up.shape
    f = f2 // 2
    t = n // bsz
    tc = _tile(f, 256)
    nf = f // tc

    def body(ug_ref, uv_ref, da_ref, wg_ref, wv_ref, bg_ref, bv_ref,
             dug_ref, duv_ref, dwg_ref, dwv_ref, dbg_ref, dbv_ref):
        @pl.when(pl.program_id(1) == 0)
        def _():
            for r in (dwg_ref, dwv_ref, dbg_ref, dbv_ref):
                r[...] = jnp.zeros_like(r)

        row = lax.broadcasted_iota(jnp.int32, (t, tc), 0)
        ug, uv = ug_ref[...], uv_ref[...]
        g, ug1, ug2 = _ffn_conv(ug, wg_ref, bg_ref, row)
        v, uv1, uv2 = _ffn_conv(uv, wv_ref, bv_ref, row)
        gl, dgl = _gelu_and_grad(g)
        da = da_ref[...]
        dg = da * v * dgl
        dv = da * gl

        def conv_bwd(du, w_ref, x0, x1, x2, dx_ref, dw_ref, db_ref):
            dx = w_ref[2:3, :] * du + w_ref[1:2, :] * _shift_up(du, 1, row) + w_ref[0:1, :] * _shift_up(du, 2, row)
            dx_ref[...] = dx.astype(dx_ref.dtype)
            dw_ref[0:1, :] += jnp.sum(du * x2, axis=0, keepdims=True)
            dw_ref[1:2, :] += jnp.sum(du * x1, axis=0, keepdims=True)
            dw_ref[2:3, :] += jnp.sum(du * x0, axis=0, keepdims=True)
            db_ref[...] += jnp.sum(du, axis=0, keepdims=True)

        conv_bwd(dg, wg_ref, ug, ug1, ug2, dug_ref, dwg_ref, dbg_ref)
        conv_bwd(dv, wv_ref, uv, uv1, uv2, duv_ref, dwv_ref, dbv_ref)

    blk = pl.BlockSpec((t, tc), lambda j, b: (b, j))
    wspec = pl.BlockSpec((FFN_CONV, tc), lambda j, b: (0, j))
    bspec = pl.BlockSpec((1, tc), lambda j, b: (0, j))
    outs = pl.pallas_call(
        body, grid=(nf, bsz),
        in_specs=[blk, pl.BlockSpec((t, tc), lambda j, b: (b, j + nf)), blk,
                  wspec, pl.BlockSpec((FFN_CONV, tc), lambda j, b: (0, j + nf)),
                  bspec, pl.BlockSpec((1, tc), lambda j, b: (0, j + nf))],
        out_specs=[blk, blk, wspec, wspec, bspec, bspec],
        out_shape=[jax.ShapeDtypeStruct((n, f), MXU), jax.ShapeDtypeStruct((n, f), MXU),
                   jax.ShapeDtypeStruct((FFN_CONV, f), F32), jax.ShapeDtypeStruct((FFN_CONV, f), F32),
                   jax.ShapeDtypeStruct((1, f), F32), jax.ShapeDtypeStruct((1, f), F32)],
        name=name, compiler_params=_cp((PAR, ARB)))(up, up, dact, wconv, wconv, bconv, bconv)
    dug, duv, dwg, dwv, dbg, dbv = outs
    return dug, duv, jnp.concatenate([dwg, dwv], axis=1), jnp.concatenate([dbg, dbv], axis=1)


def _mem_attn_fwd(proj, q_col_block, mkv, ycat, bsz, *, name):
    n = proj.shape[0]
    t = n // bsz
    tq = _tile(t, 512, SUBLANE)
    nt = t // tq
    scale = HEAD_DIM ** -0.5

    def body(q_ref, kv_ref, old_ref, o_ref):
        del old_ref
        outs = []
        for h in range(MEM_HEADS):
            sl = slice(h * HEAD_DIM, (h + 1) * HEAD_DIM)
            q = q_ref[:, sl].astype(MXU)
            k = kv_ref[:, sl].astype(MXU)
            v = kv_ref[:, MEM_WIDTH + h * HEAD_DIM: MEM_WIDTH + (h + 1) * HEAD_DIM].astype(MXU)
            s = _dot_nt(q, k) * scale
            m = jnp.max(s, axis=-1, keepdims=True)
            p = jnp.exp(s - m)
            p = p / jnp.sum(p, axis=-1, keepdims=True)
            outs.append(_dot(p.astype(MXU), v))
        o_ref[...] = jnp.concatenate(outs, axis=-1).astype(o_ref.dtype)

    return pl.pallas_call(
        body, grid=(bsz, nt),
        in_specs=[pl.BlockSpec((tq, MEM_WIDTH), lambda b, i: (b * nt + i, q_col_block)),
                  pl.BlockSpec((MEM_LEN, 2 * MEM_WIDTH), lambda b, i: (b, 0)),
                  pl.BlockSpec(memory_space=pl.ANY)],
        out_specs=pl.BlockSpec((tq, MEM_WIDTH), lambda b, i: (b * nt + i, MIX_WIDTH // MEM_WIDTH)),
        out_shape=jax.ShapeDtypeStruct(ycat.shape, ycat.dtype),
        input_output_aliases={2: 0},
        name=name, compiler_params=_cp((PAR, PAR)))(proj, mkv, ycat)


def _mem_attn_bwd(proj, q_col_block, mkv, dycat, dproj, bsz, *, name):
    n = proj.shape[0]
    t = n // bsz
    tq = _tile(t, 512, SUBLANE)
    nt = t // tq
    scale = HEAD_DIM ** -0.5

    def body(q_ref, kv_ref, do_ref, old_ref, dq_ref, dkv_ref):
        del old_ref

        @pl.when(pl.program_id(1) == 0)
        def _():
            dkv_ref[...] = jnp.zeros_like(dkv_ref)

        dqs, dks, dvs = [], [], []
        for h in range(MEM_HEADS):
            sl = slice(h * HEAD_DIM, (h + 1) * HEAD_DIM)
            q = q_ref[:, sl].astype(MXU)
            k = kv_ref[:, sl].astype(MXU)
            v = kv_ref[:, MEM_WIDTH + h * HEAD_DIM: MEM_WIDTH + (h + 1) * HEAD_DIM].astype(MXU)
            do = do_ref[:, sl].astype(MXU)
            s = _dot_nt(q, k) * scale
            m = jnp.max(s, axis=-1, keepdims=True)
            p = jnp.exp(s - m)
            p = p / jnp.sum(p, axis=-1, keepdims=True)
            dvs.append(_dot_tn(p.astype(MXU), do))
            dp = _dot_nt(do, v)
            ds = (p * (dp - jnp.sum(dp * p, axis=-1, keepdims=True)) * scale).astype(MXU)
            dqs.append(_dot(ds, k))
            dks.append(_dot_tn(ds, q))
        dq_ref[...] = jnp.concatenate(dqs, axis=-1).astype(dq_ref.dtype)
        dkv_ref[...] += jnp.concatenate(dks + dvs, axis=-1)

    return pl.pallas_call(
        body, grid=(bsz, nt),
        in_specs=[pl.BlockSpec((tq, MEM_WIDTH), lambda b, i: (b * nt + i, q_col_block)),
                  pl.BlockSpec((MEM_LEN, 2 * MEM_WIDTH), lambda b, i: (b, 0)),
                  pl.BlockSpec((tq, MEM_WIDTH), lambda b, i: (b * nt + i, MIX_WIDTH // MEM_WIDTH)),
                  pl.BlockSpec(memory_space=pl.ANY)],
        out_specs=[pl.BlockSpec((tq, MEM_WIDTH), lambda b, i: (b * nt + i, q_col_block)),
                   pl.BlockSpec((MEM_LEN, 2 * MEM_WIDTH), lambda b, i: (b, 0))],
        out_shape=[jax.ShapeDtypeStruct(dproj.shape, dproj.dtype),
                   jax.ShapeDtypeStruct((bsz * MEM_LEN, 2 * MEM_WIDTH), F32)],
        input_output_aliases={3: 0},
        name=name, compiler_params=_cp((PAR, ARB)))(proj, mkv, dycat, dproj)


def _swa_scores(q, k, h, dist, mask, sink):
    s = _dot_nt(q, k) * (HEAD_DIM ** -0.5)
    s = jnp.where(mask, s - SLOPES[h] * dist, -jnp.inf)
    m = jnp.maximum(jnp.max(s, axis=-1, keepdims=True), sink)
    p = jnp.exp(s - m)
    psink = jnp.exp(sink - m)
    inv = 1.0 / (jnp.sum(p, axis=-1, keepdims=True) + psink)
    return p * inv, psink * inv


def _swa_mask(n):
    qi = lax.broadcasted_iota(jnp.int32, (WINDOW, 2 * WINDOW), 0) + WINDOW
    ki = lax.broadcasted_iota(jnp.int32, (WINDOW, 2 * WINDOW), 1)
    dist = qi - ki
    mask = (dist >= 0) & (dist < WINDOW) & ((n > 0) | (ki >= WINDOW))
    return dist.astype(F32), mask


def _swa_fwd(proj, kv, sinks, bsz, *, name):
    n_tok = proj.shape[0]
    nb = n_tok // bsz // WINDOW
    kvw = SWA_KV_HEADS * HEAD_DIM

    def body(sink_ref, q_ref, kvp_ref, kvc_ref, o_ref):
        n = pl.program_id(1)
        dist, mask = _swa_mask(n)
        kk = jnp.concatenate([kvp_ref[:, :kvw], kvc_ref[:, :kvw]], axis=0).astype(MXU)
        vv = jnp.concatenate([kvp_ref[:, kvw:], kvc_ref[:, kvw:]], axis=0).astype(MXU)
        outs = []
        for h in range(SWA_HEADS):
            c = h // SWA_GROUP
            q = q_ref[:, h * HEAD_DIM:(h + 1) * HEAD_DIM].astype(MXU)
            p, _ = _swa_scores(q, kk[:, c * HEAD_DIM:(c + 1) * HEAD_DIM], h, dist, mask, sink_ref[h])
            outs.append(_dot(p.astype(MXU), vv[:, c * HEAD_DIM:(c + 1) * HEAD_DIM]))
        o_ref[...] = jnp.concatenate(outs, axis=-1).astype(o_ref.dtype)

    return pl.pallas_call(
        body, grid=(bsz, nb),
        in_specs=[pl.BlockSpec(memory_space=pltpu.SMEM),
                  pl.BlockSpec((WINDOW, MIX_WIDTH), lambda b, n: (b * nb + n, 0)),
                  pl.BlockSpec((WINDOW, 2 * kvw), lambda b, n: (b * nb + jnp.maximum(n - 1, 0), 0)),
                  pl.BlockSpec((WINDOW, 2 * kvw), lambda b, n: (b * nb + n, 0))],
        out_specs=pl.BlockSpec((WINDOW, MIX_WIDTH), lambda b, n: (b * nb + n, 0)),
        out_shape=jax.ShapeDtypeStruct((n_tok, D_MODEL), MXU),
        name=name, compiler_params=_cp((PAR, PAR)))(sinks, proj, kv, kv)


def _swa_bwd(proj, kv, sinks, dycat, bsz, *, name):
    n_tok = proj.shape[0]
    nb = n_tok // bsz // WINDOW
    kvw = SWA_KV_HEADS * HEAD_DIM

    def body(sink_ref, q_ref, kvp_ref, kvc_ref, do_ref, dq_ref, dkvc_ref, dkvp_ref, dsink_ref):
        n = pl.program_id(1)

        @pl.when((pl.program_id(0) == 0) & (n == 0))
        def _():
            dsink_ref[...] = jnp.zeros_like(dsink_ref)

        dist, mask = _swa_mask(n)
        kk = jnp.concatenate([kvp_ref[:, :kvw], kvc_ref[:, :kvw]], axis=0).astype(MXU)
        vv = jnp.concatenate([kvp_ref[:, kvw:], kvc_ref[:, kvw:]], axis=0).astype(MXU)
        lane = lax.broadcasted_iota(jnp.int32, (SUBLANE, LANE), 1)
        dqs = []
        dks = [None] * SWA_KV_HEADS
        dvs = [None] * SWA_KV_HEADS
        dsink = jnp.zeros((SUBLANE, LANE), F32)
        for h in range(SWA_HEADS):
            c = h // SWA_GROUP
            k = kk[:, c * HEAD_DIM:(c + 1) * HEAD_DIM]
            v = vv[:, c * HEAD_DIM:(c + 1) * HEAD_DIM]
            q = q_ref[:, h * HEAD_DIM:(h + 1) * HEAD_DIM].astype(MXU)
            do = do_ref[:, h * HEAD_DIM:(h + 1) * HEAD_DIM].astype(MXU)
            p, psink = _swa_scores(q, k, h, dist, mask, sink_ref[h])
            dv = _dot_tn(p.astype(MXU), do)
            dp = _dot_nt(do, v)
            rs = jnp.sum(dp * p, axis=-1, keepdims=True)
            ds = (p * (dp - rs) * (HEAD_DIM ** -0.5)).astype(MXU)
            dsink = dsink + jnp.where(lane == h, jnp.sum(-psink * rs, axis=0, keepdims=True), 0.0)
            dqs.append(_dot(ds, k))
            dk = _dot_tn(ds, q)
            dks[c] = dk if dks[c] is None else dks[c] + dk
            dvs[c] = dv if dvs[c] is None else dvs[c] + dv
        dq_ref[...] = jnp.concatenate(dqs, axis=-1).astype(dq_ref.dtype)
        dkv = jnp.concatenate(dks + dvs, axis=-1)
        dkvp_ref[...] = dkv[:WINDOW]
        dkvc_ref[...] = dkv[WINDOW:]
        dsink_ref[...] += dsink

    qspec = pl.BlockSpec((WINDOW, MIX_WIDTH), lambda b, n: (b * nb + n, 0))
    kvspec = pl.BlockSpec((WINDOW, 2 * kvw), lambda b, n: (b * nb + n, 0))
    return pl.pallas_call(
        body, grid=(bsz, nb),
        in_specs=[pl.BlockSpec(memory_space=pltpu.SMEM), qspec,
                  pl.BlockSpec((WINDOW, 2 * kvw), lambda b, n: (b * nb + jnp.maximum(n - 1, 0), 0)),
                  kvspec, qspec],
        out_specs=[qspec, kvspec, kvspec, pl.BlockSpec((SUBLANE, LANE), lambda b, n: (0, 0))],
        out_shape=[jax.ShapeDtypeStruct((n_tok, D_MODEL), MXU),
                   jax.ShapeDtypeStruct((n_tok, 2 * kvw), F32),
                   jax.ShapeDtypeStruct((n_tok, 2 * kvw), F32),
                   jax.ShapeDtypeStruct((SUBLANE, LANE), F32)],
        name=name, compiler_params=_cp((ARB, ARB)))(sinks, proj, kv, kv, dycat)


def _swa_dkv_combine(curs, prevs, bsz, *, name):
    n_tok, w = curs[0].shape
    nb = n_tok // bsz // WINDOW
    k = len(curs)

    def body(*refs):
        o_ref = refs[-1]
        n = pl.program_id(1)
        acc = refs[0][...]
        for r in refs[1:k]:
            acc = acc + r[...]
        nxt = refs[k][...]
        for r in refs[k + 1:2 * k]:
            nxt = nxt + r[...]
        o_ref[...] = (acc + jnp.where(n < nb - 1, nxt, 0.0)).astype(o_ref.dtype)

    cur = pl.BlockSpec((WINDOW, w), lambda b, n: (b * nb + n, 0))
    prv = pl.BlockSpec((WINDOW, w), lambda b, n: (b * nb + jnp.minimum(n + 1, nb - 1), 0))
    return pl.pallas_call(
        body, grid=(bsz, nb), in_specs=[cur] * k + [prv] * k, out_specs=cur,
        out_shape=jax.ShapeDtypeStruct((n_tok, w), MXU),
        name=name, compiler_params=_cp((PAR, PAR)))(*curs, *prevs)


def _lru_gates(ux, halo, ext_ref, wc_ref, bc_ref, wr_ref, br_ref, wi_ref, bi_ref, lam_ref):
    tt = ux.shape[0]
    ext_ref[0:SUBLANE, :] = halo
    ext_ref[SUBLANE:, :] = ux
    xs = [ux] + [ext_ref[pl.ds(SUBLANE - k, tt), :] for k in range(1, LRU_CONV)]
    xc = bc_ref[...] + wc_ref[3:4, :] * xs[0] + wc_ref[2:3, :] * xs[1] + wc_ref[1:2, :] * xs[2] + wc_ref[0:1, :] * xs[3]
    pre_r, pre_i = [], []
    for blk in range(MIX_WIDTH // GATE_TILE):
        xb = xc[:, blk * GATE_TILE:(blk + 1) * GATE_TILE].astype(MXU)
        pre_r.append(_dot(xb, wr_ref[blk]))
        pre_i.append(_dot(xb, wi_ref[blk]))
    r = jax.nn.sigmoid(jnp.concatenate(pre_r, axis=-1) + br_ref[...])
    i = jax.nn.sigmoid(jnp.concatenate(pre_i, axis=-1) + bi_ref[...])
    nlam = -lam_ref[...]
    sp = jnp.maximum(nlam, 0.0) + jnp.log(1.0 + jnp.exp(-jnp.abs(nlam)))
    log_a = -LRU_C * r * sp
    a = jnp.exp(log_a)
    om = -jnp.tanh(log_a) * (a * a + 1.0)
    s = jnp.sqrt(om)
    return xs, xc, r, i, sp, a, s


def _lru_fwd(proj, wconv, bconv, wr, br, wi, bi, lam, bsz, *, name):
    n_tok = proj.shape[0]
    t = n_tok // bsz
    tt = _tile(t, 256, SUBLANE)
    nt = t // tt
    w = MIX_WIDTH
    ng = tt // SUBLANE

    def body(pg_ref, halo_ref, wc_ref, bc_ref, wr_ref, br_ref, wi_ref, bi_ref, lam_ref,
             y_ref, h_ref, ext_ref, a_ref, b_ref, carry_ref):
        ti = pl.program_id(1)

        @pl.when(ti == 0)
        def _():
            carry_ref[...] = jnp.zeros_like(carry_ref)

        gate = pg_ref[:, :w]
        ux = pg_ref[:, w:]
        halo = jnp.where(ti > 0, halo_ref[...], 0.0)
        _, xc, _, i, _, a, s = _lru_gates(ux, halo, ext_ref, wc_ref, bc_ref, wr_ref, br_ref, wi_ref, bi_ref, lam_ref)
        a_ref[...] = a
        b_ref[...] = s * (i * xc)
        row = lax.broadcasted_iota(jnp.int32, (SUBLANE, w), 0)

        def group(g, hprev):
            off = pl.multiple_of(g * SUBLANE, SUBLANE)
            ca = a_ref[pl.ds(off, SUBLANE), :]
            cb = b_ref[pl.ds(off, SUBLANE), :]
            for d in (1, 2, 4):
                a_sh = jnp.where(row >= d, pltpu.roll(ca, d, axis=0), 1.0)
                b_sh = jnp.where(row >= d, pltpu.roll(cb, d, axis=0), 0.0)
                cb = ca * b_sh + cb
                ca = ca * a_sh
            h = ca * hprev + cb
            b_ref[pl.ds(off, SUBLANE), :] = h
            return jnp.broadcast_to(h[SUBLANE - 1:SUBLANE, :], (SUBLANE, w))

        carry_ref[...] = lax.fori_loop(0, ng, group, carry_ref[...])
        h = b_ref[...]
        h_ref[...] = h
        y_ref[...] = (h * _gelu(gate)).astype(y_ref.dtype)

    vec = lambda r: pl.BlockSpec((r, w), lambda b, i: (0, 0))
    wspec = pl.BlockSpec((w // GATE_TILE, GATE_TILE, GATE_TILE), lambda b, i: (0, 0, 0))
    hb = tt // SUBLANE
    return pl.pallas_call(
        body, grid=(bsz, nt),
        in_specs=[pl.BlockSpec((tt, 2 * w), lambda b, i: (b * nt + i, 0)),
                  pl.BlockSpec((SUBLANE, w), lambda b, i: (jnp.maximum((b * nt + i) * hb - 1, 0), 1)),
                  vec(LRU_CONV), vec(1), wspec, vec(1), wspec, vec(1), vec(1)],
        out_specs=[pl.BlockSpec((tt, w), lambda b, i: (b * nt + i, 0)),
                   pl.BlockSpec((tt, w), lambda b, i: (b * nt + i, 0))],
        out_shape=[jax.ShapeDtypeStruct((n_tok, D_MODEL), MXU), jax.ShapeDtypeStruct((n_tok, w), F32)],
        scratch_shapes=[pltpu.VMEM((tt + SUBLANE, w), F32), pltpu.VMEM((tt, w), F32),
                        pltpu.VMEM((tt, w), F32), pltpu.VMEM((SUBLANE, w), F32)],
        name=name, compiler_params=_cp((PAR, ARB)))(proj, proj, wconv, bconv, wr, br, wi, bi, lam)


def _lru_bwd(proj, hs, dycat, wconv, bconv, wr, br, wi, bi, lam, bsz, *, name):
    n_tok = proj.shape[0]
    t = n_tok // bsz
    tt = _tile(t, 256, SUBLANE)
    nt = t // tt
    w = MIX_WIDTH
    ng = tt // SUBLANE
    nblk = w // GATE_TILE

    def body(pg_ref, halo_ref, h_ref, hhalo_ref, dy_ref, wc_ref, bc_ref, wr_ref, br_ref, wi_ref, bi_ref, lam_ref,
             dp_ref, dwc_ref, dbc_ref, dwr_ref, dbr_ref, dwi_ref, dbi_ref, dlam_ref,
             ext_ref, a_ref, c_ref, g_ref, gcarry_ref, xcarry_ref):
        bi_ = pl.program_id(0)
        ti = nt - 1 - pl.program_id(1)

        @pl.when((bi_ == 0) & (pl.program_id(1) == 0))
        def _():
            for r in (dwc_ref, dbc_ref, dwr_ref, dbr_ref, dwi_ref, dbi_ref, dlam_ref):
                r[...] = jnp.zeros_like(r)

        @pl.when(pl.program_id(1) == 0)
        def _():
            gcarry_ref[...] = jnp.zeros_like(gcarry_ref)
            xcarry_ref[...] = jnp.zeros_like(xcarry_ref)

        gate = pg_ref[:, :w]
        ux = pg_ref[:, w:]
        halo = jnp.where(ti > 0, halo_ref[...], 0.0)
        xs, xc, r, i, sp, a, s = _lru_gates(ux, halo, ext_ref, wc_ref, bc_ref, wr_ref, br_ref, wi_ref, bi_ref, lam_ref)
        h = h_ref[...]
        gl, dgl = _gelu_and_grad(gate)
        dy = dy_ref[...]
        dgate = dy * h * dgl
        row_t = lax.broadcasted_iota(jnp.int32, (tt, w), 0)
        g_ref[...] = dy * gl + jnp.where(row_t == tt - 1, gcarry_ref[0:1, :], 0.0)
        c_ref[...] = _shift_up(a, 1, row_t)
        row = lax.broadcasted_iota(jnp.int32, (SUBLANE, w), 0)

        a_ref[...] = a

        def group(k, gnext):
            off = pl.multiple_of((ng - 1 - k) * SUBLANE, SUBLANE)
            cc = c_ref[pl.ds(off, SUBLANE), :]
            cb = g_ref[pl.ds(off, SUBLANE), :]
            cb = cb + jnp.where(row == SUBLANE - 1, gnext, 0.0)
            cc = jnp.where(row == SUBLANE - 1, 0.0, cc)
            for d in (1, 2, 4):
                c_sh = jnp.where(row < SUBLANE - d, pltpu.roll(cc, SUBLANE - d, axis=0), 1.0)
                b_sh = jnp.where(row < SUBLANE - d, pltpu.roll(cb, SUBLANE - d, axis=0), 0.0)
                cb = cc * b_sh + cb
                cc = cc * c_sh
            g_ref[pl.ds(off, SUBLANE), :] = cb
            a0 = a_ref[pl.ds(off, SUBLANE), :]
            return jnp.broadcast_to(a0[0:1, :] * cb[0:1, :], (SUBLANE, w))

        gc = lax.fori_loop(0, ng, group, jnp.zeros((SUBLANE, w), F32))
        gcarry_ref[...] = gc
        gsc = g_ref[...]

        hhalo = jnp.where(ti > 0, hhalo_ref[SUBLANE - 1:SUBLANE, :], 0.0)
        hprev = jnp.where(row_t == 0, hhalo, pltpu.roll(h, 1, axis=0))
        gated = i * xc
        d_gated = gsc * s
        d_atot = gsc * hprev - (gsc * gated) * a / s
        d_loga = d_atot * a
        d_r = d_loga * (-LRU_C) * sp
        dlam_ref[...] += jnp.sum(d_loga * r, axis=0, keepdims=True) * (LRU_C * jax.nn.sigmoid(-lam_ref[...]))
        d_i = d_gated * xc
        d_xc = d_gated * i
        d_pr = d_r * r * (1.0 - r)
        d_pi = d_i * i * (1.0 - i)
        dbr_ref[...] += jnp.sum(d_pr, axis=0, keepdims=True)
        dbi_ref[...] += jnp.sum(d_pi, axis=0, keepdims=True)
        extra = []
        for blk in range(nblk):
            sl = slice(blk * GATE_TILE, (blk + 1) * GATE_TILE)
            xb = xc[:, sl].astype(MXU)
            dr_b = d_pr[:, sl].astype(MXU)
            di_b = d_pi[:, sl].astype(MXU)
            dwr_ref[blk] += _dot_tn(xb, dr_b)
            dwi_ref[blk] += _dot_tn(xb, di_b)
            extra.append(_dot_nt(dr_b, wr_ref[blk]) + _dot_nt(di_b, wi_ref[blk]))
        d_xc = d_xc + jnp.concatenate(extra, axis=-1)
        dbc_ref[...] += jnp.sum(d_xc, axis=0, keepdims=True)
        for k in range(LRU_CONV):
            dwc_ref[k:k + 1, :] += jnp.sum(d_xc * xs[LRU_CONV - 1 - k], axis=0, keepdims=True)
        ext_ref[0:tt, :] = d_xc
        ext_ref[tt:, :] = xcarry_ref[...]
        dux = wc_ref[3:4, :] * d_xc
        for k in range(LRU_CONV - 1):
            dux = dux + wc_ref[k:k + 1, :] * ext_ref[pl.ds(LRU_CONV - 1 - k, tt), :]
        xcarry_ref[...] = d_xc[0:SUBLANE, :]
        dp_ref[:, :w] = dgate.astype(dp_ref.dtype)
        dp_ref[:, w:] = dux.astype(dp_ref.dtype)

    vec = lambda r: pl.BlockSpec((r, w), lambda b, i: (0, 0))
    wspec = pl.BlockSpec((nblk, GATE_TILE, GATE_TILE), lambda b, i: (0, 0, 0))
    hb = tt // SUBLANE
    rblk = lambda b, i: b * nt + (nt - 1 - i)
    halo_idx = lambda b, i: jnp.maximum(rblk(b, i) * hb - 1, 0)
    wide = pl.BlockSpec((tt, 2 * w), lambda b, i: (rblk(b, i), 0))
    narrow = pl.BlockSpec((tt, w), lambda b, i: (rblk(b, i), 0))
    return pl.pallas_call(
        body, grid=(bsz, nt),
        in_specs=[wide, pl.BlockSpec((SUBLANE, w), lambda b, i: (halo_idx(b, i), 1)),
                  narrow, pl.BlockSpec((SUBLANE, w), lambda b, i: (halo_idx(b, i), 0)), narrow,
                  vec(LRU_CONV), vec(1), wspec, vec(1), wspec, vec(1), vec(1)],
        out_specs=[wide, vec(LRU_CONV), vec(1), wspec, vec(1), wspec, vec(1), vec(1)],
        out_shape=[jax.ShapeDtypeStruct((n_tok, 2 * w + MEM_WIDTH), MXU),
                   jax.ShapeDtypeStruct((LRU_CONV, w), F32), jax.ShapeDtypeStruct((1, w), F32),
                   jax.ShapeDtypeStruct((nblk, GATE_TILE, GATE_TILE), F32), jax.ShapeDtypeStruct((1, w), F32),
                   jax.ShapeDtypeStruct((nblk, GATE_TILE, GATE_TILE), F32), jax.ShapeDtypeStruct((1, w), F32),
                   jax.ShapeDtypeStruct((1, w), F32)],
        scratch_shapes=[pltpu.VMEM((tt + SUBLANE, w), F32), pltpu.VMEM((tt, w), F32), pltpu.VMEM((tt, w), F32),
                        pltpu.VMEM((tt, w), F32), pltpu.VMEM((SUBLANE, w), F32), pltpu.VMEM((SUBLANE, w), F32)],
        name=name, compiler_params=_cp((ARB, ARB)))(proj, proj, hs, hs, dycat, wconv, bconv, wr, br, wi, bi, lam)


def _gate_tiles(w):
    per = GATE_TILE // HEAD_DIM
    w4 = w.reshape(LRU_BLOCKS // per, per, HEAD_DIM, HEAD_DIM)
    eye = jnp.eye(per, dtype=w.dtype)
    return jnp.einsum("bnij,nm->bnimj", w4, eye).reshape(LRU_BLOCKS // per, GATE_TILE, GATE_TILE)


def _gate_blocks(t):
    per = GATE_TILE // HEAD_DIM
    t5 = t.reshape(LRU_BLOCKS // per, per, HEAD_DIM, per, HEAD_DIM)
    eye = jnp.eye(per, dtype=t.dtype)
    return jnp.einsum("bnimj,nm->bnij", t5, eye).reshape(LRU_BLOCKS, HEAD_DIM, HEAD_DIM)


def _row(v):
    return v.reshape(1, -1)


def _local_step(x, mem, target, p, wfull, push_grad, q):
    bsz, t, d = x.shape
    n = bsz * t
    x2d = x.reshape(n, d)
    tgt = target.reshape(n, d)
    mem2d = mem.reshape(bsz * MEM_LEN, d)
    wr_t = [_gate_tiles(p["w_rg_r"][j]).astype(MXU) for j in range(N_A)]
    wi_t = [_gate_tiles(p["w_rg_i"][j]).astype(MXU) for j in range(N_A)]

    mn = [_norm_fwd(mem2d, _row(p["g_mem"][l]), name=f"mem_norm{l}") for l in range(DEPTH)]
    mkv = [None] * DEPTH
    h = _norm_fwd(x2d, _row(p["g_mix_pre"][0]), name="in_norm")
    xin = x2d
    sv = []
    kv = hkv = None
    for l in range(DEPTH):
        s = {"xin": xin, "h": h}
        mkv[l] = _mm_nn(mn[l], wfull("w_mem_kv", l), name=f"mem_kv{l}", q=q)
        if l < N_A:
            proj = _mm_nn(h, wfull("w_in_a", l), name=f"in_proj{l}", q=q)
            ycat, hs = _lru_fwd(proj, p["w_conv_a"][l], _row(p["b_conv_a"][l]), wr_t[l], _row(p["b_rg_r"][l]),
                                wi_t[l], _row(p["b_rg_i"][l]), _row(p["lru_lambda"][l]), bsz, name=f"lru_fwd{l}")
            s["hs"] = hs
            qblk = 2 * MIX_WIDTH // MEM_WIDTH
        else:
            if l == N_A:
                kv = _mm_nn(hkv, wfull("w_kv", 0), name="kv_proj", q=q)
            proj = _mm_nn(h, wfull("w_in_b", l - N_A), name=f"in_proj{l}", q=q)
            ycat = _swa_fwd(proj, kv, p["sinks_b"][l - N_A], bsz, name=f"swa_fwd{l}")
            qblk = MIX_WIDTH // MEM_WIDTH
        ycat = _mem_attn_fwd(proj, qblk, mkv[l], ycat, bsz, name=f"mem_attn_fwd{l}")
        y = _mm_nn(ycat, wfull("w_mix_out", l), name=f"mix_out{l}", q=q)
        x1, (h2,) = _resid_norm_fwd(xin, y, _row(p["g_mix_post"][l]), [_row(p["g_ffn_pre"][l])], name=f"mix_resid{l}")
        up = _mm_nn(h2, wfull("w_ffn_up", l), name=f"ffn_up{l}", q=q)
        act = _ffn_act_fwd(up, p["w_ffn_conv"][l], _row(p["b_ffn_conv"][l]), bsz, name=f"ffn_act{l}")
        f = _mm_nn(act, wfull("w_ffn_down", l), name=f"ffn_down{l}", q=q)
        s.update(proj=proj, qblk=qblk, ycat=ycat, y=y, x1=x1, h2=h2, up=up, act=act, f=f)
        sv.append(s)
        if l < DEPTH - 1:
            g_pres = [_row(p["g_mix_pre"][l + 1])] + ([_row(p["g_kv"])] if l + 1 == N_A else [])
            xin, hn = _resid_norm_fwd(x1, f, _row(p["g_ffn_post"][l]), g_pres, name=f"ffn_resid{l}")
            h = hn[0]
            if l + 1 == N_A:
                hkv = hn[1]
        else:
            g_tot, sq = _loss_fwd(x1, f, _row(p["g_ffn_post"][l]), tgt, name="loss")

    gs = {k: [None] * DEPTH for k in ("g_mix_pre", "g_mix_post", "g_ffn_pre", "g_ffn_post", "g_mem",
                                       "w_ffn_conv", "b_ffn_conv")}
    ga = {k: [None] * N_A for k in ("w_conv_a", "b_conv_a", "w_rg_r", "b_rg_r", "w_rg_i", "b_rg_i", "lru_lambda")}
    gsink = [None] * (DEPTH - N_A)
    dkv_cur, dkv_prev = [], []
    g_tot, df, _, gs["g_ffn_post"][DEPTH - 1] = _resid_norm_bwd(
        g_tot, [], None, [], sv[-1]["f"], _row(p["g_ffn_post"][DEPTH - 1]), name="loss_bwd")
    grad_x = None
    for l in reversed(range(DEPTH)):
        s = sv[l]
        dact = _mm_nt(df, wfull("w_ffn_down", l), name=f"d_act{l}", q=q)
        push_grad("w_ffn_down", l, _mm_tn(s["act"], df, name=f"dw_down{l}", q=q))
        dug, duv, gs["w_ffn_conv"][l], gs["b_ffn_conv"][l] = _ffn_act_bwd(
            s["up"], dact, p["w_ffn_conv"][l], _row(p["b_ffn_conv"][l]), bsz, name=f"ffn_act_bwd{l}")
        dh2 = _mm_ffn_dh(dug, duv, wfull("w_ffn_up", l), name=f"d_h2_{l}", q=q)
        dwu = _mm_tn(s["h2"], dug, name=f"dw_up_g{l}", q=q, n_total=2 * D_FF)
        push_grad("w_ffn_up", l, _mm_tn(s["h2"], duv, name=f"dw_up_v{l}", q=q, out=dwu, n_total=2 * D_FF,
                                        col_block_offset=1))
        g1, dy, (gs["g_ffn_pre"][l],), gs["g_mix_post"][l] = _resid_norm_bwd(
            g_tot, [dh2], s["x1"], [_row(p["g_ffn_pre"][l])], s["y"], _row(p["g_mix_post"][l]), name=f"mix_resid_bwd{l}")
        dycat = _mm_nt(dy, wfull("w_mix_out", l), name=f"d_ycat{l}", q=q)
        push_grad("w_mix_out", l, _mm_tn(s["ycat"], dy, name=f"dw_mix_out{l}", q=q))
        if l < N_A:
            dproj, dwc, dbc, dwr, dbr, dwi, dbi, dlam = _lru_bwd(
                s["proj"], s["hs"], dycat, p["w_conv_a"][l], _row(p["b_conv_a"][l]), wr_t[l], _row(p["b_rg_r"][l]),
                wi_t[l], _row(p["b_rg_i"][l]), _row(p["lru_lambda"][l]), bsz, name=f"lru_bwd{l}")
            ga["w_conv_a"][l], ga["b_conv_a"][l], ga["lru_lambda"][l] = dwc, dbc[0], dlam[0]
            ga["w_rg_r"][l], ga["w_rg_i"][l] = _gate_blocks(dwr), _gate_blocks(dwi)
            ga["b_rg_r"][l] = dbr.reshape(LRU_BLOCKS, HEAD_DIM)
            ga["b_rg_i"][l] = dbi.reshape(LRU_BLOCKS, HEAD_DIM)
            w_in, j = "w_in_a", l
        else:
            dproj, dc, dp_, dsk = _swa_bwd(s["proj"], kv, p["sinks_b"][l - N_A], dycat, bsz, name=f"swa_bwd{l}")
            dkv_cur.append(dc)
            dkv_prev.append(dp_)
            gsink[l - N_A] = dsk[0, :SWA_HEADS]
            w_in, j = "w_in_b", l - N_A
        dproj, dmkv = _mem_attn_bwd(s["proj"], s["qblk"], mkv[l], dycat, dproj, bsz, name=f"mem_attn_bwd{l}")
        dh = _mm_nt(dproj, wfull(w_in, j), name=f"d_h{l}", q=q)
        push_grad(w_in, j, _mm_tn(s["h"], dproj, name=f"dw_in{l}", q=q))
        dmkv = dmkv.astype(MXU)
        dmn = _mm_nt(dmkv, wfull("w_mem_kv", l), name=f"d_mem_norm{l}", q=q)
        push_grad("w_mem_kv", l, _mm_tn(mn[l], dmkv, name=f"dw_mem_kv{l}", q=q))
        gs["g_mem"][l] = _norm_bwd_dg(dmn, mem2d, _row(p["g_mem"][l]), name=f"mem_norm_bwd{l}")
        dhs, g_pres = [dh], [_row(p["g_mix_pre"][l])]
        if l == N_A:
            dkv = _swa_dkv_combine(dkv_cur, dkv_prev, bsz, name="dkv_combine")
            dhs.append(_mm_nt(dkv, wfull("w_kv", 0), name="d_hkv", q=q))
            g_pres.append(_row(p["g_kv"]))
            push_grad("w_kv", 0, _mm_tn(hkv, dkv, name="dw_kv", q=q))
        if l > 0:
            g_tot, df, dgpre, gs["g_ffn_post"][l - 1] = _resid_norm_bwd(
                g1, dhs, s["xin"], g_pres, sv[l - 1]["f"], _row(p["g_ffn_post"][l - 1]), name=f"ffn_resid_bwd{l - 1}")
        else:
            grad_x, _, dgpre, _ = _resid_norm_bwd(g1, dhs, s["xin"], g_pres, None, None, name="in_norm_bwd")
        gs["g_mix_pre"][l] = dgpre[0]
        if l == N_A:
            g_kv = dgpre[1][0]

    grads = {}
    for k in ("g_mix_pre", "g_mix_post", "g_ffn_pre", "g_ffn_post", "g_mem", "b_ffn_conv"):
        grads[k] = jnp.concatenate(gs[k], axis=0)
    grads["w_ffn_conv"] = jnp.stack(gs["w_ffn_conv"])
    for k, v in ga.items():
        grads[k] = jnp.stack(v)
    grads["sinks_b"] = jnp.stack(gsink)
    grads["g_kv"] = g_kv
    return jnp.sum(sq), grad_x.reshape(bsz, t, d), grads


N_CHIP = 4
HALF_ALIGN = 16


def _full_shape(kind, shard_shape):
    l, r, c = shard_shape
    return {"row": (l, N_CHIP * r, c), "col": (l, r, N_CHIP * c), "slot": (N_CHIP, l, r, c)}[kind]


def _slot_view(ref, kind, shard_shape, s, hf):
    _, r, c = shard_shape
    rh = r // 2
    if hf is None:
        start, size = 0, r
    else:
        start, size = hf * rh, rh
    if kind == "row":
        start = s * r + start
    if not isinstance(start, int):
        start = pl.multiple_of(start, HALF_ALIGN)
    rows = pl.ds(start, size)
    if kind == "row":
        return ref.at[:, rows, :]
    if kind == "col":
        return ref.at[:, rows, pl.ds(s * c, c)]
    return ref.at[s, :, rows, :]


def _half_view(ref, shard_shape, hf):
    rh = shard_shape[1] // 2
    return ref.at[:, pl.ds(pl.multiple_of(hf * rh, HALF_ALIGN), rh), :]


def _with_slot(kind, s, fn):
    if kind != "col" or isinstance(s, int):
        fn(s)
        return
    for k in range(N_CHIP):
        @pl.when(s == k)
        def _(k=k):
            fn(k)


def _mesh_pos():
    return lax.axis_index("x"), lax.axis_index("y"), lax.axis_index("c")


def _other_chips(x, y):
    return [(1 - x, y), (x, 1 - y), (1 - x, 1 - y)]


ICI_BYTES_PER_US = 8.0e4
D2D_BYTES_PER_US = 4.0e5


class _Chunk:
    def __init__(self, group, cost, ins, out_shapes, alias, n_sem, start, finish, done):
        self.group, self.cost, self.ins, self.out_shapes, self.alias, self.n_sem = group, cost, ins, out_shapes, alias, n_sem
        self.start, self.finish, self.done = start, finish, done


LAST_GROUP = 1 << 30


class _CommQueue:
    def __init__(self):
        self.pending = []
        self.flushes = 0

    def push(self, chunk):
        self.pending.append(chunk)

    def take(self, budget_us):
        got, used = [], 0.0
        for ch in sorted(self.pending, key=lambda ch: (-ch.cost, ch.group)):
            if used + ch.cost <= budget_us:
                got.append(ch)
                used += ch.cost
        self.pending = [ch for ch in self.pending if ch not in got]
        return got

    def flush(self, group=LAST_GROUP):
        while True:
            chunks = [ch for ch in self.pending if ch.group <= group]
            if not chunks:
                return
            self.pending = [ch for ch in self.pending if ch.group > group]
            _run_chunks(chunks, name=f"comm_flush{self.flushes}")
            self.flushes += 1


def _run_chunks(chunks, *, name):
    ins = [a for ch in chunks for a in ch.ins]
    outs = [s for ch in chunks for s in ch.out_shapes]
    alias, offs = {}, []
    i0 = o0 = s0 = 0
    for ch in chunks:
        offs.append((i0, o0, s0))
        for ci, co in ch.alias.items():
            alias[i0 + ci] = o0 + co
        i0 += len(ch.ins)
        o0 += len(ch.out_shapes)
        s0 += ch.n_sem

    def body(*refs):
        send_sems, recv_sems = refs[i0 + o0:]
        for phase in ("start", "finish"):
            for ch, (a, b, s) in zip(chunks, offs):
                getattr(ch, phase)(refs[a:a + len(ch.ins)], refs[i0 + b:i0 + b + len(ch.out_shapes)],
                                   send_sems, recv_sems, s)

    hbm = pl.BlockSpec(memory_space=pl.ANY)
    res = pl.pallas_call(
        body, in_specs=[hbm] * i0, out_specs=[hbm] * o0, out_shape=outs,
        scratch_shapes=[pltpu.SemaphoreType.DMA((s0,)), pltpu.SemaphoreType.DMA((s0,))],
        input_output_aliases=alias, name=name, compiler_params=pltpu.CompilerParams(has_side_effects=True))(*ins)
    for ch, (_, b, _) in zip(chunks, offs):
        ch.done(list(res[b:b + len(ch.out_shapes)]))


def _remote(src, dst, send_sems, recv_sems, k, dev):
    return pltpu.make_async_remote_copy(src_ref=src, dst_ref=dst, send_sem=send_sems.at[k], recv_sem=recv_sems.at[k],
                                        device_id=dev, device_id_type=MESH_T)


def _gather_chunks(q, group, kind, shard, l, ready):
    _, r, c = shard.shape
    shp = (1, r, c)
    half_bytes = (r // 2) * c * shard.dtype.itemsize

    def any_half(full):
        return _slot_view(full, kind, shp, 0, 0)

    def start1(ins, outs, ss, rs, b):
        x, y, c_ = _mesh_pos()
        src, full = ins[0].at[pl.ds(l, 1)], outs[0]
        _with_slot(kind, 2 * x + y, lambda s: pltpu.make_async_copy(
            src, _slot_view(full, kind, shp, s, None), ss.at[b + N_CHIP - 1]).start())
        for j, (ox, oy) in enumerate(_other_chips(x, y)):
            _with_slot(kind, 2 * x + y, lambda s, j=j, ox=ox, oy=oy: _remote(
                _half_view(src, shp, c_), _slot_view(full, kind, shp, s, c_), ss, rs, b + j, (ox, oy, c_)).start())

    def finish1(ins, outs, ss, rs, b):
        x, y, c_ = _mesh_pos()
        h = any_half(outs[0])
        for j in range(N_CHIP - 1):
            _remote(h, h, ss, rs, b + j, (x, y, 1 - c_)).wait()
        pltpu.make_async_copy(ins[0].at[pl.ds(l, 1)], _slot_view(outs[0], kind, shp, 0, None), ss.at[b + N_CHIP - 1]).wait()

    def start2(ins, outs, ss, rs, b):
        x, y, c_ = _mesh_pos()
        for j, (ox, oy) in enumerate(_other_chips(x, y)):
            def forward(s, j=j):
                v = _slot_view(outs[0], kind, shp, s, c_)
                _remote(v, v, ss, rs, b + j, (x, y, 1 - c_)).start()
            _with_slot(kind, 2 * ox + oy, forward)

    def finish2(ins, outs, ss, rs, b):
        x, y, c_ = _mesh_pos()
        h = any_half(outs[0])
        for j in range(N_CHIP - 1):
            _remote(h, h, ss, rs, b + j, (x, y, 1 - c_)).wait()

    def done1(outs):
        q.push(_Chunk(group, 3 * half_bytes / D2D_BYTES_PER_US, outs,
                      [jax.ShapeDtypeStruct(outs[0].shape, outs[0].dtype)],
                      {0: 0}, N_CHIP - 1, start2, finish2, lambda o: ready(o[0])))

    q.push(_Chunk(group, 3 * half_bytes / ICI_BYTES_PER_US, [shard],
                  [jax.ShapeDtypeStruct(_full_shape(kind, shp), shard.dtype)], {}, N_CHIP, start1, finish1, done1))


def _reduce_scatter_chunks(q, kind, grad, shard_shape, pos, name, ready):
    _, r, c = shard_shape
    shp = (1, r, c)
    rh = r // 2

    def start1(ins, outs, ss, rs, b):
        x, y, c_ = _mesh_pos()
        for s in range(N_CHIP):
            _remote(_slot_view(ins[0], kind, shp, s, 1 - c_), outs[0].at[s], ss, rs, b + s, (x, y, 1 - c_)).start()

    def finish1(ins, outs, ss, rs, b):
        x, y, c_ = _mesh_pos()
        for s in range(N_CHIP):
            _remote(outs[0].at[s], outs[0].at[s], ss, rs, b + s, (x, y, 1 - c_)).wait()

    def start2(ins, outs, ss, rs, b):
        x, y, c_ = _mesh_pos()
        for j, (ox, oy) in enumerate(_other_chips(x, y)):
            _remote(ins[0].at[2 * ox + oy], outs[0].at[j], ss, rs, b + j, (ox, oy, c_)).start()

    def finish2(ins, outs, ss, rs, b):
        x, y, c_ = _mesh_pos()
        for j in range(N_CHIP - 1):
            _remote(outs[0].at[j], outs[0].at[j], ss, rs, b + j, (x, y, 1 - c_)).wait()

    def start3(ins, outs, ss, rs, b):
        x, y, c_ = _mesh_pos()
        v = _half_view(outs[0], shp, c_)
        _remote(v, v, ss, rs, b, (x, y, 1 - c_)).start()

    def finish3(ins, outs, ss, rs, b):
        x, y, c_ = _mesh_pos()
        v = _half_view(outs[0], shp, c_)
        _remote(v, v, ss, rs, b, (x, y, 1 - c_)).wait()

    def done2(pair, outs):
        half = _rs_chip_add(pair, outs[0], shp, pos, name=f"rs_chip_add_{name}")
        q.push(_Chunk(LAST_GROUP, rh * c * 4 / D2D_BYTES_PER_US, [half], [jax.ShapeDtypeStruct(half.shape, half.dtype)],
                      {0: 0}, 1, start3, finish3, lambda o: ready(o[0])))

    def done1(outs):
        pair, wire = _rs_pair_add(grad, outs[0], kind, shp, pos, name=f"rs_pair_add_{name}")
        q.push(_Chunk(LAST_GROUP, 3 * rh * c * wire.dtype.itemsize / ICI_BYTES_PER_US, [wire],
                      [jax.ShapeDtypeStruct((N_CHIP - 1, 1, rh, c), wire.dtype)], {}, N_CHIP - 1,
                      start2, finish2, functools.partial(done2, pair)))

    q.push(_Chunk(LAST_GROUP, N_CHIP * rh * c * 4 / D2D_BYTES_PER_US, [grad],
                  [jax.ShapeDtypeStruct((N_CHIP, 1, rh, c), F32)], {}, N_CHIP, start1, finish1, done1))


def _exchange8(vec, reduce, *, name):
    r = vec.shape[0]
    n_dev = 8

    def body(v_ref, o_ref, *rest):
        if reduce:
            buf, send_sems, recv_sems = rest
        else:
            buf = o_ref
            send_sems, recv_sems = rest
        x, y, c = _mesh_pos()
        me = 4 * x + 2 * y + c
        copies = []
        for k in range(1, n_dev):
            kx, ky, kc = (k >> 2) & 1, (k >> 1) & 1, k & 1
            peer = ((1 - x) if kx else x, (1 - y) if ky else y, (1 - c) if kc else c)
            cp = pltpu.make_async_remote_copy(src_ref=v_ref, dst_ref=buf.at[me], send_sem=send_sems.at[k - 1],
                                              recv_sem=recv_sems.at[k - 1], device_id=peer, device_id_type=MESH_T)
            cp.start()
            copies.append(cp)
        buf[me] = v_ref[...]
        for cp in copies:
            cp.wait()
        if reduce:
            acc = buf[0]
            for d in range(1, n_dev):
                acc = acc + buf[d]
            o_ref[...] = acc

    vm = pl.BlockSpec(memory_space=pltpu.VMEM)
    scratch = [pltpu.SemaphoreType.DMA((n_dev - 1,)), pltpu.SemaphoreType.DMA((n_dev - 1,))]
    if reduce:
        scratch = [pltpu.VMEM((n_dev, r, LANE), F32)] + scratch
        out_shape = jax.ShapeDtypeStruct((r, LANE), F32)
    else:
        out_shape = jax.ShapeDtypeStruct((n_dev, r, LANE), F32)
    return pl.pallas_call(
        body, in_specs=[vm], out_specs=vm, out_shape=out_shape, scratch_shapes=scratch,
        name=name, compiler_params=pltpu.CompilerParams(has_side_effects=True, vmem_limit_bytes=VMEM_LIMIT_V7X))(vec)


def _rs_pair_add(g, recv, kind, shape, pos, *, name):
    l, r, c = shape
    rh = r // 2
    if kind == "row":
        gspec = pl.BlockSpec((None, rh, c), lambda s, i, pos: (i, 2 * s + pos[0], 0))
    elif kind == "col":
        gspec = pl.BlockSpec((None, rh, c), lambda s, i, pos: (i, pos[0], s))
    else:
        gspec = pl.BlockSpec((None, None, rh, c), lambda s, i, pos: (s, i, pos[0], 0))
    pspec = pl.BlockSpec((None, None, rh, c), lambda s, i, pos: (s, i, 0, 0))

    def body(pos_ref, g_ref, r_ref, p_ref, pw_ref):
        del pos_ref
        v = g_ref[...] + r_ref[...]
        p_ref[...] = v
        pw_ref[...] = v.astype(pw_ref.dtype)

    return pl.pallas_call(
        body,
        grid_spec=pltpu.PrefetchScalarGridSpec(
            num_scalar_prefetch=1, grid=(N_CHIP, l), in_specs=[gspec, pspec], out_specs=[pspec, pspec]),
        out_shape=[jax.ShapeDtypeStruct((N_CHIP, l, rh, c), F32), jax.ShapeDtypeStruct((N_CHIP, l, rh, c), MXU)],
        name=name, compiler_params=_cp((PAR, PAR)))(pos, g, recv)


def _rs_chip_add(p, recv, shape, pos, *, name):
    l, r, c = shape
    rh = r // 2

    def body(pos_ref, p_ref, r_ref, o_ref):
        del pos_ref
        acc = p_ref[...]
        for j in range(N_CHIP - 1):
            acc = acc + r_ref[j].astype(F32)
        o_ref[...] = acc

    return pl.pallas_call(
        body,
        grid_spec=pltpu.PrefetchScalarGridSpec(
            num_scalar_prefetch=1, grid=(l,),
            in_specs=[pl.BlockSpec((None, None, rh, c), lambda i, pos: (pos[1], i, 0, 0)),
                      pl.BlockSpec((N_CHIP - 1, None, rh, c), lambda i, pos: (0, i, 0, 0))],
            out_specs=pl.BlockSpec((None, rh, c), lambda i, pos: (i, pos[0], 0))),
        out_shape=jax.ShapeDtypeStruct((l, r, c), F32),
        name=name, compiler_params=_cp((PAR,)))(pos, p, recv)


ADAM_BLOCK_ELEMS = 384 * 1024


def _adam_math(w, g, m, v):
    c1 = 1.0 / (1.0 - ADAM_B1 ** ADAM_STEP)
    c2 = 1.0 / (1.0 - ADAM_B2 ** ADAM_STEP)
    nm = ADAM_B1 * m + (1.0 - ADAM_B1) * g
    nv = ADAM_B2 * v + (1.0 - ADAM_B2) * (g * g)
    return -ADAM_LR * ((nm * c1) / (jnp.sqrt(nv * c2) + ADAM_EPS) + ADAM_WD * w), nm, nv


def _adamw_layer(w, g, m, v, outs, l, *, name):
    _, r, c = w.shape
    tr = _tile(r, max(SUBLANE, ADAM_BLOCK_ELEMS // c // SUBLANE * SUBLANE), SUBLANE)

    def body(w_ref, g_ref, m_ref, v_ref, *rest):
        go_ref, d_ref, nm_ref, nv_ref = rest[4:]
        gg = g_ref[...]
        go_ref[...] = gg
        d_ref[...], nm_ref[...], nv_ref[...] = _adam_math(w_ref[...], gg, m_ref[...], v_ref[...])

    lay = pl.BlockSpec((None, tr, c), lambda j: (l, j, 0))
    hbm = pl.BlockSpec(memory_space=pl.ANY)
    return pl.pallas_call(
        body, grid=(r // tr,),
        in_specs=[lay, pl.BlockSpec((None, tr, c), lambda j: (0, j, 0)), lay, lay] + [hbm] * 4,
        out_specs=[lay] * 4, out_shape=[jax.ShapeDtypeStruct(w.shape, F32)] * 4,
        input_output_aliases={4 + i: i for i in range(4)},
        name=name, compiler_params=_cp((PAR,)))(w, g, m, v, *outs)


def _adamw(w, g, m, v, *, name):
    shape = w.shape
    if w.ndim == 2:
        w, g, m, v = (a[None] for a in (w, g, m, v))
    l, r, c = w.shape
    tr = _tile(r, max(SUBLANE, ADAM_BLOCK_ELEMS // c // SUBLANE * SUBLANE), SUBLANE)

    def body(w_ref, g_ref, m_ref, v_ref, d_ref, nm_ref, nv_ref):
        d_ref[...], nm_ref[...], nv_ref[...] = _adam_math(w_ref[...], g_ref[...], m_ref[...], v_ref[...])

    spec = pl.BlockSpec((None, tr, c), lambda i, j: (i, j, 0))
    outs = pl.pallas_call(
        body, grid=(l, r // tr), in_specs=[spec] * 4, out_specs=[spec] * 3,
        out_shape=[jax.ShapeDtypeStruct((l, r, c), F32)] * 3,
        name=name, compiler_params=_cp((PAR, PAR)))(w, g, m, v)
    return tuple(o.reshape(shape) for o in outs)


PACK_ROWS = 512 * LANE


def _pack(arrays):
    flat = jnp.concatenate([a.reshape(-1).astype(F32) for a in arrays])
    pad = (-flat.shape[0]) % PACK_ROWS
    return jnp.pad(flat, (0, pad)).reshape(-1, LANE)


def _unpack(packed, shapes):
    flat = packed.reshape(-1)
    out, off = [], 0
    for s in shapes:
        size = int(np.prod(s))
        out.append(flat[off:off + size].reshape(s))
        off += size
    return out


BIG = (("w_mem_kv", "row"), ("w_mix_out", "row"), ("w_ffn_up", "col"), ("w_ffn_down", "row"),
       ("w_in_a", "slot"), ("w_in_b", "row"), ("w_kv", "row"))
SMALL_SHARDED = (("w_ffn_conv", 2), ("w_conv_a", 2), ("b_conv_a", 1), ("lru_lambda", 1))
SMALL_REPLICATED = ("g_mix_pre", "g_mix_post", "g_ffn_pre", "g_ffn_post", "g_mem", "b_ffn_conv",
                    "w_rg_r", "b_rg_r", "w_rg_i", "b_rg_i", "sinks_b", "g_kv")
WEIGHTS = ("g_mix_pre", "g_mix_post", "g_ffn_pre", "g_ffn_post", "g_mem", "w_mem_kv", "w_mix_out", "w_ffn_up",
           "w_ffn_conv", "b_ffn_conv", "w_ffn_down", "w_in_a", "w_conv_a", "b_conv_a", "w_rg_r", "b_rg_r", "w_rg_i",
           "b_rg_i", "lru_lambda", "w_in_b", "sinks_b", "g_kv", "w_kv")


def _slot_to_cols(a):
    s, l, r, c = a.shape
    return a.transpose(1, 2, 0, 3).reshape(l, r, s * c)


def _cols_to_slot(a):
    l, r, c4 = a.shape
    return a.reshape(l, r, N_CHIP, c4 // N_CHIP).transpose(2, 0, 1, 3)


def _layer_weights(layer):
    names = [("w_mem_kv", layer), ("w_in_a", layer) if layer < N_A else ("w_in_b", layer - N_A)]
    if layer == N_A:
        names.append(("w_kv", 0))
    return names + [("w_mix_out", layer), ("w_ffn_up", layer), ("w_ffn_down", layer)]


def _train_step(x, mem, target, w, m, v):
    xi, yi, ci = _mesh_pos()
    chip = 2 * xi + yi
    pos = jnp.stack([ci, chip]).astype(jnp.int32)

    q = _CommQueue()
    kinds = dict(BIG)
    as3 = lambda a: a if a.ndim == 3 else a[None]
    w3, m3, v3 = ({k: as3(d[k]) for k, _ in BIG} for d in (w, m, v))
    shards = {k: w3[k].astype(MXU) for k, _ in BIG}

    gathered = {}

    def on_gathered(k, l, full):
        gathered[k, l] = _slot_to_cols(full) if kinds[k] == "slot" else full

    group_of = {}

    def queue_gathers(layer):
        for k, l in _layer_weights(layer):
            group_of[k, l] = layer
            _gather_chunks(q, layer, kinds[k], shards[k], l, functools.partial(on_gathered, k, l))

    def wfull(k, l):
        if (k, l) not in gathered:
            q.flush(group_of[k, l])
        return gathered[k, l]

    queue_gathers(0)
    q.flush()
    for layer in range(1, DEPTH):
        queue_gathers(layer)

    big_out = {k: [lax.empty(w3[k].shape, F32) for _ in range(4)] for k, _ in BIG}

    def on_reduced(k, l, g):
        big_out[k] = _adamw_layer(w3[k], g, m3[k], v3[k], big_out[k], l, name=f"adamw_{k}{l}")

    def push_grad(k, l, g):
        if kinds[k] == "slot":
            g = _cols_to_slot(g)
        _reduce_scatter_chunks(q, kinds[k], g, (1,) + w3[k].shape[1:], pos, f"{k}{l}", functools.partial(on_reduced, k, l))

    small_shapes = [w[k].shape for k, _ in SMALL_SHARDED]
    stacked = _exchange8(_pack([w[k] for k, _ in SMALL_SHARDED]), False, name="gather_small")
    per_chip = [_unpack(stacked[2 * s], small_shapes) for s in range(N_CHIP)]
    p = {k: w[k] for k in SMALL_REPLICATED}
    for i, (k, axis) in enumerate(SMALL_SHARDED):
        p[k] = jnp.concatenate([per_chip[s][i] for s in range(N_CHIP)], axis=axis)

    sq, grad_x, g = _local_step(x, mem, target, p, wfull, push_grad, q)
    loss = lax.psum(0.5 * sq / D_MODEL, ("x", "y", "c"))
    q.flush()

    small_names = [k for k, _ in SMALL_SHARDED] + list(SMALL_REPLICATED)
    summed = _exchange8(_pack([g[k] for k in small_names]), True, name="allreduce_small")
    gsum = dict(zip(small_names, _unpack(summed, [p[k].shape for k in small_names])))
    for k, axis in SMALL_SHARDED:
        gsum[k] = lax.dynamic_slice_in_dim(gsum[k], chip * w[k].shape[axis], w[k].shape[axis], axis)

    delta, new_m, new_v = {}, {}, {}
    for k, _ in BIG:
        gsum[k], delta[k], new_m[k], new_v[k] = (o.reshape(w[k].shape) for o in big_out[k])
    packed = [_pack([d[k] for k in small_names]) for d in (w, gsum, m, v)]
    outs = _adamw(*packed, name="adamw_small")
    for d, o in zip((delta, new_m, new_v), outs):
        d.update(zip(small_names, _unpack(o, [w[k].shape for k in small_names])))
    return (loss, grad_x, *[gsum[k] for k in WEIGHTS], *[delta[k] for k in WEIGHTS],
            *[new_m[k] for k in WEIGHTS], *[new_v[k] for k in WEIGHTS])


def kernel(x, mem, g_mix_pre, g_mix_post, g_ffn_pre, g_ffn_post, g_mem, w_mem_kv, w_mix_out, w_ffn_up, w_ffn_conv, b_ffn_conv, w_ffn_down, w_in_a, w_conv_a, b_conv_a, w_rg_r, b_rg_r, w_rg_i, b_rg_i, lru_lambda, w_in_b, sinks_b, g_kv, w_kv, loss_target, m_g_mix_pre, m_g_mix_post, m_g_ffn_pre, m_g_ffn_post, m_g_mem, m_w_mem_kv, m_w_mix_out, m_w_ffn_up, m_w_ffn_conv, m_b_ffn_conv, m_w_ffn_down, m_w_in_a, m_w_conv_a, m_b_conv_a, m_w_rg_r, m_b_rg_r, m_w_rg_i, m_b_rg_i, m_lru_lambda, m_w_in_b, m_sinks_b, m_g_kv, m_w_kv, v_g_mix_pre, v_g_mix_post, v_g_ffn_pre, v_g_ffn_post, v_g_mem, v_w_mem_kv, v_w_mix_out, v_w_ffn_up, v_w_ffn_conv, v_b_ffn_conv, v_w_ffn_down, v_w_in_a, v_w_conv_a, v_b_conv_a, v_w_rg_r, v_b_rg_r, v_w_rg_i, v_b_rg_i, v_lru_lambda, v_w_in_b, v_sinks_b, v_g_kv, v_w_kv):
    args = (g_mix_pre, g_mix_post, g_ffn_pre, g_ffn_post, g_mem, w_mem_kv, w_mix_out, w_ffn_up, w_ffn_conv, b_ffn_conv, w_ffn_down, w_in_a, w_conv_a, b_conv_a, w_rg_r, b_rg_r, w_rg_i, b_rg_i, lru_lambda, w_in_b, sinks_b, g_kv, w_kv)
    ms = (m_g_mix_pre, m_g_mix_post, m_g_ffn_pre, m_g_ffn_post, m_g_mem, m_w_mem_kv, m_w_mix_out, m_w_ffn_up, m_w_ffn_conv, m_b_ffn_conv, m_w_ffn_down, m_w_in_a, m_w_conv_a, m_b_conv_a, m_w_rg_r, m_b_rg_r, m_w_rg_i, m_b_rg_i, m_lru_lambda, m_w_in_b, m_sinks_b, m_g_kv, m_w_kv)
    vs = (v_g_mix_pre, v_g_mix_post, v_g_ffn_pre, v_g_ffn_post, v_g_mem, v_w_mem_kv, v_w_mix_out, v_w_ffn_up, v_w_ffn_conv, v_b_ffn_conv, v_w_ffn_down, v_w_in_a, v_w_conv_a, v_b_conv_a, v_w_rg_r, v_b_rg_r, v_w_rg_i, v_b_rg_i, v_lru_lambda, v_w_in_b, v_sinks_b, v_g_kv, v_w_kv)
    return _train_step(x, mem, loss_target, dict(zip(WEIGHTS, args)), dict(zip(WEIGHTS, ms)), dict(zip(WEIGHTS, vs)))
```

```python
import functools
import math

import numpy as np
import jax
import jax.numpy as jnp
from jax import lax
from jax.experimental import pallas as pl
from jax.experimental.pallas import tpu as pltpu

F32 = jnp.float32
MXU = jnp.bfloat16

D_MODEL = 1024
HEAD_DIM = 64
MEM_LEN = 256
MEM_HEADS = 4
MEM_WIDTH = MEM_HEADS * HEAD_DIM
MIX_WIDTH = D_MODEL - MEM_WIDTH
LRU_BLOCKS = MIX_WIDTH // HEAD_DIM
LRU_CONV = 4
LRU_C = 8.0
SWA_HEADS = MIX_WIDTH // HEAD_DIM
SWA_KV_HEADS = 4
SWA_GROUP = SWA_HEADS // SWA_KV_HEADS
WINDOW = 128
D_FF = 2816
FFN_CONV = 3
EPS = 1e-6
DEPTH = 4
N_A = 2

ADAM_LR = 0.001
ADAM_B1 = 0.9
ADAM_B2 = 0.999
ADAM_EPS = 1e-08
ADAM_WD = 0.01
ADAM_STEP = 10

VMEM_LIMIT_V7X = 56 * 1024 * 1024
LANE = 128
SUBLANE = 8
GATE_TILE = 256
MESH_T = pl.DeviceIdType.MESH


def _alibi_slopes(n):
    def pow2_slopes(m):
        start = 2.0 ** (-8.0 / m)
        return [start ** (i + 1) for i in range(m)]
    c = 2 ** int(math.floor(math.log2(n)))
    s = pow2_slopes(c)
    if c != n:
        s = s + pow2_slopes(2 * c)[0::2][: n - c]
    return [float(np.float32(v)) for v in s]


SLOPES = _alibi_slopes(SWA_HEADS)


def _tile(n, cap, mult=LANE):
    best = None
    for t in range(mult, min(n, cap) + 1, mult):
        if n % t == 0:
            best = t
    return best if best is not None else n


def _cp(sem):
    return pltpu.CompilerParams(dimension_semantics=sem, vmem_limit_bytes=VMEM_LIMIT_V7X)


MM_VMEM_BUDGET = 40 * 1024 * 1024
HBM_BYTES_PER_US_V7X = 3.0e6
GRID_STEP_US = 0.35


def _divisors(n, mult):
    return [t for t in range(mult, n + 1, mult) if n % t == 0] or [n]


def _mm_tiles(m, k, n, out_bytes):
    best = None
    for tm in _divisors(m, 256):
        for tn in _divisors(n, LANE):
            vmem = 2 * (tm * k * 2 + k * tn * 2 + tm * tn * out_bytes)
            if vmem > MM_VMEM_BUDGET:
                continue
            steps = (m // tm) * (n // tn)
            b_reads = 1 if tn == n else m // tm
            traffic = m * k * 2 + k * n * 2 * b_reads + m * n * out_bytes
            first = tm * k * 2 + k * tn * 2
            cost = (traffic + first) / HBM_BYTES_PER_US_V7X + steps * GRID_STEP_US
            if best is None or cost < best[0]:
                best = (cost, tm, tn)
    return best[1], best[2]


def _mm_tn_tiles(k, m, n):
    best = None
    for tm in _divisors(m, LANE):
        for tn in _divisors(n, LANE):
            for tk in _divisors(k, 512):
                vmem = 2 * (tk * tm * 2 + tk * tn * 2 + tm * tn * 4)
                if vmem > MM_VMEM_BUDGET:
                    continue
                steps = (m // tm) * (n // tn) * (k // tk)
                traffic = k * m * 2 * (n // tn) + k * n * 2 * (m // tm) + m * n * 4
                cost = traffic / HBM_BYTES_PER_US_V7X + steps * GRID_STEP_US
                if best is None or cost < best[0]:
                    best = (cost, tk, tm, tn)
    return best[1], best[2], best[3]


ARB = "arbitrary"
PAR = "parallel"


def _rms_fwd(x, g):
    r = lax.rsqrt(jnp.mean(x * x, axis=-1, keepdims=True) + EPS)
    return x * r * g


def _rms_bwd(dy, x, g):
    r = lax.rsqrt(jnp.mean(x * x, axis=-1, keepdims=True) + EPS)
    xh = x * r
    gdy = dy * g
    dx = r * (gdy - xh * jnp.mean(gdy * xh, axis=-1, keepdims=True))
    dg = jnp.sum(dy * xh, axis=0, keepdims=True)
    return dx, dg


_GELU_K = math.sqrt(2.0 / math.pi)
_GELU_C = 0.044715


def _gelu(x):
    t = jnp.tanh(_GELU_K * (x + _GELU_C * x * x * x))
    return 0.5 * x * (1.0 + t)


def _gelu_and_grad(x):
    x2 = x * x
    t = jnp.tanh(_GELU_K * (x + _GELU_C * x2 * x))
    g = 0.5 * x * (1.0 + t)
    dg = 0.5 * (1.0 + t) + 0.5 * x * (1.0 - t * t) * (_GELU_K * (1.0 + 3.0 * _GELU_C * x2))
    return g, dg


def _shift_down(x, k, row):
    return jnp.where(row >= k, pltpu.roll(x, k, axis=0), 0.0)


def _shift_up(x, k, row):
    n = x.shape[0]
    return jnp.where(row < n - k, pltpu.roll(x, n - k, axis=0), 0.0)


def _shift_down_edge(x, k):
    r = pltpu.roll(x, k, axis=0)
    row = lax.broadcasted_iota(jnp.int32, (SUBLANE, x.shape[1]), 0)
    return jnp.concatenate([jnp.where(row >= k, r[:SUBLANE], 0.0), r[SUBLANE:]], axis=0)


def _shift_up_edge(x, k):
    n = x.shape[0]
    r = pltpu.roll(x, n - k, axis=0)
    row = lax.broadcasted_iota(jnp.int32, (SUBLANE, x.shape[1]), 0)
    return jnp.concatenate([r[:n - SUBLANE], jnp.where(row < SUBLANE - k, r[n - SUBLANE:], 0.0)], axis=0)


def _dot(a, b):
    return jnp.dot(a, b, preferred_element_type=F32)


def _dot_nt(a, b):
    return lax.dot_general(a, b, (((1,), (1,)), ((), ())), preferred_element_type=F32)


def _dot_tn(a, b):
    return lax.dot_general(a, b, (((0,), (0,)), ((), ())), preferred_element_type=F32)


MXU_FLOPS_PER_US = 7.0e8


def _hosted_call(body, *, grid, in_specs, out_specs, out_shape, args, name, aliases=None, q=None, flops=0.0):
    chunks = q.take(flops / MXU_FLOPS_PER_US) if q is not None else []
    if not chunks:
        return pl.pallas_call(
            body, grid=grid, in_specs=in_specs, out_specs=out_specs, out_shape=out_shape,
            input_output_aliases=aliases or {}, name=name,
            compiler_params=_cp((ARB,) * len(grid)))(*args)
    n_in = len(args)
    c_ins = [a for ch in chunks for a in ch.ins]
    c_outs = [s for ch in chunks for s in ch.out_shapes]
    alias = dict(aliases or {})
    in_off, out_off, sem_off = [], [], []
    i0 = o0 = s0 = 0
    for ch in chunks:
        in_off.append(i0)
        out_off.append(o0)
        sem_off.append(s0)
        for ci, co in ch.alias.items():
            alias[n_in + i0 + ci] = 1 + o0 + co
        i0 += len(ch.ins)
        o0 += len(ch.out_shapes)
        s0 += ch.n_sem

    def wrapped(*refs):
        ins = refs[:n_in]
        cin = refs[n_in:n_in + i0]
        o_ref = refs[n_in + i0]
        cout = refs[n_in + i0 + 1:n_in + i0 + 1 + o0]
        send_sems, recv_sems = refs[n_in + i0 + 1 + o0:]
        first = functools.reduce(lambda u, v: u & v, [pl.program_id(d) == 0 for d in range(len(grid))])
        last = functools.reduce(lambda u, v: u & v, [pl.program_id(d) == grid[d] - 1 for d in range(len(grid))])

        def each(phase):
            for ch, a, b, s in zip(chunks, in_off, out_off, sem_off):
                getattr(ch, phase)(cin[a:a + len(ch.ins)], cout[b:b + len(ch.out_shapes)], send_sems, recv_sems, s)

        pl.when(first)(lambda: each("start"))
        body(*ins, o_ref)
        pl.when(last)(lambda: each("finish"))

    hbm = pl.BlockSpec(memory_space=pl.ANY)
    res = pl.pallas_call(
        wrapped, grid=grid, in_specs=list(in_specs) + [hbm] * i0, out_specs=[out_specs] + [hbm] * o0,
        out_shape=[out_shape] + c_outs,
        scratch_shapes=[pltpu.SemaphoreType.DMA((s0,)), pltpu.SemaphoreType.DMA((s0,))],
        input_output_aliases=alias, name=name,
        compiler_params=pltpu.CompilerParams(dimension_semantics=(ARB,) * len(grid), vmem_limit_bytes=VMEM_LIMIT_V7X,
                                             has_side_effects=True))(*args, *c_ins)
    for ch, b in zip(chunks, out_off):
        ch.done(list(res[1 + b:1 + b + len(ch.out_shapes)]))
    return res[0]


def _mm_nn(a, b, *, name, q=None, out_dtype=F32):
    m, k = a.shape
    n = b.shape[-1]
    tm, tn = _mm_tiles(m, k, n, jnp.dtype(out_dtype).itemsize)

    def body(a_ref, b_ref, o_ref):
        o_ref[...] = _dot(a_ref[...], b_ref[...]).astype(o_ref.dtype)

    return _hosted_call(
        body, grid=(m // tm, n // tn),
        in_specs=[pl.BlockSpec((tm, k), lambda i, j: (i, 0)),
                  pl.BlockSpec((None, k, tn), lambda i, j: (0, 0, j))],
        out_specs=pl.BlockSpec((tm, tn), lambda i, j: (i, j)),
        out_shape=jax.ShapeDtypeStruct((m, n), out_dtype),
        args=(a, b), name=name, q=q, flops=2.0 * m * k * n)


def _mm_nt(a, b, *, name, q=None, out_dtype=F32):
    m, k = a.shape
    n = b.shape[-2]
    tm, tn = _mm_tiles(m, k, n, jnp.dtype(out_dtype).itemsize)

    def body(a_ref, b_ref, o_ref):
        o_ref[...] = _dot_nt(a_ref[...], b_ref[...]).astype(o_ref.dtype)

    return _hosted_call(
        body, grid=(m // tm, n // tn),
        in_specs=[pl.BlockSpec((tm, k), lambda i, j: (i, 0)),
                  pl.BlockSpec((None, tn, k), lambda i, j: (0, j, 0))],
        out_specs=pl.BlockSpec((tm, tn), lambda i, j: (i, j)),
        out_shape=jax.ShapeDtypeStruct((m, n), out_dtype),
        args=(a, b), name=name, q=q, flops=2.0 * m * k * n)


def _mm_tn(a, b, *, name, q=None, out=None, n_total=None, col_block_offset=0):
    k, m = a.shape
    n = b.shape[-1]
    tk, tm, tn = _mm_tn_tiles(k, m, n)
    off = col_block_offset * (n // tn)

    def body(a_ref, b_ref, *rest):
        o_ref = rest[-1]
        part = _dot_tn(a_ref[...], b_ref[...])

        @pl.when(pl.program_id(2) == 0)
        def _():
            o_ref[...] = part

        @pl.when(pl.program_id(2) > 0)
        def _():
            o_ref[...] += part

    in_specs = [pl.BlockSpec((tk, tm), lambda i, j, s: (s, i)), pl.BlockSpec((tk, tn), lambda i, j, s: (s, j))]
    args = (a, b)
    if out is not None:
        in_specs.append(pl.BlockSpec(memory_space=pl.ANY))
        args = (a, b, out)
    return _hosted_call(
        body, grid=(m // tm, n // tn, k // tk), in_specs=in_specs,
        out_specs=pl.BlockSpec((None, tm, tn), lambda i, j, s: (0, i, j + off)),
        out_shape=jax.ShapeDtypeStruct((1, m, n_total or n), F32),
        aliases={2: 0} if out is not None else None,
        args=args, name=name, q=q, flops=2.0 * m * k * n)


def _mm_ffn_dh(dg, dv, w_up, *, name, q=None):
    m, f = dg.shape
    d = w_up.shape[-2]
    tm, tn = _mm_tiles(m, 2 * f, d, 4)

    def body(dg_ref, dv_ref, wg_ref, wv_ref, o_ref):
        o_ref[...] = _dot_nt(dg_ref[...], wg_ref[...]) + _dot_nt(dv_ref[...], wv_ref[...])

    return _hosted_call(
        body, grid=(m // tm, d // tn),
        in_specs=[pl.BlockSpec((tm, f), lambda i, j: (i, 0)),
                  pl.BlockSpec((tm, f), lambda i, j: (i, 0)),
                  pl.BlockSpec((None, tn, f), lambda i, j: (0, j, 0)),
                  pl.BlockSpec((None, tn, f), lambda i, j: (0, j, 1))],
        out_specs=pl.BlockSpec((tm, tn), lambda i, j: (i, j)),
        out_shape=jax.ShapeDtypeStruct((m, d), F32),
        args=(dg, dv, w_up, w_up), name=name, q=q, flops=4.0 * m * f * d)


def _norm_fwd(x, g, *, name):
    n, d = x.shape
    tm = _tile(n, 256, SUBLANE)

    def body(x_ref, g_ref, o_ref):
        o_ref[...] = _rms_fwd(x_ref[...], g_ref[...]).astype(o_ref.dtype)

    return pl.pallas_call(
        body, grid=(n // tm,),
        in_specs=[pl.BlockSpec((tm, d), lambda i: (i, 0)), pl.BlockSpec((1, d), lambda i: (0, 0))],
        out_specs=pl.BlockSpec((tm, d), lambda i: (i, 0)),
        out_shape=jax.ShapeDtypeStruct((n, d), MXU),
        name=name, compiler_params=_cp((PAR,)))(x, g)


def _norm_bwd_dg(dy, x, g, *, name):
    n, d = x.shape
    tm = _tile(n, 256, SUBLANE)

    def body(dy_ref, x_ref, g_ref, dg_ref):
        @pl.when(pl.program_id(0) == 0)
        def _():
            dg_ref[...] = jnp.zeros_like(dg_ref)
        _, dg = _rms_bwd(dy_ref[...], x_ref[...], g_ref[...])
        dg_ref[...] += dg

    return pl.pallas_call(
        body, grid=(n // tm,),
        in_specs=[pl.BlockSpec((tm, d), lambda i: (i, 0)), pl.BlockSpec((tm, d), lambda i: (i, 0)),
                  pl.BlockSpec((1, d), lambda i: (0, 0))],
        out_specs=pl.BlockSpec((1, d), lambda i: (0, 0)),
        out_shape=jax.ShapeDtypeStruct((1, d), F32),
        name=name, compiler_params=_cp((ARB,)))(dy, x, g)


def _resid_norm_fwd(x, y, g_post, g_pres, *, name):
    n, d = x.shape
    tm = _tile(n, 256, SUBLANE)
    nh = len(g_pres)

    def body(x_ref, y_ref, gp_ref, *rest):
        gpre = rest[:nh]
        xo_ref = rest[nh]
        h_refs = rest[nh + 1:]
        xo = x_ref[...] + _rms_fwd(y_ref[...], gp_ref[...])
        xo_ref[...] = xo
        for g_ref, h_ref in zip(gpre, h_refs):
            h_ref[...] = _rms_fwd(xo, g_ref[...]).astype(h_ref.dtype)

    row = pl.BlockSpec((tm, d), lambda i: (i, 0))
    vec = pl.BlockSpec((1, d), lambda i: (0, 0))
    outs = pl.pallas_call(
        body, grid=(n // tm,),
        in_specs=[row, row, vec] + [vec] * nh,
        out_specs=[row] + [row] * nh,
        out_shape=[jax.ShapeDtypeStruct((n, d), F32)] + [jax.ShapeDtypeStruct((n, d), MXU)] * nh,
        name=name, compiler_params=_cp((PAR,)))(x, y, g_post, *g_pres)
    return outs[0], list(outs[1:])


def _loss_fwd(x, y, g_post, target, *, name):
    n, d = x.shape
    tm = _tile(n, 256, SUBLANE)

    def body(x_ref, y_ref, gp_ref, t_ref, dx_ref, sq_ref):
        @pl.when(pl.program_id(0) == 0)
        def _():
            sq_ref[...] = jnp.zeros_like(sq_ref)
        err = x_ref[...] + _rms_fwd(y_ref[...], gp_ref[...]) - t_ref[...]
        dx_ref[...] = err * (1.0 / d)
        sq_ref[...] += jnp.sum(err * err, axis=0, keepdims=True)

    row = pl.BlockSpec((tm, d), lambda i: (i, 0))
    vec = pl.BlockSpec((1, d), lambda i: (0, 0))
    return pl.pallas_call(
        body, grid=(n // tm,),
        in_specs=[row, row, vec, row],
        out_specs=[row, vec],
        out_shape=[jax.ShapeDtypeStruct((n, d), F32), jax.ShapeDtypeStruct((1, d), F32)],
        name=name, compiler_params=_cp((ARB,)))(x, y, g_post, target)


def _resid_norm_bwd(dx_out, dhs, x_out, g_pres, y, g_post, *, name):
    n, d = dx_out.shape
    tm = _tile(n, 256, SUBLANE)
    nh = len(dhs)
    has_y = y is not None

    def body(*refs):
        it = iter(refs)
        dxo_ref = next(it)
        dh_refs = [next(it) for _ in range(nh)]
        xo_ref = next(it) if nh else None
        gpre_refs = [next(it) for _ in range(nh)]
        y_ref = next(it) if has_y else None
        gpost_ref = next(it) if has_y else None
        g_out = next(it)
        dy_out = next(it) if has_y else None
        dgpre_out = [next(it) for _ in range(nh)]
        dgpost_out = next(it) if has_y else None

        @pl.when(pl.program_id(0) == 0)
        def _():
            for r in dgpre_out:
                r[...] = jnp.zeros_like(r)
            if has_y:
                dgpost_out[...] = jnp.zeros_like(dgpost_out)

        g = dxo_ref[...]
        if nh:
            xo = xo_ref[...]
            for dh_ref, gp_ref, dg_ref in zip(dh_refs, gpre_refs, dgpre_out):
                dx, dg = _rms_bwd(dh_ref[...], xo, gp_ref[...])
                g = g + dx
                dg_ref[...] += dg
        g_out[...] = g
        if has_y:
            dy, dg = _rms_bwd(g, y_ref[...], gpost_ref[...])
            dy_out[...] = dy.astype(dy_out.dtype)
            dgpost_out[...] += dg

    row = pl.BlockSpec((tm, d), lambda i: (i, 0))
    vec = pl.BlockSpec((1, d), lambda i: (0, 0))
    ins, in_specs = [dx_out], [row]
    ins += list(dhs)
    in_specs += [row] * nh
    if nh:
        ins.append(x_out)
        in_specs.append(row)
    ins += list(g_pres)
    in_specs += [vec] * nh
    if has_y:
        ins += [y, g_post]
        in_specs += [row, vec]
    out_specs, out_shape = [row], [jax.ShapeDtypeStruct((n, d), F32)]
    if has_y:
        out_specs.append(row)
        out_shape.append(jax.ShapeDtypeStruct((n, d), MXU))
    out_specs += [vec] * nh
    out_shape += [jax.ShapeDtypeStruct((1, d), F32)] * nh
    if has_y:
        out_specs.append(vec)
        out_shape.append(jax.ShapeDtypeStruct((1, d), F32))
    outs = list(pl.pallas_call(
        body, grid=(n // tm,), in_specs=in_specs, out_specs=out_specs, out_shape=out_shape,
        name=name, compiler_params=_cp((ARB,)))(*ins))
    g = outs.pop(0)
    dy = outs.pop(0) if has_y else None
    dgpre = [outs.pop(0) for _ in range(nh)]
    dgpost = outs.pop(0) if has_y else None
    return g, dy, dgpre, dgpost


def _ffn_conv(up, w_ref, b_ref):
    return (w_ref[0:1, :] * _shift_down_edge(up, 2) + w_ref[1:2, :] * _shift_down_edge(up, 1)
            + w_ref[2:3, :] * up + b_ref[...])


def _ffn_act_fwd(up, wconv, bconv, bsz, *, name):
    n, f2 = up.shape
    f = f2 // 2
    t = n // bsz
    tc = _tile(f, 256)
    nf = f // tc

    def body(ug_ref, uv_ref, wg_ref, wv_ref, bg_ref, bv_ref, o_ref, g_ref, v_ref):
        g = _ffn_conv(ug_ref[...], wg_ref, bg_ref)
        v = _ffn_conv(uv_ref[...], wv_ref, bv_ref)
        g_ref[...] = g
        v_ref[...] = v
        o_ref[...] = (_gelu(g) * v).astype(o_ref.dtype)

    blk = pl.BlockSpec((t, tc), lambda b, j: (b, j))
    return pl.pallas_call(
        body, grid=(bsz, nf),
        in_specs=[blk, pl.BlockSpec((t, tc), lambda b, j: (b, j + nf)),
                  pl.BlockSpec((FFN_CONV, tc), lambda b, j: (0, j)),
                  pl.BlockSpec((FFN_CONV, tc), lambda b, j: (0, j + nf)),
                  pl.BlockSpec((1, tc), lambda b, j: (0, j)),
                  pl.BlockSpec((1, tc), lambda b, j: (0, j + nf))],
        out_specs=[blk, blk, blk],
        out_shape=[jax.ShapeDtypeStruct((n, f), MXU), jax.ShapeDtypeStruct((n, f), F32),
                   jax.ShapeDtypeStruct((n, f), F32)],
        name=name, compiler_params=_cp((PAR, PAR)))(up, up, wconv, wconv, bconv, bconv)


def _ffn_act_bwd(up, ug, uv, dact, wconv, bsz, *, name):
    n, f2 = up.shape
    f = f2 // 2
    t = n // bsz
    tc = _tile(f, 256)
    nf = f // tc

    def body(xg_ref, xv_ref, g_ref, v_ref, da_ref, wg_ref, wv_ref,
             dug_ref, duv_ref, dwg_ref, dwv_ref, dbg_ref, dbv_ref):
        @pl.when(pl.program_id(1) == 0)
        def _():
            for r in (dwg_ref, dwv_ref, dbg_ref, dbv_ref):
                r[...] = jnp.zeros_like(r)

        gl, dgl = _gelu_and_grad(g_ref[...])
        da = da_ref[...]
        dg = da * v_ref[...] * dgl
        dv = da * gl

        def conv_bwd(du, w_ref, x_ref, dx_ref, dw_ref, db_ref):
            du1, du2 = _shift_up_edge(du, 1), _shift_up_edge(du, 2)
            dx_ref[...] = (w_ref[2:3, :] * du + w_ref[1:2, :] * du1 + w_ref[0:1, :] * du2).astype(dx_ref.dtype)
            x = x_ref[...]
            dw_ref[0:1, :] += jnp.sum(x * du2, axis=0, keepdims=True)
            dw_ref[1:2, :] += jnp.sum(x * du1, axis=0, keepdims=True)
            dw_ref[2:3, :] += jnp.sum(x * du, axis=0, keepdims=True)
            db_ref[...] += jnp.sum(du, axis=0, keepdims=True)

        conv_bwd(dg, wg_ref, xg_ref, dug_ref, dwg_ref, dbg_ref)
        conv_bwd(dv, wv_ref, xv_ref, duv_ref, dwv_ref, dbv_ref)

    blk = pl.BlockSpec((t, tc), lambda j, b: (b, j))
    wspec = pl.BlockSpec((FFN_CONV, tc), lambda j, b: (0, j))
    bspec = pl.BlockSpec((1, tc), lambda j, b: (0, j))
    outs = pl.pallas_call(
        body, grid=(nf, bsz),
        in_specs=[blk, pl.BlockSpec((t, tc), lambda j, b: (b, j + nf)), blk, blk, blk,
                  wspec, pl.BlockSpec((FFN_CONV, tc), lambda j, b: (0, j + nf))],
        out_specs=[blk, blk, wspec, wspec, bspec, bspec],
        out_shape=[jax.ShapeDtypeStruct((n, f), MXU), jax.ShapeDtypeStruct((n, f), MXU),
                   jax.ShapeDtypeStruct((FFN_CONV, f), F32), jax.ShapeDtypeStruct((FFN_CONV, f), F32),
                   jax.ShapeDtypeStruct((1, f), F32), jax.ShapeDtypeStruct((1, f), F32)],
        name=name, compiler_params=_cp((PAR, ARB)))(up, up, ug, uv, dact, wconv, wconv)
    dug, duv, dwg, dwv, dbg, dbv = outs
    return dug, duv, jnp.concatenate([dwg, dwv], axis=1), jnp.concatenate([dbg, dbv], axis=1)


def _mem_attn_fwd(proj, q_col_block, mkv, ycat, bsz, *, name):
    n = proj.shape[0]
    t = n // bsz
    tq = _tile(t, 512, SUBLANE)
    nt = t // tq
    scale = HEAD_DIM ** -0.5

    def body(q_ref, kv_ref, old_ref, o_ref):
        del old_ref
        outs = []
        for h in range(MEM_HEADS):
            sl = slice(h * HEAD_DIM, (h + 1) * HEAD_DIM)
            q = q_ref[:, sl].astype(MXU)
            k = kv_ref[:, sl].astype(MXU)
            v = kv_ref[:, MEM_WIDTH + h * HEAD_DIM: MEM_WIDTH + (h + 1) * HEAD_DIM].astype(MXU)
            s = _dot_nt(q, k) * scale
            m = jnp.max(s, axis=-1, keepdims=True)
            p = jnp.exp(s - m)
            p = p / jnp.sum(p, axis=-1, keepdims=True)
            outs.append(_dot(p.astype(MXU), v))
        o_ref[...] = jnp.concatenate(outs, axis=-1).astype(o_ref.dtype)

    return pl.pallas_call(
        body, grid=(bsz, nt),
        in_specs=[pl.BlockSpec((tq, MEM_WIDTH), lambda b, i: (b * nt + i, q_col_block)),
                  pl.BlockSpec((MEM_LEN, 2 * MEM_WIDTH), lambda b, i: (b, 0)),
                  pl.BlockSpec(memory_space=pl.ANY)],
        out_specs=pl.BlockSpec((tq, MEM_WIDTH), lambda b, i: (b * nt + i, MIX_WIDTH // MEM_WIDTH)),
        out_shape=jax.ShapeDtypeStruct(ycat.shape, ycat.dtype),
        input_output_aliases={2: 0},
        name=name, compiler_params=_cp((PAR, PAR)))(proj, mkv, ycat)


def _mem_attn_bwd(proj, q_col_block, mkv, dycat, dproj, bsz, *, name):
    n = proj.shape[0]
    t = n // bsz
    tq = _tile(t, 512, SUBLANE)
    nt = t // tq
    scale = HEAD_DIM ** -0.5

    def body(q_ref, kv_ref, do_ref, old_ref, dq_ref, dkv_ref):
        del old_ref

        @pl.when(pl.program_id(1) == 0)
        def _():
            dkv_ref[...] = jnp.zeros_like(dkv_ref)

        dqs, dks, dvs = [], [], []
        for h in range(MEM_HEADS):
            sl = slice(h * HEAD_DIM, (h + 1) * HEAD_DIM)
            q = q_ref[:, sl].astype(MXU)
            k = kv_ref[:, sl].astype(MXU)
            v = kv_ref[:, MEM_WIDTH + h * HEAD_DIM: MEM_WIDTH + (h + 1) * HEAD_DIM].astype(MXU)
            do = do_ref[:, sl].astype(MXU)
            s = _dot_nt(q, k) * scale
            m = jnp.max(s, axis=-1, keepdims=True)
            p = jnp.exp(s - m)
            p = p / jnp.sum(p, axis=-1, keepdims=True)
            dvs.append(_dot_tn(p.astype(MXU), do))
            dp = _dot_nt(do, v)
            ds = (p * (dp - jnp.sum(dp * p, axis=-1, keepdims=True)) * scale).astype(MXU)
            dqs.append(_dot(ds, k))
            dks.append(_dot_tn(ds, q))
        dq_ref[...] = jnp.concatenate(dqs, axis=-1).astype(dq_ref.dtype)
        dkv_ref[...] += jnp.concatenate(dks + dvs, axis=-1)

    return pl.pallas_call(
        body, grid=(bsz, nt),
        in_specs=[pl.BlockSpec((tq, MEM_WIDTH), lambda b, i: (b * nt + i, q_col_block)),
                  pl.BlockSpec((MEM_LEN, 2 * MEM_WIDTH), lambda b, i: (b, 0)),
                  pl.BlockSpec((tq, MEM_WIDTH), lambda b, i: (b * nt + i, MIX_WIDTH // MEM_WIDTH)),
                  pl.BlockSpec(memory_space=pl.ANY)],
        out_specs=[pl.BlockSpec((tq, MEM_WIDTH), lambda b, i: (b * nt + i, q_col_block)),
                   pl.BlockSpec((MEM_LEN, 2 * MEM_WIDTH), lambda b, i: (b, 0))],
        out_shape=[jax.ShapeDtypeStruct(dproj.shape, dproj.dtype),
                   jax.ShapeDtypeStruct((bsz * MEM_LEN, 2 * MEM_WIDTH), F32)],
        input_output_aliases={3: 0},
        name=name, compiler_params=_cp((PAR, ARB)))(proj, mkv, dycat, dproj)


def _swa_scores(q, k, h, dist, mask, sink):
    s = _dot_nt(q, k) * (HEAD_DIM ** -0.5)
    s = jnp.where(mask, s - SLOPES[h] * dist, -jnp.inf)
    m = jnp.maximum(jnp.max(s, axis=-1, keepdims=True), sink)
    p = jnp.exp(s - m)
    psink = jnp.exp(sink - m)
    inv = 1.0 / (jnp.sum(p, axis=-1, keepdims=True) + psink)
    return p * inv, psink * inv


def _swa_mask(n):
    qi = lax.broadcasted_iota(jnp.int32, (WINDOW, 2 * WINDOW), 0) + WINDOW
    ki = lax.broadcasted_iota(jnp.int32, (WINDOW, 2 * WINDOW), 1)
    dist = qi - ki
    mask = (dist >= 0) & (dist < WINDOW) & ((n > 0) | (ki >= WINDOW))
    return dist.astype(F32), mask


def _swa_fwd(proj, kv, sinks, bsz, *, name):
    n_tok = proj.shape[0]
    nb = n_tok // bsz // WINDOW
    kvw = SWA_KV_HEADS * HEAD_DIM

    def body(sink_ref, q_ref, kvp_ref, kvc_ref, o_ref):
        n = pl.program_id(1)
        dist, mask = _swa_mask(n)
        kk = jnp.concatenate([kvp_ref[:, :kvw], kvc_ref[:, :kvw]], axis=0).astype(MXU)
        vv = jnp.concatenate([kvp_ref[:, kvw:], kvc_ref[:, kvw:]], axis=0).astype(MXU)
        outs = []
        for h in range(SWA_HEADS):
            c = h // SWA_GROUP
            q = q_ref[:, h * HEAD_DIM:(h + 1) * HEAD_DIM].astype(MXU)
            p, _ = _swa_scores(q, kk[:, c * HEAD_DIM:(c + 1) * HEAD_DIM], h, dist, mask, sink_ref[h])
            outs.append(_dot(p.astype(MXU), vv[:, c * HEAD_DIM:(c + 1) * HEAD_DIM]))
        o_ref[...] = jnp.concatenate(outs, axis=-1).astype(o_ref.dtype)

    return pl.pallas_call(
        body, grid=(bsz, nb),
        in_specs=[pl.BlockSpec(memory_space=pltpu.SMEM),
                  pl.BlockSpec((WINDOW, MIX_WIDTH), lambda b, n: (b * nb + n, 0)),
                  pl.BlockSpec((WINDOW, 2 * kvw), lambda b, n: (b * nb + jnp.maximum(n - 1, 0), 0)),
                  pl.BlockSpec((WINDOW, 2 * kvw), lambda b, n: (b * nb + n, 0))],
        out_specs=pl.BlockSpec((WINDOW, MIX_WIDTH), lambda b, n: (b * nb + n, 0)),
        out_shape=jax.ShapeDtypeStruct((n_tok, D_MODEL), MXU),
        name=name, compiler_params=_cp((PAR, PAR)))(sinks, proj, kv, kv)


def _swa_bwd(proj, kv, sinks, dycat, bsz, *, name):
    n_tok = proj.shape[0]
    nb = n_tok // bsz // WINDOW
    kvw = SWA_KV_HEADS * HEAD_DIM

    def body(sink_ref, q_ref, kvp_ref, kvc_ref, do_ref, dq_ref, dkvc_ref, dkvp_ref, dsink_ref):
        n = pl.program_id(1)

        @pl.when((pl.program_id(0) == 0) & (n == 0))
        def _():
            dsink_ref[...] = jnp.zeros_like(dsink_ref)

        dist, mask = _swa_mask(n)
        kk = jnp.concatenate([kvp_ref[:, :kvw], kvc_ref[:, :kvw]], axis=0).astype(MXU)
        vv = jnp.concatenate([kvp_ref[:, kvw:], kvc_ref[:, kvw:]], axis=0).astype(MXU)
        lane = lax.broadcasted_iota(jnp.int32, (SUBLANE, LANE), 1)
        dqs = []
        dks = [None] * SWA_KV_HEADS
        dvs = [None] * SWA_KV_HEADS
        dsink = jnp.zeros((SUBLANE, LANE), F32)
        for h in range(SWA_HEADS):
            c = h // SWA_GROUP
            k = kk[:, c * HEAD_DIM:(c + 1) * HEAD_DIM]
            v = vv[:, c * HEAD_DIM:(c + 1) * HEAD_DIM]
            q = q_ref[:, h * HEAD_DIM:(h + 1) * HEAD_DIM].astype(MXU)
            do = do_ref[:, h * HEAD_DIM:(h + 1) * HEAD_DIM].astype(MXU)
            p, psink = _swa_scores(q, k, h, dist, mask, sink_ref[h])
            dv = _dot_tn(p.astype(MXU), do)
            dp = _dot_nt(do, v)
            rs = jnp.sum(dp * p, axis=-1, keepdims=True)
            ds = (p * (dp - rs) * (HEAD_DIM ** -0.5)).astype(MXU)
            dsink = dsink + jnp.where(lane == h, jnp.sum(-psink * rs, axis=0, keepdims=True), 0.0)
            dqs.append(_dot(ds, k))
            dk = _dot_tn(ds, q)
            dks[c] = dk if dks[c] is None else dks[c] + dk
            dvs[c] = dv if dvs[c] is None else dvs[c] + dv
        dq_ref[...] = jnp.concatenate(dqs, axis=-1).astype(dq_ref.dtype)
        dkv = jnp.concatenate(dks + dvs, axis=-1)
        dkvp_ref[...] = dkv[:WINDOW]
        dkvc_ref[...] = dkv[WINDOW:]
        dsink_ref[...] += dsink

    qspec = pl.BlockSpec((WINDOW, MIX_WIDTH), lambda b, n: (b * nb + n, 0))
    kvspec = pl.BlockSpec((WINDOW, 2 * kvw), lambda b, n: (b * nb + n, 0))
    return pl.pallas_call(
        body, grid=(bsz, nb),
        in_specs=[pl.BlockSpec(memory_space=pltpu.SMEM), qspec,
                  pl.BlockSpec((WINDOW, 2 * kvw), lambda b, n: (b * nb + jnp.maximum(n - 1, 0), 0)),
                  kvspec, qspec],
        out_specs=[qspec, kvspec, kvspec, pl.BlockSpec((SUBLANE, LANE), lambda b, n: (0, 0))],
        out_shape=[jax.ShapeDtypeStruct((n_tok, D_MODEL), MXU),
                   jax.ShapeDtypeStruct((n_tok, 2 * kvw), F32),
                   jax.ShapeDtypeStruct((n_tok, 2 * kvw), F32),
                   jax.ShapeDtypeStruct((SUBLANE, LANE), F32)],
        name=name, compiler_params=_cp((ARB, ARB)))(sinks, proj, kv, kv, dycat)


def _swa_dkv_combine(curs, prevs, bsz, *, name):
    n_tok, w = curs[0].shape
    nb = n_tok // bsz // WINDOW
    k = len(curs)

    def body(*refs):
        o_ref = refs[-1]
        n = pl.program_id(1)
        acc = refs[0][...]
        for r in refs[1:k]:
            acc = acc + r[...]
        nxt = refs[k][...]
        for r in refs[k + 1:2 * k]:
            nxt = nxt + r[...]
        o_ref[...] = (acc + jnp.where(n < nb - 1, nxt, 0.0)).astype(o_ref.dtype)

    cur = pl.BlockSpec((WINDOW, w), lambda b, n: (b * nb + n, 0))
    prv = pl.BlockSpec((WINDOW, w), lambda b, n: (b * nb + jnp.minimum(n + 1, nb - 1), 0))
    return pl.pallas_call(
        body, grid=(bsz, nb), in_specs=[cur] * k + [prv] * k, out_specs=cur,
        out_shape=jax.ShapeDtypeStruct((n_tok, w), MXU),
        name=name, compiler_params=_cp((PAR, PAR)))(*curs, *prevs)


def _lru_gates(ux, halo, ext_ref, wc_ref, bc_ref, wr_ref, br_ref, wi_ref, bi_ref, lam_ref):
    tt = ux.shape[0]
    ext_ref[0:SUBLANE, :] = halo
    ext_ref[SUBLANE:, :] = ux
    xs = [ux] + [ext_ref[pl.ds(SUBLANE - k, tt), :] for k in range(1, LRU_CONV)]
    xc = bc_ref[...] + wc_ref[3:4, :] * xs[0] + wc_ref[2:3, :] * xs[1] + wc_ref[1:2, :] * xs[2] + wc_ref[0:1, :] * xs[3]
    pre_r, pre_i = [], []
    for blk in range(MIX_WIDTH // GATE_TILE):
        xb = xc[:, blk * GATE_TILE:(blk + 1) * GATE_TILE].astype(MXU)
        pre_r.append(_dot(xb, wr_ref[blk]))
        pre_i.append(_dot(xb, wi_ref[blk]))
    r = jax.nn.sigmoid(jnp.concatenate(pre_r, axis=-1) + br_ref[...])
    i = jax.nn.sigmoid(jnp.concatenate(pre_i, axis=-1) + bi_ref[...])
    nlam = -lam_ref[...]
    sp = jnp.maximum(nlam, 0.0) + jnp.log(1.0 + jnp.exp(-jnp.abs(nlam)))
    log_a = -LRU_C * r * sp
    a = jnp.exp(log_a)
    om = -jnp.tanh(log_a) * (a * a + 1.0)
    s = jnp.sqrt(om)
    return xs, xc, r, i, sp, a, s


def _lru_fwd(proj, wconv, bconv, wr, br, wi, bi, lam, bsz, *, name):
    n_tok = proj.shape[0]
    t = n_tok // bsz
    tt = _tile(t, 256, SUBLANE)
    nt = t // tt
    w = MIX_WIDTH
    ng = tt // SUBLANE

    def body(pg_ref, halo_ref, wc_ref, bc_ref, wr_ref, br_ref, wi_ref, bi_ref, lam_ref,
             y_ref, h_ref, ext_ref, a_ref, b_ref, carry_ref):
        ti = pl.program_id(1)

        @pl.when(ti == 0)
        def _():
            carry_ref[...] = jnp.zeros_like(carry_ref)

        gate = pg_ref[:, :w]
        ux = pg_ref[:, w:]
        halo = jnp.where(ti > 0, halo_ref[...], 0.0)
        _, xc, _, i, _, a, s = _lru_gates(ux, halo, ext_ref, wc_ref, bc_ref, wr_ref, br_ref, wi_ref, bi_ref, lam_ref)
        a_ref[...] = a
        b_ref[...] = s * (i * xc)
        row = lax.broadcasted_iota(jnp.int32, (SUBLANE, w), 0)

        def group(g, hprev):
            off = pl.multiple_of(g * SUBLANE, SUBLANE)
            ca = a_ref[pl.ds(off, SUBLANE), :]
            cb = b_ref[pl.ds(off, SUBLANE), :]
            for d in (1, 2, 4):
                a_sh = jnp.where(row >= d, pltpu.roll(ca, d, axis=0), 1.0)
                b_sh = jnp.where(row >= d, pltpu.roll(cb, d, axis=0), 0.0)
                cb = ca * b_sh + cb
                ca = ca * a_sh
            h = ca * hprev + cb
            b_ref[pl.ds(off, SUBLANE), :] = h
            return jnp.broadcast_to(h[SUBLANE - 1:SUBLANE, :], (SUBLANE, w))

        carry_ref[...] = lax.fori_loop(0, ng, group, carry_ref[...])
        h = b_ref[...]
        h_ref[...] = h
        y_ref[...] = (h * _gelu(gate)).astype(y_ref.dtype)

    vec = lambda r: pl.BlockSpec((r, w), lambda b, i: (0, 0))
    wspec = pl.BlockSpec((w // GATE_TILE, GATE_TILE, GATE_TILE), lambda b, i: (0, 0, 0))
    hb = tt // SUBLANE
    return pl.pallas_call(
        body, grid=(bsz, nt),
        in_specs=[pl.BlockSpec((tt, 2 * w), lambda b, i: (b * nt + i, 0)),
                  pl.BlockSpec((SUBLANE, w), lambda b, i: (jnp.maximum((b * nt + i) * hb - 1, 0), 1)),
                  vec(LRU_CONV), vec(1), wspec, vec(1), wspec, vec(1), vec(1)],
        out_specs=[pl.BlockSpec((tt, w), lambda b, i: (b * nt + i, 0)),
                   pl.BlockSpec((tt, w), lambda b, i: (b * nt + i, 0))],
        out_shape=[jax.ShapeDtypeStruct((n_tok, D_MODEL), MXU), jax.ShapeDtypeStruct((n_tok, w), F32)],
        scratch_shapes=[pltpu.VMEM((tt + SUBLANE, w), F32), pltpu.VMEM((tt, w), F32),
                        pltpu.VMEM((tt, w), F32), pltpu.VMEM((SUBLANE, w), F32)],
        name=name, compiler_params=_cp((PAR, ARB)))(proj, proj, wconv, bconv, wr, br, wi, bi, lam)


def _lru_bwd(proj, hs, dycat, wconv, bconv, wr, br, wi, bi, lam, bsz, *, name):
    n_tok = proj.shape[0]
    t = n_tok // bsz
    tt = _tile(t, 256, SUBLANE)
    nt = t // tt
    w = MIX_WIDTH
    ng = tt // SUBLANE
    nblk = w // GATE_TILE

    def body(pg_ref, halo_ref, h_ref, hhalo_ref, dy_ref, wc_ref, bc_ref, wr_ref, br_ref, wi_ref, bi_ref, lam_ref,
             dp_ref, dwc_ref, dbc_ref, dwr_ref, dbr_ref, dwi_ref, dbi_ref, dlam_ref,
             ext_ref, a_ref, c_ref, g_ref, gcarry_ref, xcarry_ref):
        bi_ = pl.program_id(0)
        ti = nt - 1 - pl.program_id(1)

        @pl.when((bi_ == 0) & (pl.program_id(1) == 0))
        def _():
            for r in (dwc_ref, dbc_ref, dwr_ref, dbr_ref, dwi_ref, dbi_ref, dlam_ref):
                r[...] = jnp.zeros_like(r)

        @pl.when(pl.program_id(1) == 0)
        def _():
            gcarry_ref[...] = jnp.zeros_like(gcarry_ref)
            xcarry_ref[...] = jnp.zeros_like(xcarry_ref)

        gate = pg_ref[:, :w]
        ux = pg_ref[:, w:]
        halo = jnp.where(ti > 0, halo_ref[...], 0.0)
        xs, xc, r, i, sp, a, s = _lru_gates(ux, halo, ext_ref, wc_ref, bc_ref, wr_ref, br_ref, wi_ref, bi_ref, lam_ref)
        h = h_ref[...]
        gl, dgl = _gelu_and_grad(gate)
        dy = dy_ref[...]
        dgate = dy * h * dgl
        row_t = lax.broadcasted_iota(jnp.int32, (tt, w), 0)
        g_ref[...] = dy * gl + jnp.where(row_t == tt - 1, gcarry_ref[0:1, :], 0.0)
        c_ref[...] = _shift_up(a, 1, row_t)
        row = lax.broadcasted_iota(jnp.int32, (SUBLANE, w), 0)

        a_ref[...] = a

        def group(k, gnext):
            off = pl.multiple_of((ng - 1 - k) * SUBLANE, SUBLANE)
            cc = c_ref[pl.ds(off, SUBLANE), :]
            cb = g_ref[pl.ds(off, SUBLANE), :]
            cb = cb + jnp.where(row == SUBLANE - 1, gnext, 0.0)
            cc = jnp.where(row == SUBLANE - 1, 0.0, cc)
            for d in (1, 2, 4):
                c_sh = jnp.where(row < SUBLANE - d, pltpu.roll(cc, SUBLANE - d, axis=0), 1.0)
                b_sh = jnp.where(row < SUBLANE - d, pltpu.roll(cb, SUBLANE - d, axis=0), 0.0)
                cb = cc * b_sh + cb
                cc = cc * c_sh
            g_ref[pl.ds(off, SUBLANE), :] = cb
            a0 = a_ref[pl.ds(off, SUBLANE), :]
            return jnp.broadcast_to(a0[0:1, :] * cb[0:1, :], (SUBLANE, w))

        gc = lax.fori_loop(0, ng, group, jnp.zeros((SUBLANE, w), F32))
        gcarry_ref[...] = gc
        gsc = g_ref[...]

        hhalo = jnp.where(ti > 0, hhalo_ref[SUBLANE - 1:SUBLANE, :], 0.0)
        hprev = jnp.where(row_t == 0, hhalo, pltpu.roll(h, 1, axis=0))
        gated = i * xc
        d_gated = gsc * s
        d_atot = gsc * hprev - (gsc * gated) * a / s
        d_loga = d_atot * a
        d_r = d_loga * (-LRU_C) * sp
        dlam_ref[...] += jnp.sum(d_loga * r, axis=0, keepdims=True) * (LRU_C * jax.nn.sigmoid(-lam_ref[...]))
        d_i = d_gated * xc
        d_xc = d_gated * i
        d_pr = d_r * r * (1.0 - r)
        d_pi = d_i * i * (1.0 - i)
        dbr_ref[...] += jnp.sum(d_pr, axis=0, keepdims=True)
        dbi_ref[...] += jnp.sum(d_pi, axis=0, keepdims=True)
        extra = []
        for blk in range(nblk):
            sl = slice(blk * GATE_TILE, (blk + 1) * GATE_TILE)
            xb = xc[:, sl].astype(MXU)
            dr_b = d_pr[:, sl].astype(MXU)
            di_b = d_pi[:, sl].astype(MXU)
            dwr_ref[blk] += _dot_tn(xb, dr_b)
            dwi_ref[blk] += _dot_tn(xb, di_b)
            extra.append(_dot_nt(dr_b, wr_ref[blk]) + _dot_nt(di_b, wi_ref[blk]))
        d_xc = d_xc + jnp.concatenate(extra, axis=-1)
        dbc_ref[...] += jnp.sum(d_xc, axis=0, keepdims=True)
        for k in range(LRU_CONV):
            dwc_ref[k:k + 1, :] += jnp.sum(d_xc * xs[LRU_CONV - 1 - k], axis=0, keepdims=True)
        ext_ref[0:tt, :] = d_xc
        ext_ref[tt:, :] = xcarry_ref[...]
        dux = wc_ref[3:4, :] * d_xc
        for k in range(LRU_CONV - 1):
            dux = dux + wc_ref[k:k + 1, :] * ext_ref[pl.ds(LRU_CONV - 1 - k, tt), :]
        xcarry_ref[...] = d_xc[0:SUBLANE, :]
        dp_ref[:, :w] = dgate.astype(dp_ref.dtype)
        dp_ref[:, w:] = dux.astype(dp_ref.dtype)

    vec = lambda r: pl.BlockSpec((r, w), lambda b, i: (0, 0))
    wspec = pl.BlockSpec((nblk, GATE_TILE, GATE_TILE), lambda b, i: (0, 0, 0))
    hb = tt // SUBLANE
    rblk = lambda b, i: b * nt + (nt - 1 - i)
    halo_idx = lambda b, i: jnp.maximum(rblk(b, i) * hb - 1, 0)
    wide = pl.BlockSpec((tt, 2 * w), lambda b, i: (rblk(b, i), 0))
    narrow = pl.BlockSpec((tt, w), lambda b, i: (rblk(b, i), 0))
    return pl.pallas_call(
        body, grid=(bsz, nt),
        in_specs=[wide, pl.BlockSpec((SUBLANE, w), lambda b, i: (halo_idx(b, i), 1)),
                  narrow, pl.BlockSpec((SUBLANE, w), lambda b, i: (halo_idx(b, i), 0)), narrow,
                  vec(LRU_CONV), vec(1), wspec, vec(1), wspec, vec(1), vec(1)],
        out_specs=[wide, vec(LRU_CONV), vec(1), wspec, vec(1), wspec, vec(1), vec(1)],
        out_shape=[jax.ShapeDtypeStruct((n_tok, 2 * w + MEM_WIDTH), MXU),
                   jax.ShapeDtypeStruct((LRU_CONV, w), F32), jax.ShapeDtypeStruct((1, w), F32),
                   jax.ShapeDtypeStruct((nblk, GATE_TILE, GATE_TILE), F32), jax.ShapeDtypeStruct((1, w), F32),
                   jax.ShapeDtypeStruct((nblk, GATE_TILE, GATE_TILE), F32), jax.ShapeDtypeStruct((1, w), F32),
                   jax.ShapeDtypeStruct((1, w), F32)],
        scratch_shapes=[pltpu.VMEM((tt + SUBLANE, w), F32), pltpu.VMEM((tt, w), F32), pltpu.VMEM((tt, w), F32),
                        pltpu.VMEM((tt, w), F32), pltpu.VMEM((SUBLANE, w), F32), pltpu.VMEM((SUBLANE, w), F32)],
        name=name, compiler_params=_cp((ARB, ARB)))(proj, proj, hs, hs, dycat, wconv, bconv, wr, br, wi, bi, lam)


def _gate_tiles(w):
    per = GATE_TILE // HEAD_DIM
    w4 = w.reshape(LRU_BLOCKS // per, per, HEAD_DIM, HEAD_DIM)
    eye = jnp.eye(per, dtype=w.dtype)
    return jnp.einsum("bnij,nm->bnimj", w4, eye).reshape(LRU_BLOCKS // per, GATE_TILE, GATE_TILE)


def _gate_blocks(t):
    per = GATE_TILE // HEAD_DIM
    t5 = t.reshape(LRU_BLOCKS // per, per, HEAD_DIM, per, HEAD_DIM)
    eye = jnp.eye(per, dtype=t.dtype)
    return jnp.einsum("bnimj,nm->bnij", t5, eye).reshape(LRU_BLOCKS, HEAD_DIM, HEAD_DIM)


def _row(v):
    return v.reshape(1, -1)


def _local_step(x, mem, target, p, wfull, push_grad, q):
    bsz, t, d = x.shape
    n = bsz * t
    x2d = x.reshape(n, d)
    tgt = target.reshape(n, d)
    mem2d = mem.reshape(bsz * MEM_LEN, d)
    wr_t = [_gate_tiles(p["w_rg_r"][j]).astype(MXU) for j in range(N_A)]
    wi_t = [_gate_tiles(p["w_rg_i"][j]).astype(MXU) for j in range(N_A)]

    mn = [_norm_fwd(mem2d, _row(p["g_mem"][l]), name=f"mem_norm{l}") for l in range(DEPTH)]
    mkv = [None] * DEPTH
    h = _norm_fwd(x2d, _row(p["g_mix_pre"][0]), name="in_norm")
    xin = x2d
    sv = []
    kv = hkv = None
    for l in range(DEPTH):
        s = {"xin": xin, "h": h}
        mkv[l] = _mm_nn(mn[l], wfull("w_mem_kv", l), name=f"mem_kv{l}", q=q)
        if l < N_A:
            proj = _mm_nn(h, wfull("w_in_a", l), name=f"in_proj{l}", q=q)
            ycat, hs = _lru_fwd(proj, p["w_conv_a"][l], _row(p["b_conv_a"][l]), wr_t[l], _row(p["b_rg_r"][l]),
                                wi_t[l], _row(p["b_rg_i"][l]), _row(p["lru_lambda"][l]), bsz, name=f"lru_fwd{l}")
            s["hs"] = hs
            qblk = 2 * MIX_WIDTH // MEM_WIDTH
        else:
            if l == N_A:
                kv = _mm_nn(hkv, wfull("w_kv", 0), name="kv_proj", q=q)
            proj = _mm_nn(h, wfull("w_in_b", l - N_A), name=f"in_proj{l}", q=q)
            ycat = _swa_fwd(proj, kv, p["sinks_b"][l - N_A], bsz, name=f"swa_fwd{l}")
            qblk = MIX_WIDTH // MEM_WIDTH
        ycat = _mem_attn_fwd(proj, qblk, mkv[l], ycat, bsz, name=f"mem_attn_fwd{l}")
        y = _mm_nn(ycat, wfull("w_mix_out", l), name=f"mix_out{l}", q=q)
        x1, (h2,) = _resid_norm_fwd(xin, y, _row(p["g_mix_post"][l]), [_row(p["g_ffn_pre"][l])], name=f"mix_resid{l}")
        up = _mm_nn(h2, wfull("w_ffn_up", l), name=f"ffn_up{l}", q=q)
        act, ug, uv = _ffn_act_fwd(up, p["w_ffn_conv"][l], _row(p["b_ffn_conv"][l]), bsz, name=f"ffn_act{l}")
        f = _mm_nn(act, wfull("w_ffn_down", l), name=f"ffn_down{l}", q=q)
        s.update(proj=proj, qblk=qblk, ycat=ycat, y=y, x1=x1, h2=h2, up=up, ug=ug, uv=uv, act=act, f=f)
        sv.append(s)
        if l < DEPTH - 1:
            g_pres = [_row(p["g_mix_pre"][l + 1])] + ([_row(p["g_kv"])] if l + 1 == N_A else [])
            xin, hn = _resid_norm_fwd(x1, f, _row(p["g_ffn_post"][l]), g_pres, name=f"ffn_resid{l}")
            h = hn[0]
            if l + 1 == N_A:
                hkv = hn[1]
        else:
            g_tot, sq = _loss_fwd(x1, f, _row(p["g_ffn_post"][l]), tgt, name="loss")

    gs = {k: [None] * DEPTH for k in ("g_mix_pre", "g_mix_post", "g_ffn_pre", "g_ffn_post", "g_mem",
                                       "w_ffn_conv", "b_ffn_conv")}
    ga = {k: [None] * N_A for k in ("w_conv_a", "b_conv_a", "w_rg_r", "b_rg_r", "w_rg_i", "b_rg_i", "lru_lambda")}
    gsink = [None] * (DEPTH - N_A)
    dkv_cur, dkv_prev = [], []
    g_tot, df, _, gs["g_ffn_post"][DEPTH - 1] = _resid_norm_bwd(
        g_tot, [], None, [], sv[-1]["f"], _row(p["g_ffn_post"][DEPTH - 1]), name="loss_bwd")
    grad_x = None
    for l in reversed(range(DEPTH)):
        s = sv[l]
        dact = _mm_nt(df, wfull("w_ffn_down", l), name=f"d_act{l}", q=q)
        push_grad("w_ffn_down", l, _mm_tn(s["act"], df, name=f"dw_down{l}", q=q))
        dug, duv, gs["w_ffn_conv"][l], gs["b_ffn_conv"][l] = _ffn_act_bwd(
            s["up"], s["ug"], s["uv"], dact, p["w_ffn_conv"][l], bsz, name=f"ffn_act_bwd{l}")
        dh2 = _mm_ffn_dh(dug, duv, wfull("w_ffn_up", l), name=f"d_h2_{l}", q=q)
        dwu = _mm_tn(s["h2"], dug, name=f"dw_up_g{l}", q=q, n_total=2 * D_FF)
        push_grad("w_ffn_up", l, _mm_tn(s["h2"], duv, name=f"dw_up_v{l}", q=q, out=dwu, n_total=2 * D_FF,
                                        col_block_offset=1))
        g1, dy, (gs["g_ffn_pre"][l],), gs["g_mix_post"][l] = _resid_norm_bwd(
            g_tot, [dh2], s["x1"], [_row(p["g_ffn_pre"][l])], s["y"], _row(p["g_mix_post"][l]), name=f"mix_resid_bwd{l}")
        dycat = _mm_nt(dy, wfull("w_mix_out", l), name=f"d_ycat{l}", q=q)
        push_grad("w_mix_out", l, _mm_tn(s["ycat"], dy, name=f"dw_mix_out{l}", q=q))
        if l < N_A:
            dproj, dwc, dbc, dwr, dbr, dwi, dbi, dlam = _lru_bwd(
                s["proj"], s["hs"], dycat, p["w_conv_a"][l], _row(p["b_conv_a"][l]), wr_t[l], _row(p["b_rg_r"][l]),
                wi_t[l], _row(p["b_rg_i"][l]), _row(p["lru_lambda"][l]), bsz, name=f"lru_bwd{l}")
            ga["w_conv_a"][l], ga["b_conv_a"][l], ga["lru_lambda"][l] = dwc, dbc[0], dlam[0]
            ga["w_rg_r"][l], ga["w_rg_i"][l] = _gate_blocks(dwr), _gate_blocks(dwi)
            ga["b_rg_r"][l] = dbr.reshape(LRU_BLOCKS, HEAD_DIM)
            ga["b_rg_i"][l] = dbi.reshape(LRU_BLOCKS, HEAD_DIM)
            w_in, j = "w_in_a", l
        else:
            dproj, dc, dp_, dsk = _swa_bwd(s["proj"], kv, p["sinks_b"][l - N_A], dycat, bsz, name=f"swa_bwd{l}")
            dkv_cur.append(dc)
            dkv_prev.append(dp_)
            gsink[l - N_A] = dsk[0, :SWA_HEADS]
            w_in, j = "w_in_b", l - N_A
        dproj, dmkv = _mem_attn_bwd(s["proj"], s["qblk"], mkv[l], dycat, dproj, bsz, name=f"mem_attn_bwd{l}")
        dh = _mm_nt(dproj, wfull(w_in, j), name=f"d_h{l}", q=q)
        push_grad(w_in, j, _mm_tn(s["h"], dproj, name=f"dw_in{l}", q=q))
        dmkv = dmkv.astype(MXU)
        dmn = _mm_nt(dmkv, wfull("w_mem_kv", l), name=f"d_mem_norm{l}", q=q)
        push_grad("w_mem_kv", l, _mm_tn(mn[l], dmkv, name=f"dw_mem_kv{l}", q=q))
        gs["g_mem"][l] = _norm_bwd_dg(dmn, mem2d, _row(p["g_mem"][l]), name=f"mem_norm_bwd{l}")
        dhs, g_pres = [dh], [_row(p["g_mix_pre"][l])]
        if l == N_A:
            dkv = _swa_dkv_combine(dkv_cur, dkv_prev, bsz, name="dkv_combine")
            dhs.append(_mm_nt(dkv, wfull("w_kv", 0), name="d_hkv", q=q))
            g_pres.append(_row(p["g_kv"]))
            push_grad("w_kv", 0, _mm_tn(hkv, dkv, name="dw_kv", q=q))
        if l > 0:
            g_tot, df, dgpre, gs["g_ffn_post"][l - 1] = _resid_norm_bwd(
                g1, dhs, s["xin"], g_pres, sv[l - 1]["f"], _row(p["g_ffn_post"][l - 1]), name=f"ffn_resid_bwd{l - 1}")
        else:
            grad_x, _, dgpre, _ = _resid_norm_bwd(g1, dhs, s["xin"], g_pres, None, None, name="in_norm_bwd")
        gs["g_mix_pre"][l] = dgpre[0]
        if l == N_A:
            g_kv = dgpre[1][0]

    grads = {}
    for k in ("g_mix_pre", "g_mix_post", "g_ffn_pre", "g_ffn_post", "g_mem", "b_ffn_conv"):
        grads[k] = jnp.concatenate(gs[k], axis=0)
    grads["w_ffn_conv"] = jnp.stack(gs["w_ffn_conv"])
    for k, v in ga.items():
        grads[k] = jnp.stack(v)
    grads["sinks_b"] = jnp.stack(gsink)
    grads["g_kv"] = g_kv
    return jnp.sum(sq), grad_x.reshape(bsz, t, d), grads


N_CHIP = 4
HALF_ALIGN = 16


def _full_shape(kind, shard_shape):
    l, r, c = shard_shape
    return {"row": (l, N_CHIP * r, c), "col": (l, r, N_CHIP * c), "slot": (N_CHIP, l, r, c)}[kind]


def _slot_view(ref, kind, shard_shape, s, hf):
    _, r, c = shard_shape
    rh = r // 2
    if hf is None:
        start, size = 0, r
    else:
        start, size = hf * rh, rh
    if kind == "row":
        start = s * r + start
    if not isinstance(start, int):
        start = pl.multiple_of(start, HALF_ALIGN)
    rows = pl.ds(start, size)
    if kind == "row":
        return ref.at[:, rows, :]
    if kind == "col":
        return ref.at[:, rows, pl.ds(s * c, c)]
    return ref.at[s, :, rows, :]


def _half_view(ref, shard_shape, hf):
    rh = shard_shape[1] // 2
    return ref.at[:, pl.ds(pl.multiple_of(hf * rh, HALF_ALIGN), rh), :]


def _with_slot(kind, s, fn):
    if kind != "col" or isinstance(s, int):
        fn(s)
        return
    for k in range(N_CHIP):
        @pl.when(s == k)
        def _(k=k):
            fn(k)


def _mesh_pos():
    return lax.axis_index("x"), lax.axis_index("y"), lax.axis_index("c")


def _other_chips(x, y):
    return [(1 - x, y), (x, 1 - y), (1 - x, 1 - y)]


ICI_BYTES_PER_US = 8.0e4
D2D_BYTES_PER_US = 4.0e5


class _Chunk:
    def __init__(self, group, cost, ins, out_shapes, alias, n_sem, start, finish, done):
        self.group, self.cost, self.ins, self.out_shapes, self.alias, self.n_sem = group, cost, ins, out_shapes, alias, n_sem
        self.start, self.finish, self.done = start, finish, done


LAST_GROUP = 1 << 30


class _CommQueue:
    def __init__(self):
        self.pending = []
        self.flushes = 0

    def push(self, chunk):
        self.pending.append(chunk)

    def take(self, budget_us):
        got, used = [], 0.0
        for ch in sorted(self.pending, key=lambda ch: (-ch.cost, ch.group)):
            if used + ch.cost <= budget_us:
                got.append(ch)
                used += ch.cost
        self.pending = [ch for ch in self.pending if ch not in got]
        return got

    def flush(self, group=LAST_GROUP):
        while True:
            chunks = [ch for ch in self.pending if ch.group <= group]
            if not chunks:
                return
            self.pending = [ch for ch in self.pending if ch.group > group]
            _run_chunks(chunks, name=f"comm_flush{self.flushes}")
            self.flushes += 1


def _run_chunks(chunks, *, name):
    ins = [a for ch in chunks for a in ch.ins]
    outs = [s for ch in chunks for s in ch.out_shapes]
    alias, offs = {}, []
    i0 = o0 = s0 = 0
    for ch in chunks:
        offs.append((i0, o0, s0))
        for ci, co in ch.alias.items():
            alias[i0 + ci] = o0 + co
        i0 += len(ch.ins)
        o0 += len(ch.out_shapes)
        s0 += ch.n_sem

    def body(*refs):
        send_sems, recv_sems = refs[i0 + o0:]
        for phase in ("start", "finish"):
            for ch, (a, b, s) in zip(chunks, offs):
                getattr(ch, phase)(refs[a:a + len(ch.ins)], refs[i0 + b:i0 + b + len(ch.out_shapes)],
                                   send_sems, recv_sems, s)

    hbm = pl.BlockSpec(memory_space=pl.ANY)
    res = pl.pallas_call(
        body, in_specs=[hbm] * i0, out_specs=[hbm] * o0, out_shape=outs,
        scratch_shapes=[pltpu.SemaphoreType.DMA((s0,)), pltpu.SemaphoreType.DMA((s0,))],
        input_output_aliases=alias, name=name, compiler_params=pltpu.CompilerParams(has_side_effects=True))(*ins)
    for ch, (_, b, _) in zip(chunks, offs):
        ch.done(list(res[b:b + len(ch.out_shapes)]))


def _remote(src, dst, send_sems, recv_sems, k, dev):
    return pltpu.make_async_remote_copy(src_ref=src, dst_ref=dst, send_sem=send_sems.at[k], recv_sem=recv_sems.at[k],
                                        device_id=dev, device_id_type=MESH_T)


def _gather_chunks(q, group, kind, shard, l, ready):
    _, r, c = shard.shape
    shp = (1, r, c)
    half_bytes = (r // 2) * c * shard.dtype.itemsize

    def any_half(full):
        return _slot_view(full, kind, shp, 0, 0)

    def start1(ins, outs, ss, rs, b):
        x, y, c_ = _mesh_pos()
        src, full = ins[0].at[pl.ds(l, 1)], outs[0]
        _with_slot(kind, 2 * x + y, lambda s: pltpu.make_async_copy(
            src, _slot_view(full, kind, shp, s, None), ss.at[b + N_CHIP - 1]).start())
        for j, (ox, oy) in enumerate(_other_chips(x, y)):
            _with_slot(kind, 2 * x + y, lambda s, j=j, ox=ox, oy=oy: _remote(
                _half_view(src, shp, c_), _slot_view(full, kind, shp, s, c_), ss, rs, b + j, (ox, oy, c_)).start())

    def finish1(ins, outs, ss, rs, b):
        x, y, c_ = _mesh_pos()
        h = any_half(outs[0])
        for j in range(N_CHIP - 1):
            _remote(h, h, ss, rs, b + j, (x, y, 1 - c_)).wait()
        pltpu.make_async_copy(ins[0].at[pl.ds(l, 1)], _slot_view(outs[0], kind, shp, 0, None), ss.at[b + N_CHIP - 1]).wait()

    def start2(ins, outs, ss, rs, b):
        x, y, c_ = _mesh_pos()
        for j, (ox, oy) in enumerate(_other_chips(x, y)):
            def forward(s, j=j):
                v = _slot_view(outs[0], kind, shp, s, c_)
                _remote(v, v, ss, rs, b + j, (x, y, 1 - c_)).start()
            _with_slot(kind, 2 * ox + oy, forward)

    def finish2(ins, outs, ss, rs, b):
        x, y, c_ = _mesh_pos()
        h = any_half(outs[0])
        for j in range(N_CHIP - 1):
            _remote(h, h, ss, rs, b + j, (x, y, 1 - c_)).wait()

    def done1(outs):
        q.push(_Chunk(group, 3 * half_bytes / D2D_BYTES_PER_US, outs,
                      [jax.ShapeDtypeStruct(outs[0].shape, outs[0].dtype)],
                      {0: 0}, N_CHIP - 1, start2, finish2, lambda o: ready(o[0])))

    q.push(_Chunk(group, 3 * half_bytes / ICI_BYTES_PER_US, [shard],
                  [jax.ShapeDtypeStruct(_full_shape(kind, shp), shard.dtype)], {}, N_CHIP, start1, finish1, done1))


def _reduce_scatter_chunks(q, kind, grad, shard_shape, pos, name, ready):
    _, r, c = shard_shape
    shp = (1, r, c)
    rh = r // 2

    def start1(ins, outs, ss, rs, b):
        x, y, c_ = _mesh_pos()
        for s in range(N_CHIP):
            _remote(_slot_view(ins[0], kind, shp, s, 1 - c_), outs[0].at[s], ss, rs, b + s, (x, y, 1 - c_)).start()

    def finish1(ins, outs, ss, rs, b):
        x, y, c_ = _mesh_pos()
        for s in range(N_CHIP):
            _remote(outs[0].at[s], outs[0].at[s], ss, rs, b + s, (x, y, 1 - c_)).wait()

    def start2(ins, outs, ss, rs, b):
        x, y, c_ = _mesh_pos()
        for j, (ox, oy) in enumerate(_other_chips(x, y)):
            _remote(ins[0].at[2 * ox + oy], outs[0].at[j], ss, rs, b + j, (ox, oy, c_)).start()

    def finish2(ins, outs, ss, rs, b):
        x, y, c_ = _mesh_pos()
        for j in range(N_CHIP - 1):
            _remote(outs[0].at[j], outs[0].at[j], ss, rs, b + j, (x, y, 1 - c_)).wait()

    def start3(ins, outs, ss, rs, b):
        x, y, c_ = _mesh_pos()
        v = _half_view(outs[0], shp, c_)
        _remote(v, v, ss, rs, b, (x, y, 1 - c_)).start()

    def finish3(ins, outs, ss, rs, b):
        x, y, c_ = _mesh_pos()
        v = _half_view(outs[0], shp, c_)
        _remote(v, v, ss, rs, b, (x, y, 1 - c_)).wait()

    def done2(pair, outs):
        half = _rs_chip_add(pair, outs[0], shp, pos, name=f"rs_chip_add_{name}")
        q.push(_Chunk(LAST_GROUP, rh * c * 4 / D2D_BYTES_PER_US, [half], [jax.ShapeDtypeStruct(half.shape, half.dtype)],
                      {0: 0}, 1, start3, finish3, lambda o: ready(o[0])))

    def done1(outs):
        pair, wire = _rs_pair_add(grad, outs[0], kind, shp, pos, name=f"rs_pair_add_{name}")
        q.push(_Chunk(LAST_GROUP, 3 * rh * c * wire.dtype.itemsize / ICI_BYTES_PER_US, [wire],
                      [jax.ShapeDtypeStruct((N_CHIP - 1, 1, rh, c), wire.dtype)], {}, N_CHIP - 1,
                      start2, finish2, functools.partial(done2, pair)))

    q.push(_Chunk(LAST_GROUP, N_CHIP * rh * c * 4 / D2D_BYTES_PER_US, [grad],
                  [jax.ShapeDtypeStruct((N_CHIP, 1, rh, c), F32)], {}, N_CHIP, start1, finish1, done1))


def _allgather8(vec, *, name):
    r = vec.shape[0]
    n_dev = 8

    def body(v_ref, buf, send_sems, recv_sems):
        x, y, c = _mesh_pos()
        me = 4 * x + 2 * y + c
        copies = []
        for k in range(1, n_dev):
            kx, ky, kc = (k >> 2) & 1, (k >> 1) & 1, k & 1
            peer = ((1 - x) if kx else x, (1 - y) if ky else y, (1 - c) if kc else c)
            cp = _remote(v_ref, buf.at[me], send_sems, recv_sems, k - 1, peer)
            cp.start()
            copies.append(cp)
        buf[me] = v_ref[...]
        for cp in copies:
            cp.wait()

    vm = pl.BlockSpec(memory_space=pltpu.VMEM)
    return pl.pallas_call(
        body, in_specs=[vm], out_specs=vm, out_shape=jax.ShapeDtypeStruct((n_dev, r, LANE), F32),
        scratch_shapes=[pltpu.SemaphoreType.DMA((n_dev - 1,)), pltpu.SemaphoreType.DMA((n_dev - 1,))],
        name=name, compiler_params=pltpu.CompilerParams(has_side_effects=True, vmem_limit_bytes=VMEM_LIMIT_V7X))(vec)


def _allreduce8(vec, *, name):
    r = vec.shape[0]
    rh = r // 2

    def body(v_ref, o_ref, sib_ref, chips_ref, send_sems, recv_sems):
        x, y, c = _mesh_pos()
        sib = (x, y, 1 - c)
        me = 2 * x + y
        pair = _remote(v_ref, sib_ref, send_sems, recv_sems, 0, sib)
        pair.start()
        pair.wait()
        rows = pl.ds(pl.multiple_of(c * rh, SUBLANE), rh)
        chips_ref[me] = v_ref[rows, :] + sib_ref[rows, :]
        copies = []
        for j, (ox, oy) in enumerate(_other_chips(x, y)):
            cp = _remote(chips_ref.at[me], chips_ref.at[me], send_sems, recv_sems, 1 + j, (ox, oy, c))
            cp.start()
            copies.append(cp)
        for cp in copies:
            cp.wait()
        acc = chips_ref[0]
        for s in range(1, N_CHIP):
            acc = acc + chips_ref[s]
        o_ref[rows, :] = acc
        swap = _remote(o_ref.at[rows, :], o_ref.at[rows, :], send_sems, recv_sems, N_CHIP, sib)
        swap.start()
        swap.wait()

    vm = pl.BlockSpec(memory_space=pltpu.VMEM)
    return pl.pallas_call(
        body, in_specs=[vm], out_specs=vm, out_shape=jax.ShapeDtypeStruct((r, LANE), F32),
        scratch_shapes=[pltpu.VMEM((r, LANE), F32), pltpu.VMEM((N_CHIP, rh, LANE), F32),
                        pltpu.SemaphoreType.DMA((N_CHIP + 1,)), pltpu.SemaphoreType.DMA((N_CHIP + 1,))],
        name=name, compiler_params=pltpu.CompilerParams(has_side_effects=True, vmem_limit_bytes=VMEM_LIMIT_V7X))(vec)


def _rs_pair_add(g, recv, kind, shape, pos, *, name):
    l, r, c = shape
    rh = r // 2
    if kind == "row":
        gspec = pl.BlockSpec((None, rh, c), lambda s, i, pos: (i, 2 * s + pos[0], 0))
    elif kind == "col":
        gspec = pl.BlockSpec((None, rh, c), lambda s, i, pos: (i, pos[0], s))
    else:
        gspec = pl.BlockSpec((None, None, rh, c), lambda s, i, pos: (s, i, pos[0], 0))
    pspec = pl.BlockSpec((None, None, rh, c), lambda s, i, pos: (s, i, 0, 0))

    def body(pos_ref, g_ref, r_ref, p_ref, pw_ref):
        del pos_ref
        v = g_ref[...] + r_ref[...]
        p_ref[...] = v
        pw_ref[...] = v.astype(pw_ref.dtype)

    return pl.pallas_call(
        body,
        grid_spec=pltpu.PrefetchScalarGridSpec(
            num_scalar_prefetch=1, grid=(N_CHIP, l), in_specs=[gspec, pspec], out_specs=[pspec, pspec]),
        out_shape=[jax.ShapeDtypeStruct((N_CHIP, l, rh, c), F32), jax.ShapeDtypeStruct((N_CHIP, l, rh, c), MXU)],
        name=name, compiler_params=_cp((PAR, PAR)))(pos, g, recv)


def _rs_chip_add(p, recv, shape, pos, *, name):
    l, r, c = shape
    rh = r // 2

    def body(pos_ref, p_ref, r_ref, o_ref):
        del pos_ref
        acc = p_ref[...]
        for j in range(N_CHIP - 1):
            acc = acc + r_ref[j].astype(F32)
        o_ref[...] = acc

    return pl.pallas_call(
        body,
        grid_spec=pltpu.PrefetchScalarGridSpec(
            num_scalar_prefetch=1, grid=(l,),
            in_specs=[pl.BlockSpec((None, None, rh, c), lambda i, pos: (pos[1], i, 0, 0)),
                      pl.BlockSpec((N_CHIP - 1, None, rh, c), lambda i, pos: (0, i, 0, 0))],
            out_specs=pl.BlockSpec((None, rh, c), lambda i, pos: (i, pos[0], 0))),
        out_shape=jax.ShapeDtypeStruct((l, r, c), F32),
        name=name, compiler_params=_cp((PAR,)))(pos, p, recv)


ADAM_BLOCK_ELEMS = 384 * 1024


def _adam_math(w, g, m, v):
    c1 = 1.0 / (1.0 - ADAM_B1 ** ADAM_STEP)
    c2 = 1.0 / (1.0 - ADAM_B2 ** ADAM_STEP)
    nm = ADAM_B1 * m + (1.0 - ADAM_B1) * g
    nv = ADAM_B2 * v + (1.0 - ADAM_B2) * (g * g)
    return -ADAM_LR * ((nm * c1) / (jnp.sqrt(nv * c2) + ADAM_EPS) + ADAM_WD * w), nm, nv


def _adamw_layer(w, g, m, v, outs, l, *, name):
    _, r, c = w.shape
    tr = _tile(r, max(SUBLANE, ADAM_BLOCK_ELEMS // c // SUBLANE * SUBLANE), SUBLANE)

    def body(w_ref, g_ref, m_ref, v_ref, *rest):
        go_ref, d_ref, nm_ref, nv_ref = rest[4:]
        gg = g_ref[...]
        go_ref[...] = gg
        d_ref[...], nm_ref[...], nv_ref[...] = _adam_math(w_ref[...], gg, m_ref[...], v_ref[...])

    lay = pl.BlockSpec((None, tr, c), lambda j: (l, j, 0))
    hbm = pl.BlockSpec(memory_space=pl.ANY)
    return pl.pallas_call(
        body, grid=(r // tr,),
        in_specs=[lay, pl.BlockSpec((None, tr, c), lambda j: (0, j, 0)), lay, lay] + [hbm] * 4,
        out_specs=[lay] * 4, out_shape=[jax.ShapeDtypeStruct(w.shape, F32)] * 4,
        input_output_aliases={4 + i: i for i in range(4)},
        name=name, compiler_params=_cp((PAR,)))(w, g, m, v, *outs)


def _adamw(w, g, m, v, *, name):
    shape = w.shape
    if w.ndim == 2:
        w, g, m, v = (a[None] for a in (w, g, m, v))
    l, r, c = w.shape
    tr = _tile(r, max(SUBLANE, ADAM_BLOCK_ELEMS // c // SUBLANE * SUBLANE), SUBLANE)

    def body(w_ref, g_ref, m_ref, v_ref, d_ref, nm_ref, nv_ref):
        d_ref[...], nm_ref[...], nv_ref[...] = _adam_math(w_ref[...], g_ref[...], m_ref[...], v_ref[...])

    spec = pl.BlockSpec((None, tr, c), lambda i, j: (i, j, 0))
    outs = pl.pallas_call(
        body, grid=(l, r // tr), in_specs=[spec] * 4, out_specs=[spec] * 3,
        out_shape=[jax.ShapeDtypeStruct((l, r, c), F32)] * 3,
        name=name, compiler_params=_cp((PAR, PAR)))(w, g, m, v)
    return tuple(o.reshape(shape) for o in outs)


PACK_ROWS = 512 * LANE


def _pack(arrays):
    flat = jnp.concatenate([a.reshape(-1).astype(F32) for a in arrays])
    pad = (-flat.shape[0]) % PACK_ROWS
    return jnp.pad(flat, (0, pad)).reshape(-1, LANE)


def _unpack(packed, shapes):
    flat = packed.reshape(-1)
    out, off = [], 0
    for s in shapes:
        size = int(np.prod(s))
        out.append(flat[off:off + size].reshape(s))
        off += size
    return out


BIG = (("w_mem_kv", "row"), ("w_mix_out", "row"), ("w_ffn_up", "col"), ("w_ffn_down", "row"),
       ("w_in_a", "slot"), ("w_in_b", "row"), ("w_kv", "row"))
SMALL_SHARDED = (("w_ffn_conv", 2), ("w_conv_a", 2), ("b_conv_a", 1), ("lru_lambda", 1))
SMALL_REPLICATED = ("g_mix_pre", "g_mix_post", "g_ffn_pre", "g_ffn_post", "g_mem", "b_ffn_conv",
                    "w_rg_r", "b_rg_r", "w_rg_i", "b_rg_i", "sinks_b", "g_kv")
WEIGHTS = ("g_mix_pre", "g_mix_post", "g_ffn_pre", "g_ffn_post", "g_mem", "w_mem_kv", "w_mix_out", "w_ffn_up",
           "w_ffn_conv", "b_ffn_conv", "w_ffn_down", "w_in_a", "w_conv_a", "b_conv_a", "w_rg_r", "b_rg_r", "w_rg_i",
           "b_rg_i", "lru_lambda", "w_in_b", "sinks_b", "g_kv", "w_kv")


def _slot_to_cols(a):
    s, l, r, c = a.shape
    return a.transpose(1, 2, 0, 3).reshape(l, r, s * c)


def _cols_to_slot(a):
    l, r, c4 = a.shape
    return a.reshape(l, r, N_CHIP, c4 // N_CHIP).transpose(2, 0, 1, 3)


def _layer_weights(layer):
    names = [("w_mem_kv", layer), ("w_in_a", layer) if layer < N_A else ("w_in_b", layer - N_A)]
    if layer == N_A:
        names.append(("w_kv", 0))
    return names + [("w_mix_out", layer), ("w_ffn_up", layer), ("w_ffn_down", layer)]


def _train_step(x, mem, target, w, m, v):
    xi, yi, ci = _mesh_pos()
    chip = 2 * xi + yi
    pos = jnp.stack([ci, chip]).astype(jnp.int32)

    q = _CommQueue()
    kinds = dict(BIG)
    as3 = lambda a: a if a.ndim == 3 else a[None]
    w3, m3, v3 = ({k: as3(d[k]) for k, _ in BIG} for d in (w, m, v))
    shards = {k: w3[k].astype(MXU) for k, _ in BIG}

    gathered = {}

    def on_gathered(k, l, full):
        gathered[k, l] = _slot_to_cols(full) if kinds[k] == "slot" else full

    group_of = {}

    def queue_gathers(layer):
        for k, l in _layer_weights(layer):
            group_of[k, l] = layer
            _gather_chunks(q, layer, kinds[k], shards[k], l, functools.partial(on_gathered, k, l))

    def wfull(k, l):
        if (k, l) not in gathered:
            q.flush(group_of[k, l])
        return gathered[k, l]

    queue_gathers(0)
    q.flush()
    for layer in range(1, DEPTH):
        queue_gathers(layer)

    big_out = {k: [lax.empty(w3[k].shape, F32) for _ in range(4)] for k, _ in BIG}

    def on_reduced(k, l, g):
        big_out[k] = _adamw_layer(w3[k], g, m3[k], v3[k], big_out[k], l, name=f"adamw_{k}{l}")

    def push_grad(k, l, g):
        if kinds[k] == "slot":
            g = _cols_to_slot(g)
        _reduce_scatter_chunks(q, kinds[k], g, (1,) + w3[k].shape[1:], pos, f"{k}{l}", functools.partial(on_reduced, k, l))

    small_shapes = [w[k].shape for k, _ in SMALL_SHARDED]
    stacked = _allgather8(_pack([w[k] for k, _ in SMALL_SHARDED]), name="gather_small")
    per_chip = [_unpack(stacked[2 * s], small_shapes) for s in range(N_CHIP)]
    p = {k: w[k] for k in SMALL_REPLICATED}
    for i, (k, axis) in enumerate(SMALL_SHARDED):
        p[k] = jnp.concatenate([per_chip[s][i] for s in range(N_CHIP)], axis=axis)

    sq, grad_x, g = _local_step(x, mem, target, p, wfull, push_grad, q)
    loss = lax.psum(0.5 * sq / D_MODEL, ("x", "y", "c"))
    q.flush()

    small_names = [k for k, _ in SMALL_SHARDED] + list(SMALL_REPLICATED)
    summed = _allreduce8(_pack([g[k] for k in small_names]), name="allreduce_small")
    gsum = dict(zip(small_names, _unpack(summed, [p[k].shape for k in small_names])))
    for k, axis in SMALL_SHARDED:
        gsum[k] = lax.dynamic_slice_in_dim(gsum[k], chip * w[k].shape[axis], w[k].shape[axis], axis)

    delta, new_m, new_v = {}, {}, {}
    for k, _ in BIG:
        gsum[k], delta[k], new_m[k], new_v[k] = (o.reshape(w[k].shape) for o in big_out[k])
    packed = [_pack([d[k] for k in small_names]) for d in (w, gsum, m, v)]
    outs = _adamw(*packed, name="adamw_small")
    for d, o in zip((delta, new_m, new_v), outs):
        d.update(zip(small_names, _unpack(o, [w[k].shape for k in small_names])))
    return (loss, grad_x, *[gsum[k] for k in WEIGHTS], *[delta[k] for k in WEIGHTS],
            *[new_m[k] for k in WEIGHTS], *[new_v[k] for k in WEIGHTS])


def kernel(x, mem, g_mix_pre, g_mix_post, g_ffn_pre, g_ffn_post, g_mem, w_mem_kv, w_mix_out, w_ffn_up, w_ffn_conv, b_ffn_conv, w_ffn_down, w_in_a, w_conv_a, b_conv_a, w_rg_r, b_rg_r, w_rg_i, b_rg_i, lru_lambda, w_in_b, sinks_b, g_kv, w_kv, loss_target, m_g_mix_pre, m_g_mix_post, m_g_ffn_pre, m_g_ffn_post, m_g_mem, m_w_mem_kv, m_w_mix_out, m_w_ffn_up, m_w_ffn_conv, m_b_ffn_conv, m_w_ffn_down, m_w_in_a, m_w_conv_a, m_b_conv_a, m_w_rg_r, m_b_rg_r, m_w_rg_i, m_b_rg_i, m_lru_lambda, m_w_in_b, m_sinks_b, m_g_kv, m_w_kv, v_g_mix_pre, v_g_mix_post, v_g_ffn_pre, v_g_ffn_post, v_g_mem, v_w_mem_kv, v_w_mix_out, v_w_ffn_up, v_w_ffn_conv, v_b_ffn_conv, v_w_ffn_down, v_w_in_a, v_w_conv_a, v_b_conv_a, v_w_rg_r, v_b_rg_r, v_w_rg_i, v_b_rg_i, v_lru_lambda, v_w_in_b, v_sinks_b, v_g_kv, v_w_kv):
    args = (g_mix_pre, g_mix_post, g_ffn_pre, g_ffn_post, g_mem, w_mem_kv, w_mix_out, w_ffn_up, w_ffn_conv, b_ffn_conv, w_ffn_down, w_in_a, w_conv_a, b_conv_a, w_rg_r, b_rg_r, w_rg_i, b_rg_i, lru_lambda, w_in_b, sinks_b, g_kv, w_kv)
    ms = (m_g_mix_pre, m_g_mix_post, m_g_ffn_pre, m_g_ffn_post, m_g_mem, m_w_mem_kv, m_w_mix_out, m_w_ffn_up, m_w_ffn_conv, m_b_ffn_conv, m_w_ffn_down, m_w_in_a, m_w_conv_a, m_b_conv_a, m_w_rg_r, m_b_rg_r, m_w_rg_i, m_b_rg_i, m_lru_lambda, m_w_in_b, m_sinks_b, m_g_kv, m_w_kv)
    vs = (v_g_mix_pre, v_g_mix_post, v_g_ffn_pre, v_g_ffn_post, v_g_mem, v_w_mem_kv, v_w_mix_out, v_w_ffn_up, v_w_ffn_conv, v_b_ffn_conv, v_w_ffn_down, v_w_in_a, v_w_conv_a, v_b_conv_a, v_w_rg_r, v_b_rg_r, v_w_rg_i, v_b_rg_i, v_lru_lambda, v_w_in_b, v_sinks_b, v_g_kv, v_w_kv)
    return _train_step(x, mem, loss_target, dict(zip(WEIGHTS, args)), dict(zip(WEIGHTS, ms)), dict(zip(WEIGHTS, vs)))
```

```python
import functools
import math

import numpy as np
import jax
import jax.numpy as jnp
from jax import lax
from jax.experimental import pallas as pl
from jax.experimental.pallas import tpu as pltpu

F32 = jnp.float32
MXU = jnp.bfloat16

D_MODEL = 1024
HEAD_DIM = 64
MEM_LEN = 256
MEM_HEADS = 4
MEM_WIDTH = MEM_HEADS * HEAD_DIM
MIX_WIDTH = D_MODEL - MEM_WIDTH
LRU_BLOCKS = MIX_WIDTH // HEAD_DIM
LRU_CONV = 4
LRU_C = 8.0
SWA_HEADS = MIX_WIDTH // HEAD_DIM
SWA_KV_HEADS = 4
SWA_GROUP = SWA_HEADS // SWA_KV_HEADS
WINDOW = 128
D_FF = 2816
FFN_CONV = 3
EPS = 1e-6
DEPTH = 4
N_A = 2

ADAM_LR = 0.001
ADAM_B1 = 0.9
ADAM_B2 = 0.999
ADAM_EPS = 1e-08
ADAM_WD = 0.01
ADAM_STEP = 10

VMEM_LIMIT_V7X = 56 * 1024 * 1024
LANE = 128
SUBLANE = 8
GATE_TILE = 256
MESH_T = pl.DeviceIdType.MESH


def _alibi_slopes(n):
    def pow2_slopes(m):
        start = 2.0 ** (-8.0 / m)
        return [start ** (i + 1) for i in range(m)]
    c = 2 ** int(math.floor(math.log2(n)))
    s = pow2_slopes(c)
    if c != n:
        s = s + pow2_slopes(2 * c)[0::2][: n - c]
    return [float(np.float32(v)) for v in s]


SLOPES = _alibi_slopes(SWA_HEADS)


def _tile(n, cap, mult=LANE):
    best = None
    for t in range(mult, min(n, cap) + 1, mult):
        if n % t == 0:
            best = t
    return best if best is not None else n


def _cp(sem):
    return pltpu.CompilerParams(dimension_semantics=sem, vmem_limit_bytes=VMEM_LIMIT_V7X)


MM_VMEM_BUDGET = 40 * 1024 * 1024
HBM_BYTES_PER_US_V7X = 3.0e6
GRID_STEP_US = 0.35


def _divisors(n, mult):
    return [t for t in range(mult, n + 1, mult) if n % t == 0] or [n]


def _mm_tiles(m, k, n, out_bytes):
    best = None
    for tm in _divisors(m, 256):
        for tn in _divisors(n, LANE):
            vmem = 2 * (tm * k * 2 + k * tn * 2 + tm * tn * out_bytes)
            if vmem > MM_VMEM_BUDGET:
                continue
            steps = (m // tm) * (n // tn)
            b_reads = 1 if tn == n else m // tm
            traffic = m * k * 2 + k * n * 2 * b_reads + m * n * out_bytes
            first = tm * k * 2 + k * tn * 2
            cost = (traffic + first) / HBM_BYTES_PER_US_V7X + steps * GRID_STEP_US
            if best is None or cost < best[0]:
                best = (cost, tm, tn)
    return best[1], best[2]


def _mm_tn_tiles(k, m, n):
    best = None
    for tm in _divisors(m, LANE):
        for tn in _divisors(n, LANE):
            for tk in _divisors(k, 512):
                vmem = 2 * (tk * tm * 2 + tk * tn * 2 + tm * tn * 4)
                if vmem > MM_VMEM_BUDGET:
                    continue
                steps = (m // tm) * (n // tn) * (k // tk)
                traffic = k * m * 2 * (n // tn) + k * n * 2 * (m // tm) + m * n * 4
                cost = traffic / HBM_BYTES_PER_US_V7X + steps * GRID_STEP_US
                if best is None or cost < best[0]:
                    best = (cost, tk, tm, tn)
    return best[1], best[2], best[3]


ARB = "arbitrary"
PAR = "parallel"


def _rms_fwd(x, g):
    r = lax.rsqrt(jnp.mean(x * x, axis=-1, keepdims=True) + EPS)
    return x * r * g


def _rms_bwd(dy, x, g):
    r = lax.rsqrt(jnp.mean(x * x, axis=-1, keepdims=True) + EPS)
    xh = x * r
    gdy = dy * g
    dx = r * (gdy - xh * jnp.mean(gdy * xh, axis=-1, keepdims=True))
    dg = jnp.sum(dy * xh, axis=0, keepdims=True)
    return dx, dg


_GELU_K = math.sqrt(2.0 / math.pi)
_GELU_C = 0.044715


def _gelu(x):
    t = jnp.tanh(_GELU_K * (x + _GELU_C * x * x * x))
    return 0.5 * x * (1.0 + t)


def _gelu_and_grad(x):
    x2 = x * x
    t = jnp.tanh(_GELU_K * (x + _GELU_C * x2 * x))
    g = 0.5 * x * (1.0 + t)
    dg = 0.5 * (1.0 + t) + 0.5 * x * (1.0 - t * t) * (_GELU_K * (1.0 + 3.0 * _GELU_C * x2))
    return g, dg


def _shift_down(x, k, row):
    return jnp.where(row >= k, pltpu.roll(x, k, axis=0), 0.0)


def _shift_up(x, k, row):
    n = x.shape[0]
    return jnp.where(row < n - k, pltpu.roll(x, n - k, axis=0), 0.0)


def _shift_down_edge(x, k):
    r = pltpu.roll(x, k, axis=0)
    row = lax.broadcasted_iota(jnp.int32, (SUBLANE, x.shape[1]), 0)
    return jnp.concatenate([jnp.where(row >= k, r[:SUBLANE], 0.0), r[SUBLANE:]], axis=0)


def _shift_up_edge(x, k):
    n = x.shape[0]
    r = pltpu.roll(x, n - k, axis=0)
    row = lax.broadcasted_iota(jnp.int32, (SUBLANE, x.shape[1]), 0)
    return jnp.concatenate([r[:n - SUBLANE], jnp.where(row < SUBLANE - k, r[n - SUBLANE:], 0.0)], axis=0)


def _dot(a, b):
    return jnp.dot(a, b, preferred_element_type=F32)


def _dot_nt(a, b):
    return lax.dot_general(a, b, (((1,), (1,)), ((), ())), preferred_element_type=F32)


def _dot_tn(a, b):
    return lax.dot_general(a, b, (((0,), (0,)), ((), ())), preferred_element_type=F32)


MXU_FLOPS_PER_US = 7.0e8
HOST_US = {"lru_fwd": 44.0, "lru_bwd": 94.0, "swa_fwd": 60.0, "swa_bwd": 160.0, "mem_attn_fwd": 21.0,
           "mem_attn_bwd": 33.0, "ffn_act": 70.0, "ffn_act_bwd": 100.0, "resid": 22.0, "resid_bwd": 33.0}


def _hosted_call(body, *, grid, in_specs, out_specs, out_shape, args, name, aliases=None, scratch_shapes=(),
                 q=None, flops=0.0, budget_us=0.0):
    chunks = q.take(flops / MXU_FLOPS_PER_US + budget_us) if q is not None else []
    if not chunks:
        return pl.pallas_call(
            body, grid=grid, in_specs=in_specs, out_specs=out_specs, out_shape=out_shape,
            scratch_shapes=list(scratch_shapes), input_output_aliases=aliases or {}, name=name,
            compiler_params=_cp((ARB,) * len(grid)))(*args)
    single = not isinstance(out_shape, (list, tuple))
    o_shapes = [out_shape] if single else list(out_shape)
    o_specs = [out_specs] if single else list(out_specs)
    n_in, n_out, n_scr = len(args), len(o_shapes), len(scratch_shapes)
    c_ins = [a for ch in chunks for a in ch.ins]
    c_outs = [s for ch in chunks for s in ch.out_shapes]
    alias = dict(aliases or {})
    in_off, out_off, sem_off = [], [], []
    i0 = o0 = s0 = 0
    for ch in chunks:
        in_off.append(i0)
        out_off.append(o0)
        sem_off.append(s0)
        for ci, co in ch.alias.items():
            alias[n_in + i0 + ci] = n_out + o0 + co
        i0 += len(ch.ins)
        o0 += len(ch.out_shapes)
        s0 += ch.n_sem

    def wrapped(*refs):
        ins = refs[:n_in]
        cin = refs[n_in:n_in + i0]
        outs = refs[n_in + i0:n_in + i0 + n_out]
        cout = refs[n_in + i0 + n_out:n_in + i0 + n_out + o0]
        scr = refs[n_in + i0 + n_out + o0:n_in + i0 + n_out + o0 + n_scr]
        send_sems, recv_sems = refs[n_in + i0 + n_out + o0 + n_scr:]
        first = functools.reduce(lambda u, v: u & v, [pl.program_id(d) == 0 for d in range(len(grid))])
        last = functools.reduce(lambda u, v: u & v, [pl.program_id(d) == grid[d] - 1 for d in range(len(grid))])

        def each(phase):
            for ch, a, b, s in zip(chunks, in_off, out_off, sem_off):
                getattr(ch, phase)(cin[a:a + len(ch.ins)], cout[b:b + len(ch.out_shapes)], send_sems, recv_sems, s)

        pl.when(first)(lambda: each("start"))
        body(*ins, *outs, *scr)
        pl.when(last)(lambda: each("finish"))

    hbm = pl.BlockSpec(memory_space=pl.ANY)
    res = pl.pallas_call(
        wrapped, grid=grid, in_specs=list(in_specs) + [hbm] * i0, out_specs=o_specs + [hbm] * o0,
        out_shape=o_shapes + c_outs,
        scratch_shapes=list(scratch_shapes) + [pltpu.SemaphoreType.DMA((s0,)), pltpu.SemaphoreType.DMA((s0,))],
        input_output_aliases=alias, name=name,
        compiler_params=pltpu.CompilerParams(dimension_semantics=(ARB,) * len(grid), vmem_limit_bytes=VMEM_LIMIT_V7X,
                                             has_side_effects=True))(*args, *c_ins)
    for ch, b in zip(chunks, out_off):
        ch.done(list(res[n_out + b:n_out + b + len(ch.out_shapes)]))
    return res[0] if single else list(res[:n_out])


def _mm_nn(a, b, *, name, q=None, out_dtype=F32):
    m, k = a.shape
    n = b.shape[-1]
    tm, tn = _mm_tiles(m, k, n, jnp.dtype(out_dtype).itemsize)

    def body(a_ref, b_ref, o_ref):
        o_ref[...] = _dot(a_ref[...], b_ref[...]).astype(o_ref.dtype)

    return _hosted_call(
        body, grid=(m // tm, n // tn),
        in_specs=[pl.BlockSpec((tm, k), lambda i, j: (i, 0)),
                  pl.BlockSpec((None, k, tn), lambda i, j: (0, 0, j))],
        out_specs=pl.BlockSpec((tm, tn), lambda i, j: (i, j)),
        out_shape=jax.ShapeDtypeStruct((m, n), out_dtype),
        args=(a, b), name=name, q=q, flops=2.0 * m * k * n)


def _mm_nt(a, b, *, name, q=None, out_dtype=F32):
    m, k = a.shape
    n = b.shape[-2]
    tm, tn = _mm_tiles(m, k, n, jnp.dtype(out_dtype).itemsize)

    def body(a_ref, b_ref, o_ref):
        o_ref[...] = _dot_nt(a_ref[...], b_ref[...]).astype(o_ref.dtype)

    return _hosted_call(
        body, grid=(m // tm, n // tn),
        in_specs=[pl.BlockSpec((tm, k), lambda i, j: (i, 0)),
                  pl.BlockSpec((None, tn, k), lambda i, j: (0, j, 0))],
        out_specs=pl.BlockSpec((tm, tn), lambda i, j: (i, j)),
        out_shape=jax.ShapeDtypeStruct((m, n), out_dtype),
        args=(a, b), name=name, q=q, flops=2.0 * m * k * n)


def _mm_tn(a, b, *, name, q=None, out=None, n_total=None, col_block_offset=0):
    k, m = a.shape
    n = b.shape[-1]
    tk, tm, tn = _mm_tn_tiles(k, m, n)
    off = col_block_offset * (n // tn)

    def body(a_ref, b_ref, *rest):
        o_ref = rest[-1]
        part = _dot_tn(a_ref[...], b_ref[...])

        @pl.when(pl.program_id(2) == 0)
        def _():
            o_ref[...] = part

        @pl.when(pl.program_id(2) > 0)
        def _():
            o_ref[...] += part

    in_specs = [pl.BlockSpec((tk, tm), lambda i, j, s: (s, i)), pl.BlockSpec((tk, tn), lambda i, j, s: (s, j))]
    args = (a, b)
    if out is not None:
        in_specs.append(pl.BlockSpec(memory_space=pl.ANY))
        args = (a, b, out)
    return _hosted_call(
        body, grid=(m // tm, n // tn, k // tk), in_specs=in_specs,
        out_specs=pl.BlockSpec((None, tm, tn), lambda i, j, s: (0, i, j + off)),
        out_shape=jax.ShapeDtypeStruct((1, m, n_total or n), F32),
        aliases={2: 0} if out is not None else None,
        args=args, name=name, q=q, flops=2.0 * m * k * n)


def _mm_ffn_dh(dg, dv, w_up, *, name, q=None):
    m, f = dg.shape
    d = w_up.shape[-2]
    tm, tn = _mm_tiles(m, 2 * f, d, 4)

    def body(dg_ref, dv_ref, wg_ref, wv_ref, o_ref):
        o_ref[...] = _dot_nt(dg_ref[...], wg_ref[...]) + _dot_nt(dv_ref[...], wv_ref[...])

    return _hosted_call(
        body, grid=(m // tm, d // tn),
        in_specs=[pl.BlockSpec((tm, f), lambda i, j: (i, 0)),
                  pl.BlockSpec((tm, f), lambda i, j: (i, 0)),
                  pl.BlockSpec((None, tn, f), lambda i, j: (0, j, 0)),
                  pl.BlockSpec((None, tn, f), lambda i, j: (0, j, 1))],
        out_specs=pl.BlockSpec((tm, tn), lambda i, j: (i, j)),
        out_shape=jax.ShapeDtypeStruct((m, d), F32),
        args=(dg, dv, w_up, w_up), name=name, q=q, flops=4.0 * m * f * d)


def _norm_fwd(x, g, *, name):
    n, d = x.shape
    tm = _tile(n, 256, SUBLANE)

    def body(x_ref, g_ref, o_ref):
        o_ref[...] = _rms_fwd(x_ref[...], g_ref[...]).astype(o_ref.dtype)

    return pl.pallas_call(
        body, grid=(n // tm,),
        in_specs=[pl.BlockSpec((tm, d), lambda i: (i, 0)), pl.BlockSpec((1, d), lambda i: (0, 0))],
        out_specs=pl.BlockSpec((tm, d), lambda i: (i, 0)),
        out_shape=jax.ShapeDtypeStruct((n, d), MXU),
        name=name, compiler_params=_cp((PAR,)))(x, g)


def _norm_bwd_dg(dy, x, g, *, name):
    n, d = x.shape
    tm = _tile(n, 256, SUBLANE)

    def body(dy_ref, x_ref, g_ref, dg_ref):
        @pl.when(pl.program_id(0) == 0)
        def _():
            dg_ref[...] = jnp.zeros_like(dg_ref)
        _, dg = _rms_bwd(dy_ref[...], x_ref[...], g_ref[...])
        dg_ref[...] += dg

    return pl.pallas_call(
        body, grid=(n // tm,),
        in_specs=[pl.BlockSpec((tm, d), lambda i: (i, 0)), pl.BlockSpec((tm, d), lambda i: (i, 0)),
                  pl.BlockSpec((1, d), lambda i: (0, 0))],
        out_specs=pl.BlockSpec((1, d), lambda i: (0, 0)),
        out_shape=jax.ShapeDtypeStruct((1, d), F32),
        name=name, compiler_params=_cp((ARB,)))(dy, x, g)


def _resid_norm_fwd(x, y, g_post, g_pres, *, name, q=None):
    n, d = x.shape
    tm = _tile(n, 256, SUBLANE)
    nh = len(g_pres)

    def body(x_ref, y_ref, gp_ref, *rest):
        gpre = rest[:nh]
        xo_ref = rest[nh]
        h_refs = rest[nh + 1:]
        xo = x_ref[...] + _rms_fwd(y_ref[...], gp_ref[...])
        xo_ref[...] = xo
        for g_ref, h_ref in zip(gpre, h_refs):
            h_ref[...] = _rms_fwd(xo, g_ref[...]).astype(h_ref.dtype)

    row = pl.BlockSpec((tm, d), lambda i: (i, 0))
    vec = pl.BlockSpec((1, d), lambda i: (0, 0))
    outs = _hosted_call(
        body, grid=(n // tm,),
        in_specs=[row, row, vec] + [vec] * nh,
        out_specs=[row] + [row] * nh,
        out_shape=[jax.ShapeDtypeStruct((n, d), F32)] + [jax.ShapeDtypeStruct((n, d), MXU)] * nh,
        args=(x, y, g_post, *g_pres), name=name, q=q, budget_us=HOST_US["resid"])
    return outs[0], list(outs[1:])


def _loss_fwd(x, y, g_post, target, *, name):
    n, d = x.shape
    tm = _tile(n, 256, SUBLANE)

    def body(x_ref, y_ref, gp_ref, t_ref, dx_ref, sq_ref):
        @pl.when(pl.program_id(0) == 0)
        def _():
            sq_ref[...] = jnp.zeros_like(sq_ref)
        err = x_ref[...] + _rms_fwd(y_ref[...], gp_ref[...]) - t_ref[...]
        dx_ref[...] = err * (1.0 / d)
        sq_ref[...] += jnp.sum(err * err, axis=0, keepdims=True)

    row = pl.BlockSpec((tm, d), lambda i: (i, 0))
    vec = pl.BlockSpec((1, d), lambda i: (0, 0))
    return pl.pallas_call(
        body, grid=(n // tm,),
        in_specs=[row, row, vec, row],
        out_specs=[row, vec],
        out_shape=[jax.ShapeDtypeStruct((n, d), F32), jax.ShapeDtypeStruct((1, d), F32)],
        name=name, compiler_params=_cp((ARB,)))(x, y, g_post, target)


def _resid_norm_bwd(dx_out, dhs, x_out, g_pres, y, g_post, *, name, q=None):
    n, d = dx_out.shape
    tm = _tile(n, 256, SUBLANE)
    nh = len(dhs)
    has_y = y is not None

    def body(*refs):
        it = iter(refs)
        dxo_ref = next(it)
        dh_refs = [next(it) for _ in range(nh)]
        xo_ref = next(it) if nh else None
        gpre_refs = [next(it) for _ in range(nh)]
        y_ref = next(it) if has_y else None
        gpost_ref = next(it) if has_y else None
        g_out = next(it)
        dy_out = next(it) if has_y else None
        dgpre_out = [next(it) for _ in range(nh)]
        dgpost_out = next(it) if has_y else None

        @pl.when(pl.program_id(0) == 0)
        def _():
            for r in dgpre_out:
                r[...] = jnp.zeros_like(r)
            if has_y:
                dgpost_out[...] = jnp.zeros_like(dgpost_out)

        g = dxo_ref[...]
        if nh:
            xo = xo_ref[...]
            for dh_ref, gp_ref, dg_ref in zip(dh_refs, gpre_refs, dgpre_out):
                dx, dg = _rms_bwd(dh_ref[...], xo, gp_ref[...])
                g = g + dx
                dg_ref[...] += dg
        g_out[...] = g
        if has_y:
            dy, dg = _rms_bwd(g, y_ref[...], gpost_ref[...])
            dy_out[...] = dy.astype(dy_out.dtype)
            dgpost_out[...] += dg

    row = pl.BlockSpec((tm, d), lambda i: (i, 0))
    vec = pl.BlockSpec((1, d), lambda i: (0, 0))
    ins, in_specs = [dx_out], [row]
    ins += list(dhs)
    in_specs += [row] * nh
    if nh:
        ins.append(x_out)
        in_specs.append(row)
    ins += list(g_pres)
    in_specs += [vec] * nh
    if has_y:
        ins += [y, g_post]
        in_specs += [row, vec]
    out_specs, out_shape = [row], [jax.ShapeDtypeStruct((n, d), F32)]
    if has_y:
        out_specs.append(row)
        out_shape.append(jax.ShapeDtypeStruct((n, d), MXU))
    out_specs += [vec] * nh
    out_shape += [jax.ShapeDtypeStruct((1, d), F32)] * nh
    if has_y:
        out_specs.append(vec)
        out_shape.append(jax.ShapeDtypeStruct((1, d), F32))
    outs = list(_hosted_call(
        body, grid=(n // tm,), in_specs=in_specs, out_specs=out_specs, out_shape=out_shape,
        args=tuple(ins), name=name, q=q, budget_us=HOST_US["resid_bwd"]))
    g = outs.pop(0)
    dy = outs.pop(0) if has_y else None
    dgpre = [outs.pop(0) for _ in range(nh)]
    dgpost = outs.pop(0) if has_y else None
    return g, dy, dgpre, dgpost


def _ffn_conv(up, w_ref, b_ref):
    return (w_ref[0:1, :] * _shift_down_edge(up, 2) + w_ref[1:2, :] * _shift_down_edge(up, 1)
            + w_ref[2:3, :] * up + b_ref[...])


def _ffn_act_fwd(up, wconv, bconv, bsz, *, name, q=None):
    n, f2 = up.shape
    f = f2 // 2
    t = n // bsz
    tc = _tile(f, 256)
    nf = f // tc

    def body(ug_ref, uv_ref, wg_ref, wv_ref, bg_ref, bv_ref, o_ref, g_ref, v_ref):
        g = _ffn_conv(ug_ref[...], wg_ref, bg_ref)
        v = _ffn_conv(uv_ref[...], wv_ref, bv_ref)
        g_ref[...] = g
        v_ref[...] = v
        o_ref[...] = (_gelu(g) * v).astype(o_ref.dtype)

    blk = pl.BlockSpec((t, tc), lambda b, j: (b, j))
    return _hosted_call(
        body, grid=(bsz, nf),
        in_specs=[blk, pl.BlockSpec((t, tc), lambda b, j: (b, j + nf)),
                  pl.BlockSpec((FFN_CONV, tc), lambda b, j: (0, j)),
                  pl.BlockSpec((FFN_CONV, tc), lambda b, j: (0, j + nf)),
                  pl.BlockSpec((1, tc), lambda b, j: (0, j)),
                  pl.BlockSpec((1, tc), lambda b, j: (0, j + nf))],
        out_specs=[blk, blk, blk],
        out_shape=[jax.ShapeDtypeStruct((n, f), MXU), jax.ShapeDtypeStruct((n, f), F32),
                   jax.ShapeDtypeStruct((n, f), F32)],
        args=(up, up, wconv, wconv, bconv, bconv), name=name, q=q, budget_us=HOST_US["ffn_act"])


def _ffn_act_bwd(up, ug, uv, dact, wconv, bsz, *, name, q=None):
    n, f2 = up.shape
    f = f2 // 2
    t = n // bsz
    tc = _tile(f, 256)
    nf = f // tc

    def body(xg_ref, xv_ref, g_ref, v_ref, da_ref, wg_ref, wv_ref,
             dug_ref, duv_ref, dwg_ref, dwv_ref, dbg_ref, dbv_ref):
        @pl.when(pl.program_id(1) == 0)
        def _():
            for r in (dwg_ref, dwv_ref, dbg_ref, dbv_ref):
                r[...] = jnp.zeros_like(r)

        gl, dgl = _gelu_and_grad(g_ref[...])
        da = da_ref[...]
        dg = da * v_ref[...] * dgl
        dv = da * gl

        def conv_bwd(du, w_ref, x_ref, dx_ref, dw_ref, db_ref):
            du1, du2 = _shift_up_edge(du, 1), _shift_up_edge(du, 2)
            dx_ref[...] = (w_ref[2:3, :] * du + w_ref[1:2, :] * du1 + w_ref[0:1, :] * du2).astype(dx_ref.dtype)
            x = x_ref[...]
            dw_ref[0:1, :] += jnp.sum(x * du2, axis=0, keepdims=True)
            dw_ref[1:2, :] += jnp.sum(x * du1, axis=0, keepdims=True)
            dw_ref[2:3, :] += jnp.sum(x * du, axis=0, keepdims=True)
            db_ref[...] += jnp.sum(du, axis=0, keepdims=True)

        conv_bwd(dg, wg_ref, xg_ref, dug_ref, dwg_ref, dbg_ref)
        conv_bwd(dv, wv_ref, xv_ref, duv_ref, dwv_ref, dbv_ref)

    blk = pl.BlockSpec((t, tc), lambda j, b: (b, j))
    wspec = pl.BlockSpec((FFN_CONV, tc), lambda j, b: (0, j))
    bspec = pl.BlockSpec((1, tc), lambda j, b: (0, j))
    outs = _hosted_call(
        body, grid=(nf, bsz),
        in_specs=[blk, pl.BlockSpec((t, tc), lambda j, b: (b, j + nf)), blk, blk, blk,
                  wspec, pl.BlockSpec((FFN_CONV, tc), lambda j, b: (0, j + nf))],
        out_specs=[blk, blk, wspec, wspec, bspec, bspec],
        out_shape=[jax.ShapeDtypeStruct((n, f), MXU), jax.ShapeDtypeStruct((n, f), MXU),
                   jax.ShapeDtypeStruct((FFN_CONV, f), F32), jax.ShapeDtypeStruct((FFN_CONV, f), F32),
                   jax.ShapeDtypeStruct((1, f), F32), jax.ShapeDtypeStruct((1, f), F32)],
        args=(up, up, ug, uv, dact, wconv, wconv), name=name, q=q, budget_us=HOST_US["ffn_act_bwd"])
    dug, duv, dwg, dwv, dbg, dbv = outs
    return dug, duv, jnp.concatenate([dwg, dwv], axis=1), jnp.concatenate([dbg, dbv], axis=1)


def _mem_attn_fwd(proj, q_col_block, mkv, ycat, bsz, *, name, q=None):
    n = proj.shape[0]
    t = n // bsz
    tq = _tile(t, 512, SUBLANE)
    nt = t // tq
    scale = HEAD_DIM ** -0.5

    def body(q_ref, kv_ref, old_ref, o_ref):
        del old_ref
        outs = []
        for h in range(MEM_HEADS):
            sl = slice(h * HEAD_DIM, (h + 1) * HEAD_DIM)
            q = q_ref[:, sl].astype(MXU)
            k = kv_ref[:, sl].astype(MXU)
            v = kv_ref[:, MEM_WIDTH + h * HEAD_DIM: MEM_WIDTH + (h + 1) * HEAD_DIM].astype(MXU)
            s = _dot_nt(q, k) * scale
            m = jnp.max(s, axis=-1, keepdims=True)
            p = jnp.exp(s - m)
            p = p / jnp.sum(p, axis=-1, keepdims=True)
            outs.append(_dot(p.astype(MXU), v))
        o_ref[...] = jnp.concatenate(outs, axis=-1).astype(o_ref.dtype)

    return _hosted_call(
        body, grid=(bsz, nt),
        in_specs=[pl.BlockSpec((tq, MEM_WIDTH), lambda b, i: (b * nt + i, q_col_block)),
                  pl.BlockSpec((MEM_LEN, 2 * MEM_WIDTH), lambda b, i: (b, 0)),
                  pl.BlockSpec(memory_space=pl.ANY)],
        out_specs=pl.BlockSpec((tq, MEM_WIDTH), lambda b, i: (b * nt + i, MIX_WIDTH // MEM_WIDTH)),
        out_shape=jax.ShapeDtypeStruct(ycat.shape, ycat.dtype),
        aliases={2: 0}, args=(proj, mkv, ycat), name=name, q=q, budget_us=HOST_US["mem_attn_fwd"])


def _mem_attn_bwd(proj, q_col_block, mkv, dycat, dproj, bsz, *, name, q=None):
    n = proj.shape[0]
    t = n // bsz
    tq = _tile(t, 512, SUBLANE)
    nt = t // tq
    scale = HEAD_DIM ** -0.5

    def body(q_ref, kv_ref, do_ref, old_ref, dq_ref, dkv_ref):
        del old_ref

        @pl.when(pl.program_id(1) == 0)
        def _():
            dkv_ref[...] = jnp.zeros_like(dkv_ref)

        dqs, dks, dvs = [], [], []
        for h in range(MEM_HEADS):
            sl = slice(h * HEAD_DIM, (h + 1) * HEAD_DIM)
            q = q_ref[:, sl].astype(MXU)
            k = kv_ref[:, sl].astype(MXU)
            v = kv_ref[:, MEM_WIDTH + h * HEAD_DIM: MEM_WIDTH + (h + 1) * HEAD_DIM].astype(MXU)
            do = do_ref[:, sl].astype(MXU)
            s = _dot_nt(q, k) * scale
            m = jnp.max(s, axis=-1, keepdims=True)
            p = jnp.exp(s - m)
            p = p / jnp.sum(p, axis=-1, keepdims=True)
            dvs.append(_dot_tn(p.astype(MXU), do))
            dp = _dot_nt(do, v)
            ds = (p * (dp - jnp.sum(dp * p, axis=-1, keepdims=True)) * scale).astype(MXU)
            dqs.append(_dot(ds, k))
            dks.append(_dot_tn(ds, q))
        dq_ref[...] = jnp.concatenate(dqs, axis=-1).astype(dq_ref.dtype)
        dkv_ref[...] += jnp.concatenate(dks + dvs, axis=-1)

    return _hosted_call(
        body, grid=(bsz, nt),
        in_specs=[pl.BlockSpec((tq, MEM_WIDTH), lambda b, i: (b * nt + i, q_col_block)),
                  pl.BlockSpec((MEM_LEN, 2 * MEM_WIDTH), lambda b, i: (b, 0)),
                  pl.BlockSpec((tq, MEM_WIDTH), lambda b, i: (b * nt + i, MIX_WIDTH // MEM_WIDTH)),
                  pl.BlockSpec(memory_space=pl.ANY)],
        out_specs=[pl.BlockSpec((tq, MEM_WIDTH), lambda b, i: (b * nt + i, q_col_block)),
                   pl.BlockSpec((MEM_LEN, 2 * MEM_WIDTH), lambda b, i: (b, 0))],
        out_shape=[jax.ShapeDtypeStruct(dproj.shape, dproj.dtype),
                   jax.ShapeDtypeStruct((bsz * MEM_LEN, 2 * MEM_WIDTH), F32)],
        aliases={3: 0}, args=(proj, mkv, dycat, dproj), name=name, q=q, budget_us=HOST_US["mem_attn_bwd"])


def _swa_scores(q, k, h, dist, mask, sink):
    s = _dot_nt(q, k) * (HEAD_DIM ** -0.5)
    s = jnp.where(mask, s - SLOPES[h] * dist, -jnp.inf)
    m = jnp.maximum(jnp.max(s, axis=-1, keepdims=True), sink)
    p = jnp.exp(s - m)
    psink = jnp.exp(sink - m)
    inv = 1.0 / (jnp.sum(p, axis=-1, keepdims=True) + psink)
    return p * inv, psink * inv


def _swa_mask(n):
    qi = lax.broadcasted_iota(jnp.int32, (WINDOW, 2 * WINDOW), 0) + WINDOW
    ki = lax.broadcasted_iota(jnp.int32, (WINDOW, 2 * WINDOW), 1)
    dist = qi - ki
    mask = (dist >= 0) & (dist < WINDOW) & ((n > 0) | (ki >= WINDOW))
    return dist.astype(F32), mask


def _swa_fwd(proj, kv, sinks, bsz, *, name, q=None):
    n_tok = proj.shape[0]
    nb = n_tok // bsz // WINDOW
    kvw = SWA_KV_HEADS * HEAD_DIM

    def body(sink_ref, q_ref, kvp_ref, kvc_ref, o_ref):
        n = pl.program_id(1)
        dist, mask = _swa_mask(n)
        kk = jnp.concatenate([kvp_ref[:, :kvw], kvc_ref[:, :kvw]], axis=0).astype(MXU)
        vv = jnp.concatenate([kvp_ref[:, kvw:], kvc_ref[:, kvw:]], axis=0).astype(MXU)
        outs = []
        for h in range(SWA_HEADS):
            c = h // SWA_GROUP
            q = q_ref[:, h * HEAD_DIM:(h + 1) * HEAD_DIM].astype(MXU)
            p, _ = _swa_scores(q, kk[:, c * HEAD_DIM:(c + 1) * HEAD_DIM], h, dist, mask, sink_ref[h])
            outs.append(_dot(p.astype(MXU), vv[:, c * HEAD_DIM:(c + 1) * HEAD_DIM]))
        o_ref[...] = jnp.concatenate(outs, axis=-1).astype(o_ref.dtype)

    return _hosted_call(
        body, grid=(bsz, nb),
        in_specs=[pl.BlockSpec(memory_space=pltpu.SMEM),
                  pl.BlockSpec((WINDOW, MIX_WIDTH), lambda b, n: (b * nb + n, 0)),
                  pl.BlockSpec((WINDOW, 2 * kvw), lambda b, n: (b * nb + jnp.maximum(n - 1, 0), 0)),
                  pl.BlockSpec((WINDOW, 2 * kvw), lambda b, n: (b * nb + n, 0))],
        out_specs=pl.BlockSpec((WINDOW, MIX_WIDTH), lambda b, n: (b * nb + n, 0)),
        out_shape=jax.ShapeDtypeStruct((n_tok, D_MODEL), MXU),
        args=(sinks, proj, kv, kv), name=name, q=q, budget_us=HOST_US["swa_fwd"])


def _swa_bwd(proj, kv, sinks, dycat, bsz, *, name, q=None):
    n_tok = proj.shape[0]
    nb = n_tok // bsz // WINDOW
    kvw = SWA_KV_HEADS * HEAD_DIM

    def body(sink_ref, q_ref, kvp_ref, kvc_ref, do_ref, dq_ref, dkvc_ref, dkvp_ref, dsink_ref):
        n = pl.program_id(1)

        @pl.when((pl.program_id(0) == 0) & (n == 0))
        def _():
            dsink_ref[...] = jnp.zeros_like(dsink_ref)

        dist, mask = _swa_mask(n)
        kk = jnp.concatenate([kvp_ref[:, :kvw], kvc_ref[:, :kvw]], axis=0).astype(MXU)
        vv = jnp.concatenate([kvp_ref[:, kvw:], kvc_ref[:, kvw:]], axis=0).astype(MXU)
        lane = lax.broadcasted_iota(jnp.int32, (SUBLANE, LANE), 1)
        dqs = []
        dks = [None] * SWA_KV_HEADS
        dvs = [None] * SWA_KV_HEADS
        dsink = jnp.zeros((SUBLANE, LANE), F32)
        for h in range(SWA_HEADS):
            c = h // SWA_GROUP
            k = kk[:, c * HEAD_DIM:(c + 1) * HEAD_DIM]
            v = vv[:, c * HEAD_DIM:(c + 1) * HEAD_DIM]
            q = q_ref[:, h * HEAD_DIM:(h + 1) * HEAD_DIM].astype(MXU)
            do = do_ref[:, h * HEAD_DIM:(h + 1) * HEAD_DIM].astype(MXU)
            p, psink = _swa_scores(q, k, h, dist, mask, sink_ref[h])
            dv = _dot_tn(p.astype(MXU), do)
            dp = _dot_nt(do, v)
            rs = jnp.sum(dp * p, axis=-1, keepdims=True)
            ds = (p * (dp - rs) * (HEAD_DIM ** -0.5)).astype(MXU)
            dsink = dsink + jnp.where(lane == h, jnp.sum(-psink * rs, axis=0, keepdims=True), 0.0)
            dqs.append(_dot(ds, k))
            dk = _dot_tn(ds, q)
            dks[c] = dk if dks[c] is None else dks[c] + dk
            dvs[c] = dv if dvs[c] is None else dvs[c] + dv
        dq_ref[...] = jnp.concatenate(dqs, axis=-1).astype(dq_ref.dtype)
        dkv = jnp.concatenate(dks + dvs, axis=-1)
        dkvp_ref[...] = dkv[:WINDOW]
        dkvc_ref[...] = dkv[WINDOW:]
        dsink_ref[...] += dsink

    qspec = pl.BlockSpec((WINDOW, MIX_WIDTH), lambda b, n: (b * nb + n, 0))
    kvspec = pl.BlockSpec((WINDOW, 2 * kvw), lambda b, n: (b * nb + n, 0))
    return _hosted_call(
        body, grid=(bsz, nb),
        in_specs=[pl.BlockSpec(memory_space=pltpu.SMEM), qspec,
                  pl.BlockSpec((WINDOW, 2 * kvw), lambda b, n: (b * nb + jnp.maximum(n - 1, 0), 0)),
                  kvspec, qspec],
        out_specs=[qspec, kvspec, kvspec, pl.BlockSpec((SUBLANE, LANE), lambda b, n: (0, 0))],
        out_shape=[jax.ShapeDtypeStruct((n_tok, D_MODEL), MXU),
                   jax.ShapeDtypeStruct((n_tok, 2 * kvw), F32),
                   jax.ShapeDtypeStruct((n_tok, 2 * kvw), F32),
                   jax.ShapeDtypeStruct((SUBLANE, LANE), F32)],
        args=(sinks, proj, kv, kv, dycat), name=name, q=q, budget_us=HOST_US["swa_bwd"])


def _swa_dkv_combine(curs, prevs, bsz, *, name):
    n_tok, w = curs[0].shape
    nb = n_tok // bsz // WINDOW
    k = len(curs)

    def body(*refs):
        o_ref = refs[-1]
        n = pl.program_id(1)
        acc = refs[0][...]
        for r in refs[1:k]:
            acc = acc + r[...]
        nxt = refs[k][...]
        for r in refs[k + 1:2 * k]:
            nxt = nxt + r[...]
        o_ref[...] = (acc + jnp.where(n < nb - 1, nxt, 0.0)).astype(o_ref.dtype)

    cur = pl.BlockSpec((WINDOW, w), lambda b, n: (b * nb + n, 0))
    prv = pl.BlockSpec((WINDOW, w), lambda b, n: (b * nb + jnp.minimum(n + 1, nb - 1), 0))
    return pl.pallas_call(
        body, grid=(bsz, nb), in_specs=[cur] * k + [prv] * k, out_specs=cur,
        out_shape=jax.ShapeDtypeStruct((n_tok, w), MXU),
        name=name, compiler_params=_cp((PAR, PAR)))(*curs, *prevs)


def _lru_gates(ux, halo, ext_ref, wc_ref, bc_ref, wr_ref, br_ref, wi_ref, bi_ref, lam_ref):
    tt = ux.shape[0]
    ext_ref[0:SUBLANE, :] = halo
    ext_ref[SUBLANE:, :] = ux
    xs = [ux] + [ext_ref[pl.ds(SUBLANE - k, tt), :] for k in range(1, LRU_CONV)]
    xc = bc_ref[...] + wc_ref[3:4, :] * xs[0] + wc_ref[2:3, :] * xs[1] + wc_ref[1:2, :] * xs[2] + wc_ref[0:1, :] * xs[3]
    pre_r, pre_i = [], []
    for blk in range(MIX_WIDTH // GATE_TILE):
        xb = xc[:, blk * GATE_TILE:(blk + 1) * GATE_TILE].astype(MXU)
        pre_r.append(_dot(xb, wr_ref[blk]))
        pre_i.append(_dot(xb, wi_ref[blk]))
    r = jax.nn.sigmoid(jnp.concatenate(pre_r, axis=-1) + br_ref[...])
    i = jax.nn.sigmoid(jnp.concatenate(pre_i, axis=-1) + bi_ref[...])
    nlam = -lam_ref[...]
    sp = jnp.maximum(nlam, 0.0) + jnp.log(1.0 + jnp.exp(-jnp.abs(nlam)))
    log_a = -LRU_C * r * sp
    a = jnp.exp(log_a)
    om = -jnp.tanh(log_a) * (a * a + 1.0)
    s = jnp.sqrt(om)
    return xs, xc, r, i, sp, a, s


def _lru_fwd(proj, wconv, bconv, wr, br, wi, bi, lam, bsz, *, name, q=None):
    n_tok = proj.shape[0]
    t = n_tok // bsz
    tt = _tile(t, 256, SUBLANE)
    nt = t // tt
    w = MIX_WIDTH
    ng = tt // SUBLANE

    def body(pg_ref, halo_ref, wc_ref, bc_ref, wr_ref, br_ref, wi_ref, bi_ref, lam_ref,
             y_ref, h_ref, ext_ref, a_ref, b_ref, carry_ref):
        ti = pl.program_id(1)

        @pl.when(ti == 0)
        def _():
            carry_ref[...] = jnp.zeros_like(carry_ref)

        gate = pg_ref[:, :w]
        ux = pg_ref[:, w:]
        halo = jnp.where(ti > 0, halo_ref[...], 0.0)
        _, xc, _, i, _, a, s = _lru_gates(ux, halo, ext_ref, wc_ref, bc_ref, wr_ref, br_ref, wi_ref, bi_ref, lam_ref)
        a_ref[...] = a
        b_ref[...] = s * (i * xc)
        row = lax.broadcasted_iota(jnp.int32, (SUBLANE, w), 0)

        def group(g, hprev):
            off = pl.multiple_of(g * SUBLANE, SUBLANE)
            ca = a_ref[pl.ds(off, SUBLANE), :]
            cb = b_ref[pl.ds(off, SUBLANE), :]
            for d in (1, 2, 4):
                a_sh = jnp.where(row >= d, pltpu.roll(ca, d, axis=0), 1.0)
                b_sh = jnp.where(row >= d, pltpu.roll(cb, d, axis=0), 0.0)
                cb = ca * b_sh + cb
                ca = ca * a_sh
            h = ca * hprev + cb
            b_ref[pl.ds(off, SUBLANE), :] = h
            return jnp.broadcast_to(h[SUBLANE - 1:SUBLANE, :], (SUBLANE, w))

        carry_ref[...] = lax.fori_loop(0, ng, group, carry_ref[...])
        h = b_ref[...]
        h_ref[...] = h
        y_ref[...] = (h * _gelu(gate)).astype(y_ref.dtype)

    vec = lambda r: pl.BlockSpec((r, w), lambda b, i: (0, 0))
    wspec = pl.BlockSpec((w // GATE_TILE, GATE_TILE, GATE_TILE), lambda b, i: (0, 0, 0))
    hb = tt // SUBLANE
    return _hosted_call(
        body, grid=(bsz, nt),
        in_specs=[pl.BlockSpec((tt, 2 * w), lambda b, i: (b * nt + i, 0)),
                  pl.BlockSpec((SUBLANE, w), lambda b, i: (jnp.maximum((b * nt + i) * hb - 1, 0), 1)),
                  vec(LRU_CONV), vec(1), wspec, vec(1), wspec, vec(1), vec(1)],
        out_specs=[pl.BlockSpec((tt, w), lambda b, i: (b * nt + i, 0)),
                   pl.BlockSpec((tt, w), lambda b, i: (b * nt + i, 0))],
        out_shape=[jax.ShapeDtypeStruct((n_tok, D_MODEL), MXU), jax.ShapeDtypeStruct((n_tok, w), F32)],
        scratch_shapes=[pltpu.VMEM((tt + SUBLANE, w), F32), pltpu.VMEM((tt, w), F32),
                        pltpu.VMEM((tt, w), F32), pltpu.VMEM((SUBLANE, w), F32)],
        args=(proj, proj, wconv, bconv, wr, br, wi, bi, lam), name=name, q=q, budget_us=HOST_US["lru_fwd"])


def _lru_bwd(proj, hs, dycat, wconv, bconv, wr, br, wi, bi, lam, bsz, *, name, q=None):
    n_tok = proj.shape[0]
    t = n_tok // bsz
    tt = _tile(t, 256, SUBLANE)
    nt = t // tt
    w = MIX_WIDTH
    ng = tt // SUBLANE
    nblk = w // GATE_TILE

    def body(pg_ref, halo_ref, h_ref, hhalo_ref, dy_ref, wc_ref, bc_ref, wr_ref, br_ref, wi_ref, bi_ref, lam_ref,
             dp_ref, dwc_ref, dbc_ref, dwr_ref, dbr_ref, dwi_ref, dbi_ref, dlam_ref,
             ext_ref, a_ref, c_ref, g_ref, gcarry_ref, xcarry_ref):
        bi_ = pl.program_id(0)
        ti = nt - 1 - pl.program_id(1)

        @pl.when((bi_ == 0) & (pl.program_id(1) == 0))
        def _():
            for r in (dwc_ref, dbc_ref, dwr_ref, dbr_ref, dwi_ref, dbi_ref, dlam_ref):
                r[...] = jnp.zeros_like(r)

        @pl.when(pl.program_id(1) == 0)
        def _():
            gcarry_ref[...] = jnp.zeros_like(gcarry_ref)
            xcarry_ref[...] = jnp.zeros_like(xcarry_ref)

        gate = pg_ref[:, :w]
        ux = pg_ref[:, w:]
        halo = jnp.where(ti > 0, halo_ref[...], 0.0)
        xs, xc, r, i, sp, a, s = _lru_gates(ux, halo, ext_ref, wc_ref, bc_ref, wr_ref, br_ref, wi_ref, bi_ref, lam_ref)
        h = h_ref[...]
        gl, dgl = _gelu_and_grad(gate)
        dy = dy_ref[...]
        dgate = dy * h * dgl
        row_t = lax.broadcasted_iota(jnp.int32, (tt, w), 0)
        g_ref[...] = dy * gl + jnp.where(row_t == tt - 1, gcarry_ref[0:1, :], 0.0)
        c_ref[...] = _shift_up(a, 1, row_t)
        row = lax.broadcasted_iota(jnp.int32, (SUBLANE, w), 0)

        a_ref[...] = a

        def group(k, gnext):
            off = pl.multiple_of((ng - 1 - k) * SUBLANE, SUBLANE)
            cc = c_ref[pl.ds(off, SUBLANE), :]
            cb = g_ref[pl.ds(off, SUBLANE), :]
            cb = cb + jnp.where(row == SUBLANE - 1, gnext, 0.0)
            cc = jnp.where(row == SUBLANE - 1, 0.0, cc)
            for d in (1, 2, 4):
                c_sh = jnp.where(row < SUBLANE - d, pltpu.roll(cc, SUBLANE - d, axis=0), 1.0)
                b_sh = jnp.where(row < SUBLANE - d, pltpu.roll(cb, SUBLANE - d, axis=0), 0.0)
                cb = cc * b_sh + cb
                cc = cc * c_sh
            g_ref[pl.ds(off, SUBLANE), :] = cb
            a0 = a_ref[pl.ds(off, SUBLANE), :]
            return jnp.broadcast_to(a0[0:1, :] * cb[0:1, :], (SUBLANE, w))

        gc = lax.fori_loop(0, ng, group, jnp.zeros((SUBLANE, w), F32))
        gcarry_ref[...] = gc
        gsc = g_ref[...]

        hhalo = jnp.where(ti > 0, hhalo_ref[SUBLANE - 1:SUBLANE, :], 0.0)
        hprev = jnp.where(row_t == 0, hhalo, pltpu.roll(h, 1, axis=0))
        gated = i * xc
        d_gated = gsc * s
        d_atot = gsc * hprev - (gsc * gated) * a / s
        d_loga = d_atot * a
        d_r = d_loga * (-LRU_C) * sp
        dlam_ref[...] += jnp.sum(d_loga * r, axis=0, keepdims=True) * (LRU_C * jax.nn.sigmoid(-lam_ref[...]))
        d_i = d_gated * xc
        d_xc = d_gated * i
        d_pr = d_r * r * (1.0 - r)
        d_pi = d_i * i * (1.0 - i)
        dbr_ref[...] += jnp.sum(d_pr, axis=0, keepdims=True)
        dbi_ref[...] += jnp.sum(d_pi, axis=0, keepdims=True)
        extra = []
        for blk in range(nblk):
            sl = slice(blk * GATE_TILE, (blk + 1) * GATE_TILE)
            xb = xc[:, sl].astype(MXU)
            dr_b = d_pr[:, sl].astype(MXU)
            di_b = d_pi[:, sl].astype(MXU)
            dwr_ref[blk] += _dot_tn(xb, dr_b)
            dwi_ref[blk] += _dot_tn(xb, di_b)
            extra.append(_dot_nt(dr_b, wr_ref[blk]) + _dot_nt(di_b, wi_ref[blk]))
        d_xc = d_xc + jnp.concatenate(extra, axis=-1)
        dbc_ref[...] += jnp.sum(d_xc, axis=0, keepdims=True)
        for k in range(LRU_CONV):
            dwc_ref[k:k + 1, :] += jnp.sum(d_xc * xs[LRU_CONV - 1 - k], axis=0, keepdims=True)
        ext_ref[0:tt, :] = d_xc
        ext_ref[tt:, :] = xcarry_ref[...]
        dux = wc_ref[3:4, :] * d_xc
        for k in range(LRU_CONV - 1):
            dux = dux + wc_ref[k:k + 1, :] * ext_ref[pl.ds(LRU_CONV - 1 - k, tt), :]
        xcarry_ref[...] = d_xc[0:SUBLANE, :]
        dp_ref[:, :w] = dgate.astype(dp_ref.dtype)
        dp_ref[:, w:] = dux.astype(dp_ref.dtype)

    vec = lambda r: pl.BlockSpec((r, w), lambda b, i: (0, 0))
    wspec = pl.BlockSpec((nblk, GATE_TILE, GATE_TILE), lambda b, i: (0, 0, 0))
    hb = tt // SUBLANE
    rblk = lambda b, i: b * nt + (nt - 1 - i)
    halo_idx = lambda b, i: jnp.maximum(rblk(b, i) * hb - 1, 0)
    wide = pl.BlockSpec((tt, 2 * w), lambda b, i: (rblk(b, i), 0))
    narrow = pl.BlockSpec((tt, w), lambda b, i: (rblk(b, i), 0))
    return _hosted_call(
        body, grid=(bsz, nt),
        in_specs=[wide, pl.BlockSpec((SUBLANE, w), lambda b, i: (halo_idx(b, i), 1)),
                  narrow, pl.BlockSpec((SUBLANE, w), lambda b, i: (halo_idx(b, i), 0)), narrow,
                  vec(LRU_CONV), vec(1), wspec, vec(1), wspec, vec(1), vec(1)],
        out_specs=[wide, vec(LRU_CONV), vec(1), wspec, vec(1), wspec, vec(1), vec(1)],
        out_shape=[jax.ShapeDtypeStruct((n_tok, 2 * w + MEM_WIDTH), MXU),
                   jax.ShapeDtypeStruct((LRU_CONV, w), F32), jax.ShapeDtypeStruct((1, w), F32),
                   jax.ShapeDtypeStruct((nblk, GATE_TILE, GATE_TILE), F32), jax.ShapeDtypeStruct((1, w), F32),
                   jax.ShapeDtypeStruct((nblk, GATE_TILE, GATE_TILE), F32), jax.ShapeDtypeStruct((1, w), F32),
                   jax.ShapeDtypeStruct((1, w), F32)],
        scratch_shapes=[pltpu.VMEM((tt + SUBLANE, w), F32), pltpu.VMEM((tt, w), F32), pltpu.VMEM((tt, w), F32),
                        pltpu.VMEM((tt, w), F32), pltpu.VMEM((SUBLANE, w), F32), pltpu.VMEM((SUBLANE, w), F32)],
        args=(proj, proj, hs, hs, dycat, wconv, bconv, wr, br, wi, bi, lam), name=name, q=q,
        budget_us=HOST_US["lru_bwd"])


def _gate_tiles(w):
    per = GATE_TILE // HEAD_DIM
    w4 = w.reshape(LRU_BLOCKS // per, per, HEAD_DIM, HEAD_DIM)
    eye = jnp.eye(per, dtype=w.dtype)
    return jnp.einsum("bnij,nm->bnimj", w4, eye).reshape(LRU_BLOCKS // per, GATE_TILE, GATE_TILE)


def _gate_blocks(t):
    per = GATE_TILE // HEAD_DIM
    t5 = t.reshape(LRU_BLOCKS // per, per, HEAD_DIM, per, HEAD_DIM)
    eye = jnp.eye(per, dtype=t.dtype)
    return jnp.einsum("bnimj,nm->bnij", t5, eye).reshape(LRU_BLOCKS, HEAD_DIM, HEAD_DIM)


def _row(v):
    return v.reshape(1, -1)


def _local_step(x, mem, target, p, wfull, push_grad, q):
    bsz, t, d = x.shape
    n = bsz * t
    x2d = x.reshape(n, d)
    tgt = target.reshape(n, d)
    mem2d = mem.reshape(bsz * MEM_LEN, d)
    wr_t = [_gate_tiles(p["w_rg_r"][j]).astype(MXU) for j in range(N_A)]
    wi_t = [_gate_tiles(p["w_rg_i"][j]).astype(MXU) for j in range(N_A)]

    mn = [_norm_fwd(mem2d, _row(p["g_mem"][l]), name=f"mem_norm{l}") for l in range(DEPTH)]
    mkv = [None] * DEPTH
    h = _norm_fwd(x2d, _row(p["g_mix_pre"][0]), name="in_norm")
    xin = x2d
    sv = []
    kv = hkv = None
    for l in range(DEPTH):
        s = {"xin": xin, "h": h}
        mkv[l] = _mm_nn(mn[l], wfull("w_mem_kv", l), name=f"mem_kv{l}", q=q)
        if l < N_A:
            proj = _mm_nn(h, wfull("w_in_a", l), name=f"in_proj{l}", q=q)
            ycat, hs = _lru_fwd(proj, p["w_conv_a"][l], _row(p["b_conv_a"][l]), wr_t[l], _row(p["b_rg_r"][l]),
                                wi_t[l], _row(p["b_rg_i"][l]), _row(p["lru_lambda"][l]), bsz, name=f"lru_fwd{l}", q=q)
            s["hs"] = hs
            qblk = 2 * MIX_WIDTH // MEM_WIDTH
        else:
            if l == N_A:
                kv = _mm_nn(hkv, wfull("w_kv", 0), name="kv_proj", q=q)
            proj = _mm_nn(h, wfull("w_in_b", l - N_A), name=f"in_proj{l}", q=q)
            ycat = _swa_fwd(proj, kv, p["sinks_b"][l - N_A], bsz, name=f"swa_fwd{l}", q=q)
            qblk = MIX_WIDTH // MEM_WIDTH
        ycat = _mem_attn_fwd(proj, qblk, mkv[l], ycat, bsz, name=f"mem_attn_fwd{l}", q=q)
        y = _mm_nn(ycat, wfull("w_mix_out", l), name=f"mix_out{l}", q=q)
        x1, (h2,) = _resid_norm_fwd(xin, y, _row(p["g_mix_post"][l]), [_row(p["g_ffn_pre"][l])], name=f"mix_resid{l}", q=q)
        up = _mm_nn(h2, wfull("w_ffn_up", l), name=f"ffn_up{l}", q=q)
        act, ug, uv = _ffn_act_fwd(up, p["w_ffn_conv"][l], _row(p["b_ffn_conv"][l]), bsz, name=f"ffn_act{l}", q=q)
        f = _mm_nn(act, wfull("w_ffn_down", l), name=f"ffn_down{l}", q=q)
        s.update(proj=proj, qblk=qblk, ycat=ycat, y=y, x1=x1, h2=h2, up=up, ug=ug, uv=uv, act=act, f=f)
        sv.append(s)
        if l < DEPTH - 1:
            g_pres = [_row(p["g_mix_pre"][l + 1])] + ([_row(p["g_kv"])] if l + 1 == N_A else [])
            xin, hn = _resid_norm_fwd(x1, f, _row(p["g_ffn_post"][l]), g_pres, name=f"ffn_resid{l}", q=q)
            h = hn[0]
            if l + 1 == N_A:
                hkv = hn[1]
        else:
            g_tot, sq = _loss_fwd(x1, f, _row(p["g_ffn_post"][l]), tgt, name="loss")

    gs = {k: [None] * DEPTH for k in ("g_mix_pre", "g_mix_post", "g_ffn_pre", "g_ffn_post", "g_mem",
                                       "w_ffn_conv", "b_ffn_conv")}
    ga = {k: [None] * N_A for k in ("w_conv_a", "b_conv_a", "w_rg_r", "b_rg_r", "w_rg_i", "b_rg_i", "lru_lambda")}
    gsink = [None] * (DEPTH - N_A)
    dkv_cur, dkv_prev = [], []
    g_tot, df, _, gs["g_ffn_post"][DEPTH - 1] = _resid_norm_bwd(
        g_tot, [], None, [], sv[-1]["f"], _row(p["g_ffn_post"][DEPTH - 1]), name="loss_bwd")
    grad_x = None
    for l in reversed(range(DEPTH)):
        s = sv[l]
        dact = _mm_nt(df, wfull("w_ffn_down", l), name=f"d_act{l}", q=q)
        push_grad("w_ffn_down", l, _mm_tn(s["act"], df, name=f"dw_down{l}", q=q))
        dug, duv, gs["w_ffn_conv"][l], gs["b_ffn_conv"][l] = _ffn_act_bwd(
            s["up"], s["ug"], s["uv"], dact, p["w_ffn_conv"][l], bsz, name=f"ffn_act_bwd{l}", q=q)
        dh2 = _mm_ffn_dh(dug, duv, wfull("w_ffn_up", l), name=f"d_h2_{l}", q=q)
        dwu = _mm_tn(s["h2"], dug, name=f"dw_up_g{l}", q=q, n_total=2 * D_FF)
        push_grad("w_ffn_up", l, _mm_tn(s["h2"], duv, name=f"dw_up_v{l}", q=q, out=dwu, n_total=2 * D_FF,
                                        col_block_offset=1))
        g1, dy, (gs["g_ffn_pre"][l],), gs["g_mix_post"][l] = _resid_norm_bwd(
            g_tot, [dh2], s["x1"], [_row(p["g_ffn_pre"][l])], s["y"], _row(p["g_mix_post"][l]), name=f"mix_resid_bwd{l}", q=q)
        dycat = _mm_nt(dy, wfull("w_mix_out", l), name=f"d_ycat{l}", q=q)
        push_grad("w_mix_out", l, _mm_tn(s["ycat"], dy, name=f"dw_mix_out{l}", q=q))
        if l < N_A:
            dproj, dwc, dbc, dwr, dbr, dwi, dbi, dlam = _lru_bwd(
                s["proj"], s["hs"], dycat, p["w_conv_a"][l], _row(p["b_conv_a"][l]), wr_t[l], _row(p["b_rg_r"][l]),
                wi_t[l], _row(p["b_rg_i"][l]), _row(p["lru_lambda"][l]), bsz, name=f"lru_bwd{l}", q=q)
            ga["w_conv_a"][l], ga["b_conv_a"][l], ga["lru_lambda"][l] = dwc, dbc[0], dlam[0]
            ga["w_rg_r"][l], ga["w_rg_i"][l] = _gate_blocks(dwr), _gate_blocks(dwi)
            ga["b_rg_r"][l] = dbr.reshape(LRU_BLOCKS, HEAD_DIM)
            ga["b_rg_i"][l] = dbi.reshape(LRU_BLOCKS, HEAD_DIM)
            w_in, j = "w_in_a", l
        else:
            dproj, dc, dp_, dsk = _swa_bwd(s["proj"], kv, p["sinks_b"][l - N_A], dycat, bsz, name=f"swa_bwd{l}", q=q)
            dkv_cur.append(dc)
            dkv_prev.append(dp_)
            gsink[l - N_A] = dsk[0, :SWA_HEADS]
            w_in, j = "w_in_b", l - N_A
        dproj, dmkv = _mem_attn_bwd(s["proj"], s["qblk"], mkv[l], dycat, dproj, bsz, name=f"mem_attn_bwd{l}", q=q)
        dh = _mm_nt(dproj, wfull(w_in, j), name=f"d_h{l}", q=q)
        push_grad(w_in, j, _mm_tn(s["h"], dproj, name=f"dw_in{l}", q=q))
        dmkv = dmkv.astype(MXU)
        dmn = _mm_nt(dmkv, wfull("w_mem_kv", l), name=f"d_mem_norm{l}", q=q)
        push_grad("w_mem_kv", l, _mm_tn(mn[l], dmkv, name=f"dw_mem_kv{l}", q=q))
        gs["g_mem"][l] = _norm_bwd_dg(dmn, mem2d, _row(p["g_mem"][l]), name=f"mem_norm_bwd{l}")
        dhs, g_pres = [dh], [_row(p["g_mix_pre"][l])]
        if l == N_A:
            dkv = _swa_dkv_combine(dkv_cur, dkv_prev, bsz, name="dkv_combine")
            dhs.append(_mm_nt(dkv, wfull("w_kv", 0), name="d_hkv", q=q))
            g_pres.append(_row(p["g_kv"]))
            push_grad("w_kv", 0, _mm_tn(hkv, dkv, name="dw_kv", q=q))
        if l > 0:
            g_tot, df, dgpre, gs["g_ffn_post"][l - 1] = _resid_norm_bwd(
                g1, dhs, s["xin"], g_pres, sv[l - 1]["f"], _row(p["g_ffn_post"][l - 1]), name=f"ffn_resid_bwd{l - 1}", q=q)
        else:
            grad_x, _, dgpre, _ = _resid_norm_bwd(g1, dhs, s["xin"], g_pres, None, None, name="in_norm_bwd", q=q)
        gs["g_mix_pre"][l] = dgpre[0]
        if l == N_A:
            g_kv = dgpre[1][0]

    grads = {}
    for k in ("g_mix_pre", "g_mix_post", "g_ffn_pre", "g_ffn_post", "g_mem", "b_ffn_conv"):
        grads[k] = jnp.concatenate(gs[k], axis=0)
    grads["w_ffn_conv"] = jnp.stack(gs["w_ffn_conv"])
    for k, v in ga.items():
        grads[k] = jnp.stack(v)
    grads["sinks_b"] = jnp.stack(gsink)
    grads["g_kv"] = g_kv
    return jnp.sum(sq), grad_x.reshape(bsz, t, d), grads


N_CHIP = 4
HALF_ALIGN = 16
MIN_PART_BYTES = 256 * 1024


def _full_shape(kind, shard_shape):
    l, r, c = shard_shape
    return {"row": (l, N_CHIP * r, c), "col": (l, r, N_CHIP * c), "slot": (N_CHIP, l, r, c)}[kind]


def _slot_view(ref, kind, shard_shape, s, hf, sub=(0, 1)):
    _, r, c = shard_shape
    rh = r // 2
    if hf is None:
        start, size = 0, r
    else:
        size = rh // sub[1]
        start = hf * rh + sub[0] * size
    if kind == "row":
        start = s * r + start
    if not isinstance(start, int):
        start = pl.multiple_of(start, HALF_ALIGN)
    rows = pl.ds(start, size)
    if kind == "row":
        return ref.at[:, rows, :]
    if kind == "col":
        return ref.at[:, rows, pl.ds(s * c, c)]
    return ref.at[s, :, rows, :]


def _half_view(ref, shard_shape, hf, sub=(0, 1)):
    rh = shard_shape[1] // 2
    size = rh // sub[1]
    return ref.at[:, pl.ds(pl.multiple_of(hf * rh + sub[0] * size, HALF_ALIGN), size), :]


def _with_slot(kind, s, fn):
    if kind != "col" or isinstance(s, int):
        fn(s)
        return
    for k in range(N_CHIP):
        @pl.when(s == k)
        def _(k=k):
            fn(k)


def _mesh_pos():
    return lax.axis_index("x"), lax.axis_index("y"), lax.axis_index("c")


def _other_chips(x, y):
    return [(1 - x, y), (x, 1 - y), (1 - x, 1 - y)]


ICI_BYTES_PER_US = 8.0e4
D2D_BYTES_PER_US = 4.0e5


class _Chunk:
    def __init__(self, group, cost, ins, out_shapes, alias, n_sem, start, finish, done, buffer=None, bind=None):
        self.group, self.cost, self.ins, self.out_shapes, self.alias, self.n_sem = group, cost, ins, out_shapes, alias, n_sem
        self.start, self.finish, self.done = start, finish, done
        self.buffer = buffer
        self.bind = bind

    def prepare(self):
        if self.bind is not None:
            self.bind(self)


LAST_GROUP = 1 << 30


class _CommQueue:
    def __init__(self):
        self.pending = []
        self.flushes = 0

    def push(self, chunk):
        self.pending.append(chunk)

    def take(self, budget_us):
        got, used = [], 0.0
        for ch in sorted(self.pending, key=lambda ch: (ch.group, -ch.cost)):
            if used + ch.cost <= budget_us and not self._shares_buffer(ch, got):
                got.append(ch)
                used += ch.cost
        return self._taken(got)

    @staticmethod
    def _shares_buffer(ch, others):
        return ch.buffer is not None and any(o.buffer is ch.buffer for o in others)

    def _taken(self, got):
        self.pending = [ch for ch in self.pending if ch not in got]
        for ch in got:
            ch.prepare()
        return got

    def flush(self, group=LAST_GROUP):
        while True:
            chunks = []
            for ch in self.pending:
                if ch.group <= group and not self._shares_buffer(ch, chunks):
                    chunks.append(ch)
            if not chunks:
                return
            _run_chunks(self._taken(chunks), name=f"comm_flush{self.flushes}")
            self.flushes += 1


def _run_chunks(chunks, *, name):
    ins = [a for ch in chunks for a in ch.ins]
    outs = [s for ch in chunks for s in ch.out_shapes]
    alias, offs = {}, []
    i0 = o0 = s0 = 0
    for ch in chunks:
        offs.append((i0, o0, s0))
        for ci, co in ch.alias.items():
            alias[i0 + ci] = o0 + co
        i0 += len(ch.ins)
        o0 += len(ch.out_shapes)
        s0 += ch.n_sem

    def body(*refs):
        send_sems, recv_sems = refs[i0 + o0:]
        for phase in ("start", "finish"):
            for ch, (a, b, s) in zip(chunks, offs):
                getattr(ch, phase)(refs[a:a + len(ch.ins)], refs[i0 + b:i0 + b + len(ch.out_shapes)],
                                   send_sems, recv_sems, s)

    hbm = pl.BlockSpec(memory_space=pl.ANY)
    res = pl.pallas_call(
        body, in_specs=[hbm] * i0, out_specs=[hbm] * o0, out_shape=outs,
        scratch_shapes=[pltpu.SemaphoreType.DMA((s0,)), pltpu.SemaphoreType.DMA((s0,))],
        input_output_aliases=alias, name=name, compiler_params=pltpu.CompilerParams(has_side_effects=True))(*ins)
    for ch, (_, b, _) in zip(chunks, offs):
        ch.done(list(res[b:b + len(ch.out_shapes)]))


def _remote(src, dst, send_sems, recv_sems, k, dev):
    return pltpu.make_async_remote_copy(src_ref=src, dst_ref=dst, send_sem=send_sems.at[k], recv_sem=recv_sems.at[k],
                                        device_id=dev, device_id_type=MESH_T)


def _gather_chunks(q, group, kind, shard, l, ready):
    _, r, c = shard.shape
    shp = (1, r, c)
    rh = r // 2
    parts = max(p for p in (4, 2, 1)
                if (rh // p) % HALF_ALIGN == 0 and (p == 1 or (rh // p) * c * shard.dtype.itemsize >= MIN_PART_BYTES))
    part_bytes = (rh // parts) * c * shard.dtype.itemsize
    full_type = jax.ShapeDtypeStruct(_full_shape(kind, shp), shard.dtype)
    state = {"full": None, "parts_done": 0}

    def bind_first(ch):
        ch.ins, ch.alias = ([shard], {}) if state["full"] is None else ([shard, state["full"]], {1: 0})

    def bind_full(ch):
        ch.ins = [state["full"]]

    def make_part(p):
        sub = (p, parts)

        def any_part(full):
            return _slot_view(full, kind, shp, 0, 0, sub)

        def start1(ins, outs, ss, rs, b):
            x, y, c_ = _mesh_pos()
            src, full = ins[0].at[pl.ds(l, 1)], outs[0]
            if p == 0:
                _with_slot(kind, 2 * x + y, lambda s: pltpu.make_async_copy(
                    src, _slot_view(full, kind, shp, s, None), ss.at[b + N_CHIP - 1]).start())
            for j, (ox, oy) in enumerate(_other_chips(x, y)):
                _with_slot(kind, 2 * x + y, lambda s, j=j, ox=ox, oy=oy: _remote(
                    _half_view(src, shp, c_, sub), _slot_view(full, kind, shp, s, c_, sub), ss, rs, b + j,
                    (ox, oy, c_)).start())

        def finish1(ins, outs, ss, rs, b):
            x, y, c_ = _mesh_pos()
            h = any_part(outs[0])
            for j in range(N_CHIP - 1):
                _remote(h, h, ss, rs, b + j, (x, y, 1 - c_)).wait()
            if p == 0:
                pltpu.make_async_copy(ins[0].at[pl.ds(l, 1)], _slot_view(outs[0], kind, shp, 0, None),
                                      ss.at[b + N_CHIP - 1]).wait()

        def start2(ins, outs, ss, rs, b):
            x, y, c_ = _mesh_pos()
            for j, (ox, oy) in enumerate(_other_chips(x, y)):
                def forward(s, j=j):
                    v = _slot_view(outs[0], kind, shp, s, c_, sub)
                    _remote(v, v, ss, rs, b + j, (x, y, 1 - c_)).start()
                _with_slot(kind, 2 * ox + oy, forward)

        def finish2(ins, outs, ss, rs, b):
            x, y, c_ = _mesh_pos()
            h = any_part(outs[0])
            for j in range(N_CHIP - 1):
                _remote(h, h, ss, rs, b + j, (x, y, 1 - c_)).wait()

        def done2(outs):
            state["full"] = outs[0]
            state["parts_done"] += 1
            if state["parts_done"] == parts:
                ready(outs[0])

        def done1(outs):
            state["full"] = outs[0]
            q.push(_Chunk(group, 3 * part_bytes / D2D_BYTES_PER_US, None, [full_type], {0: 0}, N_CHIP - 1,
                          start2, finish2, done2, buffer=state, bind=bind_full))

        return _Chunk(group, 3 * part_bytes / ICI_BYTES_PER_US, None, [full_type], None,
                      N_CHIP if p == 0 else N_CHIP - 1, start1, finish1, done1, buffer=state, bind=bind_first)

    for p in range(parts):
        q.push(make_part(p))


def _reduce_scatter_chunks(q, kind, grad, shard_shape, pos, name, ready):
    _, r, c = shard_shape
    shp = (1, r, c)
    rh = r // 2

    def start1(ins, outs, ss, rs, b):
        x, y, c_ = _mesh_pos()
        for s in range(N_CHIP):
            _remote(_slot_view(ins[0], kind, shp, s, 1 - c_), outs[0].at[s], ss, rs, b + s, (x, y, 1 - c_)).start()

    def finish1(ins, outs, ss, rs, b):
        x, y, c_ = _mesh_pos()
        for s in range(N_CHIP):
            _remote(outs[0].at[s], outs[0].at[s], ss, rs, b + s, (x, y, 1 - c_)).wait()

    def start2(ins, outs, ss, rs, b):
        x, y, c_ = _mesh_pos()
        for j, (ox, oy) in enumerate(_other_chips(x, y)):
            _remote(ins[0].at[2 * ox + oy], outs[0].at[j], ss, rs, b + j, (ox, oy, c_)).start()

    def finish2(ins, outs, ss, rs, b):
        x, y, c_ = _mesh_pos()
        for j in range(N_CHIP - 1):
            _remote(outs[0].at[j], outs[0].at[j], ss, rs, b + j, (x, y, 1 - c_)).wait()

    def start3(ins, outs, ss, rs, b):
        x, y, c_ = _mesh_pos()
        v = _half_view(outs[0], shp, c_)
        _remote(v, v, ss, rs, b, (x, y, 1 - c_)).start()

    def finish3(ins, outs, ss, rs, b):
        x, y, c_ = _mesh_pos()
        v = _half_view(outs[0], shp, c_)
        _remote(v, v, ss, rs, b, (x, y, 1 - c_)).wait()

    def done2(pair, outs):
        half = _rs_chip_add(pair, outs[0], shp, pos, name=f"rs_chip_add_{name}")
        q.push(_Chunk(LAST_GROUP, rh * c * 4 / D2D_BYTES_PER_US, [half], [jax.ShapeDtypeStruct(half.shape, half.dtype)],
                      {0: 0}, 1, start3, finish3, lambda o: ready(o[0])))

    def done1(outs):
        pair, wire = _rs_pair_add(grad, outs[0], kind, shp, pos, name=f"rs_pair_add_{name}")
        q.push(_Chunk(LAST_GROUP, 3 * rh * c * wire.dtype.itemsize / ICI_BYTES_PER_US, [wire],
                      [jax.ShapeDtypeStruct((N_CHIP - 1, 1, rh, c), wire.dtype)], {}, N_CHIP - 1,
                      start2, finish2, functools.partial(done2, pair)))

    q.push(_Chunk(LAST_GROUP, N_CHIP * rh * c * 4 / D2D_BYTES_PER_US, [grad],
                  [jax.ShapeDtypeStruct((N_CHIP, 1, rh, c), F32)], {}, N_CHIP, start1, finish1, done1))


def _allgather8(vec, *, name):
    r = vec.shape[0]
    n_dev = 8

    def body(v_ref, buf, send_sems, recv_sems):
        x, y, c = _mesh_pos()
        me = 4 * x + 2 * y + c
        copies = []
        for k in range(1, n_dev):
            kx, ky, kc = (k >> 2) & 1, (k >> 1) & 1, k & 1
            peer = ((1 - x) if kx else x, (1 - y) if ky else y, (1 - c) if kc else c)
            cp = _remote(v_ref, buf.at[me], send_sems, recv_sems, k - 1, peer)
            cp.start()
            copies.append(cp)
        buf[me] = v_ref[...]
        for cp in copies:
            cp.wait()

    vm = pl.BlockSpec(memory_space=pltpu.VMEM)
    return pl.pallas_call(
        body, in_specs=[vm], out_specs=vm, out_shape=jax.ShapeDtypeStruct((n_dev, r, LANE), F32),
        scratch_shapes=[pltpu.SemaphoreType.DMA((n_dev - 1,)), pltpu.SemaphoreType.DMA((n_dev - 1,))],
        name=name, compiler_params=pltpu.CompilerParams(has_side_effects=True, vmem_limit_bytes=VMEM_LIMIT_V7X))(vec)


def _allreduce8(vec, *, name):
    r = vec.shape[0]
    rh = r // 2

    def body(v_ref, o_ref, sib_ref, chips_ref, send_sems, recv_sems):
        x, y, c = _mesh_pos()
        sib = (x, y, 1 - c)
        me = 2 * x + y
        pair = _remote(v_ref, sib_ref, send_sems, recv_sems, 0, sib)
        pair.start()
        pair.wait()
        rows = pl.ds(pl.multiple_of(c * rh, SUBLANE), rh)
        chips_ref[me] = v_ref[rows, :] + sib_ref[rows, :]
        copies = []
        for j, (ox, oy) in enumerate(_other_chips(x, y)):
            cp = _remote(chips_ref.at[me], chips_ref.at[me], send_sems, recv_sems, 1 + j, (ox, oy, c))
            cp.start()
            copies.append(cp)
        for cp in copies:
            cp.wait()
        acc = chips_ref[0]
        for s in range(1, N_CHIP):
            acc = acc + chips_ref[s]
        o_ref[rows, :] = acc
        swap = _remote(o_ref.at[rows, :], o_ref.at[rows, :], send_sems, recv_sems, N_CHIP, sib)
        swap.start()
        swap.wait()

    vm = pl.BlockSpec(memory_space=pltpu.VMEM)
    return pl.pallas_call(
        body, in_specs=[vm], out_specs=vm, out_shape=jax.ShapeDtypeStruct((r, LANE), F32),
        scratch_shapes=[pltpu.VMEM((r, LANE), F32), pltpu.VMEM((N_CHIP, rh, LANE), F32),
                        pltpu.SemaphoreType.DMA((N_CHIP + 1,)), pltpu.SemaphoreType.DMA((N_CHIP + 1,))],
        name=name, compiler_params=pltpu.CompilerParams(has_side_effects=True, vmem_limit_bytes=VMEM_LIMIT_V7X))(vec)


def _rs_pair_add(g, recv, kind, shape, pos, *, name):
    l, r, c = shape
    rh = r // 2
    if kind == "row":
        gspec = pl.BlockSpec((None, rh, c), lambda s, i, pos: (i, 2 * s + pos[0], 0))
    elif kind == "col":
        gspec = pl.BlockSpec((None, rh, c), lambda s, i, pos: (i, pos[0], s))
    else:
        gspec = pl.BlockSpec((None, None, rh, c), lambda s, i, pos: (s, i, pos[0], 0))
    pspec = pl.BlockSpec((None, None, rh, c), lambda s, i, pos: (s, i, 0, 0))

    def body(pos_ref, g_ref, r_ref, p_ref, pw_ref):
        del pos_ref
        v = g_ref[...] + r_ref[...]
        p_ref[...] = v
        pw_ref[...] = v.astype(pw_ref.dtype)

    return pl.pallas_call(
        body,
        grid_spec=pltpu.PrefetchScalarGridSpec(
            num_scalar_prefetch=1, grid=(N_CHIP, l), in_specs=[gspec, pspec], out_specs=[pspec, pspec]),
        out_shape=[jax.ShapeDtypeStruct((N_CHIP, l, rh, c), F32), jax.ShapeDtypeStruct((N_CHIP, l, rh, c), MXU)],
        name=name, compiler_params=_cp((PAR, PAR)))(pos, g, recv)


def _rs_chip_add(p, recv, shape, pos, *, name):
    l, r, c = shape
    rh = r // 2

    def body(pos_ref, p_ref, r_ref, o_ref):
        del pos_ref
        acc = p_ref[...]
        for j in range(N_CHIP - 1):
            acc = acc + r_ref[j].astype(F32)
        o_ref[...] = acc

    return pl.pallas_call(
        body,
        grid_spec=pltpu.PrefetchScalarGridSpec(
            num_scalar_prefetch=1, grid=(l,),
            in_specs=[pl.BlockSpec((None, None, rh, c), lambda i, pos: (pos[1], i, 0, 0)),
                      pl.BlockSpec((N_CHIP - 1, None, rh, c), lambda i, pos: (0, i, 0, 0))],
            out_specs=pl.BlockSpec((None, rh, c), lambda i, pos: (i, pos[0], 0))),
        out_shape=jax.ShapeDtypeStruct((l, r, c), F32),
        name=name, compiler_params=_cp((PAR,)))(pos, p, recv)


ADAM_BLOCK_ELEMS = 384 * 1024


def _adam_math(w, g, m, v):
    c1 = 1.0 / (1.0 - ADAM_B1 ** ADAM_STEP)
    c2 = 1.0 / (1.0 - ADAM_B2 ** ADAM_STEP)
    nm = ADAM_B1 * m + (1.0 - ADAM_B1) * g
    nv = ADAM_B2 * v + (1.0 - ADAM_B2) * (g * g)
    return -ADAM_LR * ((nm * c1) / (jnp.sqrt(nv * c2) + ADAM_EPS) + ADAM_WD * w), nm, nv


def _adamw_layer(w, g, m, v, outs, l, *, name):
    _, r, c = w.shape
    tr = _tile(r, max(SUBLANE, ADAM_BLOCK_ELEMS // c // SUBLANE * SUBLANE), SUBLANE)

    def body(w_ref, g_ref, m_ref, v_ref, *rest):
        go_ref, d_ref, nm_ref, nv_ref = rest[4:]
        gg = g_ref[...]
        go_ref[...] = gg
        d_ref[...], nm_ref[...], nv_ref[...] = _adam_math(w_ref[...], gg, m_ref[...], v_ref[...])

    lay = pl.BlockSpec((None, tr, c), lambda j: (l, j, 0))
    hbm = pl.BlockSpec(memory_space=pl.ANY)
    return pl.pallas_call(
        body, grid=(r // tr,),
        in_specs=[lay, pl.BlockSpec((None, tr, c), lambda j: (0, j, 0)), lay, lay] + [hbm] * 4,
        out_specs=[lay] * 4, out_shape=[jax.ShapeDtypeStruct(w.shape, F32)] * 4,
        input_output_aliases={4 + i: i for i in range(4)},
        name=name, compiler_params=_cp((PAR,)))(w, g, m, v, *outs)


def _adamw(w, g, m, v, *, name):
    shape = w.shape
    if w.ndim == 2:
        w, g, m, v = (a[None] for a in (w, g, m, v))
    l, r, c = w.shape
    tr = _tile(r, max(SUBLANE, ADAM_BLOCK_ELEMS // c // SUBLANE * SUBLANE), SUBLANE)

    def body(w_ref, g_ref, m_ref, v_ref, d_ref, nm_ref, nv_ref):
        d_ref[...], nm_ref[...], nv_ref[...] = _adam_math(w_ref[...], g_ref[...], m_ref[...], v_ref[...])

    spec = pl.BlockSpec((None, tr, c), lambda i, j: (i, j, 0))
    outs = pl.pallas_call(
        body, grid=(l, r // tr), in_specs=[spec] * 4, out_specs=[spec] * 3,
        out_shape=[jax.ShapeDtypeStruct((l, r, c), F32)] * 3,
        name=name, compiler_params=_cp((PAR, PAR)))(w, g, m, v)
    return tuple(o.reshape(shape) for o in outs)


PACK_ROWS = 512 * LANE


def _pack(arrays):
    flat = jnp.concatenate([a.reshape(-1).astype(F32) for a in arrays])
    pad = (-flat.shape[0]) % PACK_ROWS
    return jnp.pad(flat, (0, pad)).reshape(-1, LANE)


def _unpack(packed, shapes):
    flat = packed.reshape(-1)
    out, off = [], 0
    for s in shapes:
        size = int(np.prod(s))
        out.append(flat[off:off + size].reshape(s))
        off += size
    return out


BIG = (("w_mem_kv", "row"), ("w_mix_out", "row"), ("w_ffn_up", "col"), ("w_ffn_down", "row"),
       ("w_in_a", "slot"), ("w_in_b", "row"), ("w_kv", "row"))
SMALL_SHARDED = (("w_ffn_conv", 2), ("w_conv_a", 2), ("b_conv_a", 1), ("lru_lambda", 1))
SMALL_REPLICATED = ("g_mix_pre", "g_mix_post", "g_ffn_pre", "g_ffn_post", "g_mem", "b_ffn_conv",
                    "w_rg_r", "b_rg_r", "w_rg_i", "b_rg_i", "sinks_b", "g_kv")
WEIGHTS = ("g_mix_pre", "g_mix_post", "g_ffn_pre", "g_ffn_post", "g_mem", "w_mem_kv", "w_mix_out", "w_ffn_up",
           "w_ffn_conv", "b_ffn_conv", "w_ffn_down", "w_in_a", "w_conv_a", "b_conv_a", "w_rg_r", "b_rg_r", "w_rg_i",
           "b_rg_i", "lru_lambda", "w_in_b", "sinks_b", "g_kv", "w_kv")


def _slot_to_cols(a):
    s, l, r, c = a.shape
    return a.transpose(1, 2, 0, 3).reshape(l, r, s * c)


def _cols_to_slot(a):
    l, r, c4 = a.shape
    return a.reshape(l, r, N_CHIP, c4 // N_CHIP).transpose(2, 0, 1, 3)


GROUPS_PER_LAYER = 8


def _layer_weights(layer):
    names = [("w_mem_kv", layer), ("w_in_a", layer) if layer < N_A else ("w_in_b", layer - N_A)]
    if layer == N_A:
        names.append(("w_kv", 0))
    return names + [("w_mix_out", layer), ("w_ffn_up", layer), ("w_ffn_down", layer)]


def _train_step(x, mem, target, w, m, v):
    xi, yi, ci = _mesh_pos()
    chip = 2 * xi + yi
    pos = jnp.stack([ci, chip]).astype(jnp.int32)

    q = _CommQueue()
    kinds = dict(BIG)
    as3 = lambda a: a if a.ndim == 3 else a[None]
    w3, m3, v3 = ({k: as3(d[k]) for k, _ in BIG} for d in (w, m, v))
    shards = {k: w3[k].astype(MXU) for k, _ in BIG}

    gathered = {}

    def on_gathered(k, l, full):
        gathered[k, l] = _slot_to_cols(full) if kinds[k] == "slot" else full

    group_of = {}

    for layer in range(DEPTH):
        for i, (k, l) in enumerate(_layer_weights(layer)):
            group_of[k, l] = layer * GROUPS_PER_LAYER + i
            _gather_chunks(q, group_of[k, l], kinds[k], shards[k], l, functools.partial(on_gathered, k, l))

    def wfull(k, l):
        if (k, l) not in gathered:
            q.flush(group_of[k, l])
        return gathered[k, l]

    q.flush(1)

    big_out = {k: [lax.empty(w3[k].shape, F32) for _ in range(4)] for k, _ in BIG}

    def on_reduced(k, l, g):
        big_out[k] = _adamw_layer(w3[k], g, m3[k], v3[k], big_out[k], l, name=f"adamw_{k}{l}")

    def push_grad(k, l, g):
        if kinds[k] == "slot":
            g = _cols_to_slot(g)
        _reduce_scatter_chunks(q, kinds[k], g, (1,) + w3[k].shape[1:], pos, f"{k}{l}", functools.partial(on_reduced, k, l))

    small_shapes = [w[k].shape for k, _ in SMALL_SHARDED]
    stacked = _allgather8(_pack([w[k] for k, _ in SMALL_SHARDED]), name="gather_small")
    per_chip = [_unpack(stacked[2 * s], small_shapes) for s in range(N_CHIP)]
    p = {k: w[k] for k in SMALL_REPLICATED}
    for i, (k, axis) in enumerate(SMALL_SHARDED):
        p[k] = jnp.concatenate([per_chip[s][i] for s in range(N_CHIP)], axis=axis)

    sq, grad_x, g = _local_step(x, mem, target, p, wfull, push_grad, q)
    loss = lax.psum(0.5 * sq / D_MODEL, ("x", "y", "c"))
    q.flush()

    small_names = [k for k, _ in SMALL_SHARDED] + list(SMALL_REPLICATED)
    summed = _allreduce8(_pack([g[k] for k in small_names]), name="allreduce_small")
    gsum = dict(zip(small_names, _unpack(summed, [p[k].shape for k in small_names])))
    for k, axis in SMALL_SHARDED:
        gsum[k] = lax.dynamic_slice_in_dim(gsum[k], chip * w[k].shape[axis], w[k].shape[axis], axis)

    delta, new_m, new_v = {}, {}, {}
    for k, _ in BIG:
        gsum[k], delta[k], new_m[k], new_v[k] = (o.reshape(w[k].shape) for o in big_out[k])
    packed = [_pack([d[k] for k in small_names]) for d in (w, gsum, m, v)]
    outs = _adamw(*packed, name="adamw_small")
    for d, o in zip((delta, new_m, new_v), outs):
        d.update(zip(small_names, _unpack(o, [w[k].shape for k in small_names])))
    return (loss, grad_x, *[gsum[k] for k in WEIGHTS], *[delta[k] for k in WEIGHTS],
            *[new_m[k] for k in WEIGHTS], *[new_v[k] for k in WEIGHTS])


def kernel(x, mem, g_mix_pre, g_mix_post, g_ffn_pre, g_ffn_post, g_mem, w_mem_kv, w_mix_out, w_ffn_up, w_ffn_conv, b_ffn_conv, w_ffn_down, w_in_a, w_conv_a, b_conv_a, w_rg_r, b_rg_r, w_rg_i, b_rg_i, lru_lambda, w_in_b, sinks_b, g_kv, w_kv, loss_target, m_g_mix_pre, m_g_mix_post, m_g_ffn_pre, m_g_ffn_post, m_g_mem, m_w_mem_kv, m_w_mix_out, m_w_ffn_up, m_w_ffn_conv, m_b_ffn_conv, m_w_ffn_down, m_w_in_a, m_w_conv_a, m_b_conv_a, m_w_rg_r, m_b_rg_r, m_w_rg_i, m_b_rg_i, m_lru_lambda, m_w_in_b, m_sinks_b, m_g_kv, m_w_kv, v_g_mix_pre, v_g_mix_post, v_g_ffn_pre, v_g_ffn_post, v_g_mem, v_w_mem_kv, v_w_mix_out, v_w_ffn_up, v_w_ffn_conv, v_b_ffn_conv, v_w_ffn_down, v_w_in_a, v_w_conv_a, v_b_conv_a, v_w_rg_r, v_b_rg_r, v_w_rg_i, v_b_rg_i, v_lru_lambda, v_w_in_b, v_sinks_b, v_g_kv, v_w_kv):
    args = (g_mix_pre, g_mix_post, g_ffn_pre, g_ffn_post, g_mem, w_mem_kv, w_mix_out, w_ffn_up, w_ffn_conv, b_ffn_conv, w_ffn_down, w_in_a, w_conv_a, b_conv_a, w_rg_r, b_rg_r, w_rg_i, b_rg_i, lru_lambda, w_in_b, sinks_b, g_kv, w_kv)
    ms = (m_g_mix_pre, m_g_mix_post, m_g_ffn_pre, m_g_ffn_post, m_g_mem, m_w_mem_kv, m_w_mix_out, m_w_ffn_up, m_w_ffn_conv, m_b_ffn_conv, m_w_ffn_down, m_w_in_a, m_w_conv_a, m_b_conv_a, m_w_rg_r, m_b_rg_r, m_w_rg_i, m_b_rg_i, m_lru_lambda, m_w_in_b, m_sinks_b, m_g_kv, m_w_kv)
    vs = (v_g_mix_pre, v_g_mix_post, v_g_ffn_pre, v_g_ffn_post, v_g_mem, v_w_mem_kv, v_w_mix_out, v_w_ffn_up, v_w_ffn_conv, v_b_ffn_conv, v_w_ffn_down, v_w_in_a, v_w_conv_a, v_b_conv_a, v_w_rg_r, v_b_rg_r, v_w_rg_i, v_b_rg_i, v_lru_lambda, v_w_in_b, v_sinks_b, v_g_kv, v_w_kv)
    return _train_step(x, mem, loss_target, dict(zip(WEIGHTS, args)), dict(zip(WEIGHTS, ms)), dict(zip(WEIGHTS, vs)))
```

```python
import functools
import math

import numpy as np
import jax
import jax.numpy as jnp
from jax import lax
from jax.experimental import pallas as pl
from jax.experimental.pallas import tpu as pltpu

F32 = jnp.float32
MXU = jnp.bfloat16

D_MODEL = 1024
HEAD_DIM = 64
MEM_LEN = 256
MEM_HEADS = 4
MEM_WIDTH = MEM_HEADS * HEAD_DIM
MIX_WIDTH = D_MODEL - MEM_WIDTH
LRU_BLOCKS = MIX_WIDTH // HEAD_DIM
LRU_CONV = 4
LRU_C = 8.0
SWA_HEADS = MIX_WIDTH // HEAD_DIM
SWA_KV_HEADS = 4
SWA_GROUP = SWA_HEADS // SWA_KV_HEADS
WINDOW = 128
D_FF = 2816
FFN_CONV = 3
EPS = 1e-6
DEPTH = 4
N_A = 2

ADAM_LR = 0.001
ADAM_B1 = 0.9
ADAM_B2 = 0.999
ADAM_EPS = 1e-08
ADAM_WD = 0.01
ADAM_STEP = 10

VMEM_LIMIT_V7X = 56 * 1024 * 1024
LANE = 128
SUBLANE = 8
GATE_TILE = 256
MESH_T = pl.DeviceIdType.MESH


def _alibi_slopes(n):
    def pow2_slopes(m):
        start = 2.0 ** (-8.0 / m)
        return [start ** (i + 1) for i in range(m)]
    c = 2 ** int(math.floor(math.log2(n)))
    s = pow2_slopes(c)
    if c != n:
        s = s + pow2_slopes(2 * c)[0::2][: n - c]
    return [float(np.float32(v)) for v in s]


SLOPES = _alibi_slopes(SWA_HEADS)


def _tile(n, cap, mult=LANE):
    best = None
    for t in range(mult, min(n, cap) + 1, mult):
        if n % t == 0:
            best = t
    return best if best is not None else n


def _cp(sem):
    return pltpu.CompilerParams(dimension_semantics=sem, vmem_limit_bytes=VMEM_LIMIT_V7X)


MM_VMEM_BUDGET = 40 * 1024 * 1024
HBM_BYTES_PER_US_V7X = 3.0e6
GRID_STEP_US = 0.35


def _divisors(n, mult):
    return [t for t in range(mult, n + 1, mult) if n % t == 0] or [n]


def _mm_tiles(m, k, n, out_bytes):
    best = None
    for tm in _divisors(m, 256):
        for tn in _divisors(n, LANE):
            vmem = 2 * (tm * k * 2 + k * tn * 2 + tm * tn * out_bytes)
            if vmem > MM_VMEM_BUDGET:
                continue
            steps = (m // tm) * (n // tn)
            b_reads = 1 if tn == n else m // tm
            traffic = m * k * 2 + k * n * 2 * b_reads + m * n * out_bytes
            first = tm * k * 2 + k * tn * 2
            cost = (traffic + first) / HBM_BYTES_PER_US_V7X + steps * GRID_STEP_US
            if best is None or cost < best[0]:
                best = (cost, tm, tn)
    return best[1], best[2]


def _mm_tn_tiles(k, m, n):
    best = None
    for tm in _divisors(m, LANE):
        for tn in _divisors(n, LANE):
            for tk in _divisors(k, 512):
                vmem = 2 * (tk * tm * 2 + tk * tn * 2 + tm * tn * 4)
                if vmem > MM_VMEM_BUDGET:
                    continue
                steps = (m // tm) * (n // tn) * (k // tk)
                traffic = k * m * 2 * (n // tn) + k * n * 2 * (m // tm) + m * n * 4
                cost = traffic / HBM_BYTES_PER_US_V7X + steps * GRID_STEP_US
                if best is None or cost < best[0]:
                    best = (cost, tk, tm, tn)
    return best[1], best[2], best[3]


ARB = "arbitrary"
PAR = "parallel"


def _rms_fwd(x, g):
    r = lax.rsqrt(jnp.mean(x * x, axis=-1, keepdims=True) + EPS)
    return x * r * g


def _rms_bwd(dy, x, g):
    r = lax.rsqrt(jnp.mean(x * x, axis=-1, keepdims=True) + EPS)
    xh = x * r
    gdy = dy * g
    dx = r * (gdy - xh * jnp.mean(gdy * xh, axis=-1, keepdims=True))
    dg = jnp.sum(dy * xh, axis=0, keepdims=True)
    return dx, dg


_GELU_K = math.sqrt(2.0 / math.pi)
_GELU_C = 0.044715


def _gelu(x):
    t = jnp.tanh(_GELU_K * (x + _GELU_C * x * x * x))
    return 0.5 * x * (1.0 + t)


def _gelu_and_grad(x):
    x2 = x * x
    t = jnp.tanh(_GELU_K * (x + _GELU_C * x2 * x))
    g = 0.5 * x * (1.0 + t)
    dg = 0.5 * (1.0 + t) + 0.5 * x * (1.0 - t * t) * (_GELU_K * (1.0 + 3.0 * _GELU_C * x2))
    return g, dg


def _shift_down(x, k, row):
    return jnp.where(row >= k, pltpu.roll(x, k, axis=0), 0.0)


def _shift_up(x, k, row):
    n = x.shape[0]
    return jnp.where(row < n - k, pltpu.roll(x, n - k, axis=0), 0.0)


def _shift_down_edge(x, k):
    r = pltpu.roll(x, k, axis=0)
    row = lax.broadcasted_iota(jnp.int32, (SUBLANE, x.shape[1]), 0)
    return jnp.concatenate([jnp.where(row >= k, r[:SUBLANE], 0.0), r[SUBLANE:]], axis=0)


def _shift_up_edge(x, k):
    n = x.shape[0]
    r = pltpu.roll(x, n - k, axis=0)
    row = lax.broadcasted_iota(jnp.int32, (SUBLANE, x.shape[1]), 0)
    return jnp.concatenate([r[:n - SUBLANE], jnp.where(row < SUBLANE - k, r[n - SUBLANE:], 0.0)], axis=0)


def _dot(a, b):
    return jnp.dot(a, b, preferred_element_type=F32)


def _dot_nt(a, b):
    return lax.dot_general(a, b, (((1,), (1,)), ((), ())), preferred_element_type=F32)


def _dot_tn(a, b):
    return lax.dot_general(a, b, (((0,), (0,)), ((), ())), preferred_element_type=F32)


MXU_FLOPS_PER_US = 7.0e8
HOST_US = {"lru_fwd": 44.0, "lru_bwd": 94.0, "swa_fwd": 60.0, "swa_bwd": 160.0, "mem_attn_fwd": 21.0,
           "mem_attn_bwd": 33.0, "ffn_act": 70.0, "ffn_act_bwd": 100.0, "resid": 22.0, "resid_bwd": 33.0}


def _hosted_call(body, *, grid, in_specs, out_specs, out_shape, args, name, aliases=None, scratch_shapes=(),
                 q=None, flops=0.0, budget_us=0.0):
    chunks = q.take(flops / MXU_FLOPS_PER_US + budget_us) if q is not None else []
    if not chunks:
        return pl.pallas_call(
            body, grid=grid, in_specs=in_specs, out_specs=out_specs, out_shape=out_shape,
            scratch_shapes=list(scratch_shapes), input_output_aliases=aliases or {}, name=name,
            compiler_params=_cp((ARB,) * len(grid)))(*args)
    single = not isinstance(out_shape, (list, tuple))
    o_shapes = [out_shape] if single else list(out_shape)
    o_specs = [out_specs] if single else list(out_specs)
    n_in, n_out, n_scr = len(args), len(o_shapes), len(scratch_shapes)
    c_ins = [a for ch in chunks for a in ch.ins]
    c_outs = [s for ch in chunks for s in ch.out_shapes]
    alias = dict(aliases or {})
    in_off, out_off, sem_off = [], [], []
    i0 = o0 = s0 = 0
    for ch in chunks:
        in_off.append(i0)
        out_off.append(o0)
        sem_off.append(s0)
        for ci, co in ch.alias.items():
            alias[n_in + i0 + ci] = n_out + o0 + co
        i0 += len(ch.ins)
        o0 += len(ch.out_shapes)
        s0 += ch.n_sem

    def wrapped(*refs):
        ins = refs[:n_in]
        cin = refs[n_in:n_in + i0]
        outs = refs[n_in + i0:n_in + i0 + n_out]
        cout = refs[n_in + i0 + n_out:n_in + i0 + n_out + o0]
        scr = refs[n_in + i0 + n_out + o0:n_in + i0 + n_out + o0 + n_scr]
        send_sems, recv_sems = refs[n_in + i0 + n_out + o0 + n_scr:]
        first = functools.reduce(lambda u, v: u & v, [pl.program_id(d) == 0 for d in range(len(grid))])
        last = functools.reduce(lambda u, v: u & v, [pl.program_id(d) == grid[d] - 1 for d in range(len(grid))])

        def each(phase):
            for ch, a, b, s in zip(chunks, in_off, out_off, sem_off):
                getattr(ch, phase)(cin[a:a + len(ch.ins)], cout[b:b + len(ch.out_shapes)], send_sems, recv_sems, s)

        pl.when(first)(lambda: each("start"))
        body(*ins, *outs, *scr)
        pl.when(last)(lambda: each("finish"))

    hbm = pl.BlockSpec(memory_space=pl.ANY)
    res = pl.pallas_call(
        wrapped, grid=grid, in_specs=list(in_specs) + [hbm] * i0, out_specs=o_specs + [hbm] * o0,
        out_shape=o_shapes + c_outs,
        scratch_shapes=list(scratch_shapes) + [pltpu.SemaphoreType.DMA((s0,)), pltpu.SemaphoreType.DMA((s0,))],
        input_output_aliases=alias, name=name,
        compiler_params=pltpu.CompilerParams(dimension_semantics=(ARB,) * len(grid), vmem_limit_bytes=VMEM_LIMIT_V7X,
                                             has_side_effects=True))(*args, *c_ins)
    for ch, b in zip(chunks, out_off):
        ch.done(list(res[n_out + b:n_out + b + len(ch.out_shapes)]))
    return res[0] if single else list(res[:n_out])


def _mm_nn(a, b, *, name, q=None, out_dtype=F32):
    m, k = a.shape
    n = b.shape[-1]
    tm, tn = _mm_tiles(m, k, n, jnp.dtype(out_dtype).itemsize)

    def body(a_ref, b_ref, o_ref):
        o_ref[...] = _dot(a_ref[...], b_ref[...]).astype(o_ref.dtype)

    return _hosted_call(
        body, grid=(m // tm, n // tn),
        in_specs=[pl.BlockSpec((tm, k), lambda i, j: (i, 0)),
                  pl.BlockSpec((None, k, tn), lambda i, j: (0, 0, j))],
        out_specs=pl.BlockSpec((tm, tn), lambda i, j: (i, j)),
        out_shape=jax.ShapeDtypeStruct((m, n), out_dtype),
        args=(a, b), name=name, q=q, flops=2.0 * m * k * n)


def _mm_nt(a, b, *, name, q=None, out_dtype=F32):
    m, k = a.shape
    n = b.shape[-2]
    tm, tn = _mm_tiles(m, k, n, jnp.dtype(out_dtype).itemsize)

    def body(a_ref, b_ref, o_ref):
        o_ref[...] = _dot_nt(a_ref[...], b_ref[...]).astype(o_ref.dtype)

    return _hosted_call(
        body, grid=(m // tm, n // tn),
        in_specs=[pl.BlockSpec((tm, k), lambda i, j: (i, 0)),
                  pl.BlockSpec((None, tn, k), lambda i, j: (0, j, 0))],
        out_specs=pl.BlockSpec((tm, tn), lambda i, j: (i, j)),
        out_shape=jax.ShapeDtypeStruct((m, n), out_dtype),
        args=(a, b), name=name, q=q, flops=2.0 * m * k * n)


def _mm_tn(a, b, *, name, q=None, out=None, n_total=None, col_block_offset=0):
    k, m = a.shape
    n = b.shape[-1]
    tk, tm, tn = _mm_tn_tiles(k, m, n)
    off = col_block_offset * (n // tn)

    def body(a_ref, b_ref, *rest):
        o_ref = rest[-1]
        part = _dot_tn(a_ref[...], b_ref[...])

        @pl.when(pl.program_id(2) == 0)
        def _():
            o_ref[...] = part

        @pl.when(pl.program_id(2) > 0)
        def _():
            o_ref[...] += part

    in_specs = [pl.BlockSpec((tk, tm), lambda i, j, s: (s, i)), pl.BlockSpec((tk, tn), lambda i, j, s: (s, j))]
    args = (a, b)
    if out is not None:
        in_specs.append(pl.BlockSpec(memory_space=pl.ANY))
        args = (a, b, out)
    return _hosted_call(
        body, grid=(m // tm, n // tn, k // tk), in_specs=in_specs,
        out_specs=pl.BlockSpec((None, tm, tn), lambda i, j, s: (0, i, j + off)),
        out_shape=jax.ShapeDtypeStruct((1, m, n_total or n), F32),
        aliases={2: 0} if out is not None else None,
        args=args, name=name, q=q, flops=2.0 * m * k * n)


def _mm_ffn_dh(dg, dv, w_up, *, name, q=None):
    m, f = dg.shape
    d = w_up.shape[-2]
    tm, tn = _mm_tiles(m, 2 * f, d, 4)

    def body(dg_ref, dv_ref, wg_ref, wv_ref, o_ref):
        o_ref[...] = _dot_nt(dg_ref[...], wg_ref[...]) + _dot_nt(dv_ref[...], wv_ref[...])

    return _hosted_call(
        body, grid=(m // tm, d // tn),
        in_specs=[pl.BlockSpec((tm, f), lambda i, j: (i, 0)),
                  pl.BlockSpec((tm, f), lambda i, j: (i, 0)),
                  pl.BlockSpec((None, tn, f), lambda i, j: (0, j, 0)),
                  pl.BlockSpec((None, tn, f), lambda i, j: (0, j, 1))],
        out_specs=pl.BlockSpec((tm, tn), lambda i, j: (i, j)),
        out_shape=jax.ShapeDtypeStruct((m, d), F32),
        args=(dg, dv, w_up, w_up), name=name, q=q, flops=4.0 * m * f * d)


def _norm_fwd(x, g, *, name):
    n, d = x.shape
    tm = _tile(n, 256, SUBLANE)

    def body(x_ref, g_ref, o_ref):
        o_ref[...] = _rms_fwd(x_ref[...], g_ref[...]).astype(o_ref.dtype)

    return pl.pallas_call(
        body, grid=(n // tm,),
        in_specs=[pl.BlockSpec((tm, d), lambda i: (i, 0)), pl.BlockSpec((1, d), lambda i: (0, 0))],
        out_specs=pl.BlockSpec((tm, d), lambda i: (i, 0)),
        out_shape=jax.ShapeDtypeStruct((n, d), MXU),
        name=name, compiler_params=_cp((PAR,)))(x, g)


def _norm_bwd_dg(dy, x, g, *, name):
    n, d = x.shape
    tm = _tile(n, 256, SUBLANE)

    def body(dy_ref, x_ref, g_ref, dg_ref):
        @pl.when(pl.program_id(0) == 0)
        def _():
            dg_ref[...] = jnp.zeros_like(dg_ref)
        _, dg = _rms_bwd(dy_ref[...], x_ref[...], g_ref[...])
        dg_ref[...] += dg

    return pl.pallas_call(
        body, grid=(n // tm,),
        in_specs=[pl.BlockSpec((tm, d), lambda i: (i, 0)), pl.BlockSpec((tm, d), lambda i: (i, 0)),
                  pl.BlockSpec((1, d), lambda i: (0, 0))],
        out_specs=pl.BlockSpec((1, d), lambda i: (0, 0)),
        out_shape=jax.ShapeDtypeStruct((1, d), F32),
        name=name, compiler_params=_cp((ARB,)))(dy, x, g)


def _resid_norm_fwd(x, y, g_post, g_pres, *, name, q=None):
    n, d = x.shape
    tm = _tile(n, 256, SUBLANE)
    nh = len(g_pres)

    def body(x_ref, y_ref, gp_ref, *rest):
        gpre = rest[:nh]
        xo_ref = rest[nh]
        h_refs = rest[nh + 1:]
        xo = x_ref[...] + _rms_fwd(y_ref[...], gp_ref[...])
        xo_ref[...] = xo
        for g_ref, h_ref in zip(gpre, h_refs):
            h_ref[...] = _rms_fwd(xo, g_ref[...]).astype(h_ref.dtype)

    row = pl.BlockSpec((tm, d), lambda i: (i, 0))
    vec = pl.BlockSpec((1, d), lambda i: (0, 0))
    outs = _hosted_call(
        body, grid=(n // tm,),
        in_specs=[row, row, vec] + [vec] * nh,
        out_specs=[row] + [row] * nh,
        out_shape=[jax.ShapeDtypeStruct((n, d), F32)] + [jax.ShapeDtypeStruct((n, d), MXU)] * nh,
        args=(x, y, g_post, *g_pres), name=name, q=q, budget_us=HOST_US["resid"])
    return outs[0], list(outs[1:])


def _loss_fwd(x, y, g_post, target, *, name):
    n, d = x.shape
    tm = _tile(n, 256, SUBLANE)

    def body(x_ref, y_ref, gp_ref, t_ref, dx_ref, sq_ref):
        @pl.when(pl.program_id(0) == 0)
        def _():
            sq_ref[...] = jnp.zeros_like(sq_ref)
        err = x_ref[...] + _rms_fwd(y_ref[...], gp_ref[...]) - t_ref[...]
        dx_ref[...] = err * (1.0 / d)
        sq_ref[...] += jnp.sum(err * err, axis=0, keepdims=True)

    row = pl.BlockSpec((tm, d), lambda i: (i, 0))
    vec = pl.BlockSpec((1, d), lambda i: (0, 0))
    return pl.pallas_call(
        body, grid=(n // tm,),
        in_specs=[row, row, vec, row],
        out_specs=[row, vec],
        out_shape=[jax.ShapeDtypeStruct((n, d), F32), jax.ShapeDtypeStruct((1, d), F32)],
        name=name, compiler_params=_cp((ARB,)))(x, y, g_post, target)


def _resid_norm_bwd(dx_out, dhs, x_out, g_pres, y, g_post, *, name, q=None):
    n, d = dx_out.shape
    tm = _tile(n, 256, SUBLANE)
    nh = len(dhs)
    has_y = y is not None

    def body(*refs):
        it = iter(refs)
        dxo_ref = next(it)
        dh_refs = [next(it) for _ in range(nh)]
        xo_ref = next(it) if nh else None
        gpre_refs = [next(it) for _ in range(nh)]
        y_ref = next(it) if has_y else None
        gpost_ref = next(it) if has_y else None
        g_out = next(it)
        dy_out = next(it) if has_y else None
        dgpre_out = [next(it) for _ in range(nh)]
        dgpost_out = next(it) if has_y else None

        @pl.when(pl.program_id(0) == 0)
        def _():
            for r in dgpre_out:
                r[...] = jnp.zeros_like(r)
            if has_y:
                dgpost_out[...] = jnp.zeros_like(dgpost_out)

        g = dxo_ref[...]
        if nh:
            xo = xo_ref[...]
            for dh_ref, gp_ref, dg_ref in zip(dh_refs, gpre_refs, dgpre_out):
                dx, dg = _rms_bwd(dh_ref[...], xo, gp_ref[...])
                g = g + dx
                dg_ref[...] += dg
        g_out[...] = g
        if has_y:
            dy, dg = _rms_bwd(g, y_ref[...], gpost_ref[...])
            dy_out[...] = dy.astype(dy_out.dtype)
            dgpost_out[...] += dg

    row = pl.BlockSpec((tm, d), lambda i: (i, 0))
    vec = pl.BlockSpec((1, d), lambda i: (0, 0))
    ins, in_specs = [dx_out], [row]
    ins += list(dhs)
    in_specs += [row] * nh
    if nh:
        ins.append(x_out)
        in_specs.append(row)
    ins += list(g_pres)
    in_specs += [vec] * nh
    if has_y:
        ins += [y, g_post]
        in_specs += [row, vec]
    out_specs, out_shape = [row], [jax.ShapeDtypeStruct((n, d), F32)]
    if has_y:
        out_specs.append(row)
        out_shape.append(jax.ShapeDtypeStruct((n, d), MXU))
    out_specs += [vec] * nh
    out_shape += [jax.ShapeDtypeStruct((1, d), F32)] * nh
    if has_y:
        out_specs.append(vec)
        out_shape.append(jax.ShapeDtypeStruct((1, d), F32))
    outs = list(_hosted_call(
        body, grid=(n // tm,), in_specs=in_specs, out_specs=out_specs, out_shape=out_shape,
        args=tuple(ins), name=name, q=q, budget_us=HOST_US["resid_bwd"]))
    g = outs.pop(0)
    dy = outs.pop(0) if has_y else None
    dgpre = [outs.pop(0) for _ in range(nh)]
    dgpost = outs.pop(0) if has_y else None
    return g, dy, dgpre, dgpost


def _ffn_conv(up, w_ref, b_ref):
    return (w_ref[0:1, :] * _shift_down_edge(up, 2) + w_ref[1:2, :] * _shift_down_edge(up, 1)
            + w_ref[2:3, :] * up + b_ref[...])


def _ffn_act_fwd(up, wconv, bconv, bsz, *, name, q=None):
    n, f2 = up.shape
    f = f2 // 2
    t = n // bsz
    tc = _tile(f, 256)
    nf = f // tc

    def body(ug_ref, uv_ref, wg_ref, wv_ref, bg_ref, bv_ref, o_ref, g_ref, v_ref):
        g = _ffn_conv(ug_ref[...].astype(F32), wg_ref, bg_ref)
        v = _ffn_conv(uv_ref[...].astype(F32), wv_ref, bv_ref)
        g_ref[...] = g.astype(g_ref.dtype)
        v_ref[...] = v.astype(v_ref.dtype)
        o_ref[...] = (_gelu(g) * v).astype(o_ref.dtype)

    blk = pl.BlockSpec((t, tc), lambda b, j: (b, j))
    return _hosted_call(
        body, grid=(bsz, nf),
        in_specs=[blk, pl.BlockSpec((t, tc), lambda b, j: (b, j + nf)),
                  pl.BlockSpec((FFN_CONV, tc), lambda b, j: (0, j)),
                  pl.BlockSpec((FFN_CONV, tc), lambda b, j: (0, j + nf)),
                  pl.BlockSpec((1, tc), lambda b, j: (0, j)),
                  pl.BlockSpec((1, tc), lambda b, j: (0, j + nf))],
        out_specs=[blk, blk, blk],
        out_shape=[jax.ShapeDtypeStruct((n, f), MXU)] * 3,
        args=(up, up, wconv, wconv, bconv, bconv), name=name, q=q, budget_us=HOST_US["ffn_act"])


def _ffn_act_bwd(up, ug, uv, dact, wconv, bsz, *, name, q=None):
    n, f2 = up.shape
    f = f2 // 2
    t = n // bsz
    tc = _tile(f, 256)
    nf = f // tc

    def body(xg_ref, xv_ref, g_ref, v_ref, da_ref, wg_ref, wv_ref,
             dug_ref, duv_ref, dwg_ref, dwv_ref, dbg_ref, dbv_ref):
        @pl.when(pl.program_id(1) == 0)
        def _():
            for r in (dwg_ref, dwv_ref, dbg_ref, dbv_ref):
                r[...] = jnp.zeros_like(r)

        gl, dgl = _gelu_and_grad(g_ref[...].astype(F32))
        da = da_ref[...].astype(F32)
        dg = da * v_ref[...].astype(F32) * dgl
        dv = da * gl

        def conv_bwd(du, w_ref, x_ref, dx_ref, dw_ref, db_ref):
            du1, du2 = _shift_up_edge(du, 1), _shift_up_edge(du, 2)
            dx_ref[...] = (w_ref[2:3, :] * du + w_ref[1:2, :] * du1 + w_ref[0:1, :] * du2).astype(dx_ref.dtype)
            x = x_ref[...].astype(F32)
            dw_ref[0:1, :] += jnp.sum(x * du2, axis=0, keepdims=True)
            dw_ref[1:2, :] += jnp.sum(x * du1, axis=0, keepdims=True)
            dw_ref[2:3, :] += jnp.sum(x * du, axis=0, keepdims=True)
            db_ref[...] += jnp.sum(du, axis=0, keepdims=True)

        conv_bwd(dg, wg_ref, xg_ref, dug_ref, dwg_ref, dbg_ref)
        conv_bwd(dv, wv_ref, xv_ref, duv_ref, dwv_ref, dbv_ref)

    blk = pl.BlockSpec((t, tc), lambda j, b: (b, j))
    wspec = pl.BlockSpec((FFN_CONV, tc), lambda j, b: (0, j))
    bspec = pl.BlockSpec((1, tc), lambda j, b: (0, j))
    outs = _hosted_call(
        body, grid=(nf, bsz),
        in_specs=[blk, pl.BlockSpec((t, tc), lambda j, b: (b, j + nf)), blk, blk, blk,
                  wspec, pl.BlockSpec((FFN_CONV, tc), lambda j, b: (0, j + nf))],
        out_specs=[blk, blk, wspec, wspec, bspec, bspec],
        out_shape=[jax.ShapeDtypeStruct((n, f), MXU), jax.ShapeDtypeStruct((n, f), MXU),
                   jax.ShapeDtypeStruct((FFN_CONV, f), F32), jax.ShapeDtypeStruct((FFN_CONV, f), F32),
                   jax.ShapeDtypeStruct((1, f), F32), jax.ShapeDtypeStruct((1, f), F32)],
        args=(up, up, ug, uv, dact, wconv, wconv), name=name, q=q, budget_us=HOST_US["ffn_act_bwd"])
    dug, duv, dwg, dwv, dbg, dbv = outs
    return dug, duv, jnp.concatenate([dwg, dwv], axis=1), jnp.concatenate([dbg, dbv], axis=1)


def _mem_attn_fwd(proj, q_col_block, mkv, ycat, bsz, *, name, q=None):
    n = proj.shape[0]
    t = n // bsz
    tq = _tile(t, 512, SUBLANE)
    nt = t // tq
    scale = HEAD_DIM ** -0.5

    def body(q_ref, kv_ref, old_ref, o_ref):
        del old_ref
        outs = []
        for h in range(MEM_HEADS):
            sl = slice(h * HEAD_DIM, (h + 1) * HEAD_DIM)
            q = q_ref[:, sl].astype(MXU)
            k = kv_ref[:, sl].astype(MXU)
            v = kv_ref[:, MEM_WIDTH + h * HEAD_DIM: MEM_WIDTH + (h + 1) * HEAD_DIM].astype(MXU)
            s = _dot_nt(q, k) * scale
            m = jnp.max(s, axis=-1, keepdims=True)
            p = jnp.exp(s - m)
            p = p / jnp.sum(p, axis=-1, keepdims=True)
            outs.append(_dot(p.astype(MXU), v))
        o_ref[...] = jnp.concatenate(outs, axis=-1).astype(o_ref.dtype)

    return _hosted_call(
        body, grid=(bsz, nt),
        in_specs=[pl.BlockSpec((tq, MEM_WIDTH), lambda b, i: (b * nt + i, q_col_block)),
                  pl.BlockSpec((MEM_LEN, 2 * MEM_WIDTH), lambda b, i: (b, 0)),
                  pl.BlockSpec(memory_space=pl.ANY)],
        out_specs=pl.BlockSpec((tq, MEM_WIDTH), lambda b, i: (b * nt + i, MIX_WIDTH // MEM_WIDTH)),
        out_shape=jax.ShapeDtypeStruct(ycat.shape, ycat.dtype),
        aliases={2: 0}, args=(proj, mkv, ycat), name=name, q=q, budget_us=HOST_US["mem_attn_fwd"])


def _mem_attn_bwd(proj, q_col_block, mkv, dycat, dproj, bsz, *, name, q=None):
    n = proj.shape[0]
    t = n // bsz
    tq = _tile(t, 512, SUBLANE)
    nt = t // tq
    scale = HEAD_DIM ** -0.5

    def body(q_ref, kv_ref, do_ref, old_ref, dq_ref, dkv_ref):
        del old_ref

        @pl.when(pl.program_id(1) == 0)
        def _():
            dkv_ref[...] = jnp.zeros_like(dkv_ref)

        dqs, dks, dvs = [], [], []
        for h in range(MEM_HEADS):
            sl = slice(h * HEAD_DIM, (h + 1) * HEAD_DIM)
            q = q_ref[:, sl].astype(MXU)
            k = kv_ref[:, sl].astype(MXU)
            v = kv_ref[:, MEM_WIDTH + h * HEAD_DIM: MEM_WIDTH + (h + 1) * HEAD_DIM].astype(MXU)
            do = do_ref[:, sl].astype(MXU)
            s = _dot_nt(q, k) * scale
            m = jnp.max(s, axis=-1, keepdims=True)
            p = jnp.exp(s - m)
            p = p / jnp.sum(p, axis=-1, keepdims=True)
            dvs.append(_dot_tn(p.astype(MXU), do))
            dp = _dot_nt(do, v)
            ds = (p * (dp - jnp.sum(dp * p, axis=-1, keepdims=True)) * scale).astype(MXU)
            dqs.append(_dot(ds, k))
            dks.append(_dot_tn(ds, q))
        dq_ref[...] = jnp.concatenate(dqs, axis=-1).astype(dq_ref.dtype)
        dkv_ref[...] += jnp.concatenate(dks + dvs, axis=-1)

    return _hosted_call(
        body, grid=(bsz, nt),
        in_specs=[pl.BlockSpec((tq, MEM_WIDTH), lambda b, i: (b * nt + i, q_col_block)),
                  pl.BlockSpec((MEM_LEN, 2 * MEM_WIDTH), lambda b, i: (b, 0)),
                  pl.BlockSpec((tq, MEM_WIDTH), lambda b, i: (b * nt + i, MIX_WIDTH // MEM_WIDTH)),
                  pl.BlockSpec(memory_space=pl.ANY)],
        out_specs=[pl.BlockSpec((tq, MEM_WIDTH), lambda b, i: (b * nt + i, q_col_block)),
                   pl.BlockSpec((MEM_LEN, 2 * MEM_WIDTH), lambda b, i: (b, 0))],
        out_shape=[jax.ShapeDtypeStruct(dproj.shape, dproj.dtype),
                   jax.ShapeDtypeStruct((bsz * MEM_LEN, 2 * MEM_WIDTH), F32)],
        aliases={3: 0}, args=(proj, mkv, dycat, dproj), name=name, q=q, budget_us=HOST_US["mem_attn_bwd"])


def _swa_scores(q, k, h, dist, mask, sink):
    s = _dot_nt(q, k) * (HEAD_DIM ** -0.5)
    s = jnp.where(mask, s - SLOPES[h] * dist, -jnp.inf)
    m = jnp.maximum(jnp.max(s, axis=-1, keepdims=True), sink)
    p = jnp.exp(s - m)
    psink = jnp.exp(sink - m)
    inv = 1.0 / (jnp.sum(p, axis=-1, keepdims=True) + psink)
    return p * inv, psink * inv


def _swa_mask(n):
    qi = lax.broadcasted_iota(jnp.int32, (WINDOW, 2 * WINDOW), 0) + WINDOW
    ki = lax.broadcasted_iota(jnp.int32, (WINDOW, 2 * WINDOW), 1)
    dist = qi - ki
    mask = (dist >= 0) & (dist < WINDOW) & ((n > 0) | (ki >= WINDOW))
    return dist.astype(F32), mask


def _swa_fwd(proj, kv, sinks, bsz, *, name, q=None):
    n_tok = proj.shape[0]
    nb = n_tok // bsz // WINDOW
    kvw = SWA_KV_HEADS * HEAD_DIM

    def body(sink_ref, q_ref, kvp_ref, kvc_ref, o_ref):
        n = pl.program_id(1)
        dist, mask = _swa_mask(n)
        kk = jnp.concatenate([kvp_ref[:, :kvw], kvc_ref[:, :kvw]], axis=0).astype(MXU)
        vv = jnp.concatenate([kvp_ref[:, kvw:], kvc_ref[:, kvw:]], axis=0).astype(MXU)
        outs = []
        for h in range(SWA_HEADS):
            c = h // SWA_GROUP
            q = q_ref[:, h * HEAD_DIM:(h + 1) * HEAD_DIM].astype(MXU)
            p, _ = _swa_scores(q, kk[:, c * HEAD_DIM:(c + 1) * HEAD_DIM], h, dist, mask, sink_ref[h])
            outs.append(_dot(p.astype(MXU), vv[:, c * HEAD_DIM:(c + 1) * HEAD_DIM]))
        o_ref[...] = jnp.concatenate(outs, axis=-1).astype(o_ref.dtype)

    return _hosted_call(
        body, grid=(bsz, nb),
        in_specs=[pl.BlockSpec(memory_space=pltpu.SMEM),
                  pl.BlockSpec((WINDOW, MIX_WIDTH), lambda b, n: (b * nb + n, 0)),
                  pl.BlockSpec((WINDOW, 2 * kvw), lambda b, n: (b * nb + jnp.maximum(n - 1, 0), 0)),
                  pl.BlockSpec((WINDOW, 2 * kvw), lambda b, n: (b * nb + n, 0))],
        out_specs=pl.BlockSpec((WINDOW, MIX_WIDTH), lambda b, n: (b * nb + n, 0)),
        out_shape=jax.ShapeDtypeStruct((n_tok, D_MODEL), MXU),
        args=(sinks, proj, kv, kv), name=name, q=q, budget_us=HOST_US["swa_fwd"])


def _swa_bwd(proj, kv, sinks, dycat, bsz, *, name, q=None):
    n_tok = proj.shape[0]
    nb = n_tok // bsz // WINDOW
    kvw = SWA_KV_HEADS * HEAD_DIM

    def body(sink_ref, q_ref, kvp_ref, kvc_ref, do_ref, dq_ref, dkvc_ref, dkvp_ref, dsink_ref):
        n = pl.program_id(1)

        @pl.when((pl.program_id(0) == 0) & (n == 0))
        def _():
            dsink_ref[...] = jnp.zeros_like(dsink_ref)

        dist, mask = _swa_mask(n)
        kk = jnp.concatenate([kvp_ref[:, :kvw], kvc_ref[:, :kvw]], axis=0).astype(MXU)
        vv = jnp.concatenate([kvp_ref[:, kvw:], kvc_ref[:, kvw:]], axis=0).astype(MXU)
        lane = lax.broadcasted_iota(jnp.int32, (SUBLANE, LANE), 1)
        dqs = []
        dks = [None] * SWA_KV_HEADS
        dvs = [None] * SWA_KV_HEADS
        dsink = jnp.zeros((SUBLANE, LANE), F32)
        for h in range(SWA_HEADS):
            c = h // SWA_GROUP
            k = kk[:, c * HEAD_DIM:(c + 1) * HEAD_DIM]
            v = vv[:, c * HEAD_DIM:(c + 1) * HEAD_DIM]
            q = q_ref[:, h * HEAD_DIM:(h + 1) * HEAD_DIM].astype(MXU)
            do = do_ref[:, h * HEAD_DIM:(h + 1) * HEAD_DIM].astype(MXU)
            p, psink = _swa_scores(q, k, h, dist, mask, sink_ref[h])
            dv = _dot_tn(p.astype(MXU), do)
            dp = _dot_nt(do, v)
            rs = jnp.sum(dp * p, axis=-1, keepdims=True)
            ds = (p * (dp - rs) * (HEAD_DIM ** -0.5)).astype(MXU)
            dsink = dsink + jnp.where(lane == h, jnp.sum(-psink * rs, axis=0, keepdims=True), 0.0)
            dqs.append(_dot(ds, k))
            dk = _dot_tn(ds, q)
            dks[c] = dk if dks[c] is None else dks[c] + dk
            dvs[c] = dv if dvs[c] is None else dvs[c] + dv
        dq_ref[...] = jnp.concatenate(dqs, axis=-1).astype(dq_ref.dtype)
        dkv = jnp.concatenate(dks + dvs, axis=-1)
        dkvp_ref[...] = dkv[:WINDOW]
        dkvc_ref[...] = dkv[WINDOW:]
        dsink_ref[...] += dsink

    qspec = pl.BlockSpec((WINDOW, MIX_WIDTH), lambda b, n: (b * nb + n, 0))
    kvspec = pl.BlockSpec((WINDOW, 2 * kvw), lambda b, n: (b * nb + n, 0))
    return _hosted_call(
        body, grid=(bsz, nb),
        in_specs=[pl.BlockSpec(memory_space=pltpu.SMEM), qspec,
                  pl.BlockSpec((WINDOW, 2 * kvw), lambda b, n: (b * nb + jnp.maximum(n - 1, 0), 0)),
                  kvspec, qspec],
        out_specs=[qspec, kvspec, kvspec, pl.BlockSpec((SUBLANE, LANE), lambda b, n: (0, 0))],
        out_shape=[jax.ShapeDtypeStruct((n_tok, D_MODEL), MXU),
                   jax.ShapeDtypeStruct((n_tok, 2 * kvw), F32),
                   jax.ShapeDtypeStruct((n_tok, 2 * kvw), F32),
                   jax.ShapeDtypeStruct((SUBLANE, LANE), F32)],
        args=(sinks, proj, kv, kv, dycat), name=name, q=q, budget_us=HOST_US["swa_bwd"])


def _swa_dkv_combine(curs, prevs, bsz, *, name):
    n_tok, w = curs[0].shape
    nb = n_tok // bsz // WINDOW
    k = len(curs)

    def body(*refs):
        o_ref = refs[-1]
        n = pl.program_id(1)
        acc = refs[0][...]
        for r in refs[1:k]:
            acc = acc + r[...]
        nxt = refs[k][...]
        for r in refs[k + 1:2 * k]:
            nxt = nxt + r[...]
        o_ref[...] = (acc + jnp.where(n < nb - 1, nxt, 0.0)).astype(o_ref.dtype)

    cur = pl.BlockSpec((WINDOW, w), lambda b, n: (b * nb + n, 0))
    prv = pl.BlockSpec((WINDOW, w), lambda b, n: (b * nb + jnp.minimum(n + 1, nb - 1), 0))
    return pl.pallas_call(
        body, grid=(bsz, nb), in_specs=[cur] * k + [prv] * k, out_specs=cur,
        out_shape=jax.ShapeDtypeStruct((n_tok, w), MXU),
        name=name, compiler_params=_cp((PAR, PAR)))(*curs, *prevs)


def _lru_gates(ux, halo, ext_ref, wc_ref, bc_ref, wr_ref, br_ref, wi_ref, bi_ref, lam_ref):
    tt = ux.shape[0]
    ext_ref[0:SUBLANE, :] = halo
    ext_ref[SUBLANE:, :] = ux
    xs = [ux] + [ext_ref[pl.ds(SUBLANE - k, tt), :] for k in range(1, LRU_CONV)]
    xc = bc_ref[...] + wc_ref[3:4, :] * xs[0] + wc_ref[2:3, :] * xs[1] + wc_ref[1:2, :] * xs[2] + wc_ref[0:1, :] * xs[3]
    pre_r, pre_i = [], []
    for blk in range(MIX_WIDTH // GATE_TILE):
        xb = xc[:, blk * GATE_TILE:(blk + 1) * GATE_TILE].astype(MXU)
        pre_r.append(_dot(xb, wr_ref[blk]))
        pre_i.append(_dot(xb, wi_ref[blk]))
    r = jax.nn.sigmoid(jnp.concatenate(pre_r, axis=-1) + br_ref[...])
    i = jax.nn.sigmoid(jnp.concatenate(pre_i, axis=-1) + bi_ref[...])
    nlam = -lam_ref[...]
    sp = jnp.maximum(nlam, 0.0) + jnp.log(1.0 + jnp.exp(-jnp.abs(nlam)))
    log_a = -LRU_C * r * sp
    a = jnp.exp(log_a)
    om = -jnp.tanh(log_a) * (a * a + 1.0)
    s = jnp.sqrt(om)
    return xs, xc, r, i, sp, a, s


def _lru_fwd(proj, wconv, bconv, wr, br, wi, bi, lam, bsz, *, name, q=None):
    n_tok = proj.shape[0]
    t = n_tok // bsz
    tt = _tile(t, 256, SUBLANE)
    nt = t // tt
    w = MIX_WIDTH
    ng = tt // SUBLANE

    def body(pg_ref, halo_ref, wc_ref, bc_ref, wr_ref, br_ref, wi_ref, bi_ref, lam_ref,
             y_ref, h_ref, ext_ref, a_ref, b_ref, carry_ref):
        ti = pl.program_id(1)

        @pl.when(ti == 0)
        def _():
            carry_ref[...] = jnp.zeros_like(carry_ref)

        gate = pg_ref[:, :w]
        ux = pg_ref[:, w:]
        halo = jnp.where(ti > 0, halo_ref[...], 0.0)
        _, xc, _, i, _, a, s = _lru_gates(ux, halo, ext_ref, wc_ref, bc_ref, wr_ref, br_ref, wi_ref, bi_ref, lam_ref)
        a_ref[...] = a
        b_ref[...] = s * (i * xc)
        row = lax.broadcasted_iota(jnp.int32, (SUBLANE, w), 0)

        def group(g, hprev):
            off = pl.multiple_of(g * SUBLANE, SUBLANE)
            ca = a_ref[pl.ds(off, SUBLANE), :]
            cb = b_ref[pl.ds(off, SUBLANE), :]
            for d in (1, 2, 4):
                a_sh = jnp.where(row >= d, pltpu.roll(ca, d, axis=0), 1.0)
                b_sh = jnp.where(row >= d, pltpu.roll(cb, d, axis=0), 0.0)
                cb = ca * b_sh + cb
                ca = ca * a_sh
            h = ca * hprev + cb
            b_ref[pl.ds(off, SUBLANE), :] = h
            return jnp.broadcast_to(h[SUBLANE - 1:SUBLANE, :], (SUBLANE, w))

        carry_ref[...] = lax.fori_loop(0, ng, group, carry_ref[...])
        h = b_ref[...]
        h_ref[...] = h
        y_ref[...] = (h * _gelu(gate)).astype(y_ref.dtype)

    vec = lambda r: pl.BlockSpec((r, w), lambda b, i: (0, 0))
    wspec = pl.BlockSpec((w // GATE_TILE, GATE_TILE, GATE_TILE), lambda b, i: (0, 0, 0))
    hb = tt // SUBLANE
    return _hosted_call(
        body, grid=(bsz, nt),
        in_specs=[pl.BlockSpec((tt, 2 * w), lambda b, i: (b * nt + i, 0)),
                  pl.BlockSpec((SUBLANE, w), lambda b, i: (jnp.maximum((b * nt + i) * hb - 1, 0), 1)),
                  vec(LRU_CONV), vec(1), wspec, vec(1), wspec, vec(1), vec(1)],
        out_specs=[pl.BlockSpec((tt, w), lambda b, i: (b * nt + i, 0)),
                   pl.BlockSpec((tt, w), lambda b, i: (b * nt + i, 0))],
        out_shape=[jax.ShapeDtypeStruct((n_tok, D_MODEL), MXU), jax.ShapeDtypeStruct((n_tok, w), F32)],
        scratch_shapes=[pltpu.VMEM((tt + SUBLANE, w), F32), pltpu.VMEM((tt, w), F32),
                        pltpu.VMEM((tt, w), F32), pltpu.VMEM((SUBLANE, w), F32)],
        args=(proj, proj, wconv, bconv, wr, br, wi, bi, lam), name=name, q=q, budget_us=HOST_US["lru_fwd"])


def _lru_bwd(proj, hs, dycat, wconv, bconv, wr, br, wi, bi, lam, bsz, *, name, q=None):
    n_tok = proj.shape[0]
    t = n_tok // bsz
    tt = _tile(t, 256, SUBLANE)
    nt = t // tt
    w = MIX_WIDTH
    ng = tt // SUBLANE
    nblk = w // GATE_TILE

    def body(pg_ref, halo_ref, h_ref, hhalo_ref, dy_ref, wc_ref, bc_ref, wr_ref, br_ref, wi_ref, bi_ref, lam_ref,
             dp_ref, dwc_ref, dbc_ref, dwr_ref, dbr_ref, dwi_ref, dbi_ref, dlam_ref,
             ext_ref, a_ref, c_ref, g_ref, gcarry_ref, xcarry_ref):
        bi_ = pl.program_id(0)
        ti = nt - 1 - pl.program_id(1)

        @pl.when((bi_ == 0) & (pl.program_id(1) == 0))
        def _():
            for r in (dwc_ref, dbc_ref, dwr_ref, dbr_ref, dwi_ref, dbi_ref, dlam_ref):
                r[...] = jnp.zeros_like(r)

        @pl.when(pl.program_id(1) == 0)
        def _():
            gcarry_ref[...] = jnp.zeros_like(gcarry_ref)
            xcarry_ref[...] = jnp.zeros_like(xcarry_ref)

        gate = pg_ref[:, :w]
        ux = pg_ref[:, w:]
        halo = jnp.where(ti > 0, halo_ref[...], 0.0)
        xs, xc, r, i, sp, a, s = _lru_gates(ux, halo, ext_ref, wc_ref, bc_ref, wr_ref, br_ref, wi_ref, bi_ref, lam_ref)
        h = h_ref[...]
        gl, dgl = _gelu_and_grad(gate)
        dy = dy_ref[...]
        dgate = dy * h * dgl
        row_t = lax.broadcasted_iota(jnp.int32, (tt, w), 0)
        g_ref[...] = dy * gl + jnp.where(row_t == tt - 1, gcarry_ref[0:1, :], 0.0)
        c_ref[...] = _shift_up(a, 1, row_t)
        row = lax.broadcasted_iota(jnp.int32, (SUBLANE, w), 0)

        a_ref[...] = a

        def group(k, gnext):
            off = pl.multiple_of((ng - 1 - k) * SUBLANE, SUBLANE)
            cc = c_ref[pl.ds(off, SUBLANE), :]
            cb = g_ref[pl.ds(off, SUBLANE), :]
            cb = cb + jnp.where(row == SUBLANE - 1, gnext, 0.0)
            cc = jnp.where(row == SUBLANE - 1, 0.0, cc)
            for d in (1, 2, 4):
                c_sh = jnp.where(row < SUBLANE - d, pltpu.roll(cc, SUBLANE - d, axis=0), 1.0)
                b_sh = jnp.where(row < SUBLANE - d, pltpu.roll(cb, SUBLANE - d, axis=0), 0.0)
                cb = cc * b_sh + cb
                cc = cc * c_sh
            g_ref[pl.ds(off, SUBLANE), :] = cb
            a0 = a_ref[pl.ds(off, SUBLANE), :]
            return jnp.broadcast_to(a0[0:1, :] * cb[0:1, :], (SUBLANE, w))

        gc = lax.fori_loop(0, ng, group, jnp.zeros((SUBLANE, w), F32))
        gcarry_ref[...] = gc
        gsc = g_ref[...]

        hhalo = jnp.where(ti > 0, hhalo_ref[SUBLANE - 1:SUBLANE, :], 0.0)
        hprev = jnp.where(row_t == 0, hhalo, pltpu.roll(h, 1, axis=0))
        gated = i * xc
        d_gated = gsc * s
        d_atot = gsc * hprev - (gsc * gated) * a / s
        d_loga = d_atot * a
        d_r = d_loga * (-LRU_C) * sp
        dlam_ref[...] += jnp.sum(d_loga * r, axis=0, keepdims=True) * (LRU_C * jax.nn.sigmoid(-lam_ref[...]))
        d_i = d_gated * xc
        d_xc = d_gated * i
        d_pr = d_r * r * (1.0 - r)
        d_pi = d_i * i * (1.0 - i)
        dbr_ref[...] += jnp.sum(d_pr, axis=0, keepdims=True)
        dbi_ref[...] += jnp.sum(d_pi, axis=0, keepdims=True)
        extra = []
        for blk in range(nblk):
            sl = slice(blk * GATE_TILE, (blk + 1) * GATE_TILE)
            xb = xc[:, sl].astype(MXU)
            dr_b = d_pr[:, sl].astype(MXU)
            di_b = d_pi[:, sl].astype(MXU)
            dwr_ref[blk] += _dot_tn(xb, dr_b)
            dwi_ref[blk] += _dot_tn(xb, di_b)
            extra.append(_dot_nt(dr_b, wr_ref[blk]) + _dot_nt(di_b, wi_ref[blk]))
        d_xc = d_xc + jnp.concatenate(extra, axis=-1)
        dbc_ref[...] += jnp.sum(d_xc, axis=0, keepdims=True)
        for k in range(LRU_CONV):
            dwc_ref[k:k + 1, :] += jnp.sum(d_xc * xs[LRU_CONV - 1 - k], axis=0, keepdims=True)
        ext_ref[0:tt, :] = d_xc
        ext_ref[tt:, :] = xcarry_ref[...]
        dux = wc_ref[3:4, :] * d_xc
        for k in range(LRU_CONV - 1):
            dux = dux + wc_ref[k:k + 1, :] * ext_ref[pl.ds(LRU_CONV - 1 - k, tt), :]
        xcarry_ref[...] = d_xc[0:SUBLANE, :]
        dp_ref[:, :w] = dgate.astype(dp_ref.dtype)
        dp_ref[:, w:] = dux.astype(dp_ref.dtype)

    vec = lambda r: pl.BlockSpec((r, w), lambda b, i: (0, 0))
    wspec = pl.BlockSpec((nblk, GATE_TILE, GATE_TILE), lambda b, i: (0, 0, 0))
    hb = tt // SUBLANE
    rblk = lambda b, i: b * nt + (nt - 1 - i)
    halo_idx = lambda b, i: jnp.maximum(rblk(b, i) * hb - 1, 0)
    wide = pl.BlockSpec((tt, 2 * w), lambda b, i: (rblk(b, i), 0))
    narrow = pl.BlockSpec((tt, w), lambda b, i: (rblk(b, i), 0))
    return _hosted_call(
        body, grid=(bsz, nt),
        in_specs=[wide, pl.BlockSpec((SUBLANE, w), lambda b, i: (halo_idx(b, i), 1)),
                  narrow, pl.BlockSpec((SUBLANE, w), lambda b, i: (halo_idx(b, i), 0)), narrow,
                  vec(LRU_CONV), vec(1), wspec, vec(1), wspec, vec(1), vec(1)],
        out_specs=[wide, vec(LRU_CONV), vec(1), wspec, vec(1), wspec, vec(1), vec(1)],
        out_shape=[jax.ShapeDtypeStruct((n_tok, 2 * w + MEM_WIDTH), MXU),
                   jax.ShapeDtypeStruct((LRU_CONV, w), F32), jax.ShapeDtypeStruct((1, w), F32),
                   jax.ShapeDtypeStruct((nblk, GATE_TILE, GATE_TILE), F32), jax.ShapeDtypeStruct((1, w), F32),
                   jax.ShapeDtypeStruct((nblk, GATE_TILE, GATE_TILE), F32), jax.ShapeDtypeStruct((1, w), F32),
                   jax.ShapeDtypeStruct((1, w), F32)],
        scratch_shapes=[pltpu.VMEM((tt + SUBLANE, w), F32), pltpu.VMEM((tt, w), F32), pltpu.VMEM((tt, w), F32),
                        pltpu.VMEM((tt, w), F32), pltpu.VMEM((SUBLANE, w), F32), pltpu.VMEM((SUBLANE, w), F32)],
        args=(proj, proj, hs, hs, dycat, wconv, bconv, wr, br, wi, bi, lam), name=name, q=q,
        budget_us=HOST_US["lru_bwd"])


def _gate_tiles(w):
    per = GATE_TILE // HEAD_DIM
    w4 = w.reshape(LRU_BLOCKS // per, per, HEAD_DIM, HEAD_DIM)
    eye = jnp.eye(per, dtype=w.dtype)
    return jnp.einsum("bnij,nm->bnimj", w4, eye).reshape(LRU_BLOCKS // per, GATE_TILE, GATE_TILE)


def _gate_blocks(t):
    per = GATE_TILE // HEAD_DIM
    t5 = t.reshape(LRU_BLOCKS // per, per, HEAD_DIM, per, HEAD_DIM)
    eye = jnp.eye(per, dtype=t.dtype)
    return jnp.einsum("bnimj,nm->bnij", t5, eye).reshape(LRU_BLOCKS, HEAD_DIM, HEAD_DIM)


def _row(v):
    return v.reshape(1, -1)


def _local_step(x, mem, target, p, wfull, push_grad, q):
    bsz, t, d = x.shape
    n = bsz * t
    x2d = x.reshape(n, d)
    tgt = target.reshape(n, d)
    mem2d = mem.reshape(bsz * MEM_LEN, d)
    wr_t = [_gate_tiles(p["w_rg_r"][j]).astype(MXU) for j in range(N_A)]
    wi_t = [_gate_tiles(p["w_rg_i"][j]).astype(MXU) for j in range(N_A)]

    mn = [_norm_fwd(mem2d, _row(p["g_mem"][l]), name=f"mem_norm{l}") for l in range(DEPTH)]
    mkv = [None] * DEPTH
    h = _norm_fwd(x2d, _row(p["g_mix_pre"][0]), name="in_norm")
    xin = x2d
    sv = []
    kv = hkv = None
    for l in range(DEPTH):
        s = {"xin": xin, "h": h}
        if q is not None:
            q.horizon = (l + 2) * GROUPS_PER_LAYER
        mkv[l] = _mm_nn(mn[l], wfull("w_mem_kv", l), name=f"mem_kv{l}", q=q)
        if l < N_A:
            proj = _mm_nn(h, wfull("w_in_a", l), name=f"in_proj{l}", q=q)
            ycat, hs = _lru_fwd(proj, p["w_conv_a"][l], _row(p["b_conv_a"][l]), wr_t[l], _row(p["b_rg_r"][l]),
                                wi_t[l], _row(p["b_rg_i"][l]), _row(p["lru_lambda"][l]), bsz, name=f"lru_fwd{l}", q=q)
            s["hs"] = hs
            qblk = 2 * MIX_WIDTH // MEM_WIDTH
        else:
            if l == N_A:
                kv = _mm_nn(hkv, wfull("w_kv", 0), name="kv_proj", q=q)
            proj = _mm_nn(h, wfull("w_in_b", l - N_A), name=f"in_proj{l}", q=q)
            ycat = _swa_fwd(proj, kv, p["sinks_b"][l - N_A], bsz, name=f"swa_fwd{l}", q=q)
            qblk = MIX_WIDTH // MEM_WIDTH
        ycat = _mem_attn_fwd(proj, qblk, mkv[l], ycat, bsz, name=f"mem_attn_fwd{l}", q=q)
        y = _mm_nn(ycat, wfull("w_mix_out", l), name=f"mix_out{l}", q=q)
        x1, (h2,) = _resid_norm_fwd(xin, y, _row(p["g_mix_post"][l]), [_row(p["g_ffn_pre"][l])], name=f"mix_resid{l}", q=q)
        up = _mm_nn(h2, wfull("w_ffn_up", l), name=f"ffn_up{l}", q=q, out_dtype=MXU)
        act, ug, uv = _ffn_act_fwd(up, p["w_ffn_conv"][l], _row(p["b_ffn_conv"][l]), bsz, name=f"ffn_act{l}", q=q)
        f = _mm_nn(act, wfull("w_ffn_down", l), name=f"ffn_down{l}", q=q)
        s.update(proj=proj, qblk=qblk, ycat=ycat, y=y, x1=x1, h2=h2, up=up, ug=ug, uv=uv, act=act, f=f)
        sv.append(s)
        if l < DEPTH - 1:
            g_pres = [_row(p["g_mix_pre"][l + 1])] + ([_row(p["g_kv"])] if l + 1 == N_A else [])
            xin, hn = _resid_norm_fwd(x1, f, _row(p["g_ffn_post"][l]), g_pres, name=f"ffn_resid{l}", q=q)
            h = hn[0]
            if l + 1 == N_A:
                hkv = hn[1]
        else:
            g_tot, sq = _loss_fwd(x1, f, _row(p["g_ffn_post"][l]), tgt, name="loss")

    if q is not None:
        q.horizon = LAST_GROUP
    gs = {k: [None] * DEPTH for k in ("g_mix_pre", "g_mix_post", "g_ffn_pre", "g_ffn_post", "g_mem",
                                       "w_ffn_conv", "b_ffn_conv")}
    ga = {k: [None] * N_A for k in ("w_conv_a", "b_conv_a", "w_rg_r", "b_rg_r", "w_rg_i", "b_rg_i", "lru_lambda")}
    gsink = [None] * (DEPTH - N_A)
    dkv_cur, dkv_prev = [], []
    g_tot, df, _, gs["g_ffn_post"][DEPTH - 1] = _resid_norm_bwd(
        g_tot, [], None, [], sv[-1]["f"], _row(p["g_ffn_post"][DEPTH - 1]), name="loss_bwd")
    grad_x = None
    for l in reversed(range(DEPTH)):
        s = sv[l]
        dact = _mm_nt(df, wfull("w_ffn_down", l), name=f"d_act{l}", q=q, out_dtype=MXU)
        push_grad("w_ffn_down", l, _mm_tn(s["act"], df, name=f"dw_down{l}", q=q))
        dug, duv, gs["w_ffn_conv"][l], gs["b_ffn_conv"][l] = _ffn_act_bwd(
            s["up"], s["ug"], s["uv"], dact, p["w_ffn_conv"][l], bsz, name=f"ffn_act_bwd{l}", q=q)
        dh2 = _mm_ffn_dh(dug, duv, wfull("w_ffn_up", l), name=f"d_h2_{l}", q=q)
        dwu = _mm_tn(s["h2"], dug, name=f"dw_up_g{l}", q=q, n_total=2 * D_FF)
        push_grad("w_ffn_up", l, _mm_tn(s["h2"], duv, name=f"dw_up_v{l}", q=q, out=dwu, n_total=2 * D_FF,
                                        col_block_offset=1))
        g1, dy, (gs["g_ffn_pre"][l],), gs["g_mix_post"][l] = _resid_norm_bwd(
            g_tot, [dh2], s["x1"], [_row(p["g_ffn_pre"][l])], s["y"], _row(p["g_mix_post"][l]), name=f"mix_resid_bwd{l}", q=q)
        dycat = _mm_nt(dy, wfull("w_mix_out", l), name=f"d_ycat{l}", q=q)
        push_grad("w_mix_out", l, _mm_tn(s["ycat"], dy, name=f"dw_mix_out{l}", q=q))
        if l < N_A:
            dproj, dwc, dbc, dwr, dbr, dwi, dbi, dlam = _lru_bwd(
                s["proj"], s["hs"], dycat, p["w_conv_a"][l], _row(p["b_conv_a"][l]), wr_t[l], _row(p["b_rg_r"][l]),
                wi_t[l], _row(p["b_rg_i"][l]), _row(p["lru_lambda"][l]), bsz, name=f"lru_bwd{l}", q=q)
            ga["w_conv_a"][l], ga["b_conv_a"][l], ga["lru_lambda"][l] = dwc, dbc[0], dlam[0]
            ga["w_rg_r"][l], ga["w_rg_i"][l] = _gate_blocks(dwr), _gate_blocks(dwi)
            ga["b_rg_r"][l] = dbr.reshape(LRU_BLOCKS, HEAD_DIM)
            ga["b_rg_i"][l] = dbi.reshape(LRU_BLOCKS, HEAD_DIM)
            w_in, j = "w_in_a", l
        else:
            dproj, dc, dp_, dsk = _swa_bwd(s["proj"], kv, p["sinks_b"][l - N_A], dycat, bsz, name=f"swa_bwd{l}", q=q)
            dkv_cur.append(dc)
            dkv_prev.append(dp_)
            gsink[l - N_A] = dsk[0, :SWA_HEADS]
            w_in, j = "w_in_b", l - N_A
        dproj, dmkv = _mem_attn_bwd(s["proj"], s["qblk"], mkv[l], dycat, dproj, bsz, name=f"mem_attn_bwd{l}", q=q)
        dh = _mm_nt(dproj, wfull(w_in, j), name=f"d_h{l}", q=q)
        push_grad(w_in, j, _mm_tn(s["h"], dproj, name=f"dw_in{l}", q=q))
        dmkv = dmkv.astype(MXU)
        dmn = _mm_nt(dmkv, wfull("w_mem_kv", l), name=f"d_mem_norm{l}", q=q)
        push_grad("w_mem_kv", l, _mm_tn(mn[l], dmkv, name=f"dw_mem_kv{l}", q=q))
        gs["g_mem"][l] = _norm_bwd_dg(dmn, mem2d, _row(p["g_mem"][l]), name=f"mem_norm_bwd{l}")
        dhs, g_pres = [dh], [_row(p["g_mix_pre"][l])]
        if l == N_A:
            dkv = _swa_dkv_combine(dkv_cur, dkv_prev, bsz, name="dkv_combine")
            dhs.append(_mm_nt(dkv, wfull("w_kv", 0), name="d_hkv", q=q))
            g_pres.append(_row(p["g_kv"]))
            push_grad("w_kv", 0, _mm_tn(hkv, dkv, name="dw_kv", q=q))
        if l > 0:
            g_tot, df, dgpre, gs["g_ffn_post"][l - 1] = _resid_norm_bwd(
                g1, dhs, s["xin"], g_pres, sv[l - 1]["f"], _row(p["g_ffn_post"][l - 1]), name=f"ffn_resid_bwd{l - 1}", q=q)
        else:
            grad_x, _, dgpre, _ = _resid_norm_bwd(g1, dhs, s["xin"], g_pres, None, None, name="in_norm_bwd", q=q)
        gs["g_mix_pre"][l] = dgpre[0]
        if l == N_A:
            g_kv = dgpre[1][0]

    grads = {}
    for k in ("g_mix_pre", "g_mix_post", "g_ffn_pre", "g_ffn_post", "g_mem", "b_ffn_conv"):
        grads[k] = jnp.concatenate(gs[k], axis=0)
    grads["w_ffn_conv"] = jnp.stack(gs["w_ffn_conv"])
    for k, v in ga.items():
        grads[k] = jnp.stack(v)
    grads["sinks_b"] = jnp.stack(gsink)
    grads["g_kv"] = g_kv
    return jnp.sum(sq), grad_x.reshape(bsz, t, d), grads


N_CHIP = 4
HALF_ALIGN = 16
MIN_PART_BYTES = 128 * 1024


def _full_shape(kind, shard_shape):
    l, r, c = shard_shape
    return {"row": (l, N_CHIP * r, c), "col": (l, r, N_CHIP * c), "slot": (N_CHIP, l, r, c)}[kind]


def _slot_view(ref, kind, shard_shape, s, hf, sub=(0, 1)):
    _, r, c = shard_shape
    rh = r // 2
    if hf is None:
        start, size = 0, r
    else:
        size = rh // sub[1]
        start = hf * rh + sub[0] * size
    if kind == "row":
        start = s * r + start
    if not isinstance(start, int):
        start = pl.multiple_of(start, HALF_ALIGN)
    rows = pl.ds(start, size)
    if kind == "row":
        return ref.at[:, rows, :]
    if kind == "col":
        return ref.at[:, rows, pl.ds(s * c, c)]
    return ref.at[s, :, rows, :]


def _half_view(ref, shard_shape, hf, sub=(0, 1)):
    rh = shard_shape[1] // 2
    size = rh // sub[1]
    return ref.at[:, pl.ds(pl.multiple_of(hf * rh + sub[0] * size, HALF_ALIGN), size), :]


def _with_slot(kind, s, fn):
    if kind != "col" or isinstance(s, int):
        fn(s)
        return
    for k in range(N_CHIP):
        @pl.when(s == k)
        def _(k=k):
            fn(k)


def _mesh_pos():
    return lax.axis_index("x"), lax.axis_index("y"), lax.axis_index("c")


def _other_chips(x, y):
    return [(1 - x, y), (x, 1 - y), (1 - x, 1 - y)]


ICI_BYTES_PER_US = 6.0e4
ICI_GATHER_BYTES_PER_US = 4.5e4
D2D_BYTES_PER_US = 4.0e5


class _Chunk:
    def __init__(self, group, cost, ins, out_shapes, alias, n_sem, start, finish, done, buffer=None, bind=None):
        self.group, self.cost, self.ins, self.out_shapes, self.alias, self.n_sem = group, cost, ins, out_shapes, alias, n_sem
        self.start, self.finish, self.done = start, finish, done
        self.buffer = buffer
        self.bind = bind

    def prepare(self):
        if self.bind is not None:
            self.bind(self)


def _merged(chunks):
    groups, by_buffer = [], {}
    for ch in chunks:
        key = None if ch.buffer is None else (id(ch.buffer[0]), ch.buffer[1])
        if key is not None and key in by_buffer:
            by_buffer[key].append(ch)
        else:
            groups.append([ch])
            if key is not None:
                by_buffer[key] = groups[-1]
    out = []
    for parts in groups:
        if len(parts) == 1:
            out.append(parts[0])
            continue
        offs = [sum(p.n_sem for p in parts[:i]) for i in range(len(parts))]

        def run(phase, ins, outs, ss, rs, b, parts=parts, offs=offs):
            for p, o in zip(parts, offs):
                getattr(p, phase)(ins, outs, ss, rs, b + o)

        def done(outs, parts=parts):
            for p in parts:
                p.done(outs)

        first = parts[0]
        out.append(_Chunk(first.group, sum(p.cost for p in parts), first.ins, first.out_shapes, first.alias,
                          sum(p.n_sem for p in parts), functools.partial(run, "start"),
                          functools.partial(run, "finish"), done))
    return out


LAST_GROUP = 1 << 30


class _CommQueue:
    def __init__(self):
        self.pending = []
        self.flushes = 0
        self.horizon = LAST_GROUP

    def push(self, chunk):
        self.pending.append(chunk)

    def take(self, budget_us):
        got, used = [], 0.0
        for ch in sorted(self.pending, key=lambda ch: (ch.group, -ch.cost)):
            if ch.group >= self.horizon and ch.group != LAST_GROUP:
                continue
            if used + ch.cost <= budget_us and not self._shares_buffer(ch, got):
                got.append(ch)
                used += ch.cost
        return self._taken(got)

    @staticmethod
    def _shares_buffer(ch, others):
        return ch.buffer is not None and any(
            o.buffer is not None and o.buffer[0] is ch.buffer[0] and o.buffer[1] != ch.buffer[1] for o in others)

    def _taken(self, got):
        self.pending = [ch for ch in self.pending if ch not in got]
        for ch in got:
            ch.prepare()
        return _merged(got)

    def flush(self, group=LAST_GROUP):
        while True:
            chunks = []
            for ch in self.pending:
                if ch.group <= group and not self._shares_buffer(ch, chunks):
                    chunks.append(ch)
            if not chunks:
                return
            _run_chunks(self._taken(chunks), name=f"comm_flush{self.flushes}")
            self.flushes += 1


def _run_chunks(chunks, *, name):
    ins = [a for ch in chunks for a in ch.ins]
    outs = [s for ch in chunks for s in ch.out_shapes]
    alias, offs = {}, []
    i0 = o0 = s0 = 0
    for ch in chunks:
        offs.append((i0, o0, s0))
        for ci, co in ch.alias.items():
            alias[i0 + ci] = o0 + co
        i0 += len(ch.ins)
        o0 += len(ch.out_shapes)
        s0 += ch.n_sem

    def body(*refs):
        send_sems, recv_sems = refs[i0 + o0:]
        for phase in ("start", "finish"):
            for ch, (a, b, s) in zip(chunks, offs):
                getattr(ch, phase)(refs[a:a + len(ch.ins)], refs[i0 + b:i0 + b + len(ch.out_shapes)],
                                   send_sems, recv_sems, s)

    hbm = pl.BlockSpec(memory_space=pl.ANY)
    res = pl.pallas_call(
        body, in_specs=[hbm] * i0, out_specs=[hbm] * o0, out_shape=outs,
        scratch_shapes=[pltpu.SemaphoreType.DMA((s0,)), pltpu.SemaphoreType.DMA((s0,))],
        input_output_aliases=alias, name=name, compiler_params=pltpu.CompilerParams(has_side_effects=True))(*ins)
    for ch, (_, b, _) in zip(chunks, offs):
        ch.done(list(res[b:b + len(ch.out_shapes)]))


def _remote(src, dst, send_sems, recv_sems, k, dev):
    return pltpu.make_async_remote_copy(src_ref=src, dst_ref=dst, send_sem=send_sems.at[k], recv_sem=recv_sems.at[k],
                                        device_id=dev, device_id_type=MESH_T)


def _gather_chunks(q, group, kind, shard, l, ready):
    _, r, c = shard.shape
    shp = (1, r, c)
    rh = r // 2
    parts = max(p for p in (8, 4, 2, 1)
                if (rh // p) % HALF_ALIGN == 0 and (p == 1 or (rh // p) * c * shard.dtype.itemsize >= MIN_PART_BYTES))
    part_bytes = (rh // parts) * c * shard.dtype.itemsize
    full_type = jax.ShapeDtypeStruct(_full_shape(kind, shp), shard.dtype)
    state = {"full": None, "parts_done": 0}

    def bind_first(ch):
        ch.ins, ch.alias = ([shard], {}) if state["full"] is None else ([shard, state["full"]], {1: 0})

    def bind_full(ch):
        ch.ins = [state["full"]]

    def make_part(p):
        sub = (p, parts)

        def any_part(full):
            return _slot_view(full, kind, shp, 0, 0, sub)

        def start1(ins, outs, ss, rs, b):
            x, y, c_ = _mesh_pos()
            src, full = ins[0].at[pl.ds(l, 1)], outs[0]
            if p == 0:
                _with_slot(kind, 2 * x + y, lambda s: pltpu.make_async_copy(
                    src, _slot_view(full, kind, shp, s, None), ss.at[b + N_CHIP - 1]).start())
            for j, (ox, oy) in enumerate(_other_chips(x, y)):
                _with_slot(kind, 2 * x + y, lambda s, j=j, ox=ox, oy=oy: _remote(
                    _half_view(src, shp, c_, sub), _slot_view(full, kind, shp, s, c_, sub), ss, rs, b + j,
                    (ox, oy, c_)).start())

        def finish1(ins, outs, ss, rs, b):
            x, y, c_ = _mesh_pos()
            h = any_part(outs[0])
            for j in range(N_CHIP - 1):
                _remote(h, h, ss, rs, b + j, (x, y, 1 - c_)).wait()
            if p == 0:
                pltpu.make_async_copy(ins[0].at[pl.ds(l, 1)], _slot_view(outs[0], kind, shp, 0, None),
                                      ss.at[b + N_CHIP - 1]).wait()

        def start2(ins, outs, ss, rs, b):
            x, y, c_ = _mesh_pos()
            for j, (ox, oy) in enumerate(_other_chips(x, y)):
                def forward(s, j=j):
                    v = _slot_view(outs[0], kind, shp, s, c_, sub)
                    _remote(v, v, ss, rs, b + j, (x, y, 1 - c_)).start()
                _with_slot(kind, 2 * ox + oy, forward)

        def finish2(ins, outs, ss, rs, b):
            x, y, c_ = _mesh_pos()
            h = any_part(outs[0])
            for j in range(N_CHIP - 1):
                _remote(h, h, ss, rs, b + j, (x, y, 1 - c_)).wait()

        def done2(outs):
            state["full"] = outs[0]
            state["parts_done"] += 1
            if state["parts_done"] == parts:
                ready(outs[0])

        def done1(outs):
            state["full"] = outs[0]
            q.push(_Chunk(group, 3 * part_bytes / D2D_BYTES_PER_US, None, [full_type], {0: 0}, N_CHIP - 1,
                          start2, finish2, done2, buffer=(state, 2), bind=bind_full))

        return _Chunk(group, 3 * part_bytes / ICI_GATHER_BYTES_PER_US, None, [full_type], None,
                      N_CHIP if p == 0 else N_CHIP - 1, start1, finish1, done1, buffer=(state, 1), bind=bind_first)

    for p in range(parts):
        q.push(make_part(p))


def _reduce_scatter_chunks(q, kind, grad, shard_shape, pos, name, ready):
    _, r, c = shard_shape
    shp = (1, r, c)
    rh = r // 2

    def start1(ins, outs, ss, rs, b):
        x, y, c_ = _mesh_pos()
        for s in range(N_CHIP):
            _remote(_slot_view(ins[0], kind, shp, s, 1 - c_), outs[0].at[s], ss, rs, b + s, (x, y, 1 - c_)).start()

    def finish1(ins, outs, ss, rs, b):
        x, y, c_ = _mesh_pos()
        for s in range(N_CHIP):
            _remote(outs[0].at[s], outs[0].at[s], ss, rs, b + s, (x, y, 1 - c_)).wait()

    def start2(ins, outs, ss, rs, b):
        x, y, c_ = _mesh_pos()
        for j, (ox, oy) in enumerate(_other_chips(x, y)):
            _remote(ins[0].at[2 * ox + oy], outs[0].at[j], ss, rs, b + j, (ox, oy, c_)).start()

    def finish2(ins, outs, ss, rs, b):
        x, y, c_ = _mesh_pos()
        for j in range(N_CHIP - 1):
            _remote(outs[0].at[j], outs[0].at[j], ss, rs, b + j, (x, y, 1 - c_)).wait()

    def start3(ins, outs, ss, rs, b):
        x, y, c_ = _mesh_pos()
        v = _half_view(outs[0], shp, c_)
        _remote(v, v, ss, rs, b, (x, y, 1 - c_)).start()

    def finish3(ins, outs, ss, rs, b):
        x, y, c_ = _mesh_pos()
        v = _half_view(outs[0], shp, c_)
        _remote(v, v, ss, rs, b, (x, y, 1 - c_)).wait()

    def done2(pair, outs):
        half = _rs_chip_add(pair, outs[0], shp, pos, name=f"rs_chip_add_{name}")
        q.push(_Chunk(LAST_GROUP, rh * c * 4 / D2D_BYTES_PER_US, [half], [jax.ShapeDtypeStruct(half.shape, half.dtype)],
                      {0: 0}, 1, start3, finish3, lambda o: ready(o[0])))

    def done1(outs):
        pair, wire = _rs_pair_add(grad, outs[0], kind, shp, pos, name=f"rs_pair_add_{name}")
        q.push(_Chunk(LAST_GROUP, 3 * rh * c * wire.dtype.itemsize / ICI_BYTES_PER_US, [wire],
                      [jax.ShapeDtypeStruct((N_CHIP - 1, 1, rh, c), wire.dtype)], {}, N_CHIP - 1,
                      start2, finish2, functools.partial(done2, pair)))

    q.push(_Chunk(LAST_GROUP, N_CHIP * rh * c * 4 / D2D_BYTES_PER_US, [grad],
                  [jax.ShapeDtypeStruct((N_CHIP, 1, rh, c), F32)], {}, N_CHIP, start1, finish1, done1))


def _allgather8(vec, *, name):
    r = vec.shape[0]
    n_dev = 8

    def body(v_ref, buf, send_sems, recv_sems):
        x, y, c = _mesh_pos()
        me = 4 * x + 2 * y + c
        copies = []
        for k in range(1, n_dev):
            kx, ky, kc = (k >> 2) & 1, (k >> 1) & 1, k & 1
            peer = ((1 - x) if kx else x, (1 - y) if ky else y, (1 - c) if kc else c)
            cp = _remote(v_ref, buf.at[me], send_sems, recv_sems, k - 1, peer)
            cp.start()
            copies.append(cp)
        buf[me] = v_ref[...]
        for cp in copies:
            cp.wait()

    vm = pl.BlockSpec(memory_space=pltpu.VMEM)
    return pl.pallas_call(
        body, in_specs=[vm], out_specs=vm, out_shape=jax.ShapeDtypeStruct((n_dev, r, LANE), F32),
        scratch_shapes=[pltpu.SemaphoreType.DMA((n_dev - 1,)), pltpu.SemaphoreType.DMA((n_dev - 1,))],
        name=name, compiler_params=pltpu.CompilerParams(has_side_effects=True, vmem_limit_bytes=VMEM_LIMIT_V7X))(vec)


def _allreduce8(vec, *, name):
    r = vec.shape[0]
    rh = r // 2

    def body(v_ref, o_ref, sib_ref, chips_ref, send_sems, recv_sems):
        x, y, c = _mesh_pos()
        sib = (x, y, 1 - c)
        me = 2 * x + y
        pair = _remote(v_ref, sib_ref, send_sems, recv_sems, 0, sib)
        pair.start()
        pair.wait()
        rows = pl.ds(pl.multiple_of(c * rh, SUBLANE), rh)
        chips_ref[me] = v_ref[rows, :] + sib_ref[rows, :]
        copies = []
        for j, (ox, oy) in enumerate(_other_chips(x, y)):
            cp = _remote(chips_ref.at[me], chips_ref.at[me], send_sems, recv_sems, 1 + j, (ox, oy, c))
            cp.start()
            copies.append(cp)
        for cp in copies:
            cp.wait()
        acc = chips_ref[0]
        for s in range(1, N_CHIP):
            acc = acc + chips_ref[s]
        o_ref[rows, :] = acc
        swap = _remote(o_ref.at[rows, :], o_ref.at[rows, :], send_sems, recv_sems, N_CHIP, sib)
        swap.start()
        swap.wait()

    vm = pl.BlockSpec(memory_space=pltpu.VMEM)
    return pl.pallas_call(
        body, in_specs=[vm], out_specs=vm, out_shape=jax.ShapeDtypeStruct((r, LANE), F32),
        scratch_shapes=[pltpu.VMEM((r, LANE), F32), pltpu.VMEM((N_CHIP, rh, LANE), F32),
                        pltpu.SemaphoreType.DMA((N_CHIP + 1,)), pltpu.SemaphoreType.DMA((N_CHIP + 1,))],
        name=name, compiler_params=pltpu.CompilerParams(has_side_effects=True, vmem_limit_bytes=VMEM_LIMIT_V7X))(vec)


def _rs_pair_add(g, recv, kind, shape, pos, *, name):
    l, r, c = shape
    rh = r // 2
    if kind == "row":
        gspec = pl.BlockSpec((None, rh, c), lambda s, i, pos: (i, 2 * s + pos[0], 0))
    elif kind == "col":
        gspec = pl.BlockSpec((None, rh, c), lambda s, i, pos: (i, pos[0], s))
    else:
        gspec = pl.BlockSpec((None, None, rh, c), lambda s, i, pos: (s, i, pos[0], 0))
    pspec = pl.BlockSpec((None, None, rh, c), lambda s, i, pos: (s, i, 0, 0))

    def body(pos_ref, g_ref, r_ref, p_ref, pw_ref):
        del pos_ref
        v = g_ref[...] + r_ref[...]
        p_ref[...] = v
        pw_ref[...] = v.astype(pw_ref.dtype)

    return pl.pallas_call(
        body,
        grid_spec=pltpu.PrefetchScalarGridSpec(
            num_scalar_prefetch=1, grid=(N_CHIP, l), in_specs=[gspec, pspec], out_specs=[pspec, pspec]),
        out_shape=[jax.ShapeDtypeStruct((N_CHIP, l, rh, c), F32), jax.ShapeDtypeStruct((N_CHIP, l, rh, c), MXU)],
        name=name, compiler_params=_cp((PAR, PAR)))(pos, g, recv)


def _rs_chip_add(p, recv, shape, pos, *, name):
    l, r, c = shape
    rh = r // 2

    def body(pos_ref, p_ref, r_ref, o_ref):
        del pos_ref
        acc = p_ref[...]
        for j in range(N_CHIP - 1):
            acc = acc + r_ref[j].astype(F32)
        o_ref[...] = acc

    return pl.pallas_call(
        body,
        grid_spec=pltpu.PrefetchScalarGridSpec(
            num_scalar_prefetch=1, grid=(l,),
            in_specs=[pl.BlockSpec((None, None, rh, c), lambda i, pos: (pos[1], i, 0, 0)),
                      pl.BlockSpec((N_CHIP - 1, None, rh, c), lambda i, pos: (0, i, 0, 0))],
            out_specs=pl.BlockSpec((None, rh, c), lambda i, pos: (i, pos[0], 0))),
        out_shape=jax.ShapeDtypeStruct((l, r, c), F32),
        name=name, compiler_params=_cp((PAR,)))(pos, p, recv)


ADAM_BLOCK_ELEMS = 384 * 1024


def _adam_math(w, g, m, v):
    c1 = 1.0 / (1.0 - ADAM_B1 ** ADAM_STEP)
    c2 = 1.0 / (1.0 - ADAM_B2 ** ADAM_STEP)
    nm = ADAM_B1 * m + (1.0 - ADAM_B1) * g
    nv = ADAM_B2 * v + (1.0 - ADAM_B2) * (g * g)
    return -ADAM_LR * ((nm * c1) / (jnp.sqrt(nv * c2) + ADAM_EPS) + ADAM_WD * w), nm, nv


def _adamw_layer(w, g, m, v, outs, l, *, name):
    _, r, c = w.shape
    tr = _tile(r, max(SUBLANE, ADAM_BLOCK_ELEMS // c // SUBLANE * SUBLANE), SUBLANE)

    def body(w_ref, g_ref, m_ref, v_ref, *rest):
        go_ref, d_ref, nm_ref, nv_ref = rest[4:]
        gg = g_ref[...]
        go_ref[...] = gg
        d_ref[...], nm_ref[...], nv_ref[...] = _adam_math(w_ref[...], gg, m_ref[...], v_ref[...])

    lay = pl.BlockSpec((None, tr, c), lambda j: (l, j, 0))
    hbm = pl.BlockSpec(memory_space=pl.ANY)
    return pl.pallas_call(
        body, grid=(r // tr,),
        in_specs=[lay, pl.BlockSpec((None, tr, c), lambda j: (0, j, 0)), lay, lay] + [hbm] * 4,
        out_specs=[lay] * 4, out_shape=[jax.ShapeDtypeStruct(w.shape, F32)] * 4,
        input_output_aliases={4 + i: i for i in range(4)},
        name=name, compiler_params=_cp((PAR,)))(w, g, m, v, *outs)


def _adamw(w, g, m, v, *, name):
    shape = w.shape
    if w.ndim == 2:
        w, g, m, v = (a[None] for a in (w, g, m, v))
    l, r, c = w.shape
    tr = _tile(r, max(SUBLANE, ADAM_BLOCK_ELEMS // c // SUBLANE * SUBLANE), SUBLANE)

    def body(w_ref, g_ref, m_ref, v_ref, d_ref, nm_ref, nv_ref):
        d_ref[...], nm_ref[...], nv_ref[...] = _adam_math(w_ref[...], g_ref[...], m_ref[...], v_ref[...])

    spec = pl.BlockSpec((None, tr, c), lambda i, j: (i, j, 0))
    outs = pl.pallas_call(
        body, grid=(l, r // tr), in_specs=[spec] * 4, out_specs=[spec] * 3,
        out_shape=[jax.ShapeDtypeStruct((l, r, c), F32)] * 3,
        name=name, compiler_params=_cp((PAR, PAR)))(w, g, m, v)
    return tuple(o.reshape(shape) for o in outs)


PACK_ROWS = 512 * LANE


def _pack(arrays):
    flat = jnp.concatenate([a.reshape(-1).astype(F32) for a in arrays])
    pad = (-flat.shape[0]) % PACK_ROWS
    return jnp.pad(flat, (0, pad)).reshape(-1, LANE)


def _unpack(packed, shapes):
    flat = packed.reshape(-1)
    out, off = [], 0
    for s in shapes:
        size = int(np.prod(s))
        out.append(flat[off:off + size].reshape(s))
        off += size
    return out


BIG = (("w_mem_kv", "row"), ("w_mix_out", "row"), ("w_ffn_up", "col"), ("w_ffn_down", "row"),
       ("w_in_a", "slot"), ("w_in_b", "row"), ("w_kv", "row"))
SMALL_SHARDED = (("w_ffn_conv", 2), ("w_conv_a", 2), ("b_conv_a", 1), ("lru_lambda", 1))
SMALL_REPLICATED = ("g_mix_pre", "g_mix_post", "g_ffn_pre", "g_ffn_post", "g_mem", "b_ffn_conv",
                    "w_rg_r", "b_rg_r", "w_rg_i", "b_rg_i", "sinks_b", "g_kv")
WEIGHTS = ("g_mix_pre", "g_mix_post", "g_ffn_pre", "g_ffn_post", "g_mem", "w_mem_kv", "w_mix_out", "w_ffn_up",
           "w_ffn_conv", "b_ffn_conv", "w_ffn_down", "w_in_a", "w_conv_a", "b_conv_a", "w_rg_r", "b_rg_r", "w_rg_i",
           "b_rg_i", "lru_lambda", "w_in_b", "sinks_b", "g_kv", "w_kv")


def _slot_to_cols(a):
    s, l, r, c = a.shape
    return a.transpose(1, 2, 0, 3).reshape(l, r, s * c)


def _cols_to_slot(a):
    l, r, c4 = a.shape
    return a.reshape(l, r, N_CHIP, c4 // N_CHIP).transpose(2, 0, 1, 3)


GROUPS_PER_LAYER = 8


def _layer_weights(layer):
    names = [("w_mem_kv", layer), ("w_in_a", layer) if layer < N_A else ("w_in_b", layer - N_A)]
    if layer == N_A:
        names.append(("w_kv", 0))
    return names + [("w_mix_out", layer), ("w_ffn_up", layer), ("w_ffn_down", layer)]


def _train_step(x, mem, target, w, m, v):
    xi, yi, ci = _mesh_pos()
    chip = 2 * xi + yi
    pos = jnp.stack([ci, chip]).astype(jnp.int32)

    q = _CommQueue()
    kinds = dict(BIG)
    as3 = lambda a: a if a.ndim == 3 else a[None]
    w3, m3, v3 = ({k: as3(d[k]) for k, _ in BIG} for d in (w, m, v))
    shards = {k: w3[k].astype(MXU) for k, _ in BIG}

    gathered = {}

    def on_gathered(k, l, full):
        gathered[k, l] = _slot_to_cols(full) if kinds[k] == "slot" else full

    group_of = {}

    for layer in range(DEPTH):
        for i, (k, l) in enumerate(_layer_weights(layer)):
            group_of[k, l] = layer * GROUPS_PER_LAYER + i
            _gather_chunks(q, group_of[k, l], kinds[k], shards[k], l, functools.partial(on_gathered, k, l))

    def wfull(k, l):
        if (k, l) not in gathered:
            q.flush(group_of[k, l])
        return gathered[k, l]

    q.flush(1)

    big_out = {k: [lax.empty(w3[k].shape, F32) for _ in range(4)] for k, _ in BIG}

    def on_reduced(k, l, g):
        big_out[k] = _adamw_layer(w3[k], g, m3[k], v3[k], big_out[k], l, name=f"adamw_{k}{l}")

    def push_grad(k, l, g):
        if kinds[k] == "slot":
            g = _cols_to_slot(g)
        _reduce_scatter_chunks(q, kinds[k], g, (1,) + w3[k].shape[1:], pos, f"{k}{l}", functools.partial(on_reduced, k, l))

    small_shapes = [w[k].shape for k, _ in SMALL_SHARDED]
    stacked = _allgather8(_pack([w[k] for k, _ in SMALL_SHARDED]), name="gather_small")
    per_chip = [_unpack(stacked[2 * s], small_shapes) for s in range(N_CHIP)]
    p = {k: w[k] for k in SMALL_REPLICATED}
    for i, (k, axis) in enumerate(SMALL_SHARDED):
        p[k] = jnp.concatenate([per_chip[s][i] for s in range(N_CHIP)], axis=axis)

    sq, grad_x, g = _local_step(x, mem, target, p, wfull, push_grad, q)
    loss = lax.psum(0.5 * sq / D_MODEL, ("x", "y", "c"))
    q.flush()

    small_names = [k for k, _ in SMALL_SHARDED] + list(SMALL_REPLICATED)
    summed = _allreduce8(_pack([g[k] for k in small_names]), name="allreduce_small")
    gsum = dict(zip(small_names, _unpack(summed, [p[k].shape for k in small_names])))
    for k, axis in SMALL_SHARDED:
        gsum[k] = lax.dynamic_slice_in_dim(gsum[k], chip * w[k].shape[axis], w[k].shape[axis], axis)

    delta, new_m, new_v = {}, {}, {}
    for k, _ in BIG:
        gsum[k], delta[k], new_m[k], new_v[k] = (o.reshape(w[k].shape) for o in big_out[k])
    packed = [_pack([d[k] for k in small_names]) for d in (w, gsum, m, v)]
    outs = _adamw(*packed, name="adamw_small")
    for d, o in zip((delta, new_m, new_v), outs):
        d.update(zip(small_names, _unpack(o, [w[k].shape for k in small_names])))
    return (loss, grad_x, *[gsum[k] for k in WEIGHTS], *[delta[k] for k in WEIGHTS],
            *[new_m[k] for k in WEIGHTS], *[new_v[k] for k in WEIGHTS])


def kernel(x, mem, g_mix_pre, g_mix_post, g_ffn_pre, g_ffn_post, g_mem, w_mem_kv, w_mix_out, w_ffn_up, w_ffn_conv, b_ffn_conv, w_ffn_down, w_in_a, w_conv_a, b_conv_a, w_rg_r, b_rg_r, w_rg_i, b_rg_i, lru_lambda, w_in_b, sinks_b, g_kv, w_kv, loss_target, m_g_mix_pre, m_g_mix_post, m_g_ffn_pre, m_g_ffn_post, m_g_mem, m_w_mem_kv, m_w_mix_out, m_w_ffn_up, m_w_ffn_conv, m_b_ffn_conv, m_w_ffn_down, m_w_in_a, m_w_conv_a, m_b_conv_a, m_w_rg_r, m_b_rg_r, m_w_rg_i, m_b_rg_i, m_lru_lambda, m_w_in_b, m_sinks_b, m_g_kv, m_w_kv, v_g_mix_pre, v_g_mix_post, v_g_ffn_pre, v_g_ffn_post, v_g_mem, v_w_mem_kv, v_w_mix_out, v_w_ffn_up, v_w_ffn_conv, v_b_ffn_conv, v_w_ffn_down, v_w_in_a, v_w_conv_a, v_b_conv_a, v_w_rg_r, v_b_rg_r, v_w_rg_i, v_b_rg_i, v_lru_lambda, v_w_in_b, v_sinks_b, v_g_kv, v_w_kv):
    args = (g_mix_pre, g_mix_post, g_ffn_pre, g_ffn_post, g_mem, w_mem_kv, w_mix_out, w_ffn_up, w_ffn_conv, b_ffn_conv, w_ffn_down, w_in_a, w_conv_a, b_conv_a, w_rg_r, b_rg_r, w_rg_i, b_rg_i, lru_lambda, w_in_b, sinks_b, g_kv, w_kv)
    ms = (m_g_mix_pre, m_g_mix_post, m_g_ffn_pre, m_g_ffn_post, m_g_mem, m_w_mem_kv, m_w_mix_out, m_w_ffn_up, m_w_ffn_conv, m_b_ffn_conv, m_w_ffn_down, m_w_in_a, m_w_conv_a, m_b_conv_a, m_w_rg_r, m_b_rg_r, m_w_rg_i, m_b_rg_i, m_lru_lambda, m_w_in_b, m_sinks_b, m_g_kv, m_w_kv)
    vs = (v_g_mix_pre, v_g_mix_post, v_g_ffn_pre, v_g_ffn_post, v_g_mem, v_w_mem_kv, v_w_mix_out, v_w_ffn_up, v_w_ffn_conv, v_b_ffn_conv, v_w_ffn_down, v_w_in_a, v_w_conv_a, v_b_conv_a, v_w_rg_r, v_b_rg_r, v_w_rg_i, v_b_rg_i, v_lru_lambda, v_w_in_b, v_sinks_b, v_g_kv, v_w_kv)
    return _train_step(x, mem, loss_target, dict(zip(WEIGHTS, args)), dict(zip(WEIGHTS, ms)), dict(zip(WEIGHTS, vs)))
```

```python
import functools
import math

import numpy as np
import jax
import jax.numpy as jnp
from jax import lax
from jax.experimental import pallas as pl
from jax.experimental.pallas import tpu as pltpu

F32 = jnp.float32
MXU = jnp.bfloat16

D_MODEL = 1024
HEAD_DIM = 64
MEM_LEN = 256
MEM_HEADS = 4
MEM_WIDTH = MEM_HEADS * HEAD_DIM
MIX_WIDTH = D_MODEL - MEM_WIDTH
LRU_BLOCKS = MIX_WIDTH // HEAD_DIM
LRU_CONV = 4
LRU_C = 8.0
SWA_HEADS = MIX_WIDTH // HEAD_DIM
SWA_KV_HEADS = 4
SWA_GROUP = SWA_HEADS // SWA_KV_HEADS
WINDOW = 128
D_FF = 2816
FFN_CONV = 3
EPS = 1e-6
DEPTH = 4
N_A = 2

ADAM_LR = 0.001
ADAM_B1 = 0.9
ADAM_B2 = 0.999
ADAM_EPS = 1e-08
ADAM_WD = 0.01
ADAM_STEP = 10

VMEM_LIMIT_V7X = 56 * 1024 * 1024
LANE = 128
SUBLANE = 8
GATE_TILE = 256
MESH_T = pl.DeviceIdType.MESH


def _alibi_slopes(n):
    def pow2_slopes(m):
        start = 2.0 ** (-8.0 / m)
        return [start ** (i + 1) for i in range(m)]
    c = 2 ** int(math.floor(math.log2(n)))
    s = pow2_slopes(c)
    if c != n:
        s = s + pow2_slopes(2 * c)[0::2][: n - c]
    return [float(np.float32(v)) for v in s]


SLOPES = _alibi_slopes(SWA_HEADS)


def _tile(n, cap, mult=LANE):
    best = None
    for t in range(mult, min(n, cap) + 1, mult):
        if n % t == 0:
            best = t
    return best if best is not None else n


def _cp(sem):
    return pltpu.CompilerParams(dimension_semantics=sem, vmem_limit_bytes=VMEM_LIMIT_V7X)


MM_VMEM_BUDGET = 40 * 1024 * 1024
HBM_BYTES_PER_US_V7X = 3.0e6
GRID_STEP_US = 0.35


def _divisors(n, mult):
    return [t for t in range(mult, n + 1, mult) if n % t == 0] or [n]


def _mm_tiles(m, k, n, out_bytes):
    best = None
    for tm in _divisors(m, 256):
        for tn in _divisors(n, LANE):
            vmem = 2 * (tm * k * 2 + k * tn * 2 + tm * tn * out_bytes)
            if vmem > MM_VMEM_BUDGET:
                continue
            steps = (m // tm) * (n // tn)
            b_reads = 1 if tn == n else m // tm
            traffic = m * k * 2 + k * n * 2 * b_reads + m * n * out_bytes
            first = tm * k * 2 + k * tn * 2
            cost = (traffic + first) / HBM_BYTES_PER_US_V7X + steps * GRID_STEP_US
            if best is None or cost < best[0]:
                best = (cost, tm, tn)
    return best[1], best[2]


def _mm_tn_tiles(k, m, n, whole_n=False):
    best = None
    for tm in _divisors(m, LANE):
        for tn in ([n] if whole_n else _divisors(n, LANE)):
            for tk in _divisors(k, 512):
                vmem = 2 * (tk * tm * 2 + tk * tn * 2 + tm * tn * 4)
                if vmem > MM_VMEM_BUDGET:
                    continue
                steps = (m // tm) * (n // tn) * (k // tk)
                traffic = k * m * 2 * (n // tn) + k * n * 2 * (m // tm) + m * n * 4
                cost = traffic / HBM_BYTES_PER_US_V7X + steps * GRID_STEP_US
                if best is None or cost < best[0]:
                    best = (cost, tk, tm, tn)
    return best[1], best[2], best[3]


ARB = "arbitrary"
PAR = "parallel"


def _rms_fwd(x, g):
    r = lax.rsqrt(jnp.mean(x * x, axis=-1, keepdims=True) + EPS)
    return x * r * g


def _rms_bwd(dy, x, g):
    r = lax.rsqrt(jnp.mean(x * x, axis=-1, keepdims=True) + EPS)
    xh = x * r
    gdy = dy * g
    dx = r * (gdy - xh * jnp.mean(gdy * xh, axis=-1, keepdims=True))
    dg = jnp.sum(dy * xh, axis=0, keepdims=True)
    return dx, dg


_GELU_K = math.sqrt(2.0 / math.pi)
_GELU_C = 0.044715


def _gelu(x):
    t = jnp.tanh(_GELU_K * (x + _GELU_C * x * x * x))
    return 0.5 * x * (1.0 + t)


def _gelu_and_grad(x):
    x2 = x * x
    t = jnp.tanh(_GELU_K * (x + _GELU_C * x2 * x))
    g = 0.5 * x * (1.0 + t)
    dg = 0.5 * (1.0 + t) + 0.5 * x * (1.0 - t * t) * (_GELU_K * (1.0 + 3.0 * _GELU_C * x2))
    return g, dg


def _shift_down(x, k, row):
    return jnp.where(row >= k, pltpu.roll(x, k, axis=0), 0.0)


def _shift_up(x, k, row):
    n = x.shape[0]
    return jnp.where(row < n - k, pltpu.roll(x, n - k, axis=0), 0.0)


def _shift_down_edge(x, k):
    r = pltpu.roll(x, k, axis=0)
    row = lax.broadcasted_iota(jnp.int32, (SUBLANE, x.shape[1]), 0)
    return jnp.concatenate([jnp.where(row >= k, r[:SUBLANE], 0.0), r[SUBLANE:]], axis=0)


def _shift_up_edge(x, k):
    n = x.shape[0]
    r = pltpu.roll(x, n - k, axis=0)
    row = lax.broadcasted_iota(jnp.int32, (SUBLANE, x.shape[1]), 0)
    return jnp.concatenate([r[:n - SUBLANE], jnp.where(row < SUBLANE - k, r[n - SUBLANE:], 0.0)], axis=0)


def _dot(a, b):
    return jnp.dot(a, b, preferred_element_type=F32)


def _dot_nt(a, b):
    return lax.dot_general(a, b, (((1,), (1,)), ((), ())), preferred_element_type=F32)


def _dot_tn(a, b):
    return lax.dot_general(a, b, (((0,), (0,)), ((), ())), preferred_element_type=F32)


MXU_FLOPS_PER_US = 7.0e8
HOST_US = {"lru_fwd": 44.0, "lru_bwd": 94.0, "swa_fwd": 60.0, "swa_bwd": 160.0, "mem_attn_fwd": 21.0,
           "mem_attn_bwd": 33.0, "ffn_act": 70.0, "ffn_act_bwd": 100.0, "resid": 22.0, "resid_bwd": 33.0}


def _hosted_call(body, *, grid, in_specs, out_specs, out_shape, args, name, aliases=None, scratch_shapes=(),
                 q=None, flops=0.0, budget_us=0.0):
    chunks = q.take(flops / MXU_FLOPS_PER_US + budget_us) if q is not None else []
    if not chunks:
        return pl.pallas_call(
            body, grid=grid, in_specs=in_specs, out_specs=out_specs, out_shape=out_shape,
            scratch_shapes=list(scratch_shapes), input_output_aliases=aliases or {}, name=name,
            compiler_params=_cp((ARB,) * len(grid)))(*args)
    single = not isinstance(out_shape, (list, tuple))
    o_shapes = [out_shape] if single else list(out_shape)
    o_specs = [out_specs] if single else list(out_specs)
    n_in, n_out, n_scr = len(args), len(o_shapes), len(scratch_shapes)
    c_ins = [a for ch in chunks for a in ch.ins]
    c_outs = [s for ch in chunks for s in ch.out_shapes]
    alias = dict(aliases or {})
    in_off, out_off, sem_off = [], [], []
    i0 = o0 = s0 = 0
    for ch in chunks:
        in_off.append(i0)
        out_off.append(o0)
        sem_off.append(s0)
        for ci, co in ch.alias.items():
            alias[n_in + i0 + ci] = n_out + o0 + co
        i0 += len(ch.ins)
        o0 += len(ch.out_shapes)
        s0 += ch.n_sem

    def wrapped(*refs):
        ins = refs[:n_in]
        cin = refs[n_in:n_in + i0]
        outs = refs[n_in + i0:n_in + i0 + n_out]
        cout = refs[n_in + i0 + n_out:n_in + i0 + n_out + o0]
        scr = refs[n_in + i0 + n_out + o0:n_in + i0 + n_out + o0 + n_scr]
        send_sems, recv_sems = refs[n_in + i0 + n_out + o0 + n_scr:]
        first = functools.reduce(lambda u, v: u & v, [pl.program_id(d) == 0 for d in range(len(grid))])
        last = functools.reduce(lambda u, v: u & v, [pl.program_id(d) == grid[d] - 1 for d in range(len(grid))])

        def each(phase):
            for ch, a, b, s in zip(chunks, in_off, out_off, sem_off):
                getattr(ch, phase)(cin[a:a + len(ch.ins)], cout[b:b + len(ch.out_shapes)], send_sems, recv_sems, s)

        pl.when(first)(lambda: each("start"))
        body(*ins, *outs, *scr)
        pl.when(last)(lambda: each("finish"))

    hbm = pl.BlockSpec(memory_space=pl.ANY)
    res = pl.pallas_call(
        wrapped, grid=grid, in_specs=list(in_specs) + [hbm] * i0, out_specs=o_specs + [hbm] * o0,
        out_shape=o_shapes + c_outs,
        scratch_shapes=list(scratch_shapes) + [pltpu.SemaphoreType.DMA((s0,)), pltpu.SemaphoreType.DMA((s0,))],
        input_output_aliases=alias, name=name,
        compiler_params=pltpu.CompilerParams(dimension_semantics=(ARB,) * len(grid), vmem_limit_bytes=VMEM_LIMIT_V7X,
                                             has_side_effects=True))(*args, *c_ins)
    for ch, b in zip(chunks, out_off):
        ch.done(list(res[n_out + b:n_out + b + len(ch.out_shapes)]))
    return res[0] if single else list(res[:n_out])


def _mm_nn(a, b, *, name, q=None, out_dtype=F32):
    m, k = a.shape
    n = b.shape[-1]
    tm, tn = _mm_tiles(m, k, n, jnp.dtype(out_dtype).itemsize)

    def body(a_ref, b_ref, o_ref):
        o_ref[...] = _dot(a_ref[...], b_ref[...]).astype(o_ref.dtype)

    return _hosted_call(
        body, grid=(m // tm, n // tn),
        in_specs=[pl.BlockSpec((tm, k), lambda i, j: (i, 0)),
                  pl.BlockSpec((None, k, tn), lambda i, j: (0, 0, j))],
        out_specs=pl.BlockSpec((tm, tn), lambda i, j: (i, j)),
        out_shape=jax.ShapeDtypeStruct((m, n), out_dtype),
        args=(a, b), name=name, q=q, flops=2.0 * m * k * n)


def _mm_nt(a, b, *, name, q=None, out_dtype=F32):
    m, k = a.shape
    n = b.shape[-2]
    tm, tn = _mm_tiles(m, k, n, jnp.dtype(out_dtype).itemsize)

    def body(a_ref, b_ref, o_ref):
        o_ref[...] = _dot_nt(a_ref[...], b_ref[...]).astype(o_ref.dtype)

    return _hosted_call(
        body, grid=(m // tm, n // tn),
        in_specs=[pl.BlockSpec((tm, k), lambda i, j: (i, 0)),
                  pl.BlockSpec((None, tn, k), lambda i, j: (0, j, 0))],
        out_specs=pl.BlockSpec((tm, tn), lambda i, j: (i, j)),
        out_shape=jax.ShapeDtypeStruct((m, n), out_dtype),
        args=(a, b), name=name, q=q, flops=2.0 * m * k * n)


def _mm_nn_slots(a, b4, *, name, q=None, out_dtype=F32):
    m, k = a.shape
    s_, _, _, c = b4.shape
    ob = jnp.dtype(out_dtype).itemsize
    tm = max(t for t in _divisors(m, 256) if 2 * (t * k * 2 + k * c * 2 + t * c * ob) <= MM_VMEM_BUDGET)

    def body(a_ref, b_ref, o_ref):
        o_ref[...] = _dot(a_ref[...], b_ref[...]).astype(o_ref.dtype)

    return _hosted_call(
        body, grid=(m // tm, s_),
        in_specs=[pl.BlockSpec((tm, k), lambda i, j: (i, 0)),
                  pl.BlockSpec((None, None, k, c), lambda i, j: (j, 0, 0, 0))],
        out_specs=pl.BlockSpec((tm, c), lambda i, j: (i, j)),
        out_shape=jax.ShapeDtypeStruct((m, s_ * c), out_dtype),
        args=(a, b4), name=name, q=q, flops=2.0 * m * k * s_ * c)


def _mm_tn_slots(a, b, *, name, slot_cols, n_slots, first_slot=0, q=None, out=None):
    k, m = a.shape
    c = slot_cols
    tk, tm, _ = _mm_tn_tiles(k, m, c, whole_n=True)

    def body(a_ref, b_ref, *rest):
        o_ref = rest[-1]
        part = _dot_tn(a_ref[...], b_ref[...])

        @pl.when(pl.program_id(2) == 0)
        def _():
            o_ref[...] = part

        @pl.when(pl.program_id(2) > 0)
        def _():
            o_ref[...] += part

    in_specs = [pl.BlockSpec((tk, tm), lambda i, j, s: (s, i)), pl.BlockSpec((tk, c), lambda i, j, s: (s, j))]
    args = (a, b)
    if out is not None:
        in_specs.append(pl.BlockSpec(memory_space=pl.ANY))
        args = (a, b, out)
    return _hosted_call(
        body, grid=(m // tm, b.shape[-1] // c, k // tk), in_specs=in_specs,
        out_specs=pl.BlockSpec((None, None, tm, c), lambda i, j, s: (first_slot + j, 0, i, 0)),
        out_shape=jax.ShapeDtypeStruct((n_slots, 1, m, c), F32),
        aliases={2: 0} if out is not None else None,
        args=args, name=name, q=q, flops=2.0 * m * k * b.shape[-1])


def _mm_tn(a, b, *, name, q=None, out=None, n_total=None, col_block_offset=0):
    k, m = a.shape
    n = b.shape[-1]
    tk, tm, tn = _mm_tn_tiles(k, m, n)
    off = col_block_offset * (n // tn)

    def body(a_ref, b_ref, *rest):
        o_ref = rest[-1]
        part = _dot_tn(a_ref[...], b_ref[...])

        @pl.when(pl.program_id(2) == 0)
        def _():
            o_ref[...] = part

        @pl.when(pl.program_id(2) > 0)
        def _():
            o_ref[...] += part

    in_specs = [pl.BlockSpec((tk, tm), lambda i, j, s: (s, i)), pl.BlockSpec((tk, tn), lambda i, j, s: (s, j))]
    args = (a, b)
    if out is not None:
        in_specs.append(pl.BlockSpec(memory_space=pl.ANY))
        args = (a, b, out)
    return _hosted_call(
        body, grid=(m // tm, n // tn, k // tk), in_specs=in_specs,
        out_specs=pl.BlockSpec((None, tm, tn), lambda i, j, s: (0, i, j + off)),
        out_shape=jax.ShapeDtypeStruct((1, m, n_total or n), F32),
        aliases={2: 0} if out is not None else None,
        args=args, name=name, q=q, flops=2.0 * m * k * n)


def _mm_ffn_dh(dg, dv, w4, *, name, q=None):
    m, f = dg.shape
    n_slots, _, d, c = w4.shape
    tm, tn = _mm_tiles(m, 2 * f, d, 4)

    def body(dg_ref, dv_ref, *rest):
        w_refs, o_ref = rest[:n_slots], rest[n_slots]
        acc = None
        for s, w_ref in enumerate(w_refs):
            x_ref = dg_ref if s < n_slots // 2 else dv_ref
            off = (s % (n_slots // 2)) * c
            part = _dot_nt(x_ref[:, off:off + c], w_ref[...])
            acc = part if acc is None else acc + part
        o_ref[...] = acc

    wspec = lambda s: pl.BlockSpec((None, None, tn, c), lambda i, j: (s, 0, j, 0))
    return _hosted_call(
        body, grid=(m // tm, d // tn),
        in_specs=[pl.BlockSpec((tm, f), lambda i, j: (i, 0)),
                  pl.BlockSpec((tm, f), lambda i, j: (i, 0))] + [wspec(s) for s in range(n_slots)],
        out_specs=pl.BlockSpec((tm, tn), lambda i, j: (i, j)),
        out_shape=jax.ShapeDtypeStruct((m, d), F32),
        args=(dg, dv) + (w4,) * n_slots, name=name, q=q, flops=4.0 * m * f * d)


def _norm_fwd(x, g, *, name):
    n, d = x.shape
    tm = _tile(n, 256, SUBLANE)

    def body(x_ref, g_ref, o_ref):
        o_ref[...] = _rms_fwd(x_ref[...], g_ref[...]).astype(o_ref.dtype)

    return pl.pallas_call(
        body, grid=(n // tm,),
        in_specs=[pl.BlockSpec((tm, d), lambda i: (i, 0)), pl.BlockSpec((1, d), lambda i: (0, 0))],
        out_specs=pl.BlockSpec((tm, d), lambda i: (i, 0)),
        out_shape=jax.ShapeDtypeStruct((n, d), MXU),
        name=name, compiler_params=_cp((PAR,)))(x, g)


def _norm_bwd_dg(dy, x, g, *, name):
    n, d = x.shape
    tm = _tile(n, 256, SUBLANE)

    def body(dy_ref, x_ref, g_ref, dg_ref):
        @pl.when(pl.program_id(0) == 0)
        def _():
            dg_ref[...] = jnp.zeros_like(dg_ref)
        _, dg = _rms_bwd(dy_ref[...], x_ref[...], g_ref[...])
        dg_ref[...] += dg

    return pl.pallas_call(
        body, grid=(n // tm,),
        in_specs=[pl.BlockSpec((tm, d), lambda i: (i, 0)), pl.BlockSpec((tm, d), lambda i: (i, 0)),
                  pl.BlockSpec((1, d), lambda i: (0, 0))],
        out_specs=pl.BlockSpec((1, d), lambda i: (0, 0)),
        out_shape=jax.ShapeDtypeStruct((1, d), F32),
        name=name, compiler_params=_cp((ARB,)))(dy, x, g)


def _resid_norm_fwd(x, y, g_post, g_pres, *, name, q=None):
    n, d = x.shape
    tm = _tile(n, 256, SUBLANE)
    nh = len(g_pres)

    def body(x_ref, y_ref, gp_ref, *rest):
        gpre = rest[:nh]
        xo_ref = rest[nh]
        h_refs = rest[nh + 1:]
        xo = x_ref[...] + _rms_fwd(y_ref[...], gp_ref[...])
        xo_ref[...] = xo
        for g_ref, h_ref in zip(gpre, h_refs):
            h_ref[...] = _rms_fwd(xo, g_ref[...]).astype(h_ref.dtype)

    row = pl.BlockSpec((tm, d), lambda i: (i, 0))
    vec = pl.BlockSpec((1, d), lambda i: (0, 0))
    outs = _hosted_call(
        body, grid=(n // tm,),
        in_specs=[row, row, vec] + [vec] * nh,
        out_specs=[row] + [row] * nh,
        out_shape=[jax.ShapeDtypeStruct((n, d), F32)] + [jax.ShapeDtypeStruct((n, d), MXU)] * nh,
        args=(x, y, g_post, *g_pres), name=name, q=q, budget_us=HOST_US["resid"])
    return outs[0], list(outs[1:])


def _loss_fwd(x, y, g_post, target, *, name):
    n, d = x.shape
    tm = _tile(n, 256, SUBLANE)

    def body(x_ref, y_ref, gp_ref, t_ref, dx_ref, sq_ref):
        @pl.when(pl.program_id(0) == 0)
        def _():
            sq_ref[...] = jnp.zeros_like(sq_ref)
        err = x_ref[...] + _rms_fwd(y_ref[...], gp_ref[...]) - t_ref[...]
        dx_ref[...] = err * (1.0 / d)
        sq_ref[...] += jnp.sum(err * err, axis=0, keepdims=True)

    row = pl.BlockSpec((tm, d), lambda i: (i, 0))
    vec = pl.BlockSpec((1, d), lambda i: (0, 0))
    return pl.pallas_call(
        body, grid=(n // tm,),
        in_specs=[row, row, vec, row],
        out_specs=[row, vec],
        out_shape=[jax.ShapeDtypeStruct((n, d), F32), jax.ShapeDtypeStruct((1, d), F32)],
        name=name, compiler_params=_cp((ARB,)))(x, y, g_post, target)


def _resid_norm_bwd(dx_out, dhs, x_out, g_pres, y, g_post, *, name, q=None):
    n, d = dx_out.shape
    tm = _tile(n, 256, SUBLANE)
    nh = len(dhs)
    has_y = y is not None

    def body(*refs):
        it = iter(refs)
        dxo_ref = next(it)
        dh_refs = [next(it) for _ in range(nh)]
        xo_ref = next(it) if nh else None
        gpre_refs = [next(it) for _ in range(nh)]
        y_ref = next(it) if has_y else None
        gpost_ref = next(it) if has_y else None
        g_out = next(it)
        dy_out = next(it) if has_y else None
        dgpre_out = [next(it) for _ in range(nh)]
        dgpost_out = next(it) if has_y else None

        @pl.when(pl.program_id(0) == 0)
        def _():
            for r in dgpre_out:
                r[...] = jnp.zeros_like(r)
            if has_y:
                dgpost_out[...] = jnp.zeros_like(dgpost_out)

        g = dxo_ref[...]
        if nh:
            xo = xo_ref[...]
            for dh_ref, gp_ref, dg_ref in zip(dh_refs, gpre_refs, dgpre_out):
                dx, dg = _rms_bwd(dh_ref[...], xo, gp_ref[...])
                g = g + dx
                dg_ref[...] += dg
        g_out[...] = g
        if has_y:
            dy, dg = _rms_bwd(g, y_ref[...], gpost_ref[...])
            dy_out[...] = dy.astype(dy_out.dtype)
            dgpost_out[...] += dg

    row = pl.BlockSpec((tm, d), lambda i: (i, 0))
    vec = pl.BlockSpec((1, d), lambda i: (0, 0))
    ins, in_specs = [dx_out], [row]
    ins += list(dhs)
    in_specs += [row] * nh
    if nh:
        ins.append(x_out)
        in_specs.append(row)
    ins += list(g_pres)
    in_specs += [vec] * nh
    if has_y:
        ins += [y, g_post]
        in_specs += [row, vec]
    out_specs, out_shape = [row], [jax.ShapeDtypeStruct((n, d), F32)]
    if has_y:
        out_specs.append(row)
        out_shape.append(jax.ShapeDtypeStruct((n, d), MXU))
    out_specs += [vec] * nh
    out_shape += [jax.ShapeDtypeStruct((1, d), F32)] * nh
    if has_y:
        out_specs.append(vec)
        out_shape.append(jax.ShapeDtypeStruct((1, d), F32))
    outs = list(_hosted_call(
        body, grid=(n // tm,), in_specs=in_specs, out_specs=out_specs, out_shape=out_shape,
        args=tuple(ins), name=name, q=q, budget_us=HOST_US["resid_bwd"]))
    g = outs.pop(0)
    dy = outs.pop(0) if has_y else None
    dgpre = [outs.pop(0) for _ in range(nh)]
    dgpost = outs.pop(0) if has_y else None
    return g, dy, dgpre, dgpost


def _ffn_conv(up, w_ref, b_ref):
    return (w_ref[0:1, :] * _shift_down_edge(up, 2) + w_ref[1:2, :] * _shift_down_edge(up, 1)
            + w_ref[2:3, :] * up + b_ref[...])


def _ffn_act_fwd(up, wconv, bconv, bsz, *, name, q=None):
    n, f2 = up.shape
    f = f2 // 2
    t = n // bsz
    tc = _tile(f, 256)
    nf = f // tc

    def body(ug_ref, uv_ref, wg_ref, wv_ref, bg_ref, bv_ref, o_ref, g_ref, v_ref):
        g = _ffn_conv(ug_ref[...].astype(F32), wg_ref, bg_ref)
        v = _ffn_conv(uv_ref[...].astype(F32), wv_ref, bv_ref)
        g_ref[...] = g.astype(g_ref.dtype)
        v_ref[...] = v.astype(v_ref.dtype)
        o_ref[...] = (_gelu(g) * v).astype(o_ref.dtype)

    blk = pl.BlockSpec((t, tc), lambda b, j: (b, j))
    return _hosted_call(
        body, grid=(bsz, nf),
        in_specs=[blk, pl.BlockSpec((t, tc), lambda b, j: (b, j + nf)),
                  pl.BlockSpec((FFN_CONV, tc), lambda b, j: (0, j)),
                  pl.BlockSpec((FFN_CONV, tc), lambda b, j: (0, j + nf)),
                  pl.BlockSpec((1, tc), lambda b, j: (0, j)),
                  pl.BlockSpec((1, tc), lambda b, j: (0, j + nf))],
        out_specs=[blk, blk, blk],
        out_shape=[jax.ShapeDtypeStruct((n, f), MXU)] * 3,
        args=(up, up, wconv, wconv, bconv, bconv), name=name, q=q, budget_us=HOST_US["ffn_act"])


def _ffn_act_bwd(up, ug, uv, dact, wconv, bsz, *, name, q=None):
    n, f2 = up.shape
    f = f2 // 2
    t = n // bsz
    tc = _tile(f, 256)
    nf = f // tc

    def body(xg_ref, xv_ref, g_ref, v_ref, da_ref, wg_ref, wv_ref,
             dug_ref, duv_ref, dwg_ref, dwv_ref, dbg_ref, dbv_ref):
        @pl.when(pl.program_id(1) == 0)
        def _():
            for r in (dwg_ref, dwv_ref, dbg_ref, dbv_ref):
                r[...] = jnp.zeros_like(r)

        gl, dgl = _gelu_and_grad(g_ref[...].astype(F32))
        da = da_ref[...].astype(F32)
        dg = da * v_ref[...].astype(F32) * dgl
        dv = da * gl

        def conv_bwd(du, w_ref, x_ref, dx_ref, dw_ref, db_ref):
            du1, du2 = _shift_up_edge(du, 1), _shift_up_edge(du, 2)
            dx_ref[...] = (w_ref[2:3, :] * du + w_ref[1:2, :] * du1 + w_ref[0:1, :] * du2).astype(dx_ref.dtype)
            x = x_ref[...].astype(F32)
            dw_ref[0:1, :] += jnp.sum(x * du2, axis=0, keepdims=True)
            dw_ref[1:2, :] += jnp.sum(x * du1, axis=0, keepdims=True)
            dw_ref[2:3, :] += jnp.sum(x * du, axis=0, keepdims=True)
            db_ref[...] += jnp.sum(du, axis=0, keepdims=True)

        conv_bwd(dg, wg_ref, xg_ref, dug_ref, dwg_ref, dbg_ref)
        conv_bwd(dv, wv_ref, xv_ref, duv_ref, dwv_ref, dbv_ref)

    blk = pl.BlockSpec((t, tc), lambda j, b: (b, j))
    wspec = pl.BlockSpec((FFN_CONV, tc), lambda j, b: (0, j))
    bspec = pl.BlockSpec((1, tc), lambda j, b: (0, j))
    outs = _hosted_call(
        body, grid=(nf, bsz),
        in_specs=[blk, pl.BlockSpec((t, tc), lambda j, b: (b, j + nf)), blk, blk, blk,
                  wspec, pl.BlockSpec((FFN_CONV, tc), lambda j, b: (0, j + nf))],
        out_specs=[blk, blk, wspec, wspec, bspec, bspec],
        out_shape=[jax.ShapeDtypeStruct((n, f), MXU), jax.ShapeDtypeStruct((n, f), MXU),
                   jax.ShapeDtypeStruct((FFN_CONV, f), F32), jax.ShapeDtypeStruct((FFN_CONV, f), F32),
                   jax.ShapeDtypeStruct((1, f), F32), jax.ShapeDtypeStruct((1, f), F32)],
        args=(up, up, ug, uv, dact, wconv, wconv), name=name, q=q, budget_us=HOST_US["ffn_act_bwd"])
    dug, duv, dwg, dwv, dbg, dbv = outs
    return dug, duv, jnp.concatenate([dwg, dwv], axis=1), jnp.concatenate([dbg, dbv], axis=1)


def _mem_attn_fwd(proj, q_col_block, mkv, ycat, bsz, *, name, q=None):
    n = proj.shape[0]
    t = n // bsz
    tq = _tile(t, 512, SUBLANE)
    nt = t // tq
    scale = HEAD_DIM ** -0.5

    def body(q_ref, kv_ref, old_ref, o_ref):
        del old_ref
        outs = []
        for h in range(MEM_HEADS):
            sl = slice(h * HEAD_DIM, (h + 1) * HEAD_DIM)
            q = q_ref[:, sl].astype(MXU)
            k = kv_ref[:, sl].astype(MXU)
            v = kv_ref[:, MEM_WIDTH + h * HEAD_DIM: MEM_WIDTH + (h + 1) * HEAD_DIM].astype(MXU)
            s = _dot_nt(q, k) * scale
            m = jnp.max(s, axis=-1, keepdims=True)
            p = jnp.exp(s - m)
            p = p / jnp.sum(p, axis=-1, keepdims=True)
            outs.append(_dot(p.astype(MXU), v))
        o_ref[...] = jnp.concatenate(outs, axis=-1).astype(o_ref.dtype)

    return _hosted_call(
        body, grid=(bsz, nt),
        in_specs=[pl.BlockSpec((tq, MEM_WIDTH), lambda b, i: (b * nt + i, q_col_block)),
                  pl.BlockSpec((MEM_LEN, 2 * MEM_WIDTH), lambda b, i: (b, 0)),
                  pl.BlockSpec(memory_space=pl.ANY)],
        out_specs=pl.BlockSpec((tq, MEM_WIDTH), lambda b, i: (b * nt + i, MIX_WIDTH // MEM_WIDTH)),
        out_shape=jax.ShapeDtypeStruct(ycat.shape, ycat.dtype),
        aliases={2: 0}, args=(proj, mkv, ycat), name=name, q=q, budget_us=HOST_US["mem_attn_fwd"])


def _mem_attn_bwd(proj, q_col_block, mkv, dycat, dproj, bsz, *, name, q=None):
    n = proj.shape[0]
    t = n // bsz
    tq = _tile(t, 512, SUBLANE)
    nt = t // tq
    scale = HEAD_DIM ** -0.5

    def body(q_ref, kv_ref, do_ref, old_ref, dq_ref, dkv_ref):
        del old_ref

        @pl.when(pl.program_id(1) == 0)
        def _():
            dkv_ref[...] = jnp.zeros_like(dkv_ref)

        dqs, dks, dvs = [], [], []
        for h in range(MEM_HEADS):
            sl = slice(h * HEAD_DIM, (h + 1) * HEAD_DIM)
            q = q_ref[:, sl].astype(MXU)
            k = kv_ref[:, sl].astype(MXU)
            v = kv_ref[:, MEM_WIDTH + h * HEAD_DIM: MEM_WIDTH + (h + 1) * HEAD_DIM].astype(MXU)
            do = do_ref[:, sl].astype(MXU)
            s = _dot_nt(q, k) * scale
            m = jnp.max(s, axis=-1, keepdims=True)
            p = jnp.exp(s - m)
            p = p / jnp.sum(p, axis=-1, keepdims=True)
            dvs.append(_dot_tn(p.astype(MXU), do))
            dp = _dot_nt(do, v)
            ds = (p * (dp - jnp.sum(dp * p, axis=-1, keepdims=True)) * scale).astype(MXU)
            dqs.append(_dot(ds, k))
            dks.append(_dot_tn(ds, q))
        dq_ref[...] = jnp.concatenate(dqs, axis=-1).astype(dq_ref.dtype)
        dkv_ref[...] += jnp.concatenate(dks + dvs, axis=-1)

    return _hosted_call(
        body, grid=(bsz, nt),
        in_specs=[pl.BlockSpec((tq, MEM_WIDTH), lambda b, i: (b * nt + i, q_col_block)),
                  pl.BlockSpec((MEM_LEN, 2 * MEM_WIDTH), lambda b, i: (b, 0)),
                  pl.BlockSpec((tq, MEM_WIDTH), lambda b, i: (b * nt + i, MIX_WIDTH // MEM_WIDTH)),
                  pl.BlockSpec(memory_space=pl.ANY)],
        out_specs=[pl.BlockSpec((tq, MEM_WIDTH), lambda b, i: (b * nt + i, q_col_block)),
                   pl.BlockSpec((MEM_LEN, 2 * MEM_WIDTH), lambda b, i: (b, 0))],
        out_shape=[jax.ShapeDtypeStruct(dproj.shape, dproj.dtype),
                   jax.ShapeDtypeStruct((bsz * MEM_LEN, 2 * MEM_WIDTH), F32)],
        aliases={3: 0}, args=(proj, mkv, dycat, dproj), name=name, q=q, budget_us=HOST_US["mem_attn_bwd"])


def _swa_scores(q, k, h, dist, mask, sink):
    s = _dot_nt(q, k) * (HEAD_DIM ** -0.5)
    s = jnp.where(mask, s - SLOPES[h] * dist, -jnp.inf)
    m = jnp.maximum(jnp.max(s, axis=-1, keepdims=True), sink)
    p = jnp.exp(s - m)
    psink = jnp.exp(sink - m)
    inv = 1.0 / (jnp.sum(p, axis=-1, keepdims=True) + psink)
    return p * inv, psink * inv


def _swa_mask(n):
    qi = lax.broadcasted_iota(jnp.int32, (WINDOW, 2 * WINDOW), 0) + WINDOW
    ki = lax.broadcasted_iota(jnp.int32, (WINDOW, 2 * WINDOW), 1)
    dist = qi - ki
    mask = (dist >= 0) & (dist < WINDOW) & ((n > 0) | (ki >= WINDOW))
    return dist.astype(F32), mask


def _swa_fwd(proj, kv, sinks, bsz, *, name, q=None):
    n_tok = proj.shape[0]
    nb = n_tok // bsz // WINDOW
    kvw = SWA_KV_HEADS * HEAD_DIM

    def body(sink_ref, q_ref, kvp_ref, kvc_ref, o_ref):
        n = pl.program_id(1)
        dist, mask = _swa_mask(n)
        kk = jnp.concatenate([kvp_ref[:, :kvw], kvc_ref[:, :kvw]], axis=0).astype(MXU)
        vv = jnp.concatenate([kvp_ref[:, kvw:], kvc_ref[:, kvw:]], axis=0).astype(MXU)
        outs = []
        for h in range(SWA_HEADS):
            c = h // SWA_GROUP
            q = q_ref[:, h * HEAD_DIM:(h + 1) * HEAD_DIM].astype(MXU)
            p, _ = _swa_scores(q, kk[:, c * HEAD_DIM:(c + 1) * HEAD_DIM], h, dist, mask, sink_ref[h])
            outs.append(_dot(p.astype(MXU), vv[:, c * HEAD_DIM:(c + 1) * HEAD_DIM]))
        o_ref[...] = jnp.concatenate(outs, axis=-1).astype(o_ref.dtype)

    return _hosted_call(
        body, grid=(bsz, nb),
        in_specs=[pl.BlockSpec(memory_space=pltpu.SMEM),
                  pl.BlockSpec((WINDOW, MIX_WIDTH), lambda b, n: (b * nb + n, 0)),
                  pl.BlockSpec((WINDOW, 2 * kvw), lambda b, n: (b * nb + jnp.maximum(n - 1, 0), 0)),
                  pl.BlockSpec((WINDOW, 2 * kvw), lambda b, n: (b * nb + n, 0))],
        out_specs=pl.BlockSpec((WINDOW, MIX_WIDTH), lambda b, n: (b * nb + n, 0)),
        out_shape=jax.ShapeDtypeStruct((n_tok, D_MODEL), MXU),
        args=(sinks, proj, kv, kv), name=name, q=q, budget_us=HOST_US["swa_fwd"])


def _swa_bwd(proj, kv, sinks, dycat, bsz, *, name, q=None):
    n_tok = proj.shape[0]
    nb = n_tok // bsz // WINDOW
    kvw = SWA_KV_HEADS * HEAD_DIM

    def body(sink_ref, q_ref, kvp_ref, kvc_ref, do_ref, dq_ref, dkvc_ref, dkvp_ref, dsink_ref):
        n = pl.program_id(1)

        @pl.when((pl.program_id(0) == 0) & (n == 0))
        def _():
            dsink_ref[...] = jnp.zeros_like(dsink_ref)

        dist, mask = _swa_mask(n)
        kk = jnp.concatenate([kvp_ref[:, :kvw], kvc_ref[:, :kvw]], axis=0).astype(MXU)
        vv = jnp.concatenate([kvp_ref[:, kvw:], kvc_ref[:, kvw:]], axis=0).astype(MXU)
        lane = lax.broadcasted_iota(jnp.int32, (SUBLANE, LANE), 1)
        dqs = []
        dks = [None] * SWA_KV_HEADS
        dvs = [None] * SWA_KV_HEADS
        dsink = jnp.zeros((SUBLANE, LANE), F32)
        for h in range(SWA_HEADS):
            c = h // SWA_GROUP
            k = kk[:, c * HEAD_DIM:(c + 1) * HEAD_DIM]
            v = vv[:, c * HEAD_DIM:(c + 1) * HEAD_DIM]
            q = q_ref[:, h * HEAD_DIM:(h + 1) * HEAD_DIM].astype(MXU)
            do = do_ref[:, h * HEAD_DIM:(h + 1) * HEAD_DIM].astype(MXU)
            p, psink = _swa_scores(q, k, h, dist, mask, sink_ref[h])
            dv = _dot_tn(p.astype(MXU), do)
            dp = _dot_nt(do, v)
            rs = jnp.sum(dp * p, axis=-1, keepdims=True)
            ds = (p * (dp - rs) * (HEAD_DIM ** -0.5)).astype(MXU)
            dsink = dsink + jnp.where(lane == h, jnp.sum(-psink * rs, axis=0, keepdims=True), 0.0)
            dqs.append(_dot(ds, k))
            dk = _dot_tn(ds, q)
            dks[c] = dk if dks[c] is None else dks[c] + dk
            dvs[c] = dv if dvs[c] is None else dvs[c] + dv
        dq_ref[...] = jnp.concatenate(dqs, axis=-1).astype(dq_ref.dtype)
        dkv = jnp.concatenate(dks + dvs, axis=-1)
        dkvp_ref[...] = dkv[:WINDOW]
        dkvc_ref[...] = dkv[WINDOW:]
        dsink_ref[...] += dsink

    qspec = pl.BlockSpec((WINDOW, MIX_WIDTH), lambda b, n: (b * nb + n, 0))
    kvspec = pl.BlockSpec((WINDOW, 2 * kvw), lambda b, n: (b * nb + n, 0))
    return _hosted_call(
        body, grid=(bsz, nb),
        in_specs=[pl.BlockSpec(memory_space=pltpu.SMEM), qspec,
                  pl.BlockSpec((WINDOW, 2 * kvw), lambda b, n: (b * nb + jnp.maximum(n - 1, 0), 0)),
                  kvspec, qspec],
        out_specs=[qspec, kvspec, kvspec, pl.BlockSpec((SUBLANE, LANE), lambda b, n: (0, 0))],
        out_shape=[jax.ShapeDtypeStruct((n_tok, D_MODEL), MXU),
                   jax.ShapeDtypeStruct((n_tok, 2 * kvw), F32),
                   jax.ShapeDtypeStruct((n_tok, 2 * kvw), F32),
                   jax.ShapeDtypeStruct((SUBLANE, LANE), F32)],
        args=(sinks, proj, kv, kv, dycat), name=name, q=q, budget_us=HOST_US["swa_bwd"])


def _swa_dkv_combine(curs, prevs, bsz, *, name):
    n_tok, w = curs[0].shape
    nb = n_tok // bsz // WINDOW
    k = len(curs)

    def body(*refs):
        o_ref = refs[-1]
        n = pl.program_id(1)
        acc = refs[0][...]
        for r in refs[1:k]:
            acc = acc + r[...]
        nxt = refs[k][...]
        for r in refs[k + 1:2 * k]:
            nxt = nxt + r[...]
        o_ref[...] = (acc + jnp.where(n < nb - 1, nxt, 0.0)).astype(o_ref.dtype)

    cur = pl.BlockSpec((WINDOW, w), lambda b, n: (b * nb + n, 0))
    prv = pl.BlockSpec((WINDOW, w), lambda b, n: (b * nb + jnp.minimum(n + 1, nb - 1), 0))
    return pl.pallas_call(
        body, grid=(bsz, nb), in_specs=[cur] * k + [prv] * k, out_specs=cur,
        out_shape=jax.ShapeDtypeStruct((n_tok, w), MXU),
        name=name, compiler_params=_cp((PAR, PAR)))(*curs, *prevs)


def _lru_gates(ux, halo, ext_ref, wc_ref, bc_ref, wr_ref, br_ref, wi_ref, bi_ref, lam_ref):
    tt = ux.shape[0]
    ext_ref[0:SUBLANE, :] = halo
    ext_ref[SUBLANE:, :] = ux
    xs = [ux] + [ext_ref[pl.ds(SUBLANE - k, tt), :] for k in range(1, LRU_CONV)]
    xc = bc_ref[...] + wc_ref[3:4, :] * xs[0] + wc_ref[2:3, :] * xs[1] + wc_ref[1:2, :] * xs[2] + wc_ref[0:1, :] * xs[3]
    pre_r, pre_i = [], []
    for blk in range(MIX_WIDTH // GATE_TILE):
        xb = xc[:, blk * GATE_TILE:(blk + 1) * GATE_TILE].astype(MXU)
        pre_r.append(_dot(xb, wr_ref[blk]))
        pre_i.append(_dot(xb, wi_ref[blk]))
    r = jax.nn.sigmoid(jnp.concatenate(pre_r, axis=-1) + br_ref[...])
    i = jax.nn.sigmoid(jnp.concatenate(pre_i, axis=-1) + bi_ref[...])
    nlam = -lam_ref[...]
    sp = jnp.maximum(nlam, 0.0) + jnp.log(1.0 + jnp.exp(-jnp.abs(nlam)))
    log_a = -LRU_C * r * sp
    a = jnp.exp(log_a)
    om = -jnp.tanh(log_a) * (a * a + 1.0)
    s = jnp.sqrt(om)
    return xs, xc, r, i, sp, a, s


def _lru_fwd(proj, wconv, bconv, wr, br, wi, bi, lam, bsz, *, name, q=None):
    n_tok = proj.shape[0]
    t = n_tok // bsz
    tt = _tile(t, 256, SUBLANE)
    nt = t // tt
    w = MIX_WIDTH
    ng = tt // SUBLANE

    def body(pg_ref, halo_ref, wc_ref, bc_ref, wr_ref, br_ref, wi_ref, bi_ref, lam_ref,
             y_ref, h_ref, ext_ref, a_ref, b_ref, carry_ref):
        ti = pl.program_id(1)

        @pl.when(ti == 0)
        def _():
            carry_ref[...] = jnp.zeros_like(carry_ref)

        gate = pg_ref[:, :w]
        ux = pg_ref[:, w:]
        halo = jnp.where(ti > 0, halo_ref[...], 0.0)
        _, xc, _, i, _, a, s = _lru_gates(ux, halo, ext_ref, wc_ref, bc_ref, wr_ref, br_ref, wi_ref, bi_ref, lam_ref)
        a_ref[...] = a
        b_ref[...] = s * (i * xc)
        row = lax.broadcasted_iota(jnp.int32, (SUBLANE, w), 0)

        def group(g, hprev):
            off = pl.multiple_of(g * SUBLANE, SUBLANE)
            ca = a_ref[pl.ds(off, SUBLANE), :]
            cb = b_ref[pl.ds(off, SUBLANE), :]
            for d in (1, 2, 4):
                a_sh = jnp.where(row >= d, pltpu.roll(ca, d, axis=0), 1.0)
                b_sh = jnp.where(row >= d, pltpu.roll(cb, d, axis=0), 0.0)
                cb = ca * b_sh + cb
                ca = ca * a_sh
            h = ca * hprev + cb
            b_ref[pl.ds(off, SUBLANE), :] = h
            return jnp.broadcast_to(h[SUBLANE - 1:SUBLANE, :], (SUBLANE, w))

        carry_ref[...] = lax.fori_loop(0, ng, group, carry_ref[...])
        h = b_ref[...]
        h_ref[...] = h
        y_ref[...] = (h * _gelu(gate)).astype(y_ref.dtype)

    vec = lambda r: pl.BlockSpec((r, w), lambda b, i: (0, 0))
    wspec = pl.BlockSpec((w // GATE_TILE, GATE_TILE, GATE_TILE), lambda b, i: (0, 0, 0))
    hb = tt // SUBLANE
    return _hosted_call(
        body, grid=(bsz, nt),
        in_specs=[pl.BlockSpec((tt, 2 * w), lambda b, i: (b * nt + i, 0)),
                  pl.BlockSpec((SUBLANE, w), lambda b, i: (jnp.maximum((b * nt + i) * hb - 1, 0), 1)),
                  vec(LRU_CONV), vec(1), wspec, vec(1), wspec, vec(1), vec(1)],
        out_specs=[pl.BlockSpec((tt, w), lambda b, i: (b * nt + i, 0)),
                   pl.BlockSpec((tt, w), lambda b, i: (b * nt + i, 0))],
        out_shape=[jax.ShapeDtypeStruct((n_tok, D_MODEL), MXU), jax.ShapeDtypeStruct((n_tok, w), F32)],
        scratch_shapes=[pltpu.VMEM((tt + SUBLANE, w), F32), pltpu.VMEM((tt, w), F32),
                        pltpu.VMEM((tt, w), F32), pltpu.VMEM((SUBLANE, w), F32)],
        args=(proj, proj, wconv, bconv, wr, br, wi, bi, lam), name=name, q=q, budget_us=HOST_US["lru_fwd"])


def _lru_bwd(proj, hs, dycat, wconv, bconv, wr, br, wi, bi, lam, bsz, *, name, q=None):
    n_tok = proj.shape[0]
    t = n_tok // bsz
    tt = _tile(t, 256, SUBLANE)
    nt = t // tt
    w = MIX_WIDTH
    ng = tt // SUBLANE
    nblk = w // GATE_TILE

    def body(pg_ref, halo_ref, h_ref, hhalo_ref, dy_ref, wc_ref, bc_ref, wr_ref, br_ref, wi_ref, bi_ref, lam_ref,
             dp_ref, dwc_ref, dbc_ref, dwr_ref, dbr_ref, dwi_ref, dbi_ref, dlam_ref,
             ext_ref, a_ref, c_ref, g_ref, gcarry_ref, xcarry_ref):
        bi_ = pl.program_id(0)
        ti = nt - 1 - pl.program_id(1)

        @pl.when((bi_ == 0) & (pl.program_id(1) == 0))
        def _():
            for r in (dwc_ref, dbc_ref, dwr_ref, dbr_ref, dwi_ref, dbi_ref, dlam_ref):
                r[...] = jnp.zeros_like(r)

        @pl.when(pl.program_id(1) == 0)
        def _():
            gcarry_ref[...] = jnp.zeros_like(gcarry_ref)
            xcarry_ref[...] = jnp.zeros_like(xcarry_ref)

        gate = pg_ref[:, :w]
        ux = pg_ref[:, w:]
        halo = jnp.where(ti > 0, halo_ref[...], 0.0)
        xs, xc, r, i, sp, a, s = _lru_gates(ux, halo, ext_ref, wc_ref, bc_ref, wr_ref, br_ref, wi_ref, bi_ref, lam_ref)
        h = h_ref[...]
        gl, dgl = _gelu_and_grad(gate)
        dy = dy_ref[...]
        dgate = dy * h * dgl
        row_t = lax.broadcasted_iota(jnp.int32, (tt, w), 0)
        g_ref[...] = dy * gl + jnp.where(row_t == tt - 1, gcarry_ref[0:1, :], 0.0)
        c_ref[...] = _shift_up(a, 1, row_t)
        row = lax.broadcasted_iota(jnp.int32, (SUBLANE, w), 0)

        a_ref[...] = a

        def group(k, gnext):
            off = pl.multiple_of((ng - 1 - k) * SUBLANE, SUBLANE)
            cc = c_ref[pl.ds(off, SUBLANE), :]
            cb = g_ref[pl.ds(off, SUBLANE), :]
            cb = cb + jnp.where(row == SUBLANE - 1, gnext, 0.0)
            cc = jnp.where(row == SUBLANE - 1, 0.0, cc)
            for d in (1, 2, 4):
                c_sh = jnp.where(row < SUBLANE - d, pltpu.roll(cc, SUBLANE - d, axis=0), 1.0)
                b_sh = jnp.where(row < SUBLANE - d, pltpu.roll(cb, SUBLANE - d, axis=0), 0.0)
                cb = cc * b_sh + cb
                cc = cc * c_sh
            g_ref[pl.ds(off, SUBLANE), :] = cb
            a0 = a_ref[pl.ds(off, SUBLANE), :]
            return jnp.broadcast_to(a0[0:1, :] * cb[0:1, :], (SUBLANE, w))

        gc = lax.fori_loop(0, ng, group, jnp.zeros((SUBLANE, w), F32))
        gcarry_ref[...] = gc
        gsc = g_ref[...]

        hhalo = jnp.where(ti > 0, hhalo_ref[SUBLANE - 1:SUBLANE, :], 0.0)
        hprev = jnp.where(row_t == 0, hhalo, pltpu.roll(h, 1, axis=0))
        gated = i * xc
        d_gated = gsc * s
        d_atot = gsc * hprev - (gsc * gated) * a / s
        d_loga = d_atot * a
        d_r = d_loga * (-LRU_C) * sp
        dlam_ref[...] += jnp.sum(d_loga * r, axis=0, keepdims=True) * (LRU_C * jax.nn.sigmoid(-lam_ref[...]))
        d_i = d_gated * xc
        d_xc = d_gated * i
        d_pr = d_r * r * (1.0 - r)
        d_pi = d_i * i * (1.0 - i)
        dbr_ref[...] += jnp.sum(d_pr, axis=0, keepdims=True)
        dbi_ref[...] += jnp.sum(d_pi, axis=0, keepdims=True)
        extra = []
        for blk in range(nblk):
            sl = slice(blk * GATE_TILE, (blk + 1) * GATE_TILE)
            xb = xc[:, sl].astype(MXU)
            dr_b = d_pr[:, sl].astype(MXU)
            di_b = d_pi[:, sl].astype(MXU)
            dwr_ref[blk] += _dot_tn(xb, dr_b)
            dwi_ref[blk] += _dot_tn(xb, di_b)
            extra.append(_dot_nt(dr_b, wr_ref[blk]) + _dot_nt(di_b, wi_ref[blk]))
        d_xc = d_xc + jnp.concatenate(extra, axis=-1)
        dbc_ref[...] += jnp.sum(d_xc, axis=0, keepdims=True)
        for k in range(LRU_CONV):
            dwc_ref[k:k + 1, :] += jnp.sum(d_xc * xs[LRU_CONV - 1 - k], axis=0, keepdims=True)
        ext_ref[0:tt, :] = d_xc
        ext_ref[tt:, :] = xcarry_ref[...]
        dux = wc_ref[3:4, :] * d_xc
        for k in range(LRU_CONV - 1):
            dux = dux + wc_ref[k:k + 1, :] * ext_ref[pl.ds(LRU_CONV - 1 - k, tt), :]
        xcarry_ref[...] = d_xc[0:SUBLANE, :]
        dp_ref[:, :w] = dgate.astype(dp_ref.dtype)
        dp_ref[:, w:] = dux.astype(dp_ref.dtype)

    vec = lambda r: pl.BlockSpec((r, w), lambda b, i: (0, 0))
    wspec = pl.BlockSpec((nblk, GATE_TILE, GATE_TILE), lambda b, i: (0, 0, 0))
    hb = tt // SUBLANE
    rblk = lambda b, i: b * nt + (nt - 1 - i)
    halo_idx = lambda b, i: jnp.maximum(rblk(b, i) * hb - 1, 0)
    wide = pl.BlockSpec((tt, 2 * w), lambda b, i: (rblk(b, i), 0))
    narrow = pl.BlockSpec((tt, w), lambda b, i: (rblk(b, i), 0))
    return _hosted_call(
        body, grid=(bsz, nt),
        in_specs=[wide, pl.BlockSpec((SUBLANE, w), lambda b, i: (halo_idx(b, i), 1)),
                  narrow, pl.BlockSpec((SUBLANE, w), lambda b, i: (halo_idx(b, i), 0)), narrow,
                  vec(LRU_CONV), vec(1), wspec, vec(1), wspec, vec(1), vec(1)],
        out_specs=[wide, vec(LRU_CONV), vec(1), wspec, vec(1), wspec, vec(1), vec(1)],
        out_shape=[jax.ShapeDtypeStruct((n_tok, 2 * w + MEM_WIDTH), MXU),
                   jax.ShapeDtypeStruct((LRU_CONV, w), F32), jax.ShapeDtypeStruct((1, w), F32),
                   jax.ShapeDtypeStruct((nblk, GATE_TILE, GATE_TILE), F32), jax.ShapeDtypeStruct((1, w), F32),
                   jax.ShapeDtypeStruct((nblk, GATE_TILE, GATE_TILE), F32), jax.ShapeDtypeStruct((1, w), F32),
                   jax.ShapeDtypeStruct((1, w), F32)],
        scratch_shapes=[pltpu.VMEM((tt + SUBLANE, w), F32), pltpu.VMEM((tt, w), F32), pltpu.VMEM((tt, w), F32),
                        pltpu.VMEM((tt, w), F32), pltpu.VMEM((SUBLANE, w), F32), pltpu.VMEM((SUBLANE, w), F32)],
        args=(proj, proj, hs, hs, dycat, wconv, bconv, wr, br, wi, bi, lam), name=name, q=q,
        budget_us=HOST_US["lru_bwd"])


def _gate_tiles(w):
    per = GATE_TILE // HEAD_DIM
    w4 = w.reshape(LRU_BLOCKS // per, per, HEAD_DIM, HEAD_DIM)
    eye = jnp.eye(per, dtype=w.dtype)
    return jnp.einsum("bnij,nm->bnimj", w4, eye).reshape(LRU_BLOCKS // per, GATE_TILE, GATE_TILE)


def _gate_blocks(t):
    per = GATE_TILE // HEAD_DIM
    t5 = t.reshape(LRU_BLOCKS // per, per, HEAD_DIM, per, HEAD_DIM)
    eye = jnp.eye(per, dtype=t.dtype)
    return jnp.einsum("bnimj,nm->bnij", t5, eye).reshape(LRU_BLOCKS, HEAD_DIM, HEAD_DIM)


def _row(v):
    return v.reshape(1, -1)


def _local_step(x, mem, target, p, wfull, push_grad, q):
    bsz, t, d = x.shape
    n = bsz * t
    x2d = x.reshape(n, d)
    tgt = target.reshape(n, d)
    mem2d = mem.reshape(bsz * MEM_LEN, d)
    wr_t = [_gate_tiles(p["w_rg_r"][j]).astype(MXU) for j in range(N_A)]
    wi_t = [_gate_tiles(p["w_rg_i"][j]).astype(MXU) for j in range(N_A)]

    mn = [_norm_fwd(mem2d, _row(p["g_mem"][l]), name=f"mem_norm{l}") for l in range(DEPTH)]
    mkv = [None] * DEPTH
    h = _norm_fwd(x2d, _row(p["g_mix_pre"][0]), name="in_norm")
    xin = x2d
    sv = []
    kv = hkv = None
    for l in range(DEPTH):
        s = {"xin": xin, "h": h}
        if q is not None:
            q.horizon = (l + 2) * GROUPS_PER_LAYER
        mkv[l] = _mm_nn(mn[l], wfull("w_mem_kv", l), name=f"mem_kv{l}", q=q)
        if l < N_A:
            proj = _mm_nn(h, wfull("w_in_a", l), name=f"in_proj{l}", q=q)
            ycat, hs = _lru_fwd(proj, p["w_conv_a"][l], _row(p["b_conv_a"][l]), wr_t[l], _row(p["b_rg_r"][l]),
                                wi_t[l], _row(p["b_rg_i"][l]), _row(p["lru_lambda"][l]), bsz, name=f"lru_fwd{l}", q=q)
            s["hs"] = hs
            qblk = 2 * MIX_WIDTH // MEM_WIDTH
        else:
            if l == N_A:
                kv = _mm_nn(hkv, wfull("w_kv", 0), name="kv_proj", q=q)
            proj = _mm_nn(h, wfull("w_in_b", l - N_A), name=f"in_proj{l}", q=q)
            ycat = _swa_fwd(proj, kv, p["sinks_b"][l - N_A], bsz, name=f"swa_fwd{l}", q=q)
            qblk = MIX_WIDTH // MEM_WIDTH
        ycat = _mem_attn_fwd(proj, qblk, mkv[l], ycat, bsz, name=f"mem_attn_fwd{l}", q=q)
        y = _mm_nn(ycat, wfull("w_mix_out", l), name=f"mix_out{l}", q=q)
        x1, (h2,) = _resid_norm_fwd(xin, y, _row(p["g_mix_post"][l]), [_row(p["g_ffn_pre"][l])], name=f"mix_resid{l}", q=q)
        up = _mm_nn_slots(h2, wfull("w_ffn_up", l), name=f"ffn_up{l}", q=q, out_dtype=MXU)
        act, ug, uv = _ffn_act_fwd(up, p["w_ffn_conv"][l], _row(p["b_ffn_conv"][l]), bsz, name=f"ffn_act{l}", q=q)
        f = _mm_nn(act, wfull("w_ffn_down", l), name=f"ffn_down{l}", q=q)
        s.update(proj=proj, qblk=qblk, ycat=ycat, y=y, x1=x1, h2=h2, up=up, ug=ug, uv=uv, act=act, f=f)
        sv.append(s)
        if l < DEPTH - 1:
            g_pres = [_row(p["g_mix_pre"][l + 1])] + ([_row(p["g_kv"])] if l + 1 == N_A else [])
            xin, hn = _resid_norm_fwd(x1, f, _row(p["g_ffn_post"][l]), g_pres, name=f"ffn_resid{l}", q=q)
            h = hn[0]
            if l + 1 == N_A:
                hkv = hn[1]
        else:
            g_tot, sq = _loss_fwd(x1, f, _row(p["g_ffn_post"][l]), tgt, name="loss")

    if q is not None:
        q.horizon = LAST_GROUP
    gs = {k: [None] * DEPTH for k in ("g_mix_pre", "g_mix_post", "g_ffn_pre", "g_ffn_post", "g_mem",
                                       "w_ffn_conv", "b_ffn_conv")}
    ga = {k: [None] * N_A for k in ("w_conv_a", "b_conv_a", "w_rg_r", "b_rg_r", "w_rg_i", "b_rg_i", "lru_lambda")}
    gsink = [None] * (DEPTH - N_A)
    dkv_cur, dkv_prev = [], []
    g_tot, df, _, gs["g_ffn_post"][DEPTH - 1] = _resid_norm_bwd(
        g_tot, [], None, [], sv[-1]["f"], _row(p["g_ffn_post"][DEPTH - 1]), name="loss_bwd")
    grad_x = None
    for l in reversed(range(DEPTH)):
        s = sv[l]
        dact = _mm_nt(df, wfull("w_ffn_down", l), name=f"d_act{l}", q=q, out_dtype=MXU)
        push_grad("w_ffn_down", l, _mm_tn(s["act"], df, name=f"dw_down{l}", q=q))
        dug, duv, gs["w_ffn_conv"][l], gs["b_ffn_conv"][l] = _ffn_act_bwd(
            s["up"], s["ug"], s["uv"], dact, p["w_ffn_conv"][l], bsz, name=f"ffn_act_bwd{l}", q=q)
        dh2 = _mm_ffn_dh(dug, duv, wfull("w_ffn_up", l), name=f"d_h2_{l}", q=q)
        up_slots = dict(slot_cols=2 * D_FF // N_CHIP, n_slots=N_CHIP)
        dwu = _mm_tn_slots(s["h2"], dug, name=f"dw_up_g{l}", q=q, **up_slots)
        push_grad("w_ffn_up", l, _mm_tn_slots(s["h2"], duv, name=f"dw_up_v{l}", q=q, out=dwu,
                                              first_slot=N_CHIP // 2, **up_slots))
        g1, dy, (gs["g_ffn_pre"][l],), gs["g_mix_post"][l] = _resid_norm_bwd(
            g_tot, [dh2], s["x1"], [_row(p["g_ffn_pre"][l])], s["y"], _row(p["g_mix_post"][l]), name=f"mix_resid_bwd{l}", q=q)
        dycat = _mm_nt(dy, wfull("w_mix_out", l), name=f"d_ycat{l}", q=q)
        push_grad("w_mix_out", l, _mm_tn(s["ycat"], dy, name=f"dw_mix_out{l}", q=q))
        if l < N_A:
            dproj, dwc, dbc, dwr, dbr, dwi, dbi, dlam = _lru_bwd(
                s["proj"], s["hs"], dycat, p["w_conv_a"][l], _row(p["b_conv_a"][l]), wr_t[l], _row(p["b_rg_r"][l]),
                wi_t[l], _row(p["b_rg_i"][l]), _row(p["lru_lambda"][l]), bsz, name=f"lru_bwd{l}", q=q)
            ga["w_conv_a"][l], ga["b_conv_a"][l], ga["lru_lambda"][l] = dwc, dbc[0], dlam[0]
            ga["w_rg_r"][l], ga["w_rg_i"][l] = _gate_blocks(dwr), _gate_blocks(dwi)
            ga["b_rg_r"][l] = dbr.reshape(LRU_BLOCKS, HEAD_DIM)
            ga["b_rg_i"][l] = dbi.reshape(LRU_BLOCKS, HEAD_DIM)
            w_in, j = "w_in_a", l
        else:
            dproj, dc, dp_, dsk = _swa_bwd(s["proj"], kv, p["sinks_b"][l - N_A], dycat, bsz, name=f"swa_bwd{l}", q=q)
            dkv_cur.append(dc)
            dkv_prev.append(dp_)
            gsink[l - N_A] = dsk[0, :SWA_HEADS]
            w_in, j = "w_in_b", l - N_A
        dproj, dmkv = _mem_attn_bwd(s["proj"], s["qblk"], mkv[l], dycat, dproj, bsz, name=f"mem_attn_bwd{l}", q=q)
        dh = _mm_nt(dproj, wfull(w_in, j), name=f"d_h{l}", q=q)
        push_grad(w_in, j, _mm_tn(s["h"], dproj, name=f"dw_in{l}", q=q))
        dmkv = dmkv.astype(MXU)
        dmn = _mm_nt(dmkv, wfull("w_mem_kv", l), name=f"d_mem_norm{l}", q=q)
        push_grad("w_mem_kv", l, _mm_tn(mn[l], dmkv, name=f"dw_mem_kv{l}", q=q))
        gs["g_mem"][l] = _norm_bwd_dg(dmn, mem2d, _row(p["g_mem"][l]), name=f"mem_norm_bwd{l}")
        dhs, g_pres = [dh], [_row(p["g_mix_pre"][l])]
        if l == N_A:
            dkv = _swa_dkv_combine(dkv_cur, dkv_prev, bsz, name="dkv_combine")
            dhs.append(_mm_nt(dkv, wfull("w_kv", 0), name="d_hkv", q=q))
            g_pres.append(_row(p["g_kv"]))
            push_grad("w_kv", 0, _mm_tn(hkv, dkv, name="dw_kv", q=q))
        if l > 0:
            g_tot, df, dgpre, gs["g_ffn_post"][l - 1] = _resid_norm_bwd(
                g1, dhs, s["xin"], g_pres, sv[l - 1]["f"], _row(p["g_ffn_post"][l - 1]), name=f"ffn_resid_bwd{l - 1}", q=q)
        else:
            grad_x, _, dgpre, _ = _resid_norm_bwd(g1, dhs, s["xin"], g_pres, None, None, name="in_norm_bwd", q=q)
        gs["g_mix_pre"][l] = dgpre[0]
        if l == N_A:
            g_kv = dgpre[1][0]

    grads = {}
    for k in ("g_mix_pre", "g_mix_post", "g_ffn_pre", "g_ffn_post", "g_mem", "b_ffn_conv"):
        grads[k] = jnp.concatenate(gs[k], axis=0)
    grads["w_ffn_conv"] = jnp.stack(gs["w_ffn_conv"])
    for k, v in ga.items():
        grads[k] = jnp.stack(v)
    grads["sinks_b"] = jnp.stack(gsink)
    grads["g_kv"] = g_kv
    return jnp.sum(sq), grad_x.reshape(bsz, t, d), grads


N_CHIP = 4
HALF_ALIGN = 16
MIN_PART_BYTES = 128 * 1024


def _full_shape(kind, shard_shape):
    l, r, c = shard_shape
    return {"row": (l, N_CHIP * r, c), "col": (l, r, N_CHIP * c), "slot": (N_CHIP, l, r, c)}[kind]


def _slot_view(ref, kind, shard_shape, s, hf, sub=(0, 1)):
    _, r, c = shard_shape
    rh = r // 2
    if hf is None:
        start, size = 0, r
    else:
        size = rh // sub[1]
        start = hf * rh + sub[0] * size
    if kind == "row":
        start = s * r + start
    if not isinstance(start, int):
        start = pl.multiple_of(start, HALF_ALIGN)
    rows = pl.ds(start, size)
    if kind == "row":
        return ref.at[:, rows, :]
    if kind == "col":
        return ref.at[:, rows, pl.ds(s * c, c)]
    return ref.at[s, :, rows, :]


def _half_view(ref, shard_shape, hf, sub=(0, 1)):
    rh = shard_shape[1] // 2
    size = rh // sub[1]
    return ref.at[:, pl.ds(pl.multiple_of(hf * rh + sub[0] * size, HALF_ALIGN), size), :]


def _with_slot(kind, s, fn):
    if kind != "col" or isinstance(s, int):
        fn(s)
        return
    for k in range(N_CHIP):
        @pl.when(s == k)
        def _(k=k):
            fn(k)


def _mesh_pos():
    return lax.axis_index("x"), lax.axis_index("y"), lax.axis_index("c")


def _other_chips(x, y):
    return [(1 - x, y), (x, 1 - y), (1 - x, 1 - y)]


ICI_BYTES_PER_US = 6.0e4
ICI_GATHER_BYTES_PER_US = 5.5e4
D2D_BYTES_PER_US = 4.0e5


class _Chunk:
    def __init__(self, group, cost, ins, out_shapes, alias, n_sem, start, finish, done, buffer=None, bind=None):
        self.group, self.cost, self.ins, self.out_shapes, self.alias, self.n_sem = group, cost, ins, out_shapes, alias, n_sem
        self.start, self.finish, self.done = start, finish, done
        self.buffer = buffer
        self.bind = bind

    def prepare(self):
        if self.bind is not None:
            self.bind(self)


def _merged(chunks):
    groups, by_buffer = [], {}
    for ch in chunks:
        key = None if ch.buffer is None else (id(ch.buffer[0]), ch.buffer[1])
        if key is not None and key in by_buffer:
            by_buffer[key].append(ch)
        else:
            groups.append([ch])
            if key is not None:
                by_buffer[key] = groups[-1]
    out = []
    for parts in groups:
        if len(parts) == 1:
            out.append(parts[0])
            continue
        offs = [sum(p.n_sem for p in parts[:i]) for i in range(len(parts))]

        def run(phase, ins, outs, ss, rs, b, parts=parts, offs=offs):
            for p, o in zip(parts, offs):
                getattr(p, phase)(ins, outs, ss, rs, b + o)

        def done(outs, parts=parts):
            for p in parts:
                p.done(outs)

        first = parts[0]
        out.append(_Chunk(first.group, sum(p.cost for p in parts), first.ins, first.out_shapes, first.alias,
                          sum(p.n_sem for p in parts), functools.partial(run, "start"),
                          functools.partial(run, "finish"), done))
    return out


LAST_GROUP = 1 << 30


class _CommQueue:
    def __init__(self):
        self.pending = []
        self.flushes = 0
        self.horizon = LAST_GROUP

    def push(self, chunk):
        self.pending.append(chunk)

    def take(self, budget_us):
        got, used = [], 0.0
        for ch in sorted(self.pending, key=lambda ch: (ch.group, -ch.cost)):
            if ch.group >= self.horizon and ch.group != LAST_GROUP:
                continue
            if used + ch.cost <= budget_us and not self._shares_buffer(ch, got):
                got.append(ch)
                used += ch.cost
        return self._taken(got)

    @staticmethod
    def _shares_buffer(ch, others):
        return ch.buffer is not None and any(
            o.buffer is not None and o.buffer[0] is ch.buffer[0] and o.buffer[1] != ch.buffer[1] for o in others)

    def _taken(self, got):
        self.pending = [ch for ch in self.pending if ch not in got]
        for ch in got:
            ch.prepare()
        return _merged(got)

    def flush(self, group=LAST_GROUP):
        while True:
            chunks = []
            for ch in self.pending:
                if ch.group <= group and not self._shares_buffer(ch, chunks):
                    chunks.append(ch)
            if not chunks:
                return
            _run_chunks(self._taken(chunks), name=f"comm_flush{self.flushes}")
            self.flushes += 1


def _run_chunks(chunks, *, name):
    ins = [a for ch in chunks for a in ch.ins]
    outs = [s for ch in chunks for s in ch.out_shapes]
    alias, offs = {}, []
    i0 = o0 = s0 = 0
    for ch in chunks:
        offs.append((i0, o0, s0))
        for ci, co in ch.alias.items():
            alias[i0 + ci] = o0 + co
        i0 += len(ch.ins)
        o0 += len(ch.out_shapes)
        s0 += ch.n_sem

    def body(*refs):
        send_sems, recv_sems = refs[i0 + o0:]
        for phase in ("start", "finish"):
            for ch, (a, b, s) in zip(chunks, offs):
                getattr(ch, phase)(refs[a:a + len(ch.ins)], refs[i0 + b:i0 + b + len(ch.out_shapes)],
                                   send_sems, recv_sems, s)

    hbm = pl.BlockSpec(memory_space=pl.ANY)
    res = pl.pallas_call(
        body, in_specs=[hbm] * i0, out_specs=[hbm] * o0, out_shape=outs,
        scratch_shapes=[pltpu.SemaphoreType.DMA((s0,)), pltpu.SemaphoreType.DMA((s0,))],
        input_output_aliases=alias, name=name, compiler_params=pltpu.CompilerParams(has_side_effects=True))(*ins)
    for ch, (_, b, _) in zip(chunks, offs):
        ch.done(list(res[b:b + len(ch.out_shapes)]))


def _remote(src, dst, send_sems, recv_sems, k, dev):
    return pltpu.make_async_remote_copy(src_ref=src, dst_ref=dst, send_sem=send_sems.at[k], recv_sem=recv_sems.at[k],
                                        device_id=dev, device_id_type=MESH_T)


def _gather_chunks(q, group, kind, shard, l, ready):
    _, r, c = shard.shape
    shp = (1, r, c)
    rh = r // 2
    parts = max(p for p in (8, 4, 2, 1)
                if (rh // p) % HALF_ALIGN == 0 and (p == 1 or (rh // p) * c * shard.dtype.itemsize >= MIN_PART_BYTES))
    part_bytes = (rh // parts) * c * shard.dtype.itemsize
    full_type = jax.ShapeDtypeStruct(_full_shape(kind, shp), shard.dtype)
    state = {"full": None, "parts_done": 0}

    def bind_first(ch):
        ch.ins, ch.alias = ([shard], {}) if state["full"] is None else ([shard, state["full"]], {1: 0})

    def bind_full(ch):
        ch.ins = [state["full"]]

    def make_part(p):
        sub = (p, parts)

        def any_part(full):
            return _slot_view(full, kind, shp, 0, 0, sub)

        def start1(ins, outs, ss, rs, b):
            x, y, c_ = _mesh_pos()
            src, full = ins[0].at[pl.ds(l, 1)], outs[0]
            if p == 0:
                _with_slot(kind, 2 * x + y, lambda s: pltpu.make_async_copy(
                    src, _slot_view(full, kind, shp, s, None), ss.at[b + N_CHIP - 1]).start())
            for j, (ox, oy) in enumerate(_other_chips(x, y)):
                _with_slot(kind, 2 * x + y, lambda s, j=j, ox=ox, oy=oy: _remote(
                    _half_view(src, shp, c_, sub), _slot_view(full, kind, shp, s, c_, sub), ss, rs, b + j,
                    (ox, oy, c_)).start())

        def finish1(ins, outs, ss, rs, b):
            x, y, c_ = _mesh_pos()
            h = any_part(outs[0])
            for j in range(N_CHIP - 1):
                _remote(h, h, ss, rs, b + j, (x, y, 1 - c_)).wait()
            if p == 0:
                pltpu.make_async_copy(ins[0].at[pl.ds(l, 1)], _slot_view(outs[0], kind, shp, 0, None),
                                      ss.at[b + N_CHIP - 1]).wait()

        def start2(ins, outs, ss, rs, b):
            x, y, c_ = _mesh_pos()
            for j, (ox, oy) in enumerate(_other_chips(x, y)):
                def forward(s, j=j):
                    v = _slot_view(outs[0], kind, shp, s, c_, sub)
                    _remote(v, v, ss, rs, b + j, (x, y, 1 - c_)).start()
                _with_slot(kind, 2 * ox + oy, forward)

        def finish2(ins, outs, ss, rs, b):
            x, y, c_ = _mesh_pos()
            h = any_part(outs[0])
            for j in range(N_CHIP - 1):
                _remote(h, h, ss, rs, b + j, (x, y, 1 - c_)).wait()

        def done2(outs):
            state["full"] = outs[0]
            state["parts_done"] += 1
            if state["parts_done"] == parts:
                ready(outs[0])

        def done1(outs):
            state["full"] = outs[0]
            q.push(_Chunk(group, 3 * part_bytes / D2D_BYTES_PER_US, None, [full_type], {0: 0}, N_CHIP - 1,
                          start2, finish2, done2, buffer=(state, 2), bind=bind_full))

        return _Chunk(group, 3 * part_bytes / ICI_GATHER_BYTES_PER_US, None, [full_type], None,
                      N_CHIP if p == 0 else N_CHIP - 1, start1, finish1, done1, buffer=(state, 1), bind=bind_first)

    for p in range(parts):
        q.push(make_part(p))


def _reduce_scatter_chunks(q, kind, grad, shard_shape, pos, name, ready):
    _, r, c = shard_shape
    shp = (1, r, c)
    rh = r // 2

    def start1(ins, outs, ss, rs, b):
        x, y, c_ = _mesh_pos()
        for s in range(N_CHIP):
            _remote(_slot_view(ins[0], kind, shp, s, 1 - c_), outs[0].at[s], ss, rs, b + s, (x, y, 1 - c_)).start()

    def finish1(ins, outs, ss, rs, b):
        x, y, c_ = _mesh_pos()
        for s in range(N_CHIP):
            _remote(outs[0].at[s], outs[0].at[s], ss, rs, b + s, (x, y, 1 - c_)).wait()

    def start2(ins, outs, ss, rs, b):
        x, y, c_ = _mesh_pos()
        for j, (ox, oy) in enumerate(_other_chips(x, y)):
            _remote(ins[0].at[2 * ox + oy], outs[0].at[j], ss, rs, b + j, (ox, oy, c_)).start()

    def finish2(ins, outs, ss, rs, b):
        x, y, c_ = _mesh_pos()
        for j in range(N_CHIP - 1):
            _remote(outs[0].at[j], outs[0].at[j], ss, rs, b + j, (x, y, 1 - c_)).wait()

    def start3(ins, outs, ss, rs, b):
        x, y, c_ = _mesh_pos()
        v = _half_view(outs[0], shp, c_)
        _remote(v, v, ss, rs, b, (x, y, 1 - c_)).start()

    def finish3(ins, outs, ss, rs, b):
        x, y, c_ = _mesh_pos()
        v = _half_view(outs[0], shp, c_)
        _remote(v, v, ss, rs, b, (x, y, 1 - c_)).wait()

    def done2(pair, outs):
        half = _rs_chip_add(pair, outs[0], shp, pos, name=f"rs_chip_add_{name}")
        q.push(_Chunk(LAST_GROUP, rh * c * 4 / D2D_BYTES_PER_US, [half], [jax.ShapeDtypeStruct(half.shape, half.dtype)],
                      {0: 0}, 1, start3, finish3, lambda o: ready(o[0])))

    def done1(outs):
        pair, wire = _rs_pair_add(grad, outs[0], kind, shp, pos, name=f"rs_pair_add_{name}")
        q.push(_Chunk(LAST_GROUP, 3 * rh * c * wire.dtype.itemsize / ICI_BYTES_PER_US, [wire],
                      [jax.ShapeDtypeStruct((N_CHIP - 1, 1, rh, c), wire.dtype)], {}, N_CHIP - 1,
                      start2, finish2, functools.partial(done2, pair)))

    q.push(_Chunk(LAST_GROUP, N_CHIP * rh * c * 4 / D2D_BYTES_PER_US, [grad],
                  [jax.ShapeDtypeStruct((N_CHIP, 1, rh, c), F32)], {}, N_CHIP, start1, finish1, done1))


def _allgather8(vec, *, name):
    r = vec.shape[0]
    n_dev = 8

    def body(v_ref, buf, send_sems, recv_sems):
        x, y, c = _mesh_pos()
        me = 4 * x + 2 * y + c
        copies = []
        for k in range(1, n_dev):
            kx, ky, kc = (k >> 2) & 1, (k >> 1) & 1, k & 1
            peer = ((1 - x) if kx else x, (1 - y) if ky else y, (1 - c) if kc else c)
            cp = _remote(v_ref, buf.at[me], send_sems, recv_sems, k - 1, peer)
            cp.start()
            copies.append(cp)
        buf[me] = v_ref[...]
        for cp in copies:
            cp.wait()

    vm = pl.BlockSpec(memory_space=pltpu.VMEM)
    return pl.pallas_call(
        body, in_specs=[vm], out_specs=vm, out_shape=jax.ShapeDtypeStruct((n_dev, r, LANE), F32),
        scratch_shapes=[pltpu.SemaphoreType.DMA((n_dev - 1,)), pltpu.SemaphoreType.DMA((n_dev - 1,))],
        name=name, compiler_params=pltpu.CompilerParams(has_side_effects=True, vmem_limit_bytes=VMEM_LIMIT_V7X))(vec)


def _allreduce8(vec, *, name):
    r = vec.shape[0]
    rh = r // 2

    def body(v_ref, o_ref, sib_ref, chips_ref, send_sems, recv_sems):
        x, y, c = _mesh_pos()
        sib = (x, y, 1 - c)
        me = 2 * x + y
        pair = _remote(v_ref, sib_ref, send_sems, recv_sems, 0, sib)
        pair.start()
        pair.wait()
        rows = pl.ds(pl.multiple_of(c * rh, SUBLANE), rh)
        chips_ref[me] = v_ref[rows, :] + sib_ref[rows, :]
        copies = []
        for j, (ox, oy) in enumerate(_other_chips(x, y)):
            cp = _remote(chips_ref.at[me], chips_ref.at[me], send_sems, recv_sems, 1 + j, (ox, oy, c))
            cp.start()
            copies.append(cp)
        for cp in copies:
            cp.wait()
        acc = chips_ref[0]
        for s in range(1, N_CHIP):
            acc = acc + chips_ref[s]
        o_ref[rows, :] = acc
        swap = _remote(o_ref.at[rows, :], o_ref.at[rows, :], send_sems, recv_sems, N_CHIP, sib)
        swap.start()
        swap.wait()

    vm = pl.BlockSpec(memory_space=pltpu.VMEM)
    return pl.pallas_call(
        body, in_specs=[vm], out_specs=vm, out_shape=jax.ShapeDtypeStruct((r, LANE), F32),
        scratch_shapes=[pltpu.VMEM((r, LANE), F32), pltpu.VMEM((N_CHIP, rh, LANE), F32),
                        pltpu.SemaphoreType.DMA((N_CHIP + 1,)), pltpu.SemaphoreType.DMA((N_CHIP + 1,))],
        name=name, compiler_params=pltpu.CompilerParams(has_side_effects=True, vmem_limit_bytes=VMEM_LIMIT_V7X))(vec)


def _rs_pair_add(g, recv, kind, shape, pos, *, name):
    l, r, c = shape
    rh = r // 2
    if kind == "row":
        gspec = pl.BlockSpec((None, rh, c), lambda s, i, pos: (i, 2 * s + pos[0], 0))
    elif kind == "col":
        gspec = pl.BlockSpec((None, rh, c), lambda s, i, pos: (i, pos[0], s))
    else:
        gspec = pl.BlockSpec((None, None, rh, c), lambda s, i, pos: (s, i, pos[0], 0))
    pspec = pl.BlockSpec((None, None, rh, c), lambda s, i, pos: (s, i, 0, 0))

    def body(pos_ref, g_ref, r_ref, p_ref, pw_ref):
        del pos_ref
        v = g_ref[...] + r_ref[...]
        p_ref[...] = v
        pw_ref[...] = v.astype(pw_ref.dtype)

    return pl.pallas_call(
        body,
        grid_spec=pltpu.PrefetchScalarGridSpec(
            num_scalar_prefetch=1, grid=(N_CHIP, l), in_specs=[gspec, pspec], out_specs=[pspec, pspec]),
        out_shape=[jax.ShapeDtypeStruct((N_CHIP, l, rh, c), F32), jax.ShapeDtypeStruct((N_CHIP, l, rh, c), MXU)],
        name=name, compiler_params=_cp((PAR, PAR)))(pos, g, recv)


def _rs_chip_add(p, recv, shape, pos, *, name):
    l, r, c = shape
    rh = r // 2

    def body(pos_ref, p_ref, r_ref, o_ref):
        del pos_ref
        acc = p_ref[...]
        for j in range(N_CHIP - 1):
            acc = acc + r_ref[j].astype(F32)
        o_ref[...] = acc

    return pl.pallas_call(
        body,
        grid_spec=pltpu.PrefetchScalarGridSpec(
            num_scalar_prefetch=1, grid=(l,),
            in_specs=[pl.BlockSpec((None, None, rh, c), lambda i, pos: (pos[1], i, 0, 0)),
                      pl.BlockSpec((N_CHIP - 1, None, rh, c), lambda i, pos: (0, i, 0, 0))],
            out_specs=pl.BlockSpec((None, rh, c), lambda i, pos: (i, pos[0], 0))),
        out_shape=jax.ShapeDtypeStruct((l, r, c), F32),
        name=name, compiler_params=_cp((PAR,)))(pos, p, recv)


ADAM_BLOCK_ELEMS = 384 * 1024


def _adam_math(w, g, m, v):
    c1 = 1.0 / (1.0 - ADAM_B1 ** ADAM_STEP)
    c2 = 1.0 / (1.0 - ADAM_B2 ** ADAM_STEP)
    nm = ADAM_B1 * m + (1.0 - ADAM_B1) * g
    nv = ADAM_B2 * v + (1.0 - ADAM_B2) * (g * g)
    return -ADAM_LR * ((nm * c1) / (jnp.sqrt(nv * c2) + ADAM_EPS) + ADAM_WD * w), nm, nv


def _adamw_layer(w, g, m, v, outs, l, *, name):
    _, r, c = w.shape
    tr = _tile(r, max(SUBLANE, ADAM_BLOCK_ELEMS // c // SUBLANE * SUBLANE), SUBLANE)

    def body(w_ref, g_ref, m_ref, v_ref, *rest):
        go_ref, d_ref, nm_ref, nv_ref = rest[4:]
        gg = g_ref[...]
        go_ref[...] = gg
        d_ref[...], nm_ref[...], nv_ref[...] = _adam_math(w_ref[...], gg, m_ref[...], v_ref[...])

    lay = pl.BlockSpec((None, tr, c), lambda j: (l, j, 0))
    hbm = pl.BlockSpec(memory_space=pl.ANY)
    return pl.pallas_call(
        body, grid=(r // tr,),
        in_specs=[lay, pl.BlockSpec((None, tr, c), lambda j: (0, j, 0)), lay, lay] + [hbm] * 4,
        out_specs=[lay] * 4, out_shape=[jax.ShapeDtypeStruct(w.shape, F32)] * 4,
        input_output_aliases={4 + i: i for i in range(4)},
        name=name, compiler_params=_cp((PAR,)))(w, g, m, v, *outs)


def _adamw(w, g, m, v, *, name):
    shape = w.shape
    if w.ndim == 2:
        w, g, m, v = (a[None] for a in (w, g, m, v))
    l, r, c = w.shape
    tr = _tile(r, max(SUBLANE, ADAM_BLOCK_ELEMS // c // SUBLANE * SUBLANE), SUBLANE)

    def body(w_ref, g_ref, m_ref, v_ref, d_ref, nm_ref, nv_ref):
        d_ref[...], nm_ref[...], nv_ref[...] = _adam_math(w_ref[...], g_ref[...], m_ref[...], v_ref[...])

    spec = pl.BlockSpec((None, tr, c), lambda i, j: (i, j, 0))
    outs = pl.pallas_call(
        body, grid=(l, r // tr), in_specs=[spec] * 4, out_specs=[spec] * 3,
        out_shape=[jax.ShapeDtypeStruct((l, r, c), F32)] * 3,
        name=name, compiler_params=_cp((PAR, PAR)))(w, g, m, v)
    return tuple(o.reshape(shape) for o in outs)


PACK_ROWS = 512 * LANE


def _pack(arrays):
    flat = jnp.concatenate([a.reshape(-1).astype(F32) for a in arrays])
    pad = (-flat.shape[0]) % PACK_ROWS
    return jnp.pad(flat, (0, pad)).reshape(-1, LANE)


def _unpack(packed, shapes):
    flat = packed.reshape(-1)
    out, off = [], 0
    for s in shapes:
        size = int(np.prod(s))
        out.append(flat[off:off + size].reshape(s))
        off += size
    return out


BIG = (("w_mem_kv", "row"), ("w_mix_out", "row"), ("w_ffn_up", "slot"), ("w_ffn_down", "row"),
       ("w_in_a", "slot"), ("w_in_b", "row"), ("w_kv", "row"))
COLUMN_SHARDED_AS_COLUMNS = ("w_in_a",)
SMALL_SHARDED = (("w_ffn_conv", 2), ("w_conv_a", 2), ("b_conv_a", 1), ("lru_lambda", 1))
SMALL_REPLICATED = ("g_mix_pre", "g_mix_post", "g_ffn_pre", "g_ffn_post", "g_mem", "b_ffn_conv",
                    "w_rg_r", "b_rg_r", "w_rg_i", "b_rg_i", "sinks_b", "g_kv")
WEIGHTS = ("g_mix_pre", "g_mix_post", "g_ffn_pre", "g_ffn_post", "g_mem", "w_mem_kv", "w_mix_out", "w_ffn_up",
           "w_ffn_conv", "b_ffn_conv", "w_ffn_down", "w_in_a", "w_conv_a", "b_conv_a", "w_rg_r", "b_rg_r", "w_rg_i",
           "b_rg_i", "lru_lambda", "w_in_b", "sinks_b", "g_kv", "w_kv")


def _slot_to_cols(a):
    s, l, r, c = a.shape
    return a.transpose(1, 2, 0, 3).reshape(l, r, s * c)


def _cols_to_slot(a):
    l, r, c4 = a.shape
    return a.reshape(l, r, N_CHIP, c4 // N_CHIP).transpose(2, 0, 1, 3)


GROUPS_PER_LAYER = 8


def _layer_weights(layer):
    names = [("w_mem_kv", layer), ("w_in_a", layer) if layer < N_A else ("w_in_b", layer - N_A)]
    if layer == N_A:
        names.append(("w_kv", 0))
    return names + [("w_mix_out", layer), ("w_ffn_up", layer), ("w_ffn_down", layer)]


def _train_step(x, mem, target, w, m, v):
    xi, yi, ci = _mesh_pos()
    chip = 2 * xi + yi
    pos = jnp.stack([ci, chip]).astype(jnp.int32)

    q = _CommQueue()
    kinds = dict(BIG)
    as3 = lambda a: a if a.ndim == 3 else a[None]
    w3, m3, v3 = ({k: as3(d[k]) for k, _ in BIG} for d in (w, m, v))
    shards = {k: w3[k].astype(MXU) for k, _ in BIG}

    gathered = {}

    def on_gathered(k, l, full):
        gathered[k, l] = _slot_to_cols(full) if k in COLUMN_SHARDED_AS_COLUMNS else full

    group_of = {}

    for layer in range(DEPTH):
        for i, (k, l) in enumerate(_layer_weights(layer)):
            group_of[k, l] = layer * GROUPS_PER_LAYER + i
            _gather_chunks(q, group_of[k, l], kinds[k], shards[k], l, functools.partial(on_gathered, k, l))

    def wfull(k, l):
        if (k, l) not in gathered:
            q.flush(group_of[k, l])
        return gathered[k, l]

    q.flush(1)

    big_out = {k: [lax.empty(w3[k].shape, F32) for _ in range(4)] for k, _ in BIG}

    def on_reduced(k, l, g):
        big_out[k] = _adamw_layer(w3[k], g, m3[k], v3[k], big_out[k], l, name=f"adamw_{k}{l}")

    def push_grad(k, l, g):
        if k in COLUMN_SHARDED_AS_COLUMNS:
            g = _cols_to_slot(g)
        _reduce_scatter_chunks(q, kinds[k], g, (1,) + w3[k].shape[1:], pos, f"{k}{l}", functools.partial(on_reduced, k, l))

    small_shapes = [w[k].shape for k, _ in SMALL_SHARDED]
    stacked = _allgather8(_pack([w[k] for k, _ in SMALL_SHARDED]), name="gather_small")
    per_chip = [_unpack(stacked[2 * s], small_shapes) for s in range(N_CHIP)]
    p = {k: w[k] for k in SMALL_REPLICATED}
    for i, (k, axis) in enumerate(SMALL_SHARDED):
        p[k] = jnp.concatenate([per_chip[s][i] for s in range(N_CHIP)], axis=axis)

    sq, grad_x, g = _local_step(x, mem, target, p, wfull, push_grad, q)
    loss = lax.psum(0.5 * sq / D_MODEL, ("x", "y", "c"))
    q.flush()

    small_names = [k for k, _ in SMALL_SHARDED] + list(SMALL_REPLICATED)
    summed = _allreduce8(_pack([g[k] for k in small_names]), name="allreduce_small")
    gsum = dict(zip(small_names, _unpack(summed, [p[k].shape for k in small_names])))
    for k, axis in SMALL_SHARDED:
        gsum[k] = lax.dynamic_slice_in_dim(gsum[k], chip * w[k].shape[axis], w[k].shape[axis], axis)

    delta, new_m, new_v = {}, {}, {}
    for k, _ in BIG:
        gsum[k], delta[k], new_m[k], new_v[k] = (o.reshape(w[k].shape) for o in big_out[k])
    packed = [_pack([d[k] for k in small_names]) for d in (w, gsum, m, v)]
    outs = _adamw(*packed, name="adamw_small")
    for d, o in zip((delta, new_m, new_v), outs):
        d.update(zip(small_names, _unpack(o, [w[k].shape for k in small_names])))
    return (loss, grad_x, *[gsum[k] for k in WEIGHTS], *[delta[k] for k in WEIGHTS],
            *[new_m[k] for k in WEIGHTS], *[new_v[k] for k in WEIGHTS])


def kernel(x, mem, g_mix_pre, g_mix_post, g_ffn_pre, g_ffn_post, g_mem, w_mem_kv, w_mix_out, w_ffn_up, w_ffn_conv, b_ffn_conv, w_ffn_down, w_in_a, w_conv_a, b_conv_a, w_rg_r, b_rg_r, w_rg_i, b_rg_i, lru_lambda, w_in_b, sinks_b, g_kv, w_kv, loss_target, m_g_mix_pre, m_g_mix_post, m_g_ffn_pre, m_g_ffn_post, m_g_mem, m_w_mem_kv, m_w_mix_out, m_w_ffn_up, m_w_ffn_conv, m_b_ffn_conv, m_w_ffn_down, m_w_in_a, m_w_conv_a, m_b_conv_a, m_w_rg_r, m_b_rg_r, m_w_rg_i, m_b_rg_i, m_lru_lambda, m_w_in_b, m_sinks_b, m_g_kv, m_w_kv, v_g_mix_pre, v_g_mix_post, v_g_ffn_pre, v_g_ffn_post, v_g_mem, v_w_mem_kv, v_w_mix_out, v_w_ffn_up, v_w_ffn_conv, v_b_ffn_conv, v_w_ffn_down, v_w_in_a, v_w_conv_a, v_b_conv_a, v_w_rg_r, v_b_rg_r, v_w_rg_i, v_b_rg_i, v_lru_lambda, v_w_in_b, v_sinks_b, v_g_kv, v_w_kv):
    args = (g_mix_pre, g_mix_post, g_ffn_pre, g_ffn_post, g_mem, w_mem_kv, w_mix_out, w_ffn_up, w_ffn_conv, b_ffn_conv, w_ffn_down, w_in_a, w_conv_a, b_conv_a, w_rg_r, b_rg_r, w_rg_i, b_rg_i, lru_lambda, w_in_b, sinks_b, g_kv, w_kv)
    ms = (m_g_mix_pre, m_g_mix_post, m_g_ffn_pre, m_g_ffn_post, m_g_mem, m_w_mem_kv, m_w_mix_out, m_w_ffn_up, m_w_ffn_conv, m_b_ffn_conv, m_w_ffn_down, m_w_in_a, m_w_conv_a, m_b_conv_a, m_w_rg_r, m_b_rg_r, m_w_rg_i, m_b_rg_i, m_lru_lambda, m_w_in_b, m_sinks_b, m_g_kv, m_w_kv)
    vs = (v_g_mix_pre, v_g_mix_post, v_g_ffn_pre, v_g_ffn_post, v_g_mem, v_w_mem_kv, v_w_mix_out, v_w_ffn_up, v_w_ffn_conv, v_b_ffn_conv, v_w_ffn_down, v_w_in_a, v_w_conv_a, v_b_conv_a, v_w_rg_r, v_b_rg_r, v_w_rg_i, v_b_rg_i, v_lru_lambda, v_w_in_b, v_sinks_b, v_g_kv, v_w_kv)
    return _train_step(x, mem, loss_target, dict(zip(WEIGHTS, args)), dict(zip(WEIGHTS, ms)), dict(zip(WEIGHTS, vs)))
```

```python
import functools
import math

import numpy as np
import jax
import jax.numpy as jnp
from jax import lax
from jax.experimental import pallas as pl
from jax.experimental.pallas import tpu as pltpu

F32 = jnp.float32
MXU = jnp.bfloat16

D_MODEL = 1024
HEAD_DIM = 64
MEM_LEN = 256
MEM_HEADS = 4
MEM_WIDTH = MEM_HEADS * HEAD_DIM
MIX_WIDTH = D_MODEL - MEM_WIDTH
LRU_BLOCKS = MIX_WIDTH // HEAD_DIM
LRU_CONV = 4
LRU_C = 8.0
SWA_HEADS = MIX_WIDTH // HEAD_DIM
SWA_KV_HEADS = 4
SWA_GROUP = SWA_HEADS // SWA_KV_HEADS
WINDOW = 128
D_FF = 2816
FFN_CONV = 3
EPS = 1e-6
DEPTH = 4
N_A = 2

ADAM_LR = 0.001
ADAM_B1 = 0.9
ADAM_B2 = 0.999
ADAM_EPS = 1e-08
ADAM_WD = 0.01
ADAM_STEP = 10

VMEM_LIMIT_V7X = 56 * 1024 * 1024
LANE = 128
SUBLANE = 8
GATE_TILE = 256
MESH_T = pl.DeviceIdType.MESH


def _alibi_slopes(n):
    def pow2_slopes(m):
        start = 2.0 ** (-8.0 / m)
        return [start ** (i + 1) for i in range(m)]
    c = 2 ** int(math.floor(math.log2(n)))
    s = pow2_slopes(c)
    if c != n:
        s = s + pow2_slopes(2 * c)[0::2][: n - c]
    return [float(np.float32(v)) for v in s]


SLOPES = _alibi_slopes(SWA_HEADS)


def _tile(n, cap, mult=LANE):
    best = None
    for t in range(mult, min(n, cap) + 1, mult):
        if n % t == 0:
            best = t
    return best if best is not None else n


def _cp(sem):
    return pltpu.CompilerParams(dimension_semantics=sem, vmem_limit_bytes=VMEM_LIMIT_V7X)


MM_VMEM_BUDGET = 40 * 1024 * 1024
HBM_BYTES_PER_US_V7X = 3.0e6
GRID_STEP_US = 0.35


def _divisors(n, mult):
    return [t for t in range(mult, n + 1, mult) if n % t == 0] or [n]


def _mm_tiles(m, k, n, out_bytes):
    best = None
    for tm in _divisors(m, 256):
        for tn in _divisors(n, LANE):
            vmem = 2 * (tm * k * 2 + k * tn * 2 + tm * tn * out_bytes)
            if vmem > MM_VMEM_BUDGET:
                continue
            steps = (m // tm) * (n // tn)
            b_reads = 1 if tn == n else m // tm
            traffic = m * k * 2 + k * n * 2 * b_reads + m * n * out_bytes
            first = tm * k * 2 + k * tn * 2
            cost = (traffic + first) / HBM_BYTES_PER_US_V7X + steps * GRID_STEP_US
            if best is None or cost < best[0]:
                best = (cost, tm, tn)
    return best[1], best[2]


def _mm_tn_tiles(k, m, n, whole_n=False):
    best = None
    for tm in _divisors(m, LANE):
        for tn in ([n] if whole_n else _divisors(n, LANE)):
            for tk in _divisors(k, 512):
                vmem = 2 * (tk * tm * 2 + tk * tn * 2 + tm * tn * 4)
                if vmem > MM_VMEM_BUDGET:
                    continue
                steps = (m // tm) * (n // tn) * (k // tk)
                traffic = k * m * 2 * (n // tn) + k * n * 2 * (m // tm) + m * n * 4
                cost = traffic / HBM_BYTES_PER_US_V7X + steps * GRID_STEP_US
                if best is None or cost < best[0]:
                    best = (cost, tk, tm, tn)
    return best[1], best[2], best[3]


ARB = "arbitrary"
PAR = "parallel"


def _rms_fwd(x, g):
    r = lax.rsqrt(jnp.mean(x * x, axis=-1, keepdims=True) + EPS)
    return x * r * g


def _rms_bwd(dy, x, g):
    r = lax.rsqrt(jnp.mean(x * x, axis=-1, keepdims=True) + EPS)
    xh = x * r
    gdy = dy * g
    dx = r * (gdy - xh * jnp.mean(gdy * xh, axis=-1, keepdims=True))
    dg = jnp.sum(dy * xh, axis=0, keepdims=True)
    return dx, dg


_GELU_K = math.sqrt(2.0 / math.pi)
_GELU_C = 0.044715


def _gelu(x):
    t = jnp.tanh(_GELU_K * (x + _GELU_C * x * x * x))
    return 0.5 * x * (1.0 + t)


def _gelu_and_grad(x):
    x2 = x * x
    t = jnp.tanh(_GELU_K * (x + _GELU_C * x2 * x))
    g = 0.5 * x * (1.0 + t)
    dg = 0.5 * (1.0 + t) + 0.5 * x * (1.0 - t * t) * (_GELU_K * (1.0 + 3.0 * _GELU_C * x2))
    return g, dg


def _shift_down(x, k, row):
    return jnp.where(row >= k, pltpu.roll(x, k, axis=0), 0.0)


def _shift_up(x, k, row):
    n = x.shape[0]
    return jnp.where(row < n - k, pltpu.roll(x, n - k, axis=0), 0.0)


def _shift_down_edge(x, k):
    r = pltpu.roll(x, k, axis=0)
    row = lax.broadcasted_iota(jnp.int32, (SUBLANE, x.shape[1]), 0)
    return jnp.concatenate([jnp.where(row >= k, r[:SUBLANE], 0.0), r[SUBLANE:]], axis=0)


def _shift_up_edge(x, k):
    n = x.shape[0]
    r = pltpu.roll(x, n - k, axis=0)
    row = lax.broadcasted_iota(jnp.int32, (SUBLANE, x.shape[1]), 0)
    return jnp.concatenate([r[:n - SUBLANE], jnp.where(row < SUBLANE - k, r[n - SUBLANE:], 0.0)], axis=0)


def _dot(a, b):
    return jnp.dot(a, b, preferred_element_type=F32)


def _dot_nt(a, b):
    return lax.dot_general(a, b, (((1,), (1,)), ((), ())), preferred_element_type=F32)


def _dot_tn(a, b):
    return lax.dot_general(a, b, (((0,), (0,)), ((), ())), preferred_element_type=F32)


MXU_FLOPS_PER_US = 7.0e8
HOST_US = {"lru_fwd": 44.0, "lru_bwd": 94.0, "swa_fwd": 60.0, "swa_bwd": 160.0, "mem_attn_fwd": 21.0,
           "mem_attn_bwd": 33.0, "ffn_act": 70.0, "ffn_act_bwd": 100.0, "resid": 22.0, "resid_bwd": 33.0}


def _hosted_call(body, *, grid, in_specs, out_specs, out_shape, args, name, aliases=None, scratch_shapes=(),
                 q=None, flops=0.0, budget_us=0.0):
    chunks = q.take(flops / MXU_FLOPS_PER_US + budget_us) if q is not None else []
    if not chunks:
        return pl.pallas_call(
            body, grid=grid, in_specs=in_specs, out_specs=out_specs, out_shape=out_shape,
            scratch_shapes=list(scratch_shapes), input_output_aliases=aliases or {}, name=name,
            compiler_params=_cp((ARB,) * len(grid)))(*args)
    single = not isinstance(out_shape, (list, tuple))
    o_shapes = [out_shape] if single else list(out_shape)
    o_specs = [out_specs] if single else list(out_specs)
    n_in, n_out, n_scr = len(args), len(o_shapes), len(scratch_shapes)
    c_ins = [a for ch in chunks for a in ch.ins]
    c_outs = [s for ch in chunks for s in ch.out_shapes]
    alias = dict(aliases or {})
    in_off, out_off, sem_off = [], [], []
    i0 = o0 = s0 = 0
    for ch in chunks:
        in_off.append(i0)
        out_off.append(o0)
        sem_off.append(s0)
        for ci, co in ch.alias.items():
            alias[n_in + i0 + ci] = n_out + o0 + co
        i0 += len(ch.ins)
        o0 += len(ch.out_shapes)
        s0 += ch.n_sem

    def wrapped(*refs):
        ins = refs[:n_in]
        cin = refs[n_in:n_in + i0]
        outs = refs[n_in + i0:n_in + i0 + n_out]
        cout = refs[n_in + i0 + n_out:n_in + i0 + n_out + o0]
        scr = refs[n_in + i0 + n_out + o0:n_in + i0 + n_out + o0 + n_scr]
        send_sems, recv_sems = refs[n_in + i0 + n_out + o0 + n_scr:]
        first = functools.reduce(lambda u, v: u & v, [pl.program_id(d) == 0 for d in range(len(grid))])
        last = functools.reduce(lambda u, v: u & v, [pl.program_id(d) == grid[d] - 1 for d in range(len(grid))])

        def each(phase):
            for ch, a, b, s in zip(chunks, in_off, out_off, sem_off):
                getattr(ch, phase)(cin[a:a + len(ch.ins)], cout[b:b + len(ch.out_shapes)], send_sems, recv_sems, s)

        pl.when(first)(lambda: each("start"))
        body(*ins, *outs, *scr)
        pl.when(last)(lambda: each("finish"))

    hbm = pl.BlockSpec(memory_space=pl.ANY)
    res = pl.pallas_call(
        wrapped, grid=grid, in_specs=list(in_specs) + [hbm] * i0, out_specs=o_specs + [hbm] * o0,
        out_shape=o_shapes + c_outs,
        scratch_shapes=list(scratch_shapes) + [pltpu.SemaphoreType.DMA((s0,)), pltpu.SemaphoreType.DMA((s0,))],
        input_output_aliases=alias, name=name,
        compiler_params=pltpu.CompilerParams(dimension_semantics=(ARB,) * len(grid), vmem_limit_bytes=VMEM_LIMIT_V7X,
                                             has_side_effects=True))(*args, *c_ins)
    for ch, b in zip(chunks, out_off):
        ch.done(list(res[n_out + b:n_out + b + len(ch.out_shapes)]))
    return res[0] if single else list(res[:n_out])


def _mm_nn(a, b, *, name, q=None, out_dtype=F32):
    m, k = a.shape
    n = b.shape[-1]
    tm, tn = _mm_tiles(m, k, n, jnp.dtype(out_dtype).itemsize)

    def body(a_ref, b_ref, o_ref):
        o_ref[...] = _dot(a_ref[...], b_ref[...]).astype(o_ref.dtype)

    return _hosted_call(
        body, grid=(m // tm, n // tn),
        in_specs=[pl.BlockSpec((tm, k), lambda i, j: (i, 0)),
                  pl.BlockSpec((None, k, tn), lambda i, j: (0, 0, j))],
        out_specs=pl.BlockSpec((tm, tn), lambda i, j: (i, j)),
        out_shape=jax.ShapeDtypeStruct((m, n), out_dtype),
        args=(a, b), name=name, q=q, flops=2.0 * m * k * n)


def _mm_nt(a, b, *, name, q=None, out_dtype=F32):
    m, k = a.shape
    n = b.shape[-2]
    tm, tn = _mm_tiles(m, k, n, jnp.dtype(out_dtype).itemsize)

    def body(a_ref, b_ref, o_ref):
        o_ref[...] = _dot_nt(a_ref[...], b_ref[...]).astype(o_ref.dtype)

    return _hosted_call(
        body, grid=(m // tm, n // tn),
        in_specs=[pl.BlockSpec((tm, k), lambda i, j: (i, 0)),
                  pl.BlockSpec((None, tn, k), lambda i, j: (0, j, 0))],
        out_specs=pl.BlockSpec((tm, tn), lambda i, j: (i, j)),
        out_shape=jax.ShapeDtypeStruct((m, n), out_dtype),
        args=(a, b), name=name, q=q, flops=2.0 * m * k * n)


def _mm_nn_slots(a, b4, *, name, q=None, out_dtype=F32):
    m, k = a.shape
    s_, _, _, c = b4.shape
    ob = jnp.dtype(out_dtype).itemsize
    tm = max(t for t in _divisors(m, 256) if 2 * (t * k * 2 + k * c * 2 + t * c * ob) <= MM_VMEM_BUDGET)

    def body(a_ref, b_ref, o_ref):
        o_ref[...] = _dot(a_ref[...], b_ref[...]).astype(o_ref.dtype)

    return _hosted_call(
        body, grid=(m // tm, s_),
        in_specs=[pl.BlockSpec((tm, k), lambda i, j: (i, 0)),
                  pl.BlockSpec((None, None, k, c), lambda i, j: (j, 0, 0, 0))],
        out_specs=pl.BlockSpec((tm, c), lambda i, j: (i, j)),
        out_shape=jax.ShapeDtypeStruct((m, s_ * c), out_dtype),
        args=(a, b4), name=name, q=q, flops=2.0 * m * k * s_ * c)


def _mm_tn_slots(a, b, *, name, slot_cols, n_slots, first_slot=0, q=None, out=None):
    k, m = a.shape
    c = slot_cols
    tk, tm, _ = _mm_tn_tiles(k, m, c, whole_n=True)

    def body(a_ref, b_ref, *rest):
        o_ref = rest[-1]
        part = _dot_tn(a_ref[...], b_ref[...])

        @pl.when(pl.program_id(2) == 0)
        def _():
            o_ref[...] = part

        @pl.when(pl.program_id(2) > 0)
        def _():
            o_ref[...] += part

    in_specs = [pl.BlockSpec((tk, tm), lambda i, j, s: (s, i)), pl.BlockSpec((tk, c), lambda i, j, s: (s, j))]
    args = (a, b)
    if out is not None:
        in_specs.append(pl.BlockSpec(memory_space=pl.ANY))
        args = (a, b, out)
    return _hosted_call(
        body, grid=(m // tm, b.shape[-1] // c, k // tk), in_specs=in_specs,
        out_specs=pl.BlockSpec((None, None, tm, c), lambda i, j, s: (first_slot + j, 0, i, 0)),
        out_shape=jax.ShapeDtypeStruct((n_slots, 1, m, c), F32),
        aliases={2: 0} if out is not None else None,
        args=args, name=name, q=q, flops=2.0 * m * k * b.shape[-1])


def _mm_tn(a, b, *, name, q=None, out=None, n_total=None, col_block_offset=0):
    k, m = a.shape
    n = b.shape[-1]
    tk, tm, tn = _mm_tn_tiles(k, m, n)
    off = col_block_offset * (n // tn)

    def body(a_ref, b_ref, *rest):
        o_ref = rest[-1]
        part = _dot_tn(a_ref[...], b_ref[...])

        @pl.when(pl.program_id(2) == 0)
        def _():
            o_ref[...] = part

        @pl.when(pl.program_id(2) > 0)
        def _():
            o_ref[...] += part

    in_specs = [pl.BlockSpec((tk, tm), lambda i, j, s: (s, i)), pl.BlockSpec((tk, tn), lambda i, j, s: (s, j))]
    args = (a, b)
    if out is not None:
        in_specs.append(pl.BlockSpec(memory_space=pl.ANY))
        args = (a, b, out)
    return _hosted_call(
        body, grid=(m // tm, n // tn, k // tk), in_specs=in_specs,
        out_specs=pl.BlockSpec((None, tm, tn), lambda i, j, s: (0, i, j + off)),
        out_shape=jax.ShapeDtypeStruct((1, m, n_total or n), F32),
        aliases={2: 0} if out is not None else None,
        args=args, name=name, q=q, flops=2.0 * m * k * n)


def _mm_ffn_dh(dg, dv, w4, *, name, q=None):
    m, f = dg.shape
    n_slots, _, d, c = w4.shape
    tm, tn = _mm_tiles(m, 2 * f, d, 4)

    def body(dg_ref, dv_ref, *rest):
        w_refs, o_ref = rest[:n_slots], rest[n_slots]
        acc = None
        for s, w_ref in enumerate(w_refs):
            x_ref = dg_ref if s < n_slots // 2 else dv_ref
            off = (s % (n_slots // 2)) * c
            part = _dot_nt(x_ref[:, off:off + c], w_ref[...])
            acc = part if acc is None else acc + part
        o_ref[...] = acc.astype(o_ref.dtype)

    wspec = lambda s: pl.BlockSpec((None, None, tn, c), lambda i, j: (s, 0, j, 0))
    return _hosted_call(
        body, grid=(m // tm, d // tn),
        in_specs=[pl.BlockSpec((tm, f), lambda i, j: (i, 0)),
                  pl.BlockSpec((tm, f), lambda i, j: (i, 0))] + [wspec(s) for s in range(n_slots)],
        out_specs=pl.BlockSpec((tm, tn), lambda i, j: (i, j)),
        out_shape=jax.ShapeDtypeStruct((m, d), MXU),
        args=(dg, dv) + (w4,) * n_slots, name=name, q=q, flops=4.0 * m * f * d)


def _norm_fwd(x, g, *, name):
    n, d = x.shape
    tm = _tile(n, 256, SUBLANE)

    def body(x_ref, g_ref, o_ref):
        o_ref[...] = _rms_fwd(x_ref[...], g_ref[...]).astype(o_ref.dtype)

    return pl.pallas_call(
        body, grid=(n // tm,),
        in_specs=[pl.BlockSpec((tm, d), lambda i: (i, 0)), pl.BlockSpec((1, d), lambda i: (0, 0))],
        out_specs=pl.BlockSpec((tm, d), lambda i: (i, 0)),
        out_shape=jax.ShapeDtypeStruct((n, d), MXU),
        name=name, compiler_params=_cp((PAR,)))(x, g)


def _norm_bwd_dg(dy, x, g, *, name):
    n, d = x.shape
    tm = _tile(n, 256, SUBLANE)

    def body(dy_ref, x_ref, g_ref, dg_ref):
        @pl.when(pl.program_id(0) == 0)
        def _():
            dg_ref[...] = jnp.zeros_like(dg_ref)
        _, dg = _rms_bwd(dy_ref[...], x_ref[...], g_ref[...])
        dg_ref[...] += dg

    return pl.pallas_call(
        body, grid=(n // tm,),
        in_specs=[pl.BlockSpec((tm, d), lambda i: (i, 0)), pl.BlockSpec((tm, d), lambda i: (i, 0)),
                  pl.BlockSpec((1, d), lambda i: (0, 0))],
        out_specs=pl.BlockSpec((1, d), lambda i: (0, 0)),
        out_shape=jax.ShapeDtypeStruct((1, d), F32),
        name=name, compiler_params=_cp((ARB,)))(dy, x, g)


def _resid_norm_fwd(x, y, g_post, g_pres, *, name, q=None):
    n, d = x.shape
    tm = _tile(n, 256, SUBLANE)
    nh = len(g_pres)

    def body(x_ref, y_ref, gp_ref, *rest):
        gpre = rest[:nh]
        xo_ref = rest[nh]
        h_refs = rest[nh + 1:]
        xo = x_ref[...] + _rms_fwd(y_ref[...].astype(F32), gp_ref[...])
        xo_ref[...] = xo
        for g_ref, h_ref in zip(gpre, h_refs):
            h_ref[...] = _rms_fwd(xo, g_ref[...]).astype(h_ref.dtype)

    row = pl.BlockSpec((tm, d), lambda i: (i, 0))
    vec = pl.BlockSpec((1, d), lambda i: (0, 0))
    outs = _hosted_call(
        body, grid=(n // tm,),
        in_specs=[row, row, vec] + [vec] * nh,
        out_specs=[row] + [row] * nh,
        out_shape=[jax.ShapeDtypeStruct((n, d), F32)] + [jax.ShapeDtypeStruct((n, d), MXU)] * nh,
        args=(x, y, g_post, *g_pres), name=name, q=q, budget_us=HOST_US["resid"])
    return outs[0], list(outs[1:])


def _loss_fwd(x, y, g_post, target, *, name):
    n, d = x.shape
    tm = _tile(n, 256, SUBLANE)

    def body(x_ref, y_ref, gp_ref, t_ref, dx_ref, sq_ref):
        @pl.when(pl.program_id(0) == 0)
        def _():
            sq_ref[...] = jnp.zeros_like(sq_ref)
        err = x_ref[...] + _rms_fwd(y_ref[...].astype(F32), gp_ref[...]) - t_ref[...]
        dx_ref[...] = err * (1.0 / d)
        sq_ref[...] += jnp.sum(err * err, axis=0, keepdims=True)

    row = pl.BlockSpec((tm, d), lambda i: (i, 0))
    vec = pl.BlockSpec((1, d), lambda i: (0, 0))
    return pl.pallas_call(
        body, grid=(n // tm,),
        in_specs=[row, row, vec, row],
        out_specs=[row, vec],
        out_shape=[jax.ShapeDtypeStruct((n, d), F32), jax.ShapeDtypeStruct((1, d), F32)],
        name=name, compiler_params=_cp((ARB,)))(x, y, g_post, target)


def _resid_norm_bwd(dx_out, dhs, x_out, g_pres, y, g_post, *, name, q=None):
    n, d = dx_out.shape
    tm = _tile(n, 256, SUBLANE)
    nh = len(dhs)
    has_y = y is not None

    def body(*refs):
        it = iter(refs)
        dxo_ref = next(it)
        dh_refs = [next(it) for _ in range(nh)]
        xo_ref = next(it) if nh else None
        gpre_refs = [next(it) for _ in range(nh)]
        y_ref = next(it) if has_y else None
        gpost_ref = next(it) if has_y else None
        g_out = next(it)
        dy_out = next(it) if has_y else None
        dgpre_out = [next(it) for _ in range(nh)]
        dgpost_out = next(it) if has_y else None

        @pl.when(pl.program_id(0) == 0)
        def _():
            for r in dgpre_out:
                r[...] = jnp.zeros_like(r)
            if has_y:
                dgpost_out[...] = jnp.zeros_like(dgpost_out)

        g = dxo_ref[...]
        if nh:
            xo = xo_ref[...]
            for dh_ref, gp_ref, dg_ref in zip(dh_refs, gpre_refs, dgpre_out):
                dx, dg = _rms_bwd(dh_ref[...].astype(F32), xo, gp_ref[...])
                g = g + dx
                dg_ref[...] += dg
        g_out[...] = g
        if has_y:
            dy, dg = _rms_bwd(g, y_ref[...].astype(F32), gpost_ref[...])
            dy_out[...] = dy.astype(dy_out.dtype)
            dgpost_out[...] += dg

    row = pl.BlockSpec((tm, d), lambda i: (i, 0))
    vec = pl.BlockSpec((1, d), lambda i: (0, 0))
    ins, in_specs = [dx_out], [row]
    ins += list(dhs)
    in_specs += [row] * nh
    if nh:
        ins.append(x_out)
        in_specs.append(row)
    ins += list(g_pres)
    in_specs += [vec] * nh
    if has_y:
        ins += [y, g_post]
        in_specs += [row, vec]
    out_specs, out_shape = [row], [jax.ShapeDtypeStruct((n, d), F32)]
    if has_y:
        out_specs.append(row)
        out_shape.append(jax.ShapeDtypeStruct((n, d), MXU))
    out_specs += [vec] * nh
    out_shape += [jax.ShapeDtypeStruct((1, d), F32)] * nh
    if has_y:
        out_specs.append(vec)
        out_shape.append(jax.ShapeDtypeStruct((1, d), F32))
    outs = list(_hosted_call(
        body, grid=(n // tm,), in_specs=in_specs, out_specs=out_specs, out_shape=out_shape,
        args=tuple(ins), name=name, q=q, budget_us=HOST_US["resid_bwd"]))
    g = outs.pop(0)
    dy = outs.pop(0) if has_y else None
    dgpre = [outs.pop(0) for _ in range(nh)]
    dgpost = outs.pop(0) if has_y else None
    return g, dy, dgpre, dgpost


def _ffn_conv(up, w_ref, b_ref):
    return (w_ref[0:1, :] * _shift_down_edge(up, 2) + w_ref[1:2, :] * _shift_down_edge(up, 1)
            + w_ref[2:3, :] * up + b_ref[...])


def _ffn_act_fwd(up, wconv, bconv, bsz, *, name, q=None):
    n, f2 = up.shape
    f = f2 // 2
    t = n // bsz
    tc = _tile(f, 256)
    nf = f // tc

    def body(ug_ref, uv_ref, wg_ref, wv_ref, bg_ref, bv_ref, o_ref, g_ref, v_ref):
        g = _ffn_conv(ug_ref[...].astype(F32), wg_ref, bg_ref)
        v = _ffn_conv(uv_ref[...].astype(F32), wv_ref, bv_ref)
        g_ref[...] = g.astype(g_ref.dtype)
        v_ref[...] = v.astype(v_ref.dtype)
        o_ref[...] = (_gelu(g) * v).astype(o_ref.dtype)

    blk = pl.BlockSpec((t, tc), lambda b, j: (b, j))
    return _hosted_call(
        body, grid=(bsz, nf),
        in_specs=[blk, pl.BlockSpec((t, tc), lambda b, j: (b, j + nf)),
                  pl.BlockSpec((FFN_CONV, tc), lambda b, j: (0, j)),
                  pl.BlockSpec((FFN_CONV, tc), lambda b, j: (0, j + nf)),
                  pl.BlockSpec((1, tc), lambda b, j: (0, j)),
                  pl.BlockSpec((1, tc), lambda b, j: (0, j + nf))],
        out_specs=[blk, blk, blk],
        out_shape=[jax.ShapeDtypeStruct((n, f), MXU)] * 3,
        args=(up, up, wconv, wconv, bconv, bconv), name=name, q=q, budget_us=HOST_US["ffn_act"])


def _ffn_act_bwd(up, ug, uv, dact, wconv, bsz, *, name, q=None):
    n, f2 = up.shape
    f = f2 // 2
    t = n // bsz
    tc = _tile(f, 256)
    nf = f // tc

    def body(xg_ref, xv_ref, g_ref, v_ref, da_ref, wg_ref, wv_ref,
             dug_ref, duv_ref, dwg_ref, dwv_ref, dbg_ref, dbv_ref):
        @pl.when(pl.program_id(1) == 0)
        def _():
            for r in (dwg_ref, dwv_ref, dbg_ref, dbv_ref):
                r[...] = jnp.zeros_like(r)

        gl, dgl = _gelu_and_grad(g_ref[...].astype(F32))
        da = da_ref[...].astype(F32)
        dg = da * v_ref[...].astype(F32) * dgl
        dv = da * gl

        def conv_bwd(du, w_ref, x_ref, dx_ref, dw_ref, db_ref):
            du1, du2 = _shift_up_edge(du, 1), _shift_up_edge(du, 2)
            dx_ref[...] = (w_ref[2:3, :] * du + w_ref[1:2, :] * du1 + w_ref[0:1, :] * du2).astype(dx_ref.dtype)
            x = x_ref[...].astype(F32)
            dw_ref[0:1, :] += jnp.sum(x * du2, axis=0, keepdims=True)
            dw_ref[1:2, :] += jnp.sum(x * du1, axis=0, keepdims=True)
            dw_ref[2:3, :] += jnp.sum(x * du, axis=0, keepdims=True)
            db_ref[...] += jnp.sum(du, axis=0, keepdims=True)

        conv_bwd(dg, wg_ref, xg_ref, dug_ref, dwg_ref, dbg_ref)
        conv_bwd(dv, wv_ref, xv_ref, duv_ref, dwv_ref, dbv_ref)

    blk = pl.BlockSpec((t, tc), lambda j, b: (b, j))
    wspec = pl.BlockSpec((FFN_CONV, tc), lambda j, b: (0, j))
    bspec = pl.BlockSpec((1, tc), lambda j, b: (0, j))
    outs = _hosted_call(
        body, grid=(nf, bsz),
        in_specs=[blk, pl.BlockSpec((t, tc), lambda j, b: (b, j + nf)), blk, blk, blk,
                  wspec, pl.BlockSpec((FFN_CONV, tc), lambda j, b: (0, j + nf))],
        out_specs=[blk, blk, wspec, wspec, bspec, bspec],
        out_shape=[jax.ShapeDtypeStruct((n, f), MXU), jax.ShapeDtypeStruct((n, f), MXU),
                   jax.ShapeDtypeStruct((FFN_CONV, f), F32), jax.ShapeDtypeStruct((FFN_CONV, f), F32),
                   jax.ShapeDtypeStruct((1, f), F32), jax.ShapeDtypeStruct((1, f), F32)],
        args=(up, up, ug, uv, dact, wconv, wconv), name=name, q=q, budget_us=HOST_US["ffn_act_bwd"])
    dug, duv, dwg, dwv, dbg, dbv = outs
    return dug, duv, jnp.concatenate([dwg, dwv], axis=1), jnp.concatenate([dbg, dbv], axis=1)


def _mem_attn_fwd(proj, q_col_block, mkv, ycat, bsz, *, name, q=None):
    n = proj.shape[0]
    t = n // bsz
    tq = _tile(t, 512, SUBLANE)
    nt = t // tq
    scale = HEAD_DIM ** -0.5

    def body(q_ref, kv_ref, old_ref, o_ref):
        del old_ref
        outs = []
        for h in range(MEM_HEADS):
            sl = slice(h * HEAD_DIM, (h + 1) * HEAD_DIM)
            q = q_ref[:, sl].astype(MXU)
            k = kv_ref[:, sl].astype(MXU)
            v = kv_ref[:, MEM_WIDTH + h * HEAD_DIM: MEM_WIDTH + (h + 1) * HEAD_DIM].astype(MXU)
            s = _dot_nt(q, k) * scale
            m = jnp.max(s, axis=-1, keepdims=True)
            p = jnp.exp(s - m)
            p = p / jnp.sum(p, axis=-1, keepdims=True)
            outs.append(_dot(p.astype(MXU), v))
        o_ref[...] = jnp.concatenate(outs, axis=-1).astype(o_ref.dtype)

    return _hosted_call(
        body, grid=(bsz, nt),
        in_specs=[pl.BlockSpec((tq, MEM_WIDTH), lambda b, i: (b * nt + i, q_col_block)),
                  pl.BlockSpec((MEM_LEN, 2 * MEM_WIDTH), lambda b, i: (b, 0)),
                  pl.BlockSpec(memory_space=pl.ANY)],
        out_specs=pl.BlockSpec((tq, MEM_WIDTH), lambda b, i: (b * nt + i, MIX_WIDTH // MEM_WIDTH)),
        out_shape=jax.ShapeDtypeStruct(ycat.shape, ycat.dtype),
        aliases={2: 0}, args=(proj, mkv, ycat), name=name, q=q, budget_us=HOST_US["mem_attn_fwd"])


def _mem_attn_bwd(proj, q_col_block, mkv, dycat, dproj, bsz, *, name, q=None):
    n = proj.shape[0]
    t = n // bsz
    tq = _tile(t, 512, SUBLANE)
    nt = t // tq
    scale = HEAD_DIM ** -0.5

    def body(q_ref, kv_ref, do_ref, old_ref, dq_ref, dkv_ref):
        del old_ref

        @pl.when(pl.program_id(1) == 0)
        def _():
            dkv_ref[...] = jnp.zeros_like(dkv_ref)

        dqs, dks, dvs = [], [], []
        for h in range(MEM_HEADS):
            sl = slice(h * HEAD_DIM, (h + 1) * HEAD_DIM)
            q = q_ref[:, sl].astype(MXU)
            k = kv_ref[:, sl].astype(MXU)
            v = kv_ref[:, MEM_WIDTH + h * HEAD_DIM: MEM_WIDTH + (h + 1) * HEAD_DIM].astype(MXU)
            do = do_ref[:, sl].astype(MXU)
            s = _dot_nt(q, k) * scale
            m = jnp.max(s, axis=-1, keepdims=True)
            p = jnp.exp(s - m)
            p = p / jnp.sum(p, axis=-1, keepdims=True)
            dvs.append(_dot_tn(p.astype(MXU), do))
            dp = _dot_nt(do, v)
            ds = (p * (dp - jnp.sum(dp * p, axis=-1, keepdims=True)) * scale).astype(MXU)
            dqs.append(_dot(ds, k))
            dks.append(_dot_tn(ds, q))
        dq_ref[...] = jnp.concatenate(dqs, axis=-1).astype(dq_ref.dtype)
        dkv_ref[...] += jnp.concatenate(dks + dvs, axis=-1)

    return _hosted_call(
        body, grid=(bsz, nt),
        in_specs=[pl.BlockSpec((tq, MEM_WIDTH), lambda b, i: (b * nt + i, q_col_block)),
                  pl.BlockSpec((MEM_LEN, 2 * MEM_WIDTH), lambda b, i: (b, 0)),
                  pl.BlockSpec((tq, MEM_WIDTH), lambda b, i: (b * nt + i, MIX_WIDTH // MEM_WIDTH)),
                  pl.BlockSpec(memory_space=pl.ANY)],
        out_specs=[pl.BlockSpec((tq, MEM_WIDTH), lambda b, i: (b * nt + i, q_col_block)),
                   pl.BlockSpec((MEM_LEN, 2 * MEM_WIDTH), lambda b, i: (b, 0))],
        out_shape=[jax.ShapeDtypeStruct(dproj.shape, dproj.dtype),
                   jax.ShapeDtypeStruct((bsz * MEM_LEN, 2 * MEM_WIDTH), F32)],
        aliases={3: 0}, args=(proj, mkv, dycat, dproj), name=name, q=q, budget_us=HOST_US["mem_attn_bwd"])


def _swa_scores(q, k, slope, dist, mask, sink):
    s = _dot_nt(q, k) * (HEAD_DIM ** -0.5)
    s = jnp.where(mask, s - slope * dist, -jnp.inf)
    m = jnp.maximum(jnp.max(s, axis=-1, keepdims=True), sink)
    p = jnp.exp(s - m)
    psink = jnp.exp(sink - m)
    inv = 1.0 / (jnp.sum(p, axis=-1, keepdims=True) + psink)
    return p * inv, psink * inv


def _swa_mask(n):
    rows = SWA_GROUP * WINDOW
    qi = lax.broadcasted_iota(jnp.int32, (rows, 2 * WINDOW), 0) % WINDOW + WINDOW
    ki = lax.broadcasted_iota(jnp.int32, (rows, 2 * WINDOW), 1)
    dist = qi - ki
    mask = (dist >= 0) & (dist < WINDOW) & ((n > 0) | (ki >= WINDOW))
    return dist.astype(F32), mask


def _swa_group_columns(sink_ref, c):
    heads = range(c * SWA_GROUP, (c + 1) * SWA_GROUP)
    slope = jnp.concatenate([jnp.full((WINDOW, 1), SLOPES[h], F32) for h in heads], axis=0)
    sink = jnp.concatenate([jnp.full((WINDOW, 1), sink_ref[h], F32) for h in heads], axis=0)
    return slope, sink


def _swa_stack(ref, c):
    return jnp.concatenate([ref[:, h * HEAD_DIM:(h + 1) * HEAD_DIM].astype(MXU)
                            for h in range(c * SWA_GROUP, (c + 1) * SWA_GROUP)], axis=0)


def _swa_unstack(x):
    return jnp.concatenate([x[g * WINDOW:(g + 1) * WINDOW] for g in range(SWA_GROUP)], axis=-1)


def _swa_fwd(proj, kv, sinks, bsz, *, name, q=None):
    n_tok = proj.shape[0]
    nb = n_tok // bsz // WINDOW
    kvw = SWA_KV_HEADS * HEAD_DIM

    def body(sink_ref, q_ref, kvp_ref, kvc_ref, o_ref):
        n = pl.program_id(1)
        dist, mask = _swa_mask(n)
        kk = jnp.concatenate([kvp_ref[:, :kvw], kvc_ref[:, :kvw]], axis=0).astype(MXU)
        vv = jnp.concatenate([kvp_ref[:, kvw:], kvc_ref[:, kvw:]], axis=0).astype(MXU)
        outs = []
        for c in range(SWA_KV_HEADS):
            slope, sink = _swa_group_columns(sink_ref, c)
            p, _ = _swa_scores(_swa_stack(q_ref, c), kk[:, c * HEAD_DIM:(c + 1) * HEAD_DIM], slope, dist, mask, sink)
            outs.append(_swa_unstack(_dot(p.astype(MXU), vv[:, c * HEAD_DIM:(c + 1) * HEAD_DIM])))
        o_ref[...] = jnp.concatenate(outs, axis=-1).astype(o_ref.dtype)

    return _hosted_call(
        body, grid=(bsz, nb),
        in_specs=[pl.BlockSpec(memory_space=pltpu.SMEM),
                  pl.BlockSpec((WINDOW, MIX_WIDTH), lambda b, n: (b * nb + n, 0)),
                  pl.BlockSpec((WINDOW, 2 * kvw), lambda b, n: (b * nb + jnp.maximum(n - 1, 0), 0)),
                  pl.BlockSpec((WINDOW, 2 * kvw), lambda b, n: (b * nb + n, 0))],
        out_specs=pl.BlockSpec((WINDOW, MIX_WIDTH), lambda b, n: (b * nb + n, 0)),
        out_shape=jax.ShapeDtypeStruct((n_tok, D_MODEL), MXU),
        args=(sinks, proj, kv, kv), name=name, q=q, budget_us=HOST_US["swa_fwd"])


def _swa_bwd(proj, kv, sinks, dycat, bsz, *, name, q=None):
    n_tok = proj.shape[0]
    nb = n_tok // bsz // WINDOW
    kvw = SWA_KV_HEADS * HEAD_DIM

    def body(sink_ref, q_ref, kvp_ref, kvc_ref, do_ref, dq_ref, dkvc_ref, dkvp_ref, dsink_ref):
        n = pl.program_id(1)

        @pl.when((pl.program_id(0) == 0) & (n == 0))
        def _():
            dsink_ref[...] = jnp.zeros_like(dsink_ref)

        dist, mask = _swa_mask(n)
        kk = jnp.concatenate([kvp_ref[:, :kvw], kvc_ref[:, :kvw]], axis=0).astype(MXU)
        vv = jnp.concatenate([kvp_ref[:, kvw:], kvc_ref[:, kvw:]], axis=0).astype(MXU)
        lane = lax.broadcasted_iota(jnp.int32, (SUBLANE, LANE), 1)
        dqs, dks, dvs = [], [], []
        dsink = jnp.zeros((SUBLANE, LANE), F32)
        for c in range(SWA_KV_HEADS):
            k = kk[:, c * HEAD_DIM:(c + 1) * HEAD_DIM]
            v = vv[:, c * HEAD_DIM:(c + 1) * HEAD_DIM]
            q = _swa_stack(q_ref, c)
            do = _swa_stack(do_ref, c)
            slope, sink = _swa_group_columns(sink_ref, c)
            p, psink = _swa_scores(q, k, slope, dist, mask, sink)
            dvs.append(_dot_tn(p.astype(MXU), do))
            dp = _dot_nt(do, v)
            rs = jnp.sum(dp * p, axis=-1, keepdims=True)
            ds = (p * (dp - rs) * (HEAD_DIM ** -0.5)).astype(MXU)
            dsk = -psink * rs
            for g in range(SWA_GROUP):
                dsink = dsink + jnp.where(lane == c * SWA_GROUP + g,
                                          jnp.sum(dsk[g * WINDOW:(g + 1) * WINDOW], axis=0, keepdims=True), 0.0)
            dqs.append(_swa_unstack(_dot(ds, k)))
            dks.append(_dot_tn(ds, q))
        dq_ref[...] = jnp.concatenate(dqs, axis=-1).astype(dq_ref.dtype)
        dkv = jnp.concatenate(dks + dvs, axis=-1)
        dkvp_ref[...] = dkv[:WINDOW]
        dkvc_ref[...] = dkv[WINDOW:]
        dsink_ref[...] += dsink

    qspec = pl.BlockSpec((WINDOW, MIX_WIDTH), lambda b, n: (b * nb + n, 0))
    kvspec = pl.BlockSpec((WINDOW, 2 * kvw), lambda b, n: (b * nb + n, 0))
    return _hosted_call(
        body, grid=(bsz, nb),
        in_specs=[pl.BlockSpec(memory_space=pltpu.SMEM), qspec,
                  pl.BlockSpec((WINDOW, 2 * kvw), lambda b, n: (b * nb + jnp.maximum(n - 1, 0), 0)),
                  kvspec, qspec],
        out_specs=[qspec, kvspec, kvspec, pl.BlockSpec((SUBLANE, LANE), lambda b, n: (0, 0))],
        out_shape=[jax.ShapeDtypeStruct((n_tok, D_MODEL), MXU),
                   jax.ShapeDtypeStruct((n_tok, 2 * kvw), F32),
                   jax.ShapeDtypeStruct((n_tok, 2 * kvw), F32),
                   jax.ShapeDtypeStruct((SUBLANE, LANE), F32)],
        args=(sinks, proj, kv, kv, dycat), name=name, q=q, budget_us=HOST_US["swa_bwd"])


def _swa_dkv_combine(curs, prevs, bsz, *, name):
    n_tok, w = curs[0].shape
    nb = n_tok // bsz // WINDOW
    k = len(curs)

    def body(*refs):
        o_ref = refs[-1]
        n = pl.program_id(1)
        acc = refs[0][...]
        for r in refs[1:k]:
            acc = acc + r[...]
        nxt = refs[k][...]
        for r in refs[k + 1:2 * k]:
            nxt = nxt + r[...]
        o_ref[...] = (acc + jnp.where(n < nb - 1, nxt, 0.0)).astype(o_ref.dtype)

    cur = pl.BlockSpec((WINDOW, w), lambda b, n: (b * nb + n, 0))
    prv = pl.BlockSpec((WINDOW, w), lambda b, n: (b * nb + jnp.minimum(n + 1, nb - 1), 0))
    return pl.pallas_call(
        body, grid=(bsz, nb), in_specs=[cur] * k + [prv] * k, out_specs=cur,
        out_shape=jax.ShapeDtypeStruct((n_tok, w), MXU),
        name=name, compiler_params=_cp((PAR, PAR)))(*curs, *prevs)


def _lru_gates(ux, halo, ext_ref, wc_ref, bc_ref, wr_ref, br_ref, wi_ref, bi_ref, lam_ref):
    tt = ux.shape[0]
    ext_ref[0:SUBLANE, :] = halo
    ext_ref[SUBLANE:, :] = ux
    xs = [ux] + [ext_ref[pl.ds(SUBLANE - k, tt), :] for k in range(1, LRU_CONV)]
    xc = bc_ref[...] + wc_ref[3:4, :] * xs[0] + wc_ref[2:3, :] * xs[1] + wc_ref[1:2, :] * xs[2] + wc_ref[0:1, :] * xs[3]
    pre_r, pre_i = [], []
    for blk in range(MIX_WIDTH // GATE_TILE):
        xb = xc[:, blk * GATE_TILE:(blk + 1) * GATE_TILE].astype(MXU)
        pre_r.append(_dot(xb, wr_ref[blk]))
        pre_i.append(_dot(xb, wi_ref[blk]))
    r = jax.nn.sigmoid(jnp.concatenate(pre_r, axis=-1) + br_ref[...])
    i = jax.nn.sigmoid(jnp.concatenate(pre_i, axis=-1) + bi_ref[...])
    nlam = -lam_ref[...]
    sp = jnp.maximum(nlam, 0.0) + jnp.log(1.0 + jnp.exp(-jnp.abs(nlam)))
    log_a = -LRU_C * r * sp
    a = jnp.exp(log_a)
    om = -jnp.tanh(log_a) * (a * a + 1.0)
    s = jnp.sqrt(om)
    return xs, xc, r, i, sp, a, s


def _lru_fwd(proj, wconv, bconv, wr, br, wi, bi, lam, bsz, *, name, q=None):
    n_tok = proj.shape[0]
    t = n_tok // bsz
    tt = _tile(t, 256, SUBLANE)
    nt = t // tt
    w = MIX_WIDTH
    ng = tt // SUBLANE

    def body(pg_ref, halo_ref, wc_ref, bc_ref, wr_ref, br_ref, wi_ref, bi_ref, lam_ref,
             y_ref, h_ref, ext_ref, a_ref, b_ref, carry_ref):
        ti = pl.program_id(1)

        @pl.when(ti == 0)
        def _():
            carry_ref[...] = jnp.zeros_like(carry_ref)

        gate = pg_ref[:, :w]
        ux = pg_ref[:, w:]
        halo = jnp.where(ti > 0, halo_ref[...], 0.0)
        _, xc, _, i, _, a, s = _lru_gates(ux, halo, ext_ref, wc_ref, bc_ref, wr_ref, br_ref, wi_ref, bi_ref, lam_ref)
        a_ref[...] = a
        b_ref[...] = s * (i * xc)
        row = lax.broadcasted_iota(jnp.int32, (SUBLANE, w), 0)

        def group(g, hprev):
            off = pl.multiple_of(g * SUBLANE, SUBLANE)
            ca = a_ref[pl.ds(off, SUBLANE), :]
            cb = b_ref[pl.ds(off, SUBLANE), :]
            for d in (1, 2, 4):
                a_sh = jnp.where(row >= d, pltpu.roll(ca, d, axis=0), 1.0)
                b_sh = jnp.where(row >= d, pltpu.roll(cb, d, axis=0), 0.0)
                cb = ca * b_sh + cb
                ca = ca * a_sh
            h = ca * hprev + cb
            b_ref[pl.ds(off, SUBLANE), :] = h
            return jnp.broadcast_to(h[SUBLANE - 1:SUBLANE, :], (SUBLANE, w))

        carry_ref[...] = lax.fori_loop(0, ng, group, carry_ref[...])
        h = b_ref[...]
        h_ref[...] = h
        y_ref[...] = (h * _gelu(gate)).astype(y_ref.dtype)

    vec = lambda r: pl.BlockSpec((r, w), lambda b, i: (0, 0))
    wspec = pl.BlockSpec((w // GATE_TILE, GATE_TILE, GATE_TILE), lambda b, i: (0, 0, 0))
    hb = tt // SUBLANE
    return _hosted_call(
        body, grid=(bsz, nt),
        in_specs=[pl.BlockSpec((tt, 2 * w), lambda b, i: (b * nt + i, 0)),
                  pl.BlockSpec((SUBLANE, w), lambda b, i: (jnp.maximum((b * nt + i) * hb - 1, 0), 1)),
                  vec(LRU_CONV), vec(1), wspec, vec(1), wspec, vec(1), vec(1)],
        out_specs=[pl.BlockSpec((tt, w), lambda b, i: (b * nt + i, 0)),
                   pl.BlockSpec((tt, w), lambda b, i: (b * nt + i, 0))],
        out_shape=[jax.ShapeDtypeStruct((n_tok, D_MODEL), MXU), jax.ShapeDtypeStruct((n_tok, w), F32)],
        scratch_shapes=[pltpu.VMEM((tt + SUBLANE, w), F32), pltpu.VMEM((tt, w), F32),
                        pltpu.VMEM((tt, w), F32), pltpu.VMEM((SUBLANE, w), F32)],
        args=(proj, proj, wconv, bconv, wr, br, wi, bi, lam), name=name, q=q, budget_us=HOST_US["lru_fwd"])


def _lru_bwd(proj, hs, dycat, wconv, bconv, wr, br, wi, bi, lam, bsz, *, name, q=None):
    n_tok = proj.shape[0]
    t = n_tok // bsz
    tt = _tile(t, 256, SUBLANE)
    nt = t // tt
    w = MIX_WIDTH
    ng = tt // SUBLANE
    nblk = w // GATE_TILE

    def body(pg_ref, halo_ref, h_ref, hhalo_ref, dy_ref, wc_ref, bc_ref, wr_ref, br_ref, wi_ref, bi_ref, lam_ref,
             dp_ref, dwc_ref, dbc_ref, dwr_ref, dbr_ref, dwi_ref, dbi_ref, dlam_ref,
             ext_ref, a_ref, c_ref, g_ref, gcarry_ref, xcarry_ref):
        bi_ = pl.program_id(0)
        ti = nt - 1 - pl.program_id(1)

        @pl.when((bi_ == 0) & (pl.program_id(1) == 0))
        def _():
            for r in (dwc_ref, dbc_ref, dwr_ref, dbr_ref, dwi_ref, dbi_ref, dlam_ref):
                r[...] = jnp.zeros_like(r)

        @pl.when(pl.program_id(1) == 0)
        def _():
            gcarry_ref[...] = jnp.zeros_like(gcarry_ref)
            xcarry_ref[...] = jnp.zeros_like(xcarry_ref)

        gate = pg_ref[:, :w]
        ux = pg_ref[:, w:]
        halo = jnp.where(ti > 0, halo_ref[...], 0.0)
        xs, xc, r, i, sp, a, s = _lru_gates(ux, halo, ext_ref, wc_ref, bc_ref, wr_ref, br_ref, wi_ref, bi_ref, lam_ref)
        h = h_ref[...]
        gl, dgl = _gelu_and_grad(gate)
        dy = dy_ref[...].astype(F32)
        dgate = dy * h * dgl
        row_t = lax.broadcasted_iota(jnp.int32, (tt, w), 0)
        g_ref[...] = dy * gl + jnp.where(row_t == tt - 1, gcarry_ref[0:1, :], 0.0)
        c_ref[...] = _shift_up(a, 1, row_t)
        row = lax.broadcasted_iota(jnp.int32, (SUBLANE, w), 0)

        a_ref[...] = a

        def group(k, gnext):
            off = pl.multiple_of((ng - 1 - k) * SUBLANE, SUBLANE)
            cc = c_ref[pl.ds(off, SUBLANE), :]
            cb = g_ref[pl.ds(off, SUBLANE), :]
            cb = cb + jnp.where(row == SUBLANE - 1, gnext, 0.0)
            cc = jnp.where(row == SUBLANE - 1, 0.0, cc)
            for d in (1, 2, 4):
                c_sh = jnp.where(row < SUBLANE - d, pltpu.roll(cc, SUBLANE - d, axis=0), 1.0)
                b_sh = jnp.where(row < SUBLANE - d, pltpu.roll(cb, SUBLANE - d, axis=0), 0.0)
                cb = cc * b_sh + cb
                cc = cc * c_sh
            g_ref[pl.ds(off, SUBLANE), :] = cb
            a0 = a_ref[pl.ds(off, SUBLANE), :]
            return jnp.broadcast_to(a0[0:1, :] * cb[0:1, :], (SUBLANE, w))

        gc = lax.fori_loop(0, ng, group, jnp.zeros((SUBLANE, w), F32))
        gcarry_ref[...] = gc
        gsc = g_ref[...]

        hhalo = jnp.where(ti > 0, hhalo_ref[SUBLANE - 1:SUBLANE, :], 0.0)
        hprev = jnp.where(row_t == 0, hhalo, pltpu.roll(h, 1, axis=0))
        gated = i * xc
        d_gated = gsc * s
        d_atot = gsc * hprev - (gsc * gated) * a / s
        d_loga = d_atot * a
        d_r = d_loga * (-LRU_C) * sp
        dlam_ref[...] += jnp.sum(d_loga * r, axis=0, keepdims=True) * (LRU_C * jax.nn.sigmoid(-lam_ref[...]))
        d_i = d_gated * xc
        d_xc = d_gated * i
        d_pr = d_r * r * (1.0 - r)
        d_pi = d_i * i * (1.0 - i)
        dbr_ref[...] += jnp.sum(d_pr, axis=0, keepdims=True)
        dbi_ref[...] += jnp.sum(d_pi, axis=0, keepdims=True)
        extra = []
        for blk in range(nblk):
            sl = slice(blk * GATE_TILE, (blk + 1) * GATE_TILE)
            xb = xc[:, sl].astype(MXU)
            dr_b = d_pr[:, sl].astype(MXU)
            di_b = d_pi[:, sl].astype(MXU)
            dwr_ref[blk] += _dot_tn(xb, dr_b)
            dwi_ref[blk] += _dot_tn(xb, di_b)
            extra.append(_dot_nt(dr_b, wr_ref[blk]) + _dot_nt(di_b, wi_ref[blk]))
        d_xc = d_xc + jnp.concatenate(extra, axis=-1)
        dbc_ref[...] += jnp.sum(d_xc, axis=0, keepdims=True)
        for k in range(LRU_CONV):
            dwc_ref[k:k + 1, :] += jnp.sum(d_xc * xs[LRU_CONV - 1 - k], axis=0, keepdims=True)
        ext_ref[0:tt, :] = d_xc
        ext_ref[tt:, :] = xcarry_ref[...]
        dux = wc_ref[3:4, :] * d_xc
        for k in range(LRU_CONV - 1):
            dux = dux + wc_ref[k:k + 1, :] * ext_ref[pl.ds(LRU_CONV - 1 - k, tt), :]
        xcarry_ref[...] = d_xc[0:SUBLANE, :]
        dp_ref[:, :w] = dgate.astype(dp_ref.dtype)
        dp_ref[:, w:] = dux.astype(dp_ref.dtype)

    vec = lambda r: pl.BlockSpec((r, w), lambda b, i: (0, 0))
    wspec = pl.BlockSpec((nblk, GATE_TILE, GATE_TILE), lambda b, i: (0, 0, 0))
    hb = tt // SUBLANE
    rblk = lambda b, i: b * nt + (nt - 1 - i)
    halo_idx = lambda b, i: jnp.maximum(rblk(b, i) * hb - 1, 0)
    wide = pl.BlockSpec((tt, 2 * w), lambda b, i: (rblk(b, i), 0))
    narrow = pl.BlockSpec((tt, w), lambda b, i: (rblk(b, i), 0))
    return _hosted_call(
        body, grid=(bsz, nt),
        in_specs=[wide, pl.BlockSpec((SUBLANE, w), lambda b, i: (halo_idx(b, i), 1)),
                  narrow, pl.BlockSpec((SUBLANE, w), lambda b, i: (halo_idx(b, i), 0)), narrow,
                  vec(LRU_CONV), vec(1), wspec, vec(1), wspec, vec(1), vec(1)],
        out_specs=[wide, vec(LRU_CONV), vec(1), wspec, vec(1), wspec, vec(1), vec(1)],
        out_shape=[jax.ShapeDtypeStruct((n_tok, 2 * w + MEM_WIDTH), MXU),
                   jax.ShapeDtypeStruct((LRU_CONV, w), F32), jax.ShapeDtypeStruct((1, w), F32),
                   jax.ShapeDtypeStruct((nblk, GATE_TILE, GATE_TILE), F32), jax.ShapeDtypeStruct((1, w), F32),
                   jax.ShapeDtypeStruct((nblk, GATE_TILE, GATE_TILE), F32), jax.ShapeDtypeStruct((1, w), F32),
                   jax.ShapeDtypeStruct((1, w), F32)],
        scratch_shapes=[pltpu.VMEM((tt + SUBLANE, w), F32), pltpu.VMEM((tt, w), F32), pltpu.VMEM((tt, w), F32),
                        pltpu.VMEM((tt, w), F32), pltpu.VMEM((SUBLANE, w), F32), pltpu.VMEM((SUBLANE, w), F32)],
        args=(proj, proj, hs, hs, dycat, wconv, bconv, wr, br, wi, bi, lam), name=name, q=q,
        budget_us=HOST_US["lru_bwd"])


def _gate_tiles(w):
    per = GATE_TILE // HEAD_DIM
    w4 = w.reshape(LRU_BLOCKS // per, per, HEAD_DIM, HEAD_DIM)
    eye = jnp.eye(per, dtype=w.dtype)
    return jnp.einsum("bnij,nm->bnimj", w4, eye).reshape(LRU_BLOCKS // per, GATE_TILE, GATE_TILE)


def _gate_blocks(t):
    per = GATE_TILE // HEAD_DIM
    t5 = t.reshape(LRU_BLOCKS // per, per, HEAD_DIM, per, HEAD_DIM)
    eye = jnp.eye(per, dtype=t.dtype)
    return jnp.einsum("bnimj,nm->bnij", t5, eye).reshape(LRU_BLOCKS, HEAD_DIM, HEAD_DIM)


def _row(v):
    return v.reshape(1, -1)


def _local_step(x, mem, target, p, wfull, push_grad, q):
    bsz, t, d = x.shape
    n = bsz * t
    x2d = x.reshape(n, d)
    tgt = target.reshape(n, d)
    mem2d = mem.reshape(bsz * MEM_LEN, d)
    wr_t = [_gate_tiles(p["w_rg_r"][j]).astype(MXU) for j in range(N_A)]
    wi_t = [_gate_tiles(p["w_rg_i"][j]).astype(MXU) for j in range(N_A)]

    mn = [_norm_fwd(mem2d, _row(p["g_mem"][l]), name=f"mem_norm{l}") for l in range(DEPTH)]
    mkv = [None] * DEPTH
    h = _norm_fwd(x2d, _row(p["g_mix_pre"][0]), name="in_norm")
    xin = x2d
    sv = []
    kv = hkv = None
    for l in range(DEPTH):
        s = {"xin": xin, "h": h}
        if q is not None:
            q.horizon = (l + 2) * GROUPS_PER_LAYER
        mkv[l] = _mm_nn(mn[l], wfull("w_mem_kv", l), name=f"mem_kv{l}", q=q)
        if l < N_A:
            proj = _mm_nn(h, wfull("w_in_a", l), name=f"in_proj{l}", q=q)
            ycat, hs = _lru_fwd(proj, p["w_conv_a"][l], _row(p["b_conv_a"][l]), wr_t[l], _row(p["b_rg_r"][l]),
                                wi_t[l], _row(p["b_rg_i"][l]), _row(p["lru_lambda"][l]), bsz, name=f"lru_fwd{l}", q=q)
            s["hs"] = hs
            qblk = 2 * MIX_WIDTH // MEM_WIDTH
        else:
            if l == N_A:
                kv = _mm_nn(hkv, wfull("w_kv", 0), name="kv_proj", q=q)
            proj = _mm_nn(h, wfull("w_in_b", l - N_A), name=f"in_proj{l}", q=q)
            ycat = _swa_fwd(proj, kv, p["sinks_b"][l - N_A], bsz, name=f"swa_fwd{l}", q=q)
            qblk = MIX_WIDTH // MEM_WIDTH
        ycat = _mem_attn_fwd(proj, qblk, mkv[l], ycat, bsz, name=f"mem_attn_fwd{l}", q=q)
        y = _mm_nn(ycat, wfull("w_mix_out", l), name=f"mix_out{l}", q=q, out_dtype=MXU)
        x1, (h2,) = _resid_norm_fwd(xin, y, _row(p["g_mix_post"][l]), [_row(p["g_ffn_pre"][l])], name=f"mix_resid{l}", q=q)
        up = _mm_nn_slots(h2, wfull("w_ffn_up", l), name=f"ffn_up{l}", q=q, out_dtype=MXU)
        act, ug, uv = _ffn_act_fwd(up, p["w_ffn_conv"][l], _row(p["b_ffn_conv"][l]), bsz, name=f"ffn_act{l}", q=q)
        f = _mm_nn(act, wfull("w_ffn_down", l), name=f"ffn_down{l}", q=q, out_dtype=MXU)
        s.update(proj=proj, qblk=qblk, ycat=ycat, y=y, x1=x1, h2=h2, up=up, ug=ug, uv=uv, act=act, f=f)
        sv.append(s)
        if l < DEPTH - 1:
            g_pres = [_row(p["g_mix_pre"][l + 1])] + ([_row(p["g_kv"])] if l + 1 == N_A else [])
            xin, hn = _resid_norm_fwd(x1, f, _row(p["g_ffn_post"][l]), g_pres, name=f"ffn_resid{l}", q=q)
            h = hn[0]
            if l + 1 == N_A:
                hkv = hn[1]
        else:
            g_tot, sq = _loss_fwd(x1, f, _row(p["g_ffn_post"][l]), tgt, name="loss")

    if q is not None:
        q.horizon = LAST_GROUP
    gs = {k: [None] * DEPTH for k in ("g_mix_pre", "g_mix_post", "g_ffn_pre", "g_ffn_post", "g_mem",
                                       "w_ffn_conv", "b_ffn_conv")}
    ga = {k: [None] * N_A for k in ("w_conv_a", "b_conv_a", "w_rg_r", "b_rg_r", "w_rg_i", "b_rg_i", "lru_lambda")}
    gsink = [None] * (DEPTH - N_A)
    dkv_cur, dkv_prev = [], []
    g_tot, df, _, gs["g_ffn_post"][DEPTH - 1] = _resid_norm_bwd(
        g_tot, [], None, [], sv[-1]["f"], _row(p["g_ffn_post"][DEPTH - 1]), name="loss_bwd")
    grad_x = None
    for l in reversed(range(DEPTH)):
        s = sv[l]
        dact = _mm_nt(df, wfull("w_ffn_down", l), name=f"d_act{l}", q=q, out_dtype=MXU)
        push_grad("w_ffn_down", l, _mm_tn(s["act"], df, name=f"dw_down{l}", q=q))
        dug, duv, gs["w_ffn_conv"][l], gs["b_ffn_conv"][l] = _ffn_act_bwd(
            s["up"], s["ug"], s["uv"], dact, p["w_ffn_conv"][l], bsz, name=f"ffn_act_bwd{l}", q=q)
        dh2 = _mm_ffn_dh(dug, duv, wfull("w_ffn_up", l), name=f"d_h2_{l}", q=q)
        up_slots = dict(slot_cols=2 * D_FF // N_CHIP, n_slots=N_CHIP)
        dwu = _mm_tn_slots(s["h2"], dug, name=f"dw_up_g{l}", q=q, **up_slots)
        push_grad("w_ffn_up", l, _mm_tn_slots(s["h2"], duv, name=f"dw_up_v{l}", q=q, out=dwu,
                                              first_slot=N_CHIP // 2, **up_slots))
        g1, dy, (gs["g_ffn_pre"][l],), gs["g_mix_post"][l] = _resid_norm_bwd(
            g_tot, [dh2], s["x1"], [_row(p["g_ffn_pre"][l])], s["y"], _row(p["g_mix_post"][l]), name=f"mix_resid_bwd{l}", q=q)
        dycat = _mm_nt(dy, wfull("w_mix_out", l), name=f"d_ycat{l}", q=q, out_dtype=MXU)
        push_grad("w_mix_out", l, _mm_tn(s["ycat"], dy, name=f"dw_mix_out{l}", q=q))
        if l < N_A:
            dproj, dwc, dbc, dwr, dbr, dwi, dbi, dlam = _lru_bwd(
                s["proj"], s["hs"], dycat, p["w_conv_a"][l], _row(p["b_conv_a"][l]), wr_t[l], _row(p["b_rg_r"][l]),
                wi_t[l], _row(p["b_rg_i"][l]), _row(p["lru_lambda"][l]), bsz, name=f"lru_bwd{l}", q=q)
            ga["w_conv_a"][l], ga["b_conv_a"][l], ga["lru_lambda"][l] = dwc, dbc[0], dlam[0]
            ga["w_rg_r"][l], ga["w_rg_i"][l] = _gate_blocks(dwr), _gate_blocks(dwi)
            ga["b_rg_r"][l] = dbr.reshape(LRU_BLOCKS, HEAD_DIM)
            ga["b_rg_i"][l] = dbi.reshape(LRU_BLOCKS, HEAD_DIM)
            w_in, j = "w_in_a", l
        else:
            dproj, dc, dp_, dsk = _swa_bwd(s["proj"], kv, p["sinks_b"][l - N_A], dycat, bsz, name=f"swa_bwd{l}", q=q)
            dkv_cur.append(dc)
            dkv_prev.append(dp_)
            gsink[l - N_A] = dsk[0, :SWA_HEADS]
            w_in, j = "w_in_b", l - N_A
        dproj, dmkv = _mem_attn_bwd(s["proj"], s["qblk"], mkv[l], dycat, dproj, bsz, name=f"mem_attn_bwd{l}", q=q)
        dh = _mm_nt(dproj, wfull(w_in, j), name=f"d_h{l}", q=q, out_dtype=MXU)
        push_grad(w_in, j, _mm_tn(s["h"], dproj, name=f"dw_in{l}", q=q))
        dmkv = dmkv.astype(MXU)
        dmn = _mm_nt(dmkv, wfull("w_mem_kv", l), name=f"d_mem_norm{l}", q=q)
        push_grad("w_mem_kv", l, _mm_tn(mn[l], dmkv, name=f"dw_mem_kv{l}", q=q))
        gs["g_mem"][l] = _norm_bwd_dg(dmn, mem2d, _row(p["g_mem"][l]), name=f"mem_norm_bwd{l}")
        dhs, g_pres = [dh], [_row(p["g_mix_pre"][l])]
        if l == N_A:
            dkv = _swa_dkv_combine(dkv_cur, dkv_prev, bsz, name="dkv_combine")
            dhs.append(_mm_nt(dkv, wfull("w_kv", 0), name="d_hkv", q=q, out_dtype=MXU))
            g_pres.append(_row(p["g_kv"]))
            push_grad("w_kv", 0, _mm_tn(hkv, dkv, name="dw_kv", q=q))
        if l > 0:
            g_tot, df, dgpre, gs["g_ffn_post"][l - 1] = _resid_norm_bwd(
                g1, dhs, s["xin"], g_pres, sv[l - 1]["f"], _row(p["g_ffn_post"][l - 1]), name=f"ffn_resid_bwd{l - 1}", q=q)
        else:
            grad_x, _, dgpre, _ = _resid_norm_bwd(g1, dhs, s["xin"], g_pres, None, None, name="in_norm_bwd", q=q)
        gs["g_mix_pre"][l] = dgpre[0]
        if l == N_A:
            g_kv = dgpre[1][0]

    grads = {}
    for k in ("g_mix_pre", "g_mix_post", "g_ffn_pre", "g_ffn_post", "g_mem", "b_ffn_conv"):
        grads[k] = jnp.concatenate(gs[k], axis=0)
    grads["w_ffn_conv"] = jnp.stack(gs["w_ffn_conv"])
    for k, v in ga.items():
        grads[k] = jnp.stack(v)
    grads["sinks_b"] = jnp.stack(gsink)
    grads["g_kv"] = g_kv
    return jnp.sum(sq), grad_x.reshape(bsz, t, d), grads


N_CHIP = 4
HALF_ALIGN = 16
MIN_PART_BYTES = 128 * 1024


def _full_shape(kind, shard_shape):
    l, r, c = shard_shape
    return {"row": (l, N_CHIP * r, c), "col": (l, r, N_CHIP * c), "slot": (N_CHIP, l, r, c)}[kind]


def _slot_view(ref, kind, shard_shape, s, hf, sub=(0, 1)):
    _, r, c = shard_shape
    rh = r // 2
    if hf is None:
        start, size = 0, r
    else:
        size = rh // sub[1]
        start = hf * rh + sub[0] * size
    if kind == "row":
        start = s * r + start
    if not isinstance(start, int):
        start = pl.multiple_of(start, HALF_ALIGN)
    rows = pl.ds(start, size)
    if kind == "row":
        return ref.at[:, rows, :]
    if kind == "col":
        return ref.at[:, rows, pl.ds(s * c, c)]
    return ref.at[s, :, rows, :]


def _half_view(ref, shard_shape, hf, sub=(0, 1)):
    rh = shard_shape[1] // 2
    size = rh // sub[1]
    return ref.at[:, pl.ds(pl.multiple_of(hf * rh + sub[0] * size, HALF_ALIGN), size), :]


def _with_slot(kind, s, fn):
    if kind != "col" or isinstance(s, int):
        fn(s)
        return
    for k in range(N_CHIP):
        @pl.when(s == k)
        def _(k=k):
            fn(k)


def _mesh_pos():
    return lax.axis_index("x"), lax.axis_index("y"), lax.axis_index("c")


def _other_chips(x, y):
    return [(1 - x, y), (x, 1 - y), (1 - x, 1 - y)]


ICI_BYTES_PER_US = 6.0e4
ICI_GATHER_BYTES_PER_US = 5.5e4
D2D_BYTES_PER_US = 4.0e5


class _Chunk:
    def __init__(self, group, cost, ins, out_shapes, alias, n_sem, start, finish, done, buffer=None, bind=None):
        self.group, self.cost, self.ins, self.out_shapes, self.alias, self.n_sem = group, cost, ins, out_shapes, alias, n_sem
        self.start, self.finish, self.done = start, finish, done
        self.buffer = buffer
        self.bind = bind

    def prepare(self):
        if self.bind is not None:
            self.bind(self)


def _merged(chunks):
    groups, by_buffer = [], {}
    for ch in chunks:
        key = None if ch.buffer is None else (id(ch.buffer[0]), ch.buffer[1])
        if key is not None and key in by_buffer:
            by_buffer[key].append(ch)
        else:
            groups.append([ch])
            if key is not None:
                by_buffer[key] = groups[-1]
    out = []
    for parts in groups:
        if len(parts) == 1:
            out.append(parts[0])
            continue
        offs = [sum(p.n_sem for p in parts[:i]) for i in range(len(parts))]

        def run(phase, ins, outs, ss, rs, b, parts=parts, offs=offs):
            for p, o in zip(parts, offs):
                getattr(p, phase)(ins, outs, ss, rs, b + o)

        def done(outs, parts=parts):
            for p in parts:
                p.done(outs)

        first = parts[0]
        out.append(_Chunk(first.group, sum(p.cost for p in parts), first.ins, first.out_shapes, first.alias,
                          sum(p.n_sem for p in parts), functools.partial(run, "start"),
                          functools.partial(run, "finish"), done))
    return out


LAST_GROUP = 1 << 30


class _CommQueue:
    def __init__(self):
        self.pending = []
        self.flushes = 0
        self.horizon = LAST_GROUP

    def push(self, chunk):
        self.pending.append(chunk)

    def take(self, budget_us):
        got, used = [], 0.0
        for ch in sorted(self.pending, key=lambda ch: (ch.group, -ch.cost)):
            if ch.group >= self.horizon and ch.group != LAST_GROUP:
                continue
            if used + ch.cost <= budget_us and not self._shares_buffer(ch, got):
                got.append(ch)
                used += ch.cost
        return self._taken(got)

    @staticmethod
    def _shares_buffer(ch, others):
        return ch.buffer is not None and any(
            o.buffer is not None and o.buffer[0] is ch.buffer[0] and o.buffer[1] != ch.buffer[1] for o in others)

    def _taken(self, got):
        self.pending = [ch for ch in self.pending if ch not in got]
        for ch in got:
            ch.prepare()
        return _merged(got)

    def flush(self, group=LAST_GROUP):
        while True:
            chunks = []
            for ch in self.pending:
                if ch.group <= group and not self._shares_buffer(ch, chunks):
                    chunks.append(ch)
            if not chunks:
                return
            _run_chunks(self._taken(chunks), name=f"comm_flush{self.flushes}")
            self.flushes += 1


def _run_chunks(chunks, *, name):
    ins = [a for ch in chunks for a in ch.ins]
    outs = [s for ch in chunks for s in ch.out_shapes]
    alias, offs = {}, []
    i0 = o0 = s0 = 0
    for ch in chunks:
        offs.append((i0, o0, s0))
        for ci, co in ch.alias.items():
            alias[i0 + ci] = o0 + co
        i0 += len(ch.ins)
        o0 += len(ch.out_shapes)
        s0 += ch.n_sem

    def body(*refs):
        send_sems, recv_sems = refs[i0 + o0:]
        for phase in ("start", "finish"):
            for ch, (a, b, s) in zip(chunks, offs):
                getattr(ch, phase)(refs[a:a + len(ch.ins)], refs[i0 + b:i0 + b + len(ch.out_shapes)],
                                   send_sems, recv_sems, s)

    hbm = pl.BlockSpec(memory_space=pl.ANY)
    res = pl.pallas_call(
        body, in_specs=[hbm] * i0, out_specs=[hbm] * o0, out_shape=outs,
        scratch_shapes=[pltpu.SemaphoreType.DMA((s0,)), pltpu.SemaphoreType.DMA((s0,))],
        input_output_aliases=alias, name=name, compiler_params=pltpu.CompilerParams(has_side_effects=True))(*ins)
    for ch, (_, b, _) in zip(chunks, offs):
        ch.done(list(res[b:b + len(ch.out_shapes)]))


def _remote(src, dst, send_sems, recv_sems, k, dev):
    return pltpu.make_async_remote_copy(src_ref=src, dst_ref=dst, send_sem=send_sems.at[k], recv_sem=recv_sems.at[k],
                                        device_id=dev, device_id_type=MESH_T)


def _gather_chunks(q, group, kind, shard, l, ready):
    _, r, c = shard.shape
    shp = (1, r, c)
    rh = r // 2
    parts = max(p for p in (8, 4, 2, 1)
                if (rh // p) % HALF_ALIGN == 0 and (p == 1 or (rh // p) * c * shard.dtype.itemsize >= MIN_PART_BYTES))
    part_bytes = (rh // parts) * c * shard.dtype.itemsize
    full_type = jax.ShapeDtypeStruct(_full_shape(kind, shp), shard.dtype)
    state = {"full": None, "parts_done": 0}

    def bind_first(ch):
        ch.ins, ch.alias = ([shard], {}) if state["full"] is None else ([shard, state["full"]], {1: 0})

    def bind_full(ch):
        ch.ins = [state["full"]]

    def make_part(p):
        sub = (p, parts)

        def any_part(full):
            return _slot_view(full, kind, shp, 0, 0, sub)

        def start1(ins, outs, ss, rs, b):
            x, y, c_ = _mesh_pos()
            src, full = ins[0].at[pl.ds(l, 1)], outs[0]
            if p == 0:
                _with_slot(kind, 2 * x + y, lambda s: pltpu.make_async_copy(
                    src, _slot_view(full, kind, shp, s, None), ss.at[b + N_CHIP - 1]).start())
            for j, (ox, oy) in enumerate(_other_chips(x, y)):
                _with_slot(kind, 2 * x + y, lambda s, j=j, ox=ox, oy=oy: _remote(
                    _half_view(src, shp, c_, sub), _slot_view(full, kind, shp, s, c_, sub), ss, rs, b + j,
                    (ox, oy, c_)).start())

        def finish1(ins, outs, ss, rs, b):
            x, y, c_ = _mesh_pos()
            h = any_part(outs[0])
            for j in range(N_CHIP - 1):
                _remote(h, h, ss, rs, b + j, (x, y, 1 - c_)).wait()
            if p == 0:
                pltpu.make_async_copy(ins[0].at[pl.ds(l, 1)], _slot_view(outs[0], kind, shp, 0, None),
                                      ss.at[b + N_CHIP - 1]).wait()

        def start2(ins, outs, ss, rs, b):
            x, y, c_ = _mesh_pos()
            for j, (ox, oy) in enumerate(_other_chips(x, y)):
                def forward(s, j=j):
                    v = _slot_view(outs[0], kind, shp, s, c_, sub)
                    _remote(v, v, ss, rs, b + j, (x, y, 1 - c_)).start()
                _with_slot(kind, 2 * ox + oy, forward)

        def finish2(ins, outs, ss, rs, b):
            x, y, c_ = _mesh_pos()
            h = any_part(outs[0])
            for j in range(N_CHIP - 1):
                _remote(h, h, ss, rs, b + j, (x, y, 1 - c_)).wait()

        def done2(outs):
            state["full"] = outs[0]
            state["parts_done"] += 1
            if state["parts_done"] == parts:
                ready(outs[0])

        def done1(outs):
            state["full"] = outs[0]
            q.push(_Chunk(group, 3 * part_bytes / D2D_BYTES_PER_US, None, [full_type], {0: 0}, N_CHIP - 1,
                          start2, finish2, done2, buffer=(state, 2), bind=bind_full))

        return _Chunk(group, 3 * part_bytes / ICI_GATHER_BYTES_PER_US, None, [full_type], None,
                      N_CHIP if p == 0 else N_CHIP - 1, start1, finish1, done1, buffer=(state, 1), bind=bind_first)

    for p in range(parts):
        q.push(make_part(p))


def _reduce_scatter_chunks(q, kind, grad, shard_shape, pos, name, ready):
    _, r, c = shard_shape
    shp = (1, r, c)
    rh = r // 2

    def start1(ins, outs, ss, rs, b):
        x, y, c_ = _mesh_pos()
        for s in range(N_CHIP):
            _remote(_slot_view(ins[0], kind, shp, s, 1 - c_), outs[0].at[s], ss, rs, b + s, (x, y, 1 - c_)).start()

    def finish1(ins, outs, ss, rs, b):
        x, y, c_ = _mesh_pos()
        for s in range(N_CHIP):
            _remote(outs[0].at[s], outs[0].at[s], ss, rs, b + s, (x, y, 1 - c_)).wait()

    def start2(ins, outs, ss, rs, b):
        x, y, c_ = _mesh_pos()
        for j, (ox, oy) in enumerate(_other_chips(x, y)):
            _remote(ins[0].at[2 * ox + oy], outs[0].at[j], ss, rs, b + j, (ox, oy, c_)).start()

    def finish2(ins, outs, ss, rs, b):
        x, y, c_ = _mesh_pos()
        for j in range(N_CHIP - 1):
            _remote(outs[0].at[j], outs[0].at[j], ss, rs, b + j, (x, y, 1 - c_)).wait()

    def start3(ins, outs, ss, rs, b):
        x, y, c_ = _mesh_pos()
        v = _half_view(outs[0], shp, c_)
        _remote(v, v, ss, rs, b, (x, y, 1 - c_)).start()

    def finish3(ins, outs, ss, rs, b):
        x, y, c_ = _mesh_pos()
        v = _half_view(outs[0], shp, c_)
        _remote(v, v, ss, rs, b, (x, y, 1 - c_)).wait()

    def done2(pair, outs):
        half = _rs_chip_add(pair, outs[0], shp, pos, name=f"rs_chip_add_{name}")
        q.push(_Chunk(LAST_GROUP, rh * c * 4 / D2D_BYTES_PER_US, [half], [jax.ShapeDtypeStruct(half.shape, half.dtype)],
                      {0: 0}, 1, start3, finish3, lambda o: ready(o[0])))

    def done1(outs):
        pair, wire = _rs_pair_add(grad, outs[0], kind, shp, pos, name=f"rs_pair_add_{name}")
        q.push(_Chunk(LAST_GROUP, 3 * rh * c * wire.dtype.itemsize / ICI_BYTES_PER_US, [wire],
                      [jax.ShapeDtypeStruct((N_CHIP - 1, 1, rh, c), wire.dtype)], {}, N_CHIP - 1,
                      start2, finish2, functools.partial(done2, pair)))

    q.push(_Chunk(LAST_GROUP, N_CHIP * rh * c * 4 / D2D_BYTES_PER_US, [grad],
                  [jax.ShapeDtypeStruct((N_CHIP, 1, rh, c), F32)], {}, N_CHIP, start1, finish1, done1))


def _allgather8(vec, *, name):
    r = vec.shape[0]
    n_dev = 8

    def body(v_ref, buf, send_sems, recv_sems):
        x, y, c = _mesh_pos()
        me = 4 * x + 2 * y + c
        copies = []
        for k in range(1, n_dev):
            kx, ky, kc = (k >> 2) & 1, (k >> 1) & 1, k & 1
            peer = ((1 - x) if kx else x, (1 - y) if ky else y, (1 - c) if kc else c)
            cp = _remote(v_ref, buf.at[me], send_sems, recv_sems, k - 1, peer)
            cp.start()
            copies.append(cp)
        buf[me] = v_ref[...]
        for cp in copies:
            cp.wait()

    vm = pl.BlockSpec(memory_space=pltpu.VMEM)
    return pl.pallas_call(
        body, in_specs=[vm], out_specs=vm, out_shape=jax.ShapeDtypeStruct((n_dev, r, LANE), F32),
        scratch_shapes=[pltpu.SemaphoreType.DMA((n_dev - 1,)), pltpu.SemaphoreType.DMA((n_dev - 1,))],
        name=name, compiler_params=pltpu.CompilerParams(has_side_effects=True, vmem_limit_bytes=VMEM_LIMIT_V7X))(vec)


def _allreduce8(vec, *, name):
    r = vec.shape[0]
    rh = r // 2

    def body(v_ref, o_ref, sib_ref, chips_ref, send_sems, recv_sems):
        x, y, c = _mesh_pos()
        sib = (x, y, 1 - c)
        me = 2 * x + y
        pair = _remote(v_ref, sib_ref, send_sems, recv_sems, 0, sib)
        pair.start()
        pair.wait()
        rows = pl.ds(pl.multiple_of(c * rh, SUBLANE), rh)
        chips_ref[me] = v_ref[rows, :] + sib_ref[rows, :]
        copies = []
        for j, (ox, oy) in enumerate(_other_chips(x, y)):
            cp = _remote(chips_ref.at[me], chips_ref.at[me], send_sems, recv_sems, 1 + j, (ox, oy, c))
            cp.start()
            copies.append(cp)
        for cp in copies:
            cp.wait()
        acc = chips_ref[0]
        for s in range(1, N_CHIP):
            acc = acc + chips_ref[s]
        o_ref[rows, :] = acc
        swap = _remote(o_ref.at[rows, :], o_ref.at[rows, :], send_sems, recv_sems, N_CHIP, sib)
        swap.start()
        swap.wait()

    vm = pl.BlockSpec(memory_space=pltpu.VMEM)
    return pl.pallas_call(
        body, in_specs=[vm], out_specs=vm, out_shape=jax.ShapeDtypeStruct((r, LANE), F32),
        scratch_shapes=[pltpu.VMEM((r, LANE), F32), pltpu.VMEM((N_CHIP, rh, LANE), F32),
                        pltpu.SemaphoreType.DMA((N_CHIP + 1,)), pltpu.SemaphoreType.DMA((N_CHIP + 1,))],
        name=name, compiler_params=pltpu.CompilerParams(has_side_effects=True, vmem_limit_bytes=VMEM_LIMIT_V7X))(vec)


def _rs_pair_add(g, recv, kind, shape, pos, *, name):
    l, r, c = shape
    rh = r // 2
    if kind == "row":
        gspec = pl.BlockSpec((None, rh, c), lambda s, i, pos: (i, 2 * s + pos[0], 0))
    elif kind == "col":
        gspec = pl.BlockSpec((None, rh, c), lambda s, i, pos: (i, pos[0], s))
    else:
        gspec = pl.BlockSpec((None, None, rh, c), lambda s, i, pos: (s, i, pos[0], 0))
    pspec = pl.BlockSpec((None, None, rh, c), lambda s, i, pos: (s, i, 0, 0))

    def body(pos_ref, g_ref, r_ref, p_ref, pw_ref):
        del pos_ref
        v = g_ref[...] + r_ref[...]
        p_ref[...] = v
        pw_ref[...] = v.astype(pw_ref.dtype)

    return pl.pallas_call(
        body,
        grid_spec=pltpu.PrefetchScalarGridSpec(
            num_scalar_prefetch=1, grid=(N_CHIP, l), in_specs=[gspec, pspec], out_specs=[pspec, pspec]),
        out_shape=[jax.ShapeDtypeStruct((N_CHIP, l, rh, c), F32), jax.ShapeDtypeStruct((N_CHIP, l, rh, c), MXU)],
        name=name, compiler_params=_cp((PAR, PAR)))(pos, g, recv)


def _rs_chip_add(p, recv, shape, pos, *, name):
    l, r, c = shape
    rh = r // 2

    def body(pos_ref, p_ref, r_ref, o_ref):
        del pos_ref
        acc = p_ref[...]
        for j in range(N_CHIP - 1):
            acc = acc + r_ref[j].astype(F32)
        o_ref[...] = acc

    return pl.pallas_call(
        body,
        grid_spec=pltpu.PrefetchScalarGridSpec(
            num_scalar_prefetch=1, grid=(l,),
            in_specs=[pl.BlockSpec((None, None, rh, c), lambda i, pos: (pos[1], i, 0, 0)),
                      pl.BlockSpec((N_CHIP - 1, None, rh, c), lambda i, pos: (0, i, 0, 0))],
            out_specs=pl.BlockSpec((None, rh, c), lambda i, pos: (i, pos[0], 0))),
        out_shape=jax.ShapeDtypeStruct((l, r, c), F32),
        name=name, compiler_params=_cp((PAR,)))(pos, p, recv)


ADAM_BLOCK_ELEMS = 384 * 1024


def _adam_math(w, g, m, v):
    c1 = 1.0 / (1.0 - ADAM_B1 ** ADAM_STEP)
    c2 = 1.0 / (1.0 - ADAM_B2 ** ADAM_STEP)
    nm = ADAM_B1 * m + (1.0 - ADAM_B1) * g
    nv = ADAM_B2 * v + (1.0 - ADAM_B2) * (g * g)
    return -ADAM_LR * ((nm * c1) / (jnp.sqrt(nv * c2) + ADAM_EPS) + ADAM_WD * w), nm, nv


def _adamw_layer(w, g, m, v, outs, l, *, name):
    _, r, c = w.shape
    tr = _tile(r, max(SUBLANE, ADAM_BLOCK_ELEMS // c // SUBLANE * SUBLANE), SUBLANE)

    def body(w_ref, g_ref, m_ref, v_ref, *rest):
        go_ref, d_ref, nm_ref, nv_ref = rest[4:]
        gg = g_ref[...]
        go_ref[...] = gg
        d_ref[...], nm_ref[...], nv_ref[...] = _adam_math(w_ref[...], gg, m_ref[...], v_ref[...])

    lay = pl.BlockSpec((None, tr, c), lambda j: (l, j, 0))
    hbm = pl.BlockSpec(memory_space=pl.ANY)
    return pl.pallas_call(
        body, grid=(r // tr,),
        in_specs=[lay, pl.BlockSpec((None, tr, c), lambda j: (0, j, 0)), lay, lay] + [hbm] * 4,
        out_specs=[lay] * 4, out_shape=[jax.ShapeDtypeStruct(w.shape, F32)] * 4,
        input_output_aliases={4 + i: i for i in range(4)},
        name=name, compiler_params=_cp((PAR,)))(w, g, m, v, *outs)


def _adamw(w, g, m, v, *, name):
    shape = w.shape
    if w.ndim == 2:
        w, g, m, v = (a[None] for a in (w, g, m, v))
    l, r, c = w.shape
    tr = _tile(r, max(SUBLANE, ADAM_BLOCK_ELEMS // c // SUBLANE * SUBLANE), SUBLANE)

    def body(w_ref, g_ref, m_ref, v_ref, d_ref, nm_ref, nv_ref):
        d_ref[...], nm_ref[...], nv_ref[...] = _adam_math(w_ref[...], g_ref[...], m_ref[...], v_ref[...])

    spec = pl.BlockSpec((None, tr, c), lambda i, j: (i, j, 0))
    outs = pl.pallas_call(
        body, grid=(l, r // tr), in_specs=[spec] * 4, out_specs=[spec] * 3,
        out_shape=[jax.ShapeDtypeStruct((l, r, c), F32)] * 3,
        name=name, compiler_params=_cp((PAR, PAR)))(w, g, m, v)
    return tuple(o.reshape(shape) for o in outs)


PACK_ROWS = 512 * LANE


def _pack(arrays):
    flat = jnp.concatenate([a.reshape(-1).astype(F32) for a in arrays])
    pad = (-flat.shape[0]) % PACK_ROWS
    return jnp.pad(flat, (0, pad)).reshape(-1, LANE)


def _unpack(packed, shapes):
    flat = packed.reshape(-1)
    out, off = [], 0
    for s in shapes:
        size = int(np.prod(s))
        out.append(flat[off:off + size].reshape(s))
        off += size
    return out


BIG = (("w_mem_kv", "row"), ("w_mix_out", "row"), ("w_ffn_up", "slot"), ("w_ffn_down", "row"),
       ("w_in_a", "slot"), ("w_in_b", "row"), ("w_kv", "row"))
COLUMN_SHARDED_AS_COLUMNS = ("w_in_a",)
SMALL_SHARDED = (("w_ffn_conv", 2), ("w_conv_a", 2), ("b_conv_a", 1), ("lru_lambda", 1))
SMALL_REPLICATED = ("g_mix_pre", "g_mix_post", "g_ffn_pre", "g_ffn_post", "g_mem", "b_ffn_conv",
                    "w_rg_r", "b_rg_r", "w_rg_i", "b_rg_i", "sinks_b", "g_kv")
WEIGHTS = ("g_mix_pre", "g_mix_post", "g_ffn_pre", "g_ffn_post", "g_mem", "w_mem_kv", "w_mix_out", "w_ffn_up",
           "w_ffn_conv", "b_ffn_conv", "w_ffn_down", "w_in_a", "w_conv_a", "b_conv_a", "w_rg_r", "b_rg_r", "w_rg_i",
           "b_rg_i", "lru_lambda", "w_in_b", "sinks_b", "g_kv", "w_kv")


def _slot_to_cols(a):
    s, l, r, c = a.shape
    return a.transpose(1, 2, 0, 3).reshape(l, r, s * c)


def _cols_to_slot(a):
    l, r, c4 = a.shape
    return a.reshape(l, r, N_CHIP, c4 // N_CHIP).transpose(2, 0, 1, 3)


GROUPS_PER_LAYER = 8


def _layer_weights(layer):
    names = [("w_mem_kv", layer), ("w_in_a", layer) if layer < N_A else ("w_in_b", layer - N_A)]
    if layer == N_A:
        names.append(("w_kv", 0))
    return names + [("w_mix_out", layer), ("w_ffn_up", layer), ("w_ffn_down", layer)]


def _train_step(x, mem, target, w, m, v):
    xi, yi, ci = _mesh_pos()
    chip = 2 * xi + yi
    pos = jnp.stack([ci, chip]).astype(jnp.int32)

    q = _CommQueue()
    kinds = dict(BIG)
    as3 = lambda a: a if a.ndim == 3 else a[None]
    w3, m3, v3 = ({k: as3(d[k]) for k, _ in BIG} for d in (w, m, v))
    shards = {k: w3[k].astype(MXU) for k, _ in BIG}

    gathered = {}

    def on_gathered(k, l, full):
        gathered[k, l] = _slot_to_cols(full) if k in COLUMN_SHARDED_AS_COLUMNS else full

    group_of = {}

    for layer in range(DEPTH):
        for i, (k, l) in enumerate(_layer_weights(layer)):
            group_of[k, l] = layer * GROUPS_PER_LAYER + i
            _gather_chunks(q, group_of[k, l], kinds[k], shards[k], l, functools.partial(on_gathered, k, l))

    def wfull(k, l):
        if (k, l) not in gathered:
            q.flush(group_of[k, l])
        return gathered[k, l]

    q.flush(1)

    big_out = {k: [lax.empty(w3[k].shape, F32) for _ in range(4)] for k, _ in BIG}

    def on_reduced(k, l, g):
        big_out[k] = _adamw_layer(w3[k], g, m3[k], v3[k], big_out[k], l, name=f"adamw_{k}{l}")

    def push_grad(k, l, g):
        if k in COLUMN_SHARDED_AS_COLUMNS:
            g = _cols_to_slot(g)
        _reduce_scatter_chunks(q, kinds[k], g, (1,) + w3[k].shape[1:], pos, f"{k}{l}", functools.partial(on_reduced, k, l))

    small_shapes = [w[k].shape for k, _ in SMALL_SHARDED]
    stacked = _allgather8(_pack([w[k] for k, _ in SMALL_SHARDED]), name="gather_small")
    per_chip = [_unpack(stacked[2 * s], small_shapes) for s in range(N_CHIP)]
    p = {k: w[k] for k in SMALL_REPLICATED}
    for i, (k, axis) in enumerate(SMALL_SHARDED):
        p[k] = jnp.concatenate([per_chip[s][i] for s in range(N_CHIP)], axis=axis)

    sq, grad_x, g = _local_step(x, mem, target, p, wfull, push_grad, q)
    loss = lax.psum(0.5 * sq / D_MODEL, ("x", "y", "c"))
    q.flush()

    small_names = [k for k, _ in SMALL_SHARDED] + list(SMALL_REPLICATED)
    summed = _allreduce8(_pack([g[k] for k in small_names]), name="allreduce_small")
    gsum = dict(zip(small_names, _unpack(summed, [p[k].shape for k in small_names])))
    for k, axis in SMALL_SHARDED:
        gsum[k] = lax.dynamic_slice_in_dim(gsum[k], chip * w[k].shape[axis], w[k].shape[axis], axis)

    delta, new_m, new_v = {}, {}, {}
    for k, _ in BIG:
        gsum[k], delta[k], new_m[k], new_v[k] = (o.reshape(w[k].shape) for o in big_out[k])
    packed = [_pack([d[k] for k in small_names]) for d in (w, gsum, m, v)]
    outs = _adamw(*packed, name="adamw_small")
    for d, o in zip((delta, new_m, new_v), outs):
        d.update(zip(small_names, _unpack(o, [w[k].shape for k in small_names])))
    return (loss, grad_x, *[gsum[k] for k in WEIGHTS], *[delta[k] for k in WEIGHTS],
            *[new_m[k] for k in WEIGHTS], *[new_v[k] for k in WEIGHTS])


def kernel(x, mem, g_mix_pre, g_mix_post, g_ffn_pre, g_ffn_post, g_mem, w_mem_kv, w_mix_out, w_ffn_up, w_ffn_conv, b_ffn_conv, w_ffn_down, w_in_a, w_conv_a, b_conv_a, w_rg_r, b_rg_r, w_rg_i, b_rg_i, lru_lambda, w_in_b, sinks_b, g_kv, w_kv, loss_target, m_g_mix_pre, m_g_mix_post, m_g_ffn_pre, m_g_ffn_post, m_g_mem, m_w_mem_kv, m_w_mix_out, m_w_ffn_up, m_w_ffn_conv, m_b_ffn_conv, m_w_ffn_down, m_w_in_a, m_w_conv_a, m_b_conv_a, m_w_rg_r, m_b_rg_r, m_w_rg_i, m_b_rg_i, m_lru_lambda, m_w_in_b, m_sinks_b, m_g_kv, m_w_kv, v_g_mix_pre, v_g_mix_post, v_g_ffn_pre, v_g_ffn_post, v_g_mem, v_w_mem_kv, v_w_mix_out, v_w_ffn_up, v_w_ffn_conv, v_b_ffn_conv, v_w_ffn_down, v_w_in_a, v_w_conv_a, v_b_conv_a, v_w_rg_r, v_b_rg_r, v_w_rg_i, v_b_rg_i, v_lru_lambda, v_w_in_b, v_sinks_b, v_g_kv, v_w_kv):
    args = (g_mix_pre, g_mix_post, g_ffn_pre, g_ffn_post, g_mem, w_mem_kv, w_mix_out, w_ffn_up, w_ffn_conv, b_ffn_conv, w_ffn_down, w_in_a, w_conv_a, b_conv_a, w_rg_r, b_rg_r, w_rg_i, b_rg_i, lru_lambda, w_in_b, sinks_b, g_kv, w_kv)
    ms = (m_g_mix_pre, m_g_mix_post, m_g_ffn_pre, m_g_ffn_post, m_g_mem, m_w_mem_kv, m_w_mix_out, m_w_ffn_up, m_w_ffn_conv, m_b_ffn_conv, m_w_ffn_down, m_w_in_a, m_w_conv_a, m_b_conv_a, m_w_rg_r, m_b_rg_r, m_w_rg_i, m_b_rg_i, m_lru_lambda, m_w_in_b, m_sinks_b, m_g_kv, m_w_kv)
    vs = (v_g_mix_pre, v_g_mix_post, v_g_ffn_pre, v_g_ffn_post, v_g_mem, v_w_mem_kv, v_w_mix_out, v_w_ffn_up, v_w_ffn_conv, v_b_ffn_conv, v_w_ffn_down, v_w_in_a, v_w_conv_a, v_b_conv_a, v_w_rg_r, v_b_rg_r, v_w_rg_i, v_b_rg_i, v_lru_lambda, v_w_in_b, v_sinks_b, v_g_kv, v_w_kv)
    return _train_step(x, mem, loss_target, dict(zip(WEIGHTS, args)), dict(zip(WEIGHTS, ms)), dict(zip(WEIGHTS, vs)))
```

```python
import functools
import math

import numpy as np
import jax
import jax.numpy as jnp
from jax import lax
from jax.experimental import pallas as pl
from jax.experimental.pallas import tpu as pltpu

F32 = jnp.float32
MXU = jnp.bfloat16

D_MODEL = 1024
HEAD_DIM = 64
MEM_LEN = 256
MEM_HEADS = 4
MEM_WIDTH = MEM_HEADS * HEAD_DIM
MIX_WIDTH = D_MODEL - MEM_WIDTH
LRU_BLOCKS = MIX_WIDTH // HEAD_DIM
LRU_CONV = 4
LRU_C = 8.0
SWA_HEADS = MIX_WIDTH // HEAD_DIM
SWA_KV_HEADS = 4
SWA_GROUP = SWA_HEADS // SWA_KV_HEADS
WINDOW = 128
D_FF = 2816
FFN_CONV = 3
EPS = 1e-6
DEPTH = 4
N_A = 2

ADAM_LR = 0.001
ADAM_B1 = 0.9
ADAM_B2 = 0.999
ADAM_EPS = 1e-08
ADAM_WD = 0.01
ADAM_STEP = 10

VMEM_LIMIT_V7X = 56 * 1024 * 1024
LANE = 128
SUBLANE = 8
GATE_TILE = 256
MESH_T = pl.DeviceIdType.MESH


def _alibi_slopes(n):
    def pow2_slopes(m):
        start = 2.0 ** (-8.0 / m)
        return [start ** (i + 1) for i in range(m)]
    c = 2 ** int(math.floor(math.log2(n)))
    s = pow2_slopes(c)
    if c != n:
        s = s + pow2_slopes(2 * c)[0::2][: n - c]
    return [float(np.float32(v)) for v in s]


SLOPES = _alibi_slopes(SWA_HEADS)


def _tile(n, cap, mult=LANE):
    best = None
    for t in range(mult, min(n, cap) + 1, mult):
        if n % t == 0:
            best = t
    return best if best is not None else n


def _cp(sem):
    return pltpu.CompilerParams(dimension_semantics=sem, vmem_limit_bytes=VMEM_LIMIT_V7X)


MM_VMEM_BUDGET = 40 * 1024 * 1024
HBM_BYTES_PER_US_V7X = 3.0e6
GRID_STEP_US = 0.35


def _divisors(n, mult):
    return [t for t in range(mult, n + 1, mult) if n % t == 0] or [n]


def _mm_tiles(m, k, n, out_bytes):
    best = None
    for tm in _divisors(m, 256):
        for tn in _divisors(n, LANE):
            vmem = 2 * (tm * k * 2 + k * tn * 2 + tm * tn * out_bytes)
            if vmem > MM_VMEM_BUDGET:
                continue
            steps = (m // tm) * (n // tn)
            b_reads = 1 if tn == n else m // tm
            traffic = m * k * 2 + k * n * 2 * b_reads + m * n * out_bytes
            first = tm * k * 2 + k * tn * 2
            cost = (traffic + first) / HBM_BYTES_PER_US_V7X + steps * GRID_STEP_US
            if best is None or cost < best[0]:
                best = (cost, tm, tn)
    return best[1], best[2]


def _mm_tn_tiles(k, m, n, whole_n=False):
    best = None
    for tm in _divisors(m, LANE):
        for tn in ([n] if whole_n else _divisors(n, LANE)):
            for tk in _divisors(k, 512):
                vmem = 2 * (tk * tm * 2 + tk * tn * 2 + tm * tn * 4)
                if vmem > MM_VMEM_BUDGET:
                    continue
                steps = (m // tm) * (n // tn) * (k // tk)
                traffic = k * m * 2 * (n // tn) + k * n * 2 * (m // tm) + m * n * 4
                cost = traffic / HBM_BYTES_PER_US_V7X + steps * GRID_STEP_US
                if best is None or cost < best[0]:
                    best = (cost, tk, tm, tn)
    return best[1], best[2], best[3]


ARB = "arbitrary"
PAR = "parallel"


def _rms_fwd(x, g):
    r = lax.rsqrt(jnp.mean(x * x, axis=-1, keepdims=True) + EPS)
    return x * r * g


def _rms_bwd(dy, x, g):
    r = lax.rsqrt(jnp.mean(x * x, axis=-1, keepdims=True) + EPS)
    xh = x * r
    gdy = dy * g
    dx = r * (gdy - xh * jnp.mean(gdy * xh, axis=-1, keepdims=True))
    dg = jnp.sum(dy * xh, axis=0, keepdims=True)
    return dx, dg


_GELU_K = math.sqrt(2.0 / math.pi)
_GELU_C = 0.044715


def _gelu(x):
    t = jnp.tanh(_GELU_K * (x + _GELU_C * x * x * x))
    return 0.5 * x * (1.0 + t)


def _gelu_and_grad(x):
    x2 = x * x
    t = jnp.tanh(_GELU_K * (x + _GELU_C * x2 * x))
    g = 0.5 * x * (1.0 + t)
    dg = 0.5 * (1.0 + t) + 0.5 * x * (1.0 - t * t) * (_GELU_K * (1.0 + 3.0 * _GELU_C * x2))
    return g, dg


def _shift_down(x, k, row):
    return jnp.where(row >= k, pltpu.roll(x, k, axis=0), 0.0)


def _shift_up(x, k, row):
    n = x.shape[0]
    return jnp.where(row < n - k, pltpu.roll(x, n - k, axis=0), 0.0)


def _shift_down_edge(x, k):
    r = pltpu.roll(x, k, axis=0)
    row = lax.broadcasted_iota(jnp.int32, (SUBLANE, x.shape[1]), 0)
    return jnp.concatenate([jnp.where(row >= k, r[:SUBLANE], 0.0), r[SUBLANE:]], axis=0)


def _shift_up_edge(x, k):
    n = x.shape[0]
    r = pltpu.roll(x, n - k, axis=0)
    row = lax.broadcasted_iota(jnp.int32, (SUBLANE, x.shape[1]), 0)
    return jnp.concatenate([r[:n - SUBLANE], jnp.where(row < SUBLANE - k, r[n - SUBLANE:], 0.0)], axis=0)


def _dot(a, b):
    return jnp.dot(a, b, preferred_element_type=F32)


def _dot_nt(a, b):
    return lax.dot_general(a, b, (((1,), (1,)), ((), ())), preferred_element_type=F32)


def _dot_tn(a, b):
    return lax.dot_general(a, b, (((0,), (0,)), ((), ())), preferred_element_type=F32)


MXU_FLOPS_PER_US = 7.0e8
HOST_US = {"lru_fwd": 44.0, "lru_bwd": 94.0, "swa_fwd": 60.0, "swa_bwd": 160.0, "mem_attn_fwd": 21.0,
           "mem_attn_bwd": 33.0, "ffn_act": 70.0, "ffn_act_bwd": 100.0, "resid": 22.0, "resid_bwd": 33.0}


def _hosted_call(body, *, grid, in_specs, out_specs, out_shape, args, name, aliases=None, scratch_shapes=(),
                 q=None, flops=0.0, budget_us=0.0):
    chunks = q.take(flops / MXU_FLOPS_PER_US + budget_us) if q is not None else []
    if not chunks:
        return pl.pallas_call(
            body, grid=grid, in_specs=in_specs, out_specs=out_specs, out_shape=out_shape,
            scratch_shapes=list(scratch_shapes), input_output_aliases=aliases or {}, name=name,
            compiler_params=_cp((ARB,) * len(grid)))(*args)
    single = not isinstance(out_shape, (list, tuple))
    o_shapes = [out_shape] if single else list(out_shape)
    o_specs = [out_specs] if single else list(out_specs)
    n_in, n_out, n_scr = len(args), len(o_shapes), len(scratch_shapes)
    c_ins = [a for ch in chunks for a in ch.ins]
    c_outs = [s for ch in chunks for s in ch.out_shapes]
    alias = dict(aliases or {})
    in_off, out_off, sem_off = [], [], []
    i0 = o0 = s0 = 0
    for ch in chunks:
        in_off.append(i0)
        out_off.append(o0)
        sem_off.append(s0)
        for ci, co in ch.alias.items():
            alias[n_in + i0 + ci] = n_out + o0 + co
        i0 += len(ch.ins)
        o0 += len(ch.out_shapes)
        s0 += ch.n_sem

    def wrapped(*refs):
        ins = refs[:n_in]
        cin = refs[n_in:n_in + i0]
        outs = refs[n_in + i0:n_in + i0 + n_out]
        cout = refs[n_in + i0 + n_out:n_in + i0 + n_out + o0]
        scr = refs[n_in + i0 + n_out + o0:n_in + i0 + n_out + o0 + n_scr]
        send_sems, recv_sems = refs[n_in + i0 + n_out + o0 + n_scr:]
        first = functools.reduce(lambda u, v: u & v, [pl.program_id(d) == 0 for d in range(len(grid))])
        last = functools.reduce(lambda u, v: u & v, [pl.program_id(d) == grid[d] - 1 for d in range(len(grid))])

        def each(phase):
            for ch, a, b, s in zip(chunks, in_off, out_off, sem_off):
                getattr(ch, phase)(cin[a:a + len(ch.ins)], cout[b:b + len(ch.out_shapes)], send_sems, recv_sems, s)

        pl.when(first)(lambda: each("start"))
        body(*ins, *outs, *scr)
        pl.when(last)(lambda: each("finish"))

    hbm = pl.BlockSpec(memory_space=pl.ANY)
    res = pl.pallas_call(
        wrapped, grid=grid, in_specs=list(in_specs) + [hbm] * i0, out_specs=o_specs + [hbm] * o0,
        out_shape=o_shapes + c_outs,
        scratch_shapes=list(scratch_shapes) + [pltpu.SemaphoreType.DMA((s0,)), pltpu.SemaphoreType.DMA((s0,))],
        input_output_aliases=alias, name=name,
        compiler_params=pltpu.CompilerParams(dimension_semantics=(ARB,) * len(grid), vmem_limit_bytes=VMEM_LIMIT_V7X,
                                             has_side_effects=True))(*args, *c_ins)
    for ch, b in zip(chunks, out_off):
        ch.done(list(res[n_out + b:n_out + b + len(ch.out_shapes)]))
    return res[0] if single else list(res[:n_out])


def _mm_nn(a, b, *, name, q=None, out_dtype=F32):
    m, k = a.shape
    n = b.shape[-1]
    tm, tn = _mm_tiles(m, k, n, jnp.dtype(out_dtype).itemsize)

    def body(a_ref, b_ref, o_ref):
        o_ref[...] = _dot(a_ref[...], b_ref[...]).astype(o_ref.dtype)

    return _hosted_call(
        body, grid=(m // tm, n // tn),
        in_specs=[pl.BlockSpec((tm, k), lambda i, j: (i, 0)),
                  pl.BlockSpec((None, k, tn), lambda i, j: (0, 0, j))],
        out_specs=pl.BlockSpec((tm, tn), lambda i, j: (i, j)),
        out_shape=jax.ShapeDtypeStruct((m, n), out_dtype),
        args=(a, b), name=name, q=q, flops=2.0 * m * k * n)


def _mm_nt(a, b, *, name, q=None, out_dtype=F32):
    m, k = a.shape
    n = b.shape[-2]
    tm, tn = _mm_tiles(m, k, n, jnp.dtype(out_dtype).itemsize)

    def body(a_ref, b_ref, o_ref):
        o_ref[...] = _dot_nt(a_ref[...], b_ref[...]).astype(o_ref.dtype)

    return _hosted_call(
        body, grid=(m // tm, n // tn),
        in_specs=[pl.BlockSpec((tm, k), lambda i, j: (i, 0)),
                  pl.BlockSpec((None, tn, k), lambda i, j: (0, j, 0))],
        out_specs=pl.BlockSpec((tm, tn), lambda i, j: (i, j)),
        out_shape=jax.ShapeDtypeStruct((m, n), out_dtype),
        args=(a, b), name=name, q=q, flops=2.0 * m * k * n)


def _mm_nn_slots(a, b4, *, name, q=None, out_dtype=F32):
    m, k = a.shape
    s_, _, _, c = b4.shape
    ob = jnp.dtype(out_dtype).itemsize
    tm = max(t for t in _divisors(m, 256) if 2 * (t * k * 2 + k * c * 2 + t * c * ob) <= MM_VMEM_BUDGET)

    def body(a_ref, b_ref, o_ref):
        o_ref[...] = _dot(a_ref[...], b_ref[...]).astype(o_ref.dtype)

    return _hosted_call(
        body, grid=(m // tm, s_),
        in_specs=[pl.BlockSpec((tm, k), lambda i, j: (i, 0)),
                  pl.BlockSpec((None, None, k, c), lambda i, j: (j, 0, 0, 0))],
        out_specs=pl.BlockSpec((tm, c), lambda i, j: (i, j)),
        out_shape=jax.ShapeDtypeStruct((m, s_ * c), out_dtype),
        args=(a, b4), name=name, q=q, flops=2.0 * m * k * s_ * c)


def _mm_tn_slots(a, b, *, name, slot_cols, n_slots, first_slot=0, q=None, out=None):
    k, m = a.shape
    c = slot_cols
    tk, tm, _ = _mm_tn_tiles(k, m, c, whole_n=True)

    def body(a_ref, b_ref, *rest):
        o_ref = rest[-1]
        part = _dot_tn(a_ref[...], b_ref[...])

        @pl.when(pl.program_id(2) == 0)
        def _():
            o_ref[...] = part

        @pl.when(pl.program_id(2) > 0)
        def _():
            o_ref[...] += part

    in_specs = [pl.BlockSpec((tk, tm), lambda i, j, s: (s, i)), pl.BlockSpec((tk, c), lambda i, j, s: (s, j))]
    args = (a, b)
    if out is not None:
        in_specs.append(pl.BlockSpec(memory_space=pl.ANY))
        args = (a, b, out)
    return _hosted_call(
        body, grid=(m // tm, b.shape[-1] // c, k // tk), in_specs=in_specs,
        out_specs=pl.BlockSpec((None, None, tm, c), lambda i, j, s: (first_slot + j, 0, i, 0)),
        out_shape=jax.ShapeDtypeStruct((n_slots, 1, m, c), F32),
        aliases={2: 0} if out is not None else None,
        args=args, name=name, q=q, flops=2.0 * m * k * b.shape[-1])


def _mm_tn(a, b, *, name, q=None, out=None, n_total=None, col_block_offset=0):
    k, m = a.shape
    n = b.shape[-1]
    tk, tm, tn = _mm_tn_tiles(k, m, n)
    off = col_block_offset * (n // tn)

    def body(a_ref, b_ref, *rest):
        o_ref = rest[-1]
        part = _dot_tn(a_ref[...], b_ref[...])

        @pl.when(pl.program_id(2) == 0)
        def _():
            o_ref[...] = part

        @pl.when(pl.program_id(2) > 0)
        def _():
            o_ref[...] += part

    in_specs = [pl.BlockSpec((tk, tm), lambda i, j, s: (s, i)), pl.BlockSpec((tk, tn), lambda i, j, s: (s, j))]
    args = (a, b)
    if out is not None:
        in_specs.append(pl.BlockSpec(memory_space=pl.ANY))
        args = (a, b, out)
    return _hosted_call(
        body, grid=(m // tm, n // tn, k // tk), in_specs=in_specs,
        out_specs=pl.BlockSpec((None, tm, tn), lambda i, j, s: (0, i, j + off)),
        out_shape=jax.ShapeDtypeStruct((1, m, n_total or n), F32),
        aliases={2: 0} if out is not None else None,
        args=args, name=name, q=q, flops=2.0 * m * k * n)


def _mm_ffn_dh(dg, dv, w4, *, name, q=None):
    m, f = dg.shape
    n_slots, _, d, c = w4.shape
    tm, tn = _mm_tiles(m, 2 * f, d, 4)

    def body(dg_ref, dv_ref, *rest):
        w_refs, o_ref = rest[:n_slots], rest[n_slots]
        acc = None
        for s, w_ref in enumerate(w_refs):
            x_ref = dg_ref if s < n_slots // 2 else dv_ref
            off = (s % (n_slots // 2)) * c
            part = _dot_nt(x_ref[:, off:off + c], w_ref[...])
            acc = part if acc is None else acc + part
        o_ref[...] = acc.astype(o_ref.dtype)

    wspec = lambda s: pl.BlockSpec((None, None, tn, c), lambda i, j: (s, 0, j, 0))
    return _hosted_call(
        body, grid=(m // tm, d // tn),
        in_specs=[pl.BlockSpec((tm, f), lambda i, j: (i, 0)),
                  pl.BlockSpec((tm, f), lambda i, j: (i, 0))] + [wspec(s) for s in range(n_slots)],
        out_specs=pl.BlockSpec((tm, tn), lambda i, j: (i, j)),
        out_shape=jax.ShapeDtypeStruct((m, d), MXU),
        args=(dg, dv) + (w4,) * n_slots, name=name, q=q, flops=4.0 * m * f * d)


def _norm_fwd(x, g, *, name):
    n, d = x.shape
    tm = _tile(n, 256, SUBLANE)

    def body(x_ref, g_ref, o_ref):
        o_ref[...] = _rms_fwd(x_ref[...], g_ref[...]).astype(o_ref.dtype)

    return pl.pallas_call(
        body, grid=(n // tm,),
        in_specs=[pl.BlockSpec((tm, d), lambda i: (i, 0)), pl.BlockSpec((1, d), lambda i: (0, 0))],
        out_specs=pl.BlockSpec((tm, d), lambda i: (i, 0)),
        out_shape=jax.ShapeDtypeStruct((n, d), MXU),
        name=name, compiler_params=_cp((PAR,)))(x, g)


def _norm_bwd_dg(dy, x, g, *, name):
    n, d = x.shape
    tm = _tile(n, 256, SUBLANE)

    def body(dy_ref, x_ref, g_ref, dg_ref):
        @pl.when(pl.program_id(0) == 0)
        def _():
            dg_ref[...] = jnp.zeros_like(dg_ref)
        _, dg = _rms_bwd(dy_ref[...], x_ref[...], g_ref[...])
        dg_ref[...] += dg

    return pl.pallas_call(
        body, grid=(n // tm,),
        in_specs=[pl.BlockSpec((tm, d), lambda i: (i, 0)), pl.BlockSpec((tm, d), lambda i: (i, 0)),
                  pl.BlockSpec((1, d), lambda i: (0, 0))],
        out_specs=pl.BlockSpec((1, d), lambda i: (0, 0)),
        out_shape=jax.ShapeDtypeStruct((1, d), F32),
        name=name, compiler_params=_cp((ARB,)))(dy, x, g)


def _resid_norm_fwd(x, y, g_post, g_pres, *, name, q=None):
    n, d = x.shape
    tm = _tile(n, 256, SUBLANE)
    nh = len(g_pres)

    def body(x_ref, y_ref, gp_ref, *rest):
        gpre = rest[:nh]
        xo_ref = rest[nh]
        h_refs = rest[nh + 1:]
        xo = x_ref[...] + _rms_fwd(y_ref[...].astype(F32), gp_ref[...])
        xo_ref[...] = xo
        for g_ref, h_ref in zip(gpre, h_refs):
            h_ref[...] = _rms_fwd(xo, g_ref[...]).astype(h_ref.dtype)

    row = pl.BlockSpec((tm, d), lambda i: (i, 0))
    vec = pl.BlockSpec((1, d), lambda i: (0, 0))
    outs = _hosted_call(
        body, grid=(n // tm,),
        in_specs=[row, row, vec] + [vec] * nh,
        out_specs=[row] + [row] * nh,
        out_shape=[jax.ShapeDtypeStruct((n, d), F32)] + [jax.ShapeDtypeStruct((n, d), MXU)] * nh,
        args=(x, y, g_post, *g_pres), name=name, q=q, budget_us=HOST_US["resid"])
    return outs[0], list(outs[1:])


def _loss_fwd(x, y, g_post, target, *, name):
    n, d = x.shape
    tm = _tile(n, 256, SUBLANE)

    def body(x_ref, y_ref, gp_ref, t_ref, dx_ref, sq_ref):
        @pl.when(pl.program_id(0) == 0)
        def _():
            sq_ref[...] = jnp.zeros_like(sq_ref)
        err = x_ref[...] + _rms_fwd(y_ref[...].astype(F32), gp_ref[...]) - t_ref[...]
        dx_ref[...] = err * (1.0 / d)
        sq_ref[...] += jnp.sum(err * err, axis=0, keepdims=True)

    row = pl.BlockSpec((tm, d), lambda i: (i, 0))
    vec = pl.BlockSpec((1, d), lambda i: (0, 0))
    return pl.pallas_call(
        body, grid=(n // tm,),
        in_specs=[row, row, vec, row],
        out_specs=[row, vec],
        out_shape=[jax.ShapeDtypeStruct((n, d), F32), jax.ShapeDtypeStruct((1, d), F32)],
        name=name, compiler_params=_cp((ARB,)))(x, y, g_post, target)


def _resid_norm_bwd(dx_out, dhs, x_out, g_pres, y, g_post, *, name, q=None):
    n, d = dx_out.shape
    tm = _tile(n, 256, SUBLANE)
    nh = len(dhs)
    has_y = y is not None

    def body(*refs):
        it = iter(refs)
        dxo_ref = next(it)
        dh_refs = [next(it) for _ in range(nh)]
        xo_ref = next(it) if nh else None
        gpre_refs = [next(it) for _ in range(nh)]
        y_ref = next(it) if has_y else None
        gpost_ref = next(it) if has_y else None
        g_out = next(it)
        dy_out = next(it) if has_y else None
        dgpre_out = [next(it) for _ in range(nh)]
        dgpost_out = next(it) if has_y else None

        @pl.when(pl.program_id(0) == 0)
        def _():
            for r in dgpre_out:
                r[...] = jnp.zeros_like(r)
            if has_y:
                dgpost_out[...] = jnp.zeros_like(dgpost_out)

        g = dxo_ref[...]
        if nh:
            xo = xo_ref[...]
            for dh_ref, gp_ref, dg_ref in zip(dh_refs, gpre_refs, dgpre_out):
                dx, dg = _rms_bwd(dh_ref[...].astype(F32), xo, gp_ref[...])
                g = g + dx
                dg_ref[...] += dg
        g_out[...] = g
        if has_y:
            dy, dg = _rms_bwd(g, y_ref[...].astype(F32), gpost_ref[...])
            dy_out[...] = dy.astype(dy_out.dtype)
            dgpost_out[...] += dg

    row = pl.BlockSpec((tm, d), lambda i: (i, 0))
    vec = pl.BlockSpec((1, d), lambda i: (0, 0))
    ins, in_specs = [dx_out], [row]
    ins += list(dhs)
    in_specs += [row] * nh
    if nh:
        ins.append(x_out)
        in_specs.append(row)
    ins += list(g_pres)
    in_specs += [vec] * nh
    if has_y:
        ins += [y, g_post]
        in_specs += [row, vec]
    out_specs, out_shape = [row], [jax.ShapeDtypeStruct((n, d), F32)]
    if has_y:
        out_specs.append(row)
        out_shape.append(jax.ShapeDtypeStruct((n, d), MXU))
    out_specs += [vec] * nh
    out_shape += [jax.ShapeDtypeStruct((1, d), F32)] * nh
    if has_y:
        out_specs.append(vec)
        out_shape.append(jax.ShapeDtypeStruct((1, d), F32))
    outs = list(_hosted_call(
        body, grid=(n // tm,), in_specs=in_specs, out_specs=out_specs, out_shape=out_shape,
        args=tuple(ins), name=name, q=q, budget_us=HOST_US["resid_bwd"]))
    g = outs.pop(0)
    dy = outs.pop(0) if has_y else None
    dgpre = [outs.pop(0) for _ in range(nh)]
    dgpost = outs.pop(0) if has_y else None
    return g, dy, dgpre, dgpost


def _ffn_conv(up, w_ref, b_ref):
    return (w_ref[0:1, :] * _shift_down_edge(up, 2) + w_ref[1:2, :] * _shift_down_edge(up, 1)
            + w_ref[2:3, :] * up + b_ref[...])


def _ffn_act_fwd(up, wconv, bconv, bsz, *, name, q=None):
    n, f2 = up.shape
    f = f2 // 2
    t = n // bsz
    tc = _tile(f, 256)
    nf = f // tc

    def body(ug_ref, uv_ref, wg_ref, wv_ref, bg_ref, bv_ref, o_ref, g_ref, v_ref):
        g = _ffn_conv(ug_ref[...].astype(F32), wg_ref, bg_ref)
        v = _ffn_conv(uv_ref[...].astype(F32), wv_ref, bv_ref)
        g_ref[...] = g.astype(g_ref.dtype)
        v_ref[...] = v.astype(v_ref.dtype)
        o_ref[...] = (_gelu(g) * v).astype(o_ref.dtype)

    blk = pl.BlockSpec((t, tc), lambda b, j: (b, j))
    return _hosted_call(
        body, grid=(bsz, nf),
        in_specs=[blk, pl.BlockSpec((t, tc), lambda b, j: (b, j + nf)),
                  pl.BlockSpec((FFN_CONV, tc), lambda b, j: (0, j)),
                  pl.BlockSpec((FFN_CONV, tc), lambda b, j: (0, j + nf)),
                  pl.BlockSpec((1, tc), lambda b, j: (0, j)),
                  pl.BlockSpec((1, tc), lambda b, j: (0, j + nf))],
        out_specs=[blk, blk, blk],
        out_shape=[jax.ShapeDtypeStruct((n, f), MXU)] * 3,
        args=(up, up, wconv, wconv, bconv, bconv), name=name, q=q, budget_us=HOST_US["ffn_act"])


def _ffn_act_bwd(up, ug, uv, dact, wconv, bsz, *, name, q=None):
    n, f2 = up.shape
    f = f2 // 2
    t = n // bsz
    tc = _tile(f, 256)
    nf = f // tc

    def body(xg_ref, xv_ref, g_ref, v_ref, da_ref, wg_ref, wv_ref,
             dug_ref, duv_ref, dwg_ref, dwv_ref, dbg_ref, dbv_ref):
        @pl.when(pl.program_id(1) == 0)
        def _():
            for r in (dwg_ref, dwv_ref, dbg_ref, dbv_ref):
                r[...] = jnp.zeros_like(r)

        gl, dgl = _gelu_and_grad(g_ref[...].astype(F32))
        da = da_ref[...].astype(F32)
        dg = da * v_ref[...].astype(F32) * dgl
        dv = da * gl

        def conv_bwd(du, w_ref, x_ref, dx_ref, dw_ref, db_ref):
            du1, du2 = _shift_up_edge(du, 1), _shift_up_edge(du, 2)
            dx_ref[...] = (w_ref[2:3, :] * du + w_ref[1:2, :] * du1 + w_ref[0:1, :] * du2).astype(dx_ref.dtype)
            x = x_ref[...].astype(F32)
            dw_ref[0:1, :] += jnp.sum(x * du2, axis=0, keepdims=True)
            dw_ref[1:2, :] += jnp.sum(x * du1, axis=0, keepdims=True)
            dw_ref[2:3, :] += jnp.sum(x * du, axis=0, keepdims=True)
            db_ref[...] += jnp.sum(du, axis=0, keepdims=True)

        conv_bwd(dg, wg_ref, xg_ref, dug_ref, dwg_ref, dbg_ref)
        conv_bwd(dv, wv_ref, xv_ref, duv_ref, dwv_ref, dbv_ref)

    blk = pl.BlockSpec((t, tc), lambda j, b: (b, j))
    wspec = pl.BlockSpec((FFN_CONV, tc), lambda j, b: (0, j))
    bspec = pl.BlockSpec((1, tc), lambda j, b: (0, j))
    outs = _hosted_call(
        body, grid=(nf, bsz),
        in_specs=[blk, pl.BlockSpec((t, tc), lambda j, b: (b, j + nf)), blk, blk, blk,
                  wspec, pl.BlockSpec((FFN_CONV, tc), lambda j, b: (0, j + nf))],
        out_specs=[blk, blk, wspec, wspec, bspec, bspec],
        out_shape=[jax.ShapeDtypeStruct((n, f), MXU), jax.ShapeDtypeStruct((n, f), MXU),
                   jax.ShapeDtypeStruct((FFN_CONV, f), F32), jax.ShapeDtypeStruct((FFN_CONV, f), F32),
                   jax.ShapeDtypeStruct((1, f), F32), jax.ShapeDtypeStruct((1, f), F32)],
        args=(up, up, ug, uv, dact, wconv, wconv), name=name, q=q, budget_us=HOST_US["ffn_act_bwd"])
    dug, duv, dwg, dwv, dbg, dbv = outs
    return dug, duv, jnp.concatenate([dwg, dwv], axis=1), jnp.concatenate([dbg, dbv], axis=1)


def _mem_attn_fwd(proj, q_col_block, mkv, ycat, bsz, *, name, q=None):
    n = proj.shape[0]
    t = n // bsz
    tq = _tile(t, 512, SUBLANE)
    nt = t // tq
    scale = HEAD_DIM ** -0.5

    def body(q_ref, kv_ref, old_ref, o_ref):
        del old_ref
        outs = []
        for h in range(MEM_HEADS):
            sl = slice(h * HEAD_DIM, (h + 1) * HEAD_DIM)
            q = q_ref[:, sl].astype(MXU)
            k = kv_ref[:, sl].astype(MXU)
            v = kv_ref[:, MEM_WIDTH + h * HEAD_DIM: MEM_WIDTH + (h + 1) * HEAD_DIM].astype(MXU)
            s = _dot_nt(q, k) * scale
            m = jnp.max(s, axis=-1, keepdims=True)
            p = jnp.exp(s - m)
            p = p / jnp.sum(p, axis=-1, keepdims=True)
            outs.append(_dot(p.astype(MXU), v))
        o_ref[...] = jnp.concatenate(outs, axis=-1).astype(o_ref.dtype)

    return _hosted_call(
        body, grid=(bsz, nt),
        in_specs=[pl.BlockSpec((tq, MEM_WIDTH), lambda b, i: (b * nt + i, q_col_block)),
                  pl.BlockSpec((MEM_LEN, 2 * MEM_WIDTH), lambda b, i: (b, 0)),
                  pl.BlockSpec(memory_space=pl.ANY)],
        out_specs=pl.BlockSpec((tq, MEM_WIDTH), lambda b, i: (b * nt + i, MIX_WIDTH // MEM_WIDTH)),
        out_shape=jax.ShapeDtypeStruct(ycat.shape, ycat.dtype),
        aliases={2: 0}, args=(proj, mkv, ycat), name=name, q=q, budget_us=HOST_US["mem_attn_fwd"])


def _mem_attn_bwd(proj, q_col_block, mkv, dycat, dproj, bsz, *, name, q=None):
    n = proj.shape[0]
    t = n // bsz
    tq = _tile(t, 512, SUBLANE)
    nt = t // tq
    scale = HEAD_DIM ** -0.5

    def body(q_ref, kv_ref, do_ref, old_ref, dq_ref, dkv_ref):
        del old_ref

        @pl.when(pl.program_id(1) == 0)
        def _():
            dkv_ref[...] = jnp.zeros_like(dkv_ref)

        dqs, dks, dvs = [], [], []
        for h in range(MEM_HEADS):
            sl = slice(h * HEAD_DIM, (h + 1) * HEAD_DIM)
            q = q_ref[:, sl].astype(MXU)
            k = kv_ref[:, sl].astype(MXU)
            v = kv_ref[:, MEM_WIDTH + h * HEAD_DIM: MEM_WIDTH + (h + 1) * HEAD_DIM].astype(MXU)
            do = do_ref[:, sl].astype(MXU)
            s = _dot_nt(q, k) * scale
            m = jnp.max(s, axis=-1, keepdims=True)
            p = jnp.exp(s - m)
            p = p / jnp.sum(p, axis=-1, keepdims=True)
            dvs.append(_dot_tn(p.astype(MXU), do))
            dp = _dot_nt(do, v)
            ds = (p * (dp - jnp.sum(dp * p, axis=-1, keepdims=True)) * scale).astype(MXU)
            dqs.append(_dot(ds, k))
            dks.append(_dot_tn(ds, q))
        dq_ref[...] = jnp.concatenate(dqs, axis=-1).astype(dq_ref.dtype)
        dkv_ref[...] += jnp.concatenate(dks + dvs, axis=-1)

    return _hosted_call(
        body, grid=(bsz, nt),
        in_specs=[pl.BlockSpec((tq, MEM_WIDTH), lambda b, i: (b * nt + i, q_col_block)),
                  pl.BlockSpec((MEM_LEN, 2 * MEM_WIDTH), lambda b, i: (b, 0)),
                  pl.BlockSpec((tq, MEM_WIDTH), lambda b, i: (b * nt + i, MIX_WIDTH // MEM_WIDTH)),
                  pl.BlockSpec(memory_space=pl.ANY)],
        out_specs=[pl.BlockSpec((tq, MEM_WIDTH), lambda b, i: (b * nt + i, q_col_block)),
                   pl.BlockSpec((MEM_LEN, 2 * MEM_WIDTH), lambda b, i: (b, 0))],
        out_shape=[jax.ShapeDtypeStruct(dproj.shape, dproj.dtype),
                   jax.ShapeDtypeStruct((bsz * MEM_LEN, 2 * MEM_WIDTH), F32)],
        aliases={3: 0}, args=(proj, mkv, dycat, dproj), name=name, q=q, budget_us=HOST_US["mem_attn_bwd"])


def _swa_scores(q, k, h, dist, mask, sink):
    s = _dot_nt(q, k) * (HEAD_DIM ** -0.5)
    s = jnp.where(mask, s - SLOPES[h] * dist, -jnp.inf)
    m = jnp.maximum(jnp.max(s, axis=-1, keepdims=True), sink)
    p = jnp.exp(s - m)
    psink = jnp.exp(sink - m)
    inv = 1.0 / (jnp.sum(p, axis=-1, keepdims=True) + psink)
    return p * inv, psink * inv


def _swa_mask(n):
    qi = lax.broadcasted_iota(jnp.int32, (WINDOW, 2 * WINDOW), 0) + WINDOW
    ki = lax.broadcasted_iota(jnp.int32, (WINDOW, 2 * WINDOW), 1)
    dist = qi - ki
    mask = (dist >= 0) & (dist < WINDOW) & ((n > 0) | (ki >= WINDOW))
    return dist.astype(F32), mask


def _swa_fwd(proj, kv, sinks, bsz, *, name, q=None):
    n_tok = proj.shape[0]
    nb = n_tok // bsz // WINDOW
    kvw = SWA_KV_HEADS * HEAD_DIM

    def body(sink_ref, q_ref, kvp_ref, kvc_ref, o_ref):
        n = pl.program_id(1)
        dist, mask = _swa_mask(n)
        kk = jnp.concatenate([kvp_ref[:, :kvw], kvc_ref[:, :kvw]], axis=0).astype(MXU)
        vv = jnp.concatenate([kvp_ref[:, kvw:], kvc_ref[:, kvw:]], axis=0).astype(MXU)
        outs = []
        for h in range(SWA_HEADS):
            c = h // SWA_GROUP
            q = q_ref[:, h * HEAD_DIM:(h + 1) * HEAD_DIM].astype(MXU)
            p, _ = _swa_scores(q, kk[:, c * HEAD_DIM:(c + 1) * HEAD_DIM], h, dist, mask, sink_ref[h])
            outs.append(_dot(p.astype(MXU), vv[:, c * HEAD_DIM:(c + 1) * HEAD_DIM]))
        o_ref[...] = jnp.concatenate(outs, axis=-1).astype(o_ref.dtype)

    return _hosted_call(
        body, grid=(bsz, nb),
        in_specs=[pl.BlockSpec(memory_space=pltpu.SMEM),
                  pl.BlockSpec((WINDOW, MIX_WIDTH), lambda b, n: (b * nb + n, 0)),
                  pl.BlockSpec((WINDOW, 2 * kvw), lambda b, n: (b * nb + jnp.maximum(n - 1, 0), 0)),
                  pl.BlockSpec((WINDOW, 2 * kvw), lambda b, n: (b * nb + n, 0))],
        out_specs=pl.BlockSpec((WINDOW, MIX_WIDTH), lambda b, n: (b * nb + n, 0)),
        out_shape=jax.ShapeDtypeStruct((n_tok, D_MODEL), MXU),
        args=(sinks, proj, kv, kv), name=name, q=q, budget_us=HOST_US["swa_fwd"])


def _swa_bwd(proj, kv, sinks, dycat, bsz, *, name, q=None):
    n_tok = proj.shape[0]
    nb = n_tok // bsz // WINDOW
    kvw = SWA_KV_HEADS * HEAD_DIM

    def body(sink_ref, q_ref, kvp_ref, kvc_ref, do_ref, dq_ref, dkvc_ref, dkvp_ref, dsink_ref):
        n = pl.program_id(1)

        @pl.when((pl.program_id(0) == 0) & (n == 0))
        def _():
            dsink_ref[...] = jnp.zeros_like(dsink_ref)

        dist, mask = _swa_mask(n)
        kk = jnp.concatenate([kvp_ref[:, :kvw], kvc_ref[:, :kvw]], axis=0).astype(MXU)
        vv = jnp.concatenate([kvp_ref[:, kvw:], kvc_ref[:, kvw:]], axis=0).astype(MXU)
        lane = lax.broadcasted_iota(jnp.int32, (SUBLANE, LANE), 1)
        dqs = []
        dks = [None] * SWA_KV_HEADS
        dvs = [None] * SWA_KV_HEADS
        dsink = jnp.zeros((SUBLANE, LANE), F32)
        for h in range(SWA_HEADS):
            c = h // SWA_GROUP
            k = kk[:, c * HEAD_DIM:(c + 1) * HEAD_DIM]
            v = vv[:, c * HEAD_DIM:(c + 1) * HEAD_DIM]
            q = q_ref[:, h * HEAD_DIM:(h + 1) * HEAD_DIM].astype(MXU)
            do = do_ref[:, h * HEAD_DIM:(h + 1) * HEAD_DIM].astype(MXU)
            p, psink = _swa_scores(q, k, h, dist, mask, sink_ref[h])
            dv = _dot_tn(p.astype(MXU), do)
            dp = _dot_nt(do, v)
            rs = jnp.sum(dp * p, axis=-1, keepdims=True)
            ds = (p * (dp - rs) * (HEAD_DIM ** -0.5)).astype(MXU)
            dsink = dsink + jnp.where(lane == h, jnp.sum(-psink * rs, axis=0, keepdims=True), 0.0)
            dqs.append(_dot(ds, k))
            dk = _dot_tn(ds, q)
            dks[c] = dk if dks[c] is None else dks[c] + dk
            dvs[c] = dv if dvs[c] is None else dvs[c] + dv
        dq_ref[...] = jnp.concatenate(dqs, axis=-1).astype(dq_ref.dtype)
        dkv = jnp.concatenate(dks + dvs, axis=-1)
        dkvp_ref[...] = dkv[:WINDOW]
        dkvc_ref[...] = dkv[WINDOW:]
        dsink_ref[...] += dsink

    qspec = pl.BlockSpec((WINDOW, MIX_WIDTH), lambda b, n: (b * nb + n, 0))
    kvspec = pl.BlockSpec((WINDOW, 2 * kvw), lambda b, n: (b * nb + n, 0))
    return _hosted_call(
        body, grid=(bsz, nb),
        in_specs=[pl.BlockSpec(memory_space=pltpu.SMEM), qspec,
                  pl.BlockSpec((WINDOW, 2 * kvw), lambda b, n: (b * nb + jnp.maximum(n - 1, 0), 0)),
                  kvspec, qspec],
        out_specs=[qspec, kvspec, kvspec, pl.BlockSpec((SUBLANE, LANE), lambda b, n: (0, 0))],
        out_shape=[jax.ShapeDtypeStruct((n_tok, D_MODEL), MXU),
                   jax.ShapeDtypeStruct((n_tok, 2 * kvw), F32),
                   jax.ShapeDtypeStruct((n_tok, 2 * kvw), F32),
                   jax.ShapeDtypeStruct((SUBLANE, LANE), F32)],
        args=(sinks, proj, kv, kv, dycat), name=name, q=q, budget_us=HOST_US["swa_bwd"])


def _swa_dkv_combine(curs, prevs, bsz, *, name):
    n_tok, w = curs[0].shape
    nb = n_tok // bsz // WINDOW
    k = len(curs)

    def body(*refs):
        o_ref = refs[-1]
        n = pl.program_id(1)
        acc = refs[0][...]
        for r in refs[1:k]:
            acc = acc + r[...]
        nxt = refs[k][...]
        for r in refs[k + 1:2 * k]:
            nxt = nxt + r[...]
        o_ref[...] = (acc + jnp.where(n < nb - 1, nxt, 0.0)).astype(o_ref.dtype)

    cur = pl.BlockSpec((WINDOW, w), lambda b, n: (b * nb + n, 0))
    prv = pl.BlockSpec((WINDOW, w), lambda b, n: (b * nb + jnp.minimum(n + 1, nb - 1), 0))
    return pl.pallas_call(
        body, grid=(bsz, nb), in_specs=[cur] * k + [prv] * k, out_specs=cur,
        out_shape=jax.ShapeDtypeStruct((n_tok, w), MXU),
        name=name, compiler_params=_cp((PAR, PAR)))(*curs, *prevs)


def _lru_gates(ux, halo, ext_ref, wc_ref, bc_ref, wr_ref, br_ref, wi_ref, bi_ref, lam_ref):
    tt = ux.shape[0]
    ext_ref[0:SUBLANE, :] = halo
    ext_ref[SUBLANE:, :] = ux
    xs = [ux] + [ext_ref[pl.ds(SUBLANE - k, tt), :] for k in range(1, LRU_CONV)]
    xc = bc_ref[...] + wc_ref[3:4, :] * xs[0] + wc_ref[2:3, :] * xs[1] + wc_ref[1:2, :] * xs[2] + wc_ref[0:1, :] * xs[3]
    pre_r, pre_i = [], []
    for blk in range(MIX_WIDTH // GATE_TILE):
        xb = xc[:, blk * GATE_TILE:(blk + 1) * GATE_TILE].astype(MXU)
        pre_r.append(_dot(xb, wr_ref[blk]))
        pre_i.append(_dot(xb, wi_ref[blk]))
    r = jax.nn.sigmoid(jnp.concatenate(pre_r, axis=-1) + br_ref[...])
    i = jax.nn.sigmoid(jnp.concatenate(pre_i, axis=-1) + bi_ref[...])
    nlam = -lam_ref[...]
    sp = jnp.maximum(nlam, 0.0) + jnp.log(1.0 + jnp.exp(-jnp.abs(nlam)))
    log_a = -LRU_C * r * sp
    a = jnp.exp(log_a)
    om = -jnp.tanh(log_a) * (a * a + 1.0)
    s = jnp.sqrt(om)
    return xs, xc, r, i, sp, a, s


def _lru_fwd(proj, wconv, bconv, wr, br, wi, bi, lam, bsz, *, name, q=None):
    n_tok = proj.shape[0]
    t = n_tok // bsz
    tt = _tile(t, 256, SUBLANE)
    nt = t // tt
    w = MIX_WIDTH
    ng = tt // SUBLANE

    def body(pg_ref, halo_ref, wc_ref, bc_ref, wr_ref, br_ref, wi_ref, bi_ref, lam_ref,
             y_ref, h_ref, ext_ref, a_ref, b_ref, carry_ref):
        ti = pl.program_id(1)

        @pl.when(ti == 0)
        def _():
            carry_ref[...] = jnp.zeros_like(carry_ref)

        gate = pg_ref[:, :w]
        ux = pg_ref[:, w:]
        halo = jnp.where(ti > 0, halo_ref[...], 0.0)
        _, xc, _, i, _, a, s = _lru_gates(ux, halo, ext_ref, wc_ref, bc_ref, wr_ref, br_ref, wi_ref, bi_ref, lam_ref)
        a_ref[...] = a
        b_ref[...] = s * (i * xc)
        row = lax.broadcasted_iota(jnp.int32, (SUBLANE, w), 0)

        def group(g, hprev):
            off = pl.multiple_of(g * SUBLANE, SUBLANE)
            ca = a_ref[pl.ds(off, SUBLANE), :]
            cb = b_ref[pl.ds(off, SUBLANE), :]
            for d in (1, 2, 4):
                a_sh = jnp.where(row >= d, pltpu.roll(ca, d, axis=0), 1.0)
                b_sh = jnp.where(row >= d, pltpu.roll(cb, d, axis=0), 0.0)
                cb = ca * b_sh + cb
                ca = ca * a_sh
            h = ca * hprev + cb
            b_ref[pl.ds(off, SUBLANE), :] = h
            return jnp.broadcast_to(h[SUBLANE - 1:SUBLANE, :], (SUBLANE, w))

        carry_ref[...] = lax.fori_loop(0, ng, group, carry_ref[...])
        h = b_ref[...]
        h_ref[...] = h
        y_ref[...] = (h * _gelu(gate)).astype(y_ref.dtype)

    vec = lambda r: pl.BlockSpec((r, w), lambda b, i: (0, 0))
    wspec = pl.BlockSpec((w // GATE_TILE, GATE_TILE, GATE_TILE), lambda b, i: (0, 0, 0))
    hb = tt // SUBLANE
    return _hosted_call(
        body, grid=(bsz, nt),
        in_specs=[pl.BlockSpec((tt, 2 * w), lambda b, i: (b * nt + i, 0)),
                  pl.BlockSpec((SUBLANE, w), lambda b, i: (jnp.maximum((b * nt + i) * hb - 1, 0), 1)),
                  vec(LRU_CONV), vec(1), wspec, vec(1), wspec, vec(1), vec(1)],
        out_specs=[pl.BlockSpec((tt, w), lambda b, i: (b * nt + i, 0)),
                   pl.BlockSpec((tt, w), lambda b, i: (b * nt + i, 0))],
        out_shape=[jax.ShapeDtypeStruct((n_tok, D_MODEL), MXU), jax.ShapeDtypeStruct((n_tok, w), F32)],
        scratch_shapes=[pltpu.VMEM((tt + SUBLANE, w), F32), pltpu.VMEM((tt, w), F32),
                        pltpu.VMEM((tt, w), F32), pltpu.VMEM((SUBLANE, w), F32)],
        args=(proj, proj, wconv, bconv, wr, br, wi, bi, lam), name=name, q=q, budget_us=HOST_US["lru_fwd"])


def _lru_bwd(proj, hs, dycat, wconv, bconv, wr, br, wi, bi, lam, bsz, *, name, q=None):
    n_tok = proj.shape[0]
    t = n_tok // bsz
    tt = _tile(t, 256, SUBLANE)
    nt = t // tt
    w = MIX_WIDTH
    ng = tt // SUBLANE
    nblk = w // GATE_TILE

    def body(pg_ref, halo_ref, h_ref, hhalo_ref, dy_ref, wc_ref, bc_ref, wr_ref, br_ref, wi_ref, bi_ref, lam_ref,
             dp_ref, dwc_ref, dbc_ref, dwr_ref, dbr_ref, dwi_ref, dbi_ref, dlam_ref,
             ext_ref, a_ref, c_ref, g_ref, gcarry_ref, xcarry_ref):
        bi_ = pl.program_id(0)
        ti = nt - 1 - pl.program_id(1)

        @pl.when((bi_ == 0) & (pl.program_id(1) == 0))
        def _():
            for r in (dwc_ref, dbc_ref, dwr_ref, dbr_ref, dwi_ref, dbi_ref, dlam_ref):
                r[...] = jnp.zeros_like(r)

        @pl.when(pl.program_id(1) == 0)
        def _():
            gcarry_ref[...] = jnp.zeros_like(gcarry_ref)
            xcarry_ref[...] = jnp.zeros_like(xcarry_ref)

        gate = pg_ref[:, :w]
        ux = pg_ref[:, w:]
        halo = jnp.where(ti > 0, halo_ref[...], 0.0)
        xs, xc, r, i, sp, a, s = _lru_gates(ux, halo, ext_ref, wc_ref, bc_ref, wr_ref, br_ref, wi_ref, bi_ref, lam_ref)
        h = h_ref[...]
        gl, dgl = _gelu_and_grad(gate)
        dy = dy_ref[...].astype(F32)
        dgate = dy * h * dgl
        row_t = lax.broadcasted_iota(jnp.int32, (tt, w), 0)
        g_ref[...] = dy * gl + jnp.where(row_t == tt - 1, gcarry_ref[0:1, :], 0.0)
        c_ref[...] = _shift_up(a, 1, row_t)
        row = lax.broadcasted_iota(jnp.int32, (SUBLANE, w), 0)

        a_ref[...] = a

        def group(k, gnext):
            off = pl.multiple_of((ng - 1 - k) * SUBLANE, SUBLANE)
            cc = c_ref[pl.ds(off, SUBLANE), :]
            cb = g_ref[pl.ds(off, SUBLANE), :]
            cb = cb + jnp.where(row == SUBLANE - 1, gnext, 0.0)
            cc = jnp.where(row == SUBLANE - 1, 0.0, cc)
            for d in (1, 2, 4):
                c_sh = jnp.where(row < SUBLANE - d, pltpu.roll(cc, SUBLANE - d, axis=0), 1.0)
                b_sh = jnp.where(row < SUBLANE - d, pltpu.roll(cb, SUBLANE - d, axis=0), 0.0)
                cb = cc * b_sh + cb
                cc = cc * c_sh
            g_ref[pl.ds(off, SUBLANE), :] = cb
            a0 = a_ref[pl.ds(off, SUBLANE), :]
            return jnp.broadcast_to(a0[0:1, :] * cb[0:1, :], (SUBLANE, w))

        gc = lax.fori_loop(0, ng, group, jnp.zeros((SUBLANE, w), F32))
        gcarry_ref[...] = gc
        gsc = g_ref[...]

        hhalo = jnp.where(ti > 0, hhalo_ref[SUBLANE - 1:SUBLANE, :], 0.0)
        hprev = jnp.where(row_t == 0, hhalo, pltpu.roll(h, 1, axis=0))
        gated = i * xc
        d_gated = gsc * s
        d_atot = gsc * hprev - (gsc * gated) * a / s
        d_loga = d_atot * a
        d_r = d_loga * (-LRU_C) * sp
        dlam_ref[...] += jnp.sum(d_loga * r, axis=0, keepdims=True) * (LRU_C * jax.nn.sigmoid(-lam_ref[...]))
        d_i = d_gated * xc
        d_xc = d_gated * i
        d_pr = d_r * r * (1.0 - r)
        d_pi = d_i * i * (1.0 - i)
        dbr_ref[...] += jnp.sum(d_pr, axis=0, keepdims=True)
        dbi_ref[...] += jnp.sum(d_pi, axis=0, keepdims=True)
        extra = []
        for blk in range(nblk):
            sl = slice(blk * GATE_TILE, (blk + 1) * GATE_TILE)
            xb = xc[:, sl].astype(MXU)
            dr_b = d_pr[:, sl].astype(MXU)
            di_b = d_pi[:, sl].astype(MXU)
            dwr_ref[blk] += _dot_tn(xb, dr_b)
            dwi_ref[blk] += _dot_tn(xb, di_b)
            extra.append(_dot_nt(dr_b, wr_ref[blk]) + _dot_nt(di_b, wi_ref[blk]))
        d_xc = d_xc + jnp.concatenate(extra, axis=-1)
        dbc_ref[...] += jnp.sum(d_xc, axis=0, keepdims=True)
        for k in range(LRU_CONV):
            dwc_ref[k:k + 1, :] += jnp.sum(d_xc * xs[LRU_CONV - 1 - k], axis=0, keepdims=True)
        ext_ref[0:tt, :] = d_xc
        ext_ref[tt:, :] = xcarry_ref[...]
        dux = wc_ref[3:4, :] * d_xc
        for k in range(LRU_CONV - 1):
            dux = dux + wc_ref[k:k + 1, :] * ext_ref[pl.ds(LRU_CONV - 1 - k, tt), :]
        xcarry_ref[...] = d_xc[0:SUBLANE, :]
        dp_ref[:, :w] = dgate.astype(dp_ref.dtype)
        dp_ref[:, w:] = dux.astype(dp_ref.dtype)

    vec = lambda r: pl.BlockSpec((r, w), lambda b, i: (0, 0))
    wspec = pl.BlockSpec((nblk, GATE_TILE, GATE_TILE), lambda b, i: (0, 0, 0))
    hb = tt // SUBLANE
    rblk = lambda b, i: b * nt + (nt - 1 - i)
    halo_idx = lambda b, i: jnp.maximum(rblk(b, i) * hb - 1, 0)
    wide = pl.BlockSpec((tt, 2 * w), lambda b, i: (rblk(b, i), 0))
    narrow = pl.BlockSpec((tt, w), lambda b, i: (rblk(b, i), 0))
    return _hosted_call(
        body, grid=(bsz, nt),
        in_specs=[wide, pl.BlockSpec((SUBLANE, w), lambda b, i: (halo_idx(b, i), 1)),
                  narrow, pl.BlockSpec((SUBLANE, w), lambda b, i: (halo_idx(b, i), 0)), narrow,
                  vec(LRU_CONV), vec(1), wspec, vec(1), wspec, vec(1), vec(1)],
        out_specs=[wide, vec(LRU_CONV), vec(1), wspec, vec(1), wspec, vec(1), vec(1)],
        out_shape=[jax.ShapeDtypeStruct((n_tok, 2 * w + MEM_WIDTH), MXU),
                   jax.ShapeDtypeStruct((LRU_CONV, w), F32), jax.ShapeDtypeStruct((1, w), F32),
                   jax.ShapeDtypeStruct((nblk, GATE_TILE, GATE_TILE), F32), jax.ShapeDtypeStruct((1, w), F32),
                   jax.ShapeDtypeStruct((nblk, GATE_TILE, GATE_TILE), F32), jax.ShapeDtypeStruct((1, w), F32),
                   jax.ShapeDtypeStruct((1, w), F32)],
        scratch_shapes=[pltpu.VMEM((tt + SUBLANE, w), F32), pltpu.VMEM((tt, w), F32), pltpu.VMEM((tt, w), F32),
                        pltpu.VMEM((tt, w), F32), pltpu.VMEM((SUBLANE, w), F32), pltpu.VMEM((SUBLANE, w), F32)],
        args=(proj, proj, hs, hs, dycat, wconv, bconv, wr, br, wi, bi, lam), name=name, q=q,
        budget_us=HOST_US["lru_bwd"])


def _gate_tiles(w):
    per = GATE_TILE // HEAD_DIM
    w4 = w.reshape(LRU_BLOCKS // per, per, HEAD_DIM, HEAD_DIM)
    eye = jnp.eye(per, dtype=w.dtype)
    return jnp.einsum("bnij,nm->bnimj", w4, eye).reshape(LRU_BLOCKS // per, GATE_TILE, GATE_TILE)


def _gate_blocks(t):
    per = GATE_TILE // HEAD_DIM
    t5 = t.reshape(LRU_BLOCKS // per, per, HEAD_DIM, per, HEAD_DIM)
    eye = jnp.eye(per, dtype=t.dtype)
    return jnp.einsum("bnimj,nm->bnij", t5, eye).reshape(LRU_BLOCKS, HEAD_DIM, HEAD_DIM)


def _row(v):
    return v.reshape(1, -1)


def _local_step(x, mem, target, p, wfull, push_grad, q):
    bsz, t, d = x.shape
    n = bsz * t
    x2d = x.reshape(n, d)
    tgt = target.reshape(n, d)
    mem2d = mem.reshape(bsz * MEM_LEN, d)
    wr_t = [_gate_tiles(p["w_rg_r"][j]).astype(MXU) for j in range(N_A)]
    wi_t = [_gate_tiles(p["w_rg_i"][j]).astype(MXU) for j in range(N_A)]

    mn = [_norm_fwd(mem2d, _row(p["g_mem"][l]), name=f"mem_norm{l}") for l in range(DEPTH)]
    mkv = [None] * DEPTH
    h = _norm_fwd(x2d, _row(p["g_mix_pre"][0]), name="in_norm")
    xin = x2d
    sv = []
    kv = hkv = None
    for l in range(DEPTH):
        s = {"xin": xin, "h": h}
        if q is not None:
            q.horizon = (l + 2) * GROUPS_PER_LAYER
        mkv[l] = _mm_nn(mn[l], wfull("w_mem_kv", l), name=f"mem_kv{l}", q=q)
        if l < N_A:
            proj = _mm_nn(h, wfull("w_in_a", l), name=f"in_proj{l}", q=q)
            ycat, hs = _lru_fwd(proj, p["w_conv_a"][l], _row(p["b_conv_a"][l]), wr_t[l], _row(p["b_rg_r"][l]),
                                wi_t[l], _row(p["b_rg_i"][l]), _row(p["lru_lambda"][l]), bsz, name=f"lru_fwd{l}", q=q)
            s["hs"] = hs
            qblk = 2 * MIX_WIDTH // MEM_WIDTH
        else:
            if l == N_A:
                kv = _mm_nn(hkv, wfull("w_kv", 0), name="kv_proj", q=q)
            proj = _mm_nn(h, wfull("w_in_b", l - N_A), name=f"in_proj{l}", q=q)
            ycat = _swa_fwd(proj, kv, p["sinks_b"][l - N_A], bsz, name=f"swa_fwd{l}", q=q)
            qblk = MIX_WIDTH // MEM_WIDTH
        ycat = _mem_attn_fwd(proj, qblk, mkv[l], ycat, bsz, name=f"mem_attn_fwd{l}", q=q)
        y = _mm_nn(ycat, wfull("w_mix_out", l), name=f"mix_out{l}", q=q, out_dtype=MXU)
        x1, (h2,) = _resid_norm_fwd(xin, y, _row(p["g_mix_post"][l]), [_row(p["g_ffn_pre"][l])], name=f"mix_resid{l}", q=q)
        up = _mm_nn_slots(h2, wfull("w_ffn_up", l), name=f"ffn_up{l}", q=q, out_dtype=MXU)
        act, ug, uv = _ffn_act_fwd(up, p["w_ffn_conv"][l], _row(p["b_ffn_conv"][l]), bsz, name=f"ffn_act{l}", q=q)
        f = _mm_nn(act, wfull("w_ffn_down", l), name=f"ffn_down{l}", q=q, out_dtype=MXU)
        s.update(proj=proj, qblk=qblk, ycat=ycat, y=y, x1=x1, h2=h2, up=up, ug=ug, uv=uv, act=act, f=f)
        sv.append(s)
        if l < DEPTH - 1:
            g_pres = [_row(p["g_mix_pre"][l + 1])] + ([_row(p["g_kv"])] if l + 1 == N_A else [])
            xin, hn = _resid_norm_fwd(x1, f, _row(p["g_ffn_post"][l]), g_pres, name=f"ffn_resid{l}", q=q)
            h = hn[0]
            if l + 1 == N_A:
                hkv = hn[1]
        else:
            g_tot, sq = _loss_fwd(x1, f, _row(p["g_ffn_post"][l]), tgt, name="loss")

    if q is not None:
        q.horizon = LAST_GROUP
    gs = {k: [None] * DEPTH for k in ("g_mix_pre", "g_mix_post", "g_ffn_pre", "g_ffn_post", "g_mem",
                                       "w_ffn_conv", "b_ffn_conv")}
    ga = {k: [None] * N_A for k in ("w_conv_a", "b_conv_a", "w_rg_r", "b_rg_r", "w_rg_i", "b_rg_i", "lru_lambda")}
    gsink = [None] * (DEPTH - N_A)
    dkv_cur, dkv_prev = [], []
    g_tot, df, _, gs["g_ffn_post"][DEPTH - 1] = _resid_norm_bwd(
        g_tot, [], None, [], sv[-1]["f"], _row(p["g_ffn_post"][DEPTH - 1]), name="loss_bwd")
    grad_x = None
    for l in reversed(range(DEPTH)):
        s = sv[l]
        dact = _mm_nt(df, wfull("w_ffn_down", l), name=f"d_act{l}", q=q, out_dtype=MXU)
        push_grad("w_ffn_down", l, _mm_tn(s["act"], df, name=f"dw_down{l}", q=q))
        dug, duv, gs["w_ffn_conv"][l], gs["b_ffn_conv"][l] = _ffn_act_bwd(
            s["up"], s["ug"], s["uv"], dact, p["w_ffn_conv"][l], bsz, name=f"ffn_act_bwd{l}", q=q)
        dh2 = _mm_ffn_dh(dug, duv, wfull("w_ffn_up", l), name=f"d_h2_{l}", q=q)
        up_slots = dict(slot_cols=2 * D_FF // N_CHIP, n_slots=N_CHIP)
        dwu = _mm_tn_slots(s["h2"], dug, name=f"dw_up_g{l}", q=q, **up_slots)
        push_grad("w_ffn_up", l, _mm_tn_slots(s["h2"], duv, name=f"dw_up_v{l}", q=q, out=dwu,
                                              first_slot=N_CHIP // 2, **up_slots))
        g1, dy, (gs["g_ffn_pre"][l],), gs["g_mix_post"][l] = _resid_norm_bwd(
            g_tot, [dh2], s["x1"], [_row(p["g_ffn_pre"][l])], s["y"], _row(p["g_mix_post"][l]), name=f"mix_resid_bwd{l}", q=q)
        dycat = _mm_nt(dy, wfull("w_mix_out", l), name=f"d_ycat{l}", q=q, out_dtype=MXU)
        push_grad("w_mix_out", l, _mm_tn(s["ycat"], dy, name=f"dw_mix_out{l}", q=q))
        if l < N_A:
            dproj, dwc, dbc, dwr, dbr, dwi, dbi, dlam = _lru_bwd(
                s["proj"], s["hs"], dycat, p["w_conv_a"][l], _row(p["b_conv_a"][l]), wr_t[l], _row(p["b_rg_r"][l]),
                wi_t[l], _row(p["b_rg_i"][l]), _row(p["lru_lambda"][l]), bsz, name=f"lru_bwd{l}", q=q)
            ga["w_conv_a"][l], ga["b_conv_a"][l], ga["lru_lambda"][l] = dwc, dbc[0], dlam[0]
            ga["w_rg_r"][l], ga["w_rg_i"][l] = _gate_blocks(dwr), _gate_blocks(dwi)
            ga["b_rg_r"][l] = dbr.reshape(LRU_BLOCKS, HEAD_DIM)
            ga["b_rg_i"][l] = dbi.reshape(LRU_BLOCKS, HEAD_DIM)
            w_in, j = "w_in_a", l
        else:
            dproj, dc, dp_, dsk = _swa_bwd(s["proj"], kv, p["sinks_b"][l - N_A], dycat, bsz, name=f"swa_bwd{l}", q=q)
            dkv_cur.append(dc)
            dkv_prev.append(dp_)
            gsink[l - N_A] = dsk[0, :SWA_HEADS]
            w_in, j = "w_in_b", l - N_A
        dproj, dmkv = _mem_attn_bwd(s["proj"], s["qblk"], mkv[l], dycat, dproj, bsz, name=f"mem_attn_bwd{l}", q=q)
        dh = _mm_nt(dproj, wfull(w_in, j), name=f"d_h{l}", q=q, out_dtype=MXU)
        push_grad(w_in, j, _mm_tn(s["h"], dproj, name=f"dw_in{l}", q=q))
        dmkv = dmkv.astype(MXU)
        dmn = _mm_nt(dmkv, wfull("w_mem_kv", l), name=f"d_mem_norm{l}", q=q)
        push_grad("w_mem_kv", l, _mm_tn(mn[l], dmkv, name=f"dw_mem_kv{l}", q=q))
        gs["g_mem"][l] = _norm_bwd_dg(dmn, mem2d, _row(p["g_mem"][l]), name=f"mem_norm_bwd{l}")
        dhs, g_pres = [dh], [_row(p["g_mix_pre"][l])]
        if l == N_A:
            dkv = _swa_dkv_combine(dkv_cur, dkv_prev, bsz, name="dkv_combine")
            dhs.append(_mm_nt(dkv, wfull("w_kv", 0), name="d_hkv", q=q, out_dtype=MXU))
            g_pres.append(_row(p["g_kv"]))
            push_grad("w_kv", 0, _mm_tn(hkv, dkv, name="dw_kv", q=q))
        if l > 0:
            g_tot, df, dgpre, gs["g_ffn_post"][l - 1] = _resid_norm_bwd(
                g1, dhs, s["xin"], g_pres, sv[l - 1]["f"], _row(p["g_ffn_post"][l - 1]), name=f"ffn_resid_bwd{l - 1}", q=q)
        else:
            grad_x, _, dgpre, _ = _resid_norm_bwd(g1, dhs, s["xin"], g_pres, None, None, name="in_norm_bwd", q=q)
        gs["g_mix_pre"][l] = dgpre[0]
        if l == N_A:
            g_kv = dgpre[1][0]

    grads = {}
    for k in ("g_mix_pre", "g_mix_post", "g_ffn_pre", "g_ffn_post", "g_mem", "b_ffn_conv"):
        grads[k] = jnp.concatenate(gs[k], axis=0)
    grads["w_ffn_conv"] = jnp.stack(gs["w_ffn_conv"])
    for k, v in ga.items():
        grads[k] = jnp.stack(v)
    grads["sinks_b"] = jnp.stack(gsink)
    grads["g_kv"] = g_kv
    return jnp.sum(sq), grad_x.reshape(bsz, t, d), grads


N_CHIP = 4
HALF_ALIGN = 16
MIN_PART_BYTES = 128 * 1024


def _full_shape(kind, shard_shape):
    l, r, c = shard_shape
    return {"row": (l, N_CHIP * r, c), "col": (l, r, N_CHIP * c), "slot": (N_CHIP, l, r, c)}[kind]


def _slot_view(ref, kind, shard_shape, s, hf, sub=(0, 1)):
    _, r, c = shard_shape
    rh = r // 2
    if hf is None:
        start, size = 0, r
    else:
        size = rh // sub[1]
        start = hf * rh + sub[0] * size
    if kind == "row":
        start = s * r + start
    if not isinstance(start, int):
        start = pl.multiple_of(start, HALF_ALIGN)
    rows = pl.ds(start, size)
    if kind == "row":
        return ref.at[:, rows, :]
    if kind == "col":
        return ref.at[:, rows, pl.ds(s * c, c)]
    return ref.at[s, :, rows, :]


def _half_view(ref, shard_shape, hf, sub=(0, 1)):
    rh = shard_shape[1] // 2
    size = rh // sub[1]
    return ref.at[:, pl.ds(pl.multiple_of(hf * rh + sub[0] * size, HALF_ALIGN), size), :]


def _with_slot(kind, s, fn):
    if kind != "col" or isinstance(s, int):
        fn(s)
        return
    for k in range(N_CHIP):
        @pl.when(s == k)
        def _(k=k):
            fn(k)


def _mesh_pos():
    return lax.axis_index("x"), lax.axis_index("y"), lax.axis_index("c")


def _other_chips(x, y):
    return [(1 - x, y), (x, 1 - y), (1 - x, 1 - y)]


ICI_BYTES_PER_US = 6.0e4
ICI_GATHER_BYTES_PER_US = 5.5e4
D2D_BYTES_PER_US = 4.0e5


class _Chunk:
    def __init__(self, group, cost, ins, out_shapes, alias, n_sem, start, finish, done, buffer=None, bind=None):
        self.group, self.cost, self.ins, self.out_shapes, self.alias, self.n_sem = group, cost, ins, out_shapes, alias, n_sem
        self.start, self.finish, self.done = start, finish, done
        self.buffer = buffer
        self.bind = bind

    def prepare(self):
        if self.bind is not None:
            self.bind(self)


def _merged(chunks):
    groups, by_buffer = [], {}
    for ch in chunks:
        key = None if ch.buffer is None else (id(ch.buffer[0]), ch.buffer[1])
        if key is not None and key in by_buffer:
            by_buffer[key].append(ch)
        else:
            groups.append([ch])
            if key is not None:
                by_buffer[key] = groups[-1]
    out = []
    for parts in groups:
        if len(parts) == 1:
            out.append(parts[0])
            continue
        offs = [sum(p.n_sem for p in parts[:i]) for i in range(len(parts))]

        def run(phase, ins, outs, ss, rs, b, parts=parts, offs=offs):
            for p, o in zip(parts, offs):
                getattr(p, phase)(ins, outs, ss, rs, b + o)

        def done(outs, parts=parts):
            for p in parts:
                p.done(outs)

        first = parts[0]
        out.append(_Chunk(first.group, sum(p.cost for p in parts), first.ins, first.out_shapes, first.alias,
                          sum(p.n_sem for p in parts), functools.partial(run, "start"),
                          functools.partial(run, "finish"), done))
    return out


LAST_GROUP = 1 << 30
MIN_CARRIED_US = 8.0


class _CommQueue:
    def __init__(self):
        self.pending = []
        self.flushes = 0
        self.horizon = LAST_GROUP

    def push(self, chunk):
        self.pending.append(chunk)

    def take(self, budget_us):
        got, used = [], 0.0
        for ch in sorted(self.pending, key=lambda ch: (ch.group, -ch.cost)):
            if ch.group >= self.horizon and ch.group != LAST_GROUP:
                continue
            if used + ch.cost <= budget_us and not self._shares_buffer(ch, got):
                got.append(ch)
                used += ch.cost
        if used < MIN_CARRIED_US:
            return []
        return self._taken(got)

    @staticmethod
    def _shares_buffer(ch, others):
        return ch.buffer is not None and any(
            o.buffer is not None and o.buffer[0] is ch.buffer[0] and o.buffer[1] != ch.buffer[1] for o in others)

    def _taken(self, got):
        self.pending = [ch for ch in self.pending if ch not in got]
        for ch in got:
            ch.prepare()
        return _merged(got)

    def flush(self, group=LAST_GROUP):
        while True:
            chunks = []
            for ch in self.pending:
                if ch.group <= group and not self._shares_buffer(ch, chunks):
                    chunks.append(ch)
            if not chunks:
                return
            _run_chunks(self._taken(chunks), name=f"comm_flush{self.flushes}")
            self.flushes += 1


def _run_chunks(chunks, *, name):
    ins = [a for ch in chunks for a in ch.ins]
    outs = [s for ch in chunks for s in ch.out_shapes]
    alias, offs = {}, []
    i0 = o0 = s0 = 0
    for ch in chunks:
        offs.append((i0, o0, s0))
        for ci, co in ch.alias.items():
            alias[i0 + ci] = o0 + co
        i0 += len(ch.ins)
        o0 += len(ch.out_shapes)
        s0 += ch.n_sem

    def body(*refs):
        send_sems, recv_sems = refs[i0 + o0:]
        for phase in ("start", "finish"):
            for ch, (a, b, s) in zip(chunks, offs):
                getattr(ch, phase)(refs[a:a + len(ch.ins)], refs[i0 + b:i0 + b + len(ch.out_shapes)],
                                   send_sems, recv_sems, s)

    hbm = pl.BlockSpec(memory_space=pl.ANY)
    res = pl.pallas_call(
        body, in_specs=[hbm] * i0, out_specs=[hbm] * o0, out_shape=outs,
        scratch_shapes=[pltpu.SemaphoreType.DMA((s0,)), pltpu.SemaphoreType.DMA((s0,))],
        input_output_aliases=alias, name=name, compiler_params=pltpu.CompilerParams(has_side_effects=True))(*ins)
    for ch, (_, b, _) in zip(chunks, offs):
        ch.done(list(res[b:b + len(ch.out_shapes)]))


def _remote(src, dst, send_sems, recv_sems, k, dev):
    return pltpu.make_async_remote_copy(src_ref=src, dst_ref=dst, send_sem=send_sems.at[k], recv_sem=recv_sems.at[k],
                                        device_id=dev, device_id_type=MESH_T)


def _gather_chunks(q, group, kind, shard, l, ready):
    _, r, c = shard.shape
    shp = (1, r, c)
    rh = r // 2
    parts = max(p for p in (8, 4, 2, 1)
                if (rh // p) % HALF_ALIGN == 0 and (p == 1 or (rh // p) * c * shard.dtype.itemsize >= MIN_PART_BYTES))
    part_bytes = (rh // parts) * c * shard.dtype.itemsize
    full_type = jax.ShapeDtypeStruct(_full_shape(kind, shp), shard.dtype)
    state = {"full": None, "parts_done": 0}

    def bind_first(ch):
        ch.ins, ch.alias = ([shard], {}) if state["full"] is None else ([shard, state["full"]], {1: 0})

    def bind_full(ch):
        ch.ins = [state["full"]]

    def make_part(p):
        sub = (p, parts)

        def any_part(full):
            return _slot_view(full, kind, shp, 0, 0, sub)

        def start1(ins, outs, ss, rs, b):
            x, y, c_ = _mesh_pos()
            src, full = ins[0].at[pl.ds(l, 1)], outs[0]
            if p == 0:
                _with_slot(kind, 2 * x + y, lambda s: pltpu.make_async_copy(
                    src, _slot_view(full, kind, shp, s, None), ss.at[b + N_CHIP - 1]).start())
            for j, (ox, oy) in enumerate(_other_chips(x, y)):
                _with_slot(kind, 2 * x + y, lambda s, j=j, ox=ox, oy=oy: _remote(
                    _half_view(src, shp, c_, sub), _slot_view(full, kind, shp, s, c_, sub), ss, rs, b + j,
                    (ox, oy, c_)).start())

        def finish1(ins, outs, ss, rs, b):
            x, y, c_ = _mesh_pos()
            h = any_part(outs[0])
            for j in range(N_CHIP - 1):
                _remote(h, h, ss, rs, b + j, (x, y, 1 - c_)).wait()
            if p == 0:
                pltpu.make_async_copy(ins[0].at[pl.ds(l, 1)], _slot_view(outs[0], kind, shp, 0, None),
                                      ss.at[b + N_CHIP - 1]).wait()

        def start2(ins, outs, ss, rs, b):
            x, y, c_ = _mesh_pos()
            for j, (ox, oy) in enumerate(_other_chips(x, y)):
                def forward(s, j=j):
                    v = _slot_view(outs[0], kind, shp, s, c_, sub)
                    _remote(v, v, ss, rs, b + j, (x, y, 1 - c_)).start()
                _with_slot(kind, 2 * ox + oy, forward)

        def finish2(ins, outs, ss, rs, b):
            x, y, c_ = _mesh_pos()
            h = any_part(outs[0])
            for j in range(N_CHIP - 1):
                _remote(h, h, ss, rs, b + j, (x, y, 1 - c_)).wait()

        def done2(outs):
            state["full"] = outs[0]
            state["parts_done"] += 1
            if state["parts_done"] == parts:
                ready(outs[0])

        def done1(outs):
            state["full"] = outs[0]
            q.push(_Chunk(group, 3 * part_bytes / D2D_BYTES_PER_US, None, [full_type], {0: 0}, N_CHIP - 1,
                          start2, finish2, done2, buffer=(state, 2), bind=bind_full))

        return _Chunk(group, 3 * part_bytes / ICI_GATHER_BYTES_PER_US, None, [full_type], None,
                      N_CHIP if p == 0 else N_CHIP - 1, start1, finish1, done1, buffer=(state, 1), bind=bind_first)

    for p in range(parts):
        q.push(make_part(p))


def _reduce_scatter_chunks(q, kind, grad, shard_shape, pos, name, ready):
    _, r, c = shard_shape
    shp = (1, r, c)
    rh = r // 2

    def start1(ins, outs, ss, rs, b):
        x, y, c_ = _mesh_pos()
        for s in range(N_CHIP):
            _remote(_slot_view(ins[0], kind, shp, s, 1 - c_), outs[0].at[s], ss, rs, b + s, (x, y, 1 - c_)).start()

    def finish1(ins, outs, ss, rs, b):
        x, y, c_ = _mesh_pos()
        for s in range(N_CHIP):
            _remote(outs[0].at[s], outs[0].at[s], ss, rs, b + s, (x, y, 1 - c_)).wait()

    def start2(ins, outs, ss, rs, b):
        x, y, c_ = _mesh_pos()
        for j, (ox, oy) in enumerate(_other_chips(x, y)):
            _remote(ins[0].at[2 * ox + oy], outs[0].at[j], ss, rs, b + j, (ox, oy, c_)).start()

    def finish2(ins, outs, ss, rs, b):
        x, y, c_ = _mesh_pos()
        for j in range(N_CHIP - 1):
            _remote(outs[0].at[j], outs[0].at[j], ss, rs, b + j, (x, y, 1 - c_)).wait()

    def start3(ins, outs, ss, rs, b):
        x, y, c_ = _mesh_pos()
        v = _half_view(outs[0], shp, c_)
        _remote(v, v, ss, rs, b, (x, y, 1 - c_)).start()

    def finish3(ins, outs, ss, rs, b):
        x, y, c_ = _mesh_pos()
        v = _half_view(outs[0], shp, c_)
        _remote(v, v, ss, rs, b, (x, y, 1 - c_)).wait()

    def done2(pair, outs):
        half = _rs_chip_add(pair, outs[0], shp, pos, name=f"rs_chip_add_{name}")
        q.push(_Chunk(LAST_GROUP, rh * c * 4 / D2D_BYTES_PER_US, [half], [jax.ShapeDtypeStruct(half.shape, half.dtype)],
                      {0: 0}, 1, start3, finish3, lambda o: ready(o[0])))

    def done1(outs):
        pair, wire = _rs_pair_add(grad, outs[0], kind, shp, pos, name=f"rs_pair_add_{name}")
        q.push(_Chunk(LAST_GROUP, 3 * rh * c * wire.dtype.itemsize / ICI_BYTES_PER_US, [wire],
                      [jax.ShapeDtypeStruct((N_CHIP - 1, 1, rh, c), wire.dtype)], {}, N_CHIP - 1,
                      start2, finish2, functools.partial(done2, pair)))

    q.push(_Chunk(LAST_GROUP, N_CHIP * rh * c * 4 / D2D_BYTES_PER_US, [grad],
                  [jax.ShapeDtypeStruct((N_CHIP, 1, rh, c), F32)], {}, N_CHIP, start1, finish1, done1))


def _allgather8(vec, *, name):
    r = vec.shape[0]
    n_dev = 8

    def body(v_ref, buf, send_sems, recv_sems):
        x, y, c = _mesh_pos()
        me = 4 * x + 2 * y + c
        copies = []
        for k in range(1, n_dev):
            kx, ky, kc = (k >> 2) & 1, (k >> 1) & 1, k & 1
            peer = ((1 - x) if kx else x, (1 - y) if ky else y, (1 - c) if kc else c)
            cp = _remote(v_ref, buf.at[me], send_sems, recv_sems, k - 1, peer)
            cp.start()
            copies.append(cp)
        buf[me] = v_ref[...]
        for cp in copies:
            cp.wait()

    vm = pl.BlockSpec(memory_space=pltpu.VMEM)
    return pl.pallas_call(
        body, in_specs=[vm], out_specs=vm, out_shape=jax.ShapeDtypeStruct((n_dev, r, LANE), F32),
        scratch_shapes=[pltpu.SemaphoreType.DMA((n_dev - 1,)), pltpu.SemaphoreType.DMA((n_dev - 1,))],
        name=name, compiler_params=pltpu.CompilerParams(has_side_effects=True, vmem_limit_bytes=VMEM_LIMIT_V7X))(vec)


def _allreduce8(vec, *, name):
    r = vec.shape[0]
    rh = r // 2

    def body(v_ref, o_ref, sib_ref, chips_ref, send_sems, recv_sems):
        x, y, c = _mesh_pos()
        sib = (x, y, 1 - c)
        me = 2 * x + y
        pair = _remote(v_ref, sib_ref, send_sems, recv_sems, 0, sib)
        pair.start()
        pair.wait()
        rows = pl.ds(pl.multiple_of(c * rh, SUBLANE), rh)
        chips_ref[me] = v_ref[rows, :] + sib_ref[rows, :]
        copies = []
        for j, (ox, oy) in enumerate(_other_chips(x, y)):
            cp = _remote(chips_ref.at[me], chips_ref.at[me], send_sems, recv_sems, 1 + j, (ox, oy, c))
            cp.start()
            copies.append(cp)
        for cp in copies:
            cp.wait()
        acc = chips_ref[0]
        for s in range(1, N_CHIP):
            acc = acc + chips_ref[s]
        o_ref[rows, :] = acc
        swap = _remote(o_ref.at[rows, :], o_ref.at[rows, :], send_sems, recv_sems, N_CHIP, sib)
        swap.start()
        swap.wait()

    vm = pl.BlockSpec(memory_space=pltpu.VMEM)
    return pl.pallas_call(
        body, in_specs=[vm], out_specs=vm, out_shape=jax.ShapeDtypeStruct((r, LANE), F32),
        scratch_shapes=[pltpu.VMEM((r, LANE), F32), pltpu.VMEM((N_CHIP, rh, LANE), F32),
                        pltpu.SemaphoreType.DMA((N_CHIP + 1,)), pltpu.SemaphoreType.DMA((N_CHIP + 1,))],
        name=name, compiler_params=pltpu.CompilerParams(has_side_effects=True, vmem_limit_bytes=VMEM_LIMIT_V7X))(vec)


def _rs_pair_add(g, recv, kind, shape, pos, *, name):
    l, r, c = shape
    rh = r // 2
    if kind == "row":
        gspec = pl.BlockSpec((None, rh, c), lambda s, i, pos: (i, 2 * s + pos[0], 0))
    elif kind == "col":
        gspec = pl.BlockSpec((None, rh, c), lambda s, i, pos: (i, pos[0], s))
    else:
        gspec = pl.BlockSpec((None, None, rh, c), lambda s, i, pos: (s, i, pos[0], 0))
    pspec = pl.BlockSpec((None, None, rh, c), lambda s, i, pos: (s, i, 0, 0))

    def body(pos_ref, g_ref, r_ref, p_ref, pw_ref):
        del pos_ref
        v = g_ref[...] + r_ref[...]
        p_ref[...] = v
        pw_ref[...] = v.astype(pw_ref.dtype)

    return pl.pallas_call(
        body,
        grid_spec=pltpu.PrefetchScalarGridSpec(
            num_scalar_prefetch=1, grid=(N_CHIP, l), in_specs=[gspec, pspec], out_specs=[pspec, pspec]),
        out_shape=[jax.ShapeDtypeStruct((N_CHIP, l, rh, c), F32), jax.ShapeDtypeStruct((N_CHIP, l, rh, c), MXU)],
        name=name, compiler_params=_cp((PAR, PAR)))(pos, g, recv)


def _rs_chip_add(p, recv, shape, pos, *, name):
    l, r, c = shape
    rh = r // 2

    def body(pos_ref, p_ref, r_ref, o_ref):
        del pos_ref
        acc = p_ref[...]
        for j in range(N_CHIP - 1):
            acc = acc + r_ref[j].astype(F32)
        o_ref[...] = acc

    return pl.pallas_call(
        body,
        grid_spec=pltpu.PrefetchScalarGridSpec(
            num_scalar_prefetch=1, grid=(l,),
            in_specs=[pl.BlockSpec((None, None, rh, c), lambda i, pos: (pos[1], i, 0, 0)),
                      pl.BlockSpec((N_CHIP - 1, None, rh, c), lambda i, pos: (0, i, 0, 0))],
            out_specs=pl.BlockSpec((None, rh, c), lambda i, pos: (i, pos[0], 0))),
        out_shape=jax.ShapeDtypeStruct((l, r, c), F32),
        name=name, compiler_params=_cp((PAR,)))(pos, p, recv)


ADAM_BLOCK_ELEMS = 384 * 1024


def _adam_math(w, g, m, v):
    c1 = 1.0 / (1.0 - ADAM_B1 ** ADAM_STEP)
    c2 = 1.0 / (1.0 - ADAM_B2 ** ADAM_STEP)
    nm = ADAM_B1 * m + (1.0 - ADAM_B1) * g
    nv = ADAM_B2 * v + (1.0 - ADAM_B2) * (g * g)
    return -ADAM_LR * ((nm * c1) / (jnp.sqrt(nv * c2) + ADAM_EPS) + ADAM_WD * w), nm, nv


def _adamw_layer(w, g, m, v, outs, l, *, name):
    _, r, c = w.shape
    tr = _tile(r, max(SUBLANE, ADAM_BLOCK_ELEMS // c // SUBLANE * SUBLANE), SUBLANE)

    def body(w_ref, g_ref, m_ref, v_ref, *rest):
        go_ref, d_ref, nm_ref, nv_ref = rest[4:]
        gg = g_ref[...]
        go_ref[...] = gg
        d_ref[...], nm_ref[...], nv_ref[...] = _adam_math(w_ref[...], gg, m_ref[...], v_ref[...])

    lay = pl.BlockSpec((None, tr, c), lambda j: (l, j, 0))
    hbm = pl.BlockSpec(memory_space=pl.ANY)
    return pl.pallas_call(
        body, grid=(r // tr,),
        in_specs=[lay, pl.BlockSpec((None, tr, c), lambda j: (0, j, 0)), lay, lay] + [hbm] * 4,
        out_specs=[lay] * 4, out_shape=[jax.ShapeDtypeStruct(w.shape, F32)] * 4,
        input_output_aliases={4 + i: i for i in range(4)},
        name=name, compiler_params=_cp((PAR,)))(w, g, m, v, *outs)


def _adamw(w, g, m, v, *, name):
    shape = w.shape
    if w.ndim == 2:
        w, g, m, v = (a[None] for a in (w, g, m, v))
    l, r, c = w.shape
    tr = _tile(r, max(SUBLANE, ADAM_BLOCK_ELEMS // c // SUBLANE * SUBLANE), SUBLANE)

    def body(w_ref, g_ref, m_ref, v_ref, d_ref, nm_ref, nv_ref):
        d_ref[...], nm_ref[...], nv_ref[...] = _adam_math(w_ref[...], g_ref[...], m_ref[...], v_ref[...])

    spec = pl.BlockSpec((None, tr, c), lambda i, j: (i, j, 0))
    outs = pl.pallas_call(
        body, grid=(l, r // tr), in_specs=[spec] * 4, out_specs=[spec] * 3,
        out_shape=[jax.ShapeDtypeStruct((l, r, c), F32)] * 3,
        name=name, compiler_params=_cp((PAR, PAR)))(w, g, m, v)
    return tuple(o.reshape(shape) for o in outs)


PACK_ROWS = 512 * LANE


def _pack(arrays):
    flat = jnp.concatenate([a.reshape(-1).astype(F32) for a in arrays])
    pad = (-flat.shape[0]) % PACK_ROWS
    return jnp.pad(flat, (0, pad)).reshape(-1, LANE)


def _unpack(packed, shapes):
    flat = packed.reshape(-1)
    out, off = [], 0
    for s in shapes:
        size = int(np.prod(s))
        out.append(flat[off:off + size].reshape(s))
        off += size
    return out


BIG = (("w_mem_kv", "row"), ("w_mix_out", "row"), ("w_ffn_up", "slot"), ("w_ffn_down", "row"),
       ("w_in_a", "slot"), ("w_in_b", "row"), ("w_kv", "row"))
COLUMN_SHARDED_AS_COLUMNS = ("w_in_a",)
SMALL_SHARDED = (("w_ffn_conv", 2), ("w_conv_a", 2), ("b_conv_a", 1), ("lru_lambda", 1))
SMALL_REPLICATED = ("g_mix_pre", "g_mix_post", "g_ffn_pre", "g_ffn_post", "g_mem", "b_ffn_conv",
                    "w_rg_r", "b_rg_r", "w_rg_i", "b_rg_i", "sinks_b", "g_kv")
WEIGHTS = ("g_mix_pre", "g_mix_post", "g_ffn_pre", "g_ffn_post", "g_mem", "w_mem_kv", "w_mix_out", "w_ffn_up",
           "w_ffn_conv", "b_ffn_conv", "w_ffn_down", "w_in_a", "w_conv_a", "b_conv_a", "w_rg_r", "b_rg_r", "w_rg_i",
           "b_rg_i", "lru_lambda", "w_in_b", "sinks_b", "g_kv", "w_kv")


def _slot_to_cols(a):
    s, l, r, c = a.shape
    return a.transpose(1, 2, 0, 3).reshape(l, r, s * c)


def _cols_to_slot(a):
    l, r, c4 = a.shape
    return a.reshape(l, r, N_CHIP, c4 // N_CHIP).transpose(2, 0, 1, 3)


GROUPS_PER_LAYER = 8


def _layer_weights(layer):
    names = [("w_mem_kv", layer), ("w_in_a", layer) if layer < N_A else ("w_in_b", layer - N_A)]
    if layer == N_A:
        names.append(("w_kv", 0))
    return names + [("w_mix_out", layer), ("w_ffn_up", layer), ("w_ffn_down", layer)]


def _train_step(x, mem, target, w, m, v):
    xi, yi, ci = _mesh_pos()
    chip = 2 * xi + yi
    pos = jnp.stack([ci, chip]).astype(jnp.int32)

    q = _CommQueue()
    kinds = dict(BIG)
    as3 = lambda a: a if a.ndim == 3 else a[None]
    w3, m3, v3 = ({k: as3(d[k]) for k, _ in BIG} for d in (w, m, v))
    shards = {k: w3[k].astype(MXU) for k, _ in BIG}

    gathered = {}

    def on_gathered(k, l, full):
        gathered[k, l] = _slot_to_cols(full) if k in COLUMN_SHARDED_AS_COLUMNS else full

    group_of = {}

    for layer in range(DEPTH):
        for i, (k, l) in enumerate(_layer_weights(layer)):
            group_of[k, l] = layer * GROUPS_PER_LAYER + i
            _gather_chunks(q, group_of[k, l], kinds[k], shards[k], l, functools.partial(on_gathered, k, l))

    def wfull(k, l):
        if (k, l) not in gathered:
            q.flush(group_of[k, l])
        return gathered[k, l]

    q.flush(1)

    big_out = {k: [lax.empty(w3[k].shape, F32) for _ in range(4)] for k, _ in BIG}

    def on_reduced(k, l, g):
        big_out[k] = _adamw_layer(w3[k], g, m3[k], v3[k], big_out[k], l, name=f"adamw_{k}{l}")

    def push_grad(k, l, g):
        if k in COLUMN_SHARDED_AS_COLUMNS:
            g = _cols_to_slot(g)
        _reduce_scatter_chunks(q, kinds[k], g, (1,) + w3[k].shape[1:], pos, f"{k}{l}", functools.partial(on_reduced, k, l))

    small_shapes = [w[k].shape for k, _ in SMALL_SHARDED]
    stacked = _allgather8(_pack([w[k] for k, _ in SMALL_SHARDED]), name="gather_small")
    per_chip = [_unpack(stacked[2 * s], small_shapes) for s in range(N_CHIP)]
    p = {k: w[k] for k in SMALL_REPLICATED}
    for i, (k, axis) in enumerate(SMALL_SHARDED):
        p[k] = jnp.concatenate([per_chip[s][i] for s in range(N_CHIP)], axis=axis)

    sq, grad_x, g = _local_step(x, mem, target, p, wfull, push_grad, q)
    loss = lax.psum(0.5 * sq / D_MODEL, ("x", "y", "c"))
    q.flush()

    small_names = [k for k, _ in SMALL_SHARDED] + list(SMALL_REPLICATED)
    summed = _allreduce8(_pack([g[k] for k in small_names]), name="allreduce_small")
    gsum = dict(zip(small_names, _unpack(summed, [p[k].shape for k in small_names])))
    for k, axis in SMALL_SHARDED:
        gsum[k] = lax.dynamic_slice_in_dim(gsum[k], chip * w[k].shape[axis], w[k].shape[axis], axis)

    delta, new_m, new_v = {}, {}, {}
    for k, _ in BIG:
        gsum[k], delta[k], new_m[k], new_v[k] = (o.reshape(w[k].shape) for o in big_out[k])
    packed = [_pack([d[k] for k in small_names]) for d in (w, gsum, m, v)]
    outs = _adamw(*packed, name="adamw_small")
    for d, o in zip((delta, new_m, new_v), outs):
        d.update(zip(small_names, _unpack(o, [w[k].shape for k in small_names])))
    return (loss, grad_x, *[gsum[k] for k in WEIGHTS], *[delta[k] for k in WEIGHTS],
            *[new_m[k] for k in WEIGHTS], *[new_v[k] for k in WEIGHTS])


def kernel(x, mem, g_mix_pre, g_mix_post, g_ffn_pre, g_ffn_post, g_mem, w_mem_kv, w_mix_out, w_ffn_up, w_ffn_conv, b_ffn_conv, w_ffn_down, w_in_a, w_conv_a, b_conv_a, w_rg_r, b_rg_r, w_rg_i, b_rg_i, lru_lambda, w_in_b, sinks_b, g_kv, w_kv, loss_target, m_g_mix_pre, m_g_mix_post, m_g_ffn_pre, m_g_ffn_post, m_g_mem, m_w_mem_kv, m_w_mix_out, m_w_ffn_up, m_w_ffn_conv, m_b_ffn_conv, m_w_ffn_down, m_w_in_a, m_w_conv_a, m_b_conv_a, m_w_rg_r, m_b_rg_r, m_w_rg_i, m_b_rg_i, m_lru_lambda, m_w_in_b, m_sinks_b, m_g_kv, m_w_kv, v_g_mix_pre, v_g_mix_post, v_g_ffn_pre, v_g_ffn_post, v_g_mem, v_w_mem_kv, v_w_mix_out, v_w_ffn_up, v_w_ffn_conv, v_b_ffn_conv, v_w_ffn_down, v_w_in_a, v_w_conv_a, v_b_conv_a, v_w_rg_r, v_b_rg_r, v_w_rg_i, v_b_rg_i, v_lru_lambda, v_w_in_b, v_sinks_b, v_g_kv, v_w_kv):
    args = (g_mix_pre, g_mix_post, g_ffn_pre, g_ffn_post, g_mem, w_mem_kv, w_mix_out, w_ffn_up, w_ffn_conv, b_ffn_conv, w_ffn_down, w_in_a, w_conv_a, b_conv_a, w_rg_r, b_rg_r, w_rg_i, b_rg_i, lru_lambda, w_in_b, sinks_b, g_kv, w_kv)
    ms = (m_g_mix_pre, m_g_mix_post, m_g_ffn_pre, m_g_ffn_post, m_g_mem, m_w_mem_kv, m_w_mix_out, m_w_ffn_up, m_w_ffn_conv, m_b_ffn_conv, m_w_ffn_down, m_w_in_a, m_w_conv_a, m_b_conv_a, m_w_rg_r, m_b_rg_r, m_w_rg_i, m_b_rg_i, m_lru_lambda, m_w_in_b, m_sinks_b, m_g_kv, m_w_kv)
    vs = (v_g_mix_pre, v_g_mix_post, v_g_ffn_pre, v_g_ffn_post, v_g_mem, v_w_mem_kv, v_w_mix_out, v_w_ffn_up, v_w_ffn_conv, v_b_ffn_conv, v_w_ffn_down, v_w_in_a, v_w_conv_a, v_b_conv_a, v_w_rg_r, v_b_rg_r, v_w_rg_i, v_b_rg_i, v_lru_lambda, v_w_in_b, v_sinks_b, v_g_kv, v_w_kv)
    return _train_step(x, mem, loss_target, dict(zip(WEIGHTS, args)), dict(zip(WEIGHTS, ms)), dict(zip(WEIGHTS, vs)))
```

```python
import functools
import math

import numpy as np
import jax
import jax.numpy as jnp
from jax import lax
from jax.experimental import pallas as pl
from jax.experimental.pallas import tpu as pltpu

F32 = jnp.float32
MXU = jnp.bfloat16

D_MODEL = 1024
HEAD_DIM = 64
MEM_LEN = 256
MEM_HEADS = 4
MEM_WIDTH = MEM_HEADS * HEAD_DIM
MIX_WIDTH = D_MODEL - MEM_WIDTH
LRU_BLOCKS = MIX_WIDTH // HEAD_DIM
LRU_CONV = 4
LRU_C = 8.0
SWA_HEADS = MIX_WIDTH // HEAD_DIM
SWA_KV_HEADS = 4
SWA_GROUP = SWA_HEADS // SWA_KV_HEADS
WINDOW = 128
D_FF = 2816
FFN_CONV = 3
EPS = 1e-6
DEPTH = 4
N_A = 2

ADAM_LR = 0.001
ADAM_B1 = 0.9
ADAM_B2 = 0.999
ADAM_EPS = 1e-08
ADAM_WD = 0.01
ADAM_STEP = 10

VMEM_LIMIT_V7X = 56 * 1024 * 1024
LANE = 128
SUBLANE = 8
GATE_TILE = 256
MESH_T = pl.DeviceIdType.MESH


def _alibi_slopes(n):
    def pow2_slopes(m):
        start = 2.0 ** (-8.0 / m)
        return [start ** (i + 1) for i in range(m)]
    c = 2 ** int(math.floor(math.log2(n)))
    s = pow2_slopes(c)
    if c != n:
        s = s + pow2_slopes(2 * c)[0::2][: n - c]
    return [float(np.float32(v)) for v in s]


SLOPES = _alibi_slopes(SWA_HEADS)


def _tile(n, cap, mult=LANE):
    best = None
    for t in range(mult, min(n, cap) + 1, mult):
        if n % t == 0:
            best = t
    return best if best is not None else n


def _cp(sem):
    return pltpu.CompilerParams(dimension_semantics=sem, vmem_limit_bytes=VMEM_LIMIT_V7X)


MM_VMEM_BUDGET = 40 * 1024 * 1024
HBM_BYTES_PER_US_V7X = 3.0e6
GRID_STEP_US = 0.35


def _divisors(n, mult):
    return [t for t in range(mult, n + 1, mult) if n % t == 0] or [n]


def _mm_tiles(m, k, n, out_bytes):
    best = None
    for tm in _divisors(m, 256):
        for tn in _divisors(n, LANE):
            vmem = 2 * (tm * k * 2 + k * tn * 2 + tm * tn * out_bytes)
            if vmem > MM_VMEM_BUDGET:
                continue
            steps = (m // tm) * (n // tn)
            b_reads = 1 if tn == n else m // tm
            traffic = m * k * 2 + k * n * 2 * b_reads + m * n * out_bytes
            first = tm * k * 2 + k * tn * 2
            cost = (traffic + first) / HBM_BYTES_PER_US_V7X + steps * GRID_STEP_US
            if best is None or cost < best[0]:
                best = (cost, tm, tn)
    return best[1], best[2]


def _mm_tn_tiles(k, m, n, whole_n=False):
    best = None
    for tm in _divisors(m, LANE):
        for tn in ([n] if whole_n else _divisors(n, LANE)):
            for tk in _divisors(k, 512):
                vmem = 2 * (tk * tm * 2 + tk * tn * 2 + tm * tn * 4)
                if vmem > MM_VMEM_BUDGET:
                    continue
                steps = (m // tm) * (n // tn) * (k // tk)
                traffic = k * m * 2 * (n // tn) + k * n * 2 * (m // tm) + m * n * 4
                cost = traffic / HBM_BYTES_PER_US_V7X + steps * GRID_STEP_US
                if best is None or cost < best[0]:
                    best = (cost, tk, tm, tn)
    return best[1], best[2], best[3]


ARB = "arbitrary"
PAR = "parallel"


def _rms_fwd(x, g):
    r = lax.rsqrt(jnp.mean(x * x, axis=-1, keepdims=True) + EPS)
    return x * r * g


def _rms_bwd(dy, x, g):
    r = lax.rsqrt(jnp.mean(x * x, axis=-1, keepdims=True) + EPS)
    xh = x * r
    gdy = dy * g
    dx = r * (gdy - xh * jnp.mean(gdy * xh, axis=-1, keepdims=True))
    dg = jnp.sum(dy * xh, axis=0, keepdims=True)
    return dx, dg


_GELU_K = math.sqrt(2.0 / math.pi)
_GELU_C = 0.044715


def _gelu(x):
    t = jnp.tanh(_GELU_K * (x + _GELU_C * x * x * x))
    return 0.5 * x * (1.0 + t)


def _gelu_and_grad(x):
    x2 = x * x
    t = jnp.tanh(_GELU_K * (x + _GELU_C * x2 * x))
    g = 0.5 * x * (1.0 + t)
    dg = 0.5 * (1.0 + t) + 0.5 * x * (1.0 - t * t) * (_GELU_K * (1.0 + 3.0 * _GELU_C * x2))
    return g, dg


def _shift_down(x, k, row):
    return jnp.where(row >= k, pltpu.roll(x, k, axis=0), 0.0)


def _shift_up(x, k, row):
    n = x.shape[0]
    return jnp.where(row < n - k, pltpu.roll(x, n - k, axis=0), 0.0)


def _shift_down_edge(x, k):
    r = pltpu.roll(x, k, axis=0)
    row = lax.broadcasted_iota(jnp.int32, (SUBLANE, x.shape[1]), 0)
    return jnp.concatenate([jnp.where(row >= k, r[:SUBLANE], 0.0), r[SUBLANE:]], axis=0)


def _shift_up_edge(x, k):
    n = x.shape[0]
    r = pltpu.roll(x, n - k, axis=0)
    row = lax.broadcasted_iota(jnp.int32, (SUBLANE, x.shape[1]), 0)
    return jnp.concatenate([r[:n - SUBLANE], jnp.where(row < SUBLANE - k, r[n - SUBLANE:], 0.0)], axis=0)


def _dot(a, b):
    return jnp.dot(a, b, preferred_element_type=F32)


def _dot_nt(a, b):
    return lax.dot_general(a, b, (((1,), (1,)), ((), ())), preferred_element_type=F32)


def _dot_tn(a, b):
    return lax.dot_general(a, b, (((0,), (0,)), ((), ())), preferred_element_type=F32)


MXU_FLOPS_PER_US = 7.0e8
HOST_US = {"lru_fwd": 44.0, "lru_bwd": 94.0, "swa_fwd": 60.0, "swa_bwd": 160.0, "mem_attn_fwd": 21.0,
           "mem_attn_bwd": 33.0, "ffn_act": 70.0, "ffn_act_bwd": 100.0, "resid": 22.0, "resid_bwd": 33.0}


def _hosted_call(body, *, grid, in_specs, out_specs, out_shape, args, name, aliases=None, scratch_shapes=(),
                 q=None, flops=0.0, budget_us=0.0):
    chunks = q.take(flops / MXU_FLOPS_PER_US + budget_us) if q is not None else []
    if not chunks:
        return pl.pallas_call(
            body, grid=grid, in_specs=in_specs, out_specs=out_specs, out_shape=out_shape,
            scratch_shapes=list(scratch_shapes), input_output_aliases=aliases or {}, name=name,
            compiler_params=_cp((ARB,) * len(grid)))(*args)
    single = not isinstance(out_shape, (list, tuple))
    o_shapes = [out_shape] if single else list(out_shape)
    o_specs = [out_specs] if single else list(out_specs)
    n_in, n_out, n_scr = len(args), len(o_shapes), len(scratch_shapes)
    c_ins = [a for ch in chunks for a in ch.ins]
    c_outs = [s for ch in chunks for s in ch.out_shapes]
    alias = dict(aliases or {})
    in_off, out_off, sem_off = [], [], []
    i0 = o0 = s0 = 0
    for ch in chunks:
        in_off.append(i0)
        out_off.append(o0)
        sem_off.append(s0)
        for ci, co in ch.alias.items():
            alias[n_in + i0 + ci] = n_out + o0 + co
        i0 += len(ch.ins)
        o0 += len(ch.out_shapes)
        s0 += ch.n_sem

    def wrapped(*refs):
        ins = refs[:n_in]
        cin = refs[n_in:n_in + i0]
        outs = refs[n_in + i0:n_in + i0 + n_out]
        cout = refs[n_in + i0 + n_out:n_in + i0 + n_out + o0]
        scr = refs[n_in + i0 + n_out + o0:n_in + i0 + n_out + o0 + n_scr]
        send_sems, recv_sems = refs[n_in + i0 + n_out + o0 + n_scr:]
        first = functools.reduce(lambda u, v: u & v, [pl.program_id(d) == 0 for d in range(len(grid))])
        last = functools.reduce(lambda u, v: u & v, [pl.program_id(d) == grid[d] - 1 for d in range(len(grid))])

        def each(phase):
            for ch, a, b, s in zip(chunks, in_off, out_off, sem_off):
                getattr(ch, phase)(cin[a:a + len(ch.ins)], cout[b:b + len(ch.out_shapes)], send_sems, recv_sems, s)

        pl.when(first)(lambda: each("start"))
        body(*ins, *outs, *scr)
        pl.when(last)(lambda: each("finish"))

    hbm = pl.BlockSpec(memory_space=pl.ANY)
    res = pl.pallas_call(
        wrapped, grid=grid, in_specs=list(in_specs) + [hbm] * i0, out_specs=o_specs + [hbm] * o0,
        out_shape=o_shapes + c_outs,
        scratch_shapes=list(scratch_shapes) + [pltpu.SemaphoreType.DMA((s0,)), pltpu.SemaphoreType.DMA((s0,))],
        input_output_aliases=alias, name=name,
        compiler_params=pltpu.CompilerParams(dimension_semantics=(ARB,) * len(grid), vmem_limit_bytes=VMEM_LIMIT_V7X,
                                             has_side_effects=True))(*args, *c_ins)
    for ch, b in zip(chunks, out_off):
        ch.done(list(res[n_out + b:n_out + b + len(ch.out_shapes)]))
    return res[0] if single else list(res[:n_out])


def _mm_nn(a, b, *, name, q=None, out_dtype=F32):
    m, k = a.shape
    n = b.shape[-1]
    tm, tn = _mm_tiles(m, k, n, jnp.dtype(out_dtype).itemsize)

    def body(a_ref, b_ref, o_ref):
        o_ref[...] = _dot(a_ref[...], b_ref[...]).astype(o_ref.dtype)

    return _hosted_call(
        body, grid=(m // tm, n // tn),
        in_specs=[pl.BlockSpec((tm, k), lambda i, j: (i, 0)),
                  pl.BlockSpec((None, k, tn), lambda i, j: (0, 0, j))],
        out_specs=pl.BlockSpec((tm, tn), lambda i, j: (i, j)),
        out_shape=jax.ShapeDtypeStruct((m, n), out_dtype),
        args=(a, b), name=name, q=q, flops=2.0 * m * k * n)


def _mm_nt(a, b, *, name, q=None, out_dtype=F32):
    m, k = a.shape
    n = b.shape[-2]
    tm, tn = _mm_tiles(m, k, n, jnp.dtype(out_dtype).itemsize)

    def body(a_ref, b_ref, o_ref):
        o_ref[...] = _dot_nt(a_ref[...], b_ref[...]).astype(o_ref.dtype)

    return _hosted_call(
        body, grid=(m // tm, n // tn),
        in_specs=[pl.BlockSpec((tm, k), lambda i, j: (i, 0)),
                  pl.BlockSpec((None, tn, k), lambda i, j: (0, j, 0))],
        out_specs=pl.BlockSpec((tm, tn), lambda i, j: (i, j)),
        out_shape=jax.ShapeDtypeStruct((m, n), out_dtype),
        args=(a, b), name=name, q=q, flops=2.0 * m * k * n)


def _mm_nn_slots(a, b4, *, name, q=None, out_dtype=F32):
    m, k = a.shape
    s_, _, _, c = b4.shape
    ob = jnp.dtype(out_dtype).itemsize
    tm = max(t for t in _divisors(m, 256) if 2 * (t * k * 2 + k * c * 2 + t * c * ob) <= MM_VMEM_BUDGET)

    def body(a_ref, b_ref, o_ref):
        o_ref[...] = _dot(a_ref[...], b_ref[...]).astype(o_ref.dtype)

    return _hosted_call(
        body, grid=(m // tm, s_),
        in_specs=[pl.BlockSpec((tm, k), lambda i, j: (i, 0)),
                  pl.BlockSpec((None, None, k, c), lambda i, j: (j, 0, 0, 0))],
        out_specs=pl.BlockSpec((tm, c), lambda i, j: (i, j)),
        out_shape=jax.ShapeDtypeStruct((m, s_ * c), out_dtype),
        args=(a, b4), name=name, q=q, flops=2.0 * m * k * s_ * c)


def _mm_tn_slots(a, b, *, name, slot_cols, n_slots, first_slot=0, q=None, out=None):
    k, m = a.shape
    c = slot_cols
    tk, tm, _ = _mm_tn_tiles(k, m, c, whole_n=True)

    def body(a_ref, b_ref, *rest):
        o_ref = rest[-1]
        part = _dot_tn(a_ref[...], b_ref[...])

        @pl.when(pl.program_id(2) == 0)
        def _():
            o_ref[...] = part

        @pl.when(pl.program_id(2) > 0)
        def _():
            o_ref[...] += part

    in_specs = [pl.BlockSpec((tk, tm), lambda i, j, s: (s, i)), pl.BlockSpec((tk, c), lambda i, j, s: (s, j))]
    args = (a, b)
    if out is not None:
        in_specs.append(pl.BlockSpec(memory_space=pl.ANY))
        args = (a, b, out)
    return _hosted_call(
        body, grid=(m // tm, b.shape[-1] // c, k // tk), in_specs=in_specs,
        out_specs=pl.BlockSpec((None, None, tm, c), lambda i, j, s: (first_slot + j, 0, i, 0)),
        out_shape=jax.ShapeDtypeStruct((n_slots, 1, m, c), F32),
        aliases={2: 0} if out is not None else None,
        args=args, name=name, q=q, flops=2.0 * m * k * b.shape[-1])


def _mm_tn(a, b, *, name, q=None, out=None, n_total=None, col_block_offset=0):
    k, m = a.shape
    n = b.shape[-1]
    tk, tm, tn = _mm_tn_tiles(k, m, n)
    off = col_block_offset * (n // tn)

    def body(a_ref, b_ref, *rest):
        o_ref = rest[-1]
        part = _dot_tn(a_ref[...], b_ref[...])

        @pl.when(pl.program_id(2) == 0)
        def _():
            o_ref[...] = part

        @pl.when(pl.program_id(2) > 0)
        def _():
            o_ref[...] += part

    in_specs = [pl.BlockSpec((tk, tm), lambda i, j, s: (s, i)), pl.BlockSpec((tk, tn), lambda i, j, s: (s, j))]
    args = (a, b)
    if out is not None:
        in_specs.append(pl.BlockSpec(memory_space=pl.ANY))
        args = (a, b, out)
    return _hosted_call(
        body, grid=(m // tm, n // tn, k // tk), in_specs=in_specs,
        out_specs=pl.BlockSpec((None, tm, tn), lambda i, j, s: (0, i, j + off)),
        out_shape=jax.ShapeDtypeStruct((1, m, n_total or n), F32),
        aliases={2: 0} if out is not None else None,
        args=args, name=name, q=q, flops=2.0 * m * k * n)


def _mm_ffn_dh(dg, dv, w4, *, name, q=None):
    m, f = dg.shape
    n_slots, _, d, c = w4.shape
    tm, tn = _mm_tiles(m, 2 * f, d, 4)

    def body(dg_ref, dv_ref, *rest):
        w_refs, o_ref = rest[:n_slots], rest[n_slots]
        acc = None
        for s, w_ref in enumerate(w_refs):
            x_ref = dg_ref if s < n_slots // 2 else dv_ref
            off = (s % (n_slots // 2)) * c
            part = _dot_nt(x_ref[:, off:off + c], w_ref[...])
            acc = part if acc is None else acc + part
        o_ref[...] = acc.astype(o_ref.dtype)

    wspec = lambda s: pl.BlockSpec((None, None, tn, c), lambda i, j: (s, 0, j, 0))
    return _hosted_call(
        body, grid=(m // tm, d // tn),
        in_specs=[pl.BlockSpec((tm, f), lambda i, j: (i, 0)),
                  pl.BlockSpec((tm, f), lambda i, j: (i, 0))] + [wspec(s) for s in range(n_slots)],
        out_specs=pl.BlockSpec((tm, tn), lambda i, j: (i, j)),
        out_shape=jax.ShapeDtypeStruct((m, d), MXU),
        args=(dg, dv) + (w4,) * n_slots, name=name, q=q, flops=4.0 * m * f * d)


def _norm_fwd(x, g, *, name):
    n, d = x.shape
    tm = _tile(n, 256, SUBLANE)

    def body(x_ref, g_ref, o_ref):
        o_ref[...] = _rms_fwd(x_ref[...], g_ref[...]).astype(o_ref.dtype)

    return pl.pallas_call(
        body, grid=(n // tm,),
        in_specs=[pl.BlockSpec((tm, d), lambda i: (i, 0)), pl.BlockSpec((1, d), lambda i: (0, 0))],
        out_specs=pl.BlockSpec((tm, d), lambda i: (i, 0)),
        out_shape=jax.ShapeDtypeStruct((n, d), MXU),
        name=name, compiler_params=_cp((PAR,)))(x, g)


def _norm_bwd_dg(dy, x, g, *, name):
    n, d = x.shape
    tm = _tile(n, 256, SUBLANE)

    def body(dy_ref, x_ref, g_ref, dg_ref):
        @pl.when(pl.program_id(0) == 0)
        def _():
            dg_ref[...] = jnp.zeros_like(dg_ref)
        _, dg = _rms_bwd(dy_ref[...], x_ref[...], g_ref[...])
        dg_ref[...] += dg

    return pl.pallas_call(
        body, grid=(n // tm,),
        in_specs=[pl.BlockSpec((tm, d), lambda i: (i, 0)), pl.BlockSpec((tm, d), lambda i: (i, 0)),
                  pl.BlockSpec((1, d), lambda i: (0, 0))],
        out_specs=pl.BlockSpec((1, d), lambda i: (0, 0)),
        out_shape=jax.ShapeDtypeStruct((1, d), F32),
        name=name, compiler_params=_cp((ARB,)))(dy, x, g)


def _resid_norm_fwd(x, y, g_post, g_pres, *, name, q=None):
    n, d = x.shape
    tm = _tile(n, 256, SUBLANE)
    nh = len(g_pres)

    def body(x_ref, y_ref, gp_ref, *rest):
        gpre = rest[:nh]
        xo_ref = rest[nh]
        h_refs = rest[nh + 1:]
        xo = x_ref[...] + _rms_fwd(y_ref[...].astype(F32), gp_ref[...])
        xo_ref[...] = xo
        for g_ref, h_ref in zip(gpre, h_refs):
            h_ref[...] = _rms_fwd(xo, g_ref[...]).astype(h_ref.dtype)

    row = pl.BlockSpec((tm, d), lambda i: (i, 0))
    vec = pl.BlockSpec((1, d), lambda i: (0, 0))
    outs = _hosted_call(
        body, grid=(n // tm,),
        in_specs=[row, row, vec] + [vec] * nh,
        out_specs=[row] + [row] * nh,
        out_shape=[jax.ShapeDtypeStruct((n, d), F32)] + [jax.ShapeDtypeStruct((n, d), MXU)] * nh,
        args=(x, y, g_post, *g_pres), name=name, q=q, budget_us=HOST_US["resid"])
    return outs[0], list(outs[1:])


def _loss_fwd(x, y, g_post, target, *, name):
    n, d = x.shape
    tm = _tile(n, 256, SUBLANE)

    def body(x_ref, y_ref, gp_ref, t_ref, dx_ref, sq_ref):
        @pl.when(pl.program_id(0) == 0)
        def _():
            sq_ref[...] = jnp.zeros_like(sq_ref)
        err = x_ref[...] + _rms_fwd(y_ref[...].astype(F32), gp_ref[...]) - t_ref[...]
        dx_ref[...] = err * (1.0 / d)
        sq_ref[...] += jnp.sum(err * err, axis=0, keepdims=True)

    row = pl.BlockSpec((tm, d), lambda i: (i, 0))
    vec = pl.BlockSpec((1, d), lambda i: (0, 0))
    return pl.pallas_call(
        body, grid=(n // tm,),
        in_specs=[row, row, vec, row],
        out_specs=[row, vec],
        out_shape=[jax.ShapeDtypeStruct((n, d), F32), jax.ShapeDtypeStruct((1, d), F32)],
        name=name, compiler_params=_cp((ARB,)))(x, y, g_post, target)


def _resid_norm_bwd(dx_out, dhs, x_out, g_pres, y, g_post, *, name, q=None):
    n, d = dx_out.shape
    tm = _tile(n, 256, SUBLANE)
    nh = len(dhs)
    has_y = y is not None

    def body(*refs):
        it = iter(refs)
        dxo_ref = next(it)
        dh_refs = [next(it) for _ in range(nh)]
        xo_ref = next(it) if nh else None
        gpre_refs = [next(it) for _ in range(nh)]
        y_ref = next(it) if has_y else None
        gpost_ref = next(it) if has_y else None
        g_out = next(it)
        dy_out = next(it) if has_y else None
        dgpre_out = [next(it) for _ in range(nh)]
        dgpost_out = next(it) if has_y else None

        @pl.when(pl.program_id(0) == 0)
        def _():
            for r in dgpre_out:
                r[...] = jnp.zeros_like(r)
            if has_y:
                dgpost_out[...] = jnp.zeros_like(dgpost_out)

        g = dxo_ref[...]
        if nh:
            xo = xo_ref[...]
            for dh_ref, gp_ref, dg_ref in zip(dh_refs, gpre_refs, dgpre_out):
                dx, dg = _rms_bwd(dh_ref[...].astype(F32), xo, gp_ref[...])
                g = g + dx
                dg_ref[...] += dg
        g_out[...] = g
        if has_y:
            dy, dg = _rms_bwd(g, y_ref[...].astype(F32), gpost_ref[...])
            dy_out[...] = dy.astype(dy_out.dtype)
            dgpost_out[...] += dg

    row = pl.BlockSpec((tm, d), lambda i: (i, 0))
    vec = pl.BlockSpec((1, d), lambda i: (0, 0))
    ins, in_specs = [dx_out], [row]
    ins += list(dhs)
    in_specs += [row] * nh
    if nh:
        ins.append(x_out)
        in_specs.append(row)
    ins += list(g_pres)
    in_specs += [vec] * nh
    if has_y:
        ins += [y, g_post]
        in_specs += [row, vec]
    out_specs, out_shape = [row], [jax.ShapeDtypeStruct((n, d), F32)]
    if has_y:
        out_specs.append(row)
        out_shape.append(jax.ShapeDtypeStruct((n, d), MXU))
    out_specs += [vec] * nh
    out_shape += [jax.ShapeDtypeStruct((1, d), F32)] * nh
    if has_y:
        out_specs.append(vec)
        out_shape.append(jax.ShapeDtypeStruct((1, d), F32))
    outs = list(_hosted_call(
        body, grid=(n // tm,), in_specs=in_specs, out_specs=out_specs, out_shape=out_shape,
        args=tuple(ins), name=name, q=q, budget_us=HOST_US["resid_bwd"]))
    g = outs.pop(0)
    dy = outs.pop(0) if has_y else None
    dgpre = [outs.pop(0) for _ in range(nh)]
    dgpost = outs.pop(0) if has_y else None
    return g, dy, dgpre, dgpost


def _ffn_conv(up, w_ref, b_ref):
    return (w_ref[0:1, :] * _shift_down_edge(up, 2) + w_ref[1:2, :] * _shift_down_edge(up, 1)
            + w_ref[2:3, :] * up + b_ref[...])


def _ffn_act_fwd(up, wconv, bconv, bsz, *, name, q=None):
    n, f2 = up.shape
    f = f2 // 2
    t = n // bsz
    tc = _tile(f, 256)
    nf = f // tc

    def body(ug_ref, uv_ref, wg_ref, wv_ref, bg_ref, bv_ref, o_ref, g_ref, v_ref):
        g = _ffn_conv(ug_ref[...].astype(F32), wg_ref, bg_ref)
        v = _ffn_conv(uv_ref[...].astype(F32), wv_ref, bv_ref)
        g_ref[...] = g.astype(g_ref.dtype)
        v_ref[...] = v.astype(v_ref.dtype)
        o_ref[...] = (_gelu(g) * v).astype(o_ref.dtype)

    blk = pl.BlockSpec((t, tc), lambda b, j: (b, j))
    return _hosted_call(
        body, grid=(bsz, nf),
        in_specs=[blk, pl.BlockSpec((t, tc), lambda b, j: (b, j + nf)),
                  pl.BlockSpec((FFN_CONV, tc), lambda b, j: (0, j)),
                  pl.BlockSpec((FFN_CONV, tc), lambda b, j: (0, j + nf)),
                  pl.BlockSpec((1, tc), lambda b, j: (0, j)),
                  pl.BlockSpec((1, tc), lambda b, j: (0, j + nf))],
        out_specs=[blk, blk, blk],
        out_shape=[jax.ShapeDtypeStruct((n, f), MXU)] * 3,
        args=(up, up, wconv, wconv, bconv, bconv), name=name, q=q, budget_us=HOST_US["ffn_act"])


def _ffn_act_bwd(up, ug, uv, dact, wconv, bsz, *, name, q=None):
    n, f2 = up.shape
    f = f2 // 2
    t = n // bsz
    tc = _tile(f, 256)
    nf = f // tc

    def body(xg_ref, xv_ref, g_ref, v_ref, da_ref, wg_ref, wv_ref,
             dug_ref, duv_ref, dwg_ref, dwv_ref, dbg_ref, dbv_ref):
        @pl.when(pl.program_id(1) == 0)
        def _():
            for r in (dwg_ref, dwv_ref, dbg_ref, dbv_ref):
                r[...] = jnp.zeros_like(r)

        gl, dgl = _gelu_and_grad(g_ref[...].astype(F32))
        da = da_ref[...].astype(F32)
        dg = da * v_ref[...].astype(F32) * dgl
        dv = da * gl

        def conv_bwd(du, w_ref, x_ref, dx_ref, dw_ref, db_ref):
            du1, du2 = _shift_up_edge(du, 1), _shift_up_edge(du, 2)
            dx_ref[...] = (w_ref[2:3, :] * du + w_ref[1:2, :] * du1 + w_ref[0:1, :] * du2).astype(dx_ref.dtype)
            x = x_ref[...].astype(F32)
            dw_ref[0:1, :] += jnp.sum(x * du2, axis=0, keepdims=True)
            dw_ref[1:2, :] += jnp.sum(x * du1, axis=0, keepdims=True)
            dw_ref[2:3, :] += jnp.sum(x * du, axis=0, keepdims=True)
            db_ref[...] += jnp.sum(du, axis=0, keepdims=True)

        conv_bwd(dg, wg_ref, xg_ref, dug_ref, dwg_ref, dbg_ref)
        conv_bwd(dv, wv_ref, xv_ref, duv_ref, dwv_ref, dbv_ref)

    blk = pl.BlockSpec((t, tc), lambda j, b: (b, j))
    wspec = pl.BlockSpec((FFN_CONV, tc), lambda j, b: (0, j))
    bspec = pl.BlockSpec((1, tc), lambda j, b: (0, j))
    outs = _hosted_call(
        body, grid=(nf, bsz),
        in_specs=[blk, pl.BlockSpec((t, tc), lambda j, b: (b, j + nf)), blk, blk, blk,
                  wspec, pl.BlockSpec((FFN_CONV, tc), lambda j, b: (0, j + nf))],
        out_specs=[blk, blk, wspec, wspec, bspec, bspec],
        out_shape=[jax.ShapeDtypeStruct((n, f), MXU), jax.ShapeDtypeStruct((n, f), MXU),
                   jax.ShapeDtypeStruct((FFN_CONV, f), F32), jax.ShapeDtypeStruct((FFN_CONV, f), F32),
                   jax.ShapeDtypeStruct((1, f), F32), jax.ShapeDtypeStruct((1, f), F32)],
        args=(up, up, ug, uv, dact, wconv, wconv), name=name, q=q, budget_us=HOST_US["ffn_act_bwd"])
    dug, duv, dwg, dwv, dbg, dbv = outs
    return dug, duv, jnp.concatenate([dwg, dwv], axis=1), jnp.concatenate([dbg, dbv], axis=1)


def _mem_attn_fwd(proj, q_col_block, mkv, ycat, bsz, *, name, q=None):
    n = proj.shape[0]
    t = n // bsz
    tq = _tile(t, 512, SUBLANE)
    nt = t // tq
    scale = HEAD_DIM ** -0.5

    def body(q_ref, kv_ref, old_ref, o_ref):
        del old_ref
        outs = []
        for h in range(MEM_HEADS):
            sl = slice(h * HEAD_DIM, (h + 1) * HEAD_DIM)
            q = q_ref[:, sl].astype(MXU)
            k = kv_ref[:, sl].astype(MXU)
            v = kv_ref[:, MEM_WIDTH + h * HEAD_DIM: MEM_WIDTH + (h + 1) * HEAD_DIM].astype(MXU)
            s = _dot_nt(q, k) * scale
            m = jnp.max(s, axis=-1, keepdims=True)
            p = jnp.exp(s - m)
            p = p / jnp.sum(p, axis=-1, keepdims=True)
            outs.append(_dot(p.astype(MXU), v))
        o_ref[...] = jnp.concatenate(outs, axis=-1).astype(o_ref.dtype)

    return _hosted_call(
        body, grid=(bsz, nt),
        in_specs=[pl.BlockSpec((tq, MEM_WIDTH), lambda b, i: (b * nt + i, q_col_block)),
                  pl.BlockSpec((MEM_LEN, 2 * MEM_WIDTH), lambda b, i: (b, 0)),
                  pl.BlockSpec(memory_space=pl.ANY)],
        out_specs=pl.BlockSpec((tq, MEM_WIDTH), lambda b, i: (b * nt + i, MIX_WIDTH // MEM_WIDTH)),
        out_shape=jax.ShapeDtypeStruct(ycat.shape, ycat.dtype),
        aliases={2: 0}, args=(proj, mkv, ycat), name=name, q=q, budget_us=HOST_US["mem_attn_fwd"])


def _mem_attn_bwd(proj, q_col_block, mkv, dycat, dproj, bsz, *, name, q=None):
    n = proj.shape[0]
    t = n // bsz
    tq = _tile(t, 512, SUBLANE)
    nt = t // tq
    scale = HEAD_DIM ** -0.5

    def body(q_ref, kv_ref, do_ref, old_ref, dq_ref, dkv_ref):
        del old_ref

        @pl.when(pl.program_id(1) == 0)
        def _():
            dkv_ref[...] = jnp.zeros_like(dkv_ref)

        dqs, dks, dvs = [], [], []
        for h in range(MEM_HEADS):
            sl = slice(h * HEAD_DIM, (h + 1) * HEAD_DIM)
            q = q_ref[:, sl].astype(MXU)
            k = kv_ref[:, sl].astype(MXU)
            v = kv_ref[:, MEM_WIDTH + h * HEAD_DIM: MEM_WIDTH + (h + 1) * HEAD_DIM].astype(MXU)
            do = do_ref[:, sl].astype(MXU)
            s = _dot_nt(q, k) * scale
            m = jnp.max(s, axis=-1, keepdims=True)
            p = jnp.exp(s - m)
            p = p / jnp.sum(p, axis=-1, keepdims=True)
            dvs.append(_dot_tn(p.astype(MXU), do))
            dp = _dot_nt(do, v)
            ds = (p * (dp - jnp.sum(dp * p, axis=-1, keepdims=True)) * scale).astype(MXU)
            dqs.append(_dot(ds, k))
            dks.append(_dot_tn(ds, q))
        dq_ref[...] = jnp.concatenate(dqs, axis=-1).astype(dq_ref.dtype)
        dkv_ref[...] += jnp.concatenate(dks + dvs, axis=-1)

    return _hosted_call(
        body, grid=(bsz, nt),
        in_specs=[pl.BlockSpec((tq, MEM_WIDTH), lambda b, i: (b * nt + i, q_col_block)),
                  pl.BlockSpec((MEM_LEN, 2 * MEM_WIDTH), lambda b, i: (b, 0)),
                  pl.BlockSpec((tq, MEM_WIDTH), lambda b, i: (b * nt + i, MIX_WIDTH // MEM_WIDTH)),
                  pl.BlockSpec(memory_space=pl.ANY)],
        out_specs=[pl.BlockSpec((tq, MEM_WIDTH), lambda b, i: (b * nt + i, q_col_block)),
                   pl.BlockSpec((MEM_LEN, 2 * MEM_WIDTH), lambda b, i: (b, 0))],
        out_shape=[jax.ShapeDtypeStruct(dproj.shape, dproj.dtype),
                   jax.ShapeDtypeStruct((bsz * MEM_LEN, 2 * MEM_WIDTH), F32)],
        aliases={3: 0}, args=(proj, mkv, dycat, dproj), name=name, q=q, budget_us=HOST_US["mem_attn_bwd"])


def _swa_scores(q, k, h, dist, mask, sink):
    s = _dot_nt(q, k) * (HEAD_DIM ** -0.5)
    s = jnp.where(mask, s - SLOPES[h] * dist, -jnp.inf)
    m = jnp.maximum(jnp.max(s, axis=-1, keepdims=True), sink)
    p = jnp.exp(s - m)
    psink = jnp.exp(sink - m)
    inv = 1.0 / (jnp.sum(p, axis=-1, keepdims=True) + psink)
    return p * inv, psink * inv


def _swa_mask(n):
    qi = lax.broadcasted_iota(jnp.int32, (WINDOW, 2 * WINDOW), 0) + WINDOW
    ki = lax.broadcasted_iota(jnp.int32, (WINDOW, 2 * WINDOW), 1)
    dist = qi - ki
    mask = (dist >= 0) & (dist < WINDOW) & ((n > 0) | (ki >= WINDOW))
    return dist.astype(F32), mask


def _swa_fwd(proj, kv, sinks, bsz, *, name, q=None):
    n_tok = proj.shape[0]
    nb = n_tok // bsz // WINDOW
    kvw = SWA_KV_HEADS * HEAD_DIM

    def body(sink_ref, q_ref, kvp_ref, kvc_ref, o_ref):
        n = pl.program_id(1)
        dist, mask = _swa_mask(n)
        kk = jnp.concatenate([kvp_ref[:, :kvw], kvc_ref[:, :kvw]], axis=0).astype(MXU)
        vv = jnp.concatenate([kvp_ref[:, kvw:], kvc_ref[:, kvw:]], axis=0).astype(MXU)
        outs = []
        for h in range(SWA_HEADS):
            c = h // SWA_GROUP
            q = q_ref[:, h * HEAD_DIM:(h + 1) * HEAD_DIM].astype(MXU)
            p, _ = _swa_scores(q, kk[:, c * HEAD_DIM:(c + 1) * HEAD_DIM], h, dist, mask, sink_ref[h])
            outs.append(_dot(p.astype(MXU), vv[:, c * HEAD_DIM:(c + 1) * HEAD_DIM]))
        o_ref[...] = jnp.concatenate(outs, axis=-1).astype(o_ref.dtype)

    return _hosted_call(
        body, grid=(bsz, nb),
        in_specs=[pl.BlockSpec(memory_space=pltpu.SMEM),
                  pl.BlockSpec((WINDOW, MIX_WIDTH), lambda b, n: (b * nb + n, 0)),
                  pl.BlockSpec((WINDOW, 2 * kvw), lambda b, n: (b * nb + jnp.maximum(n - 1, 0), 0)),
                  pl.BlockSpec((WINDOW, 2 * kvw), lambda b, n: (b * nb + n, 0))],
        out_specs=pl.BlockSpec((WINDOW, MIX_WIDTH), lambda b, n: (b * nb + n, 0)),
        out_shape=jax.ShapeDtypeStruct((n_tok, D_MODEL), MXU),
        args=(sinks, proj, kv, kv), name=name, q=q, budget_us=HOST_US["swa_fwd"])


def _swa_bwd(proj, kv, sinks, dycat, bsz, *, name, q=None):
    n_tok = proj.shape[0]
    nb = n_tok // bsz // WINDOW
    kvw = SWA_KV_HEADS * HEAD_DIM

    def body(sink_ref, q_ref, kvp_ref, kvc_ref, do_ref, dq_ref, dkvc_ref, dkvp_ref, dsink_ref):
        n = pl.program_id(1)

        @pl.when((pl.program_id(0) == 0) & (n == 0))
        def _():
            dsink_ref[...] = jnp.zeros_like(dsink_ref)

        dist, mask = _swa_mask(n)
        kk = jnp.concatenate([kvp_ref[:, :kvw], kvc_ref[:, :kvw]], axis=0).astype(MXU)
        vv = jnp.concatenate([kvp_ref[:, kvw:], kvc_ref[:, kvw:]], axis=0).astype(MXU)
        lane = lax.broadcasted_iota(jnp.int32, (SUBLANE, LANE), 1)
        dqs = []
        dks = [None] * SWA_KV_HEADS
        dvs = [None] * SWA_KV_HEADS
        dsink = jnp.zeros((SUBLANE, LANE), F32)
        for h in range(SWA_HEADS):
            c = h // SWA_GROUP
            k = kk[:, c * HEAD_DIM:(c + 1) * HEAD_DIM]
            v = vv[:, c * HEAD_DIM:(c + 1) * HEAD_DIM]
            q = q_ref[:, h * HEAD_DIM:(h + 1) * HEAD_DIM].astype(MXU)
            do = do_ref[:, h * HEAD_DIM:(h + 1) * HEAD_DIM].astype(MXU)
            p, psink = _swa_scores(q, k, h, dist, mask, sink_ref[h])
            dv = _dot_tn(p.astype(MXU), do)
            dp = _dot_nt(do, v)
            rs = jnp.sum(dp * p, axis=-1, keepdims=True)
            ds = (p * (dp - rs) * (HEAD_DIM ** -0.5)).astype(MXU)
            dsink = dsink + jnp.where(lane == h, jnp.sum(-psink * rs, axis=0, keepdims=True), 0.0)
            dqs.append(_dot(ds, k))
            dk = _dot_tn(ds, q)
            dks[c] = dk if dks[c] is None else dks[c] + dk
            dvs[c] = dv if dvs[c] is None else dvs[c] + dv
        dq_ref[...] = jnp.concatenate(dqs, axis=-1).astype(dq_ref.dtype)
        dkv = jnp.concatenate(dks + dvs, axis=-1)
        dkvp_ref[...] = dkv[:WINDOW]
        dkvc_ref[...] = dkv[WINDOW:]
        dsink_ref[...] += dsink

    qspec = pl.BlockSpec((WINDOW, MIX_WIDTH), lambda b, n: (b * nb + n, 0))
    kvspec = pl.BlockSpec((WINDOW, 2 * kvw), lambda b, n: (b * nb + n, 0))
    return _hosted_call(
        body, grid=(bsz, nb),
        in_specs=[pl.BlockSpec(memory_space=pltpu.SMEM), qspec,
                  pl.BlockSpec((WINDOW, 2 * kvw), lambda b, n: (b * nb + jnp.maximum(n - 1, 0), 0)),
                  kvspec, qspec],
        out_specs=[qspec, kvspec, kvspec, pl.BlockSpec((SUBLANE, LANE), lambda b, n: (0, 0))],
        out_shape=[jax.ShapeDtypeStruct((n_tok, D_MODEL), MXU),
                   jax.ShapeDtypeStruct((n_tok, 2 * kvw), F32),
                   jax.ShapeDtypeStruct((n_tok, 2 * kvw), F32),
                   jax.ShapeDtypeStruct((SUBLANE, LANE), F32)],
        args=(sinks, proj, kv, kv, dycat), name=name, q=q, budget_us=HOST_US["swa_bwd"])


def _swa_dkv_combine(curs, prevs, bsz, *, name):
    n_tok, w = curs[0].shape
    nb = n_tok // bsz // WINDOW
    k = len(curs)

    def body(*refs):
        o_ref = refs[-1]
        n = pl.program_id(1)
        acc = refs[0][...]
        for r in refs[1:k]:
            acc = acc + r[...]
        nxt = refs[k][...]
        for r in refs[k + 1:2 * k]:
            nxt = nxt + r[...]
        o_ref[...] = (acc + jnp.where(n < nb - 1, nxt, 0.0)).astype(o_ref.dtype)

    cur = pl.BlockSpec((WINDOW, w), lambda b, n: (b * nb + n, 0))
    prv = pl.BlockSpec((WINDOW, w), lambda b, n: (b * nb + jnp.minimum(n + 1, nb - 1), 0))
    return pl.pallas_call(
        body, grid=(bsz, nb), in_specs=[cur] * k + [prv] * k, out_specs=cur,
        out_shape=jax.ShapeDtypeStruct((n_tok, w), MXU),
        name=name, compiler_params=_cp((PAR, PAR)))(*curs, *prevs)


def _lru_gates(ux, halo, ext_ref, wc_ref, bc_ref, wr_ref, br_ref, wi_ref, bi_ref, lam_ref):
    tt = ux.shape[0]
    ext_ref[0:SUBLANE, :] = halo
    ext_ref[SUBLANE:, :] = ux
    xs = [ux] + [ext_ref[pl.ds(SUBLANE - k, tt), :] for k in range(1, LRU_CONV)]
    xc = bc_ref[...] + wc_ref[3:4, :] * xs[0] + wc_ref[2:3, :] * xs[1] + wc_ref[1:2, :] * xs[2] + wc_ref[0:1, :] * xs[3]
    pre_r, pre_i = [], []
    for blk in range(MIX_WIDTH // GATE_TILE):
        xb = xc[:, blk * GATE_TILE:(blk + 1) * GATE_TILE].astype(MXU)
        pre_r.append(_dot(xb, wr_ref[blk]))
        pre_i.append(_dot(xb, wi_ref[blk]))
    r = jax.nn.sigmoid(jnp.concatenate(pre_r, axis=-1) + br_ref[...])
    i = jax.nn.sigmoid(jnp.concatenate(pre_i, axis=-1) + bi_ref[...])
    nlam = -lam_ref[...]
    sp = jnp.maximum(nlam, 0.0) + jnp.log(1.0 + jnp.exp(-jnp.abs(nlam)))
    log_a = -LRU_C * r * sp
    a = jnp.exp(log_a)
    om = -jnp.tanh(log_a) * (a * a + 1.0)
    s = jnp.sqrt(om)
    return xs, xc, r, i, sp, a, s


def _lru_fwd(proj, wconv, bconv, wr, br, wi, bi, lam, bsz, *, name, q=None):
    n_tok = proj.shape[0]
    t = n_tok // bsz
    tt = _tile(t, 256, SUBLANE)
    nt = t // tt
    w = MIX_WIDTH
    ng = tt // SUBLANE

    def body(pg_ref, halo_ref, wc_ref, bc_ref, wr_ref, br_ref, wi_ref, bi_ref, lam_ref,
             y_ref, h_ref, ext_ref, a_ref, b_ref, carry_ref):
        ti = pl.program_id(1)

        @pl.when(ti == 0)
        def _():
            carry_ref[...] = jnp.zeros_like(carry_ref)

        gate = pg_ref[:, :w]
        ux = pg_ref[:, w:]
        halo = jnp.where(ti > 0, halo_ref[...], 0.0)
        _, xc, _, i, _, a, s = _lru_gates(ux, halo, ext_ref, wc_ref, bc_ref, wr_ref, br_ref, wi_ref, bi_ref, lam_ref)
        a_ref[...] = a
        b_ref[...] = s * (i * xc)
        row = lax.broadcasted_iota(jnp.int32, (SUBLANE, w), 0)

        def group(g, hprev):
            off = pl.multiple_of(g * SUBLANE, SUBLANE)
            ca = a_ref[pl.ds(off, SUBLANE), :]
            cb = b_ref[pl.ds(off, SUBLANE), :]
            for d in (1, 2, 4):
                a_sh = jnp.where(row >= d, pltpu.roll(ca, d, axis=0), 1.0)
                b_sh = jnp.where(row >= d, pltpu.roll(cb, d, axis=0), 0.0)
                cb = ca * b_sh + cb
                ca = ca * a_sh
            h = ca * hprev + cb
            b_ref[pl.ds(off, SUBLANE), :] = h
            return jnp.broadcast_to(h[SUBLANE - 1:SUBLANE, :], (SUBLANE, w))

        carry_ref[...] = lax.fori_loop(0, ng, group, carry_ref[...])
        h = b_ref[...]
        h_ref[...] = h
        y_ref[...] = (h * _gelu(gate)).astype(y_ref.dtype)

    vec = lambda r: pl.BlockSpec((r, w), lambda b, i: (0, 0))
    wspec = pl.BlockSpec((w // GATE_TILE, GATE_TILE, GATE_TILE), lambda b, i: (0, 0, 0))
    hb = tt // SUBLANE
    return _hosted_call(
        body, grid=(bsz, nt),
        in_specs=[pl.BlockSpec((tt, 2 * w), lambda b, i: (b * nt + i, 0)),
                  pl.BlockSpec((SUBLANE, w), lambda b, i: (jnp.maximum((b * nt + i) * hb - 1, 0), 1)),
                  vec(LRU_CONV), vec(1), wspec, vec(1), wspec, vec(1), vec(1)],
        out_specs=[pl.BlockSpec((tt, w), lambda b, i: (b * nt + i, 0)),
                   pl.BlockSpec((tt, w), lambda b, i: (b * nt + i, 0))],
        out_shape=[jax.ShapeDtypeStruct((n_tok, D_MODEL), MXU), jax.ShapeDtypeStruct((n_tok, w), F32)],
        scratch_shapes=[pltpu.VMEM((tt + SUBLANE, w), F32), pltpu.VMEM((tt, w), F32),
                        pltpu.VMEM((tt, w), F32), pltpu.VMEM((SUBLANE, w), F32)],
        args=(proj, proj, wconv, bconv, wr, br, wi, bi, lam), name=name, q=q, budget_us=HOST_US["lru_fwd"])


def _lru_bwd(proj, hs, dycat, wconv, bconv, wr, br, wi, bi, lam, bsz, *, name, q=None):
    n_tok = proj.shape[0]
    t = n_tok // bsz
    tt = _tile(t, 256, SUBLANE)
    nt = t // tt
    w = MIX_WIDTH
    ng = tt // SUBLANE
    nblk = w // GATE_TILE

    def body(pg_ref, halo_ref, h_ref, hhalo_ref, dy_ref, wc_ref, bc_ref, wr_ref, br_ref, wi_ref, bi_ref, lam_ref,
             dp_ref, dwc_ref, dbc_ref, dwr_ref, dbr_ref, dwi_ref, dbi_ref, dlam_ref,
             ext_ref, a_ref, c_ref, g_ref, gcarry_ref, xcarry_ref):
        bi_ = pl.program_id(0)
        ti = nt - 1 - pl.program_id(1)

        @pl.when((bi_ == 0) & (pl.program_id(1) == 0))
        def _():
            for r in (dwc_ref, dbc_ref, dwr_ref, dbr_ref, dwi_ref, dbi_ref, dlam_ref):
                r[...] = jnp.zeros_like(r)

        @pl.when(pl.program_id(1) == 0)
        def _():
            gcarry_ref[...] = jnp.zeros_like(gcarry_ref)
            xcarry_ref[...] = jnp.zeros_like(xcarry_ref)

        gate = pg_ref[:, :w]
        ux = pg_ref[:, w:]
        halo = jnp.where(ti > 0, halo_ref[...], 0.0)
        xs, xc, r, i, sp, a, s = _lru_gates(ux, halo, ext_ref, wc_ref, bc_ref, wr_ref, br_ref, wi_ref, bi_ref, lam_ref)
        h = h_ref[...]
        gl, dgl = _gelu_and_grad(gate)
        dy = dy_ref[...].astype(F32)
        dgate = dy * h * dgl
        row_t = lax.broadcasted_iota(jnp.int32, (tt, w), 0)
        g_ref[...] = dy * gl + jnp.where(row_t == tt - 1, gcarry_ref[0:1, :], 0.0)
        c_ref[...] = _shift_up(a, 1, row_t)
        row = lax.broadcasted_iota(jnp.int32, (SUBLANE, w), 0)

        a_ref[...] = a

        def group(k, gnext):
            off = pl.multiple_of((ng - 1 - k) * SUBLANE, SUBLANE)
            cc = c_ref[pl.ds(off, SUBLANE), :]
            cb = g_ref[pl.ds(off, SUBLANE), :]
            cb = cb + jnp.where(row == SUBLANE - 1, gnext, 0.0)
            cc = jnp.where(row == SUBLANE - 1, 0.0, cc)
            for d in (1, 2, 4):
                c_sh = jnp.where(row < SUBLANE - d, pltpu.roll(cc, SUBLANE - d, axis=0), 1.0)
                b_sh = jnp.where(row < SUBLANE - d, pltpu.roll(cb, SUBLANE - d, axis=0), 0.0)
                cb = cc * b_sh + cb
                cc = cc * c_sh
            g_ref[pl.ds(off, SUBLANE), :] = cb
            a0 = a_ref[pl.ds(off, SUBLANE), :]
            return jnp.broadcast_to(a0[0:1, :] * cb[0:1, :], (SUBLANE, w))

        gc = lax.fori_loop(0, ng, group, jnp.zeros((SUBLANE, w), F32))
        gcarry_ref[...] = gc
        gsc = g_ref[...]

        hhalo = jnp.where(ti > 0, hhalo_ref[SUBLANE - 1:SUBLANE, :], 0.0)
        hprev = jnp.where(row_t == 0, hhalo, pltpu.roll(h, 1, axis=0))
        gated = i * xc
        d_gated = gsc * s
        d_atot = gsc * hprev - (gsc * gated) * a / s
        d_loga = d_atot * a
        d_r = d_loga * (-LRU_C) * sp
        dlam_ref[...] += jnp.sum(d_loga * r, axis=0, keepdims=True) * (LRU_C * jax.nn.sigmoid(-lam_ref[...]))
        d_i = d_gated * xc
        d_xc = d_gated * i
        d_pr = d_r * r * (1.0 - r)
        d_pi = d_i * i * (1.0 - i)
        dbr_ref[...] += jnp.sum(d_pr, axis=0, keepdims=True)
        dbi_ref[...] += jnp.sum(d_pi, axis=0, keepdims=True)
        extra = []
        for blk in range(nblk):
            sl = slice(blk * GATE_TILE, (blk + 1) * GATE_TILE)
            xb = xc[:, sl].astype(MXU)
            dr_b = d_pr[:, sl].astype(MXU)
            di_b = d_pi[:, sl].astype(MXU)
            dwr_ref[blk] += _dot_tn(xb, dr_b)
            dwi_ref[blk] += _dot_tn(xb, di_b)
            extra.append(_dot_nt(dr_b, wr_ref[blk]) + _dot_nt(di_b, wi_ref[blk]))
        d_xc = d_xc + jnp.concatenate(extra, axis=-1)
        dbc_ref[...] += jnp.sum(d_xc, axis=0, keepdims=True)
        for k in range(LRU_CONV):
            dwc_ref[k:k + 1, :] += jnp.sum(d_xc * xs[LRU_CONV - 1 - k], axis=0, keepdims=True)
        ext_ref[0:tt, :] = d_xc
        ext_ref[tt:, :] = xcarry_ref[...]
        dux = wc_ref[3:4, :] * d_xc
        for k in range(LRU_CONV - 1):
            dux = dux + wc_ref[k:k + 1, :] * ext_ref[pl.ds(LRU_CONV - 1 - k, tt), :]
        xcarry_ref[...] = d_xc[0:SUBLANE, :]
        dp_ref[:, :w] = dgate.astype(dp_ref.dtype)
        dp_ref[:, w:] = dux.astype(dp_ref.dtype)

    vec = lambda r: pl.BlockSpec((r, w), lambda b, i: (0, 0))
    wspec = pl.BlockSpec((nblk, GATE_TILE, GATE_TILE), lambda b, i: (0, 0, 0))
    hb = tt // SUBLANE
    rblk = lambda b, i: b * nt + (nt - 1 - i)
    halo_idx = lambda b, i: jnp.maximum(rblk(b, i) * hb - 1, 0)
    wide = pl.BlockSpec((tt, 2 * w), lambda b, i: (rblk(b, i), 0))
    narrow = pl.BlockSpec((tt, w), lambda b, i: (rblk(b, i), 0))
    return _hosted_call(
        body, grid=(bsz, nt),
        in_specs=[wide, pl.BlockSpec((SUBLANE, w), lambda b, i: (halo_idx(b, i), 1)),
                  narrow, pl.BlockSpec((SUBLANE, w), lambda b, i: (halo_idx(b, i), 0)), narrow,
                  vec(LRU_CONV), vec(1), wspec, vec(1), wspec, vec(1), vec(1)],
        out_specs=[wide, vec(LRU_CONV), vec(1), wspec, vec(1), wspec, vec(1), vec(1)],
        out_shape=[jax.ShapeDtypeStruct((n_tok, 2 * w + MEM_WIDTH), MXU),
                   jax.ShapeDtypeStruct((LRU_CONV, w), F32), jax.ShapeDtypeStruct((1, w), F32),
                   jax.ShapeDtypeStruct((nblk, GATE_TILE, GATE_TILE), F32), jax.ShapeDtypeStruct((1, w), F32),
                   jax.ShapeDtypeStruct((nblk, GATE_TILE, GATE_TILE), F32), jax.ShapeDtypeStruct((1, w), F32),
                   jax.ShapeDtypeStruct((1, w), F32)],
        scratch_shapes=[pltpu.VMEM((tt + SUBLANE, w), F32), pltpu.VMEM((tt, w), F32), pltpu.VMEM((tt, w), F32),
                        pltpu.VMEM((tt, w), F32), pltpu.VMEM((SUBLANE, w), F32), pltpu.VMEM((SUBLANE, w), F32)],
        args=(proj, proj, hs, hs, dycat, wconv, bconv, wr, br, wi, bi, lam), name=name, q=q,
        budget_us=HOST_US["lru_bwd"])


def _gate_tiles(w):
    per = GATE_TILE // HEAD_DIM
    w4 = w.reshape(LRU_BLOCKS // per, per, HEAD_DIM, HEAD_DIM)
    eye = jnp.eye(per, dtype=w.dtype)
    return jnp.einsum("bnij,nm->bnimj", w4, eye).reshape(LRU_BLOCKS // per, GATE_TILE, GATE_TILE)


def _gate_blocks(t):
    per = GATE_TILE // HEAD_DIM
    t5 = t.reshape(LRU_BLOCKS // per, per, HEAD_DIM, per, HEAD_DIM)
    eye = jnp.eye(per, dtype=t.dtype)
    return jnp.einsum("bnimj,nm->bnij", t5, eye).reshape(LRU_BLOCKS, HEAD_DIM, HEAD_DIM)


def _row(v):
    return v.reshape(1, -1)


def _local_step(x, mem, target, p, wfull, push_grad, q):
    bsz, t, d = x.shape
    n = bsz * t
    x2d = x.reshape(n, d)
    tgt = target.reshape(n, d)
    mem2d = mem.reshape(bsz * MEM_LEN, d)
    wr_t = [_gate_tiles(p["w_rg_r"][j]).astype(MXU) for j in range(N_A)]
    wi_t = [_gate_tiles(p["w_rg_i"][j]).astype(MXU) for j in range(N_A)]

    mn = [_norm_fwd(mem2d, _row(p["g_mem"][l]), name=f"mem_norm{l}") for l in range(DEPTH)]
    mkv = [None] * DEPTH
    h = _norm_fwd(x2d, _row(p["g_mix_pre"][0]), name="in_norm")
    xin = x2d
    sv = []
    kv = hkv = None
    for l in range(DEPTH):
        s = {"xin": xin, "h": h}
        if q is not None:
            q.horizon = (l + 2) * GROUPS_PER_LAYER
        mkv[l] = _mm_nn(mn[l], wfull("w_mem_kv", l), name=f"mem_kv{l}", q=q)
        if l < N_A:
            proj = _mm_nn(h, wfull("w_in_a", l), name=f"in_proj{l}", q=q)
            ycat, hs = _lru_fwd(proj, p["w_conv_a"][l], _row(p["b_conv_a"][l]), wr_t[l], _row(p["b_rg_r"][l]),
                                wi_t[l], _row(p["b_rg_i"][l]), _row(p["lru_lambda"][l]), bsz, name=f"lru_fwd{l}", q=q)
            s["hs"] = hs
            qblk = 2 * MIX_WIDTH // MEM_WIDTH
        else:
            if l == N_A:
                kv = _mm_nn(hkv, wfull("w_kv", 0), name="kv_proj", q=q)
            proj = _mm_nn(h, wfull("w_in_b", l - N_A), name=f"in_proj{l}", q=q)
            ycat = _swa_fwd(proj, kv, p["sinks_b"][l - N_A], bsz, name=f"swa_fwd{l}", q=q)
            qblk = MIX_WIDTH // MEM_WIDTH
        ycat = _mem_attn_fwd(proj, qblk, mkv[l], ycat, bsz, name=f"mem_attn_fwd{l}", q=q)
        y = _mm_nn(ycat, wfull("w_mix_out", l), name=f"mix_out{l}", q=q, out_dtype=MXU)
        x1, (h2,) = _resid_norm_fwd(xin, y, _row(p["g_mix_post"][l]), [_row(p["g_ffn_pre"][l])], name=f"mix_resid{l}", q=q)
        up = _mm_nn_slots(h2, wfull("w_ffn_up", l), name=f"ffn_up{l}", q=q, out_dtype=MXU)
        act, ug, uv = _ffn_act_fwd(up, p["w_ffn_conv"][l], _row(p["b_ffn_conv"][l]), bsz, name=f"ffn_act{l}", q=q)
        f = _mm_nn(act, wfull("w_ffn_down", l), name=f"ffn_down{l}", q=q, out_dtype=MXU)
        s.update(proj=proj, qblk=qblk, ycat=ycat, y=y, x1=x1, h2=h2, up=up, ug=ug, uv=uv, act=act, f=f)
        sv.append(s)
        if l < DEPTH - 1:
            g_pres = [_row(p["g_mix_pre"][l + 1])] + ([_row(p["g_kv"])] if l + 1 == N_A else [])
            xin, hn = _resid_norm_fwd(x1, f, _row(p["g_ffn_post"][l]), g_pres, name=f"ffn_resid{l}", q=q)
            h = hn[0]
            if l + 1 == N_A:
                hkv = hn[1]
        else:
            g_tot, sq = _loss_fwd(x1, f, _row(p["g_ffn_post"][l]), tgt, name="loss")

    if q is not None:
        q.horizon = LAST_GROUP
    gs = {k: [None] * DEPTH for k in ("g_mix_pre", "g_mix_post", "g_ffn_pre", "g_ffn_post", "g_mem",
                                       "w_ffn_conv", "b_ffn_conv")}
    ga = {k: [None] * N_A for k in ("w_conv_a", "b_conv_a", "w_rg_r", "b_rg_r", "w_rg_i", "b_rg_i", "lru_lambda")}
    gsink = [None] * (DEPTH - N_A)
    dkv_cur, dkv_prev = [], []
    g_tot, df, _, gs["g_ffn_post"][DEPTH - 1] = _resid_norm_bwd(
        g_tot, [], None, [], sv[-1]["f"], _row(p["g_ffn_post"][DEPTH - 1]), name="loss_bwd")
    grad_x = None
    for l in reversed(range(DEPTH)):
        s = sv[l]
        dact = _mm_nt(df, wfull("w_ffn_down", l), name=f"d_act{l}", q=q, out_dtype=MXU)
        push_grad("w_ffn_down", l, _mm_tn(s["act"], df, name=f"dw_down{l}", q=q))
        dug, duv, gs["w_ffn_conv"][l], gs["b_ffn_conv"][l] = _ffn_act_bwd(
            s["up"], s["ug"], s["uv"], dact, p["w_ffn_conv"][l], bsz, name=f"ffn_act_bwd{l}", q=q)
        dh2 = _mm_ffn_dh(dug, duv, wfull("w_ffn_up", l), name=f"d_h2_{l}", q=q)
        up_slots = dict(slot_cols=2 * D_FF // N_CHIP, n_slots=N_CHIP)
        dwu = _mm_tn_slots(s["h2"], dug, name=f"dw_up_g{l}", q=q, **up_slots)
        push_grad("w_ffn_up", l, _mm_tn_slots(s["h2"], duv, name=f"dw_up_v{l}", q=q, out=dwu,
                                              first_slot=N_CHIP // 2, **up_slots))
        g1, dy, (gs["g_ffn_pre"][l],), gs["g_mix_post"][l] = _resid_norm_bwd(
            g_tot, [dh2], s["x1"], [_row(p["g_ffn_pre"][l])], s["y"], _row(p["g_mix_post"][l]), name=f"mix_resid_bwd{l}", q=q)
        dycat = _mm_nt(dy, wfull("w_mix_out", l), name=f"d_ycat{l}", q=q, out_dtype=MXU)
        push_grad("w_mix_out", l, _mm_tn(s["ycat"], dy, name=f"dw_mix_out{l}", q=q))
        if l < N_A:
            dproj, dwc, dbc, dwr, dbr, dwi, dbi, dlam = _lru_bwd(
                s["proj"], s["hs"], dycat, p["w_conv_a"][l], _row(p["b_conv_a"][l]), wr_t[l], _row(p["b_rg_r"][l]),
                wi_t[l], _row(p["b_rg_i"][l]), _row(p["lru_lambda"][l]), bsz, name=f"lru_bwd{l}", q=q)
            ga["w_conv_a"][l], ga["b_conv_a"][l], ga["lru_lambda"][l] = dwc, dbc[0], dlam[0]
            ga["w_rg_r"][l], ga["w_rg_i"][l] = _gate_blocks(dwr), _gate_blocks(dwi)
            ga["b_rg_r"][l] = dbr.reshape(LRU_BLOCKS, HEAD_DIM)
            ga["b_rg_i"][l] = dbi.reshape(LRU_BLOCKS, HEAD_DIM)
            w_in, j = "w_in_a", l
        else:
            dproj, dc, dp_, dsk = _swa_bwd(s["proj"], kv, p["sinks_b"][l - N_A], dycat, bsz, name=f"swa_bwd{l}", q=q)
            dkv_cur.append(dc)
            dkv_prev.append(dp_)
            gsink[l - N_A] = dsk[0, :SWA_HEADS]
            w_in, j = "w_in_b", l - N_A
        dproj, dmkv = _mem_attn_bwd(s["proj"], s["qblk"], mkv[l], dycat, dproj, bsz, name=f"mem_attn_bwd{l}", q=q)
        dh = _mm_nt(dproj, wfull(w_in, j), name=f"d_h{l}", q=q, out_dtype=MXU)
        push_grad(w_in, j, _mm_tn(s["h"], dproj, name=f"dw_in{l}", q=q))
        dmkv = dmkv.astype(MXU)
        dmn = _mm_nt(dmkv, wfull("w_mem_kv", l), name=f"d_mem_norm{l}", q=q)
        push_grad("w_mem_kv", l, _mm_tn(mn[l], dmkv, name=f"dw_mem_kv{l}", q=q))
        gs["g_mem"][l] = _norm_bwd_dg(dmn, mem2d, _row(p["g_mem"][l]), name=f"mem_norm_bwd{l}")
        dhs, g_pres = [dh], [_row(p["g_mix_pre"][l])]
        if l == N_A:
            dkv = _swa_dkv_combine(dkv_cur, dkv_prev, bsz, name="dkv_combine")
            dhs.append(_mm_nt(dkv, wfull("w_kv", 0), name="d_hkv", q=q, out_dtype=MXU))
            g_pres.append(_row(p["g_kv"]))
            push_grad("w_kv", 0, _mm_tn(hkv, dkv, name="dw_kv", q=q))
        if l > 0:
            g_tot, df, dgpre, gs["g_ffn_post"][l - 1] = _resid_norm_bwd(
                g1, dhs, s["xin"], g_pres, sv[l - 1]["f"], _row(p["g_ffn_post"][l - 1]), name=f"ffn_resid_bwd{l - 1}", q=q)
        else:
            grad_x, _, dgpre, _ = _resid_norm_bwd(g1, dhs, s["xin"], g_pres, None, None, name="in_norm_bwd", q=q)
        gs["g_mix_pre"][l] = dgpre[0]
        if l == N_A:
            g_kv = dgpre[1][0]

    grads = {}
    for k in ("g_mix_pre", "g_mix_post", "g_ffn_pre", "g_ffn_post", "g_mem", "b_ffn_conv"):
        grads[k] = jnp.concatenate(gs[k], axis=0)
    grads["w_ffn_conv"] = jnp.stack(gs["w_ffn_conv"])
    for k, v in ga.items():
        grads[k] = jnp.stack(v)
    grads["sinks_b"] = jnp.stack(gsink)
    grads["g_kv"] = g_kv
    return jnp.sum(sq), grad_x.reshape(bsz, t, d), grads


N_CHIP = 4
HALF_ALIGN = 16
MIN_PART_BYTES = 128 * 1024


def _full_shape(kind, shard_shape):
    l, r, c = shard_shape
    return {"row": (l, N_CHIP * r, c), "col": (l, r, N_CHIP * c), "slot": (N_CHIP, l, r, c)}[kind]


def _slot_view(ref, kind, shard_shape, s, hf, sub=(0, 1)):
    _, r, c = shard_shape
    rh = r // 2
    if hf is None:
        size = r // sub[1]
        start = sub[0] * size
    else:
        size = rh // sub[1]
        start = hf * rh + sub[0] * size
    if kind == "row":
        start = s * r + start
    if not isinstance(start, int):
        start = pl.multiple_of(start, HALF_ALIGN)
    rows = pl.ds(start, size)
    if kind == "row":
        return ref.at[:, rows, :]
    if kind == "col":
        return ref.at[:, rows, pl.ds(s * c, c)]
    return ref.at[s, :, rows, :]


def _half_view(ref, shard_shape, hf, sub=(0, 1)):
    rh = shard_shape[1] // 2
    size = rh // sub[1]
    return ref.at[:, pl.ds(pl.multiple_of(hf * rh + sub[0] * size, HALF_ALIGN), size), :]


def _with_slot(kind, s, fn):
    if kind != "col" or isinstance(s, int):
        fn(s)
        return
    for k in range(N_CHIP):
        @pl.when(s == k)
        def _(k=k):
            fn(k)


def _mesh_pos():
    return lax.axis_index("x"), lax.axis_index("y"), lax.axis_index("c")


def _other_chips(x, y):
    return [(1 - x, y), (x, 1 - y), (1 - x, 1 - y)]


ICI_BYTES_PER_US = 6.0e4
ICI_GATHER_BYTES_PER_US = 5.5e4
D2D_BYTES_PER_US = 4.0e5


class _Chunk:
    def __init__(self, group, cost, ins, out_shapes, alias, n_sem, start, finish, done, buffer=None, bind=None):
        self.group, self.cost, self.ins, self.out_shapes, self.alias, self.n_sem = group, cost, ins, out_shapes, alias, n_sem
        self.start, self.finish, self.done = start, finish, done
        self.buffer = buffer
        self.bind = bind

    def prepare(self):
        if self.bind is not None:
            self.bind(self)


def _merged(chunks):
    groups, by_buffer = [], {}
    for ch in chunks:
        key = None if ch.buffer is None else (id(ch.buffer[0]), ch.buffer[1])
        if key is not None and key in by_buffer:
            by_buffer[key].append(ch)
        else:
            groups.append([ch])
            if key is not None:
                by_buffer[key] = groups[-1]
    out = []
    for parts in groups:
        if len(parts) == 1:
            out.append(parts[0])
            continue
        offs = [sum(p.n_sem for p in parts[:i]) for i in range(len(parts))]

        def run(phase, ins, outs, ss, rs, b, parts=parts, offs=offs):
            for p, o in zip(parts, offs):
                getattr(p, phase)(ins, outs, ss, rs, b + o)

        def done(outs, parts=parts):
            for p in parts:
                p.done(outs)

        first = parts[0]
        out.append(_Chunk(first.group, sum(p.cost for p in parts), first.ins, first.out_shapes, first.alias,
                          sum(p.n_sem for p in parts), functools.partial(run, "start"),
                          functools.partial(run, "finish"), done))
    return out


LAST_GROUP = 1 << 30
MIN_CARRIED_US = 8.0


class _CommQueue:
    def __init__(self):
        self.pending = []
        self.flushes = 0
        self.horizon = LAST_GROUP

    def push(self, chunk):
        self.pending.append(chunk)

    def take(self, budget_us):
        got, used = [], 0.0
        for ch in sorted(self.pending, key=lambda ch: (ch.group, -ch.cost)):
            if ch.group >= self.horizon and ch.group != LAST_GROUP:
                continue
            if used + ch.cost <= budget_us and not self._shares_buffer(ch, got):
                got.append(ch)
                used += ch.cost
        if used < MIN_CARRIED_US:
            return []
        return self._taken(got)

    @staticmethod
    def _shares_buffer(ch, others):
        return ch.buffer is not None and any(
            o.buffer is not None and o.buffer[0] is ch.buffer[0] and o.buffer[1] != ch.buffer[1] for o in others)

    def _taken(self, got):
        self.pending = [ch for ch in self.pending if ch not in got]
        for ch in got:
            ch.prepare()
        return _merged(got)

    def flush(self, group=LAST_GROUP):
        while True:
            chunks = []
            for ch in self.pending:
                if ch.group <= group and not self._shares_buffer(ch, chunks):
                    chunks.append(ch)
            if not chunks:
                return
            _run_chunks(self._taken(chunks), name=f"comm_flush{self.flushes}")
            self.flushes += 1


def _run_chunks(chunks, *, name):
    ins = [a for ch in chunks for a in ch.ins]
    outs = [s for ch in chunks for s in ch.out_shapes]
    alias, offs = {}, []
    i0 = o0 = s0 = 0
    for ch in chunks:
        offs.append((i0, o0, s0))
        for ci, co in ch.alias.items():
            alias[i0 + ci] = o0 + co
        i0 += len(ch.ins)
        o0 += len(ch.out_shapes)
        s0 += ch.n_sem

    def body(*refs):
        send_sems, recv_sems = refs[i0 + o0:]
        for phase in ("start", "finish"):
            for ch, (a, b, s) in zip(chunks, offs):
                getattr(ch, phase)(refs[a:a + len(ch.ins)], refs[i0 + b:i0 + b + len(ch.out_shapes)],
                                   send_sems, recv_sems, s)

    hbm = pl.BlockSpec(memory_space=pl.ANY)
    res = pl.pallas_call(
        body, in_specs=[hbm] * i0, out_specs=[hbm] * o0, out_shape=outs,
        scratch_shapes=[pltpu.SemaphoreType.DMA((s0,)), pltpu.SemaphoreType.DMA((s0,))],
        input_output_aliases=alias, name=name, compiler_params=pltpu.CompilerParams(has_side_effects=True))(*ins)
    for ch, (_, b, _) in zip(chunks, offs):
        ch.done(list(res[b:b + len(ch.out_shapes)]))


def _remote(src, dst, send_sems, recv_sems, k, dev):
    return pltpu.make_async_remote_copy(src_ref=src, dst_ref=dst, send_sem=send_sems.at[k], recv_sem=recv_sems.at[k],
                                        device_id=dev, device_id_type=MESH_T)


def _gather_chunks(q, group, kind, shard, l, ready):
    _, r, c = shard.shape
    shp = (1, r, c)
    rh = r // 2
    parts = max(p for p in (8, 4, 2, 1)
                if (rh // p) % HALF_ALIGN == 0 and (p == 1 or (rh // p) * c * shard.dtype.itemsize >= MIN_PART_BYTES))
    part_bytes = (rh // parts) * c * shard.dtype.itemsize
    full_type = jax.ShapeDtypeStruct(_full_shape(kind, shp), shard.dtype)
    state = {"full": None, "parts_done": 0}

    def bind_first(ch):
        ch.ins, ch.alias = ([shard], {}) if state["full"] is None else ([shard, state["full"]], {1: 0})

    def bind_full(ch):
        ch.ins = [state["full"]]

    def make_part(p):
        sub = (p, parts)

        def any_part(full):
            return _slot_view(full, kind, shp, 0, 0, sub)

        def own_rows(src):
            return src.at[:, pl.ds(p * (r // parts), r // parts), :]

        def start1(ins, outs, ss, rs, b):
            x, y, c_ = _mesh_pos()
            src, full = ins[0].at[pl.ds(l, 1)], outs[0]
            _with_slot(kind, 2 * x + y, lambda s: pltpu.make_async_copy(
                own_rows(src), _slot_view(full, kind, shp, s, None, sub), ss.at[b + N_CHIP - 1]).start())
            for j, (ox, oy) in enumerate(_other_chips(x, y)):
                _with_slot(kind, 2 * x + y, lambda s, j=j, ox=ox, oy=oy: _remote(
                    _half_view(src, shp, c_, sub), _slot_view(full, kind, shp, s, c_, sub), ss, rs, b + j,
                    (ox, oy, c_)).start())

        def finish1(ins, outs, ss, rs, b):
            x, y, c_ = _mesh_pos()
            h = any_part(outs[0])
            for j in range(N_CHIP - 1):
                _remote(h, h, ss, rs, b + j, (x, y, 1 - c_)).wait()
            pltpu.make_async_copy(own_rows(ins[0].at[pl.ds(l, 1)]), _slot_view(outs[0], kind, shp, 0, None, sub),
                                  ss.at[b + N_CHIP - 1]).wait()

        def start2(ins, outs, ss, rs, b):
            x, y, c_ = _mesh_pos()
            for j, (ox, oy) in enumerate(_other_chips(x, y)):
                def forward(s, j=j):
                    v = _slot_view(outs[0], kind, shp, s, c_, sub)
                    _remote(v, v, ss, rs, b + j, (x, y, 1 - c_)).start()
                _with_slot(kind, 2 * ox + oy, forward)

        def finish2(ins, outs, ss, rs, b):
            x, y, c_ = _mesh_pos()
            h = any_part(outs[0])
            for j in range(N_CHIP - 1):
                _remote(h, h, ss, rs, b + j, (x, y, 1 - c_)).wait()

        def done2(outs):
            state["full"] = outs[0]
            state["parts_done"] += 1
            if state["parts_done"] == parts:
                ready(outs[0])

        def done1(outs):
            state["full"] = outs[0]
            q.push(_Chunk(group, 3 * part_bytes / D2D_BYTES_PER_US, None, [full_type], {0: 0}, N_CHIP - 1,
                          start2, finish2, done2, buffer=(state, 2), bind=bind_full))

        return _Chunk(group, 3 * part_bytes / ICI_GATHER_BYTES_PER_US, None, [full_type], None,
                      N_CHIP, start1, finish1, done1, buffer=(state, 1), bind=bind_first)

    for p in range(parts):
        q.push(make_part(p))


def _reduce_scatter_chunks(q, kind, grad, shard_shape, pos, name, ready):
    _, r, c = shard_shape
    shp = (1, r, c)
    rh = r // 2

    def start1(ins, outs, ss, rs, b):
        x, y, c_ = _mesh_pos()
        for s in range(N_CHIP):
            _remote(_slot_view(ins[0], kind, shp, s, 1 - c_), outs[0].at[s], ss, rs, b + s, (x, y, 1 - c_)).start()

    def finish1(ins, outs, ss, rs, b):
        x, y, c_ = _mesh_pos()
        for s in range(N_CHIP):
            _remote(outs[0].at[s], outs[0].at[s], ss, rs, b + s, (x, y, 1 - c_)).wait()

    def start2(ins, outs, ss, rs, b):
        x, y, c_ = _mesh_pos()
        for j, (ox, oy) in enumerate(_other_chips(x, y)):
            _remote(ins[0].at[2 * ox + oy], outs[0].at[j], ss, rs, b + j, (ox, oy, c_)).start()

    def finish2(ins, outs, ss, rs, b):
        x, y, c_ = _mesh_pos()
        for j in range(N_CHIP - 1):
            _remote(outs[0].at[j], outs[0].at[j], ss, rs, b + j, (x, y, 1 - c_)).wait()

    def start3(ins, outs, ss, rs, b):
        x, y, c_ = _mesh_pos()
        v = _half_view(outs[0], shp, c_)
        _remote(v, v, ss, rs, b, (x, y, 1 - c_)).start()

    def finish3(ins, outs, ss, rs, b):
        x, y, c_ = _mesh_pos()
        v = _half_view(outs[0], shp, c_)
        _remote(v, v, ss, rs, b, (x, y, 1 - c_)).wait()

    def done2(pair, outs):
        half = _rs_chip_add(pair, outs[0], shp, pos, name=f"rs_chip_add_{name}")
        q.push(_Chunk(LAST_GROUP, rh * c * 4 / D2D_BYTES_PER_US, [half], [jax.ShapeDtypeStruct(half.shape, half.dtype)],
                      {0: 0}, 1, start3, finish3, lambda o: ready(o[0])))

    def done1(outs):
        pair, wire = _rs_pair_add(grad, outs[0], kind, shp, pos, name=f"rs_pair_add_{name}")
        q.push(_Chunk(LAST_GROUP, 3 * rh * c * wire.dtype.itemsize / ICI_BYTES_PER_US, [wire],
                      [jax.ShapeDtypeStruct((N_CHIP - 1, 1, rh, c), wire.dtype)], {}, N_CHIP - 1,
                      start2, finish2, functools.partial(done2, pair)))

    q.push(_Chunk(LAST_GROUP, N_CHIP * rh * c * 4 / D2D_BYTES_PER_US, [grad],
                  [jax.ShapeDtypeStruct((N_CHIP, 1, rh, c), F32)], {}, N_CHIP, start1, finish1, done1))


def _allgather8(vec, *, name):
    r = vec.shape[0]
    n_dev = 8

    def body(v_ref, buf, send_sems, recv_sems):
        x, y, c = _mesh_pos()
        me = 4 * x + 2 * y + c
        copies = []
        for k in range(1, n_dev):
            kx, ky, kc = (k >> 2) & 1, (k >> 1) & 1, k & 1
            peer = ((1 - x) if kx else x, (1 - y) if ky else y, (1 - c) if kc else c)
            cp = _remote(v_ref, buf.at[me], send_sems, recv_sems, k - 1, peer)
            cp.start()
            copies.append(cp)
        buf[me] = v_ref[...]
        for cp in copies:
            cp.wait()

    vm = pl.BlockSpec(memory_space=pltpu.VMEM)
    return pl.pallas_call(
        body, in_specs=[vm], out_specs=vm, out_shape=jax.ShapeDtypeStruct((n_dev, r, LANE), F32),
        scratch_shapes=[pltpu.SemaphoreType.DMA((n_dev - 1,)), pltpu.SemaphoreType.DMA((n_dev - 1,))],
        name=name, compiler_params=pltpu.CompilerParams(has_side_effects=True, vmem_limit_bytes=VMEM_LIMIT_V7X))(vec)


def _allreduce8(vec, *, name):
    r = vec.shape[0]
    rh = r // 2

    def body(v_ref, o_ref, sib_ref, chips_ref, send_sems, recv_sems):
        x, y, c = _mesh_pos()
        sib = (x, y, 1 - c)
        me = 2 * x + y
        pair = _remote(v_ref, sib_ref, send_sems, recv_sems, 0, sib)
        pair.start()
        pair.wait()
        rows = pl.ds(pl.multiple_of(c * rh, SUBLANE), rh)
        chips_ref[me] = v_ref[rows, :] + sib_ref[rows, :]
        copies = []
        for j, (ox, oy) in enumerate(_other_chips(x, y)):
            cp = _remote(chips_ref.at[me], chips_ref.at[me], send_sems, recv_sems, 1 + j, (ox, oy, c))
            cp.start()
            copies.append(cp)
        for cp in copies:
            cp.wait()
        acc = chips_ref[0]
        for s in range(1, N_CHIP):
            acc = acc + chips_ref[s]
        o_ref[rows, :] = acc
        swap = _remote(o_ref.at[rows, :], o_ref.at[rows, :], send_sems, recv_sems, N_CHIP, sib)
        swap.start()
        swap.wait()

    vm = pl.BlockSpec(memory_space=pltpu.VMEM)
    return pl.pallas_call(
        body, in_specs=[vm], out_specs=vm, out_shape=jax.ShapeDtypeStruct((r, LANE), F32),
        scratch_shapes=[pltpu.VMEM((r, LANE), F32), pltpu.VMEM((N_CHIP, rh, LANE), F32),
                        pltpu.SemaphoreType.DMA((N_CHIP + 1,)), pltpu.SemaphoreType.DMA((N_CHIP + 1,))],
        name=name, compiler_params=pltpu.CompilerParams(has_side_effects=True, vmem_limit_bytes=VMEM_LIMIT_V7X))(vec)


def _rs_pair_add(g, recv, kind, shape, pos, *, name):
    l, r, c = shape
    rh = r // 2
    if kind == "row":
        gspec = pl.BlockSpec((None, rh, c), lambda s, i, pos: (i, 2 * s + pos[0], 0))
    elif kind == "col":
        gspec = pl.BlockSpec((None, rh, c), lambda s, i, pos: (i, pos[0], s))
    else:
        gspec = pl.BlockSpec((None, None, rh, c), lambda s, i, pos: (s, i, pos[0], 0))
    pspec = pl.BlockSpec((None, None, rh, c), lambda s, i, pos: (s, i, 0, 0))

    def body(pos_ref, g_ref, r_ref, p_ref, pw_ref):
        del pos_ref
        v = g_ref[...] + r_ref[...]
        p_ref[...] = v
        pw_ref[...] = v.astype(pw_ref.dtype)

    return pl.pallas_call(
        body,
        grid_spec=pltpu.PrefetchScalarGridSpec(
            num_scalar_prefetch=1, grid=(N_CHIP, l), in_specs=[gspec, pspec], out_specs=[pspec, pspec]),
        out_shape=[jax.ShapeDtypeStruct((N_CHIP, l, rh, c), F32), jax.ShapeDtypeStruct((N_CHIP, l, rh, c), MXU)],
        name=name, compiler_params=_cp((PAR, PAR)))(pos, g, recv)


def _rs_chip_add(p, recv, shape, pos, *, name):
    l, r, c = shape
    rh = r // 2

    def body(pos_ref, p_ref, r_ref, o_ref):
        del pos_ref
        acc = p_ref[...]
        for j in range(N_CHIP - 1):
            acc = acc + r_ref[j].astype(F32)
        o_ref[...] = acc

    return pl.pallas_call(
        body,
        grid_spec=pltpu.PrefetchScalarGridSpec(
            num_scalar_prefetch=1, grid=(l,),
            in_specs=[pl.BlockSpec((None, None, rh, c), lambda i, pos: (pos[1], i, 0, 0)),
                      pl.BlockSpec((N_CHIP - 1, None, rh, c), lambda i, pos: (0, i, 0, 0))],
            out_specs=pl.BlockSpec((None, rh, c), lambda i, pos: (i, pos[0], 0))),
        out_shape=jax.ShapeDtypeStruct((l, r, c), F32),
        name=name, compiler_params=_cp((PAR,)))(pos, p, recv)


ADAM_BLOCK_ELEMS = 384 * 1024


def _adam_math(w, g, m, v):
    c1 = 1.0 / (1.0 - ADAM_B1 ** ADAM_STEP)
    c2 = 1.0 / (1.0 - ADAM_B2 ** ADAM_STEP)
    nm = ADAM_B1 * m + (1.0 - ADAM_B1) * g
    nv = ADAM_B2 * v + (1.0 - ADAM_B2) * (g * g)
    return -ADAM_LR * ((nm * c1) / (jnp.sqrt(nv * c2) + ADAM_EPS) + ADAM_WD * w), nm, nv


def _adamw_layer(w, g, m, v, outs, l, *, name):
    _, r, c = w.shape
    tr = _tile(r, max(SUBLANE, ADAM_BLOCK_ELEMS // c // SUBLANE * SUBLANE), SUBLANE)

    def body(w_ref, g_ref, m_ref, v_ref, *rest):
        go_ref, d_ref, nm_ref, nv_ref = rest[4:]
        gg = g_ref[...]
        go_ref[...] = gg
        d_ref[...], nm_ref[...], nv_ref[...] = _adam_math(w_ref[...], gg, m_ref[...], v_ref[...])

    lay = pl.BlockSpec((None, tr, c), lambda j: (l, j, 0))
    hbm = pl.BlockSpec(memory_space=pl.ANY)
    return pl.pallas_call(
        body, grid=(r // tr,),
        in_specs=[lay, pl.BlockSpec((None, tr, c), lambda j: (0, j, 0)), lay, lay] + [hbm] * 4,
        out_specs=[lay] * 4, out_shape=[jax.ShapeDtypeStruct(w.shape, F32)] * 4,
        input_output_aliases={4 + i: i for i in range(4)},
        name=name, compiler_params=_cp((PAR,)))(w, g, m, v, *outs)


def _adamw(w, g, m, v, *, name):
    shape = w.shape
    if w.ndim == 2:
        w, g, m, v = (a[None] for a in (w, g, m, v))
    l, r, c = w.shape
    tr = _tile(r, max(SUBLANE, ADAM_BLOCK_ELEMS // c // SUBLANE * SUBLANE), SUBLANE)

    def body(w_ref, g_ref, m_ref, v_ref, d_ref, nm_ref, nv_ref):
        d_ref[...], nm_ref[...], nv_ref[...] = _adam_math(w_ref[...], g_ref[...], m_ref[...], v_ref[...])

    spec = pl.BlockSpec((None, tr, c), lambda i, j: (i, j, 0))
    outs = pl.pallas_call(
        body, grid=(l, r // tr), in_specs=[spec] * 4, out_specs=[spec] * 3,
        out_shape=[jax.ShapeDtypeStruct((l, r, c), F32)] * 3,
        name=name, compiler_params=_cp((PAR, PAR)))(w, g, m, v)
    return tuple(o.reshape(shape) for o in outs)


PACK_ROWS = 512 * LANE


def _pack(arrays):
    flat = jnp.concatenate([a.reshape(-1).astype(F32) for a in arrays])
    pad = (-flat.shape[0]) % PACK_ROWS
    return jnp.pad(flat, (0, pad)).reshape(-1, LANE)


def _unpack(packed, shapes):
    flat = packed.reshape(-1)
    out, off = [], 0
    for s in shapes:
        size = int(np.prod(s))
        out.append(flat[off:off + size].reshape(s))
        off += size
    return out


BIG = (("w_mem_kv", "row"), ("w_mix_out", "row"), ("w_ffn_up", "slot"), ("w_ffn_down", "row"),
       ("w_in_a", "slot"), ("w_in_b", "row"), ("w_kv", "row"))
COLUMN_SHARDED_AS_COLUMNS = ("w_in_a",)
SMALL_SHARDED = (("w_ffn_conv", 2), ("w_conv_a", 2), ("b_conv_a", 1), ("lru_lambda", 1))
SMALL_REPLICATED = ("g_mix_pre", "g_mix_post", "g_ffn_pre", "g_ffn_post", "g_mem", "b_ffn_conv",
                    "w_rg_r", "b_rg_r", "w_rg_i", "b_rg_i", "sinks_b", "g_kv")
WEIGHTS = ("g_mix_pre", "g_mix_post", "g_ffn_pre", "g_ffn_post", "g_mem", "w_mem_kv", "w_mix_out", "w_ffn_up",
           "w_ffn_conv", "b_ffn_conv", "w_ffn_down", "w_in_a", "w_conv_a", "b_conv_a", "w_rg_r", "b_rg_r", "w_rg_i",
           "b_rg_i", "lru_lambda", "w_in_b", "sinks_b", "g_kv", "w_kv")


def _slot_to_cols(a):
    s, l, r, c = a.shape
    return a.transpose(1, 2, 0, 3).reshape(l, r, s * c)


def _cols_to_slot(a):
    l, r, c4 = a.shape
    return a.reshape(l, r, N_CHIP, c4 // N_CHIP).transpose(2, 0, 1, 3)


GROUPS_PER_LAYER = 8


def _layer_weights(layer):
    names = [("w_mem_kv", layer), ("w_in_a", layer) if layer < N_A else ("w_in_b", layer - N_A)]
    if layer == N_A:
        names.append(("w_kv", 0))
    return names + [("w_mix_out", layer), ("w_ffn_up", layer), ("w_ffn_down", layer)]


def _train_step(x, mem, target, w, m, v):
    xi, yi, ci = _mesh_pos()
    chip = 2 * xi + yi
    pos = jnp.stack([ci, chip]).astype(jnp.int32)

    q = _CommQueue()
    kinds = dict(BIG)
    as3 = lambda a: a if a.ndim == 3 else a[None]
    w3, m3, v3 = ({k: as3(d[k]) for k, _ in BIG} for d in (w, m, v))
    shards = {k: w3[k].astype(MXU) for k, _ in BIG}

    gathered = {}

    def on_gathered(k, l, full):
        gathered[k, l] = _slot_to_cols(full) if k in COLUMN_SHARDED_AS_COLUMNS else full

    group_of = {}

    for layer in range(DEPTH):
        for i, (k, l) in enumerate(_layer_weights(layer)):
            group_of[k, l] = layer * GROUPS_PER_LAYER + i
            _gather_chunks(q, group_of[k, l], kinds[k], shards[k], l, functools.partial(on_gathered, k, l))

    def wfull(k, l):
        if (k, l) not in gathered:
            q.flush(group_of[k, l])
        return gathered[k, l]

    q.flush(1)

    big_out = {k: [lax.empty(w3[k].shape, F32) for _ in range(4)] for k, _ in BIG}

    def on_reduced(k, l, g):
        big_out[k] = _adamw_layer(w3[k], g, m3[k], v3[k], big_out[k], l, name=f"adamw_{k}{l}")

    def push_grad(k, l, g):
        if k in COLUMN_SHARDED_AS_COLUMNS:
            g = _cols_to_slot(g)
        _reduce_scatter_chunks(q, kinds[k], g, (1,) + w3[k].shape[1:], pos, f"{k}{l}", functools.partial(on_reduced, k, l))

    small_shapes = [w[k].shape for k, _ in SMALL_SHARDED]
    stacked = _allgather8(_pack([w[k] for k, _ in SMALL_SHARDED]), name="gather_small")
    per_chip = [_unpack(stacked[2 * s], small_shapes) for s in range(N_CHIP)]
    p = {k: w[k] for k in SMALL_REPLICATED}
    for i, (k, axis) in enumerate(SMALL_SHARDED):
        p[k] = jnp.concatenate([per_chip[s][i] for s in range(N_CHIP)], axis=axis)

    sq, grad_x, g = _local_step(x, mem, target, p, wfull, push_grad, q)
    loss = lax.psum(0.5 * sq / D_MODEL, ("x", "y", "c"))
    q.flush()

    small_names = [k for k, _ in SMALL_SHARDED] + list(SMALL_REPLICATED)
    summed = _allreduce8(_pack([g[k] for k in small_names]), name="allreduce_small")
    gsum = dict(zip(small_names, _unpack(summed, [p[k].shape for k in small_names])))
    for k, axis in SMALL_SHARDED:
        gsum[k] = lax.dynamic_slice_in_dim(gsum[k], chip * w[k].shape[axis], w[k].shape[axis], axis)

    delta, new_m, new_v = {}, {}, {}
    for k, _ in BIG:
        gsum[k], delta[k], new_m[k], new_v[k] = (o.reshape(w[k].shape) for o in big_out[k])
    packed = [_pack([d[k] for k in small_names]) for d in (w, gsum, m, v)]
    outs = _adamw(*packed, name="adamw_small")
    for d, o in zip((delta, new_m, new_v), outs):
        d.update(zip(small_names, _unpack(o, [w[k].shape for k in small_names])))
    return (loss, grad_x, *[gsum[k] for k in WEIGHTS], *[delta[k] for k in WEIGHTS],
            *[new_m[k] for k in WEIGHTS], *[new_v[k] for k in WEIGHTS])


def kernel(x, mem, g_mix_pre, g_mix_post, g_ffn_pre, g_ffn_post, g_mem, w_mem_kv, w_mix_out, w_ffn_up, w_ffn_conv, b_ffn_conv, w_ffn_down, w_in_a, w_conv_a, b_conv_a, w_rg_r, b_rg_r, w_rg_i, b_rg_i, lru_lambda, w_in_b, sinks_b, g_kv, w_kv, loss_target, m_g_mix_pre, m_g_mix_post, m_g_ffn_pre, m_g_ffn_post, m_g_mem, m_w_mem_kv, m_w_mix_out, m_w_ffn_up, m_w_ffn_conv, m_b_ffn_conv, m_w_ffn_down, m_w_in_a, m_w_conv_a, m_b_conv_a, m_w_rg_r, m_b_rg_r, m_w_rg_i, m_b_rg_i, m_lru_lambda, m_w_in_b, m_sinks_b, m_g_kv, m_w_kv, v_g_mix_pre, v_g_mix_post, v_g_ffn_pre, v_g_ffn_post, v_g_mem, v_w_mem_kv, v_w_mix_out, v_w_ffn_up, v_w_ffn_conv, v_b_ffn_conv, v_w_ffn_down, v_w_in_a, v_w_conv_a, v_b_conv_a, v_w_rg_r, v_b_rg_r, v_w_rg_i, v_b_rg_i, v_lru_lambda, v_w_in_b, v_sinks_b, v_g_kv, v_w_kv):
    args = (g_mix_pre, g_mix_post, g_ffn_pre, g_ffn_post, g_mem, w_mem_kv, w_mix_out, w_ffn_up, w_ffn_conv, b_ffn_conv, w_ffn_down, w_in_a, w_conv_a, b_conv_a, w_rg_r, b_rg_r, w_rg_i, b_rg_i, lru_lambda, w_in_b, sinks_b, g_kv, w_kv)
    ms = (m_g_mix_pre, m_g_mix_post, m_g_ffn_pre, m_g_ffn_post, m_g_mem, m_w_mem_kv, m_w_mix_out, m_w_ffn_up, m_w_ffn_conv, m_b_ffn_conv, m_w_ffn_down, m_w_in_a, m_w_conv_a, m_b_conv_a, m_w_rg_r, m_b_rg_r, m_w_rg_i, m_b_rg_i, m_lru_lambda, m_w_in_b, m_sinks_b, m_g_kv, m_w_kv)
    vs = (v_g_mix_pre, v_g_mix_post, v_g_ffn_pre, v_g_ffn_post, v_g_mem, v_w_mem_kv, v_w_mix_out, v_w_ffn_up, v_w_ffn_conv, v_b_ffn_conv, v_w_ffn_down, v_w_in_a, v_w_conv_a, v_b_conv_a, v_w_rg_r, v_b_rg_r, v_w_rg_i, v_b_rg_i, v_lru_lambda, v_w_in_b, v_sinks_b, v_g_kv, v_w_kv)
    return _train_step(x, mem, loss_target, dict(zip(WEIGHTS, args)), dict(zip(WEIGHTS, ms)), dict(zip(WEIGHTS, vs)))
```

```python
import functools
import math

import numpy as np
import jax
import jax.numpy as jnp
from jax import lax
from jax.experimental import pallas as pl
from jax.experimental.pallas import tpu as pltpu

F32 = jnp.float32
MXU = jnp.bfloat16

D_MODEL = 1024
HEAD_DIM = 64
MEM_LEN = 256
MEM_HEADS = 4
MEM_WIDTH = MEM_HEADS * HEAD_DIM
MIX_WIDTH = D_MODEL - MEM_WIDTH
LRU_BLOCKS = MIX_WIDTH // HEAD_DIM
LRU_CONV = 4
LRU_C = 8.0
SWA_HEADS = MIX_WIDTH // HEAD_DIM
SWA_KV_HEADS = 4
SWA_GROUP = SWA_HEADS // SWA_KV_HEADS
WINDOW = 128
D_FF = 2816
FFN_CONV = 3
EPS = 1e-6
DEPTH = 4
N_A = 2

ADAM_LR = 0.001
ADAM_B1 = 0.9
ADAM_B2 = 0.999
ADAM_EPS = 1e-08
ADAM_WD = 0.01
ADAM_STEP = 10

VMEM_LIMIT_V7X = 56 * 1024 * 1024
LANE = 128
SUBLANE = 8
GATE_TILE = 256
MESH_T = pl.DeviceIdType.MESH


def _alibi_slopes(n):
    def pow2_slopes(m):
        start = 2.0 ** (-8.0 / m)
        return [start ** (i + 1) for i in range(m)]
    c = 2 ** int(math.floor(math.log2(n)))
    s = pow2_slopes(c)
    if c != n:
        s = s + pow2_slopes(2 * c)[0::2][: n - c]
    return [float(np.float32(v)) for v in s]


SLOPES = _alibi_slopes(SWA_HEADS)


def _tile(n, cap, mult=LANE):
    best = None
    for t in range(mult, min(n, cap) + 1, mult):
        if n % t == 0:
            best = t
    return best if best is not None else n


def _cp(sem):
    return pltpu.CompilerParams(dimension_semantics=sem, vmem_limit_bytes=VMEM_LIMIT_V7X)


MM_VMEM_BUDGET = 40 * 1024 * 1024
HBM_BYTES_PER_US_V7X = 3.0e6
GRID_STEP_US = 0.35


def _divisors(n, mult):
    return [t for t in range(mult, n + 1, mult) if n % t == 0] or [n]


def _mm_tiles(m, k, n, out_bytes):
    best = None
    for tm in _divisors(m, 256):
        for tn in _divisors(n, LANE):
            vmem = 2 * (tm * k * 2 + k * tn * 2 + tm * tn * out_bytes)
            if vmem > MM_VMEM_BUDGET:
                continue
            steps = (m // tm) * (n // tn)
            b_reads = 1 if tn == n else m // tm
            traffic = m * k * 2 + k * n * 2 * b_reads + m * n * out_bytes
            first = tm * k * 2 + k * tn * 2
            cost = (traffic + first) / HBM_BYTES_PER_US_V7X + steps * GRID_STEP_US
            if best is None or cost < best[0]:
                best = (cost, tm, tn)
    return best[1], best[2]


def _mm_tn_tiles(k, m, n, whole_n=False):
    best = None
    for tm in _divisors(m, LANE):
        for tn in ([n] if whole_n else _divisors(n, LANE)):
            for tk in _divisors(k, 512):
                vmem = 2 * (tk * tm * 2 + tk * tn * 2 + tm * tn * 4)
                if vmem > MM_VMEM_BUDGET:
                    continue
                steps = (m // tm) * (n // tn) * (k // tk)
                traffic = k * m * 2 * (n // tn) + k * n * 2 * (m // tm) + m * n * 4
                cost = traffic / HBM_BYTES_PER_US_V7X + steps * GRID_STEP_US
                if best is None or cost < best[0]:
                    best = (cost, tk, tm, tn)
    return best[1], best[2], best[3]


ARB = "arbitrary"
PAR = "parallel"


def _rms_fwd(x, g):
    r = lax.rsqrt(jnp.mean(x * x, axis=-1, keepdims=True) + EPS)
    return x * r * g


def _rms_bwd(dy, x, g):
    r = lax.rsqrt(jnp.mean(x * x, axis=-1, keepdims=True) + EPS)
    xh = x * r
    gdy = dy * g
    dx = r * (gdy - xh * jnp.mean(gdy * xh, axis=-1, keepdims=True))
    dg = jnp.sum(dy * xh, axis=0, keepdims=True)
    return dx, dg


_GELU_K = math.sqrt(2.0 / math.pi)
_GELU_C = 0.044715


def _gelu(x):
    t = jnp.tanh(_GELU_K * (x + _GELU_C * x * x * x))
    return 0.5 * x * (1.0 + t)


def _gelu_and_grad(x):
    x2 = x * x
    t = jnp.tanh(_GELU_K * (x + _GELU_C * x2 * x))
    g = 0.5 * x * (1.0 + t)
    dg = 0.5 * (1.0 + t) + 0.5 * x * (1.0 - t * t) * (_GELU_K * (1.0 + 3.0 * _GELU_C * x2))
    return g, dg


def _shift_down(x, k, row):
    return jnp.where(row >= k, pltpu.roll(x, k, axis=0), 0.0)


def _shift_up(x, k, row):
    n = x.shape[0]
    return jnp.where(row < n - k, pltpu.roll(x, n - k, axis=0), 0.0)


def _shift_down_edge(x, k):
    r = pltpu.roll(x, k, axis=0)
    row = lax.broadcasted_iota(jnp.int32, (SUBLANE, x.shape[1]), 0)
    return jnp.concatenate([jnp.where(row >= k, r[:SUBLANE], 0.0), r[SUBLANE:]], axis=0)


def _shift_up_edge(x, k):
    n = x.shape[0]
    r = pltpu.roll(x, n - k, axis=0)
    row = lax.broadcasted_iota(jnp.int32, (SUBLANE, x.shape[1]), 0)
    return jnp.concatenate([r[:n - SUBLANE], jnp.where(row < SUBLANE - k, r[n - SUBLANE:], 0.0)], axis=0)


def _dot(a, b):
    return jnp.dot(a, b, preferred_element_type=F32)


def _dot_nt(a, b):
    return lax.dot_general(a, b, (((1,), (1,)), ((), ())), preferred_element_type=F32)


def _dot_tn(a, b):
    return lax.dot_general(a, b, (((0,), (0,)), ((), ())), preferred_element_type=F32)


MXU_FLOPS_PER_US = 7.0e8
HOST_US = {"lru_fwd": 44.0, "lru_bwd": 94.0, "swa_fwd": 60.0, "swa_bwd": 160.0, "mem_attn_fwd": 21.0,
           "mem_attn_bwd": 33.0, "ffn_act": 70.0, "ffn_act_bwd": 100.0, "resid": 22.0, "resid_bwd": 33.0}


def _hosted_call(body, *, grid, in_specs, out_specs, out_shape, args, name, aliases=None, scratch_shapes=(),
                 q=None, flops=0.0, budget_us=0.0):
    chunks = q.take(flops / MXU_FLOPS_PER_US + budget_us) if q is not None else []
    if not chunks:
        return pl.pallas_call(
            body, grid=grid, in_specs=in_specs, out_specs=out_specs, out_shape=out_shape,
            scratch_shapes=list(scratch_shapes), input_output_aliases=aliases or {}, name=name,
            compiler_params=_cp((ARB,) * len(grid)))(*args)
    single = not isinstance(out_shape, (list, tuple))
    o_shapes = [out_shape] if single else list(out_shape)
    o_specs = [out_specs] if single else list(out_specs)
    n_in, n_out, n_scr = len(args), len(o_shapes), len(scratch_shapes)
    c_ins = [a for ch in chunks for a in ch.ins]
    c_outs = [s for ch in chunks for s in ch.out_shapes]
    alias = dict(aliases or {})
    in_off, out_off, sem_off = [], [], []
    i0 = o0 = s0 = 0
    for ch in chunks:
        in_off.append(i0)
        out_off.append(o0)
        sem_off.append(s0)
        for ci, co in ch.alias.items():
            alias[n_in + i0 + ci] = n_out + o0 + co
        i0 += len(ch.ins)
        o0 += len(ch.out_shapes)
        s0 += ch.n_sem

    def wrapped(*refs):
        ins = refs[:n_in]
        cin = refs[n_in:n_in + i0]
        outs = refs[n_in + i0:n_in + i0 + n_out]
        cout = refs[n_in + i0 + n_out:n_in + i0 + n_out + o0]
        scr = refs[n_in + i0 + n_out + o0:n_in + i0 + n_out + o0 + n_scr]
        send_sems, recv_sems = refs[n_in + i0 + n_out + o0 + n_scr:]
        first = functools.reduce(lambda u, v: u & v, [pl.program_id(d) == 0 for d in range(len(grid))])
        last = functools.reduce(lambda u, v: u & v, [pl.program_id(d) == grid[d] - 1 for d in range(len(grid))])

        def each(phase):
            for ch, a, b, s in zip(chunks, in_off, out_off, sem_off):
                getattr(ch, phase)(cin[a:a + len(ch.ins)], cout[b:b + len(ch.out_shapes)], send_sems, recv_sems, s)

        pl.when(first)(lambda: each("start"))
        body(*ins, *outs, *scr)
        pl.when(last)(lambda: each("finish"))

    hbm = pl.BlockSpec(memory_space=pl.ANY)
    res = pl.pallas_call(
        wrapped, grid=grid, in_specs=list(in_specs) + [hbm] * i0, out_specs=o_specs + [hbm] * o0,
        out_shape=o_shapes + c_outs,
        scratch_shapes=list(scratch_shapes) + [pltpu.SemaphoreType.DMA((s0,)), pltpu.SemaphoreType.DMA((s0,))],
        input_output_aliases=alias, name=name,
        compiler_params=pltpu.CompilerParams(dimension_semantics=(ARB,) * len(grid), vmem_limit_bytes=VMEM_LIMIT_V7X,
                                             has_side_effects=True))(*args, *c_ins)
    for ch, b in zip(chunks, out_off):
        ch.done(list(res[n_out + b:n_out + b + len(ch.out_shapes)]))
    return res[0] if single else list(res[:n_out])


def _mm_nn(a, b, *, name, q=None, out_dtype=F32):
    m, k = a.shape
    n = b.shape[-1]
    tm, tn = _mm_tiles(m, k, n, jnp.dtype(out_dtype).itemsize)

    def body(a_ref, b_ref, o_ref):
        o_ref[...] = _dot(a_ref[...], b_ref[...]).astype(o_ref.dtype)

    return _hosted_call(
        body, grid=(m // tm, n // tn),
        in_specs=[pl.BlockSpec((tm, k), lambda i, j: (i, 0)),
                  pl.BlockSpec((None, k, tn), lambda i, j: (0, 0, j))],
        out_specs=pl.BlockSpec((tm, tn), lambda i, j: (i, j)),
        out_shape=jax.ShapeDtypeStruct((m, n), out_dtype),
        args=(a, b), name=name, q=q, flops=2.0 * m * k * n)


def _mm_nt(a, b, *, name, q=None, out_dtype=F32):
    m, k = a.shape
    n = b.shape[-2]
    tm, tn = _mm_tiles(m, k, n, jnp.dtype(out_dtype).itemsize)

    def body(a_ref, b_ref, o_ref):
        o_ref[...] = _dot_nt(a_ref[...], b_ref[...]).astype(o_ref.dtype)

    return _hosted_call(
        body, grid=(m // tm, n // tn),
        in_specs=[pl.BlockSpec((tm, k), lambda i, j: (i, 0)),
                  pl.BlockSpec((None, tn, k), lambda i, j: (0, j, 0))],
        out_specs=pl.BlockSpec((tm, tn), lambda i, j: (i, j)),
        out_shape=jax.ShapeDtypeStruct((m, n), out_dtype),
        args=(a, b), name=name, q=q, flops=2.0 * m * k * n)


def _mm_nn_slots(a, b4, *, name, q=None, out_dtype=F32):
    m, k = a.shape
    s_, _, _, c = b4.shape
    ob = jnp.dtype(out_dtype).itemsize
    tm = max(t for t in _divisors(m, 256) if 2 * (t * k * 2 + k * c * 2 + t * c * ob) <= MM_VMEM_BUDGET)

    def body(a_ref, b_ref, o_ref):
        o_ref[...] = _dot(a_ref[...], b_ref[...]).astype(o_ref.dtype)

    return _hosted_call(
        body, grid=(m // tm, s_),
        in_specs=[pl.BlockSpec((tm, k), lambda i, j: (i, 0)),
                  pl.BlockSpec((None, None, k, c), lambda i, j: (j, 0, 0, 0))],
        out_specs=pl.BlockSpec((tm, c), lambda i, j: (i, j)),
        out_shape=jax.ShapeDtypeStruct((m, s_ * c), out_dtype),
        args=(a, b4), name=name, q=q, flops=2.0 * m * k * s_ * c)


def _mm_tn_slots(a, b, *, name, slot_cols, n_slots, first_slot=0, q=None, out=None):
    k, m = a.shape
    c = slot_cols
    tk, tm, _ = _mm_tn_tiles(k, m, c, whole_n=True)

    def body(a_ref, b_ref, *rest):
        o_ref = rest[-1]
        part = _dot_tn(a_ref[...], b_ref[...])

        @pl.when(pl.program_id(2) == 0)
        def _():
            o_ref[...] = part

        @pl.when(pl.program_id(2) > 0)
        def _():
            o_ref[...] += part

    in_specs = [pl.BlockSpec((tk, tm), lambda i, j, s: (s, i)), pl.BlockSpec((tk, c), lambda i, j, s: (s, j))]
    args = (a, b)
    if out is not None:
        in_specs.append(pl.BlockSpec(memory_space=pl.ANY))
        args = (a, b, out)
    return _hosted_call(
        body, grid=(m // tm, b.shape[-1] // c, k // tk), in_specs=in_specs,
        out_specs=pl.BlockSpec((None, None, tm, c), lambda i, j, s: (first_slot + j, 0, i, 0)),
        out_shape=jax.ShapeDtypeStruct((n_slots, 1, m, c), F32),
        aliases={2: 0} if out is not None else None,
        args=args, name=name, q=q, flops=2.0 * m * k * b.shape[-1])


def _mm_tn(a, b, *, name, q=None, out=None, n_total=None, col_block_offset=0):
    k, m = a.shape
    n = b.shape[-1]
    tk, tm, tn = _mm_tn_tiles(k, m, n)
    off = col_block_offset * (n // tn)

    def body(a_ref, b_ref, *rest):
        o_ref = rest[-1]
        part = _dot_tn(a_ref[...], b_ref[...])

        @pl.when(pl.program_id(2) == 0)
        def _():
            o_ref[...] = part

        @pl.when(pl.program_id(2) > 0)
        def _():
            o_ref[...] += part

    in_specs = [pl.BlockSpec((tk, tm), lambda i, j, s: (s, i)), pl.BlockSpec((tk, tn), lambda i, j, s: (s, j))]
    args = (a, b)
    if out is not None:
        in_specs.append(pl.BlockSpec(memory_space=pl.ANY))
        args = (a, b, out)
    return _hosted_call(
        body, grid=(m // tm, n // tn, k // tk), in_specs=in_specs,
        out_specs=pl.BlockSpec((None, tm, tn), lambda i, j, s: (0, i, j + off)),
        out_shape=jax.ShapeDtypeStruct((1, m, n_total or n), F32),
        aliases={2: 0} if out is not None else None,
        args=args, name=name, q=q, flops=2.0 * m * k * n)


def _mm_ffn_dh(dg, dv, w4, *, name, q=None):
    m, f = dg.shape
    n_slots, _, d, c = w4.shape
    tm, tn = _mm_tiles(m, 2 * f, d, 4)

    def body(dg_ref, dv_ref, *rest):
        w_refs, o_ref = rest[:n_slots], rest[n_slots]
        acc = None
        for s, w_ref in enumerate(w_refs):
            x_ref = dg_ref if s < n_slots // 2 else dv_ref
            off = (s % (n_slots // 2)) * c
            part = _dot_nt(x_ref[:, off:off + c], w_ref[...])
            acc = part if acc is None else acc + part
        o_ref[...] = acc.astype(o_ref.dtype)

    wspec = lambda s: pl.BlockSpec((None, None, tn, c), lambda i, j: (s, 0, j, 0))
    return _hosted_call(
        body, grid=(m // tm, d // tn),
        in_specs=[pl.BlockSpec((tm, f), lambda i, j: (i, 0)),
                  pl.BlockSpec((tm, f), lambda i, j: (i, 0))] + [wspec(s) for s in range(n_slots)],
        out_specs=pl.BlockSpec((tm, tn), lambda i, j: (i, j)),
        out_shape=jax.ShapeDtypeStruct((m, d), MXU),
        args=(dg, dv) + (w4,) * n_slots, name=name, q=q, flops=4.0 * m * f * d)


def _norm_fwd(x, g, *, name):
    n, d = x.shape
    tm = _tile(n, 256, SUBLANE)

    def body(x_ref, g_ref, o_ref):
        o_ref[...] = _rms_fwd(x_ref[...], g_ref[...]).astype(o_ref.dtype)

    return pl.pallas_call(
        body, grid=(n // tm,),
        in_specs=[pl.BlockSpec((tm, d), lambda i: (i, 0)), pl.BlockSpec((1, d), lambda i: (0, 0))],
        out_specs=pl.BlockSpec((tm, d), lambda i: (i, 0)),
        out_shape=jax.ShapeDtypeStruct((n, d), MXU),
        name=name, compiler_params=_cp((PAR,)))(x, g)


def _norm_bwd_dg(dy, x, g, *, name):
    n, d = x.shape
    tm = _tile(n, 256, SUBLANE)

    def body(dy_ref, x_ref, g_ref, dg_ref):
        @pl.when(pl.program_id(0) == 0)
        def _():
            dg_ref[...] = jnp.zeros_like(dg_ref)
        _, dg = _rms_bwd(dy_ref[...], x_ref[...], g_ref[...])
        dg_ref[...] += dg

    return pl.pallas_call(
        body, grid=(n // tm,),
        in_specs=[pl.BlockSpec((tm, d), lambda i: (i, 0)), pl.BlockSpec((tm, d), lambda i: (i, 0)),
                  pl.BlockSpec((1, d), lambda i: (0, 0))],
        out_specs=pl.BlockSpec((1, d), lambda i: (0, 0)),
        out_shape=jax.ShapeDtypeStruct((1, d), F32),
        name=name, compiler_params=_cp((ARB,)))(dy, x, g)


def _resid_norm_fwd(x, y, g_post, g_pres, *, name, q=None):
    n, d = x.shape
    tm = _tile(n, 256, SUBLANE)
    nh = len(g_pres)

    def body(x_ref, y_ref, gp_ref, *rest):
        gpre = rest[:nh]
        xo_ref = rest[nh]
        h_refs = rest[nh + 1:]
        xo = x_ref[...] + _rms_fwd(y_ref[...].astype(F32), gp_ref[...])
        xo_ref[...] = xo
        for g_ref, h_ref in zip(gpre, h_refs):
            h_ref[...] = _rms_fwd(xo, g_ref[...]).astype(h_ref.dtype)

    row = pl.BlockSpec((tm, d), lambda i: (i, 0))
    vec = pl.BlockSpec((1, d), lambda i: (0, 0))
    outs = _hosted_call(
        body, grid=(n // tm,),
        in_specs=[row, row, vec] + [vec] * nh,
        out_specs=[row] + [row] * nh,
        out_shape=[jax.ShapeDtypeStruct((n, d), F32)] + [jax.ShapeDtypeStruct((n, d), MXU)] * nh,
        args=(x, y, g_post, *g_pres), name=name, q=q, budget_us=HOST_US["resid"])
    return outs[0], list(outs[1:])


def _loss_fwd(x, y, g_post, target, *, name):
    n, d = x.shape
    tm = _tile(n, 256, SUBLANE)

    def body(x_ref, y_ref, gp_ref, t_ref, dx_ref, sq_ref):
        @pl.when(pl.program_id(0) == 0)
        def _():
            sq_ref[...] = jnp.zeros_like(sq_ref)
        err = x_ref[...] + _rms_fwd(y_ref[...].astype(F32), gp_ref[...]) - t_ref[...]
        dx_ref[...] = err * (1.0 / d)
        sq_ref[...] += jnp.sum(err * err, axis=0, keepdims=True)

    row = pl.BlockSpec((tm, d), lambda i: (i, 0))
    vec = pl.BlockSpec((1, d), lambda i: (0, 0))
    return pl.pallas_call(
        body, grid=(n // tm,),
        in_specs=[row, row, vec, row],
        out_specs=[row, vec],
        out_shape=[jax.ShapeDtypeStruct((n, d), F32), jax.ShapeDtypeStruct((1, d), F32)],
        name=name, compiler_params=_cp((ARB,)))(x, y, g_post, target)


def _resid_norm_bwd(dx_out, dhs, x_out, g_pres, y, g_post, *, name, q=None):
    n, d = dx_out.shape
    tm = _tile(n, 256, SUBLANE)
    nh = len(dhs)
    has_y = y is not None

    def body(*refs):
        it = iter(refs)
        dxo_ref = next(it)
        dh_refs = [next(it) for _ in range(nh)]
        xo_ref = next(it) if nh else None
        gpre_refs = [next(it) for _ in range(nh)]
        y_ref = next(it) if has_y else None
        gpost_ref = next(it) if has_y else None
        g_out = next(it)
        dy_out = next(it) if has_y else None
        dgpre_out = [next(it) for _ in range(nh)]
        dgpost_out = next(it) if has_y else None

        @pl.when(pl.program_id(0) == 0)
        def _():
            for r in dgpre_out:
                r[...] = jnp.zeros_like(r)
            if has_y:
                dgpost_out[...] = jnp.zeros_like(dgpost_out)

        g = dxo_ref[...]
        if nh:
            xo = xo_ref[...]
            for dh_ref, gp_ref, dg_ref in zip(dh_refs, gpre_refs, dgpre_out):
                dx, dg = _rms_bwd(dh_ref[...].astype(F32), xo, gp_ref[...])
                g = g + dx
                dg_ref[...] += dg
        g_out[...] = g
        if has_y:
            dy, dg = _rms_bwd(g, y_ref[...].astype(F32), gpost_ref[...])
            dy_out[...] = dy.astype(dy_out.dtype)
            dgpost_out[...] += dg

    row = pl.BlockSpec((tm, d), lambda i: (i, 0))
    vec = pl.BlockSpec((1, d), lambda i: (0, 0))
    ins, in_specs = [dx_out], [row]
    ins += list(dhs)
    in_specs += [row] * nh
    if nh:
        ins.append(x_out)
        in_specs.append(row)
    ins += list(g_pres)
    in_specs += [vec] * nh
    if has_y:
        ins += [y, g_post]
        in_specs += [row, vec]
    out_specs, out_shape = [row], [jax.ShapeDtypeStruct((n, d), F32)]
    if has_y:
        out_specs.append(row)
        out_shape.append(jax.ShapeDtypeStruct((n, d), MXU))
    out_specs += [vec] * nh
    out_shape += [jax.ShapeDtypeStruct((1, d), F32)] * nh
    if has_y:
        out_specs.append(vec)
        out_shape.append(jax.ShapeDtypeStruct((1, d), F32))
    outs = list(_hosted_call(
        body, grid=(n // tm,), in_specs=in_specs, out_specs=out_specs, out_shape=out_shape,
        args=tuple(ins), name=name, q=q, budget_us=HOST_US["resid_bwd"]))
    g = outs.pop(0)
    dy = outs.pop(0) if has_y else None
    dgpre = [outs.pop(0) for _ in range(nh)]
    dgpost = outs.pop(0) if has_y else None
    return g, dy, dgpre, dgpost


def _ffn_conv(up, w_ref, b_ref):
    return (w_ref[0:1, :] * _shift_down_edge(up, 2) + w_ref[1:2, :] * _shift_down_edge(up, 1)
            + w_ref[2:3, :] * up + b_ref[...])


def _ffn_act_fwd(up, wconv, bconv, bsz, *, name, q=None):
    n, f2 = up.shape
    f = f2 // 2
    t = n // bsz
    tc = _tile(f, 256)
    nf = f // tc

    def body(ug_ref, uv_ref, wg_ref, wv_ref, bg_ref, bv_ref, o_ref, dag_ref, dav_ref):
        g = _ffn_conv(ug_ref[...].astype(F32), wg_ref, bg_ref)
        v = _ffn_conv(uv_ref[...].astype(F32), wv_ref, bv_ref)
        gl, dgl = _gelu_and_grad(g)
        dag_ref[...] = (v * dgl).astype(dag_ref.dtype)
        dav_ref[...] = gl.astype(dav_ref.dtype)
        o_ref[...] = (gl * v).astype(o_ref.dtype)

    blk = pl.BlockSpec((t, tc), lambda b, j: (b, j))
    return _hosted_call(
        body, grid=(bsz, nf),
        in_specs=[blk, pl.BlockSpec((t, tc), lambda b, j: (b, j + nf)),
                  pl.BlockSpec((FFN_CONV, tc), lambda b, j: (0, j)),
                  pl.BlockSpec((FFN_CONV, tc), lambda b, j: (0, j + nf)),
                  pl.BlockSpec((1, tc), lambda b, j: (0, j)),
                  pl.BlockSpec((1, tc), lambda b, j: (0, j + nf))],
        out_specs=[blk, blk, blk],
        out_shape=[jax.ShapeDtypeStruct((n, f), MXU)] * 3,
        args=(up, up, wconv, wconv, bconv, bconv), name=name, q=q, budget_us=HOST_US["ffn_act"])


def _ffn_act_bwd(up, ug, uv, dact, wconv, bsz, *, name, q=None):
    n, f2 = up.shape
    f = f2 // 2
    t = n // bsz
    tc = _tile(f, 256)
    nf = f // tc

    def body(xg_ref, xv_ref, g_ref, v_ref, da_ref, wg_ref, wv_ref,
             dug_ref, duv_ref, dwg_ref, dwv_ref, dbg_ref, dbv_ref):
        @pl.when(pl.program_id(1) == 0)
        def _():
            for r in (dwg_ref, dwv_ref, dbg_ref, dbv_ref):
                r[...] = jnp.zeros_like(r)

        da = da_ref[...].astype(F32)
        dg = da * g_ref[...].astype(F32)
        dv = da * v_ref[...].astype(F32)

        def conv_bwd(du, w_ref, x_ref, dx_ref, dw_ref, db_ref):
            du1, du2 = _shift_up_edge(du, 1), _shift_up_edge(du, 2)
            dx_ref[...] = (w_ref[2:3, :] * du + w_ref[1:2, :] * du1 + w_ref[0:1, :] * du2).astype(dx_ref.dtype)
            x = x_ref[...].astype(F32)
            dw_ref[0:1, :] += jnp.sum(x * du2, axis=0, keepdims=True)
            dw_ref[1:2, :] += jnp.sum(x * du1, axis=0, keepdims=True)
            dw_ref[2:3, :] += jnp.sum(x * du, axis=0, keepdims=True)
            db_ref[...] += jnp.sum(du, axis=0, keepdims=True)

        conv_bwd(dg, wg_ref, xg_ref, dug_ref, dwg_ref, dbg_ref)
        conv_bwd(dv, wv_ref, xv_ref, duv_ref, dwv_ref, dbv_ref)

    blk = pl.BlockSpec((t, tc), lambda j, b: (b, j))
    wspec = pl.BlockSpec((FFN_CONV, tc), lambda j, b: (0, j))
    bspec = pl.BlockSpec((1, tc), lambda j, b: (0, j))
    outs = _hosted_call(
        body, grid=(nf, bsz),
        in_specs=[blk, pl.BlockSpec((t, tc), lambda j, b: (b, j + nf)), blk, blk, blk,
                  wspec, pl.BlockSpec((FFN_CONV, tc), lambda j, b: (0, j + nf))],
        out_specs=[blk, blk, wspec, wspec, bspec, bspec],
        out_shape=[jax.ShapeDtypeStruct((n, f), MXU), jax.ShapeDtypeStruct((n, f), MXU),
                   jax.ShapeDtypeStruct((FFN_CONV, f), F32), jax.ShapeDtypeStruct((FFN_CONV, f), F32),
                   jax.ShapeDtypeStruct((1, f), F32), jax.ShapeDtypeStruct((1, f), F32)],
        args=(up, up, ug, uv, dact, wconv, wconv), name=name, q=q, budget_us=HOST_US["ffn_act_bwd"])
    dug, duv, dwg, dwv, dbg, dbv = outs
    return dug, duv, jnp.concatenate([dwg, dwv], axis=1), jnp.concatenate([dbg, dbv], axis=1)


def _mem_attn_fwd(proj, q_col_block, mkv, ycat, bsz, *, name, q=None):
    n = proj.shape[0]
    t = n // bsz
    tq = _tile(t, 512, SUBLANE)
    nt = t // tq
    scale = HEAD_DIM ** -0.5

    def body(q_ref, kv_ref, old_ref, o_ref):
        del old_ref
        outs = []
        for h in range(MEM_HEADS):
            sl = slice(h * HEAD_DIM, (h + 1) * HEAD_DIM)
            q = q_ref[:, sl].astype(MXU)
            k = kv_ref[:, sl].astype(MXU)
            v = kv_ref[:, MEM_WIDTH + h * HEAD_DIM: MEM_WIDTH + (h + 1) * HEAD_DIM].astype(MXU)
            s = _dot_nt(q, k) * scale
            m = jnp.max(s, axis=-1, keepdims=True)
            p = jnp.exp(s - m)
            p = p / jnp.sum(p, axis=-1, keepdims=True)
            outs.append(_dot(p.astype(MXU), v))
        o_ref[...] = jnp.concatenate(outs, axis=-1).astype(o_ref.dtype)

    return _hosted_call(
        body, grid=(bsz, nt),
        in_specs=[pl.BlockSpec((tq, MEM_WIDTH), lambda b, i: (b * nt + i, q_col_block)),
                  pl.BlockSpec((MEM_LEN, 2 * MEM_WIDTH), lambda b, i: (b, 0)),
                  pl.BlockSpec(memory_space=pl.ANY)],
        out_specs=pl.BlockSpec((tq, MEM_WIDTH), lambda b, i: (b * nt + i, MIX_WIDTH // MEM_WIDTH)),
        out_shape=jax.ShapeDtypeStruct(ycat.shape, ycat.dtype),
        aliases={2: 0}, args=(proj, mkv, ycat), name=name, q=q, budget_us=HOST_US["mem_attn_fwd"])


def _mem_attn_bwd(proj, q_col_block, mkv, dycat, dproj, bsz, *, name, q=None):
    n = proj.shape[0]
    t = n // bsz
    tq = _tile(t, 512, SUBLANE)
    nt = t // tq
    scale = HEAD_DIM ** -0.5

    def body(q_ref, kv_ref, do_ref, old_ref, dq_ref, dkv_ref):
        del old_ref

        @pl.when(pl.program_id(1) == 0)
        def _():
            dkv_ref[...] = jnp.zeros_like(dkv_ref)

        dqs, dks, dvs = [], [], []
        for h in range(MEM_HEADS):
            sl = slice(h * HEAD_DIM, (h + 1) * HEAD_DIM)
            q = q_ref[:, sl].astype(MXU)
            k = kv_ref[:, sl].astype(MXU)
            v = kv_ref[:, MEM_WIDTH + h * HEAD_DIM: MEM_WIDTH + (h + 1) * HEAD_DIM].astype(MXU)
            do = do_ref[:, sl].astype(MXU)
            s = _dot_nt(q, k) * scale
            m = jnp.max(s, axis=-1, keepdims=True)
            p = jnp.exp(s - m)
            p = p / jnp.sum(p, axis=-1, keepdims=True)
            dvs.append(_dot_tn(p.astype(MXU), do))
            dp = _dot_nt(do, v)
            ds = (p * (dp - jnp.sum(dp * p, axis=-1, keepdims=True)) * scale).astype(MXU)
            dqs.append(_dot(ds, k))
            dks.append(_dot_tn(ds, q))
        dq_ref[...] = jnp.concatenate(dqs, axis=-1).astype(dq_ref.dtype)
        dkv_ref[...] += jnp.concatenate(dks + dvs, axis=-1)

    return _hosted_call(
        body, grid=(bsz, nt),
        in_specs=[pl.BlockSpec((tq, MEM_WIDTH), lambda b, i: (b * nt + i, q_col_block)),
                  pl.BlockSpec((MEM_LEN, 2 * MEM_WIDTH), lambda b, i: (b, 0)),
                  pl.BlockSpec((tq, MEM_WIDTH), lambda b, i: (b * nt + i, MIX_WIDTH // MEM_WIDTH)),
                  pl.BlockSpec(memory_space=pl.ANY)],
        out_specs=[pl.BlockSpec((tq, MEM_WIDTH), lambda b, i: (b * nt + i, q_col_block)),
                   pl.BlockSpec((MEM_LEN, 2 * MEM_WIDTH), lambda b, i: (b, 0))],
        out_shape=[jax.ShapeDtypeStruct(dproj.shape, dproj.dtype),
                   jax.ShapeDtypeStruct((bsz * MEM_LEN, 2 * MEM_WIDTH), F32)],
        aliases={3: 0}, args=(proj, mkv, dycat, dproj), name=name, q=q, budget_us=HOST_US["mem_attn_bwd"])


def _swa_scores(q, k, h, dist, mask, sink):
    s = _dot_nt(q, k) * (HEAD_DIM ** -0.5)
    s = jnp.where(mask, s - SLOPES[h] * dist, -jnp.inf)
    m = jnp.maximum(jnp.max(s, axis=-1, keepdims=True), sink)
    p = jnp.exp(s - m)
    psink = jnp.exp(sink - m)
    inv = 1.0 / (jnp.sum(p, axis=-1, keepdims=True) + psink)
    return p * inv, psink * inv


def _swa_mask(n):
    qi = lax.broadcasted_iota(jnp.int32, (WINDOW, 2 * WINDOW), 0) + WINDOW
    ki = lax.broadcasted_iota(jnp.int32, (WINDOW, 2 * WINDOW), 1)
    dist = qi - ki
    mask = (dist >= 0) & (dist < WINDOW) & ((n > 0) | (ki >= WINDOW))
    return dist.astype(F32), mask


def _swa_fwd(proj, kv, sinks, bsz, *, name, q=None):
    n_tok = proj.shape[0]
    nb = n_tok // bsz // WINDOW
    kvw = SWA_KV_HEADS * HEAD_DIM

    def body(sink_ref, q_ref, kvp_ref, kvc_ref, o_ref):
        n = pl.program_id(1)
        dist, mask = _swa_mask(n)
        kk = jnp.concatenate([kvp_ref[:, :kvw], kvc_ref[:, :kvw]], axis=0).astype(MXU)
        vv = jnp.concatenate([kvp_ref[:, kvw:], kvc_ref[:, kvw:]], axis=0).astype(MXU)
        outs = []
        for h in range(SWA_HEADS):
            c = h // SWA_GROUP
            q = q_ref[:, h * HEAD_DIM:(h + 1) * HEAD_DIM].astype(MXU)
            p, _ = _swa_scores(q, kk[:, c * HEAD_DIM:(c + 1) * HEAD_DIM], h, dist, mask, sink_ref[h])
            outs.append(_dot(p.astype(MXU), vv[:, c * HEAD_DIM:(c + 1) * HEAD_DIM]))
        o_ref[...] = jnp.concatenate(outs, axis=-1).astype(o_ref.dtype)

    return _hosted_call(
        body, grid=(bsz, nb),
        in_specs=[pl.BlockSpec(memory_space=pltpu.SMEM),
                  pl.BlockSpec((WINDOW, MIX_WIDTH), lambda b, n: (b * nb + n, 0)),
                  pl.BlockSpec((WINDOW, 2 * kvw), lambda b, n: (b * nb + jnp.maximum(n - 1, 0), 0)),
                  pl.BlockSpec((WINDOW, 2 * kvw), lambda b, n: (b * nb + n, 0))],
        out_specs=pl.BlockSpec((WINDOW, MIX_WIDTH), lambda b, n: (b * nb + n, 0)),
        out_shape=jax.ShapeDtypeStruct((n_tok, D_MODEL), MXU),
        args=(sinks, proj, kv, kv), name=name, q=q, budget_us=HOST_US["swa_fwd"])


def _swa_bwd(proj, kv, sinks, dycat, bsz, *, name, q=None):
    n_tok = proj.shape[0]
    nb = n_tok // bsz // WINDOW
    kvw = SWA_KV_HEADS * HEAD_DIM

    def body(sink_ref, q_ref, kvp_ref, kvc_ref, do_ref, dq_ref, dkvc_ref, dkvp_ref, dsink_ref):
        n = pl.program_id(1)

        @pl.when((pl.program_id(0) == 0) & (n == 0))
        def _():
            dsink_ref[...] = jnp.zeros_like(dsink_ref)

        dist, mask = _swa_mask(n)
        kk = jnp.concatenate([kvp_ref[:, :kvw], kvc_ref[:, :kvw]], axis=0).astype(MXU)
        vv = jnp.concatenate([kvp_ref[:, kvw:], kvc_ref[:, kvw:]], axis=0).astype(MXU)
        lane = lax.broadcasted_iota(jnp.int32, (SUBLANE, LANE), 1)
        dqs = []
        dks = [None] * SWA_KV_HEADS
        dvs = [None] * SWA_KV_HEADS
        dsink = jnp.zeros((SUBLANE, LANE), F32)
        for h in range(SWA_HEADS):
            c = h // SWA_GROUP
            k = kk[:, c * HEAD_DIM:(c + 1) * HEAD_DIM]
            v = vv[:, c * HEAD_DIM:(c + 1) * HEAD_DIM]
            q = q_ref[:, h * HEAD_DIM:(h + 1) * HEAD_DIM].astype(MXU)
            do = do_ref[:, h * HEAD_DIM:(h + 1) * HEAD_DIM].astype(MXU)
            p, psink = _swa_scores(q, k, h, dist, mask, sink_ref[h])
            dv = _dot_tn(p.astype(MXU), do)
            dp = _dot_nt(do, v)
            rs = jnp.sum(dp * p, axis=-1, keepdims=True)
            ds = (p * (dp - rs) * (HEAD_DIM ** -0.5)).astype(MXU)
            dsink = dsink + jnp.where(lane == h, jnp.sum(-psink * rs, axis=0, keepdims=True), 0.0)
            dqs.append(_dot(ds, k))
            dk = _dot_tn(ds, q)
            dks[c] = dk if dks[c] is None else dks[c] + dk
            dvs[c] = dv if dvs[c] is None else dvs[c] + dv
        dq_ref[...] = jnp.concatenate(dqs, axis=-1).astype(dq_ref.dtype)
        dkv = jnp.concatenate(dks + dvs, axis=-1)
        dkvp_ref[...] = dkv[:WINDOW]
        dkvc_ref[...] = dkv[WINDOW:]
        dsink_ref[...] += dsink

    qspec = pl.BlockSpec((WINDOW, MIX_WIDTH), lambda b, n: (b * nb + n, 0))
    kvspec = pl.BlockSpec((WINDOW, 2 * kvw), lambda b, n: (b * nb + n, 0))
    return _hosted_call(
        body, grid=(bsz, nb),
        in_specs=[pl.BlockSpec(memory_space=pltpu.SMEM), qspec,
                  pl.BlockSpec((WINDOW, 2 * kvw), lambda b, n: (b * nb + jnp.maximum(n - 1, 0), 0)),
                  kvspec, qspec],
        out_specs=[qspec, kvspec, kvspec, pl.BlockSpec((SUBLANE, LANE), lambda b, n: (0, 0))],
        out_shape=[jax.ShapeDtypeStruct((n_tok, D_MODEL), MXU),
                   jax.ShapeDtypeStruct((n_tok, 2 * kvw), F32),
                   jax.ShapeDtypeStruct((n_tok, 2 * kvw), F32),
                   jax.ShapeDtypeStruct((SUBLANE, LANE), F32)],
        args=(sinks, proj, kv, kv, dycat), name=name, q=q, budget_us=HOST_US["swa_bwd"])


def _swa_dkv_combine(curs, prevs, bsz, *, name):
    n_tok, w = curs[0].shape
    nb = n_tok // bsz // WINDOW
    k = len(curs)

    def body(*refs):
        o_ref = refs[-1]
        n = pl.program_id(1)
        acc = refs[0][...]
        for r in refs[1:k]:
            acc = acc + r[...]
        nxt = refs[k][...]
        for r in refs[k + 1:2 * k]:
            nxt = nxt + r[...]
        o_ref[...] = (acc + jnp.where(n < nb - 1, nxt, 0.0)).astype(o_ref.dtype)

    cur = pl.BlockSpec((WINDOW, w), lambda b, n: (b * nb + n, 0))
    prv = pl.BlockSpec((WINDOW, w), lambda b, n: (b * nb + jnp.minimum(n + 1, nb - 1), 0))
    return pl.pallas_call(
        body, grid=(bsz, nb), in_specs=[cur] * k + [prv] * k, out_specs=cur,
        out_shape=jax.ShapeDtypeStruct((n_tok, w), MXU),
        name=name, compiler_params=_cp((PAR, PAR)))(*curs, *prevs)


def _lru_gates(ux, halo, ext_ref, wc_ref, bc_ref, wr_ref, br_ref, wi_ref, bi_ref, lam_ref):
    tt = ux.shape[0]
    ext_ref[0:SUBLANE, :] = halo
    ext_ref[SUBLANE:, :] = ux
    xs = [ux] + [ext_ref[pl.ds(SUBLANE - k, tt), :] for k in range(1, LRU_CONV)]
    xc = bc_ref[...] + wc_ref[3:4, :] * xs[0] + wc_ref[2:3, :] * xs[1] + wc_ref[1:2, :] * xs[2] + wc_ref[0:1, :] * xs[3]
    pre_r, pre_i = [], []
    for blk in range(MIX_WIDTH // GATE_TILE):
        xb = xc[:, blk * GATE_TILE:(blk + 1) * GATE_TILE].astype(MXU)
        pre_r.append(_dot(xb, wr_ref[blk]))
        pre_i.append(_dot(xb, wi_ref[blk]))
    r = jax.nn.sigmoid(jnp.concatenate(pre_r, axis=-1) + br_ref[...])
    i = jax.nn.sigmoid(jnp.concatenate(pre_i, axis=-1) + bi_ref[...])
    nlam = -lam_ref[...]
    sp = jnp.maximum(nlam, 0.0) + jnp.log(1.0 + jnp.exp(-jnp.abs(nlam)))
    log_a = -LRU_C * r * sp
    a = jnp.exp(log_a)
    om = -jnp.tanh(log_a) * (a * a + 1.0)
    s = jnp.sqrt(om)
    return xs, xc, r, i, sp, a, s


def _lru_fwd(proj, wconv, bconv, wr, br, wi, bi, lam, bsz, *, name, q=None):
    n_tok = proj.shape[0]
    t = n_tok // bsz
    tt = _tile(t, 256, SUBLANE)
    nt = t // tt
    w = MIX_WIDTH
    ng = tt // SUBLANE

    def body(pg_ref, halo_ref, wc_ref, bc_ref, wr_ref, br_ref, wi_ref, bi_ref, lam_ref,
             y_ref, h_ref, ext_ref, a_ref, b_ref, carry_ref):
        ti = pl.program_id(1)

        @pl.when(ti == 0)
        def _():
            carry_ref[...] = jnp.zeros_like(carry_ref)

        gate = pg_ref[:, :w]
        ux = pg_ref[:, w:]
        halo = jnp.where(ti > 0, halo_ref[...], 0.0)
        _, xc, _, i, _, a, s = _lru_gates(ux, halo, ext_ref, wc_ref, bc_ref, wr_ref, br_ref, wi_ref, bi_ref, lam_ref)
        a_ref[...] = a
        b_ref[...] = s * (i * xc)
        row = lax.broadcasted_iota(jnp.int32, (SUBLANE, w), 0)

        def group(g, hprev):
            off = pl.multiple_of(g * SUBLANE, SUBLANE)
            ca = a_ref[pl.ds(off, SUBLANE), :]
            cb = b_ref[pl.ds(off, SUBLANE), :]
            for d in (1, 2, 4):
                a_sh = jnp.where(row >= d, pltpu.roll(ca, d, axis=0), 1.0)
                b_sh = jnp.where(row >= d, pltpu.roll(cb, d, axis=0), 0.0)
                cb = ca * b_sh + cb
                ca = ca * a_sh
            h = ca * hprev + cb
            b_ref[pl.ds(off, SUBLANE), :] = h
            return jnp.broadcast_to(h[SUBLANE - 1:SUBLANE, :], (SUBLANE, w))

        carry_ref[...] = lax.fori_loop(0, ng, group, carry_ref[...])
        h = b_ref[...]
        h_ref[...] = h
        y_ref[...] = (h * _gelu(gate)).astype(y_ref.dtype)

    vec = lambda r: pl.BlockSpec((r, w), lambda b, i: (0, 0))
    wspec = pl.BlockSpec((w // GATE_TILE, GATE_TILE, GATE_TILE), lambda b, i: (0, 0, 0))
    hb = tt // SUBLANE
    return _hosted_call(
        body, grid=(bsz, nt),
        in_specs=[pl.BlockSpec((tt, 2 * w), lambda b, i: (b * nt + i, 0)),
                  pl.BlockSpec((SUBLANE, w), lambda b, i: (jnp.maximum((b * nt + i) * hb - 1, 0), 1)),
                  vec(LRU_CONV), vec(1), wspec, vec(1), wspec, vec(1), vec(1)],
        out_specs=[pl.BlockSpec((tt, w), lambda b, i: (b * nt + i, 0)),
                   pl.BlockSpec((tt, w), lambda b, i: (b * nt + i, 0))],
        out_shape=[jax.ShapeDtypeStruct((n_tok, D_MODEL), MXU), jax.ShapeDtypeStruct((n_tok, w), F32)],
        scratch_shapes=[pltpu.VMEM((tt + SUBLANE, w), F32), pltpu.VMEM((tt, w), F32),
                        pltpu.VMEM((tt, w), F32), pltpu.VMEM((SUBLANE, w), F32)],
        args=(proj, proj, wconv, bconv, wr, br, wi, bi, lam), name=name, q=q, budget_us=HOST_US["lru_fwd"])


def _lru_bwd(proj, hs, dycat, wconv, bconv, wr, br, wi, bi, lam, bsz, *, name, q=None):
    n_tok = proj.shape[0]
    t = n_tok // bsz
    tt = _tile(t, 256, SUBLANE)
    nt = t // tt
    w = MIX_WIDTH
    ng = tt // SUBLANE
    nblk = w // GATE_TILE

    def body(pg_ref, halo_ref, h_ref, hhalo_ref, dy_ref, wc_ref, bc_ref, wr_ref, br_ref, wi_ref, bi_ref, lam_ref,
             dp_ref, dwc_ref, dbc_ref, dwr_ref, dbr_ref, dwi_ref, dbi_ref, dlam_ref,
             ext_ref, a_ref, c_ref, g_ref, gcarry_ref, xcarry_ref):
        bi_ = pl.program_id(0)
        ti = nt - 1 - pl.program_id(1)

        @pl.when((bi_ == 0) & (pl.program_id(1) == 0))
        def _():
            for r in (dwc_ref, dbc_ref, dwr_ref, dbr_ref, dwi_ref, dbi_ref, dlam_ref):
                r[...] = jnp.zeros_like(r)

        @pl.when(pl.program_id(1) == 0)
        def _():
            gcarry_ref[...] = jnp.zeros_like(gcarry_ref)
            xcarry_ref[...] = jnp.zeros_like(xcarry_ref)

        gate = pg_ref[:, :w]
        ux = pg_ref[:, w:]
        halo = jnp.where(ti > 0, halo_ref[...], 0.0)
        xs, xc, r, i, sp, a, s = _lru_gates(ux, halo, ext_ref, wc_ref, bc_ref, wr_ref, br_ref, wi_ref, bi_ref, lam_ref)
        h = h_ref[...]
        gl, dgl = _gelu_and_grad(gate)
        dy = dy_ref[...].astype(F32)
        dgate = dy * h * dgl
        row_t = lax.broadcasted_iota(jnp.int32, (tt, w), 0)
        g_ref[...] = dy * gl + jnp.where(row_t == tt - 1, gcarry_ref[0:1, :], 0.0)
        c_ref[...] = _shift_up(a, 1, row_t)
        row = lax.broadcasted_iota(jnp.int32, (SUBLANE, w), 0)

        a_ref[...] = a

        def group(k, gnext):
            off = pl.multiple_of((ng - 1 - k) * SUBLANE, SUBLANE)
            cc = c_ref[pl.ds(off, SUBLANE), :]
            cb = g_ref[pl.ds(off, SUBLANE), :]
            cb = cb + jnp.where(row == SUBLANE - 1, gnext, 0.0)
            cc = jnp.where(row == SUBLANE - 1, 0.0, cc)
            for d in (1, 2, 4):
                c_sh = jnp.where(row < SUBLANE - d, pltpu.roll(cc, SUBLANE - d, axis=0), 1.0)
                b_sh = jnp.where(row < SUBLANE - d, pltpu.roll(cb, SUBLANE - d, axis=0), 0.0)
                cb = cc * b_sh + cb
                cc = cc * c_sh
            g_ref[pl.ds(off, SUBLANE), :] = cb
            a0 = a_ref[pl.ds(off, SUBLANE), :]
            return jnp.broadcast_to(a0[0:1, :] * cb[0:1, :], (SUBLANE, w))

        gc = lax.fori_loop(0, ng, group, jnp.zeros((SUBLANE, w), F32))
        gcarry_ref[...] = gc
        gsc = g_ref[...]

        hhalo = jnp.where(ti > 0, hhalo_ref[SUBLANE - 1:SUBLANE, :], 0.0)
        hprev = jnp.where(row_t == 0, hhalo, pltpu.roll(h, 1, axis=0))
        gated = i * xc
        d_gated = gsc * s
        d_atot = gsc * hprev - (gsc * gated) * a / s
        d_loga = d_atot * a
        d_r = d_loga * (-LRU_C) * sp
        dlam_ref[...] += jnp.sum(d_loga * r, axis=0, keepdims=True) * (LRU_C * jax.nn.sigmoid(-lam_ref[...]))
        d_i = d_gated * xc
        d_xc = d_gated * i
        d_pr = d_r * r * (1.0 - r)
        d_pi = d_i * i * (1.0 - i)
        dbr_ref[...] += jnp.sum(d_pr, axis=0, keepdims=True)
        dbi_ref[...] += jnp.sum(d_pi, axis=0, keepdims=True)
        extra = []
        for blk in range(nblk):
            sl = slice(blk * GATE_TILE, (blk + 1) * GATE_TILE)
            xb = xc[:, sl].astype(MXU)
            dr_b = d_pr[:, sl].astype(MXU)
            di_b = d_pi[:, sl].astype(MXU)
            dwr_ref[blk] += _dot_tn(xb, dr_b)
            dwi_ref[blk] += _dot_tn(xb, di_b)
            extra.append(_dot_nt(dr_b, wr_ref[blk]) + _dot_nt(di_b, wi_ref[blk]))
        d_xc = d_xc + jnp.concatenate(extra, axis=-1)
        dbc_ref[...] += jnp.sum(d_xc, axis=0, keepdims=True)
        for k in range(LRU_CONV):
            dwc_ref[k:k + 1, :] += jnp.sum(d_xc * xs[LRU_CONV - 1 - k], axis=0, keepdims=True)
        ext_ref[0:tt, :] = d_xc
        ext_ref[tt:, :] = xcarry_ref[...]
        dux = wc_ref[3:4, :] * d_xc
        for k in range(LRU_CONV - 1):
            dux = dux + wc_ref[k:k + 1, :] * ext_ref[pl.ds(LRU_CONV - 1 - k, tt), :]
        xcarry_ref[...] = d_xc[0:SUBLANE, :]
        dp_ref[:, :w] = dgate.astype(dp_ref.dtype)
        dp_ref[:, w:] = dux.astype(dp_ref.dtype)

    vec = lambda r: pl.BlockSpec((r, w), lambda b, i: (0, 0))
    wspec = pl.BlockSpec((nblk, GATE_TILE, GATE_TILE), lambda b, i: (0, 0, 0))
    hb = tt // SUBLANE
    rblk = lambda b, i: b * nt + (nt - 1 - i)
    halo_idx = lambda b, i: jnp.maximum(rblk(b, i) * hb - 1, 0)
    wide = pl.BlockSpec((tt, 2 * w), lambda b, i: (rblk(b, i), 0))
    narrow = pl.BlockSpec((tt, w), lambda b, i: (rblk(b, i), 0))
    return _hosted_call(
        body, grid=(bsz, nt),
        in_specs=[wide, pl.BlockSpec((SUBLANE, w), lambda b, i: (halo_idx(b, i), 1)),
                  narrow, pl.BlockSpec((SUBLANE, w), lambda b, i: (halo_idx(b, i), 0)), narrow,
                  vec(LRU_CONV), vec(1), wspec, vec(1), wspec, vec(1), vec(1)],
        out_specs=[wide, vec(LRU_CONV), vec(1), wspec, vec(1), wspec, vec(1), vec(1)],
        out_shape=[jax.ShapeDtypeStruct((n_tok, 2 * w + MEM_WIDTH), MXU),
                   jax.ShapeDtypeStruct((LRU_CONV, w), F32), jax.ShapeDtypeStruct((1, w), F32),
                   jax.ShapeDtypeStruct((nblk, GATE_TILE, GATE_TILE), F32), jax.ShapeDtypeStruct((1, w), F32),
                   jax.ShapeDtypeStruct((nblk, GATE_TILE, GATE_TILE), F32), jax.ShapeDtypeStruct((1, w), F32),
                   jax.ShapeDtypeStruct((1, w), F32)],
        scratch_shapes=[pltpu.VMEM((tt + SUBLANE, w), F32), pltpu.VMEM((tt, w), F32), pltpu.VMEM((tt, w), F32),
                        pltpu.VMEM((tt, w), F32), pltpu.VMEM((SUBLANE, w), F32), pltpu.VMEM((SUBLANE, w), F32)],
        args=(proj, proj, hs, hs, dycat, wconv, bconv, wr, br, wi, bi, lam), name=name, q=q,
        budget_us=HOST_US["lru_bwd"])


def _gate_tiles(w):
    per = GATE_TILE // HEAD_DIM
    w4 = w.reshape(LRU_BLOCKS // per, per, HEAD_DIM, HEAD_DIM)
    eye = jnp.eye(per, dtype=w.dtype)
    return jnp.einsum("bnij,nm->bnimj", w4, eye).reshape(LRU_BLOCKS // per, GATE_TILE, GATE_TILE)


def _gate_blocks(t):
    per = GATE_TILE // HEAD_DIM
    t5 = t.reshape(LRU_BLOCKS // per, per, HEAD_DIM, per, HEAD_DIM)
    eye = jnp.eye(per, dtype=t.dtype)
    return jnp.einsum("bnimj,nm->bnij", t5, eye).reshape(LRU_BLOCKS, HEAD_DIM, HEAD_DIM)


def _row(v):
    return v.reshape(1, -1)


def _local_step(x, mem, target, p, wfull, push_grad, q):
    bsz, t, d = x.shape
    n = bsz * t
    x2d = x.reshape(n, d)
    tgt = target.reshape(n, d)
    mem2d = mem.reshape(bsz * MEM_LEN, d)
    wr_t = [_gate_tiles(p["w_rg_r"][j]).astype(MXU) for j in range(N_A)]
    wi_t = [_gate_tiles(p["w_rg_i"][j]).astype(MXU) for j in range(N_A)]

    mn = [_norm_fwd(mem2d, _row(p["g_mem"][l]), name=f"mem_norm{l}") for l in range(DEPTH)]
    mkv = [None] * DEPTH
    h = _norm_fwd(x2d, _row(p["g_mix_pre"][0]), name="in_norm")
    xin = x2d
    sv = []
    kv = hkv = None
    for l in range(DEPTH):
        s = {"xin": xin, "h": h}
        if q is not None:
            q.horizon = (l + 2) * GROUPS_PER_LAYER
        mkv[l] = _mm_nn(mn[l], wfull("w_mem_kv", l), name=f"mem_kv{l}", q=q)
        if l < N_A:
            proj = _mm_nn(h, wfull("w_in_a", l), name=f"in_proj{l}", q=q)
            ycat, hs = _lru_fwd(proj, p["w_conv_a"][l], _row(p["b_conv_a"][l]), wr_t[l], _row(p["b_rg_r"][l]),
                                wi_t[l], _row(p["b_rg_i"][l]), _row(p["lru_lambda"][l]), bsz, name=f"lru_fwd{l}", q=q)
            s["hs"] = hs
            qblk = 2 * MIX_WIDTH // MEM_WIDTH
        else:
            if l == N_A:
                kv = _mm_nn(hkv, wfull("w_kv", 0), name="kv_proj", q=q)
            proj = _mm_nn(h, wfull("w_in_b", l - N_A), name=f"in_proj{l}", q=q)
            ycat = _swa_fwd(proj, kv, p["sinks_b"][l - N_A], bsz, name=f"swa_fwd{l}", q=q)
            qblk = MIX_WIDTH // MEM_WIDTH
        ycat = _mem_attn_fwd(proj, qblk, mkv[l], ycat, bsz, name=f"mem_attn_fwd{l}", q=q)
        y = _mm_nn(ycat, wfull("w_mix_out", l), name=f"mix_out{l}", q=q, out_dtype=MXU)
        x1, (h2,) = _resid_norm_fwd(xin, y, _row(p["g_mix_post"][l]), [_row(p["g_ffn_pre"][l])], name=f"mix_resid{l}", q=q)
        up = _mm_nn_slots(h2, wfull("w_ffn_up", l), name=f"ffn_up{l}", q=q, out_dtype=MXU)
        act, ug, uv = _ffn_act_fwd(up, p["w_ffn_conv"][l], _row(p["b_ffn_conv"][l]), bsz, name=f"ffn_act{l}", q=q)
        f = _mm_nn(act, wfull("w_ffn_down", l), name=f"ffn_down{l}", q=q, out_dtype=MXU)
        s.update(proj=proj, qblk=qblk, ycat=ycat, y=y, x1=x1, h2=h2, up=up, ug=ug, uv=uv, act=act, f=f)
        sv.append(s)
        if l < DEPTH - 1:
            g_pres = [_row(p["g_mix_pre"][l + 1])] + ([_row(p["g_kv"])] if l + 1 == N_A else [])
            xin, hn = _resid_norm_fwd(x1, f, _row(p["g_ffn_post"][l]), g_pres, name=f"ffn_resid{l}", q=q)
            h = hn[0]
            if l + 1 == N_A:
                hkv = hn[1]
        else:
            g_tot, sq = _loss_fwd(x1, f, _row(p["g_ffn_post"][l]), tgt, name="loss")

    if q is not None:
        q.horizon = LAST_GROUP
    gs = {k: [None] * DEPTH for k in ("g_mix_pre", "g_mix_post", "g_ffn_pre", "g_ffn_post", "g_mem",
                                       "w_ffn_conv", "b_ffn_conv")}
    ga = {k: [None] * N_A for k in ("w_conv_a", "b_conv_a", "w_rg_r", "b_rg_r", "w_rg_i", "b_rg_i", "lru_lambda")}
    gsink = [None] * (DEPTH - N_A)
    dkv_cur, dkv_prev = [], []
    g_tot, df, _, gs["g_ffn_post"][DEPTH - 1] = _resid_norm_bwd(
        g_tot, [], None, [], sv[-1]["f"], _row(p["g_ffn_post"][DEPTH - 1]), name="loss_bwd")
    grad_x = None
    for l in reversed(range(DEPTH)):
        s = sv[l]
        dact = _mm_nt(df, wfull("w_ffn_down", l), name=f"d_act{l}", q=q, out_dtype=MXU)
        push_grad("w_ffn_down", l, _mm_tn(s["act"], df, name=f"dw_down{l}", q=q))
        dug, duv, gs["w_ffn_conv"][l], gs["b_ffn_conv"][l] = _ffn_act_bwd(
            s["up"], s["ug"], s["uv"], dact, p["w_ffn_conv"][l], bsz, name=f"ffn_act_bwd{l}", q=q)
        dh2 = _mm_ffn_dh(dug, duv, wfull("w_ffn_up", l), name=f"d_h2_{l}", q=q)
        up_slots = dict(slot_cols=2 * D_FF // N_CHIP, n_slots=N_CHIP)
        dwu = _mm_tn_slots(s["h2"], dug, name=f"dw_up_g{l}", q=q, **up_slots)
        push_grad("w_ffn_up", l, _mm_tn_slots(s["h2"], duv, name=f"dw_up_v{l}", q=q, out=dwu,
                                              first_slot=N_CHIP // 2, **up_slots))
        g1, dy, (gs["g_ffn_pre"][l],), gs["g_mix_post"][l] = _resid_norm_bwd(
            g_tot, [dh2], s["x1"], [_row(p["g_ffn_pre"][l])], s["y"], _row(p["g_mix_post"][l]), name=f"mix_resid_bwd{l}", q=q)
        dycat = _mm_nt(dy, wfull("w_mix_out", l), name=f"d_ycat{l}", q=q, out_dtype=MXU)
        push_grad("w_mix_out", l, _mm_tn(s["ycat"], dy, name=f"dw_mix_out{l}", q=q))
        if l < N_A:
            dproj, dwc, dbc, dwr, dbr, dwi, dbi, dlam = _lru_bwd(
                s["proj"], s["hs"], dycat, p["w_conv_a"][l], _row(p["b_conv_a"][l]), wr_t[l], _row(p["b_rg_r"][l]),
                wi_t[l], _row(p["b_rg_i"][l]), _row(p["lru_lambda"][l]), bsz, name=f"lru_bwd{l}", q=q)
            ga["w_conv_a"][l], ga["b_conv_a"][l], ga["lru_lambda"][l] = dwc, dbc[0], dlam[0]
            ga["w_rg_r"][l], ga["w_rg_i"][l] = _gate_blocks(dwr), _gate_blocks(dwi)
            ga["b_rg_r"][l] = dbr.reshape(LRU_BLOCKS, HEAD_DIM)
            ga["b_rg_i"][l] = dbi.reshape(LRU_BLOCKS, HEAD_DIM)
            w_in, j = "w_in_a", l
        else:
            dproj, dc, dp_, dsk = _swa_bwd(s["proj"], kv, p["sinks_b"][l - N_A], dycat, bsz, name=f"swa_bwd{l}", q=q)
            dkv_cur.append(dc)
            dkv_prev.append(dp_)
            gsink[l - N_A] = dsk[0, :SWA_HEADS]
            w_in, j = "w_in_b", l - N_A
        dproj, dmkv = _mem_attn_bwd(s["proj"], s["qblk"], mkv[l], dycat, dproj, bsz, name=f"mem_attn_bwd{l}", q=q)
        dh = _mm_nt(dproj, wfull(w_in, j), name=f"d_h{l}", q=q, out_dtype=MXU)
        push_grad(w_in, j, _mm_tn(s["h"], dproj, name=f"dw_in{l}", q=q))
        dmkv = dmkv.astype(MXU)
        dmn = _mm_nt(dmkv, wfull("w_mem_kv", l), name=f"d_mem_norm{l}", q=q)
        push_grad("w_mem_kv", l, _mm_tn(mn[l], dmkv, name=f"dw_mem_kv{l}", q=q))
        gs["g_mem"][l] = _norm_bwd_dg(dmn, mem2d, _row(p["g_mem"][l]), name=f"mem_norm_bwd{l}")
        dhs, g_pres = [dh], [_row(p["g_mix_pre"][l])]
        if l == N_A:
            dkv = _swa_dkv_combine(dkv_cur, dkv_prev, bsz, name="dkv_combine")
            dhs.append(_mm_nt(dkv, wfull("w_kv", 0), name="d_hkv", q=q, out_dtype=MXU))
            g_pres.append(_row(p["g_kv"]))
            push_grad("w_kv", 0, _mm_tn(hkv, dkv, name="dw_kv", q=q))
        if l > 0:
            g_tot, df, dgpre, gs["g_ffn_post"][l - 1] = _resid_norm_bwd(
                g1, dhs, s["xin"], g_pres, sv[l - 1]["f"], _row(p["g_ffn_post"][l - 1]), name=f"ffn_resid_bwd{l - 1}", q=q)
        else:
            grad_x, _, dgpre, _ = _resid_norm_bwd(g1, dhs, s["xin"], g_pres, None, None, name="in_norm_bwd", q=q)
        gs["g_mix_pre"][l] = dgpre[0]
        if l == N_A:
            g_kv = dgpre[1][0]

    grads = {}
    for k in ("g_mix_pre", "g_mix_post", "g_ffn_pre", "g_ffn_post", "g_mem", "b_ffn_conv"):
        grads[k] = jnp.concatenate(gs[k], axis=0)
    grads["w_ffn_conv"] = jnp.stack(gs["w_ffn_conv"])
    for k, v in ga.items():
        grads[k] = jnp.stack(v)
    grads["sinks_b"] = jnp.stack(gsink)
    grads["g_kv"] = g_kv
    return jnp.sum(sq), grad_x.reshape(bsz, t, d), grads


N_CHIP = 4
HALF_ALIGN = 16
D2D_STREAMS = 2
MIN_PART_BYTES = 128 * 1024


def _full_shape(kind, shard_shape):
    l, r, c = shard_shape
    return {"row": (l, N_CHIP * r, c), "col": (l, r, N_CHIP * c), "slot": (N_CHIP, l, r, c)}[kind]


def _slot_view(ref, kind, shard_shape, s, hf, sub=(0, 1)):
    _, r, c = shard_shape
    rh = r // 2
    if hf is None:
        size = r // sub[1]
        start = sub[0] * size
    else:
        size = rh // sub[1]
        start = hf * rh + sub[0] * size
    if kind == "row":
        start = s * r + start
    if not isinstance(start, int):
        start = pl.multiple_of(start, HALF_ALIGN)
    rows = pl.ds(start, size)
    if kind == "row":
        return ref.at[:, rows, :]
    if kind == "col":
        return ref.at[:, rows, pl.ds(s * c, c)]
    return ref.at[s, :, rows, :]


def _half_view(ref, shard_shape, hf, sub=(0, 1)):
    rh = shard_shape[1] // 2
    size = rh // sub[1]
    return ref.at[:, pl.ds(pl.multiple_of(hf * rh + sub[0] * size, HALF_ALIGN), size), :]


def _with_slot(kind, s, fn):
    if kind != "col" or isinstance(s, int):
        fn(s)
        return
    for k in range(N_CHIP):
        @pl.when(s == k)
        def _(k=k):
            fn(k)


def _mesh_pos():
    return lax.axis_index("x"), lax.axis_index("y"), lax.axis_index("c")


def _other_chips(x, y):
    return [(1 - x, y), (x, 1 - y), (1 - x, 1 - y)]


ICI_BYTES_PER_US = 6.0e4
ICI_GATHER_BYTES_PER_US = 5.5e4
D2D_BYTES_PER_US = 4.0e5


class _Chunk:
    def __init__(self, group, cost, ins, out_shapes, alias, n_sem, start, finish, done, buffer=None, bind=None):
        self.group, self.cost, self.ins, self.out_shapes, self.alias, self.n_sem = group, cost, ins, out_shapes, alias, n_sem
        self.start, self.finish, self.done = start, finish, done
        self.buffer = buffer
        self.bind = bind

    def prepare(self):
        if self.bind is not None:
            self.bind(self)


def _merged(chunks):
    groups, by_buffer = [], {}
    for ch in chunks:
        key = None if ch.buffer is None else (id(ch.buffer[0]), ch.buffer[1])
        if key is not None and key in by_buffer:
            by_buffer[key].append(ch)
        else:
            groups.append([ch])
            if key is not None:
                by_buffer[key] = groups[-1]
    out = []
    for parts in groups:
        if len(parts) == 1:
            out.append(parts[0])
            continue
        offs = [sum(p.n_sem for p in parts[:i]) for i in range(len(parts))]

        def run(phase, ins, outs, ss, rs, b, parts=parts, offs=offs):
            for p, o in zip(parts, offs):
                getattr(p, phase)(ins, outs, ss, rs, b + o)

        def done(outs, parts=parts):
            for p in parts:
                p.done(outs)

        first = parts[0]
        out.append(_Chunk(first.group, sum(p.cost for p in parts), first.ins, first.out_shapes, first.alias,
                          sum(p.n_sem for p in parts), functools.partial(run, "start"),
                          functools.partial(run, "finish"), done))
    return out


LAST_GROUP = 1 << 30
MIN_CARRIED_US = 8.0


class _CommQueue:
    def __init__(self):
        self.pending = []
        self.flushes = 0
        self.horizon = LAST_GROUP

    def push(self, chunk):
        self.pending.append(chunk)

    def take(self, budget_us):
        got, used = [], 0.0
        for ch in sorted(self.pending, key=lambda ch: (ch.group, -ch.cost)):
            if ch.group >= self.horizon and ch.group != LAST_GROUP:
                continue
            if used + ch.cost <= budget_us and not self._shares_buffer(ch, got):
                got.append(ch)
                used += ch.cost
        if used < MIN_CARRIED_US:
            return []
        return self._taken(got)

    @staticmethod
    def _shares_buffer(ch, others):
        return ch.buffer is not None and any(
            o.buffer is not None and o.buffer[0] is ch.buffer[0] and o.buffer[1] != ch.buffer[1] for o in others)

    def _taken(self, got):
        self.pending = [ch for ch in self.pending if ch not in got]
        for ch in got:
            ch.prepare()
        return _merged(got)

    def flush(self, group=LAST_GROUP):
        while True:
            chunks = []
            for ch in self.pending:
                if ch.group <= group and not self._shares_buffer(ch, chunks):
                    chunks.append(ch)
            if not chunks:
                return
            _run_chunks(self._taken(chunks), name=f"comm_flush{self.flushes}")
            self.flushes += 1


def _run_chunks(chunks, *, name):
    ins = [a for ch in chunks for a in ch.ins]
    outs = [s for ch in chunks for s in ch.out_shapes]
    alias, offs = {}, []
    i0 = o0 = s0 = 0
    for ch in chunks:
        offs.append((i0, o0, s0))
        for ci, co in ch.alias.items():
            alias[i0 + ci] = o0 + co
        i0 += len(ch.ins)
        o0 += len(ch.out_shapes)
        s0 += ch.n_sem

    def body(*refs):
        send_sems, recv_sems = refs[i0 + o0:]
        for phase in ("start", "finish"):
            for ch, (a, b, s) in zip(chunks, offs):
                getattr(ch, phase)(refs[a:a + len(ch.ins)], refs[i0 + b:i0 + b + len(ch.out_shapes)],
                                   send_sems, recv_sems, s)

    hbm = pl.BlockSpec(memory_space=pl.ANY)
    res = pl.pallas_call(
        body, in_specs=[hbm] * i0, out_specs=[hbm] * o0, out_shape=outs,
        scratch_shapes=[pltpu.SemaphoreType.DMA((s0,)), pltpu.SemaphoreType.DMA((s0,))],
        input_output_aliases=alias, name=name, compiler_params=pltpu.CompilerParams(has_side_effects=True))(*ins)
    for ch, (_, b, _) in zip(chunks, offs):
        ch.done(list(res[b:b + len(ch.out_shapes)]))


def _remote(src, dst, send_sems, recv_sems, k, dev):
    return pltpu.make_async_remote_copy(src_ref=src, dst_ref=dst, send_sem=send_sems.at[k], recv_sem=recv_sems.at[k],
                                        device_id=dev, device_id_type=MESH_T)


def _gather_chunks(q, group, kind, shard, l, ready):
    _, r, c = shard.shape
    shp = (1, r, c)
    rh = r // 2
    parts = max(p for p in (8, 4, 2, 1)
                if (rh // p) % HALF_ALIGN == 0 and (p == 1 or (rh // p) * c * shard.dtype.itemsize >= MIN_PART_BYTES))
    part_bytes = (rh // parts) * c * shard.dtype.itemsize
    full_type = jax.ShapeDtypeStruct(_full_shape(kind, shp), shard.dtype)
    state = {"full": None, "parts_done": 0}

    def bind_first(ch):
        ch.ins, ch.alias = ([shard], {}) if state["full"] is None else ([shard, state["full"]], {1: 0})

    def bind_full(ch):
        ch.ins = [state["full"]]

    def make_part(p):
        sub = (p, parts)

        def any_part(full):
            return _slot_view(full, kind, shp, 0, 0, sub)

        def own_rows(src):
            return src.at[:, pl.ds(p * (r // parts), r // parts), :]

        def start1(ins, outs, ss, rs, b):
            x, y, c_ = _mesh_pos()
            src, full = ins[0].at[pl.ds(l, 1)], outs[0]
            _with_slot(kind, 2 * x + y, lambda s: pltpu.make_async_copy(
                own_rows(src), _slot_view(full, kind, shp, s, None, sub), ss.at[b + N_CHIP - 1]).start())
            for j, (ox, oy) in enumerate(_other_chips(x, y)):
                _with_slot(kind, 2 * x + y, lambda s, j=j, ox=ox, oy=oy: _remote(
                    _half_view(src, shp, c_, sub), _slot_view(full, kind, shp, s, c_, sub), ss, rs, b + j,
                    (ox, oy, c_)).start())

        def finish1(ins, outs, ss, rs, b):
            x, y, c_ = _mesh_pos()
            h = any_part(outs[0])
            for j in range(N_CHIP - 1):
                _remote(h, h, ss, rs, b + j, (x, y, 1 - c_)).wait()
            pltpu.make_async_copy(own_rows(ins[0].at[pl.ds(l, 1)]), _slot_view(outs[0], kind, shp, 0, None, sub),
                                  ss.at[b + N_CHIP - 1]).wait()

        def start2(ins, outs, ss, rs, b):
            x, y, c_ = _mesh_pos()
            for j, (ox, oy) in enumerate(_other_chips(x, y)):
                def forward(s, j=j):
                    v = _slot_view(outs[0], kind, shp, s, c_, sub)
                    _remote(v, v, ss, rs, b + j, (x, y, 1 - c_)).start()
                _with_slot(kind, 2 * ox + oy, forward)

        def finish2(ins, outs, ss, rs, b):
            x, y, c_ = _mesh_pos()
            h = any_part(outs[0])
            for j in range(N_CHIP - 1):
                _remote(h, h, ss, rs, b + j, (x, y, 1 - c_)).wait()

        def done2(outs):
            state["full"] = outs[0]
            state["parts_done"] += 1
            if state["parts_done"] == parts:
                ready(outs[0])

        def done1(outs):
            state["full"] = outs[0]
            q.push(_Chunk(group, 3 * part_bytes / D2D_BYTES_PER_US, None, [full_type], {0: 0}, N_CHIP - 1,
                          start2, finish2, done2, buffer=(state, 2), bind=bind_full))

        return _Chunk(group, 3 * part_bytes / ICI_GATHER_BYTES_PER_US, None, [full_type], None,
                      N_CHIP, start1, finish1, done1, buffer=(state, 1), bind=bind_first)

    for p in range(parts):
        q.push(make_part(p))


def _reduce_scatter_chunks(q, kind, grad, shard_shape, pos, name, ready):
    _, r, c = shard_shape
    shp = (1, r, c)
    rh = r // 2

    rp = rh // D2D_STREAMS

    def landing(ref, s, i):
        return ref.at[s, :, pl.ds(i * rp, rp), :]

    def start1(ins, outs, ss, rs, b):
        x, y, c_ = _mesh_pos()
        for s in range(N_CHIP):
            for i in range(D2D_STREAMS):
                _remote(_slot_view(ins[0], kind, shp, s, 1 - c_, (i, D2D_STREAMS)), landing(outs[0], s, i),
                        ss, rs, b + s * D2D_STREAMS + i, (x, y, 1 - c_)).start()

    def finish1(ins, outs, ss, rs, b):
        x, y, c_ = _mesh_pos()
        for s in range(N_CHIP):
            for i in range(D2D_STREAMS):
                v = landing(outs[0], s, i)
                _remote(v, v, ss, rs, b + s * D2D_STREAMS + i, (x, y, 1 - c_)).wait()

    def start2(ins, outs, ss, rs, b):
        x, y, c_ = _mesh_pos()
        for j, (ox, oy) in enumerate(_other_chips(x, y)):
            _remote(ins[0].at[2 * ox + oy], outs[0].at[j], ss, rs, b + j, (ox, oy, c_)).start()

    def finish2(ins, outs, ss, rs, b):
        x, y, c_ = _mesh_pos()
        for j in range(N_CHIP - 1):
            _remote(outs[0].at[j], outs[0].at[j], ss, rs, b + j, (x, y, 1 - c_)).wait()

    def start3(ins, outs, ss, rs, b):
        x, y, c_ = _mesh_pos()
        for i in range(D2D_STREAMS):
            v = _half_view(outs[0], shp, c_, (i, D2D_STREAMS))
            _remote(v, v, ss, rs, b + i, (x, y, 1 - c_)).start()

    def finish3(ins, outs, ss, rs, b):
        x, y, c_ = _mesh_pos()
        for i in range(D2D_STREAMS):
            v = _half_view(outs[0], shp, c_, (i, D2D_STREAMS))
            _remote(v, v, ss, rs, b + i, (x, y, 1 - c_)).wait()

    def done2(pair, outs):
        half = _rs_chip_add(pair, outs[0], shp, pos, name=f"rs_chip_add_{name}")
        q.push(_Chunk(LAST_GROUP, rh * c * 4 / D2D_BYTES_PER_US, [half], [jax.ShapeDtypeStruct(half.shape, half.dtype)],
                      {0: 0}, D2D_STREAMS, start3, finish3, lambda o: ready(o[0])))

    def done1(outs):
        pair, wire = _rs_pair_add(grad, outs[0], kind, shp, pos, name=f"rs_pair_add_{name}")
        q.push(_Chunk(LAST_GROUP, 3 * rh * c * wire.dtype.itemsize / ICI_BYTES_PER_US, [wire],
                      [jax.ShapeDtypeStruct((N_CHIP - 1, 1, rh, c), wire.dtype)], {}, N_CHIP - 1,
                      start2, finish2, functools.partial(done2, pair)))

    q.push(_Chunk(LAST_GROUP, N_CHIP * rh * c * 4 / D2D_BYTES_PER_US, [grad],
                  [jax.ShapeDtypeStruct((N_CHIP, 1, rh, c), F32)], {}, N_CHIP * D2D_STREAMS, start1, finish1, done1))


def _allgather8(vec, *, name):
    r = vec.shape[0]
    n_dev = 8

    def body(v_ref, buf, send_sems, recv_sems):
        x, y, c = _mesh_pos()
        me = 4 * x + 2 * y + c
        copies = []
        for k in range(1, n_dev):
            kx, ky, kc = (k >> 2) & 1, (k >> 1) & 1, k & 1
            peer = ((1 - x) if kx else x, (1 - y) if ky else y, (1 - c) if kc else c)
            cp = _remote(v_ref, buf.at[me], send_sems, recv_sems, k - 1, peer)
            cp.start()
            copies.append(cp)
        buf[me] = v_ref[...]
        for cp in copies:
            cp.wait()

    vm = pl.BlockSpec(memory_space=pltpu.VMEM)
    return pl.pallas_call(
        body, in_specs=[vm], out_specs=vm, out_shape=jax.ShapeDtypeStruct((n_dev, r, LANE), F32),
        scratch_shapes=[pltpu.SemaphoreType.DMA((n_dev - 1,)), pltpu.SemaphoreType.DMA((n_dev - 1,))],
        name=name, compiler_params=pltpu.CompilerParams(has_side_effects=True, vmem_limit_bytes=VMEM_LIMIT_V7X))(vec)


def _allreduce8(vec, *, name):
    r = vec.shape[0]
    rh = r // 2

    def body(v_ref, o_ref, sib_ref, chips_ref, send_sems, recv_sems):
        x, y, c = _mesh_pos()
        sib = (x, y, 1 - c)
        me = 2 * x + y
        pair = _remote(v_ref, sib_ref, send_sems, recv_sems, 0, sib)
        pair.start()
        pair.wait()
        rows = pl.ds(pl.multiple_of(c * rh, SUBLANE), rh)
        chips_ref[me] = v_ref[rows, :] + sib_ref[rows, :]
        copies = []
        for j, (ox, oy) in enumerate(_other_chips(x, y)):
            cp = _remote(chips_ref.at[me], chips_ref.at[me], send_sems, recv_sems, 1 + j, (ox, oy, c))
            cp.start()
            copies.append(cp)
        for cp in copies:
            cp.wait()
        acc = chips_ref[0]
        for s in range(1, N_CHIP):
            acc = acc + chips_ref[s]
        o_ref[rows, :] = acc
        swap = _remote(o_ref.at[rows, :], o_ref.at[rows, :], send_sems, recv_sems, N_CHIP, sib)
        swap.start()
        swap.wait()

    vm = pl.BlockSpec(memory_space=pltpu.VMEM)
    return pl.pallas_call(
        body, in_specs=[vm], out_specs=vm, out_shape=jax.ShapeDtypeStruct((r, LANE), F32),
        scratch_shapes=[pltpu.VMEM((r, LANE), F32), pltpu.VMEM((N_CHIP, rh, LANE), F32),
                        pltpu.SemaphoreType.DMA((N_CHIP + 1,)), pltpu.SemaphoreType.DMA((N_CHIP + 1,))],
        name=name, compiler_params=pltpu.CompilerParams(has_side_effects=True, vmem_limit_bytes=VMEM_LIMIT_V7X))(vec)


def _rs_pair_add(g, recv, kind, shape, pos, *, name):
    l, r, c = shape
    rh = r // 2
    if kind == "row":
        gspec = pl.BlockSpec((None, rh, c), lambda s, i, pos: (i, 2 * s + pos[0], 0))
    elif kind == "col":
        gspec = pl.BlockSpec((None, rh, c), lambda s, i, pos: (i, pos[0], s))
    else:
        gspec = pl.BlockSpec((None, None, rh, c), lambda s, i, pos: (s, i, pos[0], 0))
    pspec = pl.BlockSpec((None, None, rh, c), lambda s, i, pos: (s, i, 0, 0))

    def body(pos_ref, g_ref, r_ref, p_ref, pw_ref):
        del pos_ref
        v = g_ref[...] + r_ref[...]
        p_ref[...] = v
        pw_ref[...] = v.astype(pw_ref.dtype)

    return pl.pallas_call(
        body,
        grid_spec=pltpu.PrefetchScalarGridSpec(
            num_scalar_prefetch=1, grid=(N_CHIP, l), in_specs=[gspec, pspec], out_specs=[pspec, pspec]),
        out_shape=[jax.ShapeDtypeStruct((N_CHIP, l, rh, c), F32), jax.ShapeDtypeStruct((N_CHIP, l, rh, c), MXU)],
        name=name, compiler_params=_cp((PAR, PAR)))(pos, g, recv)


def _rs_chip_add(p, recv, shape, pos, *, name):
    l, r, c = shape
    rh = r // 2

    def body(pos_ref, p_ref, r_ref, o_ref):
        del pos_ref
        acc = p_ref[...]
        for j in range(N_CHIP - 1):
            acc = acc + r_ref[j].astype(F32)
        o_ref[...] = acc

    return pl.pallas_call(
        body,
        grid_spec=pltpu.PrefetchScalarGridSpec(
            num_scalar_prefetch=1, grid=(l,),
            in_specs=[pl.BlockSpec((None, None, rh, c), lambda i, pos: (pos[1], i, 0, 0)),
                      pl.BlockSpec((N_CHIP - 1, None, rh, c), lambda i, pos: (0, i, 0, 0))],
            out_specs=pl.BlockSpec((None, rh, c), lambda i, pos: (i, pos[0], 0))),
        out_shape=jax.ShapeDtypeStruct((l, r, c), F32),
        name=name, compiler_params=_cp((PAR,)))(pos, p, recv)


ADAM_BLOCK_ELEMS = 384 * 1024


def _adam_math(w, g, m, v):
    c1 = 1.0 / (1.0 - ADAM_B1 ** ADAM_STEP)
    c2 = 1.0 / (1.0 - ADAM_B2 ** ADAM_STEP)
    nm = ADAM_B1 * m + (1.0 - ADAM_B1) * g
    nv = ADAM_B2 * v + (1.0 - ADAM_B2) * (g * g)
    return -ADAM_LR * ((nm * c1) / (jnp.sqrt(nv * c2) + ADAM_EPS) + ADAM_WD * w), nm, nv


def _adamw_layer(w, g, m, v, outs, l, *, name):
    _, r, c = w.shape
    tr = _tile(r, max(SUBLANE, ADAM_BLOCK_ELEMS // c // SUBLANE * SUBLANE), SUBLANE)

    def body(w_ref, g_ref, m_ref, v_ref, *rest):
        go_ref, d_ref, nm_ref, nv_ref = rest[4:]
        gg = g_ref[...]
        go_ref[...] = gg
        d_ref[...], nm_ref[...], nv_ref[...] = _adam_math(w_ref[...], gg, m_ref[...], v_ref[...])

    lay = pl.BlockSpec((None, tr, c), lambda j: (l, j, 0))
    hbm = pl.BlockSpec(memory_space=pl.ANY)
    return pl.pallas_call(
        body, grid=(r // tr,),
        in_specs=[lay, pl.BlockSpec((None, tr, c), lambda j: (0, j, 0)), lay, lay] + [hbm] * 4,
        out_specs=[lay] * 4, out_shape=[jax.ShapeDtypeStruct(w.shape, F32)] * 4,
        input_output_aliases={4 + i: i for i in range(4)},
        name=name, compiler_params=_cp((PAR,)))(w, g, m, v, *outs)


def _adamw(w, g, m, v, *, name):
    shape = w.shape
    if w.ndim == 2:
        w, g, m, v = (a[None] for a in (w, g, m, v))
    l, r, c = w.shape
    tr = _tile(r, max(SUBLANE, ADAM_BLOCK_ELEMS // c // SUBLANE * SUBLANE), SUBLANE)

    def body(w_ref, g_ref, m_ref, v_ref, d_ref, nm_ref, nv_ref):
        d_ref[...], nm_ref[...], nv_ref[...] = _adam_math(w_ref[...], g_ref[...], m_ref[...], v_ref[...])

    spec = pl.BlockSpec((None, tr, c), lambda i, j: (i, j, 0))
    outs = pl.pallas_call(
        body, grid=(l, r // tr), in_specs=[spec] * 4, out_specs=[spec] * 3,
        out_shape=[jax.ShapeDtypeStruct((l, r, c), F32)] * 3,
        name=name, compiler_params=_cp((PAR, PAR)))(w, g, m, v)
    return tuple(o.reshape(shape) for o in outs)


PACK_ROWS = 512 * LANE


def _pack(arrays):
    flat = jnp.concatenate([a.reshape(-1).astype(F32) for a in arrays])
    pad = (-flat.shape[0]) % PACK_ROWS
    return jnp.pad(flat, (0, pad)).reshape(-1, LANE)


def _unpack(packed, shapes):
    flat = packed.reshape(-1)
    out, off = [], 0
    for s in shapes:
        size = int(np.prod(s))
        out.append(flat[off:off + size].reshape(s))
        off += size
    return out


BIG = (("w_mem_kv", "row"), ("w_mix_out", "row"), ("w_ffn_up", "slot"), ("w_ffn_down", "row"),
       ("w_in_a", "slot"), ("w_in_b", "row"), ("w_kv", "row"))
COLUMN_SHARDED_AS_COLUMNS = ("w_in_a",)
SMALL_SHARDED = (("w_ffn_conv", 2), ("w_conv_a", 2), ("b_conv_a", 1), ("lru_lambda", 1))
SMALL_REPLICATED = ("g_mix_pre", "g_mix_post", "g_ffn_pre", "g_ffn_post", "g_mem", "b_ffn_conv",
                    "w_rg_r", "b_rg_r", "w_rg_i", "b_rg_i", "sinks_b", "g_kv")
WEIGHTS = ("g_mix_pre", "g_mix_post", "g_ffn_pre", "g_ffn_post", "g_mem", "w_mem_kv", "w_mix_out", "w_ffn_up",
           "w_ffn_conv", "b_ffn_conv", "w_ffn_down", "w_in_a", "w_conv_a", "b_conv_a", "w_rg_r", "b_rg_r", "w_rg_i",
           "b_rg_i", "lru_lambda", "w_in_b", "sinks_b", "g_kv", "w_kv")


def _slot_to_cols(a):
    s, l, r, c = a.shape
    return a.transpose(1, 2, 0, 3).reshape(l, r, s * c)


def _cols_to_slot(a):
    l, r, c4 = a.shape
    return a.reshape(l, r, N_CHIP, c4 // N_CHIP).transpose(2, 0, 1, 3)


GROUPS_PER_LAYER = 8


def _layer_weights(layer):
    names = [("w_mem_kv", layer), ("w_in_a", layer) if layer < N_A else ("w_in_b", layer - N_A)]
    if layer == N_A:
        names.append(("w_kv", 0))
    return names + [("w_mix_out", layer), ("w_ffn_up", layer), ("w_ffn_down", layer)]


def _train_step(x, mem, target, w, m, v):
    xi, yi, ci = _mesh_pos()
    chip = 2 * xi + yi
    pos = jnp.stack([ci, chip]).astype(jnp.int32)

    q = _CommQueue()
    kinds = dict(BIG)
    as3 = lambda a: a if a.ndim == 3 else a[None]
    w3, m3, v3 = ({k: as3(d[k]) for k, _ in BIG} for d in (w, m, v))
    shards = {k: w3[k].astype(MXU) for k, _ in BIG}

    gathered = {}

    def on_gathered(k, l, full):
        gathered[k, l] = _slot_to_cols(full) if k in COLUMN_SHARDED_AS_COLUMNS else full

    group_of = {}

    for layer in range(DEPTH):
        for i, (k, l) in enumerate(_layer_weights(layer)):
            group_of[k, l] = layer * GROUPS_PER_LAYER + i
            _gather_chunks(q, group_of[k, l], kinds[k], shards[k], l, functools.partial(on_gathered, k, l))

    def wfull(k, l):
        if (k, l) not in gathered:
            q.flush(group_of[k, l])
        return gathered[k, l]

    q.flush(1)

    big_out = {k: [lax.empty(w3[k].shape, F32) for _ in range(4)] for k, _ in BIG}

    def on_reduced(k, l, g):
        big_out[k] = _adamw_layer(w3[k], g, m3[k], v3[k], big_out[k], l, name=f"adamw_{k}{l}")

    def push_grad(k, l, g):
        if k in COLUMN_SHARDED_AS_COLUMNS:
            g = _cols_to_slot(g)
        _reduce_scatter_chunks(q, kinds[k], g, (1,) + w3[k].shape[1:], pos, f"{k}{l}", functools.partial(on_reduced, k, l))

    small_shapes = [w[k].shape for k, _ in SMALL_SHARDED]
    stacked = _allgather8(_pack([w[k] for k, _ in SMALL_SHARDED]), name="gather_small")
    per_chip = [_unpack(stacked[2 * s], small_shapes) for s in range(N_CHIP)]
    p = {k: w[k] for k in SMALL_REPLICATED}
    for i, (k, axis) in enumerate(SMALL_SHARDED):
        p[k] = jnp.concatenate([per_chip[s][i] for s in range(N_CHIP)], axis=axis)

    sq, grad_x, g = _local_step(x, mem, target, p, wfull, push_grad, q)
    loss = lax.psum(0.5 * sq / D_MODEL, ("x", "y", "c"))
    q.flush()

    small_names = [k for k, _ in SMALL_SHARDED] + list(SMALL_REPLICATED)
    summed = _allreduce8(_pack([g[k] for k in small_names]), name="allreduce_small")
    gsum = dict(zip(small_names, _unpack(summed, [p[k].shape for k in small_names])))
    for k, axis in SMALL_SHARDED:
        gsum[k] = lax.dynamic_slice_in_dim(gsum[k], chip * w[k].shape[axis], w[k].shape[axis], axis)

    delta, new_m, new_v = {}, {}, {}
    for k, _ in BIG:
        gsum[k], delta[k], new_m[k], new_v[k] = (o.reshape(w[k].shape) for o in big_out[k])
    packed = [_pack([d[k] for k in small_names]) for d in (w, gsum, m, v)]
    outs = _adamw(*packed, name="adamw_small")
    for d, o in zip((delta, new_m, new_v), outs):
        d.update(zip(small_names, _unpack(o, [w[k].shape for k in small_names])))
    return (loss, grad_x, *[gsum[k] for k in WEIGHTS], *[delta[k] for k in WEIGHTS],
            *[new_m[k] for k in WEIGHTS], *[new_v[k] for k in WEIGHTS])


def kernel(x, mem, g_mix_pre, g_mix_post, g_ffn_pre, g_ffn_post, g_mem, w_mem_kv, w_mix_out, w_ffn_up, w_ffn_conv, b_ffn_conv, w_ffn_down, w_in_a, w_conv_a, b_conv_a, w_rg_r, b_rg_r, w_rg_i, b_rg_i, lru_lambda, w_in_b, sinks_b, g_kv, w_kv, loss_target, m_g_mix_pre, m_g_mix_post, m_g_ffn_pre, m_g_ffn_post, m_g_mem, m_w_mem_kv, m_w_mix_out, m_w_ffn_up, m_w_ffn_conv, m_b_ffn_conv, m_w_ffn_down, m_w_in_a, m_w_conv_a, m_b_conv_a, m_w_rg_r, m_b_rg_r, m_w_rg_i, m_b_rg_i, m_lru_lambda, m_w_in_b, m_sinks_b, m_g_kv, m_w_kv, v_g_mix_pre, v_g_mix_post, v_g_ffn_pre, v_g_ffn_post, v_g_mem, v_w_mem_kv, v_w_mix_out, v_w_ffn_up, v_w_ffn_conv, v_b_ffn_conv, v_w_ffn_down, v_w_in_a, v_w_conv_a, v_b_conv_a, v_w_rg_r, v_b_rg_r, v_w_rg_i, v_b_rg_i, v_lru_lambda, v_w_in_b, v_sinks_b, v_g_kv, v_w_kv):
    args = (g_mix_pre, g_mix_post, g_ffn_pre, g_ffn_post, g_mem, w_mem_kv, w_mix_out, w_ffn_up, w_ffn_conv, b_ffn_conv, w_ffn_down, w_in_a, w_conv_a, b_conv_a, w_rg_r, b_rg_r, w_rg_i, b_rg_i, lru_lambda, w_in_b, sinks_b, g_kv, w_kv)
    ms = (m_g_mix_pre, m_g_mix_post, m_g_ffn_pre, m_g_ffn_post, m_g_mem, m_w_mem_kv, m_w_mix_out, m_w_ffn_up, m_w_ffn_conv, m_b_ffn_conv, m_w_ffn_down, m_w_in_a, m_w_conv_a, m_b_conv_a, m_w_rg_r, m_b_rg_r, m_w_rg_i, m_b_rg_i, m_lru_lambda, m_w_in_b, m_sinks_b, m_g_kv, m_w_kv)
    vs = (v_g_mix_pre, v_g_mix_post, v_g_ffn_pre, v_g_ffn_post, v_g_mem, v_w_mem_kv, v_w_mix_out, v_w_ffn_up, v_w_ffn_conv, v_b_ffn_conv, v_w_ffn_down, v_w_in_a, v_w_conv_a, v_b_conv_a, v_w_rg_r, v_b_rg_r, v_w_rg_i, v_b_rg_i, v_lru_lambda, v_w_in_b, v_sinks_b, v_g_kv, v_w_kv)
    return _train_step(x, mem, loss_target, dict(zip(WEIGHTS, args)), dict(zip(WEIGHTS, ms)), dict(zip(WEIGHTS, vs)))
```

```python
import functools
import math

import numpy as np
import jax
import jax.numpy as jnp
from jax import lax
from jax.experimental import pallas as pl
from jax.experimental.pallas import tpu as pltpu

F32 = jnp.float32
MXU = jnp.bfloat16

D_MODEL = 1024
HEAD_DIM = 64
MEM_LEN = 256
MEM_HEADS = 4
MEM_WIDTH = MEM_HEADS * HEAD_DIM
MIX_WIDTH = D_MODEL - MEM_WIDTH
LRU_BLOCKS = MIX_WIDTH // HEAD_DIM
LRU_CONV = 4
LRU_C = 8.0
SWA_HEADS = MIX_WIDTH // HEAD_DIM
SWA_KV_HEADS = 4
SWA_GROUP = SWA_HEADS // SWA_KV_HEADS
WINDOW = 128
D_FF = 2816
FFN_CONV = 3
EPS = 1e-6
DEPTH = 4
N_A = 2

ADAM_LR = 0.001
ADAM_B1 = 0.9
ADAM_B2 = 0.999
ADAM_EPS = 1e-08
ADAM_WD = 0.01
ADAM_STEP = 10

VMEM_LIMIT_V7X = 56 * 1024 * 1024
LANE = 128
SUBLANE = 8
GATE_TILE = 256
MESH_T = pl.DeviceIdType.MESH


def _alibi_slopes(n):
    def pow2_slopes(m):
        start = 2.0 ** (-8.0 / m)
        return [start ** (i + 1) for i in range(m)]
    c = 2 ** int(math.floor(math.log2(n)))
    s = pow2_slopes(c)
    if c != n:
        s = s + pow2_slopes(2 * c)[0::2][: n - c]
    return [float(np.float32(v)) for v in s]


SLOPES = _alibi_slopes(SWA_HEADS)


def _tile(n, cap, mult=LANE):
    best = None
    for t in range(mult, min(n, cap) + 1, mult):
        if n % t == 0:
            best = t
    return best if best is not None else n


def _cp(sem):
    return pltpu.CompilerParams(dimension_semantics=sem, vmem_limit_bytes=VMEM_LIMIT_V7X)


MM_VMEM_BUDGET = 40 * 1024 * 1024
HBM_BYTES_PER_US_V7X = 3.0e6
GRID_STEP_US = 0.35


def _divisors(n, mult):
    return [t for t in range(mult, n + 1, mult) if n % t == 0] or [n]


def _mm_tiles(m, k, n, out_bytes):
    best = None
    for tm in _divisors(m, 256):
        for tn in _divisors(n, LANE):
            vmem = 2 * (tm * k * 2 + k * tn * 2 + tm * tn * out_bytes)
            if vmem > MM_VMEM_BUDGET:
                continue
            steps = (m // tm) * (n // tn)
            b_reads = 1 if tn == n else m // tm
            traffic = m * k * 2 + k * n * 2 * b_reads + m * n * out_bytes
            first = tm * k * 2 + k * tn * 2
            cost = (traffic + first) / HBM_BYTES_PER_US_V7X + steps * GRID_STEP_US
            if best is None or cost < best[0]:
                best = (cost, tm, tn)
    return best[1], best[2]


def _mm_tn_tiles(k, m, n, whole_n=False):
    best = None
    for tm in _divisors(m, LANE):
        for tn in ([n] if whole_n else _divisors(n, LANE)):
            for tk in _divisors(k, 512):
                vmem = 2 * (tk * tm * 2 + tk * tn * 2 + tm * tn * 4)
                if vmem > MM_VMEM_BUDGET:
                    continue
                steps = (m // tm) * (n // tn) * (k // tk)
                traffic = k * m * 2 * (n // tn) + k * n * 2 * (m // tm) + m * n * 4
                cost = traffic / HBM_BYTES_PER_US_V7X + steps * GRID_STEP_US
                if best is None or cost < best[0]:
                    best = (cost, tk, tm, tn)
    return best[1], best[2], best[3]


ARB = "arbitrary"
PAR = "parallel"


def _rms_fwd(x, g):
    r = lax.rsqrt(jnp.mean(x * x, axis=-1, keepdims=True) + EPS)
    return x * r * g


def _rms_bwd(dy, x, g):
    r = lax.rsqrt(jnp.mean(x * x, axis=-1, keepdims=True) + EPS)
    xh = x * r
    gdy = dy * g
    dx = r * (gdy - xh * jnp.mean(gdy * xh, axis=-1, keepdims=True))
    dg = jnp.sum(dy * xh, axis=0, keepdims=True)
    return dx, dg


_GELU_K = math.sqrt(2.0 / math.pi)
_GELU_C = 0.044715


def _gelu(x):
    t = jnp.tanh(_GELU_K * (x + _GELU_C * x * x * x))
    return 0.5 * x * (1.0 + t)


def _gelu_and_grad(x):
    x2 = x * x
    u = 0.5 * jnp.tanh(x * (_GELU_K + (_GELU_K * _GELU_C) * x2)) + 0.5
    dz2 = (6.0 * _GELU_K * _GELU_C) * x2 + 2.0 * _GELU_K
    return x * u, u * ((x * (1.0 - u)) * dz2 + 1.0)


def _shift_down(x, k, row):
    return jnp.where(row >= k, pltpu.roll(x, k, axis=0), 0.0)


def _shift_up(x, k, row):
    n = x.shape[0]
    return jnp.where(row < n - k, pltpu.roll(x, n - k, axis=0), 0.0)


def _shift_down_edge(x, k):
    r = pltpu.roll(x, k, axis=0)
    row = lax.broadcasted_iota(jnp.int32, (SUBLANE, x.shape[1]), 0)
    return jnp.concatenate([jnp.where(row >= k, r[:SUBLANE], 0.0), r[SUBLANE:]], axis=0)


def _shift_up_edge(x, k):
    n = x.shape[0]
    r = pltpu.roll(x, n - k, axis=0)
    row = lax.broadcasted_iota(jnp.int32, (SUBLANE, x.shape[1]), 0)
    return jnp.concatenate([r[:n - SUBLANE], jnp.where(row < SUBLANE - k, r[n - SUBLANE:], 0.0)], axis=0)


def _dot(a, b):
    return jnp.dot(a, b, preferred_element_type=F32)


def _dot_nt(a, b):
    return lax.dot_general(a, b, (((1,), (1,)), ((), ())), preferred_element_type=F32)


def _dot_tn(a, b):
    return lax.dot_general(a, b, (((0,), (0,)), ((), ())), preferred_element_type=F32)


MXU_FLOPS_PER_US = 7.0e8
HOST_US = {"lru_fwd": 44.0, "lru_bwd": 94.0, "swa_fwd": 60.0, "swa_bwd": 160.0, "mem_attn_fwd": 21.0,
           "mem_attn_bwd": 33.0, "ffn_act": 70.0, "ffn_act_bwd": 100.0, "resid": 22.0, "resid_bwd": 33.0}


def _hosted_call(body, *, grid, in_specs, out_specs, out_shape, args, name, aliases=None, scratch_shapes=(),
                 q=None, flops=0.0, budget_us=0.0):
    chunks = q.take(flops / MXU_FLOPS_PER_US + budget_us) if q is not None else []
    if not chunks:
        return pl.pallas_call(
            body, grid=grid, in_specs=in_specs, out_specs=out_specs, out_shape=out_shape,
            scratch_shapes=list(scratch_shapes), input_output_aliases=aliases or {}, name=name,
            compiler_params=_cp((ARB,) * len(grid)))(*args)
    single = not isinstance(out_shape, (list, tuple))
    o_shapes = [out_shape] if single else list(out_shape)
    o_specs = [out_specs] if single else list(out_specs)
    n_in, n_out, n_scr = len(args), len(o_shapes), len(scratch_shapes)
    c_ins = [a for ch in chunks for a in ch.ins]
    c_outs = [s for ch in chunks for s in ch.out_shapes]
    alias = dict(aliases or {})
    in_off, out_off, sem_off = [], [], []
    i0 = o0 = s0 = 0
    for ch in chunks:
        in_off.append(i0)
        out_off.append(o0)
        sem_off.append(s0)
        for ci, co in ch.alias.items():
            alias[n_in + i0 + ci] = n_out + o0 + co
        i0 += len(ch.ins)
        o0 += len(ch.out_shapes)
        s0 += ch.n_sem

    def wrapped(*refs):
        ins = refs[:n_in]
        cin = refs[n_in:n_in + i0]
        outs = refs[n_in + i0:n_in + i0 + n_out]
        cout = refs[n_in + i0 + n_out:n_in + i0 + n_out + o0]
        scr = refs[n_in + i0 + n_out + o0:n_in + i0 + n_out + o0 + n_scr]
        send_sems, recv_sems = refs[n_in + i0 + n_out + o0 + n_scr:]
        first = functools.reduce(lambda u, v: u & v, [pl.program_id(d) == 0 for d in range(len(grid))])
        last = functools.reduce(lambda u, v: u & v, [pl.program_id(d) == grid[d] - 1 for d in range(len(grid))])

        def each(phase):
            for ch, a, b, s in zip(chunks, in_off, out_off, sem_off):
                getattr(ch, phase)(cin[a:a + len(ch.ins)], cout[b:b + len(ch.out_shapes)], send_sems, recv_sems, s)

        pl.when(first)(lambda: each("start"))
        body(*ins, *outs, *scr)
        pl.when(last)(lambda: each("finish"))

    hbm = pl.BlockSpec(memory_space=pl.ANY)
    res = pl.pallas_call(
        wrapped, grid=grid, in_specs=list(in_specs) + [hbm] * i0, out_specs=o_specs + [hbm] * o0,
        out_shape=o_shapes + c_outs,
        scratch_shapes=list(scratch_shapes) + [pltpu.SemaphoreType.DMA((s0,)), pltpu.SemaphoreType.DMA((s0,))],
        input_output_aliases=alias, name=name,
        compiler_params=pltpu.CompilerParams(dimension_semantics=(ARB,) * len(grid), vmem_limit_bytes=VMEM_LIMIT_V7X,
                                             has_side_effects=True))(*args, *c_ins)
    for ch, b in zip(chunks, out_off):
        ch.done(list(res[n_out + b:n_out + b + len(ch.out_shapes)]))
    return res[0] if single else list(res[:n_out])


def _mm_nn(a, b, *, name, q=None, out_dtype=F32):
    m, k = a.shape
    n = b.shape[-1]
    tm, tn = _mm_tiles(m, k, n, jnp.dtype(out_dtype).itemsize)

    def body(a_ref, b_ref, o_ref):
        o_ref[...] = _dot(a_ref[...], b_ref[...]).astype(o_ref.dtype)

    return _hosted_call(
        body, grid=(m // tm, n // tn),
        in_specs=[pl.BlockSpec((tm, k), lambda i, j: (i, 0)),
                  pl.BlockSpec((None, k, tn), lambda i, j: (0, 0, j))],
        out_specs=pl.BlockSpec((tm, tn), lambda i, j: (i, j)),
        out_shape=jax.ShapeDtypeStruct((m, n), out_dtype),
        args=(a, b), name=name, q=q, flops=2.0 * m * k * n)


def _mm_nt(a, b, *, name, q=None, out_dtype=F32):
    m, k = a.shape
    n = b.shape[-2]
    tm, tn = _mm_tiles(m, k, n, jnp.dtype(out_dtype).itemsize)

    def body(a_ref, b_ref, o_ref):
        o_ref[...] = _dot_nt(a_ref[...], b_ref[...]).astype(o_ref.dtype)

    return _hosted_call(
        body, grid=(m // tm, n // tn),
        in_specs=[pl.BlockSpec((tm, k), lambda i, j: (i, 0)),
                  pl.BlockSpec((None, tn, k), lambda i, j: (0, j, 0))],
        out_specs=pl.BlockSpec((tm, tn), lambda i, j: (i, j)),
        out_shape=jax.ShapeDtypeStruct((m, n), out_dtype),
        args=(a, b), name=name, q=q, flops=2.0 * m * k * n)


def _mm_nn_slots(a, b4, *, name, q=None, out_dtype=F32):
    m, k = a.shape
    s_, _, _, c = b4.shape
    ob = jnp.dtype(out_dtype).itemsize
    tm = max(t for t in _divisors(m, 256) if 2 * (t * k * 2 + k * c * 2 + t * c * ob) <= MM_VMEM_BUDGET)

    def body(a_ref, b_ref, o_ref):
        o_ref[...] = _dot(a_ref[...], b_ref[...]).astype(o_ref.dtype)

    return _hosted_call(
        body, grid=(m // tm, s_),
        in_specs=[pl.BlockSpec((tm, k), lambda i, j: (i, 0)),
                  pl.BlockSpec((None, None, k, c), lambda i, j: (j, 0, 0, 0))],
        out_specs=pl.BlockSpec((tm, c), lambda i, j: (i, j)),
        out_shape=jax.ShapeDtypeStruct((m, s_ * c), out_dtype),
        args=(a, b4), name=name, q=q, flops=2.0 * m * k * s_ * c)


def _mm_tn_slots(a, b, *, name, slot_cols, n_slots, first_slot=0, q=None, out=None):
    k, m = a.shape
    c = slot_cols
    tk, tm, _ = _mm_tn_tiles(k, m, c, whole_n=True)

    def body(a_ref, b_ref, *rest):
        o_ref = rest[-1]
        part = _dot_tn(a_ref[...], b_ref[...])

        @pl.when(pl.program_id(2) == 0)
        def _():
            o_ref[...] = part

        @pl.when(pl.program_id(2) > 0)
        def _():
            o_ref[...] += part

    in_specs = [pl.BlockSpec((tk, tm), lambda i, j, s: (s, i)), pl.BlockSpec((tk, c), lambda i, j, s: (s, j))]
    args = (a, b)
    if out is not None:
        in_specs.append(pl.BlockSpec(memory_space=pl.ANY))
        args = (a, b, out)
    return _hosted_call(
        body, grid=(m // tm, b.shape[-1] // c, k // tk), in_specs=in_specs,
        out_specs=pl.BlockSpec((None, None, tm, c), lambda i, j, s: (first_slot + j, 0, i, 0)),
        out_shape=jax.ShapeDtypeStruct((n_slots, 1, m, c), F32),
        aliases={2: 0} if out is not None else None,
        args=args, name=name, q=q, flops=2.0 * m * k * b.shape[-1])


def _mm_tn(a, b, *, name, q=None, out=None, n_total=None, col_block_offset=0):
    k, m = a.shape
    n = b.shape[-1]
    tk, tm, tn = _mm_tn_tiles(k, m, n)
    off = col_block_offset * (n // tn)

    def body(a_ref, b_ref, *rest):
        o_ref = rest[-1]
        part = _dot_tn(a_ref[...], b_ref[...])

        @pl.when(pl.program_id(2) == 0)
        def _():
            o_ref[...] = part

        @pl.when(pl.program_id(2) > 0)
        def _():
            o_ref[...] += part

    in_specs = [pl.BlockSpec((tk, tm), lambda i, j, s: (s, i)), pl.BlockSpec((tk, tn), lambda i, j, s: (s, j))]
    args = (a, b)
    if out is not None:
        in_specs.append(pl.BlockSpec(memory_space=pl.ANY))
        args = (a, b, out)
    return _hosted_call(
        body, grid=(m // tm, n // tn, k // tk), in_specs=in_specs,
        out_specs=pl.BlockSpec((None, tm, tn), lambda i, j, s: (0, i, j + off)),
        out_shape=jax.ShapeDtypeStruct((1, m, n_total or n), F32),
        aliases={2: 0} if out is not None else None,
        args=args, name=name, q=q, flops=2.0 * m * k * n)


def _mm_ffn_dh(dg, dv, w4, *, name, q=None):
    m, f = dg.shape
    n_slots, _, d, c = w4.shape
    tm, tn = _mm_tiles(m, 2 * f, d, 4)

    def body(dg_ref, dv_ref, *rest):
        w_refs, o_ref = rest[:n_slots], rest[n_slots]
        acc = None
        for s, w_ref in enumerate(w_refs):
            x_ref = dg_ref if s < n_slots // 2 else dv_ref
            off = (s % (n_slots // 2)) * c
            part = _dot_nt(x_ref[:, off:off + c], w_ref[...])
            acc = part if acc is None else acc + part
        o_ref[...] = acc.astype(o_ref.dtype)

    wspec = lambda s: pl.BlockSpec((None, None, tn, c), lambda i, j: (s, 0, j, 0))
    return _hosted_call(
        body, grid=(m // tm, d // tn),
        in_specs=[pl.BlockSpec((tm, f), lambda i, j: (i, 0)),
                  pl.BlockSpec((tm, f), lambda i, j: (i, 0))] + [wspec(s) for s in range(n_slots)],
        out_specs=pl.BlockSpec((tm, tn), lambda i, j: (i, j)),
        out_shape=jax.ShapeDtypeStruct((m, d), MXU),
        args=(dg, dv) + (w4,) * n_slots, name=name, q=q, flops=4.0 * m * f * d)


def _norm_fwd(x, g, *, name):
    n, d = x.shape
    tm = _tile(n, 256, SUBLANE)

    def body(x_ref, g_ref, o_ref):
        o_ref[...] = _rms_fwd(x_ref[...], g_ref[...]).astype(o_ref.dtype)

    return pl.pallas_call(
        body, grid=(n // tm,),
        in_specs=[pl.BlockSpec((tm, d), lambda i: (i, 0)), pl.BlockSpec((1, d), lambda i: (0, 0))],
        out_specs=pl.BlockSpec((tm, d), lambda i: (i, 0)),
        out_shape=jax.ShapeDtypeStruct((n, d), MXU),
        name=name, compiler_params=_cp((PAR,)))(x, g)


def _norm_bwd_dg(dy, x, g, *, name):
    n, d = x.shape
    tm = _tile(n, 256, SUBLANE)

    def body(dy_ref, x_ref, g_ref, dg_ref):
        @pl.when(pl.program_id(0) == 0)
        def _():
            dg_ref[...] = jnp.zeros_like(dg_ref)
        _, dg = _rms_bwd(dy_ref[...], x_ref[...], g_ref[...])
        dg_ref[...] += dg

    return pl.pallas_call(
        body, grid=(n // tm,),
        in_specs=[pl.BlockSpec((tm, d), lambda i: (i, 0)), pl.BlockSpec((tm, d), lambda i: (i, 0)),
                  pl.BlockSpec((1, d), lambda i: (0, 0))],
        out_specs=pl.BlockSpec((1, d), lambda i: (0, 0)),
        out_shape=jax.ShapeDtypeStruct((1, d), F32),
        name=name, compiler_params=_cp((ARB,)))(dy, x, g)


def _resid_norm_fwd(x, y, g_post, g_pres, *, name, q=None):
    n, d = x.shape
    tm = _tile(n, 256, SUBLANE)
    nh = len(g_pres)

    def body(x_ref, y_ref, gp_ref, *rest):
        gpre = rest[:nh]
        xo_ref = rest[nh]
        h_refs = rest[nh + 1:]
        xo = x_ref[...] + _rms_fwd(y_ref[...].astype(F32), gp_ref[...])
        xo_ref[...] = xo
        for g_ref, h_ref in zip(gpre, h_refs):
            h_ref[...] = _rms_fwd(xo, g_ref[...]).astype(h_ref.dtype)

    row = pl.BlockSpec((tm, d), lambda i: (i, 0))
    vec = pl.BlockSpec((1, d), lambda i: (0, 0))
    outs = _hosted_call(
        body, grid=(n // tm,),
        in_specs=[row, row, vec] + [vec] * nh,
        out_specs=[row] + [row] * nh,
        out_shape=[jax.ShapeDtypeStruct((n, d), F32)] + [jax.ShapeDtypeStruct((n, d), MXU)] * nh,
        args=(x, y, g_post, *g_pres), name=name, q=q, budget_us=HOST_US["resid"])
    return outs[0], list(outs[1:])


def _loss_fwd(x, y, g_post, target, *, name):
    n, d = x.shape
    tm = _tile(n, 256, SUBLANE)

    def body(x_ref, y_ref, gp_ref, t_ref, dx_ref, sq_ref):
        @pl.when(pl.program_id(0) == 0)
        def _():
            sq_ref[...] = jnp.zeros_like(sq_ref)
        err = x_ref[...] + _rms_fwd(y_ref[...].astype(F32), gp_ref[...]) - t_ref[...]
        dx_ref[...] = err * (1.0 / d)
        sq_ref[...] += jnp.sum(err * err, axis=0, keepdims=True)

    row = pl.BlockSpec((tm, d), lambda i: (i, 0))
    vec = pl.BlockSpec((1, d), lambda i: (0, 0))
    return pl.pallas_call(
        body, grid=(n // tm,),
        in_specs=[row, row, vec, row],
        out_specs=[row, vec],
        out_shape=[jax.ShapeDtypeStruct((n, d), F32), jax.ShapeDtypeStruct((1, d), F32)],
        name=name, compiler_params=_cp((ARB,)))(x, y, g_post, target)


def _resid_norm_bwd(dx_out, dhs, x_out, g_pres, y, g_post, *, name, q=None):
    n, d = dx_out.shape
    tm = _tile(n, 256, SUBLANE)
    nh = len(dhs)
    has_y = y is not None

    def body(*refs):
        it = iter(refs)
        dxo_ref = next(it)
        dh_refs = [next(it) for _ in range(nh)]
        xo_ref = next(it) if nh else None
        gpre_refs = [next(it) for _ in range(nh)]
        y_ref = next(it) if has_y else None
        gpost_ref = next(it) if has_y else None
        g_out = next(it)
        dy_out = next(it) if has_y else None
        dgpre_out = [next(it) for _ in range(nh)]
        dgpost_out = next(it) if has_y else None

        @pl.when(pl.program_id(0) == 0)
        def _():
            for r in dgpre_out:
                r[...] = jnp.zeros_like(r)
            if has_y:
                dgpost_out[...] = jnp.zeros_like(dgpost_out)

        g = dxo_ref[...]
        if nh:
            xo = xo_ref[...]
            for dh_ref, gp_ref, dg_ref in zip(dh_refs, gpre_refs, dgpre_out):
                dx, dg = _rms_bwd(dh_ref[...].astype(F32), xo, gp_ref[...])
                g = g + dx
                dg_ref[...] += dg
        g_out[...] = g
        if has_y:
            dy, dg = _rms_bwd(g, y_ref[...].astype(F32), gpost_ref[...])
            dy_out[...] = dy.astype(dy_out.dtype)
            dgpost_out[...] += dg

    row = pl.BlockSpec((tm, d), lambda i: (i, 0))
    vec = pl.BlockSpec((1, d), lambda i: (0, 0))
    ins, in_specs = [dx_out], [row]
    ins += list(dhs)
    in_specs += [row] * nh
    if nh:
        ins.append(x_out)
        in_specs.append(row)
    ins += list(g_pres)
    in_specs += [vec] * nh
    if has_y:
        ins += [y, g_post]
        in_specs += [row, vec]
    out_specs, out_shape = [row], [jax.ShapeDtypeStruct((n, d), F32)]
    if has_y:
        out_specs.append(row)
        out_shape.append(jax.ShapeDtypeStruct((n, d), MXU))
    out_specs += [vec] * nh
    out_shape += [jax.ShapeDtypeStruct((1, d), F32)] * nh
    if has_y:
        out_specs.append(vec)
        out_shape.append(jax.ShapeDtypeStruct((1, d), F32))
    outs = list(_hosted_call(
        body, grid=(n // tm,), in_specs=in_specs, out_specs=out_specs, out_shape=out_shape,
        args=tuple(ins), name=name, q=q, budget_us=HOST_US["resid_bwd"]))
    g = outs.pop(0)
    dy = outs.pop(0) if has_y else None
    dgpre = [outs.pop(0) for _ in range(nh)]
    dgpost = outs.pop(0) if has_y else None
    return g, dy, dgpre, dgpost


def _ffn_conv(up, w_ref, b_ref):
    return (w_ref[0:1, :] * _shift_down_edge(up, 2) + w_ref[1:2, :] * _shift_down_edge(up, 1)
            + w_ref[2:3, :] * up + b_ref[...])


def _ffn_act_fwd(up, wconv, bconv, bsz, *, name, q=None):
    n, f2 = up.shape
    f = f2 // 2
    t = n // bsz
    tc = _tile(f, 256)
    nf = f // tc

    def body(ug_ref, uv_ref, wg_ref, wv_ref, bg_ref, bv_ref, o_ref, dag_ref, dav_ref):
        g = _ffn_conv(ug_ref[...].astype(F32), wg_ref, bg_ref)
        v = _ffn_conv(uv_ref[...].astype(F32), wv_ref, bv_ref)
        gl, dgl = _gelu_and_grad(g)
        dag_ref[...] = (v * dgl).astype(dag_ref.dtype)
        dav_ref[...] = gl.astype(dav_ref.dtype)
        o_ref[...] = (gl * v).astype(o_ref.dtype)

    blk = pl.BlockSpec((t, tc), lambda b, j: (b, j))
    return _hosted_call(
        body, grid=(bsz, nf),
        in_specs=[blk, pl.BlockSpec((t, tc), lambda b, j: (b, j + nf)),
                  pl.BlockSpec((FFN_CONV, tc), lambda b, j: (0, j)),
                  pl.BlockSpec((FFN_CONV, tc), lambda b, j: (0, j + nf)),
                  pl.BlockSpec((1, tc), lambda b, j: (0, j)),
                  pl.BlockSpec((1, tc), lambda b, j: (0, j + nf))],
        out_specs=[blk, blk, blk],
        out_shape=[jax.ShapeDtypeStruct((n, f), MXU)] * 3,
        args=(up, up, wconv, wconv, bconv, bconv), name=name, q=q, budget_us=HOST_US["ffn_act"])


def _ffn_act_bwd(up, ug, uv, dact, wconv, bsz, *, name, q=None):
    n, f2 = up.shape
    f = f2 // 2
    t = n // bsz
    tc = _tile(f, 256)
    nf = f // tc

    def body(xg_ref, xv_ref, g_ref, v_ref, da_ref, wg_ref, wv_ref,
             dug_ref, duv_ref, dwg_ref, dwv_ref, dbg_ref, dbv_ref):
        @pl.when(pl.program_id(1) == 0)
        def _():
            for r in (dwg_ref, dwv_ref, dbg_ref, dbv_ref):
                r[...] = jnp.zeros_like(r)

        da = da_ref[...].astype(F32)
        dg = da * g_ref[...].astype(F32)
        dv = da * v_ref[...].astype(F32)

        def conv_bwd(du, w_ref, x_ref, dx_ref, dw_ref, db_ref):
            du1, du2 = _shift_up_edge(du, 1), _shift_up_edge(du, 2)
            dx_ref[...] = (w_ref[2:3, :] * du + w_ref[1:2, :] * du1 + w_ref[0:1, :] * du2).astype(dx_ref.dtype)
            x = x_ref[...].astype(F32)
            dw_ref[0:1, :] += jnp.sum(x * du2, axis=0, keepdims=True)
            dw_ref[1:2, :] += jnp.sum(x * du1, axis=0, keepdims=True)
            dw_ref[2:3, :] += jnp.sum(x * du, axis=0, keepdims=True)
            db_ref[...] += jnp.sum(du, axis=0, keepdims=True)

        conv_bwd(dg, wg_ref, xg_ref, dug_ref, dwg_ref, dbg_ref)
        conv_bwd(dv, wv_ref, xv_ref, duv_ref, dwv_ref, dbv_ref)

    blk = pl.BlockSpec((t, tc), lambda j, b: (b, j))
    wspec = pl.BlockSpec((FFN_CONV, tc), lambda j, b: (0, j))
    bspec = pl.BlockSpec((1, tc), lambda j, b: (0, j))
    outs = _hosted_call(
        body, grid=(nf, bsz),
        in_specs=[blk, pl.BlockSpec((t, tc), lambda j, b: (b, j + nf)), blk, blk, blk,
                  wspec, pl.BlockSpec((FFN_CONV, tc), lambda j, b: (0, j + nf))],
        out_specs=[blk, blk, wspec, wspec, bspec, bspec],
        out_shape=[jax.ShapeDtypeStruct((n, f), MXU), jax.ShapeDtypeStruct((n, f), MXU),
                   jax.ShapeDtypeStruct((FFN_CONV, f), F32), jax.ShapeDtypeStruct((FFN_CONV, f), F32),
                   jax.ShapeDtypeStruct((1, f), F32), jax.ShapeDtypeStruct((1, f), F32)],
        args=(up, up, ug, uv, dact, wconv, wconv), name=name, q=q, budget_us=HOST_US["ffn_act_bwd"])
    dug, duv, dwg, dwv, dbg, dbv = outs
    return dug, duv, jnp.concatenate([dwg, dwv], axis=1), jnp.concatenate([dbg, dbv], axis=1)


def _mem_attn_fwd(proj, q_col_block, mkv, ycat, bsz, *, name, q=None):
    n = proj.shape[0]
    t = n // bsz
    tq = _tile(t, 512, SUBLANE)
    nt = t // tq
    scale = HEAD_DIM ** -0.5

    def body(q_ref, kv_ref, old_ref, o_ref):
        del old_ref
        outs = []
        for h in range(MEM_HEADS):
            sl = slice(h * HEAD_DIM, (h + 1) * HEAD_DIM)
            q = q_ref[:, sl].astype(MXU)
            k = kv_ref[:, sl].astype(MXU)
            v = kv_ref[:, MEM_WIDTH + h * HEAD_DIM: MEM_WIDTH + (h + 1) * HEAD_DIM].astype(MXU)
            s = _dot_nt(q, k) * scale
            m = jnp.max(s, axis=-1, keepdims=True)
            p = jnp.exp(s - m)
            p = p / jnp.sum(p, axis=-1, keepdims=True)
            outs.append(_dot(p.astype(MXU), v))
        o_ref[...] = jnp.concatenate(outs, axis=-1).astype(o_ref.dtype)

    return _hosted_call(
        body, grid=(bsz, nt),
        in_specs=[pl.BlockSpec((tq, MEM_WIDTH), lambda b, i: (b * nt + i, q_col_block)),
                  pl.BlockSpec((MEM_LEN, 2 * MEM_WIDTH), lambda b, i: (b, 0)),
                  pl.BlockSpec(memory_space=pl.ANY)],
        out_specs=pl.BlockSpec((tq, MEM_WIDTH), lambda b, i: (b * nt + i, MIX_WIDTH // MEM_WIDTH)),
        out_shape=jax.ShapeDtypeStruct(ycat.shape, ycat.dtype),
        aliases={2: 0}, args=(proj, mkv, ycat), name=name, q=q, budget_us=HOST_US["mem_attn_fwd"])


def _mem_attn_bwd(proj, q_col_block, mkv, dycat, dproj, bsz, *, name, q=None):
    n = proj.shape[0]
    t = n // bsz
    tq = _tile(t, 512, SUBLANE)
    nt = t // tq
    scale = HEAD_DIM ** -0.5

    def body(q_ref, kv_ref, do_ref, old_ref, dq_ref, dkv_ref):
        del old_ref

        @pl.when(pl.program_id(1) == 0)
        def _():
            dkv_ref[...] = jnp.zeros_like(dkv_ref)

        dqs, dks, dvs = [], [], []
        for h in range(MEM_HEADS):
            sl = slice(h * HEAD_DIM, (h + 1) * HEAD_DIM)
            q = q_ref[:, sl].astype(MXU)
            k = kv_ref[:, sl].astype(MXU)
            v = kv_ref[:, MEM_WIDTH + h * HEAD_DIM: MEM_WIDTH + (h + 1) * HEAD_DIM].astype(MXU)
            do = do_ref[:, sl].astype(MXU)
            s = _dot_nt(q, k) * scale
            m = jnp.max(s, axis=-1, keepdims=True)
            p = jnp.exp(s - m)
            p = p / jnp.sum(p, axis=-1, keepdims=True)
            dvs.append(_dot_tn(p.astype(MXU), do))
            dp = _dot_nt(do, v)
            ds = (p * (dp - jnp.sum(dp * p, axis=-1, keepdims=True)) * scale).astype(MXU)
            dqs.append(_dot(ds, k))
            dks.append(_dot_tn(ds, q))
        dq_ref[...] = jnp.concatenate(dqs, axis=-1).astype(dq_ref.dtype)
        dkv_ref[...] += jnp.concatenate(dks + dvs, axis=-1)

    return _hosted_call(
        body, grid=(bsz, nt),
        in_specs=[pl.BlockSpec((tq, MEM_WIDTH), lambda b, i: (b * nt + i, q_col_block)),
                  pl.BlockSpec((MEM_LEN, 2 * MEM_WIDTH), lambda b, i: (b, 0)),
                  pl.BlockSpec((tq, MEM_WIDTH), lambda b, i: (b * nt + i, MIX_WIDTH // MEM_WIDTH)),
                  pl.BlockSpec(memory_space=pl.ANY)],
        out_specs=[pl.BlockSpec((tq, MEM_WIDTH), lambda b, i: (b * nt + i, q_col_block)),
                   pl.BlockSpec((MEM_LEN, 2 * MEM_WIDTH), lambda b, i: (b, 0))],
        out_shape=[jax.ShapeDtypeStruct(dproj.shape, dproj.dtype),
                   jax.ShapeDtypeStruct((bsz * MEM_LEN, 2 * MEM_WIDTH), F32)],
        aliases={3: 0}, args=(proj, mkv, dycat, dproj), name=name, q=q, budget_us=HOST_US["mem_attn_bwd"])


def _swa_scores(q, k, h, dist, mask, sink):
    s = _dot_nt(q, k) * (HEAD_DIM ** -0.5)
    s = jnp.where(mask, s - SLOPES[h] * dist, -jnp.inf)
    m = jnp.maximum(jnp.max(s, axis=-1, keepdims=True), sink)
    p = jnp.exp(s - m)
    psink = jnp.exp(sink - m)
    inv = 1.0 / (jnp.sum(p, axis=-1, keepdims=True) + psink)
    return p * inv, psink * inv


def _swa_mask(n):
    qi = lax.broadcasted_iota(jnp.int32, (WINDOW, 2 * WINDOW), 0) + WINDOW
    ki = lax.broadcasted_iota(jnp.int32, (WINDOW, 2 * WINDOW), 1)
    dist = qi - ki
    mask = (dist >= 0) & (dist < WINDOW) & ((n > 0) | (ki >= WINDOW))
    return dist.astype(F32), mask


def _swa_fwd(proj, kv, sinks, bsz, *, name, q=None):
    n_tok = proj.shape[0]
    nb = n_tok // bsz // WINDOW
    kvw = SWA_KV_HEADS * HEAD_DIM

    def body(sink_ref, q_ref, kvp_ref, kvc_ref, o_ref):
        n = pl.program_id(1)
        dist, mask = _swa_mask(n)
        kk = jnp.concatenate([kvp_ref[:, :kvw], kvc_ref[:, :kvw]], axis=0).astype(MXU)
        vv = jnp.concatenate([kvp_ref[:, kvw:], kvc_ref[:, kvw:]], axis=0).astype(MXU)
        outs = []
        for h in range(SWA_HEADS):
            c = h // SWA_GROUP
            q = q_ref[:, h * HEAD_DIM:(h + 1) * HEAD_DIM].astype(MXU)
            p, _ = _swa_scores(q, kk[:, c * HEAD_DIM:(c + 1) * HEAD_DIM], h, dist, mask, sink_ref[h])
            outs.append(_dot(p.astype(MXU), vv[:, c * HEAD_DIM:(c + 1) * HEAD_DIM]))
        o_ref[...] = jnp.concatenate(outs, axis=-1).astype(o_ref.dtype)

    return _hosted_call(
        body, grid=(bsz, nb),
        in_specs=[pl.BlockSpec(memory_space=pltpu.SMEM),
                  pl.BlockSpec((WINDOW, MIX_WIDTH), lambda b, n: (b * nb + n, 0)),
                  pl.BlockSpec((WINDOW, 2 * kvw), lambda b, n: (b * nb + jnp.maximum(n - 1, 0), 0)),
                  pl.BlockSpec((WINDOW, 2 * kvw), lambda b, n: (b * nb + n, 0))],
        out_specs=pl.BlockSpec((WINDOW, MIX_WIDTH), lambda b, n: (b * nb + n, 0)),
        out_shape=jax.ShapeDtypeStruct((n_tok, D_MODEL), MXU),
        args=(sinks, proj, kv, kv), name=name, q=q, budget_us=HOST_US["swa_fwd"])


def _swa_bwd(proj, kv, sinks, dycat, bsz, *, name, q=None):
    n_tok = proj.shape[0]
    nb = n_tok // bsz // WINDOW
    kvw = SWA_KV_HEADS * HEAD_DIM

    def body(sink_ref, q_ref, kvp_ref, kvc_ref, do_ref, dq_ref, dkvc_ref, dkvp_ref, dsink_ref):
        n = pl.program_id(1)

        @pl.when((pl.program_id(0) == 0) & (n == 0))
        def _():
            dsink_ref[...] = jnp.zeros_like(dsink_ref)

        dist, mask = _swa_mask(n)
        kk = jnp.concatenate([kvp_ref[:, :kvw], kvc_ref[:, :kvw]], axis=0).astype(MXU)
        vv = jnp.concatenate([kvp_ref[:, kvw:], kvc_ref[:, kvw:]], axis=0).astype(MXU)
        lane = lax.broadcasted_iota(jnp.int32, (SUBLANE, LANE), 1)
        dqs = []
        dks = [None] * SWA_KV_HEADS
        dvs = [None] * SWA_KV_HEADS
        dsink = jnp.zeros((SUBLANE, LANE), F32)
        for h in range(SWA_HEADS):
            c = h // SWA_GROUP
            k = kk[:, c * HEAD_DIM:(c + 1) * HEAD_DIM]
            v = vv[:, c * HEAD_DIM:(c + 1) * HEAD_DIM]
            q = q_ref[:, h * HEAD_DIM:(h + 1) * HEAD_DIM].astype(MXU)
            do = do_ref[:, h * HEAD_DIM:(h + 1) * HEAD_DIM].astype(MXU)
            p, psink = _swa_scores(q, k, h, dist, mask, sink_ref[h])
            dv = _dot_tn(p.astype(MXU), do)
            dp = _dot_nt(do, v)
            rs = jnp.sum(dp * p, axis=-1, keepdims=True)
            ds = (p * (dp - rs) * (HEAD_DIM ** -0.5)).astype(MXU)
            dsink = dsink + jnp.where(lane == h, jnp.sum(-psink * rs, axis=0, keepdims=True), 0.0)
            dqs.append(_dot(ds, k))
            dk = _dot_tn(ds, q)
            dks[c] = dk if dks[c] is None else dks[c] + dk
            dvs[c] = dv if dvs[c] is None else dvs[c] + dv
        dq_ref[...] = jnp.concatenate(dqs, axis=-1).astype(dq_ref.dtype)
        dkv = jnp.concatenate(dks + dvs, axis=-1)
        dkvp_ref[...] = dkv[:WINDOW]
        dkvc_ref[...] = dkv[WINDOW:]
        dsink_ref[...] += dsink

    qspec = pl.BlockSpec((WINDOW, MIX_WIDTH), lambda b, n: (b * nb + n, 0))
    kvspec = pl.BlockSpec((WINDOW, 2 * kvw), lambda b, n: (b * nb + n, 0))
    return _hosted_call(
        body, grid=(bsz, nb),
        in_specs=[pl.BlockSpec(memory_space=pltpu.SMEM), qspec,
                  pl.BlockSpec((WINDOW, 2 * kvw), lambda b, n: (b * nb + jnp.maximum(n - 1, 0), 0)),
                  kvspec, qspec],
        out_specs=[qspec, kvspec, kvspec, pl.BlockSpec((SUBLANE, LANE), lambda b, n: (0, 0))],
        out_shape=[jax.ShapeDtypeStruct((n_tok, D_MODEL), MXU),
                   jax.ShapeDtypeStruct((n_tok, 2 * kvw), F32),
                   jax.ShapeDtypeStruct((n_tok, 2 * kvw), F32),
                   jax.ShapeDtypeStruct((SUBLANE, LANE), F32)],
        args=(sinks, proj, kv, kv, dycat), name=name, q=q, budget_us=HOST_US["swa_bwd"])


def _swa_dkv_combine(curs, prevs, bsz, *, name):
    n_tok, w = curs[0].shape
    nb = n_tok // bsz // WINDOW
    k = len(curs)

    def body(*refs):
        o_ref = refs[-1]
        n = pl.program_id(1)
        acc = refs[0][...]
        for r in refs[1:k]:
            acc = acc + r[...]
        nxt = refs[k][...]
        for r in refs[k + 1:2 * k]:
            nxt = nxt + r[...]
        o_ref[...] = (acc + jnp.where(n < nb - 1, nxt, 0.0)).astype(o_ref.dtype)

    cur = pl.BlockSpec((WINDOW, w), lambda b, n: (b * nb + n, 0))
    prv = pl.BlockSpec((WINDOW, w), lambda b, n: (b * nb + jnp.minimum(n + 1, nb - 1), 0))
    return pl.pallas_call(
        body, grid=(bsz, nb), in_specs=[cur] * k + [prv] * k, out_specs=cur,
        out_shape=jax.ShapeDtypeStruct((n_tok, w), MXU),
        name=name, compiler_params=_cp((PAR, PAR)))(*curs, *prevs)


def _lru_gates(ux, halo, ext_ref, wc_ref, bc_ref, wr_ref, br_ref, wi_ref, bi_ref, lam_ref):
    tt = ux.shape[0]
    ext_ref[0:SUBLANE, :] = halo
    ext_ref[SUBLANE:, :] = ux
    xs = [ux] + [ext_ref[pl.ds(SUBLANE - k, tt), :] for k in range(1, LRU_CONV)]
    xc = bc_ref[...] + wc_ref[3:4, :] * xs[0] + wc_ref[2:3, :] * xs[1] + wc_ref[1:2, :] * xs[2] + wc_ref[0:1, :] * xs[3]
    pre_r, pre_i = [], []
    for blk in range(MIX_WIDTH // GATE_TILE):
        xb = xc[:, blk * GATE_TILE:(blk + 1) * GATE_TILE].astype(MXU)
        pre_r.append(_dot(xb, wr_ref[blk]))
        pre_i.append(_dot(xb, wi_ref[blk]))
    r = jax.nn.sigmoid(jnp.concatenate(pre_r, axis=-1) + br_ref[...])
    i = jax.nn.sigmoid(jnp.concatenate(pre_i, axis=-1) + bi_ref[...])
    nlam = -lam_ref[...]
    sp = jnp.maximum(nlam, 0.0) + jnp.log(1.0 + jnp.exp(-jnp.abs(nlam)))
    log_a = -LRU_C * r * sp
    a = jnp.exp(log_a)
    om = -jnp.tanh(log_a) * (a * a + 1.0)
    s = jnp.sqrt(om)
    return xs, xc, r, i, sp, a, s


def _lru_fwd(proj, wconv, bconv, wr, br, wi, bi, lam, bsz, *, name, q=None):
    n_tok = proj.shape[0]
    t = n_tok // bsz
    tt = _tile(t, 256, SUBLANE)
    nt = t // tt
    w = MIX_WIDTH
    ng = tt // SUBLANE

    def body(pg_ref, halo_ref, wc_ref, bc_ref, wr_ref, br_ref, wi_ref, bi_ref, lam_ref,
             y_ref, h_ref, ext_ref, a_ref, b_ref, carry_ref):
        ti = pl.program_id(1)

        @pl.when(ti == 0)
        def _():
            carry_ref[...] = jnp.zeros_like(carry_ref)

        gate = pg_ref[:, :w]
        ux = pg_ref[:, w:]
        halo = jnp.where(ti > 0, halo_ref[...], 0.0)
        _, xc, _, i, _, a, s = _lru_gates(ux, halo, ext_ref, wc_ref, bc_ref, wr_ref, br_ref, wi_ref, bi_ref, lam_ref)
        a_ref[...] = a
        b_ref[...] = s * (i * xc)
        row = lax.broadcasted_iota(jnp.int32, (SUBLANE, w), 0)

        def group(g, hprev):
            off = pl.multiple_of(g * SUBLANE, SUBLANE)
            ca = a_ref[pl.ds(off, SUBLANE), :]
            cb = b_ref[pl.ds(off, SUBLANE), :]
            for d in (1, 2, 4):
                a_sh = jnp.where(row >= d, pltpu.roll(ca, d, axis=0), 1.0)
                b_sh = jnp.where(row >= d, pltpu.roll(cb, d, axis=0), 0.0)
                cb = ca * b_sh + cb
                ca = ca * a_sh
            h = ca * hprev + cb
            b_ref[pl.ds(off, SUBLANE), :] = h
            return jnp.broadcast_to(h[SUBLANE - 1:SUBLANE, :], (SUBLANE, w))

        carry_ref[...] = lax.fori_loop(0, ng, group, carry_ref[...])
        h = b_ref[...]
        h_ref[...] = h
        y_ref[...] = (h * _gelu(gate)).astype(y_ref.dtype)

    vec = lambda r: pl.BlockSpec((r, w), lambda b, i: (0, 0))
    wspec = pl.BlockSpec((w // GATE_TILE, GATE_TILE, GATE_TILE), lambda b, i: (0, 0, 0))
    hb = tt // SUBLANE
    return _hosted_call(
        body, grid=(bsz, nt),
        in_specs=[pl.BlockSpec((tt, 2 * w), lambda b, i: (b * nt + i, 0)),
                  pl.BlockSpec((SUBLANE, w), lambda b, i: (jnp.maximum((b * nt + i) * hb - 1, 0), 1)),
                  vec(LRU_CONV), vec(1), wspec, vec(1), wspec, vec(1), vec(1)],
        out_specs=[pl.BlockSpec((tt, w), lambda b, i: (b * nt + i, 0)),
                   pl.BlockSpec((tt, w), lambda b, i: (b * nt + i, 0))],
        out_shape=[jax.ShapeDtypeStruct((n_tok, D_MODEL), MXU), jax.ShapeDtypeStruct((n_tok, w), F32)],
        scratch_shapes=[pltpu.VMEM((tt + SUBLANE, w), F32), pltpu.VMEM((tt, w), F32),
                        pltpu.VMEM((tt, w), F32), pltpu.VMEM((SUBLANE, w), F32)],
        args=(proj, proj, wconv, bconv, wr, br, wi, bi, lam), name=name, q=q, budget_us=HOST_US["lru_fwd"])


def _lru_bwd(proj, hs, dycat, wconv, bconv, wr, br, wi, bi, lam, bsz, *, name, q=None):
    n_tok = proj.shape[0]
    t = n_tok // bsz
    tt = _tile(t, 256, SUBLANE)
    nt = t // tt
    w = MIX_WIDTH
    ng = tt // SUBLANE
    nblk = w // GATE_TILE

    def body(pg_ref, halo_ref, h_ref, hhalo_ref, dy_ref, wc_ref, bc_ref, wr_ref, br_ref, wi_ref, bi_ref, lam_ref,
             dp_ref, dwc_ref, dbc_ref, dwr_ref, dbr_ref, dwi_ref, dbi_ref, dlam_ref,
             ext_ref, a_ref, c_ref, g_ref, gcarry_ref, xcarry_ref):
        bi_ = pl.program_id(0)
        ti = nt - 1 - pl.program_id(1)

        @pl.when((bi_ == 0) & (pl.program_id(1) == 0))
        def _():
            for r in (dwc_ref, dbc_ref, dwr_ref, dbr_ref, dwi_ref, dbi_ref, dlam_ref):
                r[...] = jnp.zeros_like(r)

        @pl.when(pl.program_id(1) == 0)
        def _():
            gcarry_ref[...] = jnp.zeros_like(gcarry_ref)
            xcarry_ref[...] = jnp.zeros_like(xcarry_ref)

        gate = pg_ref[:, :w]
        ux = pg_ref[:, w:]
        halo = jnp.where(ti > 0, halo_ref[...], 0.0)
        xs, xc, r, i, sp, a, s = _lru_gates(ux, halo, ext_ref, wc_ref, bc_ref, wr_ref, br_ref, wi_ref, bi_ref, lam_ref)
        h = h_ref[...]
        gl, dgl = _gelu_and_grad(gate)
        dy = dy_ref[...].astype(F32)
        dgate = dy * h * dgl
        row_t = lax.broadcasted_iota(jnp.int32, (tt, w), 0)
        g_ref[...] = dy * gl + jnp.where(row_t == tt - 1, gcarry_ref[0:1, :], 0.0)
        c_ref[...] = _shift_up(a, 1, row_t)
        row = lax.broadcasted_iota(jnp.int32, (SUBLANE, w), 0)

        a_ref[...] = a

        def group(k, gnext):
            off = pl.multiple_of((ng - 1 - k) * SUBLANE, SUBLANE)
            cc = c_ref[pl.ds(off, SUBLANE), :]
            cb = g_ref[pl.ds(off, SUBLANE), :]
            cb = cb + jnp.where(row == SUBLANE - 1, gnext, 0.0)
            cc = jnp.where(row == SUBLANE - 1, 0.0, cc)
            for d in (1, 2, 4):
                c_sh = jnp.where(row < SUBLANE - d, pltpu.roll(cc, SUBLANE - d, axis=0), 1.0)
                b_sh = jnp.where(row < SUBLANE - d, pltpu.roll(cb, SUBLANE - d, axis=0), 0.0)
                cb = cc * b_sh + cb
                cc = cc * c_sh
            g_ref[pl.ds(off, SUBLANE), :] = cb
            a0 = a_ref[pl.ds(off, SUBLANE), :]
            return jnp.broadcast_to(a0[0:1, :] * cb[0:1, :], (SUBLANE, w))

        gc = lax.fori_loop(0, ng, group, jnp.zeros((SUBLANE, w), F32))
        gcarry_ref[...] = gc
        gsc = g_ref[...]

        hhalo = jnp.where(ti > 0, hhalo_ref[SUBLANE - 1:SUBLANE, :], 0.0)
        hprev = jnp.where(row_t == 0, hhalo, pltpu.roll(h, 1, axis=0))
        gated = i * xc
        d_gated = gsc * s
        d_atot = gsc * hprev - (gsc * gated) * a / s
        d_loga = d_atot * a
        d_r = d_loga * (-LRU_C) * sp
        dlam_ref[...] += jnp.sum(d_loga * r, axis=0, keepdims=True) * (LRU_C * jax.nn.sigmoid(-lam_ref[...]))
        d_i = d_gated * xc
        d_xc = d_gated * i
        d_pr = d_r * r * (1.0 - r)
        d_pi = d_i * i * (1.0 - i)
        dbr_ref[...] += jnp.sum(d_pr, axis=0, keepdims=True)
        dbi_ref[...] += jnp.sum(d_pi, axis=0, keepdims=True)
        extra = []
        for blk in range(nblk):
            sl = slice(blk * GATE_TILE, (blk + 1) * GATE_TILE)
            xb = xc[:, sl].astype(MXU)
            dr_b = d_pr[:, sl].astype(MXU)
            di_b = d_pi[:, sl].astype(MXU)
            dwr_ref[blk] += _dot_tn(xb, dr_b)
            dwi_ref[blk] += _dot_tn(xb, di_b)
            extra.append(_dot_nt(dr_b, wr_ref[blk]) + _dot_nt(di_b, wi_ref[blk]))
        d_xc = d_xc + jnp.concatenate(extra, axis=-1)
        dbc_ref[...] += jnp.sum(d_xc, axis=0, keepdims=True)
        for k in range(LRU_CONV):
            dwc_ref[k:k + 1, :] += jnp.sum(d_xc * xs[LRU_CONV - 1 - k], axis=0, keepdims=True)
        ext_ref[0:tt, :] = d_xc
        ext_ref[tt:, :] = xcarry_ref[...]
        dux = wc_ref[3:4, :] * d_xc
        for k in range(LRU_CONV - 1):
            dux = dux + wc_ref[k:k + 1, :] * ext_ref[pl.ds(LRU_CONV - 1 - k, tt), :]
        xcarry_ref[...] = d_xc[0:SUBLANE, :]
        dp_ref[:, :w] = dgate.astype(dp_ref.dtype)
        dp_ref[:, w:] = dux.astype(dp_ref.dtype)

    vec = lambda r: pl.BlockSpec((r, w), lambda b, i: (0, 0))
    wspec = pl.BlockSpec((nblk, GATE_TILE, GATE_TILE), lambda b, i: (0, 0, 0))
    hb = tt // SUBLANE
    rblk = lambda b, i: b * nt + (nt - 1 - i)
    halo_idx = lambda b, i: jnp.maximum(rblk(b, i) * hb - 1, 0)
    wide = pl.BlockSpec((tt, 2 * w), lambda b, i: (rblk(b, i), 0))
    narrow = pl.BlockSpec((tt, w), lambda b, i: (rblk(b, i), 0))
    return _hosted_call(
        body, grid=(bsz, nt),
        in_specs=[wide, pl.BlockSpec((SUBLANE, w), lambda b, i: (halo_idx(b, i), 1)),
                  narrow, pl.BlockSpec((SUBLANE, w), lambda b, i: (halo_idx(b, i), 0)), narrow,
                  vec(LRU_CONV), vec(1), wspec, vec(1), wspec, vec(1), vec(1)],
        out_specs=[wide, vec(LRU_CONV), vec(1), wspec, vec(1), wspec, vec(1), vec(1)],
        out_shape=[jax.ShapeDtypeStruct((n_tok, 2 * w + MEM_WIDTH), MXU),
                   jax.ShapeDtypeStruct((LRU_CONV, w), F32), jax.ShapeDtypeStruct((1, w), F32),
                   jax.ShapeDtypeStruct((nblk, GATE_TILE, GATE_TILE), F32), jax.ShapeDtypeStruct((1, w), F32),
                   jax.ShapeDtypeStruct((nblk, GATE_TILE, GATE_TILE), F32), jax.ShapeDtypeStruct((1, w), F32),
                   jax.ShapeDtypeStruct((1, w), F32)],
        scratch_shapes=[pltpu.VMEM((tt + SUBLANE, w), F32), pltpu.VMEM((tt, w), F32), pltpu.VMEM((tt, w), F32),
                        pltpu.VMEM((tt, w), F32), pltpu.VMEM((SUBLANE, w), F32), pltpu.VMEM((SUBLANE, w), F32)],
        args=(proj, proj, hs, hs, dycat, wconv, bconv, wr, br, wi, bi, lam), name=name, q=q,
        budget_us=HOST_US["lru_bwd"])


def _gate_tiles(w):
    per = GATE_TILE // HEAD_DIM
    w4 = w.reshape(LRU_BLOCKS // per, per, HEAD_DIM, HEAD_DIM)
    eye = jnp.eye(per, dtype=w.dtype)
    return jnp.einsum("bnij,nm->bnimj", w4, eye).reshape(LRU_BLOCKS // per, GATE_TILE, GATE_TILE)


def _gate_blocks(t):
    per = GATE_TILE // HEAD_DIM
    t5 = t.reshape(LRU_BLOCKS // per, per, HEAD_DIM, per, HEAD_DIM)
    eye = jnp.eye(per, dtype=t.dtype)
    return jnp.einsum("bnimj,nm->bnij", t5, eye).reshape(LRU_BLOCKS, HEAD_DIM, HEAD_DIM)


def _row(v):
    return v.reshape(1, -1)


def _local_step(x, mem, target, p, wfull, push_grad, q):
    bsz, t, d = x.shape
    n = bsz * t
    x2d = x.reshape(n, d)
    tgt = target.reshape(n, d)
    mem2d = mem.reshape(bsz * MEM_LEN, d)
    wr_t = [_gate_tiles(p["w_rg_r"][j]).astype(MXU) for j in range(N_A)]
    wi_t = [_gate_tiles(p["w_rg_i"][j]).astype(MXU) for j in range(N_A)]

    mn = [_norm_fwd(mem2d, _row(p["g_mem"][l]), name=f"mem_norm{l}") for l in range(DEPTH)]
    mkv = [None] * DEPTH
    h = _norm_fwd(x2d, _row(p["g_mix_pre"][0]), name="in_norm")
    xin = x2d
    sv = []
    kv = hkv = None
    for l in range(DEPTH):
        s = {"xin": xin, "h": h}
        if q is not None:
            q.horizon = (l + 2) * GROUPS_PER_LAYER
        mkv[l] = _mm_nn(mn[l], wfull("w_mem_kv", l), name=f"mem_kv{l}", q=q)
        if l < N_A:
            proj = _mm_nn(h, wfull("w_in_a", l), name=f"in_proj{l}", q=q)
            ycat, hs = _lru_fwd(proj, p["w_conv_a"][l], _row(p["b_conv_a"][l]), wr_t[l], _row(p["b_rg_r"][l]),
                                wi_t[l], _row(p["b_rg_i"][l]), _row(p["lru_lambda"][l]), bsz, name=f"lru_fwd{l}", q=q)
            s["hs"] = hs
            qblk = 2 * MIX_WIDTH // MEM_WIDTH
        else:
            if l == N_A:
                kv = _mm_nn(hkv, wfull("w_kv", 0), name="kv_proj", q=q)
            proj = _mm_nn(h, wfull("w_in_b", l - N_A), name=f"in_proj{l}", q=q)
            ycat = _swa_fwd(proj, kv, p["sinks_b"][l - N_A], bsz, name=f"swa_fwd{l}", q=q)
            qblk = MIX_WIDTH // MEM_WIDTH
        ycat = _mem_attn_fwd(proj, qblk, mkv[l], ycat, bsz, name=f"mem_attn_fwd{l}", q=q)
        y = _mm_nn(ycat, wfull("w_mix_out", l), name=f"mix_out{l}", q=q, out_dtype=MXU)
        x1, (h2,) = _resid_norm_fwd(xin, y, _row(p["g_mix_post"][l]), [_row(p["g_ffn_pre"][l])], name=f"mix_resid{l}", q=q)
        up = _mm_nn_slots(h2, wfull("w_ffn_up", l), name=f"ffn_up{l}", q=q, out_dtype=MXU)
        act, ug, uv = _ffn_act_fwd(up, p["w_ffn_conv"][l], _row(p["b_ffn_conv"][l]), bsz, name=f"ffn_act{l}", q=q)
        f = _mm_nn(act, wfull("w_ffn_down", l), name=f"ffn_down{l}", q=q, out_dtype=MXU)
        s.update(proj=proj, qblk=qblk, ycat=ycat, y=y, x1=x1, h2=h2, up=up, ug=ug, uv=uv, act=act, f=f)
        sv.append(s)
        if l < DEPTH - 1:
            g_pres = [_row(p["g_mix_pre"][l + 1])] + ([_row(p["g_kv"])] if l + 1 == N_A else [])
            xin, hn = _resid_norm_fwd(x1, f, _row(p["g_ffn_post"][l]), g_pres, name=f"ffn_resid{l}", q=q)
            h = hn[0]
            if l + 1 == N_A:
                hkv = hn[1]
        else:
            g_tot, sq = _loss_fwd(x1, f, _row(p["g_ffn_post"][l]), tgt, name="loss")

    if q is not None:
        q.horizon = LAST_GROUP
    gs = {k: [None] * DEPTH for k in ("g_mix_pre", "g_mix_post", "g_ffn_pre", "g_ffn_post", "g_mem",
                                       "w_ffn_conv", "b_ffn_conv")}
    ga = {k: [None] * N_A for k in ("w_conv_a", "b_conv_a", "w_rg_r", "b_rg_r", "w_rg_i", "b_rg_i", "lru_lambda")}
    gsink = [None] * (DEPTH - N_A)
    dkv_cur, dkv_prev = [], []
    g_tot, df, _, gs["g_ffn_post"][DEPTH - 1] = _resid_norm_bwd(
        g_tot, [], None, [], sv[-1]["f"], _row(p["g_ffn_post"][DEPTH - 1]), name="loss_bwd")
    grad_x = None
    for l in reversed(range(DEPTH)):
        s = sv[l]
        dact = _mm_nt(df, wfull("w_ffn_down", l), name=f"d_act{l}", q=q, out_dtype=MXU)
        push_grad("w_ffn_down", l, _mm_tn(s["act"], df, name=f"dw_down{l}", q=q))
        dug, duv, gs["w_ffn_conv"][l], gs["b_ffn_conv"][l] = _ffn_act_bwd(
            s["up"], s["ug"], s["uv"], dact, p["w_ffn_conv"][l], bsz, name=f"ffn_act_bwd{l}", q=q)
        dh2 = _mm_ffn_dh(dug, duv, wfull("w_ffn_up", l), name=f"d_h2_{l}", q=q)
        up_slots = dict(slot_cols=2 * D_FF // N_CHIP, n_slots=N_CHIP)
        dwu = _mm_tn_slots(s["h2"], dug, name=f"dw_up_g{l}", q=q, **up_slots)
        push_grad("w_ffn_up", l, _mm_tn_slots(s["h2"], duv, name=f"dw_up_v{l}", q=q, out=dwu,
                                              first_slot=N_CHIP // 2, **up_slots))
        g1, dy, (gs["g_ffn_pre"][l],), gs["g_mix_post"][l] = _resid_norm_bwd(
            g_tot, [dh2], s["x1"], [_row(p["g_ffn_pre"][l])], s["y"], _row(p["g_mix_post"][l]), name=f"mix_resid_bwd{l}", q=q)
        dycat = _mm_nt(dy, wfull("w_mix_out", l), name=f"d_ycat{l}", q=q, out_dtype=MXU)
        push_grad("w_mix_out", l, _mm_tn(s["ycat"], dy, name=f"dw_mix_out{l}", q=q))
        if l < N_A:
            dproj, dwc, dbc, dwr, dbr, dwi, dbi, dlam = _lru_bwd(
                s["proj"], s["hs"], dycat, p["w_conv_a"][l], _row(p["b_conv_a"][l]), wr_t[l], _row(p["b_rg_r"][l]),
                wi_t[l], _row(p["b_rg_i"][l]), _row(p["lru_lambda"][l]), bsz, name=f"lru_bwd{l}", q=q)
            ga["w_conv_a"][l], ga["b_conv_a"][l], ga["lru_lambda"][l] = dwc, dbc[0], dlam[0]
            ga["w_rg_r"][l], ga["w_rg_i"][l] = _gate_blocks(dwr), _gate_blocks(dwi)
            ga["b_rg_r"][l] = dbr.reshape(LRU_BLOCKS, HEAD_DIM)
            ga["b_rg_i"][l] = dbi.reshape(LRU_BLOCKS, HEAD_DIM)
            w_in, j = "w_in_a", l
        else:
            dproj, dc, dp_, dsk = _swa_bwd(s["proj"], kv, p["sinks_b"][l - N_A], dycat, bsz, name=f"swa_bwd{l}", q=q)
            dkv_cur.append(dc)
            dkv_prev.append(dp_)
            gsink[l - N_A] = dsk[0, :SWA_HEADS]
            w_in, j = "w_in_b", l - N_A
        dproj, dmkv = _mem_attn_bwd(s["proj"], s["qblk"], mkv[l], dycat, dproj, bsz, name=f"mem_attn_bwd{l}", q=q)
        dh = _mm_nt(dproj, wfull(w_in, j), name=f"d_h{l}", q=q, out_dtype=MXU)
        push_grad(w_in, j, _mm_tn(s["h"], dproj, name=f"dw_in{l}", q=q))
        dmkv = dmkv.astype(MXU)
        dmn = _mm_nt(dmkv, wfull("w_mem_kv", l), name=f"d_mem_norm{l}", q=q)
        push_grad("w_mem_kv", l, _mm_tn(mn[l], dmkv, name=f"dw_mem_kv{l}", q=q))
        gs["g_mem"][l] = _norm_bwd_dg(dmn, mem2d, _row(p["g_mem"][l]), name=f"mem_norm_bwd{l}")
        dhs, g_pres = [dh], [_row(p["g_mix_pre"][l])]
        if l == N_A:
            dkv = _swa_dkv_combine(dkv_cur, dkv_prev, bsz, name="dkv_combine")
            dhs.append(_mm_nt(dkv, wfull("w_kv", 0), name="d_hkv", q=q, out_dtype=MXU))
            g_pres.append(_row(p["g_kv"]))
            push_grad("w_kv", 0, _mm_tn(hkv, dkv, name="dw_kv", q=q))
        if l > 0:
            g_tot, df, dgpre, gs["g_ffn_post"][l - 1] = _resid_norm_bwd(
                g1, dhs, s["xin"], g_pres, sv[l - 1]["f"], _row(p["g_ffn_post"][l - 1]), name=f"ffn_resid_bwd{l - 1}", q=q)
        else:
            grad_x, _, dgpre, _ = _resid_norm_bwd(g1, dhs, s["xin"], g_pres, None, None, name="in_norm_bwd", q=q)
        gs["g_mix_pre"][l] = dgpre[0]
        if l == N_A:
            g_kv = dgpre[1][0]

    grads = {}
    for k in ("g_mix_pre", "g_mix_post", "g_ffn_pre", "g_ffn_post", "g_mem", "b_ffn_conv"):
        grads[k] = jnp.concatenate(gs[k], axis=0)
    grads["w_ffn_conv"] = jnp.stack(gs["w_ffn_conv"])
    for k, v in ga.items():
        grads[k] = jnp.stack(v)
    grads["sinks_b"] = jnp.stack(gsink)
    grads["g_kv"] = g_kv
    return jnp.sum(sq), grad_x.reshape(bsz, t, d), grads


N_CHIP = 4
HALF_ALIGN = 16
D2D_STREAMS = 2
MIN_PART_BYTES = 128 * 1024


def _full_shape(kind, shard_shape):
    l, r, c = shard_shape
    return {"row": (l, N_CHIP * r, c), "col": (l, r, N_CHIP * c), "slot": (N_CHIP, l, r, c)}[kind]


def _slot_view(ref, kind, shard_shape, s, hf, sub=(0, 1)):
    _, r, c = shard_shape
    rh = r // 2
    if hf is None:
        size = r // sub[1]
        start = sub[0] * size
    else:
        size = rh // sub[1]
        start = hf * rh + sub[0] * size
    if kind == "row":
        start = s * r + start
    if not isinstance(start, int):
        start = pl.multiple_of(start, HALF_ALIGN)
    rows = pl.ds(start, size)
    if kind == "row":
        return ref.at[:, rows, :]
    if kind == "col":
        return ref.at[:, rows, pl.ds(s * c, c)]
    return ref.at[s, :, rows, :]


def _half_view(ref, shard_shape, hf, sub=(0, 1)):
    rh = shard_shape[1] // 2
    size = rh // sub[1]
    return ref.at[:, pl.ds(pl.multiple_of(hf * rh + sub[0] * size, HALF_ALIGN), size), :]


def _with_slot(kind, s, fn):
    if kind != "col" or isinstance(s, int):
        fn(s)
        return
    for k in range(N_CHIP):
        @pl.when(s == k)
        def _(k=k):
            fn(k)


def _mesh_pos():
    return lax.axis_index("x"), lax.axis_index("y"), lax.axis_index("c")


def _other_chips(x, y):
    return [(1 - x, y), (x, 1 - y), (1 - x, 1 - y)]


ICI_BYTES_PER_US = 6.0e4
ICI_GATHER_BYTES_PER_US = 5.5e4
D2D_BYTES_PER_US = 4.0e5


class _Chunk:
    def __init__(self, group, cost, ins, out_shapes, alias, n_sem, start, finish, done, buffer=None, bind=None):
        self.group, self.cost, self.ins, self.out_shapes, self.alias, self.n_sem = group, cost, ins, out_shapes, alias, n_sem
        self.start, self.finish, self.done = start, finish, done
        self.buffer = buffer
        self.bind = bind

    def prepare(self):
        if self.bind is not None:
            self.bind(self)


def _merged(chunks):
    groups, by_buffer = [], {}
    for ch in chunks:
        key = None if ch.buffer is None else (id(ch.buffer[0]), ch.buffer[1])
        if key is not None and key in by_buffer:
            by_buffer[key].append(ch)
        else:
            groups.append([ch])
            if key is not None:
                by_buffer[key] = groups[-1]
    out = []
    for parts in groups:
        if len(parts) == 1:
            out.append(parts[0])
            continue
        offs = [sum(p.n_sem for p in parts[:i]) for i in range(len(parts))]

        def run(phase, ins, outs, ss, rs, b, parts=parts, offs=offs):
            for p, o in zip(parts, offs):
                getattr(p, phase)(ins, outs, ss, rs, b + o)

        def done(outs, parts=parts):
            for p in parts:
                p.done(outs)

        first = parts[0]
        out.append(_Chunk(first.group, sum(p.cost for p in parts), first.ins, first.out_shapes, first.alias,
                          sum(p.n_sem for p in parts), functools.partial(run, "start"),
                          functools.partial(run, "finish"), done))
    return out


LAST_GROUP = 1 << 30
MIN_CARRIED_US = 8.0


class _CommQueue:
    def __init__(self):
        self.pending = []
        self.flushes = 0
        self.horizon = LAST_GROUP

    def push(self, chunk):
        self.pending.append(chunk)

    def take(self, budget_us):
        got, used = [], 0.0
        for ch in sorted(self.pending, key=lambda ch: (ch.group, -ch.cost)):
            if ch.group >= self.horizon and ch.group != LAST_GROUP:
                continue
            if used + ch.cost <= budget_us and not self._shares_buffer(ch, got):
                got.append(ch)
                used += ch.cost
        if used < MIN_CARRIED_US:
            return []
        return self._taken(got)

    @staticmethod
    def _shares_buffer(ch, others):
        return ch.buffer is not None and any(
            o.buffer is not None and o.buffer[0] is ch.buffer[0] and o.buffer[1] != ch.buffer[1] for o in others)

    def _taken(self, got):
        self.pending = [ch for ch in self.pending if ch not in got]
        for ch in got:
            ch.prepare()
        return _merged(got)

    def flush(self, group=LAST_GROUP):
        while True:
            chunks = []
            for ch in self.pending:
                if ch.group <= group and not self._shares_buffer(ch, chunks):
                    chunks.append(ch)
            if not chunks:
                return
            _run_chunks(self._taken(chunks), name=f"comm_flush{self.flushes}")
            self.flushes += 1


def _run_chunks(chunks, *, name):
    ins = [a for ch in chunks for a in ch.ins]
    outs = [s for ch in chunks for s in ch.out_shapes]
    alias, offs = {}, []
    i0 = o0 = s0 = 0
    for ch in chunks:
        offs.append((i0, o0, s0))
        for ci, co in ch.alias.items():
            alias[i0 + ci] = o0 + co
        i0 += len(ch.ins)
        o0 += len(ch.out_shapes)
        s0 += ch.n_sem

    def body(*refs):
        send_sems, recv_sems = refs[i0 + o0:]
        for phase in ("start", "finish"):
            for ch, (a, b, s) in zip(chunks, offs):
                getattr(ch, phase)(refs[a:a + len(ch.ins)], refs[i0 + b:i0 + b + len(ch.out_shapes)],
                                   send_sems, recv_sems, s)

    hbm = pl.BlockSpec(memory_space=pl.ANY)
    res = pl.pallas_call(
        body, in_specs=[hbm] * i0, out_specs=[hbm] * o0, out_shape=outs,
        scratch_shapes=[pltpu.SemaphoreType.DMA((s0,)), pltpu.SemaphoreType.DMA((s0,))],
        input_output_aliases=alias, name=name, compiler_params=pltpu.CompilerParams(has_side_effects=True))(*ins)
    for ch, (_, b, _) in zip(chunks, offs):
        ch.done(list(res[b:b + len(ch.out_shapes)]))


def _remote(src, dst, send_sems, recv_sems, k, dev):
    return pltpu.make_async_remote_copy(src_ref=src, dst_ref=dst, send_sem=send_sems.at[k], recv_sem=recv_sems.at[k],
                                        device_id=dev, device_id_type=MESH_T)


def _gather_chunks(q, group, kind, shard, l, ready):
    _, r, c = shard.shape
    shp = (1, r, c)
    rh = r // 2
    parts = max(p for p in (8, 4, 2, 1)
                if (rh // p) % HALF_ALIGN == 0 and (p == 1 or (rh // p) * c * shard.dtype.itemsize >= MIN_PART_BYTES))
    part_bytes = (rh // parts) * c * shard.dtype.itemsize
    full_type = jax.ShapeDtypeStruct(_full_shape(kind, shp), shard.dtype)
    state = {"full": None, "parts_done": 0}

    def bind_first(ch):
        ch.ins, ch.alias = ([shard], {}) if state["full"] is None else ([shard, state["full"]], {1: 0})

    def bind_full(ch):
        ch.ins = [state["full"]]

    def make_part(p):
        sub = (p, parts)

        def any_part(full):
            return _slot_view(full, kind, shp, 0, 0, sub)

        def own_rows(src):
            return src.at[:, pl.ds(p * (r // parts), r // parts), :]

        def start1(ins, outs, ss, rs, b):
            x, y, c_ = _mesh_pos()
            src, full = ins[0].at[pl.ds(l, 1)], outs[0]
            _with_slot(kind, 2 * x + y, lambda s: pltpu.make_async_copy(
                own_rows(src), _slot_view(full, kind, shp, s, None, sub), ss.at[b + N_CHIP - 1]).start())
            for j, (ox, oy) in enumerate(_other_chips(x, y)):
                _with_slot(kind, 2 * x + y, lambda s, j=j, ox=ox, oy=oy: _remote(
                    _half_view(src, shp, c_, sub), _slot_view(full, kind, shp, s, c_, sub), ss, rs, b + j,
                    (ox, oy, c_)).start())

        def finish1(ins, outs, ss, rs, b):
            x, y, c_ = _mesh_pos()
            h = any_part(outs[0])
            for j in range(N_CHIP - 1):
                _remote(h, h, ss, rs, b + j, (x, y, 1 - c_)).wait()
            pltpu.make_async_copy(own_rows(ins[0].at[pl.ds(l, 1)]), _slot_view(outs[0], kind, shp, 0, None, sub),
                                  ss.at[b + N_CHIP - 1]).wait()

        def start2(ins, outs, ss, rs, b):
            x, y, c_ = _mesh_pos()
            for j, (ox, oy) in enumerate(_other_chips(x, y)):
                def forward(s, j=j):
                    v = _slot_view(outs[0], kind, shp, s, c_, sub)
                    _remote(v, v, ss, rs, b + j, (x, y, 1 - c_)).start()
                _with_slot(kind, 2 * ox + oy, forward)

        def finish2(ins, outs, ss, rs, b):
            x, y, c_ = _mesh_pos()
            h = any_part(outs[0])
            for j in range(N_CHIP - 1):
                _remote(h, h, ss, rs, b + j, (x, y, 1 - c_)).wait()

        def done2(outs):
            state["full"] = outs[0]
            state["parts_done"] += 1
            if state["parts_done"] == parts:
                ready(outs[0])

        def done1(outs):
            state["full"] = outs[0]
            q.push(_Chunk(group, 3 * part_bytes / D2D_BYTES_PER_US, None, [full_type], {0: 0}, N_CHIP - 1,
                          start2, finish2, done2, buffer=(state, 2), bind=bind_full))

        return _Chunk(group, 3 * part_bytes / ICI_GATHER_BYTES_PER_US, None, [full_type], None,
                      N_CHIP, start1, finish1, done1, buffer=(state, 1), bind=bind_first)

    for p in range(parts):
        q.push(make_part(p))


def _reduce_scatter_chunks(q, kind, grad, shard_shape, pos, name, ready):
    _, r, c = shard_shape
    shp = (1, r, c)
    rh = r // 2

    rp = rh // D2D_STREAMS

    def landing(ref, s, i):
        return ref.at[s, :, pl.ds(i * rp, rp), :]

    def start1(ins, outs, ss, rs, b):
        x, y, c_ = _mesh_pos()
        for s in range(N_CHIP):
            for i in range(D2D_STREAMS):
                _remote(_slot_view(ins[0], kind, shp, s, 1 - c_, (i, D2D_STREAMS)), landing(outs[0], s, i),
                        ss, rs, b + s * D2D_STREAMS + i, (x, y, 1 - c_)).start()

    def finish1(ins, outs, ss, rs, b):
        x, y, c_ = _mesh_pos()
        for s in range(N_CHIP):
            for i in range(D2D_STREAMS):
                v = landing(outs[0], s, i)
                _remote(v, v, ss, rs, b + s * D2D_STREAMS + i, (x, y, 1 - c_)).wait()

    def start2(ins, outs, ss, rs, b):
        x, y, c_ = _mesh_pos()
        for j, (ox, oy) in enumerate(_other_chips(x, y)):
            _remote(ins[0].at[2 * ox + oy], outs[0].at[j], ss, rs, b + j, (ox, oy, c_)).start()

    def finish2(ins, outs, ss, rs, b):
        x, y, c_ = _mesh_pos()
        for j in range(N_CHIP - 1):
            _remote(outs[0].at[j], outs[0].at[j], ss, rs, b + j, (x, y, 1 - c_)).wait()

    def start3(ins, outs, ss, rs, b):
        x, y, c_ = _mesh_pos()
        for i in range(D2D_STREAMS):
            v = _half_view(outs[0], shp, c_, (i, D2D_STREAMS))
            _remote(v, v, ss, rs, b + i, (x, y, 1 - c_)).start()

    def finish3(ins, outs, ss, rs, b):
        x, y, c_ = _mesh_pos()
        for i in range(D2D_STREAMS):
            v = _half_view(outs[0], shp, c_, (i, D2D_STREAMS))
            _remote(v, v, ss, rs, b + i, (x, y, 1 - c_)).wait()

    def done2(pair, outs):
        half = _rs_chip_add(pair, outs[0], shp, pos, name=f"rs_chip_add_{name}")
        q.push(_Chunk(LAST_GROUP, rh * c * 4 / D2D_BYTES_PER_US, [half], [jax.ShapeDtypeStruct(half.shape, half.dtype)],
                      {0: 0}, D2D_STREAMS, start3, finish3, lambda o: ready(o[0])))

    def done1(outs):
        pair, wire = _rs_pair_add(grad, outs[0], kind, shp, pos, name=f"rs_pair_add_{name}")
        q.push(_Chunk(LAST_GROUP, 3 * rh * c * wire.dtype.itemsize / ICI_BYTES_PER_US, [wire],
                      [jax.ShapeDtypeStruct((N_CHIP - 1, 1, rh, c), wire.dtype)], {}, N_CHIP - 1,
                      start2, finish2, functools.partial(done2, pair)))

    q.push(_Chunk(LAST_GROUP, N_CHIP * rh * c * 4 / D2D_BYTES_PER_US, [grad],
                  [jax.ShapeDtypeStruct((N_CHIP, 1, rh, c), F32)], {}, N_CHIP * D2D_STREAMS, start1, finish1, done1))


def _allgather8(vec, *, name):
    r = vec.shape[0]
    n_dev = 8

    def body(v_ref, buf, send_sems, recv_sems):
        x, y, c = _mesh_pos()
        me = 4 * x + 2 * y + c
        copies = []
        for k in range(1, n_dev):
            kx, ky, kc = (k >> 2) & 1, (k >> 1) & 1, k & 1
            peer = ((1 - x) if kx else x, (1 - y) if ky else y, (1 - c) if kc else c)
            cp = _remote(v_ref, buf.at[me], send_sems, recv_sems, k - 1, peer)
            cp.start()
            copies.append(cp)
        buf[me] = v_ref[...]
        for cp in copies:
            cp.wait()

    vm = pl.BlockSpec(memory_space=pltpu.VMEM)
    return pl.pallas_call(
        body, in_specs=[vm], out_specs=vm, out_shape=jax.ShapeDtypeStruct((n_dev, r, LANE), F32),
        scratch_shapes=[pltpu.SemaphoreType.DMA((n_dev - 1,)), pltpu.SemaphoreType.DMA((n_dev - 1,))],
        name=name, compiler_params=pltpu.CompilerParams(has_side_effects=True, vmem_limit_bytes=VMEM_LIMIT_V7X))(vec)


def _allreduce8(vec, *, name):
    r = vec.shape[0]
    rh = r // 2

    def body(v_ref, o_ref, sib_ref, chips_ref, send_sems, recv_sems):
        x, y, c = _mesh_pos()
        sib = (x, y, 1 - c)
        me = 2 * x + y
        pair = _remote(v_ref, sib_ref, send_sems, recv_sems, 0, sib)
        pair.start()
        pair.wait()
        rows = pl.ds(pl.multiple_of(c * rh, SUBLANE), rh)
        chips_ref[me] = v_ref[rows, :] + sib_ref[rows, :]
        copies = []
        for j, (ox, oy) in enumerate(_other_chips(x, y)):
            cp = _remote(chips_ref.at[me], chips_ref.at[me], send_sems, recv_sems, 1 + j, (ox, oy, c))
            cp.start()
            copies.append(cp)
        for cp in copies:
            cp.wait()
        acc = chips_ref[0]
        for s in range(1, N_CHIP):
            acc = acc + chips_ref[s]
        o_ref[rows, :] = acc
        swap = _remote(o_ref.at[rows, :], o_ref.at[rows, :], send_sems, recv_sems, N_CHIP, sib)
        swap.start()
        swap.wait()

    vm = pl.BlockSpec(memory_space=pltpu.VMEM)
    return pl.pallas_call(
        body, in_specs=[vm], out_specs=vm, out_shape=jax.ShapeDtypeStruct((r, LANE), F32),
        scratch_shapes=[pltpu.VMEM((r, LANE), F32), pltpu.VMEM((N_CHIP, rh, LANE), F32),
                        pltpu.SemaphoreType.DMA((N_CHIP + 1,)), pltpu.SemaphoreType.DMA((N_CHIP + 1,))],
        name=name, compiler_params=pltpu.CompilerParams(has_side_effects=True, vmem_limit_bytes=VMEM_LIMIT_V7X))(vec)


def _rs_pair_add(g, recv, kind, shape, pos, *, name):
    l, r, c = shape
    assert l == 1
    rh = r // 2
    if kind == "row":
        gspec = pl.BlockSpec((None, rh, c), lambda s, pos: (0, 2 * s + pos[0], 0))
    else:
        gspec = pl.BlockSpec((None, None, rh, c), lambda s, pos: (s, 0, pos[0], 0))
    pspec = pl.BlockSpec((None, None, rh, c), lambda s, pos: (s, 0, 0, 0))

    def body(pos_ref, g_ref, r_ref, own_ref, pw_ref):
        v = g_ref[...] + r_ref[...]
        pw_ref[...] = v.astype(pw_ref.dtype)

        @pl.when(pl.program_id(0) == pos_ref[1])
        def _():
            own_ref[...] = v

    return pl.pallas_call(
        body,
        grid_spec=pltpu.PrefetchScalarGridSpec(
            num_scalar_prefetch=1, grid=(N_CHIP,), in_specs=[gspec, pspec],
            out_specs=[pl.BlockSpec((None, rh, c), lambda s, pos: (0, 0, 0)), pspec]),
        out_shape=[jax.ShapeDtypeStruct((1, rh, c), F32), jax.ShapeDtypeStruct((N_CHIP, 1, rh, c), MXU)],
        name=name, compiler_params=_cp((ARB,)))(pos, g, recv)


def _rs_chip_add(p, recv, shape, pos, *, name):
    l, r, c = shape
    rh = r // 2

    def body(pos_ref, p_ref, r_ref, o_ref):
        del pos_ref
        acc = p_ref[...]
        for j in range(N_CHIP - 1):
            acc = acc + r_ref[j].astype(F32)
        o_ref[...] = acc

    return pl.pallas_call(
        body,
        grid_spec=pltpu.PrefetchScalarGridSpec(
            num_scalar_prefetch=1, grid=(l,),
            in_specs=[pl.BlockSpec((None, rh, c), lambda i, pos: (i, 0, 0)),
                      pl.BlockSpec((N_CHIP - 1, None, rh, c), lambda i, pos: (0, i, 0, 0))],
            out_specs=pl.BlockSpec((None, rh, c), lambda i, pos: (i, pos[0], 0))),
        out_shape=jax.ShapeDtypeStruct((l, r, c), F32),
        name=name, compiler_params=_cp((PAR,)))(pos, p, recv)


ADAM_BLOCK_ELEMS = 384 * 1024


def _adam_math(w, g, m, v):
    c1 = 1.0 / (1.0 - ADAM_B1 ** ADAM_STEP)
    c2 = 1.0 / (1.0 - ADAM_B2 ** ADAM_STEP)
    nm = ADAM_B1 * m + (1.0 - ADAM_B1) * g
    nv = ADAM_B2 * v + (1.0 - ADAM_B2) * (g * g)
    return -ADAM_LR * ((nm * c1) / (jnp.sqrt(nv * c2) + ADAM_EPS) + ADAM_WD * w), nm, nv


def _adamw_layer(w, g, m, v, outs, l, *, name):
    _, r, c = w.shape
    tr = _tile(r, max(SUBLANE, ADAM_BLOCK_ELEMS // c // SUBLANE * SUBLANE), SUBLANE)

    def body(w_ref, g_ref, m_ref, v_ref, *rest):
        go_ref, d_ref, nm_ref, nv_ref = rest[4:]
        gg = g_ref[...]
        go_ref[...] = gg
        d_ref[...], nm_ref[...], nv_ref[...] = _adam_math(w_ref[...], gg, m_ref[...], v_ref[...])

    lay = pl.BlockSpec((None, tr, c), lambda j: (l, j, 0))
    hbm = pl.BlockSpec(memory_space=pl.ANY)
    return pl.pallas_call(
        body, grid=(r // tr,),
        in_specs=[lay, pl.BlockSpec((None, tr, c), lambda j: (0, j, 0)), lay, lay] + [hbm] * 4,
        out_specs=[lay] * 4, out_shape=[jax.ShapeDtypeStruct(w.shape, F32)] * 4,
        input_output_aliases={4 + i: i for i in range(4)},
        name=name, compiler_params=_cp((PAR,)))(w, g, m, v, *outs)


def _adamw(w, g, m, v, *, name):
    shape = w.shape
    if w.ndim == 2:
        w, g, m, v = (a[None] for a in (w, g, m, v))
    l, r, c = w.shape
    tr = _tile(r, max(SUBLANE, ADAM_BLOCK_ELEMS // c // SUBLANE * SUBLANE), SUBLANE)

    def body(w_ref, g_ref, m_ref, v_ref, d_ref, nm_ref, nv_ref):
        d_ref[...], nm_ref[...], nv_ref[...] = _adam_math(w_ref[...], g_ref[...], m_ref[...], v_ref[...])

    spec = pl.BlockSpec((None, tr, c), lambda i, j: (i, j, 0))
    outs = pl.pallas_call(
        body, grid=(l, r // tr), in_specs=[spec] * 4, out_specs=[spec] * 3,
        out_shape=[jax.ShapeDtypeStruct((l, r, c), F32)] * 3,
        name=name, compiler_params=_cp((PAR, PAR)))(w, g, m, v)
    return tuple(o.reshape(shape) for o in outs)


PACK_ROWS = 512 * LANE


def _pack(arrays):
    flat = jnp.concatenate([a.reshape(-1).astype(F32) for a in arrays])
    pad = (-flat.shape[0]) % PACK_ROWS
    return jnp.pad(flat, (0, pad)).reshape(-1, LANE)


def _unpack(packed, shapes):
    flat = packed.reshape(-1)
    out, off = [], 0
    for s in shapes:
        size = int(np.prod(s))
        out.append(flat[off:off + size].reshape(s))
        off += size
    return out


BIG = (("w_mem_kv", "row"), ("w_mix_out", "row"), ("w_ffn_up", "slot"), ("w_ffn_down", "row"),
       ("w_in_a", "slot"), ("w_in_b", "row"), ("w_kv", "row"))
COLUMN_SHARDED_AS_COLUMNS = ("w_in_a",)
SMALL_SHARDED = (("w_ffn_conv", 2), ("w_conv_a", 2), ("b_conv_a", 1), ("lru_lambda", 1))
SMALL_REPLICATED = ("g_mix_pre", "g_mix_post", "g_ffn_pre", "g_ffn_post", "g_mem", "b_ffn_conv",
                    "w_rg_r", "b_rg_r", "w_rg_i", "b_rg_i", "sinks_b", "g_kv")
WEIGHTS = ("g_mix_pre", "g_mix_post", "g_ffn_pre", "g_ffn_post", "g_mem", "w_mem_kv", "w_mix_out", "w_ffn_up",
           "w_ffn_conv", "b_ffn_conv", "w_ffn_down", "w_in_a", "w_conv_a", "b_conv_a", "w_rg_r", "b_rg_r", "w_rg_i",
           "b_rg_i", "lru_lambda", "w_in_b", "sinks_b", "g_kv", "w_kv")


def _slot_to_cols(a):
    s, l, r, c = a.shape
    return a.transpose(1, 2, 0, 3).reshape(l, r, s * c)


def _cols_to_slot(a):
    l, r, c4 = a.shape
    return a.reshape(l, r, N_CHIP, c4 // N_CHIP).transpose(2, 0, 1, 3)


GROUPS_PER_LAYER = 8


def _layer_weights(layer):
    names = [("w_mem_kv", layer), ("w_in_a", layer) if layer < N_A else ("w_in_b", layer - N_A)]
    if layer == N_A:
        names.append(("w_kv", 0))
    return names + [("w_mix_out", layer), ("w_ffn_up", layer), ("w_ffn_down", layer)]


def _train_step(x, mem, target, w, m, v):
    xi, yi, ci = _mesh_pos()
    chip = 2 * xi + yi
    pos = jnp.stack([ci, chip]).astype(jnp.int32)

    q = _CommQueue()
    kinds = dict(BIG)
    as3 = lambda a: a if a.ndim == 3 else a[None]
    w3, m3, v3 = ({k: as3(d[k]) for k, _ in BIG} for d in (w, m, v))
    shards = {k: w3[k].astype(MXU) for k, _ in BIG}

    gathered = {}

    def on_gathered(k, l, full):
        gathered[k, l] = _slot_to_cols(full) if k in COLUMN_SHARDED_AS_COLUMNS else full

    group_of = {}

    for layer in range(DEPTH):
        for i, (k, l) in enumerate(_layer_weights(layer)):
            group_of[k, l] = layer * GROUPS_PER_LAYER + i
            _gather_chunks(q, group_of[k, l], kinds[k], shards[k], l, functools.partial(on_gathered, k, l))

    def wfull(k, l):
        if (k, l) not in gathered:
            q.flush(group_of[k, l])
        return gathered[k, l]

    q.flush(1)

    big_out = {k: [lax.empty(w3[k].shape, F32) for _ in range(4)] for k, _ in BIG}

    def on_reduced(k, l, g):
        big_out[k] = _adamw_layer(w3[k], g, m3[k], v3[k], big_out[k], l, name=f"adamw_{k}{l}")

    def push_grad(k, l, g):
        if k in COLUMN_SHARDED_AS_COLUMNS:
            g = _cols_to_slot(g)
        _reduce_scatter_chunks(q, kinds[k], g, (1,) + w3[k].shape[1:], pos, f"{k}{l}", functools.partial(on_reduced, k, l))

    small_shapes = [w[k].shape for k, _ in SMALL_SHARDED]
    stacked = _allgather8(_pack([w[k] for k, _ in SMALL_SHARDED]), name="gather_small")
    per_chip = [_unpack(stacked[2 * s], small_shapes) for s in range(N_CHIP)]
    p = {k: w[k] for k in SMALL_REPLICATED}
    for i, (k, axis) in enumerate(SMALL_SHARDED):
        p[k] = jnp.concatenate([per_chip[s][i] for s in range(N_CHIP)], axis=axis)

    sq, grad_x, g = _local_step(x, mem, target, p, wfull, push_grad, q)
    loss = lax.psum(0.5 * sq / D_MODEL, ("x", "y", "c"))
    q.flush()

    small_names = [k for k, _ in SMALL_SHARDED] + list(SMALL_REPLICATED)
    summed = _allreduce8(_pack([g[k] for k in small_names]), name="allreduce_small")
    gsum = dict(zip(small_names, _unpack(summed, [p[k].shape for k in small_names])))
    for k, axis in SMALL_SHARDED:
        gsum[k] = lax.dynamic_slice_in_dim(gsum[k], chip * w[k].shape[axis], w[k].shape[axis], axis)

    delta, new_m, new_v = {}, {}, {}
    for k, _ in BIG:
        gsum[k], delta[k], new_m[k], new_v[k] = (o.reshape(w[k].shape) for o in big_out[k])
    for k in small_names:
        as2 = lambda a: a.reshape(-1, a.shape[-1])
        outs = _adamw(as2(w[k]), as2(gsum[k]), as2(m[k]), as2(v[k]), name=f"adamw_{k}")
        delta[k], new_m[k], new_v[k] = (o.reshape(w[k].shape) for o in outs)
    return (loss, grad_x, *[gsum[k] for k in WEIGHTS], *[delta[k] for k in WEIGHTS],
            *[new_m[k] for k in WEIGHTS], *[new_v[k] for k in WEIGHTS])


def kernel(x, mem, g_mix_pre, g_mix_post, g_ffn_pre, g_ffn_post, g_mem, w_mem_kv, w_mix_out, w_ffn_up, w_ffn_conv, b_ffn_conv, w_ffn_down, w_in_a, w_conv_a, b_conv_a, w_rg_r, b_rg_r, w_rg_i, b_rg_i, lru_lambda, w_in_b, sinks_b, g_kv, w_kv, loss_target, m_g_mix_pre, m_g_mix_post, m_g_ffn_pre, m_g_ffn_post, m_g_mem, m_w_mem_kv, m_w_mix_out, m_w_ffn_up, m_w_ffn_conv, m_b_ffn_conv, m_w_ffn_down, m_w_in_a, m_w_conv_a, m_b_conv_a, m_w_rg_r, m_b_rg_r, m_w_rg_i, m_b_rg_i, m_lru_lambda, m_w_in_b, m_sinks_b, m_g_kv, m_w_kv, v_g_mix_pre, v_g_mix_post, v_g_ffn_pre, v_g_ffn_post, v_g_mem, v_w_mem_kv, v_w_mix_out, v_w_ffn_up, v_w_ffn_conv, v_b_ffn_conv, v_w_ffn_down, v_w_in_a, v_w_conv_a, v_b_conv_a, v_w_rg_r, v_b_rg_r, v_w_rg_i, v_b_rg_i, v_lru_lambda, v_w_in_b, v_sinks_b, v_g_kv, v_w_kv):
    args = (g_mix_pre, g_mix_post, g_ffn_pre, g_ffn_post, g_mem, w_mem_kv, w_mix_out, w_ffn_up, w_ffn_conv, b_ffn_conv, w_ffn_down, w_in_a, w_conv_a, b_conv_a, w_rg_r, b_rg_r, w_rg_i, b_rg_i, lru_lambda, w_in_b, sinks_b, g_kv, w_kv)
    ms = (m_g_mix_pre, m_g_mix_post, m_g_ffn_pre, m_g_ffn_post, m_g_mem, m_w_mem_kv, m_w_mix_out, m_w_ffn_up, m_w_ffn_conv, m_b_ffn_conv, m_w_ffn_down, m_w_in_a, m_w_conv_a, m_b_conv_a, m_w_rg_r, m_b_rg_r, m_w_rg_i, m_b_rg_i, m_lru_lambda, m_w_in_b, m_sinks_b, m_g_kv, m_w_kv)
    vs = (v_g_mix_pre, v_g_mix_post, v_g_ffn_pre, v_g_ffn_post, v_g_mem, v_w_mem_kv, v_w_mix_out, v_w_ffn_up, v_w_ffn_conv, v_b_ffn_conv, v_w_ffn_down, v_w_in_a, v_w_conv_a, v_b_conv_a, v_w_rg_r, v_b_rg_r, v_w_rg_i, v_b_rg_i, v_lru_lambda, v_w_in_b, v_sinks_b, v_g_kv, v_w_kv)
    return _train_step(x, mem, loss_target, dict(zip(WEIGHTS, args)), dict(zip(WEIGHTS, ms)), dict(zip(WEIGHTS, vs)))
```

```python
import functools
import math

import numpy as np
import jax
import jax.numpy as jnp
from jax import lax
from jax.experimental import pallas as pl
from jax.experimental.pallas import tpu as pltpu

F32 = jnp.float32
MXU = jnp.bfloat16

D_MODEL = 1024
HEAD_DIM = 64
MEM_LEN = 256
MEM_HEADS = 4
MEM_WIDTH = MEM_HEADS * HEAD_DIM
MIX_WIDTH = D_MODEL - MEM_WIDTH
LRU_BLOCKS = MIX_WIDTH // HEAD_DIM
LRU_CONV = 4
LRU_C = 8.0
SWA_HEADS = MIX_WIDTH // HEAD_DIM
SWA_KV_HEADS = 4
SWA_GROUP = SWA_HEADS // SWA_KV_HEADS
WINDOW = 128
D_FF = 2816
FFN_CONV = 3
EPS = 1e-6
DEPTH = 4
N_A = 2

ADAM_LR = 0.001
ADAM_B1 = 0.9
ADAM_B2 = 0.999
ADAM_EPS = 1e-08
ADAM_WD = 0.01
ADAM_STEP = 10

VMEM_LIMIT_V7X = 56 * 1024 * 1024
LANE = 128
SUBLANE = 8
GATE_TILE = 256
MESH_T = pl.DeviceIdType.MESH


def _alibi_slopes(n):
    def pow2_slopes(m):
        start = 2.0 ** (-8.0 / m)
        return [start ** (i + 1) for i in range(m)]
    c = 2 ** int(math.floor(math.log2(n)))
    s = pow2_slopes(c)
    if c != n:
        s = s + pow2_slopes(2 * c)[0::2][: n - c]
    return [float(np.float32(v)) for v in s]


SLOPES = _alibi_slopes(SWA_HEADS)


def _tile(n, cap, mult=LANE):
    best = None
    for t in range(mult, min(n, cap) + 1, mult):
        if n % t == 0:
            best = t
    return best if best is not None else n


def _cp(sem):
    return pltpu.CompilerParams(dimension_semantics=sem, vmem_limit_bytes=VMEM_LIMIT_V7X)


MM_VMEM_BUDGET = 40 * 1024 * 1024
HBM_BYTES_PER_US_V7X = 3.0e6
GRID_STEP_US = 0.35


def _divisors(n, mult):
    return [t for t in range(mult, n + 1, mult) if n % t == 0] or [n]


def _mm_tiles(m, k, n, out_bytes):
    best = None
    for tm in _divisors(m, 256):
        for tn in _divisors(n, LANE):
            vmem = 2 * (tm * k * 2 + k * tn * 2 + tm * tn * out_bytes)
            if vmem > MM_VMEM_BUDGET:
                continue
            steps = (m // tm) * (n // tn)
            b_reads = 1 if tn == n else m // tm
            traffic = m * k * 2 + k * n * 2 * b_reads + m * n * out_bytes
            first = tm * k * 2 + k * tn * 2
            cost = (traffic + first) / HBM_BYTES_PER_US_V7X + steps * GRID_STEP_US
            if best is None or cost < best[0]:
                best = (cost, tm, tn)
    return best[1], best[2]


def _mm_tn_tiles(k, m, n, whole_n=False):
    best = None
    for tm in _divisors(m, LANE):
        for tn in ([n] if whole_n else _divisors(n, LANE)):
            for tk in _divisors(k, 512):
                vmem = 2 * (tk * tm * 2 + tk * tn * 2 + tm * tn * 4)
                if vmem > MM_VMEM_BUDGET:
                    continue
                steps = (m // tm) * (n // tn) * (k // tk)
                traffic = k * m * 2 * (n // tn) + k * n * 2 * (m // tm) + m * n * 4
                cost = traffic / HBM_BYTES_PER_US_V7X + steps * GRID_STEP_US
                if best is None or cost < best[0]:
                    best = (cost, tk, tm, tn)
    return best[1], best[2], best[3]


ARB = "arbitrary"
PAR = "parallel"


def _rms_fwd(x, g):
    r = lax.rsqrt(jnp.mean(x * x, axis=-1, keepdims=True) + EPS)
    return x * r * g


def _rms_bwd(dy, x, g):
    r = lax.rsqrt(jnp.mean(x * x, axis=-1, keepdims=True) + EPS)
    xh = x * r
    gdy = dy * g
    dx = r * (gdy - xh * jnp.mean(gdy * xh, axis=-1, keepdims=True))
    dg = jnp.sum(dy * xh, axis=0, keepdims=True)
    return dx, dg


_GELU_K = math.sqrt(2.0 / math.pi)
_GELU_C = 0.044715


def _gelu(x):
    t = jnp.tanh(_GELU_K * (x + _GELU_C * x * x * x))
    return 0.5 * x * (1.0 + t)


def _gelu_and_grad(x):
    x2 = x * x
    u = 0.5 * jnp.tanh(x * (_GELU_K + (_GELU_K * _GELU_C) * x2)) + 0.5
    dz2 = (6.0 * _GELU_K * _GELU_C) * x2 + 2.0 * _GELU_K
    return x * u, u * ((x * (1.0 - u)) * dz2 + 1.0)


def _shift_down(x, k, row):
    return jnp.where(row >= k, pltpu.roll(x, k, axis=0), 0.0)


def _shift_up(x, k, row):
    n = x.shape[0]
    return jnp.where(row < n - k, pltpu.roll(x, n - k, axis=0), 0.0)


def _shift_down_edge(x, k):
    r = pltpu.roll(x, k, axis=0)
    row = lax.broadcasted_iota(jnp.int32, (SUBLANE, x.shape[1]), 0)
    return jnp.concatenate([jnp.where(row >= k, r[:SUBLANE], 0.0), r[SUBLANE:]], axis=0)


def _shift_up_edge(x, k):
    n = x.shape[0]
    r = pltpu.roll(x, n - k, axis=0)
    row = lax.broadcasted_iota(jnp.int32, (SUBLANE, x.shape[1]), 0)
    return jnp.concatenate([r[:n - SUBLANE], jnp.where(row < SUBLANE - k, r[n - SUBLANE:], 0.0)], axis=0)


def _dot(a, b):
    return jnp.dot(a, b, preferred_element_type=F32)


def _dot_nt(a, b):
    return lax.dot_general(a, b, (((1,), (1,)), ((), ())), preferred_element_type=F32)


def _dot_tn(a, b):
    return lax.dot_general(a, b, (((0,), (0,)), ((), ())), preferred_element_type=F32)


MXU_FLOPS_PER_US = 7.0e8
HOST_US = {"lru_fwd": 44.0, "lru_bwd": 94.0, "swa_fwd": 60.0, "swa_bwd": 160.0, "mem_attn_fwd": 21.0,
           "mem_attn_bwd": 33.0, "ffn_act": 70.0, "ffn_act_bwd": 100.0, "resid": 22.0, "resid_bwd": 33.0}


def _hosted_call(body, *, grid, in_specs, out_specs, out_shape, args, name, aliases=None, scratch_shapes=(),
                 q=None, flops=0.0, budget_us=0.0):
    chunks = q.take(flops / MXU_FLOPS_PER_US + budget_us) if q is not None else []
    if not chunks:
        return pl.pallas_call(
            body, grid=grid, in_specs=in_specs, out_specs=out_specs, out_shape=out_shape,
            scratch_shapes=list(scratch_shapes), input_output_aliases=aliases or {}, name=name,
            compiler_params=_cp((ARB,) * len(grid)))(*args)
    single = not isinstance(out_shape, (list, tuple))
    o_shapes = [out_shape] if single else list(out_shape)
    o_specs = [out_specs] if single else list(out_specs)
    n_in, n_out, n_scr = len(args), len(o_shapes), len(scratch_shapes)
    c_ins = [a for ch in chunks for a in ch.ins]
    c_outs = [s for ch in chunks for s in ch.out_shapes]
    alias = dict(aliases or {})
    in_off, out_off, sem_off = [], [], []
    i0 = o0 = s0 = 0
    for ch in chunks:
        in_off.append(i0)
        out_off.append(o0)
        sem_off.append(s0)
        for ci, co in ch.alias.items():
            alias[n_in + i0 + ci] = n_out + o0 + co
        i0 += len(ch.ins)
        o0 += len(ch.out_shapes)
        s0 += ch.n_sem

    def wrapped(*refs):
        ins = refs[:n_in]
        cin = refs[n_in:n_in + i0]
        outs = refs[n_in + i0:n_in + i0 + n_out]
        cout = refs[n_in + i0 + n_out:n_in + i0 + n_out + o0]
        scr = refs[n_in + i0 + n_out + o0:n_in + i0 + n_out + o0 + n_scr]
        send_sems, recv_sems = refs[n_in + i0 + n_out + o0 + n_scr:]
        first = functools.reduce(lambda u, v: u & v, [pl.program_id(d) == 0 for d in range(len(grid))])
        last = functools.reduce(lambda u, v: u & v, [pl.program_id(d) == grid[d] - 1 for d in range(len(grid))])

        def each(phase):
            for ch, a, b, s in zip(chunks, in_off, out_off, sem_off):
                getattr(ch, phase)(cin[a:a + len(ch.ins)], cout[b:b + len(ch.out_shapes)], send_sems, recv_sems, s)

        pl.when(first)(lambda: each("start"))
        body(*ins, *outs, *scr)
        pl.when(last)(lambda: each("finish"))

    hbm = pl.BlockSpec(memory_space=pl.ANY)
    res = pl.pallas_call(
        wrapped, grid=grid, in_specs=list(in_specs) + [hbm] * i0, out_specs=o_specs + [hbm] * o0,
        out_shape=o_shapes + c_outs,
        scratch_shapes=list(scratch_shapes) + [pltpu.SemaphoreType.DMA((s0,)), pltpu.SemaphoreType.DMA((s0,))],
        input_output_aliases=alias, name=name,
        compiler_params=pltpu.CompilerParams(dimension_semantics=(ARB,) * len(grid), vmem_limit_bytes=VMEM_LIMIT_V7X,
                                             has_side_effects=True))(*args, *c_ins)
    for ch, b in zip(chunks, out_off):
        ch.done(list(res[n_out + b:n_out + b + len(ch.out_shapes)]))
    return res[0] if single else list(res[:n_out])


def _mm_nn(a, b, *, name, q=None, out_dtype=F32):
    m, k = a.shape
    n = b.shape[-1]
    tm, tn = _mm_tiles(m, k, n, jnp.dtype(out_dtype).itemsize)

    def body(a_ref, b_ref, o_ref):
        o_ref[...] = _dot(a_ref[...], b_ref[...]).astype(o_ref.dtype)

    return _hosted_call(
        body, grid=(m // tm, n // tn),
        in_specs=[pl.BlockSpec((tm, k), lambda i, j: (i, 0)),
                  pl.BlockSpec((None, k, tn), lambda i, j: (0, 0, j))],
        out_specs=pl.BlockSpec((tm, tn), lambda i, j: (i, j)),
        out_shape=jax.ShapeDtypeStruct((m, n), out_dtype),
        args=(a, b), name=name, q=q, flops=2.0 * m * k * n)


def _mm_nt(a, b, *, name, q=None, out_dtype=F32):
    m, k = a.shape
    n = b.shape[-2]
    tm, tn = _mm_tiles(m, k, n, jnp.dtype(out_dtype).itemsize)

    def body(a_ref, b_ref, o_ref):
        o_ref[...] = _dot_nt(a_ref[...], b_ref[...]).astype(o_ref.dtype)

    return _hosted_call(
        body, grid=(m // tm, n // tn),
        in_specs=[pl.BlockSpec((tm, k), lambda i, j: (i, 0)),
                  pl.BlockSpec((None, tn, k), lambda i, j: (0, j, 0))],
        out_specs=pl.BlockSpec((tm, tn), lambda i, j: (i, j)),
        out_shape=jax.ShapeDtypeStruct((m, n), out_dtype),
        args=(a, b), name=name, q=q, flops=2.0 * m * k * n)


def _mm_nn_slots(a, b4, *, name, q=None, out_dtype=F32):
    m, k = a.shape
    s_, _, _, c = b4.shape
    ob = jnp.dtype(out_dtype).itemsize
    tm = max(t for t in _divisors(m, 256) if 2 * (t * k * 2 + k * c * 2 + t * c * ob) <= MM_VMEM_BUDGET)

    def body(a_ref, b_ref, o_ref):
        o_ref[...] = _dot(a_ref[...], b_ref[...]).astype(o_ref.dtype)

    return _hosted_call(
        body, grid=(m // tm, s_),
        in_specs=[pl.BlockSpec((tm, k), lambda i, j: (i, 0)),
                  pl.BlockSpec((None, None, k, c), lambda i, j: (j, 0, 0, 0))],
        out_specs=pl.BlockSpec((tm, c), lambda i, j: (i, j)),
        out_shape=jax.ShapeDtypeStruct((m, s_ * c), out_dtype),
        args=(a, b4), name=name, q=q, flops=2.0 * m * k * s_ * c)


def _mm_tn_slots(a, b, *, name, slot_cols, n_slots, first_slot=0, q=None, out=None):
    k, m = a.shape
    c = slot_cols
    tk, tm, _ = _mm_tn_tiles(k, m, c, whole_n=True)

    def body(a_ref, b_ref, *rest):
        o_ref = rest[-1]
        part = _dot_tn(a_ref[...], b_ref[...])

        @pl.when(pl.program_id(2) == 0)
        def _():
            o_ref[...] = part

        @pl.when(pl.program_id(2) > 0)
        def _():
            o_ref[...] += part

    in_specs = [pl.BlockSpec((tk, tm), lambda i, j, s: (s, i)), pl.BlockSpec((tk, c), lambda i, j, s: (s, j))]
    args = (a, b)
    if out is not None:
        in_specs.append(pl.BlockSpec(memory_space=pl.ANY))
        args = (a, b, out)
    return _hosted_call(
        body, grid=(m // tm, b.shape[-1] // c, k // tk), in_specs=in_specs,
        out_specs=pl.BlockSpec((None, None, tm, c), lambda i, j, s: (first_slot + j, 0, i, 0)),
        out_shape=jax.ShapeDtypeStruct((n_slots, 1, m, c), F32),
        aliases={2: 0} if out is not None else None,
        args=args, name=name, q=q, flops=2.0 * m * k * b.shape[-1])


def _mm_tn(a, b, *, name, q=None, out=None, n_total=None, col_block_offset=0):
    k, m = a.shape
    n = b.shape[-1]
    tk, tm, tn = _mm_tn_tiles(k, m, n)
    off = col_block_offset * (n // tn)

    def body(a_ref, b_ref, *rest):
        o_ref = rest[-1]
        part = _dot_tn(a_ref[...], b_ref[...])

        @pl.when(pl.program_id(2) == 0)
        def _():
            o_ref[...] = part

        @pl.when(pl.program_id(2) > 0)
        def _():
            o_ref[...] += part

    in_specs = [pl.BlockSpec((tk, tm), lambda i, j, s: (s, i)), pl.BlockSpec((tk, tn), lambda i, j, s: (s, j))]
    args = (a, b)
    if out is not None:
        in_specs.append(pl.BlockSpec(memory_space=pl.ANY))
        args = (a, b, out)
    return _hosted_call(
        body, grid=(m // tm, n // tn, k // tk), in_specs=in_specs,
        out_specs=pl.BlockSpec((None, tm, tn), lambda i, j, s: (0, i, j + off)),
        out_shape=jax.ShapeDtypeStruct((1, m, n_total or n), F32),
        aliases={2: 0} if out is not None else None,
        args=args, name=name, q=q, flops=2.0 * m * k * n)


def _mm_ffn_dh(dg, dv, w4, *, name, q=None):
    m, f = dg.shape
    n_slots, _, d, c = w4.shape
    tm, tn = _mm_tiles(m, 2 * f, d, 4)

    def body(dg_ref, dv_ref, *rest):
        w_refs, o_ref = rest[:n_slots], rest[n_slots]
        acc = None
        for s, w_ref in enumerate(w_refs):
            x_ref = dg_ref if s < n_slots // 2 else dv_ref
            off = (s % (n_slots // 2)) * c
            part = _dot_nt(x_ref[:, off:off + c], w_ref[...])
            acc = part if acc is None else acc + part
        o_ref[...] = acc.astype(o_ref.dtype)

    wspec = lambda s: pl.BlockSpec((None, None, tn, c), lambda i, j: (s, 0, j, 0))
    return _hosted_call(
        body, grid=(m // tm, d // tn),
        in_specs=[pl.BlockSpec((tm, f), lambda i, j: (i, 0)),
                  pl.BlockSpec((tm, f), lambda i, j: (i, 0))] + [wspec(s) for s in range(n_slots)],
        out_specs=pl.BlockSpec((tm, tn), lambda i, j: (i, j)),
        out_shape=jax.ShapeDtypeStruct((m, d), MXU),
        args=(dg, dv) + (w4,) * n_slots, name=name, q=q, flops=4.0 * m * f * d)


ROW_TILE = 512


def _norm_fwd(x, g, *, name):
    n, d = x.shape
    tm = _tile(n, ROW_TILE, SUBLANE)

    def body(x_ref, g_ref, o_ref):
        o_ref[...] = _rms_fwd(x_ref[...], g_ref[...]).astype(o_ref.dtype)

    return pl.pallas_call(
        body, grid=(n // tm,),
        in_specs=[pl.BlockSpec((tm, d), lambda i: (i, 0)), pl.BlockSpec((1, d), lambda i: (0, 0))],
        out_specs=pl.BlockSpec((tm, d), lambda i: (i, 0)),
        out_shape=jax.ShapeDtypeStruct((n, d), MXU),
        name=name, compiler_params=_cp((PAR,)))(x, g)


def _norm_bwd_dg(dy, x, g, *, name):
    n, d = x.shape
    tm = _tile(n, ROW_TILE, SUBLANE)

    def body(dy_ref, x_ref, g_ref, dg_ref):
        @pl.when(pl.program_id(0) == 0)
        def _():
            dg_ref[...] = jnp.zeros_like(dg_ref)
        _, dg = _rms_bwd(dy_ref[...], x_ref[...], g_ref[...])
        dg_ref[...] += dg

    return pl.pallas_call(
        body, grid=(n // tm,),
        in_specs=[pl.BlockSpec((tm, d), lambda i: (i, 0)), pl.BlockSpec((tm, d), lambda i: (i, 0)),
                  pl.BlockSpec((1, d), lambda i: (0, 0))],
        out_specs=pl.BlockSpec((1, d), lambda i: (0, 0)),
        out_shape=jax.ShapeDtypeStruct((1, d), F32),
        name=name, compiler_params=_cp((ARB,)))(dy, x, g)


def _resid_norm_fwd(x, y, g_post, g_pres, *, name, q=None):
    n, d = x.shape
    tm = _tile(n, ROW_TILE, SUBLANE)
    nh = len(g_pres)

    def body(x_ref, y_ref, gp_ref, *rest):
        gpre = rest[:nh]
        xo_ref = rest[nh]
        h_refs = rest[nh + 1:]
        xo = x_ref[...] + _rms_fwd(y_ref[...].astype(F32), gp_ref[...])
        xo_ref[...] = xo
        for g_ref, h_ref in zip(gpre, h_refs):
            h_ref[...] = _rms_fwd(xo, g_ref[...]).astype(h_ref.dtype)

    row = pl.BlockSpec((tm, d), lambda i: (i, 0))
    vec = pl.BlockSpec((1, d), lambda i: (0, 0))
    outs = _hosted_call(
        body, grid=(n // tm,),
        in_specs=[row, row, vec] + [vec] * nh,
        out_specs=[row] + [row] * nh,
        out_shape=[jax.ShapeDtypeStruct((n, d), F32)] + [jax.ShapeDtypeStruct((n, d), MXU)] * nh,
        args=(x, y, g_post, *g_pres), name=name, q=q, budget_us=HOST_US["resid"])
    return outs[0], list(outs[1:])


def _loss_fwd(x, y, g_post, target, *, name):
    n, d = x.shape
    tm = _tile(n, ROW_TILE, SUBLANE)

    def body(x_ref, y_ref, gp_ref, t_ref, dx_ref, sq_ref):
        @pl.when(pl.program_id(0) == 0)
        def _():
            sq_ref[...] = jnp.zeros_like(sq_ref)
        err = x_ref[...] + _rms_fwd(y_ref[...].astype(F32), gp_ref[...]) - t_ref[...]
        dx_ref[...] = err * (1.0 / d)
        sq_ref[...] += jnp.sum(err * err, axis=0, keepdims=True)

    row = pl.BlockSpec((tm, d), lambda i: (i, 0))
    vec = pl.BlockSpec((1, d), lambda i: (0, 0))
    return pl.pallas_call(
        body, grid=(n // tm,),
        in_specs=[row, row, vec, row],
        out_specs=[row, vec],
        out_shape=[jax.ShapeDtypeStruct((n, d), F32), jax.ShapeDtypeStruct((1, d), F32)],
        name=name, compiler_params=_cp((ARB,)))(x, y, g_post, target)


def _resid_norm_bwd(dx_out, dhs, x_out, g_pres, y, g_post, *, name, q=None):
    n, d = dx_out.shape
    tm = _tile(n, ROW_TILE, SUBLANE)
    nh = len(dhs)
    has_y = y is not None

    def body(*refs):
        it = iter(refs)
        dxo_ref = next(it)
        dh_refs = [next(it) for _ in range(nh)]
        xo_ref = next(it) if nh else None
        gpre_refs = [next(it) for _ in range(nh)]
        y_ref = next(it) if has_y else None
        gpost_ref = next(it) if has_y else None
        g_out = next(it)
        dy_out = next(it) if has_y else None
        dgpre_out = [next(it) for _ in range(nh)]
        dgpost_out = next(it) if has_y else None

        @pl.when(pl.program_id(0) == 0)
        def _():
            for r in dgpre_out:
                r[...] = jnp.zeros_like(r)
            if has_y:
                dgpost_out[...] = jnp.zeros_like(dgpost_out)

        g = dxo_ref[...]
        if nh:
            xo = xo_ref[...]
            for dh_ref, gp_ref, dg_ref in zip(dh_refs, gpre_refs, dgpre_out):
                dx, dg = _rms_bwd(dh_ref[...].astype(F32), xo, gp_ref[...])
                g = g + dx
                dg_ref[...] += dg
        g_out[...] = g
        if has_y:
            dy, dg = _rms_bwd(g, y_ref[...].astype(F32), gpost_ref[...])
            dy_out[...] = dy.astype(dy_out.dtype)
            dgpost_out[...] += dg

    row = pl.BlockSpec((tm, d), lambda i: (i, 0))
    vec = pl.BlockSpec((1, d), lambda i: (0, 0))
    ins, in_specs = [dx_out], [row]
    ins += list(dhs)
    in_specs += [row] * nh
    if nh:
        ins.append(x_out)
        in_specs.append(row)
    ins += list(g_pres)
    in_specs += [vec] * nh
    if has_y:
        ins += [y, g_post]
        in_specs += [row, vec]
    out_specs, out_shape = [row], [jax.ShapeDtypeStruct((n, d), F32)]
    if has_y:
        out_specs.append(row)
        out_shape.append(jax.ShapeDtypeStruct((n, d), MXU))
    out_specs += [vec] * nh
    out_shape += [jax.ShapeDtypeStruct((1, d), F32)] * nh
    if has_y:
        out_specs.append(vec)
        out_shape.append(jax.ShapeDtypeStruct((1, d), F32))
    outs = list(_hosted_call(
        body, grid=(n // tm,), in_specs=in_specs, out_specs=out_specs, out_shape=out_shape,
        args=tuple(ins), name=name, q=q, budget_us=HOST_US["resid_bwd"]))
    g = outs.pop(0)
    dy = outs.pop(0) if has_y else None
    dgpre = [outs.pop(0) for _ in range(nh)]
    dgpost = outs.pop(0) if has_y else None
    return g, dy, dgpre, dgpost


def _ffn_conv(up, w_ref, b_ref):
    return (w_ref[0:1, :] * _shift_down_edge(up, 2) + w_ref[1:2, :] * _shift_down_edge(up, 1)
            + w_ref[2:3, :] * up + b_ref[...])


def _ffn_act_fwd(up, wconv, bconv, bsz, *, name, q=None):
    n, f2 = up.shape
    f = f2 // 2
    t = n // bsz
    tc = _tile(f, 256)
    nf = f // tc

    def body(ug_ref, uv_ref, wg_ref, wv_ref, bg_ref, bv_ref, o_ref, dag_ref, dav_ref):
        g = _ffn_conv(ug_ref[...].astype(F32), wg_ref, bg_ref)
        v = _ffn_conv(uv_ref[...].astype(F32), wv_ref, bv_ref)
        gl, dgl = _gelu_and_grad(g)
        dag_ref[...] = (v * dgl).astype(dag_ref.dtype)
        dav_ref[...] = gl.astype(dav_ref.dtype)
        o_ref[...] = (gl * v).astype(o_ref.dtype)

    blk = pl.BlockSpec((t, tc), lambda b, j: (b, j))
    return _hosted_call(
        body, grid=(bsz, nf),
        in_specs=[blk, pl.BlockSpec((t, tc), lambda b, j: (b, j + nf)),
                  pl.BlockSpec((FFN_CONV, tc), lambda b, j: (0, j)),
                  pl.BlockSpec((FFN_CONV, tc), lambda b, j: (0, j + nf)),
                  pl.BlockSpec((1, tc), lambda b, j: (0, j)),
                  pl.BlockSpec((1, tc), lambda b, j: (0, j + nf))],
        out_specs=[blk, blk, blk],
        out_shape=[jax.ShapeDtypeStruct((n, f), MXU)] * 3,
        args=(up, up, wconv, wconv, bconv, bconv), name=name, q=q, budget_us=HOST_US["ffn_act"])


def _ffn_act_bwd(up, ug, uv, dact, wconv, bsz, *, name, q=None):
    n, f2 = up.shape
    f = f2 // 2
    t = n // bsz
    tc = _tile(f, 256)
    nf = f // tc

    def body(xg_ref, xv_ref, g_ref, v_ref, da_ref, wg_ref, wv_ref,
             dug_ref, duv_ref, dwg_ref, dwv_ref, dbg_ref, dbv_ref):
        @pl.when(pl.program_id(1) == 0)
        def _():
            for r in (dwg_ref, dwv_ref, dbg_ref, dbv_ref):
                r[...] = jnp.zeros_like(r)

        da = da_ref[...].astype(F32)
        dg = da * g_ref[...].astype(F32)
        dv = da * v_ref[...].astype(F32)

        def conv_bwd(du, w_ref, x_ref, dx_ref, dw_ref, db_ref):
            du1, du2 = _shift_up_edge(du, 1), _shift_up_edge(du, 2)
            dx_ref[...] = (w_ref[2:3, :] * du + w_ref[1:2, :] * du1 + w_ref[0:1, :] * du2).astype(dx_ref.dtype)
            x = x_ref[...].astype(F32)
            dw_ref[0:1, :] += jnp.sum(x * du2, axis=0, keepdims=True)
            dw_ref[1:2, :] += jnp.sum(x * du1, axis=0, keepdims=True)
            dw_ref[2:3, :] += jnp.sum(x * du, axis=0, keepdims=True)
            db_ref[...] += jnp.sum(du, axis=0, keepdims=True)

        conv_bwd(dg, wg_ref, xg_ref, dug_ref, dwg_ref, dbg_ref)
        conv_bwd(dv, wv_ref, xv_ref, duv_ref, dwv_ref, dbv_ref)

    blk = pl.BlockSpec((t, tc), lambda j, b: (b, j))
    wspec = pl.BlockSpec((FFN_CONV, tc), lambda j, b: (0, j))
    bspec = pl.BlockSpec((1, tc), lambda j, b: (0, j))
    outs = _hosted_call(
        body, grid=(nf, bsz),
        in_specs=[blk, pl.BlockSpec((t, tc), lambda j, b: (b, j + nf)), blk, blk, blk,
                  wspec, pl.BlockSpec((FFN_CONV, tc), lambda j, b: (0, j + nf))],
        out_specs=[blk, blk, wspec, wspec, bspec, bspec],
        out_shape=[jax.ShapeDtypeStruct((n, f), MXU), jax.ShapeDtypeStruct((n, f), MXU),
                   jax.ShapeDtypeStruct((FFN_CONV, f), F32), jax.ShapeDtypeStruct((FFN_CONV, f), F32),
                   jax.ShapeDtypeStruct((1, f), F32), jax.ShapeDtypeStruct((1, f), F32)],
        args=(up, up, ug, uv, dact, wconv, wconv), name=name, q=q, budget_us=HOST_US["ffn_act_bwd"])
    dug, duv, dwg, dwv, dbg, dbv = outs
    return dug, duv, jnp.concatenate([dwg, dwv], axis=1), jnp.concatenate([dbg, dbv], axis=1)


def _mem_attn_fwd(proj, q_col_block, mkv, ycat, bsz, *, name, q=None):
    n = proj.shape[0]
    t = n // bsz
    tq = _tile(t, 512, SUBLANE)
    nt = t // tq
    scale = HEAD_DIM ** -0.5

    def body(q_ref, kv_ref, old_ref, o_ref):
        del old_ref
        outs = []
        for h in range(MEM_HEADS):
            sl = slice(h * HEAD_DIM, (h + 1) * HEAD_DIM)
            q = q_ref[:, sl].astype(MXU)
            k = kv_ref[:, sl].astype(MXU)
            v = kv_ref[:, MEM_WIDTH + h * HEAD_DIM: MEM_WIDTH + (h + 1) * HEAD_DIM].astype(MXU)
            s = _dot_nt(q, k) * scale
            m = jnp.max(s, axis=-1, keepdims=True)
            p = jnp.exp(s - m)
            p = p / jnp.sum(p, axis=-1, keepdims=True)
            outs.append(_dot(p.astype(MXU), v))
        o_ref[...] = jnp.concatenate(outs, axis=-1).astype(o_ref.dtype)

    return _hosted_call(
        body, grid=(bsz, nt),
        in_specs=[pl.BlockSpec((tq, MEM_WIDTH), lambda b, i: (b * nt + i, q_col_block)),
                  pl.BlockSpec((MEM_LEN, 2 * MEM_WIDTH), lambda b, i: (b, 0)),
                  pl.BlockSpec(memory_space=pl.ANY)],
        out_specs=pl.BlockSpec((tq, MEM_WIDTH), lambda b, i: (b * nt + i, MIX_WIDTH // MEM_WIDTH)),
        out_shape=jax.ShapeDtypeStruct(ycat.shape, ycat.dtype),
        aliases={2: 0}, args=(proj, mkv, ycat), name=name, q=q, budget_us=HOST_US["mem_attn_fwd"])


def _mem_attn_bwd(proj, q_col_block, mkv, dycat, dproj, bsz, *, name, q=None):
    n = proj.shape[0]
    t = n // bsz
    tq = _tile(t, 512, SUBLANE)
    nt = t // tq
    scale = HEAD_DIM ** -0.5

    def body(q_ref, kv_ref, do_ref, old_ref, dq_ref, dkv_ref):
        del old_ref

        @pl.when(pl.program_id(1) == 0)
        def _():
            dkv_ref[...] = jnp.zeros_like(dkv_ref)

        dqs, dks, dvs = [], [], []
        for h in range(MEM_HEADS):
            sl = slice(h * HEAD_DIM, (h + 1) * HEAD_DIM)
            q = q_ref[:, sl].astype(MXU)
            k = kv_ref[:, sl].astype(MXU)
            v = kv_ref[:, MEM_WIDTH + h * HEAD_DIM: MEM_WIDTH + (h + 1) * HEAD_DIM].astype(MXU)
            do = do_ref[:, sl].astype(MXU)
            s = _dot_nt(q, k) * scale
            m = jnp.max(s, axis=-1, keepdims=True)
            p = jnp.exp(s - m)
            p = p / jnp.sum(p, axis=-1, keepdims=True)
            dvs.append(_dot_tn(p.astype(MXU), do))
            dp = _dot_nt(do, v)
            ds = (p * (dp - jnp.sum(dp * p, axis=-1, keepdims=True)) * scale).astype(MXU)
            dqs.append(_dot(ds, k))
            dks.append(_dot_tn(ds, q))
        dq_ref[...] = jnp.concatenate(dqs, axis=-1).astype(dq_ref.dtype)
        dkv_ref[...] += jnp.concatenate(dks + dvs, axis=-1)

    return _hosted_call(
        body, grid=(bsz, nt),
        in_specs=[pl.BlockSpec((tq, MEM_WIDTH), lambda b, i: (b * nt + i, q_col_block)),
                  pl.BlockSpec((MEM_LEN, 2 * MEM_WIDTH), lambda b, i: (b, 0)),
                  pl.BlockSpec((tq, MEM_WIDTH), lambda b, i: (b * nt + i, MIX_WIDTH // MEM_WIDTH)),
                  pl.BlockSpec(memory_space=pl.ANY)],
        out_specs=[pl.BlockSpec((tq, MEM_WIDTH), lambda b, i: (b * nt + i, q_col_block)),
                   pl.BlockSpec((MEM_LEN, 2 * MEM_WIDTH), lambda b, i: (b, 0))],
        out_shape=[jax.ShapeDtypeStruct(dproj.shape, dproj.dtype),
                   jax.ShapeDtypeStruct((bsz * MEM_LEN, 2 * MEM_WIDTH), F32)],
        aliases={3: 0}, args=(proj, mkv, dycat, dproj), name=name, q=q, budget_us=HOST_US["mem_attn_bwd"])


def _swa_scores(q, k, h, dist, mask, sink):
    s = _dot_nt(q, k) * (HEAD_DIM ** -0.5)
    s = jnp.where(mask, s - SLOPES[h] * dist, -jnp.inf)
    m = jnp.maximum(jnp.max(s, axis=-1, keepdims=True), sink)
    p = jnp.exp(s - m)
    psink = jnp.exp(sink - m)
    inv = 1.0 / (jnp.sum(p, axis=-1, keepdims=True) + psink)
    return p * inv, psink * inv


def _swa_mask(n):
    qi = lax.broadcasted_iota(jnp.int32, (WINDOW, 2 * WINDOW), 0) + WINDOW
    ki = lax.broadcasted_iota(jnp.int32, (WINDOW, 2 * WINDOW), 1)
    dist = qi - ki
    mask = (dist >= 0) & (dist < WINDOW) & ((n > 0) | (ki >= WINDOW))
    return dist.astype(F32), mask


def _swa_fwd(proj, kv, sinks, bsz, *, name, q=None):
    n_tok = proj.shape[0]
    nb = n_tok // bsz // WINDOW
    kvw = SWA_KV_HEADS * HEAD_DIM

    def body(sink_ref, q_ref, kvp_ref, kvc_ref, o_ref):
        n = pl.program_id(1)
        dist, mask = _swa_mask(n)
        kk = jnp.concatenate([kvp_ref[:, :kvw], kvc_ref[:, :kvw]], axis=0).astype(MXU)
        vv = jnp.concatenate([kvp_ref[:, kvw:], kvc_ref[:, kvw:]], axis=0).astype(MXU)
        outs = []
        for h in range(SWA_HEADS):
            c = h // SWA_GROUP
            q = q_ref[:, h * HEAD_DIM:(h + 1) * HEAD_DIM].astype(MXU)
            p, _ = _swa_scores(q, kk[:, c * HEAD_DIM:(c + 1) * HEAD_DIM], h, dist, mask, sink_ref[h])
            outs.append(_dot(p.astype(MXU), vv[:, c * HEAD_DIM:(c + 1) * HEAD_DIM]))
        o_ref[...] = jnp.concatenate(outs, axis=-1).astype(o_ref.dtype)

    return _hosted_call(
        body, grid=(bsz, nb),
        in_specs=[pl.BlockSpec(memory_space=pltpu.SMEM),
                  pl.BlockSpec((WINDOW, MIX_WIDTH), lambda b, n: (b * nb + n, 0)),
                  pl.BlockSpec((WINDOW, 2 * kvw), lambda b, n: (b * nb + jnp.maximum(n - 1, 0), 0)),
                  pl.BlockSpec((WINDOW, 2 * kvw), lambda b, n: (b * nb + n, 0))],
        out_specs=pl.BlockSpec((WINDOW, MIX_WIDTH), lambda b, n: (b * nb + n, 0)),
        out_shape=jax.ShapeDtypeStruct((n_tok, D_MODEL), MXU),
        args=(sinks, proj, kv, kv), name=name, q=q, budget_us=HOST_US["swa_fwd"])


def _swa_bwd(proj, kv, sinks, dycat, bsz, *, name, q=None):
    n_tok = proj.shape[0]
    nb = n_tok // bsz // WINDOW
    kvw = SWA_KV_HEADS * HEAD_DIM

    def body(sink_ref, q_ref, kvp_ref, kvc_ref, do_ref, dq_ref, dkvc_ref, dkvp_ref, dsink_ref):
        n = pl.program_id(1)

        @pl.when((pl.program_id(0) == 0) & (n == 0))
        def _():
            dsink_ref[...] = jnp.zeros_like(dsink_ref)

        dist, mask = _swa_mask(n)
        kk = jnp.concatenate([kvp_ref[:, :kvw], kvc_ref[:, :kvw]], axis=0).astype(MXU)
        vv = jnp.concatenate([kvp_ref[:, kvw:], kvc_ref[:, kvw:]], axis=0).astype(MXU)
        lane = lax.broadcasted_iota(jnp.int32, (SUBLANE, LANE), 1)
        dqs = []
        dks = [None] * SWA_KV_HEADS
        dvs = [None] * SWA_KV_HEADS
        dsink = jnp.zeros((SUBLANE, LANE), F32)
        for h in range(SWA_HEADS):
            c = h // SWA_GROUP
            k = kk[:, c * HEAD_DIM:(c + 1) * HEAD_DIM]
            v = vv[:, c * HEAD_DIM:(c + 1) * HEAD_DIM]
            q = q_ref[:, h * HEAD_DIM:(h + 1) * HEAD_DIM].astype(MXU)
            do = do_ref[:, h * HEAD_DIM:(h + 1) * HEAD_DIM].astype(MXU)
            p, psink = _swa_scores(q, k, h, dist, mask, sink_ref[h])
            dv = _dot_tn(p.astype(MXU), do)
            dp = _dot_nt(do, v)
            rs = jnp.sum(dp * p, axis=-1, keepdims=True)
            ds = (p * (dp - rs) * (HEAD_DIM ** -0.5)).astype(MXU)
            dsink = dsink + jnp.where(lane == h, jnp.sum(-psink * rs, axis=0, keepdims=True), 0.0)
            dqs.append(_dot(ds, k))
            dk = _dot_tn(ds, q)
            dks[c] = dk if dks[c] is None else dks[c] + dk
            dvs[c] = dv if dvs[c] is None else dvs[c] + dv
        dq_ref[...] = jnp.concatenate(dqs, axis=-1).astype(dq_ref.dtype)
        dkv = jnp.concatenate(dks + dvs, axis=-1)
        dkvp_ref[...] = dkv[:WINDOW]
        dkvc_ref[...] = dkv[WINDOW:]
        dsink_ref[...] += dsink

    qspec = pl.BlockSpec((WINDOW, MIX_WIDTH), lambda b, n: (b * nb + n, 0))
    kvspec = pl.BlockSpec((WINDOW, 2 * kvw), lambda b, n: (b * nb + n, 0))
    return _hosted_call(
        body, grid=(bsz, nb),
        in_specs=[pl.BlockSpec(memory_space=pltpu.SMEM), qspec,
                  pl.BlockSpec((WINDOW, 2 * kvw), lambda b, n: (b * nb + jnp.maximum(n - 1, 0), 0)),
                  kvspec, qspec],
        out_specs=[qspec, kvspec, kvspec, pl.BlockSpec((SUBLANE, LANE), lambda b, n: (0, 0))],
        out_shape=[jax.ShapeDtypeStruct((n_tok, D_MODEL), MXU),
                   jax.ShapeDtypeStruct((n_tok, 2 * kvw), F32),
                   jax.ShapeDtypeStruct((n_tok, 2 * kvw), F32),
                   jax.ShapeDtypeStruct((SUBLANE, LANE), F32)],
        args=(sinks, proj, kv, kv, dycat), name=name, q=q, budget_us=HOST_US["swa_bwd"])


def _swa_dkv_combine(curs, prevs, bsz, *, name):
    n_tok, w = curs[0].shape
    nb = n_tok // bsz // WINDOW
    k = len(curs)

    def body(*refs):
        o_ref = refs[-1]
        n = pl.program_id(1)
        acc = refs[0][...]
        for r in refs[1:k]:
            acc = acc + r[...]
        nxt = refs[k][...]
        for r in refs[k + 1:2 * k]:
            nxt = nxt + r[...]
        o_ref[...] = (acc + jnp.where(n < nb - 1, nxt, 0.0)).astype(o_ref.dtype)

    cur = pl.BlockSpec((WINDOW, w), lambda b, n: (b * nb + n, 0))
    prv = pl.BlockSpec((WINDOW, w), lambda b, n: (b * nb + jnp.minimum(n + 1, nb - 1), 0))
    return pl.pallas_call(
        body, grid=(bsz, nb), in_specs=[cur] * k + [prv] * k, out_specs=cur,
        out_shape=jax.ShapeDtypeStruct((n_tok, w), MXU),
        name=name, compiler_params=_cp((PAR, PAR)))(*curs, *prevs)


def _lru_gates(ux, halo, ext_ref, wc_ref, bc_ref, wr_ref, br_ref, wi_ref, bi_ref, lam_ref):
    tt = ux.shape[0]
    ext_ref[0:SUBLANE, :] = halo
    ext_ref[SUBLANE:, :] = ux
    xs = [ux] + [ext_ref[pl.ds(SUBLANE - k, tt), :] for k in range(1, LRU_CONV)]
    xc = bc_ref[...] + wc_ref[3:4, :] * xs[0] + wc_ref[2:3, :] * xs[1] + wc_ref[1:2, :] * xs[2] + wc_ref[0:1, :] * xs[3]
    pre_r, pre_i = [], []
    for blk in range(MIX_WIDTH // GATE_TILE):
        xb = xc[:, blk * GATE_TILE:(blk + 1) * GATE_TILE].astype(MXU)
        pre_r.append(_dot(xb, wr_ref[blk]))
        pre_i.append(_dot(xb, wi_ref[blk]))
    r = jax.nn.sigmoid(jnp.concatenate(pre_r, axis=-1) + br_ref[...])
    i = jax.nn.sigmoid(jnp.concatenate(pre_i, axis=-1) + bi_ref[...])
    nlam = -lam_ref[...]
    sp = jnp.maximum(nlam, 0.0) + jnp.log(1.0 + jnp.exp(-jnp.abs(nlam)))
    log_a = -LRU_C * r * sp
    a = jnp.exp(log_a)
    om = -jnp.tanh(log_a) * (a * a + 1.0)
    s = jnp.sqrt(om)
    return xs, xc, r, i, sp, a, s


def _lru_fwd(proj, wconv, bconv, wr, br, wi, bi, lam, bsz, *, name, q=None):
    n_tok = proj.shape[0]
    t = n_tok // bsz
    tt = _tile(t, 256, SUBLANE)
    nt = t // tt
    w = MIX_WIDTH
    ng = tt // SUBLANE

    def body(pg_ref, halo_ref, wc_ref, bc_ref, wr_ref, br_ref, wi_ref, bi_ref, lam_ref,
             y_ref, h_ref, ext_ref, a_ref, b_ref, carry_ref):
        ti = pl.program_id(1)

        @pl.when(ti == 0)
        def _():
            carry_ref[...] = jnp.zeros_like(carry_ref)

        gate = pg_ref[:, :w]
        ux = pg_ref[:, w:]
        halo = jnp.where(ti > 0, halo_ref[...], 0.0)
        _, xc, _, i, _, a, s = _lru_gates(ux, halo, ext_ref, wc_ref, bc_ref, wr_ref, br_ref, wi_ref, bi_ref, lam_ref)
        a_ref[...] = a
        b_ref[...] = s * (i * xc)
        row = lax.broadcasted_iota(jnp.int32, (SUBLANE, w), 0)

        def group(g, hprev):
            off = pl.multiple_of(g * SUBLANE, SUBLANE)
            ca = a_ref[pl.ds(off, SUBLANE), :]
            cb = b_ref[pl.ds(off, SUBLANE), :]
            for d in (1, 2, 4):
                a_sh = jnp.where(row >= d, pltpu.roll(ca, d, axis=0), 1.0)
                b_sh = jnp.where(row >= d, pltpu.roll(cb, d, axis=0), 0.0)
                cb = ca * b_sh + cb
                ca = ca * a_sh
            h = ca * hprev + cb
            b_ref[pl.ds(off, SUBLANE), :] = h
            return jnp.broadcast_to(h[SUBLANE - 1:SUBLANE, :], (SUBLANE, w))

        carry_ref[...] = lax.fori_loop(0, ng, group, carry_ref[...])
        h = b_ref[...]
        h_ref[...] = h
        y_ref[...] = (h * _gelu(gate)).astype(y_ref.dtype)

    vec = lambda r: pl.BlockSpec((r, w), lambda b, i: (0, 0))
    wspec = pl.BlockSpec((w // GATE_TILE, GATE_TILE, GATE_TILE), lambda b, i: (0, 0, 0))
    hb = tt // SUBLANE
    return _hosted_call(
        body, grid=(bsz, nt),
        in_specs=[pl.BlockSpec((tt, 2 * w), lambda b, i: (b * nt + i, 0)),
                  pl.BlockSpec((SUBLANE, w), lambda b, i: (jnp.maximum((b * nt + i) * hb - 1, 0), 1)),
                  vec(LRU_CONV), vec(1), wspec, vec(1), wspec, vec(1), vec(1)],
        out_specs=[pl.BlockSpec((tt, w), lambda b, i: (b * nt + i, 0)),
                   pl.BlockSpec((tt, w), lambda b, i: (b * nt + i, 0))],
        out_shape=[jax.ShapeDtypeStruct((n_tok, D_MODEL), MXU), jax.ShapeDtypeStruct((n_tok, w), F32)],
        scratch_shapes=[pltpu.VMEM((tt + SUBLANE, w), F32), pltpu.VMEM((tt, w), F32),
                        pltpu.VMEM((tt, w), F32), pltpu.VMEM((SUBLANE, w), F32)],
        args=(proj, proj, wconv, bconv, wr, br, wi, bi, lam), name=name, q=q, budget_us=HOST_US["lru_fwd"])


def _lru_bwd(proj, hs, dycat, wconv, bconv, wr, br, wi, bi, lam, bsz, *, name, q=None):
    n_tok = proj.shape[0]
    t = n_tok // bsz
    tt = _tile(t, 256, SUBLANE)
    nt = t // tt
    w = MIX_WIDTH
    ng = tt // SUBLANE
    nblk = w // GATE_TILE

    def body(pg_ref, halo_ref, h_ref, hhalo_ref, dy_ref, wc_ref, bc_ref, wr_ref, br_ref, wi_ref, bi_ref, lam_ref,
             dp_ref, dwc_ref, dbc_ref, dwr_ref, dbr_ref, dwi_ref, dbi_ref, dlam_ref,
             ext_ref, a_ref, c_ref, g_ref, gcarry_ref, xcarry_ref):
        bi_ = pl.program_id(0)
        ti = nt - 1 - pl.program_id(1)

        @pl.when((bi_ == 0) & (pl.program_id(1) == 0))
        def _():
            for r in (dwc_ref, dbc_ref, dwr_ref, dbr_ref, dwi_ref, dbi_ref, dlam_ref):
                r[...] = jnp.zeros_like(r)

        @pl.when(pl.program_id(1) == 0)
        def _():
            gcarry_ref[...] = jnp.zeros_like(gcarry_ref)
            xcarry_ref[...] = jnp.zeros_like(xcarry_ref)

        gate = pg_ref[:, :w]
        ux = pg_ref[:, w:]
        halo = jnp.where(ti > 0, halo_ref[...], 0.0)
        xs, xc, r, i, sp, a, s = _lru_gates(ux, halo, ext_ref, wc_ref, bc_ref, wr_ref, br_ref, wi_ref, bi_ref, lam_ref)
        h = h_ref[...]
        gl, dgl = _gelu_and_grad(gate)
        dy = dy_ref[...].astype(F32)
        dgate = dy * h * dgl
        row_t = lax.broadcasted_iota(jnp.int32, (tt, w), 0)
        g_ref[...] = dy * gl + jnp.where(row_t == tt - 1, gcarry_ref[0:1, :], 0.0)
        c_ref[...] = _shift_up(a, 1, row_t)
        row = lax.broadcasted_iota(jnp.int32, (SUBLANE, w), 0)

        a_ref[...] = a

        def group(k, gnext):
            off = pl.multiple_of((ng - 1 - k) * SUBLANE, SUBLANE)
            cc = c_ref[pl.ds(off, SUBLANE), :]
            cb = g_ref[pl.ds(off, SUBLANE), :]
            cb = cb + jnp.where(row == SUBLANE - 1, gnext, 0.0)
            cc = jnp.where(row == SUBLANE - 1, 0.0, cc)
            for d in (1, 2, 4):
                c_sh = jnp.where(row < SUBLANE - d, pltpu.roll(cc, SUBLANE - d, axis=0), 1.0)
                b_sh = jnp.where(row < SUBLANE - d, pltpu.roll(cb, SUBLANE - d, axis=0), 0.0)
                cb = cc * b_sh + cb
                cc = cc * c_sh
            g_ref[pl.ds(off, SUBLANE), :] = cb
            a0 = a_ref[pl.ds(off, SUBLANE), :]
            return jnp.broadcast_to(a0[0:1, :] * cb[0:1, :], (SUBLANE, w))

        gc = lax.fori_loop(0, ng, group, jnp.zeros((SUBLANE, w), F32))
        gcarry_ref[...] = gc
        gsc = g_ref[...]

        hhalo = jnp.where(ti > 0, hhalo_ref[SUBLANE - 1:SUBLANE, :], 0.0)
        hprev = jnp.where(row_t == 0, hhalo, pltpu.roll(h, 1, axis=0))
        gated = i * xc
        d_gated = gsc * s
        d_atot = gsc * hprev - (gsc * gated) * a / s
        d_loga = d_atot * a
        d_r = d_loga * (-LRU_C) * sp
        dlam_ref[...] += jnp.sum(d_loga * r, axis=0, keepdims=True) * (LRU_C * jax.nn.sigmoid(-lam_ref[...]))
        d_i = d_gated * xc
        d_xc = d_gated * i
        d_pr = d_r * r * (1.0 - r)
        d_pi = d_i * i * (1.0 - i)
        dbr_ref[...] += jnp.sum(d_pr, axis=0, keepdims=True)
        dbi_ref[...] += jnp.sum(d_pi, axis=0, keepdims=True)
        extra = []
        for blk in range(nblk):
            sl = slice(blk * GATE_TILE, (blk + 1) * GATE_TILE)
            xb = xc[:, sl].astype(MXU)
            dr_b = d_pr[:, sl].astype(MXU)
            di_b = d_pi[:, sl].astype(MXU)
            dwr_ref[blk] += _dot_tn(xb, dr_b)
            dwi_ref[blk] += _dot_tn(xb, di_b)
            extra.append(_dot_nt(dr_b, wr_ref[blk]) + _dot_nt(di_b, wi_ref[blk]))
        d_xc = d_xc + jnp.concatenate(extra, axis=-1)
        dbc_ref[...] += jnp.sum(d_xc, axis=0, keepdims=True)
        for k in range(LRU_CONV):
            dwc_ref[k:k + 1, :] += jnp.sum(d_xc * xs[LRU_CONV - 1 - k], axis=0, keepdims=True)
        ext_ref[0:tt, :] = d_xc
        ext_ref[tt:, :] = xcarry_ref[...]
        dux = wc_ref[3:4, :] * d_xc
        for k in range(LRU_CONV - 1):
            dux = dux + wc_ref[k:k + 1, :] * ext_ref[pl.ds(LRU_CONV - 1 - k, tt), :]
        xcarry_ref[...] = d_xc[0:SUBLANE, :]
        dp_ref[:, :w] = dgate.astype(dp_ref.dtype)
        dp_ref[:, w:] = dux.astype(dp_ref.dtype)

    vec = lambda r: pl.BlockSpec((r, w), lambda b, i: (0, 0))
    wspec = pl.BlockSpec((nblk, GATE_TILE, GATE_TILE), lambda b, i: (0, 0, 0))
    hb = tt // SUBLANE
    rblk = lambda b, i: b * nt + (nt - 1 - i)
    halo_idx = lambda b, i: jnp.maximum(rblk(b, i) * hb - 1, 0)
    wide = pl.BlockSpec((tt, 2 * w), lambda b, i: (rblk(b, i), 0))
    narrow = pl.BlockSpec((tt, w), lambda b, i: (rblk(b, i), 0))
    return _hosted_call(
        body, grid=(bsz, nt),
        in_specs=[wide, pl.BlockSpec((SUBLANE, w), lambda b, i: (halo_idx(b, i), 1)),
                  narrow, pl.BlockSpec((SUBLANE, w), lambda b, i: (halo_idx(b, i), 0)), narrow,
                  vec(LRU_CONV), vec(1), wspec, vec(1), wspec, vec(1), vec(1)],
        out_specs=[wide, vec(LRU_CONV), vec(1), wspec, vec(1), wspec, vec(1), vec(1)],
        out_shape=[jax.ShapeDtypeStruct((n_tok, 2 * w + MEM_WIDTH), MXU),
                   jax.ShapeDtypeStruct((LRU_CONV, w), F32), jax.ShapeDtypeStruct((1, w), F32),
                   jax.ShapeDtypeStruct((nblk, GATE_TILE, GATE_TILE), F32), jax.ShapeDtypeStruct((1, w), F32),
                   jax.ShapeDtypeStruct((nblk, GATE_TILE, GATE_TILE), F32), jax.ShapeDtypeStruct((1, w), F32),
                   jax.ShapeDtypeStruct((1, w), F32)],
        scratch_shapes=[pltpu.VMEM((tt + SUBLANE, w), F32), pltpu.VMEM((tt, w), F32), pltpu.VMEM((tt, w), F32),
                        pltpu.VMEM((tt, w), F32), pltpu.VMEM((SUBLANE, w), F32), pltpu.VMEM((SUBLANE, w), F32)],
        args=(proj, proj, hs, hs, dycat, wconv, bconv, wr, br, wi, bi, lam), name=name, q=q,
        budget_us=HOST_US["lru_bwd"])


def _gate_tiles(w):
    per = GATE_TILE // HEAD_DIM
    w4 = w.reshape(LRU_BLOCKS // per, per, HEAD_DIM, HEAD_DIM)
    eye = jnp.eye(per, dtype=w.dtype)
    return jnp.einsum("bnij,nm->bnimj", w4, eye).reshape(LRU_BLOCKS // per, GATE_TILE, GATE_TILE)


def _gate_blocks(t):
    per = GATE_TILE // HEAD_DIM
    t5 = t.reshape(LRU_BLOCKS // per, per, HEAD_DIM, per, HEAD_DIM)
    eye = jnp.eye(per, dtype=t.dtype)
    return jnp.einsum("bnimj,nm->bnij", t5, eye).reshape(LRU_BLOCKS, HEAD_DIM, HEAD_DIM)


def _row(v):
    return v.reshape(1, -1)


def _local_step(x, mem, target, p, wfull, push_grad, q):
    bsz, t, d = x.shape
    n = bsz * t
    x2d = x.reshape(n, d)
    tgt = target.reshape(n, d)
    mem2d = mem.reshape(bsz * MEM_LEN, d)
    wr_t = [_gate_tiles(p["w_rg_r"][j]).astype(MXU) for j in range(N_A)]
    wi_t = [_gate_tiles(p["w_rg_i"][j]).astype(MXU) for j in range(N_A)]

    mn = [_norm_fwd(mem2d, _row(p["g_mem"][l]), name=f"mem_norm{l}") for l in range(DEPTH)]
    mkv = [None] * DEPTH
    h = _norm_fwd(x2d, _row(p["g_mix_pre"][0]), name="in_norm")
    xin = x2d
    sv = []
    kv = hkv = None
    for l in range(DEPTH):
        s = {"xin": xin, "h": h}
        if q is not None:
            q.horizon = (l + 2) * GROUPS_PER_LAYER
        mkv[l] = _mm_nn(mn[l], wfull("w_mem_kv", l), name=f"mem_kv{l}", q=q)
        if l < N_A:
            proj = _mm_nn(h, wfull("w_in_a", l), name=f"in_proj{l}", q=q)
            ycat, hs = _lru_fwd(proj, p["w_conv_a"][l], _row(p["b_conv_a"][l]), wr_t[l], _row(p["b_rg_r"][l]),
                                wi_t[l], _row(p["b_rg_i"][l]), _row(p["lru_lambda"][l]), bsz, name=f"lru_fwd{l}", q=q)
            s["hs"] = hs
            qblk = 2 * MIX_WIDTH // MEM_WIDTH
        else:
            if l == N_A:
                kv = _mm_nn(hkv, wfull("w_kv", 0), name="kv_proj", q=q)
            proj = _mm_nn(h, wfull("w_in_b", l - N_A), name=f"in_proj{l}", q=q)
            ycat = _swa_fwd(proj, kv, p["sinks_b"][l - N_A], bsz, name=f"swa_fwd{l}", q=q)
            qblk = MIX_WIDTH // MEM_WIDTH
        ycat = _mem_attn_fwd(proj, qblk, mkv[l], ycat, bsz, name=f"mem_attn_fwd{l}", q=q)
        y = _mm_nn(ycat, wfull("w_mix_out", l), name=f"mix_out{l}", q=q, out_dtype=MXU)
        x1, (h2,) = _resid_norm_fwd(xin, y, _row(p["g_mix_post"][l]), [_row(p["g_ffn_pre"][l])], name=f"mix_resid{l}", q=q)
        up = _mm_nn_slots(h2, wfull("w_ffn_up", l), name=f"ffn_up{l}", q=q, out_dtype=MXU)
        act, ug, uv = _ffn_act_fwd(up, p["w_ffn_conv"][l], _row(p["b_ffn_conv"][l]), bsz, name=f"ffn_act{l}", q=q)
        f = _mm_nn(act, wfull("w_ffn_down", l), name=f"ffn_down{l}", q=q, out_dtype=MXU)
        s.update(proj=proj, qblk=qblk, ycat=ycat, y=y, x1=x1, h2=h2, up=up, ug=ug, uv=uv, act=act, f=f)
        sv.append(s)
        if l < DEPTH - 1:
            g_pres = [_row(p["g_mix_pre"][l + 1])] + ([_row(p["g_kv"])] if l + 1 == N_A else [])
            xin, hn = _resid_norm_fwd(x1, f, _row(p["g_ffn_post"][l]), g_pres, name=f"ffn_resid{l}", q=q)
            h = hn[0]
            if l + 1 == N_A:
                hkv = hn[1]
        else:
            g_tot, sq = _loss_fwd(x1, f, _row(p["g_ffn_post"][l]), tgt, name="loss")

    if q is not None:
        q.horizon = LAST_GROUP
    gs = {k: [None] * DEPTH for k in ("g_mix_pre", "g_mix_post", "g_ffn_pre", "g_ffn_post", "g_mem",
                                       "w_ffn_conv", "b_ffn_conv")}
    ga = {k: [None] * N_A for k in ("w_conv_a", "b_conv_a", "w_rg_r", "b_rg_r", "w_rg_i", "b_rg_i", "lru_lambda")}
    gsink = [None] * (DEPTH - N_A)
    dkv_cur, dkv_prev = [], []
    g_tot, df, _, gs["g_ffn_post"][DEPTH - 1] = _resid_norm_bwd(
        g_tot, [], None, [], sv[-1]["f"], _row(p["g_ffn_post"][DEPTH - 1]), name="loss_bwd")
    grad_x = None
    for l in reversed(range(DEPTH)):
        s = sv[l]
        dact = _mm_nt(df, wfull("w_ffn_down", l), name=f"d_act{l}", q=q, out_dtype=MXU)
        push_grad("w_ffn_down", l, _mm_tn(s["act"], df, name=f"dw_down{l}", q=q))
        dug, duv, gs["w_ffn_conv"][l], gs["b_ffn_conv"][l] = _ffn_act_bwd(
            s["up"], s["ug"], s["uv"], dact, p["w_ffn_conv"][l], bsz, name=f"ffn_act_bwd{l}", q=q)
        dh2 = _mm_ffn_dh(dug, duv, wfull("w_ffn_up", l), name=f"d_h2_{l}", q=q)
        up_slots = dict(slot_cols=2 * D_FF // N_CHIP, n_slots=N_CHIP)
        dwu = _mm_tn_slots(s["h2"], dug, name=f"dw_up_g{l}", q=q, **up_slots)
        push_grad("w_ffn_up", l, _mm_tn_slots(s["h2"], duv, name=f"dw_up_v{l}", q=q, out=dwu,
                                              first_slot=N_CHIP // 2, **up_slots))
        g1, dy, (gs["g_ffn_pre"][l],), gs["g_mix_post"][l] = _resid_norm_bwd(
            g_tot, [dh2], s["x1"], [_row(p["g_ffn_pre"][l])], s["y"], _row(p["g_mix_post"][l]), name=f"mix_resid_bwd{l}", q=q)
        dycat = _mm_nt(dy, wfull("w_mix_out", l), name=f"d_ycat{l}", q=q, out_dtype=MXU)
        push_grad("w_mix_out", l, _mm_tn(s["ycat"], dy, name=f"dw_mix_out{l}", q=q))
        if l < N_A:
            dproj, dwc, dbc, dwr, dbr, dwi, dbi, dlam = _lru_bwd(
                s["proj"], s["hs"], dycat, p["w_conv_a"][l], _row(p["b_conv_a"][l]), wr_t[l], _row(p["b_rg_r"][l]),
                wi_t[l], _row(p["b_rg_i"][l]), _row(p["lru_lambda"][l]), bsz, name=f"lru_bwd{l}", q=q)
            ga["w_conv_a"][l], ga["b_conv_a"][l], ga["lru_lambda"][l] = dwc, dbc[0], dlam[0]
            ga["w_rg_r"][l], ga["w_rg_i"][l] = _gate_blocks(dwr), _gate_blocks(dwi)
            ga["b_rg_r"][l] = dbr.reshape(LRU_BLOCKS, HEAD_DIM)
            ga["b_rg_i"][l] = dbi.reshape(LRU_BLOCKS, HEAD_DIM)
            w_in, j = "w_in_a", l
        else:
            dproj, dc, dp_, dsk = _swa_bwd(s["proj"], kv, p["sinks_b"][l - N_A], dycat, bsz, name=f"swa_bwd{l}", q=q)
            dkv_cur.append(dc)
            dkv_prev.append(dp_)
            gsink[l - N_A] = dsk[0, :SWA_HEADS]
            w_in, j = "w_in_b", l - N_A
        dproj, dmkv = _mem_attn_bwd(s["proj"], s["qblk"], mkv[l], dycat, dproj, bsz, name=f"mem_attn_bwd{l}", q=q)
        dh = _mm_nt(dproj, wfull(w_in, j), name=f"d_h{l}", q=q, out_dtype=MXU)
        push_grad(w_in, j, _mm_tn(s["h"], dproj, name=f"dw_in{l}", q=q))
        dmkv = dmkv.astype(MXU)
        dmn = _mm_nt(dmkv, wfull("w_mem_kv", l), name=f"d_mem_norm{l}", q=q)
        push_grad("w_mem_kv", l, _mm_tn(mn[l], dmkv, name=f"dw_mem_kv{l}", q=q))
        gs["g_mem"][l] = _norm_bwd_dg(dmn, mem2d, _row(p["g_mem"][l]), name=f"mem_norm_bwd{l}")
        dhs, g_pres = [dh], [_row(p["g_mix_pre"][l])]
        if l == N_A:
            dkv = _swa_dkv_combine(dkv_cur, dkv_prev, bsz, name="dkv_combine")
            dhs.append(_mm_nt(dkv, wfull("w_kv", 0), name="d_hkv", q=q, out_dtype=MXU))
            g_pres.append(_row(p["g_kv"]))
            push_grad("w_kv", 0, _mm_tn(hkv, dkv, name="dw_kv", q=q))
        if l > 0:
            g_tot, df, dgpre, gs["g_ffn_post"][l - 1] = _resid_norm_bwd(
                g1, dhs, s["xin"], g_pres, sv[l - 1]["f"], _row(p["g_ffn_post"][l - 1]), name=f"ffn_resid_bwd{l - 1}", q=q)
        else:
            grad_x, _, dgpre, _ = _resid_norm_bwd(g1, dhs, s["xin"], g_pres, None, None, name="in_norm_bwd", q=q)
        gs["g_mix_pre"][l] = dgpre[0]
        if l == N_A:
            g_kv = dgpre[1][0]

    grads = {}
    for k in ("g_mix_pre", "g_mix_post", "g_ffn_pre", "g_ffn_post", "g_mem", "b_ffn_conv"):
        grads[k] = jnp.concatenate(gs[k], axis=0)
    grads["w_ffn_conv"] = jnp.stack(gs["w_ffn_conv"])
    for k, v in ga.items():
        grads[k] = jnp.stack(v)
    grads["sinks_b"] = jnp.stack(gsink)
    grads["g_kv"] = g_kv
    return jnp.sum(sq), grad_x.reshape(bsz, t, d), grads


N_CHIP = 4
HALF_ALIGN = 16
D2D_STREAMS = 2
MIN_PART_BYTES = 128 * 1024


def _full_shape(kind, shard_shape):
    l, r, c = shard_shape
    return {"row": (l, N_CHIP * r, c), "col": (l, r, N_CHIP * c), "slot": (N_CHIP, l, r, c)}[kind]


def _slot_view(ref, kind, shard_shape, s, hf, sub=(0, 1)):
    _, r, c = shard_shape
    rh = r // 2
    if hf is None:
        size = r // sub[1]
        start = sub[0] * size
    else:
        size = rh // sub[1]
        start = hf * rh + sub[0] * size
    if kind == "row":
        start = s * r + start
    if not isinstance(start, int):
        start = pl.multiple_of(start, HALF_ALIGN)
    rows = pl.ds(start, size)
    if kind == "row":
        return ref.at[:, rows, :]
    if kind == "col":
        return ref.at[:, rows, pl.ds(s * c, c)]
    return ref.at[s, :, rows, :]


def _half_view(ref, shard_shape, hf, sub=(0, 1)):
    rh = shard_shape[1] // 2
    size = rh // sub[1]
    return ref.at[:, pl.ds(pl.multiple_of(hf * rh + sub[0] * size, HALF_ALIGN), size), :]


def _with_slot(kind, s, fn):
    if kind != "col" or isinstance(s, int):
        fn(s)
        return
    for k in range(N_CHIP):
        @pl.when(s == k)
        def _(k=k):
            fn(k)


def _mesh_pos():
    return lax.axis_index("x"), lax.axis_index("y"), lax.axis_index("c")


def _other_chips(x, y):
    return [(1 - x, y), (x, 1 - y), (1 - x, 1 - y)]


ICI_BYTES_PER_US = 6.0e4
ICI_GATHER_BYTES_PER_US = 5.5e4
D2D_BYTES_PER_US = 4.0e5


class _Chunk:
    def __init__(self, group, cost, ins, out_shapes, alias, n_sem, start, finish, done, buffer=None, bind=None):
        self.group, self.cost, self.ins, self.out_shapes, self.alias, self.n_sem = group, cost, ins, out_shapes, alias, n_sem
        self.start, self.finish, self.done = start, finish, done
        self.buffer = buffer
        self.bind = bind

    def prepare(self):
        if self.bind is not None:
            self.bind(self)


def _merged(chunks):
    groups, by_buffer = [], {}
    for ch in chunks:
        key = None if ch.buffer is None else (id(ch.buffer[0]), ch.buffer[1])
        if key is not None and key in by_buffer:
            by_buffer[key].append(ch)
        else:
            groups.append([ch])
            if key is not None:
                by_buffer[key] = groups[-1]
    out = []
    for parts in groups:
        if len(parts) == 1:
            out.append(parts[0])
            continue
        offs = [sum(p.n_sem for p in parts[:i]) for i in range(len(parts))]

        def run(phase, ins, outs, ss, rs, b, parts=parts, offs=offs):
            for p, o in zip(parts, offs):
                getattr(p, phase)(ins, outs, ss, rs, b + o)

        def done(outs, parts=parts):
            for p in parts:
                p.done(outs)

        first = parts[0]
        out.append(_Chunk(first.group, sum(p.cost for p in parts), first.ins, first.out_shapes, first.alias,
                          sum(p.n_sem for p in parts), functools.partial(run, "start"),
                          functools.partial(run, "finish"), done))
    return out


LAST_GROUP = 1 << 30
MIN_CARRIED_US = 8.0


class _CommQueue:
    def __init__(self):
        self.pending = []
        self.flushes = 0
        self.horizon = LAST_GROUP

    def push(self, chunk):
        self.pending.append(chunk)

    def take(self, budget_us):
        got, used = [], 0.0
        for ch in sorted(self.pending, key=lambda ch: (ch.group, -ch.cost)):
            if ch.group >= self.horizon and ch.group != LAST_GROUP:
                continue
            if used + ch.cost <= budget_us and not self._shares_buffer(ch, got):
                got.append(ch)
                used += ch.cost
        if used < MIN_CARRIED_US:
            return []
        return self._taken(got)

    @staticmethod
    def _shares_buffer(ch, others):
        return ch.buffer is not None and any(
            o.buffer is not None and o.buffer[0] is ch.buffer[0] and o.buffer[1] != ch.buffer[1] for o in others)

    def _taken(self, got):
        self.pending = [ch for ch in self.pending if ch not in got]
        for ch in got:
            ch.prepare()
        return _merged(got)

    def flush(self, group=LAST_GROUP):
        while True:
            chunks = []
            for ch in self.pending:
                if ch.group <= group and not self._shares_buffer(ch, chunks):
                    chunks.append(ch)
            if not chunks:
                return
            _run_chunks(self._taken(chunks), name=f"comm_flush{self.flushes}")
            self.flushes += 1


def _run_chunks(chunks, *, name):
    ins = [a for ch in chunks for a in ch.ins]
    outs = [s for ch in chunks for s in ch.out_shapes]
    alias, offs = {}, []
    i0 = o0 = s0 = 0
    for ch in chunks:
        offs.append((i0, o0, s0))
        for ci, co in ch.alias.items():
            alias[i0 + ci] = o0 + co
        i0 += len(ch.ins)
        o0 += len(ch.out_shapes)
        s0 += ch.n_sem

    def body(*refs):
        send_sems, recv_sems = refs[i0 + o0:]
        for phase in ("start", "finish"):
            for ch, (a, b, s) in zip(chunks, offs):
                getattr(ch, phase)(refs[a:a + len(ch.ins)], refs[i0 + b:i0 + b + len(ch.out_shapes)],
                                   send_sems, recv_sems, s)

    hbm = pl.BlockSpec(memory_space=pl.ANY)
    res = pl.pallas_call(
        body, in_specs=[hbm] * i0, out_specs=[hbm] * o0, out_shape=outs,
        scratch_shapes=[pltpu.SemaphoreType.DMA((s0,)), pltpu.SemaphoreType.DMA((s0,))],
        input_output_aliases=alias, name=name, compiler_params=pltpu.CompilerParams(has_side_effects=True))(*ins)
    for ch, (_, b, _) in zip(chunks, offs):
        ch.done(list(res[b:b + len(ch.out_shapes)]))


def _remote(src, dst, send_sems, recv_sems, k, dev):
    return pltpu.make_async_remote_copy(src_ref=src, dst_ref=dst, send_sem=send_sems.at[k], recv_sem=recv_sems.at[k],
                                        device_id=dev, device_id_type=MESH_T)


def _gather_chunks(q, group, kind, shard, l, ready):
    _, r, c = shard.shape
    shp = (1, r, c)
    rh = r // 2
    parts = max(p for p in (8, 4, 2, 1)
                if (rh // p) % HALF_ALIGN == 0 and (p == 1 or (rh // p) * c * shard.dtype.itemsize >= MIN_PART_BYTES))
    part_bytes = (rh // parts) * c * shard.dtype.itemsize
    full_type = jax.ShapeDtypeStruct(_full_shape(kind, shp), shard.dtype)
    state = {"full": None, "parts_done": 0}

    def bind_first(ch):
        ch.ins, ch.alias = ([shard], {}) if state["full"] is None else ([shard, state["full"]], {1: 0})

    def bind_full(ch):
        ch.ins = [state["full"]]

    def make_part(p):
        sub = (p, parts)

        def any_part(full):
            return _slot_view(full, kind, shp, 0, 0, sub)

        def own_rows(src):
            return src.at[:, pl.ds(p * (r // parts), r // parts), :]

        def start1(ins, outs, ss, rs, b):
            x, y, c_ = _mesh_pos()
            src, full = ins[0].at[pl.ds(l, 1)], outs[0]
            _with_slot(kind, 2 * x + y, lambda s: pltpu.make_async_copy(
                own_rows(src), _slot_view(full, kind, shp, s, None, sub), ss.at[b + N_CHIP - 1]).start())
            for j, (ox, oy) in enumerate(_other_chips(x, y)):
                _with_slot(kind, 2 * x + y, lambda s, j=j, ox=ox, oy=oy: _remote(
                    _half_view(src, shp, c_, sub), _slot_view(full, kind, shp, s, c_, sub), ss, rs, b + j,
                    (ox, oy, c_)).start())

        def finish1(ins, outs, ss, rs, b):
            x, y, c_ = _mesh_pos()
            h = any_part(outs[0])
            for j in range(N_CHIP - 1):
                _remote(h, h, ss, rs, b + j, (x, y, 1 - c_)).wait()
            pltpu.make_async_copy(own_rows(ins[0].at[pl.ds(l, 1)]), _slot_view(outs[0], kind, shp, 0, None, sub),
                                  ss.at[b + N_CHIP - 1]).wait()

        def start2(ins, outs, ss, rs, b):
            x, y, c_ = _mesh_pos()
            for j, (ox, oy) in enumerate(_other_chips(x, y)):
                def forward(s, j=j):
                    v = _slot_view(outs[0], kind, shp, s, c_, sub)
                    _remote(v, v, ss, rs, b + j, (x, y, 1 - c_)).start()
                _with_slot(kind, 2 * ox + oy, forward)

        def finish2(ins, outs, ss, rs, b):
            x, y, c_ = _mesh_pos()
            h = any_part(outs[0])
            for j in range(N_CHIP - 1):
                _remote(h, h, ss, rs, b + j, (x, y, 1 - c_)).wait()

        def done2(outs):
            state["full"] = outs[0]
            state["parts_done"] += 1
            if state["parts_done"] == parts:
                ready(outs[0])

        def done1(outs):
            state["full"] = outs[0]
            q.push(_Chunk(group, 3 * part_bytes / D2D_BYTES_PER_US, None, [full_type], {0: 0}, N_CHIP - 1,
                          start2, finish2, done2, buffer=(state, 2), bind=bind_full))

        return _Chunk(group, 3 * part_bytes / ICI_GATHER_BYTES_PER_US, None, [full_type], None,
                      N_CHIP, start1, finish1, done1, buffer=(state, 1), bind=bind_first)

    for p in range(parts):
        q.push(make_part(p))


def _reduce_scatter_chunks(q, kind, grad, shard_shape, pos, name, ready):
    _, r, c = shard_shape
    shp = (1, r, c)
    rh = r // 2

    rp = rh // D2D_STREAMS

    def landing(ref, s, i):
        return ref.at[s, :, pl.ds(i * rp, rp), :]

    def start1(ins, outs, ss, rs, b):
        x, y, c_ = _mesh_pos()
        for s in range(N_CHIP):
            for i in range(D2D_STREAMS):
                _remote(_slot_view(ins[0], kind, shp, s, 1 - c_, (i, D2D_STREAMS)), landing(outs[0], s, i),
                        ss, rs, b + s * D2D_STREAMS + i, (x, y, 1 - c_)).start()

    def finish1(ins, outs, ss, rs, b):
        x, y, c_ = _mesh_pos()
        for s in range(N_CHIP):
            for i in range(D2D_STREAMS):
                v = landing(outs[0], s, i)
                _remote(v, v, ss, rs, b + s * D2D_STREAMS + i, (x, y, 1 - c_)).wait()

    def start2(ins, outs, ss, rs, b):
        x, y, c_ = _mesh_pos()
        for j, (ox, oy) in enumerate(_other_chips(x, y)):
            _remote(ins[0].at[2 * ox + oy], outs[0].at[j], ss, rs, b + j, (ox, oy, c_)).start()

    def finish2(ins, outs, ss, rs, b):
        x, y, c_ = _mesh_pos()
        for j in range(N_CHIP - 1):
            _remote(outs[0].at[j], outs[0].at[j], ss, rs, b + j, (x, y, 1 - c_)).wait()

    def start3(ins, outs, ss, rs, b):
        x, y, c_ = _mesh_pos()
        for i in range(D2D_STREAMS):
            v = _half_view(outs[0], shp, c_, (i, D2D_STREAMS))
            _remote(v, v, ss, rs, b + i, (x, y, 1 - c_)).start()

    def finish3(ins, outs, ss, rs, b):
        x, y, c_ = _mesh_pos()
        for i in range(D2D_STREAMS):
            v = _half_view(outs[0], shp, c_, (i, D2D_STREAMS))
            _remote(v, v, ss, rs, b + i, (x, y, 1 - c_)).wait()

    def done2(pair, outs):
        half = _rs_chip_add(pair, outs[0], shp, pos, name=f"rs_chip_add_{name}")
        q.push(_Chunk(LAST_GROUP, rh * c * 4 / D2D_BYTES_PER_US, [half], [jax.ShapeDtypeStruct(half.shape, half.dtype)],
                      {0: 0}, D2D_STREAMS, start3, finish3, lambda o: ready(o[0])))

    def done1(outs):
        pair, wire = _rs_pair_add(grad, outs[0], kind, shp, pos, name=f"rs_pair_add_{name}")
        q.push(_Chunk(LAST_GROUP, 3 * rh * c * wire.dtype.itemsize / ICI_BYTES_PER_US, [wire],
                      [jax.ShapeDtypeStruct((N_CHIP - 1, 1, rh, c), wire.dtype)], {}, N_CHIP - 1,
                      start2, finish2, functools.partial(done2, pair)))

    q.push(_Chunk(LAST_GROUP, N_CHIP * rh * c * 4 / D2D_BYTES_PER_US, [grad],
                  [jax.ShapeDtypeStruct((N_CHIP, 1, rh, c), F32)], {}, N_CHIP * D2D_STREAMS, start1, finish1, done1))


N_DEV = 8


def _allgather_chunk(q, group, vec, ready):
    def peer(k, x, y, c):
        return ((1 - x) if k & 4 else x, (1 - y) if k & 2 else y, (1 - c) if k & 1 else c)

    def start(ins, outs, ss, rs, b):
        x, y, c = _mesh_pos()
        me = 4 * x + 2 * y + c
        pltpu.make_async_copy(ins[0], outs[0].at[me], ss.at[b + N_DEV - 1]).start()
        for k in range(1, N_DEV):
            _remote(ins[0], outs[0].at[me], ss, rs, b + k - 1, peer(k, x, y, c)).start()

    def finish(ins, outs, ss, rs, b):
        x, y, c = _mesh_pos()
        for k in range(1, N_DEV):
            _remote(ins[0], outs[0].at[0], ss, rs, b + k - 1, peer(k, x, y, c)).wait()
        pltpu.make_async_copy(ins[0], outs[0].at[0], ss.at[b + N_DEV - 1]).wait()

    bytes_in = (N_DEV - 2) * vec.size * 4
    q.push(_Chunk(group, bytes_in / ICI_BYTES_PER_US, [vec], [jax.ShapeDtypeStruct((N_DEV,) + vec.shape, F32)], {},
                  N_DEV, start, finish, lambda o: ready(o[0])))


def _allreduce8(vec, *, name):
    r = vec.shape[0]
    rh = r // 2

    def body(v_ref, o_ref, sib_ref, chips_ref, send_sems, recv_sems):
        x, y, c = _mesh_pos()
        sib = (x, y, 1 - c)
        me = 2 * x + y
        pair = _remote(v_ref, sib_ref, send_sems, recv_sems, 0, sib)
        pair.start()
        pair.wait()
        rows = pl.ds(pl.multiple_of(c * rh, SUBLANE), rh)
        chips_ref[me] = v_ref[rows, :] + sib_ref[rows, :]
        copies = []
        for j, (ox, oy) in enumerate(_other_chips(x, y)):
            cp = _remote(chips_ref.at[me], chips_ref.at[me], send_sems, recv_sems, 1 + j, (ox, oy, c))
            cp.start()
            copies.append(cp)
        for cp in copies:
            cp.wait()
        acc = chips_ref[0]
        for s in range(1, N_CHIP):
            acc = acc + chips_ref[s]
        o_ref[rows, :] = acc
        swap = _remote(o_ref.at[rows, :], o_ref.at[rows, :], send_sems, recv_sems, N_CHIP, sib)
        swap.start()
        swap.wait()

    vm = pl.BlockSpec(memory_space=pltpu.VMEM)
    return pl.pallas_call(
        body, in_specs=[vm], out_specs=vm, out_shape=jax.ShapeDtypeStruct((r, LANE), F32),
        scratch_shapes=[pltpu.VMEM((r, LANE), F32), pltpu.VMEM((N_CHIP, rh, LANE), F32),
                        pltpu.SemaphoreType.DMA((N_CHIP + 1,)), pltpu.SemaphoreType.DMA((N_CHIP + 1,))],
        name=name, compiler_params=pltpu.CompilerParams(has_side_effects=True, vmem_limit_bytes=VMEM_LIMIT_V7X))(vec)


def _rs_pair_add(g, recv, kind, shape, pos, *, name):
    l, r, c = shape
    assert l == 1
    rh = r // 2
    if kind == "row":
        gspec = pl.BlockSpec((None, rh, c), lambda s, pos: (0, 2 * s + pos[0], 0))
    else:
        gspec = pl.BlockSpec((None, None, rh, c), lambda s, pos: (s, 0, pos[0], 0))
    pspec = pl.BlockSpec((None, None, rh, c), lambda s, pos: (s, 0, 0, 0))

    def body(pos_ref, g_ref, r_ref, own_ref, pw_ref):
        v = g_ref[...] + r_ref[...]
        pw_ref[...] = v.astype(pw_ref.dtype)

        @pl.when(pl.program_id(0) == pos_ref[1])
        def _():
            own_ref[...] = v

    return pl.pallas_call(
        body,
        grid_spec=pltpu.PrefetchScalarGridSpec(
            num_scalar_prefetch=1, grid=(N_CHIP,), in_specs=[gspec, pspec],
            out_specs=[pl.BlockSpec((None, rh, c), lambda s, pos: (0, 0, 0)), pspec]),
        out_shape=[jax.ShapeDtypeStruct((1, rh, c), F32), jax.ShapeDtypeStruct((N_CHIP, 1, rh, c), MXU)],
        name=name, compiler_params=_cp((ARB,)))(pos, g, recv)


def _rs_chip_add(p, recv, shape, pos, *, name):
    l, r, c = shape
    rh = r // 2

    def body(pos_ref, p_ref, r_ref, o_ref):
        del pos_ref
        acc = p_ref[...]
        for j in range(N_CHIP - 1):
            acc = acc + r_ref[j].astype(F32)
        o_ref[...] = acc

    return pl.pallas_call(
        body,
        grid_spec=pltpu.PrefetchScalarGridSpec(
            num_scalar_prefetch=1, grid=(l,),
            in_specs=[pl.BlockSpec((None, rh, c), lambda i, pos: (i, 0, 0)),
                      pl.BlockSpec((N_CHIP - 1, None, rh, c), lambda i, pos: (0, i, 0, 0))],
            out_specs=pl.BlockSpec((None, rh, c), lambda i, pos: (i, pos[0], 0))),
        out_shape=jax.ShapeDtypeStruct((l, r, c), F32),
        name=name, compiler_params=_cp((PAR,)))(pos, p, recv)


ADAM_BLOCK_ELEMS = 384 * 1024


def _adam_math(w, g, m, v):
    c1 = 1.0 / (1.0 - ADAM_B1 ** ADAM_STEP)
    c2 = 1.0 / (1.0 - ADAM_B2 ** ADAM_STEP)
    nm = ADAM_B1 * m + (1.0 - ADAM_B1) * g
    nv = ADAM_B2 * v + (1.0 - ADAM_B2) * (g * g)
    return -ADAM_LR * ((nm * c1) / (jnp.sqrt(nv * c2) + ADAM_EPS) + ADAM_WD * w), nm, nv


def _adamw_layer(w, g, m, v, outs, l, *, name):
    _, r, c = w.shape
    tr = _tile(r, max(SUBLANE, ADAM_BLOCK_ELEMS // c // SUBLANE * SUBLANE), SUBLANE)

    def body(w_ref, g_ref, m_ref, v_ref, *rest):
        go_ref, d_ref, nm_ref, nv_ref = rest[4:]
        gg = g_ref[...]
        go_ref[...] = gg
        d_ref[...], nm_ref[...], nv_ref[...] = _adam_math(w_ref[...], gg, m_ref[...], v_ref[...])

    lay = pl.BlockSpec((None, tr, c), lambda j: (l, j, 0))
    hbm = pl.BlockSpec(memory_space=pl.ANY)
    return pl.pallas_call(
        body, grid=(r // tr,),
        in_specs=[lay, pl.BlockSpec((None, tr, c), lambda j: (0, j, 0)), lay, lay] + [hbm] * 4,
        out_specs=[lay] * 4, out_shape=[jax.ShapeDtypeStruct(w.shape, F32)] * 4,
        input_output_aliases={4 + i: i for i in range(4)},
        name=name, compiler_params=_cp((PAR,)))(w, g, m, v, *outs)


def _adamw(w, g, m, v, *, name):
    shape = w.shape
    if w.ndim == 2:
        w, g, m, v = (a[None] for a in (w, g, m, v))
    l, r, c = w.shape
    tr = _tile(r, max(SUBLANE, ADAM_BLOCK_ELEMS // c // SUBLANE * SUBLANE), SUBLANE)

    def body(w_ref, g_ref, m_ref, v_ref, d_ref, nm_ref, nv_ref):
        d_ref[...], nm_ref[...], nv_ref[...] = _adam_math(w_ref[...], g_ref[...], m_ref[...], v_ref[...])

    spec = pl.BlockSpec((None, tr, c), lambda i, j: (i, j, 0))
    outs = pl.pallas_call(
        body, grid=(l, r // tr), in_specs=[spec] * 4, out_specs=[spec] * 3,
        out_shape=[jax.ShapeDtypeStruct((l, r, c), F32)] * 3,
        name=name, compiler_params=_cp((PAR, PAR)))(w, g, m, v)
    return tuple(o.reshape(shape) for o in outs)


PACK_ROWS = 2 * SUBLANE * LANE


def _pack(arrays):
    flat = jnp.concatenate([a.reshape(-1).astype(F32) for a in arrays])
    pad = (-flat.shape[0]) % PACK_ROWS
    return jnp.pad(flat, (0, pad)).reshape(-1, LANE)


def _unpack(packed, shapes):
    flat = packed.reshape(-1)
    out, off = [], 0
    for s in shapes:
        size = int(np.prod(s))
        out.append(flat[off:off + size].reshape(s))
        off += size
    return out


BIG = (("w_mem_kv", "row"), ("w_mix_out", "row"), ("w_ffn_up", "slot"), ("w_ffn_down", "row"),
       ("w_in_a", "slot"), ("w_in_b", "row"), ("w_kv", "row"))
COLUMN_SHARDED_AS_COLUMNS = ("w_in_a",)
SMALL_SHARDED = (("w_ffn_conv", 2), ("w_conv_a", 2), ("b_conv_a", 1), ("lru_lambda", 1))
SMALL_REPLICATED = ("g_mix_pre", "g_mix_post", "g_ffn_pre", "g_ffn_post", "g_mem", "b_ffn_conv",
                    "w_rg_r", "b_rg_r", "w_rg_i", "b_rg_i", "sinks_b", "g_kv")
WEIGHTS = ("g_mix_pre", "g_mix_post", "g_ffn_pre", "g_ffn_post", "g_mem", "w_mem_kv", "w_mix_out", "w_ffn_up",
           "w_ffn_conv", "b_ffn_conv", "w_ffn_down", "w_in_a", "w_conv_a", "b_conv_a", "w_rg_r", "b_rg_r", "w_rg_i",
           "b_rg_i", "lru_lambda", "w_in_b", "sinks_b", "g_kv", "w_kv")


def _slot_to_cols(a):
    s, l, r, c = a.shape
    return a.transpose(1, 2, 0, 3).reshape(l, r, s * c)


def _cols_to_slot(a):
    l, r, c4 = a.shape
    return a.reshape(l, r, N_CHIP, c4 // N_CHIP).transpose(2, 0, 1, 3)


GROUPS_PER_LAYER = 8


def _layer_weights(layer):
    names = [("w_mem_kv", layer), ("w_in_a", layer) if layer < N_A else ("w_in_b", layer - N_A)]
    if layer == N_A:
        names.append(("w_kv", 0))
    return names + [("w_mix_out", layer), ("w_ffn_up", layer), ("w_ffn_down", layer)]


def _train_step(x, mem, target, w, m, v):
    xi, yi, ci = _mesh_pos()
    chip = 2 * xi + yi
    pos = jnp.stack([ci, chip]).astype(jnp.int32)

    q = _CommQueue()
    kinds = dict(BIG)
    as3 = lambda a: a if a.ndim == 3 else a[None]
    w3, m3, v3 = ({k: as3(d[k]) for k, _ in BIG} for d in (w, m, v))
    shards = {k: w3[k].astype(MXU) for k, _ in BIG}

    gathered = {}

    def on_gathered(k, l, full):
        gathered[k, l] = _slot_to_cols(full) if k in COLUMN_SHARDED_AS_COLUMNS else full

    group_of = {}

    for layer in range(DEPTH):
        for i, (k, l) in enumerate(_layer_weights(layer)):
            group_of[k, l] = layer * GROUPS_PER_LAYER + i
            _gather_chunks(q, group_of[k, l], kinds[k], shards[k], l, functools.partial(on_gathered, k, l))

    def wfull(k, l):
        if (k, l) not in gathered:
            q.flush(group_of[k, l])
        return gathered[k, l]

    small = {}
    _allgather_chunk(q, 0, _pack([w[k] for k, _ in SMALL_SHARDED]), functools.partial(small.__setitem__, "stacked"))
    q.flush(1)

    big_out = {k: [lax.empty(w3[k].shape, F32) for _ in range(4)] for k, _ in BIG}

    def on_reduced(k, l, g):
        big_out[k] = _adamw_layer(w3[k], g, m3[k], v3[k], big_out[k], l, name=f"adamw_{k}{l}")

    def push_grad(k, l, g):
        if k in COLUMN_SHARDED_AS_COLUMNS:
            g = _cols_to_slot(g)
        _reduce_scatter_chunks(q, kinds[k], g, (1,) + w3[k].shape[1:], pos, f"{k}{l}", functools.partial(on_reduced, k, l))

    small_shapes = [w[k].shape for k, _ in SMALL_SHARDED]
    per_chip = [_unpack(small["stacked"][2 * s], small_shapes) for s in range(N_CHIP)]
    p = {k: w[k] for k in SMALL_REPLICATED}
    for i, (k, axis) in enumerate(SMALL_SHARDED):
        p[k] = jnp.concatenate([per_chip[s][i] for s in range(N_CHIP)], axis=axis)

    sq, grad_x, g = _local_step(x, mem, target, p, wfull, push_grad, q)
    loss = lax.psum(0.5 * sq / D_MODEL, ("x", "y", "c"))
    q.flush()

    small_names = [k for k, _ in SMALL_SHARDED] + list(SMALL_REPLICATED)
    summed = _allreduce8(_pack([g[k] for k in small_names]), name="allreduce_small")
    gsum = dict(zip(small_names, _unpack(summed, [p[k].shape for k in small_names])))
    for k, axis in SMALL_SHARDED:
        gsum[k] = lax.dynamic_slice_in_dim(gsum[k], chip * w[k].shape[axis], w[k].shape[axis], axis)

    delta, new_m, new_v = {}, {}, {}
    for k, _ in BIG:
        gsum[k], delta[k], new_m[k], new_v[k] = (o.reshape(w[k].shape) for o in big_out[k])
    for k in small_names:
        as2 = lambda a: a.reshape(-1, a.shape[-1])
        outs = _adamw(as2(w[k]), as2(gsum[k]), as2(m[k]), as2(v[k]), name=f"adamw_{k}")
        delta[k], new_m[k], new_v[k] = (o.reshape(w[k].shape) for o in outs)
    return (loss, grad_x, *[gsum[k] for k in WEIGHTS], *[delta[k] for k in WEIGHTS],
            *[new_m[k] for k in WEIGHTS], *[new_v[k] for k in WEIGHTS])


def kernel(x, mem, g_mix_pre, g_mix_post, g_ffn_pre, g_ffn_post, g_mem, w_mem_kv, w_mix_out, w_ffn_up, w_ffn_conv, b_ffn_conv, w_ffn_down, w_in_a, w_conv_a, b_conv_a, w_rg_r, b_rg_r, w_rg_i, b_rg_i, lru_lambda, w_in_b, sinks_b, g_kv, w_kv, loss_target, m_g_mix_pre, m_g_mix_post, m_g_ffn_pre, m_g_ffn_post, m_g_mem, m_w_mem_kv, m_w_mix_out, m_w_ffn_up, m_w_ffn_conv, m_b_ffn_conv, m_w_ffn_down, m_w_in_a, m_w_conv_a, m_b_conv_a, m_w_rg_r, m_b_rg_r, m_w_rg_i, m_b_rg_i, m_lru_lambda, m_w_in_b, m_sinks_b, m_g_kv, m_w_kv, v_g_mix_pre, v_g_mix_post, v_g_ffn_pre, v_g_ffn_post, v_g_mem, v_w_mem_kv, v_w_mix_out, v_w_ffn_up, v_w_ffn_conv, v_b_ffn_conv, v_w_ffn_down, v_w_in_a, v_w_conv_a, v_b_conv_a, v_w_rg_r, v_b_rg_r, v_w_rg_i, v_b_rg_i, v_lru_lambda, v_w_in_b, v_sinks_b, v_g_kv, v_w_kv):
    args = (g_mix_pre, g_mix_post, g_ffn_pre, g_ffn_post, g_mem, w_mem_kv, w_mix_out, w_ffn_up, w_ffn_conv, b_ffn_conv, w_ffn_down, w_in_a, w_conv_a, b_conv_a, w_rg_r, b_rg_r, w_rg_i, b_rg_i, lru_lambda, w_in_b, sinks_b, g_kv, w_kv)
    ms = (m_g_mix_pre, m_g_mix_post, m_g_ffn_pre, m_g_ffn_post, m_g_mem, m_w_mem_kv, m_w_mix_out, m_w_ffn_up, m_w_ffn_conv, m_b_ffn_conv, m_w_ffn_down, m_w_in_a, m_w_conv_a, m_b_conv_a, m_w_rg_r, m_b_rg_r, m_w_rg_i, m_b_rg_i, m_lru_lambda, m_w_in_b, m_sinks_b, m_g_kv, m_w_kv)
    vs = (v_g_mix_pre, v_g_mix_post, v_g_ffn_pre, v_g_ffn_post, v_g_mem, v_w_mem_kv, v_w_mix_out, v_w_ffn_up, v_w_ffn_conv, v_b_ffn_conv, v_w_ffn_down, v_w_in_a, v_w_conv_a, v_b_conv_a, v_w_rg_r, v_b_rg_r, v_w_rg_i, v_b_rg_i, v_lru_lambda, v_w_in_b, v_sinks_b, v_g_kv, v_w_kv)
    return _train_step(x, mem, loss_target, dict(zip(WEIGHTS, args)), dict(zip(WEIGHTS, ms)), dict(zip(WEIGHTS, vs)))
```

```python
import functools
import math

import numpy as np
import jax
import jax.numpy as jnp
from jax import lax
from jax.experimental import pallas as pl
from jax.experimental.pallas import tpu as pltpu

F32 = jnp.float32
MXU = jnp.bfloat16

D_MODEL = 1024
HEAD_DIM = 64
MEM_LEN = 256
MEM_HEADS = 4
MEM_WIDTH = MEM_HEADS * HEAD_DIM
MIX_WIDTH = D_MODEL - MEM_WIDTH
LRU_BLOCKS = MIX_WIDTH // HEAD_DIM
LRU_CONV = 4
LRU_C = 8.0
SWA_HEADS = MIX_WIDTH // HEAD_DIM
SWA_KV_HEADS = 4
SWA_GROUP = SWA_HEADS // SWA_KV_HEADS
WINDOW = 128
D_FF = 2816
FFN_CONV = 3
EPS = 1e-6
DEPTH = 4
N_A = 2

ADAM_LR = 0.001
ADAM_B1 = 0.9
ADAM_B2 = 0.999
ADAM_EPS = 1e-08
ADAM_WD = 0.01
ADAM_STEP = 10

VMEM_LIMIT_V7X = 56 * 1024 * 1024
LANE = 128
SUBLANE = 8
GATE_TILE = 256
MESH_T = pl.DeviceIdType.MESH


def _alibi_slopes(n):
    def pow2_slopes(m):
        start = 2.0 ** (-8.0 / m)
        return [start ** (i + 1) for i in range(m)]
    c = 2 ** int(math.floor(math.log2(n)))
    s = pow2_slopes(c)
    if c != n:
        s = s + pow2_slopes(2 * c)[0::2][: n - c]
    return [float(np.float32(v)) for v in s]


SLOPES = _alibi_slopes(SWA_HEADS)


def _tile(n, cap, mult=LANE):
    best = None
    for t in range(mult, min(n, cap) + 1, mult):
        if n % t == 0:
            best = t
    return best if best is not None else n


def _cp(sem):
    return pltpu.CompilerParams(dimension_semantics=sem, vmem_limit_bytes=VMEM_LIMIT_V7X)


MM_VMEM_BUDGET = 40 * 1024 * 1024
HBM_BYTES_PER_US_V7X = 3.0e6
GRID_STEP_US = 0.35


def _divisors(n, mult):
    return [t for t in range(mult, n + 1, mult) if n % t == 0] or [n]


def _mm_tiles(m, k, n, out_bytes):
    best = None
    for tm in _divisors(m, 256):
        for tn in _divisors(n, LANE):
            vmem = 2 * (tm * k * 2 + k * tn * 2 + tm * tn * out_bytes)
            if vmem > MM_VMEM_BUDGET:
                continue
            steps = (m // tm) * (n // tn)
            b_reads = 1 if tn == n else m // tm
            traffic = m * k * 2 + k * n * 2 * b_reads + m * n * out_bytes
            first = tm * k * 2 + k * tn * 2
            cost = (traffic + first) / HBM_BYTES_PER_US_V7X + steps * GRID_STEP_US
            if best is None or cost < best[0]:
                best = (cost, tm, tn)
    return best[1], best[2]


def _mm_tn_tiles(k, m, n, whole_n=False):
    best = None
    for tm in _divisors(m, LANE):
        for tn in ([n] if whole_n else _divisors(n, LANE)):
            for tk in _divisors(k, 512):
                vmem = 2 * (tk * tm * 2 + tk * tn * 2 + tm * tn * 4)
                if vmem > MM_VMEM_BUDGET:
                    continue
                steps = (m // tm) * (n // tn) * (k // tk)
                traffic = k * m * 2 * (n // tn) + k * n * 2 * (m // tm) + m * n * 4
                cost = traffic / HBM_BYTES_PER_US_V7X + steps * GRID_STEP_US
                if best is None or cost < best[0]:
                    best = (cost, tk, tm, tn)
    return best[1], best[2], best[3]


ARB = "arbitrary"
PAR = "parallel"


def _rms_fwd(x, g):
    r = lax.rsqrt(jnp.mean(x * x, axis=-1, keepdims=True) + EPS)
    return x * r * g


def _rms_bwd(dy, x, g):
    r = lax.rsqrt(jnp.mean(x * x, axis=-1, keepdims=True) + EPS)
    xh = x * r
    gdy = dy * g
    dx = r * (gdy - xh * jnp.mean(gdy * xh, axis=-1, keepdims=True))
    dg = jnp.sum(dy * xh, axis=0, keepdims=True)
    return dx, dg


_GELU_K = math.sqrt(2.0 / math.pi)
_GELU_C = 0.044715


def _gelu(x):
    t = jnp.tanh(_GELU_K * (x + _GELU_C * x * x * x))
    return 0.5 * x * (1.0 + t)


def _gelu_and_grad(x):
    x2 = x * x
    u = 0.5 * jnp.tanh(x * (_GELU_K + (_GELU_K * _GELU_C) * x2)) + 0.5
    dz2 = (6.0 * _GELU_K * _GELU_C) * x2 + 2.0 * _GELU_K
    return x * u, u * ((x * (1.0 - u)) * dz2 + 1.0)


def _shift_down(x, k, row):
    return jnp.where(row >= k, pltpu.roll(x, k, axis=0), 0.0)


def _shift_up(x, k, row):
    n = x.shape[0]
    return jnp.where(row < n - k, pltpu.roll(x, n - k, axis=0), 0.0)


def _shift_down_edge(x, k):
    r = pltpu.roll(x, k, axis=0)
    row = lax.broadcasted_iota(jnp.int32, (SUBLANE, x.shape[1]), 0)
    return jnp.concatenate([jnp.where(row >= k, r[:SUBLANE], 0.0), r[SUBLANE:]], axis=0)


def _shift_up_edge(x, k):
    n = x.shape[0]
    r = pltpu.roll(x, n - k, axis=0)
    row = lax.broadcasted_iota(jnp.int32, (SUBLANE, x.shape[1]), 0)
    return jnp.concatenate([r[:n - SUBLANE], jnp.where(row < SUBLANE - k, r[n - SUBLANE:], 0.0)], axis=0)


def _dot(a, b):
    return jnp.dot(a, b, preferred_element_type=F32)


def _dot_nt(a, b):
    return lax.dot_general(a, b, (((1,), (1,)), ((), ())), preferred_element_type=F32)


def _dot_tn(a, b):
    return lax.dot_general(a, b, (((0,), (0,)), ((), ())), preferred_element_type=F32)


MXU_FLOPS_PER_US = 7.0e8
HOST_US = {"lru_fwd": 44.0, "lru_bwd": 94.0, "swa_fwd": 60.0, "swa_bwd": 160.0, "mem_attn_fwd": 21.0,
           "mem_attn_bwd": 33.0, "ffn_act": 70.0, "ffn_act_bwd": 100.0, "resid": 22.0, "resid_bwd": 33.0}


def _hosted_call(body, *, grid, in_specs, out_specs, out_shape, args, name, aliases=None, scratch_shapes=(),
                 q=None, flops=0.0, budget_us=0.0):
    chunks = q.take(flops / MXU_FLOPS_PER_US + budget_us) if q is not None else []
    if not chunks:
        return pl.pallas_call(
            body, grid=grid, in_specs=in_specs, out_specs=out_specs, out_shape=out_shape,
            scratch_shapes=list(scratch_shapes), input_output_aliases=aliases or {}, name=name,
            compiler_params=_cp((ARB,) * len(grid)))(*args)
    single = not isinstance(out_shape, (list, tuple))
    o_shapes = [out_shape] if single else list(out_shape)
    o_specs = [out_specs] if single else list(out_specs)
    n_in, n_out, n_scr = len(args), len(o_shapes), len(scratch_shapes)
    c_ins = [a for ch in chunks for a in ch.ins]
    c_outs = [s for ch in chunks for s in ch.out_shapes]
    alias = dict(aliases or {})
    in_off, out_off, sem_off = [], [], []
    i0 = o0 = s0 = 0
    for ch in chunks:
        in_off.append(i0)
        out_off.append(o0)
        sem_off.append(s0)
        for ci, co in ch.alias.items():
            alias[n_in + i0 + ci] = n_out + o0 + co
        i0 += len(ch.ins)
        o0 += len(ch.out_shapes)
        s0 += ch.n_sem

    def wrapped(*refs):
        ins = refs[:n_in]
        cin = refs[n_in:n_in + i0]
        outs = refs[n_in + i0:n_in + i0 + n_out]
        cout = refs[n_in + i0 + n_out:n_in + i0 + n_out + o0]
        scr = refs[n_in + i0 + n_out + o0:n_in + i0 + n_out + o0 + n_scr]
        send_sems, recv_sems = refs[n_in + i0 + n_out + o0 + n_scr:]
        first = functools.reduce(lambda u, v: u & v, [pl.program_id(d) == 0 for d in range(len(grid))])
        last = functools.reduce(lambda u, v: u & v, [pl.program_id(d) == grid[d] - 1 for d in range(len(grid))])

        def each(phase):
            for ch, a, b, s in zip(chunks, in_off, out_off, sem_off):
                getattr(ch, phase)(cin[a:a + len(ch.ins)], cout[b:b + len(ch.out_shapes)], send_sems, recv_sems, s)

        pl.when(first)(lambda: each("start"))
        body(*ins, *outs, *scr)
        pl.when(last)(lambda: each("finish"))

    hbm = pl.BlockSpec(memory_space=pl.ANY)
    res = pl.pallas_call(
        wrapped, grid=grid, in_specs=list(in_specs) + [hbm] * i0, out_specs=o_specs + [hbm] * o0,
        out_shape=o_shapes + c_outs,
        scratch_shapes=list(scratch_shapes) + [pltpu.SemaphoreType.DMA((s0,)), pltpu.SemaphoreType.DMA((s0,))],
        input_output_aliases=alias, name=name,
        compiler_params=pltpu.CompilerParams(dimension_semantics=(ARB,) * len(grid), vmem_limit_bytes=VMEM_LIMIT_V7X,
                                             has_side_effects=True))(*args, *c_ins)
    for ch, b in zip(chunks, out_off):
        ch.done(list(res[n_out + b:n_out + b + len(ch.out_shapes)]))
    return res[0] if single else list(res[:n_out])


def _mm_nn(a, b, *, name, q=None, out_dtype=F32):
    m, k = a.shape
    n = b.shape[-1]
    tm, tn = _mm_tiles(m, k, n, jnp.dtype(out_dtype).itemsize)

    def body(a_ref, b_ref, o_ref):
        o_ref[...] = _dot(a_ref[...], b_ref[...]).astype(o_ref.dtype)

    return _hosted_call(
        body, grid=(m // tm, n // tn),
        in_specs=[pl.BlockSpec((tm, k), lambda i, j: (i, 0)),
                  pl.BlockSpec((None, k, tn), lambda i, j: (0, 0, j))],
        out_specs=pl.BlockSpec((tm, tn), lambda i, j: (i, j)),
        out_shape=jax.ShapeDtypeStruct((m, n), out_dtype),
        args=(a, b), name=name, q=q, flops=2.0 * m * k * n)


def _mm_nt(a, b, *, name, q=None, out_dtype=F32):
    m, k = a.shape
    n = b.shape[-2]
    tm, tn = _mm_tiles(m, k, n, jnp.dtype(out_dtype).itemsize)

    def body(a_ref, b_ref, o_ref):
        o_ref[...] = _dot_nt(a_ref[...], b_ref[...]).astype(o_ref.dtype)

    return _hosted_call(
        body, grid=(m // tm, n // tn),
        in_specs=[pl.BlockSpec((tm, k), lambda i, j: (i, 0)),
                  pl.BlockSpec((None, tn, k), lambda i, j: (0, j, 0))],
        out_specs=pl.BlockSpec((tm, tn), lambda i, j: (i, j)),
        out_shape=jax.ShapeDtypeStruct((m, n), out_dtype),
        args=(a, b), name=name, q=q, flops=2.0 * m * k * n)


def _mm_nn_slots(a, b4, *, name, q=None, out_dtype=F32):
    m, k = a.shape
    s_, _, _, c = b4.shape
    ob = jnp.dtype(out_dtype).itemsize
    tm = max(t for t in _divisors(m, 256) if 2 * (t * k * 2 + k * c * 2 + t * c * ob) <= MM_VMEM_BUDGET)

    def body(a_ref, b_ref, o_ref):
        o_ref[...] = _dot(a_ref[...], b_ref[...]).astype(o_ref.dtype)

    return _hosted_call(
        body, grid=(m // tm, s_),
        in_specs=[pl.BlockSpec((tm, k), lambda i, j: (i, 0)),
                  pl.BlockSpec((None, None, k, c), lambda i, j: (j, 0, 0, 0))],
        out_specs=pl.BlockSpec((tm, c), lambda i, j: (i, j)),
        out_shape=jax.ShapeDtypeStruct((m, s_ * c), out_dtype),
        args=(a, b4), name=name, q=q, flops=2.0 * m * k * s_ * c)


def _mm_tn_slots(a, b, *, name, slot_cols, n_slots, first_slot=0, q=None, out=None):
    k, m = a.shape
    c = slot_cols
    tk, tm, _ = _mm_tn_tiles(k, m, c, whole_n=True)

    def body(a_ref, b_ref, *rest):
        o_ref = rest[-1]
        part = _dot_tn(a_ref[...], b_ref[...])

        @pl.when(pl.program_id(2) == 0)
        def _():
            o_ref[...] = part

        @pl.when(pl.program_id(2) > 0)
        def _():
            o_ref[...] += part

    in_specs = [pl.BlockSpec((tk, tm), lambda i, j, s: (s, i)), pl.BlockSpec((tk, c), lambda i, j, s: (s, j))]
    args = (a, b)
    if out is not None:
        in_specs.append(pl.BlockSpec(memory_space=pl.ANY))
        args = (a, b, out)
    return _hosted_call(
        body, grid=(m // tm, b.shape[-1] // c, k // tk), in_specs=in_specs,
        out_specs=pl.BlockSpec((None, None, tm, c), lambda i, j, s: (first_slot + j, 0, i, 0)),
        out_shape=jax.ShapeDtypeStruct((n_slots, 1, m, c), F32),
        aliases={2: 0} if out is not None else None,
        args=args, name=name, q=q, flops=2.0 * m * k * b.shape[-1])


def _mm_tn(a, b, *, name, q=None, out=None, n_total=None, col_block_offset=0):
    k, m = a.shape
    n = b.shape[-1]
    tk, tm, tn = _mm_tn_tiles(k, m, n)
    off = col_block_offset * (n // tn)

    def body(a_ref, b_ref, *rest):
        o_ref = rest[-1]
        part = _dot_tn(a_ref[...], b_ref[...])

        @pl.when(pl.program_id(2) == 0)
        def _():
            o_ref[...] = part

        @pl.when(pl.program_id(2) > 0)
        def _():
            o_ref[...] += part

    in_specs = [pl.BlockSpec((tk, tm), lambda i, j, s: (s, i)), pl.BlockSpec((tk, tn), lambda i, j, s: (s, j))]
    args = (a, b)
    if out is not None:
        in_specs.append(pl.BlockSpec(memory_space=pl.ANY))
        args = (a, b, out)
    return _hosted_call(
        body, grid=(m // tm, n // tn, k // tk), in_specs=in_specs,
        out_specs=pl.BlockSpec((None, tm, tn), lambda i, j, s: (0, i, j + off)),
        out_shape=jax.ShapeDtypeStruct((1, m, n_total or n), F32),
        aliases={2: 0} if out is not None else None,
        args=args, name=name, q=q, flops=2.0 * m * k * n)


def _mm_ffn_dh(dg, dv, w4, *, name, q=None):
    m, f = dg.shape
    n_slots, _, d, c = w4.shape
    tm, tn = _mm_tiles(m, 2 * f, d, 4)

    def body(dg_ref, dv_ref, *rest):
        w_refs, o_ref = rest[:n_slots], rest[n_slots]
        acc = None
        for s, w_ref in enumerate(w_refs):
            x_ref = dg_ref if s < n_slots // 2 else dv_ref
            off = (s % (n_slots // 2)) * c
            part = _dot_nt(x_ref[:, off:off + c], w_ref[...])
            acc = part if acc is None else acc + part
        o_ref[...] = acc.astype(o_ref.dtype)

    wspec = lambda s: pl.BlockSpec((None, None, tn, c), lambda i, j: (s, 0, j, 0))
    return _hosted_call(
        body, grid=(m // tm, d // tn),
        in_specs=[pl.BlockSpec((tm, f), lambda i, j: (i, 0)),
                  pl.BlockSpec((tm, f), lambda i, j: (i, 0))] + [wspec(s) for s in range(n_slots)],
        out_specs=pl.BlockSpec((tm, tn), lambda i, j: (i, j)),
        out_shape=jax.ShapeDtypeStruct((m, d), MXU),
        args=(dg, dv) + (w4,) * n_slots, name=name, q=q, flops=4.0 * m * f * d)


ROW_TILE = 512


def _norm_fwd(x, g, *, name):
    n, d = x.shape
    tm = _tile(n, ROW_TILE, SUBLANE)

    def body(x_ref, g_ref, o_ref):
        o_ref[...] = _rms_fwd(x_ref[...], g_ref[...]).astype(o_ref.dtype)

    return pl.pallas_call(
        body, grid=(n // tm,),
        in_specs=[pl.BlockSpec((tm, d), lambda i: (i, 0)), pl.BlockSpec((1, d), lambda i: (0, 0))],
        out_specs=pl.BlockSpec((tm, d), lambda i: (i, 0)),
        out_shape=jax.ShapeDtypeStruct((n, d), MXU),
        name=name, compiler_params=_cp((PAR,)))(x, g)


def _norm_bwd_dg(dy, x, g, *, name):
    n, d = x.shape
    tm = _tile(n, ROW_TILE, SUBLANE)

    def body(dy_ref, x_ref, g_ref, dg_ref):
        @pl.when(pl.program_id(0) == 0)
        def _():
            dg_ref[...] = jnp.zeros_like(dg_ref)
        _, dg = _rms_bwd(dy_ref[...], x_ref[...], g_ref[...])
        dg_ref[...] += dg

    return pl.pallas_call(
        body, grid=(n // tm,),
        in_specs=[pl.BlockSpec((tm, d), lambda i: (i, 0)), pl.BlockSpec((tm, d), lambda i: (i, 0)),
                  pl.BlockSpec((1, d), lambda i: (0, 0))],
        out_specs=pl.BlockSpec((1, d), lambda i: (0, 0)),
        out_shape=jax.ShapeDtypeStruct((1, d), F32),
        name=name, compiler_params=_cp((ARB,)))(dy, x, g)


def _resid_norm_fwd(x, y, g_post, g_pres, *, name, q=None):
    n, d = x.shape
    tm = _tile(n, ROW_TILE, SUBLANE)
    nh = len(g_pres)

    def body(x_ref, y_ref, gp_ref, *rest):
        gpre = rest[:nh]
        xo_ref = rest[nh]
        h_refs = rest[nh + 1:]
        xo = x_ref[...] + _rms_fwd(y_ref[...].astype(F32), gp_ref[...])
        xo_ref[...] = xo
        for g_ref, h_ref in zip(gpre, h_refs):
            h_ref[...] = _rms_fwd(xo, g_ref[...]).astype(h_ref.dtype)

    row = pl.BlockSpec((tm, d), lambda i: (i, 0))
    vec = pl.BlockSpec((1, d), lambda i: (0, 0))
    outs = _hosted_call(
        body, grid=(n // tm,),
        in_specs=[row, row, vec] + [vec] * nh,
        out_specs=[row] + [row] * nh,
        out_shape=[jax.ShapeDtypeStruct((n, d), F32)] + [jax.ShapeDtypeStruct((n, d), MXU)] * nh,
        args=(x, y, g_post, *g_pres), name=name, q=q, budget_us=HOST_US["resid"])
    return outs[0], list(outs[1:])


def _loss_fwd(x, y, g_post, target, *, name):
    n, d = x.shape
    tm = _tile(n, ROW_TILE, SUBLANE)

    def body(x_ref, y_ref, gp_ref, t_ref, dx_ref, sq_ref):
        @pl.when(pl.program_id(0) == 0)
        def _():
            sq_ref[...] = jnp.zeros_like(sq_ref)
        err = x_ref[...] + _rms_fwd(y_ref[...].astype(F32), gp_ref[...]) - t_ref[...]
        dx_ref[...] = err * (1.0 / d)
        sq_ref[...] += jnp.sum(err * err, axis=0, keepdims=True)

    row = pl.BlockSpec((tm, d), lambda i: (i, 0))
    vec = pl.BlockSpec((1, d), lambda i: (0, 0))
    return pl.pallas_call(
        body, grid=(n // tm,),
        in_specs=[row, row, vec, row],
        out_specs=[row, vec],
        out_shape=[jax.ShapeDtypeStruct((n, d), F32), jax.ShapeDtypeStruct((1, d), F32)],
        name=name, compiler_params=_cp((ARB,)))(x, y, g_post, target)


def _resid_norm_bwd(dx_out, dhs, x_out, g_pres, y, g_post, *, name, q=None):
    n, d = dx_out.shape
    tm = _tile(n, ROW_TILE, SUBLANE)
    nh = len(dhs)
    has_y = y is not None

    def body(*refs):
        it = iter(refs)
        dxo_ref = next(it)
        dh_refs = [next(it) for _ in range(nh)]
        xo_ref = next(it) if nh else None
        gpre_refs = [next(it) for _ in range(nh)]
        y_ref = next(it) if has_y else None
        gpost_ref = next(it) if has_y else None
        g_out = next(it)
        dy_out = next(it) if has_y else None
        dgpre_out = [next(it) for _ in range(nh)]
        dgpost_out = next(it) if has_y else None

        @pl.when(pl.program_id(0) == 0)
        def _():
            for r in dgpre_out:
                r[...] = jnp.zeros_like(r)
            if has_y:
                dgpost_out[...] = jnp.zeros_like(dgpost_out)

        g = dxo_ref[...]
        if nh:
            xo = xo_ref[...]
            for dh_ref, gp_ref, dg_ref in zip(dh_refs, gpre_refs, dgpre_out):
                dx, dg = _rms_bwd(dh_ref[...].astype(F32), xo, gp_ref[...])
                g = g + dx
                dg_ref[...] += dg
        g_out[...] = g
        if has_y:
            dy, dg = _rms_bwd(g, y_ref[...].astype(F32), gpost_ref[...])
            dy_out[...] = dy.astype(dy_out.dtype)
            dgpost_out[...] += dg

    row = pl.BlockSpec((tm, d), lambda i: (i, 0))
    vec = pl.BlockSpec((1, d), lambda i: (0, 0))
    ins, in_specs = [dx_out], [row]
    ins += list(dhs)
    in_specs += [row] * nh
    if nh:
        ins.append(x_out)
        in_specs.append(row)
    ins += list(g_pres)
    in_specs += [vec] * nh
    if has_y:
        ins += [y, g_post]
        in_specs += [row, vec]
    out_specs, out_shape = [row], [jax.ShapeDtypeStruct((n, d), F32)]
    if has_y:
        out_specs.append(row)
        out_shape.append(jax.ShapeDtypeStruct((n, d), MXU))
    out_specs += [vec] * nh
    out_shape += [jax.ShapeDtypeStruct((1, d), F32)] * nh
    if has_y:
        out_specs.append(vec)
        out_shape.append(jax.ShapeDtypeStruct((1, d), F32))
    outs = list(_hosted_call(
        body, grid=(n // tm,), in_specs=in_specs, out_specs=out_specs, out_shape=out_shape,
        args=tuple(ins), name=name, q=q, budget_us=HOST_US["resid_bwd"]))
    g = outs.pop(0)
    dy = outs.pop(0) if has_y else None
    dgpre = [outs.pop(0) for _ in range(nh)]
    dgpost = outs.pop(0) if has_y else None
    return g, dy, dgpre, dgpost


def _ffn_conv(up, w_ref, b_ref):
    return (w_ref[0:1, :] * _shift_down_edge(up, 2) + w_ref[1:2, :] * _shift_down_edge(up, 1)
            + w_ref[2:3, :] * up + b_ref[...])


def _ffn_act_fwd(up, wconv, bconv, bsz, *, name, q=None):
    n, f2 = up.shape
    f = f2 // 2
    t = n // bsz
    tc = _tile(f, 256)
    nf = f // tc

    def body(ug_ref, uv_ref, wg_ref, wv_ref, bg_ref, bv_ref, o_ref, dag_ref, dav_ref):
        g = _ffn_conv(ug_ref[...].astype(F32), wg_ref, bg_ref)
        v = _ffn_conv(uv_ref[...].astype(F32), wv_ref, bv_ref)
        gl, dgl = _gelu_and_grad(g)
        dag_ref[...] = (v * dgl).astype(dag_ref.dtype)
        dav_ref[...] = gl.astype(dav_ref.dtype)
        o_ref[...] = (gl * v).astype(o_ref.dtype)

    blk = pl.BlockSpec((t, tc), lambda b, j: (b, j))
    return _hosted_call(
        body, grid=(bsz, nf),
        in_specs=[blk, pl.BlockSpec((t, tc), lambda b, j: (b, j + nf)),
                  pl.BlockSpec((FFN_CONV, tc), lambda b, j: (0, j)),
                  pl.BlockSpec((FFN_CONV, tc), lambda b, j: (0, j + nf)),
                  pl.BlockSpec((1, tc), lambda b, j: (0, j)),
                  pl.BlockSpec((1, tc), lambda b, j: (0, j + nf))],
        out_specs=[blk, blk, blk],
        out_shape=[jax.ShapeDtypeStruct((n, f), MXU)] * 3,
        args=(up, up, wconv, wconv, bconv, bconv), name=name, q=q, budget_us=HOST_US["ffn_act"])


def _ffn_act_bwd(up, ug, uv, dact, wconv, bsz, *, name, q=None):
    n, f2 = up.shape
    f = f2 // 2
    t = n // bsz
    tc = _tile(f, 256)
    nf = f // tc

    def body(xg_ref, xv_ref, g_ref, v_ref, da_ref, wg_ref, wv_ref,
             dug_ref, duv_ref, dwg_ref, dwv_ref, dbg_ref, dbv_ref):
        @pl.when(pl.program_id(1) == 0)
        def _():
            for r in (dwg_ref, dwv_ref, dbg_ref, dbv_ref):
                r[...] = jnp.zeros_like(r)

        da = da_ref[...].astype(F32)
        dg = da * g_ref[...].astype(F32)
        dv = da * v_ref[...].astype(F32)

        def conv_bwd(du, w_ref, x_ref, dx_ref, dw_ref, db_ref):
            du1, du2 = _shift_up_edge(du, 1), _shift_up_edge(du, 2)
            dx_ref[...] = (w_ref[2:3, :] * du + w_ref[1:2, :] * du1 + w_ref[0:1, :] * du2).astype(dx_ref.dtype)
            x = x_ref[...].astype(F32)
            dw_ref[0:1, :] += jnp.sum(x * du2, axis=0, keepdims=True)
            dw_ref[1:2, :] += jnp.sum(x * du1, axis=0, keepdims=True)
            dw_ref[2:3, :] += jnp.sum(x * du, axis=0, keepdims=True)
            db_ref[...] += jnp.sum(du, axis=0, keepdims=True)

        conv_bwd(dg, wg_ref, xg_ref, dug_ref, dwg_ref, dbg_ref)
        conv_bwd(dv, wv_ref, xv_ref, duv_ref, dwv_ref, dbv_ref)

    blk = pl.BlockSpec((t, tc), lambda j, b: (b, j))
    wspec = pl.BlockSpec((FFN_CONV, tc), lambda j, b: (0, j))
    bspec = pl.BlockSpec((1, tc), lambda j, b: (0, j))
    outs = _hosted_call(
        body, grid=(nf, bsz),
        in_specs=[blk, pl.BlockSpec((t, tc), lambda j, b: (b, j + nf)), blk, blk, blk,
                  wspec, pl.BlockSpec((FFN_CONV, tc), lambda j, b: (0, j + nf))],
        out_specs=[blk, blk, wspec, wspec, bspec, bspec],
        out_shape=[jax.ShapeDtypeStruct((n, f), MXU), jax.ShapeDtypeStruct((n, f), MXU),
                   jax.ShapeDtypeStruct((FFN_CONV, f), F32), jax.ShapeDtypeStruct((FFN_CONV, f), F32),
                   jax.ShapeDtypeStruct((1, f), F32), jax.ShapeDtypeStruct((1, f), F32)],
        args=(up, up, ug, uv, dact, wconv, wconv), name=name, q=q, budget_us=HOST_US["ffn_act_bwd"])
    dug, duv, dwg, dwv, dbg, dbv = outs
    return dug, duv, jnp.concatenate([dwg, dwv], axis=1), jnp.concatenate([dbg, dbv], axis=1)


def _softmax(s):
    p = jnp.exp(s - jnp.max(s, axis=-1, keepdims=True))
    return p / jnp.sum(p, axis=-1, keepdims=True)


def _mem_attn_fwd(proj, q_col_block, mkv, ycat, bsz, *, name, q=None):
    n = proj.shape[0]
    t = n // bsz
    tq = _tile(t, 512, SUBLANE)
    nt = t // tq
    scale = HEAD_DIM ** -0.5

    def body(q_ref, kv_ref, old_ref, o_ref):
        del old_ref
        heads = range(MEM_HEADS)
        col = lambda ref, h, off=0: ref[:, off + h * HEAD_DIM:off + (h + 1) * HEAD_DIM].astype(MXU)
        ss = [_dot_nt(col(q_ref, h), col(kv_ref, h)) * scale for h in heads]
        ps = [_softmax(s).astype(MXU) for s in ss]
        outs = [_dot(ps[h], col(kv_ref, h, MEM_WIDTH)) for h in heads]
        o_ref[...] = jnp.concatenate(outs, axis=-1).astype(o_ref.dtype)

    return _hosted_call(
        body, grid=(bsz, nt),
        in_specs=[pl.BlockSpec((tq, MEM_WIDTH), lambda b, i: (b * nt + i, q_col_block)),
                  pl.BlockSpec((MEM_LEN, 2 * MEM_WIDTH), lambda b, i: (b, 0)),
                  pl.BlockSpec(memory_space=pl.ANY)],
        out_specs=pl.BlockSpec((tq, MEM_WIDTH), lambda b, i: (b * nt + i, MIX_WIDTH // MEM_WIDTH)),
        out_shape=jax.ShapeDtypeStruct(ycat.shape, ycat.dtype),
        aliases={2: 0}, args=(proj, mkv, ycat), name=name, q=q, budget_us=HOST_US["mem_attn_fwd"])


def _mem_attn_bwd(proj, q_col_block, mkv, dycat, dproj, bsz, *, name, q=None):
    n = proj.shape[0]
    t = n // bsz
    tq = _tile(t, 512, SUBLANE)
    nt = t // tq
    scale = HEAD_DIM ** -0.5

    def body(q_ref, kv_ref, do_ref, old_ref, dq_ref, dkv_ref):
        del old_ref

        @pl.when(pl.program_id(1) == 0)
        def _():
            dkv_ref[...] = jnp.zeros_like(dkv_ref)

        heads = range(MEM_HEADS)
        col = lambda ref, h, off=0: ref[:, off + h * HEAD_DIM:off + (h + 1) * HEAD_DIM].astype(MXU)
        qs = [col(q_ref, h) for h in heads]
        ks = [col(kv_ref, h) for h in heads]
        dos = [col(do_ref, h) for h in heads]
        ps = [_softmax(_dot_nt(qs[h], ks[h]) * scale) for h in heads]
        dps = [_dot_nt(dos[h], col(kv_ref, h, MEM_WIDTH)) for h in heads]
        dss = [(ps[h] * (dps[h] - jnp.sum(dps[h] * ps[h], axis=-1, keepdims=True)) * scale).astype(MXU) for h in heads]
        dvs = [_dot_tn(ps[h].astype(MXU), dos[h]) for h in heads]
        dqs = [_dot(dss[h], ks[h]) for h in heads]
        dks = [_dot_tn(dss[h], qs[h]) for h in heads]
        dq_ref[...] = jnp.concatenate(dqs, axis=-1).astype(dq_ref.dtype)
        dkv_ref[...] += jnp.concatenate(dks + dvs, axis=-1)

    return _hosted_call(
        body, grid=(bsz, nt),
        in_specs=[pl.BlockSpec((tq, MEM_WIDTH), lambda b, i: (b * nt + i, q_col_block)),
                  pl.BlockSpec((MEM_LEN, 2 * MEM_WIDTH), lambda b, i: (b, 0)),
                  pl.BlockSpec((tq, MEM_WIDTH), lambda b, i: (b * nt + i, MIX_WIDTH // MEM_WIDTH)),
                  pl.BlockSpec(memory_space=pl.ANY)],
        out_specs=[pl.BlockSpec((tq, MEM_WIDTH), lambda b, i: (b * nt + i, q_col_block)),
                   pl.BlockSpec((MEM_LEN, 2 * MEM_WIDTH), lambda b, i: (b, 0))],
        out_shape=[jax.ShapeDtypeStruct(dproj.shape, dproj.dtype),
                   jax.ShapeDtypeStruct((bsz * MEM_LEN, 2 * MEM_WIDTH), F32)],
        aliases={3: 0}, args=(proj, mkv, dycat, dproj), name=name, q=q, budget_us=HOST_US["mem_attn_bwd"])


def _swa_probs(s, h, dist, mask, sink):
    s = jnp.where(mask, s * (HEAD_DIM ** -0.5) - SLOPES[h] * dist, -jnp.inf)
    m = jnp.maximum(jnp.max(s, axis=-1, keepdims=True), sink)
    p = jnp.exp(s - m)
    psink = jnp.exp(sink - m)
    inv = 1.0 / (jnp.sum(p, axis=-1, keepdims=True) + psink)
    return p * inv, psink * inv


def _swa_mask(n):
    qi = lax.broadcasted_iota(jnp.int32, (WINDOW, 2 * WINDOW), 0) + WINDOW
    ki = lax.broadcasted_iota(jnp.int32, (WINDOW, 2 * WINDOW), 1)
    dist = qi - ki
    mask = (dist >= 0) & (dist < WINDOW) & ((n > 0) | (ki >= WINDOW))
    return dist.astype(F32), mask


def _swa_fwd(proj, kv, sinks, bsz, *, name, q=None):
    n_tok = proj.shape[0]
    nb = n_tok // bsz // WINDOW
    kvw = SWA_KV_HEADS * HEAD_DIM

    def body(sink_ref, q_ref, kvp_ref, kvc_ref, o_ref):
        n = pl.program_id(1)
        dist, mask = _swa_mask(n)
        kk = jnp.concatenate([kvp_ref[:, :kvw], kvc_ref[:, :kvw]], axis=0).astype(MXU)
        vv = jnp.concatenate([kvp_ref[:, kvw:], kvc_ref[:, kvw:]], axis=0).astype(MXU)
        heads = range(SWA_HEADS)
        group = lambda x, h: x[:, (h // SWA_GROUP) * HEAD_DIM:(h // SWA_GROUP + 1) * HEAD_DIM]
        ss = [_dot_nt(q_ref[:, h * HEAD_DIM:(h + 1) * HEAD_DIM].astype(MXU), group(kk, h)) for h in heads]
        ps = [_swa_probs(ss[h], h, dist, mask, sink_ref[h])[0].astype(MXU) for h in heads]
        outs = [_dot(ps[h], group(vv, h)) for h in heads]
        o_ref[...] = jnp.concatenate(outs, axis=-1).astype(o_ref.dtype)

    return _hosted_call(
        body, grid=(bsz, nb),
        in_specs=[pl.BlockSpec(memory_space=pltpu.SMEM),
                  pl.BlockSpec((WINDOW, MIX_WIDTH), lambda b, n: (b * nb + n, 0)),
                  pl.BlockSpec((WINDOW, 2 * kvw), lambda b, n: (b * nb + jnp.maximum(n - 1, 0), 0)),
                  pl.BlockSpec((WINDOW, 2 * kvw), lambda b, n: (b * nb + n, 0))],
        out_specs=pl.BlockSpec((WINDOW, MIX_WIDTH), lambda b, n: (b * nb + n, 0)),
        out_shape=jax.ShapeDtypeStruct((n_tok, D_MODEL), MXU),
        args=(sinks, proj, kv, kv), name=name, q=q, budget_us=HOST_US["swa_fwd"])


def _swa_bwd(proj, kv, sinks, dycat, bsz, *, name, q=None):
    n_tok = proj.shape[0]
    nb = n_tok // bsz // WINDOW
    kvw = SWA_KV_HEADS * HEAD_DIM

    def body(sink_ref, q_ref, kvp_ref, kvc_ref, do_ref, dq_ref, dkvc_ref, dkvp_ref, dsink_ref):
        n = pl.program_id(1)

        @pl.when((pl.program_id(0) == 0) & (n == 0))
        def _():
            dsink_ref[...] = jnp.zeros_like(dsink_ref)

        dist, mask = _swa_mask(n)
        kk = jnp.concatenate([kvp_ref[:, :kvw], kvc_ref[:, :kvw]], axis=0).astype(MXU)
        vv = jnp.concatenate([kvp_ref[:, kvw:], kvc_ref[:, kvw:]], axis=0).astype(MXU)
        lane = lax.broadcasted_iota(jnp.int32, (SUBLANE, LANE), 1)
        heads = range(SWA_HEADS)
        group = lambda x, h: x[:, (h // SWA_GROUP) * HEAD_DIM:(h // SWA_GROUP + 1) * HEAD_DIM]
        qs = [q_ref[:, h * HEAD_DIM:(h + 1) * HEAD_DIM].astype(MXU) for h in heads]
        dos = [do_ref[:, h * HEAD_DIM:(h + 1) * HEAD_DIM].astype(MXU) for h in heads]
        ss = [_dot_nt(qs[h], group(kk, h)) for h in heads]
        dps = [_dot_nt(dos[h], group(vv, h)) for h in heads]
        probs = [_swa_probs(ss[h], h, dist, mask, sink_ref[h]) for h in heads]
        rss = [jnp.sum(dps[h] * probs[h][0], axis=-1, keepdims=True) for h in heads]
        dss = [(probs[h][0] * (dps[h] - rss[h]) * (HEAD_DIM ** -0.5)).astype(MXU) for h in heads]
        dqs = [_dot(dss[h], group(kk, h)) for h in heads]
        dk_h = [_dot_tn(dss[h], qs[h]) for h in heads]
        dv_h = [_dot_tn(probs[h][0].astype(MXU), dos[h]) for h in heads]
        dsink = jnp.zeros((SUBLANE, LANE), F32)
        for h in heads:
            dsink = dsink + jnp.where(lane == h, jnp.sum(-probs[h][1] * rss[h], axis=0, keepdims=True), 0.0)
        sum_group = lambda xs, c: functools.reduce(lambda a, b: a + b, xs[c * SWA_GROUP:(c + 1) * SWA_GROUP])
        dks = [sum_group(dk_h, c) for c in range(SWA_KV_HEADS)]
        dvs = [sum_group(dv_h, c) for c in range(SWA_KV_HEADS)]
        dq_ref[...] = jnp.concatenate(dqs, axis=-1).astype(dq_ref.dtype)
        dkv = jnp.concatenate(dks + dvs, axis=-1)
        dkvp_ref[...] = dkv[:WINDOW]
        dkvc_ref[...] = dkv[WINDOW:]
        dsink_ref[...] += dsink

    qspec = pl.BlockSpec((WINDOW, MIX_WIDTH), lambda b, n: (b * nb + n, 0))
    kvspec = pl.BlockSpec((WINDOW, 2 * kvw), lambda b, n: (b * nb + n, 0))
    return _hosted_call(
        body, grid=(bsz, nb),
        in_specs=[pl.BlockSpec(memory_space=pltpu.SMEM), qspec,
                  pl.BlockSpec((WINDOW, 2 * kvw), lambda b, n: (b * nb + jnp.maximum(n - 1, 0), 0)),
                  kvspec, qspec],
        out_specs=[qspec, kvspec, kvspec, pl.BlockSpec((SUBLANE, LANE), lambda b, n: (0, 0))],
        out_shape=[jax.ShapeDtypeStruct((n_tok, D_MODEL), MXU),
                   jax.ShapeDtypeStruct((n_tok, 2 * kvw), F32),
                   jax.ShapeDtypeStruct((n_tok, 2 * kvw), F32),
                   jax.ShapeDtypeStruct((SUBLANE, LANE), F32)],
        args=(sinks, proj, kv, kv, dycat), name=name, q=q, budget_us=HOST_US["swa_bwd"])


def _swa_dkv_combine(curs, prevs, bsz, *, name):
    n_tok, w = curs[0].shape
    nb = n_tok // bsz // WINDOW
    k = len(curs)

    def body(*refs):
        o_ref = refs[-1]
        n = pl.program_id(1)
        acc = refs[0][...]
        for r in refs[1:k]:
            acc = acc + r[...]
        nxt = refs[k][...]
        for r in refs[k + 1:2 * k]:
            nxt = nxt + r[...]
        o_ref[...] = (acc + jnp.where(n < nb - 1, nxt, 0.0)).astype(o_ref.dtype)

    cur = pl.BlockSpec((WINDOW, w), lambda b, n: (b * nb + n, 0))
    prv = pl.BlockSpec((WINDOW, w), lambda b, n: (b * nb + jnp.minimum(n + 1, nb - 1), 0))
    return pl.pallas_call(
        body, grid=(bsz, nb), in_specs=[cur] * k + [prv] * k, out_specs=cur,
        out_shape=jax.ShapeDtypeStruct((n_tok, w), MXU),
        name=name, compiler_params=_cp((PAR, PAR)))(*curs, *prevs)


def _lru_gates(ux, halo, ext_ref, wc_ref, bc_ref, wr_ref, br_ref, wi_ref, bi_ref, lam_ref):
    tt = ux.shape[0]
    ext_ref[0:SUBLANE, :] = halo
    ext_ref[SUBLANE:, :] = ux
    xs = [ux] + [ext_ref[pl.ds(SUBLANE - k, tt), :] for k in range(1, LRU_CONV)]
    xc = bc_ref[...] + wc_ref[3:4, :] * xs[0] + wc_ref[2:3, :] * xs[1] + wc_ref[1:2, :] * xs[2] + wc_ref[0:1, :] * xs[3]
    pre_r, pre_i = [], []
    for blk in range(MIX_WIDTH // GATE_TILE):
        xb = xc[:, blk * GATE_TILE:(blk + 1) * GATE_TILE].astype(MXU)
        pre_r.append(_dot(xb, wr_ref[blk]))
        pre_i.append(_dot(xb, wi_ref[blk]))
    r = jax.nn.sigmoid(jnp.concatenate(pre_r, axis=-1) + br_ref[...])
    i = jax.nn.sigmoid(jnp.concatenate(pre_i, axis=-1) + bi_ref[...])
    nlam = -lam_ref[...]
    sp = jnp.maximum(nlam, 0.0) + jnp.log(1.0 + jnp.exp(-jnp.abs(nlam)))
    log_a = -LRU_C * r * sp
    a = jnp.exp(log_a)
    om = -jnp.tanh(log_a) * (a * a + 1.0)
    s = jnp.sqrt(om)
    return xs, xc, r, i, sp, a, s


def _lru_fwd(proj, wconv, bconv, wr, br, wi, bi, lam, bsz, *, name, q=None):
    n_tok = proj.shape[0]
    t = n_tok // bsz
    tt = _tile(t, 256, SUBLANE)
    nt = t // tt
    w = MIX_WIDTH
    ng = tt // SUBLANE

    def body(pg_ref, halo_ref, wc_ref, bc_ref, wr_ref, br_ref, wi_ref, bi_ref, lam_ref,
             y_ref, h_ref, ext_ref, a_ref, b_ref, carry_ref):
        ti = pl.program_id(1)

        @pl.when(ti == 0)
        def _():
            carry_ref[...] = jnp.zeros_like(carry_ref)

        gate = pg_ref[:, :w]
        ux = pg_ref[:, w:]
        halo = jnp.where(ti > 0, halo_ref[...], 0.0)
        _, xc, _, i, _, a, s = _lru_gates(ux, halo, ext_ref, wc_ref, bc_ref, wr_ref, br_ref, wi_ref, bi_ref, lam_ref)
        a_ref[...] = a
        b_ref[...] = s * (i * xc)
        row = lax.broadcasted_iota(jnp.int32, (SUBLANE, w), 0)

        def group(g, hprev):
            off = pl.multiple_of(g * SUBLANE, SUBLANE)
            ca = a_ref[pl.ds(off, SUBLANE), :]
            cb = b_ref[pl.ds(off, SUBLANE), :]
            for d in (1, 2, 4):
                a_sh = jnp.where(row >= d, pltpu.roll(ca, d, axis=0), 1.0)
                b_sh = jnp.where(row >= d, pltpu.roll(cb, d, axis=0), 0.0)
                cb = ca * b_sh + cb
                ca = ca * a_sh
            h = ca * hprev + cb
            b_ref[pl.ds(off, SUBLANE), :] = h
            return jnp.broadcast_to(h[SUBLANE - 1:SUBLANE, :], (SUBLANE, w))

        carry_ref[...] = lax.fori_loop(0, ng, group, carry_ref[...])
        h = b_ref[...]
        h_ref[...] = h
        y_ref[...] = (h * _gelu(gate)).astype(y_ref.dtype)

    vec = lambda r: pl.BlockSpec((r, w), lambda b, i: (0, 0))
    wspec = pl.BlockSpec((w // GATE_TILE, GATE_TILE, GATE_TILE), lambda b, i: (0, 0, 0))
    hb = tt // SUBLANE
    return _hosted_call(
        body, grid=(bsz, nt),
        in_specs=[pl.BlockSpec((tt, 2 * w), lambda b, i: (b * nt + i, 0)),
                  pl.BlockSpec((SUBLANE, w), lambda b, i: (jnp.maximum((b * nt + i) * hb - 1, 0), 1)),
                  vec(LRU_CONV), vec(1), wspec, vec(1), wspec, vec(1), vec(1)],
        out_specs=[pl.BlockSpec((tt, w), lambda b, i: (b * nt + i, 0)),
                   pl.BlockSpec((tt, w), lambda b, i: (b * nt + i, 0))],
        out_shape=[jax.ShapeDtypeStruct((n_tok, D_MODEL), MXU), jax.ShapeDtypeStruct((n_tok, w), F32)],
        scratch_shapes=[pltpu.VMEM((tt + SUBLANE, w), F32), pltpu.VMEM((tt, w), F32),
                        pltpu.VMEM((tt, w), F32), pltpu.VMEM((SUBLANE, w), F32)],
        args=(proj, proj, wconv, bconv, wr, br, wi, bi, lam), name=name, q=q, budget_us=HOST_US["lru_fwd"])


def _lru_bwd(proj, hs, dycat, wconv, bconv, wr, br, wi, bi, lam, bsz, *, name, q=None):
    n_tok = proj.shape[0]
    t = n_tok // bsz
    tt = _tile(t, 256, SUBLANE)
    nt = t // tt
    w = MIX_WIDTH
    ng = tt // SUBLANE
    nblk = w // GATE_TILE

    def body(pg_ref, halo_ref, h_ref, hhalo_ref, dy_ref, wc_ref, bc_ref, wr_ref, br_ref, wi_ref, bi_ref, lam_ref,
             dp_ref, dwc_ref, dbc_ref, dwr_ref, dbr_ref, dwi_ref, dbi_ref, dlam_ref,
             ext_ref, a_ref, c_ref, g_ref, gcarry_ref, xcarry_ref):
        bi_ = pl.program_id(0)
        ti = nt - 1 - pl.program_id(1)

        @pl.when((bi_ == 0) & (pl.program_id(1) == 0))
        def _():
            for r in (dwc_ref, dbc_ref, dwr_ref, dbr_ref, dwi_ref, dbi_ref, dlam_ref):
                r[...] = jnp.zeros_like(r)

        @pl.when(pl.program_id(1) == 0)
        def _():
            gcarry_ref[...] = jnp.zeros_like(gcarry_ref)
            xcarry_ref[...] = jnp.zeros_like(xcarry_ref)

        gate = pg_ref[:, :w]
        ux = pg_ref[:, w:]
        halo = jnp.where(ti > 0, halo_ref[...], 0.0)
        xs, xc, r, i, sp, a, s = _lru_gates(ux, halo, ext_ref, wc_ref, bc_ref, wr_ref, br_ref, wi_ref, bi_ref, lam_ref)
        h = h_ref[...]
        gl, dgl = _gelu_and_grad(gate)
        dy = dy_ref[...].astype(F32)
        dgate = dy * h * dgl
        row_t = lax.broadcasted_iota(jnp.int32, (tt, w), 0)
        g_ref[...] = dy * gl + jnp.where(row_t == tt - 1, gcarry_ref[0:1, :], 0.0)
        c_ref[...] = _shift_up(a, 1, row_t)
        row = lax.broadcasted_iota(jnp.int32, (SUBLANE, w), 0)

        a_ref[...] = a

        def group(k, gnext):
            off = pl.multiple_of((ng - 1 - k) * SUBLANE, SUBLANE)
            cc = c_ref[pl.ds(off, SUBLANE), :]
            cb = g_ref[pl.ds(off, SUBLANE), :]
            cb = cb + jnp.where(row == SUBLANE - 1, gnext, 0.0)
            cc = jnp.where(row == SUBLANE - 1, 0.0, cc)
            for d in (1, 2, 4):
                c_sh = jnp.where(row < SUBLANE - d, pltpu.roll(cc, SUBLANE - d, axis=0), 1.0)
                b_sh = jnp.where(row < SUBLANE - d, pltpu.roll(cb, SUBLANE - d, axis=0), 0.0)
                cb = cc * b_sh + cb
                cc = cc * c_sh
            g_ref[pl.ds(off, SUBLANE), :] = cb
            a0 = a_ref[pl.ds(off, SUBLANE), :]
            return jnp.broadcast_to(a0[0:1, :] * cb[0:1, :], (SUBLANE, w))

        gc = lax.fori_loop(0, ng, group, jnp.zeros((SUBLANE, w), F32))
        gcarry_ref[...] = gc
        gsc = g_ref[...]

        hhalo = jnp.where(ti > 0, hhalo_ref[SUBLANE - 1:SUBLANE, :], 0.0)
        hprev = jnp.where(row_t == 0, hhalo, pltpu.roll(h, 1, axis=0))
        gated = i * xc
        d_gated = gsc * s
        d_atot = gsc * hprev - (gsc * gated) * a / s
        d_loga = d_atot * a
        d_r = d_loga * (-LRU_C) * sp
        dlam_ref[...] += jnp.sum(d_loga * r, axis=0, keepdims=True) * (LRU_C * jax.nn.sigmoid(-lam_ref[...]))
        d_i = d_gated * xc
        d_xc = d_gated * i
        d_pr = d_r * r * (1.0 - r)
        d_pi = d_i * i * (1.0 - i)
        dbr_ref[...] += jnp.sum(d_pr, axis=0, keepdims=True)
        dbi_ref[...] += jnp.sum(d_pi, axis=0, keepdims=True)
        extra = []
        for blk in range(nblk):
            sl = slice(blk * GATE_TILE, (blk + 1) * GATE_TILE)
            xb = xc[:, sl].astype(MXU)
            dr_b = d_pr[:, sl].astype(MXU)
            di_b = d_pi[:, sl].astype(MXU)
            dwr_ref[blk] += _dot_tn(xb, dr_b)
            dwi_ref[blk] += _dot_tn(xb, di_b)
            extra.append(_dot_nt(dr_b, wr_ref[blk]) + _dot_nt(di_b, wi_ref[blk]))
        d_xc = d_xc + jnp.concatenate(extra, axis=-1)
        dbc_ref[...] += jnp.sum(d_xc, axis=0, keepdims=True)
        for k in range(LRU_CONV):
            dwc_ref[k:k + 1, :] += jnp.sum(d_xc * xs[LRU_CONV - 1 - k], axis=0, keepdims=True)
        ext_ref[0:tt, :] = d_xc
        ext_ref[tt:, :] = xcarry_ref[...]
        dux = wc_ref[3:4, :] * d_xc
        for k in range(LRU_CONV - 1):
            dux = dux + wc_ref[k:k + 1, :] * ext_ref[pl.ds(LRU_CONV - 1 - k, tt), :]
        xcarry_ref[...] = d_xc[0:SUBLANE, :]
        dp_ref[:, :w] = dgate.astype(dp_ref.dtype)
        dp_ref[:, w:] = dux.astype(dp_ref.dtype)

    vec = lambda r: pl.BlockSpec((r, w), lambda b, i: (0, 0))
    wspec = pl.BlockSpec((nblk, GATE_TILE, GATE_TILE), lambda b, i: (0, 0, 0))
    hb = tt // SUBLANE
    rblk = lambda b, i: b * nt + (nt - 1 - i)
    halo_idx = lambda b, i: jnp.maximum(rblk(b, i) * hb - 1, 0)
    wide = pl.BlockSpec((tt, 2 * w), lambda b, i: (rblk(b, i), 0))
    narrow = pl.BlockSpec((tt, w), lambda b, i: (rblk(b, i), 0))
    return _hosted_call(
        body, grid=(bsz, nt),
        in_specs=[wide, pl.BlockSpec((SUBLANE, w), lambda b, i: (halo_idx(b, i), 1)),
                  narrow, pl.BlockSpec((SUBLANE, w), lambda b, i: (halo_idx(b, i), 0)), narrow,
                  vec(LRU_CONV), vec(1), wspec, vec(1), wspec, vec(1), vec(1)],
        out_specs=[wide, vec(LRU_CONV), vec(1), wspec, vec(1), wspec, vec(1), vec(1)],
        out_shape=[jax.ShapeDtypeStruct((n_tok, 2 * w + MEM_WIDTH), MXU),
                   jax.ShapeDtypeStruct((LRU_CONV, w), F32), jax.ShapeDtypeStruct((1, w), F32),
                   jax.ShapeDtypeStruct((nblk, GATE_TILE, GATE_TILE), F32), jax.ShapeDtypeStruct((1, w), F32),
                   jax.ShapeDtypeStruct((nblk, GATE_TILE, GATE_TILE), F32), jax.ShapeDtypeStruct((1, w), F32),
                   jax.ShapeDtypeStruct((1, w), F32)],
        scratch_shapes=[pltpu.VMEM((tt + SUBLANE, w), F32), pltpu.VMEM((tt, w), F32), pltpu.VMEM((tt, w), F32),
                        pltpu.VMEM((tt, w), F32), pltpu.VMEM((SUBLANE, w), F32), pltpu.VMEM((SUBLANE, w), F32)],
        args=(proj, proj, hs, hs, dycat, wconv, bconv, wr, br, wi, bi, lam), name=name, q=q,
        budget_us=HOST_US["lru_bwd"])


def _gate_tiles(w):
    per = GATE_TILE // HEAD_DIM
    w4 = w.reshape(LRU_BLOCKS // per, per, HEAD_DIM, HEAD_DIM)
    eye = jnp.eye(per, dtype=w.dtype)
    return jnp.einsum("bnij,nm->bnimj", w4, eye).reshape(LRU_BLOCKS // per, GATE_TILE, GATE_TILE)


def _gate_blocks(t):
    per = GATE_TILE // HEAD_DIM
    t5 = t.reshape(LRU_BLOCKS // per, per, HEAD_DIM, per, HEAD_DIM)
    eye = jnp.eye(per, dtype=t.dtype)
    return jnp.einsum("bnimj,nm->bnij", t5, eye).reshape(LRU_BLOCKS, HEAD_DIM, HEAD_DIM)


def _row(v):
    return v.reshape(1, -1)


def _local_step(x, mem, target, p, wfull, push_grad, q):
    bsz, t, d = x.shape
    n = bsz * t
    x2d = x.reshape(n, d)
    tgt = target.reshape(n, d)
    mem2d = mem.reshape(bsz * MEM_LEN, d)
    wr_t = [_gate_tiles(p["w_rg_r"][j]).astype(MXU) for j in range(N_A)]
    wi_t = [_gate_tiles(p["w_rg_i"][j]).astype(MXU) for j in range(N_A)]

    mn = [_norm_fwd(mem2d, _row(p["g_mem"][l]), name=f"mem_norm{l}") for l in range(DEPTH)]
    mkv = [None] * DEPTH
    h = _norm_fwd(x2d, _row(p["g_mix_pre"][0]), name="in_norm")
    xin = x2d
    sv = []
    kv = hkv = None
    for l in range(DEPTH):
        s = {"xin": xin, "h": h}
        if q is not None:
            q.horizon = (l + 2) * GROUPS_PER_LAYER
        mkv[l] = _mm_nn(mn[l], wfull("w_mem_kv", l), name=f"mem_kv{l}", q=q)
        if l < N_A:
            proj = _mm_nn(h, wfull("w_in_a", l), name=f"in_proj{l}", q=q)
            ycat, hs = _lru_fwd(proj, p["w_conv_a"][l], _row(p["b_conv_a"][l]), wr_t[l], _row(p["b_rg_r"][l]),
                                wi_t[l], _row(p["b_rg_i"][l]), _row(p["lru_lambda"][l]), bsz, name=f"lru_fwd{l}", q=q)
            s["hs"] = hs
            qblk = 2 * MIX_WIDTH // MEM_WIDTH
        else:
            if l == N_A:
                kv = _mm_nn(hkv, wfull("w_kv", 0), name="kv_proj", q=q)
            proj = _mm_nn(h, wfull("w_in_b", l - N_A), name=f"in_proj{l}", q=q)
            ycat = _swa_fwd(proj, kv, p["sinks_b"][l - N_A], bsz, name=f"swa_fwd{l}", q=q)
            qblk = MIX_WIDTH // MEM_WIDTH
        ycat = _mem_attn_fwd(proj, qblk, mkv[l], ycat, bsz, name=f"mem_attn_fwd{l}", q=q)
        y = _mm_nn(ycat, wfull("w_mix_out", l), name=f"mix_out{l}", q=q, out_dtype=MXU)
        x1, (h2,) = _resid_norm_fwd(xin, y, _row(p["g_mix_post"][l]), [_row(p["g_ffn_pre"][l])], name=f"mix_resid{l}", q=q)
        up = _mm_nn_slots(h2, wfull("w_ffn_up", l), name=f"ffn_up{l}", q=q, out_dtype=MXU)
        act, ug, uv = _ffn_act_fwd(up, p["w_ffn_conv"][l], _row(p["b_ffn_conv"][l]), bsz, name=f"ffn_act{l}", q=q)
        f = _mm_nn(act, wfull("w_ffn_down", l), name=f"ffn_down{l}", q=q, out_dtype=MXU)
        s.update(proj=proj, qblk=qblk, ycat=ycat, y=y, x1=x1, h2=h2, up=up, ug=ug, uv=uv, act=act, f=f)
        sv.append(s)
        if l < DEPTH - 1:
            g_pres = [_row(p["g_mix_pre"][l + 1])] + ([_row(p["g_kv"])] if l + 1 == N_A else [])
            xin, hn = _resid_norm_fwd(x1, f, _row(p["g_ffn_post"][l]), g_pres, name=f"ffn_resid{l}", q=q)
            h = hn[0]
            if l + 1 == N_A:
                hkv = hn[1]
        else:
            g_tot, sq = _loss_fwd(x1, f, _row(p["g_ffn_post"][l]), tgt, name="loss")

    if q is not None:
        q.horizon = LAST_GROUP
    gs = {k: [None] * DEPTH for k in ("g_mix_pre", "g_mix_post", "g_ffn_pre", "g_ffn_post", "g_mem",
                                       "w_ffn_conv", "b_ffn_conv")}
    ga = {k: [None] * N_A for k in ("w_conv_a", "b_conv_a", "w_rg_r", "b_rg_r", "w_rg_i", "b_rg_i", "lru_lambda")}
    gsink = [None] * (DEPTH - N_A)
    dkv_cur, dkv_prev = [], []
    g_tot, df, _, gs["g_ffn_post"][DEPTH - 1] = _resid_norm_bwd(
        g_tot, [], None, [], sv[-1]["f"], _row(p["g_ffn_post"][DEPTH - 1]), name="loss_bwd")
    grad_x = None
    for l in reversed(range(DEPTH)):
        s = sv[l]
        dact = _mm_nt(df, wfull("w_ffn_down", l), name=f"d_act{l}", q=q, out_dtype=MXU)
        push_grad("w_ffn_down", l, _mm_tn(s["act"], df, name=f"dw_down{l}", q=q))
        dug, duv, gs["w_ffn_conv"][l], gs["b_ffn_conv"][l] = _ffn_act_bwd(
            s["up"], s["ug"], s["uv"], dact, p["w_ffn_conv"][l], bsz, name=f"ffn_act_bwd{l}", q=q)
        dh2 = _mm_ffn_dh(dug, duv, wfull("w_ffn_up", l), name=f"d_h2_{l}", q=q)
        up_slots = dict(slot_cols=2 * D_FF // N_CHIP, n_slots=N_CHIP)
        dwu = _mm_tn_slots(s["h2"], dug, name=f"dw_up_g{l}", q=q, **up_slots)
        push_grad("w_ffn_up", l, _mm_tn_slots(s["h2"], duv, name=f"dw_up_v{l}", q=q, out=dwu,
                                              first_slot=N_CHIP // 2, **up_slots))
        g1, dy, (gs["g_ffn_pre"][l],), gs["g_mix_post"][l] = _resid_norm_bwd(
            g_tot, [dh2], s["x1"], [_row(p["g_ffn_pre"][l])], s["y"], _row(p["g_mix_post"][l]), name=f"mix_resid_bwd{l}", q=q)
        dycat = _mm_nt(dy, wfull("w_mix_out", l), name=f"d_ycat{l}", q=q, out_dtype=MXU)
        push_grad("w_mix_out", l, _mm_tn(s["ycat"], dy, name=f"dw_mix_out{l}", q=q))
        if l < N_A:
            dproj, dwc, dbc, dwr, dbr, dwi, dbi, dlam = _lru_bwd(
                s["proj"], s["hs"], dycat, p["w_conv_a"][l], _row(p["b_conv_a"][l]), wr_t[l], _row(p["b_rg_r"][l]),
                wi_t[l], _row(p["b_rg_i"][l]), _row(p["lru_lambda"][l]), bsz, name=f"lru_bwd{l}", q=q)
            ga["w_conv_a"][l], ga["b_conv_a"][l], ga["lru_lambda"][l] = dwc, dbc[0], dlam[0]
            ga["w_rg_r"][l], ga["w_rg_i"][l] = _gate_blocks(dwr), _gate_blocks(dwi)
            ga["b_rg_r"][l] = dbr.reshape(LRU_BLOCKS, HEAD_DIM)
            ga["b_rg_i"][l] = dbi.reshape(LRU_BLOCKS, HEAD_DIM)
            w_in, j = "w_in_a", l
        else:
            dproj, dc, dp_, dsk = _swa_bwd(s["proj"], kv, p["sinks_b"][l - N_A], dycat, bsz, name=f"swa_bwd{l}", q=q)
            dkv_cur.append(dc)
            dkv_prev.append(dp_)
            gsink[l - N_A] = dsk[0, :SWA_HEADS]
            w_in, j = "w_in_b", l - N_A
        dproj, dmkv = _mem_attn_bwd(s["proj"], s["qblk"], mkv[l], dycat, dproj, bsz, name=f"mem_attn_bwd{l}", q=q)
        dh = _mm_nt(dproj, wfull(w_in, j), name=f"d_h{l}", q=q, out_dtype=MXU)
        push_grad(w_in, j, _mm_tn(s["h"], dproj, name=f"dw_in{l}", q=q))
        dmkv = dmkv.astype(MXU)
        dmn = _mm_nt(dmkv, wfull("w_mem_kv", l), name=f"d_mem_norm{l}", q=q)
        push_grad("w_mem_kv", l, _mm_tn(mn[l], dmkv, name=f"dw_mem_kv{l}", q=q))
        gs["g_mem"][l] = _norm_bwd_dg(dmn, mem2d, _row(p["g_mem"][l]), name=f"mem_norm_bwd{l}")
        dhs, g_pres = [dh], [_row(p["g_mix_pre"][l])]
        if l == N_A:
            dkv = _swa_dkv_combine(dkv_cur, dkv_prev, bsz, name="dkv_combine")
            dhs.append(_mm_nt(dkv, wfull("w_kv", 0), name="d_hkv", q=q, out_dtype=MXU))
            g_pres.append(_row(p["g_kv"]))
            push_grad("w_kv", 0, _mm_tn(hkv, dkv, name="dw_kv", q=q))
        if l > 0:
            g_tot, df, dgpre, gs["g_ffn_post"][l - 1] = _resid_norm_bwd(
                g1, dhs, s["xin"], g_pres, sv[l - 1]["f"], _row(p["g_ffn_post"][l - 1]), name=f"ffn_resid_bwd{l - 1}", q=q)
        else:
            grad_x, _, dgpre, _ = _resid_norm_bwd(g1, dhs, s["xin"], g_pres, None, None, name="in_norm_bwd", q=q)
        gs["g_mix_pre"][l] = dgpre[0]
        if l == N_A:
            g_kv = dgpre[1][0]

    grads = {}
    for k in ("g_mix_pre", "g_mix_post", "g_ffn_pre", "g_ffn_post", "g_mem", "b_ffn_conv"):
        grads[k] = jnp.concatenate(gs[k], axis=0)
    grads["w_ffn_conv"] = jnp.stack(gs["w_ffn_conv"])
    for k, v in ga.items():
        grads[k] = jnp.stack(v)
    grads["sinks_b"] = jnp.stack(gsink)
    grads["g_kv"] = g_kv
    return jnp.sum(sq), grad_x.reshape(bsz, t, d), grads


N_CHIP = 4
HALF_ALIGN = 16
D2D_STREAMS = 2
MIN_PART_BYTES = 128 * 1024


def _full_shape(kind, shard_shape):
    l, r, c = shard_shape
    return {"row": (l, N_CHIP * r, c), "col": (l, r, N_CHIP * c), "slot": (N_CHIP, l, r, c)}[kind]


def _slot_view(ref, kind, shard_shape, s, hf, sub=(0, 1)):
    _, r, c = shard_shape
    rh = r // 2
    if hf is None:
        size = r // sub[1]
        start = sub[0] * size
    else:
        size = rh // sub[1]
        start = hf * rh + sub[0] * size
    if kind == "row":
        start = s * r + start
    if not isinstance(start, int):
        start = pl.multiple_of(start, HALF_ALIGN)
    rows = pl.ds(start, size)
    if kind == "row":
        return ref.at[:, rows, :]
    if kind == "col":
        return ref.at[:, rows, pl.ds(s * c, c)]
    return ref.at[s, :, rows, :]


def _half_view(ref, shard_shape, hf, sub=(0, 1)):
    rh = shard_shape[1] // 2
    size = rh // sub[1]
    return ref.at[:, pl.ds(pl.multiple_of(hf * rh + sub[0] * size, HALF_ALIGN), size), :]


def _with_slot(kind, s, fn):
    if kind != "col" or isinstance(s, int):
        fn(s)
        return
    for k in range(N_CHIP):
        @pl.when(s == k)
        def _(k=k):
            fn(k)


def _mesh_pos():
    return lax.axis_index("x"), lax.axis_index("y"), lax.axis_index("c")


def _other_chips(x, y):
    return [(1 - x, y), (x, 1 - y), (1 - x, 1 - y)]


ICI_BYTES_PER_US = 6.0e4
ICI_GATHER_BYTES_PER_US = 5.5e4
D2D_BYTES_PER_US = 4.0e5


class _Chunk:
    def __init__(self, group, cost, ins, out_shapes, alias, n_sem, start, finish, done, buffer=None, bind=None):
        self.group, self.cost, self.ins, self.out_shapes, self.alias, self.n_sem = group, cost, ins, out_shapes, alias, n_sem
        self.start, self.finish, self.done = start, finish, done
        self.buffer = buffer
        self.bind = bind

    def prepare(self):
        if self.bind is not None:
            self.bind(self)


def _merged(chunks):
    groups, by_buffer = [], {}
    for ch in chunks:
        key = None if ch.buffer is None else (id(ch.buffer[0]), ch.buffer[1])
        if key is not None and key in by_buffer:
            by_buffer[key].append(ch)
        else:
            groups.append([ch])
            if key is not None:
                by_buffer[key] = groups[-1]
    out = []
    for parts in groups:
        if len(parts) == 1:
            out.append(parts[0])
            continue
        offs = [sum(p.n_sem for p in parts[:i]) for i in range(len(parts))]

        def run(phase, ins, outs, ss, rs, b, parts=parts, offs=offs):
            for p, o in zip(parts, offs):
                getattr(p, phase)(ins, outs, ss, rs, b + o)

        def done(outs, parts=parts):
            for p in parts:
                p.done(outs)

        first = parts[0]
        out.append(_Chunk(first.group, sum(p.cost for p in parts), first.ins, first.out_shapes, first.alias,
                          sum(p.n_sem for p in parts), functools.partial(run, "start"),
                          functools.partial(run, "finish"), done))
    return out


LAST_GROUP = 1 << 30
MIN_CARRIED_US = 8.0


class _CommQueue:
    def __init__(self):
        self.pending = []
        self.flushes = 0
        self.horizon = LAST_GROUP

    def push(self, chunk):
        self.pending.append(chunk)

    def take(self, budget_us):
        got, used = [], 0.0
        for ch in sorted(self.pending, key=lambda ch: (ch.group, -ch.cost)):
            if ch.group >= self.horizon and ch.group != LAST_GROUP:
                continue
            if used + ch.cost <= budget_us and not self._shares_buffer(ch, got):
                got.append(ch)
                used += ch.cost
        if used < MIN_CARRIED_US:
            return []
        return self._taken(got)

    @staticmethod
    def _shares_buffer(ch, others):
        return ch.buffer is not None and any(
            o.buffer is not None and o.buffer[0] is ch.buffer[0] and o.buffer[1] != ch.buffer[1] for o in others)

    def _taken(self, got):
        self.pending = [ch for ch in self.pending if ch not in got]
        for ch in got:
            ch.prepare()
        return _merged(got)

    def flush(self, group=LAST_GROUP):
        while True:
            chunks = []
            for ch in self.pending:
                if ch.group <= group and not self._shares_buffer(ch, chunks):
                    chunks.append(ch)
            if not chunks:
                return
            _run_chunks(self._taken(chunks), name=f"comm_flush{self.flushes}")
            self.flushes += 1


def _run_chunks(chunks, *, name):
    ins = [a for ch in chunks for a in ch.ins]
    outs = [s for ch in chunks for s in ch.out_shapes]
    alias, offs = {}, []
    i0 = o0 = s0 = 0
    for ch in chunks:
        offs.append((i0, o0, s0))
        for ci, co in ch.alias.items():
            alias[i0 + ci] = o0 + co
        i0 += len(ch.ins)
        o0 += len(ch.out_shapes)
        s0 += ch.n_sem

    def body(*refs):
        send_sems, recv_sems = refs[i0 + o0:]
        for phase in ("start", "finish"):
            for ch, (a, b, s) in zip(chunks, offs):
                getattr(ch, phase)(refs[a:a + len(ch.ins)], refs[i0 + b:i0 + b + len(ch.out_shapes)],
                                   send_sems, recv_sems, s)

    hbm = pl.BlockSpec(memory_space=pl.ANY)
    res = pl.pallas_call(
        body, in_specs=[hbm] * i0, out_specs=[hbm] * o0, out_shape=outs,
        scratch_shapes=[pltpu.SemaphoreType.DMA((s0,)), pltpu.SemaphoreType.DMA((s0,))],
        input_output_aliases=alias, name=name, compiler_params=pltpu.CompilerParams(has_side_effects=True))(*ins)
    for ch, (_, b, _) in zip(chunks, offs):
        ch.done(list(res[b:b + len(ch.out_shapes)]))


def _remote(src, dst, send_sems, recv_sems, k, dev):
    return pltpu.make_async_remote_copy(src_ref=src, dst_ref=dst, send_sem=send_sems.at[k], recv_sem=recv_sems.at[k],
                                        device_id=dev, device_id_type=MESH_T)


def _gather_chunks(q, group, kind, shard, l, ready):
    _, r, c = shard.shape
    shp = (1, r, c)
    rh = r // 2
    parts = max(p for p in (8, 4, 2, 1)
                if (rh // p) % HALF_ALIGN == 0 and (p == 1 or (rh // p) * c * shard.dtype.itemsize >= MIN_PART_BYTES))
    part_bytes = (rh // parts) * c * shard.dtype.itemsize
    full_type = jax.ShapeDtypeStruct(_full_shape(kind, shp), shard.dtype)
    state = {"full": None, "parts_done": 0}

    def bind_first(ch):
        ch.ins, ch.alias = ([shard], {}) if state["full"] is None else ([shard, state["full"]], {1: 0})

    def bind_full(ch):
        ch.ins = [state["full"]]

    def make_part(p):
        sub = (p, parts)

        def any_part(full):
            return _slot_view(full, kind, shp, 0, 0, sub)

        def own_rows(src):
            return src.at[:, pl.ds(p * (r // parts), r // parts), :]

        def start1(ins, outs, ss, rs, b):
            x, y, c_ = _mesh_pos()
            src, full = ins[0].at[pl.ds(l, 1)], outs[0]
            _with_slot(kind, 2 * x + y, lambda s: pltpu.make_async_copy(
                own_rows(src), _slot_view(full, kind, shp, s, None, sub), ss.at[b + N_CHIP - 1]).start())
            for j, (ox, oy) in enumerate(_other_chips(x, y)):
                _with_slot(kind, 2 * x + y, lambda s, j=j, ox=ox, oy=oy: _remote(
                    _half_view(src, shp, c_, sub), _slot_view(full, kind, shp, s, c_, sub), ss, rs, b + j,
                    (ox, oy, c_)).start())

        def finish1(ins, outs, ss, rs, b):
            x, y, c_ = _mesh_pos()
            h = any_part(outs[0])
            for j in range(N_CHIP - 1):
                _remote(h, h, ss, rs, b + j, (x, y, 1 - c_)).wait()
            pltpu.make_async_copy(own_rows(ins[0].at[pl.ds(l, 1)]), _slot_view(outs[0], kind, shp, 0, None, sub),
                                  ss.at[b + N_CHIP - 1]).wait()

        def start2(ins, outs, ss, rs, b):
            x, y, c_ = _mesh_pos()
            for j, (ox, oy) in enumerate(_other_chips(x, y)):
                def forward(s, j=j):
                    v = _slot_view(outs[0], kind, shp, s, c_, sub)
                    _remote(v, v, ss, rs, b + j, (x, y, 1 - c_)).start()
                _with_slot(kind, 2 * ox + oy, forward)

        def finish2(ins, outs, ss, rs, b):
            x, y, c_ = _mesh_pos()
            h = any_part(outs[0])
            for j in range(N_CHIP - 1):
                _remote(h, h, ss, rs, b + j, (x, y, 1 - c_)).wait()

        def done2(outs):
            state["full"] = outs[0]
            state["parts_done"] += 1
            if state["parts_done"] == parts:
                ready(outs[0])

        def done1(outs):
            state["full"] = outs[0]
            q.push(_Chunk(group, 3 * part_bytes / D2D_BYTES_PER_US, None, [full_type], {0: 0}, N_CHIP - 1,
                          start2, finish2, done2, buffer=(state, 2), bind=bind_full))

        return _Chunk(group, 3 * part_bytes / ICI_GATHER_BYTES_PER_US, None, [full_type], None,
                      N_CHIP, start1, finish1, done1, buffer=(state, 1), bind=bind_first)

    for p in range(parts):
        q.push(make_part(p))


def _reduce_scatter_chunks(q, kind, grad, shard_shape, pos, name, ready):
    _, r, c = shard_shape
    shp = (1, r, c)
    rh = r // 2

    rp = rh // D2D_STREAMS

    def landing(ref, s, i):
        return ref.at[s, :, pl.ds(i * rp, rp), :]

    def start1(ins, outs, ss, rs, b):
        x, y, c_ = _mesh_pos()
        for s in range(N_CHIP):
            for i in range(D2D_STREAMS):
                _remote(_slot_view(ins[0], kind, shp, s, 1 - c_, (i, D2D_STREAMS)), landing(outs[0], s, i),
                        ss, rs, b + s * D2D_STREAMS + i, (x, y, 1 - c_)).start()

    def finish1(ins, outs, ss, rs, b):
        x, y, c_ = _mesh_pos()
        for s in range(N_CHIP):
            for i in range(D2D_STREAMS):
                v = landing(outs[0], s, i)
                _remote(v, v, ss, rs, b + s * D2D_STREAMS + i, (x, y, 1 - c_)).wait()

    def start2(ins, outs, ss, rs, b):
        x, y, c_ = _mesh_pos()
        for j, (ox, oy) in enumerate(_other_chips(x, y)):
            _remote(ins[0].at[2 * ox + oy], outs[0].at[j], ss, rs, b + j, (ox, oy, c_)).start()

    def finish2(ins, outs, ss, rs, b):
        x, y, c_ = _mesh_pos()
        for j in range(N_CHIP - 1):
            _remote(outs[0].at[j], outs[0].at[j], ss, rs, b + j, (x, y, 1 - c_)).wait()

    def start3(ins, outs, ss, rs, b):
        x, y, c_ = _mesh_pos()
        for i in range(D2D_STREAMS):
            v = _half_view(outs[0], shp, c_, (i, D2D_STREAMS))
            _remote(v, v, ss, rs, b + i, (x, y, 1 - c_)).start()

    def finish3(ins, outs, ss, rs, b):
        x, y, c_ = _mesh_pos()
        for i in range(D2D_STREAMS):
            v = _half_view(outs[0], shp, c_, (i, D2D_STREAMS))
            _remote(v, v, ss, rs, b + i, (x, y, 1 - c_)).wait()

    def done2(pair, outs):
        half = _rs_chip_add(pair, outs[0], shp, pos, name=f"rs_chip_add_{name}")
        q.push(_Chunk(LAST_GROUP, rh * c * 4 / D2D_BYTES_PER_US, [half], [jax.ShapeDtypeStruct(half.shape, half.dtype)],
                      {0: 0}, D2D_STREAMS, start3, finish3, lambda o: ready(o[0])))

    def done1(outs):
        pair, wire = _rs_pair_add(grad, outs[0], kind, shp, pos, name=f"rs_pair_add_{name}")
        q.push(_Chunk(LAST_GROUP, 3 * rh * c * wire.dtype.itemsize / ICI_BYTES_PER_US, [wire],
                      [jax.ShapeDtypeStruct((N_CHIP - 1, 1, rh, c), wire.dtype)], {}, N_CHIP - 1,
                      start2, finish2, functools.partial(done2, pair)))

    q.push(_Chunk(LAST_GROUP, N_CHIP * rh * c * 4 / D2D_BYTES_PER_US, [grad],
                  [jax.ShapeDtypeStruct((N_CHIP, 1, rh, c), F32)], {}, N_CHIP * D2D_STREAMS, start1, finish1, done1))


N_DEV = 8


def _allgather_chunk(q, group, vec, ready):
    def peer(k, x, y, c):
        return ((1 - x) if k & 4 else x, (1 - y) if k & 2 else y, (1 - c) if k & 1 else c)

    def start(ins, outs, ss, rs, b):
        x, y, c = _mesh_pos()
        me = 4 * x + 2 * y + c
        pltpu.make_async_copy(ins[0], outs[0].at[me], ss.at[b + N_DEV - 1]).start()
        for k in range(1, N_DEV):
            _remote(ins[0], outs[0].at[me], ss, rs, b + k - 1, peer(k, x, y, c)).start()

    def finish(ins, outs, ss, rs, b):
        x, y, c = _mesh_pos()
        for k in range(1, N_DEV):
            _remote(ins[0], outs[0].at[0], ss, rs, b + k - 1, peer(k, x, y, c)).wait()
        pltpu.make_async_copy(ins[0], outs[0].at[0], ss.at[b + N_DEV - 1]).wait()

    bytes_in = (N_DEV - 2) * vec.size * 4
    q.push(_Chunk(group, bytes_in / ICI_BYTES_PER_US, [vec], [jax.ShapeDtypeStruct((N_DEV,) + vec.shape, F32)], {},
                  N_DEV, start, finish, lambda o: ready(o[0])))


def _allreduce8(vec, *, name):
    r = vec.shape[0]
    rh = r // 2

    def body(v_ref, o_ref, sib_ref, chips_ref, send_sems, recv_sems):
        x, y, c = _mesh_pos()
        sib = (x, y, 1 - c)
        me = 2 * x + y
        pair = _remote(v_ref, sib_ref, send_sems, recv_sems, 0, sib)
        pair.start()
        pair.wait()
        rows = pl.ds(pl.multiple_of(c * rh, SUBLANE), rh)
        chips_ref[me] = v_ref[rows, :] + sib_ref[rows, :]
        copies = []
        for j, (ox, oy) in enumerate(_other_chips(x, y)):
            cp = _remote(chips_ref.at[me], chips_ref.at[me], send_sems, recv_sems, 1 + j, (ox, oy, c))
            cp.start()
            copies.append(cp)
        for cp in copies:
            cp.wait()
        acc = chips_ref[0]
        for s in range(1, N_CHIP):
            acc = acc + chips_ref[s]
        o_ref[rows, :] = acc
        swap = _remote(o_ref.at[rows, :], o_ref.at[rows, :], send_sems, recv_sems, N_CHIP, sib)
        swap.start()
        swap.wait()

    vm = pl.BlockSpec(memory_space=pltpu.VMEM)
    return pl.pallas_call(
        body, in_specs=[vm], out_specs=vm, out_shape=jax.ShapeDtypeStruct((r, LANE), F32),
        scratch_shapes=[pltpu.VMEM((r, LANE), F32), pltpu.VMEM((N_CHIP, rh, LANE), F32),
                        pltpu.SemaphoreType.DMA((N_CHIP + 1,)), pltpu.SemaphoreType.DMA((N_CHIP + 1,))],
        name=name, compiler_params=pltpu.CompilerParams(has_side_effects=True, vmem_limit_bytes=VMEM_LIMIT_V7X))(vec)


def _rs_pair_add(g, recv, kind, shape, pos, *, name):
    l, r, c = shape
    assert l == 1
    rh = r // 2
    if kind == "row":
        gspec = pl.BlockSpec((None, rh, c), lambda s, pos: (0, 2 * s + pos[0], 0))
    else:
        gspec = pl.BlockSpec((None, None, rh, c), lambda s, pos: (s, 0, pos[0], 0))
    pspec = pl.BlockSpec((None, None, rh, c), lambda s, pos: (s, 0, 0, 0))

    def body(pos_ref, g_ref, r_ref, own_ref, pw_ref):
        v = g_ref[...] + r_ref[...]
        pw_ref[...] = v.astype(pw_ref.dtype)

        @pl.when(pl.program_id(0) == pos_ref[1])
        def _():
            own_ref[...] = v

    return pl.pallas_call(
        body,
        grid_spec=pltpu.PrefetchScalarGridSpec(
            num_scalar_prefetch=1, grid=(N_CHIP,), in_specs=[gspec, pspec],
            out_specs=[pl.BlockSpec((None, rh, c), lambda s, pos: (0, 0, 0)), pspec]),
        out_shape=[jax.ShapeDtypeStruct((1, rh, c), F32), jax.ShapeDtypeStruct((N_CHIP, 1, rh, c), MXU)],
        name=name, compiler_params=_cp((ARB,)))(pos, g, recv)


def _rs_chip_add(p, recv, shape, pos, *, name):
    l, r, c = shape
    rh = r // 2

    def body(pos_ref, p_ref, r_ref, o_ref):
        del pos_ref
        acc = p_ref[...]
        for j in range(N_CHIP - 1):
            acc = acc + r_ref[j].astype(F32)
        o_ref[...] = acc

    return pl.pallas_call(
        body,
        grid_spec=pltpu.PrefetchScalarGridSpec(
            num_scalar_prefetch=1, grid=(l,),
            in_specs=[pl.BlockSpec((None, rh, c), lambda i, pos: (i, 0, 0)),
                      pl.BlockSpec((N_CHIP - 1, None, rh, c), lambda i, pos: (0, i, 0, 0))],
            out_specs=pl.BlockSpec((None, rh, c), lambda i, pos: (i, pos[0], 0))),
        out_shape=jax.ShapeDtypeStruct((l, r, c), F32),
        name=name, compiler_params=_cp((PAR,)))(pos, p, recv)


ADAM_BLOCK_ELEMS = 384 * 1024


def _adam_math(w, g, m, v):
    c1 = 1.0 / (1.0 - ADAM_B1 ** ADAM_STEP)
    c2 = 1.0 / (1.0 - ADAM_B2 ** ADAM_STEP)
    nm = ADAM_B1 * m + (1.0 - ADAM_B1) * g
    nv = ADAM_B2 * v + (1.0 - ADAM_B2) * (g * g)
    return -ADAM_LR * ((nm * c1) / (jnp.sqrt(nv * c2) + ADAM_EPS) + ADAM_WD * w), nm, nv


def _adamw_layer(w, g, m, v, outs, l, *, name):
    _, r, c = w.shape
    tr = _tile(r, max(SUBLANE, ADAM_BLOCK_ELEMS // c // SUBLANE * SUBLANE), SUBLANE)

    def body(w_ref, g_ref, m_ref, v_ref, *rest):
        go_ref, d_ref, nm_ref, nv_ref = rest[4:]
        gg = g_ref[...]
        go_ref[...] = gg
        d_ref[...], nm_ref[...], nv_ref[...] = _adam_math(w_ref[...], gg, m_ref[...], v_ref[...])

    lay = pl.BlockSpec((None, tr, c), lambda j: (l, j, 0))
    hbm = pl.BlockSpec(memory_space=pl.ANY)
    return pl.pallas_call(
        body, grid=(r // tr,),
        in_specs=[lay, pl.BlockSpec((None, tr, c), lambda j: (0, j, 0)), lay, lay] + [hbm] * 4,
        out_specs=[lay] * 4, out_shape=[jax.ShapeDtypeStruct(w.shape, F32)] * 4,
        input_output_aliases={4 + i: i for i in range(4)},
        name=name, compiler_params=_cp((PAR,)))(w, g, m, v, *outs)


def _adamw(w, g, m, v, *, name):
    shape = w.shape
    if w.ndim == 2:
        w, g, m, v = (a[None] for a in (w, g, m, v))
    l, r, c = w.shape
    tr = _tile(r, max(SUBLANE, ADAM_BLOCK_ELEMS // c // SUBLANE * SUBLANE), SUBLANE)

    def body(w_ref, g_ref, m_ref, v_ref, d_ref, nm_ref, nv_ref):
        d_ref[...], nm_ref[...], nv_ref[...] = _adam_math(w_ref[...], g_ref[...], m_ref[...], v_ref[...])

    spec = pl.BlockSpec((None, tr, c), lambda i, j: (i, j, 0))
    outs = pl.pallas_call(
        body, grid=(l, r // tr), in_specs=[spec] * 4, out_specs=[spec] * 3,
        out_shape=[jax.ShapeDtypeStruct((l, r, c), F32)] * 3,
        name=name, compiler_params=_cp((PAR, PAR)))(w, g, m, v)
    return tuple(o.reshape(shape) for o in outs)


PACK_ROWS = 2 * SUBLANE * LANE


def _pack(arrays):
    flat = jnp.concatenate([a.reshape(-1).astype(F32) for a in arrays])
    pad = (-flat.shape[0]) % PACK_ROWS
    return jnp.pad(flat, (0, pad)).reshape(-1, LANE)


def _unpack(packed, shapes):
    flat = packed.reshape(-1)
    out, off = [], 0
    for s in shapes:
        size = int(np.prod(s))
        out.append(flat[off:off + size].reshape(s))
        off += size
    return out


BIG = (("w_mem_kv", "row"), ("w_mix_out", "row"), ("w_ffn_up", "slot"), ("w_ffn_down", "row"),
       ("w_in_a", "slot"), ("w_in_b", "row"), ("w_kv", "row"))
COLUMN_SHARDED_AS_COLUMNS = ("w_in_a",)
SMALL_SHARDED = (("w_ffn_conv", 2), ("w_conv_a", 2), ("b_conv_a", 1), ("lru_lambda", 1))
SMALL_REPLICATED = ("g_mix_pre", "g_mix_post", "g_ffn_pre", "g_ffn_post", "g_mem", "b_ffn_conv",
                    "w_rg_r", "b_rg_r", "w_rg_i", "b_rg_i", "sinks_b", "g_kv")
WEIGHTS = ("g_mix_pre", "g_mix_post", "g_ffn_pre", "g_ffn_post", "g_mem", "w_mem_kv", "w_mix_out", "w_ffn_up",
           "w_ffn_conv", "b_ffn_conv", "w_ffn_down", "w_in_a", "w_conv_a", "b_conv_a", "w_rg_r", "b_rg_r", "w_rg_i",
           "b_rg_i", "lru_lambda", "w_in_b", "sinks_b", "g_kv", "w_kv")


def _slot_to_cols(a):
    s, l, r, c = a.shape
    return a.transpose(1, 2, 0, 3).reshape(l, r, s * c)


def _cols_to_slot(a):
    l, r, c4 = a.shape
    return a.reshape(l, r, N_CHIP, c4 // N_CHIP).transpose(2, 0, 1, 3)


GROUPS_PER_LAYER = 8


def _layer_weights(layer):
    names = [("w_mem_kv", layer), ("w_in_a", layer) if layer < N_A else ("w_in_b", layer - N_A)]
    if layer == N_A:
        names.append(("w_kv", 0))
    return names + [("w_mix_out", layer), ("w_ffn_up", layer), ("w_ffn_down", layer)]


def _train_step(x, mem, target, w, m, v):
    xi, yi, ci = _mesh_pos()
    chip = 2 * xi + yi
    pos = jnp.stack([ci, chip]).astype(jnp.int32)

    q = _CommQueue()
    kinds = dict(BIG)
    as3 = lambda a: a if a.ndim == 3 else a[None]
    w3, m3, v3 = ({k: as3(d[k]) for k, _ in BIG} for d in (w, m, v))
    shards = {k: w3[k].astype(MXU) for k, _ in BIG}

    gathered = {}

    def on_gathered(k, l, full):
        gathered[k, l] = _slot_to_cols(full) if k in COLUMN_SHARDED_AS_COLUMNS else full

    group_of = {}

    for layer in range(DEPTH):
        for i, (k, l) in enumerate(_layer_weights(layer)):
            group_of[k, l] = layer * GROUPS_PER_LAYER + i
            _gather_chunks(q, group_of[k, l], kinds[k], shards[k], l, functools.partial(on_gathered, k, l))

    def wfull(k, l):
        if (k, l) not in gathered:
            q.flush(group_of[k, l])
        return gathered[k, l]

    small = {}
    _allgather_chunk(q, 0, _pack([w[k] for k, _ in SMALL_SHARDED]), functools.partial(small.__setitem__, "stacked"))
    q.flush(1)

    big_out = {k: [lax.empty(w3[k].shape, F32) for _ in range(4)] for k, _ in BIG}

    def on_reduced(k, l, g):
        big_out[k] = _adamw_layer(w3[k], g, m3[k], v3[k], big_out[k], l, name=f"adamw_{k}{l}")

    def push_grad(k, l, g):
        if k in COLUMN_SHARDED_AS_COLUMNS:
            g = _cols_to_slot(g)
        _reduce_scatter_chunks(q, kinds[k], g, (1,) + w3[k].shape[1:], pos, f"{k}{l}", functools.partial(on_reduced, k, l))

    small_shapes = [w[k].shape for k, _ in SMALL_SHARDED]
    per_chip = [_unpack(small["stacked"][2 * s], small_shapes) for s in range(N_CHIP)]
    p = {k: w[k] for k in SMALL_REPLICATED}
    for i, (k, axis) in enumerate(SMALL_SHARDED):
        p[k] = jnp.concatenate([per_chip[s][i] for s in range(N_CHIP)], axis=axis)

    sq, grad_x, g = _local_step(x, mem, target, p, wfull, push_grad, q)
    loss = lax.psum(0.5 * sq / D_MODEL, ("x", "y", "c"))
    q.flush()

    small_names = [k for k, _ in SMALL_SHARDED] + list(SMALL_REPLICATED)
    summed = _allreduce8(_pack([g[k] for k in small_names]), name="allreduce_small")
    gsum = dict(zip(small_names, _unpack(summed, [p[k].shape for k in small_names])))
    for k, axis in SMALL_SHARDED:
        gsum[k] = lax.dynamic_slice_in_dim(gsum[k], chip * w[k].shape[axis], w[k].shape[axis], axis)

    delta, new_m, new_v = {}, {}, {}
    for k, _ in BIG:
        gsum[k], delta[k], new_m[k], new_v[k] = (o.reshape(w[k].shape) for o in big_out[k])
    for k in small_names:
        as2 = lambda a: a.reshape(-1, a.shape[-1])
        outs = _adamw(as2(w[k]), as2(gsum[k]), as2(m[k]), as2(v[k]), name=f"adamw_{k}")
        delta[k], new_m[k], new_v[k] = (o.reshape(w[k].shape) for o in outs)
    return (loss, grad_x, *[gsum[k] for k in WEIGHTS], *[delta[k] for k in WEIGHTS],
            *[new_m[k] for k in WEIGHTS], *[new_v[k] for k in WEIGHTS])


def kernel(x, mem, g_mix_pre, g_mix_post, g_ffn_pre, g_ffn_post, g_mem, w_mem_kv, w_mix_out, w_ffn_up, w_ffn_conv, b_ffn_conv, w_ffn_down, w_in_a, w_conv_a, b_conv_a, w_rg_r, b_rg_r, w_rg_i, b_rg_i, lru_lambda, w_in_b, sinks_b, g_kv, w_kv, loss_target, m_g_mix_pre, m_g_mix_post, m_g_ffn_pre, m_g_ffn_post, m_g_mem, m_w_mem_kv, m_w_mix_out, m_w_ffn_up, m_w_ffn_conv, m_b_ffn_conv, m_w_ffn_down, m_w_in_a, m_w_conv_a, m_b_conv_a, m_w_rg_r, m_b_rg_r, m_w_rg_i, m_b_rg_i, m_lru_lambda, m_w_in_b, m_sinks_b, m_g_kv, m_w_kv, v_g_mix_pre, v_g_mix_post, v_g_ffn_pre, v_g_ffn_post, v_g_mem, v_w_mem_kv, v_w_mix_out, v_w_ffn_up, v_w_ffn_conv, v_b_ffn_conv, v_w_ffn_down, v_w_in_a, v_w_conv_a, v_b_conv_a, v_w_rg_r, v_b_rg_r, v_w_rg_i, v_b_rg_i, v_lru_lambda, v_w_in_b, v_sinks_b, v_g_kv, v_w_kv):
    args = (g_mix_pre, g_mix_post, g_ffn_pre, g_ffn_post, g_mem, w_mem_kv, w_mix_out, w_ffn_up, w_ffn_conv, b_ffn_conv, w_ffn_down, w_in_a, w_conv_a, b_conv_a, w_rg_r, b_rg_r, w_rg_i, b_rg_i, lru_lambda, w_in_b, sinks_b, g_kv, w_kv)
    ms = (m_g_mix_pre, m_g_mix_post, m_g_ffn_pre, m_g_ffn_post, m_g_mem, m_w_mem_kv, m_w_mix_out, m_w_ffn_up, m_w_ffn_conv, m_b_ffn_conv, m_w_ffn_down, m_w_in_a, m_w_conv_a, m_b_conv_a, m_w_rg_r, m_b_rg_r, m_w_rg_i, m_b_rg_i, m_lru_lambda, m_w_in_b, m_sinks_b, m_g_kv, m_w_kv)
    vs = (v_g_mix_pre, v_g_mix_post, v_g_ffn_pre, v_g_ffn_post, v_g_mem, v_w_mem_kv, v_w_mix_out, v_w_ffn_up, v_w_ffn_conv, v_b_ffn_conv, v_w_ffn_down, v_w_in_a, v_w_conv_a, v_b_conv_a, v_w_rg_r, v_b_rg_r, v_w_rg_i, v_b_rg_i, v_lru_lambda, v_w_in_b, v_sinks_b, v_g_kv, v_w_kv)
    return _train_step(x, mem, loss_target, dict(zip(WEIGHTS, args)), dict(zip(WEIGHTS, ms)), dict(zip(WEIGHTS, vs)))
```

```python
import functools
import math

import numpy as np
import jax
import jax.numpy as jnp
from jax import lax
from jax.experimental import pallas as pl
from jax.experimental.pallas import tpu as pltpu

F32 = jnp.float32
MXU = jnp.bfloat16

D_MODEL = 1024
HEAD_DIM = 64
MEM_LEN = 256
MEM_HEADS = 4
MEM_WIDTH = MEM_HEADS * HEAD_DIM
MIX_WIDTH = D_MODEL - MEM_WIDTH
LRU_BLOCKS = MIX_WIDTH // HEAD_DIM
LRU_CONV = 4
LRU_C = 8.0
SWA_HEADS = MIX_WIDTH // HEAD_DIM
SWA_KV_HEADS = 4
SWA_GROUP = SWA_HEADS // SWA_KV_HEADS
WINDOW = 128
D_FF = 2816
FFN_CONV = 3
EPS = 1e-6
DEPTH = 4
N_A = 2

ADAM_LR = 0.001
ADAM_B1 = 0.9
ADAM_B2 = 0.999
ADAM_EPS = 1e-08
ADAM_WD = 0.01
ADAM_STEP = 10

VMEM_LIMIT_V7X = 56 * 1024 * 1024
LANE = 128
SUBLANE = 8
GATE_TILE = 256
MESH_T = pl.DeviceIdType.MESH


def _alibi_slopes(n):
    def pow2_slopes(m):
        start = 2.0 ** (-8.0 / m)
        return [start ** (i + 1) for i in range(m)]
    c = 2 ** int(math.floor(math.log2(n)))
    s = pow2_slopes(c)
    if c != n:
        s = s + pow2_slopes(2 * c)[0::2][: n - c]
    return [float(np.float32(v)) for v in s]


SLOPES = _alibi_slopes(SWA_HEADS)


def _tile(n, cap, mult=LANE):
    best = None
    for t in range(mult, min(n, cap) + 1, mult):
        if n % t == 0:
            best = t
    return best if best is not None else n


def _cp(sem):
    return pltpu.CompilerParams(dimension_semantics=sem, vmem_limit_bytes=VMEM_LIMIT_V7X)


MM_VMEM_BUDGET = 40 * 1024 * 1024
HBM_BYTES_PER_US_V7X = 3.0e6
GRID_STEP_US = 0.35


def _divisors(n, mult):
    return [t for t in range(mult, n + 1, mult) if n % t == 0] or [n]


def _mm_tiles(m, k, n, out_bytes):
    best = None
    for tm in _divisors(m, 256):
        for tn in _divisors(n, LANE):
            vmem = 2 * (tm * k * 2 + k * tn * 2 + tm * tn * out_bytes)
            if vmem > MM_VMEM_BUDGET:
                continue
            steps = (m // tm) * (n // tn)
            b_reads = 1 if tn == n else m // tm
            traffic = m * k * 2 + k * n * 2 * b_reads + m * n * out_bytes
            first = tm * k * 2 + k * tn * 2
            cost = (traffic + first) / HBM_BYTES_PER_US_V7X + steps * GRID_STEP_US
            if best is None or cost < best[0]:
                best = (cost, tm, tn)
    return best[1], best[2]


def _mm_tn_tiles(k, m, n, whole_n=False):
    best = None
    for tm in _divisors(m, LANE):
        for tn in ([n] if whole_n else _divisors(n, LANE)):
            for tk in _divisors(k, 512):
                vmem = 2 * (tk * tm * 2 + tk * tn * 2 + tm * tn * 4)
                if vmem > MM_VMEM_BUDGET:
                    continue
                steps = (m // tm) * (n // tn) * (k // tk)
                traffic = k * m * 2 * (n // tn) + k * n * 2 * (m // tm) + m * n * 4
                cost = traffic / HBM_BYTES_PER_US_V7X + steps * GRID_STEP_US
                if best is None or cost < best[0]:
                    best = (cost, tk, tm, tn)
    return best[1], best[2], best[3]


ARB = "arbitrary"
PAR = "parallel"


def _rms_fwd(x, g):
    r = lax.rsqrt(jnp.mean(x * x, axis=-1, keepdims=True) + EPS)
    return x * r * g


def _rms_bwd(dy, x, g):
    r = lax.rsqrt(jnp.mean(x * x, axis=-1, keepdims=True) + EPS)
    xh = x * r
    gdy = dy * g
    dx = r * (gdy - xh * jnp.mean(gdy * xh, axis=-1, keepdims=True))
    dg = jnp.sum(dy * xh, axis=0, keepdims=True)
    return dx, dg


_GELU_K = math.sqrt(2.0 / math.pi)
_GELU_C = 0.044715


def _gelu(x):
    t = jnp.tanh(_GELU_K * (x + _GELU_C * x * x * x))
    return 0.5 * x * (1.0 + t)


def _gelu_and_grad(x):
    x2 = x * x
    u = 0.5 * jnp.tanh(x * (_GELU_K + (_GELU_K * _GELU_C) * x2)) + 0.5
    dz2 = (6.0 * _GELU_K * _GELU_C) * x2 + 2.0 * _GELU_K
    return x * u, u * ((x * (1.0 - u)) * dz2 + 1.0)


def _shift_down(x, k, row):
    return jnp.where(row >= k, pltpu.roll(x, k, axis=0), 0.0)


def _shift_up(x, k, row):
    n = x.shape[0]
    return jnp.where(row < n - k, pltpu.roll(x, n - k, axis=0), 0.0)


def _shift_down_edge(x, k):
    r = pltpu.roll(x, k, axis=0)
    row = lax.broadcasted_iota(jnp.int32, (SUBLANE, x.shape[1]), 0)
    return jnp.concatenate([jnp.where(row >= k, r[:SUBLANE], 0.0), r[SUBLANE:]], axis=0)


def _shift_up_edge(x, k):
    n = x.shape[0]
    r = pltpu.roll(x, n - k, axis=0)
    row = lax.broadcasted_iota(jnp.int32, (SUBLANE, x.shape[1]), 0)
    return jnp.concatenate([r[:n - SUBLANE], jnp.where(row < SUBLANE - k, r[n - SUBLANE:], 0.0)], axis=0)


def _dot(a, b):
    return jnp.dot(a, b, preferred_element_type=F32)


def _dot_nt(a, b):
    return lax.dot_general(a, b, (((1,), (1,)), ((), ())), preferred_element_type=F32)


def _dot_tn(a, b):
    return lax.dot_general(a, b, (((0,), (0,)), ((), ())), preferred_element_type=F32)


MXU_FLOPS_PER_US = 7.0e8
HOST_US = {"lru_fwd": 50.0, "lru_bwd": 94.0, "swa_fwd": 55.0, "swa_bwd": 90.0, "mem_attn_fwd": 19.0,
           "mem_attn_bwd": 27.0, "ffn_act": 75.0, "ffn_act_bwd": 75.0, "resid": 22.0, "resid_bwd": 33.0}
HOST_FILL = 1.0


def _hosted_call(body, *, grid, in_specs, out_specs, out_shape, args, name, aliases=None, scratch_shapes=(),
                 q=None, flops=0.0, budget_us=0.0):
    chunks = q.take(HOST_FILL * (flops / MXU_FLOPS_PER_US + budget_us)) if q is not None else []
    if not chunks:
        return pl.pallas_call(
            body, grid=grid, in_specs=in_specs, out_specs=out_specs, out_shape=out_shape,
            scratch_shapes=list(scratch_shapes), input_output_aliases=aliases or {}, name=name,
            compiler_params=_cp((ARB,) * len(grid)))(*args)
    single = not isinstance(out_shape, (list, tuple))
    o_shapes = [out_shape] if single else list(out_shape)
    o_specs = [out_specs] if single else list(out_specs)
    n_in, n_out, n_scr = len(args), len(o_shapes), len(scratch_shapes)
    c_ins = [a for ch in chunks for a in ch.ins]
    c_outs = [s for ch in chunks for s in ch.out_shapes]
    alias = dict(aliases or {})
    in_off, out_off, sem_off = [], [], []
    i0 = o0 = s0 = 0
    for ch in chunks:
        in_off.append(i0)
        out_off.append(o0)
        sem_off.append(s0)
        for ci, co in ch.alias.items():
            alias[n_in + i0 + ci] = n_out + o0 + co
        i0 += len(ch.ins)
        o0 += len(ch.out_shapes)
        s0 += ch.n_sem

    def wrapped(*refs):
        ins = refs[:n_in]
        cin = refs[n_in:n_in + i0]
        outs = refs[n_in + i0:n_in + i0 + n_out]
        cout = refs[n_in + i0 + n_out:n_in + i0 + n_out + o0]
        scr = refs[n_in + i0 + n_out + o0:n_in + i0 + n_out + o0 + n_scr]
        send_sems, recv_sems = refs[n_in + i0 + n_out + o0 + n_scr:]
        first = functools.reduce(lambda u, v: u & v, [pl.program_id(d) == 0 for d in range(len(grid))])
        last = functools.reduce(lambda u, v: u & v, [pl.program_id(d) == grid[d] - 1 for d in range(len(grid))])

        def each(phase):
            for ch, a, b, s in zip(chunks, in_off, out_off, sem_off):
                getattr(ch, phase)(cin[a:a + len(ch.ins)], cout[b:b + len(ch.out_shapes)], send_sems, recv_sems, s)

        pl.when(first)(lambda: each("start"))
        body(*ins, *outs, *scr)
        pl.when(last)(lambda: each("finish"))

    hbm = pl.BlockSpec(memory_space=pl.ANY)
    res = pl.pallas_call(
        wrapped, grid=grid, in_specs=list(in_specs) + [hbm] * i0, out_specs=o_specs + [hbm] * o0,
        out_shape=o_shapes + c_outs,
        scratch_shapes=list(scratch_shapes) + [pltpu.SemaphoreType.DMA((s0,)), pltpu.SemaphoreType.DMA((s0,))],
        input_output_aliases=alias, name=name,
        compiler_params=pltpu.CompilerParams(dimension_semantics=(ARB,) * len(grid), vmem_limit_bytes=VMEM_LIMIT_V7X,
                                             has_side_effects=True))(*args, *c_ins)
    for ch, b in zip(chunks, out_off):
        ch.done(list(res[n_out + b:n_out + b + len(ch.out_shapes)]))
    return res[0] if single else list(res[:n_out])


def _mm_nn(a, b, *, name, q=None, out_dtype=F32):
    m, k = a.shape
    n = b.shape[-1]
    tm, tn = _mm_tiles(m, k, n, jnp.dtype(out_dtype).itemsize)

    def body(a_ref, b_ref, o_ref):
        o_ref[...] = _dot(a_ref[...], b_ref[...]).astype(o_ref.dtype)

    return _hosted_call(
        body, grid=(m // tm, n // tn),
        in_specs=[pl.BlockSpec((tm, k), lambda i, j: (i, 0)),
                  pl.BlockSpec((None, k, tn), lambda i, j: (0, 0, j))],
        out_specs=pl.BlockSpec((tm, tn), lambda i, j: (i, j)),
        out_shape=jax.ShapeDtypeStruct((m, n), out_dtype),
        args=(a, b), name=name, q=q, flops=2.0 * m * k * n)


def _mm_nt(a, b, *, name, q=None, out_dtype=F32):
    m, k = a.shape
    n = b.shape[-2]
    tm, tn = _mm_tiles(m, k, n, jnp.dtype(out_dtype).itemsize)

    def body(a_ref, b_ref, o_ref):
        o_ref[...] = _dot_nt(a_ref[...], b_ref[...]).astype(o_ref.dtype)

    return _hosted_call(
        body, grid=(m // tm, n // tn),
        in_specs=[pl.BlockSpec((tm, k), lambda i, j: (i, 0)),
                  pl.BlockSpec((None, tn, k), lambda i, j: (0, j, 0))],
        out_specs=pl.BlockSpec((tm, tn), lambda i, j: (i, j)),
        out_shape=jax.ShapeDtypeStruct((m, n), out_dtype),
        args=(a, b), name=name, q=q, flops=2.0 * m * k * n)


def _mm_nn_slots(a, b4, *, name, q=None, out_dtype=F32):
    m, k = a.shape
    s_, _, _, c = b4.shape
    ob = jnp.dtype(out_dtype).itemsize
    tm = max(t for t in _divisors(m, 256) if 2 * (t * k * 2 + k * c * 2 + t * c * ob) <= MM_VMEM_BUDGET)

    def body(a_ref, b_ref, o_ref):
        o_ref[...] = _dot(a_ref[...], b_ref[...]).astype(o_ref.dtype)

    return _hosted_call(
        body, grid=(m // tm, s_),
        in_specs=[pl.BlockSpec((tm, k), lambda i, j: (i, 0)),
                  pl.BlockSpec((None, None, k, c), lambda i, j: (j, 0, 0, 0))],
        out_specs=pl.BlockSpec((tm, c), lambda i, j: (i, j)),
        out_shape=jax.ShapeDtypeStruct((m, s_ * c), out_dtype),
        args=(a, b4), name=name, q=q, flops=2.0 * m * k * s_ * c)


def _mm_tn_slots(a, b, *, name, slot_cols, n_slots, first_slot=0, q=None, out=None):
    k, m = a.shape
    c = slot_cols
    tk, tm, _ = _mm_tn_tiles(k, m, c, whole_n=True)

    def body(a_ref, b_ref, *rest):
        o_ref = rest[-1]
        part = _dot_tn(a_ref[...], b_ref[...])

        @pl.when(pl.program_id(2) == 0)
        def _():
            o_ref[...] = part

        @pl.when(pl.program_id(2) > 0)
        def _():
            o_ref[...] += part

    in_specs = [pl.BlockSpec((tk, tm), lambda i, j, s: (s, i)), pl.BlockSpec((tk, c), lambda i, j, s: (s, j))]
    args = (a, b)
    if out is not None:
        in_specs.append(pl.BlockSpec(memory_space=pl.ANY))
        args = (a, b, out)
    return _hosted_call(
        body, grid=(m // tm, b.shape[-1] // c, k // tk), in_specs=in_specs,
        out_specs=pl.BlockSpec((None, None, tm, c), lambda i, j, s: (first_slot + j, 0, i, 0)),
        out_shape=jax.ShapeDtypeStruct((n_slots, 1, m, c), F32),
        aliases={2: 0} if out is not None else None,
        args=args, name=name, q=q, flops=2.0 * m * k * b.shape[-1])


def _mm_tn(a, b, *, name, q=None, out=None, n_total=None, col_block_offset=0):
    k, m = a.shape
    n = b.shape[-1]
    tk, tm, tn = _mm_tn_tiles(k, m, n)
    off = col_block_offset * (n // tn)

    def body(a_ref, b_ref, *rest):
        o_ref = rest[-1]
        part = _dot_tn(a_ref[...], b_ref[...])

        @pl.when(pl.program_id(2) == 0)
        def _():
            o_ref[...] = part

        @pl.when(pl.program_id(2) > 0)
        def _():
            o_ref[...] += part

    in_specs = [pl.BlockSpec((tk, tm), lambda i, j, s: (s, i)), pl.BlockSpec((tk, tn), lambda i, j, s: (s, j))]
    args = (a, b)
    if out is not None:
        in_specs.append(pl.BlockSpec(memory_space=pl.ANY))
        args = (a, b, out)
    return _hosted_call(
        body, grid=(m // tm, n // tn, k // tk), in_specs=in_specs,
        out_specs=pl.BlockSpec((None, tm, tn), lambda i, j, s: (0, i, j + off)),
        out_shape=jax.ShapeDtypeStruct((1, m, n_total or n), F32),
        aliases={2: 0} if out is not None else None,
        args=args, name=name, q=q, flops=2.0 * m * k * n)


def _mm_ffn_dh(dg, dv, w4, *, name, q=None):
    m, f = dg.shape
    n_slots, _, d, c = w4.shape
    tm, tn = _mm_tiles(m, 2 * f, d, 4)

    def body(dg_ref, dv_ref, *rest):
        w_refs, o_ref = rest[:n_slots], rest[n_slots]
        acc = None
        for s, w_ref in enumerate(w_refs):
            x_ref = dg_ref if s < n_slots // 2 else dv_ref
            off = (s % (n_slots // 2)) * c
            part = _dot_nt(x_ref[:, off:off + c], w_ref[...])
            acc = part if acc is None else acc + part
        o_ref[...] = acc.astype(o_ref.dtype)

    wspec = lambda s: pl.BlockSpec((None, None, tn, c), lambda i, j: (s, 0, j, 0))
    return _hosted_call(
        body, grid=(m // tm, d // tn),
        in_specs=[pl.BlockSpec((tm, f), lambda i, j: (i, 0)),
                  pl.BlockSpec((tm, f), lambda i, j: (i, 0))] + [wspec(s) for s in range(n_slots)],
        out_specs=pl.BlockSpec((tm, tn), lambda i, j: (i, j)),
        out_shape=jax.ShapeDtypeStruct((m, d), MXU),
        args=(dg, dv) + (w4,) * n_slots, name=name, q=q, flops=4.0 * m * f * d)


ROW_TILE = 512


def _norm_fwd(x, g, *, name):
    n, d = x.shape
    tm = _tile(n, ROW_TILE, SUBLANE)

    def body(x_ref, g_ref, o_ref):
        o_ref[...] = _rms_fwd(x_ref[...], g_ref[...]).astype(o_ref.dtype)

    return pl.pallas_call(
        body, grid=(n // tm,),
        in_specs=[pl.BlockSpec((tm, d), lambda i: (i, 0)), pl.BlockSpec((1, d), lambda i: (0, 0))],
        out_specs=pl.BlockSpec((tm, d), lambda i: (i, 0)),
        out_shape=jax.ShapeDtypeStruct((n, d), MXU),
        name=name, compiler_params=_cp((PAR,)))(x, g)


def _norm_bwd_dg(dy, x, g, *, name):
    n, d = x.shape
    tm = _tile(n, ROW_TILE, SUBLANE)

    def body(dy_ref, x_ref, g_ref, dg_ref):
        @pl.when(pl.program_id(0) == 0)
        def _():
            dg_ref[...] = jnp.zeros_like(dg_ref)
        _, dg = _rms_bwd(dy_ref[...], x_ref[...], g_ref[...])
        dg_ref[...] += dg

    return pl.pallas_call(
        body, grid=(n // tm,),
        in_specs=[pl.BlockSpec((tm, d), lambda i: (i, 0)), pl.BlockSpec((tm, d), lambda i: (i, 0)),
                  pl.BlockSpec((1, d), lambda i: (0, 0))],
        out_specs=pl.BlockSpec((1, d), lambda i: (0, 0)),
        out_shape=jax.ShapeDtypeStruct((1, d), F32),
        name=name, compiler_params=_cp((ARB,)))(dy, x, g)


def _resid_norm_fwd(x, y, g_post, g_pres, *, name, q=None):
    n, d = x.shape
    tm = _tile(n, ROW_TILE, SUBLANE)
    nh = len(g_pres)

    def body(x_ref, y_ref, gp_ref, *rest):
        gpre = rest[:nh]
        xo_ref = rest[nh]
        h_refs = rest[nh + 1:]
        xo = x_ref[...] + _rms_fwd(y_ref[...].astype(F32), gp_ref[...])
        xo_ref[...] = xo
        for g_ref, h_ref in zip(gpre, h_refs):
            h_ref[...] = _rms_fwd(xo, g_ref[...]).astype(h_ref.dtype)

    row = pl.BlockSpec((tm, d), lambda i: (i, 0))
    vec = pl.BlockSpec((1, d), lambda i: (0, 0))
    outs = _hosted_call(
        body, grid=(n // tm,),
        in_specs=[row, row, vec] + [vec] * nh,
        out_specs=[row] + [row] * nh,
        out_shape=[jax.ShapeDtypeStruct((n, d), F32)] + [jax.ShapeDtypeStruct((n, d), MXU)] * nh,
        args=(x, y, g_post, *g_pres), name=name, q=q, budget_us=HOST_US["resid"])
    return outs[0], list(outs[1:])


def _loss_fwd(x, y, g_post, target, *, name):
    n, d = x.shape
    tm = _tile(n, ROW_TILE, SUBLANE)

    def body(x_ref, y_ref, gp_ref, t_ref, dx_ref, sq_ref, dy_ref, dg_ref):
        @pl.when(pl.program_id(0) == 0)
        def _():
            sq_ref[...] = jnp.zeros_like(sq_ref)
            dg_ref[...] = jnp.zeros_like(dg_ref)
        y = y_ref[...].astype(F32)
        err = x_ref[...] + _rms_fwd(y, gp_ref[...]) - t_ref[...]
        g = err * (1.0 / d)
        dx_ref[...] = g
        sq_ref[...] += jnp.sum(err * err, axis=0, keepdims=True)
        dy, dg = _rms_bwd(g, y, gp_ref[...])
        dy_ref[...] = dy.astype(dy_ref.dtype)
        dg_ref[...] += dg

    row = pl.BlockSpec((tm, d), lambda i: (i, 0))
    vec = pl.BlockSpec((1, d), lambda i: (0, 0))
    return pl.pallas_call(
        body, grid=(n // tm,),
        in_specs=[row, row, vec, row],
        out_specs=[row, vec, row, vec],
        out_shape=[jax.ShapeDtypeStruct((n, d), F32), jax.ShapeDtypeStruct((1, d), F32),
                   jax.ShapeDtypeStruct((n, d), MXU), jax.ShapeDtypeStruct((1, d), F32)],
        name=name, compiler_params=_cp((ARB,)))(x, y, g_post, target)


def _resid_norm_bwd(dx_out, dhs, x_out, g_pres, y, g_post, *, name, q=None):
    n, d = dx_out.shape
    tm = _tile(n, ROW_TILE, SUBLANE)
    nh = len(dhs)
    has_y = y is not None

    def body(*refs):
        it = iter(refs)
        dxo_ref = next(it)
        dh_refs = [next(it) for _ in range(nh)]
        xo_ref = next(it) if nh else None
        gpre_refs = [next(it) for _ in range(nh)]
        y_ref = next(it) if has_y else None
        gpost_ref = next(it) if has_y else None
        g_out = next(it)
        dy_out = next(it) if has_y else None
        dgpre_out = [next(it) for _ in range(nh)]
        dgpost_out = next(it) if has_y else None

        @pl.when(pl.program_id(0) == 0)
        def _():
            for r in dgpre_out:
                r[...] = jnp.zeros_like(r)
            if has_y:
                dgpost_out[...] = jnp.zeros_like(dgpost_out)

        g = dxo_ref[...]
        if nh:
            xo = xo_ref[...]
            for dh_ref, gp_ref, dg_ref in zip(dh_refs, gpre_refs, dgpre_out):
                dx, dg = _rms_bwd(dh_ref[...].astype(F32), xo, gp_ref[...])
                g = g + dx
                dg_ref[...] += dg
        g_out[...] = g
        if has_y:
            dy, dg = _rms_bwd(g, y_ref[...].astype(F32), gpost_ref[...])
            dy_out[...] = dy.astype(dy_out.dtype)
            dgpost_out[...] += dg

    row = pl.BlockSpec((tm, d), lambda i: (i, 0))
    vec = pl.BlockSpec((1, d), lambda i: (0, 0))
    ins, in_specs = [dx_out], [row]
    ins += list(dhs)
    in_specs += [row] * nh
    if nh:
        ins.append(x_out)
        in_specs.append(row)
    ins += list(g_pres)
    in_specs += [vec] * nh
    if has_y:
        ins += [y, g_post]
        in_specs += [row, vec]
    out_specs, out_shape = [row], [jax.ShapeDtypeStruct((n, d), F32)]
    if has_y:
        out_specs.append(row)
        out_shape.append(jax.ShapeDtypeStruct((n, d), MXU))
    out_specs += [vec] * nh
    out_shape += [jax.ShapeDtypeStruct((1, d), F32)] * nh
    if has_y:
        out_specs.append(vec)
        out_shape.append(jax.ShapeDtypeStruct((1, d), F32))
    outs = list(_hosted_call(
        body, grid=(n // tm,), in_specs=in_specs, out_specs=out_specs, out_shape=out_shape,
        args=tuple(ins), name=name, q=q, budget_us=HOST_US["resid_bwd"]))
    g = outs.pop(0)
    dy = outs.pop(0) if has_y else None
    dgpre = [outs.pop(0) for _ in range(nh)]
    dgpost = outs.pop(0) if has_y else None
    return g, dy, dgpre, dgpost


def _ffn_conv(up, w_ref, b_ref):
    return (w_ref[0:1, :] * _shift_down_edge(up, 2) + w_ref[1:2, :] * _shift_down_edge(up, 1)
            + w_ref[2:3, :] * up + b_ref[...])


def _ffn_act_fwd(up, wconv, bconv, bsz, *, name, q=None):
    n, f2 = up.shape
    f = f2 // 2
    t = n // bsz
    tc = _tile(f, 256)
    nf = f // tc

    def body(ug_ref, uv_ref, wg_ref, wv_ref, bg_ref, bv_ref, o_ref, dag_ref, dav_ref):
        g = _ffn_conv(ug_ref[...].astype(F32), wg_ref, bg_ref)
        v = _ffn_conv(uv_ref[...].astype(F32), wv_ref, bv_ref)
        gl, dgl = _gelu_and_grad(g)
        dag_ref[...] = (v * dgl).astype(dag_ref.dtype)
        dav_ref[...] = gl.astype(dav_ref.dtype)
        o_ref[...] = (gl * v).astype(o_ref.dtype)

    blk = pl.BlockSpec((t, tc), lambda b, j: (b, j))
    return _hosted_call(
        body, grid=(bsz, nf),
        in_specs=[blk, pl.BlockSpec((t, tc), lambda b, j: (b, j + nf)),
                  pl.BlockSpec((FFN_CONV, tc), lambda b, j: (0, j)),
                  pl.BlockSpec((FFN_CONV, tc), lambda b, j: (0, j + nf)),
                  pl.BlockSpec((1, tc), lambda b, j: (0, j)),
                  pl.BlockSpec((1, tc), lambda b, j: (0, j + nf))],
        out_specs=[blk, blk, blk],
        out_shape=[jax.ShapeDtypeStruct((n, f), MXU)] * 3,
        args=(up, up, wconv, wconv, bconv, bconv), name=name, q=q, budget_us=HOST_US["ffn_act"])


def _ffn_act_bwd(up, ug, uv, dact, wconv, bsz, *, name, q=None):
    n, f2 = up.shape
    f = f2 // 2
    t = n // bsz
    tc = _tile(f, 256)
    nf = f // tc

    def body(xg_ref, xv_ref, g_ref, v_ref, da_ref, wg_ref, wv_ref,
             dug_ref, duv_ref, dwg_ref, dwv_ref, dbg_ref, dbv_ref):
        @pl.when(pl.program_id(1) == 0)
        def _():
            for r in (dwg_ref, dwv_ref, dbg_ref, dbv_ref):
                r[...] = jnp.zeros_like(r)

        da = da_ref[...].astype(F32)
        dg = da * g_ref[...].astype(F32)
        dv = da * v_ref[...].astype(F32)

        def conv_bwd(du, w_ref, x_ref, dx_ref, dw_ref, db_ref):
            du1, du2 = _shift_up_edge(du, 1), _shift_up_edge(du, 2)
            dx_ref[...] = (w_ref[2:3, :] * du + w_ref[1:2, :] * du1 + w_ref[0:1, :] * du2).astype(dx_ref.dtype)
            x = x_ref[...].astype(F32)
            dw_ref[0:1, :] += jnp.sum(x * du2, axis=0, keepdims=True)
            dw_ref[1:2, :] += jnp.sum(x * du1, axis=0, keepdims=True)
            dw_ref[2:3, :] += jnp.sum(x * du, axis=0, keepdims=True)
            db_ref[...] += jnp.sum(du, axis=0, keepdims=True)

        conv_bwd(dg, wg_ref, xg_ref, dug_ref, dwg_ref, dbg_ref)
        conv_bwd(dv, wv_ref, xv_ref, duv_ref, dwv_ref, dbv_ref)

    blk = pl.BlockSpec((t, tc), lambda j, b: (b, j))
    wspec = pl.BlockSpec((FFN_CONV, tc), lambda j, b: (0, j))
    bspec = pl.BlockSpec((1, tc), lambda j, b: (0, j))
    outs = _hosted_call(
        body, grid=(nf, bsz),
        in_specs=[blk, pl.BlockSpec((t, tc), lambda j, b: (b, j + nf)), blk, blk, blk,
                  wspec, pl.BlockSpec((FFN_CONV, tc), lambda j, b: (0, j + nf))],
        out_specs=[blk, blk, wspec, wspec, bspec, bspec],
        out_shape=[jax.ShapeDtypeStruct((n, f), MXU), jax.ShapeDtypeStruct((n, f), MXU),
                   jax.ShapeDtypeStruct((FFN_CONV, f), F32), jax.ShapeDtypeStruct((FFN_CONV, f), F32),
                   jax.ShapeDtypeStruct((1, f), F32), jax.ShapeDtypeStruct((1, f), F32)],
        args=(up, up, ug, uv, dact, wconv, wconv), name=name, q=q, budget_us=HOST_US["ffn_act_bwd"])
    dug, duv, dwg, dwv, dbg, dbv = outs
    return dug, duv, jnp.concatenate([dwg, dwv], axis=1), jnp.concatenate([dbg, dbv], axis=1)


def _softmax(s):
    p = jnp.exp(s - jnp.max(s, axis=-1, keepdims=True))
    return p / jnp.sum(p, axis=-1, keepdims=True)


def _mem_attn_fwd(proj, q_col_block, mkv, ycat, bsz, *, name, q=None):
    n = proj.shape[0]
    t = n // bsz
    tq = _tile(t, 512, SUBLANE)
    nt = t // tq
    scale = HEAD_DIM ** -0.5

    def body(q_ref, kv_ref, old_ref, o_ref):
        del old_ref
        heads = range(MEM_HEADS)
        col = lambda ref, h, off=0: ref[:, off + h * HEAD_DIM:off + (h + 1) * HEAD_DIM].astype(MXU)
        ss = [_dot_nt(col(q_ref, h), col(kv_ref, h)) * scale for h in heads]
        ps = [_softmax(s).astype(MXU) for s in ss]
        outs = [_dot(ps[h], col(kv_ref, h, MEM_WIDTH)) for h in heads]
        o_ref[...] = jnp.concatenate(outs, axis=-1).astype(o_ref.dtype)

    return _hosted_call(
        body, grid=(bsz, nt),
        in_specs=[pl.BlockSpec((tq, MEM_WIDTH), lambda b, i: (b * nt + i, q_col_block)),
                  pl.BlockSpec((MEM_LEN, 2 * MEM_WIDTH), lambda b, i: (b, 0)),
                  pl.BlockSpec(memory_space=pl.ANY)],
        out_specs=pl.BlockSpec((tq, MEM_WIDTH), lambda b, i: (b * nt + i, MIX_WIDTH // MEM_WIDTH)),
        out_shape=jax.ShapeDtypeStruct(ycat.shape, ycat.dtype),
        aliases={2: 0}, args=(proj, mkv, ycat), name=name, q=q, budget_us=HOST_US["mem_attn_fwd"])


def _mem_attn_bwd(proj, q_col_block, mkv, dycat, dproj, bsz, *, name, q=None):
    n = proj.shape[0]
    t = n // bsz
    tq = _tile(t, 512, SUBLANE)
    nt = t // tq
    scale = HEAD_DIM ** -0.5

    def body(q_ref, kv_ref, do_ref, old_ref, dq_ref, dkv_ref):
        del old_ref

        @pl.when(pl.program_id(1) == 0)
        def _():
            dkv_ref[...] = jnp.zeros_like(dkv_ref)

        heads = range(MEM_HEADS)
        col = lambda ref, h, off=0: ref[:, off + h * HEAD_DIM:off + (h + 1) * HEAD_DIM].astype(MXU)
        qs = [col(q_ref, h) for h in heads]
        ks = [col(kv_ref, h) for h in heads]
        dos = [col(do_ref, h) for h in heads]
        ps = [_softmax(_dot_nt(qs[h], ks[h]) * scale) for h in heads]
        dps = [_dot_nt(dos[h], col(kv_ref, h, MEM_WIDTH)) for h in heads]
        dss = [(ps[h] * (dps[h] - jnp.sum(dps[h] * ps[h], axis=-1, keepdims=True)) * scale).astype(MXU) for h in heads]
        dvs = [_dot_tn(ps[h].astype(MXU), dos[h]) for h in heads]
        dqs = [_dot(dss[h], ks[h]) for h in heads]
        dks = [_dot_tn(dss[h], qs[h]) for h in heads]
        dq_ref[...] = jnp.concatenate(dqs, axis=-1).astype(dq_ref.dtype)
        dkv_ref[...] += jnp.concatenate(dks + dvs, axis=-1)

    return _hosted_call(
        body, grid=(bsz, nt),
        in_specs=[pl.BlockSpec((tq, MEM_WIDTH), lambda b, i: (b * nt + i, q_col_block)),
                  pl.BlockSpec((MEM_LEN, 2 * MEM_WIDTH), lambda b, i: (b, 0)),
                  pl.BlockSpec((tq, MEM_WIDTH), lambda b, i: (b * nt + i, MIX_WIDTH // MEM_WIDTH)),
                  pl.BlockSpec(memory_space=pl.ANY)],
        out_specs=[pl.BlockSpec((tq, MEM_WIDTH), lambda b, i: (b * nt + i, q_col_block)),
                   pl.BlockSpec((MEM_LEN, 2 * MEM_WIDTH), lambda b, i: (b, 0))],
        out_shape=[jax.ShapeDtypeStruct(dproj.shape, dproj.dtype),
                   jax.ShapeDtypeStruct((bsz * MEM_LEN, 2 * MEM_WIDTH), F32)],
        aliases={3: 0}, args=(proj, mkv, dycat, dproj), name=name, q=q, budget_us=HOST_US["mem_attn_bwd"])


def _swa_probs(s, h, dist, mask, sink):
    s = jnp.where(mask, s * (HEAD_DIM ** -0.5) - SLOPES[h] * dist, -jnp.inf)
    m = jnp.maximum(jnp.max(s, axis=-1, keepdims=True), sink)
    p = jnp.exp(s - m)
    psink = jnp.exp(sink - m)
    inv = 1.0 / (jnp.sum(p, axis=-1, keepdims=True) + psink)
    return p * inv, psink * inv


def _swa_mask(n):
    qi = lax.broadcasted_iota(jnp.int32, (WINDOW, 2 * WINDOW), 0) + WINDOW
    ki = lax.broadcasted_iota(jnp.int32, (WINDOW, 2 * WINDOW), 1)
    dist = qi - ki
    mask = (dist >= 0) & (dist < WINDOW) & ((n > 0) | (ki >= WINDOW))
    return dist.astype(F32), mask


def _swa_fwd(proj, kv, sinks, bsz, *, name, q=None):
    n_tok = proj.shape[0]
    nb = n_tok // bsz // WINDOW
    kvw = SWA_KV_HEADS * HEAD_DIM

    def body(sink_ref, q_ref, kvp_ref, kvc_ref, o_ref):
        n = pl.program_id(1)
        dist, mask = _swa_mask(n)
        kk = jnp.concatenate([kvp_ref[:, :kvw], kvc_ref[:, :kvw]], axis=0).astype(MXU)
        vv = jnp.concatenate([kvp_ref[:, kvw:], kvc_ref[:, kvw:]], axis=0).astype(MXU)
        heads = range(SWA_HEADS)
        group = lambda x, h: x[:, (h // SWA_GROUP) * HEAD_DIM:(h // SWA_GROUP + 1) * HEAD_DIM]
        ss = [_dot_nt(q_ref[:, h * HEAD_DIM:(h + 1) * HEAD_DIM].astype(MXU), group(kk, h)) for h in heads]
        ps = [_swa_probs(ss[h], h, dist, mask, sink_ref[h])[0].astype(MXU) for h in heads]
        outs = [_dot(ps[h], group(vv, h)) for h in heads]
        o_ref[...] = jnp.concatenate(outs, axis=-1).astype(o_ref.dtype)

    return _hosted_call(
        body, grid=(bsz, nb),
        in_specs=[pl.BlockSpec(memory_space=pltpu.SMEM),
                  pl.BlockSpec((WINDOW, MIX_WIDTH), lambda b, n: (b * nb + n, 0)),
                  pl.BlockSpec((WINDOW, 2 * kvw), lambda b, n: (b * nb + jnp.maximum(n - 1, 0), 0)),
                  pl.BlockSpec((WINDOW, 2 * kvw), lambda b, n: (b * nb + n, 0))],
        out_specs=pl.BlockSpec((WINDOW, MIX_WIDTH), lambda b, n: (b * nb + n, 0)),
        out_shape=jax.ShapeDtypeStruct((n_tok, D_MODEL), MXU),
        args=(sinks, proj, kv, kv), name=name, q=q, budget_us=HOST_US["swa_fwd"])


def _swa_bwd(proj, kv, sinks, dycat, bsz, *, name, q=None):
    n_tok = proj.shape[0]
    nb = n_tok // bsz // WINDOW
    kvw = SWA_KV_HEADS * HEAD_DIM

    def body(sink_ref, q_ref, kvp_ref, kvc_ref, do_ref, dq_ref, dkvc_ref, dkvp_ref, dsink_ref):
        n = pl.program_id(1)

        @pl.when((pl.program_id(0) == 0) & (n == 0))
        def _():
            dsink_ref[...] = jnp.zeros_like(dsink_ref)

        dist, mask = _swa_mask(n)
        kk = jnp.concatenate([kvp_ref[:, :kvw], kvc_ref[:, :kvw]], axis=0).astype(MXU)
        vv = jnp.concatenate([kvp_ref[:, kvw:], kvc_ref[:, kvw:]], axis=0).astype(MXU)
        lane = lax.broadcasted_iota(jnp.int32, (SUBLANE, LANE), 1)
        heads = range(SWA_HEADS)
        group = lambda x, h: x[:, (h // SWA_GROUP) * HEAD_DIM:(h // SWA_GROUP + 1) * HEAD_DIM]
        qs = [q_ref[:, h * HEAD_DIM:(h + 1) * HEAD_DIM].astype(MXU) for h in heads]
        dos = [do_ref[:, h * HEAD_DIM:(h + 1) * HEAD_DIM].astype(MXU) for h in heads]
        ss = [_dot_nt(qs[h], group(kk, h)) for h in heads]
        dps = [_dot_nt(dos[h], group(vv, h)) for h in heads]
        probs = [_swa_probs(ss[h], h, dist, mask, sink_ref[h]) for h in heads]
        rss = [jnp.sum(dps[h] * probs[h][0], axis=-1, keepdims=True) for h in heads]
        dss = [(probs[h][0] * (dps[h] - rss[h]) * (HEAD_DIM ** -0.5)).astype(MXU) for h in heads]
        dqs = [_dot(dss[h], group(kk, h)) for h in heads]
        dk_h = [_dot_tn(dss[h], qs[h]) for h in heads]
        dv_h = [_dot_tn(probs[h][0].astype(MXU), dos[h]) for h in heads]
        dsink = jnp.zeros((SUBLANE, LANE), F32)
        for h in heads:
            dsink = dsink + jnp.where(lane == h, jnp.sum(-probs[h][1] * rss[h], axis=0, keepdims=True), 0.0)
        sum_group = lambda xs, c: functools.reduce(lambda a, b: a + b, xs[c * SWA_GROUP:(c + 1) * SWA_GROUP])
        dks = [sum_group(dk_h, c) for c in range(SWA_KV_HEADS)]
        dvs = [sum_group(dv_h, c) for c in range(SWA_KV_HEADS)]
        dq_ref[...] = jnp.concatenate(dqs, axis=-1).astype(dq_ref.dtype)
        dkv = jnp.concatenate(dks + dvs, axis=-1)
        dkvp_ref[...] = dkv[:WINDOW]
        dkvc_ref[...] = dkv[WINDOW:]
        dsink_ref[...] += dsink

    qspec = pl.BlockSpec((WINDOW, MIX_WIDTH), lambda b, n: (b * nb + n, 0))
    kvspec = pl.BlockSpec((WINDOW, 2 * kvw), lambda b, n: (b * nb + n, 0))
    return _hosted_call(
        body, grid=(bsz, nb),
        in_specs=[pl.BlockSpec(memory_space=pltpu.SMEM), qspec,
                  pl.BlockSpec((WINDOW, 2 * kvw), lambda b, n: (b * nb + jnp.maximum(n - 1, 0), 0)),
                  kvspec, qspec],
        out_specs=[qspec, kvspec, kvspec, pl.BlockSpec((SUBLANE, LANE), lambda b, n: (0, 0))],
        out_shape=[jax.ShapeDtypeStruct((n_tok, D_MODEL), MXU),
                   jax.ShapeDtypeStruct((n_tok, 2 * kvw), F32),
                   jax.ShapeDtypeStruct((n_tok, 2 * kvw), F32),
                   jax.ShapeDtypeStruct((SUBLANE, LANE), F32)],
        args=(sinks, proj, kv, kv, dycat), name=name, q=q, budget_us=HOST_US["swa_bwd"])


def _swa_dkv_combine(curs, prevs, bsz, *, name):
    n_tok, w = curs[0].shape
    nb = n_tok // bsz // WINDOW
    k = len(curs)

    def body(*refs):
        o_ref = refs[-1]
        n = pl.program_id(1)
        acc = refs[0][...]
        for r in refs[1:k]:
            acc = acc + r[...]
        nxt = refs[k][...]
        for r in refs[k + 1:2 * k]:
            nxt = nxt + r[...]
        o_ref[...] = (acc + jnp.where(n < nb - 1, nxt, 0.0)).astype(o_ref.dtype)

    cur = pl.BlockSpec((WINDOW, w), lambda b, n: (b * nb + n, 0))
    prv = pl.BlockSpec((WINDOW, w), lambda b, n: (b * nb + jnp.minimum(n + 1, nb - 1), 0))
    return pl.pallas_call(
        body, grid=(bsz, nb), in_specs=[cur] * k + [prv] * k, out_specs=cur,
        out_shape=jax.ShapeDtypeStruct((n_tok, w), MXU),
        name=name, compiler_params=_cp((PAR, PAR)))(*curs, *prevs)


def _lru_gates(ux, halo, ext_ref, wc_ref, bc_ref, wr_ref, br_ref, wi_ref, bi_ref, lam_ref):
    tt = ux.shape[0]
    ext_ref[0:SUBLANE, :] = halo
    ext_ref[SUBLANE:, :] = ux
    xs = [ux] + [ext_ref[pl.ds(SUBLANE - k, tt), :] for k in range(1, LRU_CONV)]
    xc = bc_ref[...] + wc_ref[3:4, :] * xs[0] + wc_ref[2:3, :] * xs[1] + wc_ref[1:2, :] * xs[2] + wc_ref[0:1, :] * xs[3]
    pre_r, pre_i = [], []
    for blk in range(MIX_WIDTH // GATE_TILE):
        xb = xc[:, blk * GATE_TILE:(blk + 1) * GATE_TILE].astype(MXU)
        pre_r.append(_dot(xb, wr_ref[blk]))
        pre_i.append(_dot(xb, wi_ref[blk]))
    r = jax.nn.sigmoid(jnp.concatenate(pre_r, axis=-1) + br_ref[...])
    i = jax.nn.sigmoid(jnp.concatenate(pre_i, axis=-1) + bi_ref[...])
    nlam = -lam_ref[...]
    sp = jnp.maximum(nlam, 0.0) + jnp.log(1.0 + jnp.exp(-jnp.abs(nlam)))
    log_a = -LRU_C * r * sp
    a = jnp.exp(log_a)
    om = -jnp.tanh(log_a) * (a * a + 1.0)
    s = jnp.sqrt(om)
    return xs, xc, r, i, sp, a, s


def _lru_fwd(proj, wconv, bconv, wr, br, wi, bi, lam, bsz, *, name, q=None):
    n_tok = proj.shape[0]
    t = n_tok // bsz
    tt = _tile(t, 256, SUBLANE)
    nt = t // tt
    w = MIX_WIDTH
    ng = tt // SUBLANE

    def body(pg_ref, halo_ref, wc_ref, bc_ref, wr_ref, br_ref, wi_ref, bi_ref, lam_ref,
             y_ref, h_ref, ext_ref, a_ref, b_ref, carry_ref):
        ti = pl.program_id(1)

        @pl.when(ti == 0)
        def _():
            carry_ref[...] = jnp.zeros_like(carry_ref)

        gate = pg_ref[:, :w]
        ux = pg_ref[:, w:]
        halo = jnp.where(ti > 0, halo_ref[...], 0.0)
        _, xc, _, i, _, a, s = _lru_gates(ux, halo, ext_ref, wc_ref, bc_ref, wr_ref, br_ref, wi_ref, bi_ref, lam_ref)
        a_ref[...] = a
        b_ref[...] = s * (i * xc)
        row = lax.broadcasted_iota(jnp.int32, (SUBLANE, w), 0)

        def group(g, hprev):
            off = pl.multiple_of(g * SUBLANE, SUBLANE)
            ca = a_ref[pl.ds(off, SUBLANE), :]
            cb = b_ref[pl.ds(off, SUBLANE), :]
            for d in (1, 2, 4):
                a_sh = jnp.where(row >= d, pltpu.roll(ca, d, axis=0), 1.0)
                b_sh = jnp.where(row >= d, pltpu.roll(cb, d, axis=0), 0.0)
                cb = ca * b_sh + cb
                ca = ca * a_sh
            h = ca * hprev + cb
            b_ref[pl.ds(off, SUBLANE), :] = h
            return jnp.broadcast_to(h[SUBLANE - 1:SUBLANE, :], (SUBLANE, w))

        carry_ref[...] = lax.fori_loop(0, ng, group, carry_ref[...])
        h = b_ref[...]
        h_ref[...] = h
        y_ref[...] = (h * _gelu(gate)).astype(y_ref.dtype)

    vec = lambda r: pl.BlockSpec((r, w), lambda b, i: (0, 0))
    wspec = pl.BlockSpec((w // GATE_TILE, GATE_TILE, GATE_TILE), lambda b, i: (0, 0, 0))
    hb = tt // SUBLANE
    return _hosted_call(
        body, grid=(bsz, nt),
        in_specs=[pl.BlockSpec((tt, 2 * w), lambda b, i: (b * nt + i, 0)),
                  pl.BlockSpec((SUBLANE, w), lambda b, i: (jnp.maximum((b * nt + i) * hb - 1, 0), 1)),
                  vec(LRU_CONV), vec(1), wspec, vec(1), wspec, vec(1), vec(1)],
        out_specs=[pl.BlockSpec((tt, w), lambda b, i: (b * nt + i, 0)),
                   pl.BlockSpec((tt, w), lambda b, i: (b * nt + i, 0))],
        out_shape=[jax.ShapeDtypeStruct((n_tok, D_MODEL), MXU), jax.ShapeDtypeStruct((n_tok, w), F32)],
        scratch_shapes=[pltpu.VMEM((tt + SUBLANE, w), F32), pltpu.VMEM((tt, w), F32),
                        pltpu.VMEM((tt, w), F32), pltpu.VMEM((SUBLANE, w), F32)],
        args=(proj, proj, wconv, bconv, wr, br, wi, bi, lam), name=name, q=q, budget_us=HOST_US["lru_fwd"])


def _lru_bwd(proj, hs, dycat, wconv, bconv, wr, br, wi, bi, lam, bsz, *, name, q=None):
    n_tok = proj.shape[0]
    t = n_tok // bsz
    tt = _tile(t, 256, SUBLANE)
    nt = t // tt
    w = MIX_WIDTH
    ng = tt // SUBLANE
    nblk = w // GATE_TILE

    def body(pg_ref, halo_ref, h_ref, hhalo_ref, dy_ref, wc_ref, bc_ref, wr_ref, br_ref, wi_ref, bi_ref, lam_ref,
             dp_ref, dwc_ref, dbc_ref, dwr_ref, dbr_ref, dwi_ref, dbi_ref, dlam_ref,
             ext_ref, a_ref, c_ref, g_ref, gcarry_ref, xcarry_ref):
        bi_ = pl.program_id(0)
        ti = nt - 1 - pl.program_id(1)

        @pl.when((bi_ == 0) & (pl.program_id(1) == 0))
        def _():
            for r in (dwc_ref, dbc_ref, dwr_ref, dbr_ref, dwi_ref, dbi_ref, dlam_ref):
                r[...] = jnp.zeros_like(r)

        @pl.when(pl.program_id(1) == 0)
        def _():
            gcarry_ref[...] = jnp.zeros_like(gcarry_ref)
            xcarry_ref[...] = jnp.zeros_like(xcarry_ref)

        gate = pg_ref[:, :w]
        ux = pg_ref[:, w:]
        halo = jnp.where(ti > 0, halo_ref[...], 0.0)
        xs, xc, r, i, sp, a, s = _lru_gates(ux, halo, ext_ref, wc_ref, bc_ref, wr_ref, br_ref, wi_ref, bi_ref, lam_ref)
        h = h_ref[...]
        gl, dgl = _gelu_and_grad(gate)
        dy = dy_ref[...].astype(F32)
        dgate = dy * h * dgl
        row_t = lax.broadcasted_iota(jnp.int32, (tt, w), 0)
        g_ref[...] = dy * gl + jnp.where(row_t == tt - 1, gcarry_ref[0:1, :], 0.0)
        c_ref[...] = _shift_up(a, 1, row_t)
        row = lax.broadcasted_iota(jnp.int32, (SUBLANE, w), 0)

        a_ref[...] = a

        def group(k, gnext):
            off = pl.multiple_of((ng - 1 - k) * SUBLANE, SUBLANE)
            cc = c_ref[pl.ds(off, SUBLANE), :]
            cb = g_ref[pl.ds(off, SUBLANE), :]
            cb = cb + jnp.where(row == SUBLANE - 1, gnext, 0.0)
            cc = jnp.where(row == SUBLANE - 1, 0.0, cc)
            for d in (1, 2, 4):
                c_sh = jnp.where(row < SUBLANE - d, pltpu.roll(cc, SUBLANE - d, axis=0), 1.0)
                b_sh = jnp.where(row < SUBLANE - d, pltpu.roll(cb, SUBLANE - d, axis=0), 0.0)
                cb = cc * b_sh + cb
                cc = cc * c_sh
            g_ref[pl.ds(off, SUBLANE), :] = cb
            a0 = a_ref[pl.ds(off, SUBLANE), :]
            return jnp.broadcast_to(a0[0:1, :] * cb[0:1, :], (SUBLANE, w))

        gc = lax.fori_loop(0, ng, group, jnp.zeros((SUBLANE, w), F32))
        gcarry_ref[...] = gc
        gsc = g_ref[...]

        hhalo = jnp.where(ti > 0, hhalo_ref[SUBLANE - 1:SUBLANE, :], 0.0)
        hprev = jnp.where(row_t == 0, hhalo, pltpu.roll(h, 1, axis=0))
        gated = i * xc
        d_gated = gsc * s
        d_atot = gsc * hprev - (gsc * gated) * a / s
        d_loga = d_atot * a
        d_r = d_loga * (-LRU_C) * sp
        dlam_ref[...] += jnp.sum(d_loga * r, axis=0, keepdims=True) * (LRU_C * jax.nn.sigmoid(-lam_ref[...]))
        d_i = d_gated * xc
        d_xc = d_gated * i
        d_pr = d_r * r * (1.0 - r)
        d_pi = d_i * i * (1.0 - i)
        dbr_ref[...] += jnp.sum(d_pr, axis=0, keepdims=True)
        dbi_ref[...] += jnp.sum(d_pi, axis=0, keepdims=True)
        extra = []
        for blk in range(nblk):
            sl = slice(blk * GATE_TILE, (blk + 1) * GATE_TILE)
            xb = xc[:, sl].astype(MXU)
            dr_b = d_pr[:, sl].astype(MXU)
            di_b = d_pi[:, sl].astype(MXU)
            dwr_ref[blk] += _dot_tn(xb, dr_b)
            dwi_ref[blk] += _dot_tn(xb, di_b)
            extra.append(_dot_nt(dr_b, wr_ref[blk]) + _dot_nt(di_b, wi_ref[blk]))
        d_xc = d_xc + jnp.concatenate(extra, axis=-1)
        dbc_ref[...] += jnp.sum(d_xc, axis=0, keepdims=True)
        for k in range(LRU_CONV):
            dwc_ref[k:k + 1, :] += jnp.sum(d_xc * xs[LRU_CONV - 1 - k], axis=0, keepdims=True)
        ext_ref[0:tt, :] = d_xc
        ext_ref[tt:, :] = xcarry_ref[...]
        dux = wc_ref[3:4, :] * d_xc
        for k in range(LRU_CONV - 1):
            dux = dux + wc_ref[k:k + 1, :] * ext_ref[pl.ds(LRU_CONV - 1 - k, tt), :]
        xcarry_ref[...] = d_xc[0:SUBLANE, :]
        dp_ref[:, :w] = dgate.astype(dp_ref.dtype)
        dp_ref[:, w:] = dux.astype(dp_ref.dtype)

    vec = lambda r: pl.BlockSpec((r, w), lambda b, i: (0, 0))
    wspec = pl.BlockSpec((nblk, GATE_TILE, GATE_TILE), lambda b, i: (0, 0, 0))
    hb = tt // SUBLANE
    rblk = lambda b, i: b * nt + (nt - 1 - i)
    halo_idx = lambda b, i: jnp.maximum(rblk(b, i) * hb - 1, 0)
    wide = pl.BlockSpec((tt, 2 * w), lambda b, i: (rblk(b, i), 0))
    narrow = pl.BlockSpec((tt, w), lambda b, i: (rblk(b, i), 0))
    return _hosted_call(
        body, grid=(bsz, nt),
        in_specs=[wide, pl.BlockSpec((SUBLANE, w), lambda b, i: (halo_idx(b, i), 1)),
                  narrow, pl.BlockSpec((SUBLANE, w), lambda b, i: (halo_idx(b, i), 0)), narrow,
                  vec(LRU_CONV), vec(1), wspec, vec(1), wspec, vec(1), vec(1)],
        out_specs=[wide, vec(LRU_CONV), vec(1), wspec, vec(1), wspec, vec(1), vec(1)],
        out_shape=[jax.ShapeDtypeStruct((n_tok, 2 * w + MEM_WIDTH), MXU),
                   jax.ShapeDtypeStruct((LRU_CONV, w), F32), jax.ShapeDtypeStruct((1, w), F32),
                   jax.ShapeDtypeStruct((nblk, GATE_TILE, GATE_TILE), F32), jax.ShapeDtypeStruct((1, w), F32),
                   jax.ShapeDtypeStruct((nblk, GATE_TILE, GATE_TILE), F32), jax.ShapeDtypeStruct((1, w), F32),
                   jax.ShapeDtypeStruct((1, w), F32)],
        scratch_shapes=[pltpu.VMEM((tt + SUBLANE, w), F32), pltpu.VMEM((tt, w), F32), pltpu.VMEM((tt, w), F32),
                        pltpu.VMEM((tt, w), F32), pltpu.VMEM((SUBLANE, w), F32), pltpu.VMEM((SUBLANE, w), F32)],
        args=(proj, proj, hs, hs, dycat, wconv, bconv, wr, br, wi, bi, lam), name=name, q=q,
        budget_us=HOST_US["lru_bwd"])


def _gate_tiles(w):
    per = GATE_TILE // HEAD_DIM
    w4 = w.reshape(LRU_BLOCKS // per, per, HEAD_DIM, HEAD_DIM)
    eye = jnp.eye(per, dtype=w.dtype)
    return jnp.einsum("bnij,nm->bnimj", w4, eye).reshape(LRU_BLOCKS // per, GATE_TILE, GATE_TILE)


def _gate_blocks(t):
    per = GATE_TILE // HEAD_DIM
    t5 = t.reshape(LRU_BLOCKS // per, per, HEAD_DIM, per, HEAD_DIM)
    eye = jnp.eye(per, dtype=t.dtype)
    return jnp.einsum("bnimj,nm->bnij", t5, eye).reshape(LRU_BLOCKS, HEAD_DIM, HEAD_DIM)


def _row(v):
    return v.reshape(1, -1)


def _local_step(x, mem, target, p, wfull, push_grad, q):
    bsz, t, d = x.shape
    n = bsz * t
    x2d = x.reshape(n, d)
    tgt = target.reshape(n, d)
    mem2d = mem.reshape(bsz * MEM_LEN, d)
    wr_t = [_gate_tiles(p["w_rg_r"][j]).astype(MXU) for j in range(N_A)]
    wi_t = [_gate_tiles(p["w_rg_i"][j]).astype(MXU) for j in range(N_A)]

    mn = [_norm_fwd(mem2d, _row(p["g_mem"][l]), name=f"mem_norm{l}") for l in range(DEPTH)]
    mkv = [None] * DEPTH
    h = _norm_fwd(x2d, _row(p["g_mix_pre"][0]), name="in_norm")
    xin = x2d
    sv = []
    kv = hkv = None
    for l in range(DEPTH):
        s = {"xin": xin, "h": h}
        if q is not None:
            q.horizon = (l + 2) * GROUPS_PER_LAYER
        mkv[l] = _mm_nn(mn[l], wfull("w_mem_kv", l), name=f"mem_kv{l}", q=q)
        if l < N_A:
            proj = _mm_nn(h, wfull("w_in_a", l), name=f"in_proj{l}", q=q)
            ycat, hs = _lru_fwd(proj, p["w_conv_a"][l], _row(p["b_conv_a"][l]), wr_t[l], _row(p["b_rg_r"][l]),
                                wi_t[l], _row(p["b_rg_i"][l]), _row(p["lru_lambda"][l]), bsz, name=f"lru_fwd{l}", q=q)
            s["hs"] = hs
            qblk = 2 * MIX_WIDTH // MEM_WIDTH
        else:
            if l == N_A:
                kv = _mm_nn(hkv, wfull("w_kv", 0), name="kv_proj", q=q)
            proj = _mm_nn(h, wfull("w_in_b", l - N_A), name=f"in_proj{l}", q=q)
            ycat = _swa_fwd(proj, kv, p["sinks_b"][l - N_A], bsz, name=f"swa_fwd{l}", q=q)
            qblk = MIX_WIDTH // MEM_WIDTH
        ycat = _mem_attn_fwd(proj, qblk, mkv[l], ycat, bsz, name=f"mem_attn_fwd{l}", q=q)
        y = _mm_nn(ycat, wfull("w_mix_out", l), name=f"mix_out{l}", q=q, out_dtype=MXU)
        x1, (h2,) = _resid_norm_fwd(xin, y, _row(p["g_mix_post"][l]), [_row(p["g_ffn_pre"][l])], name=f"mix_resid{l}", q=q)
        up = _mm_nn_slots(h2, wfull("w_ffn_up", l), name=f"ffn_up{l}", q=q, out_dtype=MXU)
        act, ug, uv = _ffn_act_fwd(up, p["w_ffn_conv"][l], _row(p["b_ffn_conv"][l]), bsz, name=f"ffn_act{l}", q=q)
        f = _mm_nn(act, wfull("w_ffn_down", l), name=f"ffn_down{l}", q=q, out_dtype=MXU)
        s.update(proj=proj, qblk=qblk, ycat=ycat, y=y, x1=x1, h2=h2, up=up, ug=ug, uv=uv, act=act, f=f)
        sv.append(s)
        if l < DEPTH - 1:
            g_pres = [_row(p["g_mix_pre"][l + 1])] + ([_row(p["g_kv"])] if l + 1 == N_A else [])
            xin, hn = _resid_norm_fwd(x1, f, _row(p["g_ffn_post"][l]), g_pres, name=f"ffn_resid{l}", q=q)
            h = hn[0]
            if l + 1 == N_A:
                hkv = hn[1]
        else:
            g_tot, sq, df, g_post_last = _loss_fwd(x1, f, _row(p["g_ffn_post"][l]), tgt, name="loss")

    if q is not None:
        q.horizon = LAST_GROUP
    gs = {k: [None] * DEPTH for k in ("g_mix_pre", "g_mix_post", "g_ffn_pre", "g_ffn_post", "g_mem",
                                       "w_ffn_conv", "b_ffn_conv")}
    ga = {k: [None] * N_A for k in ("w_conv_a", "b_conv_a", "w_rg_r", "b_rg_r", "w_rg_i", "b_rg_i", "lru_lambda")}
    gsink = [None] * (DEPTH - N_A)
    dkv_cur, dkv_prev = [], []
    gs["g_ffn_post"][DEPTH - 1] = g_post_last
    grad_x = None
    for l in reversed(range(DEPTH)):
        s = sv[l]
        dact = _mm_nt(df, wfull("w_ffn_down", l), name=f"d_act{l}", q=q, out_dtype=MXU)
        push_grad("w_ffn_down", l, _mm_tn(s["act"], df, name=f"dw_down{l}", q=q))
        dug, duv, gs["w_ffn_conv"][l], gs["b_ffn_conv"][l] = _ffn_act_bwd(
            s["up"], s["ug"], s["uv"], dact, p["w_ffn_conv"][l], bsz, name=f"ffn_act_bwd{l}", q=q)
        dh2 = _mm_ffn_dh(dug, duv, wfull("w_ffn_up", l), name=f"d_h2_{l}", q=q)
        up_slots = dict(slot_cols=2 * D_FF // N_CHIP, n_slots=N_CHIP)
        dwu = _mm_tn_slots(s["h2"], dug, name=f"dw_up_g{l}", q=q, **up_slots)
        push_grad("w_ffn_up", l, _mm_tn_slots(s["h2"], duv, name=f"dw_up_v{l}", q=q, out=dwu,
                                              first_slot=N_CHIP // 2, **up_slots))
        g1, dy, (gs["g_ffn_pre"][l],), gs["g_mix_post"][l] = _resid_norm_bwd(
            g_tot, [dh2], s["x1"], [_row(p["g_ffn_pre"][l])], s["y"], _row(p["g_mix_post"][l]), name=f"mix_resid_bwd{l}", q=q)
        dycat = _mm_nt(dy, wfull("w_mix_out", l), name=f"d_ycat{l}", q=q, out_dtype=MXU)
        push_grad("w_mix_out", l, _mm_tn(s["ycat"], dy, name=f"dw_mix_out{l}", q=q))
        if l < N_A:
            dproj, dwc, dbc, dwr, dbr, dwi, dbi, dlam = _lru_bwd(
                s["proj"], s["hs"], dycat, p["w_conv_a"][l], _row(p["b_conv_a"][l]), wr_t[l], _row(p["b_rg_r"][l]),
                wi_t[l], _row(p["b_rg_i"][l]), _row(p["lru_lambda"][l]), bsz, name=f"lru_bwd{l}", q=q)
            ga["w_conv_a"][l], ga["b_conv_a"][l], ga["lru_lambda"][l] = dwc, dbc[0], dlam[0]
            ga["w_rg_r"][l], ga["w_rg_i"][l] = _gate_blocks(dwr), _gate_blocks(dwi)
            ga["b_rg_r"][l] = dbr.reshape(LRU_BLOCKS, HEAD_DIM)
            ga["b_rg_i"][l] = dbi.reshape(LRU_BLOCKS, HEAD_DIM)
            w_in, j = "w_in_a", l
        else:
            dproj, dc, dp_, dsk = _swa_bwd(s["proj"], kv, p["sinks_b"][l - N_A], dycat, bsz, name=f"swa_bwd{l}", q=q)
            dkv_cur.append(dc)
            dkv_prev.append(dp_)
            gsink[l - N_A] = dsk[0, :SWA_HEADS]
            w_in, j = "w_in_b", l - N_A
        dproj, dmkv = _mem_attn_bwd(s["proj"], s["qblk"], mkv[l], dycat, dproj, bsz, name=f"mem_attn_bwd{l}", q=q)
        dh = _mm_nt(dproj, wfull(w_in, j), name=f"d_h{l}", q=q, out_dtype=MXU)
        push_grad(w_in, j, _mm_tn(s["h"], dproj, name=f"dw_in{l}", q=q))
        dmkv = dmkv.astype(MXU)
        dmn = _mm_nt(dmkv, wfull("w_mem_kv", l), name=f"d_mem_norm{l}", q=q)
        push_grad("w_mem_kv", l, _mm_tn(mn[l], dmkv, name=f"dw_mem_kv{l}", q=q))
        gs["g_mem"][l] = _norm_bwd_dg(dmn, mem2d, _row(p["g_mem"][l]), name=f"mem_norm_bwd{l}")
        dhs, g_pres = [dh], [_row(p["g_mix_pre"][l])]
        if l == N_A:
            dkv = _swa_dkv_combine(dkv_cur, dkv_prev, bsz, name="dkv_combine")
            dhs.append(_mm_nt(dkv, wfull("w_kv", 0), name="d_hkv", q=q, out_dtype=MXU))
            g_pres.append(_row(p["g_kv"]))
            push_grad("w_kv", 0, _mm_tn(hkv, dkv, name="dw_kv", q=q))
        if l > 0:
            g_tot, df, dgpre, gs["g_ffn_post"][l - 1] = _resid_norm_bwd(
                g1, dhs, s["xin"], g_pres, sv[l - 1]["f"], _row(p["g_ffn_post"][l - 1]), name=f"ffn_resid_bwd{l - 1}", q=q)
        else:
            grad_x, _, dgpre, _ = _resid_norm_bwd(g1, dhs, s["xin"], g_pres, None, None, name="in_norm_bwd", q=q)
        gs["g_mix_pre"][l] = dgpre[0]
        if l == N_A:
            g_kv = dgpre[1][0]

    grads = {}
    for k in ("g_mix_pre", "g_mix_post", "g_ffn_pre", "g_ffn_post", "g_mem", "b_ffn_conv"):
        grads[k] = jnp.concatenate(gs[k], axis=0)
    grads["w_ffn_conv"] = jnp.stack(gs["w_ffn_conv"])
    for k, v in ga.items():
        grads[k] = jnp.stack(v)
    grads["sinks_b"] = jnp.stack(gsink)
    grads["g_kv"] = g_kv
    return jnp.sum(sq), grad_x.reshape(bsz, t, d), grads


N_CHIP = 4
HALF_ALIGN = 16
D2D_STREAMS = 2
MIN_PART_BYTES = 128 * 1024


def _full_shape(kind, shard_shape):
    l, r, c = shard_shape
    return {"row": (l, N_CHIP * r, c), "col": (l, r, N_CHIP * c), "slot": (N_CHIP, l, r, c)}[kind]


def _slot_view(ref, kind, shard_shape, s, hf, sub=(0, 1)):
    _, r, c = shard_shape
    rh = r // 2
    if hf is None:
        size = r // sub[1]
        start = sub[0] * size
    else:
        size = rh // sub[1]
        start = hf * rh + sub[0] * size
    if kind == "row":
        start = s * r + start
    if not isinstance(start, int):
        start = pl.multiple_of(start, HALF_ALIGN)
    rows = pl.ds(start, size)
    if kind == "row":
        return ref.at[:, rows, :]
    if kind == "col":
        return ref.at[:, rows, pl.ds(s * c, c)]
    return ref.at[s, :, rows, :]


def _half_view(ref, shard_shape, hf, sub=(0, 1)):
    rh = shard_shape[1] // 2
    size = rh // sub[1]
    return ref.at[:, pl.ds(pl.multiple_of(hf * rh + sub[0] * size, HALF_ALIGN), size), :]


def _with_slot(kind, s, fn):
    if kind != "col" or isinstance(s, int):
        fn(s)
        return
    for k in range(N_CHIP):
        @pl.when(s == k)
        def _(k=k):
            fn(k)


def _mesh_pos():
    return lax.axis_index("x"), lax.axis_index("y"), lax.axis_index("c")


def _other_chips(x, y):
    return [(1 - x, y), (x, 1 - y), (1 - x, 1 - y)]


ICI_BYTES_PER_US = 6.0e4
ICI_GATHER_BYTES_PER_US = 5.5e4
D2D_BYTES_PER_US = 4.0e5


class _Chunk:
    def __init__(self, group, cost, ins, out_shapes, alias, n_sem, start, finish, done, buffer=None, bind=None):
        self.group, self.cost, self.ins, self.out_shapes, self.alias, self.n_sem = group, cost, ins, out_shapes, alias, n_sem
        self.start, self.finish, self.done = start, finish, done
        self.buffer = buffer
        self.bind = bind

    def prepare(self):
        if self.bind is not None:
            self.bind(self)


def _merged(chunks):
    groups, by_buffer = [], {}
    for ch in chunks:
        key = None if ch.buffer is None else (id(ch.buffer[0]), ch.buffer[1])
        if key is not None and key in by_buffer:
            by_buffer[key].append(ch)
        else:
            groups.append([ch])
            if key is not None:
                by_buffer[key] = groups[-1]
    out = []
    for parts in groups:
        if len(parts) == 1:
            out.append(parts[0])
            continue
        offs = [sum(p.n_sem for p in parts[:i]) for i in range(len(parts))]

        def run(phase, ins, outs, ss, rs, b, parts=parts, offs=offs):
            for p, o in zip(parts, offs):
                getattr(p, phase)(ins, outs, ss, rs, b + o)

        def done(outs, parts=parts):
            for p in parts:
                p.done(outs)

        first = parts[0]
        out.append(_Chunk(first.group, sum(p.cost for p in parts), first.ins, first.out_shapes, first.alias,
                          sum(p.n_sem for p in parts), functools.partial(run, "start"),
                          functools.partial(run, "finish"), done))
    return out


LAST_GROUP = 1 << 30
MIN_CARRIED_US = 8.0


class _CommQueue:
    def __init__(self):
        self.pending = []
        self.flushes = 0
        self.horizon = LAST_GROUP

    def push(self, chunk):
        self.pending.append(chunk)

    def take(self, budget_us):
        got, used = [], 0.0
        for ch in sorted(self.pending, key=lambda ch: (ch.group, -ch.cost)):
            if ch.group >= self.horizon and ch.group != LAST_GROUP:
                continue
            if used + ch.cost <= budget_us and not self._shares_buffer(ch, got):
                got.append(ch)
                used += ch.cost
        if used < MIN_CARRIED_US:
            return []
        return self._taken(got)

    @staticmethod
    def _shares_buffer(ch, others):
        return ch.buffer is not None and any(
            o.buffer is not None and o.buffer[0] is ch.buffer[0] and o.buffer[1] != ch.buffer[1] for o in others)

    def _taken(self, got):
        self.pending = [ch for ch in self.pending if ch not in got]
        for ch in got:
            ch.prepare()
        return _merged(got)

    def flush(self, group=LAST_GROUP):
        while True:
            chunks = []
            for ch in self.pending:
                if ch.group <= group and not self._shares_buffer(ch, chunks):
                    chunks.append(ch)
            if not chunks:
                return
            _run_chunks(self._taken(chunks), name=f"comm_flush{self.flushes}")
            self.flushes += 1


def _run_chunks(chunks, *, name):
    ins = [a for ch in chunks for a in ch.ins]
    outs = [s for ch in chunks for s in ch.out_shapes]
    alias, offs = {}, []
    i0 = o0 = s0 = 0
    for ch in chunks:
        offs.append((i0, o0, s0))
        for ci, co in ch.alias.items():
            alias[i0 + ci] = o0 + co
        i0 += len(ch.ins)
        o0 += len(ch.out_shapes)
        s0 += ch.n_sem

    def body(*refs):
        send_sems, recv_sems = refs[i0 + o0:]
        for phase in ("start", "finish"):
            for ch, (a, b, s) in zip(chunks, offs):
                getattr(ch, phase)(refs[a:a + len(ch.ins)], refs[i0 + b:i0 + b + len(ch.out_shapes)],
                                   send_sems, recv_sems, s)

    hbm = pl.BlockSpec(memory_space=pl.ANY)
    res = pl.pallas_call(
        body, in_specs=[hbm] * i0, out_specs=[hbm] * o0, out_shape=outs,
        scratch_shapes=[pltpu.SemaphoreType.DMA((s0,)), pltpu.SemaphoreType.DMA((s0,))],
        input_output_aliases=alias, name=name, compiler_params=pltpu.CompilerParams(has_side_effects=True))(*ins)
    for ch, (_, b, _) in zip(chunks, offs):
        ch.done(list(res[b:b + len(ch.out_shapes)]))


def _remote(src, dst, send_sems, recv_sems, k, dev):
    return pltpu.make_async_remote_copy(src_ref=src, dst_ref=dst, send_sem=send_sems.at[k], recv_sem=recv_sems.at[k],
                                        device_id=dev, device_id_type=MESH_T)


def _gather_chunks(q, group, kind, shard, l, ready):
    _, r, c = shard.shape
    shp = (1, r, c)
    rh = r // 2
    parts = max(p for p in (8, 4, 2, 1)
                if (rh // p) % HALF_ALIGN == 0 and (p == 1 or (rh // p) * c * shard.dtype.itemsize >= MIN_PART_BYTES))
    part_bytes = (rh // parts) * c * shard.dtype.itemsize
    full_type = jax.ShapeDtypeStruct(_full_shape(kind, shp), shard.dtype)
    state = {"full": None, "parts_done": 0}

    def bind_first(ch):
        ch.ins, ch.alias = ([shard], {}) if state["full"] is None else ([shard, state["full"]], {1: 0})

    def bind_full(ch):
        ch.ins = [state["full"]]

    def make_part(p):
        sub = (p, parts)

        def any_part(full):
            return _slot_view(full, kind, shp, 0, 0, sub)

        def own_rows(src):
            return src.at[:, pl.ds(p * (r // parts), r // parts), :]

        def start1(ins, outs, ss, rs, b):
            x, y, c_ = _mesh_pos()
            src, full = ins[0].at[pl.ds(l, 1)], outs[0]
            _with_slot(kind, 2 * x + y, lambda s: pltpu.make_async_copy(
                own_rows(src), _slot_view(full, kind, shp, s, None, sub), ss.at[b + N_CHIP - 1]).start())
            for j, (ox, oy) in enumerate(_other_chips(x, y)):
                _with_slot(kind, 2 * x + y, lambda s, j=j, ox=ox, oy=oy: _remote(
                    _half_view(src, shp, c_, sub), _slot_view(full, kind, shp, s, c_, sub), ss, rs, b + j,
                    (ox, oy, c_)).start())

        def finish1(ins, outs, ss, rs, b):
            x, y, c_ = _mesh_pos()
            h = any_part(outs[0])
            for j in range(N_CHIP - 1):
                _remote(h, h, ss, rs, b + j, (x, y, 1 - c_)).wait()
            pltpu.make_async_copy(own_rows(ins[0].at[pl.ds(l, 1)]), _slot_view(outs[0], kind, shp, 0, None, sub),
                                  ss.at[b + N_CHIP - 1]).wait()

        def start2(ins, outs, ss, rs, b):
            x, y, c_ = _mesh_pos()
            for j, (ox, oy) in enumerate(_other_chips(x, y)):
                def forward(s, j=j):
                    v = _slot_view(outs[0], kind, shp, s, c_, sub)
                    _remote(v, v, ss, rs, b + j, (x, y, 1 - c_)).start()
                _with_slot(kind, 2 * ox + oy, forward)

        def finish2(ins, outs, ss, rs, b):
            x, y, c_ = _mesh_pos()
            h = any_part(outs[0])
            for j in range(N_CHIP - 1):
                _remote(h, h, ss, rs, b + j, (x, y, 1 - c_)).wait()

        def done2(outs):
            state["full"] = outs[0]
            state["parts_done"] += 1
            if state["parts_done"] == parts:
                ready(outs[0])

        def done1(outs):
            state["full"] = outs[0]
            q.push(_Chunk(group, 3 * part_bytes / D2D_BYTES_PER_US, None, [full_type], {0: 0}, N_CHIP - 1,
                          start2, finish2, done2, buffer=(state, 2), bind=bind_full))

        return _Chunk(group, 3 * part_bytes / ICI_GATHER_BYTES_PER_US, None, [full_type], None,
                      N_CHIP, start1, finish1, done1, buffer=(state, 1), bind=bind_first)

    for p in range(parts):
        q.push(make_part(p))


def _reduce_scatter_chunks(q, kind, grad, shard_shape, pos, name, ready):
    _, r, c = shard_shape
    shp = (1, r, c)
    rh = r // 2

    rp = rh // D2D_STREAMS

    def landing(ref, s, i):
        return ref.at[s, :, pl.ds(i * rp, rp), :]

    def start1(ins, outs, ss, rs, b):
        x, y, c_ = _mesh_pos()
        for s in range(N_CHIP):
            for i in range(D2D_STREAMS):
                _remote(_slot_view(ins[0], kind, shp, s, 1 - c_, (i, D2D_STREAMS)), landing(outs[0], s, i),
                        ss, rs, b + s * D2D_STREAMS + i, (x, y, 1 - c_)).start()

    def finish1(ins, outs, ss, rs, b):
        x, y, c_ = _mesh_pos()
        for s in range(N_CHIP):
            for i in range(D2D_STREAMS):
                v = landing(outs[0], s, i)
                _remote(v, v, ss, rs, b + s * D2D_STREAMS + i, (x, y, 1 - c_)).wait()

    def start2(ins, outs, ss, rs, b):
        x, y, c_ = _mesh_pos()
        for j, (ox, oy) in enumerate(_other_chips(x, y)):
            _remote(ins[0].at[2 * ox + oy], outs[0].at[j], ss, rs, b + j, (ox, oy, c_)).start()

    def finish2(ins, outs, ss, rs, b):
        x, y, c_ = _mesh_pos()
        for j in range(N_CHIP - 1):
            _remote(outs[0].at[j], outs[0].at[j], ss, rs, b + j, (x, y, 1 - c_)).wait()

    def start3(ins, outs, ss, rs, b):
        x, y, c_ = _mesh_pos()
        for i in range(D2D_STREAMS):
            v = _half_view(outs[0], shp, c_, (i, D2D_STREAMS))
            _remote(v, v, ss, rs, b + i, (x, y, 1 - c_)).start()

    def finish3(ins, outs, ss, rs, b):
        x, y, c_ = _mesh_pos()
        for i in range(D2D_STREAMS):
            v = _half_view(outs[0], shp, c_, (i, D2D_STREAMS))
            _remote(v, v, ss, rs, b + i, (x, y, 1 - c_)).wait()

    def done2(pair, outs):
        half = _rs_chip_add(pair, outs[0], shp, pos, name=f"rs_chip_add_{name}")
        q.push(_Chunk(LAST_GROUP, rh * c * 4 / D2D_BYTES_PER_US, [half], [jax.ShapeDtypeStruct(half.shape, half.dtype)],
                      {0: 0}, D2D_STREAMS, start3, finish3, lambda o: ready(o[0])))

    def done1(outs):
        pair, wire = _rs_pair_add(grad, outs[0], kind, shp, pos, name=f"rs_pair_add_{name}")
        q.push(_Chunk(LAST_GROUP, 3 * rh * c * wire.dtype.itemsize / ICI_BYTES_PER_US, [wire],
                      [jax.ShapeDtypeStruct((N_CHIP - 1, 1, rh, c), wire.dtype)], {}, N_CHIP - 1,
                      start2, finish2, functools.partial(done2, pair)))

    q.push(_Chunk(LAST_GROUP, N_CHIP * rh * c * 4 / D2D_BYTES_PER_US, [grad],
                  [jax.ShapeDtypeStruct((N_CHIP, 1, rh, c), F32)], {}, N_CHIP * D2D_STREAMS, start1, finish1, done1))


N_DEV = 8


def _allgather_chunk(q, group, vec, ready):
    def peer(k, x, y, c):
        return ((1 - x) if k & 4 else x, (1 - y) if k & 2 else y, (1 - c) if k & 1 else c)

    def start(ins, outs, ss, rs, b):
        x, y, c = _mesh_pos()
        me = 4 * x + 2 * y + c
        pltpu.make_async_copy(ins[0], outs[0].at[me], ss.at[b + N_DEV - 1]).start()
        for k in range(1, N_DEV):
            _remote(ins[0], outs[0].at[me], ss, rs, b + k - 1, peer(k, x, y, c)).start()

    def finish(ins, outs, ss, rs, b):
        x, y, c = _mesh_pos()
        for k in range(1, N_DEV):
            _remote(ins[0], outs[0].at[0], ss, rs, b + k - 1, peer(k, x, y, c)).wait()
        pltpu.make_async_copy(ins[0], outs[0].at[0], ss.at[b + N_DEV - 1]).wait()

    bytes_in = (N_DEV - 2) * vec.size * 4
    q.push(_Chunk(group, bytes_in / ICI_BYTES_PER_US, [vec], [jax.ShapeDtypeStruct((N_DEV,) + vec.shape, F32)], {},
                  N_DEV, start, finish, lambda o: ready(o[0])))


def _allreduce8(vec, *, name):
    r = vec.shape[0]
    rh = r // 2

    def body(v_ref, o_ref, sib_ref, chips_ref, send_sems, recv_sems):
        x, y, c = _mesh_pos()
        sib = (x, y, 1 - c)
        me = 2 * x + y
        pair = _remote(v_ref, sib_ref, send_sems, recv_sems, 0, sib)
        pair.start()
        pair.wait()
        rows = pl.ds(pl.multiple_of(c * rh, SUBLANE), rh)
        chips_ref[me] = v_ref[rows, :] + sib_ref[rows, :]
        copies = []
        for j, (ox, oy) in enumerate(_other_chips(x, y)):
            cp = _remote(chips_ref.at[me], chips_ref.at[me], send_sems, recv_sems, 1 + j, (ox, oy, c))
            cp.start()
            copies.append(cp)
        for cp in copies:
            cp.wait()
        acc = chips_ref[0]
        for s in range(1, N_CHIP):
            acc = acc + chips_ref[s]
        o_ref[rows, :] = acc
        swap = _remote(o_ref.at[rows, :], o_ref.at[rows, :], send_sems, recv_sems, N_CHIP, sib)
        swap.start()
        swap.wait()

    vm = pl.BlockSpec(memory_space=pltpu.VMEM)
    return pl.pallas_call(
        body, in_specs=[vm], out_specs=vm, out_shape=jax.ShapeDtypeStruct((r, LANE), F32),
        scratch_shapes=[pltpu.VMEM((r, LANE), F32), pltpu.VMEM((N_CHIP, rh, LANE), F32),
                        pltpu.SemaphoreType.DMA((N_CHIP + 1,)), pltpu.SemaphoreType.DMA((N_CHIP + 1,))],
        name=name, compiler_params=pltpu.CompilerParams(has_side_effects=True, vmem_limit_bytes=VMEM_LIMIT_V7X))(vec)


def _rs_pair_add(g, recv, kind, shape, pos, *, name):
    l, r, c = shape
    assert l == 1
    rh = r // 2
    if kind == "row":
        gspec = pl.BlockSpec((None, rh, c), lambda s, pos: (0, 2 * s + pos[0], 0))
    else:
        gspec = pl.BlockSpec((None, None, rh, c), lambda s, pos: (s, 0, pos[0], 0))
    pspec = pl.BlockSpec((None, None, rh, c), lambda s, pos: (s, 0, 0, 0))

    def body(pos_ref, g_ref, r_ref, own_ref, pw_ref):
        v = g_ref[...] + r_ref[...]
        pw_ref[...] = v.astype(pw_ref.dtype)

        @pl.when(pl.program_id(0) == pos_ref[1])
        def _():
            own_ref[...] = v

    return pl.pallas_call(
        body,
        grid_spec=pltpu.PrefetchScalarGridSpec(
            num_scalar_prefetch=1, grid=(N_CHIP,), in_specs=[gspec, pspec],
            out_specs=[pl.BlockSpec((None, rh, c), lambda s, pos: (0, 0, 0)), pspec]),
        out_shape=[jax.ShapeDtypeStruct((1, rh, c), F32), jax.ShapeDtypeStruct((N_CHIP, 1, rh, c), MXU)],
        name=name, compiler_params=_cp((ARB,)))(pos, g, recv)


def _rs_chip_add(p, recv, shape, pos, *, name):
    l, r, c = shape
    rh = r // 2

    def body(pos_ref, p_ref, r_ref, o_ref):
        del pos_ref
        acc = p_ref[...]
        for j in range(N_CHIP - 1):
            acc = acc + r_ref[j].astype(F32)
        o_ref[...] = acc

    return pl.pallas_call(
        body,
        grid_spec=pltpu.PrefetchScalarGridSpec(
            num_scalar_prefetch=1, grid=(l,),
            in_specs=[pl.BlockSpec((None, rh, c), lambda i, pos: (i, 0, 0)),
                      pl.BlockSpec((N_CHIP - 1, None, rh, c), lambda i, pos: (0, i, 0, 0))],
            out_specs=pl.BlockSpec((None, rh, c), lambda i, pos: (i, pos[0], 0))),
        out_shape=jax.ShapeDtypeStruct((l, r, c), F32),
        name=name, compiler_params=_cp((PAR,)))(pos, p, recv)


ADAM_BLOCK_ELEMS = 384 * 1024


def _adam_math(w, g, m, v):
    c1 = 1.0 / (1.0 - ADAM_B1 ** ADAM_STEP)
    c2 = 1.0 / (1.0 - ADAM_B2 ** ADAM_STEP)
    nm = ADAM_B1 * m + (1.0 - ADAM_B1) * g
    nv = ADAM_B2 * v + (1.0 - ADAM_B2) * (g * g)
    return -ADAM_LR * ((nm * c1) / (jnp.sqrt(nv * c2) + ADAM_EPS) + ADAM_WD * w), nm, nv


def _adamw_layer(w, g, m, v, outs, l, *, name):
    _, r, c = w.shape
    tr = _tile(r, max(SUBLANE, ADAM_BLOCK_ELEMS // c // SUBLANE * SUBLANE), SUBLANE)

    def body(w_ref, g_ref, m_ref, v_ref, *rest):
        go_ref, d_ref, nm_ref, nv_ref = rest[4:]
        gg = g_ref[...]
        go_ref[...] = gg
        d_ref[...], nm_ref[...], nv_ref[...] = _adam_math(w_ref[...], gg, m_ref[...], v_ref[...])

    lay = pl.BlockSpec((None, tr, c), lambda j: (l, j, 0))
    hbm = pl.BlockSpec(memory_space=pl.ANY)
    return pl.pallas_call(
        body, grid=(r // tr,),
        in_specs=[lay, pl.BlockSpec((None, tr, c), lambda j: (0, j, 0)), lay, lay] + [hbm] * 4,
        out_specs=[lay] * 4, out_shape=[jax.ShapeDtypeStruct(w.shape, F32)] * 4,
        input_output_aliases={4 + i: i for i in range(4)},
        name=name, compiler_params=_cp((PAR,)))(w, g, m, v, *outs)


def _adamw(w, g, m, v, *, name):
    shape = w.shape
    if w.ndim == 2:
        w, g, m, v = (a[None] for a in (w, g, m, v))
    l, r, c = w.shape
    tr = _tile(r, max(SUBLANE, ADAM_BLOCK_ELEMS // c // SUBLANE * SUBLANE), SUBLANE)

    def body(w_ref, g_ref, m_ref, v_ref, d_ref, nm_ref, nv_ref):
        d_ref[...], nm_ref[...], nv_ref[...] = _adam_math(w_ref[...], g_ref[...], m_ref[...], v_ref[...])

    spec = pl.BlockSpec((None, tr, c), lambda i, j: (i, j, 0))
    outs = pl.pallas_call(
        body, grid=(l, r // tr), in_specs=[spec] * 4, out_specs=[spec] * 3,
        out_shape=[jax.ShapeDtypeStruct((l, r, c), F32)] * 3,
        name=name, compiler_params=_cp((PAR, PAR)))(w, g, m, v)
    return tuple(o.reshape(shape) for o in outs)


PACK_ROWS = 2 * SUBLANE * LANE


def _pack(arrays):
    flat = jnp.concatenate([a.reshape(-1).astype(F32) for a in arrays])
    pad = (-flat.shape[0]) % PACK_ROWS
    return jnp.pad(flat, (0, pad)).reshape(-1, LANE)


def _unpack(packed, shapes):
    flat = packed.reshape(-1)
    out, off = [], 0
    for s in shapes:
        size = int(np.prod(s))
        out.append(flat[off:off + size].reshape(s))
        off += size
    return out


BIG = (("w_mem_kv", "row"), ("w_mix_out", "row"), ("w_ffn_up", "slot"), ("w_ffn_down", "row"),
       ("w_in_a", "slot"), ("w_in_b", "row"), ("w_kv", "row"))
COLUMN_SHARDED_AS_COLUMNS = ("w_in_a",)
SMALL_SHARDED = (("w_ffn_conv", 2), ("w_conv_a", 2), ("b_conv_a", 1), ("lru_lambda", 1))
SMALL_REPLICATED = ("g_mix_pre", "g_mix_post", "g_ffn_pre", "g_ffn_post", "g_mem", "b_ffn_conv",
                    "w_rg_r", "b_rg_r", "w_rg_i", "b_rg_i", "sinks_b", "g_kv")
WEIGHTS = ("g_mix_pre", "g_mix_post", "g_ffn_pre", "g_ffn_post", "g_mem", "w_mem_kv", "w_mix_out", "w_ffn_up",
           "w_ffn_conv", "b_ffn_conv", "w_ffn_down", "w_in_a", "w_conv_a", "b_conv_a", "w_rg_r", "b_rg_r", "w_rg_i",
           "b_rg_i", "lru_lambda", "w_in_b", "sinks_b", "g_kv", "w_kv")


def _slot_to_cols(a):
    s, l, r, c = a.shape
    return a.transpose(1, 2, 0, 3).reshape(l, r, s * c)


def _cols_to_slot(a):
    l, r, c4 = a.shape
    return a.reshape(l, r, N_CHIP, c4 // N_CHIP).transpose(2, 0, 1, 3)


GROUPS_PER_LAYER = 8


def _layer_weights(layer):
    names = [("w_mem_kv", layer), ("w_in_a", layer) if layer < N_A else ("w_in_b", layer - N_A)]
    if layer == N_A:
        names.append(("w_kv", 0))
    return names + [("w_mix_out", layer), ("w_ffn_up", layer), ("w_ffn_down", layer)]


def _train_step(x, mem, target, w, m, v):
    xi, yi, ci = _mesh_pos()
    chip = 2 * xi + yi
    pos = jnp.stack([ci, chip]).astype(jnp.int32)

    q = _CommQueue()
    kinds = dict(BIG)
    as3 = lambda a: a if a.ndim == 3 else a[None]
    w3, m3, v3 = ({k: as3(d[k]) for k, _ in BIG} for d in (w, m, v))
    shards = {k: w3[k].astype(MXU) for k, _ in BIG}

    gathered = {}

    def on_gathered(k, l, full):
        gathered[k, l] = _slot_to_cols(full) if k in COLUMN_SHARDED_AS_COLUMNS else full

    group_of = {}

    for layer in range(DEPTH):
        for i, (k, l) in enumerate(_layer_weights(layer)):
            group_of[k, l] = layer * GROUPS_PER_LAYER + i
            _gather_chunks(q, group_of[k, l], kinds[k], shards[k], l, functools.partial(on_gathered, k, l))

    def wfull(k, l):
        if (k, l) not in gathered:
            q.flush(group_of[k, l])
        return gathered[k, l]

    small = {}
    _allgather_chunk(q, 0, _pack([w[k] for k, _ in SMALL_SHARDED]), functools.partial(small.__setitem__, "stacked"))
    q.flush(1)

    big_out = {k: [lax.empty(w3[k].shape, F32) for _ in range(4)] for k, _ in BIG}

    def on_reduced(k, l, g):
        big_out[k] = _adamw_layer(w3[k], g, m3[k], v3[k], big_out[k], l, name=f"adamw_{k}{l}")

    def push_grad(k, l, g):
        if k in COLUMN_SHARDED_AS_COLUMNS:
            g = _cols_to_slot(g)
        _reduce_scatter_chunks(q, kinds[k], g, (1,) + w3[k].shape[1:], pos, f"{k}{l}", functools.partial(on_reduced, k, l))

    small_shapes = [w[k].shape for k, _ in SMALL_SHARDED]
    per_chip = [_unpack(small["stacked"][2 * s], small_shapes) for s in range(N_CHIP)]
    p = {k: w[k] for k in SMALL_REPLICATED}
    for i, (k, axis) in enumerate(SMALL_SHARDED):
        p[k] = jnp.concatenate([per_chip[s][i] for s in range(N_CHIP)], axis=axis)

    sq, grad_x, g = _local_step(x, mem, target, p, wfull, push_grad, q)
    loss = lax.psum(0.5 * sq / D_MODEL, ("x", "y", "c"))
    q.flush()

    small_names = [k for k, _ in SMALL_SHARDED] + list(SMALL_REPLICATED)
    summed = _allreduce8(_pack([g[k] for k in small_names]), name="allreduce_small")
    gsum = dict(zip(small_names, _unpack(summed, [p[k].shape for k in small_names])))
    for k, axis in SMALL_SHARDED:
        gsum[k] = lax.dynamic_slice_in_dim(gsum[k], chip * w[k].shape[axis], w[k].shape[axis], axis)

    delta, new_m, new_v = {}, {}, {}
    for k, _ in BIG:
        gsum[k], delta[k], new_m[k], new_v[k] = (o.reshape(w[k].shape) for o in big_out[k])
    for k in small_names:
        as2 = lambda a: a.reshape(-1, a.shape[-1])
        outs = _adamw(as2(w[k]), as2(gsum[k]), as2(m[k]), as2(v[k]), name=f"adamw_{k}")
        delta[k], new_m[k], new_v[k] = (o.reshape(w[k].shape) for o in outs)
    return (loss, grad_x, *[gsum[k] for k in WEIGHTS], *[delta[k] for k in WEIGHTS],
            *[new_m[k] for k in WEIGHTS], *[new_v[k] for k in WEIGHTS])


def kernel(x, mem, g_mix_pre, g_mix_post, g_ffn_pre, g_ffn_post, g_mem, w_mem_kv, w_mix_out, w_ffn_up, w_ffn_conv, b_ffn_conv, w_ffn_down, w_in_a, w_conv_a, b_conv_a, w_rg_r, b_rg_r, w_rg_i, b_rg_i, lru_lambda, w_in_b, sinks_b, g_kv, w_kv, loss_target, m_g_mix_pre, m_g_mix_post, m_g_ffn_pre, m_g_ffn_post, m_g_mem, m_w_mem_kv, m_w_mix_out, m_w_ffn_up, m_w_ffn_conv, m_b_ffn_conv, m_w_ffn_down, m_w_in_a, m_w_conv_a, m_b_conv_a, m_w_rg_r, m_b_rg_r, m_w_rg_i, m_b_rg_i, m_lru_lambda, m_w_in_b, m_sinks_b, m_g_kv, m_w_kv, v_g_mix_pre, v_g_mix_post, v_g_ffn_pre, v_g_ffn_post, v_g_mem, v_w_mem_kv, v_w_mix_out, v_w_ffn_up, v_w_ffn_conv, v_b_ffn_conv, v_w_ffn_down, v_w_in_a, v_w_conv_a, v_b_conv_a, v_w_rg_r, v_b_rg_r, v_w_rg_i, v_b_rg_i, v_lru_lambda, v_w_in_b, v_sinks_b, v_g_kv, v_w_kv):
    args = (g_mix_pre, g_mix_post, g_ffn_pre, g_ffn_post, g_mem, w_mem_kv, w_mix_out, w_ffn_up, w_ffn_conv, b_ffn_conv, w_ffn_down, w_in_a, w_conv_a, b_conv_a, w_rg_r, b_rg_r, w_rg_i, b_rg_i, lru_lambda, w_in_b, sinks_b, g_kv, w_kv)
    ms = (m_g_mix_pre, m_g_mix_post, m_g_ffn_pre, m_g_ffn_post, m_g_mem, m_w_mem_kv, m_w_mix_out, m_w_ffn_up, m_w_ffn_conv, m_b_ffn_conv, m_w_ffn_down, m_w_in_a, m_w_conv_a, m_b_conv_a, m_w_rg_r, m_b_rg_r, m_w_rg_i, m_b_rg_i, m_lru_lambda, m_w_in_b, m_sinks_b, m_g_kv, m_w_kv)
    vs = (v_g_mix_pre, v_g_mix_post, v_g_ffn_pre, v_g_ffn_post, v_g_mem, v_w_mem_kv, v_w_mix_out, v_w_ffn_up, v_w_ffn_conv, v_b_ffn_conv, v_w_ffn_down, v_w_in_a, v_w_conv_a, v_b_conv_a, v_w_rg_r, v_b_rg_r, v_w_rg_i, v_b_rg_i, v_lru_lambda, v_w_in_b, v_sinks_b, v_g_kv, v_w_kv)
    return _train_step(x, mem, loss_target, dict(zip(WEIGHTS, args)), dict(zip(WEIGHTS, ms)), dict(zip(WEIGHTS, vs)))
```

```python
import functools
import math

import numpy as np
import jax
import jax.numpy as jnp
from jax import lax
from jax.experimental import pallas as pl
from jax.experimental.pallas import tpu as pltpu

F32 = jnp.float32
MXU = jnp.bfloat16

D_MODEL = 1024
HEAD_DIM = 64
MEM_LEN = 256
MEM_HEADS = 4
MEM_WIDTH = MEM_HEADS * HEAD_DIM
MIX_WIDTH = D_MODEL - MEM_WIDTH
LRU_BLOCKS = MIX_WIDTH // HEAD_DIM
LRU_CONV = 4
LRU_C = 8.0
SWA_HEADS = MIX_WIDTH // HEAD_DIM
SWA_KV_HEADS = 4
SWA_GROUP = SWA_HEADS // SWA_KV_HEADS
WINDOW = 128
D_FF = 2816
FFN_CONV = 3
EPS = 1e-6
DEPTH = 4
N_A = 2

ADAM_LR = 0.001
ADAM_B1 = 0.9
ADAM_B2 = 0.999
ADAM_EPS = 1e-08
ADAM_WD = 0.01
ADAM_STEP = 10

VMEM_LIMIT_V7X = 56 * 1024 * 1024
LANE = 128
SUBLANE = 8
GATE_TILE = 256
MESH_T = pl.DeviceIdType.MESH


def _alibi_slopes(n):
    def pow2_slopes(m):
        start = 2.0 ** (-8.0 / m)
        return [start ** (i + 1) for i in range(m)]
    c = 2 ** int(math.floor(math.log2(n)))
    s = pow2_slopes(c)
    if c != n:
        s = s + pow2_slopes(2 * c)[0::2][: n - c]
    return [float(np.float32(v)) for v in s]


SLOPES = _alibi_slopes(SWA_HEADS)


def _tile(n, cap, mult=LANE):
    best = None
    for t in range(mult, min(n, cap) + 1, mult):
        if n % t == 0:
            best = t
    return best if best is not None else n


def _cp(sem):
    return pltpu.CompilerParams(dimension_semantics=sem, vmem_limit_bytes=VMEM_LIMIT_V7X)


MM_VMEM_BUDGET = 40 * 1024 * 1024
HBM_BYTES_PER_US_V7X = 3.0e6
GRID_STEP_US = 0.35


def _divisors(n, mult):
    return [t for t in range(mult, n + 1, mult) if n % t == 0] or [n]


def _mm_tiles(m, k, n, out_bytes):
    best = None
    for tm in _divisors(m, 256):
        for tn in _divisors(n, LANE):
            vmem = 2 * (tm * k * 2 + k * tn * 2 + tm * tn * out_bytes)
            if vmem > MM_VMEM_BUDGET:
                continue
            steps = (m // tm) * (n // tn)
            b_reads = 1 if tn == n else m // tm
            traffic = m * k * 2 + k * n * 2 * b_reads + m * n * out_bytes
            first = tm * k * 2 + k * tn * 2
            cost = (traffic + first) / HBM_BYTES_PER_US_V7X + steps * GRID_STEP_US
            if best is None or cost < best[0]:
                best = (cost, tm, tn)
    return best[1], best[2]


def _mm_tn_tiles(k, m, n, whole_n=False):
    best = None
    for tm in _divisors(m, LANE):
        for tn in ([n] if whole_n else _divisors(n, LANE)):
            for tk in _divisors(k, 512):
                vmem = 2 * (tk * tm * 2 + tk * tn * 2 + tm * tn * 4)
                if vmem > MM_VMEM_BUDGET:
                    continue
                steps = (m // tm) * (n // tn) * (k // tk)
                traffic = k * m * 2 * (n // tn) + k * n * 2 * (m // tm) + m * n * 4
                cost = traffic / HBM_BYTES_PER_US_V7X + steps * GRID_STEP_US
                if best is None or cost < best[0]:
                    best = (cost, tk, tm, tn)
    return best[1], best[2], best[3]


ARB = "arbitrary"
PAR = "parallel"


def _rms_fwd(x, g):
    r = lax.rsqrt(jnp.mean(x * x, axis=-1, keepdims=True) + EPS)
    return x * r * g


def _rms_bwd(dy, x, g):
    r = lax.rsqrt(jnp.mean(x * x, axis=-1, keepdims=True) + EPS)
    xh = x * r
    gdy = dy * g
    dx = r * (gdy - xh * jnp.mean(gdy * xh, axis=-1, keepdims=True))
    dg = jnp.sum(dy * xh, axis=0, keepdims=True)
    return dx, dg


_GELU_K = math.sqrt(2.0 / math.pi)
_GELU_C = 0.044715


def _gelu(x):
    t = jnp.tanh(_GELU_K * (x + _GELU_C * x * x * x))
    return 0.5 * x * (1.0 + t)


def _gelu_and_grad(x):
    x2 = x * x
    u = 0.5 * jnp.tanh(x * (_GELU_K + (_GELU_K * _GELU_C) * x2)) + 0.5
    dz2 = (6.0 * _GELU_K * _GELU_C) * x2 + 2.0 * _GELU_K
    return x * u, u * ((x * (1.0 - u)) * dz2 + 1.0)


def _shift_up(x, k, row):
    n = x.shape[0]
    return jnp.where(row < n - k, pltpu.roll(x, n - k, axis=0), 0.0)


def _shift_down_edge(x, k):
    r = pltpu.roll(x, k, axis=0)
    row = lax.broadcasted_iota(jnp.int32, (SUBLANE, x.shape[1]), 0)
    return jnp.concatenate([jnp.where(row >= k, r[:SUBLANE], 0.0), r[SUBLANE:]], axis=0)


def _shift_up_edge(x, k):
    n = x.shape[0]
    r = pltpu.roll(x, n - k, axis=0)
    row = lax.broadcasted_iota(jnp.int32, (SUBLANE, x.shape[1]), 0)
    return jnp.concatenate([r[:n - SUBLANE], jnp.where(row < SUBLANE - k, r[n - SUBLANE:], 0.0)], axis=0)


def _dot(a, b):
    return jnp.dot(a, b, preferred_element_type=F32)


def _dot_nt(a, b):
    return lax.dot_general(a, b, (((1,), (1,)), ((), ())), preferred_element_type=F32)


def _dot_tn(a, b):
    return lax.dot_general(a, b, (((0,), (0,)), ((), ())), preferred_element_type=F32)


MXU_FLOPS_PER_US = 7.0e8
HOST_US = {"lru_fwd": 44.0, "lru_bwd": 94.0, "swa_fwd": 55.0, "swa_bwd": 90.0, "mem_attn_fwd": 19.0,
           "mem_attn_bwd": 27.0, "ffn_act": 75.0, "ffn_act_bwd": 75.0, "resid": 22.0, "resid_bwd": 33.0}
HOST_FILL = 1.0


def _hosted_call(body, *, grid, in_specs, out_specs, out_shape, args, name, aliases=None, scratch_shapes=(),
                 q=None, flops=0.0, budget_us=0.0):
    chunks = q.take(HOST_FILL * (flops / MXU_FLOPS_PER_US + budget_us)) if q is not None else []
    if not chunks:
        return pl.pallas_call(
            body, grid=grid, in_specs=in_specs, out_specs=out_specs, out_shape=out_shape,
            scratch_shapes=list(scratch_shapes), input_output_aliases=aliases or {}, name=name,
            compiler_params=_cp((ARB,) * len(grid)))(*args)
    single = not isinstance(out_shape, (list, tuple))
    o_shapes = [out_shape] if single else list(out_shape)
    o_specs = [out_specs] if single else list(out_specs)
    n_in, n_out, n_scr = len(args), len(o_shapes), len(scratch_shapes)
    c_ins = [a for ch in chunks for a in ch.ins]
    c_outs = [s for ch in chunks for s in ch.out_shapes]
    alias = dict(aliases or {})
    in_off, out_off, sem_off = [], [], []
    i0 = o0 = s0 = 0
    for ch in chunks:
        in_off.append(i0)
        out_off.append(o0)
        sem_off.append(s0)
        for ci, co in ch.alias.items():
            alias[n_in + i0 + ci] = n_out + o0 + co
        i0 += len(ch.ins)
        o0 += len(ch.out_shapes)
        s0 += ch.n_sem

    def wrapped(*refs):
        ins = refs[:n_in]
        cin = refs[n_in:n_in + i0]
        outs = refs[n_in + i0:n_in + i0 + n_out]
        cout = refs[n_in + i0 + n_out:n_in + i0 + n_out + o0]
        scr = refs[n_in + i0 + n_out + o0:n_in + i0 + n_out + o0 + n_scr]
        send_sems, recv_sems = refs[n_in + i0 + n_out + o0 + n_scr:]
        first = functools.reduce(lambda u, v: u & v, [pl.program_id(d) == 0 for d in range(len(grid))])
        last = functools.reduce(lambda u, v: u & v, [pl.program_id(d) == grid[d] - 1 for d in range(len(grid))])

        def each(phase):
            for ch, a, b, s in zip(chunks, in_off, out_off, sem_off):
                getattr(ch, phase)(cin[a:a + len(ch.ins)], cout[b:b + len(ch.out_shapes)], send_sems, recv_sems, s)

        pl.when(first)(lambda: each("start"))
        body(*ins, *outs, *scr)
        pl.when(last)(lambda: each("finish"))

    hbm = pl.BlockSpec(memory_space=pl.ANY)
    res = pl.pallas_call(
        wrapped, grid=grid, in_specs=list(in_specs) + [hbm] * i0, out_specs=o_specs + [hbm] * o0,
        out_shape=o_shapes + c_outs,
        scratch_shapes=list(scratch_shapes) + [pltpu.SemaphoreType.DMA((s0,)), pltpu.SemaphoreType.DMA((s0,))],
        input_output_aliases=alias, name=name,
        compiler_params=pltpu.CompilerParams(dimension_semantics=(ARB,) * len(grid), vmem_limit_bytes=VMEM_LIMIT_V7X,
                                             has_side_effects=True))(*args, *c_ins)
    for ch, b in zip(chunks, out_off):
        ch.done(list(res[n_out + b:n_out + b + len(ch.out_shapes)]))
    return res[0] if single else list(res[:n_out])


def _mm_nn(a, b, *, name, q=None, out_dtype=F32):
    m, k = a.shape
    n = b.shape[-1]
    tm, tn = _mm_tiles(m, k, n, jnp.dtype(out_dtype).itemsize)

    def body(a_ref, b_ref, o_ref):
        o_ref[...] = _dot(a_ref[...], b_ref[...]).astype(o_ref.dtype)

    return _hosted_call(
        body, grid=(m // tm, n // tn),
        in_specs=[pl.BlockSpec((tm, k), lambda i, j: (i, 0)),
                  pl.BlockSpec((None, k, tn), lambda i, j: (0, 0, j))],
        out_specs=pl.BlockSpec((tm, tn), lambda i, j: (i, j)),
        out_shape=jax.ShapeDtypeStruct((m, n), out_dtype),
        args=(a, b), name=name, q=q, flops=2.0 * m * k * n)


def _mm_nt(a, b, *, name, q=None, out_dtype=F32):
    m, k = a.shape
    n = b.shape[-2]
    tm, tn = _mm_tiles(m, k, n, jnp.dtype(out_dtype).itemsize)

    def body(a_ref, b_ref, o_ref):
        o_ref[...] = _dot_nt(a_ref[...], b_ref[...]).astype(o_ref.dtype)

    return _hosted_call(
        body, grid=(m // tm, n // tn),
        in_specs=[pl.BlockSpec((tm, k), lambda i, j: (i, 0)),
                  pl.BlockSpec((None, tn, k), lambda i, j: (0, j, 0))],
        out_specs=pl.BlockSpec((tm, tn), lambda i, j: (i, j)),
        out_shape=jax.ShapeDtypeStruct((m, n), out_dtype),
        args=(a, b), name=name, q=q, flops=2.0 * m * k * n)


def _mm_nn_slots(a, b4, *, name, q=None, out_dtype=F32):
    m, k = a.shape
    s_, _, _, c = b4.shape
    ob = jnp.dtype(out_dtype).itemsize
    tm = max(t for t in _divisors(m, 256) if 2 * (t * k * 2 + k * c * 2 + t * c * ob) <= MM_VMEM_BUDGET)

    def body(a_ref, b_ref, o_ref):
        o_ref[...] = _dot(a_ref[...], b_ref[...]).astype(o_ref.dtype)

    return _hosted_call(
        body, grid=(m // tm, s_),
        in_specs=[pl.BlockSpec((tm, k), lambda i, j: (i, 0)),
                  pl.BlockSpec((None, None, k, c), lambda i, j: (j, 0, 0, 0))],
        out_specs=pl.BlockSpec((tm, c), lambda i, j: (i, j)),
        out_shape=jax.ShapeDtypeStruct((m, s_ * c), out_dtype),
        args=(a, b4), name=name, q=q, flops=2.0 * m * k * s_ * c)


def _mm_tn_slots(a, b, *, name, slot_cols, n_slots, first_slot=0, q=None, out=None):
    k, m = a.shape
    c = slot_cols
    tk, tm, _ = _mm_tn_tiles(k, m, c, whole_n=True)

    def body(a_ref, b_ref, *rest):
        o_ref = rest[-1]
        part = _dot_tn(a_ref[...], b_ref[...])

        @pl.when(pl.program_id(2) == 0)
        def _():
            o_ref[...] = part

        @pl.when(pl.program_id(2) > 0)
        def _():
            o_ref[...] += part

    in_specs = [pl.BlockSpec((tk, tm), lambda i, j, s: (s, i)), pl.BlockSpec((tk, c), lambda i, j, s: (s, j))]
    args = (a, b)
    if out is not None:
        in_specs.append(pl.BlockSpec(memory_space=pl.ANY))
        args = (a, b, out)
    return _hosted_call(
        body, grid=(m // tm, b.shape[-1] // c, k // tk), in_specs=in_specs,
        out_specs=pl.BlockSpec((None, None, tm, c), lambda i, j, s: (first_slot + j, 0, i, 0)),
        out_shape=jax.ShapeDtypeStruct((n_slots, 1, m, c), F32),
        aliases={2: 0} if out is not None else None,
        args=args, name=name, q=q, flops=2.0 * m * k * b.shape[-1])


def _mm_tn(a, b, *, name, q=None):
    k, m = a.shape
    n = b.shape[-1]
    tk, tm, tn = _mm_tn_tiles(k, m, n)

    def body(a_ref, b_ref, o_ref):
        part = _dot_tn(a_ref[...], b_ref[...])

        @pl.when(pl.program_id(2) == 0)
        def _():
            o_ref[...] = part

        @pl.when(pl.program_id(2) > 0)
        def _():
            o_ref[...] += part

    return _hosted_call(
        body, grid=(m // tm, n // tn, k // tk),
        in_specs=[pl.BlockSpec((tk, tm), lambda i, j, s: (s, i)), pl.BlockSpec((tk, tn), lambda i, j, s: (s, j))],
        out_specs=pl.BlockSpec((None, tm, tn), lambda i, j, s: (0, i, j)),
        out_shape=jax.ShapeDtypeStruct((1, m, n), F32),
        args=(a, b), name=name, q=q, flops=2.0 * m * k * n)


def _mm_ffn_dh(dg, dv, w4, *, name, q=None):
    m, f = dg.shape
    n_slots, _, d, c = w4.shape
    tm, tn = _mm_tiles(m, 2 * f, d, 4)

    def body(dg_ref, dv_ref, *rest):
        w_refs, o_ref = rest[:n_slots], rest[n_slots]
        acc = None
        for s, w_ref in enumerate(w_refs):
            x_ref = dg_ref if s < n_slots // 2 else dv_ref
            off = (s % (n_slots // 2)) * c
            part = _dot_nt(x_ref[:, off:off + c], w_ref[...])
            acc = part if acc is None else acc + part
        o_ref[...] = acc.astype(o_ref.dtype)

    wspec = lambda s: pl.BlockSpec((None, None, tn, c), lambda i, j: (s, 0, j, 0))
    return _hosted_call(
        body, grid=(m // tm, d // tn),
        in_specs=[pl.BlockSpec((tm, f), lambda i, j: (i, 0)),
                  pl.BlockSpec((tm, f), lambda i, j: (i, 0))] + [wspec(s) for s in range(n_slots)],
        out_specs=pl.BlockSpec((tm, tn), lambda i, j: (i, j)),
        out_shape=jax.ShapeDtypeStruct((m, d), MXU),
        args=(dg, dv) + (w4,) * n_slots, name=name, q=q, flops=4.0 * m * f * d)


ROW_TILE = 512


def _norm_fwd(x, g, *, name):
    n, d = x.shape
    tm = _tile(n, ROW_TILE, SUBLANE)

    def body(x_ref, g_ref, o_ref):
        o_ref[...] = _rms_fwd(x_ref[...], g_ref[...]).astype(o_ref.dtype)

    return pl.pallas_call(
        body, grid=(n // tm,),
        in_specs=[pl.BlockSpec((tm, d), lambda i: (i, 0)), pl.BlockSpec((1, d), lambda i: (0, 0))],
        out_specs=pl.BlockSpec((tm, d), lambda i: (i, 0)),
        out_shape=jax.ShapeDtypeStruct((n, d), MXU),
        name=name, compiler_params=_cp((PAR,)))(x, g)


def _norm_bwd_dg(dy, x, g, *, name):
    n, d = x.shape
    tm = _tile(n, ROW_TILE, SUBLANE)

    def body(dy_ref, x_ref, g_ref, dg_ref):
        @pl.when(pl.program_id(0) == 0)
        def _():
            dg_ref[...] = jnp.zeros_like(dg_ref)
        _, dg = _rms_bwd(dy_ref[...], x_ref[...], g_ref[...])
        dg_ref[...] += dg

    return pl.pallas_call(
        body, grid=(n // tm,),
        in_specs=[pl.BlockSpec((tm, d), lambda i: (i, 0)), pl.BlockSpec((tm, d), lambda i: (i, 0)),
                  pl.BlockSpec((1, d), lambda i: (0, 0))],
        out_specs=pl.BlockSpec((1, d), lambda i: (0, 0)),
        out_shape=jax.ShapeDtypeStruct((1, d), F32),
        name=name, compiler_params=_cp((ARB,)))(dy, x, g)


def _resid_norm_fwd(x, y, g_post, g_pres, *, name, q=None):
    n, d = x.shape
    tm = _tile(n, ROW_TILE, SUBLANE)
    nh = len(g_pres)

    def body(x_ref, y_ref, gp_ref, *rest):
        gpre = rest[:nh]
        xo_ref = rest[nh]
        h_refs = rest[nh + 1:]
        xo = x_ref[...] + _rms_fwd(y_ref[...].astype(F32), gp_ref[...])
        xo_ref[...] = xo
        for g_ref, h_ref in zip(gpre, h_refs):
            h_ref[...] = _rms_fwd(xo, g_ref[...]).astype(h_ref.dtype)

    row = pl.BlockSpec((tm, d), lambda i: (i, 0))
    vec = pl.BlockSpec((1, d), lambda i: (0, 0))
    outs = _hosted_call(
        body, grid=(n // tm,),
        in_specs=[row, row, vec] + [vec] * nh,
        out_specs=[row] + [row] * nh,
        out_shape=[jax.ShapeDtypeStruct((n, d), F32)] + [jax.ShapeDtypeStruct((n, d), MXU)] * nh,
        args=(x, y, g_post, *g_pres), name=name, q=q, budget_us=HOST_US["resid"])
    return outs[0], list(outs[1:])


def _loss_fwd(x, y, g_post, target, *, name):
    n, d = x.shape
    tm = _tile(n, ROW_TILE, SUBLANE)

    def body(x_ref, y_ref, gp_ref, t_ref, dx_ref, sq_ref, dy_ref, dg_ref):
        @pl.when(pl.program_id(0) == 0)
        def _():
            sq_ref[...] = jnp.zeros_like(sq_ref)
            dg_ref[...] = jnp.zeros_like(dg_ref)
        y = y_ref[...].astype(F32)
        err = x_ref[...] + _rms_fwd(y, gp_ref[...]) - t_ref[...]
        g = err * (1.0 / d)
        dx_ref[...] = g
        sq_ref[...] += jnp.sum(err * err, axis=0, keepdims=True)
        dy, dg = _rms_bwd(g, y, gp_ref[...])
        dy_ref[...] = dy.astype(dy_ref.dtype)
        dg_ref[...] += dg

    row = pl.BlockSpec((tm, d), lambda i: (i, 0))
    vec = pl.BlockSpec((1, d), lambda i: (0, 0))
    return pl.pallas_call(
        body, grid=(n // tm,),
        in_specs=[row, row, vec, row],
        out_specs=[row, vec, row, vec],
        out_shape=[jax.ShapeDtypeStruct((n, d), F32), jax.ShapeDtypeStruct((1, d), F32),
                   jax.ShapeDtypeStruct((n, d), MXU), jax.ShapeDtypeStruct((1, d), F32)],
        name=name, compiler_params=_cp((ARB,)))(x, y, g_post, target)


def _resid_norm_bwd(dx_out, dhs, x_out, g_pres, y, g_post, *, name, q=None):
    n, d = dx_out.shape
    tm = _tile(n, ROW_TILE, SUBLANE)
    nh = len(dhs)
    has_y = y is not None

    def body(*refs):
        it = iter(refs)
        dxo_ref = next(it)
        dh_refs = [next(it) for _ in range(nh)]
        xo_ref = next(it) if nh else None
        gpre_refs = [next(it) for _ in range(nh)]
        y_ref = next(it) if has_y else None
        gpost_ref = next(it) if has_y else None
        g_out = next(it)
        dy_out = next(it) if has_y else None
        dgpre_out = [next(it) for _ in range(nh)]
        dgpost_out = next(it) if has_y else None

        @pl.when(pl.program_id(0) == 0)
        def _():
            for r in dgpre_out:
                r[...] = jnp.zeros_like(r)
            if has_y:
                dgpost_out[...] = jnp.zeros_like(dgpost_out)

        g = dxo_ref[...]
        if nh:
            xo = xo_ref[...]
            for dh_ref, gp_ref, dg_ref in zip(dh_refs, gpre_refs, dgpre_out):
                dx, dg = _rms_bwd(dh_ref[...].astype(F32), xo, gp_ref[...])
                g = g + dx
                dg_ref[...] += dg
        g_out[...] = g
        if has_y:
            dy, dg = _rms_bwd(g, y_ref[...].astype(F32), gpost_ref[...])
            dy_out[...] = dy.astype(dy_out.dtype)
            dgpost_out[...] += dg

    row = pl.BlockSpec((tm, d), lambda i: (i, 0))
    vec = pl.BlockSpec((1, d), lambda i: (0, 0))
    ins, in_specs = [dx_out], [row]
    ins += list(dhs)
    in_specs += [row] * nh
    if nh:
        ins.append(x_out)
        in_specs.append(row)
    ins += list(g_pres)
    in_specs += [vec] * nh
    if has_y:
        ins += [y, g_post]
        in_specs += [row, vec]
    out_specs, out_shape = [row], [jax.ShapeDtypeStruct((n, d), F32)]
    if has_y:
        out_specs.append(row)
        out_shape.append(jax.ShapeDtypeStruct((n, d), MXU))
    out_specs += [vec] * nh
    out_shape += [jax.ShapeDtypeStruct((1, d), F32)] * nh
    if has_y:
        out_specs.append(vec)
        out_shape.append(jax.ShapeDtypeStruct((1, d), F32))
    outs = list(_hosted_call(
        body, grid=(n // tm,), in_specs=in_specs, out_specs=out_specs, out_shape=out_shape,
        args=tuple(ins), name=name, q=q, budget_us=HOST_US["resid_bwd"]))
    g = outs.pop(0)
    dy = outs.pop(0) if has_y else None
    dgpre = [outs.pop(0) for _ in range(nh)]
    dgpost = outs.pop(0) if has_y else None
    return g, dy, dgpre, dgpost


def _ffn_conv(up, w_ref, b_ref):
    return (w_ref[0:1, :] * _shift_down_edge(up, 2) + w_ref[1:2, :] * _shift_down_edge(up, 1)
            + w_ref[2:3, :] * up + b_ref[...])


def _ffn_act_fwd(up, wconv, bconv, bsz, *, name, q=None):
    n, f2 = up.shape
    f = f2 // 2
    t = n // bsz
    tc = _tile(f, 256)
    nf = f // tc

    def body(ug_ref, uv_ref, wg_ref, wv_ref, bg_ref, bv_ref, o_ref, dag_ref, dav_ref):
        g = _ffn_conv(ug_ref[...].astype(F32), wg_ref, bg_ref)
        v = _ffn_conv(uv_ref[...].astype(F32), wv_ref, bv_ref)
        gl, dgl = _gelu_and_grad(g)
        dag_ref[...] = (v * dgl).astype(dag_ref.dtype)
        dav_ref[...] = gl.astype(dav_ref.dtype)
        o_ref[...] = (gl * v).astype(o_ref.dtype)

    blk = pl.BlockSpec((t, tc), lambda b, j: (b, j))
    return _hosted_call(
        body, grid=(bsz, nf),
        in_specs=[blk, pl.BlockSpec((t, tc), lambda b, j: (b, j + nf)),
                  pl.BlockSpec((FFN_CONV, tc), lambda b, j: (0, j)),
                  pl.BlockSpec((FFN_CONV, tc), lambda b, j: (0, j + nf)),
                  pl.BlockSpec((1, tc), lambda b, j: (0, j)),
                  pl.BlockSpec((1, tc), lambda b, j: (0, j + nf))],
        out_specs=[blk, blk, blk],
        out_shape=[jax.ShapeDtypeStruct((n, f), MXU)] * 3,
        args=(up, up, wconv, wconv, bconv, bconv), name=name, q=q, budget_us=HOST_US["ffn_act"])


def _ffn_act_bwd(up, ug, uv, dact, wconv, bsz, *, name, q=None):
    n, f2 = up.shape
    f = f2 // 2
    t = n // bsz
    tc = _tile(f, 256)
    nf = f // tc

    def body(xg_ref, xv_ref, g_ref, v_ref, da_ref, wg_ref, wv_ref,
             dug_ref, duv_ref, dwg_ref, dwv_ref, dbg_ref, dbv_ref):
        @pl.when(pl.program_id(1) == 0)
        def _():
            for r in (dwg_ref, dwv_ref, dbg_ref, dbv_ref):
                r[...] = jnp.zeros_like(r)

        da = da_ref[...].astype(F32)
        dg = da * g_ref[...].astype(F32)
        dv = da * v_ref[...].astype(F32)

        def conv_bwd(du, w_ref, x_ref, dx_ref, dw_ref, db_ref):
            du1, du2 = _shift_up_edge(du, 1), _shift_up_edge(du, 2)
            dx_ref[...] = (w_ref[2:3, :] * du + w_ref[1:2, :] * du1 + w_ref[0:1, :] * du2).astype(dx_ref.dtype)
            x = x_ref[...].astype(F32)
            dw_ref[0:1, :] += jnp.sum(x * du2, axis=0, keepdims=True)
            dw_ref[1:2, :] += jnp.sum(x * du1, axis=0, keepdims=True)
            dw_ref[2:3, :] += jnp.sum(x * du, axis=0, keepdims=True)
            db_ref[...] += jnp.sum(du, axis=0, keepdims=True)

        conv_bwd(dg, wg_ref, xg_ref, dug_ref, dwg_ref, dbg_ref)
        conv_bwd(dv, wv_ref, xv_ref, duv_ref, dwv_ref, dbv_ref)

    blk = pl.BlockSpec((t, tc), lambda j, b: (b, j))
    wspec = pl.BlockSpec((FFN_CONV, tc), lambda j, b: (0, j))
    bspec = pl.BlockSpec((1, tc), lambda j, b: (0, j))
    outs = _hosted_call(
        body, grid=(nf, bsz),
        in_specs=[blk, pl.BlockSpec((t, tc), lambda j, b: (b, j + nf)), blk, blk, blk,
                  wspec, pl.BlockSpec((FFN_CONV, tc), lambda j, b: (0, j + nf))],
        out_specs=[blk, blk, wspec, wspec, bspec, bspec],
        out_shape=[jax.ShapeDtypeStruct((n, f), MXU), jax.ShapeDtypeStruct((n, f), MXU),
                   jax.ShapeDtypeStruct((FFN_CONV, f), F32), jax.ShapeDtypeStruct((FFN_CONV, f), F32),
                   jax.ShapeDtypeStruct((1, f), F32), jax.ShapeDtypeStruct((1, f), F32)],
        args=(up, up, ug, uv, dact, wconv, wconv), name=name, q=q, budget_us=HOST_US["ffn_act_bwd"])
    dug, duv, dwg, dwv, dbg, dbv = outs
    return dug, duv, jnp.concatenate([dwg, dwv], axis=1), jnp.concatenate([dbg, dbv], axis=1)


def _softmax(s):
    p = jnp.exp(s - jnp.max(s, axis=-1, keepdims=True))
    return p / jnp.sum(p, axis=-1, keepdims=True)


def _mem_attn_fwd(proj, q_col_block, mkv, ycat, bsz, *, name, q=None):
    n = proj.shape[0]
    t = n // bsz
    tq = _tile(t, 512, SUBLANE)
    nt = t // tq
    scale = HEAD_DIM ** -0.5

    def body(q_ref, kv_ref, old_ref, o_ref):
        del old_ref
        heads = range(MEM_HEADS)
        col = lambda ref, h, off=0: ref[:, off + h * HEAD_DIM:off + (h + 1) * HEAD_DIM].astype(MXU)
        ss = [_dot_nt(col(q_ref, h), col(kv_ref, h)) * scale for h in heads]
        ps = [_softmax(s).astype(MXU) for s in ss]
        outs = [_dot(ps[h], col(kv_ref, h, MEM_WIDTH)) for h in heads]
        o_ref[...] = jnp.concatenate(outs, axis=-1).astype(o_ref.dtype)

    return _hosted_call(
        body, grid=(bsz, nt),
        in_specs=[pl.BlockSpec((tq, MEM_WIDTH), lambda b, i: (b * nt + i, q_col_block)),
                  pl.BlockSpec((MEM_LEN, 2 * MEM_WIDTH), lambda b, i: (b, 0)),
                  pl.BlockSpec(memory_space=pl.ANY)],
        out_specs=pl.BlockSpec((tq, MEM_WIDTH), lambda b, i: (b * nt + i, MIX_WIDTH // MEM_WIDTH)),
        out_shape=jax.ShapeDtypeStruct(ycat.shape, ycat.dtype),
        aliases={2: 0}, args=(proj, mkv, ycat), name=name, q=q, budget_us=HOST_US["mem_attn_fwd"])


def _mem_attn_bwd(proj, q_col_block, mkv, dycat, dproj, bsz, *, name, q=None):
    n = proj.shape[0]
    t = n // bsz
    tq = _tile(t, 512, SUBLANE)
    nt = t // tq
    scale = HEAD_DIM ** -0.5

    def body(q_ref, kv_ref, do_ref, old_ref, dq_ref, dkv_ref):
        del old_ref

        @pl.when(pl.program_id(1) == 0)
        def _():
            dkv_ref[...] = jnp.zeros_like(dkv_ref)

        heads = range(MEM_HEADS)
        col = lambda ref, h, off=0: ref[:, off + h * HEAD_DIM:off + (h + 1) * HEAD_DIM].astype(MXU)
        qs = [col(q_ref, h) for h in heads]
        ks = [col(kv_ref, h) for h in heads]
        dos = [col(do_ref, h) for h in heads]
        ps = [_softmax(_dot_nt(qs[h], ks[h]) * scale) for h in heads]
        dps = [_dot_nt(dos[h], col(kv_ref, h, MEM_WIDTH)) for h in heads]
        dss = [(ps[h] * (dps[h] - jnp.sum(dps[h] * ps[h], axis=-1, keepdims=True)) * scale).astype(MXU) for h in heads]
        dvs = [_dot_tn(ps[h].astype(MXU), dos[h]) for h in heads]
        dqs = [_dot(dss[h], ks[h]) for h in heads]
        dks = [_dot_tn(dss[h], qs[h]) for h in heads]
        dq_ref[...] = jnp.concatenate(dqs, axis=-1).astype(dq_ref.dtype)
        dkv_ref[...] += jnp.concatenate(dks + dvs, axis=-1)

    return _hosted_call(
        body, grid=(bsz, nt),
        in_specs=[pl.BlockSpec((tq, MEM_WIDTH), lambda b, i: (b * nt + i, q_col_block)),
                  pl.BlockSpec((MEM_LEN, 2 * MEM_WIDTH), lambda b, i: (b, 0)),
                  pl.BlockSpec((tq, MEM_WIDTH), lambda b, i: (b * nt + i, MIX_WIDTH // MEM_WIDTH)),
                  pl.BlockSpec(memory_space=pl.ANY)],
        out_specs=[pl.BlockSpec((tq, MEM_WIDTH), lambda b, i: (b * nt + i, q_col_block)),
                   pl.BlockSpec((MEM_LEN, 2 * MEM_WIDTH), lambda b, i: (b, 0))],
        out_shape=[jax.ShapeDtypeStruct(dproj.shape, dproj.dtype),
                   jax.ShapeDtypeStruct((bsz * MEM_LEN, 2 * MEM_WIDTH), F32)],
        aliases={3: 0}, args=(proj, mkv, dycat, dproj), name=name, q=q, budget_us=HOST_US["mem_attn_bwd"])


def _swa_probs(s, h, dist, mask, sink):
    s = jnp.where(mask, s * (HEAD_DIM ** -0.5) - SLOPES[h] * dist, -jnp.inf)
    m = jnp.maximum(jnp.max(s, axis=-1, keepdims=True), sink)
    p = jnp.exp(s - m)
    psink = jnp.exp(sink - m)
    inv = 1.0 / (jnp.sum(p, axis=-1, keepdims=True) + psink)
    return p * inv, psink * inv


def _swa_mask(n):
    qi = lax.broadcasted_iota(jnp.int32, (WINDOW, 2 * WINDOW), 0) + WINDOW
    ki = lax.broadcasted_iota(jnp.int32, (WINDOW, 2 * WINDOW), 1)
    dist = qi - ki
    mask = (dist >= 0) & (dist < WINDOW) & ((n > 0) | (ki >= WINDOW))
    return dist.astype(F32), mask


def _swa_fwd(proj, kv, sinks, bsz, *, name, q=None):
    n_tok = proj.shape[0]
    nb = n_tok // bsz // WINDOW
    kvw = SWA_KV_HEADS * HEAD_DIM

    def body(sink_ref, q_ref, kvp_ref, kvc_ref, o_ref):
        n = pl.program_id(1)
        dist, mask = _swa_mask(n)
        kk = jnp.concatenate([kvp_ref[:, :kvw], kvc_ref[:, :kvw]], axis=0).astype(MXU)
        vv = jnp.concatenate([kvp_ref[:, kvw:], kvc_ref[:, kvw:]], axis=0).astype(MXU)
        heads = range(SWA_HEADS)
        group = lambda x, h: x[:, (h // SWA_GROUP) * HEAD_DIM:(h // SWA_GROUP + 1) * HEAD_DIM]
        ss = [_dot_nt(q_ref[:, h * HEAD_DIM:(h + 1) * HEAD_DIM].astype(MXU), group(kk, h)) for h in heads]
        ps = [_swa_probs(ss[h], h, dist, mask, sink_ref[h])[0].astype(MXU) for h in heads]
        outs = [_dot(ps[h], group(vv, h)) for h in heads]
        o_ref[...] = jnp.concatenate(outs, axis=-1).astype(o_ref.dtype)

    return _hosted_call(
        body, grid=(bsz, nb),
        in_specs=[pl.BlockSpec(memory_space=pltpu.SMEM),
                  pl.BlockSpec((WINDOW, MIX_WIDTH), lambda b, n: (b * nb + n, 0)),
                  pl.BlockSpec((WINDOW, 2 * kvw), lambda b, n: (b * nb + jnp.maximum(n - 1, 0), 0)),
                  pl.BlockSpec((WINDOW, 2 * kvw), lambda b, n: (b * nb + n, 0))],
        out_specs=pl.BlockSpec((WINDOW, MIX_WIDTH), lambda b, n: (b * nb + n, 0)),
        out_shape=jax.ShapeDtypeStruct((n_tok, D_MODEL), MXU),
        args=(sinks, proj, kv, kv), name=name, q=q, budget_us=HOST_US["swa_fwd"])


def _swa_bwd(proj, kv, sinks, dycat, bsz, *, name, q=None):
    n_tok = proj.shape[0]
    nb = n_tok // bsz // WINDOW
    kvw = SWA_KV_HEADS * HEAD_DIM

    def body(sink_ref, q_ref, kvp_ref, kvc_ref, do_ref, dq_ref, dkvc_ref, dkvp_ref, dsink_ref):
        n = pl.program_id(1)

        @pl.when((pl.program_id(0) == 0) & (n == 0))
        def _():
            dsink_ref[...] = jnp.zeros_like(dsink_ref)

        dist, mask = _swa_mask(n)
        kk = jnp.concatenate([kvp_ref[:, :kvw], kvc_ref[:, :kvw]], axis=0).astype(MXU)
        vv = jnp.concatenate([kvp_ref[:, kvw:], kvc_ref[:, kvw:]], axis=0).astype(MXU)
        lane = lax.broadcasted_iota(jnp.int32, (SUBLANE, LANE), 1)
        heads = range(SWA_HEADS)
        group = lambda x, h: x[:, (h // SWA_GROUP) * HEAD_DIM:(h // SWA_GROUP + 1) * HEAD_DIM]
        qs = [q_ref[:, h * HEAD_DIM:(h + 1) * HEAD_DIM].astype(MXU) for h in heads]
        dos = [do_ref[:, h * HEAD_DIM:(h + 1) * HEAD_DIM].astype(MXU) for h in heads]
        ss = [_dot_nt(qs[h], group(kk, h)) for h in heads]
        dps = [_dot_nt(dos[h], group(vv, h)) for h in heads]
        probs = [_swa_probs(ss[h], h, dist, mask, sink_ref[h]) for h in heads]
        rss = [jnp.sum(dps[h] * probs[h][0], axis=-1, keepdims=True) for h in heads]
        dss = [(probs[h][0] * (dps[h] - rss[h]) * (HEAD_DIM ** -0.5)).astype(MXU) for h in heads]
        dqs = [_dot(dss[h], group(kk, h)) for h in heads]
        dk_h = [_dot_tn(dss[h], qs[h]) for h in heads]
        dv_h = [_dot_tn(probs[h][0].astype(MXU), dos[h]) for h in heads]
        dsink = jnp.zeros((SUBLANE, LANE), F32)
        for h in heads:
            dsink = dsink + jnp.where(lane == h, jnp.sum(-probs[h][1] * rss[h], axis=0, keepdims=True), 0.0)
        sum_group = lambda xs, c: functools.reduce(lambda a, b: a + b, xs[c * SWA_GROUP:(c + 1) * SWA_GROUP])
        dks = [sum_group(dk_h, c) for c in range(SWA_KV_HEADS)]
        dvs = [sum_group(dv_h, c) for c in range(SWA_KV_HEADS)]
        dq_ref[...] = jnp.concatenate(dqs, axis=-1).astype(dq_ref.dtype)
        dkv = jnp.concatenate(dks + dvs, axis=-1)
        dkvp_ref[...] = dkv[:WINDOW]
        dkvc_ref[...] = dkv[WINDOW:]
        dsink_ref[...] += dsink

    qspec = pl.BlockSpec((WINDOW, MIX_WIDTH), lambda b, n: (b * nb + n, 0))
    kvspec = pl.BlockSpec((WINDOW, 2 * kvw), lambda b, n: (b * nb + n, 0))
    return _hosted_call(
        body, grid=(bsz, nb),
        in_specs=[pl.BlockSpec(memory_space=pltpu.SMEM), qspec,
                  pl.BlockSpec((WINDOW, 2 * kvw), lambda b, n: (b * nb + jnp.maximum(n - 1, 0), 0)),
                  kvspec, qspec],
        out_specs=[qspec, kvspec, kvspec, pl.BlockSpec((SUBLANE, LANE), lambda b, n: (0, 0))],
        out_shape=[jax.ShapeDtypeStruct((n_tok, D_MODEL), MXU),
                   jax.ShapeDtypeStruct((n_tok, 2 * kvw), F32),
                   jax.ShapeDtypeStruct((n_tok, 2 * kvw), F32),
                   jax.ShapeDtypeStruct((SUBLANE, LANE), F32)],
        args=(sinks, proj, kv, kv, dycat), name=name, q=q, budget_us=HOST_US["swa_bwd"])


def _swa_dkv_combine(curs, prevs, bsz, *, name):
    n_tok, w = curs[0].shape
    nb = n_tok // bsz // WINDOW
    k = len(curs)

    def body(*refs):
        o_ref = refs[-1]
        n = pl.program_id(1)
        acc = refs[0][...]
        for r in refs[1:k]:
            acc = acc + r[...]
        nxt = refs[k][...]
        for r in refs[k + 1:2 * k]:
            nxt = nxt + r[...]
        o_ref[...] = (acc + jnp.where(n < nb - 1, nxt, 0.0)).astype(o_ref.dtype)

    cur = pl.BlockSpec((WINDOW, w), lambda b, n: (b * nb + n, 0))
    prv = pl.BlockSpec((WINDOW, w), lambda b, n: (b * nb + jnp.minimum(n + 1, nb - 1), 0))
    return pl.pallas_call(
        body, grid=(bsz, nb), in_specs=[cur] * k + [prv] * k, out_specs=cur,
        out_shape=jax.ShapeDtypeStruct((n_tok, w), MXU),
        name=name, compiler_params=_cp((PAR, PAR)))(*curs, *prevs)


def _lru_gates(ux, halo, ext_ref, wc_ref, bc_ref, wr_ref, br_ref, wi_ref, bi_ref, lam_ref):
    tt = ux.shape[0]
    ext_ref[0:SUBLANE, :] = halo
    ext_ref[SUBLANE:, :] = ux
    xs = [ux] + [ext_ref[pl.ds(SUBLANE - k, tt), :] for k in range(1, LRU_CONV)]
    xc = bc_ref[...] + wc_ref[3:4, :] * xs[0] + wc_ref[2:3, :] * xs[1] + wc_ref[1:2, :] * xs[2] + wc_ref[0:1, :] * xs[3]
    pre_r, pre_i = [], []
    for blk in range(MIX_WIDTH // GATE_TILE):
        xb = xc[:, blk * GATE_TILE:(blk + 1) * GATE_TILE].astype(MXU)
        pre_r.append(_dot(xb, wr_ref[blk]))
        pre_i.append(_dot(xb, wi_ref[blk]))
    r = jax.nn.sigmoid(jnp.concatenate(pre_r, axis=-1) + br_ref[...])
    i = jax.nn.sigmoid(jnp.concatenate(pre_i, axis=-1) + bi_ref[...])
    nlam = -lam_ref[...]
    sp = jnp.maximum(nlam, 0.0) + jnp.log(1.0 + jnp.exp(-jnp.abs(nlam)))
    log_a = -LRU_C * r * sp
    a = jnp.exp(log_a)
    om = -jnp.tanh(log_a) * (a * a + 1.0)
    s = jnp.sqrt(om)
    return xs, xc, r, i, sp, a, s


def _lru_fwd(proj, wconv, bconv, wr, br, wi, bi, lam, bsz, *, name, q=None):
    n_tok = proj.shape[0]
    t = n_tok // bsz
    tt = _tile(t, 256, SUBLANE)
    nt = t // tt
    w = MIX_WIDTH
    ng = tt // SUBLANE

    def body(pg_ref, halo_ref, wc_ref, bc_ref, wr_ref, br_ref, wi_ref, bi_ref, lam_ref,
             y_ref, h_ref, ext_ref, a_ref, b_ref, carry_ref):
        ti = pl.program_id(1)

        @pl.when(ti == 0)
        def _():
            carry_ref[...] = jnp.zeros_like(carry_ref)

        gate = pg_ref[:, :w]
        ux = pg_ref[:, w:]
        halo = jnp.where(ti > 0, halo_ref[...], 0.0)
        _, xc, _, i, _, a, s = _lru_gates(ux, halo, ext_ref, wc_ref, bc_ref, wr_ref, br_ref, wi_ref, bi_ref, lam_ref)
        a_ref[...] = a
        b_ref[...] = s * (i * xc)
        row = lax.broadcasted_iota(jnp.int32, (SUBLANE, w), 0)

        def group(g, hprev):
            off = pl.multiple_of(g * SUBLANE, SUBLANE)
            ca = a_ref[pl.ds(off, SUBLANE), :]
            cb = b_ref[pl.ds(off, SUBLANE), :]
            for d in (1, 2, 4):
                a_sh = jnp.where(row >= d, pltpu.roll(ca, d, axis=0), 1.0)
                b_sh = jnp.where(row >= d, pltpu.roll(cb, d, axis=0), 0.0)
                cb = ca * b_sh + cb
                ca = ca * a_sh
            h = ca * hprev + cb
            b_ref[pl.ds(off, SUBLANE), :] = h
            return jnp.broadcast_to(h[SUBLANE - 1:SUBLANE, :], (SUBLANE, w))

        carry_ref[...] = lax.fori_loop(0, ng, group, carry_ref[...])
        h = b_ref[...]
        h_ref[...] = h
        y_ref[...] = (h * _gelu(gate)).astype(y_ref.dtype)

    vec = lambda r: pl.BlockSpec((r, w), lambda b, i: (0, 0))
    wspec = pl.BlockSpec((w // GATE_TILE, GATE_TILE, GATE_TILE), lambda b, i: (0, 0, 0))
    hb = tt // SUBLANE
    return _hosted_call(
        body, grid=(bsz, nt),
        in_specs=[pl.BlockSpec((tt, 2 * w), lambda b, i: (b * nt + i, 0)),
                  pl.BlockSpec((SUBLANE, w), lambda b, i: (jnp.maximum((b * nt + i) * hb - 1, 0), 1)),
                  vec(LRU_CONV), vec(1), wspec, vec(1), wspec, vec(1), vec(1)],
        out_specs=[pl.BlockSpec((tt, w), lambda b, i: (b * nt + i, 0)),
                   pl.BlockSpec((tt, w), lambda b, i: (b * nt + i, 0))],
        out_shape=[jax.ShapeDtypeStruct((n_tok, D_MODEL), MXU), jax.ShapeDtypeStruct((n_tok, w), F32)],
        scratch_shapes=[pltpu.VMEM((tt + SUBLANE, w), F32), pltpu.VMEM((tt, w), F32),
                        pltpu.VMEM((tt, w), F32), pltpu.VMEM((SUBLANE, w), F32)],
        args=(proj, proj, wconv, bconv, wr, br, wi, bi, lam), name=name, q=q, budget_us=HOST_US["lru_fwd"])


def _lru_bwd(proj, hs, dycat, wconv, bconv, wr, br, wi, bi, lam, bsz, *, name, q=None):
    n_tok = proj.shape[0]
    t = n_tok // bsz
    tt = _tile(t, 256, SUBLANE)
    nt = t // tt
    w = MIX_WIDTH
    ng = tt // SUBLANE
    nblk = w // GATE_TILE

    def body(pg_ref, halo_ref, h_ref, hhalo_ref, dy_ref, wc_ref, bc_ref, wr_ref, br_ref, wi_ref, bi_ref, lam_ref,
             dp_ref, dwc_ref, dbc_ref, dwr_ref, dbr_ref, dwi_ref, dbi_ref, dlam_ref,
             ext_ref, a_ref, c_ref, g_ref, gcarry_ref, xcarry_ref):
        bi_ = pl.program_id(0)
        ti = nt - 1 - pl.program_id(1)

        @pl.when((bi_ == 0) & (pl.program_id(1) == 0))
        def _():
            for r in (dwc_ref, dbc_ref, dwr_ref, dbr_ref, dwi_ref, dbi_ref, dlam_ref):
                r[...] = jnp.zeros_like(r)

        @pl.when(pl.program_id(1) == 0)
        def _():
            gcarry_ref[...] = jnp.zeros_like(gcarry_ref)
            xcarry_ref[...] = jnp.zeros_like(xcarry_ref)

        gate = pg_ref[:, :w]
        ux = pg_ref[:, w:]
        halo = jnp.where(ti > 0, halo_ref[...], 0.0)
        xs, xc, r, i, sp, a, s = _lru_gates(ux, halo, ext_ref, wc_ref, bc_ref, wr_ref, br_ref, wi_ref, bi_ref, lam_ref)
        h = h_ref[...]
        gl, dgl = _gelu_and_grad(gate)
        dy = dy_ref[...].astype(F32)
        dgate = dy * h * dgl
        row_t = lax.broadcasted_iota(jnp.int32, (tt, w), 0)
        g_ref[...] = dy * gl + jnp.where(row_t == tt - 1, gcarry_ref[0:1, :], 0.0)
        c_ref[...] = _shift_up(a, 1, row_t)
        row = lax.broadcasted_iota(jnp.int32, (SUBLANE, w), 0)

        a_ref[...] = a

        def group(k, gnext):
            off = pl.multiple_of((ng - 1 - k) * SUBLANE, SUBLANE)
            cc = c_ref[pl.ds(off, SUBLANE), :]
            cb = g_ref[pl.ds(off, SUBLANE), :]
            cb = cb + jnp.where(row == SUBLANE - 1, gnext, 0.0)
            cc = jnp.where(row == SUBLANE - 1, 0.0, cc)
            for d in (1, 2, 4):
                c_sh = jnp.where(row < SUBLANE - d, pltpu.roll(cc, SUBLANE - d, axis=0), 1.0)
                b_sh = jnp.where(row < SUBLANE - d, pltpu.roll(cb, SUBLANE - d, axis=0), 0.0)
                cb = cc * b_sh + cb
                cc = cc * c_sh
            g_ref[pl.ds(off, SUBLANE), :] = cb
            a0 = a_ref[pl.ds(off, SUBLANE), :]
            return jnp.broadcast_to(a0[0:1, :] * cb[0:1, :], (SUBLANE, w))

        gc = lax.fori_loop(0, ng, group, jnp.zeros((SUBLANE, w), F32))
        gcarry_ref[...] = gc
        gsc = g_ref[...]

        hhalo = jnp.where(ti > 0, hhalo_ref[SUBLANE - 1:SUBLANE, :], 0.0)
        hprev = jnp.where(row_t == 0, hhalo, pltpu.roll(h, 1, axis=0))
        gated = i * xc
        d_gated = gsc * s
        d_atot = gsc * hprev - (gsc * gated) * a / s
        d_loga = d_atot * a
        d_r = d_loga * (-LRU_C) * sp
        dlam_ref[...] += jnp.sum(d_loga * r, axis=0, keepdims=True) * (LRU_C * jax.nn.sigmoid(-lam_ref[...]))
        d_i = d_gated * xc
        d_xc = d_gated * i
        d_pr = d_r * r * (1.0 - r)
        d_pi = d_i * i * (1.0 - i)
        dbr_ref[...] += jnp.sum(d_pr, axis=0, keepdims=True)
        dbi_ref[...] += jnp.sum(d_pi, axis=0, keepdims=True)
        extra = []
        for blk in range(nblk):
            sl = slice(blk * GATE_TILE, (blk + 1) * GATE_TILE)
            xb = xc[:, sl].astype(MXU)
            dr_b = d_pr[:, sl].astype(MXU)
            di_b = d_pi[:, sl].astype(MXU)
            dwr_ref[blk] += _dot_tn(xb, dr_b)
            dwi_ref[blk] += _dot_tn(xb, di_b)
            extra.append(_dot_nt(dr_b, wr_ref[blk]) + _dot_nt(di_b, wi_ref[blk]))
        d_xc = d_xc + jnp.concatenate(extra, axis=-1)
        dbc_ref[...] += jnp.sum(d_xc, axis=0, keepdims=True)
        for k in range(LRU_CONV):
            dwc_ref[k:k + 1, :] += jnp.sum(d_xc * xs[LRU_CONV - 1 - k], axis=0, keepdims=True)
        ext_ref[0:tt, :] = d_xc
        ext_ref[tt:, :] = xcarry_ref[...]
        dux = wc_ref[3:4, :] * d_xc
        for k in range(LRU_CONV - 1):
            dux = dux + wc_ref[k:k + 1, :] * ext_ref[pl.ds(LRU_CONV - 1 - k, tt), :]
        xcarry_ref[...] = d_xc[0:SUBLANE, :]
        dp_ref[:, :w] = dgate.astype(dp_ref.dtype)
        dp_ref[:, w:] = dux.astype(dp_ref.dtype)

    vec = lambda r: pl.BlockSpec((r, w), lambda b, i: (0, 0))
    wspec = pl.BlockSpec((nblk, GATE_TILE, GATE_TILE), lambda b, i: (0, 0, 0))
    hb = tt // SUBLANE
    rblk = lambda b, i: b * nt + (nt - 1 - i)
    halo_idx = lambda b, i: jnp.maximum(rblk(b, i) * hb - 1, 0)
    wide = pl.BlockSpec((tt, 2 * w), lambda b, i: (rblk(b, i), 0))
    narrow = pl.BlockSpec((tt, w), lambda b, i: (rblk(b, i), 0))
    return _hosted_call(
        body, grid=(bsz, nt),
        in_specs=[wide, pl.BlockSpec((SUBLANE, w), lambda b, i: (halo_idx(b, i), 1)),
                  narrow, pl.BlockSpec((SUBLANE, w), lambda b, i: (halo_idx(b, i), 0)), narrow,
                  vec(LRU_CONV), vec(1), wspec, vec(1), wspec, vec(1), vec(1)],
        out_specs=[wide, vec(LRU_CONV), vec(1), wspec, vec(1), wspec, vec(1), vec(1)],
        out_shape=[jax.ShapeDtypeStruct((n_tok, 2 * w + MEM_WIDTH), MXU),
                   jax.ShapeDtypeStruct((LRU_CONV, w), F32), jax.ShapeDtypeStruct((1, w), F32),
                   jax.ShapeDtypeStruct((nblk, GATE_TILE, GATE_TILE), F32), jax.ShapeDtypeStruct((1, w), F32),
                   jax.ShapeDtypeStruct((nblk, GATE_TILE, GATE_TILE), F32), jax.ShapeDtypeStruct((1, w), F32),
                   jax.ShapeDtypeStruct((1, w), F32)],
        scratch_shapes=[pltpu.VMEM((tt + SUBLANE, w), F32), pltpu.VMEM((tt, w), F32), pltpu.VMEM((tt, w), F32),
                        pltpu.VMEM((tt, w), F32), pltpu.VMEM((SUBLANE, w), F32), pltpu.VMEM((SUBLANE, w), F32)],
        args=(proj, proj, hs, hs, dycat, wconv, bconv, wr, br, wi, bi, lam), name=name, q=q,
        budget_us=HOST_US["lru_bwd"])


def _gate_tiles(w):
    per = GATE_TILE // HEAD_DIM
    w4 = w.reshape(LRU_BLOCKS // per, per, HEAD_DIM, HEAD_DIM)
    eye = jnp.eye(per, dtype=w.dtype)
    return jnp.einsum("bnij,nm->bnimj", w4, eye).reshape(LRU_BLOCKS // per, GATE_TILE, GATE_TILE)


def _gate_blocks(t):
    per = GATE_TILE // HEAD_DIM
    t5 = t.reshape(LRU_BLOCKS // per, per, HEAD_DIM, per, HEAD_DIM)
    eye = jnp.eye(per, dtype=t.dtype)
    return jnp.einsum("bnimj,nm->bnij", t5, eye).reshape(LRU_BLOCKS, HEAD_DIM, HEAD_DIM)


def _row(v):
    return v.reshape(1, -1)


def _local_step(x, mem, target, p, wfull, push_grad, q):
    bsz, t, d = x.shape
    n = bsz * t
    x2d = x.reshape(n, d)
    tgt = target.reshape(n, d)
    mem2d = mem.reshape(bsz * MEM_LEN, d)
    wr_t = [_gate_tiles(p["w_rg_r"][j]).astype(MXU) for j in range(N_A)]
    wi_t = [_gate_tiles(p["w_rg_i"][j]).astype(MXU) for j in range(N_A)]

    mn = [_norm_fwd(mem2d, _row(p["g_mem"][l]), name=f"mem_norm{l}") for l in range(DEPTH)]
    mkv = [None] * DEPTH
    h = _norm_fwd(x2d, _row(p["g_mix_pre"][0]), name="in_norm")
    xin = x2d
    sv = []
    kv = hkv = None
    for l in range(DEPTH):
        s = {"xin": xin, "h": h}
        if q is not None:
            q.horizon = (l + 2) * GROUPS_PER_LAYER
        mkv[l] = _mm_nn(mn[l], wfull("w_mem_kv", l), name=f"mem_kv{l}", q=q)
        if l < N_A:
            proj = _mm_nn(h, wfull("w_in_a", l), name=f"in_proj{l}", q=q)
            ycat, hs = _lru_fwd(proj, p["w_conv_a"][l], _row(p["b_conv_a"][l]), wr_t[l], _row(p["b_rg_r"][l]),
                                wi_t[l], _row(p["b_rg_i"][l]), _row(p["lru_lambda"][l]), bsz, name=f"lru_fwd{l}", q=q)
            s["hs"] = hs
            qblk = 2 * MIX_WIDTH // MEM_WIDTH
        else:
            if l == N_A:
                kv = _mm_nn(hkv, wfull("w_kv", 0), name="kv_proj", q=q)
            proj = _mm_nn(h, wfull("w_in_b", l - N_A), name=f"in_proj{l}", q=q)
            ycat = _swa_fwd(proj, kv, p["sinks_b"][l - N_A], bsz, name=f"swa_fwd{l}", q=q)
            qblk = MIX_WIDTH // MEM_WIDTH
        ycat = _mem_attn_fwd(proj, qblk, mkv[l], ycat, bsz, name=f"mem_attn_fwd{l}", q=q)
        y = _mm_nn(ycat, wfull("w_mix_out", l), name=f"mix_out{l}", q=q, out_dtype=MXU)
        x1, (h2,) = _resid_norm_fwd(xin, y, _row(p["g_mix_post"][l]), [_row(p["g_ffn_pre"][l])], name=f"mix_resid{l}", q=q)
        up = _mm_nn_slots(h2, wfull("w_ffn_up", l), name=f"ffn_up{l}", q=q, out_dtype=MXU)
        act, ug, uv = _ffn_act_fwd(up, p["w_ffn_conv"][l], _row(p["b_ffn_conv"][l]), bsz, name=f"ffn_act{l}", q=q)
        f = _mm_nn(act, wfull("w_ffn_down", l), name=f"ffn_down{l}", q=q, out_dtype=MXU)
        s.update(proj=proj, qblk=qblk, ycat=ycat, y=y, x1=x1, h2=h2, up=up, ug=ug, uv=uv, act=act, f=f)
        sv.append(s)
        if l < DEPTH - 1:
            g_pres = [_row(p["g_mix_pre"][l + 1])] + ([_row(p["g_kv"])] if l + 1 == N_A else [])
            xin, hn = _resid_norm_fwd(x1, f, _row(p["g_ffn_post"][l]), g_pres, name=f"ffn_resid{l}", q=q)
            h = hn[0]
            if l + 1 == N_A:
                hkv = hn[1]
        else:
            g_tot, sq, df, g_post_last = _loss_fwd(x1, f, _row(p["g_ffn_post"][l]), tgt, name="loss")

    if q is not None:
        q.horizon = LAST_GROUP
    gs = {k: [None] * DEPTH for k in ("g_mix_pre", "g_mix_post", "g_ffn_pre", "g_ffn_post", "g_mem",
                                       "w_ffn_conv", "b_ffn_conv")}
    ga = {k: [None] * N_A for k in ("w_conv_a", "b_conv_a", "w_rg_r", "b_rg_r", "w_rg_i", "b_rg_i", "lru_lambda")}
    gsink = [None] * (DEPTH - N_A)
    dkv_cur, dkv_prev = [], []
    gs["g_ffn_post"][DEPTH - 1] = g_post_last
    grad_x = None
    for l in reversed(range(DEPTH)):
        s = sv[l]
        dact = _mm_nt(df, wfull("w_ffn_down", l), name=f"d_act{l}", q=q, out_dtype=MXU)
        push_grad("w_ffn_down", l, _mm_tn(s["act"], df, name=f"dw_down{l}", q=q))
        dug, duv, gs["w_ffn_conv"][l], gs["b_ffn_conv"][l] = _ffn_act_bwd(
            s["up"], s["ug"], s["uv"], dact, p["w_ffn_conv"][l], bsz, name=f"ffn_act_bwd{l}", q=q)
        dh2 = _mm_ffn_dh(dug, duv, wfull("w_ffn_up", l), name=f"d_h2_{l}", q=q)
        up_slots = dict(slot_cols=2 * D_FF // N_CHIP, n_slots=N_CHIP)
        dwu = _mm_tn_slots(s["h2"], dug, name=f"dw_up_g{l}", q=q, **up_slots)
        push_grad("w_ffn_up", l, _mm_tn_slots(s["h2"], duv, name=f"dw_up_v{l}", q=q, out=dwu,
                                              first_slot=N_CHIP // 2, **up_slots))
        g1, dy, (gs["g_ffn_pre"][l],), gs["g_mix_post"][l] = _resid_norm_bwd(
            g_tot, [dh2], s["x1"], [_row(p["g_ffn_pre"][l])], s["y"], _row(p["g_mix_post"][l]), name=f"mix_resid_bwd{l}", q=q)
        dycat = _mm_nt(dy, wfull("w_mix_out", l), name=f"d_ycat{l}", q=q, out_dtype=MXU)
        push_grad("w_mix_out", l, _mm_tn(s["ycat"], dy, name=f"dw_mix_out{l}", q=q))
        if l < N_A:
            dproj, dwc, dbc, dwr, dbr, dwi, dbi, dlam = _lru_bwd(
                s["proj"], s["hs"], dycat, p["w_conv_a"][l], _row(p["b_conv_a"][l]), wr_t[l], _row(p["b_rg_r"][l]),
                wi_t[l], _row(p["b_rg_i"][l]), _row(p["lru_lambda"][l]), bsz, name=f"lru_bwd{l}", q=q)
            ga["w_conv_a"][l], ga["b_conv_a"][l], ga["lru_lambda"][l] = dwc, dbc[0], dlam[0]
            ga["w_rg_r"][l], ga["w_rg_i"][l] = _gate_blocks(dwr), _gate_blocks(dwi)
            ga["b_rg_r"][l] = dbr.reshape(LRU_BLOCKS, HEAD_DIM)
            ga["b_rg_i"][l] = dbi.reshape(LRU_BLOCKS, HEAD_DIM)
            w_in, j = "w_in_a", l
        else:
            dproj, dc, dp_, dsk = _swa_bwd(s["proj"], kv, p["sinks_b"][l - N_A], dycat, bsz, name=f"swa_bwd{l}", q=q)
            dkv_cur.append(dc)
            dkv_prev.append(dp_)
            gsink[l - N_A] = dsk[0, :SWA_HEADS]
            w_in, j = "w_in_b", l - N_A
        dproj, dmkv = _mem_attn_bwd(s["proj"], s["qblk"], mkv[l], dycat, dproj, bsz, name=f"mem_attn_bwd{l}", q=q)
        dh = _mm_nt(dproj, wfull(w_in, j), name=f"d_h{l}", q=q, out_dtype=MXU)
        push_grad(w_in, j, _mm_tn(s["h"], dproj, name=f"dw_in{l}", q=q))
        dmkv = dmkv.astype(MXU)
        dmn = _mm_nt(dmkv, wfull("w_mem_kv", l), name=f"d_mem_norm{l}", q=q)
        push_grad("w_mem_kv", l, _mm_tn(mn[l], dmkv, name=f"dw_mem_kv{l}", q=q))
        gs["g_mem"][l] = _norm_bwd_dg(dmn, mem2d, _row(p["g_mem"][l]), name=f"mem_norm_bwd{l}")
        dhs, g_pres = [dh], [_row(p["g_mix_pre"][l])]
        if l == N_A:
            dkv = _swa_dkv_combine(dkv_cur, dkv_prev, bsz, name="dkv_combine")
            dhs.append(_mm_nt(dkv, wfull("w_kv", 0), name="d_hkv", q=q, out_dtype=MXU))
            g_pres.append(_row(p["g_kv"]))
            push_grad("w_kv", 0, _mm_tn(hkv, dkv, name="dw_kv", q=q))
        if l > 0:
            g_tot, df, dgpre, gs["g_ffn_post"][l - 1] = _resid_norm_bwd(
                g1, dhs, s["xin"], g_pres, sv[l - 1]["f"], _row(p["g_ffn_post"][l - 1]), name=f"ffn_resid_bwd{l - 1}", q=q)
        else:
            grad_x, _, dgpre, _ = _resid_norm_bwd(g1, dhs, s["xin"], g_pres, None, None, name="in_norm_bwd", q=q)
        gs["g_mix_pre"][l] = dgpre[0]
        if l == N_A:
            g_kv = dgpre[1][0]

    grads = {}
    for k in ("g_mix_pre", "g_mix_post", "g_ffn_pre", "g_ffn_post", "g_mem", "b_ffn_conv"):
        grads[k] = jnp.concatenate(gs[k], axis=0)
    grads["w_ffn_conv"] = jnp.stack(gs["w_ffn_conv"])
    for k, v in ga.items():
        grads[k] = jnp.stack(v)
    grads["sinks_b"] = jnp.stack(gsink)
    grads["g_kv"] = g_kv
    return jnp.sum(sq), grad_x.reshape(bsz, t, d), grads


N_CHIP = 4
HALF_ALIGN = 16
D2D_STREAMS = 2
MIN_PART_BYTES = 128 * 1024


def _full_shape(kind, shard_shape):
    l, r, c = shard_shape
    return {"row": (l, N_CHIP * r, c), "slot": (N_CHIP, l, r, c)}[kind]


def _slot_view(ref, kind, shard_shape, s, hf, sub=(0, 1)):
    _, r, _ = shard_shape
    rh = r // 2
    if hf is None:
        size = r // sub[1]
        start = sub[0] * size
    else:
        size = rh // sub[1]
        start = hf * rh + sub[0] * size
    if kind == "row":
        start = s * r + start
    if not isinstance(start, int):
        start = pl.multiple_of(start, HALF_ALIGN)
    rows = pl.ds(start, size)
    if kind == "row":
        return ref.at[:, rows, :]
    return ref.at[s, :, rows, :]


def _half_view(ref, shard_shape, hf, sub=(0, 1)):
    rh = shard_shape[1] // 2
    size = rh // sub[1]
    return ref.at[:, pl.ds(pl.multiple_of(hf * rh + sub[0] * size, HALF_ALIGN), size), :]


def _mesh_pos():
    return lax.axis_index("x"), lax.axis_index("y"), lax.axis_index("c")


def _other_chips(x, y):
    return [(1 - x, y), (x, 1 - y), (1 - x, 1 - y)]


ICI_BYTES_PER_US = 6.0e4
ICI_GATHER_BYTES_PER_US = 5.5e4
D2D_BYTES_PER_US = 4.0e5


class _Chunk:
    def __init__(self, group, cost, ins, out_shapes, alias, n_sem, start, finish, done, buffer=None, bind=None):
        self.group, self.cost, self.ins, self.out_shapes, self.alias, self.n_sem = group, cost, ins, out_shapes, alias, n_sem
        self.start, self.finish, self.done = start, finish, done
        self.buffer = buffer
        self.bind = bind

    def prepare(self):
        if self.bind is not None:
            self.bind(self)


def _merged(chunks):
    groups, by_buffer = [], {}
    for ch in chunks:
        key = None if ch.buffer is None else (id(ch.buffer[0]), ch.buffer[1])
        if key is not None and key in by_buffer:
            by_buffer[key].append(ch)
        else:
            groups.append([ch])
            if key is not None:
                by_buffer[key] = groups[-1]
    out = []
    for parts in groups:
        if len(parts) == 1:
            out.append(parts[0])
            continue
        offs = [sum(p.n_sem for p in parts[:i]) for i in range(len(parts))]

        def run(phase, ins, outs, ss, rs, b, parts=parts, offs=offs):
            for p, o in zip(parts, offs):
                getattr(p, phase)(ins, outs, ss, rs, b + o)

        def done(outs, parts=parts):
            for p in parts:
                p.done(outs)

        first = parts[0]
        out.append(_Chunk(first.group, sum(p.cost for p in parts), first.ins, first.out_shapes, first.alias,
                          sum(p.n_sem for p in parts), functools.partial(run, "start"),
                          functools.partial(run, "finish"), done))
    return out


LAST_GROUP = 1 << 30
MIN_CARRIED_US = 8.0


class _CommQueue:
    def __init__(self):
        self.pending = []
        self.flushes = 0
        self.horizon = LAST_GROUP

    def push(self, chunk):
        self.pending.append(chunk)

    def take(self, budget_us):
        got, used = [], 0.0
        for ch in sorted(self.pending, key=lambda ch: (ch.group, -ch.cost)):
            if ch.group >= self.horizon and ch.group != LAST_GROUP:
                continue
            if used + ch.cost <= budget_us and not self._shares_buffer(ch, got):
                got.append(ch)
                used += ch.cost
        if used < MIN_CARRIED_US:
            return []
        return self._taken(got)

    @staticmethod
    def _shares_buffer(ch, others):
        return ch.buffer is not None and any(
            o.buffer is not None and o.buffer[0] is ch.buffer[0] and o.buffer[1] != ch.buffer[1] for o in others)

    def _taken(self, got):
        self.pending = [ch for ch in self.pending if ch not in got]
        for ch in got:
            ch.prepare()
        return _merged(got)

    def flush(self, group=LAST_GROUP):
        while True:
            chunks = []
            for ch in self.pending:
                if ch.group <= group and not self._shares_buffer(ch, chunks):
                    chunks.append(ch)
            if not chunks:
                return
            _run_chunks(self._taken(chunks), name=f"comm_flush{self.flushes}")
            self.flushes += 1


def _run_chunks(chunks, *, name):
    ins = [a for ch in chunks for a in ch.ins]
    outs = [s for ch in chunks for s in ch.out_shapes]
    alias, offs = {}, []
    i0 = o0 = s0 = 0
    for ch in chunks:
        offs.append((i0, o0, s0))
        for ci, co in ch.alias.items():
            alias[i0 + ci] = o0 + co
        i0 += len(ch.ins)
        o0 += len(ch.out_shapes)
        s0 += ch.n_sem

    def body(*refs):
        send_sems, recv_sems = refs[i0 + o0:]
        for phase in ("start", "finish"):
            for ch, (a, b, s) in zip(chunks, offs):
                getattr(ch, phase)(refs[a:a + len(ch.ins)], refs[i0 + b:i0 + b + len(ch.out_shapes)],
                                   send_sems, recv_sems, s)

    hbm = pl.BlockSpec(memory_space=pl.ANY)
    res = pl.pallas_call(
        body, in_specs=[hbm] * i0, out_specs=[hbm] * o0, out_shape=outs,
        scratch_shapes=[pltpu.SemaphoreType.DMA((s0,)), pltpu.SemaphoreType.DMA((s0,))],
        input_output_aliases=alias, name=name, compiler_params=pltpu.CompilerParams(has_side_effects=True))(*ins)
    for ch, (_, b, _) in zip(chunks, offs):
        ch.done(list(res[b:b + len(ch.out_shapes)]))


def _remote(src, dst, send_sems, recv_sems, k, dev):
    return pltpu.make_async_remote_copy(src_ref=src, dst_ref=dst, send_sem=send_sems.at[k], recv_sem=recv_sems.at[k],
                                        device_id=dev, device_id_type=MESH_T)


def _gather_chunks(q, group, kind, shard, l, ready):
    _, r, c = shard.shape
    shp = (1, r, c)
    rh = r // 2
    parts = max(p for p in (8, 4, 2, 1)
                if (rh // p) % HALF_ALIGN == 0 and (p == 1 or (rh // p) * c * shard.dtype.itemsize >= MIN_PART_BYTES))
    part_bytes = (rh // parts) * c * shard.dtype.itemsize
    full_type = jax.ShapeDtypeStruct(_full_shape(kind, shp), shard.dtype)
    state = {"full": None, "parts_done": 0}

    def bind_first(ch):
        ch.ins, ch.alias = ([shard], {}) if state["full"] is None else ([shard, state["full"]], {1: 0})

    def bind_full(ch):
        ch.ins = [state["full"]]

    def make_part(p):
        sub = (p, parts)

        def any_part(full):
            return _slot_view(full, kind, shp, 0, 0, sub)

        def own_rows(src):
            return src.at[:, pl.ds(p * (r // parts), r // parts), :]

        def start1(ins, outs, ss, rs, b):
            x, y, c_ = _mesh_pos()
            src, full, me = ins[0].at[pl.ds(l, 1)], outs[0], 2 * x + y
            pltpu.make_async_copy(own_rows(src), _slot_view(full, kind, shp, me, None, sub), ss.at[b + N_CHIP - 1]).start()
            for j, (ox, oy) in enumerate(_other_chips(x, y)):
                _remote(_half_view(src, shp, c_, sub), _slot_view(full, kind, shp, me, c_, sub), ss, rs, b + j,
                        (ox, oy, c_)).start()

        def finish1(ins, outs, ss, rs, b):
            x, y, c_ = _mesh_pos()
            h = any_part(outs[0])
            for j in range(N_CHIP - 1):
                _remote(h, h, ss, rs, b + j, (x, y, 1 - c_)).wait()
            pltpu.make_async_copy(own_rows(ins[0].at[pl.ds(l, 1)]), _slot_view(outs[0], kind, shp, 0, None, sub),
                                  ss.at[b + N_CHIP - 1]).wait()

        def start2(ins, outs, ss, rs, b):
            x, y, c_ = _mesh_pos()
            for j, (ox, oy) in enumerate(_other_chips(x, y)):
                v = _slot_view(outs[0], kind, shp, 2 * ox + oy, c_, sub)
                _remote(v, v, ss, rs, b + j, (x, y, 1 - c_)).start()

        def finish2(ins, outs, ss, rs, b):
            x, y, c_ = _mesh_pos()
            h = any_part(outs[0])
            for j in range(N_CHIP - 1):
                _remote(h, h, ss, rs, b + j, (x, y, 1 - c_)).wait()

        def done2(outs):
            state["full"] = outs[0]
            state["parts_done"] += 1
            if state["parts_done"] == parts:
                ready(outs[0])

        def done1(outs):
            state["full"] = outs[0]
            q.push(_Chunk(group, 3 * part_bytes / D2D_BYTES_PER_US, None, [full_type], {0: 0}, N_CHIP - 1,
                          start2, finish2, done2, buffer=(state, 2), bind=bind_full))

        return _Chunk(group, 3 * part_bytes / ICI_GATHER_BYTES_PER_US, None, [full_type], None,
                      N_CHIP, start1, finish1, done1, buffer=(state, 1), bind=bind_first)

    for p in range(parts):
        q.push(make_part(p))


def _reduce_scatter_chunks(q, kind, grad, shard_shape, pos, name, ready):
    _, r, c = shard_shape
    shp = (1, r, c)
    rh = r // 2

    rp = rh // D2D_STREAMS

    def landing(ref, s, i):
        return ref.at[s, :, pl.ds(i * rp, rp), :]

    def start1(ins, outs, ss, rs, b):
        x, y, c_ = _mesh_pos()
        for s in range(N_CHIP):
            for i in range(D2D_STREAMS):
                _remote(_slot_view(ins[0], kind, shp, s, 1 - c_, (i, D2D_STREAMS)), landing(outs[0], s, i),
                        ss, rs, b + s * D2D_STREAMS + i, (x, y, 1 - c_)).start()

    def finish1(ins, outs, ss, rs, b):
        x, y, c_ = _mesh_pos()
        for s in range(N_CHIP):
            for i in range(D2D_STREAMS):
                v = landing(outs[0], s, i)
                _remote(v, v, ss, rs, b + s * D2D_STREAMS + i, (x, y, 1 - c_)).wait()

    def start2(ins, outs, ss, rs, b):
        x, y, c_ = _mesh_pos()
        for j, (ox, oy) in enumerate(_other_chips(x, y)):
            _remote(ins[0].at[2 * ox + oy], outs[0].at[j], ss, rs, b + j, (ox, oy, c_)).start()

    def finish2(ins, outs, ss, rs, b):
        x, y, c_ = _mesh_pos()
        for j in range(N_CHIP - 1):
            _remote(outs[0].at[j], outs[0].at[j], ss, rs, b + j, (x, y, 1 - c_)).wait()

    def start3(ins, outs, ss, rs, b):
        x, y, c_ = _mesh_pos()
        for i in range(D2D_STREAMS):
            v = _half_view(outs[0], shp, c_, (i, D2D_STREAMS))
            _remote(v, v, ss, rs, b + i, (x, y, 1 - c_)).start()

    def finish3(ins, outs, ss, rs, b):
        x, y, c_ = _mesh_pos()
        for i in range(D2D_STREAMS):
            v = _half_view(outs[0], shp, c_, (i, D2D_STREAMS))
            _remote(v, v, ss, rs, b + i, (x, y, 1 - c_)).wait()

    def done2(pair, outs):
        half = _rs_chip_add(pair, outs[0], shp, pos, name=f"rs_chip_add_{name}")
        q.push(_Chunk(LAST_GROUP, rh * c * 4 / D2D_BYTES_PER_US, [half], [jax.ShapeDtypeStruct(half.shape, half.dtype)],
                      {0: 0}, D2D_STREAMS, start3, finish3, lambda o: ready(o[0])))

    def done1(outs):
        pair, wire = _rs_pair_add(grad, outs[0], kind, shp, pos, name=f"rs_pair_add_{name}")
        q.push(_Chunk(LAST_GROUP, 3 * rh * c * wire.dtype.itemsize / ICI_BYTES_PER_US, [wire],
                      [jax.ShapeDtypeStruct((N_CHIP - 1, 1, rh, c), wire.dtype)], {}, N_CHIP - 1,
                      start2, finish2, functools.partial(done2, pair)))

    q.push(_Chunk(LAST_GROUP, N_CHIP * rh * c * 4 / D2D_BYTES_PER_US, [grad],
                  [jax.ShapeDtypeStruct((N_CHIP, 1, rh, c), F32)], {}, N_CHIP * D2D_STREAMS, start1, finish1, done1))


N_DEV = 8


def _allgather_chunk(q, group, vec, ready):
    def peer(k, x, y, c):
        return ((1 - x) if k & 4 else x, (1 - y) if k & 2 else y, (1 - c) if k & 1 else c)

    def start(ins, outs, ss, rs, b):
        x, y, c = _mesh_pos()
        me = 4 * x + 2 * y + c
        pltpu.make_async_copy(ins[0], outs[0].at[me], ss.at[b + N_DEV - 1]).start()
        for k in range(1, N_DEV):
            _remote(ins[0], outs[0].at[me], ss, rs, b + k - 1, peer(k, x, y, c)).start()

    def finish(ins, outs, ss, rs, b):
        x, y, c = _mesh_pos()
        for k in range(1, N_DEV):
            _remote(ins[0], outs[0].at[0], ss, rs, b + k - 1, peer(k, x, y, c)).wait()
        pltpu.make_async_copy(ins[0], outs[0].at[0], ss.at[b + N_DEV - 1]).wait()

    bytes_in = (N_DEV - 2) * vec.size * 4
    q.push(_Chunk(group, bytes_in / ICI_BYTES_PER_US, [vec], [jax.ShapeDtypeStruct((N_DEV,) + vec.shape, F32)], {},
                  N_DEV, start, finish, lambda o: ready(o[0])))


def _allreduce8(vec, *, name):
    r = vec.shape[0]
    rh = r // 2

    def body(v_ref, o_ref, sib_ref, chips_ref, send_sems, recv_sems):
        x, y, c = _mesh_pos()
        sib = (x, y, 1 - c)
        me = 2 * x + y
        pair = _remote(v_ref, sib_ref, send_sems, recv_sems, 0, sib)
        pair.start()
        pair.wait()
        rows = pl.ds(pl.multiple_of(c * rh, SUBLANE), rh)
        chips_ref[me] = v_ref[rows, :] + sib_ref[rows, :]
        copies = []
        for j, (ox, oy) in enumerate(_other_chips(x, y)):
            cp = _remote(chips_ref.at[me], chips_ref.at[me], send_sems, recv_sems, 1 + j, (ox, oy, c))
            cp.start()
            copies.append(cp)
        for cp in copies:
            cp.wait()
        acc = chips_ref[0]
        for s in range(1, N_CHIP):
            acc = acc + chips_ref[s]
        o_ref[rows, :] = acc
        swap = _remote(o_ref.at[rows, :], o_ref.at[rows, :], send_sems, recv_sems, N_CHIP, sib)
        swap.start()
        swap.wait()

    vm = pl.BlockSpec(memory_space=pltpu.VMEM)
    return pl.pallas_call(
        body, in_specs=[vm], out_specs=vm, out_shape=jax.ShapeDtypeStruct((r, LANE), F32),
        scratch_shapes=[pltpu.VMEM((r, LANE), F32), pltpu.VMEM((N_CHIP, rh, LANE), F32),
                        pltpu.SemaphoreType.DMA((N_CHIP + 1,)), pltpu.SemaphoreType.DMA((N_CHIP + 1,))],
        name=name, compiler_params=pltpu.CompilerParams(has_side_effects=True, vmem_limit_bytes=VMEM_LIMIT_V7X))(vec)


def _rs_pair_add(g, recv, kind, shape, pos, *, name):
    l, r, c = shape
    assert l == 1
    rh = r // 2
    if kind == "row":
        gspec = pl.BlockSpec((None, rh, c), lambda s, pos: (0, 2 * s + pos[0], 0))
    else:
        gspec = pl.BlockSpec((None, None, rh, c), lambda s, pos: (s, 0, pos[0], 0))
    pspec = pl.BlockSpec((None, None, rh, c), lambda s, pos: (s, 0, 0, 0))

    def body(pos_ref, g_ref, r_ref, own_ref, pw_ref):
        v = g_ref[...] + r_ref[...]
        pw_ref[...] = v.astype(pw_ref.dtype)

        @pl.when(pl.program_id(0) == pos_ref[1])
        def _():
            own_ref[...] = v

    return pl.pallas_call(
        body,
        grid_spec=pltpu.PrefetchScalarGridSpec(
            num_scalar_prefetch=1, grid=(N_CHIP,), in_specs=[gspec, pspec],
            out_specs=[pl.BlockSpec((None, rh, c), lambda s, pos: (0, 0, 0)), pspec]),
        out_shape=[jax.ShapeDtypeStruct((1, rh, c), F32), jax.ShapeDtypeStruct((N_CHIP, 1, rh, c), MXU)],
        name=name, compiler_params=_cp((ARB,)))(pos, g, recv)


def _rs_chip_add(p, recv, shape, pos, *, name):
    l, r, c = shape
    rh = r // 2

    def body(pos_ref, p_ref, r_ref, o_ref):
        del pos_ref
        acc = p_ref[...]
        for j in range(N_CHIP - 1):
            acc = acc + r_ref[j].astype(F32)
        o_ref[...] = acc

    return pl.pallas_call(
        body,
        grid_spec=pltpu.PrefetchScalarGridSpec(
            num_scalar_prefetch=1, grid=(l,),
            in_specs=[pl.BlockSpec((None, rh, c), lambda i, pos: (i, 0, 0)),
                      pl.BlockSpec((N_CHIP - 1, None, rh, c), lambda i, pos: (0, i, 0, 0))],
            out_specs=pl.BlockSpec((None, rh, c), lambda i, pos: (i, pos[0], 0))),
        out_shape=jax.ShapeDtypeStruct((l, r, c), F32),
        name=name, compiler_params=_cp((PAR,)))(pos, p, recv)


ADAM_BLOCK_ELEMS = 384 * 1024


def _adam_math(w, g, m, v):
    c1 = 1.0 / (1.0 - ADAM_B1 ** ADAM_STEP)
    c2 = 1.0 / (1.0 - ADAM_B2 ** ADAM_STEP)
    nm = ADAM_B1 * m + (1.0 - ADAM_B1) * g
    nv = ADAM_B2 * v + (1.0 - ADAM_B2) * (g * g)
    return -ADAM_LR * ((nm * c1) / (jnp.sqrt(nv * c2) + ADAM_EPS) + ADAM_WD * w), nm, nv


def _adamw_layer(w, g, m, v, outs, l, *, name):
    _, r, c = w.shape
    tr = _tile(r, max(SUBLANE, ADAM_BLOCK_ELEMS // c // SUBLANE * SUBLANE), SUBLANE)

    def body(w_ref, g_ref, m_ref, v_ref, *rest):
        go_ref, d_ref, nm_ref, nv_ref = rest[4:]
        gg = g_ref[...]
        go_ref[...] = gg
        d_ref[...], nm_ref[...], nv_ref[...] = _adam_math(w_ref[...], gg, m_ref[...], v_ref[...])

    lay = pl.BlockSpec((None, tr, c), lambda j: (l, j, 0))
    hbm = pl.BlockSpec(memory_space=pl.ANY)
    return pl.pallas_call(
        body, grid=(r // tr,),
        in_specs=[lay, pl.BlockSpec((None, tr, c), lambda j: (0, j, 0)), lay, lay] + [hbm] * 4,
        out_specs=[lay] * 4, out_shape=[jax.ShapeDtypeStruct(w.shape, F32)] * 4,
        input_output_aliases={4 + i: i for i in range(4)},
        name=name, compiler_params=_cp((PAR,)))(w, g, m, v, *outs)


def _adamw(w, g, m, v, *, name):
    shape = w.shape
    if w.ndim == 2:
        w, g, m, v = (a[None] for a in (w, g, m, v))
    l, r, c = w.shape
    tr = _tile(r, max(SUBLANE, ADAM_BLOCK_ELEMS // c // SUBLANE * SUBLANE), SUBLANE)

    def body(w_ref, g_ref, m_ref, v_ref, d_ref, nm_ref, nv_ref):
        d_ref[...], nm_ref[...], nv_ref[...] = _adam_math(w_ref[...], g_ref[...], m_ref[...], v_ref[...])

    spec = pl.BlockSpec((None, tr, c), lambda i, j: (i, j, 0))
    outs = pl.pallas_call(
        body, grid=(l, r // tr), in_specs=[spec] * 4, out_specs=[spec] * 3,
        out_shape=[jax.ShapeDtypeStruct((l, r, c), F32)] * 3,
        name=name, compiler_params=_cp((PAR, PAR)))(w, g, m, v)
    return tuple(o.reshape(shape) for o in outs)


PACK_ROWS = 2 * SUBLANE * LANE


def _pack(arrays):
    flat = jnp.concatenate([a.reshape(-1).astype(F32) for a in arrays])
    pad = (-flat.shape[0]) % PACK_ROWS
    return jnp.pad(flat, (0, pad)).reshape(-1, LANE)


def _unpack(packed, shapes):
    flat = packed.reshape(-1)
    out, off = [], 0
    for s in shapes:
        size = int(np.prod(s))
        out.append(flat[off:off + size].reshape(s))
        off += size
    return out


BIG = (("w_mem_kv", "row"), ("w_mix_out", "row"), ("w_ffn_up", "slot"), ("w_ffn_down", "row"),
       ("w_in_a", "slot"), ("w_in_b", "row"), ("w_kv", "row"))
COLUMN_SHARDED_AS_COLUMNS = ("w_in_a",)
SMALL_SHARDED = (("w_ffn_conv", 2), ("w_conv_a", 2), ("b_conv_a", 1), ("lru_lambda", 1))
SMALL_REPLICATED = ("g_mix_pre", "g_mix_post", "g_ffn_pre", "g_ffn_post", "g_mem", "b_ffn_conv",
                    "w_rg_r", "b_rg_r", "w_rg_i", "b_rg_i", "sinks_b", "g_kv")
WEIGHTS = ("g_mix_pre", "g_mix_post", "g_ffn_pre", "g_ffn_post", "g_mem", "w_mem_kv", "w_mix_out", "w_ffn_up",
           "w_ffn_conv", "b_ffn_conv", "w_ffn_down", "w_in_a", "w_conv_a", "b_conv_a", "w_rg_r", "b_rg_r", "w_rg_i",
           "b_rg_i", "lru_lambda", "w_in_b", "sinks_b", "g_kv", "w_kv")


def _slot_to_cols(a):
    s, l, r, c = a.shape
    return a.transpose(1, 2, 0, 3).reshape(l, r, s * c)


def _cols_to_slot(a):
    l, r, c4 = a.shape
    return a.reshape(l, r, N_CHIP, c4 // N_CHIP).transpose(2, 0, 1, 3)


GROUPS_PER_LAYER = 8


def _layer_weights(layer):
    names = [("w_mem_kv", layer), ("w_in_a", layer) if layer < N_A else ("w_in_b", layer - N_A)]
    if layer == N_A:
        names.append(("w_kv", 0))
    return names + [("w_mix_out", layer), ("w_ffn_up", layer), ("w_ffn_down", layer)]


def _train_step(x, mem, target, w, m, v):
    xi, yi, ci = _mesh_pos()
    chip = 2 * xi + yi
    pos = jnp.stack([ci, chip]).astype(jnp.int32)

    q = _CommQueue()
    kinds = dict(BIG)
    as3 = lambda a: a if a.ndim == 3 else a[None]
    w3, m3, v3 = ({k: as3(d[k]) for k, _ in BIG} for d in (w, m, v))
    shards = {k: w3[k].astype(MXU) for k, _ in BIG}

    gathered = {}

    def on_gathered(k, l, full):
        gathered[k, l] = _slot_to_cols(full) if k in COLUMN_SHARDED_AS_COLUMNS else full

    group_of = {}

    for layer in range(DEPTH):
        for i, (k, l) in enumerate(_layer_weights(layer)):
            group_of[k, l] = layer * GROUPS_PER_LAYER + i
            _gather_chunks(q, group_of[k, l], kinds[k], shards[k], l, functools.partial(on_gathered, k, l))

    def wfull(k, l):
        if (k, l) not in gathered:
            q.flush(group_of[k, l])
        return gathered[k, l]

    small = {}
    _allgather_chunk(q, 0, _pack([w[k] for k, _ in SMALL_SHARDED]), functools.partial(small.__setitem__, "stacked"))
    q.flush(1)

    big_out = {k: [lax.empty(w3[k].shape, F32) for _ in range(4)] for k, _ in BIG}

    def on_reduced(k, l, g):
        big_out[k] = _adamw_layer(w3[k], g, m3[k], v3[k], big_out[k], l, name=f"adamw_{k}{l}")

    def push_grad(k, l, g):
        if k in COLUMN_SHARDED_AS_COLUMNS:
            g = _cols_to_slot(g)
        _reduce_scatter_chunks(q, kinds[k], g, (1,) + w3[k].shape[1:], pos, f"{k}{l}", functools.partial(on_reduced, k, l))

    small_shapes = [w[k].shape for k, _ in SMALL_SHARDED]
    per_chip = [_unpack(small["stacked"][2 * s], small_shapes) for s in range(N_CHIP)]
    p = {k: w[k] for k in SMALL_REPLICATED}
    for i, (k, axis) in enumerate(SMALL_SHARDED):
        p[k] = jnp.concatenate([per_chip[s][i] for s in range(N_CHIP)], axis=axis)

    sq, grad_x, g = _local_step(x, mem, target, p, wfull, push_grad, q)
    loss = lax.psum(0.5 * sq / D_MODEL, ("x", "y", "c"))
    q.flush()

    small_names = [k for k, _ in SMALL_SHARDED] + list(SMALL_REPLICATED)
    summed = _allreduce8(_pack([g[k] for k in small_names]), name="allreduce_small")
    gsum = dict(zip(small_names, _unpack(summed, [p[k].shape for k in small_names])))
    for k, axis in SMALL_SHARDED:
        gsum[k] = lax.dynamic_slice_in_dim(gsum[k], chip * w[k].shape[axis], w[k].shape[axis], axis)

    delta, new_m, new_v = {}, {}, {}
    for k, _ in BIG:
        gsum[k], delta[k], new_m[k], new_v[k] = (o.reshape(w[k].shape) for o in big_out[k])
    for k in small_names:
        as2 = lambda a: a.reshape(-1, a.shape[-1])
        outs = _adamw(as2(w[k]), as2(gsum[k]), as2(m[k]), as2(v[k]), name=f"adamw_{k}")
        delta[k], new_m[k], new_v[k] = (o.reshape(w[k].shape) for o in outs)
    return (loss, grad_x, *[gsum[k] for k in WEIGHTS], *[delta[k] for k in WEIGHTS],
            *[new_m[k] for k in WEIGHTS], *[new_v[k] for k in WEIGHTS])


def kernel(x, mem, g_mix_pre, g_mix_post, g_ffn_pre, g_ffn_post, g_mem, w_mem_kv, w_mix_out, w_ffn_up, w_ffn_conv, b_ffn_conv, w_ffn_down, w_in_a, w_conv_a, b_conv_a, w_rg_r, b_rg_r, w_rg_i, b_rg_i, lru_lambda, w_in_b, sinks_b, g_kv, w_kv, loss_target, m_g_mix_pre, m_g_mix_post, m_g_ffn_pre, m_g_ffn_post, m_g_mem, m_w_mem_kv, m_w_mix_out, m_w_ffn_up, m_w_ffn_conv, m_b_ffn_conv, m_w_ffn_down, m_w_in_a, m_w_conv_a, m_b_conv_a, m_w_rg_r, m_b_rg_r, m_w_rg_i, m_b_rg_i, m_lru_lambda, m_w_in_b, m_sinks_b, m_g_kv, m_w_kv, v_g_mix_pre, v_g_mix_post, v_g_ffn_pre, v_g_ffn_post, v_g_mem, v_w_mem_kv, v_w_mix_out, v_w_ffn_up, v_w_ffn_conv, v_b_ffn_conv, v_w_ffn_down, v_w_in_a, v_w_conv_a, v_b_conv_a, v_w_rg_r, v_b_rg_r, v_w_rg_i, v_b_rg_i, v_lru_lambda, v_w_in_b, v_sinks_b, v_g_kv, v_w_kv):
    args = (g_mix_pre, g_mix_post, g_ffn_pre, g_ffn_post, g_mem, w_mem_kv, w_mix_out, w_ffn_up, w_ffn_conv, b_ffn_conv, w_ffn_down, w_in_a, w_conv_a, b_conv_a, w_rg_r, b_rg_r, w_rg_i, b_rg_i, lru_lambda, w_in_b, sinks_b, g_kv, w_kv)
    ms = (m_g_mix_pre, m_g_mix_post, m_g_ffn_pre, m_g_ffn_post, m_g_mem, m_w_mem_kv, m_w_mix_out, m_w_ffn_up, m_w_ffn_conv, m_b_ffn_conv, m_w_ffn_down, m_w_in_a, m_w_conv_a, m_b_conv_a, m_w_rg_r, m_b_rg_r, m_w_rg_i, m_b_rg_i, m_lru_lambda, m_w_in_b, m_sinks_b, m_g_kv, m_w_kv)
    vs = (v_g_mix_pre, v_g_mix_post, v_g_ffn_pre, v_g_ffn_post, v_g_mem, v_w_mem_kv, v_w_mix_out, v_w_ffn_up, v_w_ffn_conv, v_b_ffn_conv, v_w_ffn_down, v_w_in_a, v_w_conv_a, v_b_conv_a, v_w_rg_r, v_b_rg_r, v_w_rg_i, v_b_rg_i, v_lru_lambda, v_w_in_b, v_sinks_b, v_g_kv, v_w_kv)
    return _train_step(x, mem, loss_target, dict(zip(WEIGHTS, args)), dict(zip(WEIGHTS, ms)), dict(zip(WEIGHTS, vs)))
```

```python
import functools
import math

import numpy as np
import jax
import jax.numpy as jnp
from jax import lax
from jax.experimental import pallas as pl
from jax.experimental.pallas import tpu as pltpu

F32 = jnp.float32
MXU = jnp.bfloat16

D_MODEL = 1024
HEAD_DIM = 64
MEM_LEN = 256
MEM_HEADS = 4
MEM_WIDTH = MEM_HEADS * HEAD_DIM
MIX_WIDTH = D_MODEL - MEM_WIDTH
LRU_BLOCKS = MIX_WIDTH // HEAD_DIM
LRU_CONV = 4
LRU_C = 8.0
SWA_HEADS = MIX_WIDTH // HEAD_DIM
SWA_KV_HEADS = 4
SWA_GROUP = SWA_HEADS // SWA_KV_HEADS
WINDOW = 128
D_FF = 2816
FFN_CONV = 3
EPS = 1e-6
DEPTH = 4
N_A = 2

ADAM_LR = 0.001
ADAM_B1 = 0.9
ADAM_B2 = 0.999
ADAM_EPS = 1e-08
ADAM_WD = 0.01
ADAM_STEP = 10

VMEM_LIMIT_V7X = 56 * 1024 * 1024
LANE = 128
SUBLANE = 8
GATE_TILE = 256
MESH_T = pl.DeviceIdType.MESH


def _alibi_slopes(n):
    def pow2_slopes(m):
        start = 2.0 ** (-8.0 / m)
        return [start ** (i + 1) for i in range(m)]
    c = 2 ** int(math.floor(math.log2(n)))
    s = pow2_slopes(c)
    if c != n:
        s = s + pow2_slopes(2 * c)[0::2][: n - c]
    return [float(np.float32(v)) for v in s]


SLOPES = _alibi_slopes(SWA_HEADS)


def _tile(n, cap, mult=LANE):
    best = None
    for t in range(mult, min(n, cap) + 1, mult):
        if n % t == 0:
            best = t
    return best if best is not None else n


def _cp(sem):
    return pltpu.CompilerParams(dimension_semantics=sem, vmem_limit_bytes=VMEM_LIMIT_V7X)


MM_VMEM_BUDGET = 40 * 1024 * 1024
HBM_BYTES_PER_US_V7X = 3.0e6
GRID_STEP_US = 0.35


def _divisors(n, mult):
    return [t for t in range(mult, n + 1, mult) if n % t == 0] or [n]


def _mm_tiles(m, k, n, out_bytes):
    best = None
    for tm in _divisors(m, 256):
        for tn in _divisors(n, LANE):
            vmem = 2 * (tm * k * 2 + k * tn * 2 + tm * tn * out_bytes)
            if vmem > MM_VMEM_BUDGET:
                continue
            steps = (m // tm) * (n // tn)
            b_reads = 1 if tn == n else m // tm
            traffic = m * k * 2 + k * n * 2 * b_reads + m * n * out_bytes
            first = tm * k * 2 + k * tn * 2
            cost = (traffic + first) / HBM_BYTES_PER_US_V7X + steps * GRID_STEP_US
            if best is None or cost < best[0]:
                best = (cost, tm, tn)
    return best[1], best[2]


def _mm_tn_tiles(k, m, n, whole_n=False):
    best = None
    for tm in _divisors(m, LANE):
        for tn in ([n] if whole_n else _divisors(n, LANE)):
            for tk in _divisors(k, 512):
                vmem = 2 * (tk * tm * 2 + tk * tn * 2 + tm * tn * 4)
                if vmem > MM_VMEM_BUDGET:
                    continue
                steps = (m // tm) * (n // tn) * (k // tk)
                traffic = k * m * 2 * (n // tn) + k * n * 2 * (m // tm) + m * n * 4
                cost = traffic / HBM_BYTES_PER_US_V7X + steps * GRID_STEP_US
                if best is None or cost < best[0]:
                    best = (cost, tk, tm, tn)
    return best[1], best[2], best[3]


ARB = "arbitrary"
PAR = "parallel"


def _rms_fwd(x, g):
    r = lax.rsqrt(jnp.mean(x * x, axis=-1, keepdims=True) + EPS)
    return x * r * g


def _rms_bwd(dy, x, g):
    r = lax.rsqrt(jnp.mean(x * x, axis=-1, keepdims=True) + EPS)
    xh = x * r
    gdy = dy * g
    dx = r * (gdy - xh * jnp.mean(gdy * xh, axis=-1, keepdims=True))
    dg = jnp.sum(dy * xh, axis=0, keepdims=True)
    return dx, dg


_GELU_K = math.sqrt(2.0 / math.pi)
_GELU_C = 0.044715


def _gelu(x):
    t = jnp.tanh(_GELU_K * (x + _GELU_C * x * x * x))
    return 0.5 * x * (1.0 + t)


def _gelu_and_grad(x):
    x2 = x * x
    u = 0.5 * jnp.tanh(x * (_GELU_K + (_GELU_K * _GELU_C) * x2)) + 0.5
    dz2 = (6.0 * _GELU_K * _GELU_C) * x2 + 2.0 * _GELU_K
    return x * u, u * ((x * (1.0 - u)) * dz2 + 1.0)


def _shift_up(x, k, row):
    n = x.shape[0]
    return jnp.where(row < n - k, pltpu.roll(x, n - k, axis=0), 0.0)


def _shift_down_edge(x, k):
    r = pltpu.roll(x, k, axis=0)
    row = lax.broadcasted_iota(jnp.int32, (SUBLANE, x.shape[1]), 0)
    return jnp.concatenate([jnp.where(row >= k, r[:SUBLANE], 0.0), r[SUBLANE:]], axis=0)


def _shift_up_edge(x, k):
    n = x.shape[0]
    r = pltpu.roll(x, n - k, axis=0)
    row = lax.broadcasted_iota(jnp.int32, (SUBLANE, x.shape[1]), 0)
    return jnp.concatenate([r[:n - SUBLANE], jnp.where(row < SUBLANE - k, r[n - SUBLANE:], 0.0)], axis=0)


def _dot(a, b):
    return jnp.dot(a, b, preferred_element_type=F32)


def _dot_nt(a, b):
    return lax.dot_general(a, b, (((1,), (1,)), ((), ())), preferred_element_type=F32)


def _dot_tn(a, b):
    return lax.dot_general(a, b, (((0,), (0,)), ((), ())), preferred_element_type=F32)


MXU_FLOPS_PER_US = 8.0e8
HOST_US = {"lru_fwd": 44.0, "lru_bwd": 94.0, "swa_fwd": 55.0, "swa_bwd": 90.0, "mem_attn_fwd": 19.0,
           "mem_attn_bwd": 27.0, "ffn_act": 75.0, "ffn_act_bwd": 75.0, "resid": 22.0, "resid_bwd": 33.0}
HOST_FILL = 1.0


def _hosted_call(body, *, grid, in_specs, out_specs, out_shape, args, name, aliases=None, scratch_shapes=(),
                 q=None, flops=0.0, budget_us=0.0):
    chunks = q.take(HOST_FILL * (flops / MXU_FLOPS_PER_US + budget_us)) if q is not None else []
    if not chunks:
        return pl.pallas_call(
            body, grid=grid, in_specs=in_specs, out_specs=out_specs, out_shape=out_shape,
            scratch_shapes=list(scratch_shapes), input_output_aliases=aliases or {}, name=name,
            compiler_params=_cp((ARB,) * len(grid)))(*args)
    single = not isinstance(out_shape, (list, tuple))
    o_shapes = [out_shape] if single else list(out_shape)
    o_specs = [out_specs] if single else list(out_specs)
    n_in, n_out, n_scr = len(args), len(o_shapes), len(scratch_shapes)
    c_ins = [a for ch in chunks for a in ch.ins]
    c_outs = [s for ch in chunks for s in ch.out_shapes]
    alias = dict(aliases or {})
    in_off, out_off, sem_off = [], [], []
    i0 = o0 = s0 = 0
    for ch in chunks:
        in_off.append(i0)
        out_off.append(o0)
        sem_off.append(s0)
        for ci, co in ch.alias.items():
            alias[n_in + i0 + ci] = n_out + o0 + co
        i0 += len(ch.ins)
        o0 += len(ch.out_shapes)
        s0 += ch.n_sem

    def wrapped(*refs):
        ins = refs[:n_in]
        cin = refs[n_in:n_in + i0]
        outs = refs[n_in + i0:n_in + i0 + n_out]
        cout = refs[n_in + i0 + n_out:n_in + i0 + n_out + o0]
        scr = refs[n_in + i0 + n_out + o0:n_in + i0 + n_out + o0 + n_scr]
        send_sems, recv_sems = refs[n_in + i0 + n_out + o0 + n_scr:]
        first = functools.reduce(lambda u, v: u & v, [pl.program_id(d) == 0 for d in range(len(grid))])
        last = functools.reduce(lambda u, v: u & v, [pl.program_id(d) == grid[d] - 1 for d in range(len(grid))])

        def each(phase):
            for ch, a, b, s in zip(chunks, in_off, out_off, sem_off):
                getattr(ch, phase)(cin[a:a + len(ch.ins)], cout[b:b + len(ch.out_shapes)], send_sems, recv_sems, s)

        pl.when(first)(lambda: each("start"))
        body(*ins, *outs, *scr)
        pl.when(last)(lambda: each("finish"))

    hbm = pl.BlockSpec(memory_space=pl.ANY)
    res = pl.pallas_call(
        wrapped, grid=grid, in_specs=list(in_specs) + [hbm] * i0, out_specs=o_specs + [hbm] * o0,
        out_shape=o_shapes + c_outs,
        scratch_shapes=list(scratch_shapes) + [pltpu.SemaphoreType.DMA((s0,)), pltpu.SemaphoreType.DMA((s0,))],
        input_output_aliases=alias, name=name,
        compiler_params=pltpu.CompilerParams(dimension_semantics=(ARB,) * len(grid), vmem_limit_bytes=VMEM_LIMIT_V7X,
                                             has_side_effects=True))(*args, *c_ins)
    for ch, b in zip(chunks, out_off):
        ch.done(list(res[n_out + b:n_out + b + len(ch.out_shapes)]))
    return res[0] if single else list(res[:n_out])


def _mm_nn(a, b, *, name, q=None, out_dtype=F32):
    m, k = a.shape
    n = b.shape[-1]
    tm, tn = _mm_tiles(m, k, n, jnp.dtype(out_dtype).itemsize)

    def body(a_ref, b_ref, o_ref):
        o_ref[...] = _dot(a_ref[...], b_ref[...]).astype(o_ref.dtype)

    return _hosted_call(
        body, grid=(m // tm, n // tn),
        in_specs=[pl.BlockSpec((tm, k), lambda i, j: (i, 0)),
                  pl.BlockSpec((None, k, tn), lambda i, j: (0, 0, j))],
        out_specs=pl.BlockSpec((tm, tn), lambda i, j: (i, j)),
        out_shape=jax.ShapeDtypeStruct((m, n), out_dtype),
        args=(a, b), name=name, q=q, flops=2.0 * m * k * n)


def _mm_nt(a, b, *, name, q=None, out_dtype=F32):
    m, k = a.shape
    n = b.shape[-2]
    tm, tn = _mm_tiles(m, k, n, jnp.dtype(out_dtype).itemsize)

    def body(a_ref, b_ref, o_ref):
        o_ref[...] = _dot_nt(a_ref[...], b_ref[...]).astype(o_ref.dtype)

    return _hosted_call(
        body, grid=(m // tm, n // tn),
        in_specs=[pl.BlockSpec((tm, k), lambda i, j: (i, 0)),
                  pl.BlockSpec((None, tn, k), lambda i, j: (0, j, 0))],
        out_specs=pl.BlockSpec((tm, tn), lambda i, j: (i, j)),
        out_shape=jax.ShapeDtypeStruct((m, n), out_dtype),
        args=(a, b), name=name, q=q, flops=2.0 * m * k * n)


def _mm_nn_slots(a, b4, *, name, q=None, out_dtype=F32):
    m, k = a.shape
    s_, _, _, c = b4.shape
    ob = jnp.dtype(out_dtype).itemsize
    tm = max(t for t in _divisors(m, 256) if 2 * (t * k * 2 + k * c * 2 + t * c * ob) <= MM_VMEM_BUDGET)

    def body(a_ref, b_ref, o_ref):
        o_ref[...] = _dot(a_ref[...], b_ref[...]).astype(o_ref.dtype)

    return _hosted_call(
        body, grid=(m // tm, s_),
        in_specs=[pl.BlockSpec((tm, k), lambda i, j: (i, 0)),
                  pl.BlockSpec((None, None, k, c), lambda i, j: (j, 0, 0, 0))],
        out_specs=pl.BlockSpec((tm, c), lambda i, j: (i, j)),
        out_shape=jax.ShapeDtypeStruct((m, s_ * c), out_dtype),
        args=(a, b4), name=name, q=q, flops=2.0 * m * k * s_ * c)


def _mm_tn_slots(a, b, *, name, slot_cols, n_slots, first_slot=0, q=None, out=None):
    k, m = a.shape
    c = slot_cols
    tk, tm, _ = _mm_tn_tiles(k, m, c, whole_n=True)

    def body(a_ref, b_ref, *rest):
        o_ref = rest[-1]
        part = _dot_tn(a_ref[...], b_ref[...])

        @pl.when(pl.program_id(2) == 0)
        def _():
            o_ref[...] = part

        @pl.when(pl.program_id(2) > 0)
        def _():
            o_ref[...] += part

    in_specs = [pl.BlockSpec((tk, tm), lambda i, j, s: (s, i)), pl.BlockSpec((tk, c), lambda i, j, s: (s, j))]
    args = (a, b)
    if out is not None:
        in_specs.append(pl.BlockSpec(memory_space=pl.ANY))
        args = (a, b, out)
    return _hosted_call(
        body, grid=(m // tm, b.shape[-1] // c, k // tk), in_specs=in_specs,
        out_specs=pl.BlockSpec((None, None, tm, c), lambda i, j, s: (first_slot + j, 0, i, 0)),
        out_shape=jax.ShapeDtypeStruct((n_slots, 1, m, c), F32),
        aliases={2: 0} if out is not None else None,
        args=args, name=name, q=q, flops=2.0 * m * k * b.shape[-1])


def _mm_tn(a, b, *, name, q=None):
    k, m = a.shape
    n = b.shape[-1]
    tk, tm, tn = _mm_tn_tiles(k, m, n)

    def body(a_ref, b_ref, o_ref):
        part = _dot_tn(a_ref[...], b_ref[...])

        @pl.when(pl.program_id(2) == 0)
        def _():
            o_ref[...] = part

        @pl.when(pl.program_id(2) > 0)
        def _():
            o_ref[...] += part

    return _hosted_call(
        body, grid=(m // tm, n // tn, k // tk),
        in_specs=[pl.BlockSpec((tk, tm), lambda i, j, s: (s, i)), pl.BlockSpec((tk, tn), lambda i, j, s: (s, j))],
        out_specs=pl.BlockSpec((None, tm, tn), lambda i, j, s: (0, i, j)),
        out_shape=jax.ShapeDtypeStruct((1, m, n), F32),
        args=(a, b), name=name, q=q, flops=2.0 * m * k * n)


def _mm_ffn_dh(dg, dv, w4, *, name, q=None):
    m, f = dg.shape
    n_slots, _, d, c = w4.shape
    tm, tn = _mm_tiles(m, 2 * f, d, 4)

    def body(dg_ref, dv_ref, *rest):
        w_refs, o_ref = rest[:n_slots], rest[n_slots]
        acc = None
        for s, w_ref in enumerate(w_refs):
            x_ref = dg_ref if s < n_slots // 2 else dv_ref
            off = (s % (n_slots // 2)) * c
            part = _dot_nt(x_ref[:, off:off + c], w_ref[...])
            acc = part if acc is None else acc + part
        o_ref[...] = acc.astype(o_ref.dtype)

    wspec = lambda s: pl.BlockSpec((None, None, tn, c), lambda i, j: (s, 0, j, 0))
    return _hosted_call(
        body, grid=(m // tm, d // tn),
        in_specs=[pl.BlockSpec((tm, f), lambda i, j: (i, 0)),
                  pl.BlockSpec((tm, f), lambda i, j: (i, 0))] + [wspec(s) for s in range(n_slots)],
        out_specs=pl.BlockSpec((tm, tn), lambda i, j: (i, j)),
        out_shape=jax.ShapeDtypeStruct((m, d), MXU),
        args=(dg, dv) + (w4,) * n_slots, name=name, q=q, flops=4.0 * m * f * d)


ROW_TILE = 512


def _norm_fwd(x, g, *, name):
    n, d = x.shape
    tm = _tile(n, ROW_TILE, SUBLANE)

    def body(x_ref, g_ref, o_ref):
        o_ref[...] = _rms_fwd(x_ref[...], g_ref[...]).astype(o_ref.dtype)

    return pl.pallas_call(
        body, grid=(n // tm,),
        in_specs=[pl.BlockSpec((tm, d), lambda i: (i, 0)), pl.BlockSpec((1, d), lambda i: (0, 0))],
        out_specs=pl.BlockSpec((tm, d), lambda i: (i, 0)),
        out_shape=jax.ShapeDtypeStruct((n, d), MXU),
        name=name, compiler_params=_cp((PAR,)))(x, g)


def _norm_bwd_dg(dy, x, g, *, name):
    n, d = x.shape
    tm = _tile(n, ROW_TILE, SUBLANE)

    def body(dy_ref, x_ref, g_ref, dg_ref):
        @pl.when(pl.program_id(0) == 0)
        def _():
            dg_ref[...] = jnp.zeros_like(dg_ref)
        _, dg = _rms_bwd(dy_ref[...], x_ref[...], g_ref[...])
        dg_ref[...] += dg

    return pl.pallas_call(
        body, grid=(n // tm,),
        in_specs=[pl.BlockSpec((tm, d), lambda i: (i, 0)), pl.BlockSpec((tm, d), lambda i: (i, 0)),
                  pl.BlockSpec((1, d), lambda i: (0, 0))],
        out_specs=pl.BlockSpec((1, d), lambda i: (0, 0)),
        out_shape=jax.ShapeDtypeStruct((1, d), F32),
        name=name, compiler_params=_cp((ARB,)))(dy, x, g)


def _resid_norm_fwd(x, y, g_post, g_pres, *, name, q=None):
    n, d = x.shape
    tm = _tile(n, ROW_TILE, SUBLANE)
    nh = len(g_pres)

    def body(x_ref, y_ref, gp_ref, *rest):
        gpre = rest[:nh]
        xo_ref = rest[nh]
        h_refs = rest[nh + 1:]
        xo = x_ref[...] + _rms_fwd(y_ref[...].astype(F32), gp_ref[...])
        xo_ref[...] = xo
        for g_ref, h_ref in zip(gpre, h_refs):
            h_ref[...] = _rms_fwd(xo, g_ref[...]).astype(h_ref.dtype)

    row = pl.BlockSpec((tm, d), lambda i: (i, 0))
    vec = pl.BlockSpec((1, d), lambda i: (0, 0))
    outs = _hosted_call(
        body, grid=(n // tm,),
        in_specs=[row, row, vec] + [vec] * nh,
        out_specs=[row] + [row] * nh,
        out_shape=[jax.ShapeDtypeStruct((n, d), F32)] + [jax.ShapeDtypeStruct((n, d), MXU)] * nh,
        args=(x, y, g_post, *g_pres), name=name, q=q, budget_us=HOST_US["resid"])
    return outs[0], list(outs[1:])


def _loss_fwd(x, y, g_post, target, *, name):
    n, d = x.shape
    tm = _tile(n, ROW_TILE, SUBLANE)

    def body(x_ref, y_ref, gp_ref, t_ref, dx_ref, sq_ref, dy_ref, dg_ref):
        @pl.when(pl.program_id(0) == 0)
        def _():
            sq_ref[...] = jnp.zeros_like(sq_ref)
            dg_ref[...] = jnp.zeros_like(dg_ref)
        y = y_ref[...].astype(F32)
        err = x_ref[...] + _rms_fwd(y, gp_ref[...]) - t_ref[...]
        g = err * (1.0 / d)
        dx_ref[...] = g
        sq_ref[...] += jnp.sum(err * err, axis=0, keepdims=True)
        dy, dg = _rms_bwd(g, y, gp_ref[...])
        dy_ref[...] = dy.astype(dy_ref.dtype)
        dg_ref[...] += dg

    row = pl.BlockSpec((tm, d), lambda i: (i, 0))
    vec = pl.BlockSpec((1, d), lambda i: (0, 0))
    return pl.pallas_call(
        body, grid=(n // tm,),
        in_specs=[row, row, vec, row],
        out_specs=[row, vec, row, vec],
        out_shape=[jax.ShapeDtypeStruct((n, d), F32), jax.ShapeDtypeStruct((1, d), F32),
                   jax.ShapeDtypeStruct((n, d), MXU), jax.ShapeDtypeStruct((1, d), F32)],
        name=name, compiler_params=_cp((ARB,)))(x, y, g_post, target)


def _resid_norm_bwd(dx_out, dhs, x_out, g_pres, y, g_post, *, name, q=None):
    n, d = dx_out.shape
    tm = _tile(n, ROW_TILE, SUBLANE)
    nh = len(dhs)
    has_y = y is not None

    def body(*refs):
        it = iter(refs)
        dxo_ref = next(it)
        dh_refs = [next(it) for _ in range(nh)]
        xo_ref = next(it) if nh else None
        gpre_refs = [next(it) for _ in range(nh)]
        y_ref = next(it) if has_y else None
        gpost_ref = next(it) if has_y else None
        g_out = next(it)
        dy_out = next(it) if has_y else None
        dgpre_out = [next(it) for _ in range(nh)]
        dgpost_out = next(it) if has_y else None

        @pl.when(pl.program_id(0) == 0)
        def _():
            for r in dgpre_out:
                r[...] = jnp.zeros_like(r)
            if has_y:
                dgpost_out[...] = jnp.zeros_like(dgpost_out)

        g = dxo_ref[...]
        if nh:
            xo = xo_ref[...]
            for dh_ref, gp_ref, dg_ref in zip(dh_refs, gpre_refs, dgpre_out):
                dx, dg = _rms_bwd(dh_ref[...].astype(F32), xo, gp_ref[...])
                g = g + dx
                dg_ref[...] += dg
        g_out[...] = g
        if has_y:
            dy, dg = _rms_bwd(g, y_ref[...].astype(F32), gpost_ref[...])
            dy_out[...] = dy.astype(dy_out.dtype)
            dgpost_out[...] += dg

    row = pl.BlockSpec((tm, d), lambda i: (i, 0))
    vec = pl.BlockSpec((1, d), lambda i: (0, 0))
    ins, in_specs = [dx_out], [row]
    ins += list(dhs)
    in_specs += [row] * nh
    if nh:
        ins.append(x_out)
        in_specs.append(row)
    ins += list(g_pres)
    in_specs += [vec] * nh
    if has_y:
        ins += [y, g_post]
        in_specs += [row, vec]
    out_specs, out_shape = [row], [jax.ShapeDtypeStruct((n, d), F32)]
    if has_y:
        out_specs.append(row)
        out_shape.append(jax.ShapeDtypeStruct((n, d), MXU))
    out_specs += [vec] * nh
    out_shape += [jax.ShapeDtypeStruct((1, d), F32)] * nh
    if has_y:
        out_specs.append(vec)
        out_shape.append(jax.ShapeDtypeStruct((1, d), F32))
    outs = list(_hosted_call(
        body, grid=(n // tm,), in_specs=in_specs, out_specs=out_specs, out_shape=out_shape,
        args=tuple(ins), name=name, q=q, budget_us=HOST_US["resid_bwd"]))
    g = outs.pop(0)
    dy = outs.pop(0) if has_y else None
    dgpre = [outs.pop(0) for _ in range(nh)]
    dgpost = outs.pop(0) if has_y else None
    return g, dy, dgpre, dgpost


def _ffn_conv(up, w_ref, b_ref):
    return (w_ref[0:1, :] * _shift_down_edge(up, 2) + w_ref[1:2, :] * _shift_down_edge(up, 1)
            + w_ref[2:3, :] * up + b_ref[...])


def _ffn_act_fwd(up, wconv, bconv, bsz, *, name, q=None):
    n, f2 = up.shape
    f = f2 // 2
    t = n // bsz
    tc = _tile(f, 256)
    nf = f // tc

    def body(ug_ref, uv_ref, wg_ref, wv_ref, bg_ref, bv_ref, o_ref, dag_ref, dav_ref):
        g = _ffn_conv(ug_ref[...].astype(F32), wg_ref, bg_ref)
        v = _ffn_conv(uv_ref[...].astype(F32), wv_ref, bv_ref)
        gl, dgl = _gelu_and_grad(g)
        dag_ref[...] = (v * dgl).astype(dag_ref.dtype)
        dav_ref[...] = gl.astype(dav_ref.dtype)
        o_ref[...] = (gl * v).astype(o_ref.dtype)

    blk = pl.BlockSpec((t, tc), lambda b, j: (b, j))
    return _hosted_call(
        body, grid=(bsz, nf),
        in_specs=[blk, pl.BlockSpec((t, tc), lambda b, j: (b, j + nf)),
                  pl.BlockSpec((FFN_CONV, tc), lambda b, j: (0, j)),
                  pl.BlockSpec((FFN_CONV, tc), lambda b, j: (0, j + nf)),
                  pl.BlockSpec((1, tc), lambda b, j: (0, j)),
                  pl.BlockSpec((1, tc), lambda b, j: (0, j + nf))],
        out_specs=[blk, blk, blk],
        out_shape=[jax.ShapeDtypeStruct((n, f), MXU)] * 3,
        args=(up, up, wconv, wconv, bconv, bconv), name=name, q=q, budget_us=HOST_US["ffn_act"])


def _ffn_act_bwd(up, ug, uv, dact, wconv, bsz, *, name, q=None):
    n, f2 = up.shape
    f = f2 // 2
    t = n // bsz
    tc = _tile(f, 256)
    nf = f // tc

    def body(xg_ref, xv_ref, g_ref, v_ref, da_ref, wg_ref, wv_ref,
             dug_ref, duv_ref, dwg_ref, dwv_ref, dbg_ref, dbv_ref):
        @pl.when(pl.program_id(1) == 0)
        def _():
            for r in (dwg_ref, dwv_ref, dbg_ref, dbv_ref):
                r[...] = jnp.zeros_like(r)

        da = da_ref[...].astype(F32)
        dg = da * g_ref[...].astype(F32)
        dv = da * v_ref[...].astype(F32)

        def conv_bwd(du, w_ref, x_ref, dx_ref, dw_ref, db_ref):
            du1, du2 = _shift_up_edge(du, 1), _shift_up_edge(du, 2)
            dx_ref[...] = (w_ref[2:3, :] * du + w_ref[1:2, :] * du1 + w_ref[0:1, :] * du2).astype(dx_ref.dtype)
            x = x_ref[...].astype(F32)
            dw_ref[0:1, :] += jnp.sum(x * du2, axis=0, keepdims=True)
            dw_ref[1:2, :] += jnp.sum(x * du1, axis=0, keepdims=True)
            dw_ref[2:3, :] += jnp.sum(x * du, axis=0, keepdims=True)
            db_ref[...] += jnp.sum(du, axis=0, keepdims=True)

        conv_bwd(dg, wg_ref, xg_ref, dug_ref, dwg_ref, dbg_ref)
        conv_bwd(dv, wv_ref, xv_ref, duv_ref, dwv_ref, dbv_ref)

    blk = pl.BlockSpec((t, tc), lambda j, b: (b, j))
    wspec = pl.BlockSpec((FFN_CONV, tc), lambda j, b: (0, j))
    bspec = pl.BlockSpec((1, tc), lambda j, b: (0, j))
    outs = _hosted_call(
        body, grid=(nf, bsz),
        in_specs=[blk, pl.BlockSpec((t, tc), lambda j, b: (b, j + nf)), blk, blk, blk,
                  wspec, pl.BlockSpec((FFN_CONV, tc), lambda j, b: (0, j + nf))],
        out_specs=[blk, blk, wspec, wspec, bspec, bspec],
        out_shape=[jax.ShapeDtypeStruct((n, f), MXU), jax.ShapeDtypeStruct((n, f), MXU),
                   jax.ShapeDtypeStruct((FFN_CONV, f), F32), jax.ShapeDtypeStruct((FFN_CONV, f), F32),
                   jax.ShapeDtypeStruct((1, f), F32), jax.ShapeDtypeStruct((1, f), F32)],
        args=(up, up, ug, uv, dact, wconv, wconv), name=name, q=q, budget_us=HOST_US["ffn_act_bwd"])
    dug, duv, dwg, dwv, dbg, dbv = outs
    return dug, duv, jnp.concatenate([dwg, dwv], axis=1), jnp.concatenate([dbg, dbv], axis=1)


def _softmax(s):
    p = jnp.exp(s - jnp.max(s, axis=-1, keepdims=True))
    return p / jnp.sum(p, axis=-1, keepdims=True)


def _mem_attn_fwd(proj, q_col_block, mkv, ycat, bsz, *, name, q=None):
    n = proj.shape[0]
    t = n // bsz
    tq = _tile(t, 512, SUBLANE)
    nt = t // tq
    scale = HEAD_DIM ** -0.5

    def body(q_ref, kv_ref, old_ref, o_ref):
        del old_ref
        heads = range(MEM_HEADS)
        col = lambda ref, h, off=0: ref[:, off + h * HEAD_DIM:off + (h + 1) * HEAD_DIM].astype(MXU)
        ss = [_dot_nt(col(q_ref, h), col(kv_ref, h)) * scale for h in heads]
        ps = [_softmax(s).astype(MXU) for s in ss]
        outs = [_dot(ps[h], col(kv_ref, h, MEM_WIDTH)) for h in heads]
        o_ref[...] = jnp.concatenate(outs, axis=-1).astype(o_ref.dtype)

    return _hosted_call(
        body, grid=(bsz, nt),
        in_specs=[pl.BlockSpec((tq, MEM_WIDTH), lambda b, i: (b * nt + i, q_col_block)),
                  pl.BlockSpec((MEM_LEN, 2 * MEM_WIDTH), lambda b, i: (b, 0)),
                  pl.BlockSpec(memory_space=pl.ANY)],
        out_specs=pl.BlockSpec((tq, MEM_WIDTH), lambda b, i: (b * nt + i, MIX_WIDTH // MEM_WIDTH)),
        out_shape=jax.ShapeDtypeStruct(ycat.shape, ycat.dtype),
        aliases={2: 0}, args=(proj, mkv, ycat), name=name, q=q, budget_us=HOST_US["mem_attn_fwd"])


def _mem_attn_bwd(proj, q_col_block, mkv, dycat, dproj, bsz, *, name, q=None):
    n = proj.shape[0]
    t = n // bsz
    tq = _tile(t, 512, SUBLANE)
    nt = t // tq
    scale = HEAD_DIM ** -0.5

    def body(q_ref, kv_ref, do_ref, old_ref, dq_ref, dkv_ref):
        del old_ref

        @pl.when(pl.program_id(1) == 0)
        def _():
            dkv_ref[...] = jnp.zeros_like(dkv_ref)

        heads = range(MEM_HEADS)
        col = lambda ref, h, off=0: ref[:, off + h * HEAD_DIM:off + (h + 1) * HEAD_DIM].astype(MXU)
        qs = [col(q_ref, h) for h in heads]
        ks = [col(kv_ref, h) for h in heads]
        dos = [col(do_ref, h) for h in heads]
        ps = [_softmax(_dot_nt(qs[h], ks[h]) * scale) for h in heads]
        dps = [_dot_nt(dos[h], col(kv_ref, h, MEM_WIDTH)) for h in heads]
        dss = [(ps[h] * (dps[h] - jnp.sum(dps[h] * ps[h], axis=-1, keepdims=True)) * scale).astype(MXU) for h in heads]
        dvs = [_dot_tn(ps[h].astype(MXU), dos[h]) for h in heads]
        dqs = [_dot(dss[h], ks[h]) for h in heads]
        dks = [_dot_tn(dss[h], qs[h]) for h in heads]
        dq_ref[...] = jnp.concatenate(dqs, axis=-1).astype(dq_ref.dtype)
        dkv_ref[...] += jnp.concatenate(dks + dvs, axis=-1)

    return _hosted_call(
        body, grid=(bsz, nt),
        in_specs=[pl.BlockSpec((tq, MEM_WIDTH), lambda b, i: (b * nt + i, q_col_block)),
                  pl.BlockSpec((MEM_LEN, 2 * MEM_WIDTH), lambda b, i: (b, 0)),
                  pl.BlockSpec((tq, MEM_WIDTH), lambda b, i: (b * nt + i, MIX_WIDTH // MEM_WIDTH)),
                  pl.BlockSpec(memory_space=pl.ANY)],
        out_specs=[pl.BlockSpec((tq, MEM_WIDTH), lambda b, i: (b * nt + i, q_col_block)),
                   pl.BlockSpec((MEM_LEN, 2 * MEM_WIDTH), lambda b, i: (b, 0))],
        out_shape=[jax.ShapeDtypeStruct(dproj.shape, dproj.dtype),
                   jax.ShapeDtypeStruct((bsz * MEM_LEN, 2 * MEM_WIDTH), F32)],
        aliases={3: 0}, args=(proj, mkv, dycat, dproj), name=name, q=q, budget_us=HOST_US["mem_attn_bwd"])


def _swa_probs(s, h, dist, mask, sink):
    s = jnp.where(mask, s * (HEAD_DIM ** -0.5) - SLOPES[h] * dist, -jnp.inf)
    m = jnp.maximum(jnp.max(s, axis=-1, keepdims=True), sink)
    p = jnp.exp(s - m)
    psink = jnp.exp(sink - m)
    inv = 1.0 / (jnp.sum(p, axis=-1, keepdims=True) + psink)
    return p * inv, psink * inv


def _swa_mask(n):
    qi = lax.broadcasted_iota(jnp.int32, (WINDOW, 2 * WINDOW), 0) + WINDOW
    ki = lax.broadcasted_iota(jnp.int32, (WINDOW, 2 * WINDOW), 1)
    dist = qi - ki
    mask = (dist >= 0) & (dist < WINDOW) & ((n > 0) | (ki >= WINDOW))
    return dist.astype(F32), mask


def _swa_fwd(proj, kv, sinks, bsz, *, name, q=None):
    n_tok = proj.shape[0]
    nb = n_tok // bsz // WINDOW
    kvw = SWA_KV_HEADS * HEAD_DIM

    def body(sink_ref, q_ref, kvp_ref, kvc_ref, o_ref):
        n = pl.program_id(1)
        dist, mask = _swa_mask(n)
        kk = jnp.concatenate([kvp_ref[:, :kvw], kvc_ref[:, :kvw]], axis=0).astype(MXU)
        vv = jnp.concatenate([kvp_ref[:, kvw:], kvc_ref[:, kvw:]], axis=0).astype(MXU)
        heads = range(SWA_HEADS)
        group = lambda x, h: x[:, (h // SWA_GROUP) * HEAD_DIM:(h // SWA_GROUP + 1) * HEAD_DIM]
        ss = [_dot_nt(q_ref[:, h * HEAD_DIM:(h + 1) * HEAD_DIM].astype(MXU), group(kk, h)) for h in heads]
        ps = [_swa_probs(ss[h], h, dist, mask, sink_ref[h])[0].astype(MXU) for h in heads]
        outs = [_dot(ps[h], group(vv, h)) for h in heads]
        o_ref[...] = jnp.concatenate(outs, axis=-1).astype(o_ref.dtype)

    return _hosted_call(
        body, grid=(bsz, nb),
        in_specs=[pl.BlockSpec(memory_space=pltpu.SMEM),
                  pl.BlockSpec((WINDOW, MIX_WIDTH), lambda b, n: (b * nb + n, 0)),
                  pl.BlockSpec((WINDOW, 2 * kvw), lambda b, n: (b * nb + jnp.maximum(n - 1, 0), 0)),
                  pl.BlockSpec((WINDOW, 2 * kvw), lambda b, n: (b * nb + n, 0))],
        out_specs=pl.BlockSpec((WINDOW, MIX_WIDTH), lambda b, n: (b * nb + n, 0)),
        out_shape=jax.ShapeDtypeStruct((n_tok, D_MODEL), MXU),
        args=(sinks, proj, kv, kv), name=name, q=q, budget_us=HOST_US["swa_fwd"])


def _swa_bwd(proj, kv, sinks, dycat, bsz, *, name, q=None):
    n_tok = proj.shape[0]
    nb = n_tok // bsz // WINDOW
    kvw = SWA_KV_HEADS * HEAD_DIM

    def body(sink_ref, q_ref, kvp_ref, kvc_ref, do_ref, dq_ref, dkvc_ref, dkvp_ref, dsink_ref):
        n = pl.program_id(1)

        @pl.when((pl.program_id(0) == 0) & (n == 0))
        def _():
            dsink_ref[...] = jnp.zeros_like(dsink_ref)

        dist, mask = _swa_mask(n)
        kk = jnp.concatenate([kvp_ref[:, :kvw], kvc_ref[:, :kvw]], axis=0).astype(MXU)
        vv = jnp.concatenate([kvp_ref[:, kvw:], kvc_ref[:, kvw:]], axis=0).astype(MXU)
        lane = lax.broadcasted_iota(jnp.int32, (SUBLANE, LANE), 1)
        heads = range(SWA_HEADS)
        group = lambda x, h: x[:, (h // SWA_GROUP) * HEAD_DIM:(h // SWA_GROUP + 1) * HEAD_DIM]
        qs = [q_ref[:, h * HEAD_DIM:(h + 1) * HEAD_DIM].astype(MXU) for h in heads]
        dos = [do_ref[:, h * HEAD_DIM:(h + 1) * HEAD_DIM].astype(MXU) for h in heads]
        ss = [_dot_nt(qs[h], group(kk, h)) for h in heads]
        dps = [_dot_nt(dos[h], group(vv, h)) for h in heads]
        probs = [_swa_probs(ss[h], h, dist, mask, sink_ref[h]) for h in heads]
        rss = [jnp.sum(dps[h] * probs[h][0], axis=-1, keepdims=True) for h in heads]
        dss = [(probs[h][0] * (dps[h] - rss[h]) * (HEAD_DIM ** -0.5)).astype(MXU) for h in heads]
        dqs = [_dot(dss[h], group(kk, h)) for h in heads]
        dk_h = [_dot_tn(dss[h], qs[h]) for h in heads]
        dv_h = [_dot_tn(probs[h][0].astype(MXU), dos[h]) for h in heads]
        dsink = jnp.zeros((SUBLANE, LANE), F32)
        for h in heads:
            dsink = dsink + jnp.where(lane == h, jnp.sum(-probs[h][1] * rss[h], axis=0, keepdims=True), 0.0)
        sum_group = lambda xs, c: functools.reduce(lambda a, b: a + b, xs[c * SWA_GROUP:(c + 1) * SWA_GROUP])
        dks = [sum_group(dk_h, c) for c in range(SWA_KV_HEADS)]
        dvs = [sum_group(dv_h, c) for c in range(SWA_KV_HEADS)]
        dq_ref[...] = jnp.concatenate(dqs, axis=-1).astype(dq_ref.dtype)
        dkv = jnp.concatenate(dks + dvs, axis=-1)
        dkvp_ref[...] = dkv[:WINDOW]
        dkvc_ref[...] = dkv[WINDOW:]
        dsink_ref[...] += dsink

    qspec = pl.BlockSpec((WINDOW, MIX_WIDTH), lambda b, n: (b * nb + n, 0))
    kvspec = pl.BlockSpec((WINDOW, 2 * kvw), lambda b, n: (b * nb + n, 0))
    return _hosted_call(
        body, grid=(bsz, nb),
        in_specs=[pl.BlockSpec(memory_space=pltpu.SMEM), qspec,
                  pl.BlockSpec((WINDOW, 2 * kvw), lambda b, n: (b * nb + jnp.maximum(n - 1, 0), 0)),
                  kvspec, qspec],
        out_specs=[qspec, kvspec, kvspec, pl.BlockSpec((SUBLANE, LANE), lambda b, n: (0, 0))],
        out_shape=[jax.ShapeDtypeStruct((n_tok, D_MODEL), MXU),
                   jax.ShapeDtypeStruct((n_tok, 2 * kvw), F32),
                   jax.ShapeDtypeStruct((n_tok, 2 * kvw), F32),
                   jax.ShapeDtypeStruct((SUBLANE, LANE), F32)],
        args=(sinks, proj, kv, kv, dycat), name=name, q=q, budget_us=HOST_US["swa_bwd"])


def _swa_dkv_combine(curs, prevs, bsz, *, name):
    n_tok, w = curs[0].shape
    nb = n_tok // bsz // WINDOW
    k = len(curs)

    def body(*refs):
        o_ref = refs[-1]
        n = pl.program_id(1)
        acc = refs[0][...]
        for r in refs[1:k]:
            acc = acc + r[...]
        nxt = refs[k][...]
        for r in refs[k + 1:2 * k]:
            nxt = nxt + r[...]
        o_ref[...] = (acc + jnp.where(n < nb - 1, nxt, 0.0)).astype(o_ref.dtype)

    cur = pl.BlockSpec((WINDOW, w), lambda b, n: (b * nb + n, 0))
    prv = pl.BlockSpec((WINDOW, w), lambda b, n: (b * nb + jnp.minimum(n + 1, nb - 1), 0))
    return pl.pallas_call(
        body, grid=(bsz, nb), in_specs=[cur] * k + [prv] * k, out_specs=cur,
        out_shape=jax.ShapeDtypeStruct((n_tok, w), MXU),
        name=name, compiler_params=_cp((PAR, PAR)))(*curs, *prevs)


def _lru_gates(ux, halo, ext_ref, wc_ref, bc_ref, wr_ref, br_ref, wi_ref, bi_ref, lam_ref):
    tt = ux.shape[0]
    ext_ref[0:SUBLANE, :] = halo
    ext_ref[SUBLANE:, :] = ux
    xs = [ux] + [ext_ref[pl.ds(SUBLANE - k, tt), :] for k in range(1, LRU_CONV)]
    xc = bc_ref[...] + wc_ref[3:4, :] * xs[0] + wc_ref[2:3, :] * xs[1] + wc_ref[1:2, :] * xs[2] + wc_ref[0:1, :] * xs[3]
    pre_r, pre_i = [], []
    for blk in range(MIX_WIDTH // GATE_TILE):
        xb = xc[:, blk * GATE_TILE:(blk + 1) * GATE_TILE].astype(MXU)
        pre_r.append(_dot(xb, wr_ref[blk]))
        pre_i.append(_dot(xb, wi_ref[blk]))
    r = jax.nn.sigmoid(jnp.concatenate(pre_r, axis=-1) + br_ref[...])
    i = jax.nn.sigmoid(jnp.concatenate(pre_i, axis=-1) + bi_ref[...])
    nlam = -lam_ref[...]
    sp = jnp.maximum(nlam, 0.0) + jnp.log(1.0 + jnp.exp(-jnp.abs(nlam)))
    log_a = -LRU_C * r * sp
    a = jnp.exp(log_a)
    om = -jnp.tanh(log_a) * (a * a + 1.0)
    s = jnp.sqrt(om)
    return xs, xc, r, i, sp, a, s


def _lru_fwd(proj, wconv, bconv, wr, br, wi, bi, lam, bsz, *, name, q=None):
    n_tok = proj.shape[0]
    t = n_tok // bsz
    tt = _tile(t, 256, SUBLANE)
    nt = t // tt
    w = MIX_WIDTH
    ng = tt // SUBLANE

    def body(pg_ref, halo_ref, wc_ref, bc_ref, wr_ref, br_ref, wi_ref, bi_ref, lam_ref,
             y_ref, h_ref, ext_ref, a_ref, b_ref, carry_ref):
        ti = pl.program_id(1)

        @pl.when(ti == 0)
        def _():
            carry_ref[...] = jnp.zeros_like(carry_ref)

        gate = pg_ref[:, :w]
        ux = pg_ref[:, w:]
        halo = jnp.where(ti > 0, halo_ref[...], 0.0)
        _, xc, _, i, _, a, s = _lru_gates(ux, halo, ext_ref, wc_ref, bc_ref, wr_ref, br_ref, wi_ref, bi_ref, lam_ref)
        a_ref[...] = a
        b_ref[...] = s * (i * xc)
        row = lax.broadcasted_iota(jnp.int32, (SUBLANE, w), 0)

        def group(g, hprev):
            off = pl.multiple_of(g * SUBLANE, SUBLANE)
            ca = a_ref[pl.ds(off, SUBLANE), :]
            cb = b_ref[pl.ds(off, SUBLANE), :]
            for d in (1, 2, 4):
                a_sh = jnp.where(row >= d, pltpu.roll(ca, d, axis=0), 1.0)
                b_sh = jnp.where(row >= d, pltpu.roll(cb, d, axis=0), 0.0)
                cb = ca * b_sh + cb
                ca = ca * a_sh
            h = ca * hprev + cb
            b_ref[pl.ds(off, SUBLANE), :] = h
            return jnp.broadcast_to(h[SUBLANE - 1:SUBLANE, :], (SUBLANE, w))

        carry_ref[...] = lax.fori_loop(0, ng, group, carry_ref[...])
        h = b_ref[...]
        h_ref[...] = h
        y_ref[...] = (h * _gelu(gate)).astype(y_ref.dtype)

    vec = lambda r: pl.BlockSpec((r, w), lambda b, i: (0, 0))
    wspec = pl.BlockSpec((w // GATE_TILE, GATE_TILE, GATE_TILE), lambda b, i: (0, 0, 0))
    hb = tt // SUBLANE
    return _hosted_call(
        body, grid=(bsz, nt),
        in_specs=[pl.BlockSpec((tt, 2 * w), lambda b, i: (b * nt + i, 0)),
                  pl.BlockSpec((SUBLANE, w), lambda b, i: (jnp.maximum((b * nt + i) * hb - 1, 0), 1)),
                  vec(LRU_CONV), vec(1), wspec, vec(1), wspec, vec(1), vec(1)],
        out_specs=[pl.BlockSpec((tt, w), lambda b, i: (b * nt + i, 0)),
                   pl.BlockSpec((tt, w), lambda b, i: (b * nt + i, 0))],
        out_shape=[jax.ShapeDtypeStruct((n_tok, D_MODEL), MXU), jax.ShapeDtypeStruct((n_tok, w), F32)],
        scratch_shapes=[pltpu.VMEM((tt + SUBLANE, w), F32), pltpu.VMEM((tt, w), F32),
                        pltpu.VMEM((tt, w), F32), pltpu.VMEM((SUBLANE, w), F32)],
        args=(proj, proj, wconv, bconv, wr, br, wi, bi, lam), name=name, q=q, budget_us=HOST_US["lru_fwd"])


def _lru_bwd(proj, hs, dycat, wconv, bconv, wr, br, wi, bi, lam, bsz, *, name, q=None):
    n_tok = proj.shape[0]
    t = n_tok // bsz
    tt = _tile(t, 256, SUBLANE)
    nt = t // tt
    w = MIX_WIDTH
    ng = tt // SUBLANE
    nblk = w // GATE_TILE

    def body(pg_ref, halo_ref, h_ref, hhalo_ref, dy_ref, wc_ref, bc_ref, wr_ref, br_ref, wi_ref, bi_ref, lam_ref,
             dp_ref, dwc_ref, dbc_ref, dwr_ref, dbr_ref, dwi_ref, dbi_ref, dlam_ref,
             ext_ref, a_ref, c_ref, g_ref, gcarry_ref, xcarry_ref):
        bi_ = pl.program_id(0)
        ti = nt - 1 - pl.program_id(1)

        @pl.when((bi_ == 0) & (pl.program_id(1) == 0))
        def _():
            for r in (dwc_ref, dbc_ref, dwr_ref, dbr_ref, dwi_ref, dbi_ref, dlam_ref):
                r[...] = jnp.zeros_like(r)

        @pl.when(pl.program_id(1) == 0)
        def _():
            gcarry_ref[...] = jnp.zeros_like(gcarry_ref)
            xcarry_ref[...] = jnp.zeros_like(xcarry_ref)

        gate = pg_ref[:, :w]
        ux = pg_ref[:, w:]
        halo = jnp.where(ti > 0, halo_ref[...], 0.0)
        xs, xc, r, i, sp, a, s = _lru_gates(ux, halo, ext_ref, wc_ref, bc_ref, wr_ref, br_ref, wi_ref, bi_ref, lam_ref)
        h = h_ref[...]
        gl, dgl = _gelu_and_grad(gate)
        dy = dy_ref[...].astype(F32)
        dgate = dy * h * dgl
        row_t = lax.broadcasted_iota(jnp.int32, (tt, w), 0)
        g_ref[...] = dy * gl + jnp.where(row_t == tt - 1, gcarry_ref[0:1, :], 0.0)
        c_ref[...] = _shift_up(a, 1, row_t)
        row = lax.broadcasted_iota(jnp.int32, (SUBLANE, w), 0)

        a_ref[...] = a

        def group(k, gnext):
            off = pl.multiple_of((ng - 1 - k) * SUBLANE, SUBLANE)
            cc = c_ref[pl.ds(off, SUBLANE), :]
            cb = g_ref[pl.ds(off, SUBLANE), :]
            cb = cb + jnp.where(row == SUBLANE - 1, gnext, 0.0)
            cc = jnp.where(row == SUBLANE - 1, 0.0, cc)
            for d in (1, 2, 4):
                c_sh = jnp.where(row < SUBLANE - d, pltpu.roll(cc, SUBLANE - d, axis=0), 1.0)
                b_sh = jnp.where(row < SUBLANE - d, pltpu.roll(cb, SUBLANE - d, axis=0), 0.0)
                cb = cc * b_sh + cb
                cc = cc * c_sh
            g_ref[pl.ds(off, SUBLANE), :] = cb
            a0 = a_ref[pl.ds(off, SUBLANE), :]
            return jnp.broadcast_to(a0[0:1, :] * cb[0:1, :], (SUBLANE, w))

        gc = lax.fori_loop(0, ng, group, jnp.zeros((SUBLANE, w), F32))
        gcarry_ref[...] = gc
        gsc = g_ref[...]

        hhalo = jnp.where(ti > 0, hhalo_ref[SUBLANE - 1:SUBLANE, :], 0.0)
        hprev = jnp.where(row_t == 0, hhalo, pltpu.roll(h, 1, axis=0))
        gated = i * xc
        d_gated = gsc * s
        d_atot = gsc * hprev - (gsc * gated) * a / s
        d_loga = d_atot * a
        d_r = d_loga * (-LRU_C) * sp
        dlam_ref[...] += jnp.sum(d_loga * r, axis=0, keepdims=True) * (LRU_C * jax.nn.sigmoid(-lam_ref[...]))
        d_i = d_gated * xc
        d_xc = d_gated * i
        d_pr = d_r * r * (1.0 - r)
        d_pi = d_i * i * (1.0 - i)
        dbr_ref[...] += jnp.sum(d_pr, axis=0, keepdims=True)
        dbi_ref[...] += jnp.sum(d_pi, axis=0, keepdims=True)
        extra = []
        for blk in range(nblk):
            sl = slice(blk * GATE_TILE, (blk + 1) * GATE_TILE)
            xb = xc[:, sl].astype(MXU)
            dr_b = d_pr[:, sl].astype(MXU)
            di_b = d_pi[:, sl].astype(MXU)
            dwr_ref[blk] += _dot_tn(xb, dr_b)
            dwi_ref[blk] += _dot_tn(xb, di_b)
            extra.append(_dot_nt(dr_b, wr_ref[blk]) + _dot_nt(di_b, wi_ref[blk]))
        d_xc = d_xc + jnp.concatenate(extra, axis=-1)
        dbc_ref[...] += jnp.sum(d_xc, axis=0, keepdims=True)
        for k in range(LRU_CONV):
            dwc_ref[k:k + 1, :] += jnp.sum(d_xc * xs[LRU_CONV - 1 - k], axis=0, keepdims=True)
        ext_ref[0:tt, :] = d_xc
        ext_ref[tt:, :] = xcarry_ref[...]
        dux = wc_ref[3:4, :] * d_xc
        for k in range(LRU_CONV - 1):
            dux = dux + wc_ref[k:k + 1, :] * ext_ref[pl.ds(LRU_CONV - 1 - k, tt), :]
        xcarry_ref[...] = d_xc[0:SUBLANE, :]
        dp_ref[:, :w] = dgate.astype(dp_ref.dtype)
        dp_ref[:, w:] = dux.astype(dp_ref.dtype)

    vec = lambda r: pl.BlockSpec((r, w), lambda b, i: (0, 0))
    wspec = pl.BlockSpec((nblk, GATE_TILE, GATE_TILE), lambda b, i: (0, 0, 0))
    hb = tt // SUBLANE
    rblk = lambda b, i: b * nt + (nt - 1 - i)
    halo_idx = lambda b, i: jnp.maximum(rblk(b, i) * hb - 1, 0)
    wide = pl.BlockSpec((tt, 2 * w), lambda b, i: (rblk(b, i), 0))
    narrow = pl.BlockSpec((tt, w), lambda b, i: (rblk(b, i), 0))
    return _hosted_call(
        body, grid=(bsz, nt),
        in_specs=[wide, pl.BlockSpec((SUBLANE, w), lambda b, i: (halo_idx(b, i), 1)),
                  narrow, pl.BlockSpec((SUBLANE, w), lambda b, i: (halo_idx(b, i), 0)), narrow,
                  vec(LRU_CONV), vec(1), wspec, vec(1), wspec, vec(1), vec(1)],
        out_specs=[wide, vec(LRU_CONV), vec(1), wspec, vec(1), wspec, vec(1), vec(1)],
        out_shape=[jax.ShapeDtypeStruct((n_tok, 2 * w + MEM_WIDTH), MXU),
                   jax.ShapeDtypeStruct((LRU_CONV, w), F32), jax.ShapeDtypeStruct((1, w), F32),
                   jax.ShapeDtypeStruct((nblk, GATE_TILE, GATE_TILE), F32), jax.ShapeDtypeStruct((1, w), F32),
                   jax.ShapeDtypeStruct((nblk, GATE_TILE, GATE_TILE), F32), jax.ShapeDtypeStruct((1, w), F32),
                   jax.ShapeDtypeStruct((1, w), F32)],
        scratch_shapes=[pltpu.VMEM((tt + SUBLANE, w), F32), pltpu.VMEM((tt, w), F32), pltpu.VMEM((tt, w), F32),
                        pltpu.VMEM((tt, w), F32), pltpu.VMEM((SUBLANE, w), F32), pltpu.VMEM((SUBLANE, w), F32)],
        args=(proj, proj, hs, hs, dycat, wconv, bconv, wr, br, wi, bi, lam), name=name, q=q,
        budget_us=HOST_US["lru_bwd"])


def _gate_tiles(w):
    per = GATE_TILE // HEAD_DIM
    w4 = w.reshape(LRU_BLOCKS // per, per, HEAD_DIM, HEAD_DIM)
    eye = jnp.eye(per, dtype=w.dtype)
    return jnp.einsum("bnij,nm->bnimj", w4, eye).reshape(LRU_BLOCKS // per, GATE_TILE, GATE_TILE)


def _gate_blocks(t):
    per = GATE_TILE // HEAD_DIM
    t5 = t.reshape(LRU_BLOCKS // per, per, HEAD_DIM, per, HEAD_DIM)
    eye = jnp.eye(per, dtype=t.dtype)
    return jnp.einsum("bnimj,nm->bnij", t5, eye).reshape(LRU_BLOCKS, HEAD_DIM, HEAD_DIM)


def _row(v):
    return v.reshape(1, -1)


def _local_step(x, mem, target, p, wfull, push_grad, q):
    bsz, t, d = x.shape
    n = bsz * t
    x2d = x.reshape(n, d)
    tgt = target.reshape(n, d)
    mem2d = mem.reshape(bsz * MEM_LEN, d)
    wr_t = [_gate_tiles(p["w_rg_r"][j]).astype(MXU) for j in range(N_A)]
    wi_t = [_gate_tiles(p["w_rg_i"][j]).astype(MXU) for j in range(N_A)]

    mn = [_norm_fwd(mem2d, _row(p["g_mem"][l]), name=f"mem_norm{l}") for l in range(DEPTH)]
    mkv = [None] * DEPTH
    h = _norm_fwd(x2d, _row(p["g_mix_pre"][0]), name="in_norm")
    xin = x2d
    sv = []
    kv = hkv = None
    for l in range(DEPTH):
        s = {"xin": xin, "h": h}
        if q is not None:
            q.horizon = (l + 2) * GROUPS_PER_LAYER
        mkv[l] = _mm_nn(mn[l], wfull("w_mem_kv", l), name=f"mem_kv{l}", q=q)
        if l < N_A:
            proj = _mm_nn(h, wfull("w_in_a", l), name=f"in_proj{l}", q=q)
            ycat, hs = _lru_fwd(proj, p["w_conv_a"][l], _row(p["b_conv_a"][l]), wr_t[l], _row(p["b_rg_r"][l]),
                                wi_t[l], _row(p["b_rg_i"][l]), _row(p["lru_lambda"][l]), bsz, name=f"lru_fwd{l}", q=q)
            s["hs"] = hs
            qblk = 2 * MIX_WIDTH // MEM_WIDTH
        else:
            if l == N_A:
                kv = _mm_nn(hkv, wfull("w_kv", 0), name="kv_proj", q=q)
            proj = _mm_nn(h, wfull("w_in_b", l - N_A), name=f"in_proj{l}", q=q)
            ycat = _swa_fwd(proj, kv, p["sinks_b"][l - N_A], bsz, name=f"swa_fwd{l}", q=q)
            qblk = MIX_WIDTH // MEM_WIDTH
        ycat = _mem_attn_fwd(proj, qblk, mkv[l], ycat, bsz, name=f"mem_attn_fwd{l}", q=q)
        y = _mm_nn(ycat, wfull("w_mix_out", l), name=f"mix_out{l}", q=q, out_dtype=MXU)
        x1, (h2,) = _resid_norm_fwd(xin, y, _row(p["g_mix_post"][l]), [_row(p["g_ffn_pre"][l])], name=f"mix_resid{l}", q=q)
        up = _mm_nn_slots(h2, wfull("w_ffn_up", l), name=f"ffn_up{l}", q=q, out_dtype=MXU)
        act, ug, uv = _ffn_act_fwd(up, p["w_ffn_conv"][l], _row(p["b_ffn_conv"][l]), bsz, name=f"ffn_act{l}", q=q)
        f = _mm_nn(act, wfull("w_ffn_down", l), name=f"ffn_down{l}", q=q, out_dtype=MXU)
        s.update(proj=proj, qblk=qblk, ycat=ycat, y=y, x1=x1, h2=h2, up=up, ug=ug, uv=uv, act=act, f=f)
        sv.append(s)
        if l < DEPTH - 1:
            g_pres = [_row(p["g_mix_pre"][l + 1])] + ([_row(p["g_kv"])] if l + 1 == N_A else [])
            xin, hn = _resid_norm_fwd(x1, f, _row(p["g_ffn_post"][l]), g_pres, name=f"ffn_resid{l}", q=q)
            h = hn[0]
            if l + 1 == N_A:
                hkv = hn[1]
        else:
            g_tot, sq, df, g_post_last = _loss_fwd(x1, f, _row(p["g_ffn_post"][l]), tgt, name="loss")

    if q is not None:
        q.horizon = LAST_GROUP
    gs = {k: [None] * DEPTH for k in ("g_mix_pre", "g_mix_post", "g_ffn_pre", "g_ffn_post", "g_mem",
                                       "w_ffn_conv", "b_ffn_conv")}
    ga = {k: [None] * N_A for k in ("w_conv_a", "b_conv_a", "w_rg_r", "b_rg_r", "w_rg_i", "b_rg_i", "lru_lambda")}
    gsink = [None] * (DEPTH - N_A)
    dkv_cur, dkv_prev = [], []
    gs["g_ffn_post"][DEPTH - 1] = g_post_last
    grad_x = None
    for l in reversed(range(DEPTH)):
        s = sv[l]
        dact = _mm_nt(df, wfull("w_ffn_down", l), name=f"d_act{l}", q=q, out_dtype=MXU)
        push_grad("w_ffn_down", l, _mm_tn(s["act"], df, name=f"dw_down{l}", q=q))
        dug, duv, gs["w_ffn_conv"][l], gs["b_ffn_conv"][l] = _ffn_act_bwd(
            s["up"], s["ug"], s["uv"], dact, p["w_ffn_conv"][l], bsz, name=f"ffn_act_bwd{l}", q=q)
        dh2 = _mm_ffn_dh(dug, duv, wfull("w_ffn_up", l), name=f"d_h2_{l}", q=q)
        up_slots = dict(slot_cols=2 * D_FF // N_CHIP, n_slots=N_CHIP)
        dwu = _mm_tn_slots(s["h2"], dug, name=f"dw_up_g{l}", q=q, **up_slots)
        push_grad("w_ffn_up", l, _mm_tn_slots(s["h2"], duv, name=f"dw_up_v{l}", q=q, out=dwu,
                                              first_slot=N_CHIP // 2, **up_slots))
        g1, dy, (gs["g_ffn_pre"][l],), gs["g_mix_post"][l] = _resid_norm_bwd(
            g_tot, [dh2], s["x1"], [_row(p["g_ffn_pre"][l])], s["y"], _row(p["g_mix_post"][l]), name=f"mix_resid_bwd{l}", q=q)
        dycat = _mm_nt(dy, wfull("w_mix_out", l), name=f"d_ycat{l}", q=q, out_dtype=MXU)
        push_grad("w_mix_out", l, _mm_tn(s["ycat"], dy, name=f"dw_mix_out{l}", q=q))
        if l < N_A:
            dproj, dwc, dbc, dwr, dbr, dwi, dbi, dlam = _lru_bwd(
                s["proj"], s["hs"], dycat, p["w_conv_a"][l], _row(p["b_conv_a"][l]), wr_t[l], _row(p["b_rg_r"][l]),
                wi_t[l], _row(p["b_rg_i"][l]), _row(p["lru_lambda"][l]), bsz, name=f"lru_bwd{l}", q=q)
            ga["w_conv_a"][l], ga["b_conv_a"][l], ga["lru_lambda"][l] = dwc, dbc[0], dlam[0]
            ga["w_rg_r"][l], ga["w_rg_i"][l] = _gate_blocks(dwr), _gate_blocks(dwi)
            ga["b_rg_r"][l] = dbr.reshape(LRU_BLOCKS, HEAD_DIM)
            ga["b_rg_i"][l] = dbi.reshape(LRU_BLOCKS, HEAD_DIM)
            w_in, j = "w_in_a", l
        else:
            dproj, dc, dp_, dsk = _swa_bwd(s["proj"], kv, p["sinks_b"][l - N_A], dycat, bsz, name=f"swa_bwd{l}", q=q)
            dkv_cur.append(dc)
            dkv_prev.append(dp_)
            gsink[l - N_A] = dsk[0, :SWA_HEADS]
            w_in, j = "w_in_b", l - N_A
        dproj, dmkv = _mem_attn_bwd(s["proj"], s["qblk"], mkv[l], dycat, dproj, bsz, name=f"mem_attn_bwd{l}", q=q)
        dh = _mm_nt(dproj, wfull(w_in, j), name=f"d_h{l}", q=q, out_dtype=MXU)
        push_grad(w_in, j, _mm_tn(s["h"], dproj, name=f"dw_in{l}", q=q))
        dmkv = dmkv.astype(MXU)
        dmn = _mm_nt(dmkv, wfull("w_mem_kv", l), name=f"d_mem_norm{l}", q=q)
        push_grad("w_mem_kv", l, _mm_tn(mn[l], dmkv, name=f"dw_mem_kv{l}", q=q))
        gs["g_mem"][l] = _norm_bwd_dg(dmn, mem2d, _row(p["g_mem"][l]), name=f"mem_norm_bwd{l}")
        dhs, g_pres = [dh], [_row(p["g_mix_pre"][l])]
        if l == N_A:
            dkv = _swa_dkv_combine(dkv_cur, dkv_prev, bsz, name="dkv_combine")
            dhs.append(_mm_nt(dkv, wfull("w_kv", 0), name="d_hkv", q=q, out_dtype=MXU))
            g_pres.append(_row(p["g_kv"]))
            push_grad("w_kv", 0, _mm_tn(hkv, dkv, name="dw_kv", q=q))
        if l > 0:
            g_tot, df, dgpre, gs["g_ffn_post"][l - 1] = _resid_norm_bwd(
                g1, dhs, s["xin"], g_pres, sv[l - 1]["f"], _row(p["g_ffn_post"][l - 1]), name=f"ffn_resid_bwd{l - 1}", q=q)
        else:
            grad_x, _, dgpre, _ = _resid_norm_bwd(g1, dhs, s["xin"], g_pres, None, None, name="in_norm_bwd", q=q)
        gs["g_mix_pre"][l] = dgpre[0]
        if l == N_A:
            g_kv = dgpre[1][0]

    grads = {}
    for k in ("g_mix_pre", "g_mix_post", "g_ffn_pre", "g_ffn_post", "g_mem", "b_ffn_conv"):
        grads[k] = jnp.concatenate(gs[k], axis=0)
    grads["w_ffn_conv"] = jnp.stack(gs["w_ffn_conv"])
    for k, v in ga.items():
        grads[k] = jnp.stack(v)
    grads["sinks_b"] = jnp.stack(gsink)
    grads["g_kv"] = g_kv
    return jnp.sum(sq), grad_x.reshape(bsz, t, d), grads


N_CHIP = 4
HALF_ALIGN = 16
D2D_STREAMS = 2
MIN_PART_BYTES = 128 * 1024


def _full_shape(kind, shard_shape):
    l, r, c = shard_shape
    return {"row": (l, N_CHIP * r, c), "slot": (N_CHIP, l, r, c)}[kind]


def _slot_view(ref, kind, shard_shape, s, hf, sub=(0, 1)):
    _, r, _ = shard_shape
    rh = r // 2
    if hf is None:
        size = r // sub[1]
        start = sub[0] * size
    else:
        size = rh // sub[1]
        start = hf * rh + sub[0] * size
    if kind == "row":
        start = s * r + start
    if not isinstance(start, int):
        start = pl.multiple_of(start, HALF_ALIGN)
    rows = pl.ds(start, size)
    if kind == "row":
        return ref.at[:, rows, :]
    return ref.at[s, :, rows, :]


def _half_view(ref, shard_shape, hf, sub=(0, 1)):
    rh = shard_shape[1] // 2
    size = rh // sub[1]
    return ref.at[:, pl.ds(pl.multiple_of(hf * rh + sub[0] * size, HALF_ALIGN), size), :]


def _mesh_pos():
    return lax.axis_index("x"), lax.axis_index("y"), lax.axis_index("c")


def _other_chips(x, y):
    return [(1 - x, y), (x, 1 - y), (1 - x, 1 - y)]


ICI_BYTES_PER_US = 6.0e4
ICI_GATHER_BYTES_PER_US = 5.5e4
D2D_BYTES_PER_US = 4.0e5


class _Chunk:
    def __init__(self, group, cost, ins, out_shapes, alias, n_sem, start, finish, done, buffer=None, bind=None):
        self.group, self.cost, self.ins, self.out_shapes, self.alias, self.n_sem = group, cost, ins, out_shapes, alias, n_sem
        self.start, self.finish, self.done = start, finish, done
        self.buffer = buffer
        self.bind = bind

    def prepare(self):
        if self.bind is not None:
            self.bind(self)


def _merged(chunks):
    groups, by_buffer = [], {}
    for ch in chunks:
        key = None if ch.buffer is None else (id(ch.buffer[0]), ch.buffer[1])
        if key is not None and key in by_buffer:
            by_buffer[key].append(ch)
        else:
            groups.append([ch])
            if key is not None:
                by_buffer[key] = groups[-1]
    out = []
    for parts in groups:
        if len(parts) == 1:
            out.append(parts[0])
            continue
        offs = [sum(p.n_sem for p in parts[:i]) for i in range(len(parts))]

        def run(phase, ins, outs, ss, rs, b, parts=parts, offs=offs):
            for p, o in zip(parts, offs):
                getattr(p, phase)(ins, outs, ss, rs, b + o)

        def done(outs, parts=parts):
            for p in parts:
                p.done(outs)

        first = parts[0]
        out.append(_Chunk(first.group, sum(p.cost for p in parts), first.ins, first.out_shapes, first.alias,
                          sum(p.n_sem for p in parts), functools.partial(run, "start"),
                          functools.partial(run, "finish"), done))
    return out


LAST_GROUP = 1 << 30
MIN_CARRIED_US = 8.0


class _CommQueue:
    def __init__(self):
        self.pending = []
        self.flushes = 0
        self.horizon = LAST_GROUP

    def push(self, chunk):
        self.pending.append(chunk)

    def take(self, budget_us):
        got, used = [], 0.0
        for ch in sorted(self.pending, key=lambda ch: (ch.group, -ch.cost)):
            if ch.group >= self.horizon and ch.group != LAST_GROUP:
                continue
            if used + ch.cost <= budget_us and not self._shares_buffer(ch, got):
                got.append(ch)
                used += ch.cost
        if used < MIN_CARRIED_US:
            return []
        return self._taken(got)

    @staticmethod
    def _shares_buffer(ch, others):
        return ch.buffer is not None and any(
            o.buffer is not None and o.buffer[0] is ch.buffer[0] and o.buffer[1] != ch.buffer[1] for o in others)

    def _taken(self, got):
        self.pending = [ch for ch in self.pending if ch not in got]
        for ch in got:
            ch.prepare()
        return _merged(got)

    def flush(self, group=LAST_GROUP):
        while True:
            chunks = []
            for ch in self.pending:
                if ch.group <= group and not self._shares_buffer(ch, chunks):
                    chunks.append(ch)
            if not chunks:
                return
            _run_chunks(self._taken(chunks), name=f"comm_flush{self.flushes}")
            self.flushes += 1


def _run_chunks(chunks, *, name):
    ins = [a for ch in chunks for a in ch.ins]
    outs = [s for ch in chunks for s in ch.out_shapes]
    alias, offs = {}, []
    i0 = o0 = s0 = 0
    for ch in chunks:
        offs.append((i0, o0, s0))
        for ci, co in ch.alias.items():
            alias[i0 + ci] = o0 + co
        i0 += len(ch.ins)
        o0 += len(ch.out_shapes)
        s0 += ch.n_sem

    def body(*refs):
        send_sems, recv_sems = refs[i0 + o0:]
        for phase in ("start", "finish"):
            for ch, (a, b, s) in zip(chunks, offs):
                getattr(ch, phase)(refs[a:a + len(ch.ins)], refs[i0 + b:i0 + b + len(ch.out_shapes)],
                                   send_sems, recv_sems, s)

    hbm = pl.BlockSpec(memory_space=pl.ANY)
    res = pl.pallas_call(
        body, in_specs=[hbm] * i0, out_specs=[hbm] * o0, out_shape=outs,
        scratch_shapes=[pltpu.SemaphoreType.DMA((s0,)), pltpu.SemaphoreType.DMA((s0,))],
        input_output_aliases=alias, name=name, compiler_params=pltpu.CompilerParams(has_side_effects=True))(*ins)
    for ch, (_, b, _) in zip(chunks, offs):
        ch.done(list(res[b:b + len(ch.out_shapes)]))


def _remote(src, dst, send_sems, recv_sems, k, dev):
    return pltpu.make_async_remote_copy(src_ref=src, dst_ref=dst, send_sem=send_sems.at[k], recv_sem=recv_sems.at[k],
                                        device_id=dev, device_id_type=MESH_T)


def _gather_chunks(q, group, kind, shard, l, ready):
    _, r, c = shard.shape
    shp = (1, r, c)
    rh = r // 2
    parts = max(p for p in (8, 4, 2, 1)
                if (rh // p) % HALF_ALIGN == 0 and (p == 1 or (rh // p) * c * shard.dtype.itemsize >= MIN_PART_BYTES))
    part_bytes = (rh // parts) * c * shard.dtype.itemsize
    full_type = jax.ShapeDtypeStruct(_full_shape(kind, shp), shard.dtype)
    state = {"full": None, "parts_done": 0}

    def bind_first(ch):
        ch.ins, ch.alias = ([shard], {}) if state["full"] is None else ([shard, state["full"]], {1: 0})

    def bind_full(ch):
        ch.ins = [state["full"]]

    def make_part(p):
        sub = (p, parts)

        def any_part(full):
            return _slot_view(full, kind, shp, 0, 0, sub)

        def own_rows(src):
            return src.at[:, pl.ds(p * (r // parts), r // parts), :]

        def start1(ins, outs, ss, rs, b):
            x, y, c_ = _mesh_pos()
            src, full, me = ins[0].at[pl.ds(l, 1)], outs[0], 2 * x + y
            pltpu.make_async_copy(own_rows(src), _slot_view(full, kind, shp, me, None, sub), ss.at[b + N_CHIP - 1]).start()
            for j, (ox, oy) in enumerate(_other_chips(x, y)):
                _remote(_half_view(src, shp, c_, sub), _slot_view(full, kind, shp, me, c_, sub), ss, rs, b + j,
                        (ox, oy, c_)).start()

        def finish1(ins, outs, ss, rs, b):
            x, y, c_ = _mesh_pos()
            h = any_part(outs[0])
            for j in range(N_CHIP - 1):
                _remote(h, h, ss, rs, b + j, (x, y, 1 - c_)).wait()
            pltpu.make_async_copy(own_rows(ins[0].at[pl.ds(l, 1)]), _slot_view(outs[0], kind, shp, 0, None, sub),
                                  ss.at[b + N_CHIP - 1]).wait()

        def start2(ins, outs, ss, rs, b):
            x, y, c_ = _mesh_pos()
            for j, (ox, oy) in enumerate(_other_chips(x, y)):
                v = _slot_view(outs[0], kind, shp, 2 * ox + oy, c_, sub)
                _remote(v, v, ss, rs, b + j, (x, y, 1 - c_)).start()

        def finish2(ins, outs, ss, rs, b):
            x, y, c_ = _mesh_pos()
            h = any_part(outs[0])
            for j in range(N_CHIP - 1):
                _remote(h, h, ss, rs, b + j, (x, y, 1 - c_)).wait()

        def done2(outs):
            state["full"] = outs[0]
            state["parts_done"] += 1
            if state["parts_done"] == parts:
                ready(outs[0])

        def done1(outs):
            state["full"] = outs[0]
            q.push(_Chunk(group, 3 * part_bytes / D2D_BYTES_PER_US, None, [full_type], {0: 0}, N_CHIP - 1,
                          start2, finish2, done2, buffer=(state, 2), bind=bind_full))

        return _Chunk(group, 3 * part_bytes / ICI_GATHER_BYTES_PER_US, None, [full_type], None,
                      N_CHIP, start1, finish1, done1, buffer=(state, 1), bind=bind_first)

    for p in range(parts):
        q.push(make_part(p))


def _reduce_scatter_chunks(q, kind, grad, shard_shape, pos, name, ready):
    _, r, c = shard_shape
    shp = (1, r, c)
    rh = r // 2

    rp = rh // D2D_STREAMS

    def landing(ref, s, i):
        return ref.at[s, :, pl.ds(i * rp, rp), :]

    def start1(ins, outs, ss, rs, b):
        x, y, c_ = _mesh_pos()
        for s in range(N_CHIP):
            for i in range(D2D_STREAMS):
                _remote(_slot_view(ins[0], kind, shp, s, 1 - c_, (i, D2D_STREAMS)), landing(outs[0], s, i),
                        ss, rs, b + s * D2D_STREAMS + i, (x, y, 1 - c_)).start()

    def finish1(ins, outs, ss, rs, b):
        x, y, c_ = _mesh_pos()
        for s in range(N_CHIP):
            for i in range(D2D_STREAMS):
                v = landing(outs[0], s, i)
                _remote(v, v, ss, rs, b + s * D2D_STREAMS + i, (x, y, 1 - c_)).wait()

    def start2(ins, outs, ss, rs, b):
        x, y, c_ = _mesh_pos()
        for j, (ox, oy) in enumerate(_other_chips(x, y)):
            _remote(ins[0].at[2 * ox + oy], outs[0].at[j], ss, rs, b + j, (ox, oy, c_)).start()

    def finish2(ins, outs, ss, rs, b):
        x, y, c_ = _mesh_pos()
        for j in range(N_CHIP - 1):
            _remote(outs[0].at[j], outs[0].at[j], ss, rs, b + j, (x, y, 1 - c_)).wait()

    def start3(ins, outs, ss, rs, b):
        x, y, c_ = _mesh_pos()
        for i in range(D2D_STREAMS):
            v = _half_view(outs[0], shp, c_, (i, D2D_STREAMS))
            _remote(v, v, ss, rs, b + i, (x, y, 1 - c_)).start()

    def finish3(ins, outs, ss, rs, b):
        x, y, c_ = _mesh_pos()
        for i in range(D2D_STREAMS):
            v = _half_view(outs[0], shp, c_, (i, D2D_STREAMS))
            _remote(v, v, ss, rs, b + i, (x, y, 1 - c_)).wait()

    def done2(pair, outs):
        half = _rs_chip_add(pair, outs[0], shp, pos, name=f"rs_chip_add_{name}")
        q.push(_Chunk(LAST_GROUP, rh * c * 4 / D2D_BYTES_PER_US, [half], [jax.ShapeDtypeStruct(half.shape, half.dtype)],
                      {0: 0}, D2D_STREAMS, start3, finish3, lambda o: ready(o[0])))

    def done1(outs):
        pair, wire = _rs_pair_add(grad, outs[0], kind, shp, pos, name=f"rs_pair_add_{name}")
        q.push(_Chunk(LAST_GROUP, 3 * rh * c * wire.dtype.itemsize / ICI_BYTES_PER_US, [wire],
                      [jax.ShapeDtypeStruct((N_CHIP - 1, 1, rh, c), wire.dtype)], {}, N_CHIP - 1,
                      start2, finish2, functools.partial(done2, pair)))

    q.push(_Chunk(LAST_GROUP, N_CHIP * rh * c * 4 / D2D_BYTES_PER_US, [grad],
                  [jax.ShapeDtypeStruct((N_CHIP, 1, rh, c), F32)], {}, N_CHIP * D2D_STREAMS, start1, finish1, done1))


N_DEV = 8


def _allgather_chunk(q, group, vec, ready):
    def peer(k, x, y, c):
        return ((1 - x) if k & 4 else x, (1 - y) if k & 2 else y, (1 - c) if k & 1 else c)

    def start(ins, outs, ss, rs, b):
        x, y, c = _mesh_pos()
        me = 4 * x + 2 * y + c
        pltpu.make_async_copy(ins[0], outs[0].at[me], ss.at[b + N_DEV - 1]).start()
        for k in range(1, N_DEV):
            _remote(ins[0], outs[0].at[me], ss, rs, b + k - 1, peer(k, x, y, c)).start()

    def finish(ins, outs, ss, rs, b):
        x, y, c = _mesh_pos()
        for k in range(1, N_DEV):
            _remote(ins[0], outs[0].at[0], ss, rs, b + k - 1, peer(k, x, y, c)).wait()
        pltpu.make_async_copy(ins[0], outs[0].at[0], ss.at[b + N_DEV - 1]).wait()

    bytes_in = (N_DEV - 2) * vec.size * 4
    q.push(_Chunk(group, bytes_in / ICI_BYTES_PER_US, [vec], [jax.ShapeDtypeStruct((N_DEV,) + vec.shape, F32)], {},
                  N_DEV, start, finish, lambda o: ready(o[0])))


def _allreduce8(vec, *, name):
    r = vec.shape[0]
    rh = r // 2

    def body(v_ref, o_ref, sib_ref, chips_ref, send_sems, recv_sems):
        x, y, c = _mesh_pos()
        sib = (x, y, 1 - c)
        me = 2 * x + y
        pair = _remote(v_ref, sib_ref, send_sems, recv_sems, 0, sib)
        pair.start()
        pair.wait()
        rows = pl.ds(pl.multiple_of(c * rh, SUBLANE), rh)
        chips_ref[me] = v_ref[rows, :] + sib_ref[rows, :]
        copies = []
        for j, (ox, oy) in enumerate(_other_chips(x, y)):
            cp = _remote(chips_ref.at[me], chips_ref.at[me], send_sems, recv_sems, 1 + j, (ox, oy, c))
            cp.start()
            copies.append(cp)
        for cp in copies:
            cp.wait()
        acc = chips_ref[0]
        for s in range(1, N_CHIP):
            acc = acc + chips_ref[s]
        o_ref[rows, :] = acc
        swap = _remote(o_ref.at[rows, :], o_ref.at[rows, :], send_sems, recv_sems, N_CHIP, sib)
        swap.start()
        swap.wait()

    vm = pl.BlockSpec(memory_space=pltpu.VMEM)
    return pl.pallas_call(
        body, in_specs=[vm], out_specs=vm, out_shape=jax.ShapeDtypeStruct((r, LANE), F32),
        scratch_shapes=[pltpu.VMEM((r, LANE), F32), pltpu.VMEM((N_CHIP, rh, LANE), F32),
                        pltpu.SemaphoreType.DMA((N_CHIP + 1,)), pltpu.SemaphoreType.DMA((N_CHIP + 1,))],
        name=name, compiler_params=pltpu.CompilerParams(has_side_effects=True, vmem_limit_bytes=VMEM_LIMIT_V7X))(vec)


def _rs_pair_add(g, recv, kind, shape, pos, *, name):
    l, r, c = shape
    assert l == 1
    rh = r // 2
    if kind == "row":
        gspec = pl.BlockSpec((None, rh, c), lambda s, pos: (0, 2 * s + pos[0], 0))
    else:
        gspec = pl.BlockSpec((None, None, rh, c), lambda s, pos: (s, 0, pos[0], 0))
    pspec = pl.BlockSpec((None, None, rh, c), lambda s, pos: (s, 0, 0, 0))

    def body(pos_ref, g_ref, r_ref, own_ref, pw_ref):
        v = g_ref[...] + r_ref[...]
        pw_ref[...] = v.astype(pw_ref.dtype)

        @pl.when(pl.program_id(0) == pos_ref[1])
        def _():
            own_ref[...] = v

    return pl.pallas_call(
        body,
        grid_spec=pltpu.PrefetchScalarGridSpec(
            num_scalar_prefetch=1, grid=(N_CHIP,), in_specs=[gspec, pspec],
            out_specs=[pl.BlockSpec((None, rh, c), lambda s, pos: (0, 0, 0)), pspec]),
        out_shape=[jax.ShapeDtypeStruct((1, rh, c), F32), jax.ShapeDtypeStruct((N_CHIP, 1, rh, c), MXU)],
        name=name, compiler_params=_cp((ARB,)))(pos, g, recv)


def _rs_chip_add(p, recv, shape, pos, *, name):
    l, r, c = shape
    rh = r // 2

    def body(pos_ref, p_ref, r_ref, o_ref):
        del pos_ref
        acc = p_ref[...]
        for j in range(N_CHIP - 1):
            acc = acc + r_ref[j].astype(F32)
        o_ref[...] = acc

    return pl.pallas_call(
        body,
        grid_spec=pltpu.PrefetchScalarGridSpec(
            num_scalar_prefetch=1, grid=(l,),
            in_specs=[pl.BlockSpec((None, rh, c), lambda i, pos: (i, 0, 0)),
                      pl.BlockSpec((N_CHIP - 1, None, rh, c), lambda i, pos: (0, i, 0, 0))],
            out_specs=pl.BlockSpec((None, rh, c), lambda i, pos: (i, pos[0], 0))),
        out_shape=jax.ShapeDtypeStruct((l, r, c), F32),
        name=name, compiler_params=_cp((PAR,)))(pos, p, recv)


ADAM_BLOCK_ELEMS = 384 * 1024


def _adam_math(w, g, m, v):
    c1 = 1.0 / (1.0 - ADAM_B1 ** ADAM_STEP)
    c2 = 1.0 / (1.0 - ADAM_B2 ** ADAM_STEP)
    nm = ADAM_B1 * m + (1.0 - ADAM_B1) * g
    nv = ADAM_B2 * v + (1.0 - ADAM_B2) * (g * g)
    return -ADAM_LR * ((nm * c1) / (jnp.sqrt(nv * c2) + ADAM_EPS) + ADAM_WD * w), nm, nv


def _adamw_layer(w, g, m, v, outs, l, *, name):
    _, r, c = w.shape
    tr = _tile(r, max(SUBLANE, ADAM_BLOCK_ELEMS // c // SUBLANE * SUBLANE), SUBLANE)

    def body(w_ref, g_ref, m_ref, v_ref, *rest):
        go_ref, d_ref, nm_ref, nv_ref = rest[4:]
        gg = g_ref[...]
        go_ref[...] = gg
        d_ref[...], nm_ref[...], nv_ref[...] = _adam_math(w_ref[...], gg, m_ref[...], v_ref[...])

    lay = pl.BlockSpec((None, tr, c), lambda j: (l, j, 0))
    hbm = pl.BlockSpec(memory_space=pl.ANY)
    return pl.pallas_call(
        body, grid=(r // tr,),
        in_specs=[lay, pl.BlockSpec((None, tr, c), lambda j: (0, j, 0)), lay, lay] + [hbm] * 4,
        out_specs=[lay] * 4, out_shape=[jax.ShapeDtypeStruct(w.shape, F32)] * 4,
        input_output_aliases={4 + i: i for i in range(4)},
        name=name, compiler_params=_cp((PAR,)))(w, g, m, v, *outs)


def _adamw(w, g, m, v, *, name):
    shape = w.shape
    if w.ndim == 2:
        w, g, m, v = (a[None] for a in (w, g, m, v))
    l, r, c = w.shape
    tr = _tile(r, max(SUBLANE, ADAM_BLOCK_ELEMS // c // SUBLANE * SUBLANE), SUBLANE)

    def body(w_ref, g_ref, m_ref, v_ref, d_ref, nm_ref, nv_ref):
        d_ref[...], nm_ref[...], nv_ref[...] = _adam_math(w_ref[...], g_ref[...], m_ref[...], v_ref[...])

    spec = pl.BlockSpec((None, tr, c), lambda i, j: (i, j, 0))
    outs = pl.pallas_call(
        body, grid=(l, r // tr), in_specs=[spec] * 4, out_specs=[spec] * 3,
        out_shape=[jax.ShapeDtypeStruct((l, r, c), F32)] * 3,
        name=name, compiler_params=_cp((PAR, PAR)))(w, g, m, v)
    return tuple(o.reshape(shape) for o in outs)


PACK_ROWS = 2 * SUBLANE * LANE


def _pack(arrays):
    flat = jnp.concatenate([a.reshape(-1).astype(F32) for a in arrays])
    pad = (-flat.shape[0]) % PACK_ROWS
    return jnp.pad(flat, (0, pad)).reshape(-1, LANE)


def _unpack(packed, shapes):
    flat = packed.reshape(-1)
    out, off = [], 0
    for s in shapes:
        size = int(np.prod(s))
        out.append(flat[off:off + size].reshape(s))
        off += size
    return out


BIG = (("w_mem_kv", "row"), ("w_mix_out", "row"), ("w_ffn_up", "slot"), ("w_ffn_down", "row"),
       ("w_in_a", "slot"), ("w_in_b", "row"), ("w_kv", "row"))
COLUMN_SHARDED_AS_COLUMNS = ("w_in_a",)
SMALL_SHARDED = (("w_ffn_conv", 2), ("w_conv_a", 2), ("b_conv_a", 1), ("lru_lambda", 1))
SMALL_REPLICATED = ("g_mix_pre", "g_mix_post", "g_ffn_pre", "g_ffn_post", "g_mem", "b_ffn_conv",
                    "w_rg_r", "b_rg_r", "w_rg_i", "b_rg_i", "sinks_b", "g_kv")
WEIGHTS = ("g_mix_pre", "g_mix_post", "g_ffn_pre", "g_ffn_post", "g_mem", "w_mem_kv", "w_mix_out", "w_ffn_up",
           "w_ffn_conv", "b_ffn_conv", "w_ffn_down", "w_in_a", "w_conv_a", "b_conv_a", "w_rg_r", "b_rg_r", "w_rg_i",
           "b_rg_i", "lru_lambda", "w_in_b", "sinks_b", "g_kv", "w_kv")


def _slot_to_cols(a):
    s, l, r, c = a.shape
    return a.transpose(1, 2, 0, 3).reshape(l, r, s * c)


def _cols_to_slot(a):
    l, r, c4 = a.shape
    return a.reshape(l, r, N_CHIP, c4 // N_CHIP).transpose(2, 0, 1, 3)


GROUPS_PER_LAYER = 8


def _layer_weights(layer):
    names = [("w_mem_kv", layer), ("w_in_a", layer) if layer < N_A else ("w_in_b", layer - N_A)]
    if layer == N_A:
        names.append(("w_kv", 0))
    return names + [("w_mix_out", layer), ("w_ffn_up", layer), ("w_ffn_down", layer)]


def _train_step(x, mem, target, w, m, v):
    xi, yi, ci = _mesh_pos()
    chip = 2 * xi + yi
    pos = jnp.stack([ci, chip]).astype(jnp.int32)

    q = _CommQueue()
    kinds = dict(BIG)
    as3 = lambda a: a if a.ndim == 3 else a[None]
    w3, m3, v3 = ({k: as3(d[k]) for k, _ in BIG} for d in (w, m, v))
    shards = {k: w3[k].astype(MXU) for k, _ in BIG}

    gathered = {}

    def on_gathered(k, l, full):
        gathered[k, l] = _slot_to_cols(full) if k in COLUMN_SHARDED_AS_COLUMNS else full

    group_of = {}

    for layer in range(DEPTH):
        for i, (k, l) in enumerate(_layer_weights(layer)):
            group_of[k, l] = layer * GROUPS_PER_LAYER + i
            _gather_chunks(q, group_of[k, l], kinds[k], shards[k], l, functools.partial(on_gathered, k, l))

    def wfull(k, l):
        if (k, l) not in gathered:
            q.flush(group_of[k, l])
        return gathered[k, l]

    small = {}
    _allgather_chunk(q, 0, _pack([w[k] for k, _ in SMALL_SHARDED]), functools.partial(small.__setitem__, "stacked"))
    q.flush(1)

    big_out = {k: [lax.empty(w3[k].shape, F32) for _ in range(4)] for k, _ in BIG}

    def on_reduced(k, l, g):
        big_out[k] = _adamw_layer(w3[k], g, m3[k], v3[k], big_out[k], l, name=f"adamw_{k}{l}")

    def push_grad(k, l, g):
        if k in COLUMN_SHARDED_AS_COLUMNS:
            g = _cols_to_slot(g)
        _reduce_scatter_chunks(q, kinds[k], g, (1,) + w3[k].shape[1:], pos, f"{k}{l}", functools.partial(on_reduced, k, l))

    small_shapes = [w[k].shape for k, _ in SMALL_SHARDED]
    per_chip = [_unpack(small["stacked"][2 * s], small_shapes) for s in range(N_CHIP)]
    p = {k: w[k] for k in SMALL_REPLICATED}
    for i, (k, axis) in enumerate(SMALL_SHARDED):
        p[k] = jnp.concatenate([per_chip[s][i] for s in range(N_CHIP)], axis=axis)

    sq, grad_x, g = _local_step(x, mem, target, p, wfull, push_grad, q)
    loss = lax.psum(0.5 * sq / D_MODEL, ("x", "y", "c"))
    q.flush()

    small_names = [k for k, _ in SMALL_SHARDED] + list(SMALL_REPLICATED)
    summed = _allreduce8(_pack([g[k] for k in small_names]), name="allreduce_small")
    gsum = dict(zip(small_names, _unpack(summed, [p[k].shape for k in small_names])))
    for k, axis in SMALL_SHARDED:
        gsum[k] = lax.dynamic_slice_in_dim(gsum[k], chip * w[k].shape[axis], w[k].shape[axis], axis)

    delta, new_m, new_v = {}, {}, {}
    for k, _ in BIG:
        gsum[k], delta[k], new_m[k], new_v[k] = (o.reshape(w[k].shape) for o in big_out[k])
    for k in small_names:
        as2 = lambda a: a.reshape(-1, a.shape[-1])
        outs = _adamw(as2(w[k]), as2(gsum[k]), as2(m[k]), as2(v[k]), name=f"adamw_{k}")
        delta[k], new_m[k], new_v[k] = (o.reshape(w[k].shape) for o in outs)
    return (loss, grad_x, *[gsum[k] for k in WEIGHTS], *[delta[k] for k in WEIGHTS],
            *[new_m[k] for k in WEIGHTS], *[new_v[k] for k in WEIGHTS])


def kernel(x, mem, g_mix_pre, g_mix_post, g_ffn_pre, g_ffn_post, g_mem, w_mem_kv, w_mix_out, w_ffn_up, w_ffn_conv, b_ffn_conv, w_ffn_down, w_in_a, w_conv_a, b_conv_a, w_rg_r, b_rg_r, w_rg_i, b_rg_i, lru_lambda, w_in_b, sinks_b, g_kv, w_kv, loss_target, m_g_mix_pre, m_g_mix_post, m_g_ffn_pre, m_g_ffn_post, m_g_mem, m_w_mem_kv, m_w_mix_out, m_w_ffn_up, m_w_ffn_conv, m_b_ffn_conv, m_w_ffn_down, m_w_in_a, m_w_conv_a, m_b_conv_a, m_w_rg_r, m_b_rg_r, m_w_rg_i, m_b_rg_i, m_lru_lambda, m_w_in_b, m_sinks_b, m_g_kv, m_w_kv, v_g_mix_pre, v_g_mix_post, v_g_ffn_pre, v_g_ffn_post, v_g_mem, v_w_mem_kv, v_w_mix_out, v_w_ffn_up, v_w_ffn_conv, v_b_ffn_conv, v_w_ffn_down, v_w_in_a, v_w_conv_a, v_b_conv_a, v_w_rg_r, v_b_rg_r, v_w_rg_i, v_b_rg_i, v_lru_lambda, v_w_in_b, v_sinks_b, v_g_kv, v_w_kv):
    args = (g_mix_pre, g_mix_post, g_ffn_pre, g_ffn_post, g_mem, w_mem_kv, w_mix_out, w_ffn_up, w_ffn_conv, b_ffn_conv, w_ffn_down, w_in_a, w_conv_a, b_conv_a, w_rg_r, b_rg_r, w_rg_i, b_rg_i, lru_lambda, w_in_b, sinks_b, g_kv, w_kv)
    ms = (m_g_mix_pre, m_g_mix_post, m_g_ffn_pre, m_g_ffn_post, m_g_mem, m_w_mem_kv, m_w_mix_out, m_w_ffn_up, m_w_ffn_conv, m_b_ffn_conv, m_w_ffn_down, m_w_in_a, m_w_conv_a, m_b_conv_a, m_w_rg_r, m_b_rg_r, m_w_rg_i, m_b_rg_i, m_lru_lambda, m_w_in_b, m_sinks_b, m_g_kv, m_w_kv)
    vs = (v_g_mix_pre, v_g_mix_post, v_g_ffn_pre, v_g_ffn_post, v_g_mem, v_w_mem_kv, v_w_mix_out, v_w_ffn_up, v_w_ffn_conv, v_b_ffn_conv, v_w_ffn_down, v_w_in_a, v_w_conv_a, v_b_conv_a, v_w_rg_r, v_b_rg_r, v_w_rg_i, v_b_rg_i, v_lru_lambda, v_w_in_b, v_sinks_b, v_g_kv, v_w_kv)
    return _train_step(x, mem, loss_target, dict(zip(WEIGHTS, args)), dict(zip(WEIGHTS, ms)), dict(zip(WEIGHTS, vs)))
```

```python
import functools
import math

import numpy as np
import jax
import jax.numpy as jnp
from jax import lax
from jax.experimental import pallas as pl
from jax.experimental.pallas import tpu as pltpu

F32 = jnp.float32
MXU = jnp.bfloat16

D_MODEL = 1024
HEAD_DIM = 64
MEM_LEN = 256
MEM_HEADS = 4
MEM_WIDTH = MEM_HEADS * HEAD_DIM
MIX_WIDTH = D_MODEL - MEM_WIDTH
LRU_BLOCKS = MIX_WIDTH // HEAD_DIM
LRU_CONV = 4
LRU_C = 8.0
SWA_HEADS = MIX_WIDTH // HEAD_DIM
SWA_KV_HEADS = 4
SWA_GROUP = SWA_HEADS // SWA_KV_HEADS
WINDOW = 128
D_FF = 2816
FFN_CONV = 3
EPS = 1e-6
DEPTH = 4
N_A = 2

ADAM_LR = 0.001
ADAM_B1 = 0.9
ADAM_B2 = 0.999
ADAM_EPS = 1e-08
ADAM_WD = 0.01
ADAM_STEP = 10

VMEM_LIMIT_V7X = 56 * 1024 * 1024
LANE = 128
SUBLANE = 8
GATE_TILE = 256
MESH_T = pl.DeviceIdType.MESH


def _alibi_slopes(n):
    def pow2_slopes(m):
        start = 2.0 ** (-8.0 / m)
        return [start ** (i + 1) for i in range(m)]
    c = 2 ** int(math.floor(math.log2(n)))
    s = pow2_slopes(c)
    if c != n:
        s = s + pow2_slopes(2 * c)[0::2][: n - c]
    return [float(np.float32(v)) for v in s]


SLOPES = _alibi_slopes(SWA_HEADS)


def _tile(n, cap, mult=LANE):
    best = None
    for t in range(mult, min(n, cap) + 1, mult):
        if n % t == 0:
            best = t
    return best if best is not None else n


def _cp(sem):
    return pltpu.CompilerParams(dimension_semantics=sem, vmem_limit_bytes=VMEM_LIMIT_V7X)


MM_VMEM_BUDGET = 40 * 1024 * 1024
HBM_BYTES_PER_US_V7X = 3.0e6
GRID_STEP_US = 0.35


def _divisors(n, mult):
    return [t for t in range(mult, n + 1, mult) if n % t == 0] or [n]


def _mm_tiles(m, k, n, out_bytes):
    best = None
    for tm in _divisors(m, 256):
        for tn in _divisors(n, LANE):
            vmem = 2 * (tm * k * 2 + k * tn * 2 + tm * tn * out_bytes)
            if vmem > MM_VMEM_BUDGET:
                continue
            steps = (m // tm) * (n // tn)
            b_reads = 1 if tn == n else m // tm
            traffic = m * k * 2 + k * n * 2 * b_reads + m * n * out_bytes
            first = tm * k * 2 + k * tn * 2
            cost = (traffic + first) / HBM_BYTES_PER_US_V7X + steps * GRID_STEP_US
            if best is None or cost < best[0]:
                best = (cost, tm, tn)
    return best[1], best[2]


def _mm_tn_tiles(k, m, n, whole_n=False):
    best = None
    for tm in _divisors(m, LANE):
        for tn in ([n] if whole_n else _divisors(n, LANE)):
            for tk in _divisors(k, 512):
                vmem = 2 * (tk * tm * 2 + tk * tn * 2 + tm * tn * 4)
                if vmem > MM_VMEM_BUDGET:
                    continue
                steps = (m // tm) * (n // tn) * (k // tk)
                traffic = k * m * 2 * (n // tn) + k * n * 2 * (m // tm) + m * n * 4
                cost = traffic / HBM_BYTES_PER_US_V7X + steps * GRID_STEP_US
                if best is None or cost < best[0]:
                    best = (cost, tk, tm, tn)
    return best[1], best[2], best[3]


ARB = "arbitrary"
PAR = "parallel"


def _rms_fwd(x, g):
    r = lax.rsqrt(jnp.mean(x * x, axis=-1, keepdims=True) + EPS)
    return x * r * g


def _rms_bwd(dy, x, g):
    r = lax.rsqrt(jnp.mean(x * x, axis=-1, keepdims=True) + EPS)
    xh = x * r
    gdy = dy * g
    dx = r * (gdy - xh * jnp.mean(gdy * xh, axis=-1, keepdims=True))
    dg = jnp.sum(dy * xh, axis=0, keepdims=True)
    return dx, dg


_GELU_K = math.sqrt(2.0 / math.pi)
_GELU_C = 0.044715


def _gelu(x):
    t = jnp.tanh(_GELU_K * (x + _GELU_C * x * x * x))
    return 0.5 * x * (1.0 + t)


def _gelu_and_grad(x):
    x2 = x * x
    u = 0.5 * jnp.tanh(x * (_GELU_K + (_GELU_K * _GELU_C) * x2)) + 0.5
    dz2 = (6.0 * _GELU_K * _GELU_C) * x2 + 2.0 * _GELU_K
    return x * u, u * ((x * (1.0 - u)) * dz2 + 1.0)


def _shift_up(x, k, row):
    n = x.shape[0]
    return jnp.where(row < n - k, pltpu.roll(x, n - k, axis=0), 0.0)


def _shift_down_edge(x, k):
    r = pltpu.roll(x, k, axis=0)
    row = lax.broadcasted_iota(jnp.int32, (SUBLANE, x.shape[1]), 0)
    return jnp.concatenate([jnp.where(row >= k, r[:SUBLANE], 0.0), r[SUBLANE:]], axis=0)


def _shift_up_edge(x, k):
    n = x.shape[0]
    r = pltpu.roll(x, n - k, axis=0)
    row = lax.broadcasted_iota(jnp.int32, (SUBLANE, x.shape[1]), 0)
    return jnp.concatenate([r[:n - SUBLANE], jnp.where(row < SUBLANE - k, r[n - SUBLANE:], 0.0)], axis=0)


def _dot(a, b):
    return jnp.dot(a, b, preferred_element_type=F32)


def _dot_nt(a, b):
    return lax.dot_general(a, b, (((1,), (1,)), ((), ())), preferred_element_type=F32)


def _dot_tn(a, b):
    return lax.dot_general(a, b, (((0,), (0,)), ((), ())), preferred_element_type=F32)


MXU_FLOPS_PER_US = 8.0e8
HOST_US = {"lru_fwd": 44.0, "lru_bwd": 94.0, "swa_fwd": 55.0, "swa_bwd": 90.0, "mem_attn_fwd": 19.0,
           "mem_attn_bwd": 27.0, "ffn_act": 75.0, "ffn_act_bwd": 75.0, "resid": 22.0, "resid_bwd": 33.0}
HOST_FILL = 1.0


def _hosted_call(body, *, grid, in_specs, out_specs, out_shape, args, name, aliases=None, scratch_shapes=(),
                 q=None, flops=0.0, budget_us=0.0):
    chunks = q.take(HOST_FILL * (flops / MXU_FLOPS_PER_US + budget_us)) if q is not None else []
    if not chunks:
        return pl.pallas_call(
            body, grid=grid, in_specs=in_specs, out_specs=out_specs, out_shape=out_shape,
            scratch_shapes=list(scratch_shapes), input_output_aliases=aliases or {}, name=name,
            compiler_params=_cp((ARB,) * len(grid)))(*args)
    single = not isinstance(out_shape, (list, tuple))
    o_shapes = [out_shape] if single else list(out_shape)
    o_specs = [out_specs] if single else list(out_specs)
    n_in, n_out, n_scr = len(args), len(o_shapes), len(scratch_shapes)
    c_ins = [a for ch in chunks for a in ch.ins]
    c_outs = [s for ch in chunks for s in ch.out_shapes]
    alias = dict(aliases or {})
    in_off, out_off, sem_off = [], [], []
    i0 = o0 = s0 = 0
    for ch in chunks:
        in_off.append(i0)
        out_off.append(o0)
        sem_off.append(s0)
        for ci, co in ch.alias.items():
            alias[n_in + i0 + ci] = n_out + o0 + co
        i0 += len(ch.ins)
        o0 += len(ch.out_shapes)
        s0 += ch.n_sem

    def wrapped(*refs):
        ins = refs[:n_in]
        cin = refs[n_in:n_in + i0]
        outs = refs[n_in + i0:n_in + i0 + n_out]
        cout = refs[n_in + i0 + n_out:n_in + i0 + n_out + o0]
        scr = refs[n_in + i0 + n_out + o0:n_in + i0 + n_out + o0 + n_scr]
        send_sems, recv_sems = refs[n_in + i0 + n_out + o0 + n_scr:]
        first = functools.reduce(lambda u, v: u & v, [pl.program_id(d) == 0 for d in range(len(grid))])
        last = functools.reduce(lambda u, v: u & v, [pl.program_id(d) == grid[d] - 1 for d in range(len(grid))])

        def each(phase):
            for ch, a, b, s in zip(chunks, in_off, out_off, sem_off):
                getattr(ch, phase)(cin[a:a + len(ch.ins)], cout[b:b + len(ch.out_shapes)], send_sems, recv_sems, s)

        pl.when(first)(lambda: each("start"))
        body(*ins, *outs, *scr)
        pl.when(last)(lambda: each("finish"))

    hbm = pl.BlockSpec(memory_space=pl.ANY)
    res = pl.pallas_call(
        wrapped, grid=grid, in_specs=list(in_specs) + [hbm] * i0, out_specs=o_specs + [hbm] * o0,
        out_shape=o_shapes + c_outs,
        scratch_shapes=list(scratch_shapes) + [pltpu.SemaphoreType.DMA((s0,)), pltpu.SemaphoreType.DMA((s0,))],
        input_output_aliases=alias, name=name,
        compiler_params=pltpu.CompilerParams(dimension_semantics=(ARB,) * len(grid), vmem_limit_bytes=VMEM_LIMIT_V7X,
                                             has_side_effects=True))(*args, *c_ins)
    for ch, b in zip(chunks, out_off):
        ch.done(list(res[n_out + b:n_out + b + len(ch.out_shapes)]))
    return res[0] if single else list(res[:n_out])


def _mm_nn(a, b, *, name, q=None, out_dtype=F32):
    m, k = a.shape
    n = b.shape[-1]
    tm, tn = _mm_tiles(m, k, n, jnp.dtype(out_dtype).itemsize)

    def body(a_ref, b_ref, o_ref):
        o_ref[...] = _dot(a_ref[...], b_ref[...]).astype(o_ref.dtype)

    return _hosted_call(
        body, grid=(m // tm, n // tn),
        in_specs=[pl.BlockSpec((tm, k), lambda i, j: (i, 0)),
                  pl.BlockSpec((None, k, tn), lambda i, j: (0, 0, j))],
        out_specs=pl.BlockSpec((tm, tn), lambda i, j: (i, j)),
        out_shape=jax.ShapeDtypeStruct((m, n), out_dtype),
        args=(a, b), name=name, q=q, flops=2.0 * m * k * n)


def _mm_nt(a, b, *, name, q=None, out_dtype=F32):
    m, k = a.shape
    n = b.shape[-2]
    tm, tn = _mm_tiles(m, k, n, jnp.dtype(out_dtype).itemsize)

    def body(a_ref, b_ref, o_ref):
        o_ref[...] = _dot_nt(a_ref[...], b_ref[...]).astype(o_ref.dtype)

    return _hosted_call(
        body, grid=(m // tm, n // tn),
        in_specs=[pl.BlockSpec((tm, k), lambda i, j: (i, 0)),
                  pl.BlockSpec((None, tn, k), lambda i, j: (0, j, 0))],
        out_specs=pl.BlockSpec((tm, tn), lambda i, j: (i, j)),
        out_shape=jax.ShapeDtypeStruct((m, n), out_dtype),
        args=(a, b), name=name, q=q, flops=2.0 * m * k * n)


def _mm_nn_slots(a, b4, *, name, q=None, out_dtype=F32):
    m, k = a.shape
    s_, _, _, c = b4.shape
    ob = jnp.dtype(out_dtype).itemsize
    tm = max(t for t in _divisors(m, 256) if 2 * (t * k * 2 + k * c * 2 + t * c * ob) <= MM_VMEM_BUDGET)

    def body(a_ref, b_ref, o_ref):
        o_ref[...] = _dot(a_ref[...], b_ref[...]).astype(o_ref.dtype)

    return _hosted_call(
        body, grid=(m // tm, s_),
        in_specs=[pl.BlockSpec((tm, k), lambda i, j: (i, 0)),
                  pl.BlockSpec((None, None, k, c), lambda i, j: (j, 0, 0, 0))],
        out_specs=pl.BlockSpec((tm, c), lambda i, j: (i, j)),
        out_shape=jax.ShapeDtypeStruct((m, s_ * c), out_dtype),
        args=(a, b4), name=name, q=q, flops=2.0 * m * k * s_ * c)


def _mm_tn_slots(a, b, *, name, slot_cols, n_slots, first_slot=0, q=None, out=None):
    k, m = a.shape
    c = slot_cols
    tk, tm, _ = _mm_tn_tiles(k, m, c, whole_n=True)

    def body(a_ref, b_ref, *rest):
        o_ref = rest[-1]
        part = _dot_tn(a_ref[...], b_ref[...])

        @pl.when(pl.program_id(2) == 0)
        def _():
            o_ref[...] = part

        @pl.when(pl.program_id(2) > 0)
        def _():
            o_ref[...] += part

    in_specs = [pl.BlockSpec((tk, tm), lambda i, j, s: (s, i)), pl.BlockSpec((tk, c), lambda i, j, s: (s, j))]
    args = (a, b)
    if out is not None:
        in_specs.append(pl.BlockSpec(memory_space=pl.ANY))
        args = (a, b, out)
    return _hosted_call(
        body, grid=(m // tm, b.shape[-1] // c, k // tk), in_specs=in_specs,
        out_specs=pl.BlockSpec((None, None, tm, c), lambda i, j, s: (first_slot + j, 0, i, 0)),
        out_shape=jax.ShapeDtypeStruct((n_slots, 1, m, c), F32),
        aliases={2: 0} if out is not None else None,
        args=args, name=name, q=q, flops=2.0 * m * k * b.shape[-1])


def _mm_tn(a, b, *, name, q=None):
    k, m = a.shape
    n = b.shape[-1]
    tk, tm, tn = _mm_tn_tiles(k, m, n)

    def body(a_ref, b_ref, o_ref):
        part = _dot_tn(a_ref[...], b_ref[...])

        @pl.when(pl.program_id(2) == 0)
        def _():
            o_ref[...] = part

        @pl.when(pl.program_id(2) > 0)
        def _():
            o_ref[...] += part

    return _hosted_call(
        body, grid=(m // tm, n // tn, k // tk),
        in_specs=[pl.BlockSpec((tk, tm), lambda i, j, s: (s, i)), pl.BlockSpec((tk, tn), lambda i, j, s: (s, j))],
        out_specs=pl.BlockSpec((None, tm, tn), lambda i, j, s: (0, i, j)),
        out_shape=jax.ShapeDtypeStruct((1, m, n), F32),
        args=(a, b), name=name, q=q, flops=2.0 * m * k * n)


def _mm_ffn_dh(dg, dv, w4, *, name, q=None):
    m, f = dg.shape
    n_slots, _, d, c = w4.shape
    tm, tn = _mm_tiles(m, 2 * f, d, 4)

    def body(dg_ref, dv_ref, *rest):
        w_refs, o_ref = rest[:n_slots], rest[n_slots]
        acc = None
        for s, w_ref in enumerate(w_refs):
            x_ref = dg_ref if s < n_slots // 2 else dv_ref
            off = (s % (n_slots // 2)) * c
            part = _dot_nt(x_ref[:, off:off + c], w_ref[...])
            acc = part if acc is None else acc + part
        o_ref[...] = acc.astype(o_ref.dtype)

    wspec = lambda s: pl.BlockSpec((None, None, tn, c), lambda i, j: (s, 0, j, 0))
    return _hosted_call(
        body, grid=(m // tm, d // tn),
        in_specs=[pl.BlockSpec((tm, f), lambda i, j: (i, 0)),
                  pl.BlockSpec((tm, f), lambda i, j: (i, 0))] + [wspec(s) for s in range(n_slots)],
        out_specs=pl.BlockSpec((tm, tn), lambda i, j: (i, j)),
        out_shape=jax.ShapeDtypeStruct((m, d), MXU),
        args=(dg, dv) + (w4,) * n_slots, name=name, q=q, flops=4.0 * m * f * d)


ROW_TILE = 512


def _norm_fwd(x, g, *, name):
    n, d = x.shape
    tm = _tile(n, ROW_TILE, SUBLANE)

    def body(x_ref, g_ref, o_ref):
        o_ref[...] = _rms_fwd(x_ref[...], g_ref[...]).astype(o_ref.dtype)

    return pl.pallas_call(
        body, grid=(n // tm,),
        in_specs=[pl.BlockSpec((tm, d), lambda i: (i, 0)), pl.BlockSpec((1, d), lambda i: (0, 0))],
        out_specs=pl.BlockSpec((tm, d), lambda i: (i, 0)),
        out_shape=jax.ShapeDtypeStruct((n, d), MXU),
        name=name, compiler_params=_cp((PAR,)))(x, g)


def _norm_bwd_dg(dy, x, g, *, name):
    n, d = x.shape
    tm = _tile(n, ROW_TILE, SUBLANE)

    def body(dy_ref, x_ref, g_ref, dg_ref):
        @pl.when(pl.program_id(0) == 0)
        def _():
            dg_ref[...] = jnp.zeros_like(dg_ref)
        _, dg = _rms_bwd(dy_ref[...], x_ref[...], g_ref[...])
        dg_ref[...] += dg

    return pl.pallas_call(
        body, grid=(n // tm,),
        in_specs=[pl.BlockSpec((tm, d), lambda i: (i, 0)), pl.BlockSpec((tm, d), lambda i: (i, 0)),
                  pl.BlockSpec((1, d), lambda i: (0, 0))],
        out_specs=pl.BlockSpec((1, d), lambda i: (0, 0)),
        out_shape=jax.ShapeDtypeStruct((1, d), F32),
        name=name, compiler_params=_cp((ARB,)))(dy, x, g)


def _resid_norm_fwd(x, y, g_post, g_pres, *, name, q=None):
    n, d = x.shape
    tm = _tile(n, ROW_TILE, SUBLANE)
    nh = len(g_pres)

    def body(x_ref, y_ref, gp_ref, *rest):
        gpre = rest[:nh]
        xo_ref = rest[nh]
        h_refs = rest[nh + 1:]
        xo = x_ref[...] + _rms_fwd(y_ref[...].astype(F32), gp_ref[...])
        xo_ref[...] = xo
        for g_ref, h_ref in zip(gpre, h_refs):
            h_ref[...] = _rms_fwd(xo, g_ref[...]).astype(h_ref.dtype)

    row = pl.BlockSpec((tm, d), lambda i: (i, 0))
    vec = pl.BlockSpec((1, d), lambda i: (0, 0))
    outs = _hosted_call(
        body, grid=(n // tm,),
        in_specs=[row, row, vec] + [vec] * nh,
        out_specs=[row] + [row] * nh,
        out_shape=[jax.ShapeDtypeStruct((n, d), F32)] + [jax.ShapeDtypeStruct((n, d), MXU)] * nh,
        args=(x, y, g_post, *g_pres), name=name, q=q, budget_us=HOST_US["resid"])
    return outs[0], list(outs[1:])


def _loss_fwd(x, y, g_post, target, *, name):
    n, d = x.shape
    tm = _tile(n, ROW_TILE, SUBLANE)

    def body(x_ref, y_ref, gp_ref, t_ref, dx_ref, sq_ref, dy_ref, dg_ref):
        @pl.when(pl.program_id(0) == 0)
        def _():
            sq_ref[...] = jnp.zeros_like(sq_ref)
            dg_ref[...] = jnp.zeros_like(dg_ref)
        y = y_ref[...].astype(F32)
        err = x_ref[...] + _rms_fwd(y, gp_ref[...]) - t_ref[...]
        g = err * (1.0 / d)
        dx_ref[...] = g
        sq_ref[...] += jnp.sum(err * err, axis=0, keepdims=True)
        dy, dg = _rms_bwd(g, y, gp_ref[...])
        dy_ref[...] = dy.astype(dy_ref.dtype)
        dg_ref[...] += dg

    row = pl.BlockSpec((tm, d), lambda i: (i, 0))
    vec = pl.BlockSpec((1, d), lambda i: (0, 0))
    return pl.pallas_call(
        body, grid=(n // tm,),
        in_specs=[row, row, vec, row],
        out_specs=[row, vec, row, vec],
        out_shape=[jax.ShapeDtypeStruct((n, d), F32), jax.ShapeDtypeStruct((1, d), F32),
                   jax.ShapeDtypeStruct((n, d), MXU), jax.ShapeDtypeStruct((1, d), F32)],
        name=name, compiler_params=_cp((ARB,)))(x, y, g_post, target)


def _resid_norm_bwd(dx_out, dhs, x_out, g_pres, y, g_post, *, name, q=None):
    n, d = dx_out.shape
    tm = _tile(n, ROW_TILE, SUBLANE)
    nh = len(dhs)
    has_y = y is not None

    def body(*refs):
        it = iter(refs)
        dxo_ref = next(it)
        dh_refs = [next(it) for _ in range(nh)]
        xo_ref = next(it) if nh else None
        gpre_refs = [next(it) for _ in range(nh)]
        y_ref = next(it) if has_y else None
        gpost_ref = next(it) if has_y else None
        g_out = next(it)
        dy_out = next(it) if has_y else None
        dgpre_out = [next(it) for _ in range(nh)]
        dgpost_out = next(it) if has_y else None

        @pl.when(pl.program_id(0) == 0)
        def _():
            for r in dgpre_out:
                r[...] = jnp.zeros_like(r)
            if has_y:
                dgpost_out[...] = jnp.zeros_like(dgpost_out)

        g = dxo_ref[...]
        if nh:
            xo = xo_ref[...]
            for dh_ref, gp_ref, dg_ref in zip(dh_refs, gpre_refs, dgpre_out):
                dx, dg = _rms_bwd(dh_ref[...].astype(F32), xo, gp_ref[...])
                g = g + dx
                dg_ref[...] += dg
        g_out[...] = g
        if has_y:
            dy, dg = _rms_bwd(g, y_ref[...].astype(F32), gpost_ref[...])
            dy_out[...] = dy.astype(dy_out.dtype)
            dgpost_out[...] += dg

    row = pl.BlockSpec((tm, d), lambda i: (i, 0))
    vec = pl.BlockSpec((1, d), lambda i: (0, 0))
    ins, in_specs = [dx_out], [row]
    ins += list(dhs)
    in_specs += [row] * nh
    if nh:
        ins.append(x_out)
        in_specs.append(row)
    ins += list(g_pres)
    in_specs += [vec] * nh
    if has_y:
        ins += [y, g_post]
        in_specs += [row, vec]
    out_specs, out_shape = [row], [jax.ShapeDtypeStruct((n, d), F32)]
    if has_y:
        out_specs.append(row)
        out_shape.append(jax.ShapeDtypeStruct((n, d), MXU))
    out_specs += [vec] * nh
    out_shape += [jax.ShapeDtypeStruct((1, d), F32)] * nh
    if has_y:
        out_specs.append(vec)
        out_shape.append(jax.ShapeDtypeStruct((1, d), F32))
    outs = list(_hosted_call(
        body, grid=(n // tm,), in_specs=in_specs, out_specs=out_specs, out_shape=out_shape,
        args=tuple(ins), name=name, q=q, budget_us=HOST_US["resid_bwd"]))
    g = outs.pop(0)
    dy = outs.pop(0) if has_y else None
    dgpre = [outs.pop(0) for _ in range(nh)]
    dgpost = outs.pop(0) if has_y else None
    return g, dy, dgpre, dgpost


def _ffn_conv(up, w_ref, b_ref):
    return (w_ref[0:1, :] * _shift_down_edge(up, 2) + w_ref[1:2, :] * _shift_down_edge(up, 1)
            + w_ref[2:3, :] * up + b_ref[...])


def _ffn_act_fwd(up, wconv, bconv, bsz, *, name, q=None):
    n, f2 = up.shape
    f = f2 // 2
    t = n // bsz
    tc = _tile(f, 256)
    nf = f // tc

    def body(ug_ref, uv_ref, wg_ref, wv_ref, bg_ref, bv_ref, o_ref, dag_ref, dav_ref):
        g = _ffn_conv(ug_ref[...].astype(F32), wg_ref, bg_ref)
        v = _ffn_conv(uv_ref[...].astype(F32), wv_ref, bv_ref)
        gl, dgl = _gelu_and_grad(g)
        dag_ref[...] = (v * dgl).astype(dag_ref.dtype)
        dav_ref[...] = gl.astype(dav_ref.dtype)
        o_ref[...] = (gl * v).astype(o_ref.dtype)

    blk = pl.BlockSpec((t, tc), lambda b, j: (b, j))
    return _hosted_call(
        body, grid=(bsz, nf),
        in_specs=[blk, pl.BlockSpec((t, tc), lambda b, j: (b, j + nf)),
                  pl.BlockSpec((FFN_CONV, tc), lambda b, j: (0, j)),
                  pl.BlockSpec((FFN_CONV, tc), lambda b, j: (0, j + nf)),
                  pl.BlockSpec((1, tc), lambda b, j: (0, j)),
                  pl.BlockSpec((1, tc), lambda b, j: (0, j + nf))],
        out_specs=[blk, blk, blk],
        out_shape=[jax.ShapeDtypeStruct((n, f), MXU)] * 3,
        args=(up, up, wconv, wconv, bconv, bconv), name=name, q=q, budget_us=HOST_US["ffn_act"])


def _ffn_act_bwd(up, ug, uv, dact, wconv, bsz, *, name, q=None):
    n, f2 = up.shape
    f = f2 // 2
    t = n // bsz
    tc = _tile(f, 256)
    nf = f // tc

    def body(xg_ref, xv_ref, g_ref, v_ref, da_ref, wg_ref, wv_ref,
             dug_ref, duv_ref, dwg_ref, dwv_ref, dbg_ref, dbv_ref):
        @pl.when(pl.program_id(1) == 0)
        def _():
            for r in (dwg_ref, dwv_ref, dbg_ref, dbv_ref):
                r[...] = jnp.zeros_like(r)

        da = da_ref[...].astype(F32)
        dg = da * g_ref[...].astype(F32)
        dv = da * v_ref[...].astype(F32)

        def conv_bwd(du, w_ref, x_ref, dx_ref, dw_ref, db_ref):
            du1, du2 = _shift_up_edge(du, 1), _shift_up_edge(du, 2)
            dx_ref[...] = (w_ref[2:3, :] * du + w_ref[1:2, :] * du1 + w_ref[0:1, :] * du2).astype(dx_ref.dtype)
            x = x_ref[...].astype(F32)
            dw_ref[0:1, :] += jnp.sum(x * du2, axis=0, keepdims=True)
            dw_ref[1:2, :] += jnp.sum(x * du1, axis=0, keepdims=True)
            dw_ref[2:3, :] += jnp.sum(x * du, axis=0, keepdims=True)
            db_ref[...] += jnp.sum(du, axis=0, keepdims=True)

        conv_bwd(dg, wg_ref, xg_ref, dug_ref, dwg_ref, dbg_ref)
        conv_bwd(dv, wv_ref, xv_ref, duv_ref, dwv_ref, dbv_ref)

    blk = pl.BlockSpec((t, tc), lambda j, b: (b, j))
    wspec = pl.BlockSpec((FFN_CONV, tc), lambda j, b: (0, j))
    bspec = pl.BlockSpec((1, tc), lambda j, b: (0, j))
    outs = _hosted_call(
        body, grid=(nf, bsz),
        in_specs=[blk, pl.BlockSpec((t, tc), lambda j, b: (b, j + nf)), blk, blk, blk,
                  wspec, pl.BlockSpec((FFN_CONV, tc), lambda j, b: (0, j + nf))],
        out_specs=[blk, blk, wspec, wspec, bspec, bspec],
        out_shape=[jax.ShapeDtypeStruct((n, f), MXU), jax.ShapeDtypeStruct((n, f), MXU),
                   jax.ShapeDtypeStruct((FFN_CONV, f), F32), jax.ShapeDtypeStruct((FFN_CONV, f), F32),
                   jax.ShapeDtypeStruct((1, f), F32), jax.ShapeDtypeStruct((1, f), F32)],
        args=(up, up, ug, uv, dact, wconv, wconv), name=name, q=q, budget_us=HOST_US["ffn_act_bwd"])
    dug, duv, dwg, dwv, dbg, dbv = outs
    return dug, duv, jnp.concatenate([dwg, dwv], axis=1), jnp.concatenate([dbg, dbv], axis=1)


def _softmax(s):
    p = jnp.exp(s - jnp.max(s, axis=-1, keepdims=True))
    return p / jnp.sum(p, axis=-1, keepdims=True)


def _mem_attn_fwd(proj, q_col_block, mkv, ycat, bsz, *, name, q=None):
    n = proj.shape[0]
    t = n // bsz
    tq = _tile(t, 512, SUBLANE)
    nt = t // tq
    scale = HEAD_DIM ** -0.5

    def body(q_ref, kv_ref, old_ref, o_ref):
        del old_ref
        heads = range(MEM_HEADS)
        col = lambda ref, h, off=0: ref[:, off + h * HEAD_DIM:off + (h + 1) * HEAD_DIM].astype(MXU)
        ss = [_dot_nt(col(q_ref, h), col(kv_ref, h)) * scale for h in heads]
        ps = [_softmax(s).astype(MXU) for s in ss]
        outs = [_dot(ps[h], col(kv_ref, h, MEM_WIDTH)) for h in heads]
        o_ref[...] = jnp.concatenate(outs, axis=-1).astype(o_ref.dtype)

    return _hosted_call(
        body, grid=(bsz, nt),
        in_specs=[pl.BlockSpec((tq, MEM_WIDTH), lambda b, i: (b * nt + i, q_col_block)),
                  pl.BlockSpec((MEM_LEN, 2 * MEM_WIDTH), lambda b, i: (b, 0)),
                  pl.BlockSpec(memory_space=pl.ANY)],
        out_specs=pl.BlockSpec((tq, MEM_WIDTH), lambda b, i: (b * nt + i, MIX_WIDTH // MEM_WIDTH)),
        out_shape=jax.ShapeDtypeStruct(ycat.shape, ycat.dtype),
        aliases={2: 0}, args=(proj, mkv, ycat), name=name, q=q, budget_us=HOST_US["mem_attn_fwd"])


def _mem_attn_bwd(proj, q_col_block, mkv, dycat, dproj, bsz, *, name, q=None):
    n = proj.shape[0]
    t = n // bsz
    tq = _tile(t, 512, SUBLANE)
    nt = t // tq
    scale = HEAD_DIM ** -0.5

    def body(q_ref, kv_ref, do_ref, old_ref, dq_ref, dkv_ref):
        del old_ref

        @pl.when(pl.program_id(1) == 0)
        def _():
            dkv_ref[...] = jnp.zeros_like(dkv_ref)

        heads = range(MEM_HEADS)
        col = lambda ref, h, off=0: ref[:, off + h * HEAD_DIM:off + (h + 1) * HEAD_DIM].astype(MXU)
        qs = [col(q_ref, h) for h in heads]
        ks = [col(kv_ref, h) for h in heads]
        dos = [col(do_ref, h) for h in heads]
        ps = [_softmax(_dot_nt(qs[h], ks[h]) * scale) for h in heads]
        dps = [_dot_nt(dos[h], col(kv_ref, h, MEM_WIDTH)) for h in heads]
        dss = [(ps[h] * (dps[h] - jnp.sum(dps[h] * ps[h], axis=-1, keepdims=True)) * scale).astype(MXU) for h in heads]
        dvs = [_dot_tn(ps[h].astype(MXU), dos[h]) for h in heads]
        dqs = [_dot(dss[h], ks[h]) for h in heads]
        dks = [_dot_tn(dss[h], qs[h]) for h in heads]
        dq_ref[...] = jnp.concatenate(dqs, axis=-1).astype(dq_ref.dtype)
        dkv_ref[...] += jnp.concatenate(dks + dvs, axis=-1)

    return _hosted_call(
        body, grid=(bsz, nt),
        in_specs=[pl.BlockSpec((tq, MEM_WIDTH), lambda b, i: (b * nt + i, q_col_block)),
                  pl.BlockSpec((MEM_LEN, 2 * MEM_WIDTH), lambda b, i: (b, 0)),
                  pl.BlockSpec((tq, MEM_WIDTH), lambda b, i: (b * nt + i, MIX_WIDTH // MEM_WIDTH)),
                  pl.BlockSpec(memory_space=pl.ANY)],
        out_specs=[pl.BlockSpec((tq, MEM_WIDTH), lambda b, i: (b * nt + i, q_col_block)),
                   pl.BlockSpec((MEM_LEN, 2 * MEM_WIDTH), lambda b, i: (b, 0))],
        out_shape=[jax.ShapeDtypeStruct(dproj.shape, dproj.dtype),
                   jax.ShapeDtypeStruct((bsz * MEM_LEN, 2 * MEM_WIDTH), F32)],
        aliases={3: 0}, args=(proj, mkv, dycat, dproj), name=name, q=q, budget_us=HOST_US["mem_attn_bwd"])


def _swa_probs(s, h, dist, mask, sink):
    s = jnp.where(mask, s * (HEAD_DIM ** -0.5) - SLOPES[h] * dist, -jnp.inf)
    m = jnp.maximum(jnp.max(s, axis=-1, keepdims=True), sink)
    p = jnp.exp(s - m)
    psink = jnp.exp(sink - m)
    inv = 1.0 / (jnp.sum(p, axis=-1, keepdims=True) + psink)
    return p * inv, psink * inv


def _swa_mask(n):
    qi = lax.broadcasted_iota(jnp.int32, (WINDOW, 2 * WINDOW), 0) + WINDOW
    ki = lax.broadcasted_iota(jnp.int32, (WINDOW, 2 * WINDOW), 1)
    dist = qi - ki
    mask = (dist >= 0) & (dist < WINDOW) & ((n > 0) | (ki >= WINDOW))
    return dist.astype(F32), mask


def _swa_fwd(proj, kv, sinks, bsz, *, name, q=None):
    n_tok = proj.shape[0]
    nb = n_tok // bsz // WINDOW
    kvw = SWA_KV_HEADS * HEAD_DIM

    def body(sink_ref, q_ref, kvp_ref, kvc_ref, o_ref):
        n = pl.program_id(1)
        dist, mask = _swa_mask(n)
        kk = jnp.concatenate([kvp_ref[:, :kvw], kvc_ref[:, :kvw]], axis=0).astype(MXU)
        vv = jnp.concatenate([kvp_ref[:, kvw:], kvc_ref[:, kvw:]], axis=0).astype(MXU)
        heads = range(SWA_HEADS)
        group = lambda x, h: x[:, (h // SWA_GROUP) * HEAD_DIM:(h // SWA_GROUP + 1) * HEAD_DIM]
        ss = [_dot_nt(q_ref[:, h * HEAD_DIM:(h + 1) * HEAD_DIM].astype(MXU), group(kk, h)) for h in heads]
        ps = [_swa_probs(ss[h], h, dist, mask, sink_ref[h])[0].astype(MXU) for h in heads]
        outs = [_dot(ps[h], group(vv, h)) for h in heads]
        o_ref[...] = jnp.concatenate(outs, axis=-1).astype(o_ref.dtype)

    return _hosted_call(
        body, grid=(bsz, nb),
        in_specs=[pl.BlockSpec(memory_space=pltpu.SMEM),
                  pl.BlockSpec((WINDOW, MIX_WIDTH), lambda b, n: (b * nb + n, 0)),
                  pl.BlockSpec((WINDOW, 2 * kvw), lambda b, n: (b * nb + jnp.maximum(n - 1, 0), 0)),
                  pl.BlockSpec((WINDOW, 2 * kvw), lambda b, n: (b * nb + n, 0))],
        out_specs=pl.BlockSpec((WINDOW, MIX_WIDTH), lambda b, n: (b * nb + n, 0)),
        out_shape=jax.ShapeDtypeStruct((n_tok, D_MODEL), MXU),
        args=(sinks, proj, kv, kv), name=name, q=q, budget_us=HOST_US["swa_fwd"])


def _swa_bwd(proj, kv, sinks, dycat, bsz, *, name, q=None):
    n_tok = proj.shape[0]
    nb = n_tok // bsz // WINDOW
    kvw = SWA_KV_HEADS * HEAD_DIM

    def body(sink_ref, q_ref, kvp_ref, kvc_ref, do_ref, dq_ref, dkvc_ref, dkvp_ref, dsink_ref):
        n = pl.program_id(1)

        @pl.when((pl.program_id(0) == 0) & (n == 0))
        def _():
            dsink_ref[...] = jnp.zeros_like(dsink_ref)

        dist, mask = _swa_mask(n)
        kk = jnp.concatenate([kvp_ref[:, :kvw], kvc_ref[:, :kvw]], axis=0).astype(MXU)
        vv = jnp.concatenate([kvp_ref[:, kvw:], kvc_ref[:, kvw:]], axis=0).astype(MXU)
        lane = lax.broadcasted_iota(jnp.int32, (SUBLANE, LANE), 1)
        heads = range(SWA_HEADS)
        group = lambda x, h: x[:, (h // SWA_GROUP) * HEAD_DIM:(h // SWA_GROUP + 1) * HEAD_DIM]
        qs = [q_ref[:, h * HEAD_DIM:(h + 1) * HEAD_DIM].astype(MXU) for h in heads]
        dos = [do_ref[:, h * HEAD_DIM:(h + 1) * HEAD_DIM].astype(MXU) for h in heads]
        ss = [_dot_nt(qs[h], group(kk, h)) for h in heads]
        dps = [_dot_nt(dos[h], group(vv, h)) for h in heads]
        probs = [_swa_probs(ss[h], h, dist, mask, sink_ref[h]) for h in heads]
        rss = [jnp.sum(dps[h] * probs[h][0], axis=-1, keepdims=True) for h in heads]
        dss = [(probs[h][0] * (dps[h] - rss[h]) * (HEAD_DIM ** -0.5)).astype(MXU) for h in heads]
        dqs = [_dot(dss[h], group(kk, h)) for h in heads]
        dk_h = [_dot_tn(dss[h], qs[h]) for h in heads]
        dv_h = [_dot_tn(probs[h][0].astype(MXU), dos[h]) for h in heads]
        dsink = jnp.zeros((SUBLANE, LANE), F32)
        for h in heads:
            dsink = dsink + jnp.where(lane == h, jnp.sum(-probs[h][1] * rss[h], axis=0, keepdims=True), 0.0)
        sum_group = lambda xs, c: functools.reduce(lambda a, b: a + b, xs[c * SWA_GROUP:(c + 1) * SWA_GROUP])
        dks = [sum_group(dk_h, c) for c in range(SWA_KV_HEADS)]
        dvs = [sum_group(dv_h, c) for c in range(SWA_KV_HEADS)]
        dq_ref[...] = jnp.concatenate(dqs, axis=-1).astype(dq_ref.dtype)
        dkv = jnp.concatenate(dks + dvs, axis=-1)
        dkvp_ref[...] = dkv[:WINDOW]
        dkvc_ref[...] = dkv[WINDOW:]
        dsink_ref[...] += dsink

    qspec = pl.BlockSpec((WINDOW, MIX_WIDTH), lambda b, n: (b * nb + n, 0))
    kvspec = pl.BlockSpec((WINDOW, 2 * kvw), lambda b, n: (b * nb + n, 0))
    return _hosted_call(
        body, grid=(bsz, nb),
        in_specs=[pl.BlockSpec(memory_space=pltpu.SMEM), qspec,
                  pl.BlockSpec((WINDOW, 2 * kvw), lambda b, n: (b * nb + jnp.maximum(n - 1, 0), 0)),
                  kvspec, qspec],
        out_specs=[qspec, kvspec, kvspec, pl.BlockSpec((SUBLANE, LANE), lambda b, n: (0, 0))],
        out_shape=[jax.ShapeDtypeStruct((n_tok, D_MODEL), MXU),
                   jax.ShapeDtypeStruct((n_tok, 2 * kvw), F32),
                   jax.ShapeDtypeStruct((n_tok, 2 * kvw), F32),
                   jax.ShapeDtypeStruct((SUBLANE, LANE), F32)],
        args=(sinks, proj, kv, kv, dycat), name=name, q=q, budget_us=HOST_US["swa_bwd"])


def _swa_dkv_combine(curs, prevs, bsz, *, name):
    n_tok, w = curs[0].shape
    nb = n_tok // bsz // WINDOW
    k = len(curs)

    def body(*refs):
        o_ref = refs[-1]
        n = pl.program_id(1)
        acc = refs[0][...]
        for r in refs[1:k]:
            acc = acc + r[...]
        nxt = refs[k][...]
        for r in refs[k + 1:2 * k]:
            nxt = nxt + r[...]
        o_ref[...] = (acc + jnp.where(n < nb - 1, nxt, 0.0)).astype(o_ref.dtype)

    cur = pl.BlockSpec((WINDOW, w), lambda b, n: (b * nb + n, 0))
    prv = pl.BlockSpec((WINDOW, w), lambda b, n: (b * nb + jnp.minimum(n + 1, nb - 1), 0))
    return pl.pallas_call(
        body, grid=(bsz, nb), in_specs=[cur] * k + [prv] * k, out_specs=cur,
        out_shape=jax.ShapeDtypeStruct((n_tok, w), MXU),
        name=name, compiler_params=_cp((PAR, PAR)))(*curs, *prevs)


def _lru_gates(ux, halo, ext_ref, wc_ref, bc_ref, wr_ref, br_ref, wi_ref, bi_ref, lam_ref):
    tt = ux.shape[0]
    ext_ref[0:SUBLANE, :] = halo
    ext_ref[SUBLANE:, :] = ux
    xs = [ux] + [ext_ref[pl.ds(SUBLANE - k, tt), :] for k in range(1, LRU_CONV)]
    xc = bc_ref[...] + wc_ref[3:4, :] * xs[0] + wc_ref[2:3, :] * xs[1] + wc_ref[1:2, :] * xs[2] + wc_ref[0:1, :] * xs[3]
    pre_r, pre_i = [], []
    for blk in range(MIX_WIDTH // GATE_TILE):
        xb = xc[:, blk * GATE_TILE:(blk + 1) * GATE_TILE].astype(MXU)
        pre_r.append(_dot(xb, wr_ref[blk]))
        pre_i.append(_dot(xb, wi_ref[blk]))
    r = jax.nn.sigmoid(jnp.concatenate(pre_r, axis=-1) + br_ref[...])
    i = jax.nn.sigmoid(jnp.concatenate(pre_i, axis=-1) + bi_ref[...])
    nlam = -lam_ref[...]
    sp = jnp.maximum(nlam, 0.0) + jnp.log(1.0 + jnp.exp(-jnp.abs(nlam)))
    log_a = -LRU_C * r * sp
    a = jnp.exp(log_a)
    om = -jnp.tanh(log_a) * (a * a + 1.0)
    s = jnp.sqrt(om)
    return xs, xc, r, i, sp, a, s


def _lru_fwd(proj, wconv, bconv, wr, br, wi, bi, lam, bsz, *, name, q=None):
    n_tok = proj.shape[0]
    t = n_tok // bsz
    tt = _tile(t, 256, SUBLANE)
    nt = t // tt
    w = MIX_WIDTH
    ng = tt // SUBLANE

    def body(pg_ref, halo_ref, wc_ref, bc_ref, wr_ref, br_ref, wi_ref, bi_ref, lam_ref,
             y_ref, h_ref, ext_ref, a_ref, b_ref, carry_ref):
        ti = pl.program_id(1)

        @pl.when(ti == 0)
        def _():
            carry_ref[...] = jnp.zeros_like(carry_ref)

        gate = pg_ref[:, :w]
        ux = pg_ref[:, w:]
        halo = jnp.where(ti > 0, halo_ref[...], 0.0)
        _, xc, _, i, _, a, s = _lru_gates(ux, halo, ext_ref, wc_ref, bc_ref, wr_ref, br_ref, wi_ref, bi_ref, lam_ref)
        a_ref[...] = a
        b_ref[...] = s * (i * xc)
        row = lax.broadcasted_iota(jnp.int32, (SUBLANE, w), 0)

        def group(g, hprev):
            off = pl.multiple_of(g * SUBLANE, SUBLANE)
            ca = a_ref[pl.ds(off, SUBLANE), :]
            cb = b_ref[pl.ds(off, SUBLANE), :]
            for d in (1, 2, 4):
                a_sh = jnp.where(row >= d, pltpu.roll(ca, d, axis=0), 1.0)
                b_sh = jnp.where(row >= d, pltpu.roll(cb, d, axis=0), 0.0)
                cb = ca * b_sh + cb
                ca = ca * a_sh
            h = ca * hprev + cb
            b_ref[pl.ds(off, SUBLANE), :] = h
            return jnp.broadcast_to(h[SUBLANE - 1:SUBLANE, :], (SUBLANE, w))

        carry_ref[...] = lax.fori_loop(0, ng, group, carry_ref[...])
        h = b_ref[...]
        h_ref[...] = h
        y_ref[...] = (h * _gelu(gate)).astype(y_ref.dtype)

    vec = lambda r: pl.BlockSpec((r, w), lambda b, i: (0, 0))
    wspec = pl.BlockSpec((w // GATE_TILE, GATE_TILE, GATE_TILE), lambda b, i: (0, 0, 0))
    hb = tt // SUBLANE
    return _hosted_call(
        body, grid=(bsz, nt),
        in_specs=[pl.BlockSpec((tt, 2 * w), lambda b, i: (b * nt + i, 0)),
                  pl.BlockSpec((SUBLANE, w), lambda b, i: (jnp.maximum((b * nt + i) * hb - 1, 0), 1)),
                  vec(LRU_CONV), vec(1), wspec, vec(1), wspec, vec(1), vec(1)],
        out_specs=[pl.BlockSpec((tt, w), lambda b, i: (b * nt + i, 0)),
                   pl.BlockSpec((tt, w), lambda b, i: (b * nt + i, 0))],
        out_shape=[jax.ShapeDtypeStruct((n_tok, D_MODEL), MXU), jax.ShapeDtypeStruct((n_tok, w), F32)],
        scratch_shapes=[pltpu.VMEM((tt + SUBLANE, w), F32), pltpu.VMEM((tt, w), F32),
                        pltpu.VMEM((tt, w), F32), pltpu.VMEM((SUBLANE, w), F32)],
        args=(proj, proj, wconv, bconv, wr, br, wi, bi, lam), name=name, q=q, budget_us=HOST_US["lru_fwd"])


def _lru_bwd(proj, hs, dycat, wconv, bconv, wr, br, wi, bi, lam, bsz, *, name, q=None):
    n_tok = proj.shape[0]
    t = n_tok // bsz
    tt = _tile(t, 256, SUBLANE)
    nt = t // tt
    w = MIX_WIDTH
    ng = tt // SUBLANE
    nblk = w // GATE_TILE

    def body(pg_ref, halo_ref, h_ref, hhalo_ref, dy_ref, wc_ref, bc_ref, wr_ref, br_ref, wi_ref, bi_ref, lam_ref,
             dp_ref, dwc_ref, dbc_ref, dwr_ref, dbr_ref, dwi_ref, dbi_ref, dlam_ref,
             ext_ref, a_ref, c_ref, g_ref, gcarry_ref, xcarry_ref):
        bi_ = pl.program_id(0)
        ti = nt - 1 - pl.program_id(1)

        @pl.when((bi_ == 0) & (pl.program_id(1) == 0))
        def _():
            for r in (dwc_ref, dbc_ref, dwr_ref, dbr_ref, dwi_ref, dbi_ref, dlam_ref):
                r[...] = jnp.zeros_like(r)

        @pl.when(pl.program_id(1) == 0)
        def _():
            gcarry_ref[...] = jnp.zeros_like(gcarry_ref)
            xcarry_ref[...] = jnp.zeros_like(xcarry_ref)

        gate = pg_ref[:, :w]
        ux = pg_ref[:, w:]
        halo = jnp.where(ti > 0, halo_ref[...], 0.0)
        xs, xc, r, i, sp, a, s = _lru_gates(ux, halo, ext_ref, wc_ref, bc_ref, wr_ref, br_ref, wi_ref, bi_ref, lam_ref)
        h = h_ref[...]
        gl, dgl = _gelu_and_grad(gate)
        dy = dy_ref[...].astype(F32)
        dgate = dy * h * dgl
        row_t = lax.broadcasted_iota(jnp.int32, (tt, w), 0)
        g_ref[...] = dy * gl + jnp.where(row_t == tt - 1, gcarry_ref[0:1, :], 0.0)
        c_ref[...] = _shift_up(a, 1, row_t)
        row = lax.broadcasted_iota(jnp.int32, (SUBLANE, w), 0)

        a_ref[...] = a

        def group(k, gnext):
            off = pl.multiple_of((ng - 1 - k) * SUBLANE, SUBLANE)
            cc = c_ref[pl.ds(off, SUBLANE), :]
            cb = g_ref[pl.ds(off, SUBLANE), :]
            cb = cb + jnp.where(row == SUBLANE - 1, gnext, 0.0)
            cc = jnp.where(row == SUBLANE - 1, 0.0, cc)
            for d in (1, 2, 4):
                c_sh = jnp.where(row < SUBLANE - d, pltpu.roll(cc, SUBLANE - d, axis=0), 1.0)
                b_sh = jnp.where(row < SUBLANE - d, pltpu.roll(cb, SUBLANE - d, axis=0), 0.0)
                cb = cc * b_sh + cb
                cc = cc * c_sh
            g_ref[pl.ds(off, SUBLANE), :] = cb
            a0 = a_ref[pl.ds(off, SUBLANE), :]
            return jnp.broadcast_to(a0[0:1, :] * cb[0:1, :], (SUBLANE, w))

        gc = lax.fori_loop(0, ng, group, jnp.zeros((SUBLANE, w), F32))
        gcarry_ref[...] = gc
        gsc = g_ref[...]

        hhalo = jnp.where(ti > 0, hhalo_ref[SUBLANE - 1:SUBLANE, :], 0.0)
        hprev = jnp.where(row_t == 0, hhalo, pltpu.roll(h, 1, axis=0))
        gated = i * xc
        d_gated = gsc * s
        d_atot = gsc * hprev - (gsc * gated) * a / s
        d_loga = d_atot * a
        d_r = d_loga * (-LRU_C) * sp
        dlam_ref[...] += jnp.sum(d_loga * r, axis=0, keepdims=True) * (LRU_C * jax.nn.sigmoid(-lam_ref[...]))
        d_i = d_gated * xc
        d_xc = d_gated * i
        d_pr = d_r * r * (1.0 - r)
        d_pi = d_i * i * (1.0 - i)
        dbr_ref[...] += jnp.sum(d_pr, axis=0, keepdims=True)
        dbi_ref[...] += jnp.sum(d_pi, axis=0, keepdims=True)
        extra = []
        for blk in range(nblk):
            sl = slice(blk * GATE_TILE, (blk + 1) * GATE_TILE)
            xb = xc[:, sl].astype(MXU)
            dr_b = d_pr[:, sl].astype(MXU)
            di_b = d_pi[:, sl].astype(MXU)
            dwr_ref[blk] += _dot_tn(xb, dr_b)
            dwi_ref[blk] += _dot_tn(xb, di_b)
            extra.append(_dot_nt(dr_b, wr_ref[blk]) + _dot_nt(di_b, wi_ref[blk]))
        d_xc = d_xc + jnp.concatenate(extra, axis=-1)
        dbc_ref[...] += jnp.sum(d_xc, axis=0, keepdims=True)
        for k in range(LRU_CONV):
            dwc_ref[k:k + 1, :] += jnp.sum(d_xc * xs[LRU_CONV - 1 - k], axis=0, keepdims=True)
        ext_ref[0:tt, :] = d_xc
        ext_ref[tt:, :] = xcarry_ref[...]
        dux = wc_ref[3:4, :] * d_xc
        for k in range(LRU_CONV - 1):
            dux = dux + wc_ref[k:k + 1, :] * ext_ref[pl.ds(LRU_CONV - 1 - k, tt), :]
        xcarry_ref[...] = d_xc[0:SUBLANE, :]
        dp_ref[:, :w] = dgate.astype(dp_ref.dtype)
        dp_ref[:, w:] = dux.astype(dp_ref.dtype)

    vec = lambda r: pl.BlockSpec((r, w), lambda b, i: (0, 0))
    wspec = pl.BlockSpec((nblk, GATE_TILE, GATE_TILE), lambda b, i: (0, 0, 0))
    hb = tt // SUBLANE
    rblk = lambda b, i: b * nt + (nt - 1 - i)
    halo_idx = lambda b, i: jnp.maximum(rblk(b, i) * hb - 1, 0)
    wide = pl.BlockSpec((tt, 2 * w), lambda b, i: (rblk(b, i), 0))
    narrow = pl.BlockSpec((tt, w), lambda b, i: (rblk(b, i), 0))
    return _hosted_call(
        body, grid=(bsz, nt),
        in_specs=[wide, pl.BlockSpec((SUBLANE, w), lambda b, i: (halo_idx(b, i), 1)),
                  narrow, pl.BlockSpec((SUBLANE, w), lambda b, i: (halo_idx(b, i), 0)), narrow,
                  vec(LRU_CONV), vec(1), wspec, vec(1), wspec, vec(1), vec(1)],
        out_specs=[wide, vec(LRU_CONV), vec(1), wspec, vec(1), wspec, vec(1), vec(1)],
        out_shape=[jax.ShapeDtypeStruct((n_tok, 2 * w + MEM_WIDTH), MXU),
                   jax.ShapeDtypeStruct((LRU_CONV, w), F32), jax.ShapeDtypeStruct((1, w), F32),
                   jax.ShapeDtypeStruct((nblk, GATE_TILE, GATE_TILE), F32), jax.ShapeDtypeStruct((1, w), F32),
                   jax.ShapeDtypeStruct((nblk, GATE_TILE, GATE_TILE), F32), jax.ShapeDtypeStruct((1, w), F32),
                   jax.ShapeDtypeStruct((1, w), F32)],
        scratch_shapes=[pltpu.VMEM((tt + SUBLANE, w), F32), pltpu.VMEM((tt, w), F32), pltpu.VMEM((tt, w), F32),
                        pltpu.VMEM((tt, w), F32), pltpu.VMEM((SUBLANE, w), F32), pltpu.VMEM((SUBLANE, w), F32)],
        args=(proj, proj, hs, hs, dycat, wconv, bconv, wr, br, wi, bi, lam), name=name, q=q,
        budget_us=HOST_US["lru_bwd"])


def _gate_tiles(w):
    per = GATE_TILE // HEAD_DIM
    w4 = w.reshape(LRU_BLOCKS // per, per, HEAD_DIM, HEAD_DIM)
    eye = jnp.eye(per, dtype=w.dtype)
    return jnp.einsum("bnij,nm->bnimj", w4, eye).reshape(LRU_BLOCKS // per, GATE_TILE, GATE_TILE)


def _gate_blocks(t):
    per = GATE_TILE // HEAD_DIM
    t5 = t.reshape(LRU_BLOCKS // per, per, HEAD_DIM, per, HEAD_DIM)
    eye = jnp.eye(per, dtype=t.dtype)
    return jnp.einsum("bnimj,nm->bnij", t5, eye).reshape(LRU_BLOCKS, HEAD_DIM, HEAD_DIM)


def _row(v):
    return v.reshape(1, -1)


def _local_step(x, mem, target, p, wfull, push_grad, q):
    bsz, t, d = x.shape
    n = bsz * t
    x2d = x.reshape(n, d)
    tgt = target.reshape(n, d)
    mem2d = mem.reshape(bsz * MEM_LEN, d)
    wr_t = [_gate_tiles(p["w_rg_r"][j]).astype(MXU) for j in range(N_A)]
    wi_t = [_gate_tiles(p["w_rg_i"][j]).astype(MXU) for j in range(N_A)]

    mn = [_norm_fwd(mem2d, _row(p["g_mem"][l]), name=f"mem_norm{l}") for l in range(DEPTH)]
    mkv = [None] * DEPTH
    h = _norm_fwd(x2d, _row(p["g_mix_pre"][0]), name="in_norm")
    xin = x2d
    sv = []
    kv = hkv = None
    for l in range(DEPTH):
        s = {"xin": xin, "h": h}
        if q is not None:
            q.horizon = (l + 2) * GROUPS_PER_LAYER
        mkv[l] = _mm_nn(mn[l], wfull("w_mem_kv", l), name=f"mem_kv{l}", q=q)
        if l < N_A:
            proj = _mm_nn(h, wfull("w_in_a", l), name=f"in_proj{l}", q=q)
            ycat, hs = _lru_fwd(proj, p["w_conv_a"][l], _row(p["b_conv_a"][l]), wr_t[l], _row(p["b_rg_r"][l]),
                                wi_t[l], _row(p["b_rg_i"][l]), _row(p["lru_lambda"][l]), bsz, name=f"lru_fwd{l}", q=q)
            s["hs"] = hs
            qblk = 2 * MIX_WIDTH // MEM_WIDTH
        else:
            if l == N_A:
                kv = _mm_nn(hkv, wfull("w_kv", 0), name="kv_proj", q=q)
            proj = _mm_nn(h, wfull("w_in_b", l - N_A), name=f"in_proj{l}", q=q)
            ycat = _swa_fwd(proj, kv, p["sinks_b"][l - N_A], bsz, name=f"swa_fwd{l}", q=q)
            qblk = MIX_WIDTH // MEM_WIDTH
        ycat = _mem_attn_fwd(proj, qblk, mkv[l], ycat, bsz, name=f"mem_attn_fwd{l}", q=q)
        y = _mm_nn(ycat, wfull("w_mix_out", l), name=f"mix_out{l}", q=q, out_dtype=MXU)
        x1, (h2,) = _resid_norm_fwd(xin, y, _row(p["g_mix_post"][l]), [_row(p["g_ffn_pre"][l])], name=f"mix_resid{l}", q=q)
        up = _mm_nn_slots(h2, wfull("w_ffn_up", l), name=f"ffn_up{l}", q=q, out_dtype=MXU)
        act, ug, uv = _ffn_act_fwd(up, p["w_ffn_conv"][l], _row(p["b_ffn_conv"][l]), bsz, name=f"ffn_act{l}", q=q)
        f = _mm_nn(act, wfull("w_ffn_down", l), name=f"ffn_down{l}", q=q, out_dtype=MXU)
        s.update(proj=proj, qblk=qblk, ycat=ycat, y=y, x1=x1, h2=h2, up=up, ug=ug, uv=uv, act=act, f=f)
        sv.append(s)
        if l < DEPTH - 1:
            g_pres = [_row(p["g_mix_pre"][l + 1])] + ([_row(p["g_kv"])] if l + 1 == N_A else [])
            xin, hn = _resid_norm_fwd(x1, f, _row(p["g_ffn_post"][l]), g_pres, name=f"ffn_resid{l}", q=q)
            h = hn[0]
            if l + 1 == N_A:
                hkv = hn[1]
        else:
            g_tot, sq, df, g_post_last = _loss_fwd(x1, f, _row(p["g_ffn_post"][l]), tgt, name="loss")

    if q is not None:
        q.horizon = LAST_GROUP
    gs = {k: [None] * DEPTH for k in ("g_mix_pre", "g_mix_post", "g_ffn_pre", "g_ffn_post", "g_mem",
                                       "w_ffn_conv", "b_ffn_conv")}
    ga = {k: [None] * N_A for k in ("w_conv_a", "b_conv_a", "w_rg_r", "b_rg_r", "w_rg_i", "b_rg_i", "lru_lambda")}
    gsink = [None] * (DEPTH - N_A)
    dkv_cur, dkv_prev = [], []
    gs["g_ffn_post"][DEPTH - 1] = g_post_last
    grad_x = None
    for l in reversed(range(DEPTH)):
        s = sv[l]
        dact = _mm_nt(df, wfull("w_ffn_down", l), name=f"d_act{l}", q=q, out_dtype=MXU)
        push_grad("w_ffn_down", l, _mm_tn(s["act"], df, name=f"dw_down{l}", q=q))
        dug, duv, gs["w_ffn_conv"][l], gs["b_ffn_conv"][l] = _ffn_act_bwd(
            s["up"], s["ug"], s["uv"], dact, p["w_ffn_conv"][l], bsz, name=f"ffn_act_bwd{l}", q=q)
        dh2 = _mm_ffn_dh(dug, duv, wfull("w_ffn_up", l), name=f"d_h2_{l}", q=q)
        up_slots = dict(slot_cols=2 * D_FF // N_CHIP, n_slots=N_CHIP)
        dwu = _mm_tn_slots(s["h2"], dug, name=f"dw_up_g{l}", q=q, **up_slots)
        push_grad("w_ffn_up", l, _mm_tn_slots(s["h2"], duv, name=f"dw_up_v{l}", q=q, out=dwu,
                                              first_slot=N_CHIP // 2, **up_slots))
        g1, dy, (gs["g_ffn_pre"][l],), gs["g_mix_post"][l] = _resid_norm_bwd(
            g_tot, [dh2], s["x1"], [_row(p["g_ffn_pre"][l])], s["y"], _row(p["g_mix_post"][l]), name=f"mix_resid_bwd{l}", q=q)
        dycat = _mm_nt(dy, wfull("w_mix_out", l), name=f"d_ycat{l}", q=q, out_dtype=MXU)
        push_grad("w_mix_out", l, _mm_tn(s["ycat"], dy, name=f"dw_mix_out{l}", q=q))
        if l < N_A:
            dproj, dwc, dbc, dwr, dbr, dwi, dbi, dlam = _lru_bwd(
                s["proj"], s["hs"], dycat, p["w_conv_a"][l], _row(p["b_conv_a"][l]), wr_t[l], _row(p["b_rg_r"][l]),
                wi_t[l], _row(p["b_rg_i"][l]), _row(p["lru_lambda"][l]), bsz, name=f"lru_bwd{l}", q=q)
            ga["w_conv_a"][l], ga["b_conv_a"][l], ga["lru_lambda"][l] = dwc, dbc[0], dlam[0]
            ga["w_rg_r"][l], ga["w_rg_i"][l] = _gate_blocks(dwr), _gate_blocks(dwi)
            ga["b_rg_r"][l] = dbr.reshape(LRU_BLOCKS, HEAD_DIM)
            ga["b_rg_i"][l] = dbi.reshape(LRU_BLOCKS, HEAD_DIM)
            w_in, j = "w_in_a", l
        else:
            dproj, dc, dp_, dsk = _swa_bwd(s["proj"], kv, p["sinks_b"][l - N_A], dycat, bsz, name=f"swa_bwd{l}", q=q)
            dkv_cur.append(dc)
            dkv_prev.append(dp_)
            gsink[l - N_A] = dsk[0, :SWA_HEADS]
            w_in, j = "w_in_b", l - N_A
        dproj, dmkv = _mem_attn_bwd(s["proj"], s["qblk"], mkv[l], dycat, dproj, bsz, name=f"mem_attn_bwd{l}", q=q)
        dh = _mm_nt(dproj, wfull(w_in, j), name=f"d_h{l}", q=q, out_dtype=MXU)
        push_grad(w_in, j, _mm_tn(s["h"], dproj, name=f"dw_in{l}", q=q))
        dmkv = dmkv.astype(MXU)
        dmn = _mm_nt(dmkv, wfull("w_mem_kv", l), name=f"d_mem_norm{l}", q=q)
        push_grad("w_mem_kv", l, _mm_tn(mn[l], dmkv, name=f"dw_mem_kv{l}", q=q))
        gs["g_mem"][l] = _norm_bwd_dg(dmn, mem2d, _row(p["g_mem"][l]), name=f"mem_norm_bwd{l}")
        dhs, g_pres = [dh], [_row(p["g_mix_pre"][l])]
        if l == N_A:
            dkv = _swa_dkv_combine(dkv_cur, dkv_prev, bsz, name="dkv_combine")
            dhs.append(_mm_nt(dkv, wfull("w_kv", 0), name="d_hkv", q=q, out_dtype=MXU))
            g_pres.append(_row(p["g_kv"]))
            push_grad("w_kv", 0, _mm_tn(hkv, dkv, name="dw_kv", q=q))
        if l > 0:
            g_tot, df, dgpre, gs["g_ffn_post"][l - 1] = _resid_norm_bwd(
                g1, dhs, s["xin"], g_pres, sv[l - 1]["f"], _row(p["g_ffn_post"][l - 1]), name=f"ffn_resid_bwd{l - 1}", q=q)
        else:
            grad_x, _, dgpre, _ = _resid_norm_bwd(g1, dhs, s["xin"], g_pres, None, None, name="in_norm_bwd", q=q)
        gs["g_mix_pre"][l] = dgpre[0]
        if l == N_A:
            g_kv = dgpre[1][0]

    grads = {}
    for k in ("g_mix_pre", "g_mix_post", "g_ffn_pre", "g_ffn_post", "g_mem", "b_ffn_conv"):
        grads[k] = jnp.concatenate(gs[k], axis=0)
    grads["w_ffn_conv"] = jnp.stack(gs["w_ffn_conv"])
    for k, v in ga.items():
        grads[k] = jnp.stack(v)
    grads["sinks_b"] = jnp.stack(gsink)
    grads["g_kv"] = g_kv
    return jnp.sum(sq), grad_x.reshape(bsz, t, d), grads


N_CHIP = 4
HALF_ALIGN = 16
CARRIED_DMA_PRIORITY = 1
D2D_STREAMS = 2
MIN_PART_BYTES = 128 * 1024


def _full_shape(kind, shard_shape):
    l, r, c = shard_shape
    return {"row": (l, N_CHIP * r, c), "slot": (N_CHIP, l, r, c)}[kind]


def _slot_view(ref, kind, shard_shape, s, hf, sub=(0, 1)):
    _, r, _ = shard_shape
    rh = r // 2
    if hf is None:
        size = r // sub[1]
        start = sub[0] * size
    else:
        size = rh // sub[1]
        start = hf * rh + sub[0] * size
    if kind == "row":
        start = s * r + start
    if not isinstance(start, int):
        start = pl.multiple_of(start, HALF_ALIGN)
    rows = pl.ds(start, size)
    if kind == "row":
        return ref.at[:, rows, :]
    return ref.at[s, :, rows, :]


def _half_view(ref, shard_shape, hf, sub=(0, 1)):
    rh = shard_shape[1] // 2
    size = rh // sub[1]
    return ref.at[:, pl.ds(pl.multiple_of(hf * rh + sub[0] * size, HALF_ALIGN), size), :]


def _mesh_pos():
    return lax.axis_index("x"), lax.axis_index("y"), lax.axis_index("c")


def _other_chips(x, y):
    return [(1 - x, y), (x, 1 - y), (1 - x, 1 - y)]


ICI_BYTES_PER_US = 6.0e4
ICI_GATHER_BYTES_PER_US = 5.5e4
D2D_BYTES_PER_US = 4.0e5


class _Chunk:
    def __init__(self, group, cost, ins, out_shapes, alias, n_sem, start, finish, done, buffer=None, bind=None):
        self.group, self.cost, self.ins, self.out_shapes, self.alias, self.n_sem = group, cost, ins, out_shapes, alias, n_sem
        self.start, self.finish, self.done = start, finish, done
        self.buffer = buffer
        self.bind = bind

    def prepare(self):
        if self.bind is not None:
            self.bind(self)


def _merged(chunks):
    groups, by_buffer = [], {}
    for ch in chunks:
        key = None if ch.buffer is None else (id(ch.buffer[0]), ch.buffer[1])
        if key is not None and key in by_buffer:
            by_buffer[key].append(ch)
        else:
            groups.append([ch])
            if key is not None:
                by_buffer[key] = groups[-1]
    out = []
    for parts in groups:
        if len(parts) == 1:
            out.append(parts[0])
            continue
        offs = [sum(p.n_sem for p in parts[:i]) for i in range(len(parts))]

        def run(phase, ins, outs, ss, rs, b, parts=parts, offs=offs):
            for p, o in zip(parts, offs):
                getattr(p, phase)(ins, outs, ss, rs, b + o)

        def done(outs, parts=parts):
            for p in parts:
                p.done(outs)

        first = parts[0]
        out.append(_Chunk(first.group, sum(p.cost for p in parts), first.ins, first.out_shapes, first.alias,
                          sum(p.n_sem for p in parts), functools.partial(run, "start"),
                          functools.partial(run, "finish"), done))
    return out


LAST_GROUP = 1 << 30
MIN_CARRIED_US = 8.0


class _CommQueue:
    def __init__(self):
        self.pending = []
        self.flushes = 0
        self.horizon = LAST_GROUP

    def push(self, chunk):
        self.pending.append(chunk)

    def take(self, budget_us):
        got, used = [], 0.0
        for ch in sorted(self.pending, key=lambda ch: (ch.group, -ch.cost)):
            if ch.group >= self.horizon and ch.group != LAST_GROUP:
                continue
            if used + ch.cost <= budget_us and not self._shares_buffer(ch, got):
                got.append(ch)
                used += ch.cost
        if used < MIN_CARRIED_US:
            return []
        return self._taken(got)

    @staticmethod
    def _shares_buffer(ch, others):
        return ch.buffer is not None and any(
            o.buffer is not None and o.buffer[0] is ch.buffer[0] and o.buffer[1] != ch.buffer[1] for o in others)

    def _taken(self, got):
        self.pending = [ch for ch in self.pending if ch not in got]
        for ch in got:
            ch.prepare()
        return _merged(got)

    def flush(self, group=LAST_GROUP):
        while True:
            chunks = []
            for ch in self.pending:
                if ch.group <= group and not self._shares_buffer(ch, chunks):
                    chunks.append(ch)
            if not chunks:
                return
            _run_chunks(self._taken(chunks), name=f"comm_flush{self.flushes}")
            self.flushes += 1


def _run_chunks(chunks, *, name):
    ins = [a for ch in chunks for a in ch.ins]
    outs = [s for ch in chunks for s in ch.out_shapes]
    alias, offs = {}, []
    i0 = o0 = s0 = 0
    for ch in chunks:
        offs.append((i0, o0, s0))
        for ci, co in ch.alias.items():
            alias[i0 + ci] = o0 + co
        i0 += len(ch.ins)
        o0 += len(ch.out_shapes)
        s0 += ch.n_sem

    def body(*refs):
        send_sems, recv_sems = refs[i0 + o0:]
        for phase in ("start", "finish"):
            for ch, (a, b, s) in zip(chunks, offs):
                getattr(ch, phase)(refs[a:a + len(ch.ins)], refs[i0 + b:i0 + b + len(ch.out_shapes)],
                                   send_sems, recv_sems, s)

    hbm = pl.BlockSpec(memory_space=pl.ANY)
    res = pl.pallas_call(
        body, in_specs=[hbm] * i0, out_specs=[hbm] * o0, out_shape=outs,
        scratch_shapes=[pltpu.SemaphoreType.DMA((s0,)), pltpu.SemaphoreType.DMA((s0,))],
        input_output_aliases=alias, name=name, compiler_params=pltpu.CompilerParams(has_side_effects=True))(*ins)
    for ch, (_, b, _) in zip(chunks, offs):
        ch.done(list(res[b:b + len(ch.out_shapes)]))


def _remote(src, dst, send_sems, recv_sems, k, dev):
    return pltpu.make_async_remote_copy(src_ref=src, dst_ref=dst, send_sem=send_sems.at[k], recv_sem=recv_sems.at[k],
                                        device_id=dev, device_id_type=MESH_T)


def _gather_chunks(q, group, kind, shard, l, ready):
    _, r, c = shard.shape
    shp = (1, r, c)
    rh = r // 2
    parts = max(p for p in (8, 4, 2, 1)
                if (rh // p) % HALF_ALIGN == 0 and (p == 1 or (rh // p) * c * shard.dtype.itemsize >= MIN_PART_BYTES))
    part_bytes = (rh // parts) * c * shard.dtype.itemsize
    full_type = jax.ShapeDtypeStruct(_full_shape(kind, shp), shard.dtype)
    state = {"full": None, "parts_done": 0}

    def bind_first(ch):
        ch.ins, ch.alias = ([shard], {}) if state["full"] is None else ([shard, state["full"]], {1: 0})

    def bind_full(ch):
        ch.ins = [state["full"]]

    def make_part(p):
        sub = (p, parts)

        def any_part(full):
            return _slot_view(full, kind, shp, 0, 0, sub)

        def own_rows(src):
            return src.at[:, pl.ds(p * (r // parts), r // parts), :]

        def start1(ins, outs, ss, rs, b):
            x, y, c_ = _mesh_pos()
            src, full, me = ins[0].at[pl.ds(l, 1)], outs[0], 2 * x + y
            pltpu.make_async_copy(own_rows(src), _slot_view(full, kind, shp, me, None, sub),
                                  ss.at[b + N_CHIP - 1]).start(CARRIED_DMA_PRIORITY)
            for j, (ox, oy) in enumerate(_other_chips(x, y)):
                _remote(_half_view(src, shp, c_, sub), _slot_view(full, kind, shp, me, c_, sub), ss, rs, b + j,
                        (ox, oy, c_)).start()

        def finish1(ins, outs, ss, rs, b):
            x, y, c_ = _mesh_pos()
            h = any_part(outs[0])
            for j in range(N_CHIP - 1):
                _remote(h, h, ss, rs, b + j, (x, y, 1 - c_)).wait()
            pltpu.make_async_copy(own_rows(ins[0].at[pl.ds(l, 1)]), _slot_view(outs[0], kind, shp, 0, None, sub),
                                  ss.at[b + N_CHIP - 1]).wait()

        def start2(ins, outs, ss, rs, b):
            x, y, c_ = _mesh_pos()
            for j, (ox, oy) in enumerate(_other_chips(x, y)):
                v = _slot_view(outs[0], kind, shp, 2 * ox + oy, c_, sub)
                _remote(v, v, ss, rs, b + j, (x, y, 1 - c_)).start()

        def finish2(ins, outs, ss, rs, b):
            x, y, c_ = _mesh_pos()
            h = any_part(outs[0])
            for j in range(N_CHIP - 1):
                _remote(h, h, ss, rs, b + j, (x, y, 1 - c_)).wait()

        def done2(outs):
            state["full"] = outs[0]
            state["parts_done"] += 1
            if state["parts_done"] == parts:
                ready(outs[0])

        def done1(outs):
            state["full"] = outs[0]
            q.push(_Chunk(group, 3 * part_bytes / D2D_BYTES_PER_US, None, [full_type], {0: 0}, N_CHIP - 1,
                          start2, finish2, done2, buffer=(state, 2), bind=bind_full))

        return _Chunk(group, 3 * part_bytes / ICI_GATHER_BYTES_PER_US, None, [full_type], None,
                      N_CHIP, start1, finish1, done1, buffer=(state, 1), bind=bind_first)

    for p in range(parts):
        q.push(make_part(p))


def _reduce_scatter_chunks(q, kind, grad, shard_shape, pos, name, ready):
    _, r, c = shard_shape
    shp = (1, r, c)
    rh = r // 2

    rp = rh // D2D_STREAMS

    def landing(ref, s, i):
        return ref.at[s, :, pl.ds(i * rp, rp), :]

    def start1(ins, outs, ss, rs, b):
        x, y, c_ = _mesh_pos()
        for s in range(N_CHIP):
            for i in range(D2D_STREAMS):
                _remote(_slot_view(ins[0], kind, shp, s, 1 - c_, (i, D2D_STREAMS)), landing(outs[0], s, i),
                        ss, rs, b + s * D2D_STREAMS + i, (x, y, 1 - c_)).start()

    def finish1(ins, outs, ss, rs, b):
        x, y, c_ = _mesh_pos()
        for s in range(N_CHIP):
            for i in range(D2D_STREAMS):
                v = landing(outs[0], s, i)
                _remote(v, v, ss, rs, b + s * D2D_STREAMS + i, (x, y, 1 - c_)).wait()

    def start2(ins, outs, ss, rs, b):
        x, y, c_ = _mesh_pos()
        for j, (ox, oy) in enumerate(_other_chips(x, y)):
            _remote(ins[0].at[2 * ox + oy], outs[0].at[j], ss, rs, b + j, (ox, oy, c_)).start()

    def finish2(ins, outs, ss, rs, b):
        x, y, c_ = _mesh_pos()
        for j in range(N_CHIP - 1):
            _remote(outs[0].at[j], outs[0].at[j], ss, rs, b + j, (x, y, 1 - c_)).wait()

    def start3(ins, outs, ss, rs, b):
        x, y, c_ = _mesh_pos()
        for i in range(D2D_STREAMS):
            v = _half_view(outs[0], shp, c_, (i, D2D_STREAMS))
            _remote(v, v, ss, rs, b + i, (x, y, 1 - c_)).start()

    def finish3(ins, outs, ss, rs, b):
        x, y, c_ = _mesh_pos()
        for i in range(D2D_STREAMS):
            v = _half_view(outs[0], shp, c_, (i, D2D_STREAMS))
            _remote(v, v, ss, rs, b + i, (x, y, 1 - c_)).wait()

    def done2(pair, outs):
        half = _rs_chip_add(pair, outs[0], shp, pos, name=f"rs_chip_add_{name}")
        q.push(_Chunk(LAST_GROUP, rh * c * 4 / D2D_BYTES_PER_US, [half], [jax.ShapeDtypeStruct(half.shape, half.dtype)],
                      {0: 0}, D2D_STREAMS, start3, finish3, lambda o: ready(o[0])))

    def done1(outs):
        pair, wire = _rs_pair_add(grad, outs[0], kind, shp, pos, name=f"rs_pair_add_{name}")
        q.push(_Chunk(LAST_GROUP, 3 * rh * c * wire.dtype.itemsize / ICI_BYTES_PER_US, [wire],
                      [jax.ShapeDtypeStruct((N_CHIP - 1, 1, rh, c), wire.dtype)], {}, N_CHIP - 1,
                      start2, finish2, functools.partial(done2, pair)))

    q.push(_Chunk(LAST_GROUP, N_CHIP * rh * c * 4 / D2D_BYTES_PER_US, [grad],
                  [jax.ShapeDtypeStruct((N_CHIP, 1, rh, c), F32)], {}, N_CHIP * D2D_STREAMS, start1, finish1, done1))


N_DEV = 8


def _allgather_chunk(q, group, vec, ready):
    def peer(k, x, y, c):
        return ((1 - x) if k & 4 else x, (1 - y) if k & 2 else y, (1 - c) if k & 1 else c)

    def start(ins, outs, ss, rs, b):
        x, y, c = _mesh_pos()
        me = 4 * x + 2 * y + c
        pltpu.make_async_copy(ins[0], outs[0].at[me], ss.at[b + N_DEV - 1]).start(CARRIED_DMA_PRIORITY)
        for k in range(1, N_DEV):
            _remote(ins[0], outs[0].at[me], ss, rs, b + k - 1, peer(k, x, y, c)).start()

    def finish(ins, outs, ss, rs, b):
        x, y, c = _mesh_pos()
        for k in range(1, N_DEV):
            _remote(ins[0], outs[0].at[0], ss, rs, b + k - 1, peer(k, x, y, c)).wait()
        pltpu.make_async_copy(ins[0], outs[0].at[0], ss.at[b + N_DEV - 1]).wait()

    bytes_in = (N_DEV - 2) * vec.size * 4
    q.push(_Chunk(group, bytes_in / ICI_BYTES_PER_US, [vec], [jax.ShapeDtypeStruct((N_DEV,) + vec.shape, F32)], {},
                  N_DEV, start, finish, lambda o: ready(o[0])))


def _allreduce8(vec, *, name):
    r = vec.shape[0]
    rh = r // 2

    def body(v_ref, o_ref, sib_ref, chips_ref, send_sems, recv_sems):
        x, y, c = _mesh_pos()
        sib = (x, y, 1 - c)
        me = 2 * x + y
        pair = _remote(v_ref, sib_ref, send_sems, recv_sems, 0, sib)
        pair.start()
        pair.wait()
        rows = pl.ds(pl.multiple_of(c * rh, SUBLANE), rh)
        chips_ref[me] = v_ref[rows, :] + sib_ref[rows, :]
        copies = []
        for j, (ox, oy) in enumerate(_other_chips(x, y)):
            cp = _remote(chips_ref.at[me], chips_ref.at[me], send_sems, recv_sems, 1 + j, (ox, oy, c))
            cp.start()
            copies.append(cp)
        for cp in copies:
            cp.wait()
        acc = chips_ref[0]
        for s in range(1, N_CHIP):
            acc = acc + chips_ref[s]
        o_ref[rows, :] = acc
        swap = _remote(o_ref.at[rows, :], o_ref.at[rows, :], send_sems, recv_sems, N_CHIP, sib)
        swap.start()
        swap.wait()

    vm = pl.BlockSpec(memory_space=pltpu.VMEM)
    return pl.pallas_call(
        body, in_specs=[vm], out_specs=vm, out_shape=jax.ShapeDtypeStruct((r, LANE), F32),
        scratch_shapes=[pltpu.VMEM((r, LANE), F32), pltpu.VMEM((N_CHIP, rh, LANE), F32),
                        pltpu.SemaphoreType.DMA((N_CHIP + 1,)), pltpu.SemaphoreType.DMA((N_CHIP + 1,))],
        name=name, compiler_params=pltpu.CompilerParams(has_side_effects=True, vmem_limit_bytes=VMEM_LIMIT_V7X))(vec)


def _rs_pair_add(g, recv, kind, shape, pos, *, name):
    l, r, c = shape
    assert l == 1
    rh = r // 2
    if kind == "row":
        gspec = pl.BlockSpec((None, rh, c), lambda s, pos: (0, 2 * s + pos[0], 0))
    else:
        gspec = pl.BlockSpec((None, None, rh, c), lambda s, pos: (s, 0, pos[0], 0))
    pspec = pl.BlockSpec((None, None, rh, c), lambda s, pos: (s, 0, 0, 0))

    def body(pos_ref, g_ref, r_ref, own_ref, pw_ref):
        v = g_ref[...] + r_ref[...]
        pw_ref[...] = v.astype(pw_ref.dtype)

        @pl.when(pl.program_id(0) == pos_ref[1])
        def _():
            own_ref[...] = v

    return pl.pallas_call(
        body,
        grid_spec=pltpu.PrefetchScalarGridSpec(
            num_scalar_prefetch=1, grid=(N_CHIP,), in_specs=[gspec, pspec],
            out_specs=[pl.BlockSpec((None, rh, c), lambda s, pos: (0, 0, 0)), pspec]),
        out_shape=[jax.ShapeDtypeStruct((1, rh, c), F32), jax.ShapeDtypeStruct((N_CHIP, 1, rh, c), MXU)],
        name=name, compiler_params=_cp((ARB,)))(pos, g, recv)


def _rs_chip_add(p, recv, shape, pos, *, name):
    l, r, c = shape
    rh = r // 2

    def body(pos_ref, p_ref, r_ref, o_ref):
        del pos_ref
        acc = p_ref[...]
        for j in range(N_CHIP - 1):
            acc = acc + r_ref[j].astype(F32)
        o_ref[...] = acc

    return pl.pallas_call(
        body,
        grid_spec=pltpu.PrefetchScalarGridSpec(
            num_scalar_prefetch=1, grid=(l,),
            in_specs=[pl.BlockSpec((None, rh, c), lambda i, pos: (i, 0, 0)),
                      pl.BlockSpec((N_CHIP - 1, None, rh, c), lambda i, pos: (0, i, 0, 0))],
            out_specs=pl.BlockSpec((None, rh, c), lambda i, pos: (i, pos[0], 0))),
        out_shape=jax.ShapeDtypeStruct((l, r, c), F32),
        name=name, compiler_params=_cp((PAR,)))(pos, p, recv)


ADAM_BLOCK_ELEMS = 384 * 1024


def _adam_math(w, g, m, v):
    c1 = 1.0 / (1.0 - ADAM_B1 ** ADAM_STEP)
    c2 = 1.0 / (1.0 - ADAM_B2 ** ADAM_STEP)
    nm = ADAM_B1 * m + (1.0 - ADAM_B1) * g
    nv = ADAM_B2 * v + (1.0 - ADAM_B2) * (g * g)
    return -ADAM_LR * ((nm * c1) / (jnp.sqrt(nv * c2) + ADAM_EPS) + ADAM_WD * w), nm, nv


def _adamw_layer(w, g, m, v, outs, l, *, name):
    _, r, c = w.shape
    tr = _tile(r, max(SUBLANE, ADAM_BLOCK_ELEMS // c // SUBLANE * SUBLANE), SUBLANE)

    def body(w_ref, g_ref, m_ref, v_ref, *rest):
        go_ref, d_ref, nm_ref, nv_ref = rest[4:]
        gg = g_ref[...]
        go_ref[...] = gg
        d_ref[...], nm_ref[...], nv_ref[...] = _adam_math(w_ref[...], gg, m_ref[...], v_ref[...])

    lay = pl.BlockSpec((None, tr, c), lambda j: (l, j, 0))
    hbm = pl.BlockSpec(memory_space=pl.ANY)
    return pl.pallas_call(
        body, grid=(r // tr,),
        in_specs=[lay, pl.BlockSpec((None, tr, c), lambda j: (0, j, 0)), lay, lay] + [hbm] * 4,
        out_specs=[lay] * 4, out_shape=[jax.ShapeDtypeStruct(w.shape, F32)] * 4,
        input_output_aliases={4 + i: i for i in range(4)},
        name=name, compiler_params=_cp((PAR,)))(w, g, m, v, *outs)


def _adamw(w, g, m, v, *, name):
    shape = w.shape
    if w.ndim == 2:
        w, g, m, v = (a[None] for a in (w, g, m, v))
    l, r, c = w.shape
    tr = _tile(r, max(SUBLANE, ADAM_BLOCK_ELEMS // c // SUBLANE * SUBLANE), SUBLANE)

    def body(w_ref, g_ref, m_ref, v_ref, d_ref, nm_ref, nv_ref):
        d_ref[...], nm_ref[...], nv_ref[...] = _adam_math(w_ref[...], g_ref[...], m_ref[...], v_ref[...])

    spec = pl.BlockSpec((None, tr, c), lambda i, j: (i, j, 0))
    outs = pl.pallas_call(
        body, grid=(l, r // tr), in_specs=[spec] * 4, out_specs=[spec] * 3,
        out_shape=[jax.ShapeDtypeStruct((l, r, c), F32)] * 3,
        name=name, compiler_params=_cp((PAR, PAR)))(w, g, m, v)
    return tuple(o.reshape(shape) for o in outs)


PACK_ROWS = 2 * SUBLANE * LANE


def _pack(arrays):
    flat = jnp.concatenate([a.reshape(-1).astype(F32) for a in arrays])
    pad = (-flat.shape[0]) % PACK_ROWS
    return jnp.pad(flat, (0, pad)).reshape(-1, LANE)


def _unpack(packed, shapes):
    flat = packed.reshape(-1)
    out, off = [], 0
    for s in shapes:
        size = int(np.prod(s))
        out.append(flat[off:off + size].reshape(s))
        off += size
    return out


BIG = (("w_mem_kv", "row"), ("w_mix_out", "row"), ("w_ffn_up", "slot"), ("w_ffn_down", "row"),
       ("w_in_a", "slot"), ("w_in_b", "row"), ("w_kv", "row"))
COLUMN_SHARDED_AS_COLUMNS = ("w_in_a",)
SMALL_SHARDED = (("w_ffn_conv", 2), ("w_conv_a", 2), ("b_conv_a", 1), ("lru_lambda", 1))
SMALL_REPLICATED = ("g_mix_pre", "g_mix_post", "g_ffn_pre", "g_ffn_post", "g_mem", "b_ffn_conv",
                    "w_rg_r", "b_rg_r", "w_rg_i", "b_rg_i", "sinks_b", "g_kv")
WEIGHTS = ("g_mix_pre", "g_mix_post", "g_ffn_pre", "g_ffn_post", "g_mem", "w_mem_kv", "w_mix_out", "w_ffn_up",
           "w_ffn_conv", "b_ffn_conv", "w_ffn_down", "w_in_a", "w_conv_a", "b_conv_a", "w_rg_r", "b_rg_r", "w_rg_i",
           "b_rg_i", "lru_lambda", "w_in_b", "sinks_b", "g_kv", "w_kv")


def _slot_to_cols(a):
    s, l, r, c = a.shape
    return a.transpose(1, 2, 0, 3).reshape(l, r, s * c)


def _cols_to_slot(a):
    l, r, c4 = a.shape
    return a.reshape(l, r, N_CHIP, c4 // N_CHIP).transpose(2, 0, 1, 3)


GROUPS_PER_LAYER = 8


def _layer_weights(layer):
    names = [("w_mem_kv", layer), ("w_in_a", layer) if layer < N_A else ("w_in_b", layer - N_A)]
    if layer == N_A:
        names.append(("w_kv", 0))
    return names + [("w_mix_out", layer), ("w_ffn_up", layer), ("w_ffn_down", layer)]


def _train_step(x, mem, target, w, m, v):
    xi, yi, ci = _mesh_pos()
    chip = 2 * xi + yi
    pos = jnp.stack([ci, chip]).astype(jnp.int32)

    q = _CommQueue()
    kinds = dict(BIG)
    as3 = lambda a: a if a.ndim == 3 else a[None]
    w3, m3, v3 = ({k: as3(d[k]) for k, _ in BIG} for d in (w, m, v))
    shards = {k: w3[k].astype(MXU) for k, _ in BIG}

    gathered = {}

    def on_gathered(k, l, full):
        gathered[k, l] = _slot_to_cols(full) if k in COLUMN_SHARDED_AS_COLUMNS else full

    group_of = {}

    for layer in range(DEPTH):
        for i, (k, l) in enumerate(_layer_weights(layer)):
            group_of[k, l] = layer * GROUPS_PER_LAYER + i
            _gather_chunks(q, group_of[k, l], kinds[k], shards[k], l, functools.partial(on_gathered, k, l))

    def wfull(k, l):
        if (k, l) not in gathered:
            q.flush(group_of[k, l])
        return gathered[k, l]

    small = {}
    _allgather_chunk(q, 0, _pack([w[k] for k, _ in SMALL_SHARDED]), functools.partial(small.__setitem__, "stacked"))
    q.flush(1)

    big_out = {k: [lax.empty(w3[k].shape, F32) for _ in range(4)] for k, _ in BIG}

    def on_reduced(k, l, g):
        big_out[k] = _adamw_layer(w3[k], g, m3[k], v3[k], big_out[k], l, name=f"adamw_{k}{l}")

    def push_grad(k, l, g):
        if k in COLUMN_SHARDED_AS_COLUMNS:
            g = _cols_to_slot(g)
        _reduce_scatter_chunks(q, kinds[k], g, (1,) + w3[k].shape[1:], pos, f"{k}{l}", functools.partial(on_reduced, k, l))

    small_shapes = [w[k].shape for k, _ in SMALL_SHARDED]
    per_chip = [_unpack(small["stacked"][2 * s], small_shapes) for s in range(N_CHIP)]
    p = {k: w[k] for k in SMALL_REPLICATED}
    for i, (k, axis) in enumerate(SMALL_SHARDED):
        p[k] = jnp.concatenate([per_chip[s][i] for s in range(N_CHIP)], axis=axis)

    sq, grad_x, g = _local_step(x, mem, target, p, wfull, push_grad, q)
    loss = lax.psum(0.5 * sq / D_MODEL, ("x", "y", "c"))
    q.flush()

    small_names = [k for k, _ in SMALL_SHARDED] + list(SMALL_REPLICATED)
    summed = _allreduce8(_pack([g[k] for k in small_names]), name="allreduce_small")
    gsum = dict(zip(small_names, _unpack(summed, [p[k].shape for k in small_names])))
    for k, axis in SMALL_SHARDED:
        gsum[k] = lax.dynamic_slice_in_dim(gsum[k], chip * w[k].shape[axis], w[k].shape[axis], axis)

    delta, new_m, new_v = {}, {}, {}
    for k, _ in BIG:
        gsum[k], delta[k], new_m[k], new_v[k] = (o.reshape(w[k].shape) for o in big_out[k])
    for k in small_names:
        as2 = lambda a: a.reshape(-1, a.shape[-1])
        outs = _adamw(as2(w[k]), as2(gsum[k]), as2(m[k]), as2(v[k]), name=f"adamw_{k}")
        delta[k], new_m[k], new_v[k] = (o.reshape(w[k].shape) for o in outs)
    return (loss, grad_x, *[gsum[k] for k in WEIGHTS], *[delta[k] for k in WEIGHTS],
            *[new_m[k] for k in WEIGHTS], *[new_v[k] for k in WEIGHTS])


def kernel(x, mem, g_mix_pre, g_mix_post, g_ffn_pre, g_ffn_post, g_mem, w_mem_kv, w_mix_out, w_ffn_up, w_ffn_conv, b_ffn_conv, w_ffn_down, w_in_a, w_conv_a, b_conv_a, w_rg_r, b_rg_r, w_rg_i, b_rg_i, lru_lambda, w_in_b, sinks_b, g_kv, w_kv, loss_target, m_g_mix_pre, m_g_mix_post, m_g_ffn_pre, m_g_ffn_post, m_g_mem, m_w_mem_kv, m_w_mix_out, m_w_ffn_up, m_w_ffn_conv, m_b_ffn_conv, m_w_ffn_down, m_w_in_a, m_w_conv_a, m_b_conv_a, m_w_rg_r, m_b_rg_r, m_w_rg_i, m_b_rg_i, m_lru_lambda, m_w_in_b, m_sinks_b, m_g_kv, m_w_kv, v_g_mix_pre, v_g_mix_post, v_g_ffn_pre, v_g_ffn_post, v_g_mem, v_w_mem_kv, v_w_mix_out, v_w_ffn_up, v_w_ffn_conv, v_b_ffn_conv, v_w_ffn_down, v_w_in_a, v_w_conv_a, v_b_conv_a, v_w_rg_r, v_b_rg_r, v_w_rg_i, v_b_rg_i, v_lru_lambda, v_w_in_b, v_sinks_b, v_g_kv, v_w_kv):
    args = (g_mix_pre, g_mix_post, g_ffn_pre, g_ffn_post, g_mem, w_mem_kv, w_mix_out, w_ffn_up, w_ffn_conv, b_ffn_conv, w_ffn_down, w_in_a, w_conv_a, b_conv_a, w_rg_r, b_rg_r, w_rg_i, b_rg_i, lru_lambda, w_in_b, sinks_b, g_kv, w_kv)
    ms = (m_g_mix_pre, m_g_mix_post, m_g_ffn_pre, m_g_ffn_post, m_g_mem, m_w_mem_kv, m_w_mix_out, m_w_ffn_up, m_w_ffn_conv, m_b_ffn_conv, m_w_ffn_down, m_w_in_a, m_w_conv_a, m_b_conv_a, m_w_rg_r, m_b_rg_r, m_w_rg_i, m_b_rg_i, m_lru_lambda, m_w_in_b, m_sinks_b, m_g_kv, m_w_kv)
    vs = (v_g_mix_pre, v_g_mix_post, v_g_ffn_pre, v_g_ffn_post, v_g_mem, v_w_mem_kv, v_w_mix_out, v_w_ffn_up, v_w_ffn_conv, v_b_ffn_conv, v_w_ffn_down, v_w_in_a, v_w_conv_a, v_b_conv_a, v_w_rg_r, v_b_rg_r, v_w_rg_i, v_b_rg_i, v_lru_lambda, v_w_in_b, v_sinks_b, v_g_kv, v_w_kv)
    return _train_step(x, mem, loss_target, dict(zip(WEIGHTS, args)), dict(zip(WEIGHTS, ms)), dict(zip(WEIGHTS, vs)))
```

```python
import functools
import math

import numpy as np
import jax
import jax.numpy as jnp
from jax import lax
from jax.experimental import pallas as pl
from jax.experimental.pallas import tpu as pltpu

F32 = jnp.float32
MXU = jnp.bfloat16

D_MODEL = 1024
HEAD_DIM = 64
MEM_LEN = 256
MEM_HEADS = 4
MEM_WIDTH = MEM_HEADS * HEAD_DIM
MIX_WIDTH = D_MODEL - MEM_WIDTH
LRU_BLOCKS = MIX_WIDTH // HEAD_DIM
LRU_CONV = 4
LRU_C = 8.0
SWA_HEADS = MIX_WIDTH // HEAD_DIM
SWA_KV_HEADS = 4
SWA_GROUP = SWA_HEADS // SWA_KV_HEADS
WINDOW = 128
D_FF = 2816
FFN_CONV = 3
EPS = 1e-6
DEPTH = 4
N_A = 2

ADAM_LR = 0.001
ADAM_B1 = 0.9
ADAM_B2 = 0.999
ADAM_EPS = 1e-08
ADAM_WD = 0.01
ADAM_STEP = 10

VMEM_LIMIT_V7X = 56 * 1024 * 1024
LANE = 128
SUBLANE = 8
GATE_TILE = 256
MESH_T = pl.DeviceIdType.MESH


def _alibi_slopes(n):
    def pow2_slopes(m):
        start = 2.0 ** (-8.0 / m)
        return [start ** (i + 1) for i in range(m)]
    c = 2 ** int(math.floor(math.log2(n)))
    s = pow2_slopes(c)
    if c != n:
        s = s + pow2_slopes(2 * c)[0::2][: n - c]
    return [float(np.float32(v)) for v in s]


SLOPES = _alibi_slopes(SWA_HEADS)


def _tile(n, cap, mult=LANE):
    best = None
    for t in range(mult, min(n, cap) + 1, mult):
        if n % t == 0:
            best = t
    return best if best is not None else n


def _cp(sem):
    return pltpu.CompilerParams(dimension_semantics=sem, vmem_limit_bytes=VMEM_LIMIT_V7X)


MM_VMEM_BUDGET = 40 * 1024 * 1024
HBM_BYTES_PER_US_V7X = 3.0e6
GRID_STEP_US = 0.35


def _divisors(n, mult):
    return [t for t in range(mult, n + 1, mult) if n % t == 0] or [n]


def _mm_tiles(m, k, n, out_bytes):
    best = None
    for tm in _divisors(m, 256):
        for tn in _divisors(n, LANE):
            vmem = 2 * (tm * k * 2 + k * tn * 2 + tm * tn * out_bytes)
            if vmem > MM_VMEM_BUDGET:
                continue
            steps = (m // tm) * (n // tn)
            b_reads = 1 if tn == n else m // tm
            traffic = m * k * 2 + k * n * 2 * b_reads + m * n * out_bytes
            first = tm * k * 2 + k * tn * 2
            cost = (traffic + first) / HBM_BYTES_PER_US_V7X + steps * GRID_STEP_US
            if best is None or cost < best[0]:
                best = (cost, tm, tn)
    return best[1], best[2]


def _mm_tn_tiles(k, m, n, whole_n=False):
    best = None
    for tm in _divisors(m, LANE):
        for tn in ([n] if whole_n else _divisors(n, LANE)):
            for tk in _divisors(k, 512):
                vmem = 2 * (tk * tm * 2 + tk * tn * 2 + tm * tn * 4)
                if vmem > MM_VMEM_BUDGET:
                    continue
                steps = (m // tm) * (n // tn) * (k // tk)
                traffic = k * m * 2 * (n // tn) + k * n * 2 * (m // tm) + m * n * 4
                cost = traffic / HBM_BYTES_PER_US_V7X + steps * GRID_STEP_US
                if best is None or cost < best[0]:
                    best = (cost, tk, tm, tn)
    return best[1], best[2], best[3]


ARB = "arbitrary"
PAR = "parallel"


def _rms_fwd(x, g):
    r = lax.rsqrt(jnp.mean(x * x, axis=-1, keepdims=True) + EPS)
    return x * r * g


def _rms_bwd(dy, x, g):
    r = lax.rsqrt(jnp.mean(x * x, axis=-1, keepdims=True) + EPS)
    xh = x * r
    gdy = dy * g
    dx = r * (gdy - xh * jnp.mean(gdy * xh, axis=-1, keepdims=True))
    dg = jnp.sum(dy * xh, axis=0, keepdims=True)
    return dx, dg


_GELU_K = math.sqrt(2.0 / math.pi)
_GELU_C = 0.044715


def _gelu(x):
    t = jnp.tanh(_GELU_K * (x + _GELU_C * x * x * x))
    return 0.5 * x * (1.0 + t)


def _gelu_and_grad(x):
    x2 = x * x
    u = 0.5 * jnp.tanh(x * (_GELU_K + (_GELU_K * _GELU_C) * x2)) + 0.5
    dz2 = (6.0 * _GELU_K * _GELU_C) * x2 + 2.0 * _GELU_K
    return x * u, u * ((x * (1.0 - u)) * dz2 + 1.0)


def _shift_up(x, k, row):
    n = x.shape[0]
    return jnp.where(row < n - k, pltpu.roll(x, n - k, axis=0), 0.0)


def _shift_down_edge(x, k):
    r = pltpu.roll(x, k, axis=0)
    row = lax.broadcasted_iota(jnp.int32, (SUBLANE, x.shape[1]), 0)
    return jnp.concatenate([jnp.where(row >= k, r[:SUBLANE], 0.0), r[SUBLANE:]], axis=0)


def _shift_up_edge(x, k):
    n = x.shape[0]
    r = pltpu.roll(x, n - k, axis=0)
    row = lax.broadcasted_iota(jnp.int32, (SUBLANE, x.shape[1]), 0)
    return jnp.concatenate([r[:n - SUBLANE], jnp.where(row < SUBLANE - k, r[n - SUBLANE:], 0.0)], axis=0)


def _dot(a, b):
    return jnp.dot(a, b, preferred_element_type=F32)


def _dot_nt(a, b):
    return lax.dot_general(a, b, (((1,), (1,)), ((), ())), preferred_element_type=F32)


def _dot_tn(a, b):
    return lax.dot_general(a, b, (((0,), (0,)), ((), ())), preferred_element_type=F32)


MXU_FLOPS_PER_US = 8.0e8
HOST_US = {"lru_fwd": 44.0, "lru_bwd": 94.0, "swa_fwd": 55.0, "swa_bwd": 90.0, "mem_attn_fwd": 19.0,
           "mem_attn_bwd": 27.0, "ffn_act": 75.0, "ffn_act_bwd": 75.0, "resid": 22.0, "resid_bwd": 33.0}
HOST_FILL = 1.0


def _hosted_call(body, *, grid, in_specs, out_specs, out_shape, args, name, aliases=None, scratch_shapes=(),
                 q=None, flops=0.0, budget_us=0.0):
    chunks = q.take(HOST_FILL * (flops / MXU_FLOPS_PER_US + budget_us)) if q is not None else []
    if not chunks:
        return pl.pallas_call(
            body, grid=grid, in_specs=in_specs, out_specs=out_specs, out_shape=out_shape,
            scratch_shapes=list(scratch_shapes), input_output_aliases=aliases or {}, name=name,
            compiler_params=_cp((ARB,) * len(grid)))(*args)
    single = not isinstance(out_shape, (list, tuple))
    o_shapes = [out_shape] if single else list(out_shape)
    o_specs = [out_specs] if single else list(out_specs)
    n_in, n_out, n_scr = len(args), len(o_shapes), len(scratch_shapes)
    c_ins = [a for ch in chunks for a in ch.ins]
    c_outs = [s for ch in chunks for s in ch.out_shapes]
    alias = dict(aliases or {})
    in_off, out_off, sem_off = [], [], []
    i0 = o0 = s0 = 0
    for ch in chunks:
        in_off.append(i0)
        out_off.append(o0)
        sem_off.append(s0)
        for ci, co in ch.alias.items():
            alias[n_in + i0 + ci] = n_out + o0 + co
        i0 += len(ch.ins)
        o0 += len(ch.out_shapes)
        s0 += ch.n_sem

    def wrapped(*refs):
        ins = refs[:n_in]
        cin = refs[n_in:n_in + i0]
        outs = refs[n_in + i0:n_in + i0 + n_out]
        cout = refs[n_in + i0 + n_out:n_in + i0 + n_out + o0]
        scr = refs[n_in + i0 + n_out + o0:n_in + i0 + n_out + o0 + n_scr]
        send_sems, recv_sems = refs[n_in + i0 + n_out + o0 + n_scr:]
        first = functools.reduce(lambda u, v: u & v, [pl.program_id(d) == 0 for d in range(len(grid))])
        last = functools.reduce(lambda u, v: u & v, [pl.program_id(d) == grid[d] - 1 for d in range(len(grid))])

        def each(phase):
            for ch, a, b, s in zip(chunks, in_off, out_off, sem_off):
                getattr(ch, phase)(cin[a:a + len(ch.ins)], cout[b:b + len(ch.out_shapes)], send_sems, recv_sems, s)

        pl.when(first)(lambda: each("start"))
        body(*ins, *outs, *scr)
        pl.when(last)(lambda: each("finish"))

    hbm = pl.BlockSpec(memory_space=pl.ANY)
    res = pl.pallas_call(
        wrapped, grid=grid, in_specs=list(in_specs) + [hbm] * i0, out_specs=o_specs + [hbm] * o0,
        out_shape=o_shapes + c_outs,
        scratch_shapes=list(scratch_shapes) + [pltpu.SemaphoreType.DMA((s0,)), pltpu.SemaphoreType.DMA((s0,))],
        input_output_aliases=alias, name=name,
        compiler_params=pltpu.CompilerParams(dimension_semantics=(ARB,) * len(grid), vmem_limit_bytes=VMEM_LIMIT_V7X,
                                             has_side_effects=True))(*args, *c_ins)
    for ch, b in zip(chunks, out_off):
        ch.done(list(res[n_out + b:n_out + b + len(ch.out_shapes)]))
    return res[0] if single else list(res[:n_out])


def _mm_nn(a, b, *, name, q=None, out_dtype=F32):
    m, k = a.shape
    n = b.shape[-1]
    tm, tn = _mm_tiles(m, k, n, jnp.dtype(out_dtype).itemsize)

    def body(a_ref, b_ref, o_ref):
        o_ref[...] = _dot(a_ref[...], b_ref[...]).astype(o_ref.dtype)

    return _hosted_call(
        body, grid=(m // tm, n // tn),
        in_specs=[pl.BlockSpec((tm, k), lambda i, j: (i, 0)),
                  pl.BlockSpec((None, k, tn), lambda i, j: (0, 0, j))],
        out_specs=pl.BlockSpec((tm, tn), lambda i, j: (i, j)),
        out_shape=jax.ShapeDtypeStruct((m, n), out_dtype),
        args=(a, b), name=name, q=q, flops=2.0 * m * k * n)


def _mm_nt(a, b, *, name, q=None, out_dtype=F32):
    m, k = a.shape
    n = b.shape[-2]
    tm, tn = _mm_tiles(m, k, n, jnp.dtype(out_dtype).itemsize)

    def body(a_ref, b_ref, o_ref):
        o_ref[...] = _dot_nt(a_ref[...], b_ref[...]).astype(o_ref.dtype)

    return _hosted_call(
        body, grid=(m // tm, n // tn),
        in_specs=[pl.BlockSpec((tm, k), lambda i, j: (i, 0)),
                  pl.BlockSpec((None, tn, k), lambda i, j: (0, j, 0))],
        out_specs=pl.BlockSpec((tm, tn), lambda i, j: (i, j)),
        out_shape=jax.ShapeDtypeStruct((m, n), out_dtype),
        args=(a, b), name=name, q=q, flops=2.0 * m * k * n)


def _mm_nn_slots(a, b4, *, name, q=None, out_dtype=F32):
    m, k = a.shape
    s_, _, _, c = b4.shape
    ob = jnp.dtype(out_dtype).itemsize
    tm = max(t for t in _divisors(m, 256) if 2 * (t * k * 2 + k * c * 2 + t * c * ob) <= MM_VMEM_BUDGET)

    def body(a_ref, b_ref, o_ref):
        o_ref[...] = _dot(a_ref[...], b_ref[...]).astype(o_ref.dtype)

    return _hosted_call(
        body, grid=(m // tm, s_),
        in_specs=[pl.BlockSpec((tm, k), lambda i, j: (i, 0)),
                  pl.BlockSpec((None, None, k, c), lambda i, j: (j, 0, 0, 0))],
        out_specs=pl.BlockSpec((tm, c), lambda i, j: (i, j)),
        out_shape=jax.ShapeDtypeStruct((m, s_ * c), out_dtype),
        args=(a, b4), name=name, q=q, flops=2.0 * m * k * s_ * c)


def _mm_tn_slots(a, b, *, name, slot_cols, n_slots, first_slot=0, q=None, out=None):
    k, m = a.shape
    c = slot_cols
    tk, tm, _ = _mm_tn_tiles(k, m, c, whole_n=True)

    def body(a_ref, b_ref, *rest):
        o_ref = rest[-1]
        part = _dot_tn(a_ref[...], b_ref[...])

        @pl.when(pl.program_id(2) == 0)
        def _():
            o_ref[...] = part

        @pl.when(pl.program_id(2) > 0)
        def _():
            o_ref[...] += part

    in_specs = [pl.BlockSpec((tk, tm), lambda i, j, s: (s, i)), pl.BlockSpec((tk, c), lambda i, j, s: (s, j))]
    args = (a, b)
    if out is not None:
        in_specs.append(pl.BlockSpec(memory_space=pl.ANY))
        args = (a, b, out)
    return _hosted_call(
        body, grid=(m // tm, b.shape[-1] // c, k // tk), in_specs=in_specs,
        out_specs=pl.BlockSpec((None, None, tm, c), lambda i, j, s: (first_slot + j, 0, i, 0)),
        out_shape=jax.ShapeDtypeStruct((n_slots, 1, m, c), F32),
        aliases={2: 0} if out is not None else None,
        args=args, name=name, q=q, flops=2.0 * m * k * b.shape[-1])


def _mm_tn(a, b, *, name, q=None):
    k, m = a.shape
    n = b.shape[-1]
    tk, tm, tn = _mm_tn_tiles(k, m, n)

    def body(a_ref, b_ref, o_ref):
        part = _dot_tn(a_ref[...], b_ref[...])

        @pl.when(pl.program_id(2) == 0)
        def _():
            o_ref[...] = part

        @pl.when(pl.program_id(2) > 0)
        def _():
            o_ref[...] += part

    return _hosted_call(
        body, grid=(m // tm, n // tn, k // tk),
        in_specs=[pl.BlockSpec((tk, tm), lambda i, j, s: (s, i)), pl.BlockSpec((tk, tn), lambda i, j, s: (s, j))],
        out_specs=pl.BlockSpec((None, tm, tn), lambda i, j, s: (0, i, j)),
        out_shape=jax.ShapeDtypeStruct((1, m, n), F32),
        args=(a, b), name=name, q=q, flops=2.0 * m * k * n)


def _mm_ffn_dh(dg, dv, w4, *, name, q=None):
    m, f = dg.shape
    n_slots, _, d, c = w4.shape
    tm, tn = _mm_tiles(m, 2 * f, d, 4)

    def body(dg_ref, dv_ref, *rest):
        w_refs, o_ref = rest[:n_slots], rest[n_slots]
        acc = None
        for s, w_ref in enumerate(w_refs):
            x_ref = dg_ref if s < n_slots // 2 else dv_ref
            off = (s % (n_slots // 2)) * c
            part = _dot_nt(x_ref[:, off:off + c], w_ref[...])
            acc = part if acc is None else acc + part
        o_ref[...] = acc.astype(o_ref.dtype)

    wspec = lambda s: pl.BlockSpec((None, None, tn, c), lambda i, j: (s, 0, j, 0))
    return _hosted_call(
        body, grid=(m // tm, d // tn),
        in_specs=[pl.BlockSpec((tm, f), lambda i, j: (i, 0)),
                  pl.BlockSpec((tm, f), lambda i, j: (i, 0))] + [wspec(s) for s in range(n_slots)],
        out_specs=pl.BlockSpec((tm, tn), lambda i, j: (i, j)),
        out_shape=jax.ShapeDtypeStruct((m, d), MXU),
        args=(dg, dv) + (w4,) * n_slots, name=name, q=q, flops=4.0 * m * f * d)


ROW_TILE = 512


def _norm_fwd(x, g, *, name):
    n, d = x.shape
    tm = _tile(n, ROW_TILE, SUBLANE)

    def body(x_ref, g_ref, o_ref):
        o_ref[...] = _rms_fwd(x_ref[...], g_ref[...]).astype(o_ref.dtype)

    return pl.pallas_call(
        body, grid=(n // tm,),
        in_specs=[pl.BlockSpec((tm, d), lambda i: (i, 0)), pl.BlockSpec((1, d), lambda i: (0, 0))],
        out_specs=pl.BlockSpec((tm, d), lambda i: (i, 0)),
        out_shape=jax.ShapeDtypeStruct((n, d), MXU),
        name=name, compiler_params=_cp((PAR,)))(x, g)


def _norm_bwd_dg(dy, x, g, *, name):
    n, d = x.shape
    tm = _tile(n, ROW_TILE, SUBLANE)

    def body(dy_ref, x_ref, g_ref, dg_ref):
        @pl.when(pl.program_id(0) == 0)
        def _():
            dg_ref[...] = jnp.zeros_like(dg_ref)
        _, dg = _rms_bwd(dy_ref[...], x_ref[...], g_ref[...])
        dg_ref[...] += dg

    return pl.pallas_call(
        body, grid=(n // tm,),
        in_specs=[pl.BlockSpec((tm, d), lambda i: (i, 0)), pl.BlockSpec((tm, d), lambda i: (i, 0)),
                  pl.BlockSpec((1, d), lambda i: (0, 0))],
        out_specs=pl.BlockSpec((1, d), lambda i: (0, 0)),
        out_shape=jax.ShapeDtypeStruct((1, d), F32),
        name=name, compiler_params=_cp((ARB,)))(dy, x, g)


def _resid_norm_fwd(x, y, g_post, g_pres, *, name, q=None):
    n, d = x.shape
    tm = _tile(n, ROW_TILE, SUBLANE)
    nh = len(g_pres)

    def body(x_ref, y_ref, gp_ref, *rest):
        gpre = rest[:nh]
        xo_ref = rest[nh]
        h_refs = rest[nh + 1:]
        xo = x_ref[...] + _rms_fwd(y_ref[...].astype(F32), gp_ref[...])
        xo_ref[...] = xo
        for g_ref, h_ref in zip(gpre, h_refs):
            h_ref[...] = _rms_fwd(xo, g_ref[...]).astype(h_ref.dtype)

    row = pl.BlockSpec((tm, d), lambda i: (i, 0))
    vec = pl.BlockSpec((1, d), lambda i: (0, 0))
    outs = _hosted_call(
        body, grid=(n // tm,),
        in_specs=[row, row, vec] + [vec] * nh,
        out_specs=[row] + [row] * nh,
        out_shape=[jax.ShapeDtypeStruct((n, d), F32)] + [jax.ShapeDtypeStruct((n, d), MXU)] * nh,
        args=(x, y, g_post, *g_pres), name=name, q=q, budget_us=HOST_US["resid"])
    return outs[0], list(outs[1:])


def _loss_fwd(x, y, g_post, target, *, name):
    n, d = x.shape
    tm = _tile(n, ROW_TILE, SUBLANE)

    def body(x_ref, y_ref, gp_ref, t_ref, dx_ref, sq_ref, dy_ref, dg_ref):
        @pl.when(pl.program_id(0) == 0)
        def _():
            sq_ref[...] = jnp.zeros_like(sq_ref)
            dg_ref[...] = jnp.zeros_like(dg_ref)
        y = y_ref[...].astype(F32)
        err = x_ref[...] + _rms_fwd(y, gp_ref[...]) - t_ref[...]
        g = err * (1.0 / d)
        dx_ref[...] = g
        sq_ref[...] += jnp.sum(err * err, axis=0, keepdims=True)
        dy, dg = _rms_bwd(g, y, gp_ref[...])
        dy_ref[...] = dy.astype(dy_ref.dtype)
        dg_ref[...] += dg

    row = pl.BlockSpec((tm, d), lambda i: (i, 0))
    vec = pl.BlockSpec((1, d), lambda i: (0, 0))
    return pl.pallas_call(
        body, grid=(n // tm,),
        in_specs=[row, row, vec, row],
        out_specs=[row, vec, row, vec],
        out_shape=[jax.ShapeDtypeStruct((n, d), F32), jax.ShapeDtypeStruct((1, d), F32),
                   jax.ShapeDtypeStruct((n, d), MXU), jax.ShapeDtypeStruct((1, d), F32)],
        name=name, compiler_params=_cp((ARB,)))(x, y, g_post, target)


def _resid_norm_bwd(dx_out, dhs, x_out, g_pres, y, g_post, *, name, q=None):
    n, d = dx_out.shape
    tm = _tile(n, ROW_TILE, SUBLANE)
    nh = len(dhs)
    has_y = y is not None

    def body(*refs):
        it = iter(refs)
        dxo_ref = next(it)
        dh_refs = [next(it) for _ in range(nh)]
        xo_ref = next(it) if nh else None
        gpre_refs = [next(it) for _ in range(nh)]
        y_ref = next(it) if has_y else None
        gpost_ref = next(it) if has_y else None
        g_out = next(it)
        dy_out = next(it) if has_y else None
        dgpre_out = [next(it) for _ in range(nh)]
        dgpost_out = next(it) if has_y else None

        @pl.when(pl.program_id(0) == 0)
        def _():
            for r in dgpre_out:
                r[...] = jnp.zeros_like(r)
            if has_y:
                dgpost_out[...] = jnp.zeros_like(dgpost_out)

        g = dxo_ref[...]
        if nh:
            xo = xo_ref[...]
            for dh_ref, gp_ref, dg_ref in zip(dh_refs, gpre_refs, dgpre_out):
                dx, dg = _rms_bwd(dh_ref[...].astype(F32), xo, gp_ref[...])
                g = g + dx
                dg_ref[...] += dg
        g_out[...] = g
        if has_y:
            dy, dg = _rms_bwd(g, y_ref[...].astype(F32), gpost_ref[...])
            dy_out[...] = dy.astype(dy_out.dtype)
            dgpost_out[...] += dg

    row = pl.BlockSpec((tm, d), lambda i: (i, 0))
    vec = pl.BlockSpec((1, d), lambda i: (0, 0))
    ins, in_specs = [dx_out], [row]
    ins += list(dhs)
    in_specs += [row] * nh
    if nh:
        ins.append(x_out)
        in_specs.append(row)
    ins += list(g_pres)
    in_specs += [vec] * nh
    if has_y:
        ins += [y, g_post]
        in_specs += [row, vec]
    out_specs, out_shape = [row], [jax.ShapeDtypeStruct((n, d), F32)]
    if has_y:
        out_specs.append(row)
        out_shape.append(jax.ShapeDtypeStruct((n, d), MXU))
    out_specs += [vec] * nh
    out_shape += [jax.ShapeDtypeStruct((1, d), F32)] * nh
    if has_y:
        out_specs.append(vec)
        out_shape.append(jax.ShapeDtypeStruct((1, d), F32))
    outs = list(_hosted_call(
        body, grid=(n // tm,), in_specs=in_specs, out_specs=out_specs, out_shape=out_shape,
        args=tuple(ins), name=name, q=q, budget_us=HOST_US["resid_bwd"]))
    g = outs.pop(0)
    dy = outs.pop(0) if has_y else None
    dgpre = [outs.pop(0) for _ in range(nh)]
    dgpost = outs.pop(0) if has_y else None
    return g, dy, dgpre, dgpost


def _ffn_conv(up, w_ref, b_ref):
    return (w_ref[0:1, :] * _shift_down_edge(up, 2) + w_ref[1:2, :] * _shift_down_edge(up, 1)
            + w_ref[2:3, :] * up + b_ref[...])


def _ffn_act_fwd(up, wconv, bconv, bsz, *, name, q=None):
    n, f2 = up.shape
    f = f2 // 2
    t = n // bsz
    tc = _tile(f, 256)
    nf = f // tc

    def body(ug_ref, uv_ref, wg_ref, wv_ref, bg_ref, bv_ref, o_ref, dag_ref, dav_ref):
        g = _ffn_conv(ug_ref[...].astype(F32), wg_ref, bg_ref)
        v = _ffn_conv(uv_ref[...].astype(F32), wv_ref, bv_ref)
        gl, dgl = _gelu_and_grad(g)
        dag_ref[...] = (v * dgl).astype(dag_ref.dtype)
        dav_ref[...] = gl.astype(dav_ref.dtype)
        o_ref[...] = (gl * v).astype(o_ref.dtype)

    blk = pl.BlockSpec((t, tc), lambda b, j: (b, j))
    return _hosted_call(
        body, grid=(bsz, nf),
        in_specs=[blk, pl.BlockSpec((t, tc), lambda b, j: (b, j + nf)),
                  pl.BlockSpec((FFN_CONV, tc), lambda b, j: (0, j)),
                  pl.BlockSpec((FFN_CONV, tc), lambda b, j: (0, j + nf)),
                  pl.BlockSpec((1, tc), lambda b, j: (0, j)),
                  pl.BlockSpec((1, tc), lambda b, j: (0, j + nf))],
        out_specs=[blk, blk, blk],
        out_shape=[jax.ShapeDtypeStruct((n, f), MXU)] * 3,
        args=(up, up, wconv, wconv, bconv, bconv), name=name, q=q, budget_us=HOST_US["ffn_act"])


def _ffn_act_bwd(up, ug, uv, dact, wconv, bsz, *, name, q=None):
    n, f2 = up.shape
    f = f2 // 2
    t = n // bsz
    tc = _tile(f, 256)
    nf = f // tc

    def body(xg_ref, xv_ref, g_ref, v_ref, da_ref, wg_ref, wv_ref,
             dug_ref, duv_ref, dwg_ref, dwv_ref, dbg_ref, dbv_ref):
        @pl.when(pl.program_id(1) == 0)
        def _():
            for r in (dwg_ref, dwv_ref, dbg_ref, dbv_ref):
                r[...] = jnp.zeros_like(r)

        da = da_ref[...].astype(F32)
        dg = da * g_ref[...].astype(F32)
        dv = da * v_ref[...].astype(F32)

        def conv_bwd(du, w_ref, x_ref, dx_ref, dw_ref, db_ref):
            du1, du2 = _shift_up_edge(du, 1), _shift_up_edge(du, 2)
            dx_ref[...] = (w_ref[2:3, :] * du + w_ref[1:2, :] * du1 + w_ref[0:1, :] * du2).astype(dx_ref.dtype)
            x = x_ref[...].astype(F32)
            dw_ref[0:1, :] += jnp.sum(x * du2, axis=0, keepdims=True)
            dw_ref[1:2, :] += jnp.sum(x * du1, axis=0, keepdims=True)
            dw_ref[2:3, :] += jnp.sum(x * du, axis=0, keepdims=True)
            db_ref[...] += jnp.sum(du, axis=0, keepdims=True)

        conv_bwd(dg, wg_ref, xg_ref, dug_ref, dwg_ref, dbg_ref)
        conv_bwd(dv, wv_ref, xv_ref, duv_ref, dwv_ref, dbv_ref)

    blk = pl.BlockSpec((t, tc), lambda j, b: (b, j))
    wspec = pl.BlockSpec((FFN_CONV, tc), lambda j, b: (0, j))
    bspec = pl.BlockSpec((1, tc), lambda j, b: (0, j))
    outs = _hosted_call(
        body, grid=(nf, bsz),
        in_specs=[blk, pl.BlockSpec((t, tc), lambda j, b: (b, j + nf)), blk, blk, blk,
                  wspec, pl.BlockSpec((FFN_CONV, tc), lambda j, b: (0, j + nf))],
        out_specs=[blk, blk, wspec, wspec, bspec, bspec],
        out_shape=[jax.ShapeDtypeStruct((n, f), MXU), jax.ShapeDtypeStruct((n, f), MXU),
                   jax.ShapeDtypeStruct((FFN_CONV, f), F32), jax.ShapeDtypeStruct((FFN_CONV, f), F32),
                   jax.ShapeDtypeStruct((1, f), F32), jax.ShapeDtypeStruct((1, f), F32)],
        args=(up, up, ug, uv, dact, wconv, wconv), name=name, q=q, budget_us=HOST_US["ffn_act_bwd"])
    dug, duv, dwg, dwv, dbg, dbv = outs
    return dug, duv, jnp.concatenate([dwg, dwv], axis=1), jnp.concatenate([dbg, dbv], axis=1)


def _softmax(s):
    p = jnp.exp(s - jnp.max(s, axis=-1, keepdims=True))
    return p / jnp.sum(p, axis=-1, keepdims=True)


def _mem_attn_fwd(proj, q_col_block, mkv, ycat, bsz, *, name, q=None):
    n = proj.shape[0]
    t = n // bsz
    tq = _tile(t, 512, SUBLANE)
    nt = t // tq
    scale = HEAD_DIM ** -0.5

    def body(q_ref, kv_ref, old_ref, o_ref):
        del old_ref
        heads = range(MEM_HEADS)
        col = lambda ref, h, off=0: ref[:, off + h * HEAD_DIM:off + (h + 1) * HEAD_DIM].astype(MXU)
        ss = [_dot_nt(col(q_ref, h), col(kv_ref, h)) * scale for h in heads]
        ps = [_softmax(s).astype(MXU) for s in ss]
        outs = [_dot(ps[h], col(kv_ref, h, MEM_WIDTH)) for h in heads]
        o_ref[...] = jnp.concatenate(outs, axis=-1).astype(o_ref.dtype)

    return _hosted_call(
        body, grid=(bsz, nt),
        in_specs=[pl.BlockSpec((tq, MEM_WIDTH), lambda b, i: (b * nt + i, q_col_block)),
                  pl.BlockSpec((MEM_LEN, 2 * MEM_WIDTH), lambda b, i: (b, 0)),
                  pl.BlockSpec(memory_space=pl.ANY)],
        out_specs=pl.BlockSpec((tq, MEM_WIDTH), lambda b, i: (b * nt + i, MIX_WIDTH // MEM_WIDTH)),
        out_shape=jax.ShapeDtypeStruct(ycat.shape, ycat.dtype),
        aliases={2: 0}, args=(proj, mkv, ycat), name=name, q=q, budget_us=HOST_US["mem_attn_fwd"])


def _mem_attn_bwd(proj, q_col_block, mkv, dycat, dproj, bsz, *, name, q=None):
    n = proj.shape[0]
    t = n // bsz
    tq = _tile(t, 512, SUBLANE)
    nt = t // tq
    scale = HEAD_DIM ** -0.5

    def body(q_ref, kv_ref, do_ref, old_ref, dq_ref, dkv_ref):
        del old_ref

        @pl.when(pl.program_id(1) == 0)
        def _():
            dkv_ref[...] = jnp.zeros_like(dkv_ref)

        heads = range(MEM_HEADS)
        col = lambda ref, h, off=0: ref[:, off + h * HEAD_DIM:off + (h + 1) * HEAD_DIM].astype(MXU)
        qs = [col(q_ref, h) for h in heads]
        ks = [col(kv_ref, h) for h in heads]
        dos = [col(do_ref, h) for h in heads]
        ps = [_softmax(_dot_nt(qs[h], ks[h]) * scale) for h in heads]
        dps = [_dot_nt(dos[h], col(kv_ref, h, MEM_WIDTH)) for h in heads]
        dss = [(ps[h] * (dps[h] - jnp.sum(dps[h] * ps[h], axis=-1, keepdims=True)) * scale).astype(MXU) for h in heads]
        dvs = [_dot_tn(ps[h].astype(MXU), dos[h]) for h in heads]
        dqs = [_dot(dss[h], ks[h]) for h in heads]
        dks = [_dot_tn(dss[h], qs[h]) for h in heads]
        dq_ref[...] = jnp.concatenate(dqs, axis=-1).astype(dq_ref.dtype)
        dkv_ref[...] += jnp.concatenate(dks + dvs, axis=-1)

    return _hosted_call(
        body, grid=(bsz, nt),
        in_specs=[pl.BlockSpec((tq, MEM_WIDTH), lambda b, i: (b * nt + i, q_col_block)),
                  pl.BlockSpec((MEM_LEN, 2 * MEM_WIDTH), lambda b, i: (b, 0)),
                  pl.BlockSpec((tq, MEM_WIDTH), lambda b, i: (b * nt + i, MIX_WIDTH // MEM_WIDTH)),
                  pl.BlockSpec(memory_space=pl.ANY)],
        out_specs=[pl.BlockSpec((tq, MEM_WIDTH), lambda b, i: (b * nt + i, q_col_block)),
                   pl.BlockSpec((MEM_LEN, 2 * MEM_WIDTH), lambda b, i: (b, 0))],
        out_shape=[jax.ShapeDtypeStruct(dproj.shape, dproj.dtype),
                   jax.ShapeDtypeStruct((bsz * MEM_LEN, 2 * MEM_WIDTH), F32)],
        aliases={3: 0}, args=(proj, mkv, dycat, dproj), name=name, q=q, budget_us=HOST_US["mem_attn_bwd"])


def _swa_probs(s, h, dist, mask, sink):
    s = jnp.where(mask, s * (HEAD_DIM ** -0.5) - SLOPES[h] * dist, -jnp.inf)
    m = jnp.maximum(jnp.max(s, axis=-1, keepdims=True), sink)
    p = jnp.exp(s - m)
    psink = jnp.exp(sink - m)
    inv = 1.0 / (jnp.sum(p, axis=-1, keepdims=True) + psink)
    return p * inv, psink * inv


def _swa_mask(n):
    qi = lax.broadcasted_iota(jnp.int32, (WINDOW, 2 * WINDOW), 0) + WINDOW
    ki = lax.broadcasted_iota(jnp.int32, (WINDOW, 2 * WINDOW), 1)
    dist = qi - ki
    mask = (dist >= 0) & (dist < WINDOW) & ((n > 0) | (ki >= WINDOW))
    return dist.astype(F32), mask


def _swa_fwd(proj, kv, sinks, bsz, *, name, q=None):
    n_tok = proj.shape[0]
    nb = n_tok // bsz // WINDOW
    kvw = SWA_KV_HEADS * HEAD_DIM

    def body(sink_ref, q_ref, kvp_ref, kvc_ref, o_ref):
        n = pl.program_id(1)
        dist, mask = _swa_mask(n)
        kk = jnp.concatenate([kvp_ref[:, :kvw], kvc_ref[:, :kvw]], axis=0).astype(MXU)
        vv = jnp.concatenate([kvp_ref[:, kvw:], kvc_ref[:, kvw:]], axis=0).astype(MXU)
        heads = range(SWA_HEADS)
        group = lambda x, h: x[:, (h // SWA_GROUP) * HEAD_DIM:(h // SWA_GROUP + 1) * HEAD_DIM]
        ss = [_dot_nt(q_ref[:, h * HEAD_DIM:(h + 1) * HEAD_DIM].astype(MXU), group(kk, h)) for h in heads]
        ps = [_swa_probs(ss[h], h, dist, mask, sink_ref[h])[0].astype(MXU) for h in heads]
        outs = [_dot(ps[h], group(vv, h)) for h in heads]
        o_ref[...] = jnp.concatenate(outs, axis=-1).astype(o_ref.dtype)

    return _hosted_call(
        body, grid=(bsz, nb),
        in_specs=[pl.BlockSpec(memory_space=pltpu.SMEM),
                  pl.BlockSpec((WINDOW, MIX_WIDTH), lambda b, n: (b * nb + n, 0)),
                  pl.BlockSpec((WINDOW, 2 * kvw), lambda b, n: (b * nb + jnp.maximum(n - 1, 0), 0)),
                  pl.BlockSpec((WINDOW, 2 * kvw), lambda b, n: (b * nb + n, 0))],
        out_specs=pl.BlockSpec((WINDOW, MIX_WIDTH), lambda b, n: (b * nb + n, 0)),
        out_shape=jax.ShapeDtypeStruct((n_tok, D_MODEL), MXU),
        args=(sinks, proj, kv, kv), name=name, q=q, budget_us=HOST_US["swa_fwd"])


def _swa_bwd(proj, kv, sinks, dycat, bsz, *, name, q=None):
    n_tok = proj.shape[0]
    nb = n_tok // bsz // WINDOW
    kvw = SWA_KV_HEADS * HEAD_DIM

    def body(sink_ref, q_ref, kvp_ref, kvc_ref, do_ref, dq_ref, dkvc_ref, dkvp_ref, dsink_ref):
        n = pl.program_id(1)

        @pl.when((pl.program_id(0) == 0) & (n == 0))
        def _():
            dsink_ref[...] = jnp.zeros_like(dsink_ref)

        dist, mask = _swa_mask(n)
        kk = jnp.concatenate([kvp_ref[:, :kvw], kvc_ref[:, :kvw]], axis=0).astype(MXU)
        vv = jnp.concatenate([kvp_ref[:, kvw:], kvc_ref[:, kvw:]], axis=0).astype(MXU)
        lane = lax.broadcasted_iota(jnp.int32, (SUBLANE, LANE), 1)
        heads = range(SWA_HEADS)
        group = lambda x, h: x[:, (h // SWA_GROUP) * HEAD_DIM:(h // SWA_GROUP + 1) * HEAD_DIM]
        qs = [q_ref[:, h * HEAD_DIM:(h + 1) * HEAD_DIM].astype(MXU) for h in heads]
        dos = [do_ref[:, h * HEAD_DIM:(h + 1) * HEAD_DIM].astype(MXU) for h in heads]
        ss = [_dot_nt(qs[h], group(kk, h)) for h in heads]
        dps = [_dot_nt(dos[h], group(vv, h)) for h in heads]
        probs = [_swa_probs(ss[h], h, dist, mask, sink_ref[h]) for h in heads]
        rss = [jnp.sum(dps[h] * probs[h][0], axis=-1, keepdims=True) for h in heads]
        dss = [(probs[h][0] * (dps[h] - rss[h]) * (HEAD_DIM ** -0.5)).astype(MXU) for h in heads]
        dqs = [_dot(dss[h], group(kk, h)) for h in heads]
        dk_h = [_dot_tn(dss[h], qs[h]) for h in heads]
        dv_h = [_dot_tn(probs[h][0].astype(MXU), dos[h]) for h in heads]
        dsink = jnp.zeros((SUBLANE, LANE), F32)
        for h in heads:
            dsink = dsink + jnp.where(lane == h, jnp.sum(-probs[h][1] * rss[h], axis=0, keepdims=True), 0.0)
        sum_group = lambda xs, c: functools.reduce(lambda a, b: a + b, xs[c * SWA_GROUP:(c + 1) * SWA_GROUP])
        dks = [sum_group(dk_h, c) for c in range(SWA_KV_HEADS)]
        dvs = [sum_group(dv_h, c) for c in range(SWA_KV_HEADS)]
        dq_ref[...] = jnp.concatenate(dqs, axis=-1).astype(dq_ref.dtype)
        dkv = jnp.concatenate(dks + dvs, axis=-1)
        dkvp_ref[...] = dkv[:WINDOW]
        dkvc_ref[...] = dkv[WINDOW:]
        dsink_ref[...] += dsink

    qspec = pl.BlockSpec((WINDOW, MIX_WIDTH), lambda b, n: (b * nb + n, 0))
    kvspec = pl.BlockSpec((WINDOW, 2 * kvw), lambda b, n: (b * nb + n, 0))
    return _hosted_call(
        body, grid=(bsz, nb),
        in_specs=[pl.BlockSpec(memory_space=pltpu.SMEM), qspec,
                  pl.BlockSpec((WINDOW, 2 * kvw), lambda b, n: (b * nb + jnp.maximum(n - 1, 0), 0)),
                  kvspec, qspec],
        out_specs=[qspec, kvspec, kvspec, pl.BlockSpec((SUBLANE, LANE), lambda b, n: (0, 0))],
        out_shape=[jax.ShapeDtypeStruct((n_tok, D_MODEL), MXU),
                   jax.ShapeDtypeStruct((n_tok, 2 * kvw), F32),
                   jax.ShapeDtypeStruct((n_tok, 2 * kvw), F32),
                   jax.ShapeDtypeStruct((SUBLANE, LANE), F32)],
        args=(sinks, proj, kv, kv, dycat), name=name, q=q, budget_us=HOST_US["swa_bwd"])


def _swa_dkv_combine(curs, prevs, bsz, *, name):
    n_tok, w = curs[0].shape
    nb = n_tok // bsz // WINDOW
    k = len(curs)

    def body(*refs):
        o_ref = refs[-1]
        n = pl.program_id(1)
        acc = refs[0][...]
        for r in refs[1:k]:
            acc = acc + r[...]
        nxt = refs[k][...]
        for r in refs[k + 1:2 * k]:
            nxt = nxt + r[...]
        o_ref[...] = (acc + jnp.where(n < nb - 1, nxt, 0.0)).astype(o_ref.dtype)

    cur = pl.BlockSpec((WINDOW, w), lambda b, n: (b * nb + n, 0))
    prv = pl.BlockSpec((WINDOW, w), lambda b, n: (b * nb + jnp.minimum(n + 1, nb - 1), 0))
    return pl.pallas_call(
        body, grid=(bsz, nb), in_specs=[cur] * k + [prv] * k, out_specs=cur,
        out_shape=jax.ShapeDtypeStruct((n_tok, w), MXU),
        name=name, compiler_params=_cp((PAR, PAR)))(*curs, *prevs)


def _lru_gates(ux, halo, ext_ref, wc_ref, bc_ref, wr_ref, br_ref, wi_ref, bi_ref, lam_ref):
    tt = ux.shape[0]
    ext_ref[0:SUBLANE, :] = halo
    ext_ref[SUBLANE:, :] = ux
    xs = [ux] + [ext_ref[pl.ds(SUBLANE - k, tt), :] for k in range(1, LRU_CONV)]
    xc = bc_ref[...] + wc_ref[3:4, :] * xs[0] + wc_ref[2:3, :] * xs[1] + wc_ref[1:2, :] * xs[2] + wc_ref[0:1, :] * xs[3]
    pre_r, pre_i = [], []
    for blk in range(MIX_WIDTH // GATE_TILE):
        xb = xc[:, blk * GATE_TILE:(blk + 1) * GATE_TILE].astype(MXU)
        pre_r.append(_dot(xb, wr_ref[blk]))
        pre_i.append(_dot(xb, wi_ref[blk]))
    r = jax.nn.sigmoid(jnp.concatenate(pre_r, axis=-1) + br_ref[...])
    i = jax.nn.sigmoid(jnp.concatenate(pre_i, axis=-1) + bi_ref[...])
    nlam = -lam_ref[...]
    sp = jnp.maximum(nlam, 0.0) + jnp.log(1.0 + jnp.exp(-jnp.abs(nlam)))
    log_a = -LRU_C * r * sp
    a = jnp.exp(log_a)
    om = -jnp.tanh(log_a) * (a * a + 1.0)
    s = jnp.sqrt(om)
    return xs, xc, r, i, sp, a, s


def _lru_fwd(proj, wconv, bconv, wr, br, wi, bi, lam, bsz, *, name, q=None):
    n_tok = proj.shape[0]
    t = n_tok // bsz
    tt = _tile(t, 256, SUBLANE)
    nt = t // tt
    w = MIX_WIDTH
    ng = tt // SUBLANE

    def body(pg_ref, halo_ref, wc_ref, bc_ref, wr_ref, br_ref, wi_ref, bi_ref, lam_ref,
             y_ref, h_ref, ext_ref, a_ref, b_ref, carry_ref):
        ti = pl.program_id(1)

        @pl.when(ti == 0)
        def _():
            carry_ref[...] = jnp.zeros_like(carry_ref)

        gate = pg_ref[:, :w]
        ux = pg_ref[:, w:]
        halo = jnp.where(ti > 0, halo_ref[...], 0.0)
        _, xc, _, i, _, a, s = _lru_gates(ux, halo, ext_ref, wc_ref, bc_ref, wr_ref, br_ref, wi_ref, bi_ref, lam_ref)
        a_ref[...] = a
        b_ref[...] = s * (i * xc)
        row = lax.broadcasted_iota(jnp.int32, (SUBLANE, w), 0)

        def group(g, hprev):
            off = pl.multiple_of(g * SUBLANE, SUBLANE)
            ca = a_ref[pl.ds(off, SUBLANE), :]
            cb = b_ref[pl.ds(off, SUBLANE), :]
            for d in (1, 2, 4):
                a_sh = jnp.where(row >= d, pltpu.roll(ca, d, axis=0), 1.0)
                b_sh = jnp.where(row >= d, pltpu.roll(cb, d, axis=0), 0.0)
                cb = ca * b_sh + cb
                ca = ca * a_sh
            h = ca * hprev + cb
            b_ref[pl.ds(off, SUBLANE), :] = h
            return jnp.broadcast_to(h[SUBLANE - 1:SUBLANE, :], (SUBLANE, w))

        carry_ref[...] = lax.fori_loop(0, ng, group, carry_ref[...])
        h = b_ref[...]
        h_ref[...] = h
        y_ref[...] = (h * _gelu(gate)).astype(y_ref.dtype)

    vec = lambda r: pl.BlockSpec((r, w), lambda b, i: (0, 0))
    wspec = pl.BlockSpec((w // GATE_TILE, GATE_TILE, GATE_TILE), lambda b, i: (0, 0, 0))
    hb = tt // SUBLANE
    return _hosted_call(
        body, grid=(bsz, nt),
        in_specs=[pl.BlockSpec((tt, 2 * w), lambda b, i: (b * nt + i, 0)),
                  pl.BlockSpec((SUBLANE, w), lambda b, i: (jnp.maximum((b * nt + i) * hb - 1, 0), 1)),
                  vec(LRU_CONV), vec(1), wspec, vec(1), wspec, vec(1), vec(1)],
        out_specs=[pl.BlockSpec((tt, w), lambda b, i: (b * nt + i, 0)),
                   pl.BlockSpec((tt, w), lambda b, i: (b * nt + i, 0))],
        out_shape=[jax.ShapeDtypeStruct((n_tok, D_MODEL), MXU), jax.ShapeDtypeStruct((n_tok, w), F32)],
        scratch_shapes=[pltpu.VMEM((tt + SUBLANE, w), F32), pltpu.VMEM((tt, w), F32),
                        pltpu.VMEM((tt, w), F32), pltpu.VMEM((SUBLANE, w), F32)],
        args=(proj, proj, wconv, bconv, wr, br, wi, bi, lam), name=name, q=q, budget_us=HOST_US["lru_fwd"])


def _lru_bwd(proj, hs, dycat, wconv, bconv, wr, br, wi, bi, lam, bsz, *, name, q=None):
    n_tok = proj.shape[0]
    t = n_tok // bsz
    tt = _tile(t, 256, SUBLANE)
    nt = t // tt
    w = MIX_WIDTH
    ng = tt // SUBLANE
    nblk = w // GATE_TILE

    def body(pg_ref, halo_ref, h_ref, hhalo_ref, dy_ref, wc_ref, bc_ref, wr_ref, br_ref, wi_ref, bi_ref, lam_ref,
             dp_ref, dwc_ref, dbc_ref, dwr_ref, dbr_ref, dwi_ref, dbi_ref, dlam_ref,
             ext_ref, a_ref, c_ref, g_ref, gcarry_ref, xcarry_ref):
        bi_ = pl.program_id(0)
        ti = nt - 1 - pl.program_id(1)

        @pl.when((bi_ == 0) & (pl.program_id(1) == 0))
        def _():
            for r in (dwc_ref, dbc_ref, dwr_ref, dbr_ref, dwi_ref, dbi_ref, dlam_ref):
                r[...] = jnp.zeros_like(r)

        @pl.when(pl.program_id(1) == 0)
        def _():
            gcarry_ref[...] = jnp.zeros_like(gcarry_ref)
            xcarry_ref[...] = jnp.zeros_like(xcarry_ref)

        gate = pg_ref[:, :w]
        ux = pg_ref[:, w:]
        halo = jnp.where(ti > 0, halo_ref[...], 0.0)
        xs, xc, r, i, sp, a, s = _lru_gates(ux, halo, ext_ref, wc_ref, bc_ref, wr_ref, br_ref, wi_ref, bi_ref, lam_ref)
        h = h_ref[...]
        gl, dgl = _gelu_and_grad(gate)
        dy = dy_ref[...].astype(F32)
        dgate = dy * h * dgl
        row_t = lax.broadcasted_iota(jnp.int32, (tt, w), 0)
        g_ref[...] = dy * gl + jnp.where(row_t == tt - 1, gcarry_ref[0:1, :], 0.0)
        c_ref[...] = _shift_up(a, 1, row_t)
        row = lax.broadcasted_iota(jnp.int32, (SUBLANE, w), 0)

        a_ref[...] = a

        def group(k, gnext):
            off = pl.multiple_of((ng - 1 - k) * SUBLANE, SUBLANE)
            cc = c_ref[pl.ds(off, SUBLANE), :]
            cb = g_ref[pl.ds(off, SUBLANE), :]
            cb = cb + jnp.where(row == SUBLANE - 1, gnext, 0.0)
            cc = jnp.where(row == SUBLANE - 1, 0.0, cc)
            for d in (1, 2, 4):
                c_sh = jnp.where(row < SUBLANE - d, pltpu.roll(cc, SUBLANE - d, axis=0), 1.0)
                b_sh = jnp.where(row < SUBLANE - d, pltpu.roll(cb, SUBLANE - d, axis=0), 0.0)
                cb = cc * b_sh + cb
                cc = cc * c_sh
            g_ref[pl.ds(off, SUBLANE), :] = cb
            a0 = a_ref[pl.ds(off, SUBLANE), :]
            return jnp.broadcast_to(a0[0:1, :] * cb[0:1, :], (SUBLANE, w))

        gc = lax.fori_loop(0, ng, group, jnp.zeros((SUBLANE, w), F32))
        gcarry_ref[...] = gc
        gsc = g_ref[...]

        hhalo = jnp.where(ti > 0, hhalo_ref[SUBLANE - 1:SUBLANE, :], 0.0)
        hprev = jnp.where(row_t == 0, hhalo, pltpu.roll(h, 1, axis=0))
        gated = i * xc
        d_gated = gsc * s
        d_atot = gsc * hprev - (gsc * gated) * a / s
        d_loga = d_atot * a
        d_r = d_loga * (-LRU_C) * sp
        dlam_ref[...] += jnp.sum(d_loga * r, axis=0, keepdims=True) * (LRU_C * jax.nn.sigmoid(-lam_ref[...]))
        d_i = d_gated * xc
        d_xc = d_gated * i
        d_pr = d_r * r * (1.0 - r)
        d_pi = d_i * i * (1.0 - i)
        dbr_ref[...] += jnp.sum(d_pr, axis=0, keepdims=True)
        dbi_ref[...] += jnp.sum(d_pi, axis=0, keepdims=True)
        blocks = range(nblk)
        cut = lambda v, blk: v[:, blk * GATE_TILE:(blk + 1) * GATE_TILE].astype(MXU)
        xbs = [cut(xc, blk) for blk in blocks]
        drs = [cut(d_pr, blk) for blk in blocks]
        dis = [cut(d_pi, blk) for blk in blocks]
        extra = [_dot_nt(drs[blk], wr_ref[blk]) + _dot_nt(dis[blk], wi_ref[blk]) for blk in blocks]
        dwr = [_dot_tn(xbs[blk], drs[blk]) for blk in blocks]
        dwi = [_dot_tn(xbs[blk], dis[blk]) for blk in blocks]
        for blk in blocks:
            dwr_ref[blk] += dwr[blk]
            dwi_ref[blk] += dwi[blk]
        d_xc = d_xc + jnp.concatenate(extra, axis=-1)
        dbc_ref[...] += jnp.sum(d_xc, axis=0, keepdims=True)
        for k in range(LRU_CONV):
            dwc_ref[k:k + 1, :] += jnp.sum(d_xc * xs[LRU_CONV - 1 - k], axis=0, keepdims=True)
        ext_ref[0:tt, :] = d_xc
        ext_ref[tt:, :] = xcarry_ref[...]
        dux = wc_ref[3:4, :] * d_xc
        for k in range(LRU_CONV - 1):
            dux = dux + wc_ref[k:k + 1, :] * ext_ref[pl.ds(LRU_CONV - 1 - k, tt), :]
        xcarry_ref[...] = d_xc[0:SUBLANE, :]
        dp_ref[:, :w] = dgate.astype(dp_ref.dtype)
        dp_ref[:, w:] = dux.astype(dp_ref.dtype)

    vec = lambda r: pl.BlockSpec((r, w), lambda b, i: (0, 0))
    wspec = pl.BlockSpec((nblk, GATE_TILE, GATE_TILE), lambda b, i: (0, 0, 0))
    hb = tt // SUBLANE
    rblk = lambda b, i: b * nt + (nt - 1 - i)
    halo_idx = lambda b, i: jnp.maximum(rblk(b, i) * hb - 1, 0)
    wide = pl.BlockSpec((tt, 2 * w), lambda b, i: (rblk(b, i), 0))
    narrow = pl.BlockSpec((tt, w), lambda b, i: (rblk(b, i), 0))
    return _hosted_call(
        body, grid=(bsz, nt),
        in_specs=[wide, pl.BlockSpec((SUBLANE, w), lambda b, i: (halo_idx(b, i), 1)),
                  narrow, pl.BlockSpec((SUBLANE, w), lambda b, i: (halo_idx(b, i), 0)), narrow,
                  vec(LRU_CONV), vec(1), wspec, vec(1), wspec, vec(1), vec(1)],
        out_specs=[wide, vec(LRU_CONV), vec(1), wspec, vec(1), wspec, vec(1), vec(1)],
        out_shape=[jax.ShapeDtypeStruct((n_tok, 2 * w + MEM_WIDTH), MXU),
                   jax.ShapeDtypeStruct((LRU_CONV, w), F32), jax.ShapeDtypeStruct((1, w), F32),
                   jax.ShapeDtypeStruct((nblk, GATE_TILE, GATE_TILE), F32), jax.ShapeDtypeStruct((1, w), F32),
                   jax.ShapeDtypeStruct((nblk, GATE_TILE, GATE_TILE), F32), jax.ShapeDtypeStruct((1, w), F32),
                   jax.ShapeDtypeStruct((1, w), F32)],
        scratch_shapes=[pltpu.VMEM((tt + SUBLANE, w), F32), pltpu.VMEM((tt, w), F32), pltpu.VMEM((tt, w), F32),
                        pltpu.VMEM((tt, w), F32), pltpu.VMEM((SUBLANE, w), F32), pltpu.VMEM((SUBLANE, w), F32)],
        args=(proj, proj, hs, hs, dycat, wconv, bconv, wr, br, wi, bi, lam), name=name, q=q,
        budget_us=HOST_US["lru_bwd"])


def _gate_tiles(w):
    per = GATE_TILE // HEAD_DIM
    w4 = w.reshape(LRU_BLOCKS // per, per, HEAD_DIM, HEAD_DIM)
    eye = jnp.eye(per, dtype=w.dtype)
    return jnp.einsum("bnij,nm->bnimj", w4, eye).reshape(LRU_BLOCKS // per, GATE_TILE, GATE_TILE)


def _gate_blocks(t):
    per = GATE_TILE // HEAD_DIM
    t5 = t.reshape(LRU_BLOCKS // per, per, HEAD_DIM, per, HEAD_DIM)
    eye = jnp.eye(per, dtype=t.dtype)
    return jnp.einsum("bnimj,nm->bnij", t5, eye).reshape(LRU_BLOCKS, HEAD_DIM, HEAD_DIM)


def _row(v):
    return v.reshape(1, -1)


def _local_step(x, mem, target, p, wfull, push_grad, q):
    bsz, t, d = x.shape
    n = bsz * t
    x2d = x.reshape(n, d)
    tgt = target.reshape(n, d)
    mem2d = mem.reshape(bsz * MEM_LEN, d)
    wr_t = [_gate_tiles(p["w_rg_r"][j]).astype(MXU) for j in range(N_A)]
    wi_t = [_gate_tiles(p["w_rg_i"][j]).astype(MXU) for j in range(N_A)]

    mn = [_norm_fwd(mem2d, _row(p["g_mem"][l]), name=f"mem_norm{l}") for l in range(DEPTH)]
    mkv = [None] * DEPTH
    h = _norm_fwd(x2d, _row(p["g_mix_pre"][0]), name="in_norm")
    xin = x2d
    sv = []
    kv = hkv = None
    for l in range(DEPTH):
        s = {"xin": xin, "h": h}
        if q is not None:
            q.horizon = (l + 2) * GROUPS_PER_LAYER
        mkv[l] = _mm_nn(mn[l], wfull("w_mem_kv", l), name=f"mem_kv{l}", q=q)
        if l < N_A:
            proj = _mm_nn(h, wfull("w_in_a", l), name=f"in_proj{l}", q=q)
            ycat, hs = _lru_fwd(proj, p["w_conv_a"][l], _row(p["b_conv_a"][l]), wr_t[l], _row(p["b_rg_r"][l]),
                                wi_t[l], _row(p["b_rg_i"][l]), _row(p["lru_lambda"][l]), bsz, name=f"lru_fwd{l}", q=q)
            s["hs"] = hs
            qblk = 2 * MIX_WIDTH // MEM_WIDTH
        else:
            if l == N_A:
                kv = _mm_nn(hkv, wfull("w_kv", 0), name="kv_proj", q=q)
            proj = _mm_nn(h, wfull("w_in_b", l - N_A), name=f"in_proj{l}", q=q)
            ycat = _swa_fwd(proj, kv, p["sinks_b"][l - N_A], bsz, name=f"swa_fwd{l}", q=q)
            qblk = MIX_WIDTH // MEM_WIDTH
        ycat = _mem_attn_fwd(proj, qblk, mkv[l], ycat, bsz, name=f"mem_attn_fwd{l}", q=q)
        y = _mm_nn(ycat, wfull("w_mix_out", l), name=f"mix_out{l}", q=q, out_dtype=MXU)
        x1, (h2,) = _resid_norm_fwd(xin, y, _row(p["g_mix_post"][l]), [_row(p["g_ffn_pre"][l])], name=f"mix_resid{l}", q=q)
        up = _mm_nn_slots(h2, wfull("w_ffn_up", l), name=f"ffn_up{l}", q=q, out_dtype=MXU)
        act, ug, uv = _ffn_act_fwd(up, p["w_ffn_conv"][l], _row(p["b_ffn_conv"][l]), bsz, name=f"ffn_act{l}", q=q)
        f = _mm_nn(act, wfull("w_ffn_down", l), name=f"ffn_down{l}", q=q, out_dtype=MXU)
        s.update(proj=proj, qblk=qblk, ycat=ycat, y=y, x1=x1, h2=h2, up=up, ug=ug, uv=uv, act=act, f=f)
        sv.append(s)
        if l < DEPTH - 1:
            g_pres = [_row(p["g_mix_pre"][l + 1])] + ([_row(p["g_kv"])] if l + 1 == N_A else [])
            xin, hn = _resid_norm_fwd(x1, f, _row(p["g_ffn_post"][l]), g_pres, name=f"ffn_resid{l}", q=q)
            h = hn[0]
            if l + 1 == N_A:
                hkv = hn[1]
        else:
            g_tot, sq, df, g_post_last = _loss_fwd(x1, f, _row(p["g_ffn_post"][l]), tgt, name="loss")

    if q is not None:
        q.horizon = LAST_GROUP
    gs = {k: [None] * DEPTH for k in ("g_mix_pre", "g_mix_post", "g_ffn_pre", "g_ffn_post", "g_mem",
                                       "w_ffn_conv", "b_ffn_conv")}
    ga = {k: [None] * N_A for k in ("w_conv_a", "b_conv_a", "w_rg_r", "b_rg_r", "w_rg_i", "b_rg_i", "lru_lambda")}
    gsink = [None] * (DEPTH - N_A)
    dkv_cur, dkv_prev = [], []
    gs["g_ffn_post"][DEPTH - 1] = g_post_last
    grad_x = None
    for l in reversed(range(DEPTH)):
        s = sv[l]
        dact = _mm_nt(df, wfull("w_ffn_down", l), name=f"d_act{l}", q=q, out_dtype=MXU)
        push_grad("w_ffn_down", l, _mm_tn(s["act"], df, name=f"dw_down{l}", q=q))
        dug, duv, gs["w_ffn_conv"][l], gs["b_ffn_conv"][l] = _ffn_act_bwd(
            s["up"], s["ug"], s["uv"], dact, p["w_ffn_conv"][l], bsz, name=f"ffn_act_bwd{l}", q=q)
        dh2 = _mm_ffn_dh(dug, duv, wfull("w_ffn_up", l), name=f"d_h2_{l}", q=q)
        up_slots = dict(slot_cols=2 * D_FF // N_CHIP, n_slots=N_CHIP)
        dwu = _mm_tn_slots(s["h2"], dug, name=f"dw_up_g{l}", q=q, **up_slots)
        push_grad("w_ffn_up", l, _mm_tn_slots(s["h2"], duv, name=f"dw_up_v{l}", q=q, out=dwu,
                                              first_slot=N_CHIP // 2, **up_slots))
        g1, dy, (gs["g_ffn_pre"][l],), gs["g_mix_post"][l] = _resid_norm_bwd(
            g_tot, [dh2], s["x1"], [_row(p["g_ffn_pre"][l])], s["y"], _row(p["g_mix_post"][l]), name=f"mix_resid_bwd{l}", q=q)
        dycat = _mm_nt(dy, wfull("w_mix_out", l), name=f"d_ycat{l}", q=q, out_dtype=MXU)
        push_grad("w_mix_out", l, _mm_tn(s["ycat"], dy, name=f"dw_mix_out{l}", q=q))
        if l < N_A:
            dproj, dwc, dbc, dwr, dbr, dwi, dbi, dlam = _lru_bwd(
                s["proj"], s["hs"], dycat, p["w_conv_a"][l], _row(p["b_conv_a"][l]), wr_t[l], _row(p["b_rg_r"][l]),
                wi_t[l], _row(p["b_rg_i"][l]), _row(p["lru_lambda"][l]), bsz, name=f"lru_bwd{l}", q=q)
            ga["w_conv_a"][l], ga["b_conv_a"][l], ga["lru_lambda"][l] = dwc, dbc[0], dlam[0]
            ga["w_rg_r"][l], ga["w_rg_i"][l] = _gate_blocks(dwr), _gate_blocks(dwi)
            ga["b_rg_r"][l] = dbr.reshape(LRU_BLOCKS, HEAD_DIM)
            ga["b_rg_i"][l] = dbi.reshape(LRU_BLOCKS, HEAD_DIM)
            w_in, j = "w_in_a", l
        else:
            dproj, dc, dp_, dsk = _swa_bwd(s["proj"], kv, p["sinks_b"][l - N_A], dycat, bsz, name=f"swa_bwd{l}", q=q)
            dkv_cur.append(dc)
            dkv_prev.append(dp_)
            gsink[l - N_A] = dsk[0, :SWA_HEADS]
            w_in, j = "w_in_b", l - N_A
        dproj, dmkv = _mem_attn_bwd(s["proj"], s["qblk"], mkv[l], dycat, dproj, bsz, name=f"mem_attn_bwd{l}", q=q)
        dh = _mm_nt(dproj, wfull(w_in, j), name=f"d_h{l}", q=q, out_dtype=MXU)
        push_grad(w_in, j, _mm_tn(s["h"], dproj, name=f"dw_in{l}", q=q))
        dmkv = dmkv.astype(MXU)
        dmn = _mm_nt(dmkv, wfull("w_mem_kv", l), name=f"d_mem_norm{l}", q=q)
        push_grad("w_mem_kv", l, _mm_tn(mn[l], dmkv, name=f"dw_mem_kv{l}", q=q))
        gs["g_mem"][l] = _norm_bwd_dg(dmn, mem2d, _row(p["g_mem"][l]), name=f"mem_norm_bwd{l}")
        dhs, g_pres = [dh], [_row(p["g_mix_pre"][l])]
        if l == N_A:
            dkv = _swa_dkv_combine(dkv_cur, dkv_prev, bsz, name="dkv_combine")
            dhs.append(_mm_nt(dkv, wfull("w_kv", 0), name="d_hkv", q=q, out_dtype=MXU))
            g_pres.append(_row(p["g_kv"]))
            push_grad("w_kv", 0, _mm_tn(hkv, dkv, name="dw_kv", q=q))
        if l > 0:
            g_tot, df, dgpre, gs["g_ffn_post"][l - 1] = _resid_norm_bwd(
                g1, dhs, s["xin"], g_pres, sv[l - 1]["f"], _row(p["g_ffn_post"][l - 1]), name=f"ffn_resid_bwd{l - 1}", q=q)
        else:
            grad_x, _, dgpre, _ = _resid_norm_bwd(g1, dhs, s["xin"], g_pres, None, None, name="in_norm_bwd", q=q)
        gs["g_mix_pre"][l] = dgpre[0]
        if l == N_A:
            g_kv = dgpre[1][0]

    grads = {}
    for k in ("g_mix_pre", "g_mix_post", "g_ffn_pre", "g_ffn_post", "g_mem", "b_ffn_conv"):
        grads[k] = jnp.concatenate(gs[k], axis=0)
    grads["w_ffn_conv"] = jnp.stack(gs["w_ffn_conv"])
    for k, v in ga.items():
        grads[k] = jnp.stack(v)
    grads["sinks_b"] = jnp.stack(gsink)
    grads["g_kv"] = g_kv
    return jnp.sum(sq), grad_x.reshape(bsz, t, d), grads


N_CHIP = 4
HALF_ALIGN = 16
D2D_STREAMS = 2
MIN_PART_BYTES = 128 * 1024


def _full_shape(kind, shard_shape):
    l, r, c = shard_shape
    return {"row": (l, N_CHIP * r, c), "slot": (N_CHIP, l, r, c)}[kind]


def _slot_view(ref, kind, shard_shape, s, hf, sub=(0, 1)):
    _, r, _ = shard_shape
    rh = r // 2
    if hf is None:
        size = r // sub[1]
        start = sub[0] * size
    else:
        size = rh // sub[1]
        start = hf * rh + sub[0] * size
    if kind == "row":
        start = s * r + start
    if not isinstance(start, int):
        start = pl.multiple_of(start, HALF_ALIGN)
    rows = pl.ds(start, size)
    if kind == "row":
        return ref.at[:, rows, :]
    return ref.at[s, :, rows, :]


def _half_view(ref, shard_shape, hf, sub=(0, 1)):
    rh = shard_shape[1] // 2
    size = rh // sub[1]
    return ref.at[:, pl.ds(pl.multiple_of(hf * rh + sub[0] * size, HALF_ALIGN), size), :]


def _mesh_pos():
    return lax.axis_index("x"), lax.axis_index("y"), lax.axis_index("c")


def _other_chips(x, y):
    return [(1 - x, y), (x, 1 - y), (1 - x, 1 - y)]


ICI_BYTES_PER_US = 6.0e4
ICI_GATHER_BYTES_PER_US = 5.5e4
D2D_BYTES_PER_US = 4.0e5


class _Chunk:
    def __init__(self, group, cost, ins, out_shapes, alias, n_sem, start, finish, done, buffer=None, bind=None):
        self.group, self.cost, self.ins, self.out_shapes, self.alias, self.n_sem = group, cost, ins, out_shapes, alias, n_sem
        self.start, self.finish, self.done = start, finish, done
        self.buffer = buffer
        self.bind = bind

    def prepare(self):
        if self.bind is not None:
            self.bind(self)


def _merged(chunks):
    groups, by_buffer = [], {}
    for ch in chunks:
        key = None if ch.buffer is None else (id(ch.buffer[0]), ch.buffer[1])
        if key is not None and key in by_buffer:
            by_buffer[key].append(ch)
        else:
            groups.append([ch])
            if key is not None:
                by_buffer[key] = groups[-1]
    out = []
    for parts in groups:
        if len(parts) == 1:
            out.append(parts[0])
            continue
        offs = [sum(p.n_sem for p in parts[:i]) for i in range(len(parts))]

        def run(phase, ins, outs, ss, rs, b, parts=parts, offs=offs):
            for p, o in zip(parts, offs):
                getattr(p, phase)(ins, outs, ss, rs, b + o)

        def done(outs, parts=parts):
            for p in parts:
                p.done(outs)

        first = parts[0]
        out.append(_Chunk(first.group, sum(p.cost for p in parts), first.ins, first.out_shapes, first.alias,
                          sum(p.n_sem for p in parts), functools.partial(run, "start"),
                          functools.partial(run, "finish"), done))
    return out


LAST_GROUP = 1 << 30
MIN_CARRIED_US = 8.0


class _CommQueue:
    def __init__(self):
        self.pending = []
        self.flushes = 0
        self.horizon = LAST_GROUP

    def push(self, chunk):
        self.pending.append(chunk)

    def take(self, budget_us):
        got, used = [], 0.0
        for ch in sorted(self.pending, key=lambda ch: (ch.group, -ch.cost)):
            if ch.group >= self.horizon and ch.group != LAST_GROUP:
                continue
            if used + ch.cost <= budget_us and not self._shares_buffer(ch, got):
                got.append(ch)
                used += ch.cost
        if used < MIN_CARRIED_US:
            return []
        return self._taken(got)

    @staticmethod
    def _shares_buffer(ch, others):
        return ch.buffer is not None and any(
            o.buffer is not None and o.buffer[0] is ch.buffer[0] and o.buffer[1] != ch.buffer[1] for o in others)

    def _taken(self, got):
        self.pending = [ch for ch in self.pending if ch not in got]
        for ch in got:
            ch.prepare()
        return _merged(got)

    def flush(self, group=LAST_GROUP):
        while True:
            chunks = []
            for ch in self.pending:
                if ch.group <= group and not self._shares_buffer(ch, chunks):
                    chunks.append(ch)
            if not chunks:
                return
            _run_chunks(self._taken(chunks), name=f"comm_flush{self.flushes}")
            self.flushes += 1


def _run_chunks(chunks, *, name):
    ins = [a for ch in chunks for a in ch.ins]
    outs = [s for ch in chunks for s in ch.out_shapes]
    alias, offs = {}, []
    i0 = o0 = s0 = 0
    for ch in chunks:
        offs.append((i0, o0, s0))
        for ci, co in ch.alias.items():
            alias[i0 + ci] = o0 + co
        i0 += len(ch.ins)
        o0 += len(ch.out_shapes)
        s0 += ch.n_sem

    def body(*refs):
        send_sems, recv_sems = refs[i0 + o0:]
        for phase in ("start", "finish"):
            for ch, (a, b, s) in zip(chunks, offs):
                getattr(ch, phase)(refs[a:a + len(ch.ins)], refs[i0 + b:i0 + b + len(ch.out_shapes)],
                                   send_sems, recv_sems, s)

    hbm = pl.BlockSpec(memory_space=pl.ANY)
    res = pl.pallas_call(
        body, in_specs=[hbm] * i0, out_specs=[hbm] * o0, out_shape=outs,
        scratch_shapes=[pltpu.SemaphoreType.DMA((s0,)), pltpu.SemaphoreType.DMA((s0,))],
        input_output_aliases=alias, name=name, compiler_params=pltpu.CompilerParams(has_side_effects=True))(*ins)
    for ch, (_, b, _) in zip(chunks, offs):
        ch.done(list(res[b:b + len(ch.out_shapes)]))


def _remote(src, dst, send_sems, recv_sems, k, dev):
    return pltpu.make_async_remote_copy(src_ref=src, dst_ref=dst, send_sem=send_sems.at[k], recv_sem=recv_sems.at[k],
                                        device_id=dev, device_id_type=MESH_T)


def _gather_chunks(q, group, kind, shard, l, ready):
    _, r, c = shard.shape
    shp = (1, r, c)
    rh = r // 2
    parts = max(p for p in (8, 4, 2, 1)
                if (rh // p) % HALF_ALIGN == 0 and (p == 1 or (rh // p) * c * shard.dtype.itemsize >= MIN_PART_BYTES))
    part_bytes = (rh // parts) * c * shard.dtype.itemsize
    full_type = jax.ShapeDtypeStruct(_full_shape(kind, shp), shard.dtype)
    state = {"full": None, "parts_done": 0}

    def bind_first(ch):
        ch.ins, ch.alias = ([shard], {}) if state["full"] is None else ([shard, state["full"]], {1: 0})

    def bind_full(ch):
        ch.ins = [state["full"]]

    def make_part(p):
        sub = (p, parts)

        def any_part(full):
            return _slot_view(full, kind, shp, 0, 0, sub)

        def own_rows(src):
            return src.at[:, pl.ds(p * (r // parts), r // parts), :]

        def start1(ins, outs, ss, rs, b):
            x, y, c_ = _mesh_pos()
            src, full, me = ins[0].at[pl.ds(l, 1)], outs[0], 2 * x + y
            pltpu.make_async_copy(own_rows(src), _slot_view(full, kind, shp, me, None, sub), ss.at[b + N_CHIP - 1]).start()
            for j, (ox, oy) in enumerate(_other_chips(x, y)):
                _remote(_half_view(src, shp, c_, sub), _slot_view(full, kind, shp, me, c_, sub), ss, rs, b + j,
                        (ox, oy, c_)).start()

        def finish1(ins, outs, ss, rs, b):
            x, y, c_ = _mesh_pos()
            h = any_part(outs[0])
            for j in range(N_CHIP - 1):
                _remote(h, h, ss, rs, b + j, (x, y, 1 - c_)).wait()
            pltpu.make_async_copy(own_rows(ins[0].at[pl.ds(l, 1)]), _slot_view(outs[0], kind, shp, 0, None, sub),
                                  ss.at[b + N_CHIP - 1]).wait()

        def start2(ins, outs, ss, rs, b):
            x, y, c_ = _mesh_pos()
            for j, (ox, oy) in enumerate(_other_chips(x, y)):
                v = _slot_view(outs[0], kind, shp, 2 * ox + oy, c_, sub)
                _remote(v, v, ss, rs, b + j, (x, y, 1 - c_)).start()

        def finish2(ins, outs, ss, rs, b):
            x, y, c_ = _mesh_pos()
            h = any_part(outs[0])
            for j in range(N_CHIP - 1):
                _remote(h, h, ss, rs, b + j, (x, y, 1 - c_)).wait()

        def done2(outs):
            state["full"] = outs[0]
            state["parts_done"] += 1
            if state["parts_done"] == parts:
                ready(outs[0])

        def done1(outs):
            state["full"] = outs[0]
            q.push(_Chunk(group, 3 * part_bytes / D2D_BYTES_PER_US, None, [full_type], {0: 0}, N_CHIP - 1,
                          start2, finish2, done2, buffer=(state, 2), bind=bind_full))

        return _Chunk(group, 3 * part_bytes / ICI_GATHER_BYTES_PER_US, None, [full_type], None,
                      N_CHIP, start1, finish1, done1, buffer=(state, 1), bind=bind_first)

    for p in range(parts):
        q.push(make_part(p))


def _reduce_scatter_chunks(q, kind, grad, shard_shape, pos, name, ready):
    _, r, c = shard_shape
    shp = (1, r, c)
    rh = r // 2

    rp = rh // D2D_STREAMS

    def landing(ref, s, i):
        return ref.at[s, :, pl.ds(i * rp, rp), :]

    def start1(ins, outs, ss, rs, b):
        x, y, c_ = _mesh_pos()
        for s in range(N_CHIP):
            for i in range(D2D_STREAMS):
                _remote(_slot_view(ins[0], kind, shp, s, 1 - c_, (i, D2D_STREAMS)), landing(outs[0], s, i),
                        ss, rs, b + s * D2D_STREAMS + i, (x, y, 1 - c_)).start()

    def finish1(ins, outs, ss, rs, b):
        x, y, c_ = _mesh_pos()
        for s in range(N_CHIP):
            for i in range(D2D_STREAMS):
                v = landing(outs[0], s, i)
                _remote(v, v, ss, rs, b + s * D2D_STREAMS + i, (x, y, 1 - c_)).wait()

    def start2(ins, outs, ss, rs, b):
        x, y, c_ = _mesh_pos()
        for j, (ox, oy) in enumerate(_other_chips(x, y)):
            _remote(ins[0].at[2 * ox + oy], outs[0].at[j], ss, rs, b + j, (ox, oy, c_)).start()

    def finish2(ins, outs, ss, rs, b):
        x, y, c_ = _mesh_pos()
        for j in range(N_CHIP - 1):
            _remote(outs[0].at[j], outs[0].at[j], ss, rs, b + j, (x, y, 1 - c_)).wait()

    def start3(ins, outs, ss, rs, b):
        x, y, c_ = _mesh_pos()
        for i in range(D2D_STREAMS):
            v = _half_view(outs[0], shp, c_, (i, D2D_STREAMS))
            _remote(v, v, ss, rs, b + i, (x, y, 1 - c_)).start()

    def finish3(ins, outs, ss, rs, b):
        x, y, c_ = _mesh_pos()
        for i in range(D2D_STREAMS):
            v = _half_view(outs[0], shp, c_, (i, D2D_STREAMS))
            _remote(v, v, ss, rs, b + i, (x, y, 1 - c_)).wait()

    def done2(pair, outs):
        half = _rs_chip_add(pair, outs[0], shp, pos, name=f"rs_chip_add_{name}")
        q.push(_Chunk(LAST_GROUP, rh * c * 4 / D2D_BYTES_PER_US, [half], [jax.ShapeDtypeStruct(half.shape, half.dtype)],
                      {0: 0}, D2D_STREAMS, start3, finish3, lambda o: ready(o[0])))

    def done1(outs):
        pair, wire = _rs_pair_add(grad, outs[0], kind, shp, pos, name=f"rs_pair_add_{name}")
        q.push(_Chunk(LAST_GROUP, 3 * rh * c * wire.dtype.itemsize / ICI_BYTES_PER_US, [wire],
                      [jax.ShapeDtypeStruct((N_CHIP - 1, 1, rh, c), wire.dtype)], {}, N_CHIP - 1,
                      start2, finish2, functools.partial(done2, pair)))

    q.push(_Chunk(LAST_GROUP, N_CHIP * rh * c * 4 / D2D_BYTES_PER_US, [grad],
                  [jax.ShapeDtypeStruct((N_CHIP, 1, rh, c), F32)], {}, N_CHIP * D2D_STREAMS, start1, finish1, done1))


N_DEV = 8


def _allgather_chunk(q, group, vec, ready):
    def peer(k, x, y, c):
        return ((1 - x) if k & 4 else x, (1 - y) if k & 2 else y, (1 - c) if k & 1 else c)

    def start(ins, outs, ss, rs, b):
        x, y, c = _mesh_pos()
        me = 4 * x + 2 * y + c
        pltpu.make_async_copy(ins[0], outs[0].at[me], ss.at[b + N_DEV - 1]).start()
        for k in range(1, N_DEV):
            _remote(ins[0], outs[0].at[me], ss, rs, b + k - 1, peer(k, x, y, c)).start()

    def finish(ins, outs, ss, rs, b):
        x, y, c = _mesh_pos()
        for k in range(1, N_DEV):
            _remote(ins[0], outs[0].at[0], ss, rs, b + k - 1, peer(k, x, y, c)).wait()
        pltpu.make_async_copy(ins[0], outs[0].at[0], ss.at[b + N_DEV - 1]).wait()

    bytes_in = (N_DEV - 2) * vec.size * 4
    q.push(_Chunk(group, bytes_in / ICI_BYTES_PER_US, [vec], [jax.ShapeDtypeStruct((N_DEV,) + vec.shape, F32)], {},
                  N_DEV, start, finish, lambda o: ready(o[0])))


def _allreduce8(vec, *, name):
    r = vec.shape[0]
    rh = r // 2

    def body(v_ref, o_ref, sib_ref, chips_ref, send_sems, recv_sems):
        x, y, c = _mesh_pos()
        sib = (x, y, 1 - c)
        me = 2 * x + y
        pair = _remote(v_ref, sib_ref, send_sems, recv_sems, 0, sib)
        pair.start()
        pair.wait()
        rows = pl.ds(pl.multiple_of(c * rh, SUBLANE), rh)
        chips_ref[me] = v_ref[rows, :] + sib_ref[rows, :]
        copies = []
        for j, (ox, oy) in enumerate(_other_chips(x, y)):
            cp = _remote(chips_ref.at[me], chips_ref.at[me], send_sems, recv_sems, 1 + j, (ox, oy, c))
            cp.start()
            copies.append(cp)
        for cp in copies:
            cp.wait()
        acc = chips_ref[0]
        for s in range(1, N_CHIP):
            acc = acc + chips_ref[s]
        o_ref[rows, :] = acc
        swap = _remote(o_ref.at[rows, :], o_ref.at[rows, :], send_sems, recv_sems, N_CHIP, sib)
        swap.start()
        swap.wait()

    vm = pl.BlockSpec(memory_space=pltpu.VMEM)
    return pl.pallas_call(
        body, in_specs=[vm], out_specs=vm, out_shape=jax.ShapeDtypeStruct((r, LANE), F32),
        scratch_shapes=[pltpu.VMEM((r, LANE), F32), pltpu.VMEM((N_CHIP, rh, LANE), F32),
                        pltpu.SemaphoreType.DMA((N_CHIP + 1,)), pltpu.SemaphoreType.DMA((N_CHIP + 1,))],
        name=name, compiler_params=pltpu.CompilerParams(has_side_effects=True, vmem_limit_bytes=VMEM_LIMIT_V7X))(vec)


def _rs_pair_add(g, recv, kind, shape, pos, *, name):
    l, r, c = shape
    assert l == 1
    rh = r // 2
    if kind == "row":
        gspec = pl.BlockSpec((None, rh, c), lambda s, pos: (0, 2 * s + pos[0], 0))
    else:
        gspec = pl.BlockSpec((None, None, rh, c), lambda s, pos: (s, 0, pos[0], 0))
    pspec = pl.BlockSpec((None, None, rh, c), lambda s, pos: (s, 0, 0, 0))

    def body(pos_ref, g_ref, r_ref, own_ref, pw_ref):
        v = g_ref[...] + r_ref[...]
        pw_ref[...] = v.astype(pw_ref.dtype)

        @pl.when(pl.program_id(0) == pos_ref[1])
        def _():
            own_ref[...] = v

    return pl.pallas_call(
        body,
        grid_spec=pltpu.PrefetchScalarGridSpec(
            num_scalar_prefetch=1, grid=(N_CHIP,), in_specs=[gspec, pspec],
            out_specs=[pl.BlockSpec((None, rh, c), lambda s, pos: (0, 0, 0)), pspec]),
        out_shape=[jax.ShapeDtypeStruct((1, rh, c), F32), jax.ShapeDtypeStruct((N_CHIP, 1, rh, c), MXU)],
        name=name, compiler_params=_cp((ARB,)))(pos, g, recv)


def _rs_chip_add(p, recv, shape, pos, *, name):
    l, r, c = shape
    rh = r // 2

    def body(pos_ref, p_ref, r_ref, o_ref):
        del pos_ref
        acc = p_ref[...]
        for j in range(N_CHIP - 1):
            acc = acc + r_ref[j].astype(F32)
        o_ref[...] = acc

    return pl.pallas_call(
        body,
        grid_spec=pltpu.PrefetchScalarGridSpec(
            num_scalar_prefetch=1, grid=(l,),
            in_specs=[pl.BlockSpec((None, rh, c), lambda i, pos: (i, 0, 0)),
                      pl.BlockSpec((N_CHIP - 1, None, rh, c), lambda i, pos: (0, i, 0, 0))],
            out_specs=pl.BlockSpec((None, rh, c), lambda i, pos: (i, pos[0], 0))),
        out_shape=jax.ShapeDtypeStruct((l, r, c), F32),
        name=name, compiler_params=_cp((PAR,)))(pos, p, recv)


ADAM_BLOCK_ELEMS = 384 * 1024


def _adam_math(w, g, m, v):
    c1 = 1.0 / (1.0 - ADAM_B1 ** ADAM_STEP)
    c2 = 1.0 / (1.0 - ADAM_B2 ** ADAM_STEP)
    nm = ADAM_B1 * m + (1.0 - ADAM_B1) * g
    nv = ADAM_B2 * v + (1.0 - ADAM_B2) * (g * g)
    return -ADAM_LR * ((nm * c1) / (jnp.sqrt(nv * c2) + ADAM_EPS) + ADAM_WD * w), nm, nv


def _adamw_layer(w, g, m, v, outs, l, *, name):
    _, r, c = w.shape
    tr = _tile(r, max(SUBLANE, ADAM_BLOCK_ELEMS // c // SUBLANE * SUBLANE), SUBLANE)

    def body(w_ref, g_ref, m_ref, v_ref, *rest):
        go_ref, d_ref, nm_ref, nv_ref = rest[4:]
        gg = g_ref[...]
        go_ref[...] = gg
        d_ref[...], nm_ref[...], nv_ref[...] = _adam_math(w_ref[...], gg, m_ref[...], v_ref[...])

    lay = pl.BlockSpec((None, tr, c), lambda j: (l, j, 0))
    hbm = pl.BlockSpec(memory_space=pl.ANY)
    return pl.pallas_call(
        body, grid=(r // tr,),
        in_specs=[lay, pl.BlockSpec((None, tr, c), lambda j: (0, j, 0)), lay, lay] + [hbm] * 4,
        out_specs=[lay] * 4, out_shape=[jax.ShapeDtypeStruct(w.shape, F32)] * 4,
        input_output_aliases={4 + i: i for i in range(4)},
        name=name, compiler_params=_cp((PAR,)))(w, g, m, v, *outs)


def _adamw(w, g, m, v, *, name):
    shape = w.shape
    if w.ndim == 2:
        w, g, m, v = (a[None] for a in (w, g, m, v))
    l, r, c = w.shape
    tr = _tile(r, max(SUBLANE, ADAM_BLOCK_ELEMS // c // SUBLANE * SUBLANE), SUBLANE)

    def body(w_ref, g_ref, m_ref, v_ref, d_ref, nm_ref, nv_ref):
        d_ref[...], nm_ref[...], nv_ref[...] = _adam_math(w_ref[...], g_ref[...], m_ref[...], v_ref[...])

    spec = pl.BlockSpec((None, tr, c), lambda i, j: (i, j, 0))
    outs = pl.pallas_call(
        body, grid=(l, r // tr), in_specs=[spec] * 4, out_specs=[spec] * 3,
        out_shape=[jax.ShapeDtypeStruct((l, r, c), F32)] * 3,
        name=name, compiler_params=_cp((PAR, PAR)))(w, g, m, v)
    return tuple(o.reshape(shape) for o in outs)


PACK_ROWS = 2 * SUBLANE * LANE


def _pack(arrays):
    flat = jnp.concatenate([a.reshape(-1).astype(F32) for a in arrays])
    pad = (-flat.shape[0]) % PACK_ROWS
    return jnp.pad(flat, (0, pad)).reshape(-1, LANE)


def _unpack(packed, shapes):
    flat = packed.reshape(-1)
    out, off = [], 0
    for s in shapes:
        size = int(np.prod(s))
        out.append(flat[off:off + size].reshape(s))
        off += size
    return out


BIG = (("w_mem_kv", "row"), ("w_mix_out", "row"), ("w_ffn_up", "slot"), ("w_ffn_down", "row"),
       ("w_in_a", "slot"), ("w_in_b", "row"), ("w_kv", "row"))
COLUMN_SHARDED_AS_COLUMNS = ("w_in_a",)
SMALL_SHARDED = (("w_ffn_conv", 2), ("w_conv_a", 2), ("b_conv_a", 1), ("lru_lambda", 1))
SMALL_REPLICATED = ("g_mix_pre", "g_mix_post", "g_ffn_pre", "g_ffn_post", "g_mem", "b_ffn_conv",
                    "w_rg_r", "b_rg_r", "w_rg_i", "b_rg_i", "sinks_b", "g_kv")
WEIGHTS = ("g_mix_pre", "g_mix_post", "g_ffn_pre", "g_ffn_post", "g_mem", "w_mem_kv", "w_mix_out", "w_ffn_up",
           "w_ffn_conv", "b_ffn_conv", "w_ffn_down", "w_in_a", "w_conv_a", "b_conv_a", "w_rg_r", "b_rg_r", "w_rg_i",
           "b_rg_i", "lru_lambda", "w_in_b", "sinks_b", "g_kv", "w_kv")


def _slot_to_cols(a):
    s, l, r, c = a.shape
    return a.transpose(1, 2, 0, 3).reshape(l, r, s * c)


def _cols_to_slot(a):
    l, r, c4 = a.shape
    return a.reshape(l, r, N_CHIP, c4 // N_CHIP).transpose(2, 0, 1, 3)


GROUPS_PER_LAYER = 8


def _layer_weights(layer):
    names = [("w_mem_kv", layer), ("w_in_a", layer) if layer < N_A else ("w_in_b", layer - N_A)]
    if layer == N_A:
        names.append(("w_kv", 0))
    return names + [("w_mix_out", layer), ("w_ffn_up", layer), ("w_ffn_down", layer)]


def _train_step(x, mem, target, w, m, v):
    xi, yi, ci = _mesh_pos()
    chip = 2 * xi + yi
    pos = jnp.stack([ci, chip]).astype(jnp.int32)

    q = _CommQueue()
    kinds = dict(BIG)
    as3 = lambda a: a if a.ndim == 3 else a[None]
    w3, m3, v3 = ({k: as3(d[k]) for k, _ in BIG} for d in (w, m, v))
    shards = {k: w3[k].astype(MXU) for k, _ in BIG}

    gathered = {}

    def on_gathered(k, l, full):
        gathered[k, l] = _slot_to_cols(full) if k in COLUMN_SHARDED_AS_COLUMNS else full

    group_of = {}

    for layer in range(DEPTH):
        for i, (k, l) in enumerate(_layer_weights(layer)):
            group_of[k, l] = layer * GROUPS_PER_LAYER + i
            _gather_chunks(q, group_of[k, l], kinds[k], shards[k], l, functools.partial(on_gathered, k, l))

    def wfull(k, l):
        if (k, l) not in gathered:
            q.flush(group_of[k, l])
        return gathered[k, l]

    small = {}
    _allgather_chunk(q, 0, _pack([w[k] for k, _ in SMALL_SHARDED]), functools.partial(small.__setitem__, "stacked"))
    q.flush(1)

    big_out = {k: [lax.empty(w3[k].shape, F32) for _ in range(4)] for k, _ in BIG}

    def on_reduced(k, l, g):
        big_out[k] = _adamw_layer(w3[k], g, m3[k], v3[k], big_out[k], l, name=f"adamw_{k}{l}")

    def push_grad(k, l, g):
        if k in COLUMN_SHARDED_AS_COLUMNS:
            g = _cols_to_slot(g)
        _reduce_scatter_chunks(q, kinds[k], g, (1,) + w3[k].shape[1:], pos, f"{k}{l}", functools.partial(on_reduced, k, l))

    small_shapes = [w[k].shape for k, _ in SMALL_SHARDED]
    per_chip = [_unpack(small["stacked"][2 * s], small_shapes) for s in range(N_CHIP)]
    p = {k: w[k] for k in SMALL_REPLICATED}
    for i, (k, axis) in enumerate(SMALL_SHARDED):
        p[k] = jnp.concatenate([per_chip[s][i] for s in range(N_CHIP)], axis=axis)

    sq, grad_x, g = _local_step(x, mem, target, p, wfull, push_grad, q)
    loss = lax.psum(0.5 * sq / D_MODEL, ("x", "y", "c"))
    q.flush()

    small_names = [k for k, _ in SMALL_SHARDED] + list(SMALL_REPLICATED)
    summed = _allreduce8(_pack([g[k] for k in small_names]), name="allreduce_small")
    gsum = dict(zip(small_names, _unpack(summed, [p[k].shape for k in small_names])))
    for k, axis in SMALL_SHARDED:
        gsum[k] = lax.dynamic_slice_in_dim(gsum[k], chip * w[k].shape[axis], w[k].shape[axis], axis)

    delta, new_m, new_v = {}, {}, {}
    for k, _ in BIG:
        gsum[k], delta[k], new_m[k], new_v[k] = (o.reshape(w[k].shape) for o in big_out[k])
    for k in small_names:
        as2 = lambda a: a.reshape(-1, a.shape[-1])
        outs = _adamw(as2(w[k]), as2(gsum[k]), as2(m[k]), as2(v[k]), name=f"adamw_{k}")
        delta[k], new_m[k], new_v[k] = (o.reshape(w[k].shape) for o in outs)
    return (loss, grad_x, *[gsum[k] for k in WEIGHTS], *[delta[k] for k in WEIGHTS],
            *[new_m[k] for k in WEIGHTS], *[new_v[k] for k in WEIGHTS])


def kernel(x, mem, g_mix_pre, g_mix_post, g_ffn_pre, g_ffn_post, g_mem, w_mem_kv, w_mix_out, w_ffn_up, w_ffn_conv, b_ffn_conv, w_ffn_down, w_in_a, w_conv_a, b_conv_a, w_rg_r, b_rg_r, w_rg_i, b_rg_i, lru_lambda, w_in_b, sinks_b, g_kv, w_kv, loss_target, m_g_mix_pre, m_g_mix_post, m_g_ffn_pre, m_g_ffn_post, m_g_mem, m_w_mem_kv, m_w_mix_out, m_w_ffn_up, m_w_ffn_conv, m_b_ffn_conv, m_w_ffn_down, m_w_in_a, m_w_conv_a, m_b_conv_a, m_w_rg_r, m_b_rg_r, m_w_rg_i, m_b_rg_i, m_lru_lambda, m_w_in_b, m_sinks_b, m_g_kv, m_w_kv, v_g_mix_pre, v_g_mix_post, v_g_ffn_pre, v_g_ffn_post, v_g_mem, v_w_mem_kv, v_w_mix_out, v_w_ffn_up, v_w_ffn_conv, v_b_ffn_conv, v_w_ffn_down, v_w_in_a, v_w_conv_a, v_b_conv_a, v_w_rg_r, v_b_rg_r, v_w_rg_i, v_b_rg_i, v_lru_lambda, v_w_in_b, v_sinks_b, v_g_kv, v_w_kv):
    args = (g_mix_pre, g_mix_post, g_ffn_pre, g_ffn_post, g_mem, w_mem_kv, w_mix_out, w_ffn_up, w_ffn_conv, b_ffn_conv, w_ffn_down, w_in_a, w_conv_a, b_conv_a, w_rg_r, b_rg_r, w_rg_i, b_rg_i, lru_lambda, w_in_b, sinks_b, g_kv, w_kv)
    ms = (m_g_mix_pre, m_g_mix_post, m_g_ffn_pre, m_g_ffn_post, m_g_mem, m_w_mem_kv, m_w_mix_out, m_w_ffn_up, m_w_ffn_conv, m_b_ffn_conv, m_w_ffn_down, m_w_in_a, m_w_conv_a, m_b_conv_a, m_w_rg_r, m_b_rg_r, m_w_rg_i, m_b_rg_i, m_lru_lambda, m_w_in_b, m_sinks_b, m_g_kv, m_w_kv)
    vs = (v_g_mix_pre, v_g_mix_post, v_g_ffn_pre, v_g_ffn_post, v_g_mem, v_w_mem_kv, v_w_mix_out, v_w_ffn_up, v_w_ffn_conv, v_b_ffn_conv, v_w_ffn_down, v_w_in_a, v_w_conv_a, v_b_conv_a, v_w_rg_r, v_b_rg_r, v_w_rg_i, v_b_rg_i, v_lru_lambda, v_w_in_b, v_sinks_b, v_g_kv, v_w_kv)
    return _train_step(x, mem, loss_target, dict(zip(WEIGHTS, args)), dict(zip(WEIGHTS, ms)), dict(zip(WEIGHTS, vs)))
```
